```python
import jax, jax.numpy as jnp
from jax import lax
import numpy as np

D_MODEL = 1024
BATCH = 8
SEQ = 4096
DEPTH = 1

DN_HEADS = 8
DN_DK = 128
DN_DV = 128
DN_CONV = 4
DN_CHUNK = 64
DIL_GROUPS = ((128, 1), (512, 4), (2048, 16))
DIL_HEADS = 4
DIL_DH = 128
ATT_BLOCK = 128
NORM_EPS = 1e-6

N_DIL = len(DIL_GROUPS)
DN_QK_W = DN_HEADS * DN_DK
DN_V_W = DN_HEADS * DN_DV
DIL_W = DIL_HEADS * DIL_DH
PROJ_SIZES = (DN_QK_W, DN_QK_W, DN_V_W, DN_V_W, DN_HEADS, DN_HEADS,
              N_DIL * DIL_W, N_DIL * DIL_W, N_DIL * DIL_W, DIL_W, D_MODEL, D_MODEL)
PROJ_W = sum(PROJ_SIZES)

kernel_name = "hybrid_deltanet_dilated_alibi_block"


def _rmsnorm(x, w):
    xf = x.astype(jnp.float32)
    y = xf * lax.rsqrt(jnp.mean(xf * xf, axis=-1, keepdims=True) + NORM_EPS)
    return (y * w.astype(jnp.float32)).astype(x.dtype)


def _l2norm(x):
    return x * lax.rsqrt(jnp.sum(x * x, axis=-1, keepdims=True) + NORM_EPS)


def _split_cols(t, sizes):
    out, start = [], 0
    for s in sizes:
        out.append(t[..., start:start + s])
        start += s
    return out


def _causal_conv(u, w):
    K, C = w.shape
    return lax.conv_general_dilated(
        u, w[:, None, :].astype(u.dtype), window_strides=(1,), padding=[(K - 1, 0)],
        dimension_numbers=('NWC', 'WIO', 'NWC'), feature_group_count=C)


def _alibi_slopes(n):
    return 2.0 ** (-8.0 * jnp.arange(1, n + 1, dtype=jnp.float32) / n)


def _gated_delta_rule(q, k, v, beta, g):
    Bn, Sn, H, dk = q.shape
    dv = v.shape[-1]
    C = DN_CHUNK
    N = Sn // C

    def chunk(t):
        t = t.reshape((Bn, N, C, H) + t.shape[3:])
        return jnp.moveaxis(t, 3, 1)

    q = chunk(q) * (dk ** -0.5)
    k, v, beta, g = chunk(k), chunk(v), chunk(beta), chunk(g)
    gc = jnp.cumsum(g, axis=-1)
    causal = jnp.tril(jnp.ones((C, C), dtype=bool))
    strict = jnp.tril(jnp.ones((C, C), dtype=bool), -1)
    gamma = jnp.exp(jnp.where(causal, gc[..., :, None] - gc[..., None, :], -jnp.inf))

    kb = k * beta[..., None]
    a = jnp.einsum('bhnid,bhnjd->bhnij', kb, k) * gamma
    m = jnp.where(strict, a, 0.0) + jnp.eye(C, dtype=a.dtype)
    rhs = jnp.concatenate([v * beta[..., None], kb * jnp.exp(gc)[..., None]], axis=-1)
    sol = lax.linalg.triangular_solve(m, rhs, left_side=True, lower=True, unit_diagonal=True)
    u, w = sol[..., :dv], sol[..., dv:]

    aqk = jnp.einsum('bhnid,bhnjd->bhnij', q, k) * gamma
    qd = q * jnp.exp(gc)[..., None]
    kd = k * jnp.exp(gc[..., -1:] - gc)[..., None]
    dlast = jnp.exp(gc[..., -1])

    def step(state, xs):
        u_n, w_n, aqk_n, qd_n, kd_n, dl_n = xs
        v_new = u_n - jnp.einsum('bhck,bhkv->bhcv', w_n, state)
        o = jnp.einsum('bhck,bhkv->bhcv', qd_n, state) + jnp.einsum('bhij,bhjv->bhiv', aqk_n, v_new)
        state = state * dl_n[..., None, None] + jnp.einsum('bhck,bhcv->bhkv', kd_n, v_new)
        return state, o

    xs = (jnp.moveaxis(u, 2, 0), jnp.moveaxis(w, 2, 0), jnp.moveaxis(aqk, 2, 0),
          jnp.moveaxis(qd, 2, 0), jnp.moveaxis(kd, 2, 0), jnp.moveaxis(dlast, 2, 0))
    s0 = jnp.zeros((Bn, H, dk, dv), jnp.float32)
    _, o = lax.scan(step, s0, xs)
    o = jnp.moveaxis(o, 0, 2)
    return jnp.moveaxis(o, 1, 3).reshape(Bn, Sn, H, dv)


def _dilated_group(q, k, v, window, dilation, slopes):
    Bn, Sn, H, dh = q.shape
    L = Sn // dilation
    span = window // dilation
    nb = -(-L // ATT_BLOCK)
    n_prev = -(-span // ATT_BLOCK)
    Lp = nb * ATT_BLOCK
    KW = (n_prev + 1) * ATT_BLOCK

    def sub(t):
        return jnp.swapaxes(t.reshape(Bn, L, dilation, H, dh), 1, 2)

    qb = jnp.pad(sub(q), ((0, 0), (0, 0), (0, Lp - L), (0, 0), (0, 0)))
    qb = qb.reshape(Bn, dilation, nb, ATT_BLOCK, H, dh)

    def windows(t):
        t = jnp.pad(sub(t), ((0, 0), (0, 0), (n_prev * ATT_BLOCK, Lp - L), (0, 0), (0, 0)))
        t = t.reshape(Bn, dilation, nb + n_prev, ATT_BLOCK, H, dh)
        return jnp.concatenate([t[:, :, j:j + nb] for j in range(n_prev + 1)], axis=3)

    kw, vw = windows(k), windows(v)
    lq = (jnp.arange(nb)[:, None, None] * ATT_BLOCK + jnp.arange(ATT_BLOCK)[None, :, None])
    dist = n_prev * ATT_BLOCK + jnp.arange(ATT_BLOCK)[:, None] - jnp.arange(KW)[None, :]
    valid = (dist >= 0) & (dist <= span) & (lq - dist >= 0)
    alibi = slopes[:, None, None] * (dist * dilation).astype(jnp.float32)[None]

    s = jnp.einsum('bdnqhe,bdnkhe->bdnhqk', qb, kw).astype(jnp.float32) * (dh ** -0.5) - alibi
    s = jnp.where(valid[None, None, :, None], s, -jnp.inf)
    mx = jnp.max(s, axis=-1)
    p = jnp.exp(s - mx[..., None])
    den = jnp.sum(p, axis=-1)
    num = jnp.einsum('bdnhqk,bdnkhe->bdnqhe', p, vw.astype(jnp.float32))

    def back(t):
        t = t.reshape((Bn, dilation, Lp) + t.shape[4:])[:, :, :L]
        return jnp.swapaxes(t, 1, 2).reshape((Bn, Sn) + t.shape[3:])

    return back(num), back(jnp.swapaxes(den, 3, 4)), back(jnp.swapaxes(mx, 3, 4))


def _dilated_attention(q, k, v):
    Bn, Sn, _ = q.shape
    q = q.reshape(Bn, Sn, N_DIL, DIL_HEADS, DIL_DH)
    k = k.reshape(Bn, Sn, N_DIL, DIL_HEADS, DIL_DH)
    v = v.reshape(Bn, Sn, N_DIL, DIL_HEADS, DIL_DH)
    slopes = _alibi_slopes(N_DIL * DIL_HEADS).reshape(N_DIL, DIL_HEADS)
    parts = [_dilated_group(q[:, :, i], k[:, :, i], v[:, :, i], win, dil, slopes[i])
             for i, (win, dil) in enumerate(DIL_GROUPS)]
    m_all = parts[0][2]
    for _, _, mx in parts[1:]:
        m_all = jnp.maximum(m_all, mx)
    num = 0.0
    den = 0.0
    for nm, dn, mx in parts:
        sc = jnp.exp(mx - m_all)
        num = num + nm * sc[..., None]
        den = den + dn * sc
    return (num / den[..., None]).reshape(Bn, Sn, DIL_W)


def _fwd_setup_inputs(seed: int = 0) -> dict:
    key = jax.random.key(seed)
    ks = jax.random.split(key, 12)
    f32 = jnp.float32
    x = jax.random.normal(ks[0], (BATCH, SEQ, D_MODEL), f32)
    norm_w = 1.0 + 0.01 * jax.random.normal(ks[1], (DEPTH, D_MODEL), f32)
    w_in = jax.random.normal(ks[2], (DEPTH, D_MODEL, PROJ_W), f32) * D_MODEL ** -0.5
    conv_w = jax.random.normal(ks[3], (DEPTH, DN_CONV, 2 * DN_QK_W + DN_V_W), f32) * DN_CONV ** -0.5
    a_log = jnp.log(jax.random.uniform(ks[4], (DEPTH, DN_HEADS), f32, 1.0, 16.0))
    dt = jnp.exp(jax.random.uniform(ks[5], (DEPTH, DN_HEADS), f32, np.log(1e-3), np.log(1e-1)))
    dt_bias = dt + jnp.log(-jnp.expm1(-dt))
    dn_norm_w = 1.0 + 0.01 * jax.random.normal(ks[6], (DEPTH, DN_DV), f32)
    w_o_dn = jax.random.normal(ks[7], (DEPTH, DN_V_W, D_MODEL), f32) * DN_V_W ** -0.5
    w_o_dil = jax.random.normal(ks[8], (DEPTH, DIL_W, D_MODEL), f32) * DIL_W ** -0.5
    w_out = jax.random.normal(ks[9], (DEPTH, D_MODEL, D_MODEL), f32) * D_MODEL ** -0.5
    final_norm_w = 1.0 + 0.01 * jax.random.normal(ks[10], (D_MODEL,), f32)
    return {"x": x, "norm_w": norm_w, "w_in": w_in, "conv_w": conv_w, "a_log": a_log,
            "dt_bias": dt_bias, "dn_norm_w": dn_norm_w, "w_o_dn": w_o_dn, "w_o_dil": w_o_dil,
            "w_out": w_out, "final_norm_w": final_norm_w}


def _fwd_reference(x, norm_w, w_in, conv_w, a_log, dt_bias, dn_norm_w, w_o_dn, w_o_dil, w_out, final_norm_w):
    Bn, Sn, _ = x.shape
    f32 = jnp.float32
    for l in range(DEPTH):
        h = _rmsnorm(x, norm_w[l])
        proj = h @ w_in[l]
        (q_a, k_a, v_a, z_a, b_a, a_a, q_b, k_b, v_b, z_b, g_a, g_b) = _split_cols(proj, PROJ_SIZES)

        qkv = jax.nn.silu(_causal_conv(jnp.concatenate([q_a, k_a, v_a], axis=-1), conv_w[l]))
        q_a, k_a, v_a = _split_cols(qkv, (DN_QK_W, DN_QK_W, DN_V_W))
        qh = _l2norm(q_a.reshape(Bn, Sn, DN_HEADS, DN_DK).astype(f32))
        kh = _l2norm(k_a.reshape(Bn, Sn, DN_HEADS, DN_DK).astype(f32))
        vh = v_a.reshape(Bn, Sn, DN_HEADS, DN_DV).astype(f32)
        beta = jax.nn.sigmoid(b_a.astype(f32))
        g = -jnp.exp(a_log[l].astype(f32)) * jax.nn.softplus(a_a.astype(f32) + dt_bias[l].astype(f32))
        o_a = _gated_delta_rule(qh, kh, vh, beta, g)
        o_a = _rmsnorm(o_a, dn_norm_w[l]) * jax.nn.silu(z_a.reshape(Bn, Sn, DN_HEADS, DN_DV).astype(f32))
        y_a = o_a.reshape(Bn, Sn, DN_V_W).astype(x.dtype) @ w_o_dn[l]

        o_b = _dilated_attention(q_b, k_b, v_b) * jax.nn.silu(z_b.astype(f32))
        y_b = o_b.astype(x.dtype) @ w_o_dil[l]

        merged = jax.nn.sigmoid(g_a) * y_a + jax.nn.sigmoid(g_b) * y_b
        x = x + merged @ w_out[l]
    return _rmsnorm(x, final_norm_w)


import jax as _jax
import jax.numpy as _jnp

TWIN_FORMAT = 'train_step'
FWD_PARAMS = ['x', 'norm_w', 'w_in', 'conv_w', 'a_log', 'dt_bias', 'dn_norm_w', 'w_o_dn', 'w_o_dil', 'w_out', 'final_norm_w']
TWIN_WEIGHTS = ['norm_w', 'w_in', 'conv_w', 'a_log', 'dt_bias', 'dn_norm_w', 'w_o_dn', 'w_o_dil', 'w_out', 'final_norm_w']
TWIN_DIFF_INPUT = 'x'
TWIN_INPUTS = ['x', 'norm_w', 'w_in', 'conv_w', 'a_log', 'dt_bias', 'dn_norm_w', 'w_o_dn', 'w_o_dil', 'w_out', 'final_norm_w', 'loss_target', 'm_norm_w', 'm_w_in', 'm_conv_w', 'm_a_log', 'm_dt_bias', 'm_dn_norm_w', 'm_w_o_dn', 'm_w_o_dil', 'm_w_out', 'm_final_norm_w', 'v_norm_w', 'v_w_in', 'v_conv_w', 'v_a_log', 'v_dt_bias', 'v_dn_norm_w', 'v_w_o_dn', 'v_w_o_dil', 'v_w_out', 'v_final_norm_w']
TWIN_OUTPUTS = ['loss', 'grad_x', 'grad_norm_w', 'grad_w_in', 'grad_conv_w', 'grad_a_log', 'grad_dt_bias', 'grad_dn_norm_w', 'grad_w_o_dn', 'grad_w_o_dil', 'grad_w_out', 'grad_final_norm_w', 'delta_norm_w', 'delta_w_in', 'delta_conv_w', 'delta_a_log', 'delta_dt_bias', 'delta_dn_norm_w', 'delta_w_o_dn', 'delta_w_o_dil', 'delta_w_out', 'delta_final_norm_w', 'new_m_norm_w', 'new_m_w_in', 'new_m_conv_w', 'new_m_a_log', 'new_m_dt_bias', 'new_m_dn_norm_w', 'new_m_w_o_dn', 'new_m_w_o_dil', 'new_m_w_out', 'new_m_final_norm_w', 'new_v_norm_w', 'new_v_w_in', 'new_v_conv_w', 'new_v_a_log', 'new_v_dt_bias', 'new_v_dn_norm_w', 'new_v_w_o_dn', 'new_v_w_o_dil', 'new_v_w_out', 'new_v_final_norm_w']
TWIN_LEAF_KINDS = {'loss': 'loss', 'grad_x': 'grad_x', 'grad_norm_w': 'grad_w', 'grad_w_in': 'grad_w', 'grad_conv_w': 'grad_w', 'grad_a_log': 'grad_w', 'grad_dt_bias': 'grad_w', 'grad_dn_norm_w': 'grad_w', 'grad_w_o_dn': 'grad_w', 'grad_w_o_dil': 'grad_w', 'grad_w_out': 'grad_w', 'grad_final_norm_w': 'grad_w', 'delta_norm_w': 'delta_w', 'delta_w_in': 'delta_w', 'delta_conv_w': 'delta_w', 'delta_a_log': 'delta_w', 'delta_dt_bias': 'delta_w', 'delta_dn_norm_w': 'delta_w', 'delta_w_o_dn': 'delta_w', 'delta_w_o_dil': 'delta_w', 'delta_w_out': 'delta_w', 'delta_final_norm_w': 'delta_w', 'new_m_norm_w': 'new_m', 'new_m_w_in': 'new_m', 'new_m_conv_w': 'new_m', 'new_m_a_log': 'new_m', 'new_m_dt_bias': 'new_m', 'new_m_dn_norm_w': 'new_m', 'new_m_w_o_dn': 'new_m', 'new_m_w_o_dil': 'new_m', 'new_m_w_out': 'new_m', 'new_m_final_norm_w': 'new_m', 'new_v_norm_w': 'new_v', 'new_v_w_in': 'new_v', 'new_v_conv_w': 'new_v', 'new_v_a_log': 'new_v', 'new_v_dt_bias': 'new_v', 'new_v_dn_norm_w': 'new_v', 'new_v_w_o_dn': 'new_v', 'new_v_w_o_dil': 'new_v', 'new_v_w_out': 'new_v', 'new_v_final_norm_w': 'new_v'}


def _forward(args):
    return _fwd_reference(*[args[k] for k in FWD_PARAMS])


def _output_shape():
    out = _jax.eval_shape(lambda: _forward(_fwd_setup_inputs(0)))
    return out.shape, out.dtype

N_MICROBATCH = 1
ADAM_LR = 0.001
ADAM_B1 = 0.9
ADAM_B2 = 0.999
ADAM_EPS = 1e-08
ADAM_WD = 0.01
ADAM_STEP = 10
PER_EXAMPLE_BATCH_AXIS = {'x': 0, 'loss_target': 0}
SHARED_INPUTS = []
_WEIGHT_DTYPES = {'norm_w': _jnp.float32, 'w_in': _jnp.float32, 'conv_w': _jnp.float32, 'a_log': _jnp.float32, 'dt_bias': _jnp.float32, 'dn_norm_w': _jnp.float32, 'w_o_dn': _jnp.float32, 'w_o_dil': _jnp.float32, 'w_out': _jnp.float32, 'final_norm_w': _jnp.float32}
MOMENT_SCALE = {'norm_w': 1.004392e-01, 'w_in': 2.932285e-02, 'conv_w': 4.108626e-02, 'a_log': 2.715422e-01, 'dt_bias': 2.570587e-01, 'dn_norm_w': 1.562482e-01, 'w_o_dn': 5.360429e-02, 'w_o_dil': 1.780551e-02, 'w_out': 5.529725e-02, 'final_norm_w': 3.200738e+01}


def _to_microbatches(a, axis):
    t = _jnp.moveaxis(a, axis, 0)
    t = t.reshape((N_MICROBATCH, t.shape[0] // N_MICROBATCH) + t.shape[1:])
    return _jnp.moveaxis(t, 1, axis + 1)


def setup_inputs(seed: int = 0) -> dict:
    inp = _fwd_setup_inputs(seed)
    key = _jax.random.fold_in(_jax.random.key(seed), 7919)
    shape, _ = _output_shape()
    out = dict(inp)
    out["loss_target"] = _jax.random.normal(_jax.random.fold_in(key, 0), shape, _jnp.float32)
    for i, name in enumerate(TWIN_WEIGHTS):
        w = inp[name].astype(_jnp.float32)
        if MOMENT_SCALE is None:
            s = _jnp.sqrt(_jnp.mean(_jnp.square(w)) + 1e-30)
        else:
            s = MOMENT_SCALE[name]
        km, kv = _jax.random.split(_jax.random.fold_in(key, i + 1))
        out[name] = w
        out["m_" + name] = s * _jax.random.normal(km, w.shape, _jnp.float32)
        out["v_" + name] = (s * s) * _jax.random.uniform(kv, w.shape, _jnp.float32, 0.5, 1.5)
    if N_MICROBATCH > 1:
        for name, axis in PER_EXAMPLE_BATCH_AXIS.items():
            out[name] = _to_microbatches(out[name], axis)
    return {'x': out['x'], 'norm_w': out['norm_w'], 'w_in': out['w_in'], 'conv_w': out['conv_w'], 'a_log': out['a_log'], 'dt_bias': out['dt_bias'], 'dn_norm_w': out['dn_norm_w'], 'w_o_dn': out['w_o_dn'], 'w_o_dil': out['w_o_dil'], 'w_out': out['w_out'], 'final_norm_w': out['final_norm_w'], 'loss_target': out['loss_target'], 'm_norm_w': out['m_norm_w'], 'm_w_in': out['m_w_in'], 'm_conv_w': out['m_conv_w'], 'm_a_log': out['m_a_log'], 'm_dt_bias': out['m_dt_bias'], 'm_dn_norm_w': out['m_dn_norm_w'], 'm_w_o_dn': out['m_w_o_dn'], 'm_w_o_dil': out['m_w_o_dil'], 'm_w_out': out['m_w_out'], 'm_final_norm_w': out['m_final_norm_w'], 'v_norm_w': out['v_norm_w'], 'v_w_in': out['v_w_in'], 'v_conv_w': out['v_conv_w'], 'v_a_log': out['v_a_log'], 'v_dt_bias': out['v_dt_bias'], 'v_dn_norm_w': out['v_dn_norm_w'], 'v_w_o_dn': out['v_w_o_dn'], 'v_w_o_dil': out['v_w_o_dil'], 'v_w_out': out['v_w_out'], 'v_final_norm_w': out['v_final_norm_w']}


def _loss(weights, diff, rest, loss_target):
    with _jax.named_scope("forward"):
        args = {**rest, TWIN_DIFF_INPUT: diff, **{k: w.astype(_WEIGHT_DTYPES[k]) for k, w in weights.items()}}
        y = _forward(args)
    with _jax.named_scope("loss_head"):
        err = _jnp.square(y.astype(_jnp.float32) - loss_target)
        return 0.5 * _jnp.sum(_jnp.mean(err, axis=-1)) if err.ndim else 0.5 * err


def _adamw(w, g, m, v):
    m = ADAM_B1 * m + (1.0 - ADAM_B1) * g
    v = ADAM_B2 * v + (1.0 - ADAM_B2) * _jnp.square(g)
    m_hat = m / (1.0 - ADAM_B1 ** ADAM_STEP)
    v_hat = v / (1.0 - ADAM_B2 ** ADAM_STEP)
    delta = -ADAM_LR * (m_hat / (_jnp.sqrt(v_hat) + ADAM_EPS) + ADAM_WD * w)
    return delta, m, v


def reference(x, norm_w, w_in, conv_w, a_log, dt_bias, dn_norm_w, w_o_dn, w_o_dil, w_out, final_norm_w, loss_target, m_norm_w, m_w_in, m_conv_w, m_a_log, m_dt_bias, m_dn_norm_w, m_w_o_dn, m_w_o_dil, m_w_out, m_final_norm_w, v_norm_w, v_w_in, v_conv_w, v_a_log, v_dt_bias, v_dn_norm_w, v_w_o_dn, v_w_o_dil, v_w_out, v_final_norm_w):
    given = dict(x=x, norm_w=norm_w, w_in=w_in, conv_w=conv_w, a_log=a_log, dt_bias=dt_bias, dn_norm_w=dn_norm_w, w_o_dn=w_o_dn, w_o_dil=w_o_dil, w_out=w_out, final_norm_w=final_norm_w, loss_target=loss_target, m_norm_w=m_norm_w, m_w_in=m_w_in, m_conv_w=m_conv_w, m_a_log=m_a_log, m_dt_bias=m_dt_bias, m_dn_norm_w=m_dn_norm_w, m_w_o_dn=m_w_o_dn, m_w_o_dil=m_w_o_dil, m_w_out=m_w_out, m_final_norm_w=m_final_norm_w, v_norm_w=v_norm_w, v_w_in=v_w_in, v_conv_w=v_conv_w, v_a_log=v_a_log, v_dt_bias=v_dt_bias, v_dn_norm_w=v_dn_norm_w, v_w_o_dn=v_w_o_dn, v_w_o_dil=v_w_o_dil, v_w_out=v_w_out, v_final_norm_w=v_final_norm_w)
    weights = {n: given[n] for n in TWIN_WEIGHTS}
    shared = {n: given[n] for n in SHARED_INPUTS}
    per_example = {n: given[n] for n in ['x']}
    grad_fn = _jax.value_and_grad(_loss, argnums=(0, 1))

    def one_microbatch(ex, loss_target):
        ex = dict(ex)
        diff = ex.pop(TWIN_DIFF_INPUT)
        return grad_fn(weights, diff, {**shared, **ex}, loss_target)

    if N_MICROBATCH == 1:
        loss, (grad_w, grad_x) = one_microbatch(per_example, given["loss_target"])
    else:
        def body(carry, xs):
            loss_sum, grad_sum = carry
            l_k, (gw_k, gx_k) = one_microbatch(xs[0], xs[1])
            with _jax.named_scope("update"):
                return (loss_sum + l_k, _jax.tree.map(_jnp.add, grad_sum, gw_k)), gx_k

        init = (_jnp.zeros((), _jnp.float32), _jax.tree.map(_jnp.zeros_like, weights))
        (loss, grad_w), grad_x = _jax.lax.scan(body, init, (per_example, given["loss_target"]))
    with _jax.named_scope("update"):
        delta_w, new_m, new_v = {}, {}, {}
        for n in TWIN_WEIGHTS:
            delta_w[n], new_m[n], new_v[n] = _adamw(weights[n], grad_w[n], given["m_" + n], given["v_" + n])
    return (loss, grad_x, *[grad_w[n] for n in TWIN_WEIGHTS], *[delta_w[n] for n in TWIN_WEIGHTS],
            *[new_m[n] for n in TWIN_WEIGHTS], *[new_v[n] for n in TWIN_WEIGHTS])
```

```python
import functools
import math

import jax
import jax.numpy as jnp
from jax import lax
from jax.experimental import pallas as pl
from jax.experimental.pallas import tpu as pltpu

F32 = jnp.float32
BF16 = jnp.bfloat16
MESH = pl.DeviceIdType.MESH

D_MODEL = 1024
DN_HEADS = 8
DN_D = 128
DN_CHUNK = 64
DN_W = DN_HEADS * DN_D
DIL_GROUPS = ((128, 1), (512, 4), (2048, 16))
N_DIL = len(DIL_GROUPS)
DIL_HEADS = 4
DIL_DH = 128
DIL_W = DIL_HEADS * DIL_DH
ATT_BLOCK = 128
NORM_EPS = 1e-6
PROJ_W = 11280
N_CHIPS = 4
SHARD_W = PROJ_W // N_CHIPS

OFF_QKV_A = 0
OFF_Z_A = 3072
OFF_Q_B = 4096
OFF_K_B = 5632
OFF_V_B = 7168
OFF_Z_B = 8704
OFF_G_A = 9216
OFF_G_B = 10240
OFF_BA = 11264
PW = 11520
REF_OFF_BA = 4096

ADAM_LR = 0.001
ADAM_B1 = 0.9
ADAM_B2 = 0.999
ADAM_EPS = 1e-08
ADAM_WD = 0.01
ADAM_STEP = 10

ROW_TILE = 256
NEG = -1e30


def _dot(a, b):
    return jnp.dot(a.astype(BF16), b.astype(BF16), preferred_element_type=F32)


def _dot_nt(a, b):
    return lax.dot_general(a.astype(BF16), b.astype(BF16), (((1,), (1,)), ((), ())), preferred_element_type=F32)


def _dot_tn(a, b):
    return lax.dot_general(a.astype(BF16), b.astype(BF16), (((0,), (0,)), ((), ())), preferred_element_type=F32)


def _split(a):
    hi = a.astype(BF16)
    lo = (a - hi.astype(F32)).astype(BF16)
    return hi, lo


def _dot_exact_lhs(c, a):
    hi, lo = _split(a)
    cb = c.astype(BF16)
    return jnp.dot(cb, hi, preferred_element_type=F32) + jnp.dot(cb, lo, preferred_element_type=F32)


def _dot_exact_rhs(a, c):
    hi, lo = _split(a)
    cb = c.astype(BF16)
    return jnp.dot(hi, cb, preferred_element_type=F32) + jnp.dot(lo, cb, preferred_element_type=F32)


def _dot_tn_exact_rhs(a, c):
    hi, lo = _split(a)
    cb = c.astype(BF16)
    dn = (((0,), (0,)), ((), ()))
    return (lax.dot_general(hi, cb, dn, preferred_element_type=F32)
            + lax.dot_general(lo, cb, dn, preferred_element_type=F32))


def _sigmoid(x):
    return 1.0 / (1.0 + jnp.exp(-x))


def _silu(x):
    return x * _sigmoid(x)


def _silu_grad(x):
    s = _sigmoid(x)
    return s * (1.0 + x * (1.0 - s))


def _softplus(x):
    return jnp.maximum(x, 0.0) + jnp.log(1.0 + jnp.exp(-jnp.abs(x)))


def _cparams(*sem):
    return pltpu.CompilerParams(dimension_semantics=sem)


def _matmul(a, b, out_dtype, tm, tn, tk, name):
    m, kdim = a.shape
    _, n = b.shape
    tm, tn, tk = min(tm, m), min(tn, n), min(tk, kdim)
    assert m % tm == 0 and n % tn == 0 and kdim % tk == 0, (name, a.shape, b.shape, tm, tn, tk)
    nk = kdim // tk

    if nk == 1:
        def body(a_ref, b_ref, o_ref):
            o_ref[...] = _dot(a_ref[...], b_ref[...]).astype(o_ref.dtype)
        scratch = []
    else:
        def body(a_ref, b_ref, o_ref, acc_ref):
            k = pl.program_id(2)
            p = _dot(a_ref[...], b_ref[...])

            @pl.when(k == 0)
            def _():
                acc_ref[...] = p

            @pl.when(k > 0)
            def _():
                acc_ref[...] += p

            @pl.when(k == nk - 1)
            def _():
                o_ref[...] = acc_ref[...].astype(o_ref.dtype)
        scratch = [pltpu.VMEM((tm, tn), F32)]

    return pl.pallas_call(
        body, name=name, grid=(m // tm, n // tn, nk),
        in_specs=[pl.BlockSpec((tm, tk), lambda i, j, k: (i, k)), pl.BlockSpec((tk, tn), lambda i, j, k: (k, j))],
        out_specs=pl.BlockSpec((tm, tn), lambda i, j, k: (i, j)),
        out_shape=jax.ShapeDtypeStruct((m, n), out_dtype), scratch_shapes=scratch,
        compiler_params=_cparams("parallel", "parallel", "arbitrary"))(a, b)


def _rms_in(x, nw):
    s, d = x.shape

    def body(x_ref, w_ref, h_ref):
        xv = x_ref[...]
        r = lax.rsqrt(jnp.mean(xv * xv, axis=-1, keepdims=True) + NORM_EPS)
        h_ref[...] = (xv * r * w_ref[...]).astype(BF16)

    return pl.pallas_call(
        body, name="rms_in", grid=(s // ROW_TILE,),
        in_specs=[pl.BlockSpec((ROW_TILE, d), lambda i: (i, 0)), pl.BlockSpec((1, d), lambda i: (0, 0))],
        out_specs=pl.BlockSpec((ROW_TILE, d), lambda i: (i, 0)),
        out_shape=jax.ShapeDtypeStruct((s, d), BF16), compiler_params=_cparams("parallel"))(x, nw)


def _rms_in_bwd(x, nw, dh, dx2):
    s, d = x.shape

    def body(x_ref, w_ref, dh_ref, dx2_ref, dx_ref, dw_ref):
        i = pl.program_id(0)
        xv = x_ref[...]
        r = lax.rsqrt(jnp.mean(xv * xv, axis=-1, keepdims=True) + NORM_EPS)
        dhv = dh_ref[...]
        dyw = dhv * w_ref[...]
        dx_ref[...] = dx2_ref[...] + r * dyw - xv * (r * r * r) * jnp.mean(dyw * xv, axis=-1, keepdims=True)
        part = jnp.sum(dhv * xv * r, axis=0, keepdims=True)

        @pl.when(i == 0)
        def _():
            dw_ref[...] = part

        @pl.when(i > 0)
        def _():
            dw_ref[...] += part

    row = pl.BlockSpec((ROW_TILE, d), lambda i: (i, 0))
    vec = pl.BlockSpec((1, d), lambda i: (0, 0))
    return pl.pallas_call(
        body, name="rms_in_bwd", grid=(s // ROW_TILE,), in_specs=[row, vec, row, row], out_specs=[row, vec],
        out_shape=[jax.ShapeDtypeStruct((s, d), F32), jax.ShapeDtypeStruct((1, d), F32)],
        compiler_params=_cparams("arbitrary"))(x, nw, dh, dx2)


def _shift_down(cur, prev8, k):
    rc = pltpu.roll(cur, k, 0)
    rp = pltpu.roll(prev8, k, 0)
    row = lax.broadcasted_iota(jnp.int32, prev8.shape, 0)
    top = jnp.where(row < k, rp, rc[:8])
    return jnp.concatenate([top, rc[8:]], axis=0)


def _shift_up(cur, next8, k):
    t = cur.shape[0]
    rc = pltpu.roll(cur, t - k, 0)
    rn = pltpu.roll(next8, 8 - k, 0)
    row = lax.broadcasted_iota(jnp.int32, next8.shape, 0)
    bot = jnp.where(row >= 8 - k, rn, rc[t - 8:])
    return jnp.concatenate([rc[:t - 8], bot], axis=0)


def _conv_fwd(proj, conv_w):
    s = proj.shape[0]
    t8 = ROW_TILE // 8

    def body(u_ref, up_ref, w_ref, c_ref, y_ref):
        i = pl.program_id(0)
        part = pl.program_id(1)
        cur = u_ref[...]
        prev8 = jnp.where(i > 0, up_ref[...], 0.0)
        w = w_ref[...]
        c = cur * w[3:4, :]
        for k in (1, 2, 3):
            c = c + _shift_down(cur, prev8, k) * w[3 - k:4 - k, :]
        c_ref[...] = c
        a = _silu(c)
        for h in range(DN_HEADS):
            ah = a[:, h * DN_D:(h + 1) * DN_D]
            r = lax.rsqrt(jnp.sum(ah * ah, axis=-1, keepdims=True) + NORM_EPS)
            y_ref[:, h * DN_D:(h + 1) * DN_D] = jnp.where(part < 2, ah * r, ah)

    return pl.pallas_call(
        body, name="conv_fwd", grid=(s // ROW_TILE, 3),
        in_specs=[pl.BlockSpec((ROW_TILE, DN_W), lambda i, p: (i, p)),
                  pl.BlockSpec((8, DN_W), lambda i, p: (jnp.maximum(i * t8 - 1, 0), p)),
                  pl.BlockSpec((4, DN_W), lambda i, p: (0, p))],
        out_specs=[pl.BlockSpec((ROW_TILE, DN_W), lambda i, p: (i, p))] * 2,
        out_shape=[jax.ShapeDtypeStruct((s, 3 * DN_W), F32)] * 2,
        compiler_params=_cparams("parallel", "parallel"))(proj, proj, conv_w)


def _conv_bwd_act(c, dq, dk, dv):
    s = c.shape[0]

    def body(c_ref, dq_ref, dk_ref, dv_ref, dc_ref):
        for part, d_ref in enumerate((dq_ref, dk_ref, dv_ref)):
            for h in range(DN_HEADS):
                sl = slice(part * DN_W + h * DN_D, part * DN_W + (h + 1) * DN_D)
                ch = c_ref[:, sl]
                dyh = d_ref[:, h * DN_D:(h + 1) * DN_D]
                if part < 2:
                    ah = _silu(ch)
                    r = lax.rsqrt(jnp.sum(ah * ah, axis=-1, keepdims=True) + NORM_EPS)
                    dyh = r * dyh - ah * (r * r * r) * jnp.sum(dyh * ah, axis=-1, keepdims=True)
                dc_ref[:, sl] = dyh * _silu_grad(ch)

    wide = pl.BlockSpec((ROW_TILE, 3 * DN_W), lambda i: (i, 0))
    row = pl.BlockSpec((ROW_TILE, DN_W), lambda i: (i, 0))
    return pl.pallas_call(
        body, name="conv_bwd_act", grid=(s // ROW_TILE,), in_specs=[wide, row, row, row], out_specs=wide,
        out_shape=jax.ShapeDtypeStruct((s, 3 * DN_W), F32), compiler_params=_cparams("parallel"))(c, dq, dk, dv)


def _conv_bwd(proj, dc, conv_w):
    s = proj.shape[0]
    t8 = ROW_TILE // 8
    nrow = s // ROW_TILE
    last8 = s // 8 - 1

    def body(u_ref, up_ref, dc_ref, dcn_ref, w_ref, du_ref, dw_ref):
        i = pl.program_id(1)
        cur = u_ref[...]
        prev8 = jnp.where(i > 0, up_ref[...], 0.0)
        dcv = dc_ref[...]
        next8 = jnp.where(i < nrow - 1, dcn_ref[...], 0.0)
        w = w_ref[...]
        du = dcv * w[3:4, :]
        for k in (1, 2, 3):
            du = du + _shift_up(dcv, next8, k) * w[3 - k:4 - k, :]
        du_ref[...] = du.astype(BF16)

        @pl.when(i == 0)
        def _():
            dw_ref[...] = jnp.zeros_like(dw_ref)

        dw_ref[3:4, :] += jnp.sum(cur * dcv, axis=0, keepdims=True)
        for k in (1, 2, 3):
            dw_ref[3 - k:4 - k, :] += jnp.sum(_shift_down(cur, prev8, k) * dcv, axis=0, keepdims=True)

    blk = pl.BlockSpec((ROW_TILE, DN_W), lambda p, i: (i, p))
    return pl.pallas_call(
        body, name="conv_bwd", grid=(3, nrow),
        in_specs=[blk, pl.BlockSpec((8, DN_W), lambda p, i: (jnp.maximum(i * t8 - 1, 0), p)),
                  blk, pl.BlockSpec((8, DN_W), lambda p, i: (jnp.minimum((i + 1) * t8, last8), p)),
                  pl.BlockSpec((4, DN_W), lambda p, i: (0, p))],
        out_specs=[blk, pl.BlockSpec((4, DN_W), lambda p, i: (0, p))],
        out_shape=[jax.ShapeDtypeStruct((s, 3 * DN_W), BF16), jax.ShapeDtypeStruct((4, 3 * DN_W), F32)],
        compiler_params=_cparams("parallel", "arbitrary"))(proj, proj, dc, dc, conv_w)


def _gates_fwd(proj, gate_par):
    s = proj.shape[0]

    def body(ba_ref, par_ref, o_ref):
        v = ba_ref[...]
        lane = lax.broadcasted_iota(jnp.int32, v.shape, 1)
        beta = _sigmoid(v)
        g = -jnp.exp(par_ref[0:1, :]) * _softplus(v + par_ref[1:2, :])
        o_ref[...] = jnp.where(lane < DN_HEADS, beta, jnp.where(lane < 2 * DN_HEADS, g, 0.0))

    return pl.pallas_call(
        body, name="gates_fwd", grid=(s // ROW_TILE,),
        in_specs=[pl.BlockSpec((ROW_TILE, 128), lambda i: (i, OFF_BA // 128)), pl.BlockSpec((8, 128), lambda i: (0, 0))],
        out_specs=pl.BlockSpec((ROW_TILE, 128), lambda i: (i, 0)),
        out_shape=jax.ShapeDtypeStruct((s, 128), F32), compiler_params=_cparams("parallel"))(proj, gate_par)


def _gates_bwd(proj, gate_par, dbg):
    s = proj.shape[0]

    def body(ba_ref, par_ref, d_ref, o_ref, dpar_ref):
        i = pl.program_id(0)
        v = ba_ref[...]
        dv = d_ref[...]
        lane = lax.broadcasted_iota(jnp.int32, v.shape, 1)
        beta = _sigmoid(v)
        nega = -jnp.exp(par_ref[0:1, :])
        xs = v + par_ref[1:2, :]
        dsp = dv * nega * _sigmoid(xs)
        dal = dv * nega * _softplus(xs)
        is_b = lane < DN_HEADS
        is_g = jnp.logical_and(lane >= DN_HEADS, lane < 2 * DN_HEADS)
        o_ref[...] = jnp.where(is_b, dv * beta * (1.0 - beta), jnp.where(is_g, dsp, 0.0)).astype(BF16)
        r0 = jnp.sum(jnp.where(is_g, dal, 0.0), axis=0, keepdims=True)
        r1 = jnp.sum(jnp.where(is_g, dsp, 0.0), axis=0, keepdims=True)

        @pl.when(i == 0)
        def _():
            dpar_ref[...] = jnp.zeros_like(dpar_ref)

        dpar_ref[0:1, :] += r0
        dpar_ref[1:2, :] += r1

    return pl.pallas_call(
        body, name="gates_bwd", grid=(s // ROW_TILE,),
        in_specs=[pl.BlockSpec((ROW_TILE, 128), lambda i: (i, OFF_BA // 128)), pl.BlockSpec((8, 128), lambda i: (0, 0)),
                  pl.BlockSpec((ROW_TILE, 128), lambda i: (i, 0))],
        out_specs=[pl.BlockSpec((ROW_TILE, 128), lambda i: (i, 0)), pl.BlockSpec((8, 128), lambda i: (0, 0))],
        out_shape=[jax.ShapeDtypeStruct((s, 128), BF16), jax.ShapeDtypeStruct((8, 128), F32)],
        compiler_params=_cparams("arbitrary"))(proj, gate_par, dbg)


def _chunk_common(q_ref, k_ref, gcol_ref, grow_ref, bcol_ref):
    c = DN_CHUNK
    ii = lax.broadcasted_iota(jnp.int32, (c, c), 0)
    jj = lax.broadcasted_iota(jnp.int32, (c, c), 1)
    lower = (ii >= jj)
    strict = (ii > jj)
    k = k_ref[...]
    qs = q_ref[...] * (DN_D ** -0.5)
    beta_b = jnp.broadcast_to(bcol_ref[0], (c, DN_D))
    g_b = jnp.broadcast_to(gcol_ref[0], (c, DN_D))
    gc_b = _dot_exact_lhs(lower.astype(F32), g_b)
    g_r8 = jnp.broadcast_to(grow_ref[0, 0], (8, c))
    gc_r = _dot_exact_rhs(g_r8, (ii <= jj).astype(F32))[0:1, :]
    gam = jnp.where(lower, jnp.exp(jnp.minimum(gc_b[:, :c] - gc_r, 0.0)), 0.0)
    egc = jnp.exp(gc_b)
    gl = gc_b[c - 1:c, :]
    ekd = jnp.exp(gl - gc_b)
    dl = jnp.exp(gl)
    kb = k * beta_b
    a_strict = jnp.where(strict, _dot_nt(kb, k) * gam, 0.0)
    aqk = jnp.where(lower, _dot_nt(qs, k) * gam, 0.0)
    return dict(lower=lower, strict=strict, k=k, qs=qs, beta_b=beta_b, gc_b=gc_b, gam=gam, egc=egc, ekd=ekd,
                dl=dl, kb=kb, a_strict=a_strict, aqk=aqk, ii=ii, jj=jj)


def _unit_lower_inverse_minus_eye(n_strict, ii, jj):
    same = lax.shift_right_logical(ii, 4) == lax.shift_right_logical(jj, 4)
    dmat = jnp.where(same, n_strict, 0.0)
    omat = n_strict - dmat
    d2 = _dot(dmat, dmat)
    d4 = _dot(d2, d2)
    d8 = _dot(d4, d4)
    x1 = d2 - dmat - _dot(dmat, d2)
    x2 = x1 + d4 + _dot(x1, d4)
    x3 = x2 + d8 + _dot(x2, d8)
    n1 = omat + _dot(x3, omat)
    n2 = _dot(n1, n1)
    y = n2 - n1 - _dot(n1, n2)
    return y + x3 + _dot(y, x3)


def _gdr_fwd(qkv, bcol, gcol, grow):
    s = qkv.shape[0]
    c = DN_CHUNK
    n = s // c

    def body(q_ref, k_ref, v_ref, bcol_ref, gcol_ref, grow_ref,
             o_ref, u_ref, w_ref, vn_ref, tm_ref, st_ref, state):
        @pl.when(pl.program_id(1) == 0)
        def _():
            state[...] = jnp.zeros_like(state)

        cm = _chunk_common(q_ref, k_ref, gcol_ref, grow_ref, bcol_ref)
        tm = _unit_lower_inverse_minus_eye(cm["a_strict"], cm["ii"], cm["jj"])
        rhs_u = v_ref[...] * cm["beta_b"]
        rhs_w = cm["kb"] * cm["egc"]
        u = rhs_u + _dot(tm, rhs_u)
        w = rhs_w + _dot(tm, rhs_w)
        st = state[...]
        v_new = u - _dot(w, st)
        o_ref[...] = _dot(cm["qs"] * cm["egc"], st) + _dot(cm["aqk"], v_new)
        u_ref[...] = u
        w_ref[...] = w
        vn_ref[...] = v_new
        tm_ref[0, 0] = tm
        st_ref[0, 0] = st
        state[...] = st * cm["dl"] + _dot_tn(cm["k"] * cm["ekd"], v_new)

    def col(off):
        return pl.BlockSpec((c, DN_D), lambda h, j: (j, off + h))

    colv = pl.BlockSpec((1, c, 1), lambda h, j: (h, j, 0))
    hd = pl.BlockSpec((c, DN_D), lambda h, j: (j, h))
    return pl.pallas_call(
        body, name="gdr_fwd", grid=(DN_HEADS, n),
        in_specs=[col(0), col(DN_HEADS), col(2 * DN_HEADS), colv, colv,
                  pl.BlockSpec((1, 1, 1, c), lambda h, j: (h, j, 0, 0))],
        out_specs=[hd, hd, hd, hd, pl.BlockSpec((1, 1, c, c), lambda h, j: (h, j, 0, 0)),
                   pl.BlockSpec((1, 1, DN_D, DN_D), lambda h, j: (h, j, 0, 0))],
        out_shape=[jax.ShapeDtypeStruct((s, DN_W), F32)] * 4
        + [jax.ShapeDtypeStruct((DN_HEADS, n, c, c), F32), jax.ShapeDtypeStruct((DN_HEADS, n, DN_D, DN_D), F32)],
        scratch_shapes=[pltpu.VMEM((DN_D, DN_D), F32)],
        compiler_params=_cparams("parallel", "arbitrary"))(qkv, qkv, qkv, bcol, gcol, grow)


def _gdr_bwd(qkv, bcol, gcol, grow, u, w, vn, tmat, states, do):
    s = qkv.shape[0]
    c = DN_CHUNK
    n = s // c

    def body(q_ref, k_ref, v_ref, bcol_ref, gcol_ref, grow_ref, u_ref, w_ref, vn_ref, tm_ref, st_ref, do_ref,
             dq_ref, dk_ref, dv_ref, db_ref, dg_ref, dstate):
        @pl.when(pl.program_id(1) == 0)
        def _():
            dstate[...] = jnp.zeros_like(dstate)

        cm = _chunk_common(q_ref, k_ref, gcol_ref, grow_ref, bcol_ref)
        lower, strict, k, qs, beta_b = cm["lower"], cm["strict"], cm["k"], cm["qs"], cm["beta_b"]
        gam, egc, ekd, dl, kb = cm["gam"], cm["egc"], cm["ekd"], cm["dl"], cm["kb"]
        aqk, a_strict = cm["aqk"], cm["a_strict"]
        v = v_ref[...]
        uu, ww, v_new, tm, st, dov = u_ref[...], w_ref[...], vn_ref[...], tm_ref[0, 0], st_ref[0, 0], do_ref[...]
        dsn = dstate[...]
        qd = qs * egc
        kd = k * ekd

        dv_new = _dot_tn(aqk, dov) + _dot(kd, dsn)
        daqk = jnp.where(lower, _dot_nt(dov, v_new), 0.0)
        dqd = _dot_nt(dov, st)
        dkd = _dot_nt(v_new, dsn)
        ddl = jnp.sum(jnp.sum(dsn * st, axis=1, keepdims=True), axis=0, keepdims=True)
        dw = -_dot_nt(dv_new, st)
        dstate[...] = dsn * dl + _dot_tn(qd, dov) - _dot_tn(ww, dv_new)

        dru = dv_new + _dot_tn(tm, dv_new)
        drw = dw + _dot_tn(tm, dw)
        dn = jnp.where(strict, -(_dot_nt(dru, uu) + _dot_nt(drw, ww)), 0.0)
        dag = dn * gam
        dkb = _dot(dag, k) + drw * egc
        dk = _dot_tn(dag, kb)
        dqg = daqk * gam
        dqs = _dot(dqg, k) + dqd * egc
        dk = dk + _dot_tn(dqg, qs) + dkb * beta_b + dkd * ekd
        pmat = dn * a_strict + daqk * aqk
        ones = jnp.ones((c, DN_D), F32)
        tkd = jnp.sum(dkd * kd, axis=-1, keepdims=True)
        dgc = (jnp.sum(pmat, axis=-1, keepdims=True) - _dot_tn_exact_rhs(pmat, ones)
               + jnp.sum(drw * (kb * egc), axis=-1, keepdims=True)
               + jnp.sum(dqd * qd, axis=-1, keepdims=True) - tkd)
        last = jnp.sum(tkd, axis=0, keepdims=True) + ddl * dl
        rowi = lax.broadcasted_iota(jnp.int32, (c, DN_D), 0)
        dgc = dgc + jnp.where(rowi == c - 1, last, 0.0)
        dg = _dot_exact_lhs((cm["ii"] <= cm["jj"]).astype(F32), dgc)
        dbeta = jnp.sum(dru * v, axis=-1, keepdims=True) + jnp.sum(dkb * k, axis=-1, keepdims=True)

        dq_ref[...] = dqs * (DN_D ** -0.5)
        dk_ref[...] = dk
        dv_ref[...] = dru * beta_b
        db_ref[0] = dbeta
        dg_ref[0] = dg[:, 0:1]

    def col(off):
        return pl.BlockSpec((c, DN_D), lambda h, j: (n - 1 - j, off + h))

    colv = pl.BlockSpec((1, c, 1), lambda h, j: (h, n - 1 - j, 0))
    hd = pl.BlockSpec((c, DN_D), lambda h, j: (n - 1 - j, h))
    return pl.pallas_call(
        body, name="gdr_bwd", grid=(DN_HEADS, n),
        in_specs=[col(0), col(DN_HEADS), col(2 * DN_HEADS), colv, colv,
                  pl.BlockSpec((1, 1, 1, c), lambda h, j: (h, n - 1 - j, 0, 0)), hd, hd, hd,
                  pl.BlockSpec((1, 1, c, c), lambda h, j: (h, n - 1 - j, 0, 0)),
                  pl.BlockSpec((1, 1, DN_D, DN_D), lambda h, j: (h, n - 1 - j, 0, 0)), hd],
        out_specs=[hd, hd, hd, colv, colv],
        out_shape=[jax.ShapeDtypeStruct((s, DN_W), F32)] * 3 + [jax.ShapeDtypeStruct((DN_HEADS, s, 1), F32)] * 2,
        scratch_shapes=[pltpu.VMEM((DN_D, DN_D), F32)],
        compiler_params=_cparams("parallel", "arbitrary"))(qkv, qkv, qkv, bcol, gcol, grow, u, w, vn, tmat, states, do)


def _gdr_out(o, proj, dnw):
    s = o.shape[0]

    def body(o_ref, z_ref, w_ref, y_ref):
        ov, zv, wv = o_ref[...], z_ref[...], w_ref[...]
        for h in range(DN_HEADS):
            sl = slice(h * DN_D, (h + 1) * DN_D)
            oh = ov[:, sl]
            r = lax.rsqrt(jnp.mean(oh * oh, axis=-1, keepdims=True) + NORM_EPS)
            y_ref[:, sl] = ((oh * r * wv) * _silu(zv[:, sl])).astype(BF16)

    row = pl.BlockSpec((ROW_TILE, DN_W), lambda i: (i, 0))
    return pl.pallas_call(
        body, name="gdr_out", grid=(s // ROW_TILE,),
        in_specs=[row, pl.BlockSpec((ROW_TILE, DN_W), lambda i: (i, OFF_Z_A // DN_W)), pl.BlockSpec((1, DN_D), lambda i: (0, 0))],
        out_specs=row, out_shape=jax.ShapeDtypeStruct((s, DN_W), BF16), compiler_params=_cparams("parallel"))(o, proj, dnw)


def _gdr_out_bwd(o, proj, dnw, dy):
    s = o.shape[0]

    def body(o_ref, z_ref, w_ref, dy_ref, do_ref, dz_ref, dw_ref):
        i = pl.program_id(0)
        ov, zv, wv, dyv = o_ref[...], z_ref[...], w_ref[...], dy_ref[...]
        acc = jnp.zeros((1, DN_D), F32)
        for h in range(DN_HEADS):
            sl = slice(h * DN_D, (h + 1) * DN_D)
            oh, zh, dh = ov[:, sl], zv[:, sl], dyv[:, sl]
            r = lax.rsqrt(jnp.mean(oh * oh, axis=-1, keepdims=True) + NORM_EPS)
            dn = dh * _silu(zh)
            dz_ref[:, sl] = (dh * (oh * r * wv) * _silu_grad(zh)).astype(BF16)
            acc = acc + jnp.sum(dn * oh * r, axis=0, keepdims=True)
            dnw_ = dn * wv
            do_ref[:, sl] = r * dnw_ - oh * (r * r * r) * jnp.mean(dnw_ * oh, axis=-1, keepdims=True)

        @pl.when(i == 0)
        def _():
            dw_ref[...] = acc

        @pl.when(i > 0)
        def _():
            dw_ref[...] += acc

    row = pl.BlockSpec((ROW_TILE, DN_W), lambda i: (i, 0))
    vec = pl.BlockSpec((1, DN_D), lambda i: (0, 0))
    return pl.pallas_call(
        body, name="gdr_out_bwd", grid=(s // ROW_TILE,),
        in_specs=[row, pl.BlockSpec((ROW_TILE, DN_W), lambda i: (i, OFF_Z_A // DN_W)), vec, row],
        out_specs=[row, row, vec],
        out_shape=[jax.ShapeDtypeStruct((s, DN_W), F32), jax.ShapeDtypeStruct((s, DN_W), BF16),
                   jax.ShapeDtypeStruct((1, DN_D), F32)],
        compiler_params=_cparams("arbitrary"))(o, proj, dnw, dy)


def _slope(group, head):
    idx = (group * DIL_HEADS + head + 1).astype(F32)
    return jnp.exp(jnp.full((1, 128), -8.0 * math.log(2.0) / (N_DIL * DIL_HEADS), F32) * idx)


def _att_scores(qb, k_cur, k_prev, slope_d, has_prev):
    iq = lax.broadcasted_iota(jnp.int32, (ATT_BLOCK, ATT_BLOCK), 0)
    jk = lax.broadcasted_iota(jnp.int32, (ATT_BLOCK, ATT_BLOCK), 1)
    dist_c = (iq - jk).astype(F32)
    s_cur = jnp.where(iq >= jk, _dot_nt(qb, k_cur) - slope_d * dist_c, NEG)
    s_prev = jnp.where(jnp.logical_and(jk >= iq, has_prev),
                       _dot_nt(qb, k_prev) - slope_d * (dist_c + float(ATT_BLOCK)), NEG)
    return s_cur, s_prev


def _att_fwd(proj, group):
    s = proj.shape[0]
    dil = DIL_GROUPS[group][1]
    assert DIL_GROUPS[group][0] // dil == ATT_BLOCK
    nb = s // dil // ATT_BLOCK
    assert nb * dil * ATT_BLOCK == s

    def body(q_ref, k_ref, v_ref, num_ref, den_ref, mx_ref):
        slope_d = _slope(group, pl.program_id(0)) * float(dil)

        def step(t, carry):
            r = lax.div(t, nb)
            j = lax.rem(t, nb)
            base = r + dil * ATT_BLOCK * j
            pbase = base - dil * ATT_BLOCK * jnp.minimum(j, 1)
            if dil == 1:
                base, pbase = pl.multiple_of(base, ATT_BLOCK), pl.multiple_of(pbase, ATT_BLOCK)
            cur = pl.ds(base, ATT_BLOCK, stride=dil)
            prv = pl.ds(pbase, ATT_BLOCK, stride=dil)
            qb = q_ref[cur, :] * (DIL_DH ** -0.5)
            s_cur, s_prev = _att_scores(qb, k_ref[cur, :], k_ref[prv, :], slope_d, j > 0)
            mx = jnp.maximum(jnp.max(s_cur, axis=-1, keepdims=True), jnp.max(s_prev, axis=-1, keepdims=True))
            p_cur = jnp.exp(s_cur - mx)
            p_prev = jnp.exp(s_prev - mx)
            den = jnp.sum(p_cur, axis=-1, keepdims=True) + jnp.sum(p_prev, axis=-1, keepdims=True)
            num_ref[cur, :] = _dot(p_cur, v_ref[cur, :]) + _dot(p_prev, v_ref[prv, :])
            den_ref[cur, :] = jnp.broadcast_to(den, (ATT_BLOCK, DIL_DH))
            mx_ref[cur, :] = jnp.broadcast_to(mx, (ATT_BLOCK, DIL_DH))
            return carry

        lax.fori_loop(0, dil * nb, step, 0)

    def col(off):
        return pl.BlockSpec((s, DIL_DH), lambda h: (0, off // DIL_DH + group * DIL_HEADS + h))

    out = pl.BlockSpec((s, DIL_DH), lambda h: (0, h))
    return pl.pallas_call(
        body, name=f"att_fwd{group}", grid=(DIL_HEADS,), in_specs=[col(OFF_Q_B), col(OFF_K_B), col(OFF_V_B)],
        out_specs=[out, out, out], out_shape=[jax.ShapeDtypeStruct((s, DIL_W), F32)] * 3,
        compiler_params=_cparams("parallel"))(proj, proj, proj)


def _att_bwd(proj, group, do, lse, delta):
    s = proj.shape[0]
    dil = DIL_GROUPS[group][1]
    nb = s // dil // ATT_BLOCK

    def body(q_ref, k_ref, v_ref, do_ref, lse_ref, dl_ref, dq_ref, dk_ref, dv_ref, dq_acc, dk_acc, dv_acc):
        slope_d = _slope(group, pl.program_id(0)) * float(dil)
        dk_acc[...] = jnp.zeros_like(dk_acc)
        dv_acc[...] = jnp.zeros_like(dv_acc)

        def step(t, carry):
            r = lax.div(t, nb)
            j = lax.rem(t, nb)
            base = r + dil * ATT_BLOCK * j
            pbase = base - dil * ATT_BLOCK * jnp.minimum(j, 1)
            if dil == 1:
                base, pbase = pl.multiple_of(base, ATT_BLOCK), pl.multiple_of(pbase, ATT_BLOCK)
            cur = pl.ds(base, ATT_BLOCK, stride=dil)
            prv = pl.ds(pbase, ATT_BLOCK, stride=dil)
            qb = q_ref[cur, :] * (DIL_DH ** -0.5)
            k_cur, k_prev, v_cur, v_prev = k_ref[cur, :], k_ref[prv, :], v_ref[cur, :], v_ref[prv, :]
            s_cur, s_prev = _att_scores(qb, k_cur, k_prev, slope_d, j > 0)
            lse_b, delta_b, dob = lse_ref[cur, :], dl_ref[cur, :], do_ref[cur, :]
            p_cur = jnp.exp(s_cur - lse_b)
            p_prev = jnp.exp(s_prev - lse_b)
            ds_cur = p_cur * (_dot_nt(dob, v_cur) - delta_b)
            ds_prev = p_prev * (_dot_nt(dob, v_prev) - delta_b)
            dq_acc[cur, :] = (_dot(ds_cur, k_cur) + _dot(ds_prev, k_prev)) * (DIL_DH ** -0.5)
            dk_acc[cur, :] += _dot_tn(ds_cur, qb)
            dv_acc[cur, :] += _dot_tn(p_cur, dob)

            @pl.when(j > 0)
            def _():
                dk_acc[prv, :] += _dot_tn(ds_prev, qb)
                dv_acc[prv, :] += _dot_tn(p_prev, dob)

            return carry

        lax.fori_loop(0, dil * nb, step, 0)
        dq_ref[...] = dq_acc[...].astype(BF16)
        dk_ref[...] = dk_acc[...].astype(BF16)
        dv_ref[...] = dv_acc[...].astype(BF16)

    def col(off):
        return pl.BlockSpec((s, DIL_DH), lambda h: (0, off // DIL_DH + group * DIL_HEADS + h))

    hd = pl.BlockSpec((s, DIL_DH), lambda h: (0, h))
    return pl.pallas_call(
        body, name=f"att_bwd{group}", grid=(DIL_HEADS,),
        in_specs=[col(OFF_Q_B), col(OFF_K_B), col(OFF_V_B), hd, hd, hd], out_specs=[hd, hd, hd],
        out_shape=[jax.ShapeDtypeStruct((s, DIL_W), BF16)] * 3,
        scratch_shapes=[pltpu.VMEM((s, DIL_DH), F32)] * 3,
        compiler_params=_cparams("parallel"))(proj, proj, proj, do, lse, delta)


def _att_merge(parts, proj):
    s = proj.shape[0]

    def body(n0, d0, m0, n1, d1, m1, n2, d2, m2, z_ref, ob_ref, o_ref, lse_ref):
        m = jnp.maximum(jnp.maximum(m0[...], m1[...]), m2[...])
        num = jnp.zeros_like(m)
        den = jnp.zeros_like(m)
        for nr, dr, mr in ((n0, d0, m0), (n1, d1, m1), (n2, d2, m2)):
            sc = jnp.exp(mr[...] - m)
            num = num + nr[...] * sc
            den = den + dr[...] * sc
        o = num / den
        o_ref[...] = o
        lse_ref[...] = m + jnp.log(den)
        ob_ref[...] = (o * _silu(z_ref[...])).astype(BF16)

    row = pl.BlockSpec((ROW_TILE, DIL_W), lambda i: (i, 0))
    flat = [a for p in parts for a in p]
    return pl.pallas_call(
        body, name="att_merge", grid=(s // ROW_TILE,),
        in_specs=[row] * 9 + [pl.BlockSpec((ROW_TILE, DIL_W), lambda i: (i, OFF_Z_B // DIL_W))],
        out_specs=[row, row, row],
        out_shape=[jax.ShapeDtypeStruct((s, DIL_W), BF16), jax.ShapeDtypeStruct((s, DIL_W), F32),
                   jax.ShapeDtypeStruct((s, DIL_W), F32)],
        compiler_params=_cparams("parallel"))(*flat, proj)


def _att_merge_bwd(o, proj, dob):
    s = o.shape[0]

    def body(o_ref, z_ref, d_ref, do_ref, dl_ref, dz_ref):
        ov, zv, dv = o_ref[...], z_ref[...], d_ref[...]
        do = dv * _silu(zv)
        do_ref[...] = do
        dz_ref[...] = (dv * ov * _silu_grad(zv)).astype(BF16)
        for h in range(DIL_HEADS):
            sl = slice(h * DIL_DH, (h + 1) * DIL_DH)
            dl_ref[:, sl] = jnp.broadcast_to(jnp.sum(do[:, sl] * ov[:, sl], axis=-1, keepdims=True), (ROW_TILE, DIL_DH))

    row = pl.BlockSpec((ROW_TILE, DIL_W), lambda i: (i, 0))
    return pl.pallas_call(
        body, name="att_merge_bwd", grid=(s // ROW_TILE,),
        in_specs=[row, pl.BlockSpec((ROW_TILE, DIL_W), lambda i: (i, OFF_Z_B // DIL_W)), row],
        out_specs=[row, row, row],
        out_shape=[jax.ShapeDtypeStruct((s, DIL_W), F32), jax.ShapeDtypeStruct((s, DIL_W), F32),
                   jax.ShapeDtypeStruct((s, DIL_W), BF16)],
        compiler_params=_cparams("parallel"))(o, proj, dob)


def _merge(proj, ya, yb):
    s = proj.shape[0]

    def body(ga_ref, gb_ref, ya_ref, yb_ref, o_ref):
        o_ref[...] = (_sigmoid(ga_ref[...]) * ya_ref[...] + _sigmoid(gb_ref[...]) * yb_ref[...]).astype(BF16)

    row = pl.BlockSpec((ROW_TILE, D_MODEL), lambda i: (i, 0))
    return pl.pallas_call(
        body, name="merge", grid=(s // ROW_TILE,),
        in_specs=[pl.BlockSpec((ROW_TILE, D_MODEL), lambda i: (i, OFF_G_A // D_MODEL)),
                  pl.BlockSpec((ROW_TILE, D_MODEL), lambda i: (i, OFF_G_B // D_MODEL)), row, row],
        out_specs=row, out_shape=jax.ShapeDtypeStruct((s, D_MODEL), BF16),
        compiler_params=_cparams("parallel"))(proj, proj, ya, yb)


def _merge_bwd(proj, ya, yb, dm):
    s = proj.shape[0]

    def body(ga_ref, gb_ref, ya_ref, yb_ref, dm_ref, dya_ref, dyb_ref, dga_ref, dgb_ref):
        dmv = dm_ref[...]
        sa, sb = _sigmoid(ga_ref[...]), _sigmoid(gb_ref[...])
        dya_ref[...] = (dmv * sa).astype(BF16)
        dyb_ref[...] = (dmv * sb).astype(BF16)
        dga_ref[...] = (dmv * ya_ref[...] * sa * (1.0 - sa)).astype(BF16)
        dgb_ref[...] = (dmv * yb_ref[...] * sb * (1.0 - sb)).astype(BF16)

    row = pl.BlockSpec((ROW_TILE, D_MODEL), lambda i: (i, 0))
    return pl.pallas_call(
        body, name="merge_bwd", grid=(s // ROW_TILE,),
        in_specs=[pl.BlockSpec((ROW_TILE, D_MODEL), lambda i: (i, OFF_G_A // D_MODEL)),
                  pl.BlockSpec((ROW_TILE, D_MODEL), lambda i: (i, OFF_G_B // D_MODEL)), row, row, row],
        out_specs=[row] * 4, out_shape=[jax.ShapeDtypeStruct((s, D_MODEL), BF16)] * 4,
        compiler_params=_cparams("parallel"))(proj, proj, ya, yb, dm)


def _final(x, t, fw, tgt):
    s, d = x.shape

    def body(x_ref, t_ref, w_ref, y_ref, dx_ref, dw_ref, l_ref):
        i = pl.program_id(0)
        x2 = x_ref[...] + t_ref[...]
        wv = w_ref[...]
        r = lax.rsqrt(jnp.mean(x2 * x2, axis=-1, keepdims=True) + NORM_EPS)
        e = x2 * r * wv - y_ref[...]
        lrow = jnp.mean(e * e, axis=-1, keepdims=True)
        lpart = jnp.broadcast_to(0.5 * jnp.sum(lrow, axis=0, keepdims=True), (1, 128))
        dy = e * (1.0 / d)
        dwp = jnp.sum(dy * x2 * r, axis=0, keepdims=True)
        dyw = dy * wv
        dx_ref[...] = r * dyw - x2 * (r * r * r) * jnp.mean(dyw * x2, axis=-1, keepdims=True)

        @pl.when(i == 0)
        def _():
            dw_ref[...] = dwp
            l_ref[...] = lpart

        @pl.when(i > 0)
        def _():
            dw_ref[...] += dwp
            l_ref[...] += lpart

    row = pl.BlockSpec((ROW_TILE, d), lambda i: (i, 0))
    vec = pl.BlockSpec((1, d), lambda i: (0, 0))
    return pl.pallas_call(
        body, name="final", grid=(s // ROW_TILE,), in_specs=[row, row, vec, row],
        out_specs=[row, vec, pl.BlockSpec((1, 128), lambda i: (0, 0))],
        out_shape=[jax.ShapeDtypeStruct((s, d), F32), jax.ShapeDtypeStruct((1, d), F32), jax.ShapeDtypeStruct((1, 128), F32)],
        compiler_params=_cparams("arbitrary"))(x, t, fw, tgt)


def _adamw(w, g, m, v, name):
    r, c = w.shape
    tr = r if r <= 128 else 128
    assert r % tr == 0

    def body(w_ref, g_ref, m_ref, v_ref, d_ref, nm_ref, nv_ref):
        gv = g_ref[...]
        mn = ADAM_B1 * m_ref[...] + (1.0 - ADAM_B1) * gv
        vn = ADAM_B2 * v_ref[...] + (1.0 - ADAM_B2) * (gv * gv)
        m_hat = mn / (1.0 - ADAM_B1 ** ADAM_STEP)
        v_hat = vn / (1.0 - ADAM_B2 ** ADAM_STEP)
        d_ref[...] = -ADAM_LR * (m_hat / (jnp.sqrt(v_hat) + ADAM_EPS) + ADAM_WD * w_ref[...])
        nm_ref[...] = mn
        nv_ref[...] = vn

    blk = pl.BlockSpec((tr, c), lambda i: (i, 0))
    return pl.pallas_call(
        body, name=name, grid=(r // tr,), in_specs=[blk] * 4, out_specs=[blk] * 3,
        out_shape=[jax.ShapeDtypeStruct((r, c), F32)] * 3, compiler_params=_cparams("parallel"))(w, g, m, v)


HBM_SPEC = pl.BlockSpec(memory_space=pl.ANY)


def _place():
    x, y, c = lax.axis_index("x"), lax.axis_index("y"), lax.axis_index("c")
    chips = [(1 - x, y), (x, 1 - y), (1 - x, 1 - y)]
    return x, y, c, chips


def _ag_weights(pack):
    _, rh, wd = pack.shape

    def body(p_ref, out_ref, send_sems, recv_sems, local_sem):
        x, y, c, chips = _place()
        me, sib, j = (x, y, c), (x, y, 1 - c), 2 * x + y

        def rc(k, src, dst, to):
            return pltpu.make_async_remote_copy(src_ref=src, dst_ref=dst, send_sem=send_sems.at[k],
                                                recv_sem=recv_sems.at[k], device_id=to, device_id_type=MESH)

        mine = pltpu.make_async_copy(p_ref, out_ref.at[j], local_sem)
        mine.start()
        first = [rc(k, p_ref.at[c], out_ref.at[j, c], (cx, cy, c)) for k, (cx, cy) in enumerate(chips)]
        for cp in first:
            cp.start()
        passed = []
        for k, (cx, cy) in enumerate(chips):
            land = out_ref.at[2 * cx + cy, c]
            rc(k, p_ref.at[c], land, me).wait_recv()
            fwd = rc(3 + k, land, land, sib)
            fwd.start()
            passed.append(fwd)
        for k, (cx, cy) in enumerate(chips):
            rc(3 + k, p_ref.at[c], out_ref.at[2 * cx + cy, 1 - c], me).wait_recv()
        for cp in first + passed:
            cp.wait_send()
        mine.wait()

    return pl.pallas_call(
        body, name="ag_weights", out_shape=jax.ShapeDtypeStruct((N_CHIPS, 2, rh, wd), pack.dtype),
        in_specs=[HBM_SPEC], out_specs=HBM_SPEC,
        scratch_shapes=[pltpu.SemaphoreType.DMA((6,)), pltpu.SemaphoreType.DMA((6,)), pltpu.SemaphoreType.DMA])(pack)


def _rs_pair(gpack):
    n, _, rh, wd = gpack.shape

    def body(g_ref, out_ref, send_sems, recv_sems):
        x, y, c, _ = _place()
        sib = (x, y, 1 - c)
        cps = [pltpu.make_async_remote_copy(src_ref=g_ref.at[p, 1 - c], dst_ref=out_ref.at[p], send_sem=send_sems.at[p],
                                            recv_sem=recv_sems.at[p], device_id=sib, device_id_type=MESH)
               for p in range(n)]
        for cp in cps:
            cp.start()
        for cp in cps:
            cp.wait_recv()
        for cp in cps:
            cp.wait_send()

    return pl.pallas_call(
        body, name="rs_pair", out_shape=jax.ShapeDtypeStruct((n, rh, wd), gpack.dtype),
        in_specs=[HBM_SPEC], out_specs=HBM_SPEC,
        scratch_shapes=[pltpu.SemaphoreType.DMA((n,)), pltpu.SemaphoreType.DMA((n,))])(gpack)


def _rs_chips(csum):
    n, rh, wd = csum.shape

    def body(s_ref, out_ref, send_sems, recv_sems, local_sem):
        x, y, c, chips = _place()
        j = 2 * x + y
        mine = pltpu.make_async_copy(s_ref.at[j], out_ref.at[j], local_sem)
        mine.start()
        cps = [pltpu.make_async_remote_copy(src_ref=s_ref.at[2 * cx + cy], dst_ref=out_ref.at[j], send_sem=send_sems.at[k],
                                            recv_sem=recv_sems.at[k], device_id=(cx, cy, c), device_id_type=MESH)
               for k, (cx, cy) in enumerate(chips)]
        for cp in cps:
            cp.start()
        for k, (cx, cy) in enumerate(chips):
            pltpu.make_async_remote_copy(src_ref=s_ref.at[j], dst_ref=out_ref.at[2 * cx + cy], send_sem=send_sems.at[k],
                                         recv_sem=recv_sems.at[k], device_id=(x, y, c), device_id_type=MESH).wait_recv()
        for cp in cps:
            cp.wait_send()
        mine.wait()

    return pl.pallas_call(
        body, name="rs_chips", out_shape=jax.ShapeDtypeStruct((n, rh, wd), csum.dtype),
        in_specs=[HBM_SPEC], out_specs=HBM_SPEC,
        scratch_shapes=[pltpu.SemaphoreType.DMA((3,)), pltpu.SemaphoreType.DMA((3,)), pltpu.SemaphoreType.DMA])(csum)


def _pair_swap(half):
    rh, wd = half.shape

    def body(h_ref, out_ref, send_sem, recv_sem, local_sem):
        x, y, c, _ = _place()
        mine = pltpu.make_async_copy(h_ref, out_ref.at[c], local_sem)
        mine.start()
        cp = pltpu.make_async_remote_copy(src_ref=h_ref, dst_ref=out_ref.at[c], send_sem=send_sem, recv_sem=recv_sem,
                                          device_id=(x, y, 1 - c), device_id_type=MESH)
        cp.start()
        pltpu.make_async_remote_copy(src_ref=h_ref, dst_ref=out_ref.at[1 - c], send_sem=send_sem, recv_sem=recv_sem,
                                     device_id=(x, y, c), device_id_type=MESH).wait_recv()
        cp.wait_send()
        mine.wait()

    return pl.pallas_call(
        body, name="pair_swap", out_shape=jax.ShapeDtypeStruct((2, rh, wd), half.dtype),
        in_specs=[HBM_SPEC], out_specs=HBM_SPEC,
        scratch_shapes=[pltpu.SemaphoreType.DMA, pltpu.SemaphoreType.DMA, pltpu.SemaphoreType.DMA])(half)


def _ag_small(v):
    m_per, n = v.shape

    def body(x_ref, out_ref, send_sems, recv_sems, local_sem):
        x, y, c, chips = _place()
        me, sibling = (x, y, c), (x, y, 1 - c)

        def rows(px, py, pc):
            return out_ref.at[pl.ds((4 * px + 2 * py + pc) * m_per, m_per), :]

        def copy(k, block, to, src=None):
            return pltpu.make_async_remote_copy(
                src_ref=rows(*block) if src is None else src, dst_ref=rows(*block), send_sem=send_sems.at[k],
                recv_sem=recv_sems.at[k], device_id=to, device_id_type=MESH)

        mine = pltpu.make_async_copy(x_ref, rows(*me), local_sem)
        mine.start()
        first = [copy(0, me, sibling, src=x_ref)]
        first += [copy(1 + k, me, (*chip, c), src=x_ref) for k, chip in enumerate(chips)]
        for cp in first:
            cp.start()
        passed = [copy(4 + k, (*chip, c), sibling) for k, chip in enumerate(chips)]
        for k, chip in enumerate(chips):
            copy(1 + k, (*chip, c), me).wait_recv()
            passed[k].start()
        copy(0, sibling, me).wait_recv()
        for k, chip in enumerate(chips):
            copy(4 + k, (*chip, 1 - c), me).wait_recv()
        for cp in first + passed:
            cp.wait_send()
        mine.wait()

    return pl.pallas_call(
        body, name="ag_small", out_shape=jax.ShapeDtypeStruct((8 * m_per, n), v.dtype),
        in_specs=[pl.BlockSpec(memory_space=pltpu.VMEM)], out_specs=pl.BlockSpec(memory_space=pltpu.VMEM),
        scratch_shapes=[pltpu.SemaphoreType.DMA((7,)), pltpu.SemaphoreType.DMA((7,)), pltpu.SemaphoreType.DMA])(v)


def _sum_blocks(a, nblk, name):
    rows, wd = a.shape
    r = rows // nblk
    tr = min(r, ROW_TILE)
    assert r % tr == 0

    def body(*refs):
        acc = refs[0][...].astype(F32)
        for ref in refs[1:nblk]:
            acc = acc + ref[...].astype(F32)
        refs[nblk][...] = acc

    nt = r // tr
    return pl.pallas_call(
        body, name=name, grid=(nt,),
        in_specs=[pl.BlockSpec((tr, wd), functools.partial(lambda i, b: (b * nt + i, 0), b=b)) for b in range(nblk)],
        out_specs=pl.BlockSpec((tr, wd), lambda i: (i, 0)),
        out_shape=jax.ShapeDtypeStruct((r, wd), F32), compiler_params=_cparams("parallel"))(*([a] * nblk))


def _add_halves(gpack, other, c):
    n, _, rh, wd = gpack.shape
    tr = ROW_TILE
    assert rh % tr == 0

    def body(c_ref, g_ref, o_ref, out_ref):
        out_ref[0] = (g_ref[0, 0] + o_ref[0]).astype(BF16)

    grid_spec = pltpu.PrefetchScalarGridSpec(
        num_scalar_prefetch=1, grid=(n, rh // tr),
        in_specs=[pl.BlockSpec((1, 1, tr, wd), lambda p, i, cr: (p, cr[0], i, 0)),
                  pl.BlockSpec((1, tr, wd), lambda p, i, cr: (p, i, 0))],
        out_specs=pl.BlockSpec((1, tr, wd), lambda p, i, cr: (p, i, 0)))
    return pl.pallas_call(
        body, name="add_halves", grid_spec=grid_spec, out_shape=jax.ShapeDtypeStruct((n, rh, wd), BF16),
        compiler_params=_cparams("parallel", "parallel"))(jnp.reshape(c, (1,)).astype(jnp.int32), gpack, other)


PACK_W = 1024
ROWS_W_IN = D_MODEL * SHARD_W // PACK_W
ROWS_O_DN = DN_W // N_CHIPS
ROWS_O_DIL = DIL_W * (D_MODEL // N_CHIPS) // PACK_W
ROWS_OUT = D_MODEL // N_CHIPS
ROWS_CONV = 4 * (3 * DN_W // N_CHIPS) // PACK_W
R0 = ROWS_W_IN
R1 = R0 + ROWS_O_DN
R2 = R1 + ROWS_O_DIL
R3 = R2 + ROWS_OUT
R4 = R3 + ROWS_CONV
R5 = R4 + ROWS_CONV
PACK_ROWS = 3584
HALF_ROWS = PACK_ROWS // 2


def _to_ref_layout(wp):
    return jnp.concatenate([wp[:, :REF_OFF_BA], wp[:, OFF_BA:OFF_BA + 2 * DN_HEADS], wp[:, REF_OFF_BA:OFF_BA]], axis=1)


def _from_ref_layout(w):
    pad = jnp.zeros((w.shape[0], PW - PROJ_W), w.dtype)
    return jnp.concatenate([w[:, :REF_OFF_BA], w[:, REF_OFF_BA + 2 * DN_HEADS:], w[:, REF_OFF_BA:REF_OFF_BA + 2 * DN_HEADS], pad],
                           axis=1)


def _local_step(x, tgt, norm_w, wp, conv_full, a_log, dt_bias, dn_norm_w, w_o_dn, w_o_dil, w_out, final_norm_w):
    s = x.shape[0]
    n = s // DN_CHUNK
    h = _rms_in(x, norm_w)
    proj = _matmul(h, wp, F32, 512, 1280, 1024, "proj")
    c_pre, qkv = _conv_fwd(proj, conv_full)
    gate_par = jnp.zeros((8, 128), F32).at[0, 8:16].set(a_log[0]).at[1, 8:16].set(dt_bias[0])
    bg = _gates_fwd(proj, gate_par)
    bgt = bg[:, :2 * DN_HEADS].T
    bcol = bgt[:DN_HEADS].reshape(DN_HEADS, s, 1)
    gcol = bgt[DN_HEADS:].reshape(DN_HEADS, s, 1)
    grow = bgt[DN_HEADS:].reshape(DN_HEADS, n, 1, DN_CHUNK)
    o_a, u, w, vn, tmat, states = _gdr_fwd(qkv, bcol, gcol, grow)
    oa2 = _gdr_out(o_a, proj, dn_norm_w)
    ya = _matmul(oa2, w_o_dn, F32, 512, 1024, 1024, "ya")
    parts = [_att_fwd(proj, g) for g in range(N_DIL)]
    ob, o_att, lse = _att_merge(parts, proj)
    yb = _matmul(ob, w_o_dil, F32, 512, 1024, 512, "yb")
    mg = _merge(proj, ya, yb)
    t = _matmul(mg, w_out, F32, 512, 1024, 1024, "t_out")
    dx2, dfw, lpart = _final(x, t, final_norm_w, tgt)

    dmg = _matmul(dx2, w_out.T, F32, 512, 1024, 1024, "d_merged")
    dw_out = _matmul(mg.T, dx2, F32, 1024, 1024, 1024, "dw_out")
    dya, dyb, dga, dgb = _merge_bwd(proj, ya, yb, dmg)
    doa2 = _matmul(dya, w_o_dn.T, F32, 512, 1024, 1024, "d_oa2")
    dw_o_dn = _matmul(oa2.T, dya, F32, 1024, 1024, 1024, "dw_o_dn")
    dob = _matmul(dyb, w_o_dil.T, F32, 512, 512, 1024, "d_ob")
    dw_o_dil = _matmul(ob.T, dyb, F32, 512, 1024, 1024, "dw_o_dil")
    do_a, dz_a, ddnw = _gdr_out_bwd(o_a, proj, dn_norm_w, doa2)
    dq_a, dk_a, dv_a, dbeta, dg = _gdr_bwd(qkv, bcol, gcol, grow, u, w, vn, tmat, states, do_a)
    dbg = jnp.concatenate([dbeta.reshape(DN_HEADS, s).T, dg.reshape(DN_HEADS, s).T,
                           jnp.zeros((s, 128 - 2 * DN_HEADS), F32)], axis=1)
    dba, dpar = _gates_bwd(proj, gate_par, dbg)
    dc = _conv_bwd_act(c_pre, dq_a, dk_a, dv_a)
    du_a, dconv = _conv_bwd(proj, dc, conv_full)
    do_att, delta, dz_b = _att_merge_bwd(o_att, proj, dob)
    dqkv_b = [_att_bwd(proj, g, do_att, lse, delta) for g in range(N_DIL)]
    dproj = jnp.concatenate(
        [du_a, dz_a] + [dqkv_b[g][i] for i in range(3) for g in range(N_DIL)]
        + [dz_b, dga, dgb, dba, jnp.zeros((s, PW - OFF_BA - 128), BF16)], axis=1)
    dh = _matmul(dproj, wp.T, F32, 512, 1024, 2304, "d_h")
    dwp = _matmul(h.T, dproj, F32, 1024, 1280, 1024, "dw_in")
    grad_x, dnw = _rms_in_bwd(x, norm_w, dh, dx2)
    small = jnp.zeros((8, PACK_W), F32)
    small = small.at[0].set(dnw[0]).at[1].set(dfw[0]).at[2, :DN_D].set(ddnw[0])
    small = small.at[3, :DN_HEADS].set(dpar[0, 8:16]).at[3, DN_HEADS:2 * DN_HEADS].set(dpar[1, 8:16])
    small = small.at[4, 0].set(lpart[0, 0])
    return grad_x, _to_ref_layout(dwp), dconv, dw_o_dn, dw_o_dil, dw_out, small


def kernel(x, norm_w, w_in, conv_w, a_log, dt_bias, dn_norm_w, w_o_dn, w_o_dil, w_out, final_norm_w, loss_target, m_norm_w, m_w_in, m_conv_w, m_a_log, m_dt_bias, m_dn_norm_w, m_w_o_dn, m_w_o_dil, m_w_out, m_final_norm_w, v_norm_w, v_w_in, v_conv_w, v_a_log, v_dt_bias, v_dn_norm_w, v_w_o_dn, v_w_o_dil, v_w_out, v_final_norm_w):
    c = lax.axis_index("c")
    qw = D_MODEL // N_CHIPS

    cw = conv_w[0].reshape(ROWS_CONV, PACK_W)
    cw_hi = cw.astype(BF16)
    cw_lo = (cw - cw_hi.astype(F32)).astype(BF16)
    pack = jnp.concatenate(
        [w_in[0].astype(BF16).reshape(ROWS_W_IN, PACK_W), w_o_dn[0].astype(BF16),
         w_o_dil[0].astype(BF16).reshape(ROWS_O_DIL, PACK_W), w_out[0].astype(BF16), cw_hi, cw_lo,
         jnp.zeros((PACK_ROWS - R5, PACK_W), BF16)], axis=0).reshape(2, HALF_ROWS, PACK_W)
    allw = _ag_weights(pack).reshape(N_CHIPS, PACK_ROWS, PACK_W)
    chips = range(N_CHIPS)
    w_in_full = jnp.concatenate([allw[k, :R0].reshape(D_MODEL, SHARD_W) for k in chips], axis=1)
    w_o_dn_full = jnp.concatenate([allw[k, R0:R1] for k in chips], axis=0)
    w_o_dil_full = jnp.concatenate([allw[k, R1:R2].reshape(DIL_W, qw) for k in chips], axis=1)
    w_out_full = jnp.concatenate([allw[k, R2:R3] for k in chips], axis=0)
    conv_full = jnp.concatenate(
        [(allw[k, R3:R4].astype(F32) + allw[k, R4:R5].astype(F32)).reshape(4, 3 * DN_W // N_CHIPS) for k in chips], axis=1)
    wp = _from_ref_layout(w_in_full)

    grad_x, dw_in, dconv, dw_o_dn, dw_o_dil, dw_out, small = _local_step(
        x[0], loss_target[0], norm_w, wp, conv_full, a_log, dt_bias, dn_norm_w, w_o_dn_full, w_o_dil_full, w_out_full,
        final_norm_w.reshape(1, D_MODEL))

    cq = 3 * DN_W // N_CHIPS
    gpack = jnp.stack([
        jnp.concatenate(
            [dw_in[:, k * SHARD_W:(k + 1) * SHARD_W].reshape(ROWS_W_IN, PACK_W), dw_o_dn[k * qw:(k + 1) * qw],
             dw_o_dil[:, k * qw:(k + 1) * qw].reshape(ROWS_O_DIL, PACK_W), dw_out[k * qw:(k + 1) * qw],
             dconv[:, k * cq:(k + 1) * cq].reshape(ROWS_CONV, PACK_W), jnp.zeros((PACK_ROWS - R4, PACK_W), F32)], axis=0)
        for k in chips]).reshape(N_CHIPS, 2, HALF_ROWS, PACK_W)
    from_sib = _rs_pair(gpack)
    csum = _add_halves(gpack, from_sib, c)
    by_src = _rs_chips(csum)
    half = _sum_blocks(by_src.reshape(N_CHIPS * HALF_ROWS, PACK_W), N_CHIPS, "sum_chips")
    g = _pair_swap(half).reshape(PACK_ROWS, PACK_W)
    g_w_in = g[:R0].reshape(D_MODEL, SHARD_W)
    g_w_o_dn = g[R0:R1]
    g_w_o_dil = g[R1:R2].reshape(DIL_W, qw)
    g_w_out = g[R2:R3]
    g_conv = g[R3:R4].reshape(4, cq)

    gs = _sum_blocks(_ag_small(small), 8, "sum_small")
    loss = gs[4, 0]
    w_small = jnp.zeros((8, PACK_W), F32)

    def pack_small(nw, fw, dnw_, al, db):
        t = w_small.at[0].set(nw[0]).at[1].set(fw).at[2, :DN_D].set(dnw_[0])
        return t.at[3, :DN_HEADS].set(al[0]).at[3, DN_HEADS:2 * DN_HEADS].set(db[0])

    sm = _adamw(pack_small(norm_w, final_norm_w, dn_norm_w, a_log, dt_bias), gs,
                pack_small(m_norm_w, m_final_norm_w, m_dn_norm_w, m_a_log, m_dt_bias),
                pack_small(v_norm_w, v_final_norm_w, v_dn_norm_w, v_a_log, v_dt_bias), "adamw_small")

    def unpack_small(t):
        return dict(norm_w=t[0:1], final_norm_w=t[1], dn_norm_w=t[2:3, :DN_D], a_log=t[3:4, :DN_HEADS],
                    dt_bias=t[3:4, DN_HEADS:2 * DN_HEADS])

    res = {"grad": unpack_small(gs)}
    for kind, arr in zip(("delta", "new_m", "new_v"), sm):
        res[kind] = unpack_small(arr)
    big = dict(w_in=(w_in, g_w_in, m_w_in, v_w_in), conv_w=(conv_w, g_conv, m_conv_w, v_conv_w),
               w_o_dn=(w_o_dn, g_w_o_dn, m_w_o_dn, v_w_o_dn), w_o_dil=(w_o_dil, g_w_o_dil, m_w_o_dil, v_w_o_dil),
               w_out=(w_out, g_w_out, m_w_out, v_w_out))
    for name, (wt, gt, mt, vt) in big.items():
        d, nm, nv = _adamw(wt[0], gt, mt[0], vt[0], "adamw_" + name)
        res["grad"][name] = gt[None]
        res["delta"][name], res["new_m"][name], res["new_v"][name] = d[None], nm[None], nv[None]
    order = ["norm_w", "w_in", "conv_w", "a_log", "dt_bias", "dn_norm_w", "w_o_dn", "w_o_dil", "w_out", "final_norm_w"]
    outs = [loss, grad_x[None]]
    for kind in ("grad", "delta", "new_m", "new_v"):
        outs += [res[kind][nm] for nm in order]
    return tuple(outs)
```

```python
import functools
import math

import jax
import jax.numpy as jnp
from jax import lax
from jax.experimental import pallas as pl
from jax.experimental.pallas import tpu as pltpu

F32 = jnp.float32
BF16 = jnp.bfloat16
MESH = pl.DeviceIdType.MESH

D_MODEL = 1024
DN_HEADS = 8
DN_D = 128
DN_CHUNK = 64
DN_W = DN_HEADS * DN_D
DIL_GROUPS = ((128, 1), (512, 4), (2048, 16))
N_DIL = len(DIL_GROUPS)
DIL_HEADS = 4
DIL_DH = 128
DIL_W = DIL_HEADS * DIL_DH
ATT_BLOCK = 128
NORM_EPS = 1e-6
PROJ_W = 11280
N_CHIPS = 4
SHARD_W = PROJ_W // N_CHIPS

OFF_QKV_A = 0
OFF_Z_A = 3072
OFF_Q_B = 4096
OFF_K_B = 5632
OFF_V_B = 7168
OFF_Z_B = 8704
OFF_G_A = 9216
OFF_G_B = 10240
OFF_BA = 11264
PW = 11520
REF_OFF_BA = 4096

ADAM_LR = 0.001
ADAM_B1 = 0.9
ADAM_B2 = 0.999
ADAM_EPS = 1e-08
ADAM_WD = 0.01
ADAM_STEP = 10

ROW_TILE = 256
NEG = -1e30


def _dot(a, b):
    return jnp.dot(a.astype(BF16), b.astype(BF16), preferred_element_type=F32)


def _dot_nt(a, b):
    return lax.dot_general(a.astype(BF16), b.astype(BF16), (((1,), (1,)), ((), ())), preferred_element_type=F32)


def _dot_tn(a, b):
    return lax.dot_general(a.astype(BF16), b.astype(BF16), (((0,), (0,)), ((), ())), preferred_element_type=F32)


def _split(a):
    hi = a.astype(BF16)
    lo = (a - hi.astype(F32)).astype(BF16)
    return hi, lo


def _dot_exact_lhs(c, a):
    hi, lo = _split(a)
    cb = c.astype(BF16)
    return jnp.dot(cb, hi, preferred_element_type=F32) + jnp.dot(cb, lo, preferred_element_type=F32)


def _dot_exact_rhs(a, c):
    hi, lo = _split(a)
    cb = c.astype(BF16)
    return jnp.dot(hi, cb, preferred_element_type=F32) + jnp.dot(lo, cb, preferred_element_type=F32)


def _dot_tn_exact_rhs(a, c):
    hi, lo = _split(a)
    cb = c.astype(BF16)
    dn = (((0,), (0,)), ((), ()))
    return (lax.dot_general(hi, cb, dn, preferred_element_type=F32)
            + lax.dot_general(lo, cb, dn, preferred_element_type=F32))


def _sigmoid(x):
    return 1.0 / (1.0 + jnp.exp(-x))


def _silu(x):
    return x * _sigmoid(x)


def _silu_grad(x):
    s = _sigmoid(x)
    return s * (1.0 + x * (1.0 - s))


def _softplus(x):
    return jnp.maximum(x, 0.0) + jnp.log(1.0 + jnp.exp(-jnp.abs(x)))


def _cparams(*sem):
    return pltpu.CompilerParams(dimension_semantics=sem)


def _matmul(a, b, out_dtype, tm, tn, tk, name, nt=False):
    m, kdim = a.shape
    n = b.shape[0] if nt else b.shape[1]
    tm, tn, tk = min(tm, m), min(tn, n), min(tk, kdim)
    assert m % tm == 0 and n % tn == 0 and kdim % tk == 0, (name, a.shape, b.shape, tm, tn, tk)
    nk = kdim // tk
    dot = _dot_nt if nt else _dot
    b_spec = (pl.BlockSpec((tn, tk), lambda i, j, k: (j, k)) if nt else pl.BlockSpec((tk, tn), lambda i, j, k: (k, j)))

    if nk == 1:
        def body(a_ref, b_ref, o_ref):
            o_ref[...] = dot(a_ref[...], b_ref[...]).astype(o_ref.dtype)
        scratch = []
    else:
        def body(a_ref, b_ref, o_ref, acc_ref):
            k = pl.program_id(2)
            p = dot(a_ref[...], b_ref[...])

            @pl.when(k == 0)
            def _():
                acc_ref[...] = p

            @pl.when(k > 0)
            def _():
                acc_ref[...] += p

            @pl.when(k == nk - 1)
            def _():
                o_ref[...] = acc_ref[...].astype(o_ref.dtype)
        scratch = [pltpu.VMEM((tm, tn), F32)]

    return pl.pallas_call(
        body, name=name, grid=(m // tm, n // tn, nk),
        in_specs=[pl.BlockSpec((tm, tk), lambda i, j, k: (i, k)), b_spec],
        out_specs=pl.BlockSpec((tm, tn), lambda i, j, k: (i, j)),
        out_shape=jax.ShapeDtypeStruct((m, n), out_dtype), scratch_shapes=scratch,
        compiler_params=_cparams("parallel", "parallel", "arbitrary"))(a, b)


def _rms_in(x, nw):
    s, d = x.shape

    def body(x_ref, w_ref, h_ref):
        xv = x_ref[...]
        r = lax.rsqrt(jnp.mean(xv * xv, axis=-1, keepdims=True) + NORM_EPS)
        h_ref[...] = (xv * r * w_ref[...]).astype(BF16)

    return pl.pallas_call(
        body, name="rms_in", grid=(s // ROW_TILE,),
        in_specs=[pl.BlockSpec((ROW_TILE, d), lambda i: (i, 0)), pl.BlockSpec((1, d), lambda i: (0, 0))],
        out_specs=pl.BlockSpec((ROW_TILE, d), lambda i: (i, 0)),
        out_shape=jax.ShapeDtypeStruct((s, d), BF16), compiler_params=_cparams("parallel"))(x, nw)


def _rms_in_bwd(x, nw, dh, dx2):
    s, d = x.shape

    def body(x_ref, w_ref, dh_ref, dx2_ref, dx_ref, dw_ref):
        i = pl.program_id(0)
        xv = x_ref[...]
        r = lax.rsqrt(jnp.mean(xv * xv, axis=-1, keepdims=True) + NORM_EPS)
        dhv = dh_ref[...]
        dyw = dhv * w_ref[...]
        dx_ref[...] = dx2_ref[...] + r * dyw - xv * (r * r * r) * jnp.mean(dyw * xv, axis=-1, keepdims=True)
        part = jnp.sum(dhv * xv * r, axis=0, keepdims=True)

        @pl.when(i == 0)
        def _():
            dw_ref[...] = part

        @pl.when(i > 0)
        def _():
            dw_ref[...] += part

    row = pl.BlockSpec((ROW_TILE, d), lambda i: (i, 0))
    vec = pl.BlockSpec((1, d), lambda i: (0, 0))
    return pl.pallas_call(
        body, name="rms_in_bwd", grid=(s // ROW_TILE,), in_specs=[row, vec, row, row], out_specs=[row, vec],
        out_shape=[jax.ShapeDtypeStruct((s, d), F32), jax.ShapeDtypeStruct((1, d), F32)],
        compiler_params=_cparams("arbitrary"))(x, nw, dh, dx2)


def _shift_down(cur, prev8, k):
    rc = pltpu.roll(cur, k, 0)
    rp = pltpu.roll(prev8, k, 0)
    row = lax.broadcasted_iota(jnp.int32, prev8.shape, 0)
    top = jnp.where(row < k, rp, rc[:8])
    return jnp.concatenate([top, rc[8:]], axis=0)


def _shift_up(cur, next8, k):
    t = cur.shape[0]
    rc = pltpu.roll(cur, t - k, 0)
    rn = pltpu.roll(next8, 8 - k, 0)
    row = lax.broadcasted_iota(jnp.int32, next8.shape, 0)
    bot = jnp.where(row >= 8 - k, rn, rc[t - 8:])
    return jnp.concatenate([rc[:t - 8], bot], axis=0)


def _conv_fwd(proj, conv_w):
    s = proj.shape[0]
    t8 = ROW_TILE // 8

    def body(u_ref, up_ref, w_ref, c_ref, y_ref):
        i = pl.program_id(0)
        part = pl.program_id(1)
        cur = u_ref[...]
        prev8 = jnp.where(i > 0, up_ref[...], 0.0)
        w = w_ref[...]
        c = cur * w[3:4, :]
        for k in (1, 2, 3):
            c = c + _shift_down(cur, prev8, k) * w[3 - k:4 - k, :]
        c_ref[...] = c
        a = _silu(c)
        for h in range(DN_HEADS):
            ah = a[:, h * DN_D:(h + 1) * DN_D]
            r = lax.rsqrt(jnp.sum(ah * ah, axis=-1, keepdims=True) + NORM_EPS)
            y_ref[:, h * DN_D:(h + 1) * DN_D] = jnp.where(part < 2, ah * r, ah)

    return pl.pallas_call(
        body, name="conv_fwd", grid=(s // ROW_TILE, 3),
        in_specs=[pl.BlockSpec((ROW_TILE, DN_W), lambda i, p: (i, p)),
                  pl.BlockSpec((8, DN_W), lambda i, p: (jnp.maximum(i * t8 - 1, 0), p)),
                  pl.BlockSpec((4, DN_W), lambda i, p: (0, p))],
        out_specs=[pl.BlockSpec((ROW_TILE, DN_W), lambda i, p: (i, p))] * 2,
        out_shape=[jax.ShapeDtypeStruct((s, 3 * DN_W), F32)] * 2,
        compiler_params=_cparams("parallel", "parallel"))(proj, proj, conv_w)


def _conv_bwd_act(c, dq, dk, dv):
    s = c.shape[0]

    def body(c_ref, dq_ref, dk_ref, dv_ref, dc_ref):
        for part, d_ref in enumerate((dq_ref, dk_ref, dv_ref)):
            for h in range(DN_HEADS):
                sl = slice(part * DN_W + h * DN_D, part * DN_W + (h + 1) * DN_D)
                ch = c_ref[:, sl]
                dyh = d_ref[:, h * DN_D:(h + 1) * DN_D]
                if part < 2:
                    ah = _silu(ch)
                    r = lax.rsqrt(jnp.sum(ah * ah, axis=-1, keepdims=True) + NORM_EPS)
                    dyh = r * dyh - ah * (r * r * r) * jnp.sum(dyh * ah, axis=-1, keepdims=True)
                dc_ref[:, sl] = dyh * _silu_grad(ch)

    wide = pl.BlockSpec((ROW_TILE, 3 * DN_W), lambda i: (i, 0))
    row = pl.BlockSpec((ROW_TILE, DN_W), lambda i: (i, 0))
    return pl.pallas_call(
        body, name="conv_bwd_act", grid=(s // ROW_TILE,), in_specs=[wide, row, row, row], out_specs=wide,
        out_shape=jax.ShapeDtypeStruct((s, 3 * DN_W), F32), compiler_params=_cparams("parallel"))(c, dq, dk, dv)


def _conv_bwd(proj, dc, conv_w):
    s = proj.shape[0]
    t8 = ROW_TILE // 8
    nrow = s // ROW_TILE
    last8 = s // 8 - 1

    def body(u_ref, up_ref, dc_ref, dcn_ref, w_ref, du_ref, dw_ref):
        i = pl.program_id(1)
        cur = u_ref[...]
        prev8 = jnp.where(i > 0, up_ref[...], 0.0)
        dcv = dc_ref[...]
        next8 = jnp.where(i < nrow - 1, dcn_ref[...], 0.0)
        w = w_ref[...]
        du = dcv * w[3:4, :]
        for k in (1, 2, 3):
            du = du + _shift_up(dcv, next8, k) * w[3 - k:4 - k, :]
        du_ref[...] = du.astype(BF16)

        @pl.when(i == 0)
        def _():
            dw_ref[...] = jnp.zeros_like(dw_ref)

        dw_ref[3:4, :] += jnp.sum(cur * dcv, axis=0, keepdims=True)
        for k in (1, 2, 3):
            dw_ref[3 - k:4 - k, :] += jnp.sum(_shift_down(cur, prev8, k) * dcv, axis=0, keepdims=True)

    blk = pl.BlockSpec((ROW_TILE, DN_W), lambda p, i: (i, p))
    return pl.pallas_call(
        body, name="conv_bwd", grid=(3, nrow),
        in_specs=[blk, pl.BlockSpec((8, DN_W), lambda p, i: (jnp.maximum(i * t8 - 1, 0), p)),
                  blk, pl.BlockSpec((8, DN_W), lambda p, i: (jnp.minimum((i + 1) * t8, last8), p)),
                  pl.BlockSpec((4, DN_W), lambda p, i: (0, p))],
        out_specs=[blk, pl.BlockSpec((4, DN_W), lambda p, i: (0, p))],
        out_shape=[jax.ShapeDtypeStruct((s, 3 * DN_W), BF16), jax.ShapeDtypeStruct((4, 3 * DN_W), F32)],
        compiler_params=_cparams("parallel", "arbitrary"))(proj, proj, dc, dc, conv_w)


def _gates_fwd(proj, gate_par):
    s = proj.shape[0]

    def body(ba_ref, par_ref, o_ref):
        v = ba_ref[...]
        lane = lax.broadcasted_iota(jnp.int32, v.shape, 1)
        beta = _sigmoid(v)
        g = -jnp.exp(par_ref[0:1, :]) * _softplus(v + par_ref[1:2, :])
        o_ref[...] = jnp.where(lane < DN_HEADS, beta, jnp.where(lane < 2 * DN_HEADS, g, 0.0))

    return pl.pallas_call(
        body, name="gates_fwd", grid=(s // ROW_TILE,),
        in_specs=[pl.BlockSpec((ROW_TILE, 128), lambda i: (i, OFF_BA // 128)), pl.BlockSpec((8, 128), lambda i: (0, 0))],
        out_specs=pl.BlockSpec((ROW_TILE, 128), lambda i: (i, 0)),
        out_shape=jax.ShapeDtypeStruct((s, 128), F32), compiler_params=_cparams("parallel"))(proj, gate_par)


def _gates_bwd(proj, gate_par, dbg):
    s = proj.shape[0]

    def body(ba_ref, par_ref, d_ref, o_ref, dpar_ref):
        i = pl.program_id(0)
        v = ba_ref[...]
        dv = d_ref[...]
        lane = lax.broadcasted_iota(jnp.int32, v.shape, 1)
        beta = _sigmoid(v)
        nega = -jnp.exp(par_ref[0:1, :])
        xs = v + par_ref[1:2, :]
        dsp = dv * nega * _sigmoid(xs)
        dal = dv * nega * _softplus(xs)
        is_b = lane < DN_HEADS
        is_g = jnp.logical_and(lane >= DN_HEADS, lane < 2 * DN_HEADS)
        o_ref[...] = jnp.where(is_b, dv * beta * (1.0 - beta), jnp.where(is_g, dsp, 0.0)).astype(BF16)
        r0 = jnp.sum(jnp.where(is_g, dal, 0.0), axis=0, keepdims=True)
        r1 = jnp.sum(jnp.where(is_g, dsp, 0.0), axis=0, keepdims=True)

        @pl.when(i == 0)
        def _():
            dpar_ref[...] = jnp.zeros_like(dpar_ref)

        dpar_ref[0:1, :] += r0
        dpar_ref[1:2, :] += r1

    return pl.pallas_call(
        body, name="gates_bwd", grid=(s // ROW_TILE,),
        in_specs=[pl.BlockSpec((ROW_TILE, 128), lambda i: (i, OFF_BA // 128)), pl.BlockSpec((8, 128), lambda i: (0, 0)),
                  pl.BlockSpec((ROW_TILE, 128), lambda i: (i, 0))],
        out_specs=[pl.BlockSpec((ROW_TILE, 128), lambda i: (i, 0)), pl.BlockSpec((8, 128), lambda i: (0, 0))],
        out_shape=[jax.ShapeDtypeStruct((s, 128), BF16), jax.ShapeDtypeStruct((8, 128), F32)],
        compiler_params=_cparams("arbitrary"))(proj, gate_par, dbg)


def _chunk_masks():
    c = DN_CHUNK
    ii = lax.broadcasted_iota(jnp.int32, (c, c), 0)
    jj = lax.broadcasted_iota(jnp.int32, (c, c), 1)
    return dict(ii=ii, jj=jj, lower=(ii >= jj), strict=(ii > jj), eye=(ii == jj),
                lower_f=(ii >= jj).astype(BF16), upper_f=(ii <= jj).astype(BF16), ones8=jnp.ones((8, c), BF16))


class _Heads:
    def __init__(self, xs):
        self.xs = list(xs)

    def _bin(self, o, f):
        if isinstance(o, _Heads):
            return _Heads([f(a, b) for a, b in zip(self.xs, o.xs)])
        return _Heads([f(a, o) for a in self.xs])

    def __add__(self, o):
        return self._bin(o, lambda a, b: a + b)

    def __sub__(self, o):
        return self._bin(o, lambda a, b: a - b)

    def __mul__(self, o):
        return self._bin(o, lambda a, b: a * b)

    __radd__ = __add__
    __rmul__ = __mul__

    def __neg__(self):
        return _Heads([-a for a in self.xs])

    def __getitem__(self, i):
        return _Heads([a[i] for a in self.xs])


def _hmap(f, *args):
    n = next(len(a.xs) for a in args if isinstance(a, _Heads))
    return _Heads([f(*[(a.xs[h] if isinstance(a, _Heads) else a) for a in args]) for h in range(n)])


def _hdot(a, b):
    return _hmap(_dot, a, b)


def _hdot_nt(a, b):
    return _hmap(_dot_nt, a, b)


def _hdot_tn(a, b):
    return _hmap(_dot_tn, a, b)


def _hsum(a, axis):
    return _hmap(lambda t: jnp.sum(t, axis=axis, keepdims=True), a)


def _hwhere(c, a, b):
    return _hmap(jnp.where, c, a, b)


def _chunk_common(mk, q, k, beta_col, g_col):
    c = DN_CHUNK
    lower, strict = mk["lower"], mk["strict"]
    qs = q * (DN_D ** -0.5)
    beta_b = _hmap(lambda t: jnp.broadcast_to(t, (c, DN_D)), beta_col)
    g_b = _hmap(lambda t: jnp.broadcast_to(t, (c, DN_D)), g_col)
    gc_b = _hmap(_dot_exact_lhs, mk["lower_f"], g_b)
    gc_sq = gc_b[:, :c]
    gc_r = _hmap(_dot_exact_lhs, mk["ones8"], _hwhere(mk["eye"], gc_sq, 0.0))[0:1, :]
    gam = _hwhere(lower, _hmap(lambda t: jnp.exp(jnp.minimum(t, 0.0)), gc_sq - gc_r), 0.0)
    egc = _hmap(jnp.exp, gc_b)
    gl = gc_b[c - 1:c, :]
    ekd = _hmap(jnp.exp, gl - gc_b)
    dl = _hmap(jnp.exp, gl)
    kb = k * beta_b
    a_strict = _hwhere(strict, _hdot_nt(kb, k) * gam, 0.0)
    aqk = _hwhere(lower, _hdot_nt(qs, k) * gam, 0.0)
    return dict(k=k, qs=qs, beta_b=beta_b, gc_b=gc_b, gam=gam, egc=egc, ekd=ekd, dl=dl, kb=kb, a_strict=a_strict, aqk=aqk)


def _unit_lower_inverse_minus_eye(n_strict, ii, jj):
    same = lax.shift_right_logical(ii, 4) == lax.shift_right_logical(jj, 4)
    dmat = _hwhere(same, n_strict, 0.0)
    omat = n_strict - dmat
    d2 = _hdot(dmat, dmat)
    d4 = _hdot(d2, d2)
    d8 = _hdot(d4, d4)
    x1 = d2 - dmat - _hdot(dmat, d2)
    x2 = x1 + d4 + _hdot(x1, d4)
    x3 = x2 + d8 + _hdot(x2, d8)
    n1 = omat + _hdot(x3, omat)
    n2 = _hdot(n1, n1)
    y = n2 - n1 - _hdot(n1, n2)
    return y + x3 + _hdot(y, x3)


def _gdr_fwd(qkv, bg):
    s = qkv.shape[0]
    c = DN_CHUNK
    n = s // c

    def body(q_ref, k_ref, v_ref, bg_ref, o_ref, u_ref, w_ref, vn_ref, tm_ref, st_ref, state):
        @pl.when(pl.program_id(0) == 0)
        def _():
            state[...] = jnp.zeros_like(state)

        mk = _chunk_masks()
        bg = bg_ref[...]
        hs = range(DN_HEADS)
        sls = [slice(h * DN_D, (h + 1) * DN_D) for h in hs]
        cm = _chunk_common(mk, _Heads(q_ref[:, sl] for sl in sls), _Heads(k_ref[:, sl] for sl in sls),
                           _Heads(bg[:, h:h + 1] for h in hs), _Heads(bg[:, DN_HEADS + h:DN_HEADS + h + 1] for h in hs))
        tm = _unit_lower_inverse_minus_eye(cm["a_strict"], mk["ii"], mk["jj"])
        rhs_u = _Heads(v_ref[:, sl] for sl in sls) * cm["beta_b"]
        rhs_w = cm["kb"] * cm["egc"]
        u = rhs_u + _hdot(tm, rhs_u)
        w = rhs_w + _hdot(tm, rhs_w)
        st = _Heads(state[h] for h in hs)
        v_new = u - _hdot(w, st)
        o = _hdot(cm["qs"] * cm["egc"], st) + _hdot(cm["aqk"], v_new)
        st_new = st * cm["dl"] + _hdot_tn(cm["k"] * cm["ekd"], v_new)
        for h, sl in zip(hs, sls):
            o_ref[:, sl] = o.xs[h]
            u_ref[:, sl] = u.xs[h]
            w_ref[:, sl] = w.xs[h]
            vn_ref[:, sl] = v_new.xs[h]
            tm_ref[h, 0] = tm.xs[h]
            st_ref[h, 0] = st.xs[h]
            state[h] = st_new.xs[h]

    def part(p):
        return pl.BlockSpec((c, DN_W), lambda j: (j, p))

    return pl.pallas_call(
        body, name="gdr_fwd", grid=(n,),
        in_specs=[part(0), part(1), part(2), pl.BlockSpec((c, 128), lambda j: (j, 0))],
        out_specs=[part(0)] * 4 + [pl.BlockSpec((DN_HEADS, 1, c, c), lambda j: (0, j, 0, 0)),
                                   pl.BlockSpec((DN_HEADS, 1, DN_D, DN_D), lambda j: (0, j, 0, 0))],
        out_shape=[jax.ShapeDtypeStruct((s, DN_W), F32)] * 4
        + [jax.ShapeDtypeStruct((DN_HEADS, n, c, c), F32), jax.ShapeDtypeStruct((DN_HEADS, n, DN_D, DN_D), F32)],
        scratch_shapes=[pltpu.VMEM((DN_HEADS, DN_D, DN_D), F32)],
        compiler_params=_cparams("arbitrary"))(qkv, qkv, qkv, bg)


def _gdr_bwd(qkv, bg, u, w, vn, tmat, states, do):
    s = qkv.shape[0]
    c = DN_CHUNK
    n = s // c

    def body(q_ref, k_ref, v_ref, bg_ref, u_ref, w_ref, vn_ref, tm_ref, st_ref, do_ref,
             dq_ref, dk_ref, dv_ref, dbg_ref, dstate):
        @pl.when(pl.program_id(0) == 0)
        def _():
            dstate[...] = jnp.zeros_like(dstate)

        mk = _chunk_masks()
        lower, strict = mk["lower"], mk["strict"]
        bg = bg_ref[...]
        ones = jnp.ones((c, DN_D), BF16)
        rowi = lax.broadcasted_iota(jnp.int32, (c, DN_D), 0)
        lane = lax.broadcasted_iota(jnp.int32, (c, 128), 1)
        hs = range(DN_HEADS)
        sls = [slice(h * DN_D, (h + 1) * DN_D) for h in hs]

        def heads_of(ref):
            return _Heads(ref[:, sl] for sl in sls)

        cm = _chunk_common(mk, heads_of(q_ref), heads_of(k_ref),
                           _Heads(bg[:, h:h + 1] for h in hs), _Heads(bg[:, DN_HEADS + h:DN_HEADS + h + 1] for h in hs))
        k, qs, beta_b = cm["k"], cm["qs"], cm["beta_b"]
        gam, egc, ekd, dl, kb = cm["gam"], cm["egc"], cm["ekd"], cm["dl"], cm["kb"]
        aqk, a_strict = cm["aqk"], cm["a_strict"]
        v, uu, ww, v_new, dov = heads_of(v_ref), heads_of(u_ref), heads_of(w_ref), heads_of(vn_ref), heads_of(do_ref)
        tm = _Heads(tm_ref[h, 0] for h in hs)
        st = _Heads(st_ref[h, 0] for h in hs)
        dsn = _Heads(dstate[h] for h in hs)
        qd = qs * egc
        kd = k * ekd

        dv_new = _hdot_tn(aqk, dov) + _hdot(kd, dsn)
        daqk = _hwhere(lower, _hdot_nt(dov, v_new), 0.0)
        dqd = _hdot_nt(dov, st)
        dkd = _hdot_nt(v_new, dsn)
        ddl = _hsum(_hsum(dsn * st, 1), 0)
        dw = -_hdot_nt(dv_new, st)
        ds_new = dsn * dl + _hdot_tn(qd, dov) - _hdot_tn(ww, dv_new)

        dru = dv_new + _hdot_tn(tm, dv_new)
        drw = dw + _hdot_tn(tm, dw)
        dn = _hwhere(strict, -(_hdot_nt(dru, uu) + _hdot_nt(drw, ww)), 0.0)
        dag = dn * gam
        dkb = _hdot(dag, k) + drw * egc
        dk = _hdot_tn(dag, kb)
        dqg = daqk * gam
        dqs = _hdot(dqg, k) + dqd * egc
        dk = dk + _hdot_tn(dqg, qs) + dkb * beta_b + dkd * ekd
        pmat = dn * a_strict + daqk * aqk
        tkd = _hsum(dkd * kd, -1)
        dgc = (_hsum(pmat, -1) - _hmap(_dot_tn_exact_rhs, pmat, ones) + _hsum(drw * (kb * egc), -1)
               + _hsum(dqd * qd, -1) - tkd)
        last = _hsum(tkd, 0) + ddl * dl
        dgc = dgc + _hwhere(rowi == c - 1, last, 0.0)
        dg = _hmap(_dot_exact_lhs, mk["upper_f"], dgc)
        dbeta = _hsum(dru * v, -1) + _hsum(dkb * k, -1)
        dq = dqs * (DN_D ** -0.5)
        dv = dru * beta_b

        dbg = jnp.zeros((c, 128), F32)
        for h, sl in zip(hs, sls):
            dq_ref[:, sl] = dq.xs[h]
            dk_ref[:, sl] = dk.xs[h]
            dv_ref[:, sl] = dv.xs[h]
            dstate[h] = ds_new.xs[h]
            dbg = dbg + jnp.where(lane == h, dbeta.xs[h], 0.0) + jnp.where(lane == DN_HEADS + h, dg.xs[h], 0.0)
        dbg_ref[...] = dbg

    def part(p):
        return pl.BlockSpec((c, DN_W), lambda j: (n - 1 - j, p))

    vec = pl.BlockSpec((c, 128), lambda j: (n - 1 - j, 0))
    return pl.pallas_call(
        body, name="gdr_bwd", grid=(n,),
        in_specs=[part(0), part(1), part(2), vec, part(0), part(0), part(0),
                  pl.BlockSpec((DN_HEADS, 1, c, c), lambda j: (0, n - 1 - j, 0, 0)),
                  pl.BlockSpec((DN_HEADS, 1, DN_D, DN_D), lambda j: (0, n - 1 - j, 0, 0)), part(0)],
        out_specs=[part(0), part(0), part(0), vec],
        out_shape=[jax.ShapeDtypeStruct((s, DN_W), F32)] * 3 + [jax.ShapeDtypeStruct((s, 128), F32)],
        scratch_shapes=[pltpu.VMEM((DN_HEADS, DN_D, DN_D), F32)],
        compiler_params=_cparams("arbitrary"))(qkv, qkv, qkv, bg, u, w, vn, tmat, states, do)


def _gdr_out(o, proj, dnw):
    s = o.shape[0]

    def body(o_ref, z_ref, w_ref, y_ref):
        ov, zv, wv = o_ref[...], z_ref[...], w_ref[...]
        for h in range(DN_HEADS):
            sl = slice(h * DN_D, (h + 1) * DN_D)
            oh = ov[:, sl]
            r = lax.rsqrt(jnp.mean(oh * oh, axis=-1, keepdims=True) + NORM_EPS)
            y_ref[:, sl] = ((oh * r * wv) * _silu(zv[:, sl])).astype(BF16)

    row = pl.BlockSpec((ROW_TILE, DN_W), lambda i: (i, 0))
    return pl.pallas_call(
        body, name="gdr_out", grid=(s // ROW_TILE,),
        in_specs=[row, pl.BlockSpec((ROW_TILE, DN_W), lambda i: (i, OFF_Z_A // DN_W)), pl.BlockSpec((1, DN_D), lambda i: (0, 0))],
        out_specs=row, out_shape=jax.ShapeDtypeStruct((s, DN_W), BF16), compiler_params=_cparams("parallel"))(o, proj, dnw)


def _gdr_out_bwd(o, proj, dnw, dy):
    s = o.shape[0]

    def body(o_ref, z_ref, w_ref, dy_ref, do_ref, dz_ref, dw_ref):
        i = pl.program_id(0)
        ov, zv, wv, dyv = o_ref[...], z_ref[...], w_ref[...], dy_ref[...]
        acc = jnp.zeros((1, DN_D), F32)
        for h in range(DN_HEADS):
            sl = slice(h * DN_D, (h + 1) * DN_D)
            oh, zh, dh = ov[:, sl], zv[:, sl], dyv[:, sl]
            r = lax.rsqrt(jnp.mean(oh * oh, axis=-1, keepdims=True) + NORM_EPS)
            dn = dh * _silu(zh)
            dz_ref[:, sl] = (dh * (oh * r * wv) * _silu_grad(zh)).astype(BF16)
            acc = acc + jnp.sum(dn * oh * r, axis=0, keepdims=True)
            dnw_ = dn * wv
            do_ref[:, sl] = r * dnw_ - oh * (r * r * r) * jnp.mean(dnw_ * oh, axis=-1, keepdims=True)

        @pl.when(i == 0)
        def _():
            dw_ref[...] = acc

        @pl.when(i > 0)
        def _():
            dw_ref[...] += acc

    row = pl.BlockSpec((ROW_TILE, DN_W), lambda i: (i, 0))
    vec = pl.BlockSpec((1, DN_D), lambda i: (0, 0))
    return pl.pallas_call(
        body, name="gdr_out_bwd", grid=(s // ROW_TILE,),
        in_specs=[row, pl.BlockSpec((ROW_TILE, DN_W), lambda i: (i, OFF_Z_A // DN_W)), vec, row],
        out_specs=[row, row, vec],
        out_shape=[jax.ShapeDtypeStruct((s, DN_W), F32), jax.ShapeDtypeStruct((s, DN_W), BF16),
                   jax.ShapeDtypeStruct((1, DN_D), F32)],
        compiler_params=_cparams("arbitrary"))(o, proj, dnw, dy)


def _slope(group, head):
    idx = (group * DIL_HEADS + head + 1).astype(F32)
    return jnp.exp(jnp.full((1, 128), -8.0 * math.log(2.0) / (N_DIL * DIL_HEADS), F32) * idx)


def _att_scores(qb, k_cur, k_prev, slope_d, has_prev):
    iq = lax.broadcasted_iota(jnp.int32, (ATT_BLOCK, ATT_BLOCK), 0)
    jk = lax.broadcasted_iota(jnp.int32, (ATT_BLOCK, ATT_BLOCK), 1)
    dist_c = (iq - jk).astype(F32)
    s_cur = jnp.where(iq >= jk, _dot_nt(qb, k_cur) - slope_d * dist_c, NEG)
    s_prev = jnp.where(jnp.logical_and(jk >= iq, has_prev),
                       _dot_nt(qb, k_prev) - slope_d * (dist_c + float(ATT_BLOCK)), NEG)
    return s_cur, s_prev


def _att_fwd(proj, group):
    s = proj.shape[0]
    dil = DIL_GROUPS[group][1]
    assert DIL_GROUPS[group][0] // dil == ATT_BLOCK
    nb = s // dil // ATT_BLOCK
    assert nb * dil * ATT_BLOCK == s

    def body(q_ref, k_ref, v_ref, num_ref, den_ref, mx_ref):
        slope_d = _slope(group, pl.program_id(0)) * float(dil)

        def step(t, carry):
            r = lax.div(t, nb)
            j = lax.rem(t, nb)
            base = r + dil * ATT_BLOCK * j
            pbase = base - dil * ATT_BLOCK * jnp.minimum(j, 1)
            if dil == 1:
                base, pbase = pl.multiple_of(base, ATT_BLOCK), pl.multiple_of(pbase, ATT_BLOCK)
            cur = pl.ds(base, ATT_BLOCK, stride=dil)
            prv = pl.ds(pbase, ATT_BLOCK, stride=dil)
            qb = q_ref[cur, :] * (DIL_DH ** -0.5)
            s_cur, s_prev = _att_scores(qb, k_ref[cur, :], k_ref[prv, :], slope_d, j > 0)
            mx = jnp.maximum(jnp.max(s_cur, axis=-1, keepdims=True), jnp.max(s_prev, axis=-1, keepdims=True))
            p_cur = jnp.exp(s_cur - mx)
            p_prev = jnp.exp(s_prev - mx)
            den = jnp.sum(p_cur, axis=-1, keepdims=True) + jnp.sum(p_prev, axis=-1, keepdims=True)
            num_ref[cur, :] = _dot(p_cur, v_ref[cur, :]) + _dot(p_prev, v_ref[prv, :])
            den_ref[cur, :] = jnp.broadcast_to(den, (ATT_BLOCK, DIL_DH))
            mx_ref[cur, :] = jnp.broadcast_to(mx, (ATT_BLOCK, DIL_DH))
            return carry

        lax.fori_loop(0, dil * nb, step, 0)

    def col(off):
        return pl.BlockSpec((s, DIL_DH), lambda h: (0, off // DIL_DH + group * DIL_HEADS + h))

    out = pl.BlockSpec((s, DIL_DH), lambda h: (0, h))
    return pl.pallas_call(
        body, name=f"att_fwd{group}", grid=(DIL_HEADS,), in_specs=[col(OFF_Q_B), col(OFF_K_B), col(OFF_V_B)],
        out_specs=[out, out, out], out_shape=[jax.ShapeDtypeStruct((s, DIL_W), F32)] * 3,
        compiler_params=_cparams("parallel"))(proj, proj, proj)


def _att_bwd(proj, group, do, lse, delta):
    s = proj.shape[0]
    dil = DIL_GROUPS[group][1]
    nb = s // dil // ATT_BLOCK

    def body(q_ref, k_ref, v_ref, do_ref, lse_ref, dl_ref, dq_ref, dk_ref, dv_ref, dq_acc, dk_acc, dv_acc):
        slope_d = _slope(group, pl.program_id(0)) * float(dil)
        dk_acc[...] = jnp.zeros_like(dk_acc)
        dv_acc[...] = jnp.zeros_like(dv_acc)

        def step(t, carry):
            r = lax.div(t, nb)
            j = lax.rem(t, nb)
            base = r + dil * ATT_BLOCK * j
            pbase = base - dil * ATT_BLOCK * jnp.minimum(j, 1)
            if dil == 1:
                base, pbase = pl.multiple_of(base, ATT_BLOCK), pl.multiple_of(pbase, ATT_BLOCK)
            cur = pl.ds(base, ATT_BLOCK, stride=dil)
            prv = pl.ds(pbase, ATT_BLOCK, stride=dil)
            qb = q_ref[cur, :] * (DIL_DH ** -0.5)
            k_cur, k_prev, v_cur, v_prev = k_ref[cur, :], k_ref[prv, :], v_ref[cur, :], v_ref[prv, :]
            s_cur, s_prev = _att_scores(qb, k_cur, k_prev, slope_d, j > 0)
            lse_b, delta_b, dob = lse_ref[cur, :], dl_ref[cur, :], do_ref[cur, :]
            p_cur = jnp.exp(s_cur - lse_b)
            p_prev = jnp.exp(s_prev - lse_b)
            ds_cur = p_cur * (_dot_nt(dob, v_cur) - delta_b)
            ds_prev = p_prev * (_dot_nt(dob, v_prev) - delta_b)
            dq_acc[cur, :] = (_dot(ds_cur, k_cur) + _dot(ds_prev, k_prev)) * (DIL_DH ** -0.5)
            dk_acc[cur, :] += _dot_tn(ds_cur, qb)
            dv_acc[cur, :] += _dot_tn(p_cur, dob)

            @pl.when(j > 0)
            def _():
                dk_acc[prv, :] += _dot_tn(ds_prev, qb)
                dv_acc[prv, :] += _dot_tn(p_prev, dob)

            return carry

        lax.fori_loop(0, dil * nb, step, 0)
        dq_ref[...] = dq_acc[...].astype(BF16)
        dk_ref[...] = dk_acc[...].astype(BF16)
        dv_ref[...] = dv_acc[...].astype(BF16)

    def col(off):
        return pl.BlockSpec((s, DIL_DH), lambda h: (0, off // DIL_DH + group * DIL_HEADS + h))

    hd = pl.BlockSpec((s, DIL_DH), lambda h: (0, h))
    return pl.pallas_call(
        body, name=f"att_bwd{group}", grid=(DIL_HEADS,),
        in_specs=[col(OFF_Q_B), col(OFF_K_B), col(OFF_V_B), hd, hd, hd], out_specs=[hd, hd, hd],
        out_shape=[jax.ShapeDtypeStruct((s, DIL_W), BF16)] * 3,
        scratch_shapes=[pltpu.VMEM((s, DIL_DH), F32)] * 3,
        compiler_params=_cparams("parallel"))(proj, proj, proj, do, lse, delta)


def _att_merge(parts, proj):
    s = proj.shape[0]

    def body(n0, d0, m0, n1, d1, m1, n2, d2, m2, z_ref, ob_ref, o_ref, lse_ref):
        m = jnp.maximum(jnp.maximum(m0[...], m1[...]), m2[...])
        num = jnp.zeros_like(m)
        den = jnp.zeros_like(m)
        for nr, dr, mr in ((n0, d0, m0), (n1, d1, m1), (n2, d2, m2)):
            sc = jnp.exp(mr[...] - m)
            num = num + nr[...] * sc
            den = den + dr[...] * sc
        o = num / den
        o_ref[...] = o
        lse_ref[...] = m + jnp.log(den)
        ob_ref[...] = (o * _silu(z_ref[...])).astype(BF16)

    row = pl.BlockSpec((ROW_TILE, DIL_W), lambda i: (i, 0))
    flat = [a for p in parts for a in p]
    return pl.pallas_call(
        body, name="att_merge", grid=(s // ROW_TILE,),
        in_specs=[row] * 9 + [pl.BlockSpec((ROW_TILE, DIL_W), lambda i: (i, OFF_Z_B // DIL_W))],
        out_specs=[row, row, row],
        out_shape=[jax.ShapeDtypeStruct((s, DIL_W), BF16), jax.ShapeDtypeStruct((s, DIL_W), F32),
                   jax.ShapeDtypeStruct((s, DIL_W), F32)],
        compiler_params=_cparams("parallel"))(*flat, proj)


def _att_merge_bwd(o, proj, dob):
    s = o.shape[0]

    def body(o_ref, z_ref, d_ref, do_ref, dl_ref, dz_ref):
        ov, zv, dv = o_ref[...], z_ref[...], d_ref[...]
        do = dv * _silu(zv)
        do_ref[...] = do
        dz_ref[...] = (dv * ov * _silu_grad(zv)).astype(BF16)
        for h in range(DIL_HEADS):
            sl = slice(h * DIL_DH, (h + 1) * DIL_DH)
            dl_ref[:, sl] = jnp.broadcast_to(jnp.sum(do[:, sl] * ov[:, sl], axis=-1, keepdims=True), (ROW_TILE, DIL_DH))

    row = pl.BlockSpec((ROW_TILE, DIL_W), lambda i: (i, 0))
    return pl.pallas_call(
        body, name="att_merge_bwd", grid=(s // ROW_TILE,),
        in_specs=[row, pl.BlockSpec((ROW_TILE, DIL_W), lambda i: (i, OFF_Z_B // DIL_W)), row],
        out_specs=[row, row, row],
        out_shape=[jax.ShapeDtypeStruct((s, DIL_W), F32), jax.ShapeDtypeStruct((s, DIL_W), F32),
                   jax.ShapeDtypeStruct((s, DIL_W), BF16)],
        compiler_params=_cparams("parallel"))(o, proj, dob)


def _merge(proj, ya, yb):
    s = proj.shape[0]

    def body(ga_ref, gb_ref, ya_ref, yb_ref, o_ref):
        o_ref[...] = (_sigmoid(ga_ref[...]) * ya_ref[...] + _sigmoid(gb_ref[...]) * yb_ref[...]).astype(BF16)

    row = pl.BlockSpec((ROW_TILE, D_MODEL), lambda i: (i, 0))
    return pl.pallas_call(
        body, name="merge", grid=(s // ROW_TILE,),
        in_specs=[pl.BlockSpec((ROW_TILE, D_MODEL), lambda i: (i, OFF_G_A // D_MODEL)),
                  pl.BlockSpec((ROW_TILE, D_MODEL), lambda i: (i, OFF_G_B // D_MODEL)), row, row],
        out_specs=row, out_shape=jax.ShapeDtypeStruct((s, D_MODEL), BF16),
        compiler_params=_cparams("parallel"))(proj, proj, ya, yb)


def _merge_bwd(proj, ya, yb, dm):
    s = proj.shape[0]

    def body(ga_ref, gb_ref, ya_ref, yb_ref, dm_ref, dya_ref, dyb_ref, dga_ref, dgb_ref):
        dmv = dm_ref[...]
        sa, sb = _sigmoid(ga_ref[...]), _sigmoid(gb_ref[...])
        dya_ref[...] = (dmv * sa).astype(BF16)
        dyb_ref[...] = (dmv * sb).astype(BF16)
        dga_ref[...] = (dmv * ya_ref[...] * sa * (1.0 - sa)).astype(BF16)
        dgb_ref[...] = (dmv * yb_ref[...] * sb * (1.0 - sb)).astype(BF16)

    row = pl.BlockSpec((ROW_TILE, D_MODEL), lambda i: (i, 0))
    return pl.pallas_call(
        body, name="merge_bwd", grid=(s // ROW_TILE,),
        in_specs=[pl.BlockSpec((ROW_TILE, D_MODEL), lambda i: (i, OFF_G_A // D_MODEL)),
                  pl.BlockSpec((ROW_TILE, D_MODEL), lambda i: (i, OFF_G_B // D_MODEL)), row, row, row],
        out_specs=[row] * 4, out_shape=[jax.ShapeDtypeStruct((s, D_MODEL), BF16)] * 4,
        compiler_params=_cparams("parallel"))(proj, proj, ya, yb, dm)


def _final(x, t, fw, tgt):
    s, d = x.shape

    def body(x_ref, t_ref, w_ref, y_ref, dx_ref, dw_ref, l_ref):
        i = pl.program_id(0)
        x2 = x_ref[...] + t_ref[...]
        wv = w_ref[...]
        r = lax.rsqrt(jnp.mean(x2 * x2, axis=-1, keepdims=True) + NORM_EPS)
        e = x2 * r * wv - y_ref[...]
        lrow = jnp.mean(e * e, axis=-1, keepdims=True)
        lpart = jnp.broadcast_to(0.5 * jnp.sum(lrow, axis=0, keepdims=True), (1, 128))
        dy = e * (1.0 / d)
        dwp = jnp.sum(dy * x2 * r, axis=0, keepdims=True)
        dyw = dy * wv
        dx_ref[...] = r * dyw - x2 * (r * r * r) * jnp.mean(dyw * x2, axis=-1, keepdims=True)

        @pl.when(i == 0)
        def _():
            dw_ref[...] = dwp
            l_ref[...] = lpart

        @pl.when(i > 0)
        def _():
            dw_ref[...] += dwp
            l_ref[...] += lpart

    row = pl.BlockSpec((ROW_TILE, d), lambda i: (i, 0))
    vec = pl.BlockSpec((1, d), lambda i: (0, 0))
    return pl.pallas_call(
        body, name="final", grid=(s // ROW_TILE,), in_specs=[row, row, vec, row],
        out_specs=[row, vec, pl.BlockSpec((1, 128), lambda i: (0, 0))],
        out_shape=[jax.ShapeDtypeStruct((s, d), F32), jax.ShapeDtypeStruct((1, d), F32), jax.ShapeDtypeStruct((1, 128), F32)],
        compiler_params=_cparams("arbitrary"))(x, t, fw, tgt)


def _adamw(w, g, m, v, name):
    r, c = w.shape
    tr = r if r <= 128 else 128
    assert r % tr == 0

    def body(w_ref, g_ref, m_ref, v_ref, d_ref, nm_ref, nv_ref):
        gv = g_ref[...]
        mn = ADAM_B1 * m_ref[...] + (1.0 - ADAM_B1) * gv
        vn = ADAM_B2 * v_ref[...] + (1.0 - ADAM_B2) * (gv * gv)
        m_hat = mn / (1.0 - ADAM_B1 ** ADAM_STEP)
        v_hat = vn / (1.0 - ADAM_B2 ** ADAM_STEP)
        d_ref[...] = -ADAM_LR * (m_hat / (jnp.sqrt(v_hat) + ADAM_EPS) + ADAM_WD * w_ref[...])
        nm_ref[...] = mn
        nv_ref[...] = vn

    blk = pl.BlockSpec((tr, c), lambda i: (i, 0))
    return pl.pallas_call(
        body, name=name, grid=(r // tr,), in_specs=[blk] * 4, out_specs=[blk] * 3,
        out_shape=[jax.ShapeDtypeStruct((r, c), F32)] * 3, compiler_params=_cparams("parallel"))(w, g, m, v)


HBM_SPEC = pl.BlockSpec(memory_space=pl.ANY)


def _place():
    x, y, c = lax.axis_index("x"), lax.axis_index("y"), lax.axis_index("c")
    chips = [(1 - x, y), (x, 1 - y), (1 - x, 1 - y)]
    return x, y, c, chips


def _ag_weights(pack):
    _, rh, wd = pack.shape

    def body(p_ref, out_ref, send_sems, recv_sems, local_sem):
        x, y, c, chips = _place()
        me, sib, j = (x, y, c), (x, y, 1 - c), 2 * x + y

        def rc(k, src, dst, to):
            return pltpu.make_async_remote_copy(src_ref=src, dst_ref=dst, send_sem=send_sems.at[k],
                                                recv_sem=recv_sems.at[k], device_id=to, device_id_type=MESH)

        mine = pltpu.make_async_copy(p_ref, out_ref.at[j], local_sem)
        mine.start()
        first = [rc(k, p_ref.at[c], out_ref.at[j, c], (cx, cy, c)) for k, (cx, cy) in enumerate(chips)]
        for cp in first:
            cp.start()
        passed = []
        for k, (cx, cy) in enumerate(chips):
            land = out_ref.at[2 * cx + cy, c]
            rc(k, p_ref.at[c], land, me).wait_recv()
            fwd = rc(3 + k, land, land, sib)
            fwd.start()
            passed.append(fwd)
        for k, (cx, cy) in enumerate(chips):
            rc(3 + k, p_ref.at[c], out_ref.at[2 * cx + cy, 1 - c], me).wait_recv()
        for cp in first + passed:
            cp.wait_send()
        mine.wait()

    return pl.pallas_call(
        body, name="ag_weights", out_shape=jax.ShapeDtypeStruct((N_CHIPS, 2, rh, wd), pack.dtype),
        in_specs=[HBM_SPEC], out_specs=HBM_SPEC,
        scratch_shapes=[pltpu.SemaphoreType.DMA((6,)), pltpu.SemaphoreType.DMA((6,)), pltpu.SemaphoreType.DMA])(pack)


def _rs_pair(gpack):
    n, _, rh, wd = gpack.shape

    def body(g_ref, out_ref, send_sems, recv_sems):
        x, y, c, _ = _place()
        sib = (x, y, 1 - c)
        cps = [pltpu.make_async_remote_copy(src_ref=g_ref.at[p, 1 - c], dst_ref=out_ref.at[p], send_sem=send_sems.at[p],
                                            recv_sem=recv_sems.at[p], device_id=sib, device_id_type=MESH)
               for p in range(n)]
        for cp in cps:
            cp.start()
        for cp in cps:
            cp.wait_recv()
        for cp in cps:
            cp.wait_send()

    return pl.pallas_call(
        body, name="rs_pair", out_shape=jax.ShapeDtypeStruct((n, rh, wd), gpack.dtype),
        in_specs=[HBM_SPEC], out_specs=HBM_SPEC,
        scratch_shapes=[pltpu.SemaphoreType.DMA((n,)), pltpu.SemaphoreType.DMA((n,))])(gpack)


def _rs_chips(csum):
    n, rh, wd = csum.shape

    def body(s_ref, out_ref, send_sems, recv_sems, local_sem):
        x, y, c, chips = _place()
        j = 2 * x + y
        mine = pltpu.make_async_copy(s_ref.at[j], out_ref.at[j], local_sem)
        mine.start()
        cps = [pltpu.make_async_remote_copy(src_ref=s_ref.at[2 * cx + cy], dst_ref=out_ref.at[j], send_sem=send_sems.at[k],
                                            recv_sem=recv_sems.at[k], device_id=(cx, cy, c), device_id_type=MESH)
               for k, (cx, cy) in enumerate(chips)]
        for cp in cps:
            cp.start()
        for k, (cx, cy) in enumerate(chips):
            pltpu.make_async_remote_copy(src_ref=s_ref.at[j], dst_ref=out_ref.at[2 * cx + cy], send_sem=send_sems.at[k],
                                         recv_sem=recv_sems.at[k], device_id=(x, y, c), device_id_type=MESH).wait_recv()
        for cp in cps:
            cp.wait_send()
        mine.wait()

    return pl.pallas_call(
        body, name="rs_chips", out_shape=jax.ShapeDtypeStruct((n, rh, wd), csum.dtype),
        in_specs=[HBM_SPEC], out_specs=HBM_SPEC,
        scratch_shapes=[pltpu.SemaphoreType.DMA((3,)), pltpu.SemaphoreType.DMA((3,)), pltpu.SemaphoreType.DMA])(csum)


def _pair_swap(half):
    rh, wd = half.shape

    def body(h_ref, out_ref, send_sem, recv_sem, local_sem):
        x, y, c, _ = _place()
        mine = pltpu.make_async_copy(h_ref, out_ref.at[c], local_sem)
        mine.start()
        cp = pltpu.make_async_remote_copy(src_ref=h_ref, dst_ref=out_ref.at[c], send_sem=send_sem, recv_sem=recv_sem,
                                          device_id=(x, y, 1 - c), device_id_type=MESH)
        cp.start()
        pltpu.make_async_remote_copy(src_ref=h_ref, dst_ref=out_ref.at[1 - c], send_sem=send_sem, recv_sem=recv_sem,
                                     device_id=(x, y, c), device_id_type=MESH).wait_recv()
        cp.wait_send()
        mine.wait()

    return pl.pallas_call(
        body, name="pair_swap", out_shape=jax.ShapeDtypeStruct((2, rh, wd), half.dtype),
        in_specs=[HBM_SPEC], out_specs=HBM_SPEC,
        scratch_shapes=[pltpu.SemaphoreType.DMA, pltpu.SemaphoreType.DMA, pltpu.SemaphoreType.DMA])(half)


def _ag_small(v):
    m_per, n = v.shape

    def body(x_ref, out_ref, send_sems, recv_sems, local_sem):
        x, y, c, chips = _place()
        me, sibling = (x, y, c), (x, y, 1 - c)

        def rows(px, py, pc):
            return out_ref.at[pl.ds((4 * px + 2 * py + pc) * m_per, m_per), :]

        def copy(k, block, to, src=None):
            return pltpu.make_async_remote_copy(
                src_ref=rows(*block) if src is None else src, dst_ref=rows(*block), send_sem=send_sems.at[k],
                recv_sem=recv_sems.at[k], device_id=to, device_id_type=MESH)

        mine = pltpu.make_async_copy(x_ref, rows(*me), local_sem)
        mine.start()
        first = [copy(0, me, sibling, src=x_ref)]
        first += [copy(1 + k, me, (*chip, c), src=x_ref) for k, chip in enumerate(chips)]
        for cp in first:
            cp.start()
        passed = [copy(4 + k, (*chip, c), sibling) for k, chip in enumerate(chips)]
        for k, chip in enumerate(chips):
            copy(1 + k, (*chip, c), me).wait_recv()
            passed[k].start()
        copy(0, sibling, me).wait_recv()
        for k, chip in enumerate(chips):
            copy(4 + k, (*chip, 1 - c), me).wait_recv()
        for cp in first + passed:
            cp.wait_send()
        mine.wait()

    return pl.pallas_call(
        body, name="ag_small", out_shape=jax.ShapeDtypeStruct((8 * m_per, n), v.dtype),
        in_specs=[pl.BlockSpec(memory_space=pltpu.VMEM)], out_specs=pl.BlockSpec(memory_space=pltpu.VMEM),
        scratch_shapes=[pltpu.SemaphoreType.DMA((7,)), pltpu.SemaphoreType.DMA((7,)), pltpu.SemaphoreType.DMA])(v)


def _sum_blocks(a, nblk, name):
    rows, wd = a.shape
    r = rows // nblk
    tr = min(r, ROW_TILE)
    assert r % tr == 0

    def body(*refs):
        acc = refs[0][...].astype(F32)
        for ref in refs[1:nblk]:
            acc = acc + ref[...].astype(F32)
        refs[nblk][...] = acc

    nt = r // tr
    return pl.pallas_call(
        body, name=name, grid=(nt,),
        in_specs=[pl.BlockSpec((tr, wd), functools.partial(lambda i, b: (b * nt + i, 0), b=b)) for b in range(nblk)],
        out_specs=pl.BlockSpec((tr, wd), lambda i: (i, 0)),
        out_shape=jax.ShapeDtypeStruct((r, wd), F32), compiler_params=_cparams("parallel"))(*([a] * nblk))


def _add_halves(gpack, other, c):
    n, _, rh, wd = gpack.shape
    tr = ROW_TILE
    assert rh % tr == 0

    def body(c_ref, g_ref, o_ref, out_ref):
        out_ref[0] = (g_ref[0, 0] + o_ref[0]).astype(BF16)

    grid_spec = pltpu.PrefetchScalarGridSpec(
        num_scalar_prefetch=1, grid=(n, rh // tr),
        in_specs=[pl.BlockSpec((1, 1, tr, wd), lambda p, i, cr: (p, cr[0], i, 0)),
                  pl.BlockSpec((1, tr, wd), lambda p, i, cr: (p, i, 0))],
        out_specs=pl.BlockSpec((1, tr, wd), lambda p, i, cr: (p, i, 0)))
    return pl.pallas_call(
        body, name="add_halves", grid_spec=grid_spec, out_shape=jax.ShapeDtypeStruct((n, rh, wd), BF16),
        compiler_params=_cparams("parallel", "parallel"))(jnp.reshape(c, (1,)).astype(jnp.int32), gpack, other)


PACK_W = 1024
ROWS_W_IN = D_MODEL * SHARD_W // PACK_W
ROWS_O_DN = DN_W // N_CHIPS
ROWS_O_DIL = DIL_W * (D_MODEL // N_CHIPS) // PACK_W
ROWS_OUT = D_MODEL // N_CHIPS
ROWS_CONV = 4 * (3 * DN_W // N_CHIPS) // PACK_W
R0 = ROWS_W_IN
R1 = R0 + ROWS_O_DN
R2 = R1 + ROWS_O_DIL
R3 = R2 + ROWS_OUT
R4 = R3 + ROWS_CONV
R5 = R4 + ROWS_CONV
PACK_ROWS = 3584
HALF_ROWS = PACK_ROWS // 2


def _to_ref_layout(wp):
    return jnp.concatenate([wp[:, :REF_OFF_BA], wp[:, OFF_BA:OFF_BA + 2 * DN_HEADS], wp[:, REF_OFF_BA:OFF_BA]], axis=1)


def _from_ref_layout(w):
    pad = jnp.zeros((w.shape[0], PW - PROJ_W), w.dtype)
    return jnp.concatenate([w[:, :REF_OFF_BA], w[:, REF_OFF_BA + 2 * DN_HEADS:], w[:, REF_OFF_BA:REF_OFF_BA + 2 * DN_HEADS], pad],
                           axis=1)


def _local_step(x, tgt, norm_w, wp, conv_full, a_log, dt_bias, dn_norm_w, w_o_dn, w_o_dil, w_out, final_norm_w):
    s = x.shape[0]
    n = s // DN_CHUNK
    h = _rms_in(x, norm_w)
    proj = _matmul(h, wp, F32, 512, 1280, 1024, "proj")
    c_pre, qkv = _conv_fwd(proj, conv_full)
    gate_par = jnp.zeros((8, 128), F32).at[0, 8:16].set(a_log[0]).at[1, 8:16].set(dt_bias[0])
    bg = _gates_fwd(proj, gate_par)
    o_a, u, w, vn, tmat, states = _gdr_fwd(qkv, bg)
    oa2 = _gdr_out(o_a, proj, dn_norm_w)
    ya = _matmul(oa2, w_o_dn, F32, 512, 1024, 1024, "ya")
    parts = [_att_fwd(proj, g) for g in range(N_DIL)]
    ob, o_att, lse = _att_merge(parts, proj)
    yb = _matmul(ob, w_o_dil, F32, 512, 1024, 512, "yb")
    mg = _merge(proj, ya, yb)
    t = _matmul(mg, w_out, F32, 512, 1024, 1024, "t_out")
    dx2, dfw, lpart = _final(x, t, final_norm_w, tgt)

    dmg = _matmul(dx2, w_out, F32, 512, 1024, 1024, "d_merged", nt=True)
    dw_out = _matmul(mg.T, dx2, F32, 1024, 1024, 1024, "dw_out")
    dya, dyb, dga, dgb = _merge_bwd(proj, ya, yb, dmg)
    doa2 = _matmul(dya, w_o_dn, F32, 512, 1024, 1024, "d_oa2", nt=True)
    dw_o_dn = _matmul(oa2.T, dya, F32, 1024, 1024, 1024, "dw_o_dn")
    dob = _matmul(dyb, w_o_dil, F32, 512, 512, 1024, "d_ob", nt=True)
    dw_o_dil = _matmul(ob.T, dyb, F32, 512, 1024, 1024, "dw_o_dil")
    do_a, dz_a, ddnw = _gdr_out_bwd(o_a, proj, dn_norm_w, doa2)
    dq_a, dk_a, dv_a, dbg = _gdr_bwd(qkv, bg, u, w, vn, tmat, states, do_a)
    dba, dpar = _gates_bwd(proj, gate_par, dbg)
    dc = _conv_bwd_act(c_pre, dq_a, dk_a, dv_a)
    du_a, dconv = _conv_bwd(proj, dc, conv_full)
    do_att, delta, dz_b = _att_merge_bwd(o_att, proj, dob)
    dqkv_b = [_att_bwd(proj, g, do_att, lse, delta) for g in range(N_DIL)]
    dproj = jnp.concatenate(
        [du_a, dz_a] + [dqkv_b[g][i] for i in range(3) for g in range(N_DIL)]
        + [dz_b, dga, dgb, dba, jnp.zeros((s, PW - OFF_BA - 128), BF16)], axis=1)
    dh = _matmul(dproj, wp, F32, 512, 1024, 2304, "d_h", nt=True)
    dwp = _matmul(h.T, dproj, F32, 1024, 1280, 1024, "dw_in")
    grad_x, dnw = _rms_in_bwd(x, norm_w, dh, dx2)
    small = jnp.zeros((8, PACK_W), F32)
    small = small.at[0].set(dnw[0]).at[1].set(dfw[0]).at[2, :DN_D].set(ddnw[0])
    small = small.at[3, :DN_HEADS].set(dpar[0, 8:16]).at[3, DN_HEADS:2 * DN_HEADS].set(dpar[1, 8:16])
    small = small.at[4, 0].set(lpart[0, 0])
    return grad_x, _to_ref_layout(dwp), dconv, dw_o_dn, dw_o_dil, dw_out, small


def kernel(x, norm_w, w_in, conv_w, a_log, dt_bias, dn_norm_w, w_o_dn, w_o_dil, w_out, final_norm_w, loss_target, m_norm_w, m_w_in, m_conv_w, m_a_log, m_dt_bias, m_dn_norm_w, m_w_o_dn, m_w_o_dil, m_w_out, m_final_norm_w, v_norm_w, v_w_in, v_conv_w, v_a_log, v_dt_bias, v_dn_norm_w, v_w_o_dn, v_w_o_dil, v_w_out, v_final_norm_w):
    c = lax.axis_index("c")
    qw = D_MODEL // N_CHIPS

    cw = conv_w[0].reshape(ROWS_CONV, PACK_W)
    cw_hi = cw.astype(BF16)
    cw_lo = (cw - cw_hi.astype(F32)).astype(BF16)
    pack = jnp.concatenate(
        [w_in[0].astype(BF16).reshape(ROWS_W_IN, PACK_W), w_o_dn[0].astype(BF16),
         w_o_dil[0].astype(BF16).reshape(ROWS_O_DIL, PACK_W), w_out[0].astype(BF16), cw_hi, cw_lo,
         jnp.zeros((PACK_ROWS - R5, PACK_W), BF16)], axis=0).reshape(2, HALF_ROWS, PACK_W)
    allw = _ag_weights(pack).reshape(N_CHIPS, PACK_ROWS, PACK_W)
    chips = range(N_CHIPS)
    w_in_full = jnp.concatenate([allw[k, :R0].reshape(D_MODEL, SHARD_W) for k in chips], axis=1)
    w_o_dn_full = jnp.concatenate([allw[k, R0:R1] for k in chips], axis=0)
    w_o_dil_full = jnp.concatenate([allw[k, R1:R2].reshape(DIL_W, qw) for k in chips], axis=1)
    w_out_full = jnp.concatenate([allw[k, R2:R3] for k in chips], axis=0)
    conv_full = jnp.concatenate(
        [(allw[k, R3:R4].astype(F32) + allw[k, R4:R5].astype(F32)).reshape(4, 3 * DN_W // N_CHIPS) for k in chips], axis=1)
    wp = _from_ref_layout(w_in_full)

    grad_x, dw_in, dconv, dw_o_dn, dw_o_dil, dw_out, small = _local_step(
        x[0], loss_target[0], norm_w, wp, conv_full, a_log, dt_bias, dn_norm_w, w_o_dn_full, w_o_dil_full, w_out_full,
        final_norm_w.reshape(1, D_MODEL))

    cq = 3 * DN_W // N_CHIPS
    gpack = jnp.stack([
        jnp.concatenate(
            [dw_in[:, k * SHARD_W:(k + 1) * SHARD_W].reshape(ROWS_W_IN, PACK_W), dw_o_dn[k * qw:(k + 1) * qw],
             dw_o_dil[:, k * qw:(k + 1) * qw].reshape(ROWS_O_DIL, PACK_W), dw_out[k * qw:(k + 1) * qw],
             dconv[:, k * cq:(k + 1) * cq].reshape(ROWS_CONV, PACK_W), jnp.zeros((PACK_ROWS - R4, PACK_W), F32)], axis=0)
        for k in chips]).reshape(N_CHIPS, 2, HALF_ROWS, PACK_W)
    from_sib = _rs_pair(gpack)
    csum = _add_halves(gpack, from_sib, c)
    by_src = _rs_chips(csum)
    half = _sum_blocks(by_src.reshape(N_CHIPS * HALF_ROWS, PACK_W), N_CHIPS, "sum_chips")
    g = _pair_swap(half).reshape(PACK_ROWS, PACK_W)
    g_w_in = g[:R0].reshape(D_MODEL, SHARD_W)
    g_w_o_dn = g[R0:R1]
    g_w_o_dil = g[R1:R2].reshape(DIL_W, qw)
    g_w_out = g[R2:R3]
    g_conv = g[R3:R4].reshape(4, cq)

    gs = _sum_blocks(_ag_small(small), 8, "sum_small")
    loss = gs[4, 0]
    w_small = jnp.zeros((8, PACK_W), F32)

    def pack_small(nw, fw, dnw_, al, db):
        t = w_small.at[0].set(nw[0]).at[1].set(fw).at[2, :DN_D].set(dnw_[0])
        return t.at[3, :DN_HEADS].set(al[0]).at[3, DN_HEADS:2 * DN_HEADS].set(db[0])

    sm = _adamw(pack_small(norm_w, final_norm_w, dn_norm_w, a_log, dt_bias), gs,
                pack_small(m_norm_w, m_final_norm_w, m_dn_norm_w, m_a_log, m_dt_bias),
                pack_small(v_norm_w, v_final_norm_w, v_dn_norm_w, v_a_log, v_dt_bias), "adamw_small")

    def unpack_small(t):
        return dict(norm_w=t[0:1], final_norm_w=t[1], dn_norm_w=t[2:3, :DN_D], a_log=t[3:4, :DN_HEADS],
                    dt_bias=t[3:4, DN_HEADS:2 * DN_HEADS])

    res = {"grad": unpack_small(gs)}
    for kind, arr in zip(("delta", "new_m", "new_v"), sm):
        res[kind] = unpack_small(arr)
    big = dict(w_in=(w_in, g_w_in, m_w_in, v_w_in), conv_w=(conv_w, g_conv, m_conv_w, v_conv_w),
               w_o_dn=(w_o_dn, g_w_o_dn, m_w_o_dn, v_w_o_dn), w_o_dil=(w_o_dil, g_w_o_dil, m_w_o_dil, v_w_o_dil),
               w_out=(w_out, g_w_out, m_w_out, v_w_out))
    for name, (wt, gt, mt, vt) in big.items():
        d, nm, nv = _adamw(wt[0], gt, mt[0], vt[0], "adamw_" + name)
        res["grad"][name] = gt[None]
        res["delta"][name], res["new_m"][name], res["new_v"][name] = d[None], nm[None], nv[None]
    order = ["norm_w", "w_in", "conv_w", "a_log", "dt_bias", "dn_norm_w", "w_o_dn", "w_o_dil", "w_out", "final_norm_w"]
    outs = [loss, grad_x[None]]
    for kind in ("grad", "delta", "new_m", "new_v"):
        outs += [res[kind][nm] for nm in order]
    return tuple(outs)
```

```python
import functools
import math

import jax
import jax.numpy as jnp
from jax import lax
from jax.experimental import pallas as pl
from jax.experimental.pallas import tpu as pltpu

F32 = jnp.float32
BF16 = jnp.bfloat16
MESH = pl.DeviceIdType.MESH

D_MODEL = 1024
DN_HEADS = 8
DN_D = 128
DN_CHUNK = 64
DN_W = DN_HEADS * DN_D
DIL_GROUPS = ((128, 1), (512, 4), (2048, 16))
N_DIL = len(DIL_GROUPS)
DIL_HEADS = 4
DIL_DH = 128
DIL_W = DIL_HEADS * DIL_DH
ATT_BLOCK = 128
NORM_EPS = 1e-6
PROJ_W = 11280
N_CHIPS = 4
SHARD_W = PROJ_W // N_CHIPS

OFF_QKV_A = 0
OFF_Z_A = 3072
OFF_Q_B = 4096
OFF_K_B = 5632
OFF_V_B = 7168
OFF_Z_B = 8704
OFF_G_A = 9216
OFF_G_B = 10240
OFF_BA = 11264
PW = 11520
REF_OFF_BA = 4096

ADAM_LR = 0.001
ADAM_B1 = 0.9
ADAM_B2 = 0.999
ADAM_EPS = 1e-08
ADAM_WD = 0.01
ADAM_STEP = 10

ROW_TILE = 256
NEG = -1e30


def _dot(a, b):
    return jnp.dot(a.astype(BF16), b.astype(BF16), preferred_element_type=F32)


def _dot_nt(a, b):
    return lax.dot_general(a.astype(BF16), b.astype(BF16), (((1,), (1,)), ((), ())), preferred_element_type=F32)


def _dot_tn(a, b):
    return lax.dot_general(a.astype(BF16), b.astype(BF16), (((0,), (0,)), ((), ())), preferred_element_type=F32)


def _split(a):
    hi = a.astype(BF16)
    lo = (a - hi.astype(F32)).astype(BF16)
    return hi, lo


def _dot_exact_lhs(c, a):
    hi, lo = _split(a)
    cb = c.astype(BF16)
    return jnp.dot(cb, hi, preferred_element_type=F32) + jnp.dot(cb, lo, preferred_element_type=F32)


def _dot_exact_rhs(a, c):
    hi, lo = _split(a)
    cb = c.astype(BF16)
    return jnp.dot(hi, cb, preferred_element_type=F32) + jnp.dot(lo, cb, preferred_element_type=F32)


def _dot_tn_exact_rhs(a, c):
    hi, lo = _split(a)
    cb = c.astype(BF16)
    dn = (((0,), (0,)), ((), ()))
    return (lax.dot_general(hi, cb, dn, preferred_element_type=F32)
            + lax.dot_general(lo, cb, dn, preferred_element_type=F32))


def _sigmoid(x):
    return 1.0 / (1.0 + jnp.exp(-x))


def _silu(x):
    return x * _sigmoid(x)


def _silu_grad(x):
    s = _sigmoid(x)
    return s * (1.0 + x * (1.0 - s))


def _softplus(x):
    return jnp.maximum(x, 0.0) + jnp.log(1.0 + jnp.exp(-jnp.abs(x)))


def _cparams(*sem):
    return pltpu.CompilerParams(dimension_semantics=sem)


def _matmul(a, b, out_dtype, tm, tn, tk, name, nt=False):
    m, kdim = a.shape
    n = b.shape[0] if nt else b.shape[1]
    tm, tn, tk = min(tm, m), min(tn, n), min(tk, kdim)
    assert m % tm == 0 and n % tn == 0 and kdim % tk == 0, (name, a.shape, b.shape, tm, tn, tk)
    nk = kdim // tk
    dot = _dot_nt if nt else _dot
    b_spec = (pl.BlockSpec((tn, tk), lambda i, j, k: (j, k)) if nt else pl.BlockSpec((tk, tn), lambda i, j, k: (k, j)))

    if nk == 1:
        def body(a_ref, b_ref, o_ref):
            o_ref[...] = dot(a_ref[...], b_ref[...]).astype(o_ref.dtype)
        scratch = []
    else:
        def body(a_ref, b_ref, o_ref, acc_ref):
            k = pl.program_id(2)
            p = dot(a_ref[...], b_ref[...])

            @pl.when(k == 0)
            def _():
                acc_ref[...] = p

            @pl.when(k > 0)
            def _():
                acc_ref[...] += p

            @pl.when(k == nk - 1)
            def _():
                o_ref[...] = acc_ref[...].astype(o_ref.dtype)
        scratch = [pltpu.VMEM((tm, tn), F32)]

    return pl.pallas_call(
        body, name=name, grid=(m // tm, n // tn, nk),
        in_specs=[pl.BlockSpec((tm, tk), lambda i, j, k: (i, k)), b_spec],
        out_specs=pl.BlockSpec((tm, tn), lambda i, j, k: (i, j)),
        out_shape=jax.ShapeDtypeStruct((m, n), out_dtype), scratch_shapes=scratch,
        compiler_params=_cparams("parallel", "parallel", "arbitrary"))(a, b)


def _rms_in(x, nw):
    s, d = x.shape

    def body(x_ref, w_ref, h_ref, ht_ref):
        xv = x_ref[...]
        r = lax.rsqrt(jnp.mean(xv * xv, axis=-1, keepdims=True) + NORM_EPS)
        h = xv * r * w_ref[...]
        h_ref[...] = h.astype(BF16)
        ht_ref[...] = h.T.astype(BF16)

    return pl.pallas_call(
        body, name="rms_in", grid=(s // ROW_TILE,),
        in_specs=[pl.BlockSpec((ROW_TILE, d), lambda i: (i, 0)), pl.BlockSpec((1, d), lambda i: (0, 0))],
        out_specs=[pl.BlockSpec((ROW_TILE, d), lambda i: (i, 0)), pl.BlockSpec((d, ROW_TILE), lambda i: (0, i))],
        out_shape=[jax.ShapeDtypeStruct((s, d), BF16), jax.ShapeDtypeStruct((d, s), BF16)],
        compiler_params=_cparams("parallel"))(x, nw)


def _rms_in_bwd(x, nw, dh, dx2):
    s, d = x.shape

    def body(x_ref, w_ref, dh_ref, dx2_ref, dx_ref, dw_ref):
        i = pl.program_id(0)
        xv = x_ref[...]
        r = lax.rsqrt(jnp.mean(xv * xv, axis=-1, keepdims=True) + NORM_EPS)
        dhv = dh_ref[...]
        dyw = dhv * w_ref[...]
        dx_ref[...] = dx2_ref[...] + r * dyw - xv * (r * r * r) * jnp.mean(dyw * xv, axis=-1, keepdims=True)
        part = jnp.sum(dhv * xv * r, axis=0, keepdims=True)

        @pl.when(i == 0)
        def _():
            dw_ref[...] = part

        @pl.when(i > 0)
        def _():
            dw_ref[...] += part

    row = pl.BlockSpec((ROW_TILE, d), lambda i: (i, 0))
    vec = pl.BlockSpec((1, d), lambda i: (0, 0))
    return pl.pallas_call(
        body, name="rms_in_bwd", grid=(s // ROW_TILE,), in_specs=[row, vec, row, row], out_specs=[row, vec],
        out_shape=[jax.ShapeDtypeStruct((s, d), F32), jax.ShapeDtypeStruct((1, d), F32)],
        compiler_params=_cparams("arbitrary"))(x, nw, dh, dx2)


def _shift_down(cur, prev8, k):
    rc = pltpu.roll(cur, k, 0)
    rp = pltpu.roll(prev8, k, 0)
    row = lax.broadcasted_iota(jnp.int32, prev8.shape, 0)
    top = jnp.where(row < k, rp, rc[:8])
    return jnp.concatenate([top, rc[8:]], axis=0)


def _shift_up(cur, next8, k):
    t = cur.shape[0]
    rc = pltpu.roll(cur, t - k, 0)
    rn = pltpu.roll(next8, 8 - k, 0)
    row = lax.broadcasted_iota(jnp.int32, next8.shape, 0)
    bot = jnp.where(row >= 8 - k, rn, rc[t - 8:])
    return jnp.concatenate([rc[:t - 8], bot], axis=0)


def _conv_fwd(proj, conv_w):
    s = proj.shape[0]
    t8 = ROW_TILE // 8

    def body(u_ref, up_ref, w_ref, c_ref, y_ref):
        i = pl.program_id(0)
        part = pl.program_id(1)
        cur = u_ref[...]
        prev8 = jnp.where(i > 0, up_ref[...], 0.0)
        w = w_ref[...]
        c = cur * w[3:4, :]
        for k in (1, 2, 3):
            c = c + _shift_down(cur, prev8, k) * w[3 - k:4 - k, :]
        c_ref[...] = c
        a = _silu(c)
        for h in range(DN_HEADS):
            ah = a[:, h * DN_D:(h + 1) * DN_D]
            r = lax.rsqrt(jnp.sum(ah * ah, axis=-1, keepdims=True) + NORM_EPS)
            y_ref[:, h * DN_D:(h + 1) * DN_D] = jnp.where(part < 2, ah * r, ah)

    return pl.pallas_call(
        body, name="conv_fwd", grid=(s // ROW_TILE, 3),
        in_specs=[pl.BlockSpec((ROW_TILE, DN_W), lambda i, p: (i, p)),
                  pl.BlockSpec((8, DN_W), lambda i, p: (jnp.maximum(i * t8 - 1, 0), p)),
                  pl.BlockSpec((4, DN_W), lambda i, p: (0, p))],
        out_specs=[pl.BlockSpec((ROW_TILE, DN_W), lambda i, p: (i, p))] * 2,
        out_shape=[jax.ShapeDtypeStruct((s, 3 * DN_W), F32)] * 2,
        compiler_params=_cparams("parallel", "parallel"))(proj, proj, conv_w)


def _conv_bwd_act(c, dq, dk, dv):
    s = c.shape[0]

    def body(c_ref, dq_ref, dk_ref, dv_ref, dc_ref):
        for part, d_ref in enumerate((dq_ref, dk_ref, dv_ref)):
            for h in range(DN_HEADS):
                sl = slice(part * DN_W + h * DN_D, part * DN_W + (h + 1) * DN_D)
                ch = c_ref[:, sl]
                dyh = d_ref[:, h * DN_D:(h + 1) * DN_D]
                if part < 2:
                    ah = _silu(ch)
                    r = lax.rsqrt(jnp.sum(ah * ah, axis=-1, keepdims=True) + NORM_EPS)
                    dyh = r * dyh - ah * (r * r * r) * jnp.sum(dyh * ah, axis=-1, keepdims=True)
                dc_ref[:, sl] = dyh * _silu_grad(ch)

    wide = pl.BlockSpec((ROW_TILE, 3 * DN_W), lambda i: (i, 0))
    row = pl.BlockSpec((ROW_TILE, DN_W), lambda i: (i, 0))
    return pl.pallas_call(
        body, name="conv_bwd_act", grid=(s // ROW_TILE,), in_specs=[wide, row, row, row], out_specs=wide,
        out_shape=jax.ShapeDtypeStruct((s, 3 * DN_W), F32), compiler_params=_cparams("parallel"))(c, dq, dk, dv)


def _conv_bwd(proj, dc, conv_w):
    s = proj.shape[0]
    t8 = ROW_TILE // 8
    nrow = s // ROW_TILE
    last8 = s // 8 - 1

    def body(u_ref, up_ref, dc_ref, dcn_ref, w_ref, du_ref, dw_ref):
        i = pl.program_id(1)
        cur = u_ref[...]
        prev8 = jnp.where(i > 0, up_ref[...], 0.0)
        dcv = dc_ref[...]
        next8 = jnp.where(i < nrow - 1, dcn_ref[...], 0.0)
        w = w_ref[...]
        du = dcv * w[3:4, :]
        for k in (1, 2, 3):
            du = du + _shift_up(dcv, next8, k) * w[3 - k:4 - k, :]
        du_ref[...] = du.astype(BF16)

        @pl.when(i == 0)
        def _():
            dw_ref[...] = jnp.zeros_like(dw_ref)

        dw_ref[3:4, :] += jnp.sum(cur * dcv, axis=0, keepdims=True)
        for k in (1, 2, 3):
            dw_ref[3 - k:4 - k, :] += jnp.sum(_shift_down(cur, prev8, k) * dcv, axis=0, keepdims=True)

    blk = pl.BlockSpec((ROW_TILE, DN_W), lambda p, i: (i, p))
    return pl.pallas_call(
        body, name="conv_bwd", grid=(3, nrow),
        in_specs=[blk, pl.BlockSpec((8, DN_W), lambda p, i: (jnp.maximum(i * t8 - 1, 0), p)),
                  blk, pl.BlockSpec((8, DN_W), lambda p, i: (jnp.minimum((i + 1) * t8, last8), p)),
                  pl.BlockSpec((4, DN_W), lambda p, i: (0, p))],
        out_specs=[blk, pl.BlockSpec((4, DN_W), lambda p, i: (0, p))],
        out_shape=[jax.ShapeDtypeStruct((s, 3 * DN_W), BF16), jax.ShapeDtypeStruct((4, 3 * DN_W), F32)],
        compiler_params=_cparams("parallel", "arbitrary"))(proj, proj, dc, dc, conv_w)


def _gates_fwd(proj, gate_par):
    s = proj.shape[0]

    def body(ba_ref, par_ref, o_ref):
        v = ba_ref[...]
        lane = lax.broadcasted_iota(jnp.int32, v.shape, 1)
        beta = _sigmoid(v)
        g = -jnp.exp(par_ref[0:1, :]) * _softplus(v + par_ref[1:2, :])
        o_ref[...] = jnp.where(lane < DN_HEADS, beta, jnp.where(lane < 2 * DN_HEADS, g, 0.0))

    return pl.pallas_call(
        body, name="gates_fwd", grid=(s // ROW_TILE,),
        in_specs=[pl.BlockSpec((ROW_TILE, 128), lambda i: (i, OFF_BA // 128)), pl.BlockSpec((8, 128), lambda i: (0, 0))],
        out_specs=pl.BlockSpec((ROW_TILE, 128), lambda i: (i, 0)),
        out_shape=jax.ShapeDtypeStruct((s, 128), F32), compiler_params=_cparams("parallel"))(proj, gate_par)


def _gates_bwd(proj, gate_par, dbg):
    s = proj.shape[0]

    def body(ba_ref, par_ref, d_ref, o_ref, dpar_ref):
        i = pl.program_id(0)
        v = ba_ref[...]
        dv = d_ref[...]
        lane = lax.broadcasted_iota(jnp.int32, v.shape, 1)
        beta = _sigmoid(v)
        nega = -jnp.exp(par_ref[0:1, :])
        xs = v + par_ref[1:2, :]
        dsp = dv * nega * _sigmoid(xs)
        dal = dv * nega * _softplus(xs)
        is_b = lane < DN_HEADS
        is_g = jnp.logical_and(lane >= DN_HEADS, lane < 2 * DN_HEADS)
        o_ref[...] = jnp.where(is_b, dv * beta * (1.0 - beta), jnp.where(is_g, dsp, 0.0)).astype(BF16)
        r0 = jnp.sum(jnp.where(is_g, dal, 0.0), axis=0, keepdims=True)
        r1 = jnp.sum(jnp.where(is_g, dsp, 0.0), axis=0, keepdims=True)

        @pl.when(i == 0)
        def _():
            dpar_ref[...] = jnp.zeros_like(dpar_ref)

        dpar_ref[0:1, :] += r0
        dpar_ref[1:2, :] += r1

    return pl.pallas_call(
        body, name="gates_bwd", grid=(s // ROW_TILE,),
        in_specs=[pl.BlockSpec((ROW_TILE, 128), lambda i: (i, OFF_BA // 128)), pl.BlockSpec((8, 128), lambda i: (0, 0)),
                  pl.BlockSpec((ROW_TILE, 128), lambda i: (i, 0))],
        out_specs=[pl.BlockSpec((ROW_TILE, 128), lambda i: (i, 0)), pl.BlockSpec((8, 128), lambda i: (0, 0))],
        out_shape=[jax.ShapeDtypeStruct((s, 128), BF16), jax.ShapeDtypeStruct((8, 128), F32)],
        compiler_params=_cparams("arbitrary"))(proj, gate_par, dbg)


def _chunk_masks():
    c = DN_CHUNK
    ii = lax.broadcasted_iota(jnp.int32, (c, c), 0)
    jj = lax.broadcasted_iota(jnp.int32, (c, c), 1)
    return dict(ii=ii, jj=jj, lower=(ii >= jj), strict=(ii > jj), eye=(ii == jj),
                lower_f=(ii >= jj).astype(BF16), upper_f=(ii <= jj).astype(BF16), ones8=jnp.ones((8, c), BF16))


class _Heads:
    def __init__(self, xs):
        self.xs = list(xs)

    def _bin(self, o, f):
        if isinstance(o, _Heads):
            return _Heads([f(a, b) for a, b in zip(self.xs, o.xs)])
        return _Heads([f(a, o) for a in self.xs])

    def __add__(self, o):
        return self._bin(o, lambda a, b: a + b)

    def __sub__(self, o):
        return self._bin(o, lambda a, b: a - b)

    def __mul__(self, o):
        return self._bin(o, lambda a, b: a * b)

    __radd__ = __add__
    __rmul__ = __mul__

    def __neg__(self):
        return _Heads([-a for a in self.xs])

    def __getitem__(self, i):
        return _Heads([a[i] for a in self.xs])


def _hmap(f, *args):
    n = next(len(a.xs) for a in args if isinstance(a, _Heads))
    return _Heads([f(*[(a.xs[h] if isinstance(a, _Heads) else a) for a in args]) for h in range(n)])


def _hdot(a, b):
    return _hmap(_dot, a, b)


def _hdot_nt(a, b):
    return _hmap(_dot_nt, a, b)


def _hdot_tn(a, b):
    return _hmap(_dot_tn, a, b)


def _hsum(a, axis):
    return _hmap(lambda t: jnp.sum(t, axis=axis, keepdims=True), a)


def _hwhere(c, a, b):
    return _hmap(jnp.where, c, a, b)


def _chunk_common(mk, q, k, beta_col, g_col):
    c = DN_CHUNK
    lower, strict = mk["lower"], mk["strict"]
    qs = q * (DN_D ** -0.5)
    beta_b = _hmap(lambda t: jnp.broadcast_to(t, (c, DN_D)), beta_col)
    g_b = _hmap(lambda t: jnp.broadcast_to(t, (c, DN_D)), g_col)
    gc_b = _hmap(_dot_exact_lhs, mk["lower_f"], g_b)
    gc_sq = gc_b[:, :c]
    gc_r = _hmap(_dot_exact_lhs, mk["ones8"], _hwhere(mk["eye"], gc_sq, 0.0))[0:1, :]
    gam = _hwhere(lower, _hmap(lambda t: jnp.exp(jnp.minimum(t, 0.0)), gc_sq - gc_r), 0.0)
    egc = _hmap(jnp.exp, gc_b)
    gl = gc_b[c - 1:c, :]
    ekd = _hmap(jnp.exp, gl - gc_b)
    dl = _hmap(jnp.exp, gl)
    kb = k * beta_b
    a_strict = _hwhere(strict, _hdot_nt(kb, k) * gam, 0.0)
    aqk = _hwhere(lower, _hdot_nt(qs, k) * gam, 0.0)
    return dict(k=k, qs=qs, beta_b=beta_b, gc_b=gc_b, gam=gam, egc=egc, ekd=ekd, dl=dl, kb=kb, a_strict=a_strict, aqk=aqk)


def _unit_lower_inverse_minus_eye(n_strict, ii, jj):
    same = lax.shift_right_logical(ii, 4) == lax.shift_right_logical(jj, 4)
    dmat = _hwhere(same, n_strict, 0.0)
    omat = n_strict - dmat
    d2 = _hdot(dmat, dmat)
    d4 = _hdot(d2, d2)
    d8 = _hdot(d4, d4)
    x1 = d2 - dmat - _hdot(dmat, d2)
    x2 = x1 + d4 + _hdot(x1, d4)
    x3 = x2 + d8 + _hdot(x2, d8)
    n1 = omat + _hdot(x3, omat)
    n2 = _hdot(n1, n1)
    y = n2 - n1 - _hdot(n1, n2)
    return y + x3 + _hdot(y, x3)


def _gdr_fwd(qkv, bg):
    s = qkv.shape[0]
    c = DN_CHUNK
    n = s // c

    def body(q_ref, k_ref, v_ref, bg_ref, o_ref, u_ref, w_ref, vn_ref, tm_ref, st_ref, state):
        @pl.when(pl.program_id(0) == 0)
        def _():
            state[...] = jnp.zeros_like(state)

        mk = _chunk_masks()
        bg = bg_ref[...]
        hs = range(DN_HEADS)
        sls = [slice(h * DN_D, (h + 1) * DN_D) for h in hs]
        cm = _chunk_common(mk, _Heads(q_ref[:, sl] for sl in sls), _Heads(k_ref[:, sl] for sl in sls),
                           _Heads(bg[:, h:h + 1] for h in hs), _Heads(bg[:, DN_HEADS + h:DN_HEADS + h + 1] for h in hs))
        tm = _unit_lower_inverse_minus_eye(cm["a_strict"], mk["ii"], mk["jj"])
        rhs_u = _Heads(v_ref[:, sl] for sl in sls) * cm["beta_b"]
        rhs_w = cm["kb"] * cm["egc"]
        u = rhs_u + _hdot(tm, rhs_u)
        w = rhs_w + _hdot(tm, rhs_w)
        st = _Heads(state[h] for h in hs)
        v_new = u - _hdot(w, st)
        o = _hdot(cm["qs"] * cm["egc"], st) + _hdot(cm["aqk"], v_new)
        st_new = st * cm["dl"] + _hdot_tn(cm["k"] * cm["ekd"], v_new)
        for h, sl in zip(hs, sls):
            o_ref[:, sl] = o.xs[h]
            u_ref[:, sl] = u.xs[h]
            w_ref[:, sl] = w.xs[h]
            vn_ref[:, sl] = v_new.xs[h]
            tm_ref[h, 0] = tm.xs[h]
            st_ref[h, 0] = st.xs[h]
            state[h] = st_new.xs[h]

    def part(p):
        return pl.BlockSpec((c, DN_W), lambda j: (j, p))

    return pl.pallas_call(
        body, name="gdr_fwd", grid=(n,),
        in_specs=[part(0), part(1), part(2), pl.BlockSpec((c, 128), lambda j: (j, 0))],
        out_specs=[part(0)] * 4 + [pl.BlockSpec((DN_HEADS, 1, c, c), lambda j: (0, j, 0, 0)),
                                   pl.BlockSpec((DN_HEADS, 1, DN_D, DN_D), lambda j: (0, j, 0, 0))],
        out_shape=[jax.ShapeDtypeStruct((s, DN_W), F32)] * 4
        + [jax.ShapeDtypeStruct((DN_HEADS, n, c, c), F32), jax.ShapeDtypeStruct((DN_HEADS, n, DN_D, DN_D), F32)],
        scratch_shapes=[pltpu.VMEM((DN_HEADS, DN_D, DN_D), F32)],
        compiler_params=_cparams("arbitrary"))(qkv, qkv, qkv, bg)


def _gdr_bwd(qkv, bg, u, w, vn, tmat, states, do):
    s = qkv.shape[0]
    c = DN_CHUNK
    n = s // c

    def body(q_ref, k_ref, v_ref, bg_ref, u_ref, w_ref, vn_ref, tm_ref, st_ref, do_ref,
             dq_ref, dk_ref, dv_ref, dbg_ref, dstate):
        @pl.when(pl.program_id(0) == 0)
        def _():
            dstate[...] = jnp.zeros_like(dstate)

        mk = _chunk_masks()
        lower, strict = mk["lower"], mk["strict"]
        bg = bg_ref[...]
        ones = jnp.ones((c, DN_D), BF16)
        rowi = lax.broadcasted_iota(jnp.int32, (c, DN_D), 0)
        lane = lax.broadcasted_iota(jnp.int32, (c, 128), 1)
        hs = range(DN_HEADS)
        sls = [slice(h * DN_D, (h + 1) * DN_D) for h in hs]

        def heads_of(ref):
            return _Heads(ref[:, sl] for sl in sls)

        cm = _chunk_common(mk, heads_of(q_ref), heads_of(k_ref),
                           _Heads(bg[:, h:h + 1] for h in hs), _Heads(bg[:, DN_HEADS + h:DN_HEADS + h + 1] for h in hs))
        k, qs, beta_b = cm["k"], cm["qs"], cm["beta_b"]
        gam, egc, ekd, dl, kb = cm["gam"], cm["egc"], cm["ekd"], cm["dl"], cm["kb"]
        aqk, a_strict = cm["aqk"], cm["a_strict"]
        v, uu, ww, v_new, dov = heads_of(v_ref), heads_of(u_ref), heads_of(w_ref), heads_of(vn_ref), heads_of(do_ref)
        tm = _Heads(tm_ref[h, 0] for h in hs)
        st = _Heads(st_ref[h, 0] for h in hs)
        dsn = _Heads(dstate[h] for h in hs)
        qd = qs * egc
        kd = k * ekd

        dv_new = _hdot_tn(aqk, dov) + _hdot(kd, dsn)
        daqk = _hwhere(lower, _hdot_nt(dov, v_new), 0.0)
        dqd = _hdot_nt(dov, st)
        dkd = _hdot_nt(v_new, dsn)
        ddl = _hsum(_hsum(dsn * st, 1), 0)
        dw = -_hdot_nt(dv_new, st)
        ds_new = dsn * dl + _hdot_tn(qd, dov) - _hdot_tn(ww, dv_new)

        dru = dv_new + _hdot_tn(tm, dv_new)
        drw = dw + _hdot_tn(tm, dw)
        dn = _hwhere(strict, -(_hdot_nt(dru, uu) + _hdot_nt(drw, ww)), 0.0)
        dag = dn * gam
        dkb = _hdot(dag, k) + drw * egc
        dk = _hdot_tn(dag, kb)
        dqg = daqk * gam
        dqs = _hdot(dqg, k) + dqd * egc
        dk = dk + _hdot_tn(dqg, qs) + dkb * beta_b + dkd * ekd
        pmat = dn * a_strict + daqk * aqk
        tkd = _hsum(dkd * kd, -1)
        dgc = (_hsum(pmat, -1) - _hmap(_dot_tn_exact_rhs, pmat, ones) + _hsum(drw * (kb * egc), -1)
               + _hsum(dqd * qd, -1) - tkd)
        last = _hsum(tkd, 0) + ddl * dl
        dgc = dgc + _hwhere(rowi == c - 1, last, 0.0)
        dg = _hmap(_dot_exact_lhs, mk["upper_f"], dgc)
        dbeta = _hsum(dru * v, -1) + _hsum(dkb * k, -1)
        dq = dqs * (DN_D ** -0.5)
        dv = dru * beta_b

        dbg = jnp.zeros((c, 128), F32)
        for h, sl in zip(hs, sls):
            dq_ref[:, sl] = dq.xs[h]
            dk_ref[:, sl] = dk.xs[h]
            dv_ref[:, sl] = dv.xs[h]
            dstate[h] = ds_new.xs[h]
            dbg = dbg + jnp.where(lane == h, dbeta.xs[h], 0.0) + jnp.where(lane == DN_HEADS + h, dg.xs[h], 0.0)
        dbg_ref[...] = dbg

    def part(p):
        return pl.BlockSpec((c, DN_W), lambda j: (n - 1 - j, p))

    vec = pl.BlockSpec((c, 128), lambda j: (n - 1 - j, 0))
    return pl.pallas_call(
        body, name="gdr_bwd", grid=(n,),
        in_specs=[part(0), part(1), part(2), vec, part(0), part(0), part(0),
                  pl.BlockSpec((DN_HEADS, 1, c, c), lambda j: (0, n - 1 - j, 0, 0)),
                  pl.BlockSpec((DN_HEADS, 1, DN_D, DN_D), lambda j: (0, n - 1 - j, 0, 0)), part(0)],
        out_specs=[part(0), part(0), part(0), vec],
        out_shape=[jax.ShapeDtypeStruct((s, DN_W), F32)] * 3 + [jax.ShapeDtypeStruct((s, 128), F32)],
        scratch_shapes=[pltpu.VMEM((DN_HEADS, DN_D, DN_D), F32)],
        compiler_params=_cparams("arbitrary"))(qkv, qkv, qkv, bg, u, w, vn, tmat, states, do)


def _gdr_out(o, proj, dnw):
    s = o.shape[0]

    def body(o_ref, z_ref, w_ref, y_ref, yt_ref):
        ov, zv, wv = o_ref[...], z_ref[...], w_ref[...]
        for h in range(DN_HEADS):
            sl = slice(h * DN_D, (h + 1) * DN_D)
            oh = ov[:, sl]
            r = lax.rsqrt(jnp.mean(oh * oh, axis=-1, keepdims=True) + NORM_EPS)
            y = (oh * r * wv) * _silu(zv[:, sl])
            y_ref[:, sl] = y.astype(BF16)
            yt_ref[sl, :] = y.T.astype(BF16)

    row = pl.BlockSpec((ROW_TILE, DN_W), lambda i: (i, 0))
    return pl.pallas_call(
        body, name="gdr_out", grid=(s // ROW_TILE,),
        in_specs=[row, pl.BlockSpec((ROW_TILE, DN_W), lambda i: (i, OFF_Z_A // DN_W)), pl.BlockSpec((1, DN_D), lambda i: (0, 0))],
        out_specs=[row, pl.BlockSpec((DN_W, ROW_TILE), lambda i: (0, i))],
        out_shape=[jax.ShapeDtypeStruct((s, DN_W), BF16), jax.ShapeDtypeStruct((DN_W, s), BF16)],
        compiler_params=_cparams("parallel"))(o, proj, dnw)


def _gdr_out_bwd(o, proj, dnw, dy):
    s = o.shape[0]

    def body(o_ref, z_ref, w_ref, dy_ref, do_ref, dz_ref, dw_ref):
        i = pl.program_id(0)
        ov, zv, wv, dyv = o_ref[...], z_ref[...], w_ref[...], dy_ref[...]
        acc = jnp.zeros((1, DN_D), F32)
        for h in range(DN_HEADS):
            sl = slice(h * DN_D, (h + 1) * DN_D)
            oh, zh, dh = ov[:, sl], zv[:, sl], dyv[:, sl]
            r = lax.rsqrt(jnp.mean(oh * oh, axis=-1, keepdims=True) + NORM_EPS)
            dn = dh * _silu(zh)
            dz_ref[:, sl] = (dh * (oh * r * wv) * _silu_grad(zh)).astype(BF16)
            acc = acc + jnp.sum(dn * oh * r, axis=0, keepdims=True)
            dnw_ = dn * wv
            do_ref[:, sl] = r * dnw_ - oh * (r * r * r) * jnp.mean(dnw_ * oh, axis=-1, keepdims=True)

        @pl.when(i == 0)
        def _():
            dw_ref[...] = acc

        @pl.when(i > 0)
        def _():
            dw_ref[...] += acc

    row = pl.BlockSpec((ROW_TILE, DN_W), lambda i: (i, 0))
    vec = pl.BlockSpec((1, DN_D), lambda i: (0, 0))
    return pl.pallas_call(
        body, name="gdr_out_bwd", grid=(s // ROW_TILE,),
        in_specs=[row, pl.BlockSpec((ROW_TILE, DN_W), lambda i: (i, OFF_Z_A // DN_W)), vec, row],
        out_specs=[row, row, vec],
        out_shape=[jax.ShapeDtypeStruct((s, DN_W), F32), jax.ShapeDtypeStruct((s, DN_W), BF16),
                   jax.ShapeDtypeStruct((1, DN_D), F32)],
        compiler_params=_cparams("arbitrary"))(o, proj, dnw, dy)


def _slope(group, head):
    idx = (group * DIL_HEADS + head + 1).astype(F32)
    return jnp.exp(jnp.full((1, 128), -8.0 * math.log(2.0) / (N_DIL * DIL_HEADS), F32) * idx)


def _att_scores(qb, k_cur, k_prev, slope_d, has_prev):
    iq = lax.broadcasted_iota(jnp.int32, (ATT_BLOCK, ATT_BLOCK), 0)
    jk = lax.broadcasted_iota(jnp.int32, (ATT_BLOCK, ATT_BLOCK), 1)
    dist_c = (iq - jk).astype(F32)
    s_cur = jnp.where(iq >= jk, _dot_nt(qb, k_cur) - slope_d * dist_c, NEG)
    s_prev = jnp.where(jnp.logical_and(jk >= iq, has_prev),
                       _dot_nt(qb, k_prev) - slope_d * (dist_c + float(ATT_BLOCK)), NEG)
    return s_cur, s_prev


def _att_fwd(proj, group):
    s = proj.shape[0]
    dil = DIL_GROUPS[group][1]
    assert DIL_GROUPS[group][0] // dil == ATT_BLOCK
    nb = s // dil // ATT_BLOCK
    assert nb * dil * ATT_BLOCK == s

    def body(q_ref, k_ref, v_ref, num_ref, den_ref, mx_ref):
        slope_d = _slope(group, pl.program_id(0)) * float(dil)

        def step(t, carry):
            r = lax.div(t, nb)
            j = lax.rem(t, nb)
            base = r + dil * ATT_BLOCK * j
            pbase = base - dil * ATT_BLOCK * jnp.minimum(j, 1)
            if dil == 1:
                base, pbase = pl.multiple_of(base, ATT_BLOCK), pl.multiple_of(pbase, ATT_BLOCK)
            cur = pl.ds(base, ATT_BLOCK, stride=dil)
            prv = pl.ds(pbase, ATT_BLOCK, stride=dil)
            qb = q_ref[cur, :] * (DIL_DH ** -0.5)
            s_cur, s_prev = _att_scores(qb, k_ref[cur, :], k_ref[prv, :], slope_d, j > 0)
            mx = jnp.maximum(jnp.max(s_cur, axis=-1, keepdims=True), jnp.max(s_prev, axis=-1, keepdims=True))
            p_cur = jnp.exp(s_cur - mx)
            p_prev = jnp.exp(s_prev - mx)
            den = jnp.sum(p_cur, axis=-1, keepdims=True) + jnp.sum(p_prev, axis=-1, keepdims=True)
            num_ref[cur, :] = _dot(p_cur, v_ref[cur, :]) + _dot(p_prev, v_ref[prv, :])
            den_ref[cur, :] = jnp.broadcast_to(den, (ATT_BLOCK, DIL_DH))
            mx_ref[cur, :] = jnp.broadcast_to(mx, (ATT_BLOCK, DIL_DH))
            return carry

        lax.fori_loop(0, dil * nb, step, 0)

    def col(off):
        return pl.BlockSpec((s, DIL_DH), lambda h: (0, off // DIL_DH + group * DIL_HEADS + h))

    out = pl.BlockSpec((s, DIL_DH), lambda h: (0, h))
    return pl.pallas_call(
        body, name=f"att_fwd{group}", grid=(DIL_HEADS,), in_specs=[col(OFF_Q_B), col(OFF_K_B), col(OFF_V_B)],
        out_specs=[out, out, out], out_shape=[jax.ShapeDtypeStruct((s, DIL_W), F32)] * 3,
        compiler_params=_cparams("parallel"))(proj, proj, proj)


def _att_bwd(proj, group, do, lse, delta):
    s = proj.shape[0]
    dil = DIL_GROUPS[group][1]
    nb = s // dil // ATT_BLOCK

    def body(q_ref, k_ref, v_ref, do_ref, lse_ref, dl_ref, dq_ref, dk_ref, dv_ref, dq_acc, dk_acc, dv_acc):
        slope_d = _slope(group, pl.program_id(0)) * float(dil)
        dk_acc[...] = jnp.zeros_like(dk_acc)
        dv_acc[...] = jnp.zeros_like(dv_acc)

        def step(t, carry):
            r = lax.div(t, nb)
            j = lax.rem(t, nb)
            base = r + dil * ATT_BLOCK * j
            pbase = base - dil * ATT_BLOCK * jnp.minimum(j, 1)
            if dil == 1:
                base, pbase = pl.multiple_of(base, ATT_BLOCK), pl.multiple_of(pbase, ATT_BLOCK)
            cur = pl.ds(base, ATT_BLOCK, stride=dil)
            prv = pl.ds(pbase, ATT_BLOCK, stride=dil)
            qb = q_ref[cur, :] * (DIL_DH ** -0.5)
            k_cur, k_prev, v_cur, v_prev = k_ref[cur, :], k_ref[prv, :], v_ref[cur, :], v_ref[prv, :]
            s_cur, s_prev = _att_scores(qb, k_cur, k_prev, slope_d, j > 0)
            lse_b, delta_b, dob = lse_ref[cur, :], dl_ref[cur, :], do_ref[cur, :]
            p_cur = jnp.exp(s_cur - lse_b)
            p_prev = jnp.exp(s_prev - lse_b)
            ds_cur = p_cur * (_dot_nt(dob, v_cur) - delta_b)
            ds_prev = p_prev * (_dot_nt(dob, v_prev) - delta_b)
            dq_acc[cur, :] = (_dot(ds_cur, k_cur) + _dot(ds_prev, k_prev)) * (DIL_DH ** -0.5)
            dk_acc[cur, :] += _dot_tn(ds_cur, qb)
            dv_acc[cur, :] += _dot_tn(p_cur, dob)

            @pl.when(j > 0)
            def _():
                dk_acc[prv, :] += _dot_tn(ds_prev, qb)
                dv_acc[prv, :] += _dot_tn(p_prev, dob)

            return carry

        lax.fori_loop(0, dil * nb, step, 0)
        dq_ref[...] = dq_acc[...].astype(BF16)
        dk_ref[...] = dk_acc[...].astype(BF16)
        dv_ref[...] = dv_acc[...].astype(BF16)

    def col(off):
        return pl.BlockSpec((s, DIL_DH), lambda h: (0, off // DIL_DH + group * DIL_HEADS + h))

    hd = pl.BlockSpec((s, DIL_DH), lambda h: (0, h))
    return pl.pallas_call(
        body, name=f"att_bwd{group}", grid=(DIL_HEADS,),
        in_specs=[col(OFF_Q_B), col(OFF_K_B), col(OFF_V_B), hd, hd, hd], out_specs=[hd, hd, hd],
        out_shape=[jax.ShapeDtypeStruct((s, DIL_W), BF16)] * 3,
        scratch_shapes=[pltpu.VMEM((s, DIL_DH), F32)] * 3,
        compiler_params=_cparams("parallel"))(proj, proj, proj, do, lse, delta)


def _att_merge(parts, proj):
    s = proj.shape[0]

    def body(n0, d0, m0, n1, d1, m1, n2, d2, m2, z_ref, ob_ref, o_ref, lse_ref, obt_ref):
        m = jnp.maximum(jnp.maximum(m0[...], m1[...]), m2[...])
        num = jnp.zeros_like(m)
        den = jnp.zeros_like(m)
        for nr, dr, mr in ((n0, d0, m0), (n1, d1, m1), (n2, d2, m2)):
            sc = jnp.exp(mr[...] - m)
            num = num + nr[...] * sc
            den = den + dr[...] * sc
        o = num / den
        o_ref[...] = o
        lse_ref[...] = m + jnp.log(den)
        ob = o * _silu(z_ref[...])
        ob_ref[...] = ob.astype(BF16)
        obt_ref[...] = ob.T.astype(BF16)

    row = pl.BlockSpec((ROW_TILE, DIL_W), lambda i: (i, 0))
    flat = [a for p in parts for a in p]
    return pl.pallas_call(
        body, name="att_merge", grid=(s // ROW_TILE,),
        in_specs=[row] * 9 + [pl.BlockSpec((ROW_TILE, DIL_W), lambda i: (i, OFF_Z_B // DIL_W))],
        out_specs=[row, row, row, pl.BlockSpec((DIL_W, ROW_TILE), lambda i: (0, i))],
        out_shape=[jax.ShapeDtypeStruct((s, DIL_W), BF16), jax.ShapeDtypeStruct((s, DIL_W), F32),
                   jax.ShapeDtypeStruct((s, DIL_W), F32), jax.ShapeDtypeStruct((DIL_W, s), BF16)],
        compiler_params=_cparams("parallel"))(*flat, proj)


def _att_merge_bwd(o, proj, dob):
    s = o.shape[0]

    def body(o_ref, z_ref, d_ref, do_ref, dl_ref, dz_ref):
        ov, zv, dv = o_ref[...], z_ref[...], d_ref[...]
        do = dv * _silu(zv)
        do_ref[...] = do
        dz_ref[...] = (dv * ov * _silu_grad(zv)).astype(BF16)
        for h in range(DIL_HEADS):
            sl = slice(h * DIL_DH, (h + 1) * DIL_DH)
            dl_ref[:, sl] = jnp.broadcast_to(jnp.sum(do[:, sl] * ov[:, sl], axis=-1, keepdims=True), (ROW_TILE, DIL_DH))

    row = pl.BlockSpec((ROW_TILE, DIL_W), lambda i: (i, 0))
    return pl.pallas_call(
        body, name="att_merge_bwd", grid=(s // ROW_TILE,),
        in_specs=[row, pl.BlockSpec((ROW_TILE, DIL_W), lambda i: (i, OFF_Z_B // DIL_W)), row],
        out_specs=[row, row, row],
        out_shape=[jax.ShapeDtypeStruct((s, DIL_W), F32), jax.ShapeDtypeStruct((s, DIL_W), F32),
                   jax.ShapeDtypeStruct((s, DIL_W), BF16)],
        compiler_params=_cparams("parallel"))(o, proj, dob)


def _merge(proj, ya, yb):
    s = proj.shape[0]

    def body(ga_ref, gb_ref, ya_ref, yb_ref, o_ref, ot_ref):
        m = _sigmoid(ga_ref[...]) * ya_ref[...] + _sigmoid(gb_ref[...]) * yb_ref[...]
        o_ref[...] = m.astype(BF16)
        ot_ref[...] = m.T.astype(BF16)

    row = pl.BlockSpec((ROW_TILE, D_MODEL), lambda i: (i, 0))
    return pl.pallas_call(
        body, name="merge", grid=(s // ROW_TILE,),
        in_specs=[pl.BlockSpec((ROW_TILE, D_MODEL), lambda i: (i, OFF_G_A // D_MODEL)),
                  pl.BlockSpec((ROW_TILE, D_MODEL), lambda i: (i, OFF_G_B // D_MODEL)), row, row],
        out_specs=[row, pl.BlockSpec((D_MODEL, ROW_TILE), lambda i: (0, i))],
        out_shape=[jax.ShapeDtypeStruct((s, D_MODEL), BF16), jax.ShapeDtypeStruct((D_MODEL, s), BF16)],
        compiler_params=_cparams("parallel"))(proj, proj, ya, yb)


def _merge_bwd(proj, ya, yb, dm):
    s = proj.shape[0]

    def body(ga_ref, gb_ref, ya_ref, yb_ref, dm_ref, dya_ref, dyb_ref, dga_ref, dgb_ref):
        dmv = dm_ref[...]
        sa, sb = _sigmoid(ga_ref[...]), _sigmoid(gb_ref[...])
        dya_ref[...] = (dmv * sa).astype(BF16)
        dyb_ref[...] = (dmv * sb).astype(BF16)
        dga_ref[...] = (dmv * ya_ref[...] * sa * (1.0 - sa)).astype(BF16)
        dgb_ref[...] = (dmv * yb_ref[...] * sb * (1.0 - sb)).astype(BF16)

    row = pl.BlockSpec((ROW_TILE, D_MODEL), lambda i: (i, 0))
    return pl.pallas_call(
        body, name="merge_bwd", grid=(s // ROW_TILE,),
        in_specs=[pl.BlockSpec((ROW_TILE, D_MODEL), lambda i: (i, OFF_G_A // D_MODEL)),
                  pl.BlockSpec((ROW_TILE, D_MODEL), lambda i: (i, OFF_G_B // D_MODEL)), row, row, row],
        out_specs=[row] * 4, out_shape=[jax.ShapeDtypeStruct((s, D_MODEL), BF16)] * 4,
        compiler_params=_cparams("parallel"))(proj, proj, ya, yb, dm)


def _final(x, t, fw, tgt):
    s, d = x.shape

    def body(x_ref, t_ref, w_ref, y_ref, dx_ref, dw_ref, l_ref):
        i = pl.program_id(0)
        x2 = x_ref[...] + t_ref[...]
        wv = w_ref[...]
        r = lax.rsqrt(jnp.mean(x2 * x2, axis=-1, keepdims=True) + NORM_EPS)
        e = x2 * r * wv - y_ref[...]
        lrow = jnp.mean(e * e, axis=-1, keepdims=True)
        lpart = jnp.broadcast_to(0.5 * jnp.sum(lrow, axis=0, keepdims=True), (1, 128))
        dy = e * (1.0 / d)
        dwp = jnp.sum(dy * x2 * r, axis=0, keepdims=True)
        dyw = dy * wv
        dx_ref[...] = r * dyw - x2 * (r * r * r) * jnp.mean(dyw * x2, axis=-1, keepdims=True)

        @pl.when(i == 0)
        def _():
            dw_ref[...] = dwp
            l_ref[...] = lpart

        @pl.when(i > 0)
        def _():
            dw_ref[...] += dwp
            l_ref[...] += lpart

    row = pl.BlockSpec((ROW_TILE, d), lambda i: (i, 0))
    vec = pl.BlockSpec((1, d), lambda i: (0, 0))
    return pl.pallas_call(
        body, name="final", grid=(s // ROW_TILE,), in_specs=[row, row, vec, row],
        out_specs=[row, vec, pl.BlockSpec((1, 128), lambda i: (0, 0))],
        out_shape=[jax.ShapeDtypeStruct((s, d), F32), jax.ShapeDtypeStruct((1, d), F32), jax.ShapeDtypeStruct((1, 128), F32)],
        compiler_params=_cparams("arbitrary"))(x, t, fw, tgt)


def _adamw(w, g, m, v, name):
    r, c = w.shape
    tr = r if r <= 128 else 128
    assert r % tr == 0

    def body(w_ref, g_ref, m_ref, v_ref, d_ref, nm_ref, nv_ref):
        gv = g_ref[...]
        mn = ADAM_B1 * m_ref[...] + (1.0 - ADAM_B1) * gv
        vn = ADAM_B2 * v_ref[...] + (1.0 - ADAM_B2) * (gv * gv)
        m_hat = mn / (1.0 - ADAM_B1 ** ADAM_STEP)
        v_hat = vn / (1.0 - ADAM_B2 ** ADAM_STEP)
        d_ref[...] = -ADAM_LR * (m_hat / (jnp.sqrt(v_hat) + ADAM_EPS) + ADAM_WD * w_ref[...])
        nm_ref[...] = mn
        nv_ref[...] = vn

    blk = pl.BlockSpec((tr, c), lambda i: (i, 0))
    return pl.pallas_call(
        body, name=name, grid=(r // tr,), in_specs=[blk] * 4, out_specs=[blk] * 3,
        out_shape=[jax.ShapeDtypeStruct((r, c), F32)] * 3, compiler_params=_cparams("parallel"))(w, g, m, v)


HBM_SPEC = pl.BlockSpec(memory_space=pl.ANY)


def _place():
    x, y, c = lax.axis_index("x"), lax.axis_index("y"), lax.axis_index("c")
    chips = [(1 - x, y), (x, 1 - y), (1 - x, 1 - y)]
    return x, y, c, chips


LOCAL_CHUNKS = 8


def _local_copies(src, dst, sems, base):
    rows = src.shape[0] // LOCAL_CHUNKS
    assert rows * LOCAL_CHUNKS == src.shape[0]
    return [pltpu.make_async_copy(src.at[pl.ds(q * rows, rows)], dst.at[pl.ds(q * rows, rows)], sems.at[base + q])
            for q in range(LOCAL_CHUNKS)]


def _ag_weights(packs):
    na = len(packs)

    def body(*refs):
        p_refs, out_refs = refs[:na], refs[na:2 * na]
        send_sems, recv_sems, local_sems = refs[2 * na:]
        x, y, c, chips = _place()
        me, sib, j = (x, y, c), (x, y, 1 - c), 2 * x + y

        def rc(k, src, dst, to):
            return pltpu.make_async_remote_copy(src_ref=src, dst_ref=dst, send_sem=send_sems.at[k],
                                                recv_sem=recv_sems.at[k], device_id=to, device_id_type=MESH)

        first = [rc(6 * a + k, p_refs[a].at[c], out_refs[a].at[j, c], (cx, cy, c))
                 for a in range(na) for k, (cx, cy) in enumerate(chips)]
        for cp in first:
            cp.start()
        mine = []
        for a in range(na):
            for hf in range(2):
                mine += _local_copies(p_refs[a].at[hf], out_refs[a].at[j, hf], local_sems, (2 * a + hf) * LOCAL_CHUNKS)
        for cp in mine:
            cp.start()
        passed = []
        for a in range(na):
            for k, (cx, cy) in enumerate(chips):
                land = out_refs[a].at[2 * cx + cy, c]
                rc(6 * a + k, p_refs[a].at[c], land, me).wait_recv()
                fwd = rc(6 * a + 3 + k, land, land, sib)
                fwd.start()
                passed.append(fwd)
        for a in range(na):
            for k, (cx, cy) in enumerate(chips):
                rc(6 * a + 3 + k, p_refs[a].at[c], out_refs[a].at[2 * cx + cy, 1 - c], me).wait_recv()
        for cp in first + passed:
            cp.wait_send()
        for cp in mine:
            cp.wait()

    return pl.pallas_call(
        body, name="ag_weights",
        out_shape=[jax.ShapeDtypeStruct((N_CHIPS,) + p.shape, p.dtype) for p in packs],
        in_specs=[HBM_SPEC] * na, out_specs=[HBM_SPEC] * na,
        scratch_shapes=[pltpu.SemaphoreType.DMA((6 * na,)), pltpu.SemaphoreType.DMA((6 * na,)),
                        pltpu.SemaphoreType.DMA((2 * na * LOCAL_CHUNKS,))])(*packs)


def _rs_pair(gpacks):
    na = len(gpacks)
    n = N_CHIPS

    def body(*refs):
        g_refs, out_refs = refs[:na], refs[na:2 * na]
        send_sems, recv_sems = refs[2 * na:]
        x, y, c, _ = _place()
        sib = (x, y, 1 - c)
        cps = [pltpu.make_async_remote_copy(src_ref=g_refs[a].at[p, 1 - c], dst_ref=out_refs[a].at[p],
                                            send_sem=send_sems.at[n * a + p], recv_sem=recv_sems.at[n * a + p],
                                            device_id=sib, device_id_type=MESH)
               for a in range(na) for p in range(n)]
        for cp in cps:
            cp.start()
        for cp in cps:
            cp.wait_recv()
        for cp in cps:
            cp.wait_send()

    return pl.pallas_call(
        body, name="rs_pair",
        out_shape=[jax.ShapeDtypeStruct((n,) + g.shape[2:], g.dtype) for g in gpacks],
        in_specs=[HBM_SPEC] * na, out_specs=[HBM_SPEC] * na,
        scratch_shapes=[pltpu.SemaphoreType.DMA((n * na,)), pltpu.SemaphoreType.DMA((n * na,))])(*gpacks)


def _rs_chips(csums):
    na = len(csums)

    def body(*refs):
        s_refs, out_refs = refs[:na], refs[na:2 * na]
        send_sems, recv_sems, local_sems = refs[2 * na:]
        x, y, c, chips = _place()
        j = 2 * x + y
        cps = [pltpu.make_async_remote_copy(src_ref=s_refs[a].at[2 * cx + cy], dst_ref=out_refs[a].at[j],
                                            send_sem=send_sems.at[3 * a + k], recv_sem=recv_sems.at[3 * a + k],
                                            device_id=(cx, cy, c), device_id_type=MESH)
               for a in range(na) for k, (cx, cy) in enumerate(chips)]
        for cp in cps:
            cp.start()
        mine = []
        for a in range(na):
            mine += _local_copies(s_refs[a].at[j], out_refs[a].at[j], local_sems, a * LOCAL_CHUNKS)
        for cp in mine:
            cp.start()
        for a in range(na):
            for k, (cx, cy) in enumerate(chips):
                pltpu.make_async_remote_copy(src_ref=s_refs[a].at[j], dst_ref=out_refs[a].at[2 * cx + cy],
                                             send_sem=send_sems.at[3 * a + k], recv_sem=recv_sems.at[3 * a + k],
                                             device_id=(x, y, c), device_id_type=MESH).wait_recv()
        for cp in cps:
            cp.wait_send()
        for cp in mine:
            cp.wait()

    return pl.pallas_call(
        body, name="rs_chips", out_shape=[jax.ShapeDtypeStruct(s.shape, s.dtype) for s in csums],
        in_specs=[HBM_SPEC] * na, out_specs=[HBM_SPEC] * na,
        scratch_shapes=[pltpu.SemaphoreType.DMA((3 * na,)), pltpu.SemaphoreType.DMA((3 * na,)),
                        pltpu.SemaphoreType.DMA((na * LOCAL_CHUNKS,))])(*csums)


SWAP_CHUNKS = 4


def _pair_swap(halves):
    na = len(halves)

    def body(*refs):
        h_refs, out_refs = refs[:na], refs[na:2 * na]
        send_sems, recv_sems, local_sems = refs[2 * na:]
        x, y, c, _ = _place()
        sends, recvs, mine = [], [], []
        for a in range(na):
            rows = h_refs[a].shape[0] // SWAP_CHUNKS
            assert rows * SWAP_CHUNKS == h_refs[a].shape[0]
            for q in range(SWAP_CHUNKS):
                src = h_refs[a].at[pl.ds(q * rows, rows)]
                k = SWAP_CHUNKS * a + q
                sends.append(pltpu.make_async_remote_copy(
                    src_ref=src, dst_ref=out_refs[a].at[c, pl.ds(q * rows, rows)], send_sem=send_sems.at[k],
                    recv_sem=recv_sems.at[k], device_id=(x, y, 1 - c), device_id_type=MESH))
                recvs.append(pltpu.make_async_remote_copy(
                    src_ref=src, dst_ref=out_refs[a].at[1 - c, pl.ds(q * rows, rows)], send_sem=send_sems.at[k],
                    recv_sem=recv_sems.at[k], device_id=(x, y, c), device_id_type=MESH))
            mine += _local_copies(h_refs[a], out_refs[a].at[c], local_sems, a * LOCAL_CHUNKS)
        for cp in sends + mine:
            cp.start()
        for cp in recvs:
            cp.wait_recv()
        for cp in sends:
            cp.wait_send()
        for cp in mine:
            cp.wait()

    return pl.pallas_call(
        body, name="pair_swap", out_shape=[jax.ShapeDtypeStruct((2,) + h.shape, h.dtype) for h in halves],
        in_specs=[HBM_SPEC] * na, out_specs=[HBM_SPEC] * na,
        scratch_shapes=[pltpu.SemaphoreType.DMA((SWAP_CHUNKS * na,)), pltpu.SemaphoreType.DMA((SWAP_CHUNKS * na,)),
                        pltpu.SemaphoreType.DMA((na * LOCAL_CHUNKS,))])(*halves)


def _ag_small(v):
    m_per, n = v.shape

    def body(x_ref, out_ref, send_sems, recv_sems, local_sem):
        x, y, c, chips = _place()
        me, sibling = (x, y, c), (x, y, 1 - c)

        def rows(px, py, pc):
            return out_ref.at[pl.ds((4 * px + 2 * py + pc) * m_per, m_per), :]

        def copy(k, block, to, src=None):
            return pltpu.make_async_remote_copy(
                src_ref=rows(*block) if src is None else src, dst_ref=rows(*block), send_sem=send_sems.at[k],
                recv_sem=recv_sems.at[k], device_id=to, device_id_type=MESH)

        mine = pltpu.make_async_copy(x_ref, rows(*me), local_sem)
        mine.start()
        first = [copy(0, me, sibling, src=x_ref)]
        first += [copy(1 + k, me, (*chip, c), src=x_ref) for k, chip in enumerate(chips)]
        for cp in first:
            cp.start()
        passed = [copy(4 + k, (*chip, c), sibling) for k, chip in enumerate(chips)]
        for k, chip in enumerate(chips):
            copy(1 + k, (*chip, c), me).wait_recv()
            passed[k].start()
        copy(0, sibling, me).wait_recv()
        for k, chip in enumerate(chips):
            copy(4 + k, (*chip, 1 - c), me).wait_recv()
        for cp in first + passed:
            cp.wait_send()
        mine.wait()

    return pl.pallas_call(
        body, name="ag_small", out_shape=jax.ShapeDtypeStruct((8 * m_per, n), v.dtype),
        in_specs=[pl.BlockSpec(memory_space=pltpu.VMEM)], out_specs=pl.BlockSpec(memory_space=pltpu.VMEM),
        scratch_shapes=[pltpu.SemaphoreType.DMA((7,)), pltpu.SemaphoreType.DMA((7,)), pltpu.SemaphoreType.DMA])(v)


def _sum_blocks(a, nblk, name):
    rows, wd = a.shape
    r = rows // nblk
    tr = min(r, ROW_TILE)
    assert r % tr == 0

    def body(*refs):
        acc = refs[0][...].astype(F32)
        for ref in refs[1:nblk]:
            acc = acc + ref[...].astype(F32)
        refs[nblk][...] = acc

    nt = r // tr
    return pl.pallas_call(
        body, name=name, grid=(nt,),
        in_specs=[pl.BlockSpec((tr, wd), functools.partial(lambda i, b: (b * nt + i, 0), b=b)) for b in range(nblk)],
        out_specs=pl.BlockSpec((tr, wd), lambda i: (i, 0)),
        out_shape=jax.ShapeDtypeStruct((r, wd), F32), compiler_params=_cparams("parallel"))(*([a] * nblk))


def _add_halves(gpack, other, c, name):
    n, _, rh, wd = gpack.shape
    tr = ROW_TILE
    assert rh % tr == 0

    def body(c_ref, g_ref, o_ref, out_ref):
        out_ref[0] = (g_ref[0, 0] + o_ref[0]).astype(BF16)

    grid_spec = pltpu.PrefetchScalarGridSpec(
        num_scalar_prefetch=1, grid=(n, rh // tr),
        in_specs=[pl.BlockSpec((1, 1, tr, wd), lambda p, i, cr: (p, cr[0], i, 0)),
                  pl.BlockSpec((1, tr, wd), lambda p, i, cr: (p, i, 0))],
        out_specs=pl.BlockSpec((1, tr, wd), lambda p, i, cr: (p, i, 0)))
    return pl.pallas_call(
        body, name=name, grid_spec=grid_spec, out_shape=jax.ShapeDtypeStruct((n, rh, wd), BF16),
        compiler_params=_cparams("parallel", "parallel"))(jnp.reshape(c, (1,)).astype(jnp.int32), gpack, other)


PACK_W = 1024
ROWS_O_DN = DN_W // N_CHIPS
ROWS_O_DIL = DIL_W * (D_MODEL // N_CHIPS) // PACK_W
ROWS_OUT = D_MODEL // N_CHIPS
ROWS_CONV = 4 * (3 * DN_W // N_CHIPS) // PACK_W
R1 = ROWS_O_DN
R2 = R1 + ROWS_O_DIL
R3 = R2 + ROWS_OUT
R4 = R3 + ROWS_CONV
R5 = R4 + ROWS_CONV
PACK_ROWS = 1024
HALF_ROWS = PACK_ROWS // 2
HALF_D = D_MODEL // 2


def _to_ref_layout(wp):
    return jnp.concatenate([wp[:, :REF_OFF_BA], wp[:, OFF_BA:OFF_BA + 2 * DN_HEADS], wp[:, REF_OFF_BA:OFF_BA]], axis=1)


def _from_ref_layout(w):
    pad = jnp.zeros((w.shape[0], PW - PROJ_W), w.dtype)
    return jnp.concatenate([w[:, :REF_OFF_BA], w[:, REF_OFF_BA + 2 * DN_HEADS:], w[:, REF_OFF_BA:REF_OFF_BA + 2 * DN_HEADS], pad],
                           axis=1)


def _local_step(x, tgt, norm_w, wp, conv_full, a_log, dt_bias, dn_norm_w, w_o_dn, w_o_dil, w_out, final_norm_w):
    s = x.shape[0]
    n = s // DN_CHUNK
    h, h_t = _rms_in(x, norm_w)
    proj = _matmul(h, wp, F32, 512, 1280, 1024, "proj")
    c_pre, qkv = _conv_fwd(proj, conv_full)
    gate_par = jnp.zeros((8, 128), F32).at[0, 8:16].set(a_log[0]).at[1, 8:16].set(dt_bias[0])
    bg = _gates_fwd(proj, gate_par)
    o_a, u, w, vn, tmat, states = _gdr_fwd(qkv, bg)
    oa2, oa2_t = _gdr_out(o_a, proj, dn_norm_w)
    ya = _matmul(oa2, w_o_dn, F32, 512, 1024, 1024, "ya")
    parts = [_att_fwd(proj, g) for g in range(N_DIL)]
    ob, o_att, lse, ob_t = _att_merge(parts, proj)
    yb = _matmul(ob, w_o_dil, F32, 512, 1024, 512, "yb")
    mg, mg_t = _merge(proj, ya, yb)
    t = _matmul(mg, w_out, F32, 512, 1024, 1024, "t_out")
    dx2, dfw, lpart = _final(x, t, final_norm_w, tgt)

    dmg = _matmul(dx2, w_out, F32, 512, 1024, 1024, "d_merged", nt=True)
    dw_out = _matmul(mg_t, dx2, F32, 1024, 1024, 1024, "dw_out")
    dya, dyb, dga, dgb = _merge_bwd(proj, ya, yb, dmg)
    doa2 = _matmul(dya, w_o_dn, F32, 512, 1024, 1024, "d_oa2", nt=True)
    dw_o_dn = _matmul(oa2_t, dya, F32, 1024, 1024, 1024, "dw_o_dn")
    dob = _matmul(dyb, w_o_dil, F32, 512, 512, 1024, "d_ob", nt=True)
    dw_o_dil = _matmul(ob_t, dyb, F32, 512, 1024, 1024, "dw_o_dil")
    do_a, dz_a, ddnw = _gdr_out_bwd(o_a, proj, dn_norm_w, doa2)
    dq_a, dk_a, dv_a, dbg = _gdr_bwd(qkv, bg, u, w, vn, tmat, states, do_a)
    dba, dpar = _gates_bwd(proj, gate_par, dbg)
    dc = _conv_bwd_act(c_pre, dq_a, dk_a, dv_a)
    du_a, dconv = _conv_bwd(proj, dc, conv_full)
    do_att, delta, dz_b = _att_merge_bwd(o_att, proj, dob)
    dqkv_b = [_att_bwd(proj, g, do_att, lse, delta) for g in range(N_DIL)]
    dproj = jnp.concatenate(
        [du_a, dz_a] + [dqkv_b[g][i] for i in range(3) for g in range(N_DIL)]
        + [dz_b, dga, dgb, dba, jnp.zeros((s, PW - OFF_BA - 128), BF16)], axis=1)
    dh = _matmul(dproj, wp, F32, 512, 1024, 2304, "d_h", nt=True)
    dwp = _matmul(h_t, dproj, F32, 1024, 1280, 1024, "dw_in")
    grad_x, dnw = _rms_in_bwd(x, norm_w, dh, dx2)
    small = jnp.zeros((8, PACK_W), F32)
    small = small.at[0].set(dnw[0]).at[1].set(dfw[0]).at[2, :DN_D].set(ddnw[0])
    small = small.at[3, :DN_HEADS].set(dpar[0, 8:16]).at[3, DN_HEADS:2 * DN_HEADS].set(dpar[1, 8:16])
    small = small.at[4, 0].set(lpart[0, 0])
    return grad_x, _to_ref_layout(dwp), dconv, dw_o_dn, dw_o_dil, dw_out, small


def kernel(x, norm_w, w_in, conv_w, a_log, dt_bias, dn_norm_w, w_o_dn, w_o_dil, w_out, final_norm_w, loss_target, m_norm_w, m_w_in, m_conv_w, m_a_log, m_dt_bias, m_dn_norm_w, m_w_o_dn, m_w_o_dil, m_w_out, m_final_norm_w, v_norm_w, v_w_in, v_conv_w, v_a_log, v_dt_bias, v_dn_norm_w, v_w_o_dn, v_w_o_dil, v_w_out, v_final_norm_w):
    c = lax.axis_index("c")
    qw = D_MODEL // N_CHIPS

    cw = conv_w[0].reshape(ROWS_CONV, PACK_W)
    cw_hi = cw.astype(BF16)
    cw_lo = (cw - cw_hi.astype(F32)).astype(BF16)
    pack = jnp.concatenate(
        [w_o_dn[0].astype(BF16), w_o_dil[0].astype(BF16).reshape(ROWS_O_DIL, PACK_W), w_out[0].astype(BF16), cw_hi, cw_lo,
         jnp.zeros((PACK_ROWS - R5, PACK_W), BF16)], axis=0).reshape(2, HALF_ROWS, PACK_W)
    all_in, allw = _ag_weights([w_in[0].astype(BF16).reshape(2, HALF_D, SHARD_W), pack])
    all_in = all_in.reshape(N_CHIPS, D_MODEL, SHARD_W)
    allw = allw.reshape(N_CHIPS, PACK_ROWS, PACK_W)
    chips = range(N_CHIPS)
    w_in_full = jnp.concatenate([all_in[k] for k in chips], axis=1)
    w_o_dn_full = jnp.concatenate([allw[k, :R1] for k in chips], axis=0)
    w_o_dil_full = jnp.concatenate([allw[k, R1:R2].reshape(DIL_W, qw) for k in chips], axis=1)
    w_out_full = jnp.concatenate([allw[k, R2:R3] for k in chips], axis=0)
    conv_full = jnp.concatenate(
        [(allw[k, R3:R4].astype(F32) + allw[k, R4:R5].astype(F32)).reshape(4, 3 * DN_W // N_CHIPS) for k in chips], axis=1)
    wp = _from_ref_layout(w_in_full)

    grad_x, dw_in, dconv, dw_o_dn, dw_o_dil, dw_out, small = _local_step(
        x[0], loss_target[0], norm_w, wp, conv_full, a_log, dt_bias, dn_norm_w, w_o_dn_full, w_o_dil_full, w_out_full,
        final_norm_w.reshape(1, D_MODEL))

    cq = 3 * DN_W // N_CHIPS
    g_in = jnp.stack([dw_in[:, k * SHARD_W:(k + 1) * SHARD_W] for k in chips]).reshape(N_CHIPS, 2, HALF_D, SHARD_W)
    gpack = jnp.stack([
        jnp.concatenate(
            [dw_o_dn[k * qw:(k + 1) * qw], dw_o_dil[:, k * qw:(k + 1) * qw].reshape(ROWS_O_DIL, PACK_W),
             dw_out[k * qw:(k + 1) * qw], dconv[:, k * cq:(k + 1) * cq].reshape(ROWS_CONV, PACK_W),
             jnp.zeros((PACK_ROWS - R4, PACK_W), F32)], axis=0)
        for k in chips]).reshape(N_CHIPS, 2, HALF_ROWS, PACK_W)
    sib_in, sib_pack = _rs_pair([g_in, gpack])
    csum_in = _add_halves(g_in, sib_in, c, "add_halves_in")
    csum_pack = _add_halves(gpack, sib_pack, c, "add_halves_pack")
    src_in, src_pack = _rs_chips([csum_in, csum_pack])
    half_in = _sum_blocks(src_in.reshape(N_CHIPS * HALF_D, SHARD_W), N_CHIPS, "sum_chips_in")
    half_pack = _sum_blocks(src_pack.reshape(N_CHIPS * HALF_ROWS, PACK_W), N_CHIPS, "sum_chips_pack")
    g_w_in, g = _pair_swap([half_in, half_pack])
    g_w_in = g_w_in.reshape(D_MODEL, SHARD_W)
    g = g.reshape(PACK_ROWS, PACK_W)
    g_w_o_dn = g[:R1]
    g_w_o_dil = g[R1:R2].reshape(DIL_W, qw)
    g_w_out = g[R2:R3]
    g_conv = g[R3:R4].reshape(4, cq)

    gs = _sum_blocks(_ag_small(small), 8, "sum_small")
    loss = gs[4, 0]
    w_small = jnp.zeros((8, PACK_W), F32)

    def pack_small(nw, fw, dnw_, al, db):
        t = w_small.at[0].set(nw[0]).at[1].set(fw).at[2, :DN_D].set(dnw_[0])
        return t.at[3, :DN_HEADS].set(al[0]).at[3, DN_HEADS:2 * DN_HEADS].set(db[0])

    sm = _adamw(pack_small(norm_w, final_norm_w, dn_norm_w, a_log, dt_bias), gs,
                pack_small(m_norm_w, m_final_norm_w, m_dn_norm_w, m_a_log, m_dt_bias),
                pack_small(v_norm_w, v_final_norm_w, v_dn_norm_w, v_a_log, v_dt_bias), "adamw_small")

    def unpack_small(t):
        return dict(norm_w=t[0:1], final_norm_w=t[1], dn_norm_w=t[2:3, :DN_D], a_log=t[3:4, :DN_HEADS],
                    dt_bias=t[3:4, DN_HEADS:2 * DN_HEADS])

    res = {"grad": unpack_small(gs)}
    for kind, arr in zip(("delta", "new_m", "new_v"), sm):
        res[kind] = unpack_small(arr)
    big = dict(w_in=(w_in, g_w_in, m_w_in, v_w_in), conv_w=(conv_w, g_conv, m_conv_w, v_conv_w),
               w_o_dn=(w_o_dn, g_w_o_dn, m_w_o_dn, v_w_o_dn), w_o_dil=(w_o_dil, g_w_o_dil, m_w_o_dil, v_w_o_dil),
               w_out=(w_out, g_w_out, m_w_out, v_w_out))
    for name, (wt, gt, mt, vt) in big.items():
        d, nm, nv = _adamw(wt[0], gt, mt[0], vt[0], "adamw_" + name)
        res["grad"][name] = gt[None]
        res["delta"][name], res["new_m"][name], res["new_v"][name] = d[None], nm[None], nv[None]
    order = ["norm_w", "w_in", "conv_w", "a_log", "dt_bias", "dn_norm_w", "w_o_dn", "w_o_dil", "w_out", "final_norm_w"]
    outs = [loss, grad_x[None]]
    for kind in ("grad", "delta", "new_m", "new_v"):
        outs += [res[kind][nm] for nm in order]
    return tuple(outs)
```

```python
import functools
import math

import jax
import jax.numpy as jnp
from jax import lax
from jax.experimental import pallas as pl
from jax.experimental.pallas import tpu as pltpu

F32 = jnp.float32
BF16 = jnp.bfloat16
MESH = pl.DeviceIdType.MESH

D_MODEL = 1024
DN_HEADS = 8
DN_D = 128
DN_CHUNK = 64
DN_W = DN_HEADS * DN_D
DIL_GROUPS = ((128, 1), (512, 4), (2048, 16))
N_DIL = len(DIL_GROUPS)
DIL_HEADS = 4
DIL_DH = 128
DIL_W = DIL_HEADS * DIL_DH
ATT_BLOCK = 128
NORM_EPS = 1e-6
PROJ_W = 11280
N_CHIPS = 4
SHARD_W = PROJ_W // N_CHIPS

OFF_QKV_A = 0
OFF_Z_A = 3072
OFF_Q_B = 4096
OFF_K_B = 5632
OFF_V_B = 7168
OFF_Z_B = 8704
OFF_G_A = 9216
OFF_G_B = 10240
OFF_BA = 11264
PW = 11520
REF_OFF_BA = 4096

ADAM_LR = 0.001
ADAM_B1 = 0.9
ADAM_B2 = 0.999
ADAM_EPS = 1e-08
ADAM_WD = 0.01
ADAM_STEP = 10

ROW_TILE = 256
NEG = -1e30


def _dot(a, b):
    return jnp.dot(a.astype(BF16), b.astype(BF16), preferred_element_type=F32)


def _dot_nt(a, b):
    return lax.dot_general(a.astype(BF16), b.astype(BF16), (((1,), (1,)), ((), ())), preferred_element_type=F32)


def _dot_tn(a, b):
    return lax.dot_general(a.astype(BF16), b.astype(BF16), (((0,), (0,)), ((), ())), preferred_element_type=F32)


def _split(a):
    hi = a.astype(BF16)
    lo = (a - hi.astype(F32)).astype(BF16)
    return hi, lo


def _dot_exact_lhs(c, a):
    hi, lo = _split(a)
    cb = c.astype(BF16)
    return jnp.dot(cb, hi, preferred_element_type=F32) + jnp.dot(cb, lo, preferred_element_type=F32)


def _dot_exact_rhs(a, c):
    hi, lo = _split(a)
    cb = c.astype(BF16)
    return jnp.dot(hi, cb, preferred_element_type=F32) + jnp.dot(lo, cb, preferred_element_type=F32)


def _dot_tn_exact_rhs(a, c):
    hi, lo = _split(a)
    cb = c.astype(BF16)
    dn = (((0,), (0,)), ((), ()))
    return (lax.dot_general(hi, cb, dn, preferred_element_type=F32)
            + lax.dot_general(lo, cb, dn, preferred_element_type=F32))


def _sigmoid(x):
    return 1.0 / (1.0 + jnp.exp(-x))


def _silu(x):
    return x * _sigmoid(x)


def _silu_grad(x):
    s = _sigmoid(x)
    return s * (1.0 + x * (1.0 - s))


def _softplus(x):
    return jnp.maximum(x, 0.0) + jnp.log(1.0 + jnp.exp(-jnp.abs(x)))


def _cparams(*sem):
    return pltpu.CompilerParams(dimension_semantics=sem)


def _matmul(a, b, out_dtype, tm, tn, tk, name, nt=False):
    m, kdim = a.shape
    n = b.shape[0] if nt else b.shape[1]
    tm, tn, tk = min(tm, m), min(tn, n), min(tk, kdim)
    assert m % tm == 0 and n % tn == 0 and kdim % tk == 0, (name, a.shape, b.shape, tm, tn, tk)
    nk = kdim // tk
    dot = _dot_nt if nt else _dot
    b_spec = (pl.BlockSpec((tn, tk), lambda i, j, k: (j, k)) if nt else pl.BlockSpec((tk, tn), lambda i, j, k: (k, j)))

    if nk == 1:
        def body(a_ref, b_ref, o_ref):
            o_ref[...] = dot(a_ref[...], b_ref[...]).astype(o_ref.dtype)
        scratch = []
    else:
        def body(a_ref, b_ref, o_ref, acc_ref):
            k = pl.program_id(2)
            p = dot(a_ref[...], b_ref[...])

            @pl.when(k == 0)
            def _():
                acc_ref[...] = p

            @pl.when(k > 0)
            def _():
                acc_ref[...] += p

            @pl.when(k == nk - 1)
            def _():
                o_ref[...] = acc_ref[...].astype(o_ref.dtype)
        scratch = [pltpu.VMEM((tm, tn), F32)]

    return pl.pallas_call(
        body, name=name, grid=(m // tm, n // tn, nk),
        in_specs=[pl.BlockSpec((tm, tk), lambda i, j, k: (i, k)), b_spec],
        out_specs=pl.BlockSpec((tm, tn), lambda i, j, k: (i, j)),
        out_shape=jax.ShapeDtypeStruct((m, n), out_dtype), scratch_shapes=scratch,
        compiler_params=_cparams("parallel", "parallel", "arbitrary"))(a, b)


def _rms_in(x, nw):
    s, d = x.shape

    def body(x_ref, w_ref, h_ref, ht_ref):
        xv = x_ref[...]
        r = lax.rsqrt(jnp.mean(xv * xv, axis=-1, keepdims=True) + NORM_EPS)
        h = xv * r * w_ref[...]
        h_ref[...] = h.astype(BF16)
        ht_ref[...] = h.T.astype(BF16)

    return pl.pallas_call(
        body, name="rms_in", grid=(s // ROW_TILE,),
        in_specs=[pl.BlockSpec((ROW_TILE, d), lambda i: (i, 0)), pl.BlockSpec((1, d), lambda i: (0, 0))],
        out_specs=[pl.BlockSpec((ROW_TILE, d), lambda i: (i, 0)), pl.BlockSpec((d, ROW_TILE), lambda i: (0, i))],
        out_shape=[jax.ShapeDtypeStruct((s, d), BF16), jax.ShapeDtypeStruct((d, s), BF16)],
        compiler_params=_cparams("parallel"))(x, nw)


def _rms_in_bwd(x, nw, dh, dx2):
    s, d = x.shape

    def body(x_ref, w_ref, dh_ref, dx2_ref, dx_ref, dw_ref):
        i = pl.program_id(0)
        xv = x_ref[...]
        r = lax.rsqrt(jnp.mean(xv * xv, axis=-1, keepdims=True) + NORM_EPS)
        dhv = dh_ref[...]
        dyw = dhv * w_ref[...]
        dx_ref[...] = dx2_ref[...] + r * dyw - xv * (r * r * r) * jnp.mean(dyw * xv, axis=-1, keepdims=True)
        part = jnp.sum(dhv * xv * r, axis=0, keepdims=True)

        @pl.when(i == 0)
        def _():
            dw_ref[...] = part

        @pl.when(i > 0)
        def _():
            dw_ref[...] += part

    row = pl.BlockSpec((ROW_TILE, d), lambda i: (i, 0))
    vec = pl.BlockSpec((1, d), lambda i: (0, 0))
    return pl.pallas_call(
        body, name="rms_in_bwd", grid=(s // ROW_TILE,), in_specs=[row, vec, row, row], out_specs=[row, vec],
        out_shape=[jax.ShapeDtypeStruct((s, d), F32), jax.ShapeDtypeStruct((1, d), F32)],
        compiler_params=_cparams("arbitrary"))(x, nw, dh, dx2)


def _shift_down(cur, prev8, k):
    rc = pltpu.roll(cur, k, 0)
    rp = pltpu.roll(prev8, k, 0)
    row = lax.broadcasted_iota(jnp.int32, prev8.shape, 0)
    top = jnp.where(row < k, rp, rc[:8])
    return jnp.concatenate([top, rc[8:]], axis=0)


def _shift_up(cur, next8, k):
    t = cur.shape[0]
    rc = pltpu.roll(cur, t - k, 0)
    rn = pltpu.roll(next8, 8 - k, 0)
    row = lax.broadcasted_iota(jnp.int32, next8.shape, 0)
    bot = jnp.where(row >= 8 - k, rn, rc[t - 8:])
    return jnp.concatenate([rc[:t - 8], bot], axis=0)


def _conv_fwd(proj, conv_w):
    s = proj.shape[0]
    t8 = ROW_TILE // 8

    def body(u_ref, up_ref, w_ref, c_ref, y_ref):
        i = pl.program_id(0)
        part = pl.program_id(1)
        cur = u_ref[...]
        prev8 = jnp.where(i > 0, up_ref[...], 0.0)
        w = w_ref[...]
        c = cur * w[3:4, :]
        for k in (1, 2, 3):
            c = c + _shift_down(cur, prev8, k) * w[3 - k:4 - k, :]
        c_ref[...] = c
        a = _silu(c)
        for h in range(DN_HEADS):
            ah = a[:, h * DN_D:(h + 1) * DN_D]
            r = lax.rsqrt(jnp.sum(ah * ah, axis=-1, keepdims=True) + NORM_EPS)
            y_ref[:, h * DN_D:(h + 1) * DN_D] = jnp.where(part < 2, ah * r, ah)

    return pl.pallas_call(
        body, name="conv_fwd", grid=(s // ROW_TILE, 3),
        in_specs=[pl.BlockSpec((ROW_TILE, DN_W), lambda i, p: (i, p)),
                  pl.BlockSpec((8, DN_W), lambda i, p: (jnp.maximum(i * t8 - 1, 0), p)),
                  pl.BlockSpec((4, DN_W), lambda i, p: (0, p))],
        out_specs=[pl.BlockSpec((ROW_TILE, DN_W), lambda i, p: (i, p))] * 2,
        out_shape=[jax.ShapeDtypeStruct((s, 3 * DN_W), F32)] * 2,
        compiler_params=_cparams("parallel", "parallel"))(proj, proj, conv_w)


def _conv_bwd_act(c, dq, dk, dv):
    s = c.shape[0]

    def body(c_ref, dq_ref, dk_ref, dv_ref, dc_ref):
        for part, d_ref in enumerate((dq_ref, dk_ref, dv_ref)):
            for h in range(DN_HEADS):
                sl = slice(part * DN_W + h * DN_D, part * DN_W + (h + 1) * DN_D)
                ch = c_ref[:, sl]
                dyh = d_ref[:, h * DN_D:(h + 1) * DN_D]
                if part < 2:
                    ah = _silu(ch)
                    r = lax.rsqrt(jnp.sum(ah * ah, axis=-1, keepdims=True) + NORM_EPS)
                    dyh = r * dyh - ah * (r * r * r) * jnp.sum(dyh * ah, axis=-1, keepdims=True)
                dc_ref[:, sl] = dyh * _silu_grad(ch)

    wide = pl.BlockSpec((ROW_TILE, 3 * DN_W), lambda i: (i, 0))
    row = pl.BlockSpec((ROW_TILE, DN_W), lambda i: (i, 0))
    return pl.pallas_call(
        body, name="conv_bwd_act", grid=(s // ROW_TILE,), in_specs=[wide, row, row, row], out_specs=wide,
        out_shape=jax.ShapeDtypeStruct((s, 3 * DN_W), F32), compiler_params=_cparams("parallel"))(c, dq, dk, dv)


def _conv_bwd(proj, dc, conv_w):
    s = proj.shape[0]
    t8 = ROW_TILE // 8
    nrow = s // ROW_TILE
    last8 = s // 8 - 1

    def body(u_ref, up_ref, dc_ref, dcn_ref, w_ref, du_ref, dw_ref):
        i = pl.program_id(1)
        cur = u_ref[...]
        prev8 = jnp.where(i > 0, up_ref[...], 0.0)
        dcv = dc_ref[...]
        next8 = jnp.where(i < nrow - 1, dcn_ref[...], 0.0)
        w = w_ref[...]
        du = dcv * w[3:4, :]
        for k in (1, 2, 3):
            du = du + _shift_up(dcv, next8, k) * w[3 - k:4 - k, :]
        du_ref[...] = du.astype(BF16)

        @pl.when(i == 0)
        def _():
            dw_ref[...] = jnp.zeros_like(dw_ref)

        dw_ref[3:4, :] += jnp.sum(cur * dcv, axis=0, keepdims=True)
        for k in (1, 2, 3):
            dw_ref[3 - k:4 - k, :] += jnp.sum(_shift_down(cur, prev8, k) * dcv, axis=0, keepdims=True)

    blk = pl.BlockSpec((ROW_TILE, DN_W), lambda p, i: (i, p))
    return pl.pallas_call(
        body, name="conv_bwd", grid=(3, nrow),
        in_specs=[blk, pl.BlockSpec((8, DN_W), lambda p, i: (jnp.maximum(i * t8 - 1, 0), p)),
                  blk, pl.BlockSpec((8, DN_W), lambda p, i: (jnp.minimum((i + 1) * t8, last8), p)),
                  pl.BlockSpec((4, DN_W), lambda p, i: (0, p))],
        out_specs=[blk, pl.BlockSpec((4, DN_W), lambda p, i: (0, p))],
        out_shape=[jax.ShapeDtypeStruct((s, 3 * DN_W), BF16), jax.ShapeDtypeStruct((4, 3 * DN_W), F32)],
        compiler_params=_cparams("parallel", "arbitrary"))(proj, proj, dc, dc, conv_w)


def _gates_fwd(proj, gate_par):
    s = proj.shape[0]

    def body(ba_ref, par_ref, o_ref):
        v = ba_ref[...]
        lane = lax.broadcasted_iota(jnp.int32, v.shape, 1)
        beta = _sigmoid(v)
        g = -jnp.exp(par_ref[0:1, :]) * _softplus(v + par_ref[1:2, :])
        o_ref[...] = jnp.where(lane < DN_HEADS, beta, jnp.where(lane < 2 * DN_HEADS, g, 0.0))

    return pl.pallas_call(
        body, name="gates_fwd", grid=(s // ROW_TILE,),
        in_specs=[pl.BlockSpec((ROW_TILE, 128), lambda i: (i, OFF_BA // 128)), pl.BlockSpec((8, 128), lambda i: (0, 0))],
        out_specs=pl.BlockSpec((ROW_TILE, 128), lambda i: (i, 0)),
        out_shape=jax.ShapeDtypeStruct((s, 128), F32), compiler_params=_cparams("parallel"))(proj, gate_par)


def _gates_bwd(proj, gate_par, dbg):
    s = proj.shape[0]

    def body(ba_ref, par_ref, d_ref, o_ref, dpar_ref):
        i = pl.program_id(0)
        v = ba_ref[...]
        dv = d_ref[...]
        lane = lax.broadcasted_iota(jnp.int32, v.shape, 1)
        beta = _sigmoid(v)
        nega = -jnp.exp(par_ref[0:1, :])
        xs = v + par_ref[1:2, :]
        dsp = dv * nega * _sigmoid(xs)
        dal = dv * nega * _softplus(xs)
        is_b = lane < DN_HEADS
        is_g = jnp.logical_and(lane >= DN_HEADS, lane < 2 * DN_HEADS)
        o_ref[...] = jnp.where(is_b, dv * beta * (1.0 - beta), jnp.where(is_g, dsp, 0.0)).astype(BF16)
        r0 = jnp.sum(jnp.where(is_g, dal, 0.0), axis=0, keepdims=True)
        r1 = jnp.sum(jnp.where(is_g, dsp, 0.0), axis=0, keepdims=True)

        @pl.when(i == 0)
        def _():
            dpar_ref[...] = jnp.zeros_like(dpar_ref)

        dpar_ref[0:1, :] += r0
        dpar_ref[1:2, :] += r1

    return pl.pallas_call(
        body, name="gates_bwd", grid=(s // ROW_TILE,),
        in_specs=[pl.BlockSpec((ROW_TILE, 128), lambda i: (i, OFF_BA // 128)), pl.BlockSpec((8, 128), lambda i: (0, 0)),
                  pl.BlockSpec((ROW_TILE, 128), lambda i: (i, 0))],
        out_specs=[pl.BlockSpec((ROW_TILE, 128), lambda i: (i, 0)), pl.BlockSpec((8, 128), lambda i: (0, 0))],
        out_shape=[jax.ShapeDtypeStruct((s, 128), BF16), jax.ShapeDtypeStruct((8, 128), F32)],
        compiler_params=_cparams("arbitrary"))(proj, gate_par, dbg)


def _chunk_masks():
    c = DN_CHUNK
    ii = lax.broadcasted_iota(jnp.int32, (c, c), 0)
    jj = lax.broadcasted_iota(jnp.int32, (c, c), 1)
    return dict(ii=ii, jj=jj, lower=(ii >= jj), strict=(ii > jj), eye=(ii == jj),
                lower_f=(ii >= jj).astype(BF16), upper_f=(ii <= jj).astype(BF16), ones8=jnp.ones((8, c), BF16))


class _Heads:
    def __init__(self, xs):
        self.xs = list(xs)

    def _bin(self, o, f):
        if isinstance(o, _Heads):
            return _Heads([f(a, b) for a, b in zip(self.xs, o.xs)])
        return _Heads([f(a, o) for a in self.xs])

    def __add__(self, o):
        return self._bin(o, lambda a, b: a + b)

    def __sub__(self, o):
        return self._bin(o, lambda a, b: a - b)

    def __mul__(self, o):
        return self._bin(o, lambda a, b: a * b)

    __radd__ = __add__
    __rmul__ = __mul__

    def __neg__(self):
        return _Heads([-a for a in self.xs])

    def __getitem__(self, i):
        return _Heads([a[i] for a in self.xs])


def _hmap(f, *args):
    n = next(len(a.xs) for a in args if isinstance(a, _Heads))
    return _Heads([f(*[(a.xs[h] if isinstance(a, _Heads) else a) for a in args]) for h in range(n)])


def _hdot(a, b):
    return _hmap(_dot, a, b)


def _hdot_nt(a, b):
    return _hmap(_dot_nt, a, b)


def _hdot_tn(a, b):
    return _hmap(_dot_tn, a, b)


def _hsum(a, axis):
    return _hmap(lambda t: jnp.sum(t, axis=axis, keepdims=True), a)


def _hwhere(c, a, b):
    return _hmap(jnp.where, c, a, b)


def _chunk_common(mk, q, k, beta_col, g_col):
    c = DN_CHUNK
    lower, strict = mk["lower"], mk["strict"]
    qs = q * (DN_D ** -0.5)
    beta_b = _hmap(lambda t: jnp.broadcast_to(t, (c, DN_D)), beta_col)
    g_b = _hmap(lambda t: jnp.broadcast_to(t, (c, DN_D)), g_col)
    gc_b = _hmap(_dot_exact_lhs, mk["lower_f"], g_b)
    gc_sq = gc_b[:, :c]
    gc_r = _hmap(_dot_exact_lhs, mk["ones8"], _hwhere(mk["eye"], gc_sq, 0.0))[0:1, :]
    gam = _hwhere(lower, _hmap(lambda t: jnp.exp(jnp.minimum(t, 0.0)), gc_sq - gc_r), 0.0)
    egc = _hmap(jnp.exp, gc_b)
    gl = gc_b[c - 1:c, :]
    ekd = _hmap(jnp.exp, gl - gc_b)
    dl = _hmap(jnp.exp, gl)
    kb = k * beta_b
    a_strict = _hwhere(strict, _hdot_nt(kb, k) * gam, 0.0)
    aqk = _hwhere(lower, _hdot_nt(qs, k) * gam, 0.0)
    return dict(k=k, qs=qs, beta_b=beta_b, gc_b=gc_b, gam=gam, egc=egc, ekd=ekd, dl=dl, kb=kb, a_strict=a_strict, aqk=aqk)


def _unit_lower_inverse_minus_eye(n_strict, ii, jj):
    same = lax.shift_right_logical(ii, 4) == lax.shift_right_logical(jj, 4)
    dmat = _hwhere(same, n_strict, 0.0)
    omat = n_strict - dmat
    d2 = _hdot(dmat, dmat)
    d4 = _hdot(d2, d2)
    d8 = _hdot(d4, d4)
    x1 = d2 - dmat - _hdot(dmat, d2)
    x2 = x1 + d4 + _hdot(x1, d4)
    x3 = x2 + d8 + _hdot(x2, d8)
    n1 = omat + _hdot(x3, omat)
    n2 = _hdot(n1, n1)
    y = n2 - n1 - _hdot(n1, n2)
    return y + x3 + _hdot(y, x3)


def _gdr_fwd(qkv, bg):
    s = qkv.shape[0]
    c = DN_CHUNK
    n = s // c

    def body(q_ref, k_ref, v_ref, bg_ref, o_ref, u_ref, w_ref, vn_ref, tm_ref, st_ref, state):
        @pl.when(pl.program_id(0) == 0)
        def _():
            state[...] = jnp.zeros_like(state)

        mk = _chunk_masks()
        bg = bg_ref[...]
        hs = range(DN_HEADS)
        sls = [slice(h * DN_D, (h + 1) * DN_D) for h in hs]
        cm = _chunk_common(mk, _Heads(q_ref[:, sl] for sl in sls), _Heads(k_ref[:, sl] for sl in sls),
                           _Heads(bg[:, h:h + 1] for h in hs), _Heads(bg[:, DN_HEADS + h:DN_HEADS + h + 1] for h in hs))
        tm = _unit_lower_inverse_minus_eye(cm["a_strict"], mk["ii"], mk["jj"])
        rhs_u = _Heads(v_ref[:, sl] for sl in sls) * cm["beta_b"]
        rhs_w = cm["kb"] * cm["egc"]
        u = rhs_u + _hdot(tm, rhs_u)
        w = rhs_w + _hdot(tm, rhs_w)
        st = _Heads(state[h] for h in hs)
        v_new = u - _hdot(w, st)
        o = _hdot(cm["qs"] * cm["egc"], st) + _hdot(cm["aqk"], v_new)
        st_new = st * cm["dl"] + _hdot_tn(cm["k"] * cm["ekd"], v_new)
        for h, sl in zip(hs, sls):
            o_ref[:, sl] = o.xs[h]
            u_ref[:, sl] = u.xs[h]
            w_ref[:, sl] = w.xs[h]
            vn_ref[:, sl] = v_new.xs[h]
            tm_ref[h, 0] = tm.xs[h]
            st_ref[h, 0] = st.xs[h]
            state[h] = st_new.xs[h]

    def part(p):
        return pl.BlockSpec((c, DN_W), lambda j: (j, p))

    return pl.pallas_call(
        body, name="gdr_fwd", grid=(n,),
        in_specs=[part(0), part(1), part(2), pl.BlockSpec((c, 128), lambda j: (j, 0))],
        out_specs=[part(0)] * 4 + [pl.BlockSpec((DN_HEADS, 1, c, c), lambda j: (0, j, 0, 0)),
                                   pl.BlockSpec((DN_HEADS, 1, DN_D, DN_D), lambda j: (0, j, 0, 0))],
        out_shape=[jax.ShapeDtypeStruct((s, DN_W), F32)] * 4
        + [jax.ShapeDtypeStruct((DN_HEADS, n, c, c), F32), jax.ShapeDtypeStruct((DN_HEADS, n, DN_D, DN_D), F32)],
        scratch_shapes=[pltpu.VMEM((DN_HEADS, DN_D, DN_D), F32)],
        compiler_params=_cparams("arbitrary"))(qkv, qkv, qkv, bg)


def _gdr_bwd(qkv, bg, u, w, vn, tmat, states, do):
    s = qkv.shape[0]
    c = DN_CHUNK
    n = s // c

    def body(q_ref, k_ref, v_ref, bg_ref, u_ref, w_ref, vn_ref, tm_ref, st_ref, do_ref,
             dq_ref, dk_ref, dv_ref, dbg_ref, dstate):
        @pl.when(pl.program_id(0) == 0)
        def _():
            dstate[...] = jnp.zeros_like(dstate)

        mk = _chunk_masks()
        lower, strict = mk["lower"], mk["strict"]
        bg = bg_ref[...]
        ones = jnp.ones((c, DN_D), BF16)
        rowi = lax.broadcasted_iota(jnp.int32, (c, DN_D), 0)
        lane = lax.broadcasted_iota(jnp.int32, (c, 128), 1)
        hs = range(DN_HEADS)
        sls = [slice(h * DN_D, (h + 1) * DN_D) for h in hs]

        def heads_of(ref):
            return _Heads(ref[:, sl] for sl in sls)

        cm = _chunk_common(mk, heads_of(q_ref), heads_of(k_ref),
                           _Heads(bg[:, h:h + 1] for h in hs), _Heads(bg[:, DN_HEADS + h:DN_HEADS + h + 1] for h in hs))
        k, qs, beta_b = cm["k"], cm["qs"], cm["beta_b"]
        gam, egc, ekd, dl, kb = cm["gam"], cm["egc"], cm["ekd"], cm["dl"], cm["kb"]
        aqk, a_strict = cm["aqk"], cm["a_strict"]
        v, uu, ww, v_new, dov = heads_of(v_ref), heads_of(u_ref), heads_of(w_ref), heads_of(vn_ref), heads_of(do_ref)
        tm = _Heads(tm_ref[h, 0] for h in hs)
        st = _Heads(st_ref[h, 0] for h in hs)
        dsn = _Heads(dstate[h] for h in hs)
        qd = qs * egc
        kd = k * ekd

        dv_new = _hdot_tn(aqk, dov) + _hdot(kd, dsn)
        daqk = _hwhere(lower, _hdot_nt(dov, v_new), 0.0)
        dqd = _hdot_nt(dov, st)
        dkd = _hdot_nt(v_new, dsn)
        ddl = _hsum(_hsum(dsn * st, 1), 0)
        dw = -_hdot_nt(dv_new, st)
        ds_new = dsn * dl + _hdot_tn(qd, dov) - _hdot_tn(ww, dv_new)

        dru = dv_new + _hdot_tn(tm, dv_new)
        drw = dw + _hdot_tn(tm, dw)
        dn = _hwhere(strict, -(_hdot_nt(dru, uu) + _hdot_nt(drw, ww)), 0.0)
        dag = dn * gam
        dkb = _hdot(dag, k) + drw * egc
        dk = _hdot_tn(dag, kb)
        dqg = daqk * gam
        dqs = _hdot(dqg, k) + dqd * egc
        dk = dk + _hdot_tn(dqg, qs) + dkb * beta_b + dkd * ekd
        pmat = dn * a_strict + daqk * aqk
        tkd = _hsum(dkd * kd, -1)
        dgc = (_hsum(pmat, -1) - _hmap(_dot_tn_exact_rhs, pmat, ones) + _hsum(drw * (kb * egc), -1)
               + _hsum(dqd * qd, -1) - tkd)
        last = _hsum(tkd, 0) + ddl * dl
        dgc = dgc + _hwhere(rowi == c - 1, last, 0.0)
        dg = _hmap(_dot_exact_lhs, mk["upper_f"], dgc)
        dbeta = _hsum(dru * v, -1) + _hsum(dkb * k, -1)
        dq = dqs * (DN_D ** -0.5)
        dv = dru * beta_b

        dbg = jnp.zeros((c, 128), F32)
        for h, sl in zip(hs, sls):
            dq_ref[:, sl] = dq.xs[h]
            dk_ref[:, sl] = dk.xs[h]
            dv_ref[:, sl] = dv.xs[h]
            dstate[h] = ds_new.xs[h]
            dbg = dbg + jnp.where(lane == h, dbeta.xs[h], 0.0) + jnp.where(lane == DN_HEADS + h, dg.xs[h], 0.0)
        dbg_ref[...] = dbg

    def part(p):
        return pl.BlockSpec((c, DN_W), lambda j: (n - 1 - j, p))

    vec = pl.BlockSpec((c, 128), lambda j: (n - 1 - j, 0))
    return pl.pallas_call(
        body, name="gdr_bwd", grid=(n,),
        in_specs=[part(0), part(1), part(2), vec, part(0), part(0), part(0),
                  pl.BlockSpec((DN_HEADS, 1, c, c), lambda j: (0, n - 1 - j, 0, 0)),
                  pl.BlockSpec((DN_HEADS, 1, DN_D, DN_D), lambda j: (0, n - 1 - j, 0, 0)), part(0)],
        out_specs=[part(0), part(0), part(0), vec],
        out_shape=[jax.ShapeDtypeStruct((s, DN_W), F32)] * 3 + [jax.ShapeDtypeStruct((s, 128), F32)],
        scratch_shapes=[pltpu.VMEM((DN_HEADS, DN_D, DN_D), F32)],
        compiler_params=_cparams("arbitrary"))(qkv, qkv, qkv, bg, u, w, vn, tmat, states, do)


def _gdr_out(o, proj, dnw):
    s = o.shape[0]

    def body(o_ref, z_ref, w_ref, y_ref, yt_ref):
        ov, zv, wv = o_ref[...], z_ref[...], w_ref[...]
        for h in range(DN_HEADS):
            sl = slice(h * DN_D, (h + 1) * DN_D)
            oh = ov[:, sl]
            r = lax.rsqrt(jnp.mean(oh * oh, axis=-1, keepdims=True) + NORM_EPS)
            y = (oh * r * wv) * _silu(zv[:, sl])
            y_ref[:, sl] = y.astype(BF16)
            yt_ref[sl, :] = y.T.astype(BF16)

    row = pl.BlockSpec((ROW_TILE, DN_W), lambda i: (i, 0))
    return pl.pallas_call(
        body, name="gdr_out", grid=(s // ROW_TILE,),
        in_specs=[row, pl.BlockSpec((ROW_TILE, DN_W), lambda i: (i, OFF_Z_A // DN_W)), pl.BlockSpec((1, DN_D), lambda i: (0, 0))],
        out_specs=[row, pl.BlockSpec((DN_W, ROW_TILE), lambda i: (0, i))],
        out_shape=[jax.ShapeDtypeStruct((s, DN_W), BF16), jax.ShapeDtypeStruct((DN_W, s), BF16)],
        compiler_params=_cparams("parallel"))(o, proj, dnw)


def _gdr_out_bwd(o, proj, dnw, dy):
    s = o.shape[0]

    def body(o_ref, z_ref, w_ref, dy_ref, do_ref, dz_ref, dw_ref):
        i = pl.program_id(0)
        ov, zv, wv, dyv = o_ref[...], z_ref[...], w_ref[...], dy_ref[...]
        acc = jnp.zeros((1, DN_D), F32)
        for h in range(DN_HEADS):
            sl = slice(h * DN_D, (h + 1) * DN_D)
            oh, zh, dh = ov[:, sl], zv[:, sl], dyv[:, sl]
            r = lax.rsqrt(jnp.mean(oh * oh, axis=-1, keepdims=True) + NORM_EPS)
            dn = dh * _silu(zh)
            dz_ref[:, sl] = (dh * (oh * r * wv) * _silu_grad(zh)).astype(BF16)
            acc = acc + jnp.sum(dn * oh * r, axis=0, keepdims=True)
            dnw_ = dn * wv
            do_ref[:, sl] = r * dnw_ - oh * (r * r * r) * jnp.mean(dnw_ * oh, axis=-1, keepdims=True)

        @pl.when(i == 0)
        def _():
            dw_ref[...] = acc

        @pl.when(i > 0)
        def _():
            dw_ref[...] += acc

    row = pl.BlockSpec((ROW_TILE, DN_W), lambda i: (i, 0))
    vec = pl.BlockSpec((1, DN_D), lambda i: (0, 0))
    return pl.pallas_call(
        body, name="gdr_out_bwd", grid=(s // ROW_TILE,),
        in_specs=[row, pl.BlockSpec((ROW_TILE, DN_W), lambda i: (i, OFF_Z_A // DN_W)), vec, row],
        out_specs=[row, row, vec],
        out_shape=[jax.ShapeDtypeStruct((s, DN_W), F32), jax.ShapeDtypeStruct((s, DN_W), BF16),
                   jax.ShapeDtypeStruct((1, DN_D), F32)],
        compiler_params=_cparams("arbitrary"))(o, proj, dnw, dy)


def _slope(group, head):
    idx = (group * DIL_HEADS + head + 1).astype(F32)
    return jnp.exp(jnp.full((1, 128), -8.0 * math.log(2.0) / (N_DIL * DIL_HEADS), F32) * idx)


def _att_scores(qb, k_cur, k_prev, slope_d, has_prev):
    iq = lax.broadcasted_iota(jnp.int32, (ATT_BLOCK, ATT_BLOCK), 0)
    jk = lax.broadcasted_iota(jnp.int32, (ATT_BLOCK, ATT_BLOCK), 1)
    dist_c = (iq - jk).astype(F32)
    s_cur = jnp.where(iq >= jk, _dot_nt(qb, k_cur) - slope_d * dist_c, NEG)
    s_prev = jnp.where(jnp.logical_and(jk >= iq, has_prev),
                       _dot_nt(qb, k_prev) - slope_d * (dist_c + float(ATT_BLOCK)), NEG)
    return s_cur, s_prev


def _att_fwd(proj, group):
    s = proj.shape[0]
    dil = DIL_GROUPS[group][1]
    assert DIL_GROUPS[group][0] // dil == ATT_BLOCK
    nb = s // dil // ATT_BLOCK
    assert nb * dil * ATT_BLOCK == s

    def body(q_ref, k_ref, v_ref, num_ref, den_ref, mx_ref):
        slope_d = _slope(group, pl.program_id(0)) * float(dil)

        def step(t, carry):
            r = lax.div(t, nb)
            j = lax.rem(t, nb)
            base = r + dil * ATT_BLOCK * j
            pbase = base - dil * ATT_BLOCK * jnp.minimum(j, 1)
            if dil == 1:
                base, pbase = pl.multiple_of(base, ATT_BLOCK), pl.multiple_of(pbase, ATT_BLOCK)
            cur = pl.ds(base, ATT_BLOCK, stride=dil)
            prv = pl.ds(pbase, ATT_BLOCK, stride=dil)
            qb = q_ref[cur, :] * (DIL_DH ** -0.5)
            s_cur, s_prev = _att_scores(qb, k_ref[cur, :], k_ref[prv, :], slope_d, j > 0)
            mx = jnp.maximum(jnp.max(s_cur, axis=-1, keepdims=True), jnp.max(s_prev, axis=-1, keepdims=True))
            p_cur = jnp.exp(s_cur - mx)
            p_prev = jnp.exp(s_prev - mx)
            den = jnp.sum(p_cur, axis=-1, keepdims=True) + jnp.sum(p_prev, axis=-1, keepdims=True)
            num_ref[cur, :] = _dot(p_cur, v_ref[cur, :]) + _dot(p_prev, v_ref[prv, :])
            den_ref[cur, :] = jnp.broadcast_to(den, (ATT_BLOCK, DIL_DH))
            mx_ref[cur, :] = jnp.broadcast_to(mx, (ATT_BLOCK, DIL_DH))
            return carry

        lax.fori_loop(0, dil * nb, step, 0)

    def col(off):
        return pl.BlockSpec((s, DIL_DH), lambda h: (0, off // DIL_DH + group * DIL_HEADS + h))

    out = pl.BlockSpec((s, DIL_DH), lambda h: (0, h))
    return pl.pallas_call(
        body, name=f"att_fwd{group}", grid=(DIL_HEADS,), in_specs=[col(OFF_Q_B), col(OFF_K_B), col(OFF_V_B)],
        out_specs=[out, out, out], out_shape=[jax.ShapeDtypeStruct((s, DIL_W), F32)] * 3,
        compiler_params=_cparams("parallel"))(proj, proj, proj)


def _att_bwd(proj, group, do, lse, delta):
    s = proj.shape[0]
    dil = DIL_GROUPS[group][1]
    nb = s // dil // ATT_BLOCK

    def body(q_ref, k_ref, v_ref, do_ref, lse_ref, dl_ref, dq_ref, dk_ref, dv_ref, dq_acc, dk_acc, dv_acc):
        slope_d = _slope(group, pl.program_id(0)) * float(dil)
        dk_acc[...] = jnp.zeros_like(dk_acc)
        dv_acc[...] = jnp.zeros_like(dv_acc)

        def step(t, carry):
            r = lax.div(t, nb)
            j = lax.rem(t, nb)
            base = r + dil * ATT_BLOCK * j
            pbase = base - dil * ATT_BLOCK * jnp.minimum(j, 1)
            if dil == 1:
                base, pbase = pl.multiple_of(base, ATT_BLOCK), pl.multiple_of(pbase, ATT_BLOCK)
            cur = pl.ds(base, ATT_BLOCK, stride=dil)
            prv = pl.ds(pbase, ATT_BLOCK, stride=dil)
            qb = q_ref[cur, :] * (DIL_DH ** -0.5)
            k_cur, k_prev, v_cur, v_prev = k_ref[cur, :], k_ref[prv, :], v_ref[cur, :], v_ref[prv, :]
            s_cur, s_prev = _att_scores(qb, k_cur, k_prev, slope_d, j > 0)
            lse_b, delta_b, dob = lse_ref[cur, :], dl_ref[cur, :], do_ref[cur, :]
            p_cur = jnp.exp(s_cur - lse_b)
            p_prev = jnp.exp(s_prev - lse_b)
            ds_cur = p_cur * (_dot_nt(dob, v_cur) - delta_b)
            ds_prev = p_prev * (_dot_nt(dob, v_prev) - delta_b)
            dq_acc[cur, :] = (_dot(ds_cur, k_cur) + _dot(ds_prev, k_prev)) * (DIL_DH ** -0.5)
            dk_acc[cur, :] += _dot_tn(ds_cur, qb)
            dv_acc[cur, :] += _dot_tn(p_cur, dob)

            @pl.when(j > 0)
            def _():
                dk_acc[prv, :] += _dot_tn(ds_prev, qb)
                dv_acc[prv, :] += _dot_tn(p_prev, dob)

            return carry

        lax.fori_loop(0, dil * nb, step, 0)
        dq_ref[...] = dq_acc[...].astype(BF16)
        dk_ref[...] = dk_acc[...].astype(BF16)
        dv_ref[...] = dv_acc[...].astype(BF16)

    def col(off):
        return pl.BlockSpec((s, DIL_DH), lambda h: (0, off // DIL_DH + group * DIL_HEADS + h))

    hd = pl.BlockSpec((s, DIL_DH), lambda h: (0, h))
    return pl.pallas_call(
        body, name=f"att_bwd{group}", grid=(DIL_HEADS,),
        in_specs=[col(OFF_Q_B), col(OFF_K_B), col(OFF_V_B), hd, hd, hd], out_specs=[hd, hd, hd],
        out_shape=[jax.ShapeDtypeStruct((s, DIL_W), BF16)] * 3,
        scratch_shapes=[pltpu.VMEM((s, DIL_DH), F32)] * 3,
        compiler_params=_cparams("parallel"))(proj, proj, proj, do, lse, delta)


def _att_merge(parts, proj):
    s = proj.shape[0]

    def body(n0, d0, m0, n1, d1, m1, n2, d2, m2, z_ref, ob_ref, o_ref, lse_ref, obt_ref):
        m = jnp.maximum(jnp.maximum(m0[...], m1[...]), m2[...])
        num = jnp.zeros_like(m)
        den = jnp.zeros_like(m)
        for nr, dr, mr in ((n0, d0, m0), (n1, d1, m1), (n2, d2, m2)):
            sc = jnp.exp(mr[...] - m)
            num = num + nr[...] * sc
            den = den + dr[...] * sc
        o = num / den
        o_ref[...] = o
        lse_ref[...] = m + jnp.log(den)
        ob = o * _silu(z_ref[...])
        ob_ref[...] = ob.astype(BF16)
        obt_ref[...] = ob.T.astype(BF16)

    row = pl.BlockSpec((ROW_TILE, DIL_W), lambda i: (i, 0))
    flat = [a for p in parts for a in p]
    return pl.pallas_call(
        body, name="att_merge", grid=(s // ROW_TILE,),
        in_specs=[row] * 9 + [pl.BlockSpec((ROW_TILE, DIL_W), lambda i: (i, OFF_Z_B // DIL_W))],
        out_specs=[row, row, row, pl.BlockSpec((DIL_W, ROW_TILE), lambda i: (0, i))],
        out_shape=[jax.ShapeDtypeStruct((s, DIL_W), BF16), jax.ShapeDtypeStruct((s, DIL_W), F32),
                   jax.ShapeDtypeStruct((s, DIL_W), F32), jax.ShapeDtypeStruct((DIL_W, s), BF16)],
        compiler_params=_cparams("parallel"))(*flat, proj)


def _att_merge_bwd(o, proj, dob):
    s = o.shape[0]

    def body(o_ref, z_ref, d_ref, do_ref, dl_ref, dz_ref):
        ov, zv, dv = o_ref[...], z_ref[...], d_ref[...]
        do = dv * _silu(zv)
        do_ref[...] = do
        dz_ref[...] = (dv * ov * _silu_grad(zv)).astype(BF16)
        for h in range(DIL_HEADS):
            sl = slice(h * DIL_DH, (h + 1) * DIL_DH)
            dl_ref[:, sl] = jnp.broadcast_to(jnp.sum(do[:, sl] * ov[:, sl], axis=-1, keepdims=True), (ROW_TILE, DIL_DH))

    row = pl.BlockSpec((ROW_TILE, DIL_W), lambda i: (i, 0))
    return pl.pallas_call(
        body, name="att_merge_bwd", grid=(s // ROW_TILE,),
        in_specs=[row, pl.BlockSpec((ROW_TILE, DIL_W), lambda i: (i, OFF_Z_B // DIL_W)), row],
        out_specs=[row, row, row],
        out_shape=[jax.ShapeDtypeStruct((s, DIL_W), F32), jax.ShapeDtypeStruct((s, DIL_W), F32),
                   jax.ShapeDtypeStruct((s, DIL_W), BF16)],
        compiler_params=_cparams("parallel"))(o, proj, dob)


def _merge(proj, ya, yb):
    s = proj.shape[0]

    def body(ga_ref, gb_ref, ya_ref, yb_ref, o_ref, ot_ref):
        m = _sigmoid(ga_ref[...]) * ya_ref[...] + _sigmoid(gb_ref[...]) * yb_ref[...]
        o_ref[...] = m.astype(BF16)
        ot_ref[...] = m.T.astype(BF16)

    row = pl.BlockSpec((ROW_TILE, D_MODEL), lambda i: (i, 0))
    return pl.pallas_call(
        body, name="merge", grid=(s // ROW_TILE,),
        in_specs=[pl.BlockSpec((ROW_TILE, D_MODEL), lambda i: (i, OFF_G_A // D_MODEL)),
                  pl.BlockSpec((ROW_TILE, D_MODEL), lambda i: (i, OFF_G_B // D_MODEL)), row, row],
        out_specs=[row, pl.BlockSpec((D_MODEL, ROW_TILE), lambda i: (0, i))],
        out_shape=[jax.ShapeDtypeStruct((s, D_MODEL), BF16), jax.ShapeDtypeStruct((D_MODEL, s), BF16)],
        compiler_params=_cparams("parallel"))(proj, proj, ya, yb)


def _merge_bwd(proj, ya, yb, dm):
    s = proj.shape[0]

    def body(ga_ref, gb_ref, ya_ref, yb_ref, dm_ref, dya_ref, dyb_ref, dga_ref, dgb_ref):
        dmv = dm_ref[...]
        sa, sb = _sigmoid(ga_ref[...]), _sigmoid(gb_ref[...])
        dya_ref[...] = (dmv * sa).astype(BF16)
        dyb_ref[...] = (dmv * sb).astype(BF16)
        dga_ref[...] = (dmv * ya_ref[...] * sa * (1.0 - sa)).astype(BF16)
        dgb_ref[...] = (dmv * yb_ref[...] * sb * (1.0 - sb)).astype(BF16)

    row = pl.BlockSpec((ROW_TILE, D_MODEL), lambda i: (i, 0))
    return pl.pallas_call(
        body, name="merge_bwd", grid=(s // ROW_TILE,),
        in_specs=[pl.BlockSpec((ROW_TILE, D_MODEL), lambda i: (i, OFF_G_A // D_MODEL)),
                  pl.BlockSpec((ROW_TILE, D_MODEL), lambda i: (i, OFF_G_B // D_MODEL)), row, row, row],
        out_specs=[row] * 4, out_shape=[jax.ShapeDtypeStruct((s, D_MODEL), BF16)] * 4,
        compiler_params=_cparams("parallel"))(proj, proj, ya, yb, dm)


def _final(x, t, fw, tgt):
    s, d = x.shape

    def body(x_ref, t_ref, w_ref, y_ref, dx_ref, dw_ref, l_ref):
        i = pl.program_id(0)
        x2 = x_ref[...] + t_ref[...]
        wv = w_ref[...]
        r = lax.rsqrt(jnp.mean(x2 * x2, axis=-1, keepdims=True) + NORM_EPS)
        e = x2 * r * wv - y_ref[...]
        lrow = jnp.mean(e * e, axis=-1, keepdims=True)
        lpart = jnp.broadcast_to(0.5 * jnp.sum(lrow, axis=0, keepdims=True), (1, 128))
        dy = e * (1.0 / d)
        dwp = jnp.sum(dy * x2 * r, axis=0, keepdims=True)
        dyw = dy * wv
        dx_ref[...] = r * dyw - x2 * (r * r * r) * jnp.mean(dyw * x2, axis=-1, keepdims=True)

        @pl.when(i == 0)
        def _():
            dw_ref[...] = dwp
            l_ref[...] = lpart

        @pl.when(i > 0)
        def _():
            dw_ref[...] += dwp
            l_ref[...] += lpart

    row = pl.BlockSpec((ROW_TILE, d), lambda i: (i, 0))
    vec = pl.BlockSpec((1, d), lambda i: (0, 0))
    return pl.pallas_call(
        body, name="final", grid=(s // ROW_TILE,), in_specs=[row, row, vec, row],
        out_specs=[row, vec, pl.BlockSpec((1, 128), lambda i: (0, 0))],
        out_shape=[jax.ShapeDtypeStruct((s, d), F32), jax.ShapeDtypeStruct((1, d), F32), jax.ShapeDtypeStruct((1, 128), F32)],
        compiler_params=_cparams("arbitrary"))(x, t, fw, tgt)


def _adamw(w, g, m, v, name):
    r, c = w.shape
    tr = r if r <= 128 else 128
    assert r % tr == 0

    def body(w_ref, g_ref, m_ref, v_ref, d_ref, nm_ref, nv_ref):
        gv = g_ref[...]
        mn = ADAM_B1 * m_ref[...] + (1.0 - ADAM_B1) * gv
        vn = ADAM_B2 * v_ref[...] + (1.0 - ADAM_B2) * (gv * gv)
        m_hat = mn / (1.0 - ADAM_B1 ** ADAM_STEP)
        v_hat = vn / (1.0 - ADAM_B2 ** ADAM_STEP)
        d_ref[...] = -ADAM_LR * (m_hat / (jnp.sqrt(v_hat) + ADAM_EPS) + ADAM_WD * w_ref[...])
        nm_ref[...] = mn
        nv_ref[...] = vn

    blk = pl.BlockSpec((tr, c), lambda i: (i, 0))
    return pl.pallas_call(
        body, name=name, grid=(r // tr,), in_specs=[blk] * 4, out_specs=[blk] * 3,
        out_shape=[jax.ShapeDtypeStruct((r, c), F32)] * 3, compiler_params=_cparams("parallel"))(w, g, m, v)


HBM_SPEC = pl.BlockSpec(memory_space=pl.ANY)


def _place():
    x, y, c = lax.axis_index("x"), lax.axis_index("y"), lax.axis_index("c")
    chips = [(1 - x, y), (x, 1 - y), (1 - x, 1 - y)]
    return x, y, c, chips


def _ag_weights(packs):
    na = len(packs)

    def body(*refs):
        p_refs, out_refs = refs[:na], refs[na:2 * na]
        send_sems, recv_sems = refs[2 * na:]
        x, y, c, chips = _place()
        me, sib, j = (x, y, c), (x, y, 1 - c), 2 * x + y

        def rc(k, src, dst, to):
            return pltpu.make_async_remote_copy(src_ref=src, dst_ref=dst, send_sem=send_sems.at[k],
                                                recv_sem=recv_sems.at[k], device_id=to, device_id_type=MESH)

        first = [rc(6 * a + k, p_refs[a].at[c], out_refs[a].at[j, c], (cx, cy, c))
                 for a in range(na) for k, (cx, cy) in enumerate(chips)]
        for cp in first:
            cp.start()
        passed = []
        for a in range(na):
            for k, (cx, cy) in enumerate(chips):
                land = out_refs[a].at[2 * cx + cy, c]
                rc(6 * a + k, p_refs[a].at[c], land, me).wait_recv()
                fwd = rc(6 * a + 3 + k, land, land, sib)
                fwd.start()
                passed.append(fwd)
        for a in range(na):
            for k, (cx, cy) in enumerate(chips):
                rc(6 * a + 3 + k, p_refs[a].at[c], out_refs[a].at[2 * cx + cy, 1 - c], me).wait_recv()
        for cp in first + passed:
            cp.wait_send()

    return pl.pallas_call(
        body, name="ag_weights",
        out_shape=[jax.ShapeDtypeStruct((N_CHIPS,) + p.shape, p.dtype) for p in packs],
        in_specs=[HBM_SPEC] * na, out_specs=[HBM_SPEC] * na,
        scratch_shapes=[pltpu.SemaphoreType.DMA((6 * na,)), pltpu.SemaphoreType.DMA((6 * na,))])(*packs)


def _rs_pair(gpacks):
    na = len(gpacks)
    n = N_CHIPS

    def body(*refs):
        g_refs, out_refs = refs[:na], refs[na:2 * na]
        send_sems, recv_sems = refs[2 * na:]
        x, y, c, _ = _place()
        sib = (x, y, 1 - c)
        cps = [pltpu.make_async_remote_copy(src_ref=g_refs[a].at[p, 1 - c], dst_ref=out_refs[a].at[p],
                                            send_sem=send_sems.at[n * a + p], recv_sem=recv_sems.at[n * a + p],
                                            device_id=sib, device_id_type=MESH)
               for a in range(na) for p in range(n)]
        for cp in cps:
            cp.start()
        for cp in cps:
            cp.wait_recv()
        for cp in cps:
            cp.wait_send()

    return pl.pallas_call(
        body, name="rs_pair",
        out_shape=[jax.ShapeDtypeStruct((n,) + g.shape[2:], g.dtype) for g in gpacks],
        in_specs=[HBM_SPEC] * na, out_specs=[HBM_SPEC] * na,
        scratch_shapes=[pltpu.SemaphoreType.DMA((n * na,)), pltpu.SemaphoreType.DMA((n * na,))])(*gpacks)


def _rs_chips(csums):
    na = len(csums)

    def body(*refs):
        s_refs, out_refs = refs[:na], refs[na:2 * na]
        send_sems, recv_sems = refs[2 * na:]
        x, y, c, chips = _place()
        j = 2 * x + y
        cps = [pltpu.make_async_remote_copy(src_ref=s_refs[a].at[2 * cx + cy], dst_ref=out_refs[a].at[j],
                                            send_sem=send_sems.at[3 * a + k], recv_sem=recv_sems.at[3 * a + k],
                                            device_id=(cx, cy, c), device_id_type=MESH)
               for a in range(na) for k, (cx, cy) in enumerate(chips)]
        for cp in cps:
            cp.start()
        for a in range(na):
            for k, (cx, cy) in enumerate(chips):
                pltpu.make_async_remote_copy(src_ref=s_refs[a].at[j], dst_ref=out_refs[a].at[2 * cx + cy],
                                             send_sem=send_sems.at[3 * a + k], recv_sem=recv_sems.at[3 * a + k],
                                             device_id=(x, y, c), device_id_type=MESH).wait_recv()
        for cp in cps:
            cp.wait_send()

    return pl.pallas_call(
        body, name="rs_chips", out_shape=[jax.ShapeDtypeStruct(s.shape, s.dtype) for s in csums],
        in_specs=[HBM_SPEC] * na, out_specs=[HBM_SPEC] * na,
        scratch_shapes=[pltpu.SemaphoreType.DMA((3 * na,)), pltpu.SemaphoreType.DMA((3 * na,))])(*csums)


SWAP_CHUNKS = 4


def _pair_swap(halves):
    na = len(halves)

    def body(*refs):
        h_refs, out_refs = refs[:na], refs[na:2 * na]
        send_sems, recv_sems = refs[2 * na:]
        x, y, c, _ = _place()
        cps = []
        for a in range(na):
            rows = h_refs[a].shape[0] // SWAP_CHUNKS
            assert rows * SWAP_CHUNKS == h_refs[a].shape[0]
            for q in range(SWAP_CHUNKS):
                k = SWAP_CHUNKS * a + q
                cps.append(pltpu.make_async_remote_copy(
                    src_ref=h_refs[a].at[pl.ds(q * rows, rows)], dst_ref=out_refs[a].at[pl.ds(q * rows, rows)],
                    send_sem=send_sems.at[k], recv_sem=recv_sems.at[k], device_id=(x, y, 1 - c), device_id_type=MESH))
        for cp in cps:
            cp.start()
        for cp in cps:
            cp.wait_recv()
        for cp in cps:
            cp.wait_send()

    return pl.pallas_call(
        body, name="pair_swap", out_shape=[jax.ShapeDtypeStruct(h.shape, h.dtype) for h in halves],
        in_specs=[HBM_SPEC] * na, out_specs=[HBM_SPEC] * na,
        scratch_shapes=[pltpu.SemaphoreType.DMA((SWAP_CHUNKS * na,)), pltpu.SemaphoreType.DMA((SWAP_CHUNKS * na,))])(*halves)


def _ag_small(v):
    m_per, n = v.shape

    def body(x_ref, out_ref, send_sems, recv_sems, local_sem):
        x, y, c, chips = _place()
        me, sibling = (x, y, c), (x, y, 1 - c)

        def rows(px, py, pc):
            return out_ref.at[pl.ds((4 * px + 2 * py + pc) * m_per, m_per), :]

        def copy(k, block, to, src=None):
            return pltpu.make_async_remote_copy(
                src_ref=rows(*block) if src is None else src, dst_ref=rows(*block), send_sem=send_sems.at[k],
                recv_sem=recv_sems.at[k], device_id=to, device_id_type=MESH)

        mine = pltpu.make_async_copy(x_ref, rows(*me), local_sem)
        mine.start()
        first = [copy(0, me, sibling, src=x_ref)]
        first += [copy(1 + k, me, (*chip, c), src=x_ref) for k, chip in enumerate(chips)]
        for cp in first:
            cp.start()
        passed = [copy(4 + k, (*chip, c), sibling) for k, chip in enumerate(chips)]
        for k, chip in enumerate(chips):
            copy(1 + k, (*chip, c), me).wait_recv()
            passed[k].start()
        copy(0, sibling, me).wait_recv()
        for k, chip in enumerate(chips):
            copy(4 + k, (*chip, 1 - c), me).wait_recv()
        for cp in first + passed:
            cp.wait_send()
        mine.wait()

    return pl.pallas_call(
        body, name="ag_small", out_shape=jax.ShapeDtypeStruct((8 * m_per, n), v.dtype),
        in_specs=[pl.BlockSpec(memory_space=pltpu.VMEM)], out_specs=pl.BlockSpec(memory_space=pltpu.VMEM),
        scratch_shapes=[pltpu.SemaphoreType.DMA((7,)), pltpu.SemaphoreType.DMA((7,)), pltpu.SemaphoreType.DMA])(v)


def _sum_blocks(a, nblk, name):
    rows, wd = a.shape
    r = rows // nblk
    tr = min(r, ROW_TILE)
    assert r % tr == 0

    def body(*refs):
        acc = refs[0][...].astype(F32)
        for ref in refs[1:nblk]:
            acc = acc + ref[...].astype(F32)
        refs[nblk][...] = acc

    nt = r // tr
    return pl.pallas_call(
        body, name=name, grid=(nt,),
        in_specs=[pl.BlockSpec((tr, wd), functools.partial(lambda i, b: (b * nt + i, 0), b=b)) for b in range(nblk)],
        out_specs=pl.BlockSpec((tr, wd), lambda i: (i, 0)),
        out_shape=jax.ShapeDtypeStruct((r, wd), F32), compiler_params=_cparams("parallel"))(*([a] * nblk))


def _sum_chips(by_src, csum, j, name):
    n, rh, wd = by_src.shape
    tr = min(rh, ROW_TILE)
    assert rh % tr == 0

    def body(j_ref, *refs):
        own = refs[n][0].astype(F32)
        acc = None
        for k in range(n):
            term = jnp.where(j_ref[0] == k, own, refs[k][0].astype(F32))
            acc = term if acc is None else acc + term
        refs[n + 1][...] = acc

    def other(k):
        return pl.BlockSpec((1, tr, wd), lambda i, jr: (jnp.where(jr[0] == k, (k + 1) % n, k), i, 0))

    grid_spec = pltpu.PrefetchScalarGridSpec(
        num_scalar_prefetch=1, grid=(rh // tr,),
        in_specs=[other(k) for k in range(n)] + [pl.BlockSpec((1, tr, wd), lambda i, jr: (jr[0], i, 0))],
        out_specs=pl.BlockSpec((tr, wd), lambda i, jr: (i, 0)))
    return pl.pallas_call(
        body, name=name, grid_spec=grid_spec, out_shape=jax.ShapeDtypeStruct((rh, wd), F32),
        compiler_params=_cparams("parallel"))(jnp.reshape(j, (1,)).astype(jnp.int32), *([by_src] * n), csum)


def _add_halves(gpack, other, c, name):
    n, _, rh, wd = gpack.shape
    tr = ROW_TILE
    assert rh % tr == 0

    def body(c_ref, g_ref, o_ref, out_ref):
        out_ref[0] = (g_ref[0, 0] + o_ref[0]).astype(BF16)

    grid_spec = pltpu.PrefetchScalarGridSpec(
        num_scalar_prefetch=1, grid=(n, rh // tr),
        in_specs=[pl.BlockSpec((1, 1, tr, wd), lambda p, i, cr: (p, cr[0], i, 0)),
                  pl.BlockSpec((1, tr, wd), lambda p, i, cr: (p, i, 0))],
        out_specs=pl.BlockSpec((1, tr, wd), lambda p, i, cr: (p, i, 0)))
    return pl.pallas_call(
        body, name=name, grid_spec=grid_spec, out_shape=jax.ShapeDtypeStruct((n, rh, wd), BF16),
        compiler_params=_cparams("parallel", "parallel"))(jnp.reshape(c, (1,)).astype(jnp.int32), gpack, other)


PACK_W = 1024
ROWS_O_DN = DN_W // N_CHIPS
ROWS_O_DIL = DIL_W * (D_MODEL // N_CHIPS) // PACK_W
ROWS_OUT = D_MODEL // N_CHIPS
ROWS_CONV = 4 * (3 * DN_W // N_CHIPS) // PACK_W
R1 = ROWS_O_DN
R2 = R1 + ROWS_O_DIL
R3 = R2 + ROWS_OUT
R4 = R3 + ROWS_CONV
R5 = R4 + ROWS_CONV
PACK_ROWS = 1024
HALF_ROWS = PACK_ROWS // 2
HALF_D = D_MODEL // 2


def _to_ref_layout(wp):
    return jnp.concatenate([wp[:, :REF_OFF_BA], wp[:, OFF_BA:OFF_BA + 2 * DN_HEADS], wp[:, REF_OFF_BA:OFF_BA]], axis=1)


def _from_ref_layout(w):
    pad = jnp.zeros((w.shape[0], PW - PROJ_W), w.dtype)
    return jnp.concatenate([w[:, :REF_OFF_BA], w[:, REF_OFF_BA + 2 * DN_HEADS:], w[:, REF_OFF_BA:REF_OFF_BA + 2 * DN_HEADS], pad],
                           axis=1)


def _local_step(x, tgt, norm_w, wp, conv_full, a_log, dt_bias, dn_norm_w, w_o_dn, w_o_dil, w_out, final_norm_w):
    s = x.shape[0]
    n = s // DN_CHUNK
    h, h_t = _rms_in(x, norm_w)
    proj = _matmul(h, wp, F32, 512, 1280, 1024, "proj")
    c_pre, qkv = _conv_fwd(proj, conv_full)
    gate_par = jnp.zeros((8, 128), F32).at[0, 8:16].set(a_log[0]).at[1, 8:16].set(dt_bias[0])
    bg = _gates_fwd(proj, gate_par)
    o_a, u, w, vn, tmat, states = _gdr_fwd(qkv, bg)
    oa2, oa2_t = _gdr_out(o_a, proj, dn_norm_w)
    ya = _matmul(oa2, w_o_dn, F32, 512, 1024, 1024, "ya")
    parts = [_att_fwd(proj, g) for g in range(N_DIL)]
    ob, o_att, lse, ob_t = _att_merge(parts, proj)
    yb = _matmul(ob, w_o_dil, F32, 512, 1024, 512, "yb")
    mg, mg_t = _merge(proj, ya, yb)
    t = _matmul(mg, w_out, F32, 512, 1024, 1024, "t_out")
    dx2, dfw, lpart = _final(x, t, final_norm_w, tgt)

    dmg = _matmul(dx2, w_out, F32, 512, 1024, 1024, "d_merged", nt=True)
    dw_out = _matmul(mg_t, dx2, F32, 1024, 1024, 1024, "dw_out")
    dya, dyb, dga, dgb = _merge_bwd(proj, ya, yb, dmg)
    doa2 = _matmul(dya, w_o_dn, F32, 512, 1024, 1024, "d_oa2", nt=True)
    dw_o_dn = _matmul(oa2_t, dya, F32, 1024, 1024, 1024, "dw_o_dn")
    dob = _matmul(dyb, w_o_dil, F32, 512, 512, 1024, "d_ob", nt=True)
    dw_o_dil = _matmul(ob_t, dyb, F32, 512, 1024, 1024, "dw_o_dil")
    do_a, dz_a, ddnw = _gdr_out_bwd(o_a, proj, dn_norm_w, doa2)
    dq_a, dk_a, dv_a, dbg = _gdr_bwd(qkv, bg, u, w, vn, tmat, states, do_a)
    dba, dpar = _gates_bwd(proj, gate_par, dbg)
    dc = _conv_bwd_act(c_pre, dq_a, dk_a, dv_a)
    du_a, dconv = _conv_bwd(proj, dc, conv_full)
    do_att, delta, dz_b = _att_merge_bwd(o_att, proj, dob)
    dqkv_b = [_att_bwd(proj, g, do_att, lse, delta) for g in range(N_DIL)]
    dproj = jnp.concatenate(
        [du_a, dz_a] + [dqkv_b[g][i] for i in range(3) for g in range(N_DIL)]
        + [dz_b, dga, dgb, dba, jnp.zeros((s, PW - OFF_BA - 128), BF16)], axis=1)
    dh = _matmul(dproj, wp, F32, 512, 1024, 2304, "d_h", nt=True)
    dwp = _matmul(h_t, dproj, F32, 1024, 1280, 1024, "dw_in")
    grad_x, dnw = _rms_in_bwd(x, norm_w, dh, dx2)
    small = jnp.zeros((8, PACK_W), F32)
    small = small.at[0].set(dnw[0]).at[1].set(dfw[0]).at[2, :DN_D].set(ddnw[0])
    small = small.at[3, :DN_HEADS].set(dpar[0, 8:16]).at[3, DN_HEADS:2 * DN_HEADS].set(dpar[1, 8:16])
    small = small.at[4, 0].set(lpart[0, 0])
    return grad_x, _to_ref_layout(dwp), dconv, dw_o_dn, dw_o_dil, dw_out, small


def kernel(x, norm_w, w_in, conv_w, a_log, dt_bias, dn_norm_w, w_o_dn, w_o_dil, w_out, final_norm_w, loss_target, m_norm_w, m_w_in, m_conv_w, m_a_log, m_dt_bias, m_dn_norm_w, m_w_o_dn, m_w_o_dil, m_w_out, m_final_norm_w, v_norm_w, v_w_in, v_conv_w, v_a_log, v_dt_bias, v_dn_norm_w, v_w_o_dn, v_w_o_dil, v_w_out, v_final_norm_w):
    c = lax.axis_index("c")
    j = 2 * lax.axis_index("x") + lax.axis_index("y")
    qw = D_MODEL // N_CHIPS

    cw = conv_w[0].reshape(ROWS_CONV, PACK_W)
    cw_hi = cw.astype(BF16)
    cw_lo = (cw - cw_hi.astype(F32)).astype(BF16)
    pack = jnp.concatenate(
        [w_o_dn[0].astype(BF16), w_o_dil[0].astype(BF16).reshape(ROWS_O_DIL, PACK_W), w_out[0].astype(BF16), cw_hi, cw_lo,
         jnp.zeros((PACK_ROWS - R5, PACK_W), BF16)], axis=0).reshape(2, HALF_ROWS, PACK_W)
    own_in = w_in[0].astype(BF16)
    all_in, allw = _ag_weights([own_in.reshape(2, HALF_D, SHARD_W), pack])
    chips = range(N_CHIPS)
    all_in = [jnp.where(j == k, own_in, all_in[k].reshape(D_MODEL, SHARD_W)) for k in chips]
    allw = [jnp.where(j == k, pack, allw[k]).reshape(PACK_ROWS, PACK_W) for k in chips]
    w_in_full = jnp.concatenate(all_in, axis=1)
    w_o_dn_full = jnp.concatenate([allw[k][:R1] for k in chips], axis=0)
    w_o_dil_full = jnp.concatenate([allw[k][R1:R2].reshape(DIL_W, qw) for k in chips], axis=1)
    w_out_full = jnp.concatenate([allw[k][R2:R3] for k in chips], axis=0)
    conv_full = jnp.concatenate(
        [(allw[k][R3:R4].astype(F32) + allw[k][R4:R5].astype(F32)).reshape(4, 3 * DN_W // N_CHIPS) for k in chips], axis=1)
    wp = _from_ref_layout(w_in_full)

    grad_x, dw_in, dconv, dw_o_dn, dw_o_dil, dw_out, small = _local_step(
        x[0], loss_target[0], norm_w, wp, conv_full, a_log, dt_bias, dn_norm_w, w_o_dn_full, w_o_dil_full, w_out_full,
        final_norm_w.reshape(1, D_MODEL))

    cq = 3 * DN_W // N_CHIPS
    g_in = jnp.stack([dw_in[:, k * SHARD_W:(k + 1) * SHARD_W] for k in chips]).reshape(N_CHIPS, 2, HALF_D, SHARD_W)
    gpack = jnp.stack([
        jnp.concatenate(
            [dw_o_dn[k * qw:(k + 1) * qw], dw_o_dil[:, k * qw:(k + 1) * qw].reshape(ROWS_O_DIL, PACK_W),
             dw_out[k * qw:(k + 1) * qw], dconv[:, k * cq:(k + 1) * cq].reshape(ROWS_CONV, PACK_W),
             jnp.zeros((PACK_ROWS - R4, PACK_W), F32)], axis=0)
        for k in chips]).reshape(N_CHIPS, 2, HALF_ROWS, PACK_W)
    sib_in, sib_pack = _rs_pair([g_in, gpack])
    csum_in = _add_halves(g_in, sib_in, c, "add_halves_in")
    csum_pack = _add_halves(gpack, sib_pack, c, "add_halves_pack")
    src_in, src_pack = _rs_chips([csum_in, csum_pack])
    half_in = _sum_chips(src_in, csum_in, j, "sum_chips_in")
    half_pack = _sum_chips(src_pack, csum_pack, j, "sum_chips_pack")
    sib_half_in, sib_half_pack = _pair_swap([half_in, half_pack])

    def both_halves(mine, theirs):
        return jnp.where(c == 0, jnp.concatenate([mine, theirs], axis=0), jnp.concatenate([theirs, mine], axis=0))

    g_w_in = both_halves(half_in, sib_half_in)
    g = both_halves(half_pack, sib_half_pack)
    g_w_o_dn = g[:R1]
    g_w_o_dil = g[R1:R2].reshape(DIL_W, qw)
    g_w_out = g[R2:R3]
    g_conv = g[R3:R4].reshape(4, cq)

    gs = _sum_blocks(_ag_small(small), 8, "sum_small")
    loss = gs[4, 0]
    w_small = jnp.zeros((8, PACK_W), F32)

    def pack_small(nw, fw, dnw_, al, db):
        t = w_small.at[0].set(nw[0]).at[1].set(fw).at[2, :DN_D].set(dnw_[0])
        return t.at[3, :DN_HEADS].set(al[0]).at[3, DN_HEADS:2 * DN_HEADS].set(db[0])

    sm = _adamw(pack_small(norm_w, final_norm_w, dn_norm_w, a_log, dt_bias), gs,
                pack_small(m_norm_w, m_final_norm_w, m_dn_norm_w, m_a_log, m_dt_bias),
                pack_small(v_norm_w, v_final_norm_w, v_dn_norm_w, v_a_log, v_dt_bias), "adamw_small")

    def unpack_small(t):
        return dict(norm_w=t[0:1], final_norm_w=t[1], dn_norm_w=t[2:3, :DN_D], a_log=t[3:4, :DN_HEADS],
                    dt_bias=t[3:4, DN_HEADS:2 * DN_HEADS])

    res = {"grad": unpack_small(gs)}
    for kind, arr in zip(("delta", "new_m", "new_v"), sm):
        res[kind] = unpack_small(arr)
    big = dict(w_in=(w_in, g_w_in, m_w_in, v_w_in), conv_w=(conv_w, g_conv, m_conv_w, v_conv_w),
               w_o_dn=(w_o_dn, g_w_o_dn, m_w_o_dn, v_w_o_dn), w_o_dil=(w_o_dil, g_w_o_dil, m_w_o_dil, v_w_o_dil),
               w_out=(w_out, g_w_out, m_w_out, v_w_out))
    for name, (wt, gt, mt, vt) in big.items():
        d, nm, nv = _adamw(wt[0], gt, mt[0], vt[0], "adamw_" + name)
        res["grad"][name] = gt[None]
        res["delta"][name], res["new_m"][name], res["new_v"][name] = d[None], nm[None], nv[None]
    order = ["norm_w", "w_in", "conv_w", "a_log", "dt_bias", "dn_norm_w", "w_o_dn", "w_o_dil", "w_out", "final_norm_w"]
    outs = [loss, grad_x[None]]
    for kind in ("grad", "delta", "new_m", "new_v"):
        outs += [res[kind][nm] for nm in order]
    return tuple(outs)
```

```python
import functools
import math

import jax
import jax.numpy as jnp
from jax import lax
from jax.experimental import pallas as pl
from jax.experimental.pallas import tpu as pltpu

F32 = jnp.float32
BF16 = jnp.bfloat16
MESH = pl.DeviceIdType.MESH

D_MODEL = 1024
DN_HEADS = 8
DN_D = 128
DN_CHUNK = 64
DN_W = DN_HEADS * DN_D
DIL_GROUPS = ((128, 1), (512, 4), (2048, 16))
N_DIL = len(DIL_GROUPS)
DIL_HEADS = 4
DIL_DH = 128
DIL_W = DIL_HEADS * DIL_DH
ATT_BLOCK = 128
NORM_EPS = 1e-6
PROJ_W = 11280
N_CHIPS = 4
SHARD_W = PROJ_W // N_CHIPS

OFF_QKV_A = 0
OFF_Z_A = 3072
OFF_Q_B = 4096
OFF_K_B = 5632
OFF_V_B = 7168
OFF_Z_B = 8704
OFF_G_A = 9216
OFF_G_B = 10240
OFF_BA = 11264
PW = 11520
REF_OFF_BA = 4096

ADAM_LR = 0.001
ADAM_B1 = 0.9
ADAM_B2 = 0.999
ADAM_EPS = 1e-08
ADAM_WD = 0.01
ADAM_STEP = 10

ROW_TILE = 256
NEG = -1e30


def _dot(a, b):
    return jnp.dot(a.astype(BF16), b.astype(BF16), preferred_element_type=F32)


def _dot_nt(a, b):
    return lax.dot_general(a.astype(BF16), b.astype(BF16), (((1,), (1,)), ((), ())), preferred_element_type=F32)


def _dot_tn(a, b):
    return lax.dot_general(a.astype(BF16), b.astype(BF16), (((0,), (0,)), ((), ())), preferred_element_type=F32)


def _split(a):
    hi = a.astype(BF16)
    lo = (a - hi.astype(F32)).astype(BF16)
    return hi, lo


def _dot_exact_lhs(c, a):
    hi, lo = _split(a)
    cb = c.astype(BF16)
    return jnp.dot(cb, hi, preferred_element_type=F32) + jnp.dot(cb, lo, preferred_element_type=F32)


def _dot_exact_rhs(a, c):
    hi, lo = _split(a)
    cb = c.astype(BF16)
    return jnp.dot(hi, cb, preferred_element_type=F32) + jnp.dot(lo, cb, preferred_element_type=F32)


def _dot_tn_exact_rhs(a, c):
    hi, lo = _split(a)
    cb = c.astype(BF16)
    dn = (((0,), (0,)), ((), ()))
    return (lax.dot_general(hi, cb, dn, preferred_element_type=F32)
            + lax.dot_general(lo, cb, dn, preferred_element_type=F32))


def _sigmoid(x):
    return 1.0 / (1.0 + jnp.exp(-x))


def _silu(x):
    return x * _sigmoid(x)


def _silu_grad(x):
    s = _sigmoid(x)
    return s * (1.0 + x * (1.0 - s))


def _softplus(x):
    return jnp.maximum(x, 0.0) + jnp.log(1.0 + jnp.exp(-jnp.abs(x)))


def _cparams(*sem):
    return pltpu.CompilerParams(dimension_semantics=sem)


def _matmul(a, b, out_dtype, tm, tn, tk, name, nt=False, transpose_out=False):
    m, kdim = a.shape
    n = b.shape[0] if nt else b.shape[1]
    tm, tn, tk = min(tm, m), min(tn, n), min(tk, kdim)
    assert m % tm == 0 and n % tn == 0 and kdim % tk == 0, (name, a.shape, b.shape, tm, tn, tk)
    nk = kdim // tk
    dot = _dot_nt if nt else _dot
    b_spec = (pl.BlockSpec((tn, tk), lambda i, j, k: (j, k)) if nt else pl.BlockSpec((tk, tn), lambda i, j, k: (k, j)))

    def emit(o_ref, acc):
        o_ref[...] = (acc.T if transpose_out else acc).astype(o_ref.dtype)

    if nk == 1:
        def body(a_ref, b_ref, o_ref):
            emit(o_ref, dot(a_ref[...], b_ref[...]))
        scratch = []
    else:
        def body(a_ref, b_ref, o_ref, acc_ref):
            k = pl.program_id(2)
            p = dot(a_ref[...], b_ref[...])

            @pl.when(k == 0)
            def _():
                acc_ref[...] = p

            @pl.when(k > 0)
            def _():
                acc_ref[...] += p

            @pl.when(k == nk - 1)
            def _():
                emit(o_ref, acc_ref[...])
        scratch = [pltpu.VMEM((tm, tn), F32)]

    if transpose_out:
        out_spec, out_shape = pl.BlockSpec((tn, tm), lambda i, j, k: (j, i)), (n, m)
    else:
        out_spec, out_shape = pl.BlockSpec((tm, tn), lambda i, j, k: (i, j)), (m, n)
    return pl.pallas_call(
        body, name=name, grid=(m // tm, n // tn, nk), in_specs=[pl.BlockSpec((tm, tk), lambda i, j, k: (i, k)), b_spec],
        out_specs=out_spec, out_shape=jax.ShapeDtypeStruct(out_shape, out_dtype), scratch_shapes=scratch,
        compiler_params=_cparams("parallel", "parallel", "arbitrary"))(a, b)


def _rms_in(x, nw):
    s, d = x.shape

    def body(x_ref, w_ref, h_ref, ht_ref):
        xv = x_ref[...]
        r = lax.rsqrt(jnp.mean(xv * xv, axis=-1, keepdims=True) + NORM_EPS)
        h = xv * r * w_ref[...]
        h_ref[...] = h.astype(BF16)
        ht_ref[...] = h.T.astype(BF16)

    return pl.pallas_call(
        body, name="rms_in", grid=(s // ROW_TILE,),
        in_specs=[pl.BlockSpec((ROW_TILE, d), lambda i: (i, 0)), pl.BlockSpec((1, d), lambda i: (0, 0))],
        out_specs=[pl.BlockSpec((ROW_TILE, d), lambda i: (i, 0)), pl.BlockSpec((d, ROW_TILE), lambda i: (0, i))],
        out_shape=[jax.ShapeDtypeStruct((s, d), BF16), jax.ShapeDtypeStruct((d, s), BF16)],
        compiler_params=_cparams("parallel"))(x, nw)


def _rms_in_bwd(x, nw, dh, dx2):
    s, d = x.shape

    def body(x_ref, w_ref, dh_ref, dx2_ref, dx_ref, dw_ref):
        i = pl.program_id(0)
        xv = x_ref[...]
        r = lax.rsqrt(jnp.mean(xv * xv, axis=-1, keepdims=True) + NORM_EPS)
        dhv = dh_ref[...]
        dyw = dhv * w_ref[...]
        dx_ref[...] = dx2_ref[...] + r * dyw - xv * (r * r * r) * jnp.mean(dyw * xv, axis=-1, keepdims=True)
        part = jnp.sum(dhv * xv * r, axis=0, keepdims=True)

        @pl.when(i == 0)
        def _():
            dw_ref[...] = part

        @pl.when(i > 0)
        def _():
            dw_ref[...] += part

    row = pl.BlockSpec((ROW_TILE, d), lambda i: (i, 0))
    vec = pl.BlockSpec((1, d), lambda i: (0, 0))
    return pl.pallas_call(
        body, name="rms_in_bwd", grid=(s // ROW_TILE,), in_specs=[row, vec, row, row], out_specs=[row, vec],
        out_shape=[jax.ShapeDtypeStruct((s, d), F32), jax.ShapeDtypeStruct((1, d), F32)],
        compiler_params=_cparams("arbitrary"))(x, nw, dh, dx2)


def _shift_down(cur, prev8, k):
    rc = pltpu.roll(cur, k, 0)
    rp = pltpu.roll(prev8, k, 0)
    row = lax.broadcasted_iota(jnp.int32, prev8.shape, 0)
    top = jnp.where(row < k, rp, rc[:8])
    return jnp.concatenate([top, rc[8:]], axis=0)


def _shift_up(cur, next8, k):
    t = cur.shape[0]
    rc = pltpu.roll(cur, t - k, 0)
    rn = pltpu.roll(next8, 8 - k, 0)
    row = lax.broadcasted_iota(jnp.int32, next8.shape, 0)
    bot = jnp.where(row >= 8 - k, rn, rc[t - 8:])
    return jnp.concatenate([rc[:t - 8], bot], axis=0)


def _conv_fwd(proj, conv_w):
    s = proj.shape[0]
    t8 = ROW_TILE // 8

    def body(u_ref, up_ref, w_ref, c_ref, y_ref):
        i = pl.program_id(0)
        part = pl.program_id(1)
        cur = u_ref[...]
        prev8 = jnp.where(i > 0, up_ref[...], 0.0)
        w = w_ref[...]
        c = cur * w[3:4, :]
        for k in (1, 2, 3):
            c = c + _shift_down(cur, prev8, k) * w[3 - k:4 - k, :]
        c_ref[...] = c
        a = _silu(c)
        for h in range(DN_HEADS):
            ah = a[:, h * DN_D:(h + 1) * DN_D]
            r = lax.rsqrt(jnp.sum(ah * ah, axis=-1, keepdims=True) + NORM_EPS)
            y_ref[:, h * DN_D:(h + 1) * DN_D] = jnp.where(part < 2, ah * r, ah)

    return pl.pallas_call(
        body, name="conv_fwd", grid=(s // ROW_TILE, 3),
        in_specs=[pl.BlockSpec((ROW_TILE, DN_W), lambda i, p: (i, p)),
                  pl.BlockSpec((8, DN_W), lambda i, p: (jnp.maximum(i * t8 - 1, 0), p)),
                  pl.BlockSpec((4, DN_W), lambda i, p: (0, p))],
        out_specs=[pl.BlockSpec((ROW_TILE, DN_W), lambda i, p: (i, p))] * 2,
        out_shape=[jax.ShapeDtypeStruct((s, 3 * DN_W), F32)] * 2,
        compiler_params=_cparams("parallel", "parallel"))(proj, proj, conv_w)


def _conv_bwd_act(c, dq, dk, dv):
    s = c.shape[0]

    def body(c_ref, dq_ref, dk_ref, dv_ref, dc_ref):
        for part, d_ref in enumerate((dq_ref, dk_ref, dv_ref)):
            for h in range(DN_HEADS):
                sl = slice(part * DN_W + h * DN_D, part * DN_W + (h + 1) * DN_D)
                ch = c_ref[:, sl]
                dyh = d_ref[:, h * DN_D:(h + 1) * DN_D]
                if part < 2:
                    ah = _silu(ch)
                    r = lax.rsqrt(jnp.sum(ah * ah, axis=-1, keepdims=True) + NORM_EPS)
                    dyh = r * dyh - ah * (r * r * r) * jnp.sum(dyh * ah, axis=-1, keepdims=True)
                dc_ref[:, sl] = dyh * _silu_grad(ch)

    wide = pl.BlockSpec((ROW_TILE, 3 * DN_W), lambda i: (i, 0))
    row = pl.BlockSpec((ROW_TILE, DN_W), lambda i: (i, 0))
    return pl.pallas_call(
        body, name="conv_bwd_act", grid=(s // ROW_TILE,), in_specs=[wide, row, row, row], out_specs=wide,
        out_shape=jax.ShapeDtypeStruct((s, 3 * DN_W), F32), compiler_params=_cparams("parallel"))(c, dq, dk, dv)


def _conv_bwd(proj, dc, conv_w):
    s = proj.shape[0]
    t8 = ROW_TILE // 8
    nrow = s // ROW_TILE
    last8 = s // 8 - 1

    def body(u_ref, up_ref, dc_ref, dcn_ref, w_ref, du_ref, dw_ref):
        i = pl.program_id(1)
        cur = u_ref[...]
        prev8 = jnp.where(i > 0, up_ref[...], 0.0)
        dcv = dc_ref[...]
        next8 = jnp.where(i < nrow - 1, dcn_ref[...], 0.0)
        w = w_ref[...]
        du = dcv * w[3:4, :]
        for k in (1, 2, 3):
            du = du + _shift_up(dcv, next8, k) * w[3 - k:4 - k, :]
        du_ref[...] = du.astype(BF16)

        @pl.when(i == 0)
        def _():
            dw_ref[...] = jnp.zeros_like(dw_ref)

        dw_ref[3:4, :] += jnp.sum(cur * dcv, axis=0, keepdims=True)
        for k in (1, 2, 3):
            dw_ref[3 - k:4 - k, :] += jnp.sum(_shift_down(cur, prev8, k) * dcv, axis=0, keepdims=True)

    blk = pl.BlockSpec((ROW_TILE, DN_W), lambda p, i: (i, p))
    return pl.pallas_call(
        body, name="conv_bwd", grid=(3, nrow),
        in_specs=[blk, pl.BlockSpec((8, DN_W), lambda p, i: (jnp.maximum(i * t8 - 1, 0), p)),
                  blk, pl.BlockSpec((8, DN_W), lambda p, i: (jnp.minimum((i + 1) * t8, last8), p)),
                  pl.BlockSpec((4, DN_W), lambda p, i: (0, p))],
        out_specs=[blk, pl.BlockSpec((4, DN_W), lambda p, i: (0, p))],
        out_shape=[jax.ShapeDtypeStruct((s, 3 * DN_W), BF16), jax.ShapeDtypeStruct((4, 3 * DN_W), F32)],
        compiler_params=_cparams("parallel", "arbitrary"))(proj, proj, dc, dc, conv_w)


def _gates_fwd(proj, gate_par):
    s = proj.shape[0]

    def body(ba_ref, par_ref, o_ref):
        v = ba_ref[...]
        lane = lax.broadcasted_iota(jnp.int32, v.shape, 1)
        beta = _sigmoid(v)
        g = -jnp.exp(par_ref[0:1, :]) * _softplus(v + par_ref[1:2, :])
        o_ref[...] = jnp.where(lane < DN_HEADS, beta, jnp.where(lane < 2 * DN_HEADS, g, 0.0))

    return pl.pallas_call(
        body, name="gates_fwd", grid=(s // ROW_TILE,),
        in_specs=[pl.BlockSpec((ROW_TILE, 128), lambda i: (i, OFF_BA // 128)), pl.BlockSpec((8, 128), lambda i: (0, 0))],
        out_specs=pl.BlockSpec((ROW_TILE, 128), lambda i: (i, 0)),
        out_shape=jax.ShapeDtypeStruct((s, 128), F32), compiler_params=_cparams("parallel"))(proj, gate_par)


def _gates_bwd(proj, gate_par, dbg):
    s = proj.shape[0]

    def body(ba_ref, par_ref, d_ref, o_ref, dpar_ref):
        i = pl.program_id(0)
        v = ba_ref[...]
        dv = d_ref[...]
        lane = lax.broadcasted_iota(jnp.int32, v.shape, 1)
        beta = _sigmoid(v)
        nega = -jnp.exp(par_ref[0:1, :])
        xs = v + par_ref[1:2, :]
        dsp = dv * nega * _sigmoid(xs)
        dal = dv * nega * _softplus(xs)
        is_b = lane < DN_HEADS
        is_g = jnp.logical_and(lane >= DN_HEADS, lane < 2 * DN_HEADS)
        o_ref[...] = jnp.where(is_b, dv * beta * (1.0 - beta), jnp.where(is_g, dsp, 0.0)).astype(BF16)
        r0 = jnp.sum(jnp.where(is_g, dal, 0.0), axis=0, keepdims=True)
        r1 = jnp.sum(jnp.where(is_g, dsp, 0.0), axis=0, keepdims=True)

        @pl.when(i == 0)
        def _():
            dpar_ref[...] = jnp.zeros_like(dpar_ref)

        dpar_ref[0:1, :] += r0
        dpar_ref[1:2, :] += r1

    return pl.pallas_call(
        body, name="gates_bwd", grid=(s // ROW_TILE,),
        in_specs=[pl.BlockSpec((ROW_TILE, 128), lambda i: (i, OFF_BA // 128)), pl.BlockSpec((8, 128), lambda i: (0, 0)),
                  pl.BlockSpec((ROW_TILE, 128), lambda i: (i, 0))],
        out_specs=[pl.BlockSpec((ROW_TILE, 128), lambda i: (i, 0)), pl.BlockSpec((8, 128), lambda i: (0, 0))],
        out_shape=[jax.ShapeDtypeStruct((s, 128), BF16), jax.ShapeDtypeStruct((8, 128), F32)],
        compiler_params=_cparams("arbitrary"))(proj, gate_par, dbg)


def _chunk_masks():
    c = DN_CHUNK
    ii = lax.broadcasted_iota(jnp.int32, (c, c), 0)
    jj = lax.broadcasted_iota(jnp.int32, (c, c), 1)
    return dict(ii=ii, jj=jj, lower=(ii >= jj), strict=(ii > jj), eye=(ii == jj),
                lower_f=(ii >= jj).astype(BF16), upper_f=(ii <= jj).astype(BF16), ones8=jnp.ones((8, c), BF16))


class _Heads:
    def __init__(self, xs):
        self.xs = list(xs)

    def _bin(self, o, f):
        if isinstance(o, _Heads):
            return _Heads([f(a, b) for a, b in zip(self.xs, o.xs)])
        return _Heads([f(a, o) for a in self.xs])

    def __add__(self, o):
        return self._bin(o, lambda a, b: a + b)

    def __sub__(self, o):
        return self._bin(o, lambda a, b: a - b)

    def __mul__(self, o):
        return self._bin(o, lambda a, b: a * b)

    __radd__ = __add__
    __rmul__ = __mul__

    def __neg__(self):
        return _Heads([-a for a in self.xs])

    def __getitem__(self, i):
        return _Heads([a[i] for a in self.xs])


def _hmap(f, *args):
    n = next(len(a.xs) for a in args if isinstance(a, _Heads))
    return _Heads([f(*[(a.xs[h] if isinstance(a, _Heads) else a) for a in args]) for h in range(n)])


def _hdot(a, b):
    return _hmap(_dot, a, b)


def _hdot_nt(a, b):
    return _hmap(_dot_nt, a, b)


def _hdot_tn(a, b):
    return _hmap(_dot_tn, a, b)


def _hsum(a, axis):
    return _hmap(lambda t: jnp.sum(t, axis=axis, keepdims=True), a)


def _hwhere(c, a, b):
    return _hmap(jnp.where, c, a, b)


def _chunk_common(mk, q, k, beta_col, g_col):
    c = DN_CHUNK
    lower, strict = mk["lower"], mk["strict"]
    qs = q * (DN_D ** -0.5)
    beta_b = _hmap(lambda t: jnp.broadcast_to(t, (c, DN_D)), beta_col)
    g_b = _hmap(lambda t: jnp.broadcast_to(t, (c, DN_D)), g_col)
    gc_b = _hmap(_dot_exact_lhs, mk["lower_f"], g_b)
    gc_sq = gc_b[:, :c]
    gc_r = _hmap(_dot_exact_lhs, mk["ones8"], _hwhere(mk["eye"], gc_sq, 0.0))[0:1, :]
    gam = _hwhere(lower, _hmap(lambda t: jnp.exp(jnp.minimum(t, 0.0)), gc_sq - gc_r), 0.0)
    egc = _hmap(jnp.exp, gc_b)
    gl = gc_b[c - 1:c, :]
    ekd = _hmap(jnp.exp, gl - gc_b)
    dl = _hmap(jnp.exp, gl)
    kb = k * beta_b
    a_strict = _hwhere(strict, _hdot_nt(kb, k) * gam, 0.0)
    aqk = _hwhere(lower, _hdot_nt(qs, k) * gam, 0.0)
    return dict(k=k, qs=qs, beta_b=beta_b, gc_b=gc_b, gam=gam, egc=egc, ekd=ekd, dl=dl, kb=kb, a_strict=a_strict, aqk=aqk)


def _unit_lower_inverse_minus_eye(n_strict, ii, jj):
    same = lax.shift_right_logical(ii, 4) == lax.shift_right_logical(jj, 4)
    dmat = _hwhere(same, n_strict, 0.0)
    omat = n_strict - dmat
    d2 = _hdot(dmat, dmat)
    d4 = _hdot(d2, d2)
    d8 = _hdot(d4, d4)
    x1 = d2 - dmat - _hdot(dmat, d2)
    x2 = x1 + d4 + _hdot(x1, d4)
    x3 = x2 + d8 + _hdot(x2, d8)
    n1 = omat + _hdot(x3, omat)
    n2 = _hdot(n1, n1)
    y = n2 - n1 - _hdot(n1, n2)
    return y + x3 + _hdot(y, x3)


def _gdr_fwd(qkv, bg):
    s = qkv.shape[0]
    c = DN_CHUNK
    n = s // c

    def body(q_ref, k_ref, v_ref, bg_ref, o_ref, u_ref, w_ref, vn_ref, tm_ref, st_ref, state):
        @pl.when(pl.program_id(0) == 0)
        def _():
            state[...] = jnp.zeros_like(state)

        mk = _chunk_masks()
        bg = bg_ref[...]
        hs = range(DN_HEADS)
        sls = [slice(h * DN_D, (h + 1) * DN_D) for h in hs]
        cm = _chunk_common(mk, _Heads(q_ref[:, sl] for sl in sls), _Heads(k_ref[:, sl] for sl in sls),
                           _Heads(bg[:, h:h + 1] for h in hs), _Heads(bg[:, DN_HEADS + h:DN_HEADS + h + 1] for h in hs))
        tm = _unit_lower_inverse_minus_eye(cm["a_strict"], mk["ii"], mk["jj"])
        rhs_u = _Heads(v_ref[:, sl] for sl in sls) * cm["beta_b"]
        rhs_w = cm["kb"] * cm["egc"]
        u = rhs_u + _hdot(tm, rhs_u)
        w = rhs_w + _hdot(tm, rhs_w)
        st = _Heads(state[h] for h in hs)
        v_new = u - _hdot(w, st)
        o = _hdot(cm["qs"] * cm["egc"], st) + _hdot(cm["aqk"], v_new)
        st_new = st * cm["dl"] + _hdot_tn(cm["k"] * cm["ekd"], v_new)
        for h, sl in zip(hs, sls):
            o_ref[:, sl] = o.xs[h]
            u_ref[:, sl] = u.xs[h]
            w_ref[:, sl] = w.xs[h]
            vn_ref[:, sl] = v_new.xs[h]
            tm_ref[h, 0] = tm.xs[h]
            st_ref[h, 0] = st.xs[h]
            state[h] = st_new.xs[h]

    def part(p):
        return pl.BlockSpec((c, DN_W), lambda j: (j, p))

    return pl.pallas_call(
        body, name="gdr_fwd", grid=(n,),
        in_specs=[part(0), part(1), part(2), pl.BlockSpec((c, 128), lambda j: (j, 0))],
        out_specs=[part(0)] * 4 + [pl.BlockSpec((DN_HEADS, 1, c, c), lambda j: (0, j, 0, 0)),
                                   pl.BlockSpec((DN_HEADS, 1, DN_D, DN_D), lambda j: (0, j, 0, 0))],
        out_shape=[jax.ShapeDtypeStruct((s, DN_W), F32)] * 4
        + [jax.ShapeDtypeStruct((DN_HEADS, n, c, c), F32), jax.ShapeDtypeStruct((DN_HEADS, n, DN_D, DN_D), F32)],
        scratch_shapes=[pltpu.VMEM((DN_HEADS, DN_D, DN_D), F32)],
        compiler_params=_cparams("arbitrary"))(qkv, qkv, qkv, bg)


def _gdr_bwd(qkv, bg, u, w, vn, tmat, states, do):
    s = qkv.shape[0]
    c = DN_CHUNK
    n = s // c

    def body(q_ref, k_ref, v_ref, bg_ref, u_ref, w_ref, vn_ref, tm_ref, st_ref, do_ref,
             dq_ref, dk_ref, dv_ref, dbg_ref, dstate):
        @pl.when(pl.program_id(0) == 0)
        def _():
            dstate[...] = jnp.zeros_like(dstate)

        mk = _chunk_masks()
        lower, strict = mk["lower"], mk["strict"]
        bg = bg_ref[...]
        ones = jnp.ones((c, DN_D), BF16)
        rowi = lax.broadcasted_iota(jnp.int32, (c, DN_D), 0)
        lane = lax.broadcasted_iota(jnp.int32, (c, 128), 1)
        hs = range(DN_HEADS)
        sls = [slice(h * DN_D, (h + 1) * DN_D) for h in hs]

        def heads_of(ref):
            return _Heads(ref[:, sl] for sl in sls)

        cm = _chunk_common(mk, heads_of(q_ref), heads_of(k_ref),
                           _Heads(bg[:, h:h + 1] for h in hs), _Heads(bg[:, DN_HEADS + h:DN_HEADS + h + 1] for h in hs))
        k, qs, beta_b = cm["k"], cm["qs"], cm["beta_b"]
        gam, egc, ekd, dl, kb = cm["gam"], cm["egc"], cm["ekd"], cm["dl"], cm["kb"]
        aqk, a_strict = cm["aqk"], cm["a_strict"]
        v, uu, ww, v_new, dov = heads_of(v_ref), heads_of(u_ref), heads_of(w_ref), heads_of(vn_ref), heads_of(do_ref)
        tm = _Heads(tm_ref[h, 0] for h in hs)
        st = _Heads(st_ref[h, 0] for h in hs)
        dsn = _Heads(dstate[h] for h in hs)
        qd = qs * egc
        kd = k * ekd

        dv_new = _hdot_tn(aqk, dov) + _hdot(kd, dsn)
        daqk = _hwhere(lower, _hdot_nt(dov, v_new), 0.0)
        dqd = _hdot_nt(dov, st)
        dkd = _hdot_nt(v_new, dsn)
        ddl = _hsum(_hsum(dsn * st, 1), 0)
        dw = -_hdot_nt(dv_new, st)
        ds_new = dsn * dl + _hdot_tn(qd, dov) - _hdot_tn(ww, dv_new)

        dru = dv_new + _hdot_tn(tm, dv_new)
        drw = dw + _hdot_tn(tm, dw)
        dn = _hwhere(strict, -(_hdot_nt(dru, uu) + _hdot_nt(drw, ww)), 0.0)
        dag = dn * gam
        dkb = _hdot(dag, k) + drw * egc
        dk = _hdot_tn(dag, kb)
        dqg = daqk * gam
        dqs = _hdot(dqg, k) + dqd * egc
        dk = dk + _hdot_tn(dqg, qs) + dkb * beta_b + dkd * ekd
        pmat = dn * a_strict + daqk * aqk
        tkd = _hsum(dkd * kd, -1)
        dgc = (_hsum(pmat, -1) - _hmap(_dot_tn_exact_rhs, pmat, ones) + _hsum(drw * (kb * egc), -1)
               + _hsum(dqd * qd, -1) - tkd)
        last = _hsum(tkd, 0) + ddl * dl
        dgc = dgc + _hwhere(rowi == c - 1, last, 0.0)
        dg = _hmap(_dot_exact_lhs, mk["upper_f"], dgc)
        dbeta = _hsum(dru * v, -1) + _hsum(dkb * k, -1)
        dq = dqs * (DN_D ** -0.5)
        dv = dru * beta_b

        dbg = jnp.zeros((c, 128), F32)
        for h, sl in zip(hs, sls):
            dq_ref[:, sl] = dq.xs[h]
            dk_ref[:, sl] = dk.xs[h]
            dv_ref[:, sl] = dv.xs[h]
            dstate[h] = ds_new.xs[h]
            dbg = dbg + jnp.where(lane == h, dbeta.xs[h], 0.0) + jnp.where(lane == DN_HEADS + h, dg.xs[h], 0.0)
        dbg_ref[...] = dbg

    def part(p):
        return pl.BlockSpec((c, DN_W), lambda j: (n - 1 - j, p))

    vec = pl.BlockSpec((c, 128), lambda j: (n - 1 - j, 0))
    return pl.pallas_call(
        body, name="gdr_bwd", grid=(n,),
        in_specs=[part(0), part(1), part(2), vec, part(0), part(0), part(0),
                  pl.BlockSpec((DN_HEADS, 1, c, c), lambda j: (0, n - 1 - j, 0, 0)),
                  pl.BlockSpec((DN_HEADS, 1, DN_D, DN_D), lambda j: (0, n - 1 - j, 0, 0)), part(0)],
        out_specs=[part(0), part(0), part(0), vec],
        out_shape=[jax.ShapeDtypeStruct((s, DN_W), F32)] * 3 + [jax.ShapeDtypeStruct((s, 128), F32)],
        scratch_shapes=[pltpu.VMEM((DN_HEADS, DN_D, DN_D), F32)],
        compiler_params=_cparams("arbitrary"))(qkv, qkv, qkv, bg, u, w, vn, tmat, states, do)


def _gdr_out(o, proj, dnw):
    s = o.shape[0]

    def body(o_ref, z_ref, w_ref, y_ref, yt_ref):
        ov, zv, wv = o_ref[...], z_ref[...], w_ref[...]
        for h in range(DN_HEADS):
            sl = slice(h * DN_D, (h + 1) * DN_D)
            oh = ov[:, sl]
            r = lax.rsqrt(jnp.mean(oh * oh, axis=-1, keepdims=True) + NORM_EPS)
            y = (oh * r * wv) * _silu(zv[:, sl])
            y_ref[:, sl] = y.astype(BF16)
            yt_ref[sl, :] = y.T.astype(BF16)

    row = pl.BlockSpec((ROW_TILE, DN_W), lambda i: (i, 0))
    return pl.pallas_call(
        body, name="gdr_out", grid=(s // ROW_TILE,),
        in_specs=[row, pl.BlockSpec((ROW_TILE, DN_W), lambda i: (i, OFF_Z_A // DN_W)), pl.BlockSpec((1, DN_D), lambda i: (0, 0))],
        out_specs=[row, pl.BlockSpec((DN_W, ROW_TILE), lambda i: (0, i))],
        out_shape=[jax.ShapeDtypeStruct((s, DN_W), BF16), jax.ShapeDtypeStruct((DN_W, s), BF16)],
        compiler_params=_cparams("parallel"))(o, proj, dnw)


def _gdr_out_bwd(o, proj, dnw, dy):
    s = o.shape[0]

    def body(o_ref, z_ref, w_ref, dy_ref, do_ref, dz_ref, dw_ref):
        i = pl.program_id(0)
        ov, zv, wv, dyv = o_ref[...], z_ref[...], w_ref[...], dy_ref[...]
        acc = jnp.zeros((1, DN_D), F32)
        for h in range(DN_HEADS):
            sl = slice(h * DN_D, (h + 1) * DN_D)
            oh, zh, dh = ov[:, sl], zv[:, sl], dyv[:, sl]
            r = lax.rsqrt(jnp.mean(oh * oh, axis=-1, keepdims=True) + NORM_EPS)
            dn = dh * _silu(zh)
            dz_ref[:, sl] = (dh * (oh * r * wv) * _silu_grad(zh)).astype(BF16)
            acc = acc + jnp.sum(dn * oh * r, axis=0, keepdims=True)
            dnw_ = dn * wv
            do_ref[:, sl] = r * dnw_ - oh * (r * r * r) * jnp.mean(dnw_ * oh, axis=-1, keepdims=True)

        @pl.when(i == 0)
        def _():
            dw_ref[...] = acc

        @pl.when(i > 0)
        def _():
            dw_ref[...] += acc

    row = pl.BlockSpec((ROW_TILE, DN_W), lambda i: (i, 0))
    vec = pl.BlockSpec((1, DN_D), lambda i: (0, 0))
    return pl.pallas_call(
        body, name="gdr_out_bwd", grid=(s // ROW_TILE,),
        in_specs=[row, pl.BlockSpec((ROW_TILE, DN_W), lambda i: (i, OFF_Z_A // DN_W)), vec, row],
        out_specs=[row, row, vec],
        out_shape=[jax.ShapeDtypeStruct((s, DN_W), F32), jax.ShapeDtypeStruct((s, DN_W), BF16),
                   jax.ShapeDtypeStruct((1, DN_D), F32)],
        compiler_params=_cparams("arbitrary"))(o, proj, dnw, dy)


def _slope(group, head):
    idx = (group * DIL_HEADS + head + 1).astype(F32)
    return jnp.exp(jnp.full((1, 128), -8.0 * math.log(2.0) / (N_DIL * DIL_HEADS), F32) * idx)


def _att_scores(qb, k_cur, k_prev, slope_d, has_prev):
    iq = lax.broadcasted_iota(jnp.int32, (ATT_BLOCK, ATT_BLOCK), 0)
    jk = lax.broadcasted_iota(jnp.int32, (ATT_BLOCK, ATT_BLOCK), 1)
    dist_c = (iq - jk).astype(F32)
    s_cur = jnp.where(iq >= jk, _dot_nt(qb, k_cur) - slope_d * dist_c, NEG)
    s_prev = jnp.where(jnp.logical_and(jk >= iq, has_prev),
                       _dot_nt(qb, k_prev) - slope_d * (dist_c + float(ATT_BLOCK)), NEG)
    return s_cur, s_prev


def _att_fwd(proj, group):
    s = proj.shape[0]
    dil = DIL_GROUPS[group][1]
    assert DIL_GROUPS[group][0] // dil == ATT_BLOCK
    nb = s // dil // ATT_BLOCK
    assert nb * dil * ATT_BLOCK == s

    def body(q_ref, k_ref, v_ref, num_ref, den_ref, mx_ref):
        slope_d = _slope(group, pl.program_id(0)) * float(dil)

        def step(t, carry):
            r = lax.div(t, nb)
            j = lax.rem(t, nb)
            base = r + dil * ATT_BLOCK * j
            pbase = base - dil * ATT_BLOCK * jnp.minimum(j, 1)
            if dil == 1:
                base, pbase = pl.multiple_of(base, ATT_BLOCK), pl.multiple_of(pbase, ATT_BLOCK)
            cur = pl.ds(base, ATT_BLOCK, stride=dil)
            prv = pl.ds(pbase, ATT_BLOCK, stride=dil)
            qb = q_ref[cur, :] * (DIL_DH ** -0.5)
            s_cur, s_prev = _att_scores(qb, k_ref[cur, :], k_ref[prv, :], slope_d, j > 0)
            mx = jnp.maximum(jnp.max(s_cur, axis=-1, keepdims=True), jnp.max(s_prev, axis=-1, keepdims=True))
            p_cur = jnp.exp(s_cur - mx)
            p_prev = jnp.exp(s_prev - mx)
            den = jnp.sum(p_cur, axis=-1, keepdims=True) + jnp.sum(p_prev, axis=-1, keepdims=True)
            num_ref[cur, :] = _dot(p_cur, v_ref[cur, :]) + _dot(p_prev, v_ref[prv, :])
            den_ref[cur, :] = jnp.broadcast_to(den, (ATT_BLOCK, DIL_DH))
            mx_ref[cur, :] = jnp.broadcast_to(mx, (ATT_BLOCK, DIL_DH))
            return carry

        lax.fori_loop(0, dil * nb, step, 0)

    def col(off):
        return pl.BlockSpec((s, DIL_DH), lambda h: (0, off // DIL_DH + group * DIL_HEADS + h))

    out = pl.BlockSpec((s, DIL_DH), lambda h: (0, h))
    return pl.pallas_call(
        body, name=f"att_fwd{group}", grid=(DIL_HEADS,), in_specs=[col(OFF_Q_B), col(OFF_K_B), col(OFF_V_B)],
        out_specs=[out, out, out], out_shape=[jax.ShapeDtypeStruct((s, DIL_W), F32)] * 3,
        compiler_params=_cparams("parallel"))(proj, proj, proj)


def _att_bwd(proj, group, do, lse, delta):
    s = proj.shape[0]
    dil = DIL_GROUPS[group][1]
    nb = s // dil // ATT_BLOCK

    def body(q_ref, k_ref, v_ref, do_ref, lse_ref, dl_ref, dq_ref, dk_ref, dv_ref, dq_acc, dk_acc, dv_acc):
        slope_d = _slope(group, pl.program_id(0)) * float(dil)
        dk_acc[...] = jnp.zeros_like(dk_acc)
        dv_acc[...] = jnp.zeros_like(dv_acc)

        def step(t, carry):
            r = lax.div(t, nb)
            j = lax.rem(t, nb)
            base = r + dil * ATT_BLOCK * j
            pbase = base - dil * ATT_BLOCK * jnp.minimum(j, 1)
            if dil == 1:
                base, pbase = pl.multiple_of(base, ATT_BLOCK), pl.multiple_of(pbase, ATT_BLOCK)
            cur = pl.ds(base, ATT_BLOCK, stride=dil)
            prv = pl.ds(pbase, ATT_BLOCK, stride=dil)
            qb = q_ref[cur, :] * (DIL_DH ** -0.5)
            k_cur, k_prev, v_cur, v_prev = k_ref[cur, :], k_ref[prv, :], v_ref[cur, :], v_ref[prv, :]
            s_cur, s_prev = _att_scores(qb, k_cur, k_prev, slope_d, j > 0)
            lse_b, delta_b, dob = lse_ref[cur, :], dl_ref[cur, :], do_ref[cur, :]
            p_cur = jnp.exp(s_cur - lse_b)
            p_prev = jnp.exp(s_prev - lse_b)
            ds_cur = p_cur * (_dot_nt(dob, v_cur) - delta_b)
            ds_prev = p_prev * (_dot_nt(dob, v_prev) - delta_b)
            dq_acc[cur, :] = (_dot(ds_cur, k_cur) + _dot(ds_prev, k_prev)) * (DIL_DH ** -0.5)
            dk_acc[cur, :] += _dot_tn(ds_cur, qb)
            dv_acc[cur, :] += _dot_tn(p_cur, dob)

            @pl.when(j > 0)
            def _():
                dk_acc[prv, :] += _dot_tn(ds_prev, qb)
                dv_acc[prv, :] += _dot_tn(p_prev, dob)

            return carry

        lax.fori_loop(0, dil * nb, step, 0)
        dq_ref[...] = dq_acc[...].astype(BF16)
        dk_ref[...] = dk_acc[...].astype(BF16)
        dv_ref[...] = dv_acc[...].astype(BF16)

    def col(off):
        return pl.BlockSpec((s, DIL_DH), lambda h: (0, off // DIL_DH + group * DIL_HEADS + h))

    hd = pl.BlockSpec((s, DIL_DH), lambda h: (0, h))
    return pl.pallas_call(
        body, name=f"att_bwd{group}", grid=(DIL_HEADS,),
        in_specs=[col(OFF_Q_B), col(OFF_K_B), col(OFF_V_B), hd, hd, hd], out_specs=[hd, hd, hd],
        out_shape=[jax.ShapeDtypeStruct((s, DIL_W), BF16)] * 3,
        scratch_shapes=[pltpu.VMEM((s, DIL_DH), F32)] * 3,
        compiler_params=_cparams("parallel"))(proj, proj, proj, do, lse, delta)


def _att_merge(parts, proj):
    s = proj.shape[0]

    def body(n0, d0, m0, n1, d1, m1, n2, d2, m2, z_ref, ob_ref, o_ref, lse_ref, obt_ref):
        m = jnp.maximum(jnp.maximum(m0[...], m1[...]), m2[...])
        num = jnp.zeros_like(m)
        den = jnp.zeros_like(m)
        for nr, dr, mr in ((n0, d0, m0), (n1, d1, m1), (n2, d2, m2)):
            sc = jnp.exp(mr[...] - m)
            num = num + nr[...] * sc
            den = den + dr[...] * sc
        o = num / den
        o_ref[...] = o
        lse_ref[...] = m + jnp.log(den)
        ob = o * _silu(z_ref[...])
        ob_ref[...] = ob.astype(BF16)
        obt_ref[...] = ob.T.astype(BF16)

    row = pl.BlockSpec((ROW_TILE, DIL_W), lambda i: (i, 0))
    flat = [a for p in parts for a in p]
    return pl.pallas_call(
        body, name="att_merge", grid=(s // ROW_TILE,),
        in_specs=[row] * 9 + [pl.BlockSpec((ROW_TILE, DIL_W), lambda i: (i, OFF_Z_B // DIL_W))],
        out_specs=[row, row, row, pl.BlockSpec((DIL_W, ROW_TILE), lambda i: (0, i))],
        out_shape=[jax.ShapeDtypeStruct((s, DIL_W), BF16), jax.ShapeDtypeStruct((s, DIL_W), F32),
                   jax.ShapeDtypeStruct((s, DIL_W), F32), jax.ShapeDtypeStruct((DIL_W, s), BF16)],
        compiler_params=_cparams("parallel"))(*flat, proj)


def _att_merge_bwd(o, proj, dob):
    s = o.shape[0]

    def body(o_ref, z_ref, d_ref, do_ref, dl_ref, dz_ref):
        ov, zv, dv = o_ref[...], z_ref[...], d_ref[...]
        do = dv * _silu(zv)
        do_ref[...] = do
        dz_ref[...] = (dv * ov * _silu_grad(zv)).astype(BF16)
        for h in range(DIL_HEADS):
            sl = slice(h * DIL_DH, (h + 1) * DIL_DH)
            dl_ref[:, sl] = jnp.broadcast_to(jnp.sum(do[:, sl] * ov[:, sl], axis=-1, keepdims=True), (ROW_TILE, DIL_DH))

    row = pl.BlockSpec((ROW_TILE, DIL_W), lambda i: (i, 0))
    return pl.pallas_call(
        body, name="att_merge_bwd", grid=(s // ROW_TILE,),
        in_specs=[row, pl.BlockSpec((ROW_TILE, DIL_W), lambda i: (i, OFF_Z_B // DIL_W)), row],
        out_specs=[row, row, row],
        out_shape=[jax.ShapeDtypeStruct((s, DIL_W), F32), jax.ShapeDtypeStruct((s, DIL_W), F32),
                   jax.ShapeDtypeStruct((s, DIL_W), BF16)],
        compiler_params=_cparams("parallel"))(o, proj, dob)


def _merge(proj, ya, yb):
    s = proj.shape[0]

    def body(ga_ref, gb_ref, ya_ref, yb_ref, o_ref, ot_ref):
        m = _sigmoid(ga_ref[...]) * ya_ref[...] + _sigmoid(gb_ref[...]) * yb_ref[...]
        o_ref[...] = m.astype(BF16)
        ot_ref[...] = m.T.astype(BF16)

    row = pl.BlockSpec((ROW_TILE, D_MODEL), lambda i: (i, 0))
    return pl.pallas_call(
        body, name="merge", grid=(s // ROW_TILE,),
        in_specs=[pl.BlockSpec((ROW_TILE, D_MODEL), lambda i: (i, OFF_G_A // D_MODEL)),
                  pl.BlockSpec((ROW_TILE, D_MODEL), lambda i: (i, OFF_G_B // D_MODEL)), row, row],
        out_specs=[row, pl.BlockSpec((D_MODEL, ROW_TILE), lambda i: (0, i))],
        out_shape=[jax.ShapeDtypeStruct((s, D_MODEL), BF16), jax.ShapeDtypeStruct((D_MODEL, s), BF16)],
        compiler_params=_cparams("parallel"))(proj, proj, ya, yb)


def _merge_bwd(proj, ya, yb, dm):
    s = proj.shape[0]

    def body(ga_ref, gb_ref, ya_ref, yb_ref, dm_ref, dya_ref, dyb_ref, dga_ref, dgb_ref):
        dmv = dm_ref[...]
        sa, sb = _sigmoid(ga_ref[...]), _sigmoid(gb_ref[...])
        dya_ref[...] = (dmv * sa).astype(BF16)
        dyb_ref[...] = (dmv * sb).astype(BF16)
        dga_ref[...] = (dmv * ya_ref[...] * sa * (1.0 - sa)).astype(BF16)
        dgb_ref[...] = (dmv * yb_ref[...] * sb * (1.0 - sb)).astype(BF16)

    row = pl.BlockSpec((ROW_TILE, D_MODEL), lambda i: (i, 0))
    return pl.pallas_call(
        body, name="merge_bwd", grid=(s // ROW_TILE,),
        in_specs=[pl.BlockSpec((ROW_TILE, D_MODEL), lambda i: (i, OFF_G_A // D_MODEL)),
                  pl.BlockSpec((ROW_TILE, D_MODEL), lambda i: (i, OFF_G_B // D_MODEL)), row, row, row],
        out_specs=[row] * 4, out_shape=[jax.ShapeDtypeStruct((s, D_MODEL), BF16)] * 4,
        compiler_params=_cparams("parallel"))(proj, proj, ya, yb, dm)


def _final(x, t, fw, tgt):
    s, d = x.shape

    def body(x_ref, t_ref, w_ref, y_ref, dx_ref, dw_ref, l_ref):
        i = pl.program_id(0)
        x2 = x_ref[...] + t_ref[...]
        wv = w_ref[...]
        r = lax.rsqrt(jnp.mean(x2 * x2, axis=-1, keepdims=True) + NORM_EPS)
        e = x2 * r * wv - y_ref[...]
        lrow = jnp.mean(e * e, axis=-1, keepdims=True)
        lpart = jnp.broadcast_to(0.5 * jnp.sum(lrow, axis=0, keepdims=True), (1, 128))
        dy = e * (1.0 / d)
        dwp = jnp.sum(dy * x2 * r, axis=0, keepdims=True)
        dyw = dy * wv
        dx_ref[...] = r * dyw - x2 * (r * r * r) * jnp.mean(dyw * x2, axis=-1, keepdims=True)

        @pl.when(i == 0)
        def _():
            dw_ref[...] = dwp
            l_ref[...] = lpart

        @pl.when(i > 0)
        def _():
            dw_ref[...] += dwp
            l_ref[...] += lpart

    row = pl.BlockSpec((ROW_TILE, d), lambda i: (i, 0))
    vec = pl.BlockSpec((1, d), lambda i: (0, 0))
    return pl.pallas_call(
        body, name="final", grid=(s // ROW_TILE,), in_specs=[row, row, vec, row],
        out_specs=[row, vec, pl.BlockSpec((1, 128), lambda i: (0, 0))],
        out_shape=[jax.ShapeDtypeStruct((s, d), F32), jax.ShapeDtypeStruct((1, d), F32), jax.ShapeDtypeStruct((1, 128), F32)],
        compiler_params=_cparams("arbitrary"))(x, t, fw, tgt)


def _adamw(w, g, m, v, name):
    r, c = w.shape
    cap = max(8, (1 << 18) // c)
    tr = r if r <= 8 else max(t for t in range(8, min(r, cap) + 1, 8) if r % t == 0)

    def body(w_ref, g_ref, m_ref, v_ref, d_ref, nm_ref, nv_ref):
        gv = g_ref[...]
        mn = ADAM_B1 * m_ref[...] + (1.0 - ADAM_B1) * gv
        vn = ADAM_B2 * v_ref[...] + (1.0 - ADAM_B2) * (gv * gv)
        m_hat = mn / (1.0 - ADAM_B1 ** ADAM_STEP)
        v_hat = vn / (1.0 - ADAM_B2 ** ADAM_STEP)
        d_ref[...] = -ADAM_LR * (m_hat / (jnp.sqrt(v_hat) + ADAM_EPS) + ADAM_WD * w_ref[...])
        nm_ref[...] = mn
        nv_ref[...] = vn

    blk = pl.BlockSpec((tr, c), lambda i: (i, 0))
    return pl.pallas_call(
        body, name=name, grid=(r // tr,), in_specs=[blk] * 4, out_specs=[blk] * 3,
        out_shape=[jax.ShapeDtypeStruct((r, c), F32)] * 3, compiler_params=_cparams("parallel"))(w, g, m, v)


HBM_SPEC = pl.BlockSpec(memory_space=pl.ANY)


def _place():
    x, y, c = lax.axis_index("x"), lax.axis_index("y"), lax.axis_index("c")
    chips = [(1 - x, y), (x, 1 - y), (1 - x, 1 - y)]
    return x, y, c, chips


def _ag_weights(packs):
    na = len(packs)

    def body(*refs):
        p_refs, out_refs = refs[:na], refs[na:2 * na]
        send_sems, recv_sems = refs[2 * na:]
        x, y, c, chips = _place()
        me, sib, j = (x, y, c), (x, y, 1 - c), 2 * x + y

        def rc(k, src, dst, to):
            return pltpu.make_async_remote_copy(src_ref=src, dst_ref=dst, send_sem=send_sems.at[k],
                                                recv_sem=recv_sems.at[k], device_id=to, device_id_type=MESH)

        first = [rc(6 * a + k, p_refs[a].at[c], out_refs[a].at[j, c], (cx, cy, c))
                 for a in range(na) for k, (cx, cy) in enumerate(chips)]
        for cp in first:
            cp.start()
        passed = []
        for a in range(na):
            for k, (cx, cy) in enumerate(chips):
                land = out_refs[a].at[2 * cx + cy, c]
                rc(6 * a + k, p_refs[a].at[c], land, me).wait_recv()
                fwd = rc(6 * a + 3 + k, land, land, sib)
                fwd.start()
                passed.append(fwd)
        for a in range(na):
            for k, (cx, cy) in enumerate(chips):
                rc(6 * a + 3 + k, p_refs[a].at[c], out_refs[a].at[2 * cx + cy, 1 - c], me).wait_recv()
        for cp in first + passed:
            cp.wait_send()

    return pl.pallas_call(
        body, name="ag_weights",
        out_shape=[jax.ShapeDtypeStruct((N_CHIPS,) + p.shape, p.dtype) for p in packs],
        in_specs=[HBM_SPEC] * na, out_specs=[HBM_SPEC] * na,
        scratch_shapes=[pltpu.SemaphoreType.DMA((6 * na,)), pltpu.SemaphoreType.DMA((6 * na,))])(*packs)


def _rs_pair(gpacks):
    na = len(gpacks)
    n = N_CHIPS

    def body(*refs):
        g_refs, out_refs = refs[:na], refs[na:2 * na]
        send_sems, recv_sems = refs[2 * na:]
        x, y, c, _ = _place()
        sib = (x, y, 1 - c)
        cps = [pltpu.make_async_remote_copy(src_ref=g_refs[a].at[p, 1 - c], dst_ref=out_refs[a].at[p],
                                            send_sem=send_sems.at[n * a + p], recv_sem=recv_sems.at[n * a + p],
                                            device_id=sib, device_id_type=MESH)
               for a in range(na) for p in range(n)]
        for cp in cps:
            cp.start()
        for cp in cps:
            cp.wait_recv()
        for cp in cps:
            cp.wait_send()

    return pl.pallas_call(
        body, name="rs_pair",
        out_shape=[jax.ShapeDtypeStruct((n,) + g.shape[2:], g.dtype) for g in gpacks],
        in_specs=[HBM_SPEC] * na, out_specs=[HBM_SPEC] * na,
        scratch_shapes=[pltpu.SemaphoreType.DMA((n * na,)), pltpu.SemaphoreType.DMA((n * na,))])(*gpacks)


def _rs_chips(csums):
    na = len(csums)

    def body(*refs):
        s_refs, out_refs = refs[:na], refs[na:2 * na]
        send_sems, recv_sems = refs[2 * na:]
        x, y, c, chips = _place()
        j = 2 * x + y
        cps = [pltpu.make_async_remote_copy(src_ref=s_refs[a].at[2 * cx + cy], dst_ref=out_refs[a].at[j],
                                            send_sem=send_sems.at[3 * a + k], recv_sem=recv_sems.at[3 * a + k],
                                            device_id=(cx, cy, c), device_id_type=MESH)
               for a in range(na) for k, (cx, cy) in enumerate(chips)]
        for cp in cps:
            cp.start()
        for a in range(na):
            for k, (cx, cy) in enumerate(chips):
                pltpu.make_async_remote_copy(src_ref=s_refs[a].at[j], dst_ref=out_refs[a].at[2 * cx + cy],
                                             send_sem=send_sems.at[3 * a + k], recv_sem=recv_sems.at[3 * a + k],
                                             device_id=(x, y, c), device_id_type=MESH).wait_recv()
        for cp in cps:
            cp.wait_send()

    return pl.pallas_call(
        body, name="rs_chips", out_shape=[jax.ShapeDtypeStruct(s.shape, s.dtype) for s in csums],
        in_specs=[HBM_SPEC] * na, out_specs=[HBM_SPEC] * na,
        scratch_shapes=[pltpu.SemaphoreType.DMA((3 * na,)), pltpu.SemaphoreType.DMA((3 * na,))])(*csums)


SWAP_CHUNKS = 4


def _pair_swap(halves):
    na = len(halves)

    def body(*refs):
        h_refs, out_refs = refs[:na], refs[na:2 * na]
        send_sems, recv_sems = refs[2 * na:]
        x, y, c, _ = _place()
        cps = []
        for a in range(na):
            rows = h_refs[a].shape[0] // SWAP_CHUNKS
            assert rows * SWAP_CHUNKS == h_refs[a].shape[0]
            for q in range(SWAP_CHUNKS):
                k = SWAP_CHUNKS * a + q
                cps.append(pltpu.make_async_remote_copy(
                    src_ref=h_refs[a].at[pl.ds(q * rows, rows)], dst_ref=out_refs[a].at[pl.ds(q * rows, rows)],
                    send_sem=send_sems.at[k], recv_sem=recv_sems.at[k], device_id=(x, y, 1 - c), device_id_type=MESH))
        for cp in cps:
            cp.start()
        for cp in cps:
            cp.wait_recv()
        for cp in cps:
            cp.wait_send()

    return pl.pallas_call(
        body, name="pair_swap", out_shape=[jax.ShapeDtypeStruct(h.shape, h.dtype) for h in halves],
        in_specs=[HBM_SPEC] * na, out_specs=[HBM_SPEC] * na,
        scratch_shapes=[pltpu.SemaphoreType.DMA((SWAP_CHUNKS * na,)), pltpu.SemaphoreType.DMA((SWAP_CHUNKS * na,))])(*halves)


def _ag_small(v):
    m_per, n = v.shape

    def body(x_ref, out_ref, send_sems, recv_sems, local_sem):
        x, y, c, chips = _place()
        me, sibling = (x, y, c), (x, y, 1 - c)

        def rows(px, py, pc):
            return out_ref.at[pl.ds((4 * px + 2 * py + pc) * m_per, m_per), :]

        def copy(k, block, to, src=None):
            return pltpu.make_async_remote_copy(
                src_ref=rows(*block) if src is None else src, dst_ref=rows(*block), send_sem=send_sems.at[k],
                recv_sem=recv_sems.at[k], device_id=to, device_id_type=MESH)

        mine = pltpu.make_async_copy(x_ref, rows(*me), local_sem)
        mine.start()
        first = [copy(0, me, sibling, src=x_ref)]
        first += [copy(1 + k, me, (*chip, c), src=x_ref) for k, chip in enumerate(chips)]
        for cp in first:
            cp.start()
        passed = [copy(4 + k, (*chip, c), sibling) for k, chip in enumerate(chips)]
        for k, chip in enumerate(chips):
            copy(1 + k, (*chip, c), me).wait_recv()
            passed[k].start()
        copy(0, sibling, me).wait_recv()
        for k, chip in enumerate(chips):
            copy(4 + k, (*chip, 1 - c), me).wait_recv()
        for cp in first + passed:
            cp.wait_send()
        mine.wait()

    return pl.pallas_call(
        body, name="ag_small", out_shape=jax.ShapeDtypeStruct((8 * m_per, n), v.dtype),
        in_specs=[pl.BlockSpec(memory_space=pltpu.VMEM)], out_specs=pl.BlockSpec(memory_space=pltpu.VMEM),
        scratch_shapes=[pltpu.SemaphoreType.DMA((7,)), pltpu.SemaphoreType.DMA((7,)), pltpu.SemaphoreType.DMA])(v)


def _sum_blocks(a, nblk, name):
    rows, wd = a.shape
    r = rows // nblk
    tr = min(r, ROW_TILE)
    assert r % tr == 0

    def body(*refs):
        acc = refs[0][...].astype(F32)
        for ref in refs[1:nblk]:
            acc = acc + ref[...].astype(F32)
        refs[nblk][...] = acc

    nt = r // tr
    return pl.pallas_call(
        body, name=name, grid=(nt,),
        in_specs=[pl.BlockSpec((tr, wd), functools.partial(lambda i, b: (b * nt + i, 0), b=b)) for b in range(nblk)],
        out_specs=pl.BlockSpec((tr, wd), lambda i: (i, 0)),
        out_shape=jax.ShapeDtypeStruct((r, wd), F32), compiler_params=_cparams("parallel"))(*([a] * nblk))


def _row_tile(rows):
    best = max(t for t in range(16, 513, 16) if rows % t == 0)
    return best


def _sum_chips(by_src, csum, j, name):
    n, rh, wd = by_src.shape
    tr = _row_tile(rh)

    def body(j_ref, *refs):
        own = refs[n][0].astype(F32)
        acc = None
        for k in range(n):
            term = jnp.where(j_ref[0] == k, own, refs[k][0].astype(F32))
            acc = term if acc is None else acc + term
        refs[n + 1][...] = acc

    def other(k):
        return pl.BlockSpec((1, tr, wd), lambda i, jr: (jnp.where(jr[0] == k, (k + 1) % n, k), i, 0))

    grid_spec = pltpu.PrefetchScalarGridSpec(
        num_scalar_prefetch=1, grid=(rh // tr,),
        in_specs=[other(k) for k in range(n)] + [pl.BlockSpec((1, tr, wd), lambda i, jr: (jr[0], i, 0))],
        out_specs=pl.BlockSpec((tr, wd), lambda i, jr: (i, 0)))
    return pl.pallas_call(
        body, name=name, grid_spec=grid_spec, out_shape=jax.ShapeDtypeStruct((rh, wd), F32),
        compiler_params=_cparams("parallel"))(jnp.reshape(j, (1,)).astype(jnp.int32), *([by_src] * n), csum)


def _add_halves(gpack, other, c, name):
    n, _, rh, wd = gpack.shape
    tr = _row_tile(rh)

    def body(c_ref, g_ref, o_ref, out_ref):
        out_ref[0] = (g_ref[0, 0] + o_ref[0]).astype(BF16)

    grid_spec = pltpu.PrefetchScalarGridSpec(
        num_scalar_prefetch=1, grid=(n, rh // tr),
        in_specs=[pl.BlockSpec((1, 1, tr, wd), lambda p, i, cr: (p, cr[0], i, 0)),
                  pl.BlockSpec((1, tr, wd), lambda p, i, cr: (p, i, 0))],
        out_specs=pl.BlockSpec((1, tr, wd), lambda p, i, cr: (p, i, 0)))
    return pl.pallas_call(
        body, name=name, grid_spec=grid_spec, out_shape=jax.ShapeDtypeStruct((n, rh, wd), BF16),
        compiler_params=_cparams("parallel", "parallel"))(jnp.reshape(c, (1,)).astype(jnp.int32), gpack, other)


PACK_W = 1024
ROWS_O_DN = DN_W // N_CHIPS
ROWS_O_DIL = DIL_W * (D_MODEL // N_CHIPS) // PACK_W
ROWS_OUT = D_MODEL // N_CHIPS
ROWS_CONV = 4 * (3 * DN_W // N_CHIPS) // PACK_W
R1 = ROWS_O_DN
R2 = R1 + ROWS_O_DIL
R3 = R2 + ROWS_OUT
R4 = R3 + ROWS_CONV
R5 = R4 + ROWS_CONV
PACK_ROWS = 1024
HALF_ROWS = PACK_ROWS // 2
SHARD_PAD = 2880


def _to_ref_layout(wpt):
    return jnp.concatenate([wpt[:REF_OFF_BA], wpt[OFF_BA:OFF_BA + 2 * DN_HEADS], wpt[REF_OFF_BA:OFF_BA]], axis=0)


def _from_ref_layout(wt):
    pad = jnp.zeros((PW - PROJ_W, wt.shape[1]), wt.dtype)
    return jnp.concatenate([wt[:REF_OFF_BA], wt[REF_OFF_BA + 2 * DN_HEADS:], wt[REF_OFF_BA:REF_OFF_BA + 2 * DN_HEADS], pad],
                           axis=0)


def _local_step(x, tgt, norm_w, wpt, conv_full, a_log, dt_bias, dn_norm_w, w_o_dn, w_o_dil, w_out, final_norm_w):
    s = x.shape[0]
    h, h_t = _rms_in(x, norm_w)
    proj = _matmul(h, wpt, F32, 512, 1280, 1024, "proj", nt=True)
    c_pre, qkv = _conv_fwd(proj, conv_full)
    gate_par = jnp.zeros((8, 128), F32).at[0, 8:16].set(a_log[0]).at[1, 8:16].set(dt_bias[0])
    bg = _gates_fwd(proj, gate_par)
    o_a, u, w, vn, tmat, states = _gdr_fwd(qkv, bg)
    oa2, oa2_t = _gdr_out(o_a, proj, dn_norm_w)
    ya = _matmul(oa2, w_o_dn, F32, 512, 1024, 1024, "ya")
    parts = [_att_fwd(proj, g) for g in range(N_DIL)]
    ob, o_att, lse, ob_t = _att_merge(parts, proj)
    yb = _matmul(ob, w_o_dil, F32, 512, 1024, 512, "yb")
    mg, mg_t = _merge(proj, ya, yb)
    t = _matmul(mg, w_out, F32, 512, 1024, 1024, "t_out")
    dx2, dfw, lpart = _final(x, t, final_norm_w, tgt)

    dmg = _matmul(dx2, w_out, F32, 512, 1024, 1024, "d_merged", nt=True)
    dw_out = _matmul(mg_t, dx2, F32, 1024, 1024, 1024, "dw_out")
    dya, dyb, dga, dgb = _merge_bwd(proj, ya, yb, dmg)
    doa2 = _matmul(dya, w_o_dn, F32, 512, 1024, 1024, "d_oa2", nt=True)
    dw_o_dn = _matmul(oa2_t, dya, F32, 1024, 1024, 1024, "dw_o_dn")
    dob = _matmul(dyb, w_o_dil, F32, 512, 512, 1024, "d_ob", nt=True)
    dw_o_dil = _matmul(ob_t, dyb, F32, 512, 1024, 1024, "dw_o_dil")
    do_a, dz_a, ddnw = _gdr_out_bwd(o_a, proj, dn_norm_w, doa2)
    dq_a, dk_a, dv_a, dbg = _gdr_bwd(qkv, bg, u, w, vn, tmat, states, do_a)
    dba, dpar = _gates_bwd(proj, gate_par, dbg)
    dc = _conv_bwd_act(c_pre, dq_a, dk_a, dv_a)
    du_a, dconv = _conv_bwd(proj, dc, conv_full)
    do_att, delta, dz_b = _att_merge_bwd(o_att, proj, dob)
    dqkv_b = [_att_bwd(proj, g, do_att, lse, delta) for g in range(N_DIL)]
    dproj = jnp.concatenate(
        [du_a, dz_a] + [dqkv_b[g][i] for i in range(3) for g in range(N_DIL)]
        + [dz_b, dga, dgb, dba, jnp.zeros((s, PW - OFF_BA - 128), BF16)], axis=1)
    dh = _matmul(dproj, wpt, F32, 512, 1024, 2304, "d_h")
    dwpt = _matmul(h_t, dproj, F32, 1024, 1280, 1024, "dw_in", transpose_out=True)
    grad_x, dnw = _rms_in_bwd(x, norm_w, dh, dx2)
    small = jnp.zeros((8, PACK_W), F32)
    small = small.at[0].set(dnw[0]).at[1].set(dfw[0]).at[2, :DN_D].set(ddnw[0])
    small = small.at[3, :DN_HEADS].set(dpar[0, 8:16]).at[3, DN_HEADS:2 * DN_HEADS].set(dpar[1, 8:16])
    small = small.at[4, 0].set(lpart[0, 0])
    return grad_x, dwpt, dconv, dw_o_dn, dw_o_dil, dw_out, small


def kernel(x, norm_w, w_in, conv_w, a_log, dt_bias, dn_norm_w, w_o_dn, w_o_dil, w_out, final_norm_w, loss_target, m_norm_w, m_w_in, m_conv_w, m_a_log, m_dt_bias, m_dn_norm_w, m_w_o_dn, m_w_o_dil, m_w_out, m_final_norm_w, v_norm_w, v_w_in, v_conv_w, v_a_log, v_dt_bias, v_dn_norm_w, v_w_o_dn, v_w_o_dil, v_w_out, v_final_norm_w):
    c = lax.axis_index("c")
    j = 2 * lax.axis_index("x") + lax.axis_index("y")
    qw = D_MODEL // N_CHIPS

    cw = conv_w[0].reshape(ROWS_CONV, PACK_W)
    cw_hi = cw.astype(BF16)
    cw_lo = (cw - cw_hi.astype(F32)).astype(BF16)
    pack = jnp.concatenate(
        [w_o_dn[0].astype(BF16), w_o_dil[0].astype(BF16).reshape(ROWS_O_DIL, PACK_W), w_out[0].astype(BF16), cw_hi, cw_lo,
         jnp.zeros((PACK_ROWS - R5, PACK_W), BF16)], axis=0).reshape(2, HALF_ROWS, PACK_W)
    own_in = jnp.pad(w_in[0].T.astype(BF16), ((0, SHARD_PAD - SHARD_W), (0, 0))).reshape(2, SHARD_PAD // 2, D_MODEL)
    all_in, allw = _ag_weights([own_in, pack])
    chips = range(N_CHIPS)
    all_in = [jnp.where(j == k, own_in, all_in[k]).reshape(SHARD_PAD, D_MODEL)[:SHARD_W] for k in chips]
    allw = [jnp.where(j == k, pack, allw[k]).reshape(PACK_ROWS, PACK_W) for k in chips]
    wpt = _from_ref_layout(jnp.concatenate(all_in, axis=0))
    w_o_dn_full = jnp.concatenate([allw[k][:R1] for k in chips], axis=0)
    w_o_dil_full = jnp.concatenate([allw[k][R1:R2].reshape(DIL_W, qw) for k in chips], axis=1)
    w_out_full = jnp.concatenate([allw[k][R2:R3] for k in chips], axis=0)
    conv_full = jnp.concatenate(
        [(allw[k][R3:R4].astype(F32) + allw[k][R4:R5].astype(F32)).reshape(4, 3 * DN_W // N_CHIPS) for k in chips], axis=1)

    grad_x, dwpt, dconv, dw_o_dn, dw_o_dil, dw_out, small = _local_step(
        x[0], loss_target[0], norm_w, wpt, conv_full, a_log, dt_bias, dn_norm_w, w_o_dn_full, w_o_dil_full, w_out_full,
        final_norm_w.reshape(1, D_MODEL))

    cq = 3 * DN_W // N_CHIPS
    dw_in_t = _to_ref_layout(dwpt)
    g_in = jnp.stack([jnp.pad(dw_in_t[k * SHARD_W:(k + 1) * SHARD_W], ((0, SHARD_PAD - SHARD_W), (0, 0))) for k in chips])
    g_in = g_in.reshape(N_CHIPS, 2, SHARD_PAD // 2, D_MODEL)
    gpack = jnp.stack([
        jnp.concatenate(
            [dw_o_dn[k * qw:(k + 1) * qw], dw_o_dil[:, k * qw:(k + 1) * qw].reshape(ROWS_O_DIL, PACK_W),
             dw_out[k * qw:(k + 1) * qw], dconv[:, k * cq:(k + 1) * cq].reshape(ROWS_CONV, PACK_W),
             jnp.zeros((PACK_ROWS - R4, PACK_W), F32)], axis=0)
        for k in chips]).reshape(N_CHIPS, 2, HALF_ROWS, PACK_W)
    sib_in, sib_pack = _rs_pair([g_in, gpack])
    csum_in = _add_halves(g_in, sib_in, c, "add_halves_in")
    csum_pack = _add_halves(gpack, sib_pack, c, "add_halves_pack")
    src_in, src_pack = _rs_chips([csum_in, csum_pack])
    half_in = _sum_chips(src_in, csum_in, j, "sum_chips_in")
    half_pack = _sum_chips(src_pack, csum_pack, j, "sum_chips_pack")
    sib_half_in, sib_half_pack = _pair_swap([half_in, half_pack])

    def both_halves(mine, theirs):
        return jnp.where(c == 0, jnp.concatenate([mine, theirs], axis=0), jnp.concatenate([theirs, mine], axis=0))

    lin = (SHARD_W * D_MODEL // 128, 128)
    g_w_in = both_halves(half_in, sib_half_in)[:SHARD_W].reshape(lin)
    g = both_halves(half_pack, sib_half_pack)
    g_w_o_dn = g[:R1]
    g_w_o_dil = g[R1:R2].reshape(DIL_W, qw)
    g_w_out = g[R2:R3]
    g_conv = g[R3:R4].reshape(4, cq)

    gs = _sum_blocks(_ag_small(small), 8, "sum_small")
    loss = gs[4, 0]
    w_small = jnp.zeros((8, PACK_W), F32)

    def pack_small(nw, fw, dnw_, al, db):
        t = w_small.at[0].set(nw[0]).at[1].set(fw).at[2, :DN_D].set(dnw_[0])
        return t.at[3, :DN_HEADS].set(al[0]).at[3, DN_HEADS:2 * DN_HEADS].set(db[0])

    sm = _adamw(pack_small(norm_w, final_norm_w, dn_norm_w, a_log, dt_bias), gs,
                pack_small(m_norm_w, m_final_norm_w, m_dn_norm_w, m_a_log, m_dt_bias),
                pack_small(v_norm_w, v_final_norm_w, v_dn_norm_w, v_a_log, v_dt_bias), "adamw_small")

    def unpack_small(t):
        return dict(norm_w=t[0:1], final_norm_w=t[1], dn_norm_w=t[2:3, :DN_D], a_log=t[3:4, :DN_HEADS],
                    dt_bias=t[3:4, DN_HEADS:2 * DN_HEADS])

    res = {"grad": unpack_small(gs)}
    for kind, arr in zip(("delta", "new_m", "new_v"), sm):
        res[kind] = unpack_small(arr)
    big = dict(conv_w=(conv_w, g_conv, m_conv_w, v_conv_w), w_o_dn=(w_o_dn, g_w_o_dn, m_w_o_dn, v_w_o_dn),
               w_o_dil=(w_o_dil, g_w_o_dil, m_w_o_dil, v_w_o_dil), w_out=(w_out, g_w_out, m_w_out, v_w_out))
    for name, (wt, gt, mt, vt) in big.items():
        d, nm, nv = _adamw(wt[0], gt, mt[0], vt[0], "adamw_" + name)
        res["grad"][name] = gt[None]
        res["delta"][name], res["new_m"][name], res["new_v"][name] = d[None], nm[None], nv[None]

    def to_lin(a):
        return a[0].T.reshape(lin)

    def from_lin(a):
        return a.reshape(SHARD_W, D_MODEL).T[None]

    d, nm, nv = _adamw(to_lin(w_in), g_w_in, to_lin(m_w_in), to_lin(v_w_in), "adamw_w_in")
    res["grad"]["w_in"] = from_lin(g_w_in)
    res["delta"]["w_in"], res["new_m"]["w_in"], res["new_v"]["w_in"] = from_lin(d), from_lin(nm), from_lin(nv)
    order = ["norm_w", "w_in", "conv_w", "a_log", "dt_bias", "dn_norm_w", "w_o_dn", "w_o_dil", "w_out", "final_norm_w"]
    outs = [loss, grad_x[None]]
    for kind in ("grad", "delta", "new_m", "new_v"):
        outs += [res[kind][nm] for nm in order]
    return tuple(outs)
```

```python
import functools
import math

import jax
import jax.numpy as jnp
from jax import lax
from jax.experimental import pallas as pl
from jax.experimental.pallas import tpu as pltpu

F32 = jnp.float32
BF16 = jnp.bfloat16
MESH = pl.DeviceIdType.MESH

D_MODEL = 1024
DN_HEADS = 8
DN_D = 128
DN_CHUNK = 64
DN_W = DN_HEADS * DN_D
DIL_GROUPS = ((128, 1), (512, 4), (2048, 16))
N_DIL = len(DIL_GROUPS)
DIL_HEADS = 4
DIL_DH = 128
DIL_W = DIL_HEADS * DIL_DH
ATT_BLOCK = 128
NORM_EPS = 1e-6
PROJ_W = 11280
N_CHIPS = 4
SHARD_W = PROJ_W // N_CHIPS

OFF_QKV_A = 0
OFF_Z_A = 3072
OFF_Q_B = 4096
OFF_K_B = 5632
OFF_V_B = 7168
OFF_Z_B = 8704
OFF_G_A = 9216
OFF_G_B = 10240
OFF_BA = 11264
PW = 11520
REF_OFF_BA = 4096

ADAM_LR = 0.001
ADAM_B1 = 0.9
ADAM_B2 = 0.999
ADAM_EPS = 1e-08
ADAM_WD = 0.01
ADAM_STEP = 10

ROW_TILE = 256
NEG = -1e30


def _dot(a, b):
    return jnp.dot(a.astype(BF16), b.astype(BF16), preferred_element_type=F32)


def _dot_nt(a, b):
    return lax.dot_general(a.astype(BF16), b.astype(BF16), (((1,), (1,)), ((), ())), preferred_element_type=F32)


def _dot_tn(a, b):
    return lax.dot_general(a.astype(BF16), b.astype(BF16), (((0,), (0,)), ((), ())), preferred_element_type=F32)


def _split(a):
    hi = a.astype(BF16)
    lo = (a - hi.astype(F32)).astype(BF16)
    return hi, lo


def _dot_exact_lhs(c, a):
    hi, lo = _split(a)
    cb = c.astype(BF16)
    return jnp.dot(cb, hi, preferred_element_type=F32) + jnp.dot(cb, lo, preferred_element_type=F32)


def _dot_exact_rhs(a, c):
    hi, lo = _split(a)
    cb = c.astype(BF16)
    return jnp.dot(hi, cb, preferred_element_type=F32) + jnp.dot(lo, cb, preferred_element_type=F32)


def _dot_tn_exact_rhs(a, c):
    hi, lo = _split(a)
    cb = c.astype(BF16)
    dn = (((0,), (0,)), ((), ()))
    return (lax.dot_general(hi, cb, dn, preferred_element_type=F32)
            + lax.dot_general(lo, cb, dn, preferred_element_type=F32))


def _sigmoid(x):
    return 1.0 / (1.0 + jnp.exp(-x))


def _silu(x):
    return x * _sigmoid(x)


def _silu_grad(x):
    s = _sigmoid(x)
    return s * (1.0 + x * (1.0 - s))


def _softplus(x):
    return jnp.maximum(x, 0.0) + jnp.log(1.0 + jnp.exp(-jnp.abs(x)))


def _cparams(*sem):
    return pltpu.CompilerParams(dimension_semantics=sem)


def _matmul(a, b, out_dtype, tm, tn, tk, name, nt=False, transpose_out=False):
    m, kdim = a.shape
    n = b.shape[0] if nt else b.shape[1]
    tm, tn, tk = min(tm, m), min(tn, n), min(tk, kdim)
    assert m % tm == 0 and n % tn == 0 and kdim % tk == 0, (name, a.shape, b.shape, tm, tn, tk)
    nk = kdim // tk
    dot = _dot_nt if nt else _dot
    b_spec = (pl.BlockSpec((tn, tk), lambda i, j, k: (j, k)) if nt else pl.BlockSpec((tk, tn), lambda i, j, k: (k, j)))

    def emit(o_ref, acc):
        o_ref[...] = (acc.T if transpose_out else acc).astype(o_ref.dtype)

    if nk == 1:
        def body(a_ref, b_ref, o_ref):
            emit(o_ref, dot(a_ref[...], b_ref[...]))
        scratch = []
    else:
        def body(a_ref, b_ref, o_ref, acc_ref):
            k = pl.program_id(2)
            p = dot(a_ref[...], b_ref[...])

            @pl.when(k == 0)
            def _():
                acc_ref[...] = p

            @pl.when(k > 0)
            def _():
                acc_ref[...] += p

            @pl.when(k == nk - 1)
            def _():
                emit(o_ref, acc_ref[...])
        scratch = [pltpu.VMEM((tm, tn), F32)]

    if transpose_out:
        out_spec, out_shape = pl.BlockSpec((tn, tm), lambda i, j, k: (j, i)), (n, m)
    else:
        out_spec, out_shape = pl.BlockSpec((tm, tn), lambda i, j, k: (i, j)), (m, n)
    return pl.pallas_call(
        body, name=name, grid=(m // tm, n // tn, nk), in_specs=[pl.BlockSpec((tm, tk), lambda i, j, k: (i, k)), b_spec],
        out_specs=out_spec, out_shape=jax.ShapeDtypeStruct(out_shape, out_dtype), scratch_shapes=scratch,
        compiler_params=_cparams("parallel", "parallel", "arbitrary"))(a, b)


def _rms_in(x, nw):
    s, d = x.shape

    def body(x_ref, w_ref, h_ref, ht_ref):
        xv = x_ref[...]
        r = lax.rsqrt(jnp.mean(xv * xv, axis=-1, keepdims=True) + NORM_EPS)
        h = xv * r * w_ref[...]
        h_ref[...] = h.astype(BF16)
        ht_ref[...] = h.T.astype(BF16)

    return pl.pallas_call(
        body, name="rms_in", grid=(s // ROW_TILE,),
        in_specs=[pl.BlockSpec((ROW_TILE, d), lambda i: (i, 0)), pl.BlockSpec((1, d), lambda i: (0, 0))],
        out_specs=[pl.BlockSpec((ROW_TILE, d), lambda i: (i, 0)), pl.BlockSpec((d, ROW_TILE), lambda i: (0, i))],
        out_shape=[jax.ShapeDtypeStruct((s, d), BF16), jax.ShapeDtypeStruct((d, s), BF16)],
        compiler_params=_cparams("parallel"))(x, nw)


def _rms_in_bwd(x, nw, dh, dx2):
    s, d = x.shape

    def body(x_ref, w_ref, dh_ref, dx2_ref, dx_ref, dw_ref):
        i = pl.program_id(0)
        xv = x_ref[...]
        r = lax.rsqrt(jnp.mean(xv * xv, axis=-1, keepdims=True) + NORM_EPS)
        dhv = dh_ref[...]
        dyw = dhv * w_ref[...]
        dx_ref[...] = dx2_ref[...] + r * dyw - xv * (r * r * r) * jnp.mean(dyw * xv, axis=-1, keepdims=True)
        part = jnp.sum(dhv * xv * r, axis=0, keepdims=True)

        @pl.when(i == 0)
        def _():
            dw_ref[...] = part

        @pl.when(i > 0)
        def _():
            dw_ref[...] += part

    row = pl.BlockSpec((ROW_TILE, d), lambda i: (i, 0))
    vec = pl.BlockSpec((1, d), lambda i: (0, 0))
    return pl.pallas_call(
        body, name="rms_in_bwd", grid=(s // ROW_TILE,), in_specs=[row, vec, row, row], out_specs=[row, vec],
        out_shape=[jax.ShapeDtypeStruct((s, d), F32), jax.ShapeDtypeStruct((1, d), F32)],
        compiler_params=_cparams("arbitrary"))(x, nw, dh, dx2)


def _shift_down(cur, prev8, k):
    rc = pltpu.roll(cur, k, 0)
    rp = pltpu.roll(prev8, k, 0)
    row = lax.broadcasted_iota(jnp.int32, prev8.shape, 0)
    top = jnp.where(row < k, rp, rc[:8])
    return jnp.concatenate([top, rc[8:]], axis=0)


def _shift_up(cur, next8, k):
    t = cur.shape[0]
    rc = pltpu.roll(cur, t - k, 0)
    rn = pltpu.roll(next8, 8 - k, 0)
    row = lax.broadcasted_iota(jnp.int32, next8.shape, 0)
    bot = jnp.where(row >= 8 - k, rn, rc[t - 8:])
    return jnp.concatenate([rc[:t - 8], bot], axis=0)


def _conv_fwd(proj, conv_w):
    s = proj.shape[0]
    t8 = ROW_TILE // 8

    def body(u_ref, up_ref, w_ref, c_ref, y_ref):
        i = pl.program_id(0)
        part = pl.program_id(1)
        cur = u_ref[...]
        prev8 = jnp.where(i > 0, up_ref[...], 0.0)
        w = w_ref[...]
        c = cur * w[3:4, :]
        for k in (1, 2, 3):
            c = c + _shift_down(cur, prev8, k) * w[3 - k:4 - k, :]
        c_ref[...] = c
        a = _silu(c)
        for h in range(DN_HEADS):
            ah = a[:, h * DN_D:(h + 1) * DN_D]
            r = lax.rsqrt(jnp.sum(ah * ah, axis=-1, keepdims=True) + NORM_EPS)
            y_ref[:, h * DN_D:(h + 1) * DN_D] = jnp.where(part < 2, ah * r, ah)

    return pl.pallas_call(
        body, name="conv_fwd", grid=(s // ROW_TILE, 3),
        in_specs=[pl.BlockSpec((ROW_TILE, DN_W), lambda i, p: (i, p)),
                  pl.BlockSpec((8, DN_W), lambda i, p: (jnp.maximum(i * t8 - 1, 0), p)),
                  pl.BlockSpec((4, DN_W), lambda i, p: (0, p))],
        out_specs=[pl.BlockSpec((ROW_TILE, DN_W), lambda i, p: (i, p))] * 2,
        out_shape=[jax.ShapeDtypeStruct((s, 3 * DN_W), F32)] * 2,
        compiler_params=_cparams("parallel", "parallel"))(proj, proj, conv_w)


def _conv_bwd_act(c, dq, dk, dv):
    s = c.shape[0]

    def body(c_ref, dq_ref, dk_ref, dv_ref, dc_ref):
        for part, d_ref in enumerate((dq_ref, dk_ref, dv_ref)):
            for h in range(DN_HEADS):
                sl = slice(part * DN_W + h * DN_D, part * DN_W + (h + 1) * DN_D)
                ch = c_ref[:, sl]
                dyh = d_ref[:, h * DN_D:(h + 1) * DN_D]
                if part < 2:
                    ah = _silu(ch)
                    r = lax.rsqrt(jnp.sum(ah * ah, axis=-1, keepdims=True) + NORM_EPS)
                    dyh = r * dyh - ah * (r * r * r) * jnp.sum(dyh * ah, axis=-1, keepdims=True)
                dc_ref[:, sl] = dyh * _silu_grad(ch)

    wide = pl.BlockSpec((ROW_TILE, 3 * DN_W), lambda i: (i, 0))
    row = pl.BlockSpec((ROW_TILE, DN_W), lambda i: (i, 0))
    return pl.pallas_call(
        body, name="conv_bwd_act", grid=(s // ROW_TILE,), in_specs=[wide, row, row, row], out_specs=wide,
        out_shape=jax.ShapeDtypeStruct((s, 3 * DN_W), F32), compiler_params=_cparams("parallel"))(c, dq, dk, dv)


def _conv_bwd(proj, dc, conv_w):
    s = proj.shape[0]
    t8 = ROW_TILE // 8
    nrow = s // ROW_TILE
    last8 = s // 8 - 1

    def body(u_ref, up_ref, dc_ref, dcn_ref, w_ref, du_ref, dw_ref):
        i = pl.program_id(1)
        cur = u_ref[...]
        prev8 = jnp.where(i > 0, up_ref[...], 0.0)
        dcv = dc_ref[...]
        next8 = jnp.where(i < nrow - 1, dcn_ref[...], 0.0)
        w = w_ref[...]
        du = dcv * w[3:4, :]
        for k in (1, 2, 3):
            du = du + _shift_up(dcv, next8, k) * w[3 - k:4 - k, :]
        du_ref[...] = du.astype(BF16)

        @pl.when(i == 0)
        def _():
            dw_ref[...] = jnp.zeros_like(dw_ref)

        dw_ref[3:4, :] += jnp.sum(cur * dcv, axis=0, keepdims=True)
        for k in (1, 2, 3):
            dw_ref[3 - k:4 - k, :] += jnp.sum(_shift_down(cur, prev8, k) * dcv, axis=0, keepdims=True)

    blk = pl.BlockSpec((ROW_TILE, DN_W), lambda p, i: (i, p))
    return pl.pallas_call(
        body, name="conv_bwd", grid=(3, nrow),
        in_specs=[blk, pl.BlockSpec((8, DN_W), lambda p, i: (jnp.maximum(i * t8 - 1, 0), p)),
                  blk, pl.BlockSpec((8, DN_W), lambda p, i: (jnp.minimum((i + 1) * t8, last8), p)),
                  pl.BlockSpec((4, DN_W), lambda p, i: (0, p))],
        out_specs=[blk, pl.BlockSpec((4, DN_W), lambda p, i: (0, p))],
        out_shape=[jax.ShapeDtypeStruct((s, 3 * DN_W), BF16), jax.ShapeDtypeStruct((4, 3 * DN_W), F32)],
        compiler_params=_cparams("parallel", "arbitrary"))(proj, proj, dc, dc, conv_w)


def _gates_fwd(proj, gate_par):
    s = proj.shape[0]

    def body(ba_ref, par_ref, o_ref):
        v = ba_ref[...]
        lane = lax.broadcasted_iota(jnp.int32, v.shape, 1)
        beta = _sigmoid(v)
        g = -jnp.exp(par_ref[0:1, :]) * _softplus(v + par_ref[1:2, :])
        o_ref[...] = jnp.where(lane < DN_HEADS, beta, jnp.where(lane < 2 * DN_HEADS, g, 0.0))

    return pl.pallas_call(
        body, name="gates_fwd", grid=(s // ROW_TILE,),
        in_specs=[pl.BlockSpec((ROW_TILE, 128), lambda i: (i, OFF_BA // 128)), pl.BlockSpec((8, 128), lambda i: (0, 0))],
        out_specs=pl.BlockSpec((ROW_TILE, 128), lambda i: (i, 0)),
        out_shape=jax.ShapeDtypeStruct((s, 128), F32), compiler_params=_cparams("parallel"))(proj, gate_par)


def _gates_bwd(proj, gate_par, dbg):
    s = proj.shape[0]

    def body(ba_ref, par_ref, d_ref, o_ref, dpar_ref):
        i = pl.program_id(0)
        v = ba_ref[...]
        dv = d_ref[...]
        lane = lax.broadcasted_iota(jnp.int32, v.shape, 1)
        beta = _sigmoid(v)
        nega = -jnp.exp(par_ref[0:1, :])
        xs = v + par_ref[1:2, :]
        dsp = dv * nega * _sigmoid(xs)
        dal = dv * nega * _softplus(xs)
        is_b = lane < DN_HEADS
        is_g = jnp.logical_and(lane >= DN_HEADS, lane < 2 * DN_HEADS)
        o_ref[...] = jnp.where(is_b, dv * beta * (1.0 - beta), jnp.where(is_g, dsp, 0.0)).astype(BF16)
        r0 = jnp.sum(jnp.where(is_g, dal, 0.0), axis=0, keepdims=True)
        r1 = jnp.sum(jnp.where(is_g, dsp, 0.0), axis=0, keepdims=True)

        @pl.when(i == 0)
        def _():
            dpar_ref[...] = jnp.zeros_like(dpar_ref)

        dpar_ref[0:1, :] += r0
        dpar_ref[1:2, :] += r1

    return pl.pallas_call(
        body, name="gates_bwd", grid=(s // ROW_TILE,),
        in_specs=[pl.BlockSpec((ROW_TILE, 128), lambda i: (i, OFF_BA // 128)), pl.BlockSpec((8, 128), lambda i: (0, 0)),
                  pl.BlockSpec((ROW_TILE, 128), lambda i: (i, 0))],
        out_specs=[pl.BlockSpec((ROW_TILE, 128), lambda i: (i, 0)), pl.BlockSpec((8, 128), lambda i: (0, 0))],
        out_shape=[jax.ShapeDtypeStruct((s, 128), BF16), jax.ShapeDtypeStruct((8, 128), F32)],
        compiler_params=_cparams("arbitrary"))(proj, gate_par, dbg)


def _chunk_masks():
    c = DN_CHUNK
    ii = lax.broadcasted_iota(jnp.int32, (c, c), 0)
    jj = lax.broadcasted_iota(jnp.int32, (c, c), 1)
    return dict(ii=ii, jj=jj, lower=(ii >= jj), strict=(ii > jj), eye=(ii == jj),
                lower_f=(ii >= jj).astype(BF16), upper_f=(ii <= jj).astype(BF16), ones8=jnp.ones((8, c), BF16))


class _Heads:
    def __init__(self, xs):
        self.xs = list(xs)

    def _bin(self, o, f):
        if isinstance(o, _Heads):
            return _Heads([f(a, b) for a, b in zip(self.xs, o.xs)])
        return _Heads([f(a, o) for a in self.xs])

    def __add__(self, o):
        return self._bin(o, lambda a, b: a + b)

    def __sub__(self, o):
        return self._bin(o, lambda a, b: a - b)

    def __mul__(self, o):
        return self._bin(o, lambda a, b: a * b)

    __radd__ = __add__
    __rmul__ = __mul__

    def __neg__(self):
        return _Heads([-a for a in self.xs])

    def __getitem__(self, i):
        return _Heads([a[i] for a in self.xs])


def _hmap(f, *args):
    n = next(len(a.xs) for a in args if isinstance(a, _Heads))
    return _Heads([f(*[(a.xs[h] if isinstance(a, _Heads) else a) for a in args]) for h in range(n)])


def _hdot(a, b):
    return _hmap(_dot, a, b)


def _hdot_nt(a, b):
    return _hmap(_dot_nt, a, b)


def _hdot_tn(a, b):
    return _hmap(_dot_tn, a, b)


def _hsum(a, axis):
    return _hmap(lambda t: jnp.sum(t, axis=axis, keepdims=True), a)


def _hwhere(c, a, b):
    return _hmap(jnp.where, c, a, b)


def _chunk_common(mk, q, k, beta_col, g_col):
    c = DN_CHUNK
    lower, strict = mk["lower"], mk["strict"]
    qs = q * (DN_D ** -0.5)
    beta_b = _hmap(lambda t: jnp.broadcast_to(t, (c, DN_D)), beta_col)
    g_b = _hmap(lambda t: jnp.broadcast_to(t, (c, DN_D)), g_col)
    gc_b = _hmap(_dot_exact_lhs, mk["lower_f"], g_b)
    gc_sq = gc_b[:, :c]
    gc_r = _hmap(_dot_exact_lhs, mk["ones8"], _hwhere(mk["eye"], gc_sq, 0.0))[0:1, :]
    gam = _hwhere(lower, _hmap(lambda t: jnp.exp(jnp.minimum(t, 0.0)), gc_sq - gc_r), 0.0)
    egc = _hmap(jnp.exp, gc_b)
    gl = gc_b[c - 1:c, :]
    ekd = _hmap(jnp.exp, gl - gc_b)
    dl = _hmap(jnp.exp, gl)
    kb = k * beta_b
    a_strict = _hwhere(strict, _hdot_nt(kb, k) * gam, 0.0)
    aqk = _hwhere(lower, _hdot_nt(qs, k) * gam, 0.0)
    return dict(k=k, qs=qs, beta_b=beta_b, gc_b=gc_b, gam=gam, egc=egc, ekd=ekd, dl=dl, kb=kb, a_strict=a_strict, aqk=aqk)


def _unit_lower_inverse_minus_eye(n_strict, ii, jj):
    same = lax.shift_right_logical(ii, 4) == lax.shift_right_logical(jj, 4)
    dmat = _hwhere(same, n_strict, 0.0)
    omat = n_strict - dmat
    d2 = _hdot(dmat, dmat)
    d4 = _hdot(d2, d2)
    d8 = _hdot(d4, d4)
    x1 = d2 - dmat - _hdot(dmat, d2)
    x2 = x1 + d4 + _hdot(x1, d4)
    x3 = x2 + d8 + _hdot(x2, d8)
    n1 = omat + _hdot(x3, omat)
    n2 = _hdot(n1, n1)
    y = n2 - n1 - _hdot(n1, n2)
    return y + x3 + _hdot(y, x3)


def _gdr_fwd(qkv, bg):
    s = qkv.shape[0]
    c = DN_CHUNK
    n = s // c

    def body(q_ref, k_ref, v_ref, bg_ref, o_ref, u_ref, w_ref, vn_ref, tm_ref, st_ref, state):
        @pl.when(pl.program_id(0) == 0)
        def _():
            state[...] = jnp.zeros_like(state)

        mk = _chunk_masks()
        bg = bg_ref[...]
        hs = range(DN_HEADS)
        sls = [slice(h * DN_D, (h + 1) * DN_D) for h in hs]
        cm = _chunk_common(mk, _Heads(q_ref[:, sl] for sl in sls), _Heads(k_ref[:, sl] for sl in sls),
                           _Heads(bg[:, h:h + 1] for h in hs), _Heads(bg[:, DN_HEADS + h:DN_HEADS + h + 1] for h in hs))
        tm = _unit_lower_inverse_minus_eye(cm["a_strict"], mk["ii"], mk["jj"])
        rhs_u = _Heads(v_ref[:, sl] for sl in sls) * cm["beta_b"]
        rhs_w = cm["kb"] * cm["egc"]
        u = rhs_u + _hdot(tm, rhs_u)
        w = rhs_w + _hdot(tm, rhs_w)
        st = _Heads(state[h] for h in hs)
        v_new = u - _hdot(w, st)
        o = _hdot(cm["qs"] * cm["egc"], st) + _hdot(cm["aqk"], v_new)
        st_new = st * cm["dl"] + _hdot_tn(cm["k"] * cm["ekd"], v_new)
        for h, sl in zip(hs, sls):
            o_ref[:, sl] = o.xs[h]
            u_ref[:, sl] = u.xs[h]
            w_ref[:, sl] = w.xs[h]
            vn_ref[:, sl] = v_new.xs[h]
            tm_ref[h, 0] = tm.xs[h]
            st_ref[h, 0] = st.xs[h]
            state[h] = st_new.xs[h]

    def part(p):
        return pl.BlockSpec((c, DN_W), lambda j: (j, p))

    return pl.pallas_call(
        body, name="gdr_fwd", grid=(n,),
        in_specs=[part(0), part(1), part(2), pl.BlockSpec((c, 128), lambda j: (j, 0))],
        out_specs=[part(0)] * 4 + [pl.BlockSpec((DN_HEADS, 1, c, c), lambda j: (0, j, 0, 0)),
                                   pl.BlockSpec((DN_HEADS, 1, DN_D, DN_D), lambda j: (0, j, 0, 0))],
        out_shape=[jax.ShapeDtypeStruct((s, DN_W), F32)] * 4
        + [jax.ShapeDtypeStruct((DN_HEADS, n, c, c), F32), jax.ShapeDtypeStruct((DN_HEADS, n, DN_D, DN_D), F32)],
        scratch_shapes=[pltpu.VMEM((DN_HEADS, DN_D, DN_D), F32)],
        compiler_params=_cparams("arbitrary"))(qkv, qkv, qkv, bg)


def _gdr_bwd(qkv, bg, u, w, vn, tmat, states, do):
    s = qkv.shape[0]
    c = DN_CHUNK
    n = s // c

    def body(q_ref, k_ref, v_ref, bg_ref, u_ref, w_ref, vn_ref, tm_ref, st_ref, do_ref,
             dq_ref, dk_ref, dv_ref, dbg_ref, dstate):
        @pl.when(pl.program_id(0) == 0)
        def _():
            dstate[...] = jnp.zeros_like(dstate)

        mk = _chunk_masks()
        lower, strict = mk["lower"], mk["strict"]
        bg = bg_ref[...]
        ones = jnp.ones((c, DN_D), BF16)
        rowi = lax.broadcasted_iota(jnp.int32, (c, DN_D), 0)
        lane = lax.broadcasted_iota(jnp.int32, (c, 128), 1)
        hs = range(DN_HEADS)
        sls = [slice(h * DN_D, (h + 1) * DN_D) for h in hs]

        def heads_of(ref):
            return _Heads(ref[:, sl] for sl in sls)

        cm = _chunk_common(mk, heads_of(q_ref), heads_of(k_ref),
                           _Heads(bg[:, h:h + 1] for h in hs), _Heads(bg[:, DN_HEADS + h:DN_HEADS + h + 1] for h in hs))
        k, qs, beta_b = cm["k"], cm["qs"], cm["beta_b"]
        gam, egc, ekd, dl, kb = cm["gam"], cm["egc"], cm["ekd"], cm["dl"], cm["kb"]
        aqk, a_strict = cm["aqk"], cm["a_strict"]
        v, uu, ww, v_new, dov = heads_of(v_ref), heads_of(u_ref), heads_of(w_ref), heads_of(vn_ref), heads_of(do_ref)
        tm = _Heads(tm_ref[h, 0] for h in hs)
        st = _Heads(st_ref[h, 0] for h in hs)
        dsn = _Heads(dstate[h] for h in hs)
        qd = qs * egc
        kd = k * ekd

        dv_new = _hdot_tn(aqk, dov) + _hdot(kd, dsn)
        daqk = _hwhere(lower, _hdot_nt(dov, v_new), 0.0)
        dqd = _hdot_nt(dov, st)
        dkd = _hdot_nt(v_new, dsn)
        ddl = _hsum(_hsum(dsn * st, 1), 0)
        dw = -_hdot_nt(dv_new, st)
        ds_new = dsn * dl + _hdot_tn(qd, dov) - _hdot_tn(ww, dv_new)

        dru = dv_new + _hdot_tn(tm, dv_new)
        drw = dw + _hdot_tn(tm, dw)
        dn = _hwhere(strict, -(_hdot_nt(dru, uu) + _hdot_nt(drw, ww)), 0.0)
        dag = dn * gam
        dkb = _hdot(dag, k) + drw * egc
        dk = _hdot_tn(dag, kb)
        dqg = daqk * gam
        dqs = _hdot(dqg, k) + dqd * egc
        dk = dk + _hdot_tn(dqg, qs) + dkb * beta_b + dkd * ekd
        pmat = dn * a_strict + daqk * aqk
        tkd = _hsum(dkd * kd, -1)
        dgc = (_hsum(pmat, -1) - _hmap(_dot_tn_exact_rhs, pmat, ones) + _hsum(drw * (kb * egc), -1)
               + _hsum(dqd * qd, -1) - tkd)
        last = _hsum(tkd, 0) + ddl * dl
        dgc = dgc + _hwhere(rowi == c - 1, last, 0.0)
        dg = _hmap(_dot_exact_lhs, mk["upper_f"], dgc)
        dbeta = _hsum(dru * v, -1) + _hsum(dkb * k, -1)
        dq = dqs * (DN_D ** -0.5)
        dv = dru * beta_b

        dbg = jnp.zeros((c, 128), F32)
        for h, sl in zip(hs, sls):
            dq_ref[:, sl] = dq.xs[h]
            dk_ref[:, sl] = dk.xs[h]
            dv_ref[:, sl] = dv.xs[h]
            dstate[h] = ds_new.xs[h]
            dbg = dbg + jnp.where(lane == h, dbeta.xs[h], 0.0) + jnp.where(lane == DN_HEADS + h, dg.xs[h], 0.0)
        dbg_ref[...] = dbg

    def part(p):
        return pl.BlockSpec((c, DN_W), lambda j: (n - 1 - j, p))

    vec = pl.BlockSpec((c, 128), lambda j: (n - 1 - j, 0))
    return pl.pallas_call(
        body, name="gdr_bwd", grid=(n,),
        in_specs=[part(0), part(1), part(2), vec, part(0), part(0), part(0),
                  pl.BlockSpec((DN_HEADS, 1, c, c), lambda j: (0, n - 1 - j, 0, 0)),
                  pl.BlockSpec((DN_HEADS, 1, DN_D, DN_D), lambda j: (0, n - 1 - j, 0, 0)), part(0)],
        out_specs=[part(0), part(0), part(0), vec],
        out_shape=[jax.ShapeDtypeStruct((s, DN_W), F32)] * 3 + [jax.ShapeDtypeStruct((s, 128), F32)],
        scratch_shapes=[pltpu.VMEM((DN_HEADS, DN_D, DN_D), F32)],
        compiler_params=_cparams("arbitrary"))(qkv, qkv, qkv, bg, u, w, vn, tmat, states, do)


def _gdr_out(o, proj, dnw):
    s = o.shape[0]

    def body(o_ref, z_ref, w_ref, y_ref, yt_ref):
        ov, zv, wv = o_ref[...], z_ref[...], w_ref[...]
        for h in range(DN_HEADS):
            sl = slice(h * DN_D, (h + 1) * DN_D)
            oh = ov[:, sl]
            r = lax.rsqrt(jnp.mean(oh * oh, axis=-1, keepdims=True) + NORM_EPS)
            y = (oh * r * wv) * _silu(zv[:, sl])
            y_ref[:, sl] = y.astype(BF16)
            yt_ref[sl, :] = y.T.astype(BF16)

    row = pl.BlockSpec((ROW_TILE, DN_W), lambda i: (i, 0))
    return pl.pallas_call(
        body, name="gdr_out", grid=(s // ROW_TILE,),
        in_specs=[row, pl.BlockSpec((ROW_TILE, DN_W), lambda i: (i, OFF_Z_A // DN_W)), pl.BlockSpec((1, DN_D), lambda i: (0, 0))],
        out_specs=[row, pl.BlockSpec((DN_W, ROW_TILE), lambda i: (0, i))],
        out_shape=[jax.ShapeDtypeStruct((s, DN_W), BF16), jax.ShapeDtypeStruct((DN_W, s), BF16)],
        compiler_params=_cparams("parallel"))(o, proj, dnw)


def _gdr_out_bwd(o, proj, dnw, dy):
    s = o.shape[0]

    def body(o_ref, z_ref, w_ref, dy_ref, do_ref, dz_ref, dw_ref):
        i = pl.program_id(0)
        ov, zv, wv, dyv = o_ref[...], z_ref[...], w_ref[...], dy_ref[...]
        acc = jnp.zeros((1, DN_D), F32)
        for h in range(DN_HEADS):
            sl = slice(h * DN_D, (h + 1) * DN_D)
            oh, zh, dh = ov[:, sl], zv[:, sl], dyv[:, sl]
            r = lax.rsqrt(jnp.mean(oh * oh, axis=-1, keepdims=True) + NORM_EPS)
            dn = dh * _silu(zh)
            dz_ref[:, sl] = (dh * (oh * r * wv) * _silu_grad(zh)).astype(BF16)
            acc = acc + jnp.sum(dn * oh * r, axis=0, keepdims=True)
            dnw_ = dn * wv
            do_ref[:, sl] = r * dnw_ - oh * (r * r * r) * jnp.mean(dnw_ * oh, axis=-1, keepdims=True)

        @pl.when(i == 0)
        def _():
            dw_ref[...] = acc

        @pl.when(i > 0)
        def _():
            dw_ref[...] += acc

    row = pl.BlockSpec((ROW_TILE, DN_W), lambda i: (i, 0))
    vec = pl.BlockSpec((1, DN_D), lambda i: (0, 0))
    return pl.pallas_call(
        body, name="gdr_out_bwd", grid=(s // ROW_TILE,),
        in_specs=[row, pl.BlockSpec((ROW_TILE, DN_W), lambda i: (i, OFF_Z_A // DN_W)), vec, row],
        out_specs=[row, row, vec],
        out_shape=[jax.ShapeDtypeStruct((s, DN_W), F32), jax.ShapeDtypeStruct((s, DN_W), BF16),
                   jax.ShapeDtypeStruct((1, DN_D), F32)],
        compiler_params=_cparams("arbitrary"))(o, proj, dnw, dy)


def _slope(group, head):
    idx = (group * DIL_HEADS + head + 1).astype(F32)
    return jnp.exp(jnp.full((1, 128), -8.0 * math.log(2.0) / (N_DIL * DIL_HEADS), F32) * idx)


def _att_scores(qb, k_cur, k_prev, slope_d, has_prev):
    iq = lax.broadcasted_iota(jnp.int32, (ATT_BLOCK, ATT_BLOCK), 0)
    jk = lax.broadcasted_iota(jnp.int32, (ATT_BLOCK, ATT_BLOCK), 1)
    dist_c = (iq - jk).astype(F32)
    s_cur = jnp.where(iq >= jk, _dot_nt(qb, k_cur) - slope_d * dist_c, NEG)
    s_prev = jnp.where(jnp.logical_and(jk >= iq, has_prev),
                       _dot_nt(qb, k_prev) - slope_d * (dist_c + float(ATT_BLOCK)), NEG)
    return s_cur, s_prev


ATT_UNROLL = 4


def _att_blocks(i, dil, nb):
    per = dil * nb // ATT_UNROLL
    assert per * ATT_UNROLL == dil * nb
    for i0 in range(per):
        blocks = [divmod(i0 + u * per, nb) for u in range(ATT_UNROLL)]
        assert all(a[0] != b[0] or abs(a[1] - b[1]) >= 2 for n, a in enumerate(blocks) for b in blocks[n + 1:])
    curs, prvs, has_prev = [], [], []
    for u in range(ATT_UNROLL):
        t = i + u * per
        r = lax.div(t, nb)
        j = lax.rem(t, nb)
        base = r + dil * ATT_BLOCK * j
        pbase = base - dil * ATT_BLOCK * jnp.minimum(j, 1)
        if dil == 1:
            base, pbase = pl.multiple_of(base, ATT_BLOCK), pl.multiple_of(pbase, ATT_BLOCK)
        curs.append(pl.ds(base, ATT_BLOCK, stride=dil))
        prvs.append(pl.ds(pbase, ATT_BLOCK, stride=dil))
        has_prev.append(j > 0)
    return curs, prvs, has_prev


def _att_fwd(proj, group):
    s = proj.shape[0]
    dil = DIL_GROUPS[group][1]
    assert DIL_GROUPS[group][0] // dil == ATT_BLOCK
    nb = s // dil // ATT_BLOCK
    assert nb * dil * ATT_BLOCK == s

    def body(q_ref, k_ref, v_ref, num_ref, den_ref, mx_ref):
        slope_d = _slope(group, pl.program_id(0)) * float(dil)

        def step(i, carry):
            curs, prvs, has_prev = _att_blocks(i, dil, nb)
            us = range(ATT_UNROLL)
            qb = [q_ref[c, :] * (DIL_DH ** -0.5) for c in curs]
            sc = [_att_scores(qb[u], k_ref[curs[u], :], k_ref[prvs[u], :], slope_d, has_prev[u]) for u in us]
            mx = [jnp.maximum(jnp.max(a, axis=-1, keepdims=True), jnp.max(b, axis=-1, keepdims=True)) for a, b in sc]
            p_cur = [jnp.exp(sc[u][0] - mx[u]) for u in us]
            p_prev = [jnp.exp(sc[u][1] - mx[u]) for u in us]
            den = [jnp.sum(p_cur[u], axis=-1, keepdims=True) + jnp.sum(p_prev[u], axis=-1, keepdims=True) for u in us]
            num = [_dot(p_cur[u], v_ref[curs[u], :]) + _dot(p_prev[u], v_ref[prvs[u], :]) for u in us]
            for u in us:
                num_ref[curs[u], :] = num[u]
                den_ref[curs[u], :] = jnp.broadcast_to(den[u], (ATT_BLOCK, DIL_DH))
                mx_ref[curs[u], :] = jnp.broadcast_to(mx[u], (ATT_BLOCK, DIL_DH))
            return carry

        lax.fori_loop(0, dil * nb // ATT_UNROLL, step, 0)

    def col(off):
        return pl.BlockSpec((s, DIL_DH), lambda h: (0, off // DIL_DH + group * DIL_HEADS + h))

    out = pl.BlockSpec((s, DIL_DH), lambda h: (0, h))
    return pl.pallas_call(
        body, name=f"att_fwd{group}", grid=(DIL_HEADS,), in_specs=[col(OFF_Q_B), col(OFF_K_B), col(OFF_V_B)],
        out_specs=[out, out, out], out_shape=[jax.ShapeDtypeStruct((s, DIL_W), F32)] * 3,
        compiler_params=_cparams("parallel"))(proj, proj, proj)


def _att_bwd(proj, group, do, lse, delta):
    s = proj.shape[0]
    dil = DIL_GROUPS[group][1]
    nb = s // dil // ATT_BLOCK

    def body(q_ref, k_ref, v_ref, do_ref, lse_ref, dl_ref, dq_ref, dk_ref, dv_ref, dq_acc, dk_acc, dv_acc):
        slope_d = _slope(group, pl.program_id(0)) * float(dil)
        dk_acc[...] = jnp.zeros_like(dk_acc)
        dv_acc[...] = jnp.zeros_like(dv_acc)

        def step(i, carry):
            curs, prvs, has_prev = _att_blocks(i, dil, nb)
            us = range(ATT_UNROLL)
            qb = [q_ref[c, :] * (DIL_DH ** -0.5) for c in curs]
            k_cur, k_prev = [k_ref[c, :] for c in curs], [k_ref[p, :] for p in prvs]
            v_cur, v_prev = [v_ref[c, :] for c in curs], [v_ref[p, :] for p in prvs]
            sc = [_att_scores(qb[u], k_cur[u], k_prev[u], slope_d, has_prev[u]) for u in us]
            lse_b, delta_b, dob = [lse_ref[c, :] for c in curs], [dl_ref[c, :] for c in curs], [do_ref[c, :] for c in curs]
            p_cur = [jnp.exp(sc[u][0] - lse_b[u]) for u in us]
            p_prev = [jnp.exp(sc[u][1] - lse_b[u]) for u in us]
            ds_cur = [p_cur[u] * (_dot_nt(dob[u], v_cur[u]) - delta_b[u]) for u in us]
            ds_prev = [p_prev[u] * (_dot_nt(dob[u], v_prev[u]) - delta_b[u]) for u in us]
            dq = [(_dot(ds_cur[u], k_cur[u]) + _dot(ds_prev[u], k_prev[u])) * (DIL_DH ** -0.5) for u in us]
            dk_c = [_dot_tn(ds_cur[u], qb[u]) for u in us]
            dv_c = [_dot_tn(p_cur[u], dob[u]) for u in us]
            dk_p = [_dot_tn(ds_prev[u], qb[u]) for u in us]
            dv_p = [_dot_tn(p_prev[u], dob[u]) for u in us]
            for u in us:
                dq_acc[curs[u], :] = dq[u]
                dk_acc[curs[u], :] += dk_c[u]
                dv_acc[curs[u], :] += dv_c[u]
            for u in us:
                dk_acc[prvs[u], :] += dk_p[u]
                dv_acc[prvs[u], :] += dv_p[u]
            return carry

        lax.fori_loop(0, dil * nb // ATT_UNROLL, step, 0)
        dq_ref[...] = dq_acc[...].astype(BF16)
        dk_ref[...] = dk_acc[...].astype(BF16)
        dv_ref[...] = dv_acc[...].astype(BF16)

    def col(off):
        return pl.BlockSpec((s, DIL_DH), lambda h: (0, off // DIL_DH + group * DIL_HEADS + h))

    hd = pl.BlockSpec((s, DIL_DH), lambda h: (0, h))
    return pl.pallas_call(
        body, name=f"att_bwd{group}", grid=(DIL_HEADS,),
        in_specs=[col(OFF_Q_B), col(OFF_K_B), col(OFF_V_B), hd, hd, hd], out_specs=[hd, hd, hd],
        out_shape=[jax.ShapeDtypeStruct((s, DIL_W), BF16)] * 3,
        scratch_shapes=[pltpu.VMEM((s, DIL_DH), F32)] * 3,
        compiler_params=_cparams("parallel"))(proj, proj, proj, do, lse, delta)


def _att_merge(parts, proj):
    s = proj.shape[0]

    def body(n0, d0, m0, n1, d1, m1, n2, d2, m2, z_ref, ob_ref, o_ref, lse_ref, obt_ref):
        m = jnp.maximum(jnp.maximum(m0[...], m1[...]), m2[...])
        num = jnp.zeros_like(m)
        den = jnp.zeros_like(m)
        for nr, dr, mr in ((n0, d0, m0), (n1, d1, m1), (n2, d2, m2)):
            sc = jnp.exp(mr[...] - m)
            num = num + nr[...] * sc
            den = den + dr[...] * sc
        o = num / den
        o_ref[...] = o
        lse_ref[...] = m + jnp.log(den)
        ob = o * _silu(z_ref[...])
        ob_ref[...] = ob.astype(BF16)
        obt_ref[...] = ob.T.astype(BF16)

    row = pl.BlockSpec((ROW_TILE, DIL_W), lambda i: (i, 0))
    flat = [a for p in parts for a in p]
    return pl.pallas_call(
        body, name="att_merge", grid=(s // ROW_TILE,),
        in_specs=[row] * 9 + [pl.BlockSpec((ROW_TILE, DIL_W), lambda i: (i, OFF_Z_B // DIL_W))],
        out_specs=[row, row, row, pl.BlockSpec((DIL_W, ROW_TILE), lambda i: (0, i))],
        out_shape=[jax.ShapeDtypeStruct((s, DIL_W), BF16), jax.ShapeDtypeStruct((s, DIL_W), F32),
                   jax.ShapeDtypeStruct((s, DIL_W), F32), jax.ShapeDtypeStruct((DIL_W, s), BF16)],
        compiler_params=_cparams("parallel"))(*flat, proj)


def _att_merge_bwd(o, proj, dob):
    s = o.shape[0]

    def body(o_ref, z_ref, d_ref, do_ref, dl_ref, dz_ref):
        ov, zv, dv = o_ref[...], z_ref[...], d_ref[...]
        do = dv * _silu(zv)
        do_ref[...] = do
        dz_ref[...] = (dv * ov * _silu_grad(zv)).astype(BF16)
        for h in range(DIL_HEADS):
            sl = slice(h * DIL_DH, (h + 1) * DIL_DH)
            dl_ref[:, sl] = jnp.broadcast_to(jnp.sum(do[:, sl] * ov[:, sl], axis=-1, keepdims=True), (ROW_TILE, DIL_DH))

    row = pl.BlockSpec((ROW_TILE, DIL_W), lambda i: (i, 0))
    return pl.pallas_call(
        body, name="att_merge_bwd", grid=(s // ROW_TILE,),
        in_specs=[row, pl.BlockSpec((ROW_TILE, DIL_W), lambda i: (i, OFF_Z_B // DIL_W)), row],
        out_specs=[row, row, row],
        out_shape=[jax.ShapeDtypeStruct((s, DIL_W), F32), jax.ShapeDtypeStruct((s, DIL_W), F32),
                   jax.ShapeDtypeStruct((s, DIL_W), BF16)],
        compiler_params=_cparams("parallel"))(o, proj, dob)


def _merge(proj, ya, yb):
    s = proj.shape[0]

    def body(ga_ref, gb_ref, ya_ref, yb_ref, o_ref, ot_ref):
        m = _sigmoid(ga_ref[...]) * ya_ref[...] + _sigmoid(gb_ref[...]) * yb_ref[...]
        o_ref[...] = m.astype(BF16)
        ot_ref[...] = m.T.astype(BF16)

    row = pl.BlockSpec((ROW_TILE, D_MODEL), lambda i: (i, 0))
    return pl.pallas_call(
        body, name="merge", grid=(s // ROW_TILE,),
        in_specs=[pl.BlockSpec((ROW_TILE, D_MODEL), lambda i: (i, OFF_G_A // D_MODEL)),
                  pl.BlockSpec((ROW_TILE, D_MODEL), lambda i: (i, OFF_G_B // D_MODEL)), row, row],
        out_specs=[row, pl.BlockSpec((D_MODEL, ROW_TILE), lambda i: (0, i))],
        out_shape=[jax.ShapeDtypeStruct((s, D_MODEL), BF16), jax.ShapeDtypeStruct((D_MODEL, s), BF16)],
        compiler_params=_cparams("parallel"))(proj, proj, ya, yb)


def _merge_bwd(proj, ya, yb, dm):
    s = proj.shape[0]

    def body(ga_ref, gb_ref, ya_ref, yb_ref, dm_ref, dya_ref, dyb_ref, dga_ref, dgb_ref):
        dmv = dm_ref[...]
        sa, sb = _sigmoid(ga_ref[...]), _sigmoid(gb_ref[...])
        dya_ref[...] = (dmv * sa).astype(BF16)
        dyb_ref[...] = (dmv * sb).astype(BF16)
        dga_ref[...] = (dmv * ya_ref[...] * sa * (1.0 - sa)).astype(BF16)
        dgb_ref[...] = (dmv * yb_ref[...] * sb * (1.0 - sb)).astype(BF16)

    row = pl.BlockSpec((ROW_TILE, D_MODEL), lambda i: (i, 0))
    return pl.pallas_call(
        body, name="merge_bwd", grid=(s // ROW_TILE,),
        in_specs=[pl.BlockSpec((ROW_TILE, D_MODEL), lambda i: (i, OFF_G_A // D_MODEL)),
                  pl.BlockSpec((ROW_TILE, D_MODEL), lambda i: (i, OFF_G_B // D_MODEL)), row, row, row],
        out_specs=[row] * 4, out_shape=[jax.ShapeDtypeStruct((s, D_MODEL), BF16)] * 4,
        compiler_params=_cparams("parallel"))(proj, proj, ya, yb, dm)


def _final(x, t, fw, tgt):
    s, d = x.shape

    def body(x_ref, t_ref, w_ref, y_ref, dx_ref, dw_ref, l_ref):
        i = pl.program_id(0)
        x2 = x_ref[...] + t_ref[...]
        wv = w_ref[...]
        r = lax.rsqrt(jnp.mean(x2 * x2, axis=-1, keepdims=True) + NORM_EPS)
        e = x2 * r * wv - y_ref[...]
        lrow = jnp.mean(e * e, axis=-1, keepdims=True)
        lpart = jnp.broadcast_to(0.5 * jnp.sum(lrow, axis=0, keepdims=True), (1, 128))
        dy = e * (1.0 / d)
        dwp = jnp.sum(dy * x2 * r, axis=0, keepdims=True)
        dyw = dy * wv
        dx_ref[...] = r * dyw - x2 * (r * r * r) * jnp.mean(dyw * x2, axis=-1, keepdims=True)

        @pl.when(i == 0)
        def _():
            dw_ref[...] = dwp
            l_ref[...] = lpart

        @pl.when(i > 0)
        def _():
            dw_ref[...] += dwp
            l_ref[...] += lpart

    row = pl.BlockSpec((ROW_TILE, d), lambda i: (i, 0))
    vec = pl.BlockSpec((1, d), lambda i: (0, 0))
    return pl.pallas_call(
        body, name="final", grid=(s // ROW_TILE,), in_specs=[row, row, vec, row],
        out_specs=[row, vec, pl.BlockSpec((1, 128), lambda i: (0, 0))],
        out_shape=[jax.ShapeDtypeStruct((s, d), F32), jax.ShapeDtypeStruct((1, d), F32), jax.ShapeDtypeStruct((1, 128), F32)],
        compiler_params=_cparams("arbitrary"))(x, t, fw, tgt)


def _adamw(w, g, m, v, name):
    r, c = w.shape
    cap = max(8, (1 << 18) // c)
    tr = r if r <= 8 else max(t for t in range(8, min(r, cap) + 1, 8) if r % t == 0)

    def body(w_ref, g_ref, m_ref, v_ref, d_ref, nm_ref, nv_ref):
        gv = g_ref[...]
        mn = ADAM_B1 * m_ref[...] + (1.0 - ADAM_B1) * gv
        vn = ADAM_B2 * v_ref[...] + (1.0 - ADAM_B2) * (gv * gv)
        m_hat = mn / (1.0 - ADAM_B1 ** ADAM_STEP)
        v_hat = vn / (1.0 - ADAM_B2 ** ADAM_STEP)
        d_ref[...] = -ADAM_LR * (m_hat / (jnp.sqrt(v_hat) + ADAM_EPS) + ADAM_WD * w_ref[...])
        nm_ref[...] = mn
        nv_ref[...] = vn

    blk = pl.BlockSpec((tr, c), lambda i: (i, 0))
    return pl.pallas_call(
        body, name=name, grid=(r // tr,), in_specs=[blk] * 4, out_specs=[blk] * 3,
        out_shape=[jax.ShapeDtypeStruct((r, c), F32)] * 3, compiler_params=_cparams("parallel"))(w, g, m, v)


HBM_SPEC = pl.BlockSpec(memory_space=pl.ANY)


def _place():
    x, y, c = lax.axis_index("x"), lax.axis_index("y"), lax.axis_index("c")
    chips = [(1 - x, y), (x, 1 - y), (1 - x, 1 - y)]
    return x, y, c, chips


def _ag_weights(packs):
    na = len(packs)

    def body(*refs):
        p_refs, out_refs = refs[:na], refs[na:2 * na]
        send_sems, recv_sems = refs[2 * na:]
        x, y, c, chips = _place()
        me, sib, j = (x, y, c), (x, y, 1 - c), 2 * x + y

        def rc(k, src, dst, to):
            return pltpu.make_async_remote_copy(src_ref=src, dst_ref=dst, send_sem=send_sems.at[k],
                                                recv_sem=recv_sems.at[k], device_id=to, device_id_type=MESH)

        first = [rc(6 * a + k, p_refs[a].at[c], out_refs[a].at[j, c], (cx, cy, c))
                 for a in range(na) for k, (cx, cy) in enumerate(chips)]
        for cp in first:
            cp.start()
        passed = []
        for a in range(na):
            for k, (cx, cy) in enumerate(chips):
                land = out_refs[a].at[2 * cx + cy, c]
                rc(6 * a + k, p_refs[a].at[c], land, me).wait_recv()
                fwd = rc(6 * a + 3 + k, land, land, sib)
                fwd.start()
                passed.append(fwd)
        for a in range(na):
            for k, (cx, cy) in enumerate(chips):
                rc(6 * a + 3 + k, p_refs[a].at[c], out_refs[a].at[2 * cx + cy, 1 - c], me).wait_recv()
        for cp in first + passed:
            cp.wait_send()

    return pl.pallas_call(
        body, name="ag_weights",
        out_shape=[jax.ShapeDtypeStruct((N_CHIPS,) + p.shape, p.dtype) for p in packs],
        in_specs=[HBM_SPEC] * na, out_specs=[HBM_SPEC] * na,
        scratch_shapes=[pltpu.SemaphoreType.DMA((6 * na,)), pltpu.SemaphoreType.DMA((6 * na,))])(*packs)


def _rs_pair(gpacks):
    na = len(gpacks)
    n = N_CHIPS

    def body(*refs):
        g_refs, out_refs = refs[:na], refs[na:2 * na]
        send_sems, recv_sems = refs[2 * na:]
        x, y, c, _ = _place()
        sib = (x, y, 1 - c)
        cps = [pltpu.make_async_remote_copy(src_ref=g_refs[a].at[p, 1 - c], dst_ref=out_refs[a].at[p],
                                            send_sem=send_sems.at[n * a + p], recv_sem=recv_sems.at[n * a + p],
                                            device_id=sib, device_id_type=MESH)
               for a in range(na) for p in range(n)]
        for cp in cps:
            cp.start()
        for cp in cps:
            cp.wait_recv()
        for cp in cps:
            cp.wait_send()

    return pl.pallas_call(
        body, name="rs_pair",
        out_shape=[jax.ShapeDtypeStruct((n,) + g.shape[2:], g.dtype) for g in gpacks],
        in_specs=[HBM_SPEC] * na, out_specs=[HBM_SPEC] * na,
        scratch_shapes=[pltpu.SemaphoreType.DMA((n * na,)), pltpu.SemaphoreType.DMA((n * na,))])(*gpacks)


def _rs_chips(csums):
    na = len(csums)

    def body(*refs):
        s_refs, out_refs = refs[:na], refs[na:2 * na]
        send_sems, recv_sems = refs[2 * na:]
        x, y, c, chips = _place()
        j = 2 * x + y
        cps = [pltpu.make_async_remote_copy(src_ref=s_refs[a].at[2 * cx + cy], dst_ref=out_refs[a].at[j],
                                            send_sem=send_sems.at[3 * a + k], recv_sem=recv_sems.at[3 * a + k],
                                            device_id=(cx, cy, c), device_id_type=MESH)
               for a in range(na) for k, (cx, cy) in enumerate(chips)]
        for cp in cps:
            cp.start()
        for a in range(na):
            for k, (cx, cy) in enumerate(chips):
                pltpu.make_async_remote_copy(src_ref=s_refs[a].at[j], dst_ref=out_refs[a].at[2 * cx + cy],
                                             send_sem=send_sems.at[3 * a + k], recv_sem=recv_sems.at[3 * a + k],
                                             device_id=(x, y, c), device_id_type=MESH).wait_recv()
        for cp in cps:
            cp.wait_send()

    return pl.pallas_call(
        body, name="rs_chips", out_shape=[jax.ShapeDtypeStruct(s.shape, s.dtype) for s in csums],
        in_specs=[HBM_SPEC] * na, out_specs=[HBM_SPEC] * na,
        scratch_shapes=[pltpu.SemaphoreType.DMA((3 * na,)), pltpu.SemaphoreType.DMA((3 * na,))])(*csums)


SWAP_CHUNKS = 4


def _pair_swap(halves):
    na = len(halves)

    def body(*refs):
        h_refs, out_refs = refs[:na], refs[na:2 * na]
        send_sems, recv_sems = refs[2 * na:]
        x, y, c, _ = _place()
        cps = []
        for a in range(na):
            rows = h_refs[a].shape[0] // SWAP_CHUNKS
            assert rows * SWAP_CHUNKS == h_refs[a].shape[0]
            for q in range(SWAP_CHUNKS):
                k = SWAP_CHUNKS * a + q
                cps.append(pltpu.make_async_remote_copy(
                    src_ref=h_refs[a].at[pl.ds(q * rows, rows)], dst_ref=out_refs[a].at[pl.ds(q * rows, rows)],
                    send_sem=send_sems.at[k], recv_sem=recv_sems.at[k], device_id=(x, y, 1 - c), device_id_type=MESH))
        for cp in cps:
            cp.start()
        for cp in cps:
            cp.wait_recv()
        for cp in cps:
            cp.wait_send()

    return pl.pallas_call(
        body, name="pair_swap", out_shape=[jax.ShapeDtypeStruct(h.shape, h.dtype) for h in halves],
        in_specs=[HBM_SPEC] * na, out_specs=[HBM_SPEC] * na,
        scratch_shapes=[pltpu.SemaphoreType.DMA((SWAP_CHUNKS * na,)), pltpu.SemaphoreType.DMA((SWAP_CHUNKS * na,))])(*halves)


def _ag_small(v):
    m_per, n = v.shape

    def body(x_ref, out_ref, send_sems, recv_sems, local_sem):
        x, y, c, chips = _place()
        me, sibling = (x, y, c), (x, y, 1 - c)

        def rows(px, py, pc):
            return out_ref.at[pl.ds((4 * px + 2 * py + pc) * m_per, m_per), :]

        def copy(k, block, to, src=None):
            return pltpu.make_async_remote_copy(
                src_ref=rows(*block) if src is None else src, dst_ref=rows(*block), send_sem=send_sems.at[k],
                recv_sem=recv_sems.at[k], device_id=to, device_id_type=MESH)

        mine = pltpu.make_async_copy(x_ref, rows(*me), local_sem)
        mine.start()
        first = [copy(0, me, sibling, src=x_ref)]
        first += [copy(1 + k, me, (*chip, c), src=x_ref) for k, chip in enumerate(chips)]
        for cp in first:
            cp.start()
        passed = [copy(4 + k, (*chip, c), sibling) for k, chip in enumerate(chips)]
        for k, chip in enumerate(chips):
            copy(1 + k, (*chip, c), me).wait_recv()
            passed[k].start()
        copy(0, sibling, me).wait_recv()
        for k, chip in enumerate(chips):
            copy(4 + k, (*chip, 1 - c), me).wait_recv()
        for cp in first + passed:
            cp.wait_send()
        mine.wait()

    return pl.pallas_call(
        body, name="ag_small", out_shape=jax.ShapeDtypeStruct((8 * m_per, n), v.dtype),
        in_specs=[pl.BlockSpec(memory_space=pltpu.VMEM)], out_specs=pl.BlockSpec(memory_space=pltpu.VMEM),
        scratch_shapes=[pltpu.SemaphoreType.DMA((7,)), pltpu.SemaphoreType.DMA((7,)), pltpu.SemaphoreType.DMA])(v)


def _sum_blocks(a, nblk, name):
    rows, wd = a.shape
    r = rows // nblk
    tr = min(r, ROW_TILE)
    assert r % tr == 0

    def body(*refs):
        acc = refs[0][...].astype(F32)
        for ref in refs[1:nblk]:
            acc = acc + ref[...].astype(F32)
        refs[nblk][...] = acc

    nt = r // tr
    return pl.pallas_call(
        body, name=name, grid=(nt,),
        in_specs=[pl.BlockSpec((tr, wd), functools.partial(lambda i, b: (b * nt + i, 0), b=b)) for b in range(nblk)],
        out_specs=pl.BlockSpec((tr, wd), lambda i: (i, 0)),
        out_shape=jax.ShapeDtypeStruct((r, wd), F32), compiler_params=_cparams("parallel"))(*([a] * nblk))


def _row_tile(rows):
    best = max(t for t in range(16, 513, 16) if rows % t == 0)
    return best


def _sum_chips(by_src, csum, j, name):
    n, rh, wd = by_src.shape
    tr = _row_tile(rh)

    def body(j_ref, *refs):
        own = refs[n][0].astype(F32)
        acc = None
        for k in range(n):
            term = jnp.where(j_ref[0] == k, own, refs[k][0].astype(F32))
            acc = term if acc is None else acc + term
        refs[n + 1][...] = acc

    def other(k):
        return pl.BlockSpec((1, tr, wd), lambda i, jr: (jnp.where(jr[0] == k, (k + 1) % n, k), i, 0))

    grid_spec = pltpu.PrefetchScalarGridSpec(
        num_scalar_prefetch=1, grid=(rh // tr,),
        in_specs=[other(k) for k in range(n)] + [pl.BlockSpec((1, tr, wd), lambda i, jr: (jr[0], i, 0))],
        out_specs=pl.BlockSpec((tr, wd), lambda i, jr: (i, 0)))
    return pl.pallas_call(
        body, name=name, grid_spec=grid_spec, out_shape=jax.ShapeDtypeStruct((rh, wd), F32),
        compiler_params=_cparams("parallel"))(jnp.reshape(j, (1,)).astype(jnp.int32), *([by_src] * n), csum)


def _add_halves(gpack, other, c, name):
    n, _, rh, wd = gpack.shape
    tr = _row_tile(rh)

    def body(c_ref, g_ref, o_ref, out_ref):
        out_ref[0] = (g_ref[0, 0] + o_ref[0]).astype(BF16)

    grid_spec = pltpu.PrefetchScalarGridSpec(
        num_scalar_prefetch=1, grid=(n, rh // tr),
        in_specs=[pl.BlockSpec((1, 1, tr, wd), lambda p, i, cr: (p, cr[0], i, 0)),
                  pl.BlockSpec((1, tr, wd), lambda p, i, cr: (p, i, 0))],
        out_specs=pl.BlockSpec((1, tr, wd), lambda p, i, cr: (p, i, 0)))
    return pl.pallas_call(
        body, name=name, grid_spec=grid_spec, out_shape=jax.ShapeDtypeStruct((n, rh, wd), BF16),
        compiler_params=_cparams("parallel", "parallel"))(jnp.reshape(c, (1,)).astype(jnp.int32), gpack, other)


PACK_W = 1024
ROWS_O_DN = DN_W // N_CHIPS
ROWS_O_DIL = DIL_W * (D_MODEL // N_CHIPS) // PACK_W
ROWS_OUT = D_MODEL // N_CHIPS
ROWS_CONV = 4 * (3 * DN_W // N_CHIPS) // PACK_W
R1 = ROWS_O_DN
R2 = R1 + ROWS_O_DIL
R3 = R2 + ROWS_OUT
R4 = R3 + ROWS_CONV
R5 = R4 + ROWS_CONV
PACK_ROWS = 1024
HALF_ROWS = PACK_ROWS // 2
SHARD_PAD = 2880


def _to_ref_layout(wpt):
    return jnp.concatenate([wpt[:REF_OFF_BA], wpt[OFF_BA:OFF_BA + 2 * DN_HEADS], wpt[REF_OFF_BA:OFF_BA]], axis=0)


def _from_ref_layout(wt):
    pad = jnp.zeros((PW - PROJ_W, wt.shape[1]), wt.dtype)
    return jnp.concatenate([wt[:REF_OFF_BA], wt[REF_OFF_BA + 2 * DN_HEADS:], wt[REF_OFF_BA:REF_OFF_BA + 2 * DN_HEADS], pad],
                           axis=0)


def _local_step(x, tgt, norm_w, wpt, conv_full, a_log, dt_bias, dn_norm_w, w_o_dn, w_o_dil, w_out, final_norm_w):
    s = x.shape[0]
    h, h_t = _rms_in(x, norm_w)
    proj = _matmul(h, wpt, F32, 2048, 1280, 1024, "proj", nt=True)
    c_pre, qkv = _conv_fwd(proj, conv_full)
    gate_par = jnp.zeros((8, 128), F32).at[0, 8:16].set(a_log[0]).at[1, 8:16].set(dt_bias[0])
    bg = _gates_fwd(proj, gate_par)
    o_a, u, w, vn, tmat, states = _gdr_fwd(qkv, bg)
    oa2, oa2_t = _gdr_out(o_a, proj, dn_norm_w)
    ya = _matmul(oa2, w_o_dn, F32, 512, 1024, 1024, "ya")
    parts = [_att_fwd(proj, g) for g in range(N_DIL)]
    ob, o_att, lse, ob_t = _att_merge(parts, proj)
    yb = _matmul(ob, w_o_dil, F32, 512, 1024, 512, "yb")
    mg, mg_t = _merge(proj, ya, yb)
    t = _matmul(mg, w_out, F32, 512, 1024, 1024, "t_out")
    dx2, dfw, lpart = _final(x, t, final_norm_w, tgt)

    dmg = _matmul(dx2, w_out, F32, 512, 1024, 1024, "d_merged", nt=True)
    dw_out = _matmul(mg_t, dx2, F32, 1024, 1024, 1024, "dw_out")
    dya, dyb, dga, dgb = _merge_bwd(proj, ya, yb, dmg)
    doa2 = _matmul(dya, w_o_dn, F32, 512, 1024, 1024, "d_oa2", nt=True)
    dw_o_dn = _matmul(oa2_t, dya, F32, 1024, 1024, 1024, "dw_o_dn")
    dob = _matmul(dyb, w_o_dil, F32, 512, 512, 1024, "d_ob", nt=True)
    dw_o_dil = _matmul(ob_t, dyb, F32, 512, 1024, 1024, "dw_o_dil")
    do_a, dz_a, ddnw = _gdr_out_bwd(o_a, proj, dn_norm_w, doa2)
    dq_a, dk_a, dv_a, dbg = _gdr_bwd(qkv, bg, u, w, vn, tmat, states, do_a)
    dba, dpar = _gates_bwd(proj, gate_par, dbg)
    dc = _conv_bwd_act(c_pre, dq_a, dk_a, dv_a)
    du_a, dconv = _conv_bwd(proj, dc, conv_full)
    do_att, delta, dz_b = _att_merge_bwd(o_att, proj, dob)
    dqkv_b = [_att_bwd(proj, g, do_att, lse, delta) for g in range(N_DIL)]
    dproj = jnp.concatenate(
        [du_a, dz_a] + [dqkv_b[g][i] for i in range(3) for g in range(N_DIL)]
        + [dz_b, dga, dgb, dba, jnp.zeros((s, PW - OFF_BA - 128), BF16)], axis=1)
    dh = _matmul(dproj, wpt, F32, 1024, 1024, 2304, "d_h")
    dwpt = _matmul(h_t, dproj, F32, 1024, 1280, 1024, "dw_in", transpose_out=True)
    grad_x, dnw = _rms_in_bwd(x, norm_w, dh, dx2)
    small = jnp.zeros((8, PACK_W), F32)
    small = small.at[0].set(dnw[0]).at[1].set(dfw[0]).at[2, :DN_D].set(ddnw[0])
    small = small.at[3, :DN_HEADS].set(dpar[0, 8:16]).at[3, DN_HEADS:2 * DN_HEADS].set(dpar[1, 8:16])
    small = small.at[4, 0].set(lpart[0, 0])
    return grad_x, dwpt, dconv, dw_o_dn, dw_o_dil, dw_out, small


def kernel(x, norm_w, w_in, conv_w, a_log, dt_bias, dn_norm_w, w_o_dn, w_o_dil, w_out, final_norm_w, loss_target, m_norm_w, m_w_in, m_conv_w, m_a_log, m_dt_bias, m_dn_norm_w, m_w_o_dn, m_w_o_dil, m_w_out, m_final_norm_w, v_norm_w, v_w_in, v_conv_w, v_a_log, v_dt_bias, v_dn_norm_w, v_w_o_dn, v_w_o_dil, v_w_out, v_final_norm_w):
    c = lax.axis_index("c")
    j = 2 * lax.axis_index("x") + lax.axis_index("y")
    qw = D_MODEL // N_CHIPS

    cw = conv_w[0].reshape(ROWS_CONV, PACK_W)
    cw_hi = cw.astype(BF16)
    cw_lo = (cw - cw_hi.astype(F32)).astype(BF16)
    pack = jnp.concatenate(
        [w_o_dn[0].astype(BF16), w_o_dil[0].astype(BF16).reshape(ROWS_O_DIL, PACK_W), w_out[0].astype(BF16), cw_hi, cw_lo,
         jnp.zeros((PACK_ROWS - R5, PACK_W), BF16)], axis=0).reshape(2, HALF_ROWS, PACK_W)
    own_in = jnp.pad(w_in[0].T.astype(BF16), ((0, SHARD_PAD - SHARD_W), (0, 0))).reshape(2, SHARD_PAD // 2, D_MODEL)
    all_in, allw = _ag_weights([own_in, pack])
    chips = range(N_CHIPS)
    all_in = [jnp.where(j == k, own_in, all_in[k]).reshape(SHARD_PAD, D_MODEL)[:SHARD_W] for k in chips]
    allw = [jnp.where(j == k, pack, allw[k]).reshape(PACK_ROWS, PACK_W) for k in chips]
    wpt = _from_ref_layout(jnp.concatenate(all_in, axis=0))
    w_o_dn_full = jnp.concatenate([allw[k][:R1] for k in chips], axis=0)
    w_o_dil_full = jnp.concatenate([allw[k][R1:R2].reshape(DIL_W, qw) for k in chips], axis=1)
    w_out_full = jnp.concatenate([allw[k][R2:R3] for k in chips], axis=0)
    conv_full = jnp.concatenate(
        [(allw[k][R3:R4].astype(F32) + allw[k][R4:R5].astype(F32)).reshape(4, 3 * DN_W // N_CHIPS) for k in chips], axis=1)

    grad_x, dwpt, dconv, dw_o_dn, dw_o_dil, dw_out, small = _local_step(
        x[0], loss_target[0], norm_w, wpt, conv_full, a_log, dt_bias, dn_norm_w, w_o_dn_full, w_o_dil_full, w_out_full,
        final_norm_w.reshape(1, D_MODEL))

    cq = 3 * DN_W // N_CHIPS
    dw_in_t = _to_ref_layout(dwpt)
    g_in = jnp.stack([jnp.pad(dw_in_t[k * SHARD_W:(k + 1) * SHARD_W], ((0, SHARD_PAD - SHARD_W), (0, 0))) for k in chips])
    g_in = g_in.reshape(N_CHIPS, 2, SHARD_PAD // 2, D_MODEL)
    gpack = jnp.stack([
        jnp.concatenate(
            [dw_o_dn[k * qw:(k + 1) * qw], dw_o_dil[:, k * qw:(k + 1) * qw].reshape(ROWS_O_DIL, PACK_W),
             dw_out[k * qw:(k + 1) * qw], dconv[:, k * cq:(k + 1) * cq].reshape(ROWS_CONV, PACK_W),
             jnp.zeros((PACK_ROWS - R4, PACK_W), F32)], axis=0)
        for k in chips]).reshape(N_CHIPS, 2, HALF_ROWS, PACK_W)
    sib_in, sib_pack = _rs_pair([g_in, gpack])
    csum_in = _add_halves(g_in, sib_in, c, "add_halves_in")
    csum_pack = _add_halves(gpack, sib_pack, c, "add_halves_pack")
    src_in, src_pack = _rs_chips([csum_in, csum_pack])
    half_in = _sum_chips(src_in, csum_in, j, "sum_chips_in")
    half_pack = _sum_chips(src_pack, csum_pack, j, "sum_chips_pack")
    sib_half_in, sib_half_pack = _pair_swap([half_in, half_pack])

    def both_halves(mine, theirs):
        return jnp.where(c == 0, jnp.concatenate([mine, theirs], axis=0), jnp.concatenate([theirs, mine], axis=0))

    lin = (SHARD_W * D_MODEL // 128, 128)
    g_w_in = both_halves(half_in, sib_half_in)[:SHARD_W].reshape(lin)
    g = both_halves(half_pack, sib_half_pack)
    g_w_o_dn = g[:R1]
    g_w_o_dil = g[R1:R2].reshape(DIL_W, qw)
    g_w_out = g[R2:R3]
    g_conv = g[R3:R4].reshape(4, cq)

    gs = _sum_blocks(_ag_small(small), 8, "sum_small")
    loss = gs[4, 0]
    w_small = jnp.zeros((8, PACK_W), F32)

    def pack_small(nw, fw, dnw_, al, db):
        t = w_small.at[0].set(nw[0]).at[1].set(fw).at[2, :DN_D].set(dnw_[0])
        return t.at[3, :DN_HEADS].set(al[0]).at[3, DN_HEADS:2 * DN_HEADS].set(db[0])

    sm = _adamw(pack_small(norm_w, final_norm_w, dn_norm_w, a_log, dt_bias), gs,
                pack_small(m_norm_w, m_final_norm_w, m_dn_norm_w, m_a_log, m_dt_bias),
                pack_small(v_norm_w, v_final_norm_w, v_dn_norm_w, v_a_log, v_dt_bias), "adamw_small")

    def unpack_small(t):
        return dict(norm_w=t[0:1], final_norm_w=t[1], dn_norm_w=t[2:3, :DN_D], a_log=t[3:4, :DN_HEADS],
                    dt_bias=t[3:4, DN_HEADS:2 * DN_HEADS])

    res = {"grad": unpack_small(gs)}
    for kind, arr in zip(("delta", "new_m", "new_v"), sm):
        res[kind] = unpack_small(arr)
    big = dict(conv_w=(conv_w, g_conv, m_conv_w, v_conv_w), w_o_dn=(w_o_dn, g_w_o_dn, m_w_o_dn, v_w_o_dn),
               w_o_dil=(w_o_dil, g_w_o_dil, m_w_o_dil, v_w_o_dil), w_out=(w_out, g_w_out, m_w_out, v_w_out))
    for name, (wt, gt, mt, vt) in big.items():
        d, nm, nv = _adamw(wt[0], gt, mt[0], vt[0], "adamw_" + name)
        res["grad"][name] = gt[None]
        res["delta"][name], res["new_m"][name], res["new_v"][name] = d[None], nm[None], nv[None]

    def to_lin(a):
        return a[0].T.reshape(lin)

    def from_lin(a):
        return a.reshape(SHARD_W, D_MODEL).T[None]

    d, nm, nv = _adamw(to_lin(w_in), g_w_in, to_lin(m_w_in), to_lin(v_w_in), "adamw_w_in")
    res["grad"]["w_in"] = from_lin(g_w_in)
    res["delta"]["w_in"], res["new_m"]["w_in"], res["new_v"]["w_in"] = from_lin(d), from_lin(nm), from_lin(nv)
    order = ["norm_w", "w_in", "conv_w", "a_log", "dt_bias", "dn_norm_w", "w_o_dn", "w_o_dil", "w_out", "final_norm_w"]
    outs = [loss, grad_x[None]]
    for kind in ("grad", "delta", "new_m", "new_v"):
        outs += [res[kind][nm] for nm in order]
    return tuple(outs)
```

```python
import functools
import math

import jax
import jax.numpy as jnp
from jax import lax
from jax.experimental import pallas as pl
from jax.experimental.pallas import tpu as pltpu

F32 = jnp.float32
BF16 = jnp.bfloat16
MESH = pl.DeviceIdType.MESH

D_MODEL = 1024
DN_HEADS = 8
DN_D = 128
DN_CHUNK = 64
DN_W = DN_HEADS * DN_D
DIL_GROUPS = ((128, 1), (512, 4), (2048, 16))
N_DIL = len(DIL_GROUPS)
DIL_HEADS = 4
DIL_DH = 128
DIL_W = DIL_HEADS * DIL_DH
ATT_BLOCK = 128
NORM_EPS = 1e-6
PROJ_W = 11280
N_CHIPS = 4
SHARD_W = PROJ_W // N_CHIPS

OFF_QKV_A = 0
OFF_Z_A = 3072
OFF_Q_B = 4096
OFF_K_B = 5632
OFF_V_B = 7168
OFF_Z_B = 8704
OFF_G_A = 9216
OFF_G_B = 10240
OFF_BA = 11264
PW = 11520
REF_OFF_BA = 4096

ADAM_LR = 0.001
ADAM_B1 = 0.9
ADAM_B2 = 0.999
ADAM_EPS = 1e-08
ADAM_WD = 0.01
ADAM_STEP = 10

ROW_TILE = 256
NEG = -1e30


def _dot(a, b):
    return jnp.dot(a.astype(BF16), b.astype(BF16), preferred_element_type=F32)


def _dot_nt(a, b):
    return lax.dot_general(a.astype(BF16), b.astype(BF16), (((1,), (1,)), ((), ())), preferred_element_type=F32)


def _dot_tn(a, b):
    return lax.dot_general(a.astype(BF16), b.astype(BF16), (((0,), (0,)), ((), ())), preferred_element_type=F32)


def _split(a):
    hi = a.astype(BF16)
    lo = (a - hi.astype(F32)).astype(BF16)
    return hi, lo


def _dot_exact_lhs(c, a):
    hi, lo = _split(a)
    cb = c.astype(BF16)
    return jnp.dot(cb, hi, preferred_element_type=F32) + jnp.dot(cb, lo, preferred_element_type=F32)


def _dot_exact_rhs(a, c):
    hi, lo = _split(a)
    cb = c.astype(BF16)
    return jnp.dot(hi, cb, preferred_element_type=F32) + jnp.dot(lo, cb, preferred_element_type=F32)


def _dot_tn_exact_rhs(a, c):
    hi, lo = _split(a)
    cb = c.astype(BF16)
    dn = (((0,), (0,)), ((), ()))
    return (lax.dot_general(hi, cb, dn, preferred_element_type=F32)
            + lax.dot_general(lo, cb, dn, preferred_element_type=F32))


def _sigmoid(x):
    return 1.0 / (1.0 + jnp.exp(-x))


def _silu(x):
    return x * _sigmoid(x)


def _silu_grad(x):
    s = _sigmoid(x)
    return s * (1.0 + x * (1.0 - s))


def _softplus(x):
    return jnp.maximum(x, 0.0) + jnp.log(1.0 + jnp.exp(-jnp.abs(x)))


def _cparams(*sem):
    return pltpu.CompilerParams(dimension_semantics=sem)


def _matmul(a, b, out_dtype, tm, tn, tk, name, nt=False, transpose_out=False):
    m, kdim = a.shape
    n = b.shape[0] if nt else b.shape[1]
    tm, tn, tk = min(tm, m), min(tn, n), min(tk, kdim)
    assert m % tm == 0 and n % tn == 0 and kdim % tk == 0, (name, a.shape, b.shape, tm, tn, tk)
    nk = kdim // tk
    dot = _dot_nt if nt else _dot
    b_spec = (pl.BlockSpec((tn, tk), lambda i, j, k: (j, k)) if nt else pl.BlockSpec((tk, tn), lambda i, j, k: (k, j)))

    def emit(o_ref, acc):
        o_ref[...] = (acc.T if transpose_out else acc).astype(o_ref.dtype)

    if nk == 1:
        def body(a_ref, b_ref, o_ref):
            emit(o_ref, dot(a_ref[...], b_ref[...]))
        scratch = []
    else:
        def body(a_ref, b_ref, o_ref, acc_ref):
            k = pl.program_id(2)
            p = dot(a_ref[...], b_ref[...])

            @pl.when(k == 0)
            def _():
                acc_ref[...] = p

            @pl.when(k > 0)
            def _():
                acc_ref[...] += p

            @pl.when(k == nk - 1)
            def _():
                emit(o_ref, acc_ref[...])
        scratch = [pltpu.VMEM((tm, tn), F32)]

    if transpose_out:
        out_spec, out_shape = pl.BlockSpec((tn, tm), lambda i, j, k: (j, i)), (n, m)
    else:
        out_spec, out_shape = pl.BlockSpec((tm, tn), lambda i, j, k: (i, j)), (m, n)
    return pl.pallas_call(
        body, name=name, grid=(m // tm, n // tn, nk), in_specs=[pl.BlockSpec((tm, tk), lambda i, j, k: (i, k)), b_spec],
        out_specs=out_spec, out_shape=jax.ShapeDtypeStruct(out_shape, out_dtype), scratch_shapes=scratch,
        compiler_params=_cparams("parallel", "parallel", "arbitrary"))(a, b)


def _rms_in(x, nw):
    s, d = x.shape

    def body(x_ref, w_ref, h_ref, ht_ref):
        xv = x_ref[...]
        r = lax.rsqrt(jnp.mean(xv * xv, axis=-1, keepdims=True) + NORM_EPS)
        h = xv * r * w_ref[...]
        h_ref[...] = h.astype(BF16)
        ht_ref[...] = h.T.astype(BF16)

    return pl.pallas_call(
        body, name="rms_in", grid=(s // ROW_TILE,),
        in_specs=[pl.BlockSpec((ROW_TILE, d), lambda i: (i, 0)), pl.BlockSpec((1, d), lambda i: (0, 0))],
        out_specs=[pl.BlockSpec((ROW_TILE, d), lambda i: (i, 0)), pl.BlockSpec((d, ROW_TILE), lambda i: (0, i))],
        out_shape=[jax.ShapeDtypeStruct((s, d), BF16), jax.ShapeDtypeStruct((d, s), BF16)],
        compiler_params=_cparams("parallel"))(x, nw)


def _rms_in_bwd(x, nw, dh, dx2):
    s, d = x.shape

    def body(x_ref, w_ref, dh_ref, dx2_ref, dx_ref, dw_ref):
        i = pl.program_id(0)
        xv = x_ref[...]
        r = lax.rsqrt(jnp.mean(xv * xv, axis=-1, keepdims=True) + NORM_EPS)
        dhv = dh_ref[...]
        dyw = dhv * w_ref[...]
        dx_ref[...] = dx2_ref[...] + r * dyw - xv * (r * r * r) * jnp.mean(dyw * xv, axis=-1, keepdims=True)
        part = jnp.sum(dhv * xv * r, axis=0, keepdims=True)

        @pl.when(i == 0)
        def _():
            dw_ref[...] = part

        @pl.when(i > 0)
        def _():
            dw_ref[...] += part

    row = pl.BlockSpec((ROW_TILE, d), lambda i: (i, 0))
    vec = pl.BlockSpec((1, d), lambda i: (0, 0))
    return pl.pallas_call(
        body, name="rms_in_bwd", grid=(s // ROW_TILE,), in_specs=[row, vec, row, row], out_specs=[row, vec],
        out_shape=[jax.ShapeDtypeStruct((s, d), F32), jax.ShapeDtypeStruct((1, d), F32)],
        compiler_params=_cparams("arbitrary"))(x, nw, dh, dx2)


def _shift_down(cur, prev8, k):
    rc = pltpu.roll(cur, k, 0)
    rp = pltpu.roll(prev8, k, 0)
    row = lax.broadcasted_iota(jnp.int32, prev8.shape, 0)
    top = jnp.where(row < k, rp, rc[:8])
    return jnp.concatenate([top, rc[8:]], axis=0)


def _shift_up(cur, next8, k):
    t = cur.shape[0]
    rc = pltpu.roll(cur, t - k, 0)
    rn = pltpu.roll(next8, 8 - k, 0)
    row = lax.broadcasted_iota(jnp.int32, next8.shape, 0)
    bot = jnp.where(row >= 8 - k, rn, rc[t - 8:])
    return jnp.concatenate([rc[:t - 8], bot], axis=0)


def _conv_fwd(proj, conv_w):
    s = proj.shape[0]
    t8 = ROW_TILE // 8

    def body(u_ref, up_ref, w_ref, c_ref, y_ref):
        i = pl.program_id(0)
        part = pl.program_id(1)
        cur = u_ref[...]
        prev8 = jnp.where(i > 0, up_ref[...], 0.0)
        w = w_ref[...]
        c = cur * w[3:4, :]
        for k in (1, 2, 3):
            c = c + _shift_down(cur, prev8, k) * w[3 - k:4 - k, :]
        c_ref[...] = c
        a = _silu(c)
        for h in range(DN_HEADS):
            ah = a[:, h * DN_D:(h + 1) * DN_D]
            r = lax.rsqrt(jnp.sum(ah * ah, axis=-1, keepdims=True) + NORM_EPS)
            y_ref[:, h * DN_D:(h + 1) * DN_D] = jnp.where(part < 2, ah * r, ah)

    return pl.pallas_call(
        body, name="conv_fwd", grid=(s // ROW_TILE, 3),
        in_specs=[pl.BlockSpec((ROW_TILE, DN_W), lambda i, p: (i, p)),
                  pl.BlockSpec((8, DN_W), lambda i, p: (jnp.maximum(i * t8 - 1, 0), p)),
                  pl.BlockSpec((4, DN_W), lambda i, p: (0, p))],
        out_specs=[pl.BlockSpec((ROW_TILE, DN_W), lambda i, p: (i, p))] * 2,
        out_shape=[jax.ShapeDtypeStruct((s, 3 * DN_W), F32)] * 2,
        compiler_params=_cparams("parallel", "parallel"))(proj, proj, conv_w)


def _conv_bwd_act(c, dq, dk, dv):
    s = c.shape[0]

    def body(c_ref, dq_ref, dk_ref, dv_ref, dc_ref):
        for part, d_ref in enumerate((dq_ref, dk_ref, dv_ref)):
            for h in range(DN_HEADS):
                sl = slice(part * DN_W + h * DN_D, part * DN_W + (h + 1) * DN_D)
                ch = c_ref[:, sl]
                dyh = d_ref[:, h * DN_D:(h + 1) * DN_D]
                if part < 2:
                    ah = _silu(ch)
                    r = lax.rsqrt(jnp.sum(ah * ah, axis=-1, keepdims=True) + NORM_EPS)
                    dyh = r * dyh - ah * (r * r * r) * jnp.sum(dyh * ah, axis=-1, keepdims=True)
                dc_ref[:, sl] = dyh * _silu_grad(ch)

    wide = pl.BlockSpec((ROW_TILE, 3 * DN_W), lambda i: (i, 0))
    row = pl.BlockSpec((ROW_TILE, DN_W), lambda i: (i, 0))
    return pl.pallas_call(
        body, name="conv_bwd_act", grid=(s // ROW_TILE,), in_specs=[wide, row, row, row], out_specs=wide,
        out_shape=jax.ShapeDtypeStruct((s, 3 * DN_W), F32), compiler_params=_cparams("parallel"))(c, dq, dk, dv)


def _conv_bwd(proj, dc, conv_w):
    s = proj.shape[0]
    t8 = ROW_TILE // 8
    nrow = s // ROW_TILE
    last8 = s // 8 - 1

    def body(u_ref, up_ref, dc_ref, dcn_ref, w_ref, du_ref, dw_ref):
        i = pl.program_id(1)
        cur = u_ref[...]
        prev8 = jnp.where(i > 0, up_ref[...], 0.0)
        dcv = dc_ref[...]
        next8 = jnp.where(i < nrow - 1, dcn_ref[...], 0.0)
        w = w_ref[...]
        du = dcv * w[3:4, :]
        for k in (1, 2, 3):
            du = du + _shift_up(dcv, next8, k) * w[3 - k:4 - k, :]
        du_ref[...] = du.astype(BF16)

        @pl.when(i == 0)
        def _():
            dw_ref[...] = jnp.zeros_like(dw_ref)

        dw_ref[3:4, :] += jnp.sum(cur * dcv, axis=0, keepdims=True)
        for k in (1, 2, 3):
            dw_ref[3 - k:4 - k, :] += jnp.sum(_shift_down(cur, prev8, k) * dcv, axis=0, keepdims=True)

    blk = pl.BlockSpec((ROW_TILE, DN_W), lambda p, i: (i, p))
    return pl.pallas_call(
        body, name="conv_bwd", grid=(3, nrow),
        in_specs=[blk, pl.BlockSpec((8, DN_W), lambda p, i: (jnp.maximum(i * t8 - 1, 0), p)),
                  blk, pl.BlockSpec((8, DN_W), lambda p, i: (jnp.minimum((i + 1) * t8, last8), p)),
                  pl.BlockSpec((4, DN_W), lambda p, i: (0, p))],
        out_specs=[blk, pl.BlockSpec((4, DN_W), lambda p, i: (0, p))],
        out_shape=[jax.ShapeDtypeStruct((s, 3 * DN_W), BF16), jax.ShapeDtypeStruct((4, 3 * DN_W), F32)],
        compiler_params=_cparams("parallel", "arbitrary"))(proj, proj, dc, dc, conv_w)


def _gates_fwd(proj, gate_par):
    s = proj.shape[0]

    def body(ba_ref, par_ref, o_ref):
        v = ba_ref[...]
        lane = lax.broadcasted_iota(jnp.int32, v.shape, 1)
        beta = _sigmoid(v)
        g = -jnp.exp(par_ref[0:1, :]) * _softplus(v + par_ref[1:2, :])
        o_ref[...] = jnp.where(lane < DN_HEADS, beta, jnp.where(lane < 2 * DN_HEADS, g, 0.0))

    return pl.pallas_call(
        body, name="gates_fwd", grid=(s // ROW_TILE,),
        in_specs=[pl.BlockSpec((ROW_TILE, 128), lambda i: (i, OFF_BA // 128)), pl.BlockSpec((8, 128), lambda i: (0, 0))],
        out_specs=pl.BlockSpec((ROW_TILE, 128), lambda i: (i, 0)),
        out_shape=jax.ShapeDtypeStruct((s, 128), F32), compiler_params=_cparams("parallel"))(proj, gate_par)


def _gates_bwd(proj, gate_par, dbg):
    s = proj.shape[0]

    def body(ba_ref, par_ref, d_ref, o_ref, dpar_ref):
        i = pl.program_id(0)
        v = ba_ref[...]
        dv = d_ref[...]
        lane = lax.broadcasted_iota(jnp.int32, v.shape, 1)
        beta = _sigmoid(v)
        nega = -jnp.exp(par_ref[0:1, :])
        xs = v + par_ref[1:2, :]
        dsp = dv * nega * _sigmoid(xs)
        dal = dv * nega * _softplus(xs)
        is_b = lane < DN_HEADS
        is_g = jnp.logical_and(lane >= DN_HEADS, lane < 2 * DN_HEADS)
        o_ref[...] = jnp.where(is_b, dv * beta * (1.0 - beta), jnp.where(is_g, dsp, 0.0)).astype(BF16)
        r0 = jnp.sum(jnp.where(is_g, dal, 0.0), axis=0, keepdims=True)
        r1 = jnp.sum(jnp.where(is_g, dsp, 0.0), axis=0, keepdims=True)

        @pl.when(i == 0)
        def _():
            dpar_ref[...] = jnp.zeros_like(dpar_ref)

        dpar_ref[0:1, :] += r0
        dpar_ref[1:2, :] += r1

    return pl.pallas_call(
        body, name="gates_bwd", grid=(s // ROW_TILE,),
        in_specs=[pl.BlockSpec((ROW_TILE, 128), lambda i: (i, OFF_BA // 128)), pl.BlockSpec((8, 128), lambda i: (0, 0)),
                  pl.BlockSpec((ROW_TILE, 128), lambda i: (i, 0))],
        out_specs=[pl.BlockSpec((ROW_TILE, 128), lambda i: (i, 0)), pl.BlockSpec((8, 128), lambda i: (0, 0))],
        out_shape=[jax.ShapeDtypeStruct((s, 128), BF16), jax.ShapeDtypeStruct((8, 128), F32)],
        compiler_params=_cparams("arbitrary"))(proj, gate_par, dbg)


def _chunk_masks():
    c = DN_CHUNK
    ii = lax.broadcasted_iota(jnp.int32, (c, c), 0)
    jj = lax.broadcasted_iota(jnp.int32, (c, c), 1)
    return dict(ii=ii, jj=jj, lower=(ii >= jj), strict=(ii > jj), eye=(ii == jj),
                lower_f=(ii >= jj).astype(BF16), upper_f=(ii <= jj).astype(BF16), ones8=jnp.ones((8, c), BF16))


class _Heads:
    def __init__(self, xs):
        self.xs = list(xs)

    def _bin(self, o, f):
        if isinstance(o, _Heads):
            return _Heads([f(a, b) for a, b in zip(self.xs, o.xs)])
        return _Heads([f(a, o) for a in self.xs])

    def __add__(self, o):
        return self._bin(o, lambda a, b: a + b)

    def __sub__(self, o):
        return self._bin(o, lambda a, b: a - b)

    def __mul__(self, o):
        return self._bin(o, lambda a, b: a * b)

    __radd__ = __add__
    __rmul__ = __mul__

    def __neg__(self):
        return _Heads([-a for a in self.xs])

    def __getitem__(self, i):
        return _Heads([a[i] for a in self.xs])


def _hmap(f, *args):
    n = next(len(a.xs) for a in args if isinstance(a, _Heads))
    return _Heads([f(*[(a.xs[h] if isinstance(a, _Heads) else a) for a in args]) for h in range(n)])


def _hdot(a, b):
    return _hmap(_dot, a, b)


def _hdot_nt(a, b):
    return _hmap(_dot_nt, a, b)


def _hdot_tn(a, b):
    return _hmap(_dot_tn, a, b)


def _hsum(a, axis):
    return _hmap(lambda t: jnp.sum(t, axis=axis, keepdims=True), a)


def _hwhere(c, a, b):
    return _hmap(jnp.where, c, a, b)


def _chunk_common(mk, q, k, beta_col, g_col):
    c = DN_CHUNK
    lower, strict = mk["lower"], mk["strict"]
    qs = q * (DN_D ** -0.5)
    beta_b = _hmap(lambda t: jnp.broadcast_to(t, (c, DN_D)), beta_col)
    g_b = _hmap(lambda t: jnp.broadcast_to(t, (c, DN_D)), g_col)
    gc_b = _hmap(_dot_exact_lhs, mk["lower_f"], g_b)
    gc_sq = gc_b[:, :c]
    gc_r = _hmap(_dot_exact_lhs, mk["ones8"], _hwhere(mk["eye"], gc_sq, 0.0))[0:1, :]
    gam = _hwhere(lower, _hmap(lambda t: jnp.exp(jnp.minimum(t, 0.0)), gc_sq - gc_r), 0.0)
    egc = _hmap(jnp.exp, gc_b)
    gl = gc_b[c - 1:c, :]
    ekd = _hmap(jnp.exp, gl - gc_b)
    dl = _hmap(jnp.exp, gl)
    kb = k * beta_b
    a_strict = _hwhere(strict, _hdot_nt(kb, k) * gam, 0.0)
    aqk = _hwhere(lower, _hdot_nt(qs, k) * gam, 0.0)
    return dict(k=k, qs=qs, beta_b=beta_b, gc_b=gc_b, gam=gam, egc=egc, ekd=ekd, dl=dl, kb=kb, a_strict=a_strict, aqk=aqk)


def _unit_lower_inverse_minus_eye(n_strict, ii, jj):
    same = lax.shift_right_logical(ii, 4) == lax.shift_right_logical(jj, 4)
    dmat = _hwhere(same, n_strict, 0.0)
    omat = n_strict - dmat
    d2 = _hdot(dmat, dmat)
    d4 = _hdot(d2, d2)
    d8 = _hdot(d4, d4)
    x1 = d2 - dmat - _hdot(dmat, d2)
    x2 = x1 + d4 + _hdot(x1, d4)
    x3 = x2 + d8 + _hdot(x2, d8)
    n1 = omat + _hdot(x3, omat)
    n2 = _hdot(n1, n1)
    y = n2 - n1 - _hdot(n1, n2)
    return y + x3 + _hdot(y, x3)


def _gdr_fwd(qkv, bg):
    s = qkv.shape[0]
    c = DN_CHUNK
    n = s // c

    def body(q_ref, k_ref, v_ref, bg_ref, o_ref, u_ref, w_ref, vn_ref, tm_ref, st_ref, state):
        @pl.when(pl.program_id(0) == 0)
        def _():
            state[...] = jnp.zeros_like(state)

        mk = _chunk_masks()
        bg = bg_ref[...]
        hs = range(DN_HEADS)
        sls = [slice(h * DN_D, (h + 1) * DN_D) for h in hs]
        cm = _chunk_common(mk, _Heads(q_ref[:, sl] for sl in sls), _Heads(k_ref[:, sl] for sl in sls),
                           _Heads(bg[:, h:h + 1] for h in hs), _Heads(bg[:, DN_HEADS + h:DN_HEADS + h + 1] for h in hs))
        tm = _unit_lower_inverse_minus_eye(cm["a_strict"], mk["ii"], mk["jj"])
        rhs_u = _Heads(v_ref[:, sl] for sl in sls) * cm["beta_b"]
        rhs_w = cm["kb"] * cm["egc"]
        u = rhs_u + _hdot(tm, rhs_u)
        w = rhs_w + _hdot(tm, rhs_w)
        st = _Heads(state[h] for h in hs)
        v_new = u - _hdot(w, st)
        o = _hdot(cm["qs"] * cm["egc"], st) + _hdot(cm["aqk"], v_new)
        st_new = st * cm["dl"] + _hdot_tn(cm["k"] * cm["ekd"], v_new)
        for h, sl in zip(hs, sls):
            o_ref[:, sl] = o.xs[h]
            u_ref[:, sl] = u.xs[h]
            w_ref[:, sl] = w.xs[h]
            vn_ref[:, sl] = v_new.xs[h]
            tm_ref[h, 0] = tm.xs[h]
            st_ref[h, 0] = st.xs[h]
            state[h] = st_new.xs[h]

    def part(p):
        return pl.BlockSpec((c, DN_W), lambda j: (j, p))

    return pl.pallas_call(
        body, name="gdr_fwd", grid=(n,),
        in_specs=[part(0), part(1), part(2), pl.BlockSpec((c, 128), lambda j: (j, 0))],
        out_specs=[part(0)] * 4 + [pl.BlockSpec((DN_HEADS, 1, c, c), lambda j: (0, j, 0, 0)),
                                   pl.BlockSpec((DN_HEADS, 1, DN_D, DN_D), lambda j: (0, j, 0, 0))],
        out_shape=[jax.ShapeDtypeStruct((s, DN_W), F32)] * 4
        + [jax.ShapeDtypeStruct((DN_HEADS, n, c, c), F32), jax.ShapeDtypeStruct((DN_HEADS, n, DN_D, DN_D), F32)],
        scratch_shapes=[pltpu.VMEM((DN_HEADS, DN_D, DN_D), F32)],
        compiler_params=_cparams("arbitrary"))(qkv, qkv, qkv, bg)


def _gdr_bwd(qkv, bg, u, w, vn, tmat, states, do):
    s = qkv.shape[0]
    c = DN_CHUNK
    n = s // c

    def body(q_ref, k_ref, v_ref, bg_ref, u_ref, w_ref, vn_ref, tm_ref, st_ref, do_ref,
             dq_ref, dk_ref, dv_ref, dbg_ref, dstate):
        @pl.when(pl.program_id(0) == 0)
        def _():
            dstate[...] = jnp.zeros_like(dstate)

        mk = _chunk_masks()
        lower, strict = mk["lower"], mk["strict"]
        bg = bg_ref[...]
        ones = jnp.ones((c, DN_D), BF16)
        rowi = lax.broadcasted_iota(jnp.int32, (c, DN_D), 0)
        lane = lax.broadcasted_iota(jnp.int32, (c, 128), 1)
        hs = range(DN_HEADS)
        sls = [slice(h * DN_D, (h + 1) * DN_D) for h in hs]

        def heads_of(ref):
            return _Heads(ref[:, sl] for sl in sls)

        cm = _chunk_common(mk, heads_of(q_ref), heads_of(k_ref),
                           _Heads(bg[:, h:h + 1] for h in hs), _Heads(bg[:, DN_HEADS + h:DN_HEADS + h + 1] for h in hs))
        k, qs, beta_b = cm["k"], cm["qs"], cm["beta_b"]
        gam, egc, ekd, dl, kb = cm["gam"], cm["egc"], cm["ekd"], cm["dl"], cm["kb"]
        aqk, a_strict = cm["aqk"], cm["a_strict"]
        v, uu, ww, v_new, dov = heads_of(v_ref), heads_of(u_ref), heads_of(w_ref), heads_of(vn_ref), heads_of(do_ref)
        tm = _Heads(tm_ref[h, 0] for h in hs)
        st = _Heads(st_ref[h, 0] for h in hs)
        dsn = _Heads(dstate[h] for h in hs)
        qd = qs * egc
        kd = k * ekd

        dv_new = _hdot_tn(aqk, dov) + _hdot(kd, dsn)
        daqk = _hwhere(lower, _hdot_nt(dov, v_new), 0.0)
        dqd = _hdot_nt(dov, st)
        dkd = _hdot_nt(v_new, dsn)
        ddl = _hsum(_hsum(dsn * st, 1), 0)
        dw = -_hdot_nt(dv_new, st)
        ds_new = dsn * dl + _hdot_tn(qd, dov) - _hdot_tn(ww, dv_new)

        dru = dv_new + _hdot_tn(tm, dv_new)
        drw = dw + _hdot_tn(tm, dw)
        dn = _hwhere(strict, -(_hdot_nt(dru, uu) + _hdot_nt(drw, ww)), 0.0)
        dag = dn * gam
        dkb = _hdot(dag, k) + drw * egc
        dk = _hdot_tn(dag, kb)
        dqg = daqk * gam
        dqs = _hdot(dqg, k) + dqd * egc
        dk = dk + _hdot_tn(dqg, qs) + dkb * beta_b + dkd * ekd
        pmat = dn * a_strict + daqk * aqk
        tkd = _hsum(dkd * kd, -1)
        dgc = (_hsum(pmat, -1) - _hmap(_dot_tn_exact_rhs, pmat, ones) + _hsum(drw * (kb * egc), -1)
               + _hsum(dqd * qd, -1) - tkd)
        last = _hsum(tkd, 0) + ddl * dl
        dgc = dgc + _hwhere(rowi == c - 1, last, 0.0)
        dg = _hmap(_dot_exact_lhs, mk["upper_f"], dgc)
        dbeta = _hsum(dru * v, -1) + _hsum(dkb * k, -1)
        dq = dqs * (DN_D ** -0.5)
        dv = dru * beta_b

        dbg = jnp.zeros((c, 128), F32)
        for h, sl in zip(hs, sls):
            dq_ref[:, sl] = dq.xs[h]
            dk_ref[:, sl] = dk.xs[h]
            dv_ref[:, sl] = dv.xs[h]
            dstate[h] = ds_new.xs[h]
            dbg = dbg + jnp.where(lane == h, dbeta.xs[h], 0.0) + jnp.where(lane == DN_HEADS + h, dg.xs[h], 0.0)
        dbg_ref[...] = dbg

    def part(p):
        return pl.BlockSpec((c, DN_W), lambda j: (n - 1 - j, p))

    vec = pl.BlockSpec((c, 128), lambda j: (n - 1 - j, 0))
    return pl.pallas_call(
        body, name="gdr_bwd", grid=(n,),
        in_specs=[part(0), part(1), part(2), vec, part(0), part(0), part(0),
                  pl.BlockSpec((DN_HEADS, 1, c, c), lambda j: (0, n - 1 - j, 0, 0)),
                  pl.BlockSpec((DN_HEADS, 1, DN_D, DN_D), lambda j: (0, n - 1 - j, 0, 0)), part(0)],
        out_specs=[part(0), part(0), part(0), vec],
        out_shape=[jax.ShapeDtypeStruct((s, DN_W), F32)] * 3 + [jax.ShapeDtypeStruct((s, 128), F32)],
        scratch_shapes=[pltpu.VMEM((DN_HEADS, DN_D, DN_D), F32)],
        compiler_params=_cparams("arbitrary"))(qkv, qkv, qkv, bg, u, w, vn, tmat, states, do)


def _gdr_out(o, proj, dnw):
    s = o.shape[0]

    def body(o_ref, z_ref, w_ref, y_ref, yt_ref):
        ov, zv, wv = o_ref[...], z_ref[...], w_ref[...]
        for h in range(DN_HEADS):
            sl = slice(h * DN_D, (h + 1) * DN_D)
            oh = ov[:, sl]
            r = lax.rsqrt(jnp.mean(oh * oh, axis=-1, keepdims=True) + NORM_EPS)
            y = (oh * r * wv) * _silu(zv[:, sl])
            y_ref[:, sl] = y.astype(BF16)
            yt_ref[sl, :] = y.T.astype(BF16)

    row = pl.BlockSpec((ROW_TILE, DN_W), lambda i: (i, 0))
    return pl.pallas_call(
        body, name="gdr_out", grid=(s // ROW_TILE,),
        in_specs=[row, pl.BlockSpec((ROW_TILE, DN_W), lambda i: (i, OFF_Z_A // DN_W)), pl.BlockSpec((1, DN_D), lambda i: (0, 0))],
        out_specs=[row, pl.BlockSpec((DN_W, ROW_TILE), lambda i: (0, i))],
        out_shape=[jax.ShapeDtypeStruct((s, DN_W), BF16), jax.ShapeDtypeStruct((DN_W, s), BF16)],
        compiler_params=_cparams("parallel"))(o, proj, dnw)


def _gdr_out_bwd(o, proj, dnw, dy):
    s = o.shape[0]

    def body(o_ref, z_ref, w_ref, dy_ref, do_ref, dz_ref, dw_ref):
        i = pl.program_id(0)
        ov, zv, wv, dyv = o_ref[...], z_ref[...], w_ref[...], dy_ref[...]
        acc = jnp.zeros((1, DN_D), F32)
        for h in range(DN_HEADS):
            sl = slice(h * DN_D, (h + 1) * DN_D)
            oh, zh, dh = ov[:, sl], zv[:, sl], dyv[:, sl]
            r = lax.rsqrt(jnp.mean(oh * oh, axis=-1, keepdims=True) + NORM_EPS)
            dn = dh * _silu(zh)
            dz_ref[:, sl] = (dh * (oh * r * wv) * _silu_grad(zh)).astype(BF16)
            acc = acc + jnp.sum(dn * oh * r, axis=0, keepdims=True)
            dnw_ = dn * wv
            do_ref[:, sl] = r * dnw_ - oh * (r * r * r) * jnp.mean(dnw_ * oh, axis=-1, keepdims=True)

        @pl.when(i == 0)
        def _():
            dw_ref[...] = acc

        @pl.when(i > 0)
        def _():
            dw_ref[...] += acc

    row = pl.BlockSpec((ROW_TILE, DN_W), lambda i: (i, 0))
    vec = pl.BlockSpec((1, DN_D), lambda i: (0, 0))
    return pl.pallas_call(
        body, name="gdr_out_bwd", grid=(s // ROW_TILE,),
        in_specs=[row, pl.BlockSpec((ROW_TILE, DN_W), lambda i: (i, OFF_Z_A // DN_W)), vec, row],
        out_specs=[row, row, vec],
        out_shape=[jax.ShapeDtypeStruct((s, DN_W), F32), jax.ShapeDtypeStruct((s, DN_W), BF16),
                   jax.ShapeDtypeStruct((1, DN_D), F32)],
        compiler_params=_cparams("arbitrary"))(o, proj, dnw, dy)


def _slope(group, head):
    idx = (group * DIL_HEADS + head + 1).astype(F32)
    return jnp.exp(jnp.full((1, 128), -8.0 * math.log(2.0) / (N_DIL * DIL_HEADS), F32) * idx)


def _att_scores(qb, k_cur, k_prev, slope_d, has_prev):
    iq = lax.broadcasted_iota(jnp.int32, (ATT_BLOCK, ATT_BLOCK), 0)
    jk = lax.broadcasted_iota(jnp.int32, (ATT_BLOCK, ATT_BLOCK), 1)
    dist_c = (iq - jk).astype(F32)
    s_cur = jnp.where(iq >= jk, _dot_nt(qb, k_cur) - slope_d * dist_c, NEG)
    s_prev = jnp.where(jnp.logical_and(jk >= iq, has_prev),
                       _dot_nt(qb, k_prev) - slope_d * (dist_c + float(ATT_BLOCK)), NEG)
    return s_cur, s_prev


ATT_UNROLL = 4


def _att_blocks(i, dil, nb):
    per = dil * nb // ATT_UNROLL
    assert per * ATT_UNROLL == dil * nb
    for i0 in range(per):
        blocks = [divmod(i0 + u * per, nb) for u in range(ATT_UNROLL)]
        assert all(a[0] != b[0] or abs(a[1] - b[1]) >= 2 for n, a in enumerate(blocks) for b in blocks[n + 1:])
    curs, prvs, has_prev = [], [], []
    for u in range(ATT_UNROLL):
        t = i + u * per
        r = lax.div(t, nb)
        j = lax.rem(t, nb)
        base = r + dil * ATT_BLOCK * j
        pbase = base - dil * ATT_BLOCK * jnp.minimum(j, 1)
        if dil == 1:
            base, pbase = pl.multiple_of(base, ATT_BLOCK), pl.multiple_of(pbase, ATT_BLOCK)
        curs.append(pl.ds(base, ATT_BLOCK, stride=dil))
        prvs.append(pl.ds(pbase, ATT_BLOCK, stride=dil))
        has_prev.append(j > 0)
    return curs, prvs, has_prev


def _att_fwd(proj, group):
    s = proj.shape[0]
    dil = DIL_GROUPS[group][1]
    assert DIL_GROUPS[group][0] // dil == ATT_BLOCK
    nb = s // dil // ATT_BLOCK
    assert nb * dil * ATT_BLOCK == s

    def body(q_ref, k_ref, v_ref, num_ref, den_ref, mx_ref):
        slope_d = _slope(group, pl.program_id(0)) * float(dil)

        def step(i, carry):
            curs, prvs, has_prev = _att_blocks(i, dil, nb)
            us = range(ATT_UNROLL)
            qb = [q_ref[c, :] * (DIL_DH ** -0.5) for c in curs]
            sc = [_att_scores(qb[u], k_ref[curs[u], :], k_ref[prvs[u], :], slope_d, has_prev[u]) for u in us]
            mx = [jnp.maximum(jnp.max(a, axis=-1, keepdims=True), jnp.max(b, axis=-1, keepdims=True)) for a, b in sc]
            p_cur = [jnp.exp(sc[u][0] - mx[u]) for u in us]
            p_prev = [jnp.exp(sc[u][1] - mx[u]) for u in us]
            den = [jnp.sum(p_cur[u], axis=-1, keepdims=True) + jnp.sum(p_prev[u], axis=-1, keepdims=True) for u in us]
            num = [_dot(p_cur[u], v_ref[curs[u], :]) + _dot(p_prev[u], v_ref[prvs[u], :]) for u in us]
            for u in us:
                num_ref[curs[u], :] = num[u]
                den_ref[curs[u], :] = jnp.broadcast_to(den[u], (ATT_BLOCK, DIL_DH))
                mx_ref[curs[u], :] = jnp.broadcast_to(mx[u], (ATT_BLOCK, DIL_DH))
            return carry

        lax.fori_loop(0, dil * nb // ATT_UNROLL, step, 0)

    def col(off):
        return pl.BlockSpec((s, DIL_DH), lambda h: (0, off // DIL_DH + group * DIL_HEADS + h))

    out = pl.BlockSpec((s, DIL_DH), lambda h: (0, h))
    return pl.pallas_call(
        body, name=f"att_fwd{group}", grid=(DIL_HEADS,), in_specs=[col(OFF_Q_B), col(OFF_K_B), col(OFF_V_B)],
        out_specs=[out, out, out], out_shape=[jax.ShapeDtypeStruct((s, DIL_W), F32)] * 3,
        compiler_params=_cparams("parallel"))(proj, proj, proj)


def _att_bwd(proj, group, do, lse, delta):
    s = proj.shape[0]
    dil = DIL_GROUPS[group][1]
    nb = s // dil // ATT_BLOCK

    def body(q_ref, k_ref, v_ref, do_ref, lse_ref, dl_ref, dq_ref, dk_ref, dv_ref, dq_acc, dk_acc, dv_acc):
        slope_d = _slope(group, pl.program_id(0)) * float(dil)
        dk_acc[...] = jnp.zeros_like(dk_acc)
        dv_acc[...] = jnp.zeros_like(dv_acc)

        def step(i, carry):
            curs, prvs, has_prev = _att_blocks(i, dil, nb)
            us = range(ATT_UNROLL)
            qb = [q_ref[c, :] * (DIL_DH ** -0.5) for c in curs]
            k_cur, k_prev = [k_ref[c, :] for c in curs], [k_ref[p, :] for p in prvs]
            v_cur, v_prev = [v_ref[c, :] for c in curs], [v_ref[p, :] for p in prvs]
            sc = [_att_scores(qb[u], k_cur[u], k_prev[u], slope_d, has_prev[u]) for u in us]
            lse_b, delta_b, dob = [lse_ref[c, :] for c in curs], [dl_ref[c, :] for c in curs], [do_ref[c, :] for c in curs]
            p_cur = [jnp.exp(sc[u][0] - lse_b[u]) for u in us]
            p_prev = [jnp.exp(sc[u][1] - lse_b[u]) for u in us]
            ds_cur = [p_cur[u] * (_dot_nt(dob[u], v_cur[u]) - delta_b[u]) for u in us]
            ds_prev = [p_prev[u] * (_dot_nt(dob[u], v_prev[u]) - delta_b[u]) for u in us]
            dq = [(_dot(ds_cur[u], k_cur[u]) + _dot(ds_prev[u], k_prev[u])) * (DIL_DH ** -0.5) for u in us]
            dk_c = [_dot_tn(ds_cur[u], qb[u]) for u in us]
            dv_c = [_dot_tn(p_cur[u], dob[u]) for u in us]
            dk_p = [_dot_tn(ds_prev[u], qb[u]) for u in us]
            dv_p = [_dot_tn(p_prev[u], dob[u]) for u in us]
            for u in us:
                dq_acc[curs[u], :] = dq[u]
                dk_acc[curs[u], :] += dk_c[u]
                dv_acc[curs[u], :] += dv_c[u]
            for u in us:
                dk_acc[prvs[u], :] += dk_p[u]
                dv_acc[prvs[u], :] += dv_p[u]
            return carry

        lax.fori_loop(0, dil * nb // ATT_UNROLL, step, 0)
        dq_ref[...] = dq_acc[...].astype(BF16)
        dk_ref[...] = dk_acc[...].astype(BF16)
        dv_ref[...] = dv_acc[...].astype(BF16)

    def col(off):
        return pl.BlockSpec((s, DIL_DH), lambda h: (0, off // DIL_DH + group * DIL_HEADS + h))

    hd = pl.BlockSpec((s, DIL_DH), lambda h: (0, h))
    return pl.pallas_call(
        body, name=f"att_bwd{group}", grid=(DIL_HEADS,),
        in_specs=[col(OFF_Q_B), col(OFF_K_B), col(OFF_V_B), hd, hd, hd], out_specs=[hd, hd, hd],
        out_shape=[jax.ShapeDtypeStruct((s, DIL_W), BF16)] * 3,
        scratch_shapes=[pltpu.VMEM((s, DIL_DH), F32)] * 3,
        compiler_params=_cparams("parallel"))(proj, proj, proj, do, lse, delta)


def _att_merge(parts, proj):
    s = proj.shape[0]

    def body(n0, d0, m0, n1, d1, m1, n2, d2, m2, z_ref, ob_ref, o_ref, lse_ref, obt_ref):
        m = jnp.maximum(jnp.maximum(m0[...], m1[...]), m2[...])
        num = jnp.zeros_like(m)
        den = jnp.zeros_like(m)
        for nr, dr, mr in ((n0, d0, m0), (n1, d1, m1), (n2, d2, m2)):
            sc = jnp.exp(mr[...] - m)
            num = num + nr[...] * sc
            den = den + dr[...] * sc
        o = num / den
        o_ref[...] = o
        lse_ref[...] = m + jnp.log(den)
        ob = o * _silu(z_ref[...])
        ob_ref[...] = ob.astype(BF16)
        obt_ref[...] = ob.T.astype(BF16)

    row = pl.BlockSpec((ROW_TILE, DIL_W), lambda i: (i, 0))
    flat = [a for p in parts for a in p]
    return pl.pallas_call(
        body, name="att_merge", grid=(s // ROW_TILE,),
        in_specs=[row] * 9 + [pl.BlockSpec((ROW_TILE, DIL_W), lambda i: (i, OFF_Z_B // DIL_W))],
        out_specs=[row, row, row, pl.BlockSpec((DIL_W, ROW_TILE), lambda i: (0, i))],
        out_shape=[jax.ShapeDtypeStruct((s, DIL_W), BF16), jax.ShapeDtypeStruct((s, DIL_W), F32),
                   jax.ShapeDtypeStruct((s, DIL_W), F32), jax.ShapeDtypeStruct((DIL_W, s), BF16)],
        compiler_params=_cparams("parallel"))(*flat, proj)


def _att_merge_bwd(o, proj, dob):
    s = o.shape[0]

    def body(o_ref, z_ref, d_ref, do_ref, dl_ref, dz_ref):
        ov, zv, dv = o_ref[...], z_ref[...], d_ref[...]
        do = dv * _silu(zv)
        do_ref[...] = do
        dz_ref[...] = (dv * ov * _silu_grad(zv)).astype(BF16)
        for h in range(DIL_HEADS):
            sl = slice(h * DIL_DH, (h + 1) * DIL_DH)
            dl_ref[:, sl] = jnp.broadcast_to(jnp.sum(do[:, sl] * ov[:, sl], axis=-1, keepdims=True), (ROW_TILE, DIL_DH))

    row = pl.BlockSpec((ROW_TILE, DIL_W), lambda i: (i, 0))
    return pl.pallas_call(
        body, name="att_merge_bwd", grid=(s // ROW_TILE,),
        in_specs=[row, pl.BlockSpec((ROW_TILE, DIL_W), lambda i: (i, OFF_Z_B // DIL_W)), row],
        out_specs=[row, row, row],
        out_shape=[jax.ShapeDtypeStruct((s, DIL_W), F32), jax.ShapeDtypeStruct((s, DIL_W), F32),
                   jax.ShapeDtypeStruct((s, DIL_W), BF16)],
        compiler_params=_cparams("parallel"))(o, proj, dob)


def _merge(proj, ya, yb):
    s = proj.shape[0]

    def body(ga_ref, gb_ref, ya_ref, yb_ref, o_ref, ot_ref):
        m = _sigmoid(ga_ref[...]) * ya_ref[...] + _sigmoid(gb_ref[...]) * yb_ref[...]
        o_ref[...] = m.astype(BF16)
        ot_ref[...] = m.T.astype(BF16)

    row = pl.BlockSpec((ROW_TILE, D_MODEL), lambda i: (i, 0))
    return pl.pallas_call(
        body, name="merge", grid=(s // ROW_TILE,),
        in_specs=[pl.BlockSpec((ROW_TILE, D_MODEL), lambda i: (i, OFF_G_A // D_MODEL)),
                  pl.BlockSpec((ROW_TILE, D_MODEL), lambda i: (i, OFF_G_B // D_MODEL)), row, row],
        out_specs=[row, pl.BlockSpec((D_MODEL, ROW_TILE), lambda i: (0, i))],
        out_shape=[jax.ShapeDtypeStruct((s, D_MODEL), BF16), jax.ShapeDtypeStruct((D_MODEL, s), BF16)],
        compiler_params=_cparams("parallel"))(proj, proj, ya, yb)


def _merge_bwd(proj, ya, yb, dm):
    s = proj.shape[0]

    def body(ga_ref, gb_ref, ya_ref, yb_ref, dm_ref, dya_ref, dyb_ref, dga_ref, dgb_ref):
        dmv = dm_ref[...]
        sa, sb = _sigmoid(ga_ref[...]), _sigmoid(gb_ref[...])
        dya_ref[...] = (dmv * sa).astype(BF16)
        dyb_ref[...] = (dmv * sb).astype(BF16)
        dga_ref[...] = (dmv * ya_ref[...] * sa * (1.0 - sa)).astype(BF16)
        dgb_ref[...] = (dmv * yb_ref[...] * sb * (1.0 - sb)).astype(BF16)

    row = pl.BlockSpec((ROW_TILE, D_MODEL), lambda i: (i, 0))
    return pl.pallas_call(
        body, name="merge_bwd", grid=(s // ROW_TILE,),
        in_specs=[pl.BlockSpec((ROW_TILE, D_MODEL), lambda i: (i, OFF_G_A // D_MODEL)),
                  pl.BlockSpec((ROW_TILE, D_MODEL), lambda i: (i, OFF_G_B // D_MODEL)), row, row, row],
        out_specs=[row] * 4, out_shape=[jax.ShapeDtypeStruct((s, D_MODEL), BF16)] * 4,
        compiler_params=_cparams("parallel"))(proj, proj, ya, yb, dm)


def _final(x, t, fw, tgt):
    s, d = x.shape

    def body(x_ref, t_ref, w_ref, y_ref, dx_ref, dw_ref, l_ref):
        i = pl.program_id(0)
        x2 = x_ref[...] + t_ref[...]
        wv = w_ref[...]
        r = lax.rsqrt(jnp.mean(x2 * x2, axis=-1, keepdims=True) + NORM_EPS)
        e = x2 * r * wv - y_ref[...]
        lrow = jnp.mean(e * e, axis=-1, keepdims=True)
        lpart = jnp.broadcast_to(0.5 * jnp.sum(lrow, axis=0, keepdims=True), (1, 128))
        dy = e * (1.0 / d)
        dwp = jnp.sum(dy * x2 * r, axis=0, keepdims=True)
        dyw = dy * wv
        dx_ref[...] = r * dyw - x2 * (r * r * r) * jnp.mean(dyw * x2, axis=-1, keepdims=True)

        @pl.when(i == 0)
        def _():
            dw_ref[...] = dwp
            l_ref[...] = lpart

        @pl.when(i > 0)
        def _():
            dw_ref[...] += dwp
            l_ref[...] += lpart

    row = pl.BlockSpec((ROW_TILE, d), lambda i: (i, 0))
    vec = pl.BlockSpec((1, d), lambda i: (0, 0))
    return pl.pallas_call(
        body, name="final", grid=(s // ROW_TILE,), in_specs=[row, row, vec, row],
        out_specs=[row, vec, pl.BlockSpec((1, 128), lambda i: (0, 0))],
        out_shape=[jax.ShapeDtypeStruct((s, d), F32), jax.ShapeDtypeStruct((1, d), F32), jax.ShapeDtypeStruct((1, 128), F32)],
        compiler_params=_cparams("arbitrary"))(x, t, fw, tgt)


def _adamw(w, g, m, v, name):
    r, c = w.shape
    cap = max(8, (1 << 18) // c)
    divisors = [t for t in range(8, min(r, cap) + 1, 8) if r % t == 0]
    tr = r if r <= 8 else (max(divisors) if divisors else cap)

    def body(w_ref, g_ref, m_ref, v_ref, d_ref, nm_ref, nv_ref):
        gv = g_ref[...]
        mn = ADAM_B1 * m_ref[...] + (1.0 - ADAM_B1) * gv
        vn = ADAM_B2 * v_ref[...] + (1.0 - ADAM_B2) * (gv * gv)
        m_hat = mn / (1.0 - ADAM_B1 ** ADAM_STEP)
        v_hat = vn / (1.0 - ADAM_B2 ** ADAM_STEP)
        d_ref[...] = -ADAM_LR * (m_hat / (jnp.sqrt(v_hat) + ADAM_EPS) + ADAM_WD * w_ref[...])
        nm_ref[...] = mn
        nv_ref[...] = vn

    blk = pl.BlockSpec((tr, c), lambda i: (i, 0))
    return pl.pallas_call(
        body, name=name, grid=(pl.cdiv(r, tr),), in_specs=[blk] * 4, out_specs=[blk] * 3,
        out_shape=[jax.ShapeDtypeStruct((r, c), F32)] * 3, compiler_params=_cparams("parallel"))(w, g, m, v)


HBM_SPEC = pl.BlockSpec(memory_space=pl.ANY)


def _place():
    x, y, c = lax.axis_index("x"), lax.axis_index("y"), lax.axis_index("c")
    chips = [(1 - x, y), (x, 1 - y), (1 - x, 1 - y)]
    return x, y, c, chips


def _ag_weights(packs):
    na = len(packs)

    def body(*refs):
        p_refs, out_refs = refs[:na], refs[na:2 * na]
        send_sems, recv_sems = refs[2 * na:]
        x, y, c, chips = _place()
        me, sib, j = (x, y, c), (x, y, 1 - c), 2 * x + y

        def rc(k, src, dst, to):
            return pltpu.make_async_remote_copy(src_ref=src, dst_ref=dst, send_sem=send_sems.at[k],
                                                recv_sem=recv_sems.at[k], device_id=to, device_id_type=MESH)

        first = [rc(6 * a + k, p_refs[a].at[c], out_refs[a].at[j, c], (cx, cy, c))
                 for a in range(na) for k, (cx, cy) in enumerate(chips)]
        for cp in first:
            cp.start()
        passed = []
        for a in range(na):
            for k, (cx, cy) in enumerate(chips):
                land = out_refs[a].at[2 * cx + cy, c]
                rc(6 * a + k, p_refs[a].at[c], land, me).wait_recv()
                fwd = rc(6 * a + 3 + k, land, land, sib)
                fwd.start()
                passed.append(fwd)
        for a in range(na):
            for k, (cx, cy) in enumerate(chips):
                rc(6 * a + 3 + k, p_refs[a].at[c], out_refs[a].at[2 * cx + cy, 1 - c], me).wait_recv()
        for cp in first + passed:
            cp.wait_send()

    return pl.pallas_call(
        body, name="ag_weights",
        out_shape=[jax.ShapeDtypeStruct((N_CHIPS,) + p.shape, p.dtype) for p in packs],
        in_specs=[HBM_SPEC] * na, out_specs=[HBM_SPEC] * na,
        scratch_shapes=[pltpu.SemaphoreType.DMA((6 * na,)), pltpu.SemaphoreType.DMA((6 * na,))])(*packs)


def _rs_pair(gpacks):
    na = len(gpacks)
    n = N_CHIPS

    def body(*refs):
        g_refs, out_refs = refs[:na], refs[na:2 * na]
        send_sems, recv_sems = refs[2 * na:]
        x, y, c, _ = _place()
        sib = (x, y, 1 - c)
        cps = [pltpu.make_async_remote_copy(src_ref=g_refs[a].at[p, 1 - c], dst_ref=out_refs[a].at[p],
                                            send_sem=send_sems.at[n * a + p], recv_sem=recv_sems.at[n * a + p],
                                            device_id=sib, device_id_type=MESH)
               for a in range(na) for p in range(n)]
        for cp in cps:
            cp.start()
        for cp in cps:
            cp.wait_recv()
        for cp in cps:
            cp.wait_send()

    return pl.pallas_call(
        body, name="rs_pair",
        out_shape=[jax.ShapeDtypeStruct((n,) + g.shape[2:], g.dtype) for g in gpacks],
        in_specs=[HBM_SPEC] * na, out_specs=[HBM_SPEC] * na,
        scratch_shapes=[pltpu.SemaphoreType.DMA((n * na,)), pltpu.SemaphoreType.DMA((n * na,))])(*gpacks)


def _rs_chips(csums):
    na = len(csums)

    def body(*refs):
        s_refs, out_refs = refs[:na], refs[na:2 * na]
        send_sems, recv_sems = refs[2 * na:]
        x, y, c, chips = _place()
        j = 2 * x + y
        cps = [pltpu.make_async_remote_copy(src_ref=s_refs[a].at[2 * cx + cy], dst_ref=out_refs[a].at[j],
                                            send_sem=send_sems.at[3 * a + k], recv_sem=recv_sems.at[3 * a + k],
                                            device_id=(cx, cy, c), device_id_type=MESH)
               for a in range(na) for k, (cx, cy) in enumerate(chips)]
        for cp in cps:
            cp.start()
        for a in range(na):
            for k, (cx, cy) in enumerate(chips):
                pltpu.make_async_remote_copy(src_ref=s_refs[a].at[j], dst_ref=out_refs[a].at[2 * cx + cy],
                                             send_sem=send_sems.at[3 * a + k], recv_sem=recv_sems.at[3 * a + k],
                                             device_id=(x, y, c), device_id_type=MESH).wait_recv()
        for cp in cps:
            cp.wait_send()

    return pl.pallas_call(
        body, name="rs_chips", out_shape=[jax.ShapeDtypeStruct(s.shape, s.dtype) for s in csums],
        in_specs=[HBM_SPEC] * na, out_specs=[HBM_SPEC] * na,
        scratch_shapes=[pltpu.SemaphoreType.DMA((3 * na,)), pltpu.SemaphoreType.DMA((3 * na,))])(*csums)


SWAP_CHUNKS = 4


def _pair_swap(halves):
    na = len(halves)

    def body(*refs):
        h_refs, out_refs = refs[:na], refs[na:2 * na]
        send_sems, recv_sems = refs[2 * na:]
        x, y, c, _ = _place()
        cps = []
        for a in range(na):
            rows = h_refs[a].shape[0] // SWAP_CHUNKS
            assert rows * SWAP_CHUNKS == h_refs[a].shape[0]
            for q in range(SWAP_CHUNKS):
                k = SWAP_CHUNKS * a + q
                cps.append(pltpu.make_async_remote_copy(
                    src_ref=h_refs[a].at[pl.ds(q * rows, rows)], dst_ref=out_refs[a].at[pl.ds(q * rows, rows)],
                    send_sem=send_sems.at[k], recv_sem=recv_sems.at[k], device_id=(x, y, 1 - c), device_id_type=MESH))
        for cp in cps:
            cp.start()
        for cp in cps:
            cp.wait_recv()
        for cp in cps:
            cp.wait_send()

    return pl.pallas_call(
        body, name="pair_swap", out_shape=[jax.ShapeDtypeStruct(h.shape, h.dtype) for h in halves],
        in_specs=[HBM_SPEC] * na, out_specs=[HBM_SPEC] * na,
        scratch_shapes=[pltpu.SemaphoreType.DMA((SWAP_CHUNKS * na,)), pltpu.SemaphoreType.DMA((SWAP_CHUNKS * na,))])(*halves)


def _ag_small(v):
    m_per, n = v.shape

    def body(x_ref, out_ref, send_sems, recv_sems, local_sem):
        x, y, c, chips = _place()
        me, sibling = (x, y, c), (x, y, 1 - c)

        def rows(px, py, pc):
            return out_ref.at[pl.ds((4 * px + 2 * py + pc) * m_per, m_per), :]

        def copy(k, block, to, src=None):
            return pltpu.make_async_remote_copy(
                src_ref=rows(*block) if src is None else src, dst_ref=rows(*block), send_sem=send_sems.at[k],
                recv_sem=recv_sems.at[k], device_id=to, device_id_type=MESH)

        mine = pltpu.make_async_copy(x_ref, rows(*me), local_sem)
        mine.start()
        first = [copy(0, me, sibling, src=x_ref)]
        first += [copy(1 + k, me, (*chip, c), src=x_ref) for k, chip in enumerate(chips)]
        for cp in first:
            cp.start()
        passed = [copy(4 + k, (*chip, c), sibling) for k, chip in enumerate(chips)]
        for k, chip in enumerate(chips):
            copy(1 + k, (*chip, c), me).wait_recv()
            passed[k].start()
        copy(0, sibling, me).wait_recv()
        for k, chip in enumerate(chips):
            copy(4 + k, (*chip, 1 - c), me).wait_recv()
        for cp in first + passed:
            cp.wait_send()
        mine.wait()

    return pl.pallas_call(
        body, name="ag_small", out_shape=jax.ShapeDtypeStruct((8 * m_per, n), v.dtype),
        in_specs=[pl.BlockSpec(memory_space=pltpu.VMEM)], out_specs=pl.BlockSpec(memory_space=pltpu.VMEM),
        scratch_shapes=[pltpu.SemaphoreType.DMA((7,)), pltpu.SemaphoreType.DMA((7,)), pltpu.SemaphoreType.DMA])(v)


def _sum_blocks(a, nblk, name):
    rows, wd = a.shape
    r = rows // nblk
    tr = min(r, ROW_TILE)
    assert r % tr == 0

    def body(*refs):
        acc = refs[0][...].astype(F32)
        for ref in refs[1:nblk]:
            acc = acc + ref[...].astype(F32)
        refs[nblk][...] = acc

    nt = r // tr
    return pl.pallas_call(
        body, name=name, grid=(nt,),
        in_specs=[pl.BlockSpec((tr, wd), functools.partial(lambda i, b: (b * nt + i, 0), b=b)) for b in range(nblk)],
        out_specs=pl.BlockSpec((tr, wd), lambda i: (i, 0)),
        out_shape=jax.ShapeDtypeStruct((r, wd), F32), compiler_params=_cparams("parallel"))(*([a] * nblk))


def _row_tile(rows):
    best = max(t for t in range(16, 513, 16) if rows % t == 0)
    return best


def _sum_chips(by_src, csum, j, name):
    n, rh, wd = by_src.shape
    tr = _row_tile(rh)

    def body(j_ref, *refs):
        own = refs[n][0].astype(F32)
        acc = None
        for k in range(n):
            term = jnp.where(j_ref[0] == k, own, refs[k][0].astype(F32))
            acc = term if acc is None else acc + term
        refs[n + 1][...] = acc

    def other(k):
        return pl.BlockSpec((1, tr, wd), lambda i, jr: (jnp.where(jr[0] == k, (k + 1) % n, k), i, 0))

    grid_spec = pltpu.PrefetchScalarGridSpec(
        num_scalar_prefetch=1, grid=(rh // tr,),
        in_specs=[other(k) for k in range(n)] + [pl.BlockSpec((1, tr, wd), lambda i, jr: (jr[0], i, 0))],
        out_specs=pl.BlockSpec((tr, wd), lambda i, jr: (i, 0)))
    return pl.pallas_call(
        body, name=name, grid_spec=grid_spec, out_shape=jax.ShapeDtypeStruct((rh, wd), F32),
        compiler_params=_cparams("parallel"))(jnp.reshape(j, (1,)).astype(jnp.int32), *([by_src] * n), csum)


def _add_halves(gpack, other, c, name):
    n, _, rh, wd = gpack.shape
    tr = _row_tile(rh)

    def body(c_ref, g_ref, o_ref, out_ref):
        out_ref[0] = (g_ref[0, 0] + o_ref[0]).astype(BF16)

    grid_spec = pltpu.PrefetchScalarGridSpec(
        num_scalar_prefetch=1, grid=(n, rh // tr),
        in_specs=[pl.BlockSpec((1, 1, tr, wd), lambda p, i, cr: (p, cr[0], i, 0)),
                  pl.BlockSpec((1, tr, wd), lambda p, i, cr: (p, i, 0))],
        out_specs=pl.BlockSpec((1, tr, wd), lambda p, i, cr: (p, i, 0)))
    return pl.pallas_call(
        body, name=name, grid_spec=grid_spec, out_shape=jax.ShapeDtypeStruct((n, rh, wd), BF16),
        compiler_params=_cparams("parallel", "parallel"))(jnp.reshape(c, (1,)).astype(jnp.int32), gpack, other)


PACK_W = 1024
ROWS_O_DN = DN_W // N_CHIPS
ROWS_O_DIL = DIL_W * (D_MODEL // N_CHIPS) // PACK_W
ROWS_OUT = D_MODEL // N_CHIPS
ROWS_CONV = 4 * (3 * DN_W // N_CHIPS) // PACK_W
R1 = ROWS_O_DN
R2 = R1 + ROWS_O_DIL
R3 = R2 + ROWS_OUT
R4 = R3 + ROWS_CONV
R5 = R4 + ROWS_CONV
PACK_ROWS = 1024
HALF_ROWS = PACK_ROWS // 2
SHARD_PAD = 2880


R6 = R5 + 2 * DN_HEADS

TILE_ROWS = 16
BA_IN_SHARD1 = REF_OFF_BA - SHARD_W
LOCAL_START = (0, SHARD_W, 2 * SHARD_W - 2 * DN_HEADS, 3 * SHARD_W - 2 * DN_HEADS)
LOCAL_END = LOCAL_START[1:] + (OFF_BA,)
WIN_BASE = tuple(s // TILE_ROWS * TILE_ROWS for s in LOCAL_START)


def _to_window(k, shard):
    body = jnp.concatenate([shard[:BA_IN_SHARD1], shard[BA_IN_SHARD1 + 2 * DN_HEADS:]], axis=0) if k == 1 else shard
    lead = LOCAL_START[k] - WIN_BASE[k]
    return jnp.pad(body, ((lead, SHARD_PAD - lead - body.shape[0]), (0, 0)))


def _from_window(k, win, ba):
    lead = LOCAL_START[k] - WIN_BASE[k]
    if k != 1:
        return win[lead:lead + SHARD_W]
    return jnp.concatenate([win[lead:lead + BA_IN_SHARD1], ba, win[lead + BA_IN_SHARD1:lead + SHARD_W - 2 * DN_HEADS]], axis=0)


def _stack_windows(wins, ba):
    pieces = []
    for k in range(N_CHIPS):
        lo = WIN_BASE[k] + (TILE_ROWS if k else 0)
        hi = LOCAL_END[k] // TILE_ROWS * TILE_ROWS
        pieces.append(wins[k][lo - WIN_BASE[k]:hi - WIN_BASE[k]])
        if k + 1 < N_CHIPS:
            assert hi == WIN_BASE[k + 1]
            pieces.append(wins[k][hi - WIN_BASE[k]:hi - WIN_BASE[k] + TILE_ROWS] + wins[k + 1][:TILE_ROWS])
    pieces += [ba, jnp.zeros((PW - OFF_BA - ba.shape[0], ba.shape[1]), ba.dtype)]
    out = jnp.concatenate(pieces, axis=0)
    assert out.shape[0] == PW
    return out


def _to_ref_layout(wpt):
    return jnp.concatenate([wpt[:REF_OFF_BA], wpt[OFF_BA:OFF_BA + 2 * DN_HEADS], wpt[REF_OFF_BA:OFF_BA]], axis=0)


def _from_ref_layout(wt):
    pad = jnp.zeros((PW - PROJ_W, wt.shape[1]), wt.dtype)
    return jnp.concatenate([wt[:REF_OFF_BA], wt[REF_OFF_BA + 2 * DN_HEADS:], wt[REF_OFF_BA:REF_OFF_BA + 2 * DN_HEADS], pad],
                           axis=0)


def _local_step(x, tgt, norm_w, wpt, conv_full, a_log, dt_bias, dn_norm_w, w_o_dn, w_o_dil, w_out, final_norm_w):
    s = x.shape[0]
    h, h_t = _rms_in(x, norm_w)
    proj = _matmul(h, wpt, F32, 2048, 1280, 1024, "proj", nt=True)
    c_pre, qkv = _conv_fwd(proj, conv_full)
    gate_par = jnp.zeros((8, 128), F32).at[0, 8:16].set(a_log[0]).at[1, 8:16].set(dt_bias[0])
    bg = _gates_fwd(proj, gate_par)
    o_a, u, w, vn, tmat, states = _gdr_fwd(qkv, bg)
    oa2, oa2_t = _gdr_out(o_a, proj, dn_norm_w)
    ya = _matmul(oa2, w_o_dn, F32, 512, 1024, 1024, "ya")
    parts = [_att_fwd(proj, g) for g in range(N_DIL)]
    ob, o_att, lse, ob_t = _att_merge(parts, proj)
    yb = _matmul(ob, w_o_dil, F32, 512, 1024, 512, "yb")
    mg, mg_t = _merge(proj, ya, yb)
    t = _matmul(mg, w_out, F32, 512, 1024, 1024, "t_out")
    dx2, dfw, lpart = _final(x, t, final_norm_w, tgt)

    dmg = _matmul(dx2, w_out, F32, 512, 1024, 1024, "d_merged", nt=True)
    dw_out = _matmul(mg_t, dx2, F32, 1024, 1024, 1024, "dw_out")
    dya, dyb, dga, dgb = _merge_bwd(proj, ya, yb, dmg)
    doa2 = _matmul(dya, w_o_dn, F32, 512, 1024, 1024, "d_oa2", nt=True)
    dw_o_dn = _matmul(oa2_t, dya, F32, 1024, 1024, 1024, "dw_o_dn")
    dob = _matmul(dyb, w_o_dil, F32, 512, 512, 1024, "d_ob", nt=True)
    dw_o_dil = _matmul(ob_t, dyb, F32, 512, 1024, 1024, "dw_o_dil")
    do_a, dz_a, ddnw = _gdr_out_bwd(o_a, proj, dn_norm_w, doa2)
    dq_a, dk_a, dv_a, dbg = _gdr_bwd(qkv, bg, u, w, vn, tmat, states, do_a)
    dba, dpar = _gates_bwd(proj, gate_par, dbg)
    dc = _conv_bwd_act(c_pre, dq_a, dk_a, dv_a)
    du_a, dconv = _conv_bwd(proj, dc, conv_full)
    do_att, delta, dz_b = _att_merge_bwd(o_att, proj, dob)
    dqkv_b = [_att_bwd(proj, g, do_att, lse, delta) for g in range(N_DIL)]
    dproj = jnp.concatenate(
        [du_a, dz_a] + [dqkv_b[g][i] for i in range(3) for g in range(N_DIL)]
        + [dz_b, dga, dgb, dba, jnp.zeros((s, PW - OFF_BA - 128), BF16)], axis=1)
    dh = _matmul(dproj, wpt, F32, 1024, 1024, 2304, "d_h")
    dwpt = _matmul(h_t, dproj, F32, 1024, 1280, 1024, "dw_in", transpose_out=True)
    grad_x, dnw = _rms_in_bwd(x, norm_w, dh, dx2)
    small = jnp.zeros((8, PACK_W), F32)
    small = small.at[0].set(dnw[0]).at[1].set(dfw[0]).at[2, :DN_D].set(ddnw[0])
    small = small.at[3, :DN_HEADS].set(dpar[0, 8:16]).at[3, DN_HEADS:2 * DN_HEADS].set(dpar[1, 8:16])
    small = small.at[4, 0].set(lpart[0, 0])
    return grad_x, dwpt, dconv, dw_o_dn, dw_o_dil, dw_out, small


def kernel(x, norm_w, w_in, conv_w, a_log, dt_bias, dn_norm_w, w_o_dn, w_o_dil, w_out, final_norm_w, loss_target, m_norm_w, m_w_in, m_conv_w, m_a_log, m_dt_bias, m_dn_norm_w, m_w_o_dn, m_w_o_dil, m_w_out, m_final_norm_w, v_norm_w, v_w_in, v_conv_w, v_a_log, v_dt_bias, v_dn_norm_w, v_w_o_dn, v_w_o_dil, v_w_out, v_final_norm_w):
    c = lax.axis_index("c")
    j = 2 * lax.axis_index("x") + lax.axis_index("y")
    qw = D_MODEL // N_CHIPS

    cw = conv_w[0].reshape(ROWS_CONV, PACK_W)
    cw_hi = cw.astype(BF16)
    cw_lo = (cw - cw_hi.astype(F32)).astype(BF16)
    shard = w_in[0].T.astype(BF16)
    own_ba = jnp.where(j == 1, shard[BA_IN_SHARD1:BA_IN_SHARD1 + 2 * DN_HEADS], jnp.zeros((2 * DN_HEADS, D_MODEL), BF16))
    pack = jnp.concatenate(
        [w_o_dn[0].astype(BF16), w_o_dil[0].astype(BF16).reshape(ROWS_O_DIL, PACK_W), w_out[0].astype(BF16), cw_hi, cw_lo,
         own_ba, jnp.zeros((PACK_ROWS - R6, PACK_W), BF16)], axis=0).reshape(2, HALF_ROWS, PACK_W)
    chips = range(N_CHIPS)
    own_win = lax.switch(j, [functools.partial(_to_window, k) for k in chips], shard).reshape(2, SHARD_PAD // 2, D_MODEL)
    all_in, allw = _ag_weights([own_win, pack])
    wins = [jnp.where(j == k, own_win, all_in[k]).reshape(SHARD_PAD, D_MODEL) for k in chips]
    allw = [jnp.where(j == k, pack, allw[k]).reshape(PACK_ROWS, PACK_W) for k in chips]
    wpt = _stack_windows(wins, allw[1][R5:R6])
    w_o_dn_full = jnp.concatenate([allw[k][:R1] for k in chips], axis=0)
    w_o_dil_full = jnp.concatenate([allw[k][R1:R2].reshape(DIL_W, qw) for k in chips], axis=1)
    w_out_full = jnp.concatenate([allw[k][R2:R3] for k in chips], axis=0)
    conv_full = jnp.concatenate(
        [(allw[k][R3:R4].astype(F32) + allw[k][R4:R5].astype(F32)).reshape(4, 3 * DN_W // N_CHIPS) for k in chips], axis=1)

    grad_x, dwpt, dconv, dw_o_dn, dw_o_dil, dw_out, small = _local_step(
        x[0], loss_target[0], norm_w, wpt, conv_full, a_log, dt_bias, dn_norm_w, w_o_dn_full, w_o_dil_full, w_out_full,
        final_norm_w.reshape(1, D_MODEL))

    cq = 3 * DN_W // N_CHIPS
    g_in = jnp.stack([dwpt[WIN_BASE[k]:WIN_BASE[k] + SHARD_PAD] for k in chips]).reshape(N_CHIPS, 2, SHARD_PAD // 2, D_MODEL)
    gpack = jnp.stack([
        jnp.concatenate(
            [dw_o_dn[k * qw:(k + 1) * qw], dw_o_dil[:, k * qw:(k + 1) * qw].reshape(ROWS_O_DIL, PACK_W),
             dw_out[k * qw:(k + 1) * qw], dconv[:, k * cq:(k + 1) * cq].reshape(ROWS_CONV, PACK_W),
             dwpt[OFF_BA:OFF_BA + 2 * DN_HEADS] if k == 1 else jnp.zeros((2 * DN_HEADS, PACK_W), F32),
             jnp.zeros((PACK_ROWS - R4 - 2 * DN_HEADS, PACK_W), F32)], axis=0)
        for k in chips]).reshape(N_CHIPS, 2, HALF_ROWS, PACK_W)
    sib_in, sib_pack = _rs_pair([g_in, gpack])
    csum_in = _add_halves(g_in, sib_in, c, "add_halves_in")
    csum_pack = _add_halves(gpack, sib_pack, c, "add_halves_pack")
    src_in, src_pack = _rs_chips([csum_in, csum_pack])
    half_in = _sum_chips(src_in, csum_in, j, "sum_chips_in")
    half_pack = _sum_chips(src_pack, csum_pack, j, "sum_chips_pack")
    sib_half_in, sib_half_pack = _pair_swap([half_in, half_pack])

    def both_halves(mine, theirs):
        return jnp.where(c == 0, jnp.concatenate([mine, theirs], axis=0), jnp.concatenate([theirs, mine], axis=0))

    g = both_halves(half_pack, sib_half_pack)
    g_w_in = lax.switch(j, [functools.partial(_from_window, k) for k in chips], both_halves(half_in, sib_half_in),
                        g[R4:R4 + 2 * DN_HEADS])
    g_w_o_dn = g[:R1]
    g_w_o_dil = g[R1:R2].reshape(DIL_W, qw)
    g_w_out = g[R2:R3]
    g_conv = g[R3:R4].reshape(4, cq)

    gs = _sum_blocks(_ag_small(small), 8, "sum_small")
    loss = gs[4, 0]
    w_small = jnp.zeros((8, PACK_W), F32)

    def pack_small(nw, fw, dnw_, al, db):
        t = w_small.at[0].set(nw[0]).at[1].set(fw).at[2, :DN_D].set(dnw_[0])
        return t.at[3, :DN_HEADS].set(al[0]).at[3, DN_HEADS:2 * DN_HEADS].set(db[0])

    sm = _adamw(pack_small(norm_w, final_norm_w, dn_norm_w, a_log, dt_bias), gs,
                pack_small(m_norm_w, m_final_norm_w, m_dn_norm_w, m_a_log, m_dt_bias),
                pack_small(v_norm_w, v_final_norm_w, v_dn_norm_w, v_a_log, v_dt_bias), "adamw_small")

    def unpack_small(t):
        return dict(norm_w=t[0:1], final_norm_w=t[1], dn_norm_w=t[2:3, :DN_D], a_log=t[3:4, :DN_HEADS],
                    dt_bias=t[3:4, DN_HEADS:2 * DN_HEADS])

    res = {"grad": unpack_small(gs)}
    for kind, arr in zip(("delta", "new_m", "new_v"), sm):
        res[kind] = unpack_small(arr)
    big = dict(conv_w=(conv_w, g_conv, m_conv_w, v_conv_w), w_o_dn=(w_o_dn, g_w_o_dn, m_w_o_dn, v_w_o_dn),
               w_o_dil=(w_o_dil, g_w_o_dil, m_w_o_dil, v_w_o_dil), w_out=(w_out, g_w_out, m_w_out, v_w_out))
    for name, (wt, gt, mt, vt) in big.items():
        d, nm, nv = _adamw(wt[0], gt, mt[0], vt[0], "adamw_" + name)
        res["grad"][name] = gt[None]
        res["delta"][name], res["new_m"][name], res["new_v"][name] = d[None], nm[None], nv[None]

    d, nm, nv = _adamw(w_in[0].T, g_w_in, m_w_in[0].T, v_w_in[0].T, "adamw_w_in")
    res["grad"]["w_in"] = g_w_in.T[None]
    res["delta"]["w_in"], res["new_m"]["w_in"], res["new_v"]["w_in"] = d.T[None], nm.T[None], nv.T[None]
    order = ["norm_w", "w_in", "conv_w", "a_log", "dt_bias", "dn_norm_w", "w_o_dn", "w_o_dil", "w_out", "final_norm_w"]
    outs = [loss, grad_x[None]]
    for kind in ("grad", "delta", "new_m", "new_v"):
        outs += [res[kind][nm] for nm in order]
    return tuple(outs)
```

```python
import functools
import math

import jax
import jax.numpy as jnp
from jax import lax
from jax.experimental import pallas as pl
from jax.experimental.pallas import tpu as pltpu

F32 = jnp.float32
BF16 = jnp.bfloat16
MESH = pl.DeviceIdType.MESH

D_MODEL = 1024
DN_HEADS = 8
DN_D = 128
DN_CHUNK = 64
DN_W = DN_HEADS * DN_D
DIL_GROUPS = ((128, 1), (512, 4), (2048, 16))
N_DIL = len(DIL_GROUPS)
DIL_HEADS = 4
DIL_DH = 128
DIL_W = DIL_HEADS * DIL_DH
ATT_BLOCK = 128
NORM_EPS = 1e-6
PROJ_W = 11280
N_CHIPS = 4
SHARD_W = PROJ_W // N_CHIPS

OFF_QKV_A = 0
OFF_Z_A = 3072
OFF_Q_B = 4096
OFF_K_B = 5632
OFF_V_B = 7168
OFF_Z_B = 8704
OFF_G_A = 9216
OFF_G_B = 10240
OFF_BA = 11264
PW = 11520
REF_OFF_BA = 4096

ADAM_LR = 0.001
ADAM_B1 = 0.9
ADAM_B2 = 0.999
ADAM_EPS = 1e-08
ADAM_WD = 0.01
ADAM_STEP = 10

ROW_TILE = 256
NEG = -1e30


def _dot(a, b):
    return jnp.dot(a.astype(BF16), b.astype(BF16), preferred_element_type=F32)


def _dot_nt(a, b):
    return lax.dot_general(a.astype(BF16), b.astype(BF16), (((1,), (1,)), ((), ())), preferred_element_type=F32)


def _dot_tn(a, b):
    return lax.dot_general(a.astype(BF16), b.astype(BF16), (((0,), (0,)), ((), ())), preferred_element_type=F32)


def _split(a):
    hi = a.astype(BF16)
    lo = (a - hi.astype(F32)).astype(BF16)
    return hi, lo


def _dot_exact_lhs(c, a):
    hi, lo = _split(a)
    cb = c.astype(BF16)
    return jnp.dot(cb, hi, preferred_element_type=F32) + jnp.dot(cb, lo, preferred_element_type=F32)


def _dot_exact_rhs(a, c):
    hi, lo = _split(a)
    cb = c.astype(BF16)
    return jnp.dot(hi, cb, preferred_element_type=F32) + jnp.dot(lo, cb, preferred_element_type=F32)


def _dot_tn_exact_rhs(a, c):
    hi, lo = _split(a)
    cb = c.astype(BF16)
    dn = (((0,), (0,)), ((), ()))
    return (lax.dot_general(hi, cb, dn, preferred_element_type=F32)
            + lax.dot_general(lo, cb, dn, preferred_element_type=F32))


def _sigmoid(x):
    return 1.0 / (1.0 + jnp.exp(-x))


def _silu(x):
    return x * _sigmoid(x)


def _silu_grad(x):
    s = _sigmoid(x)
    return s * (1.0 + x * (1.0 - s))


def _softplus(x):
    return jnp.maximum(x, 0.0) + jnp.log(1.0 + jnp.exp(-jnp.abs(x)))


def _cparams(*sem):
    return pltpu.CompilerParams(dimension_semantics=sem)


def _matmul(a, b, out_dtype, tm, tn, tk, name, nt=False, transpose_out=False, after=None):
    m, kdim = a.shape
    n = b.shape[0] if nt else b.shape[1]
    tm, tn, tk = min(tm, m), min(tn, n), min(tk, kdim)
    assert m % tm == 0 and n % tn == 0 and kdim % tk == 0, (name, a.shape, b.shape, tm, tn, tk)
    nk = kdim // tk
    dot = _dot_nt if nt else _dot
    b_spec = (pl.BlockSpec((tn, tk), lambda i, j, k: (j, k)) if nt else pl.BlockSpec((tk, tn), lambda i, j, k: (k, j)))

    def emit(o_ref, acc):
        o_ref[...] = (acc.T if transpose_out else acc).astype(o_ref.dtype)

    if nk == 1:
        def body(a_ref, b_ref, *rest):
            emit(rest[-1], dot(a_ref[...], b_ref[...]))
        scratch = []
    else:
        def body(a_ref, b_ref, *rest):
            o_ref, acc_ref = rest[-2:]
            k = pl.program_id(2)
            p = dot(a_ref[...], b_ref[...])

            @pl.when(k == 0)
            def _():
                acc_ref[...] = p

            @pl.when(k > 0)
            def _():
                acc_ref[...] += p

            @pl.when(k == nk - 1)
            def _():
                emit(o_ref, acc_ref[...])
        scratch = [pltpu.VMEM((tm, tn), F32)]

    if transpose_out:
        out_spec, out_shape = pl.BlockSpec((tn, tm), lambda i, j, k: (j, i)), (n, m)
    else:
        out_spec, out_shape = pl.BlockSpec((tm, tn), lambda i, j, k: (i, j)), (m, n)
    extra = [] if after is None else [after]
    return pl.pallas_call(
        body, name=name, grid=(m // tm, n // tn, nk),
        in_specs=[pl.BlockSpec((tm, tk), lambda i, j, k: (i, k)), b_spec] + [pl.BlockSpec(memory_space=pl.ANY)] * len(extra),
        out_specs=out_spec, out_shape=jax.ShapeDtypeStruct(out_shape, out_dtype), scratch_shapes=scratch,
        compiler_params=_cparams("parallel", "parallel", "arbitrary"))(a, b, *extra)


def _rms_in(x, nw):
    s, d = x.shape

    def body(x_ref, w_ref, h_ref, ht_ref):
        xv = x_ref[...]
        r = lax.rsqrt(jnp.mean(xv * xv, axis=-1, keepdims=True) + NORM_EPS)
        h = xv * r * w_ref[...]
        h_ref[...] = h.astype(BF16)
        ht_ref[...] = h.T.astype(BF16)

    return pl.pallas_call(
        body, name="rms_in", grid=(s // ROW_TILE,),
        in_specs=[pl.BlockSpec((ROW_TILE, d), lambda i: (i, 0)), pl.BlockSpec((1, d), lambda i: (0, 0))],
        out_specs=[pl.BlockSpec((ROW_TILE, d), lambda i: (i, 0)), pl.BlockSpec((d, ROW_TILE), lambda i: (0, i))],
        out_shape=[jax.ShapeDtypeStruct((s, d), BF16), jax.ShapeDtypeStruct((d, s), BF16)],
        compiler_params=_cparams("parallel"))(x, nw)


def _rms_in_bwd(x, nw, dh, dx2):
    s, d = x.shape

    def body(x_ref, w_ref, dh_ref, dx2_ref, dx_ref, dw_ref):
        i = pl.program_id(0)
        xv = x_ref[...]
        r = lax.rsqrt(jnp.mean(xv * xv, axis=-1, keepdims=True) + NORM_EPS)
        dhv = dh_ref[...]
        dyw = dhv * w_ref[...]
        dx_ref[...] = dx2_ref[...] + r * dyw - xv * (r * r * r) * jnp.mean(dyw * xv, axis=-1, keepdims=True)
        part = jnp.sum(dhv * xv * r, axis=0, keepdims=True)

        @pl.when(i == 0)
        def _():
            dw_ref[...] = part

        @pl.when(i > 0)
        def _():
            dw_ref[...] += part

    row = pl.BlockSpec((ROW_TILE, d), lambda i: (i, 0))
    vec = pl.BlockSpec((1, d), lambda i: (0, 0))
    return pl.pallas_call(
        body, name="rms_in_bwd", grid=(s // ROW_TILE,), in_specs=[row, vec, row, row], out_specs=[row, vec],
        out_shape=[jax.ShapeDtypeStruct((s, d), F32), jax.ShapeDtypeStruct((1, d), F32)],
        compiler_params=_cparams("arbitrary"))(x, nw, dh, dx2)


def _shift_down(cur, prev8, k):
    rc = pltpu.roll(cur, k, 0)
    rp = pltpu.roll(prev8, k, 0)
    row = lax.broadcasted_iota(jnp.int32, prev8.shape, 0)
    top = jnp.where(row < k, rp, rc[:8])
    return jnp.concatenate([top, rc[8:]], axis=0)


def _shift_up(cur, next8, k):
    t = cur.shape[0]
    rc = pltpu.roll(cur, t - k, 0)
    rn = pltpu.roll(next8, 8 - k, 0)
    row = lax.broadcasted_iota(jnp.int32, next8.shape, 0)
    bot = jnp.where(row >= 8 - k, rn, rc[t - 8:])
    return jnp.concatenate([rc[:t - 8], bot], axis=0)


def _conv_fwd(proj, conv_w):
    s = proj.shape[0]
    t8 = ROW_TILE // 8

    def body(u_ref, up_ref, w_ref, c_ref, y_ref):
        i = pl.program_id(0)
        part = pl.program_id(1)
        cur = u_ref[...]
        prev8 = jnp.where(i > 0, up_ref[...], 0.0)
        w = w_ref[...]
        c = cur * w[3:4, :]
        for k in (1, 2, 3):
            c = c + _shift_down(cur, prev8, k) * w[3 - k:4 - k, :]
        c_ref[...] = c
        a = _silu(c)
        for h in range(DN_HEADS):
            ah = a[:, h * DN_D:(h + 1) * DN_D]
            r = lax.rsqrt(jnp.sum(ah * ah, axis=-1, keepdims=True) + NORM_EPS)
            y_ref[:, h * DN_D:(h + 1) * DN_D] = jnp.where(part < 2, ah * r, ah)

    return pl.pallas_call(
        body, name="conv_fwd", grid=(s // ROW_TILE, 3),
        in_specs=[pl.BlockSpec((ROW_TILE, DN_W), lambda i, p: (i, p)),
                  pl.BlockSpec((8, DN_W), lambda i, p: (jnp.maximum(i * t8 - 1, 0), p)),
                  pl.BlockSpec((4, DN_W), lambda i, p: (0, p))],
        out_specs=[pl.BlockSpec((ROW_TILE, DN_W), lambda i, p: (i, p))] * 2,
        out_shape=[jax.ShapeDtypeStruct((s, 3 * DN_W), F32)] * 2,
        compiler_params=_cparams("parallel", "parallel"))(proj, proj, conv_w)


def _conv_bwd_act(c, dq, dk, dv):
    s = c.shape[0]

    def body(c_ref, dq_ref, dk_ref, dv_ref, dc_ref):
        for part, d_ref in enumerate((dq_ref, dk_ref, dv_ref)):
            for h in range(DN_HEADS):
                sl = slice(part * DN_W + h * DN_D, part * DN_W + (h + 1) * DN_D)
                ch = c_ref[:, sl]
                dyh = d_ref[:, h * DN_D:(h + 1) * DN_D]
                if part < 2:
                    ah = _silu(ch)
                    r = lax.rsqrt(jnp.sum(ah * ah, axis=-1, keepdims=True) + NORM_EPS)
                    dyh = r * dyh - ah * (r * r * r) * jnp.sum(dyh * ah, axis=-1, keepdims=True)
                dc_ref[:, sl] = dyh * _silu_grad(ch)

    wide = pl.BlockSpec((ROW_TILE, 3 * DN_W), lambda i: (i, 0))
    row = pl.BlockSpec((ROW_TILE, DN_W), lambda i: (i, 0))
    return pl.pallas_call(
        body, name="conv_bwd_act", grid=(s // ROW_TILE,), in_specs=[wide, row, row, row], out_specs=wide,
        out_shape=jax.ShapeDtypeStruct((s, 3 * DN_W), F32), compiler_params=_cparams("parallel"))(c, dq, dk, dv)


def _conv_bwd(proj, dc, conv_w):
    s = proj.shape[0]
    t8 = ROW_TILE // 8
    nrow = s // ROW_TILE
    last8 = s // 8 - 1

    def body(u_ref, up_ref, dc_ref, dcn_ref, w_ref, du_ref, dw_ref):
        i = pl.program_id(1)
        cur = u_ref[...]
        prev8 = jnp.where(i > 0, up_ref[...], 0.0)
        dcv = dc_ref[...]
        next8 = jnp.where(i < nrow - 1, dcn_ref[...], 0.0)
        w = w_ref[...]
        du = dcv * w[3:4, :]
        for k in (1, 2, 3):
            du = du + _shift_up(dcv, next8, k) * w[3 - k:4 - k, :]
        du_ref[...] = du.astype(BF16)

        @pl.when(i == 0)
        def _():
            dw_ref[...] = jnp.zeros_like(dw_ref)

        dw_ref[3:4, :] += jnp.sum(cur * dcv, axis=0, keepdims=True)
        for k in (1, 2, 3):
            dw_ref[3 - k:4 - k, :] += jnp.sum(_shift_down(cur, prev8, k) * dcv, axis=0, keepdims=True)

    blk = pl.BlockSpec((ROW_TILE, DN_W), lambda p, i: (i, p))
    return pl.pallas_call(
        body, name="conv_bwd", grid=(3, nrow),
        in_specs=[blk, pl.BlockSpec((8, DN_W), lambda p, i: (jnp.maximum(i * t8 - 1, 0), p)),
                  blk, pl.BlockSpec((8, DN_W), lambda p, i: (jnp.minimum((i + 1) * t8, last8), p)),
                  pl.BlockSpec((4, DN_W), lambda p, i: (0, p))],
        out_specs=[blk, pl.BlockSpec((4, DN_W), lambda p, i: (0, p))],
        out_shape=[jax.ShapeDtypeStruct((s, 3 * DN_W), BF16), jax.ShapeDtypeStruct((4, 3 * DN_W), F32)],
        compiler_params=_cparams("parallel", "arbitrary"))(proj, proj, dc, dc, conv_w)


def _gates_fwd(proj, gate_par):
    s = proj.shape[0]

    def body(ba_ref, par_ref, o_ref):
        v = ba_ref[...]
        lane = lax.broadcasted_iota(jnp.int32, v.shape, 1)
        beta = _sigmoid(v)
        g = -jnp.exp(par_ref[0:1, :]) * _softplus(v + par_ref[1:2, :])
        o_ref[...] = jnp.where(lane < DN_HEADS, beta, jnp.where(lane < 2 * DN_HEADS, g, 0.0))

    return pl.pallas_call(
        body, name="gates_fwd", grid=(s // ROW_TILE,),
        in_specs=[pl.BlockSpec((ROW_TILE, 128), lambda i: (i, OFF_BA // 128)), pl.BlockSpec((8, 128), lambda i: (0, 0))],
        out_specs=pl.BlockSpec((ROW_TILE, 128), lambda i: (i, 0)),
        out_shape=jax.ShapeDtypeStruct((s, 128), F32), compiler_params=_cparams("parallel"))(proj, gate_par)


def _gates_bwd(proj, gate_par, dbg):
    s = proj.shape[0]

    def body(ba_ref, par_ref, d_ref, o_ref, dpar_ref):
        i = pl.program_id(0)
        v = ba_ref[...]
        dv = d_ref[...]
        lane = lax.broadcasted_iota(jnp.int32, v.shape, 1)
        beta = _sigmoid(v)
        nega = -jnp.exp(par_ref[0:1, :])
        xs = v + par_ref[1:2, :]
        dsp = dv * nega * _sigmoid(xs)
        dal = dv * nega * _softplus(xs)
        is_b = lane < DN_HEADS
        is_g = jnp.logical_and(lane >= DN_HEADS, lane < 2 * DN_HEADS)
        o_ref[...] = jnp.where(is_b, dv * beta * (1.0 - beta), jnp.where(is_g, dsp, 0.0)).astype(BF16)
        r0 = jnp.sum(jnp.where(is_g, dal, 0.0), axis=0, keepdims=True)
        r1 = jnp.sum(jnp.where(is_g, dsp, 0.0), axis=0, keepdims=True)

        @pl.when(i == 0)
        def _():
            dpar_ref[...] = jnp.zeros_like(dpar_ref)

        dpar_ref[0:1, :] += r0
        dpar_ref[1:2, :] += r1

    return pl.pallas_call(
        body, name="gates_bwd", grid=(s // ROW_TILE,),
        in_specs=[pl.BlockSpec((ROW_TILE, 128), lambda i: (i, OFF_BA // 128)), pl.BlockSpec((8, 128), lambda i: (0, 0)),
                  pl.BlockSpec((ROW_TILE, 128), lambda i: (i, 0))],
        out_specs=[pl.BlockSpec((ROW_TILE, 128), lambda i: (i, 0)), pl.BlockSpec((8, 128), lambda i: (0, 0))],
        out_shape=[jax.ShapeDtypeStruct((s, 128), BF16), jax.ShapeDtypeStruct((8, 128), F32)],
        compiler_params=_cparams("arbitrary"))(proj, gate_par, dbg)


def _chunk_masks():
    c = DN_CHUNK
    ii = lax.broadcasted_iota(jnp.int32, (c, c), 0)
    jj = lax.broadcasted_iota(jnp.int32, (c, c), 1)
    return dict(ii=ii, jj=jj, lower=(ii >= jj), strict=(ii > jj), eye=(ii == jj),
                lower_f=(ii >= jj).astype(BF16), upper_f=(ii <= jj).astype(BF16), ones8=jnp.ones((8, c), BF16))


class _Heads:
    def __init__(self, xs):
        self.xs = list(xs)

    def _bin(self, o, f):
        if isinstance(o, _Heads):
            return _Heads([f(a, b) for a, b in zip(self.xs, o.xs)])
        return _Heads([f(a, o) for a in self.xs])

    def __add__(self, o):
        return self._bin(o, lambda a, b: a + b)

    def __sub__(self, o):
        return self._bin(o, lambda a, b: a - b)

    def __mul__(self, o):
        return self._bin(o, lambda a, b: a * b)

    __radd__ = __add__
    __rmul__ = __mul__

    def __neg__(self):
        return _Heads([-a for a in self.xs])

    def __getitem__(self, i):
        return _Heads([a[i] for a in self.xs])


def _hmap(f, *args):
    n = next(len(a.xs) for a in args if isinstance(a, _Heads))
    return _Heads([f(*[(a.xs[h] if isinstance(a, _Heads) else a) for a in args]) for h in range(n)])


def _hdot(a, b):
    return _hmap(_dot, a, b)


def _hdot_nt(a, b):
    return _hmap(_dot_nt, a, b)


def _hdot_tn(a, b):
    return _hmap(_dot_tn, a, b)


def _hsum(a, axis):
    return _hmap(lambda t: jnp.sum(t, axis=axis, keepdims=True), a)


def _hwhere(c, a, b):
    return _hmap(jnp.where, c, a, b)


def _chunk_common(mk, q, k, beta_col, g_col):
    c = DN_CHUNK
    lower, strict = mk["lower"], mk["strict"]
    qs = q * (DN_D ** -0.5)
    beta_b = _hmap(lambda t: jnp.broadcast_to(t, (c, DN_D)), beta_col)
    g_b = _hmap(lambda t: jnp.broadcast_to(t, (c, DN_D)), g_col)
    gc_b = _hmap(_dot_exact_lhs, mk["lower_f"], g_b)
    gc_sq = gc_b[:, :c]
    gc_r = _hmap(_dot_exact_lhs, mk["ones8"], _hwhere(mk["eye"], gc_sq, 0.0))[0:1, :]
    gam = _hwhere(lower, _hmap(lambda t: jnp.exp(jnp.minimum(t, 0.0)), gc_sq - gc_r), 0.0)
    egc = _hmap(jnp.exp, gc_b)
    gl = gc_b[c - 1:c, :]
    ekd = _hmap(jnp.exp, gl - gc_b)
    dl = _hmap(jnp.exp, gl)
    kb = k * beta_b
    a_strict = _hwhere(strict, _hdot_nt(kb, k) * gam, 0.0)
    aqk = _hwhere(lower, _hdot_nt(qs, k) * gam, 0.0)
    return dict(k=k, qs=qs, beta_b=beta_b, gc_b=gc_b, gam=gam, egc=egc, ekd=ekd, dl=dl, kb=kb, a_strict=a_strict, aqk=aqk)


def _unit_lower_inverse_minus_eye(n_strict, ii, jj):
    same = lax.shift_right_logical(ii, 4) == lax.shift_right_logical(jj, 4)
    dmat = _hwhere(same, n_strict, 0.0)
    omat = n_strict - dmat
    d2 = _hdot(dmat, dmat)
    d4 = _hdot(d2, d2)
    d8 = _hdot(d4, d4)
    x1 = d2 - dmat - _hdot(dmat, d2)
    x2 = x1 + d4 + _hdot(x1, d4)
    x3 = x2 + d8 + _hdot(x2, d8)
    n1 = omat + _hdot(x3, omat)
    n2 = _hdot(n1, n1)
    y = n2 - n1 - _hdot(n1, n2)
    return y + x3 + _hdot(y, x3)


def _gdr_fwd(qkv, bg):
    s = qkv.shape[0]
    c = DN_CHUNK
    n = s // c

    def body(q_ref, k_ref, v_ref, bg_ref, o_ref, u_ref, w_ref, vn_ref, tm_ref, st_ref, state):
        @pl.when(pl.program_id(0) == 0)
        def _():
            state[...] = jnp.zeros_like(state)

        mk = _chunk_masks()
        bg = bg_ref[...]
        hs = range(DN_HEADS)
        sls = [slice(h * DN_D, (h + 1) * DN_D) for h in hs]
        cm = _chunk_common(mk, _Heads(q_ref[:, sl] for sl in sls), _Heads(k_ref[:, sl] for sl in sls),
                           _Heads(bg[:, h:h + 1] for h in hs), _Heads(bg[:, DN_HEADS + h:DN_HEADS + h + 1] for h in hs))
        tm = _unit_lower_inverse_minus_eye(cm["a_strict"], mk["ii"], mk["jj"])
        rhs_u = _Heads(v_ref[:, sl] for sl in sls) * cm["beta_b"]
        rhs_w = cm["kb"] * cm["egc"]
        u = rhs_u + _hdot(tm, rhs_u)
        w = rhs_w + _hdot(tm, rhs_w)
        st = _Heads(state[h] for h in hs)
        v_new = u - _hdot(w, st)
        o = _hdot(cm["qs"] * cm["egc"], st) + _hdot(cm["aqk"], v_new)
        st_new = st * cm["dl"] + _hdot_tn(cm["k"] * cm["ekd"], v_new)
        for h, sl in zip(hs, sls):
            o_ref[:, sl] = o.xs[h]
            u_ref[:, sl] = u.xs[h]
            w_ref[:, sl] = w.xs[h]
            vn_ref[:, sl] = v_new.xs[h]
            tm_ref[h, 0] = tm.xs[h]
            st_ref[h, 0] = st.xs[h]
            state[h] = st_new.xs[h]

    def part(p):
        return pl.BlockSpec((c, DN_W), lambda j: (j, p))

    return pl.pallas_call(
        body, name="gdr_fwd", grid=(n,),
        in_specs=[part(0), part(1), part(2), pl.BlockSpec((c, 128), lambda j: (j, 0))],
        out_specs=[part(0)] * 4 + [pl.BlockSpec((DN_HEADS, 1, c, c), lambda j: (0, j, 0, 0)),
                                   pl.BlockSpec((DN_HEADS, 1, DN_D, DN_D), lambda j: (0, j, 0, 0))],
        out_shape=[jax.ShapeDtypeStruct((s, DN_W), F32)] * 4
        + [jax.ShapeDtypeStruct((DN_HEADS, n, c, c), F32), jax.ShapeDtypeStruct((DN_HEADS, n, DN_D, DN_D), F32)],
        scratch_shapes=[pltpu.VMEM((DN_HEADS, DN_D, DN_D), F32)],
        compiler_params=_cparams("arbitrary"))(qkv, qkv, qkv, bg)


def _gdr_bwd(qkv, bg, u, w, vn, tmat, states, do):
    s = qkv.shape[0]
    c = DN_CHUNK
    n = s // c

    def body(q_ref, k_ref, v_ref, bg_ref, u_ref, w_ref, vn_ref, tm_ref, st_ref, do_ref,
             dq_ref, dk_ref, dv_ref, dbg_ref, dstate):
        @pl.when(pl.program_id(0) == 0)
        def _():
            dstate[...] = jnp.zeros_like(dstate)

        mk = _chunk_masks()
        lower, strict = mk["lower"], mk["strict"]
        bg = bg_ref[...]
        ones = jnp.ones((c, DN_D), BF16)
        rowi = lax.broadcasted_iota(jnp.int32, (c, DN_D), 0)
        lane = lax.broadcasted_iota(jnp.int32, (c, 128), 1)
        hs = range(DN_HEADS)
        sls = [slice(h * DN_D, (h + 1) * DN_D) for h in hs]

        def heads_of(ref):
            return _Heads(ref[:, sl] for sl in sls)

        cm = _chunk_common(mk, heads_of(q_ref), heads_of(k_ref),
                           _Heads(bg[:, h:h + 1] for h in hs), _Heads(bg[:, DN_HEADS + h:DN_HEADS + h + 1] for h in hs))
        k, qs, beta_b = cm["k"], cm["qs"], cm["beta_b"]
        gam, egc, ekd, dl, kb = cm["gam"], cm["egc"], cm["ekd"], cm["dl"], cm["kb"]
        aqk, a_strict = cm["aqk"], cm["a_strict"]
        v, uu, ww, v_new, dov = heads_of(v_ref), heads_of(u_ref), heads_of(w_ref), heads_of(vn_ref), heads_of(do_ref)
        tm = _Heads(tm_ref[h, 0] for h in hs)
        st = _Heads(st_ref[h, 0] for h in hs)
        dsn = _Heads(dstate[h] for h in hs)
        qd = qs * egc
        kd = k * ekd

        dv_new = _hdot_tn(aqk, dov) + _hdot(kd, dsn)
        daqk = _hwhere(lower, _hdot_nt(dov, v_new), 0.0)
        dqd = _hdot_nt(dov, st)
        dkd = _hdot_nt(v_new, dsn)
        ddl = _hsum(_hsum(dsn * st, 1), 0)
        dw = -_hdot_nt(dv_new, st)
        ds_new = dsn * dl + _hdot_tn(qd, dov) - _hdot_tn(ww, dv_new)

        dru = dv_new + _hdot_tn(tm, dv_new)
        drw = dw + _hdot_tn(tm, dw)
        dn = _hwhere(strict, -(_hdot_nt(dru, uu) + _hdot_nt(drw, ww)), 0.0)
        dag = dn * gam
        dkb = _hdot(dag, k) + drw * egc
        dk = _hdot_tn(dag, kb)
        dqg = daqk * gam
        dqs = _hdot(dqg, k) + dqd * egc
        dk = dk + _hdot_tn(dqg, qs) + dkb * beta_b + dkd * ekd
        pmat = dn * a_strict + daqk * aqk
        tkd = _hsum(dkd * kd, -1)
        dgc = (_hsum(pmat, -1) - _hmap(_dot_tn_exact_rhs, pmat, ones) + _hsum(drw * (kb * egc), -1)
               + _hsum(dqd * qd, -1) - tkd)
        last = _hsum(tkd, 0) + ddl * dl
        dgc = dgc + _hwhere(rowi == c - 1, last, 0.0)
        dg = _hmap(_dot_exact_lhs, mk["upper_f"], dgc)
        dbeta = _hsum(dru * v, -1) + _hsum(dkb * k, -1)
        dq = dqs * (DN_D ** -0.5)
        dv = dru * beta_b

        dbg = jnp.zeros((c, 128), F32)
        for h, sl in zip(hs, sls):
            dq_ref[:, sl] = dq.xs[h]
            dk_ref[:, sl] = dk.xs[h]
            dv_ref[:, sl] = dv.xs[h]
            dstate[h] = ds_new.xs[h]
            dbg = dbg + jnp.where(lane == h, dbeta.xs[h], 0.0) + jnp.where(lane == DN_HEADS + h, dg.xs[h], 0.0)
        dbg_ref[...] = dbg

    def part(p):
        return pl.BlockSpec((c, DN_W), lambda j: (n - 1 - j, p))

    vec = pl.BlockSpec((c, 128), lambda j: (n - 1 - j, 0))
    return pl.pallas_call(
        body, name="gdr_bwd", grid=(n,),
        in_specs=[part(0), part(1), part(2), vec, part(0), part(0), part(0),
                  pl.BlockSpec((DN_HEADS, 1, c, c), lambda j: (0, n - 1 - j, 0, 0)),
                  pl.BlockSpec((DN_HEADS, 1, DN_D, DN_D), lambda j: (0, n - 1 - j, 0, 0)), part(0)],
        out_specs=[part(0), part(0), part(0), vec],
        out_shape=[jax.ShapeDtypeStruct((s, DN_W), F32)] * 3 + [jax.ShapeDtypeStruct((s, 128), F32)],
        scratch_shapes=[pltpu.VMEM((DN_HEADS, DN_D, DN_D), F32)],
        compiler_params=_cparams("arbitrary"))(qkv, qkv, qkv, bg, u, w, vn, tmat, states, do)


def _gdr_out(o, proj, dnw):
    s = o.shape[0]

    def body(o_ref, z_ref, w_ref, y_ref, yt_ref):
        ov, zv, wv = o_ref[...], z_ref[...], w_ref[...]
        for h in range(DN_HEADS):
            sl = slice(h * DN_D, (h + 1) * DN_D)
            oh = ov[:, sl]
            r = lax.rsqrt(jnp.mean(oh * oh, axis=-1, keepdims=True) + NORM_EPS)
            y = (oh * r * wv) * _silu(zv[:, sl])
            y_ref[:, sl] = y.astype(BF16)
            yt_ref[sl, :] = y.T.astype(BF16)

    row = pl.BlockSpec((ROW_TILE, DN_W), lambda i: (i, 0))
    return pl.pallas_call(
        body, name="gdr_out", grid=(s // ROW_TILE,),
        in_specs=[row, pl.BlockSpec((ROW_TILE, DN_W), lambda i: (i, OFF_Z_A // DN_W)), pl.BlockSpec((1, DN_D), lambda i: (0, 0))],
        out_specs=[row, pl.BlockSpec((DN_W, ROW_TILE), lambda i: (0, i))],
        out_shape=[jax.ShapeDtypeStruct((s, DN_W), BF16), jax.ShapeDtypeStruct((DN_W, s), BF16)],
        compiler_params=_cparams("parallel"))(o, proj, dnw)


def _gdr_out_bwd(o, proj, dnw, dy):
    s = o.shape[0]

    def body(o_ref, z_ref, w_ref, dy_ref, do_ref, dz_ref, dw_ref):
        i = pl.program_id(0)
        ov, zv, wv, dyv = o_ref[...], z_ref[...], w_ref[...], dy_ref[...]
        acc = jnp.zeros((1, DN_D), F32)
        for h in range(DN_HEADS):
            sl = slice(h * DN_D, (h + 1) * DN_D)
            oh, zh, dh = ov[:, sl], zv[:, sl], dyv[:, sl]
            r = lax.rsqrt(jnp.mean(oh * oh, axis=-1, keepdims=True) + NORM_EPS)
            dn = dh * _silu(zh)
            dz_ref[:, sl] = (dh * (oh * r * wv) * _silu_grad(zh)).astype(BF16)
            acc = acc + jnp.sum(dn * oh * r, axis=0, keepdims=True)
            dnw_ = dn * wv
            do_ref[:, sl] = r * dnw_ - oh * (r * r * r) * jnp.mean(dnw_ * oh, axis=-1, keepdims=True)

        @pl.when(i == 0)
        def _():
            dw_ref[...] = acc

        @pl.when(i > 0)
        def _():
            dw_ref[...] += acc

    row = pl.BlockSpec((ROW_TILE, DN_W), lambda i: (i, 0))
    vec = pl.BlockSpec((1, DN_D), lambda i: (0, 0))
    return pl.pallas_call(
        body, name="gdr_out_bwd", grid=(s // ROW_TILE,),
        in_specs=[row, pl.BlockSpec((ROW_TILE, DN_W), lambda i: (i, OFF_Z_A // DN_W)), vec, row],
        out_specs=[row, row, vec],
        out_shape=[jax.ShapeDtypeStruct((s, DN_W), F32), jax.ShapeDtypeStruct((s, DN_W), BF16),
                   jax.ShapeDtypeStruct((1, DN_D), F32)],
        compiler_params=_cparams("arbitrary"))(o, proj, dnw, dy)


def _slope(group, head):
    idx = (group * DIL_HEADS + head + 1).astype(F32)
    return jnp.exp(jnp.full((1, 128), -8.0 * math.log(2.0) / (N_DIL * DIL_HEADS), F32) * idx)


def _att_scores(qb, k_cur, k_prev, slope_d, has_prev):
    iq = lax.broadcasted_iota(jnp.int32, (ATT_BLOCK, ATT_BLOCK), 0)
    jk = lax.broadcasted_iota(jnp.int32, (ATT_BLOCK, ATT_BLOCK), 1)
    dist_c = (iq - jk).astype(F32)
    s_cur = jnp.where(iq >= jk, _dot_nt(qb, k_cur) - slope_d * dist_c, NEG)
    s_prev = jnp.where(jnp.logical_and(jk >= iq, has_prev),
                       _dot_nt(qb, k_prev) - slope_d * (dist_c + float(ATT_BLOCK)), NEG)
    return s_cur, s_prev


ATT_UNROLL = 4


def _att_blocks(i, dil, nb):
    per = dil * nb // ATT_UNROLL
    assert per * ATT_UNROLL == dil * nb
    for i0 in range(per):
        blocks = [divmod(i0 + u * per, nb) for u in range(ATT_UNROLL)]
        assert all(a[0] != b[0] or abs(a[1] - b[1]) >= 2 for n, a in enumerate(blocks) for b in blocks[n + 1:])
    curs, prvs, has_prev = [], [], []
    for u in range(ATT_UNROLL):
        t = i + u * per
        r = lax.div(t, nb)
        j = lax.rem(t, nb)
        base = r + dil * ATT_BLOCK * j
        pbase = base - dil * ATT_BLOCK * jnp.minimum(j, 1)
        if dil == 1:
            base, pbase = pl.multiple_of(base, ATT_BLOCK), pl.multiple_of(pbase, ATT_BLOCK)
        curs.append(pl.ds(base, ATT_BLOCK, stride=dil))
        prvs.append(pl.ds(pbase, ATT_BLOCK, stride=dil))
        has_prev.append(j > 0)
    return curs, prvs, has_prev


def _att_fwd(proj, group):
    s = proj.shape[0]
    dil = DIL_GROUPS[group][1]
    assert DIL_GROUPS[group][0] // dil == ATT_BLOCK
    nb = s // dil // ATT_BLOCK
    assert nb * dil * ATT_BLOCK == s

    def body(q_ref, k_ref, v_ref, num_ref, den_ref, mx_ref):
        slope_d = _slope(group, pl.program_id(0)) * float(dil)

        def step(i, carry):
            curs, prvs, has_prev = _att_blocks(i, dil, nb)
            us = range(ATT_UNROLL)
            qb = [q_ref[c, :] * (DIL_DH ** -0.5) for c in curs]
            sc = [_att_scores(qb[u], k_ref[curs[u], :], k_ref[prvs[u], :], slope_d, has_prev[u]) for u in us]
            mx = [jnp.maximum(jnp.max(a, axis=-1, keepdims=True), jnp.max(b, axis=-1, keepdims=True)) for a, b in sc]
            p_cur = [jnp.exp(sc[u][0] - mx[u]) for u in us]
            p_prev = [jnp.exp(sc[u][1] - mx[u]) for u in us]
            den = [jnp.sum(p_cur[u], axis=-1, keepdims=True) + jnp.sum(p_prev[u], axis=-1, keepdims=True) for u in us]
            num = [_dot(p_cur[u], v_ref[curs[u], :]) + _dot(p_prev[u], v_ref[prvs[u], :]) for u in us]
            for u in us:
                num_ref[curs[u], :] = num[u]
                den_ref[curs[u], :] = jnp.broadcast_to(den[u], (ATT_BLOCK, DIL_DH))
                mx_ref[curs[u], :] = jnp.broadcast_to(mx[u], (ATT_BLOCK, DIL_DH))
            return carry

        lax.fori_loop(0, dil * nb // ATT_UNROLL, step, 0)

    def col(off):
        return pl.BlockSpec((s, DIL_DH), lambda h: (0, off // DIL_DH + group * DIL_HEADS + h))

    out = pl.BlockSpec((s, DIL_DH), lambda h: (0, h))
    return pl.pallas_call(
        body, name=f"att_fwd{group}", grid=(DIL_HEADS,), in_specs=[col(OFF_Q_B), col(OFF_K_B), col(OFF_V_B)],
        out_specs=[out, out, out], out_shape=[jax.ShapeDtypeStruct((s, DIL_W), F32)] * 3,
        compiler_params=_cparams("parallel"))(proj, proj, proj)


def _att_bwd(proj, group, do, lse, delta):
    s = proj.shape[0]
    dil = DIL_GROUPS[group][1]
    nb = s // dil // ATT_BLOCK

    def body(q_ref, k_ref, v_ref, do_ref, lse_ref, dl_ref, dq_ref, dk_ref, dv_ref, dq_acc, dk_acc, dv_acc):
        slope_d = _slope(group, pl.program_id(0)) * float(dil)
        dk_acc[...] = jnp.zeros_like(dk_acc)
        dv_acc[...] = jnp.zeros_like(dv_acc)

        def step(i, carry):
            curs, prvs, has_prev = _att_blocks(i, dil, nb)
            us = range(ATT_UNROLL)
            qb = [q_ref[c, :] * (DIL_DH ** -0.5) for c in curs]
            k_cur, k_prev = [k_ref[c, :] for c in curs], [k_ref[p, :] for p in prvs]
            v_cur, v_prev = [v_ref[c, :] for c in curs], [v_ref[p, :] for p in prvs]
            sc = [_att_scores(qb[u], k_cur[u], k_prev[u], slope_d, has_prev[u]) for u in us]
            lse_b, delta_b, dob = [lse_ref[c, :] for c in curs], [dl_ref[c, :] for c in curs], [do_ref[c, :] for c in curs]
            p_cur = [jnp.exp(sc[u][0] - lse_b[u]) for u in us]
            p_prev = [jnp.exp(sc[u][1] - lse_b[u]) for u in us]
            ds_cur = [p_cur[u] * (_dot_nt(dob[u], v_cur[u]) - delta_b[u]) for u in us]
            ds_prev = [p_prev[u] * (_dot_nt(dob[u], v_prev[u]) - delta_b[u]) for u in us]
            dq = [(_dot(ds_cur[u], k_cur[u]) + _dot(ds_prev[u], k_prev[u])) * (DIL_DH ** -0.5) for u in us]
            dk_c = [_dot_tn(ds_cur[u], qb[u]) for u in us]
            dv_c = [_dot_tn(p_cur[u], dob[u]) for u in us]
            dk_p = [_dot_tn(ds_prev[u], qb[u]) for u in us]
            dv_p = [_dot_tn(p_prev[u], dob[u]) for u in us]
            for u in us:
                dq_acc[curs[u], :] = dq[u]
                dk_acc[curs[u], :] += dk_c[u]
                dv_acc[curs[u], :] += dv_c[u]
            for u in us:
                dk_acc[prvs[u], :] += dk_p[u]
                dv_acc[prvs[u], :] += dv_p[u]
            return carry

        lax.fori_loop(0, dil * nb // ATT_UNROLL, step, 0)
        dq_ref[...] = dq_acc[...].astype(BF16)
        dk_ref[...] = dk_acc[...].astype(BF16)
        dv_ref[...] = dv_acc[...].astype(BF16)

    def col(off):
        return pl.BlockSpec((s, DIL_DH), lambda h: (0, off // DIL_DH + group * DIL_HEADS + h))

    hd = pl.BlockSpec((s, DIL_DH), lambda h: (0, h))
    return pl.pallas_call(
        body, name=f"att_bwd{group}", grid=(DIL_HEADS,),
        in_specs=[col(OFF_Q_B), col(OFF_K_B), col(OFF_V_B), hd, hd, hd], out_specs=[hd, hd, hd],
        out_shape=[jax.ShapeDtypeStruct((s, DIL_W), BF16)] * 3,
        scratch_shapes=[pltpu.VMEM((s, DIL_DH), F32)] * 3,
        compiler_params=_cparams("parallel"))(proj, proj, proj, do, lse, delta)


def _att_merge(parts, proj):
    s = proj.shape[0]

    def body(n0, d0, m0, n1, d1, m1, n2, d2, m2, z_ref, ob_ref, o_ref, lse_ref, obt_ref):
        m = jnp.maximum(jnp.maximum(m0[...], m1[...]), m2[...])
        num = jnp.zeros_like(m)
        den = jnp.zeros_like(m)
        for nr, dr, mr in ((n0, d0, m0), (n1, d1, m1), (n2, d2, m2)):
            sc = jnp.exp(mr[...] - m)
            num = num + nr[...] * sc
            den = den + dr[...] * sc
        o = num / den
        o_ref[...] = o
        lse_ref[...] = m + jnp.log(den)
        ob = o * _silu(z_ref[...])
        ob_ref[...] = ob.astype(BF16)
        obt_ref[...] = ob.T.astype(BF16)

    row = pl.BlockSpec((ROW_TILE, DIL_W), lambda i: (i, 0))
    flat = [a for p in parts for a in p]
    return pl.pallas_call(
        body, name="att_merge", grid=(s // ROW_TILE,),
        in_specs=[row] * 9 + [pl.BlockSpec((ROW_TILE, DIL_W), lambda i: (i, OFF_Z_B // DIL_W))],
        out_specs=[row, row, row, pl.BlockSpec((DIL_W, ROW_TILE), lambda i: (0, i))],
        out_shape=[jax.ShapeDtypeStruct((s, DIL_W), BF16), jax.ShapeDtypeStruct((s, DIL_W), F32),
                   jax.ShapeDtypeStruct((s, DIL_W), F32), jax.ShapeDtypeStruct((DIL_W, s), BF16)],
        compiler_params=_cparams("parallel"))(*flat, proj)


def _att_merge_bwd(o, proj, dob):
    s = o.shape[0]

    def body(o_ref, z_ref, d_ref, do_ref, dl_ref, dz_ref):
        ov, zv, dv = o_ref[...], z_ref[...], d_ref[...]
        do = dv * _silu(zv)
        do_ref[...] = do
        dz_ref[...] = (dv * ov * _silu_grad(zv)).astype(BF16)
        for h in range(DIL_HEADS):
            sl = slice(h * DIL_DH, (h + 1) * DIL_DH)
            dl_ref[:, sl] = jnp.broadcast_to(jnp.sum(do[:, sl] * ov[:, sl], axis=-1, keepdims=True), (ROW_TILE, DIL_DH))

    row = pl.BlockSpec((ROW_TILE, DIL_W), lambda i: (i, 0))
    return pl.pallas_call(
        body, name="att_merge_bwd", grid=(s // ROW_TILE,),
        in_specs=[row, pl.BlockSpec((ROW_TILE, DIL_W), lambda i: (i, OFF_Z_B // DIL_W)), row],
        out_specs=[row, row, row],
        out_shape=[jax.ShapeDtypeStruct((s, DIL_W), F32), jax.ShapeDtypeStruct((s, DIL_W), F32),
                   jax.ShapeDtypeStruct((s, DIL_W), BF16)],
        compiler_params=_cparams("parallel"))(o, proj, dob)


def _merge(proj, ya, yb):
    s = proj.shape[0]

    def body(ga_ref, gb_ref, ya_ref, yb_ref, o_ref, ot_ref):
        m = _sigmoid(ga_ref[...]) * ya_ref[...] + _sigmoid(gb_ref[...]) * yb_ref[...]
        o_ref[...] = m.astype(BF16)
        ot_ref[...] = m.T.astype(BF16)

    row = pl.BlockSpec((ROW_TILE, D_MODEL), lambda i: (i, 0))
    return pl.pallas_call(
        body, name="merge", grid=(s // ROW_TILE,),
        in_specs=[pl.BlockSpec((ROW_TILE, D_MODEL), lambda i: (i, OFF_G_A // D_MODEL)),
                  pl.BlockSpec((ROW_TILE, D_MODEL), lambda i: (i, OFF_G_B // D_MODEL)), row, row],
        out_specs=[row, pl.BlockSpec((D_MODEL, ROW_TILE), lambda i: (0, i))],
        out_shape=[jax.ShapeDtypeStruct((s, D_MODEL), BF16), jax.ShapeDtypeStruct((D_MODEL, s), BF16)],
        compiler_params=_cparams("parallel"))(proj, proj, ya, yb)


def _merge_bwd(proj, ya, yb, dm):
    s = proj.shape[0]

    def body(ga_ref, gb_ref, ya_ref, yb_ref, dm_ref, dya_ref, dyb_ref, dga_ref, dgb_ref):
        dmv = dm_ref[...]
        sa, sb = _sigmoid(ga_ref[...]), _sigmoid(gb_ref[...])
        dya_ref[...] = (dmv * sa).astype(BF16)
        dyb_ref[...] = (dmv * sb).astype(BF16)
        dga_ref[...] = (dmv * ya_ref[...] * sa * (1.0 - sa)).astype(BF16)
        dgb_ref[...] = (dmv * yb_ref[...] * sb * (1.0 - sb)).astype(BF16)

    row = pl.BlockSpec((ROW_TILE, D_MODEL), lambda i: (i, 0))
    return pl.pallas_call(
        body, name="merge_bwd", grid=(s // ROW_TILE,),
        in_specs=[pl.BlockSpec((ROW_TILE, D_MODEL), lambda i: (i, OFF_G_A // D_MODEL)),
                  pl.BlockSpec((ROW_TILE, D_MODEL), lambda i: (i, OFF_G_B // D_MODEL)), row, row, row],
        out_specs=[row] * 4, out_shape=[jax.ShapeDtypeStruct((s, D_MODEL), BF16)] * 4,
        compiler_params=_cparams("parallel"))(proj, proj, ya, yb, dm)


def _final(x, t, fw, tgt):
    s, d = x.shape

    def body(x_ref, t_ref, w_ref, y_ref, dx_ref, dw_ref, l_ref):
        i = pl.program_id(0)
        x2 = x_ref[...] + t_ref[...]
        wv = w_ref[...]
        r = lax.rsqrt(jnp.mean(x2 * x2, axis=-1, keepdims=True) + NORM_EPS)
        e = x2 * r * wv - y_ref[...]
        lrow = jnp.mean(e * e, axis=-1, keepdims=True)
        lpart = jnp.broadcast_to(0.5 * jnp.sum(lrow, axis=0, keepdims=True), (1, 128))
        dy = e * (1.0 / d)
        dwp = jnp.sum(dy * x2 * r, axis=0, keepdims=True)
        dyw = dy * wv
        dx_ref[...] = r * dyw - x2 * (r * r * r) * jnp.mean(dyw * x2, axis=-1, keepdims=True)

        @pl.when(i == 0)
        def _():
            dw_ref[...] = dwp
            l_ref[...] = lpart

        @pl.when(i > 0)
        def _():
            dw_ref[...] += dwp
            l_ref[...] += lpart

    row = pl.BlockSpec((ROW_TILE, d), lambda i: (i, 0))
    vec = pl.BlockSpec((1, d), lambda i: (0, 0))
    return pl.pallas_call(
        body, name="final", grid=(s // ROW_TILE,), in_specs=[row, row, vec, row],
        out_specs=[row, vec, pl.BlockSpec((1, 128), lambda i: (0, 0))],
        out_shape=[jax.ShapeDtypeStruct((s, d), F32), jax.ShapeDtypeStruct((1, d), F32), jax.ShapeDtypeStruct((1, 128), F32)],
        compiler_params=_cparams("arbitrary"))(x, t, fw, tgt)


def _adamw(w, g, m, v, name):
    r, c = w.shape
    cap = max(8, (1 << 18) // c)
    divisors = [t for t in range(8, min(r, cap) + 1, 8) if r % t == 0]
    tr = r if r <= 8 else (max(divisors) if divisors else cap)

    def body(w_ref, g_ref, m_ref, v_ref, d_ref, nm_ref, nv_ref):
        gv = g_ref[...]
        mn = ADAM_B1 * m_ref[...] + (1.0 - ADAM_B1) * gv
        vn = ADAM_B2 * v_ref[...] + (1.0 - ADAM_B2) * (gv * gv)
        m_hat = mn / (1.0 - ADAM_B1 ** ADAM_STEP)
        v_hat = vn / (1.0 - ADAM_B2 ** ADAM_STEP)
        d_ref[...] = -ADAM_LR * (m_hat / (jnp.sqrt(v_hat) + ADAM_EPS) + ADAM_WD * w_ref[...])
        nm_ref[...] = mn
        nv_ref[...] = vn

    blk = pl.BlockSpec((tr, c), lambda i: (i, 0))
    return pl.pallas_call(
        body, name=name, grid=(pl.cdiv(r, tr),), in_specs=[blk] * 4, out_specs=[blk] * 3,
        out_shape=[jax.ShapeDtypeStruct((r, c), F32)] * 3, compiler_params=_cparams("parallel"))(w, g, m, v)


HBM_SPEC = pl.BlockSpec(memory_space=pl.ANY)


def _place():
    x, y, c = lax.axis_index("x"), lax.axis_index("y"), lax.axis_index("c")
    chips = [(1 - x, y), (x, 1 - y), (1 - x, 1 - y)]
    return x, y, c, chips


def _ag_weights(packs):
    na = len(packs)

    def body(*refs):
        p_refs, out_refs = refs[:na], refs[na:2 * na]
        send_sems, recv_sems = refs[2 * na:]
        x, y, c, chips = _place()
        me, sib, j = (x, y, c), (x, y, 1 - c), 2 * x + y

        def rc(k, src, dst, to):
            return pltpu.make_async_remote_copy(src_ref=src, dst_ref=dst, send_sem=send_sems.at[k],
                                                recv_sem=recv_sems.at[k], device_id=to, device_id_type=MESH)

        first = [rc(6 * a + k, p_refs[a].at[c], out_refs[a].at[j, c], (cx, cy, c))
                 for a in range(na) for k, (cx, cy) in enumerate(chips)]
        for cp in first:
            cp.start()
        passed = []
        for a in range(na):
            for k, (cx, cy) in enumerate(chips):
                land = out_refs[a].at[2 * cx + cy, c]
                rc(6 * a + k, p_refs[a].at[c], land, me).wait_recv()
                fwd = rc(6 * a + 3 + k, land, land, sib)
                fwd.start()
                passed.append(fwd)
        for a in range(na):
            for k, (cx, cy) in enumerate(chips):
                rc(6 * a + 3 + k, p_refs[a].at[c], out_refs[a].at[2 * cx + cy, 1 - c], me).wait_recv()
        for cp in first + passed:
            cp.wait_send()

    return pl.pallas_call(
        body, name="ag_weights",
        out_shape=[jax.ShapeDtypeStruct((N_CHIPS,) + p.shape, p.dtype) for p in packs],
        in_specs=[HBM_SPEC] * na, out_specs=[HBM_SPEC] * na,
        scratch_shapes=[pltpu.SemaphoreType.DMA((6 * na,)), pltpu.SemaphoreType.DMA((6 * na,))])(*packs)


def _rs_pair(dwpt, gpack):
    n = N_CHIPS
    hw = SHARD_PAD // 2

    def body(d_ref, g_ref, out_d, out_g, send_sems, recv_sems):
        x, y, c, _ = _place()
        sib = (x, y, 1 - c)
        cps = []
        for p in range(n):
            start = pl.multiple_of(WIN_BASE[p] + (1 - c) * hw, TILE_ROWS)
            cps.append(pltpu.make_async_remote_copy(
                src_ref=d_ref.at[pl.ds(start, hw)], dst_ref=out_d.at[p], send_sem=send_sems.at[p],
                recv_sem=recv_sems.at[p], device_id=sib, device_id_type=MESH))
            cps.append(pltpu.make_async_remote_copy(
                src_ref=g_ref.at[p, 1 - c], dst_ref=out_g.at[p], send_sem=send_sems.at[n + p],
                recv_sem=recv_sems.at[n + p], device_id=sib, device_id_type=MESH))
        for cp in cps:
            cp.start()
        for cp in cps:
            cp.wait_recv()
        for cp in cps:
            cp.wait_send()

    return pl.pallas_call(
        body, name="rs_pair",
        out_shape=[jax.ShapeDtypeStruct((n, hw, dwpt.shape[1]), dwpt.dtype),
                   jax.ShapeDtypeStruct((n,) + gpack.shape[2:], gpack.dtype)],
        in_specs=[HBM_SPEC] * 2, out_specs=[HBM_SPEC] * 2,
        scratch_shapes=[pltpu.SemaphoreType.DMA((2 * n,)), pltpu.SemaphoreType.DMA((2 * n,))])(dwpt, gpack)


def _add_halves_win(dwpt, other, c):
    n, rh, wd = other.shape
    tr = _row_tile(rh)

    def body(s_ref, d_ref, o_ref, out_ref):
        out_ref[0] = (d_ref[...] + o_ref[0]).astype(BF16)

    scal = jnp.concatenate([jnp.reshape(c, (1,)).astype(jnp.int32), jnp.asarray(WIN_BASE, jnp.int32)])
    grid_spec = pltpu.PrefetchScalarGridSpec(
        num_scalar_prefetch=1, grid=(n, rh // tr),
        in_specs=[pl.BlockSpec((pl.Element(tr), pl.Element(wd)),
                               lambda p, i, sr: (pl.multiple_of(sr[1 + p] + sr[0] * rh + i * tr, TILE_ROWS), 0)),
                  pl.BlockSpec((1, tr, wd), lambda p, i, sr: (p, i, 0))],
        out_specs=pl.BlockSpec((1, tr, wd), lambda p, i, sr: (p, i, 0)))
    return pl.pallas_call(
        body, name="add_halves_in", grid_spec=grid_spec, out_shape=jax.ShapeDtypeStruct((n, rh, wd), BF16),
        compiler_params=_cparams("parallel", "parallel"))(scal, dwpt, other)


def _rs_chips(csums):
    na = len(csums)

    def body(*refs):
        s_refs, out_refs = refs[:na], refs[na:2 * na]
        send_sems, recv_sems = refs[2 * na:]
        x, y, c, chips = _place()
        j = 2 * x + y
        cps = [pltpu.make_async_remote_copy(src_ref=s_refs[a].at[2 * cx + cy], dst_ref=out_refs[a].at[j],
                                            send_sem=send_sems.at[3 * a + k], recv_sem=recv_sems.at[3 * a + k],
                                            device_id=(cx, cy, c), device_id_type=MESH)
               for a in range(na) for k, (cx, cy) in enumerate(chips)]
        for cp in cps:
            cp.start()
        for a in range(na):
            for k, (cx, cy) in enumerate(chips):
                pltpu.make_async_remote_copy(src_ref=s_refs[a].at[j], dst_ref=out_refs[a].at[2 * cx + cy],
                                             send_sem=send_sems.at[3 * a + k], recv_sem=recv_sems.at[3 * a + k],
                                             device_id=(x, y, c), device_id_type=MESH).wait_recv()
        for cp in cps:
            cp.wait_send()

    return pl.pallas_call(
        body, name="rs_chips", out_shape=[jax.ShapeDtypeStruct(s.shape, s.dtype) for s in csums],
        in_specs=[HBM_SPEC] * na, out_specs=[HBM_SPEC] * na,
        scratch_shapes=[pltpu.SemaphoreType.DMA((3 * na,)), pltpu.SemaphoreType.DMA((3 * na,))])(*csums)


SEM_SPEC = pl.BlockSpec(memory_space=pltpu.SEMAPHORE)
DATAFLOW_EFFECT = pltpu.SideEffectType.DATAFLOW_SIDE_EFFECTING


def _rs_chips_start(csums):
    na = len(csums)

    def body(*refs):
        s_refs, land_refs = refs[:na], refs[na:2 * na]
        send_sems, recv_sems = refs[2 * na], refs[2 * na + 1]
        token = refs[-1]
        x, y, c, chips = _place()
        j = 2 * x + y
        for a in range(na):
            for k, (cx, cy) in enumerate(chips):
                pltpu.make_async_remote_copy(src_ref=s_refs[a].at[2 * cx + cy], dst_ref=land_refs[a].at[j],
                                             send_sem=send_sems.at[3 * a + k], recv_sem=recv_sems.at[3 * a + k],
                                             device_id=(cx, cy, c), device_id_type=MESH).start()
        token[...] = jnp.zeros_like(token)

    hbm = [pltpu.HBM(s.shape, s.dtype) for s in csums]
    args = [pltpu.with_memory_space_constraint(s, pltpu.HBM) for s in csums]
    args += [pltpu.with_memory_space_constraint(lax.empty(s.shape, s.dtype), pltpu.HBM) for s in csums]
    res = pl.pallas_call(
        body, name="rs_chips_start",
        out_shape=(pltpu.SemaphoreType.DMA((3 * na,)), pltpu.SemaphoreType.DMA((3 * na,)), *hbm, *hbm,
                   jax.ShapeDtypeStruct((8, 128), F32)),
        in_specs=[pl.BlockSpec(memory_space=pltpu.HBM)] * (2 * na),
        out_specs=(SEM_SPEC, SEM_SPEC, *[pl.BlockSpec(memory_space=pltpu.HBM)] * (2 * na),
                   pl.BlockSpec(memory_space=pltpu.VMEM)),
        input_output_aliases={i: 2 + i for i in range(2 * na)},
        compiler_params=pltpu.CompilerParams(has_side_effects=DATAFLOW_EFFECT))(*args)
    return res[0], res[1], list(res[2:2 + na]), list(res[2 + na:2 + 2 * na]), res[-1]


def _rs_chips_wait(send_sems, recv_sems, csums, lands, after):
    na = len(csums)

    def body(*refs):
        s_refs, land_refs = refs[:na], refs[na:2 * na]
        send_sems, recv_sems = refs[2 * na], refs[2 * na + 1]
        x, y, c, chips = _place()
        j = 2 * x + y
        for a in range(na):
            for k, (cx, cy) in enumerate(chips):
                cp = pltpu.make_async_remote_copy(src_ref=s_refs[a].at[2 * cx + cy], dst_ref=land_refs[a].at[2 * cx + cy],
                                                  send_sem=send_sems.at[3 * a + k], recv_sem=recv_sems.at[3 * a + k],
                                                  device_id=(cx, cy, c), device_id_type=MESH)
                cp.wait_send()
                cp.wait_recv()

    hbm = [pltpu.HBM(s.shape, s.dtype) for s in csums]
    res = pl.pallas_call(
        body, name="rs_chips_wait", out_shape=(*hbm, *hbm),
        in_specs=[pl.BlockSpec(memory_space=pltpu.HBM)] * (2 * na) + [SEM_SPEC, SEM_SPEC, pl.BlockSpec(memory_space=pl.ANY)],
        out_specs=tuple([pl.BlockSpec(memory_space=pltpu.HBM)] * (2 * na)),
        input_output_aliases={i: i for i in range(2 * na)},
        compiler_params=pltpu.CompilerParams(has_side_effects=DATAFLOW_EFFECT))(*csums, *lands, send_sems, recv_sems, after)
    return list(res[:na]), list(res[na:])


SWAP_CHUNKS = 4


def _pair_swap(halves):
    na = len(halves)

    def body(*refs):
        h_refs, out_refs = refs[:na], refs[na:2 * na]
        send_sems, recv_sems = refs[2 * na:]
        x, y, c, _ = _place()
        cps = []
        for a in range(na):
            rows = h_refs[a].shape[0] // SWAP_CHUNKS
            assert rows * SWAP_CHUNKS == h_refs[a].shape[0]
            for q in range(SWAP_CHUNKS):
                k = SWAP_CHUNKS * a + q
                cps.append(pltpu.make_async_remote_copy(
                    src_ref=h_refs[a].at[pl.ds(q * rows, rows)], dst_ref=out_refs[a].at[pl.ds(q * rows, rows)],
                    send_sem=send_sems.at[k], recv_sem=recv_sems.at[k], device_id=(x, y, 1 - c), device_id_type=MESH))
        for cp in cps:
            cp.start()
        for cp in cps:
            cp.wait_recv()
        for cp in cps:
            cp.wait_send()

    return pl.pallas_call(
        body, name="pair_swap", out_shape=[jax.ShapeDtypeStruct(h.shape, h.dtype) for h in halves],
        in_specs=[HBM_SPEC] * na, out_specs=[HBM_SPEC] * na,
        scratch_shapes=[pltpu.SemaphoreType.DMA((SWAP_CHUNKS * na,)), pltpu.SemaphoreType.DMA((SWAP_CHUNKS * na,))])(*halves)


def _ag_small(v):
    m_per, n = v.shape

    def body(x_ref, out_ref, send_sems, recv_sems, local_sem):
        x, y, c, chips = _place()
        me, sibling = (x, y, c), (x, y, 1 - c)

        def rows(px, py, pc):
            return out_ref.at[pl.ds((4 * px + 2 * py + pc) * m_per, m_per), :]

        def copy(k, block, to, src=None):
            return pltpu.make_async_remote_copy(
                src_ref=rows(*block) if src is None else src, dst_ref=rows(*block), send_sem=send_sems.at[k],
                recv_sem=recv_sems.at[k], device_id=to, device_id_type=MESH)

        mine = pltpu.make_async_copy(x_ref, rows(*me), local_sem)
        mine.start()
        first = [copy(0, me, sibling, src=x_ref)]
        first += [copy(1 + k, me, (*chip, c), src=x_ref) for k, chip in enumerate(chips)]
        for cp in first:
            cp.start()
        passed = [copy(4 + k, (*chip, c), sibling) for k, chip in enumerate(chips)]
        for k, chip in enumerate(chips):
            copy(1 + k, (*chip, c), me).wait_recv()
            passed[k].start()
        copy(0, sibling, me).wait_recv()
        for k, chip in enumerate(chips):
            copy(4 + k, (*chip, 1 - c), me).wait_recv()
        for cp in first + passed:
            cp.wait_send()
        mine.wait()

    return pl.pallas_call(
        body, name="ag_small", out_shape=jax.ShapeDtypeStruct((8 * m_per, n), v.dtype),
        in_specs=[pl.BlockSpec(memory_space=pltpu.VMEM)], out_specs=pl.BlockSpec(memory_space=pltpu.VMEM),
        scratch_shapes=[pltpu.SemaphoreType.DMA((7,)), pltpu.SemaphoreType.DMA((7,)), pltpu.SemaphoreType.DMA])(v)


def _sum_blocks(a, nblk, name):
    rows, wd = a.shape
    r = rows // nblk
    tr = min(r, ROW_TILE)
    assert r % tr == 0

    def body(*refs):
        acc = refs[0][...].astype(F32)
        for ref in refs[1:nblk]:
            acc = acc + ref[...].astype(F32)
        refs[nblk][...] = acc

    nt = r // tr
    return pl.pallas_call(
        body, name=name, grid=(nt,),
        in_specs=[pl.BlockSpec((tr, wd), functools.partial(lambda i, b: (b * nt + i, 0), b=b)) for b in range(nblk)],
        out_specs=pl.BlockSpec((tr, wd), lambda i: (i, 0)),
        out_shape=jax.ShapeDtypeStruct((r, wd), F32), compiler_params=_cparams("parallel"))(*([a] * nblk))


def _row_tile(rows):
    best = max(t for t in range(16, 513, 16) if rows % t == 0)
    return best


def _sum_chips(by_src, csum, j, name):
    n, rh, wd = by_src.shape
    tr = _row_tile(rh)

    def body(j_ref, *refs):
        own = refs[n][0].astype(F32)
        acc = None
        for k in range(n):
            term = jnp.where(j_ref[0] == k, own, refs[k][0].astype(F32))
            acc = term if acc is None else acc + term
        refs[n + 1][...] = acc

    def other(k):
        return pl.BlockSpec((1, tr, wd), lambda i, jr: (jnp.where(jr[0] == k, (k + 1) % n, k), i, 0))

    grid_spec = pltpu.PrefetchScalarGridSpec(
        num_scalar_prefetch=1, grid=(rh // tr,),
        in_specs=[other(k) for k in range(n)] + [pl.BlockSpec((1, tr, wd), lambda i, jr: (jr[0], i, 0))],
        out_specs=pl.BlockSpec((tr, wd), lambda i, jr: (i, 0)))
    return pl.pallas_call(
        body, name=name, grid_spec=grid_spec, out_shape=jax.ShapeDtypeStruct((rh, wd), F32),
        compiler_params=_cparams("parallel"))(jnp.reshape(j, (1,)).astype(jnp.int32), *([by_src] * n), csum)


def _add_halves(gpack, other, c, name):
    n, _, rh, wd = gpack.shape
    tr = _row_tile(rh)

    def body(c_ref, g_ref, o_ref, out_ref):
        out_ref[0] = (g_ref[0, 0] + o_ref[0]).astype(BF16)

    grid_spec = pltpu.PrefetchScalarGridSpec(
        num_scalar_prefetch=1, grid=(n, rh // tr),
        in_specs=[pl.BlockSpec((1, 1, tr, wd), lambda p, i, cr: (p, cr[0], i, 0)),
                  pl.BlockSpec((1, tr, wd), lambda p, i, cr: (p, i, 0))],
        out_specs=pl.BlockSpec((1, tr, wd), lambda p, i, cr: (p, i, 0)))
    return pl.pallas_call(
        body, name=name, grid_spec=grid_spec, out_shape=jax.ShapeDtypeStruct((n, rh, wd), BF16),
        compiler_params=_cparams("parallel", "parallel"))(jnp.reshape(c, (1,)).astype(jnp.int32), gpack, other)


PACK_W = 1024
ROWS_O_DN = DN_W // N_CHIPS
ROWS_O_DIL = DIL_W * (D_MODEL // N_CHIPS) // PACK_W
ROWS_OUT = D_MODEL // N_CHIPS
ROWS_CONV = 4 * (3 * DN_W // N_CHIPS) // PACK_W
R1 = ROWS_O_DN
R2 = R1 + ROWS_O_DIL
R3 = R2 + ROWS_OUT
R4 = R3 + 16
R5 = R4 + 16
PACK_ROWS = 704
HALF_ROWS = PACK_ROWS // 2
SHARD_PAD = 2880


R6 = R5 + 2 * DN_HEADS

TILE_ROWS = 16
BA_IN_SHARD1 = REF_OFF_BA - SHARD_W
LOCAL_START = (0, SHARD_W, 2 * SHARD_W - 2 * DN_HEADS, 3 * SHARD_W - 2 * DN_HEADS)
LOCAL_END = LOCAL_START[1:] + (OFF_BA,)
WIN_BASE = tuple(s // TILE_ROWS * TILE_ROWS for s in LOCAL_START)


def _to_window(k, shard):
    nba = 2 * DN_HEADS
    body = shard
    if k == 1:
        row = lax.broadcasted_iota(jnp.int32, (SHARD_W - nba, 1), 0)
        body = jnp.where(row < BA_IN_SHARD1, shard[:SHARD_W - nba], shard[nba:])
    lead = LOCAL_START[k] - WIN_BASE[k]
    return jnp.pad(body, ((lead, SHARD_PAD - lead - body.shape[0]), (0, 0)))


def _from_window(k, win, ba):
    nba = 2 * DN_HEADS
    lead = LOCAL_START[k] - WIN_BASE[k]
    if k != 1:
        return win[lead:lead + SHARD_W]
    row = lax.broadcasted_iota(jnp.int32, (SHARD_W, 1), 0)
    before = win[lead:lead + SHARD_W]
    after = jnp.pad(win, ((nba, 0), (0, 0)))[lead:lead + SHARD_W]
    mid = jnp.pad(ba, ((BA_IN_SHARD1, SHARD_W - BA_IN_SHARD1 - nba), (0, 0)))
    return jnp.where(row < BA_IN_SHARD1, before, jnp.where(row < BA_IN_SHARD1 + nba, mid, after))


def _stack_windows(wins, ba):
    pieces = []
    for k in range(N_CHIPS):
        lo = WIN_BASE[k] + (TILE_ROWS if k else 0)
        hi = LOCAL_END[k] // TILE_ROWS * TILE_ROWS
        pieces.append(wins[k][lo - WIN_BASE[k]:hi - WIN_BASE[k]])
        if k + 1 < N_CHIPS:
            assert hi == WIN_BASE[k + 1]
            pieces.append(wins[k][hi - WIN_BASE[k]:hi - WIN_BASE[k] + TILE_ROWS] + wins[k + 1][:TILE_ROWS])
    pieces += [ba, jnp.zeros((PW - OFF_BA - ba.shape[0], ba.shape[1]), ba.dtype)]
    out = jnp.concatenate(pieces, axis=0)
    assert out.shape[0] == PW
    return out


def _to_ref_layout(wpt):
    return jnp.concatenate([wpt[:REF_OFF_BA], wpt[OFF_BA:OFF_BA + 2 * DN_HEADS], wpt[REF_OFF_BA:OFF_BA]], axis=0)


def _from_ref_layout(wt):
    pad = jnp.zeros((PW - PROJ_W, wt.shape[1]), wt.dtype)
    return jnp.concatenate([wt[:REF_OFF_BA], wt[REF_OFF_BA + 2 * DN_HEADS:], wt[REF_OFF_BA:REF_OFF_BA + 2 * DN_HEADS], pad],
                           axis=0)


def _local_step(x, tgt, norm_w, wpt, conv_full, a_log, dt_bias, dn_norm_w, w_o_dn, w_o_dil, w_out, final_norm_w):
    s = x.shape[0]
    h, h_t = _rms_in(x, norm_w)
    proj = _matmul(h, wpt, F32, 2048, 1280, 1024, "proj", nt=True)
    c_pre, qkv = _conv_fwd(proj, conv_full)
    gate_par = jnp.zeros((8, 128), F32).at[0, 8:16].set(a_log[0]).at[1, 8:16].set(dt_bias[0])
    bg = _gates_fwd(proj, gate_par)
    o_a, u, w, vn, tmat, states = _gdr_fwd(qkv, bg)
    oa2, oa2_t = _gdr_out(o_a, proj, dn_norm_w)
    ya = _matmul(oa2, w_o_dn, F32, 512, 1024, 1024, "ya")
    parts = [_att_fwd(proj, g) for g in range(N_DIL)]
    ob, o_att, lse, ob_t = _att_merge(parts, proj)
    yb = _matmul(ob, w_o_dil, F32, 512, 1024, 512, "yb")
    mg, mg_t = _merge(proj, ya, yb)
    t = _matmul(mg, w_out, F32, 512, 1024, 1024, "t_out")
    dx2, dfw, lpart = _final(x, t, final_norm_w, tgt)

    dmg = _matmul(dx2, w_out, F32, 512, 1024, 1024, "d_merged", nt=True)
    dw_out = _matmul(mg_t, dx2, F32, 1024, 1024, 1024, "dw_out")
    dya, dyb, dga, dgb = _merge_bwd(proj, ya, yb, dmg)
    doa2 = _matmul(dya, w_o_dn, F32, 512, 1024, 1024, "d_oa2", nt=True)
    dw_o_dn = _matmul(oa2_t, dya, F32, 1024, 1024, 1024, "dw_o_dn")
    dob = _matmul(dyb, w_o_dil, F32, 512, 512, 1024, "d_ob", nt=True)
    dw_o_dil = _matmul(ob_t, dyb, F32, 512, 1024, 1024, "dw_o_dil")
    do_a, dz_a, ddnw = _gdr_out_bwd(o_a, proj, dn_norm_w, doa2)
    dq_a, dk_a, dv_a, dbg = _gdr_bwd(qkv, bg, u, w, vn, tmat, states, do_a)
    dba, dpar = _gates_bwd(proj, gate_par, dbg)
    dc = _conv_bwd_act(c_pre, dq_a, dk_a, dv_a)
    du_a, dconv = _conv_bwd(proj, dc, conv_full)
    do_att, delta, dz_b = _att_merge_bwd(o_att, proj, dob)
    dqkv_b = [_att_bwd(proj, g, do_att, lse, delta) for g in range(N_DIL)]
    dproj = jnp.concatenate(
        [du_a, dz_a] + [dqkv_b[g][i] for i in range(3) for g in range(N_DIL)]
        + [dz_b, dga, dgb, dba, jnp.zeros((s, PW - OFF_BA - 128), BF16)], axis=1)
    dwpt = _matmul(h_t, dproj, F32, 1024, 1280, 1024, "dw_in", transpose_out=True)

    def finish(after=None):
        dh = _matmul(dproj, wpt, F32, 1024, 1024, 2304, "d_h", after=after)
        grad_x, dnw = _rms_in_bwd(x, norm_w, dh, dx2)
        small = jnp.zeros((8, PACK_W), F32)
        small = small.at[0].set(dnw[0]).at[1].set(dfw[0]).at[2, :DN_D].set(ddnw[0])
        small = small.at[3, :DN_HEADS].set(dpar[0, 8:16]).at[3, DN_HEADS:2 * DN_HEADS].set(dpar[1, 8:16])
        small = small.at[4, 0].set(lpart[0, 0])
        return grad_x, small

    return finish, dwpt, dconv, dw_o_dn, dw_o_dil, dw_out


def kernel(x, norm_w, w_in, conv_w, a_log, dt_bias, dn_norm_w, w_o_dn, w_o_dil, w_out, final_norm_w, loss_target, m_norm_w, m_w_in, m_conv_w, m_a_log, m_dt_bias, m_dn_norm_w, m_w_o_dn, m_w_o_dil, m_w_out, m_final_norm_w, v_norm_w, v_w_in, v_conv_w, v_a_log, v_dt_bias, v_dn_norm_w, v_w_o_dn, v_w_o_dil, v_w_out, v_final_norm_w):
    c = lax.axis_index("c")
    j = 2 * lax.axis_index("x") + lax.axis_index("y")
    qw = D_MODEL // N_CHIPS

    cw = conv_w[0].reshape(ROWS_CONV, PACK_W)
    cw = jnp.pad(cw, ((0, 16 - ROWS_CONV), (0, 0)))
    cw_hi = cw.astype(BF16)
    cw_lo = (cw - cw_hi.astype(F32)).astype(BF16)
    shard = w_in[0].T.astype(BF16)
    own_ba = jnp.where(j == 1, shard[BA_IN_SHARD1:BA_IN_SHARD1 + 2 * DN_HEADS], jnp.zeros((2 * DN_HEADS, D_MODEL), BF16))
    pack = jnp.concatenate(
        [w_o_dn[0].astype(BF16), w_o_dil[0].astype(BF16).reshape(ROWS_O_DIL, PACK_W), w_out[0].astype(BF16), cw_hi, cw_lo,
         own_ba, jnp.zeros((PACK_ROWS - R6, PACK_W), BF16)], axis=0).reshape(2, HALF_ROWS, PACK_W)
    chips = range(N_CHIPS)
    own_win = lax.switch(j, [functools.partial(_to_window, k) for k in chips], shard).reshape(2, SHARD_PAD // 2, D_MODEL)
    all_in, allw = _ag_weights([own_win, pack])
    wins = [jnp.where(j == k, own_win, all_in[k]).reshape(SHARD_PAD, D_MODEL) for k in chips]
    allw = [jnp.where(j == k, pack, allw[k]).reshape(PACK_ROWS, PACK_W) for k in chips]
    wpt = _stack_windows(wins, allw[1][R5:R6])
    w_o_dn_full = jnp.concatenate([allw[k][:R1] for k in chips], axis=0)
    w_o_dil_full = jnp.concatenate([allw[k][R1:R2].reshape(DIL_W, qw) for k in chips], axis=1)
    w_out_full = jnp.concatenate([allw[k][R2:R3] for k in chips], axis=0)
    conv_full = jnp.concatenate(
        [(allw[k][R3:R3 + ROWS_CONV].astype(F32) + allw[k][R4:R4 + ROWS_CONV].astype(F32)).reshape(4, 3 * DN_W // N_CHIPS)
         for k in chips], axis=1)

    finish, dwpt, dconv, dw_o_dn, dw_o_dil, dw_out = _local_step(
        x[0], loss_target[0], norm_w, wpt, conv_full, a_log, dt_bias, dn_norm_w, w_o_dn_full, w_o_dil_full, w_out_full,
        final_norm_w.reshape(1, D_MODEL))

    cq = 3 * DN_W // N_CHIPS
    gpack = jnp.stack([
        jnp.concatenate(
            [dw_o_dn[k * qw:(k + 1) * qw], dw_o_dil[:, k * qw:(k + 1) * qw].reshape(ROWS_O_DIL, PACK_W),
             dw_out[k * qw:(k + 1) * qw],
             jnp.pad(dconv[:, k * cq:(k + 1) * cq].reshape(ROWS_CONV, PACK_W), ((0, 16 - ROWS_CONV), (0, 0))),
             dwpt[OFF_BA:OFF_BA + 2 * DN_HEADS] if k == 1 else jnp.zeros((2 * DN_HEADS, PACK_W), F32),
             jnp.zeros((PACK_ROWS - R4 - 2 * DN_HEADS, PACK_W), F32)], axis=0)
        for k in chips]).reshape(N_CHIPS, 2, HALF_ROWS, PACK_W)
    sib_in, sib_pack = _rs_pair(dwpt, gpack)
    csum_in = _add_halves_win(dwpt, sib_in, c)
    csum_pack = _add_halves(gpack, sib_pack, c, "add_halves_pack")
    send_sems, recv_sems, csums, lands, token = _rs_chips_start([csum_in, csum_pack])
    grad_x, small = finish(after=token)
    (csum_in, csum_pack), (src_in, src_pack) = _rs_chips_wait(send_sems, recv_sems, csums, lands, grad_x)
    half_in = _sum_chips(src_in, csum_in, j, "sum_chips_in")
    half_pack = _sum_chips(src_pack, csum_pack, j, "sum_chips_pack")
    sib_half_in, sib_half_pack = _pair_swap([half_in, half_pack])

    def both_halves(mine, theirs):
        return jnp.where(c == 0, jnp.concatenate([mine, theirs], axis=0), jnp.concatenate([theirs, mine], axis=0))

    g = both_halves(half_pack, sib_half_pack)
    g_w_in = lax.switch(j, [functools.partial(_from_window, k) for k in chips], both_halves(half_in, sib_half_in),
                        g[R4:R4 + 2 * DN_HEADS])
    g_w_o_dn = g[:R1]
    g_w_o_dil = g[R1:R2].reshape(DIL_W, qw)
    g_w_out = g[R2:R3]
    g_conv = g[R3:R3 + ROWS_CONV].reshape(4, cq)

    gs = _sum_blocks(_ag_small(small), 8, "sum_small")
    loss = gs[4, 0]
    w_small = jnp.zeros((8, PACK_W), F32)

    def pack_small(nw, fw, dnw_, al, db):
        t = w_small.at[0].set(nw[0]).at[1].set(fw).at[2, :DN_D].set(dnw_[0])
        return t.at[3, :DN_HEADS].set(al[0]).at[3, DN_HEADS:2 * DN_HEADS].set(db[0])

    sm = _adamw(pack_small(norm_w, final_norm_w, dn_norm_w, a_log, dt_bias), gs,
                pack_small(m_norm_w, m_final_norm_w, m_dn_norm_w, m_a_log, m_dt_bias),
                pack_small(v_norm_w, v_final_norm_w, v_dn_norm_w, v_a_log, v_dt_bias), "adamw_small")

    def unpack_small(t):
        return dict(norm_w=t[0:1], final_norm_w=t[1], dn_norm_w=t[2:3, :DN_D], a_log=t[3:4, :DN_HEADS],
                    dt_bias=t[3:4, DN_HEADS:2 * DN_HEADS])

    res = {"grad": unpack_small(gs)}
    for kind, arr in zip(("delta", "new_m", "new_v"), sm):
        res[kind] = unpack_small(arr)
    big = dict(conv_w=(conv_w, g_conv, m_conv_w, v_conv_w), w_o_dn=(w_o_dn, g_w_o_dn, m_w_o_dn, v_w_o_dn),
               w_o_dil=(w_o_dil, g_w_o_dil, m_w_o_dil, v_w_o_dil), w_out=(w_out, g_w_out, m_w_out, v_w_out))
    for name, (wt, gt, mt, vt) in big.items():
        d, nm, nv = _adamw(wt[0], gt, mt[0], vt[0], "adamw_" + name)
        res["grad"][name] = gt[None]
        res["delta"][name], res["new_m"][name], res["new_v"][name] = d[None], nm[None], nv[None]

    d, nm, nv = _adamw(w_in[0].T, g_w_in, m_w_in[0].T, v_w_in[0].T, "adamw_w_in")
    res["grad"]["w_in"] = g_w_in.T[None]
    res["delta"]["w_in"], res["new_m"]["w_in"], res["new_v"]["w_in"] = d.T[None], nm.T[None], nv.T[None]
    order = ["norm_w", "w_in", "conv_w", "a_log", "dt_bias", "dn_norm_w", "w_o_dn", "w_o_dil", "w_out", "final_norm_w"]
    outs = [loss, grad_x[None]]
    for kind in ("grad", "delta", "new_m", "new_v"):
        outs += [res[kind][nm] for nm in order]
    return tuple(outs)
```

```python
import functools
import math

import jax
import jax.numpy as jnp
from jax import lax
from jax.experimental import pallas as pl
from jax.experimental.pallas import tpu as pltpu

F32 = jnp.float32
BF16 = jnp.bfloat16
MESH = pl.DeviceIdType.MESH

D_MODEL = 1024
DN_HEADS = 8
DN_D = 128
DN_CHUNK = 64
DN_W = DN_HEADS * DN_D
DIL_GROUPS = ((128, 1), (512, 4), (2048, 16))
N_DIL = len(DIL_GROUPS)
DIL_HEADS = 4
DIL_DH = 128
DIL_W = DIL_HEADS * DIL_DH
ATT_BLOCK = 128
NORM_EPS = 1e-6
PROJ_W = 11280
N_CHIPS = 4
SHARD_W = PROJ_W // N_CHIPS

OFF_QKV_A = 0
OFF_Z_A = 3072
OFF_Q_B = 4096
OFF_K_B = 5632
OFF_V_B = 7168
OFF_Z_B = 8704
OFF_G_A = 9216
OFF_G_B = 10240
OFF_BA = 11264
PW = 11520
REF_OFF_BA = 4096

ADAM_LR = 0.001
ADAM_B1 = 0.9
ADAM_B2 = 0.999
ADAM_EPS = 1e-08
ADAM_WD = 0.01
ADAM_STEP = 10

ROW_TILE = 256
NEG = -1e30


def _dot(a, b):
    return jnp.dot(a.astype(BF16), b.astype(BF16), preferred_element_type=F32)


def _dot_nt(a, b):
    return lax.dot_general(a.astype(BF16), b.astype(BF16), (((1,), (1,)), ((), ())), preferred_element_type=F32)


def _dot_tn(a, b):
    return lax.dot_general(a.astype(BF16), b.astype(BF16), (((0,), (0,)), ((), ())), preferred_element_type=F32)


def _split(a):
    hi = a.astype(BF16)
    lo = (a - hi.astype(F32)).astype(BF16)
    return hi, lo


def _dot_exact_lhs(c, a):
    hi, lo = _split(a)
    cb = c.astype(BF16)
    return jnp.dot(cb, hi, preferred_element_type=F32) + jnp.dot(cb, lo, preferred_element_type=F32)


def _dot_exact_rhs(a, c):
    hi, lo = _split(a)
    cb = c.astype(BF16)
    return jnp.dot(hi, cb, preferred_element_type=F32) + jnp.dot(lo, cb, preferred_element_type=F32)


def _dot_tn_exact_rhs(a, c):
    hi, lo = _split(a)
    cb = c.astype(BF16)
    dn = (((0,), (0,)), ((), ()))
    return (lax.dot_general(hi, cb, dn, preferred_element_type=F32)
            + lax.dot_general(lo, cb, dn, preferred_element_type=F32))


def _sigmoid(x):
    return 1.0 / (1.0 + jnp.exp(-x))


def _silu(x):
    return x * _sigmoid(x)


def _silu_grad(x):
    s = _sigmoid(x)
    return s * (1.0 + x * (1.0 - s))


def _softplus(x):
    return jnp.maximum(x, 0.0) + jnp.log(1.0 + jnp.exp(-jnp.abs(x)))


def _cparams(*sem):
    return pltpu.CompilerParams(dimension_semantics=sem)


def _matmul(a, b, out_dtype, tm, tn, tk, name, nt=False, transpose_out=False, after=None):
    m, kdim = a.shape
    n = b.shape[0] if nt else b.shape[1]
    tm, tn, tk = min(tm, m), min(tn, n), min(tk, kdim)
    assert m % tm == 0 and n % tn == 0 and kdim % tk == 0, (name, a.shape, b.shape, tm, tn, tk)
    nk = kdim // tk
    dot = _dot_nt if nt else _dot
    b_spec = (pl.BlockSpec((tn, tk), lambda i, j, k: (j, k)) if nt else pl.BlockSpec((tk, tn), lambda i, j, k: (k, j)))

    def emit(o_ref, acc):
        o_ref[...] = (acc.T if transpose_out else acc).astype(o_ref.dtype)

    if nk == 1:
        def body(a_ref, b_ref, *rest):
            emit(rest[-1], dot(a_ref[...], b_ref[...]))
        scratch = []
    else:
        def body(a_ref, b_ref, *rest):
            o_ref, acc_ref = rest[-2:]
            k = pl.program_id(2)
            p = dot(a_ref[...], b_ref[...])

            @pl.when(k == 0)
            def _():
                acc_ref[...] = p

            @pl.when(k > 0)
            def _():
                acc_ref[...] += p

            @pl.when(k == nk - 1)
            def _():
                emit(o_ref, acc_ref[...])
        scratch = [pltpu.VMEM((tm, tn), F32)]

    if transpose_out:
        out_spec, out_shape = pl.BlockSpec((tn, tm), lambda i, j, k: (j, i)), (n, m)
    else:
        out_spec, out_shape = pl.BlockSpec((tm, tn), lambda i, j, k: (i, j)), (m, n)
    extra = [] if after is None else [after]
    return pl.pallas_call(
        body, name=name, grid=(m // tm, n // tn, nk),
        in_specs=[pl.BlockSpec((tm, tk), lambda i, j, k: (i, k)), b_spec] + [pl.BlockSpec(memory_space=pl.ANY)] * len(extra),
        out_specs=out_spec, out_shape=jax.ShapeDtypeStruct(out_shape, out_dtype), scratch_shapes=scratch,
        compiler_params=_cparams("parallel", "parallel", "arbitrary"))(a, b, *extra)


def _rms_in(x, nw):
    s, d = x.shape

    def body(x_ref, w_ref, h_ref, ht_ref):
        xv = x_ref[...]
        r = lax.rsqrt(jnp.mean(xv * xv, axis=-1, keepdims=True) + NORM_EPS)
        h = xv * r * w_ref[...]
        h_ref[...] = h.astype(BF16)
        ht_ref[...] = h.T.astype(BF16)

    return pl.pallas_call(
        body, name="rms_in", grid=(s // ROW_TILE,),
        in_specs=[pl.BlockSpec((ROW_TILE, d), lambda i: (i, 0)), pl.BlockSpec((1, d), lambda i: (0, 0))],
        out_specs=[pl.BlockSpec((ROW_TILE, d), lambda i: (i, 0)), pl.BlockSpec((d, ROW_TILE), lambda i: (0, i))],
        out_shape=[jax.ShapeDtypeStruct((s, d), BF16), jax.ShapeDtypeStruct((d, s), BF16)],
        compiler_params=_cparams("parallel"))(x, nw)


def _rms_in_bwd(x, nw, dh, dx2):
    s, d = x.shape

    def body(x_ref, w_ref, dh_ref, dx2_ref, dx_ref, dw_ref):
        i = pl.program_id(0)
        xv = x_ref[...]
        r = lax.rsqrt(jnp.mean(xv * xv, axis=-1, keepdims=True) + NORM_EPS)
        dhv = dh_ref[...]
        dyw = dhv * w_ref[...]
        dx_ref[...] = dx2_ref[...] + r * dyw - xv * (r * r * r) * jnp.mean(dyw * xv, axis=-1, keepdims=True)
        part = jnp.sum(dhv * xv * r, axis=0, keepdims=True)

        @pl.when(i == 0)
        def _():
            dw_ref[...] = part

        @pl.when(i > 0)
        def _():
            dw_ref[...] += part

    row = pl.BlockSpec((ROW_TILE, d), lambda i: (i, 0))
    vec = pl.BlockSpec((1, d), lambda i: (0, 0))
    return pl.pallas_call(
        body, name="rms_in_bwd", grid=(s // ROW_TILE,), in_specs=[row, vec, row, row], out_specs=[row, vec],
        out_shape=[jax.ShapeDtypeStruct((s, d), F32), jax.ShapeDtypeStruct((1, d), F32)],
        compiler_params=_cparams("arbitrary"))(x, nw, dh, dx2)


def _shift_down(cur, prev8, k):
    rc = pltpu.roll(cur, k, 0)
    rp = pltpu.roll(prev8, k, 0)
    row = lax.broadcasted_iota(jnp.int32, prev8.shape, 0)
    top = jnp.where(row < k, rp, rc[:8])
    return jnp.concatenate([top, rc[8:]], axis=0)


def _shift_up(cur, next8, k):
    t = cur.shape[0]
    rc = pltpu.roll(cur, t - k, 0)
    rn = pltpu.roll(next8, 8 - k, 0)
    row = lax.broadcasted_iota(jnp.int32, next8.shape, 0)
    bot = jnp.where(row >= 8 - k, rn, rc[t - 8:])
    return jnp.concatenate([rc[:t - 8], bot], axis=0)


def _conv_fwd(proj, conv_w):
    s = proj.shape[0]
    t8 = ROW_TILE // 8

    def body(u_ref, up_ref, w_ref, c_ref, y_ref):
        i = pl.program_id(0)
        part = pl.program_id(1)
        cur = u_ref[...]
        prev8 = jnp.where(i > 0, up_ref[...], 0.0)
        w = w_ref[...]
        c = cur * w[3:4, :]
        for k in (1, 2, 3):
            c = c + _shift_down(cur, prev8, k) * w[3 - k:4 - k, :]
        c_ref[...] = c
        a = _silu(c)
        for h in range(DN_HEADS):
            ah = a[:, h * DN_D:(h + 1) * DN_D]
            r = lax.rsqrt(jnp.sum(ah * ah, axis=-1, keepdims=True) + NORM_EPS)
            y_ref[:, h * DN_D:(h + 1) * DN_D] = jnp.where(part < 2, ah * r, ah)

    return pl.pallas_call(
        body, name="conv_fwd", grid=(s // ROW_TILE, 3),
        in_specs=[pl.BlockSpec((ROW_TILE, DN_W), lambda i, p: (i, p)),
                  pl.BlockSpec((8, DN_W), lambda i, p: (jnp.maximum(i * t8 - 1, 0), p)),
                  pl.BlockSpec((4, DN_W), lambda i, p: (0, p))],
        out_specs=[pl.BlockSpec((ROW_TILE, DN_W), lambda i, p: (i, p))] * 2,
        out_shape=[jax.ShapeDtypeStruct((s, 3 * DN_W), F32)] * 2,
        compiler_params=_cparams("parallel", "parallel"))(proj, proj, conv_w)


def _conv_bwd_act(c, dq, dk, dv):
    s = c.shape[0]

    def body(c_ref, dq_ref, dk_ref, dv_ref, dc_ref):
        for part, d_ref in enumerate((dq_ref, dk_ref, dv_ref)):
            for h in range(DN_HEADS):
                sl = slice(part * DN_W + h * DN_D, part * DN_W + (h + 1) * DN_D)
                ch = c_ref[:, sl]
                dyh = d_ref[:, h * DN_D:(h + 1) * DN_D]
                if part < 2:
                    ah = _silu(ch)
                    r = lax.rsqrt(jnp.sum(ah * ah, axis=-1, keepdims=True) + NORM_EPS)
                    dyh = r * dyh - ah * (r * r * r) * jnp.sum(dyh * ah, axis=-1, keepdims=True)
                dc_ref[:, sl] = dyh * _silu_grad(ch)

    wide = pl.BlockSpec((ROW_TILE, 3 * DN_W), lambda i: (i, 0))
    row = pl.BlockSpec((ROW_TILE, DN_W), lambda i: (i, 0))
    return pl.pallas_call(
        body, name="conv_bwd_act", grid=(s // ROW_TILE,), in_specs=[wide, row, row, row], out_specs=wide,
        out_shape=jax.ShapeDtypeStruct((s, 3 * DN_W), F32), compiler_params=_cparams("parallel"))(c, dq, dk, dv)


def _conv_bwd(proj, dc, conv_w):
    s = proj.shape[0]
    t8 = ROW_TILE // 8
    nrow = s // ROW_TILE
    last8 = s // 8 - 1

    def body(u_ref, up_ref, dc_ref, dcn_ref, w_ref, du_ref, dw_ref):
        i = pl.program_id(1)
        cur = u_ref[...]
        prev8 = jnp.where(i > 0, up_ref[...], 0.0)
        dcv = dc_ref[...]
        next8 = jnp.where(i < nrow - 1, dcn_ref[...], 0.0)
        w = w_ref[...]
        du = dcv * w[3:4, :]
        for k in (1, 2, 3):
            du = du + _shift_up(dcv, next8, k) * w[3 - k:4 - k, :]
        du_ref[...] = du.astype(BF16)

        @pl.when(i == 0)
        def _():
            dw_ref[...] = jnp.zeros_like(dw_ref)

        dw_ref[3:4, :] += jnp.sum(cur * dcv, axis=0, keepdims=True)
        for k in (1, 2, 3):
            dw_ref[3 - k:4 - k, :] += jnp.sum(_shift_down(cur, prev8, k) * dcv, axis=0, keepdims=True)

    blk = pl.BlockSpec((ROW_TILE, DN_W), lambda p, i: (i, p))
    return pl.pallas_call(
        body, name="conv_bwd", grid=(3, nrow),
        in_specs=[blk, pl.BlockSpec((8, DN_W), lambda p, i: (jnp.maximum(i * t8 - 1, 0), p)),
                  blk, pl.BlockSpec((8, DN_W), lambda p, i: (jnp.minimum((i + 1) * t8, last8), p)),
                  pl.BlockSpec((4, DN_W), lambda p, i: (0, p))],
        out_specs=[blk, pl.BlockSpec((4, DN_W), lambda p, i: (0, p))],
        out_shape=[jax.ShapeDtypeStruct((s, 3 * DN_W), BF16), jax.ShapeDtypeStruct((4, 3 * DN_W), F32)],
        compiler_params=_cparams("parallel", "arbitrary"))(proj, proj, dc, dc, conv_w)


def _gates_fwd(proj, gate_par):
    s = proj.shape[0]

    def body(ba_ref, par_ref, o_ref):
        v = ba_ref[...]
        lane = lax.broadcasted_iota(jnp.int32, v.shape, 1)
        beta = _sigmoid(v)
        g = -jnp.exp(par_ref[0:1, :]) * _softplus(v + par_ref[1:2, :])
        o_ref[...] = jnp.where(lane < DN_HEADS, beta, jnp.where(lane < 2 * DN_HEADS, g, 0.0))

    return pl.pallas_call(
        body, name="gates_fwd", grid=(s // ROW_TILE,),
        in_specs=[pl.BlockSpec((ROW_TILE, 128), lambda i: (i, OFF_BA // 128)), pl.BlockSpec((8, 128), lambda i: (0, 0))],
        out_specs=pl.BlockSpec((ROW_TILE, 128), lambda i: (i, 0)),
        out_shape=jax.ShapeDtypeStruct((s, 128), F32), compiler_params=_cparams("parallel"))(proj, gate_par)


def _gates_bwd(proj, gate_par, dbg):
    s = proj.shape[0]

    def body(ba_ref, par_ref, d_ref, o_ref, dpar_ref):
        i = pl.program_id(0)
        v = ba_ref[...]
        dv = d_ref[...]
        lane = lax.broadcasted_iota(jnp.int32, v.shape, 1)
        beta = _sigmoid(v)
        nega = -jnp.exp(par_ref[0:1, :])
        xs = v + par_ref[1:2, :]
        dsp = dv * nega * _sigmoid(xs)
        dal = dv * nega * _softplus(xs)
        is_b = lane < DN_HEADS
        is_g = jnp.logical_and(lane >= DN_HEADS, lane < 2 * DN_HEADS)
        o_ref[...] = jnp.where(is_b, dv * beta * (1.0 - beta), jnp.where(is_g, dsp, 0.0)).astype(BF16)
        r0 = jnp.sum(jnp.where(is_g, dal, 0.0), axis=0, keepdims=True)
        r1 = jnp.sum(jnp.where(is_g, dsp, 0.0), axis=0, keepdims=True)

        @pl.when(i == 0)
        def _():
            dpar_ref[...] = jnp.zeros_like(dpar_ref)

        dpar_ref[0:1, :] += r0
        dpar_ref[1:2, :] += r1

    return pl.pallas_call(
        body, name="gates_bwd", grid=(s // ROW_TILE,),
        in_specs=[pl.BlockSpec((ROW_TILE, 128), lambda i: (i, OFF_BA // 128)), pl.BlockSpec((8, 128), lambda i: (0, 0)),
                  pl.BlockSpec((ROW_TILE, 128), lambda i: (i, 0))],
        out_specs=[pl.BlockSpec((ROW_TILE, 128), lambda i: (i, 0)), pl.BlockSpec((8, 128), lambda i: (0, 0))],
        out_shape=[jax.ShapeDtypeStruct((s, 128), BF16), jax.ShapeDtypeStruct((8, 128), F32)],
        compiler_params=_cparams("arbitrary"))(proj, gate_par, dbg)


def _chunk_masks():
    c = DN_CHUNK
    ii = lax.broadcasted_iota(jnp.int32, (c, c), 0)
    jj = lax.broadcasted_iota(jnp.int32, (c, c), 1)
    return dict(ii=ii, jj=jj, lower=(ii >= jj), strict=(ii > jj), eye=(ii == jj),
                lower_f=(ii >= jj).astype(BF16), upper_f=(ii <= jj).astype(BF16), ones8=jnp.ones((8, c), BF16))


class _Heads:
    def __init__(self, xs):
        self.xs = list(xs)

    def _bin(self, o, f):
        if isinstance(o, _Heads):
            return _Heads([f(a, b) for a, b in zip(self.xs, o.xs)])
        return _Heads([f(a, o) for a in self.xs])

    def __add__(self, o):
        return self._bin(o, lambda a, b: a + b)

    def __sub__(self, o):
        return self._bin(o, lambda a, b: a - b)

    def __mul__(self, o):
        return self._bin(o, lambda a, b: a * b)

    __radd__ = __add__
    __rmul__ = __mul__

    def __neg__(self):
        return _Heads([-a for a in self.xs])

    def __getitem__(self, i):
        return _Heads([a[i] for a in self.xs])


def _hmap(f, *args):
    n = next(len(a.xs) for a in args if isinstance(a, _Heads))
    return _Heads([f(*[(a.xs[h] if isinstance(a, _Heads) else a) for a in args]) for h in range(n)])


def _hdot(a, b):
    return _hmap(_dot, a, b)


def _hdot_nt(a, b):
    return _hmap(_dot_nt, a, b)


def _hdot_tn(a, b):
    return _hmap(_dot_tn, a, b)


def _hsum(a, axis):
    return _hmap(lambda t: jnp.sum(t, axis=axis, keepdims=True), a)


def _hwhere(c, a, b):
    return _hmap(jnp.where, c, a, b)


def _chunk_gates(mk, bg):
    c = DN_CHUNK
    gc_all = _dot_exact_lhs(mk["lower_f"], bg)
    rows = jnp.concatenate([gc_all, gc_all], axis=0).T
    hs = range(DN_HEADS)
    return (_Heads(bg[:, h:h + 1] for h in hs), _Heads(gc_all[:, DN_HEADS + h:DN_HEADS + h + 1] for h in hs),
            _Heads(rows[DN_HEADS + h:DN_HEADS + h + 1, :] for h in hs))


def _chunk_common(mk, q, k, beta_col, gc_col, gc_r):
    c = DN_CHUNK
    lower, strict = mk["lower"], mk["strict"]
    qs = q * (DN_D ** -0.5)
    beta_b = _hmap(lambda t: jnp.broadcast_to(t, (c, DN_D)), beta_col)
    gc_b = _hmap(lambda t: jnp.broadcast_to(t, (c, DN_D)), gc_col)
    gc_sq = gc_b[:, :c]
    gam = _hwhere(lower, _hmap(lambda t: jnp.exp(jnp.minimum(t, 0.0)), gc_sq - gc_r[:, :c]), 0.0)
    egc = _hmap(jnp.exp, gc_b)
    gl = gc_b[c - 1:c, :]
    ekd = _hmap(jnp.exp, gl - gc_b)
    dl = _hmap(jnp.exp, gl)
    kb = k * beta_b
    a_strict = _hwhere(strict, _hdot_nt(kb, k) * gam, 0.0)
    aqk = _hwhere(lower, _hdot_nt(qs, k) * gam, 0.0)
    return dict(k=k, qs=qs, beta_b=beta_b, gc_b=gc_b, gam=gam, egc=egc, ekd=ekd, dl=dl, kb=kb, a_strict=a_strict, aqk=aqk)


def _unit_lower_inverse_minus_eye(n_strict, ii, jj):
    same = lax.shift_right_logical(ii, 4) == lax.shift_right_logical(jj, 4)
    dmat = _hwhere(same, n_strict, 0.0)
    omat = n_strict - dmat
    d2 = _hdot(dmat, dmat)
    d4 = _hdot(d2, d2)
    d8 = _hdot(d4, d4)
    x1 = d2 - dmat - _hdot(dmat, d2)
    x2 = x1 + d4 + _hdot(x1, d4)
    x3 = x2 + d8 + _hdot(x2, d8)
    n1 = omat + _hdot(x3, omat)
    n2 = _hdot(n1, n1)
    y = n2 - n1 - _hdot(n1, n2)
    return y + x3 + _hdot(y, x3)


def _gdr_fwd(qkv, bg):
    s = qkv.shape[0]
    c = DN_CHUNK
    n = s // c

    def body(q_ref, k_ref, v_ref, bg_ref, o_ref, u_ref, w_ref, vn_ref, tm_ref, st_ref, state):
        @pl.when(pl.program_id(0) == 0)
        def _():
            state[...] = jnp.zeros_like(state)

        mk = _chunk_masks()
        bg = bg_ref[...]
        hs = range(DN_HEADS)
        sls = [slice(h * DN_D, (h + 1) * DN_D) for h in hs]
        cm = _chunk_common(mk, _Heads(q_ref[:, sl] for sl in sls), _Heads(k_ref[:, sl] for sl in sls),
                           *_chunk_gates(mk, bg))
        tm = _unit_lower_inverse_minus_eye(cm["a_strict"], mk["ii"], mk["jj"])
        rhs_u = _Heads(v_ref[:, sl] for sl in sls) * cm["beta_b"]
        rhs_w = cm["kb"] * cm["egc"]
        u = rhs_u + _hdot(tm, rhs_u)
        w = rhs_w + _hdot(tm, rhs_w)
        st = _Heads(state[h] for h in hs)
        v_new = u - _hdot(w, st)
        o = _hdot(cm["qs"] * cm["egc"], st) + _hdot(cm["aqk"], v_new)
        st_new = st * cm["dl"] + _hdot_tn(cm["k"] * cm["ekd"], v_new)
        for h, sl in zip(hs, sls):
            o_ref[:, sl] = o.xs[h]
            u_ref[:, sl] = u.xs[h]
            w_ref[:, sl] = w.xs[h]
            vn_ref[:, sl] = v_new.xs[h]
            tm_ref[h, 0] = tm.xs[h]
            st_ref[h, 0] = st.xs[h]
            state[h] = st_new.xs[h]

    def part(p):
        return pl.BlockSpec((c, DN_W), lambda j: (j, p))

    return pl.pallas_call(
        body, name="gdr_fwd", grid=(n,),
        in_specs=[part(0), part(1), part(2), pl.BlockSpec((c, 128), lambda j: (j, 0))],
        out_specs=[part(0)] * 4 + [pl.BlockSpec((DN_HEADS, 1, c, c), lambda j: (0, j, 0, 0)),
                                   pl.BlockSpec((DN_HEADS, 1, DN_D, DN_D), lambda j: (0, j, 0, 0))],
        out_shape=[jax.ShapeDtypeStruct((s, DN_W), F32)] * 4
        + [jax.ShapeDtypeStruct((DN_HEADS, n, c, c), F32), jax.ShapeDtypeStruct((DN_HEADS, n, DN_D, DN_D), F32)],
        scratch_shapes=[pltpu.VMEM((DN_HEADS, DN_D, DN_D), F32)],
        compiler_params=_cparams("arbitrary"))(qkv, qkv, qkv, bg)


def _gdr_bwd(qkv, bg, u, w, vn, tmat, states, do):
    s = qkv.shape[0]
    c = DN_CHUNK
    n = s // c

    def body(q_ref, k_ref, v_ref, bg_ref, u_ref, w_ref, vn_ref, tm_ref, st_ref, do_ref,
             dq_ref, dk_ref, dv_ref, dbg_ref, dstate):
        @pl.when(pl.program_id(0) == 0)
        def _():
            dstate[...] = jnp.zeros_like(dstate)

        mk = _chunk_masks()
        lower, strict = mk["lower"], mk["strict"]
        bg = bg_ref[...]
        ones = jnp.ones((c, DN_D), BF16)
        rowi = lax.broadcasted_iota(jnp.int32, (c, DN_D), 0)
        lane = lax.broadcasted_iota(jnp.int32, (c, 128), 1)
        hs = range(DN_HEADS)
        sls = [slice(h * DN_D, (h + 1) * DN_D) for h in hs]

        def heads_of(ref):
            return _Heads(ref[:, sl] for sl in sls)

        cm = _chunk_common(mk, heads_of(q_ref), heads_of(k_ref), *_chunk_gates(mk, bg))
        k, qs, beta_b = cm["k"], cm["qs"], cm["beta_b"]
        gam, egc, ekd, dl, kb = cm["gam"], cm["egc"], cm["ekd"], cm["dl"], cm["kb"]
        aqk, a_strict = cm["aqk"], cm["a_strict"]
        v, uu, ww, v_new, dov = heads_of(v_ref), heads_of(u_ref), heads_of(w_ref), heads_of(vn_ref), heads_of(do_ref)
        st = _Heads(st_ref[h, 0] for h in hs)
        dsn = _Heads(dstate[h] for h in hs)
        qd = qs * egc
        kd = k * ekd

        dv_new = _hdot_tn(aqk, dov) + _hdot(kd, dsn)
        daqk = _hwhere(lower, _hdot_nt(dov, v_new), 0.0)
        dqd = _hdot_nt(dov, st)
        dkd = _hdot_nt(v_new, dsn)
        ddl = _hsum(_hsum(dsn * st, 1), 0)
        dw = -_hdot_nt(dv_new, st)
        ds_new = dsn * dl + _hdot_tn(qd, dov) - _hdot_tn(ww, dv_new)

        tm = _Heads(tm_ref[h, 0] for h in hs)
        dru = dv_new + _hdot_tn(tm, dv_new)
        drw = dw + _hdot_tn(tm, dw)
        dn = _hwhere(strict, -(_hdot_nt(dru, uu) + _hdot_nt(drw, ww)), 0.0)
        dag = dn * gam
        dkb = _hdot(dag, k) + drw * egc
        dk = _hdot_tn(dag, kb)
        dqg = daqk * gam
        dqs = _hdot(dqg, k) + dqd * egc
        dk = dk + _hdot_tn(dqg, qs) + dkb * beta_b + dkd * ekd
        pmat = dn * a_strict + daqk * aqk
        tkd = _hsum(dkd * kd, -1)
        dgc = (_hsum(pmat, -1) - _hmap(_dot_tn_exact_rhs, pmat, ones) + _hsum(drw * (kb * egc), -1)
               + _hsum(dqd * qd, -1) - tkd)
        last = _hsum(tkd, 0) + ddl * dl
        dgc = dgc + _hwhere(rowi == c - 1, last, 0.0)
        dbeta = _hsum(dru * v, -1) + _hsum(dkb * k, -1)
        dq = dqs * (DN_D ** -0.5)
        dv = dru * beta_b

        dgc_all = jnp.zeros((c, 128), F32)
        dbg = jnp.zeros((c, 128), F32)
        for h, sl in zip(hs, sls):
            dq_ref[:, sl] = dq.xs[h]
            dk_ref[:, sl] = dk.xs[h]
            dv_ref[:, sl] = dv.xs[h]
            dstate[h] = ds_new.xs[h]
            dgc_all = dgc_all + jnp.where(lane == DN_HEADS + h, dgc.xs[h], 0.0)
            dbg = dbg + jnp.where(lane == h, dbeta.xs[h], 0.0)
        dbg_ref[...] = dbg + _dot_exact_lhs(mk["upper_f"], dgc_all)

    def part(p):
        return pl.BlockSpec((c, DN_W), lambda j: (n - 1 - j, p))

    vec = pl.BlockSpec((c, 128), lambda j: (n - 1 - j, 0))
    return pl.pallas_call(
        body, name="gdr_bwd", grid=(n,),
        in_specs=[part(0), part(1), part(2), vec, part(0), part(0), part(0),
                  pl.BlockSpec((DN_HEADS, 1, c, c), lambda j: (0, n - 1 - j, 0, 0)),
                  pl.BlockSpec((DN_HEADS, 1, DN_D, DN_D), lambda j: (0, n - 1 - j, 0, 0)), part(0)],
        out_specs=[part(0), part(0), part(0), vec],
        out_shape=[jax.ShapeDtypeStruct((s, DN_W), F32)] * 3 + [jax.ShapeDtypeStruct((s, 128), F32)],
        scratch_shapes=[pltpu.VMEM((DN_HEADS, DN_D, DN_D), F32)],
        compiler_params=_cparams("arbitrary"))(qkv, qkv, qkv, bg, u, w, vn, tmat, states, do)


def _gdr_out(o, proj, dnw):
    s = o.shape[0]

    def body(o_ref, z_ref, w_ref, y_ref, yt_ref):
        ov, zv, wv = o_ref[...], z_ref[...], w_ref[...]
        for h in range(DN_HEADS):
            sl = slice(h * DN_D, (h + 1) * DN_D)
            oh = ov[:, sl]
            r = lax.rsqrt(jnp.mean(oh * oh, axis=-1, keepdims=True) + NORM_EPS)
            y = (oh * r * wv) * _silu(zv[:, sl])
            y_ref[:, sl] = y.astype(BF16)
            yt_ref[sl, :] = y.T.astype(BF16)

    row = pl.BlockSpec((ROW_TILE, DN_W), lambda i: (i, 0))
    return pl.pallas_call(
        body, name="gdr_out", grid=(s // ROW_TILE,),
        in_specs=[row, pl.BlockSpec((ROW_TILE, DN_W), lambda i: (i, OFF_Z_A // DN_W)), pl.BlockSpec((1, DN_D), lambda i: (0, 0))],
        out_specs=[row, pl.BlockSpec((DN_W, ROW_TILE), lambda i: (0, i))],
        out_shape=[jax.ShapeDtypeStruct((s, DN_W), BF16), jax.ShapeDtypeStruct((DN_W, s), BF16)],
        compiler_params=_cparams("parallel"))(o, proj, dnw)


def _gdr_out_bwd(o, proj, dnw, dy):
    s = o.shape[0]

    def body(o_ref, z_ref, w_ref, dy_ref, do_ref, dz_ref, dw_ref):
        i = pl.program_id(0)
        ov, zv, wv, dyv = o_ref[...], z_ref[...], w_ref[...], dy_ref[...]
        acc = jnp.zeros((1, DN_D), F32)
        for h in range(DN_HEADS):
            sl = slice(h * DN_D, (h + 1) * DN_D)
            oh, zh, dh = ov[:, sl], zv[:, sl], dyv[:, sl]
            r = lax.rsqrt(jnp.mean(oh * oh, axis=-1, keepdims=True) + NORM_EPS)
            dn = dh * _silu(zh)
            dz_ref[:, sl] = (dh * (oh * r * wv) * _silu_grad(zh)).astype(BF16)
            acc = acc + jnp.sum(dn * oh * r, axis=0, keepdims=True)
            dnw_ = dn * wv
            do_ref[:, sl] = r * dnw_ - oh * (r * r * r) * jnp.mean(dnw_ * oh, axis=-1, keepdims=True)

        @pl.when(i == 0)
        def _():
            dw_ref[...] = acc

        @pl.when(i > 0)
        def _():
            dw_ref[...] += acc

    row = pl.BlockSpec((ROW_TILE, DN_W), lambda i: (i, 0))
    vec = pl.BlockSpec((1, DN_D), lambda i: (0, 0))
    return pl.pallas_call(
        body, name="gdr_out_bwd", grid=(s // ROW_TILE,),
        in_specs=[row, pl.BlockSpec((ROW_TILE, DN_W), lambda i: (i, OFF_Z_A // DN_W)), vec, row],
        out_specs=[row, row, vec],
        out_shape=[jax.ShapeDtypeStruct((s, DN_W), F32), jax.ShapeDtypeStruct((s, DN_W), BF16),
                   jax.ShapeDtypeStruct((1, DN_D), F32)],
        compiler_params=_cparams("arbitrary"))(o, proj, dnw, dy)


def _slope(group, head):
    idx = (group * DIL_HEADS + head + 1).astype(F32)
    return jnp.exp(jnp.full((1, 128), -8.0 * math.log(2.0) / (N_DIL * DIL_HEADS), F32) * idx)


def _att_scores(qb, k_cur, k_prev, slope_d, has_prev):
    iq = lax.broadcasted_iota(jnp.int32, (ATT_BLOCK, ATT_BLOCK), 0)
    jk = lax.broadcasted_iota(jnp.int32, (ATT_BLOCK, ATT_BLOCK), 1)
    dist_c = (iq - jk).astype(F32)
    s_cur = jnp.where(iq >= jk, _dot_nt(qb, k_cur) - slope_d * dist_c, NEG)
    s_prev = jnp.where(jnp.logical_and(jk >= iq, has_prev),
                       _dot_nt(qb, k_prev) - slope_d * (dist_c + float(ATT_BLOCK)), NEG)
    return s_cur, s_prev


ATT_UNROLL = 4


def _att_blocks(i, dil, nb):
    per = dil * nb // ATT_UNROLL
    assert per * ATT_UNROLL == dil * nb
    for i0 in range(per):
        blocks = [divmod(i0 + u * per, nb) for u in range(ATT_UNROLL)]
        assert all(a[0] != b[0] or abs(a[1] - b[1]) >= 2 for n, a in enumerate(blocks) for b in blocks[n + 1:])
    curs, prvs, has_prev = [], [], []
    for u in range(ATT_UNROLL):
        t = i + u * per
        r = lax.div(t, nb)
        j = lax.rem(t, nb)
        base = r + dil * ATT_BLOCK * j
        pbase = base - dil * ATT_BLOCK * jnp.minimum(j, 1)
        if dil == 1:
            base, pbase = pl.multiple_of(base, ATT_BLOCK), pl.multiple_of(pbase, ATT_BLOCK)
        curs.append(pl.ds(base, ATT_BLOCK, stride=dil))
        prvs.append(pl.ds(pbase, ATT_BLOCK, stride=dil))
        has_prev.append(j > 0)
    return curs, prvs, has_prev


def _att_fwd(proj, group):
    s = proj.shape[0]
    dil = DIL_GROUPS[group][1]
    assert DIL_GROUPS[group][0] // dil == ATT_BLOCK
    nb = s // dil // ATT_BLOCK
    assert nb * dil * ATT_BLOCK == s

    def body(q_ref, k_ref, v_ref, num_ref, den_ref, mx_ref):
        slope_d = _slope(group, pl.program_id(0)) * float(dil)

        def step(i, carry):
            curs, prvs, has_prev = _att_blocks(i, dil, nb)
            us = range(ATT_UNROLL)
            qb = [q_ref[c, :] * (DIL_DH ** -0.5) for c in curs]
            sc = [_att_scores(qb[u], k_ref[curs[u], :], k_ref[prvs[u], :], slope_d, has_prev[u]) for u in us]
            mx = [jnp.maximum(jnp.max(a, axis=-1, keepdims=True), jnp.max(b, axis=-1, keepdims=True)) for a, b in sc]
            p_cur = [jnp.exp(sc[u][0] - mx[u]) for u in us]
            p_prev = [jnp.exp(sc[u][1] - mx[u]) for u in us]
            den = [jnp.sum(p_cur[u], axis=-1, keepdims=True) + jnp.sum(p_prev[u], axis=-1, keepdims=True) for u in us]
            num = [_dot(p_cur[u], v_ref[curs[u], :]) + _dot(p_prev[u], v_ref[prvs[u], :]) for u in us]
            for u in us:
                num_ref[curs[u], :] = num[u]
                den_ref[curs[u], :] = jnp.broadcast_to(den[u], (ATT_BLOCK, DIL_DH))
                mx_ref[curs[u], :] = jnp.broadcast_to(mx[u], (ATT_BLOCK, DIL_DH))
            return carry

        lax.fori_loop(0, dil * nb // ATT_UNROLL, step, 0)

    def col(off):
        return pl.BlockSpec((s, DIL_DH), lambda h: (0, off // DIL_DH + group * DIL_HEADS + h))

    out = pl.BlockSpec((s, DIL_DH), lambda h: (0, h))
    return pl.pallas_call(
        body, name=f"att_fwd{group}", grid=(DIL_HEADS,), in_specs=[col(OFF_Q_B), col(OFF_K_B), col(OFF_V_B)],
        out_specs=[out, out, out], out_shape=[jax.ShapeDtypeStruct((s, DIL_W), F32)] * 3,
        compiler_params=_cparams("parallel"))(proj, proj, proj)


def _att_bwd(proj, group, do, lse, delta):
    s = proj.shape[0]
    dil = DIL_GROUPS[group][1]
    nb = s // dil // ATT_BLOCK

    def body(q_ref, k_ref, v_ref, do_ref, lse_ref, dl_ref, dq_ref, dk_ref, dv_ref, dq_acc, dk_acc, dv_acc):
        slope_d = _slope(group, pl.program_id(0)) * float(dil)
        dk_acc[...] = jnp.zeros_like(dk_acc)
        dv_acc[...] = jnp.zeros_like(dv_acc)

        def step(i, carry):
            curs, prvs, has_prev = _att_blocks(i, dil, nb)
            us = range(ATT_UNROLL)
            qb = [q_ref[c, :] * (DIL_DH ** -0.5) for c in curs]
            k_cur, k_prev = [k_ref[c, :] for c in curs], [k_ref[p, :] for p in prvs]
            v_cur, v_prev = [v_ref[c, :] for c in curs], [v_ref[p, :] for p in prvs]
            sc = [_att_scores(qb[u], k_cur[u], k_prev[u], slope_d, has_prev[u]) for u in us]
            lse_b, delta_b, dob = [lse_ref[c, :] for c in curs], [dl_ref[c, :] for c in curs], [do_ref[c, :] for c in curs]
            p_cur = [jnp.exp(sc[u][0] - lse_b[u]) for u in us]
            p_prev = [jnp.exp(sc[u][1] - lse_b[u]) for u in us]
            ds_cur = [p_cur[u] * (_dot_nt(dob[u], v_cur[u]) - delta_b[u]) for u in us]
            ds_prev = [p_prev[u] * (_dot_nt(dob[u], v_prev[u]) - delta_b[u]) for u in us]
            dq = [(_dot(ds_cur[u], k_cur[u]) + _dot(ds_prev[u], k_prev[u])) * (DIL_DH ** -0.5) for u in us]
            dk_c = [_dot_tn(ds_cur[u], qb[u]) for u in us]
            dv_c = [_dot_tn(p_cur[u], dob[u]) for u in us]
            dk_p = [_dot_tn(ds_prev[u], qb[u]) for u in us]
            dv_p = [_dot_tn(p_prev[u], dob[u]) for u in us]
            for u in us:
                dq_acc[curs[u], :] = dq[u]
                dk_acc[curs[u], :] += dk_c[u]
                dv_acc[curs[u], :] += dv_c[u]
            for u in us:
                dk_acc[prvs[u], :] += dk_p[u]
                dv_acc[prvs[u], :] += dv_p[u]
            return carry

        lax.fori_loop(0, dil * nb // ATT_UNROLL, step, 0)
        dq_ref[...] = dq_acc[...].astype(BF16)
        dk_ref[...] = dk_acc[...].astype(BF16)
        dv_ref[...] = dv_acc[...].astype(BF16)

    def col(off):
        return pl.BlockSpec((s, DIL_DH), lambda h: (0, off // DIL_DH + group * DIL_HEADS + h))

    hd = pl.BlockSpec((s, DIL_DH), lambda h: (0, h))
    return pl.pallas_call(
        body, name=f"att_bwd{group}", grid=(DIL_HEADS,),
        in_specs=[col(OFF_Q_B), col(OFF_K_B), col(OFF_V_B), hd, hd, hd], out_specs=[hd, hd, hd],
        out_shape=[jax.ShapeDtypeStruct((s, DIL_W), BF16)] * 3,
        scratch_shapes=[pltpu.VMEM((s, DIL_DH), F32)] * 3,
        compiler_params=_cparams("parallel"))(proj, proj, proj, do, lse, delta)


def _att_merge(parts, proj):
    s = proj.shape[0]

    def body(n0, d0, m0, n1, d1, m1, n2, d2, m2, z_ref, ob_ref, o_ref, lse_ref, obt_ref):
        m = jnp.maximum(jnp.maximum(m0[...], m1[...]), m2[...])
        num = jnp.zeros_like(m)
        den = jnp.zeros_like(m)
        for nr, dr, mr in ((n0, d0, m0), (n1, d1, m1), (n2, d2, m2)):
            sc = jnp.exp(mr[...] - m)
            num = num + nr[...] * sc
            den = den + dr[...] * sc
        o = num / den
        o_ref[...] = o
        lse_ref[...] = m + jnp.log(den)
        ob = o * _silu(z_ref[...])
        ob_ref[...] = ob.astype(BF16)
        obt_ref[...] = ob.T.astype(BF16)

    row = pl.BlockSpec((ROW_TILE, DIL_W), lambda i: (i, 0))
    flat = [a for p in parts for a in p]
    return pl.pallas_call(
        body, name="att_merge", grid=(s // ROW_TILE,),
        in_specs=[row] * 9 + [pl.BlockSpec((ROW_TILE, DIL_W), lambda i: (i, OFF_Z_B // DIL_W))],
        out_specs=[row, row, row, pl.BlockSpec((DIL_W, ROW_TILE), lambda i: (0, i))],
        out_shape=[jax.ShapeDtypeStruct((s, DIL_W), BF16), jax.ShapeDtypeStruct((s, DIL_W), F32),
                   jax.ShapeDtypeStruct((s, DIL_W), F32), jax.ShapeDtypeStruct((DIL_W, s), BF16)],
        compiler_params=_cparams("parallel"))(*flat, proj)


def _att_merge_bwd(o, proj, dob):
    s = o.shape[0]

    def body(o_ref, z_ref, d_ref, do_ref, dl_ref, dz_ref):
        ov, zv, dv = o_ref[...], z_ref[...], d_ref[...]
        do = dv * _silu(zv)
        do_ref[...] = do
        dz_ref[...] = (dv * ov * _silu_grad(zv)).astype(BF16)
        for h in range(DIL_HEADS):
            sl = slice(h * DIL_DH, (h + 1) * DIL_DH)
            dl_ref[:, sl] = jnp.broadcast_to(jnp.sum(do[:, sl] * ov[:, sl], axis=-1, keepdims=True), (ROW_TILE, DIL_DH))

    row = pl.BlockSpec((ROW_TILE, DIL_W), lambda i: (i, 0))
    return pl.pallas_call(
        body, name="att_merge_bwd", grid=(s // ROW_TILE,),
        in_specs=[row, pl.BlockSpec((ROW_TILE, DIL_W), lambda i: (i, OFF_Z_B // DIL_W)), row],
        out_specs=[row, row, row],
        out_shape=[jax.ShapeDtypeStruct((s, DIL_W), F32), jax.ShapeDtypeStruct((s, DIL_W), F32),
                   jax.ShapeDtypeStruct((s, DIL_W), BF16)],
        compiler_params=_cparams("parallel"))(o, proj, dob)


def _merge(proj, ya, yb):
    s = proj.shape[0]

    def body(ga_ref, gb_ref, ya_ref, yb_ref, o_ref, ot_ref):
        m = _sigmoid(ga_ref[...]) * ya_ref[...] + _sigmoid(gb_ref[...]) * yb_ref[...]
        o_ref[...] = m.astype(BF16)
        ot_ref[...] = m.T.astype(BF16)

    row = pl.BlockSpec((ROW_TILE, D_MODEL), lambda i: (i, 0))
    return pl.pallas_call(
        body, name="merge", grid=(s // ROW_TILE,),
        in_specs=[pl.BlockSpec((ROW_TILE, D_MODEL), lambda i: (i, OFF_G_A // D_MODEL)),
                  pl.BlockSpec((ROW_TILE, D_MODEL), lambda i: (i, OFF_G_B // D_MODEL)), row, row],
        out_specs=[row, pl.BlockSpec((D_MODEL, ROW_TILE), lambda i: (0, i))],
        out_shape=[jax.ShapeDtypeStruct((s, D_MODEL), BF16), jax.ShapeDtypeStruct((D_MODEL, s), BF16)],
        compiler_params=_cparams("parallel"))(proj, proj, ya, yb)


def _merge_bwd(proj, ya, yb, dm):
    s = proj.shape[0]

    def body(ga_ref, gb_ref, ya_ref, yb_ref, dm_ref, dya_ref, dyb_ref, dga_ref, dgb_ref):
        dmv = dm_ref[...]
        sa, sb = _sigmoid(ga_ref[...]), _sigmoid(gb_ref[...])
        dya_ref[...] = (dmv * sa).astype(BF16)
        dyb_ref[...] = (dmv * sb).astype(BF16)
        dga_ref[...] = (dmv * ya_ref[...] * sa * (1.0 - sa)).astype(BF16)
        dgb_ref[...] = (dmv * yb_ref[...] * sb * (1.0 - sb)).astype(BF16)

    row = pl.BlockSpec((ROW_TILE, D_MODEL), lambda i: (i, 0))
    return pl.pallas_call(
        body, name="merge_bwd", grid=(s // ROW_TILE,),
        in_specs=[pl.BlockSpec((ROW_TILE, D_MODEL), lambda i: (i, OFF_G_A // D_MODEL)),
                  pl.BlockSpec((ROW_TILE, D_MODEL), lambda i: (i, OFF_G_B // D_MODEL)), row, row, row],
        out_specs=[row] * 4, out_shape=[jax.ShapeDtypeStruct((s, D_MODEL), BF16)] * 4,
        compiler_params=_cparams("parallel"))(proj, proj, ya, yb, dm)


def _final(x, t, fw, tgt):
    s, d = x.shape

    def body(x_ref, t_ref, w_ref, y_ref, dx_ref, dw_ref, l_ref):
        i = pl.program_id(0)
        x2 = x_ref[...] + t_ref[...]
        wv = w_ref[...]
        r = lax.rsqrt(jnp.mean(x2 * x2, axis=-1, keepdims=True) + NORM_EPS)
        e = x2 * r * wv - y_ref[...]
        lrow = jnp.mean(e * e, axis=-1, keepdims=True)
        lpart = jnp.broadcast_to(0.5 * jnp.sum(lrow, axis=0, keepdims=True), (1, 128))
        dy = e * (1.0 / d)
        dwp = jnp.sum(dy * x2 * r, axis=0, keepdims=True)
        dyw = dy * wv
        dx_ref[...] = r * dyw - x2 * (r * r * r) * jnp.mean(dyw * x2, axis=-1, keepdims=True)

        @pl.when(i == 0)
        def _():
            dw_ref[...] = dwp
            l_ref[...] = lpart

        @pl.when(i > 0)
        def _():
            dw_ref[...] += dwp
            l_ref[...] += lpart

    row = pl.BlockSpec((ROW_TILE, d), lambda i: (i, 0))
    vec = pl.BlockSpec((1, d), lambda i: (0, 0))
    return pl.pallas_call(
        body, name="final", grid=(s // ROW_TILE,), in_specs=[row, row, vec, row],
        out_specs=[row, vec, pl.BlockSpec((1, 128), lambda i: (0, 0))],
        out_shape=[jax.ShapeDtypeStruct((s, d), F32), jax.ShapeDtypeStruct((1, d), F32), jax.ShapeDtypeStruct((1, 128), F32)],
        compiler_params=_cparams("arbitrary"))(x, t, fw, tgt)


def _adamw(w, g, m, v, name):
    r, c = w.shape
    cap = max(8, (1 << 18) // c)
    divisors = [t for t in range(8, min(r, cap) + 1, 8) if r % t == 0]
    tr = r if r <= 8 else (max(divisors) if divisors else cap)

    def body(w_ref, g_ref, m_ref, v_ref, d_ref, nm_ref, nv_ref):
        gv = g_ref[...]
        mn = ADAM_B1 * m_ref[...] + (1.0 - ADAM_B1) * gv
        vn = ADAM_B2 * v_ref[...] + (1.0 - ADAM_B2) * (gv * gv)
        m_hat = mn / (1.0 - ADAM_B1 ** ADAM_STEP)
        v_hat = vn / (1.0 - ADAM_B2 ** ADAM_STEP)
        d_ref[...] = -ADAM_LR * (m_hat / (jnp.sqrt(v_hat) + ADAM_EPS) + ADAM_WD * w_ref[...])
        nm_ref[...] = mn
        nv_ref[...] = vn

    blk = pl.BlockSpec((tr, c), lambda i: (i, 0))
    return pl.pallas_call(
        body, name=name, grid=(pl.cdiv(r, tr),), in_specs=[blk] * 4, out_specs=[blk] * 3,
        out_shape=[jax.ShapeDtypeStruct((r, c), F32)] * 3, compiler_params=_cparams("parallel"))(w, g, m, v)


HBM_SPEC = pl.BlockSpec(memory_space=pl.ANY)


def _place():
    x, y, c = lax.axis_index("x"), lax.axis_index("y"), lax.axis_index("c")
    chips = [(1 - x, y), (x, 1 - y), (1 - x, 1 - y)]
    return x, y, c, chips


def _ag_weights(packs):
    na = len(packs)
    nsem = 7

    def body(*refs):
        p_refs, out_refs = refs[:na], refs[na:2 * na]
        send_sems, recv_sems = refs[2 * na:]
        x, y, c, _ = _place()
        me, sib, j = (x, y, c), (x, y, 1 - c), 2 * x + y
        xn, yn = (1 - x, y, c), (x, 1 - y, c)
        jx, jy, jd = 2 * (1 - x) + y, 2 * x + (1 - y), 2 * (1 - x) + (1 - y)

        def rc(a, k, src, dst, to):
            return pltpu.make_async_remote_copy(src_ref=src, dst_ref=dst, send_sem=send_sems.at[nsem * a + k],
                                                recv_sem=recv_sems.at[nsem * a + k], device_id=to, device_id_type=MESH)

        sent = []
        for a in range(na):
            mine, land = p_refs[a].at[c], out_refs[a].at[j, c]
            sent += [rc(a, 0, mine, land, xn), rc(a, 1, mine, land, yn)]
        for cp in sent:
            cp.start()
        for a in range(na):
            half = p_refs[a].shape[1] // 2
            top, bottom = pl.ds(0, half), pl.ds(half, half)
            from_x, from_y, from_d = out_refs[a].at[jx, c], out_refs[a].at[jy, c], out_refs[a].at[jd, c]
            rc(a, 0, p_refs[a].at[c], from_x, me).wait_recv()
            later = [rc(a, 2, from_x.at[top], from_x.at[top], yn), rc(a, 4, from_x, from_x, sib)]
            for cp in later:
                cp.start()
            sent += later
            rc(a, 1, p_refs[a].at[c], from_y, me).wait_recv()
            later = [rc(a, 3, from_y.at[bottom], from_y.at[bottom], xn), rc(a, 5, from_y, from_y, sib)]
            for cp in later:
                cp.start()
            sent += later
            rc(a, 2, from_d.at[top], from_d.at[top], me).wait_recv()
            rc(a, 3, from_d.at[bottom], from_d.at[bottom], me).wait_recv()
            cp = rc(a, 6, from_d, from_d, sib)
            cp.start()
            sent.append(cp)
        for a in range(na):
            for k, jj in ((4, jx), (5, jy), (6, jd)):
                rc(a, k, p_refs[a].at[c], out_refs[a].at[jj, 1 - c], me).wait_recv()
        for cp in sent:
            cp.wait_send()

    return pl.pallas_call(
        body, name="ag_weights",
        out_shape=[jax.ShapeDtypeStruct((N_CHIPS,) + p.shape, p.dtype) for p in packs],
        in_specs=[HBM_SPEC] * na, out_specs=[HBM_SPEC] * na,
        scratch_shapes=[pltpu.SemaphoreType.DMA((nsem * na,)), pltpu.SemaphoreType.DMA((nsem * na,))])(*packs)


def _rs_pair(dwpt, gpack):
    n = N_CHIPS
    hw = SHARD_PAD // 2

    def body(d_ref, g_ref, out_d, out_g, send_sems, recv_sems):
        x, y, c, _ = _place()
        sib = (x, y, 1 - c)
        cps = []
        for p in range(n):
            start = pl.multiple_of(WIN_BASE[p] + (1 - c) * hw, TILE_ROWS)
            cps.append(pltpu.make_async_remote_copy(
                src_ref=d_ref.at[pl.ds(start, hw)], dst_ref=out_d.at[p], send_sem=send_sems.at[p],
                recv_sem=recv_sems.at[p], device_id=sib, device_id_type=MESH))
            cps.append(pltpu.make_async_remote_copy(
                src_ref=g_ref.at[p, 1 - c], dst_ref=out_g.at[p], send_sem=send_sems.at[n + p],
                recv_sem=recv_sems.at[n + p], device_id=sib, device_id_type=MESH))
        for cp in cps:
            cp.start()
        for cp in cps:
            cp.wait_recv()
        for cp in cps:
            cp.wait_send()

    return pl.pallas_call(
        body, name="rs_pair",
        out_shape=[jax.ShapeDtypeStruct((n, hw, dwpt.shape[1]), dwpt.dtype),
                   jax.ShapeDtypeStruct((n,) + gpack.shape[2:], gpack.dtype)],
        in_specs=[HBM_SPEC] * 2, out_specs=[HBM_SPEC] * 2,
        scratch_shapes=[pltpu.SemaphoreType.DMA((2 * n,)), pltpu.SemaphoreType.DMA((2 * n,))])(dwpt, gpack)


def _add_halves_win(dwpt, other, c):
    n, rh, wd = other.shape
    tr = _row_tile(rh)

    def body(s_ref, d_ref, o_ref, out_ref):
        out_ref[0] = (d_ref[...] + o_ref[0]).astype(BF16)

    scal = jnp.concatenate([jnp.reshape(c, (1,)).astype(jnp.int32), jnp.asarray(WIN_BASE, jnp.int32)])
    grid_spec = pltpu.PrefetchScalarGridSpec(
        num_scalar_prefetch=1, grid=(n, rh // tr),
        in_specs=[pl.BlockSpec((pl.Element(tr), pl.Element(wd)),
                               lambda p, i, sr: (pl.multiple_of(sr[1 + p] + sr[0] * rh + i * tr, TILE_ROWS), 0)),
                  pl.BlockSpec((1, tr, wd), lambda p, i, sr: (p, i, 0))],
        out_specs=pl.BlockSpec((1, tr, wd), lambda p, i, sr: (p, i, 0)))
    return pl.pallas_call(
        body, name="add_halves_in", grid_spec=grid_spec, out_shape=jax.ShapeDtypeStruct((n, rh, wd), BF16),
        compiler_params=_cparams("parallel", "parallel"))(scal, dwpt, other)


SEM_SPEC = pl.BlockSpec(memory_space=pltpu.SEMAPHORE)
DATAFLOW_EFFECT = pltpu.SideEffectType.DATAFLOW_SIDE_EFFECTING


def _rs_chips_start(csums):
    na = len(csums)

    def body(*refs):
        s_refs, land_refs = refs[:na], refs[na:2 * na]
        send_sems, recv_sems = refs[2 * na], refs[2 * na + 1]
        token = refs[-1]
        x, y, c, chips = _place()
        j = 2 * x + y
        for a in range(na):
            for k, (cx, cy) in enumerate(chips):
                pltpu.make_async_remote_copy(src_ref=s_refs[a].at[2 * cx + cy], dst_ref=land_refs[a].at[j],
                                             send_sem=send_sems.at[3 * a + k], recv_sem=recv_sems.at[3 * a + k],
                                             device_id=(cx, cy, c), device_id_type=MESH).start()
        token[...] = jnp.zeros_like(token)

    hbm = [pltpu.HBM(s.shape, s.dtype) for s in csums]
    args = [pltpu.with_memory_space_constraint(s, pltpu.HBM) for s in csums]
    args += [pltpu.with_memory_space_constraint(lax.empty(s.shape, s.dtype), pltpu.HBM) for s in csums]
    res = pl.pallas_call(
        body, name="rs_chips_start",
        out_shape=(pltpu.SemaphoreType.DMA((3 * na,)), pltpu.SemaphoreType.DMA((3 * na,)), *hbm, *hbm,
                   jax.ShapeDtypeStruct((8, 128), F32)),
        in_specs=[pl.BlockSpec(memory_space=pltpu.HBM)] * (2 * na),
        out_specs=(SEM_SPEC, SEM_SPEC, *[pl.BlockSpec(memory_space=pltpu.HBM)] * (2 * na),
                   pl.BlockSpec(memory_space=pltpu.VMEM)),
        input_output_aliases={i: 2 + i for i in range(2 * na)},
        compiler_params=pltpu.CompilerParams(has_side_effects=DATAFLOW_EFFECT))(*args)
    return res[0], res[1], list(res[2:2 + na]), list(res[2 + na:2 + 2 * na]), res[-1]


def _rs_chips_wait(send_sems, recv_sems, csums, lands, after):
    na = len(csums)

    def body(*refs):
        s_refs, land_refs = refs[:na], refs[na:2 * na]
        send_sems, recv_sems = refs[2 * na], refs[2 * na + 1]
        x, y, c, chips = _place()
        j = 2 * x + y
        for a in range(na):
            for k, (cx, cy) in enumerate(chips):
                cp = pltpu.make_async_remote_copy(src_ref=s_refs[a].at[2 * cx + cy], dst_ref=land_refs[a].at[2 * cx + cy],
                                                  send_sem=send_sems.at[3 * a + k], recv_sem=recv_sems.at[3 * a + k],
                                                  device_id=(cx, cy, c), device_id_type=MESH)
                cp.wait_send()
                cp.wait_recv()

    hbm = [pltpu.HBM(s.shape, s.dtype) for s in csums]
    res = pl.pallas_call(
        body, name="rs_chips_wait", out_shape=(*hbm, *hbm),
        in_specs=[pl.BlockSpec(memory_space=pltpu.HBM)] * (2 * na) + [SEM_SPEC, SEM_SPEC, pl.BlockSpec(memory_space=pl.ANY)],
        out_specs=tuple([pl.BlockSpec(memory_space=pltpu.HBM)] * (2 * na)),
        input_output_aliases={i: i for i in range(2 * na)},
        compiler_params=pltpu.CompilerParams(has_side_effects=DATAFLOW_EFFECT))(*csums, *lands, send_sems, recv_sems, after)
    return list(res[:na]), list(res[na:])


SWAP_CHUNKS = 4


def _pair_swap(halves):
    na = len(halves)

    def body(*refs):
        h_refs, out_refs = refs[:na], refs[na:2 * na]
        send_sems, recv_sems = refs[2 * na:]
        x, y, c, _ = _place()
        cps = []
        for a in range(na):
            rows = h_refs[a].shape[0] // SWAP_CHUNKS
            assert rows * SWAP_CHUNKS == h_refs[a].shape[0]
            for q in range(SWAP_CHUNKS):
                k = SWAP_CHUNKS * a + q
                cps.append(pltpu.make_async_remote_copy(
                    src_ref=h_refs[a].at[pl.ds(q * rows, rows)], dst_ref=out_refs[a].at[pl.ds(q * rows, rows)],
                    send_sem=send_sems.at[k], recv_sem=recv_sems.at[k], device_id=(x, y, 1 - c), device_id_type=MESH))
        for cp in cps:
            cp.start()
        for cp in cps:
            cp.wait_recv()
        for cp in cps:
            cp.wait_send()

    return pl.pallas_call(
        body, name="pair_swap", out_shape=[jax.ShapeDtypeStruct(h.shape, h.dtype) for h in halves],
        in_specs=[HBM_SPEC] * na, out_specs=[HBM_SPEC] * na,
        scratch_shapes=[pltpu.SemaphoreType.DMA((SWAP_CHUNKS * na,)), pltpu.SemaphoreType.DMA((SWAP_CHUNKS * na,))])(*halves)


def _ag_small(v):
    m_per, n = v.shape

    def body(x_ref, out_ref, send_sems, recv_sems, local_sem):
        x, y, c, chips = _place()
        me, sibling = (x, y, c), (x, y, 1 - c)

        def rows(px, py, pc):
            return out_ref.at[pl.ds((4 * px + 2 * py + pc) * m_per, m_per), :]

        def copy(k, block, to, src=None):
            return pltpu.make_async_remote_copy(
                src_ref=rows(*block) if src is None else src, dst_ref=rows(*block), send_sem=send_sems.at[k],
                recv_sem=recv_sems.at[k], device_id=to, device_id_type=MESH)

        mine = pltpu.make_async_copy(x_ref, rows(*me), local_sem)
        mine.start()
        first = [copy(0, me, sibling, src=x_ref)]
        first += [copy(1 + k, me, (*chip, c), src=x_ref) for k, chip in enumerate(chips)]
        for cp in first:
            cp.start()
        passed = [copy(4 + k, (*chip, c), sibling) for k, chip in enumerate(chips)]
        for k, chip in enumerate(chips):
            copy(1 + k, (*chip, c), me).wait_recv()
            passed[k].start()
        copy(0, sibling, me).wait_recv()
        for k, chip in enumerate(chips):
            copy(4 + k, (*chip, 1 - c), me).wait_recv()
        for cp in first + passed:
            cp.wait_send()
        mine.wait()

    return pl.pallas_call(
        body, name="ag_small", out_shape=jax.ShapeDtypeStruct((8 * m_per, n), v.dtype),
        in_specs=[pl.BlockSpec(memory_space=pltpu.VMEM)], out_specs=pl.BlockSpec(memory_space=pltpu.VMEM),
        scratch_shapes=[pltpu.SemaphoreType.DMA((7,)), pltpu.SemaphoreType.DMA((7,)), pltpu.SemaphoreType.DMA])(v)


def _sum_blocks(a, nblk, name):
    rows, wd = a.shape
    r = rows // nblk
    tr = min(r, ROW_TILE)
    assert r % tr == 0

    def body(*refs):
        acc = refs[0][...].astype(F32)
        for ref in refs[1:nblk]:
            acc = acc + ref[...].astype(F32)
        refs[nblk][...] = acc

    nt = r // tr
    return pl.pallas_call(
        body, name=name, grid=(nt,),
        in_specs=[pl.BlockSpec((tr, wd), functools.partial(lambda i, b: (b * nt + i, 0), b=b)) for b in range(nblk)],
        out_specs=pl.BlockSpec((tr, wd), lambda i: (i, 0)),
        out_shape=jax.ShapeDtypeStruct((r, wd), F32), compiler_params=_cparams("parallel"))(*([a] * nblk))


def _row_tile(rows):
    best = max(t for t in range(16, 513, 16) if rows % t == 0)
    return best


def _sum_chips(by_src, csum, j, name):
    n, rh, wd = by_src.shape
    tr = _row_tile(rh)

    def body(j_ref, *refs):
        own = refs[n][0].astype(F32)
        acc = None
        for k in range(n):
            term = jnp.where(j_ref[0] == k, own, refs[k][0].astype(F32))
            acc = term if acc is None else acc + term
        refs[n + 1][...] = acc

    def other(k):
        return pl.BlockSpec((1, tr, wd), lambda i, jr: (jnp.where(jr[0] == k, (k + 1) % n, k), i, 0))

    grid_spec = pltpu.PrefetchScalarGridSpec(
        num_scalar_prefetch=1, grid=(rh // tr,),
        in_specs=[other(k) for k in range(n)] + [pl.BlockSpec((1, tr, wd), lambda i, jr: (jr[0], i, 0))],
        out_specs=pl.BlockSpec((tr, wd), lambda i, jr: (i, 0)))
    return pl.pallas_call(
        body, name=name, grid_spec=grid_spec, out_shape=jax.ShapeDtypeStruct((rh, wd), F32),
        compiler_params=_cparams("parallel"))(jnp.reshape(j, (1,)).astype(jnp.int32), *([by_src] * n), csum)


def _add_halves(gpack, other, c, name):
    n, _, rh, wd = gpack.shape
    tr = _row_tile(rh)

    def body(c_ref, g_ref, o_ref, out_ref):
        out_ref[0] = (g_ref[0, 0] + o_ref[0]).astype(BF16)

    grid_spec = pltpu.PrefetchScalarGridSpec(
        num_scalar_prefetch=1, grid=(n, rh // tr),
        in_specs=[pl.BlockSpec((1, 1, tr, wd), lambda p, i, cr: (p, cr[0], i, 0)),
                  pl.BlockSpec((1, tr, wd), lambda p, i, cr: (p, i, 0))],
        out_specs=pl.BlockSpec((1, tr, wd), lambda p, i, cr: (p, i, 0)))
    return pl.pallas_call(
        body, name=name, grid_spec=grid_spec, out_shape=jax.ShapeDtypeStruct((n, rh, wd), BF16),
        compiler_params=_cparams("parallel", "parallel"))(jnp.reshape(c, (1,)).astype(jnp.int32), gpack, other)


PACK_W = 1024
ROWS_O_DN = DN_W // N_CHIPS
ROWS_O_DIL = DIL_W * (D_MODEL // N_CHIPS) // PACK_W
ROWS_OUT = D_MODEL // N_CHIPS
ROWS_CONV = 4 * (3 * DN_W // N_CHIPS) // PACK_W
R1 = ROWS_O_DN
R2 = R1 + ROWS_O_DIL
R3 = R2 + ROWS_OUT
R4 = R3 + 16
R5 = R4 + 16
PACK_ROWS = 704
HALF_ROWS = PACK_ROWS // 2
SHARD_PAD = 2880


R6 = R5 + 2 * DN_HEADS

TILE_ROWS = 16
BA_IN_SHARD1 = REF_OFF_BA - SHARD_W
LOCAL_START = (0, SHARD_W, 2 * SHARD_W - 2 * DN_HEADS, 3 * SHARD_W - 2 * DN_HEADS)
LOCAL_END = LOCAL_START[1:] + (OFF_BA,)
WIN_BASE = tuple(s // TILE_ROWS * TILE_ROWS for s in LOCAL_START)


def _to_window(k, shard):
    nba = 2 * DN_HEADS
    body = shard
    if k == 1:
        row = lax.broadcasted_iota(jnp.int32, (SHARD_W - nba, 1), 0)
        body = jnp.where(row < BA_IN_SHARD1, shard[:SHARD_W - nba], shard[nba:])
    lead = LOCAL_START[k] - WIN_BASE[k]
    return jnp.pad(body, ((lead, SHARD_PAD - lead - body.shape[0]), (0, 0)))


def _from_window(k, win, ba):
    nba = 2 * DN_HEADS
    lead = LOCAL_START[k] - WIN_BASE[k]
    if k != 1:
        return win[lead:lead + SHARD_W]
    row = lax.broadcasted_iota(jnp.int32, (SHARD_W, 1), 0)
    before = win[lead:lead + SHARD_W]
    after = jnp.pad(win, ((nba, 0), (0, 0)))[lead:lead + SHARD_W]
    mid = jnp.pad(ba, ((BA_IN_SHARD1, SHARD_W - BA_IN_SHARD1 - nba), (0, 0)))
    return jnp.where(row < BA_IN_SHARD1, before, jnp.where(row < BA_IN_SHARD1 + nba, mid, after))


def _stack_windows(wins, ba):
    pieces = []
    for k in range(N_CHIPS):
        lo = WIN_BASE[k] + (TILE_ROWS if k else 0)
        hi = LOCAL_END[k] // TILE_ROWS * TILE_ROWS
        pieces.append(wins[k][lo - WIN_BASE[k]:hi - WIN_BASE[k]])
        if k + 1 < N_CHIPS:
            assert hi == WIN_BASE[k + 1]
            pieces.append(wins[k][hi - WIN_BASE[k]:hi - WIN_BASE[k] + TILE_ROWS] + wins[k + 1][:TILE_ROWS])
    pieces += [ba, jnp.zeros((PW - OFF_BA - ba.shape[0], ba.shape[1]), ba.dtype)]
    out = jnp.concatenate(pieces, axis=0)
    assert out.shape[0] == PW
    return out


def _to_ref_layout(wpt):
    return jnp.concatenate([wpt[:REF_OFF_BA], wpt[OFF_BA:OFF_BA + 2 * DN_HEADS], wpt[REF_OFF_BA:OFF_BA]], axis=0)


def _from_ref_layout(wt):
    pad = jnp.zeros((PW - PROJ_W, wt.shape[1]), wt.dtype)
    return jnp.concatenate([wt[:REF_OFF_BA], wt[REF_OFF_BA + 2 * DN_HEADS:], wt[REF_OFF_BA:REF_OFF_BA + 2 * DN_HEADS], pad],
                           axis=0)


def _local_step(x, tgt, norm_w, wpt, conv_full, a_log, dt_bias, dn_norm_w, w_o_dn, w_o_dil, w_out, final_norm_w):
    s = x.shape[0]
    h, h_t = _rms_in(x, norm_w)
    proj = _matmul(h, wpt, F32, 2048, 1280, 1024, "proj", nt=True)
    c_pre, qkv = _conv_fwd(proj, conv_full)
    gate_par = jnp.zeros((8, 128), F32).at[0, 8:16].set(a_log[0]).at[1, 8:16].set(dt_bias[0])
    bg = _gates_fwd(proj, gate_par)
    o_a, u, w, vn, tmat, states = _gdr_fwd(qkv, bg)
    oa2, oa2_t = _gdr_out(o_a, proj, dn_norm_w)
    ya = _matmul(oa2, w_o_dn, F32, 512, 1024, 1024, "ya")
    parts = [_att_fwd(proj, g) for g in range(N_DIL)]
    ob, o_att, lse, ob_t = _att_merge(parts, proj)
    yb = _matmul(ob, w_o_dil, F32, 512, 1024, 512, "yb")
    mg, mg_t = _merge(proj, ya, yb)
    t = _matmul(mg, w_out, F32, 512, 1024, 1024, "t_out")
    dx2, dfw, lpart = _final(x, t, final_norm_w, tgt)

    dmg = _matmul(dx2, w_out, F32, 512, 1024, 1024, "d_merged", nt=True)
    dw_out = _matmul(mg_t, dx2, F32, 1024, 1024, 1024, "dw_out")
    dya, dyb, dga, dgb = _merge_bwd(proj, ya, yb, dmg)
    doa2 = _matmul(dya, w_o_dn, F32, 512, 1024, 1024, "d_oa2", nt=True)
    dw_o_dn = _matmul(oa2_t, dya, F32, 1024, 1024, 1024, "dw_o_dn")
    dob = _matmul(dyb, w_o_dil, F32, 512, 512, 1024, "d_ob", nt=True)
    dw_o_dil = _matmul(ob_t, dyb, F32, 512, 1024, 1024, "dw_o_dil")
    do_a, dz_a, ddnw = _gdr_out_bwd(o_a, proj, dn_norm_w, doa2)
    dq_a, dk_a, dv_a, dbg = _gdr_bwd(qkv, bg, u, w, vn, tmat, states, do_a)
    dba, dpar = _gates_bwd(proj, gate_par, dbg)
    dc = _conv_bwd_act(c_pre, dq_a, dk_a, dv_a)
    du_a, dconv = _conv_bwd(proj, dc, conv_full)
    do_att, delta, dz_b = _att_merge_bwd(o_att, proj, dob)
    dqkv_b = [_att_bwd(proj, g, do_att, lse, delta) for g in range(N_DIL)]
    dproj = jnp.concatenate(
        [du_a, dz_a] + [dqkv_b[g][i] for i in range(3) for g in range(N_DIL)]
        + [dz_b, dga, dgb, dba, jnp.zeros((s, PW - OFF_BA - 128), BF16)], axis=1)
    dwpt = _matmul(h_t, dproj, F32, 1024, 1280, 1024, "dw_in", transpose_out=True)

    def finish(after=None):
        dh = _matmul(dproj, wpt, F32, 1024, 1024, 2304, "d_h", after=after)
        grad_x, dnw = _rms_in_bwd(x, norm_w, dh, dx2)
        small = jnp.zeros((8, PACK_W), F32)
        small = small.at[0].set(dnw[0]).at[1].set(dfw[0]).at[2, :DN_D].set(ddnw[0])
        small = small.at[3, :DN_HEADS].set(dpar[0, 8:16]).at[3, DN_HEADS:2 * DN_HEADS].set(dpar[1, 8:16])
        small = small.at[4, 0].set(lpart[0, 0])
        return grad_x, small

    return finish, dwpt, dconv, dw_o_dn, dw_o_dil, dw_out


def kernel(x, norm_w, w_in, conv_w, a_log, dt_bias, dn_norm_w, w_o_dn, w_o_dil, w_out, final_norm_w, loss_target, m_norm_w, m_w_in, m_conv_w, m_a_log, m_dt_bias, m_dn_norm_w, m_w_o_dn, m_w_o_dil, m_w_out, m_final_norm_w, v_norm_w, v_w_in, v_conv_w, v_a_log, v_dt_bias, v_dn_norm_w, v_w_o_dn, v_w_o_dil, v_w_out, v_final_norm_w):
    c = lax.axis_index("c")
    j = 2 * lax.axis_index("x") + lax.axis_index("y")
    qw = D_MODEL // N_CHIPS

    cw = conv_w[0].reshape(ROWS_CONV, PACK_W)
    cw = jnp.pad(cw, ((0, 16 - ROWS_CONV), (0, 0)))
    cw_hi = cw.astype(BF16)
    cw_lo = (cw - cw_hi.astype(F32)).astype(BF16)
    shard = w_in[0].T.astype(BF16)
    own_ba = jnp.where(j == 1, shard[BA_IN_SHARD1:BA_IN_SHARD1 + 2 * DN_HEADS], jnp.zeros((2 * DN_HEADS, D_MODEL), BF16))
    pack = jnp.concatenate(
        [w_o_dn[0].astype(BF16), w_o_dil[0].astype(BF16).reshape(ROWS_O_DIL, PACK_W), w_out[0].astype(BF16), cw_hi, cw_lo,
         own_ba, jnp.zeros((PACK_ROWS - R6, PACK_W), BF16)], axis=0).reshape(2, HALF_ROWS, PACK_W)
    chips = range(N_CHIPS)
    own_win = lax.switch(j, [functools.partial(_to_window, k) for k in chips], shard).reshape(2, SHARD_PAD // 2, D_MODEL)
    all_in, allw = _ag_weights([own_win, pack])
    wins = [jnp.where(j == k, own_win, all_in[k]).reshape(SHARD_PAD, D_MODEL) for k in chips]
    allw = [jnp.where(j == k, pack, allw[k]).reshape(PACK_ROWS, PACK_W) for k in chips]
    wpt = _stack_windows(wins, allw[1][R5:R6])
    w_o_dn_full = jnp.concatenate([allw[k][:R1] for k in chips], axis=0)
    w_o_dil_full = jnp.concatenate([allw[k][R1:R2].reshape(DIL_W, qw) for k in chips], axis=1)
    w_out_full = jnp.concatenate([allw[k][R2:R3] for k in chips], axis=0)
    conv_full = jnp.concatenate(
        [(allw[k][R3:R3 + ROWS_CONV].astype(F32) + allw[k][R4:R4 + ROWS_CONV].astype(F32)).reshape(4, 3 * DN_W // N_CHIPS)
         for k in chips], axis=1)

    finish, dwpt, dconv, dw_o_dn, dw_o_dil, dw_out = _local_step(
        x[0], loss_target[0], norm_w, wpt, conv_full, a_log, dt_bias, dn_norm_w, w_o_dn_full, w_o_dil_full, w_out_full,
        final_norm_w.reshape(1, D_MODEL))

    cq = 3 * DN_W // N_CHIPS
    gpack = jnp.stack([
        jnp.concatenate(
            [dw_o_dn[k * qw:(k + 1) * qw], dw_o_dil[:, k * qw:(k + 1) * qw].reshape(ROWS_O_DIL, PACK_W),
             dw_out[k * qw:(k + 1) * qw],
             jnp.pad(dconv[:, k * cq:(k + 1) * cq].reshape(ROWS_CONV, PACK_W), ((0, 16 - ROWS_CONV), (0, 0))),
             dwpt[OFF_BA:OFF_BA + 2 * DN_HEADS] if k == 1 else jnp.zeros((2 * DN_HEADS, PACK_W), F32),
             jnp.zeros((PACK_ROWS - R4 - 2 * DN_HEADS, PACK_W), F32)], axis=0)
        for k in chips]).reshape(N_CHIPS, 2, HALF_ROWS, PACK_W)
    sib_in, sib_pack = _rs_pair(dwpt, gpack)
    csum_in = _add_halves_win(dwpt, sib_in, c)
    csum_pack = _add_halves(gpack, sib_pack, c, "add_halves_pack")
    send_sems, recv_sems, csums, lands, token = _rs_chips_start([csum_in, csum_pack])
    grad_x, small = finish(after=token)
    (csum_in, csum_pack), (src_in, src_pack) = _rs_chips_wait(send_sems, recv_sems, csums, lands, grad_x)
    half_in = _sum_chips(src_in, csum_in, j, "sum_chips_in")
    half_pack = _sum_chips(src_pack, csum_pack, j, "sum_chips_pack")
    sib_half_in, sib_half_pack = _pair_swap([half_in, half_pack])

    def both_halves(mine, theirs):
        return jnp.where(c == 0, jnp.concatenate([mine, theirs], axis=0), jnp.concatenate([theirs, mine], axis=0))

    g = both_halves(half_pack, sib_half_pack)
    g_w_in = lax.switch(j, [functools.partial(_from_window, k) for k in chips], both_halves(half_in, sib_half_in),
                        g[R4:R4 + 2 * DN_HEADS])
    g_w_o_dn = g[:R1]
    g_w_o_dil = g[R1:R2].reshape(DIL_W, qw)
    g_w_out = g[R2:R3]
    g_conv = g[R3:R3 + ROWS_CONV].reshape(4, cq)

    gs = _sum_blocks(_ag_small(small), 8, "sum_small")
    loss = gs[4, 0]
    w_small = jnp.zeros((8, PACK_W), F32)

    def pack_small(nw, fw, dnw_, al, db):
        t = w_small.at[0].set(nw[0]).at[1].set(fw).at[2, :DN_D].set(dnw_[0])
        return t.at[3, :DN_HEADS].set(al[0]).at[3, DN_HEADS:2 * DN_HEADS].set(db[0])

    sm = _adamw(pack_small(norm_w, final_norm_w, dn_norm_w, a_log, dt_bias), gs,
                pack_small(m_norm_w, m_final_norm_w, m_dn_norm_w, m_a_log, m_dt_bias),
                pack_small(v_norm_w, v_final_norm_w, v_dn_norm_w, v_a_log, v_dt_bias), "adamw_small")

    def unpack_small(t):
        return dict(norm_w=t[0:1], final_norm_w=t[1], dn_norm_w=t[2:3, :DN_D], a_log=t[3:4, :DN_HEADS],
                    dt_bias=t[3:4, DN_HEADS:2 * DN_HEADS])

    res = {"grad": unpack_small(gs)}
    for kind, arr in zip(("delta", "new_m", "new_v"), sm):
        res[kind] = unpack_small(arr)
    big = dict(conv_w=(conv_w, g_conv, m_conv_w, v_conv_w), w_o_dn=(w_o_dn, g_w_o_dn, m_w_o_dn, v_w_o_dn),
               w_o_dil=(w_o_dil, g_w_o_dil, m_w_o_dil, v_w_o_dil), w_out=(w_out, g_w_out, m_w_out, v_w_out))
    for name, (wt, gt, mt, vt) in big.items():
        d, nm, nv = _adamw(wt[0], gt, mt[0], vt[0], "adamw_" + name)
        res["grad"][name] = gt[None]
        res["delta"][name], res["new_m"][name], res["new_v"][name] = d[None], nm[None], nv[None]

    d, nm, nv = _adamw(w_in[0].T, g_w_in, m_w_in[0].T, v_w_in[0].T, "adamw_w_in")
    res["grad"]["w_in"] = g_w_in.T[None]
    res["delta"]["w_in"], res["new_m"]["w_in"], res["new_v"]["w_in"] = d.T[None], nm.T[None], nv.T[None]
    order = ["norm_w", "w_in", "conv_w", "a_log", "dt_bias", "dn_norm_w", "w_o_dn", "w_o_dil", "w_out", "final_norm_w"]
    outs = [loss, grad_x[None]]
    for kind in ("grad", "delta", "new_m", "new_v"):
        outs += [res[kind][nm] for nm in order]
    return tuple(outs)
```

```python
import functools
import math

import jax
import jax.numpy as jnp
from jax import lax
from jax.experimental import pallas as pl
from jax.experimental.pallas import tpu as pltpu

F32 = jnp.float32
BF16 = jnp.bfloat16
MESH = pl.DeviceIdType.MESH

D_MODEL = 1024
DN_HEADS = 8
DN_D = 128
DN_CHUNK = 64
DN_W = DN_HEADS * DN_D
DIL_GROUPS = ((128, 1), (512, 4), (2048, 16))
N_DIL = len(DIL_GROUPS)
DIL_HEADS = 4
DIL_DH = 128
DIL_W = DIL_HEADS * DIL_DH
ATT_BLOCK = 128
NORM_EPS = 1e-6
PROJ_W = 11280
N_CHIPS = 4
SHARD_W = PROJ_W // N_CHIPS

OFF_QKV_A = 0
OFF_Z_A = 3072
OFF_Q_B = 4096
OFF_K_B = 5632
OFF_V_B = 7168
OFF_Z_B = 8704
OFF_G_A = 9216
OFF_G_B = 10240
OFF_BA = 11264
PW = 11520
REF_OFF_BA = 4096

ADAM_LR = 0.001
ADAM_B1 = 0.9
ADAM_B2 = 0.999
ADAM_EPS = 1e-08
ADAM_WD = 0.01
ADAM_STEP = 10

ROW_TILE = 256
NEG = -1e30


def _dot(a, b):
    return jnp.dot(a.astype(BF16), b.astype(BF16), preferred_element_type=F32)


def _dot_nt(a, b):
    return lax.dot_general(a.astype(BF16), b.astype(BF16), (((1,), (1,)), ((), ())), preferred_element_type=F32)


def _dot_tn(a, b):
    return lax.dot_general(a.astype(BF16), b.astype(BF16), (((0,), (0,)), ((), ())), preferred_element_type=F32)


def _split(a):
    hi = a.astype(BF16)
    lo = (a - hi.astype(F32)).astype(BF16)
    return hi, lo


def _dot_exact_lhs(c, a):
    hi, lo = _split(a)
    cb = c.astype(BF16)
    return jnp.dot(cb, hi, preferred_element_type=F32) + jnp.dot(cb, lo, preferred_element_type=F32)


def _dot_exact_rhs(a, c):
    hi, lo = _split(a)
    cb = c.astype(BF16)
    return jnp.dot(hi, cb, preferred_element_type=F32) + jnp.dot(lo, cb, preferred_element_type=F32)


def _dot_tn_exact_rhs(a, c):
    hi, lo = _split(a)
    cb = c.astype(BF16)
    dn = (((0,), (0,)), ((), ()))
    return (lax.dot_general(hi, cb, dn, preferred_element_type=F32)
            + lax.dot_general(lo, cb, dn, preferred_element_type=F32))


def _sigmoid(x):
    return 1.0 / (1.0 + jnp.exp(-x))


def _silu(x):
    return x * _sigmoid(x)


def _silu_grad(x):
    s = _sigmoid(x)
    return s * (1.0 + x * (1.0 - s))


def _softplus(x):
    return jnp.maximum(x, 0.0) + jnp.log(1.0 + jnp.exp(-jnp.abs(x)))


def _cparams(*sem):
    return pltpu.CompilerParams(dimension_semantics=sem)


def _matmul(a, b, out_dtype, tm, tn, tk, name, nt=False, transpose_out=False, after=None):
    m, kdim = a.shape
    n = b.shape[0] if nt else b.shape[1]
    tm, tn, tk = min(tm, m), min(tn, n), min(tk, kdim)
    assert m % tm == 0 and n % tn == 0 and kdim % tk == 0, (name, a.shape, b.shape, tm, tn, tk)
    nk = kdim // tk
    dot = _dot_nt if nt else _dot
    b_spec = (pl.BlockSpec((tn, tk), lambda i, j, k: (j, k)) if nt else pl.BlockSpec((tk, tn), lambda i, j, k: (k, j)))

    def emit(o_ref, acc):
        o_ref[...] = (acc.T if transpose_out else acc).astype(o_ref.dtype)

    if nk == 1:
        def body(a_ref, b_ref, *rest):
            emit(rest[-1], dot(a_ref[...], b_ref[...]))
        scratch = []
    else:
        def body(a_ref, b_ref, *rest):
            o_ref, acc_ref = rest[-2:]
            k = pl.program_id(2)
            p = dot(a_ref[...], b_ref[...])

            @pl.when(k == 0)
            def _():
                acc_ref[...] = p

            @pl.when(k > 0)
            def _():
                acc_ref[...] += p

            @pl.when(k == nk - 1)
            def _():
                emit(o_ref, acc_ref[...])
        scratch = [pltpu.VMEM((tm, tn), F32)]

    if transpose_out:
        out_spec, out_shape = pl.BlockSpec((tn, tm), lambda i, j, k: (j, i)), (n, m)
    else:
        out_spec, out_shape = pl.BlockSpec((tm, tn), lambda i, j, k: (i, j)), (m, n)
    extra = [] if after is None else [after]
    return pl.pallas_call(
        body, name=name, grid=(m // tm, n // tn, nk),
        in_specs=[pl.BlockSpec((tm, tk), lambda i, j, k: (i, k)), b_spec] + [pl.BlockSpec(memory_space=pl.ANY)] * len(extra),
        out_specs=out_spec, out_shape=jax.ShapeDtypeStruct(out_shape, out_dtype), scratch_shapes=scratch,
        compiler_params=_cparams("parallel", "parallel", "arbitrary"))(a, b, *extra)


def _rms_in(x, nw):
    s, d = x.shape

    def body(x_ref, w_ref, h_ref, ht_ref):
        xv = x_ref[...]
        r = lax.rsqrt(jnp.mean(xv * xv, axis=-1, keepdims=True) + NORM_EPS)
        h = xv * r * w_ref[...]
        h_ref[...] = h.astype(BF16)
        ht_ref[...] = h.T.astype(BF16)

    return pl.pallas_call(
        body, name="rms_in", grid=(s // ROW_TILE,),
        in_specs=[pl.BlockSpec((ROW_TILE, d), lambda i: (i, 0)), pl.BlockSpec((1, d), lambda i: (0, 0))],
        out_specs=[pl.BlockSpec((ROW_TILE, d), lambda i: (i, 0)), pl.BlockSpec((d, ROW_TILE), lambda i: (0, i))],
        out_shape=[jax.ShapeDtypeStruct((s, d), BF16), jax.ShapeDtypeStruct((d, s), BF16)],
        compiler_params=_cparams("parallel"))(x, nw)


def _rms_in_bwd(x, nw, dh, dx2):
    s, d = x.shape

    def body(x_ref, w_ref, dh_ref, dx2_ref, dx_ref, dw_ref):
        i = pl.program_id(0)
        xv = x_ref[...]
        r = lax.rsqrt(jnp.mean(xv * xv, axis=-1, keepdims=True) + NORM_EPS)
        dhv = dh_ref[...]
        dyw = dhv * w_ref[...]
        dx_ref[...] = dx2_ref[...] + r * dyw - xv * (r * r * r) * jnp.mean(dyw * xv, axis=-1, keepdims=True)
        part = jnp.sum(dhv * xv * r, axis=0, keepdims=True)

        @pl.when(i == 0)
        def _():
            dw_ref[...] = part

        @pl.when(i > 0)
        def _():
            dw_ref[...] += part

    row = pl.BlockSpec((ROW_TILE, d), lambda i: (i, 0))
    vec = pl.BlockSpec((1, d), lambda i: (0, 0))
    return pl.pallas_call(
        body, name="rms_in_bwd", grid=(s // ROW_TILE,), in_specs=[row, vec, row, row], out_specs=[row, vec],
        out_shape=[jax.ShapeDtypeStruct((s, d), F32), jax.ShapeDtypeStruct((1, d), F32)],
        compiler_params=_cparams("arbitrary"))(x, nw, dh, dx2)


def _shift_down(cur, prev8, k):
    rc = pltpu.roll(cur, k, 0)
    rp = pltpu.roll(prev8, k, 0)
    row = lax.broadcasted_iota(jnp.int32, prev8.shape, 0)
    top = jnp.where(row < k, rp, rc[:8])
    return jnp.concatenate([top, rc[8:]], axis=0)


def _shift_up(cur, next8, k):
    t = cur.shape[0]
    rc = pltpu.roll(cur, t - k, 0)
    rn = pltpu.roll(next8, 8 - k, 0)
    row = lax.broadcasted_iota(jnp.int32, next8.shape, 0)
    bot = jnp.where(row >= 8 - k, rn, rc[t - 8:])
    return jnp.concatenate([rc[:t - 8], bot], axis=0)


def _conv_fwd(proj, conv_w):
    s = proj.shape[0]
    t8 = ROW_TILE // 8

    def body(u_ref, up_ref, w_ref, c_ref, y_ref):
        i = pl.program_id(0)
        part = pl.program_id(1)
        cur = u_ref[...]
        prev8 = jnp.where(i > 0, up_ref[...], 0.0)
        w = w_ref[...]
        c = cur * w[3:4, :]
        for k in (1, 2, 3):
            c = c + _shift_down(cur, prev8, k) * w[3 - k:4 - k, :]
        c_ref[...] = c
        a = _silu(c)
        for h in range(DN_HEADS):
            ah = a[:, h * DN_D:(h + 1) * DN_D]
            r = lax.rsqrt(jnp.sum(ah * ah, axis=-1, keepdims=True) + NORM_EPS)
            y_ref[:, h * DN_D:(h + 1) * DN_D] = jnp.where(part < 2, ah * r, ah)

    return pl.pallas_call(
        body, name="conv_fwd", grid=(s // ROW_TILE, 3),
        in_specs=[pl.BlockSpec((ROW_TILE, DN_W), lambda i, p: (i, p)),
                  pl.BlockSpec((8, DN_W), lambda i, p: (jnp.maximum(i * t8 - 1, 0), p)),
                  pl.BlockSpec((4, DN_W), lambda i, p: (0, p))],
        out_specs=[pl.BlockSpec((ROW_TILE, DN_W), lambda i, p: (i, p))] * 2,
        out_shape=[jax.ShapeDtypeStruct((s, 3 * DN_W), F32)] * 2,
        compiler_params=_cparams("parallel", "parallel"))(proj, proj, conv_w)


def _conv_bwd_act(c, dq, dk, dv):
    s = c.shape[0]

    def body(c_ref, dq_ref, dk_ref, dv_ref, dc_ref):
        for part, d_ref in enumerate((dq_ref, dk_ref, dv_ref)):
            for h in range(DN_HEADS):
                sl = slice(part * DN_W + h * DN_D, part * DN_W + (h + 1) * DN_D)
                ch = c_ref[:, sl]
                dyh = d_ref[:, h * DN_D:(h + 1) * DN_D]
                if part < 2:
                    ah = _silu(ch)
                    r = lax.rsqrt(jnp.sum(ah * ah, axis=-1, keepdims=True) + NORM_EPS)
                    dyh = r * dyh - ah * (r * r * r) * jnp.sum(dyh * ah, axis=-1, keepdims=True)
                dc_ref[:, sl] = dyh * _silu_grad(ch)

    wide = pl.BlockSpec((ROW_TILE, 3 * DN_W), lambda i: (i, 0))
    row = pl.BlockSpec((ROW_TILE, DN_W), lambda i: (i, 0))
    return pl.pallas_call(
        body, name="conv_bwd_act", grid=(s // ROW_TILE,), in_specs=[wide, row, row, row], out_specs=wide,
        out_shape=jax.ShapeDtypeStruct((s, 3 * DN_W), F32), compiler_params=_cparams("parallel"))(c, dq, dk, dv)


def _conv_bwd(proj, dc, conv_w):
    s = proj.shape[0]
    t8 = ROW_TILE // 8
    nrow = s // ROW_TILE
    last8 = s // 8 - 1

    def body(u_ref, up_ref, dc_ref, dcn_ref, w_ref, du_ref, dw_ref):
        i = pl.program_id(1)
        cur = u_ref[...]
        prev8 = jnp.where(i > 0, up_ref[...], 0.0)
        dcv = dc_ref[...]
        next8 = jnp.where(i < nrow - 1, dcn_ref[...], 0.0)
        w = w_ref[...]
        du = dcv * w[3:4, :]
        for k in (1, 2, 3):
            du = du + _shift_up(dcv, next8, k) * w[3 - k:4 - k, :]
        du_ref[...] = du.astype(BF16)

        @pl.when(i == 0)
        def _():
            dw_ref[...] = jnp.zeros_like(dw_ref)

        dw_ref[3:4, :] += jnp.sum(cur * dcv, axis=0, keepdims=True)
        for k in (1, 2, 3):
            dw_ref[3 - k:4 - k, :] += jnp.sum(_shift_down(cur, prev8, k) * dcv, axis=0, keepdims=True)

    blk = pl.BlockSpec((ROW_TILE, DN_W), lambda p, i: (i, p))
    return pl.pallas_call(
        body, name="conv_bwd", grid=(3, nrow),
        in_specs=[blk, pl.BlockSpec((8, DN_W), lambda p, i: (jnp.maximum(i * t8 - 1, 0), p)),
                  blk, pl.BlockSpec((8, DN_W), lambda p, i: (jnp.minimum((i + 1) * t8, last8), p)),
                  pl.BlockSpec((4, DN_W), lambda p, i: (0, p))],
        out_specs=[blk, pl.BlockSpec((4, DN_W), lambda p, i: (0, p))],
        out_shape=[jax.ShapeDtypeStruct((s, 3 * DN_W), BF16), jax.ShapeDtypeStruct((4, 3 * DN_W), F32)],
        compiler_params=_cparams("parallel", "arbitrary"))(proj, proj, dc, dc, conv_w)


def _gates_fwd(proj, gate_par):
    s = proj.shape[0]

    def body(ba_ref, par_ref, o_ref):
        v = ba_ref[...]
        lane = lax.broadcasted_iota(jnp.int32, v.shape, 1)
        beta = _sigmoid(v)
        g = -jnp.exp(par_ref[0:1, :]) * _softplus(v + par_ref[1:2, :])
        o_ref[...] = jnp.where(lane < DN_HEADS, beta, jnp.where(lane < 2 * DN_HEADS, g, 0.0))

    return pl.pallas_call(
        body, name="gates_fwd", grid=(s // ROW_TILE,),
        in_specs=[pl.BlockSpec((ROW_TILE, 128), lambda i: (i, OFF_BA // 128)), pl.BlockSpec((8, 128), lambda i: (0, 0))],
        out_specs=pl.BlockSpec((ROW_TILE, 128), lambda i: (i, 0)),
        out_shape=jax.ShapeDtypeStruct((s, 128), F32), compiler_params=_cparams("parallel"))(proj, gate_par)


def _gates_bwd(proj, gate_par, dbg):
    s = proj.shape[0]

    def body(ba_ref, par_ref, d_ref, o_ref, dpar_ref):
        i = pl.program_id(0)
        v = ba_ref[...]
        dv = d_ref[...]
        lane = lax.broadcasted_iota(jnp.int32, v.shape, 1)
        beta = _sigmoid(v)
        nega = -jnp.exp(par_ref[0:1, :])
        xs = v + par_ref[1:2, :]
        dsp = dv * nega * _sigmoid(xs)
        dal = dv * nega * _softplus(xs)
        is_b = lane < DN_HEADS
        is_g = jnp.logical_and(lane >= DN_HEADS, lane < 2 * DN_HEADS)
        o_ref[...] = jnp.where(is_b, dv * beta * (1.0 - beta), jnp.where(is_g, dsp, 0.0)).astype(BF16)
        r0 = jnp.sum(jnp.where(is_g, dal, 0.0), axis=0, keepdims=True)
        r1 = jnp.sum(jnp.where(is_g, dsp, 0.0), axis=0, keepdims=True)

        @pl.when(i == 0)
        def _():
            dpar_ref[...] = jnp.zeros_like(dpar_ref)

        dpar_ref[0:1, :] += r0
        dpar_ref[1:2, :] += r1

    return pl.pallas_call(
        body, name="gates_bwd", grid=(s // ROW_TILE,),
        in_specs=[pl.BlockSpec((ROW_TILE, 128), lambda i: (i, OFF_BA // 128)), pl.BlockSpec((8, 128), lambda i: (0, 0)),
                  pl.BlockSpec((ROW_TILE, 128), lambda i: (i, 0))],
        out_specs=[pl.BlockSpec((ROW_TILE, 128), lambda i: (i, 0)), pl.BlockSpec((8, 128), lambda i: (0, 0))],
        out_shape=[jax.ShapeDtypeStruct((s, 128), BF16), jax.ShapeDtypeStruct((8, 128), F32)],
        compiler_params=_cparams("arbitrary"))(proj, gate_par, dbg)


def _chunk_masks():
    c = DN_CHUNK
    ii = lax.broadcasted_iota(jnp.int32, (c, c), 0)
    jj = lax.broadcasted_iota(jnp.int32, (c, c), 1)
    return dict(ii=ii, jj=jj, lower=(ii >= jj), strict=(ii > jj), eye=(ii == jj),
                lower_f=(ii >= jj).astype(BF16), upper_f=(ii <= jj).astype(BF16), ones8=jnp.ones((8, c), BF16))


class _Heads:
    def __init__(self, xs):
        self.xs = list(xs)

    def _bin(self, o, f):
        if isinstance(o, _Heads):
            return _Heads([f(a, b) for a, b in zip(self.xs, o.xs)])
        return _Heads([f(a, o) for a in self.xs])

    def __add__(self, o):
        return self._bin(o, lambda a, b: a + b)

    def __sub__(self, o):
        return self._bin(o, lambda a, b: a - b)

    def __mul__(self, o):
        return self._bin(o, lambda a, b: a * b)

    __radd__ = __add__
    __rmul__ = __mul__

    def __neg__(self):
        return _Heads([-a for a in self.xs])

    def __getitem__(self, i):
        return _Heads([a[i] for a in self.xs])


def _hmap(f, *args):
    n = next(len(a.xs) for a in args if isinstance(a, _Heads))
    return _Heads([f(*[(a.xs[h] if isinstance(a, _Heads) else a) for a in args]) for h in range(n)])


def _hdot(a, b):
    return _hmap(_dot, a, b)


def _hdot_nt(a, b):
    return _hmap(_dot_nt, a, b)


def _hdot_tn(a, b):
    return _hmap(_dot_tn, a, b)


def _hcat(a, b, axis):
    return _hmap(lambda x, y: jnp.concatenate([x, y], axis=axis), a, b)


def _hsum(a, axis):
    return _hmap(lambda t: jnp.sum(t, axis=axis, keepdims=True), a)


def _hwhere(c, a, b):
    return _hmap(jnp.where, c, a, b)


def _chunk_gates(mk, bg):
    c = DN_CHUNK
    gc_all = _dot_exact_lhs(mk["lower_f"], bg)
    rows = jnp.concatenate([gc_all, gc_all], axis=0).T
    hs = range(DN_HEADS)
    return (_Heads(bg[:, h:h + 1] for h in hs), _Heads(gc_all[:, DN_HEADS + h:DN_HEADS + h + 1] for h in hs),
            _Heads(rows[DN_HEADS + h:DN_HEADS + h + 1, :] for h in hs))


def _chunk_common(mk, q, k, beta_col, gc_col, gc_r):
    c = DN_CHUNK
    lower, strict = mk["lower"], mk["strict"]
    qs = q * (DN_D ** -0.5)
    beta_b = _hmap(lambda t: jnp.broadcast_to(t, (c, DN_D)), beta_col)
    gc_b = _hmap(lambda t: jnp.broadcast_to(t, (c, DN_D)), gc_col)
    gc_sq = gc_b[:, :c]
    gam = _hwhere(lower, _hmap(lambda t: jnp.exp(jnp.minimum(t, 0.0)), gc_sq - gc_r[:, :c]), 0.0)
    egc = _hmap(jnp.exp, gc_b)
    gl = gc_b[c - 1:c, :]
    ekd = _hmap(jnp.exp, gl - gc_b)
    dl = _hmap(jnp.exp, gl)
    kb = k * beta_b
    scores = _hdot_nt(_hcat(kb, qs, 0), k)
    a_strict = _hwhere(strict, scores[:c] * gam, 0.0)
    aqk = _hwhere(lower, scores[c:] * gam, 0.0)
    return dict(k=k, qs=qs, beta_b=beta_b, gc_b=gc_b, gam=gam, egc=egc, ekd=ekd, dl=dl, kb=kb, a_strict=a_strict, aqk=aqk)


def _unit_lower_inverse_minus_eye(n_strict, ii, jj):
    same = lax.shift_right_logical(ii, 4) == lax.shift_right_logical(jj, 4)
    dmat = _hwhere(same, n_strict, 0.0)
    omat = n_strict - dmat
    d2 = _hdot(dmat, dmat)
    d4 = _hdot(d2, d2)
    d8 = _hdot(d4, d4)
    x1 = d2 - dmat - _hdot(dmat, d2)
    x2 = x1 + d4 + _hdot(x1, d4)
    x3 = x2 + d8 + _hdot(x2, d8)
    n1 = omat + _hdot(x3, omat)
    n2 = _hdot(n1, n1)
    y = n2 - n1 - _hdot(n1, n2)
    return y + x3 + _hdot(y, x3)


def _gdr_fwd(qkv, bg):
    s = qkv.shape[0]
    c = DN_CHUNK
    n = s // c

    def body(q_ref, k_ref, v_ref, bg_ref, o_ref, u_ref, w_ref, vn_ref, tm_ref, st_ref, state):
        @pl.when(pl.program_id(0) == 0)
        def _():
            state[...] = jnp.zeros_like(state)

        mk = _chunk_masks()
        bg = bg_ref[...]
        hs = range(DN_HEADS)
        sls = [slice(h * DN_D, (h + 1) * DN_D) for h in hs]
        cm = _chunk_common(mk, _Heads(q_ref[:, sl] for sl in sls), _Heads(k_ref[:, sl] for sl in sls),
                           *_chunk_gates(mk, bg))
        tm = _unit_lower_inverse_minus_eye(cm["a_strict"], mk["ii"], mk["jj"])
        rhs_u = _Heads(v_ref[:, sl] for sl in sls) * cm["beta_b"]
        rhs_w = cm["kb"] * cm["egc"]
        t_rhs = _hdot(tm, _hcat(rhs_u, rhs_w, 1))
        u = rhs_u + t_rhs[:, :DN_D]
        w = rhs_w + t_rhs[:, DN_D:]
        st = _Heads(state[h] for h in hs)
        on_state = _hdot(_hcat(w, cm["qs"] * cm["egc"], 0), st)
        v_new = u - on_state[:c]
        o = on_state[c:] + _hdot(cm["aqk"], v_new)
        st_new = st * cm["dl"] + _hdot_tn(cm["k"] * cm["ekd"], v_new)
        for h, sl in zip(hs, sls):
            o_ref[:, sl] = o.xs[h]
            u_ref[:, sl] = u.xs[h]
            w_ref[:, sl] = w.xs[h]
            vn_ref[:, sl] = v_new.xs[h]
            tm_ref[h, 0] = tm.xs[h]
            st_ref[h, 0] = st.xs[h]
            state[h] = st_new.xs[h]

    def part(p):
        return pl.BlockSpec((c, DN_W), lambda j: (j, p))

    return pl.pallas_call(
        body, name="gdr_fwd", grid=(n,),
        in_specs=[part(0), part(1), part(2), pl.BlockSpec((c, 128), lambda j: (j, 0))],
        out_specs=[part(0)] * 4 + [pl.BlockSpec((DN_HEADS, 1, c, c), lambda j: (0, j, 0, 0)),
                                   pl.BlockSpec((DN_HEADS, 1, DN_D, DN_D), lambda j: (0, j, 0, 0))],
        out_shape=[jax.ShapeDtypeStruct((s, DN_W), F32)] * 4
        + [jax.ShapeDtypeStruct((DN_HEADS, n, c, c), F32), jax.ShapeDtypeStruct((DN_HEADS, n, DN_D, DN_D), F32)],
        scratch_shapes=[pltpu.VMEM((DN_HEADS, DN_D, DN_D), F32)],
        compiler_params=_cparams("arbitrary"))(qkv, qkv, qkv, bg)


def _gdr_bwd(qkv, bg, u, w, vn, tmat, states, do):
    s = qkv.shape[0]
    c = DN_CHUNK
    n = s // c

    def body(q_ref, k_ref, v_ref, bg_ref, u_ref, w_ref, vn_ref, tm_ref, st_ref, do_ref,
             dq_ref, dk_ref, dv_ref, dbg_ref, dstate):
        @pl.when(pl.program_id(0) == 0)
        def _():
            dstate[...] = jnp.zeros_like(dstate)

        mk = _chunk_masks()
        lower, strict = mk["lower"], mk["strict"]
        bg = bg_ref[...]
        ones = jnp.ones((c, DN_D), BF16)
        rowi = lax.broadcasted_iota(jnp.int32, (c, DN_D), 0)
        lane = lax.broadcasted_iota(jnp.int32, (c, 128), 1)
        hs = range(DN_HEADS)
        sls = [slice(h * DN_D, (h + 1) * DN_D) for h in hs]

        def heads_of(ref):
            return _Heads(ref[:, sl] for sl in sls)

        cm = _chunk_common(mk, heads_of(q_ref), heads_of(k_ref), *_chunk_gates(mk, bg))
        k, qs, beta_b = cm["k"], cm["qs"], cm["beta_b"]
        gam, egc, ekd, dl, kb = cm["gam"], cm["egc"], cm["ekd"], cm["dl"], cm["kb"]
        aqk, a_strict = cm["aqk"], cm["a_strict"]
        v, uu, ww, v_new, dov = heads_of(v_ref), heads_of(u_ref), heads_of(w_ref), heads_of(vn_ref), heads_of(do_ref)
        st = _Heads(st_ref[h, 0] for h in hs)
        dsn = _Heads(dstate[h] for h in hs)
        qd = qs * egc
        kd = k * ekd

        dv_new = _hdot_tn(aqk, dov) + _hdot(kd, dsn)
        do_sv = _hdot_nt(dov, _hcat(st, v_new, 0))
        dqd = do_sv[:, :DN_D]
        daqk = _hwhere(lower, do_sv[:, DN_D:], 0.0)
        dkd = _hdot_nt(v_new, dsn)
        ddl = _hsum(_hsum(dsn * st, 1), 0)
        dw = -_hdot_nt(dv_new, st)
        ds_new = dsn * dl + _hdot_tn(_hcat(qd, -ww, 0), _hcat(dov, dv_new, 0))

        tm = _Heads(tm_ref[h, 0] for h in hs)
        tt = _hdot_tn(tm, _hcat(dv_new, dw, 1))
        dru = dv_new + tt[:, :DN_D]
        drw = dw + tt[:, DN_D:]
        dn = _hwhere(strict, -_hdot_nt(_hcat(dru, drw, 1), _hcat(uu, ww, 1)), 0.0)
        dag = dn * gam
        dqg = daqk * gam
        both = _hcat(dag, dqg, 0)
        on_k = _hdot(both, k)
        dkb = on_k[:c] + drw * egc
        dqs = on_k[c:] + dqd * egc
        dk = _hdot_tn(both, _hcat(kb, qs, 0)) + dkb * beta_b + dkd * ekd
        pmat = dn * a_strict + daqk * aqk
        tkd = _hsum(dkd * kd, -1)
        dgc = (_hsum(pmat, -1) - _hmap(_dot_tn_exact_rhs, pmat, ones) + _hsum(drw * (kb * egc), -1)
               + _hsum(dqd * qd, -1) - tkd)
        last = _hsum(tkd, 0) + ddl * dl
        dgc = dgc + _hwhere(rowi == c - 1, last, 0.0)
        dbeta = _hsum(dru * v, -1) + _hsum(dkb * k, -1)
        dq = dqs * (DN_D ** -0.5)
        dv = dru * beta_b

        dgc_all = jnp.zeros((c, 128), F32)
        dbg = jnp.zeros((c, 128), F32)
        for h, sl in zip(hs, sls):
            dq_ref[:, sl] = dq.xs[h]
            dk_ref[:, sl] = dk.xs[h]
            dv_ref[:, sl] = dv.xs[h]
            dstate[h] = ds_new.xs[h]
            dgc_all = dgc_all + jnp.where(lane == DN_HEADS + h, dgc.xs[h], 0.0)
            dbg = dbg + jnp.where(lane == h, dbeta.xs[h], 0.0)
        dbg_ref[...] = dbg + _dot_exact_lhs(mk["upper_f"], dgc_all)

    def part(p):
        return pl.BlockSpec((c, DN_W), lambda j: (n - 1 - j, p))

    vec = pl.BlockSpec((c, 128), lambda j: (n - 1 - j, 0))
    return pl.pallas_call(
        body, name="gdr_bwd", grid=(n,),
        in_specs=[part(0), part(1), part(2), vec, part(0), part(0), part(0),
                  pl.BlockSpec((DN_HEADS, 1, c, c), lambda j: (0, n - 1 - j, 0, 0)),
                  pl.BlockSpec((DN_HEADS, 1, DN_D, DN_D), lambda j: (0, n - 1 - j, 0, 0)), part(0)],
        out_specs=[part(0), part(0), part(0), vec],
        out_shape=[jax.ShapeDtypeStruct((s, DN_W), F32)] * 3 + [jax.ShapeDtypeStruct((s, 128), F32)],
        scratch_shapes=[pltpu.VMEM((DN_HEADS, DN_D, DN_D), F32)],
        compiler_params=_cparams("arbitrary"))(qkv, qkv, qkv, bg, u, w, vn, tmat, states, do)


def _gdr_out(o, proj, dnw):
    s = o.shape[0]

    def body(o_ref, z_ref, w_ref, y_ref, yt_ref):
        ov, zv, wv = o_ref[...], z_ref[...], w_ref[...]
        for h in range(DN_HEADS):
            sl = slice(h * DN_D, (h + 1) * DN_D)
            oh = ov[:, sl]
            r = lax.rsqrt(jnp.mean(oh * oh, axis=-1, keepdims=True) + NORM_EPS)
            y = (oh * r * wv) * _silu(zv[:, sl])
            y_ref[:, sl] = y.astype(BF16)
            yt_ref[sl, :] = y.T.astype(BF16)

    row = pl.BlockSpec((ROW_TILE, DN_W), lambda i: (i, 0))
    return pl.pallas_call(
        body, name="gdr_out", grid=(s // ROW_TILE,),
        in_specs=[row, pl.BlockSpec((ROW_TILE, DN_W), lambda i: (i, OFF_Z_A // DN_W)), pl.BlockSpec((1, DN_D), lambda i: (0, 0))],
        out_specs=[row, pl.BlockSpec((DN_W, ROW_TILE), lambda i: (0, i))],
        out_shape=[jax.ShapeDtypeStruct((s, DN_W), BF16), jax.ShapeDtypeStruct((DN_W, s), BF16)],
        compiler_params=_cparams("parallel"))(o, proj, dnw)


def _gdr_out_bwd(o, proj, dnw, dy):
    s = o.shape[0]

    def body(o_ref, z_ref, w_ref, dy_ref, do_ref, dz_ref, dw_ref):
        i = pl.program_id(0)
        ov, zv, wv, dyv = o_ref[...], z_ref[...], w_ref[...], dy_ref[...]
        acc = jnp.zeros((1, DN_D), F32)
        for h in range(DN_HEADS):
            sl = slice(h * DN_D, (h + 1) * DN_D)
            oh, zh, dh = ov[:, sl], zv[:, sl], dyv[:, sl]
            r = lax.rsqrt(jnp.mean(oh * oh, axis=-1, keepdims=True) + NORM_EPS)
            dn = dh * _silu(zh)
            dz_ref[:, sl] = (dh * (oh * r * wv) * _silu_grad(zh)).astype(BF16)
            acc = acc + jnp.sum(dn * oh * r, axis=0, keepdims=True)
            dnw_ = dn * wv
            do_ref[:, sl] = r * dnw_ - oh * (r * r * r) * jnp.mean(dnw_ * oh, axis=-1, keepdims=True)

        @pl.when(i == 0)
        def _():
            dw_ref[...] = acc

        @pl.when(i > 0)
        def _():
            dw_ref[...] += acc

    row = pl.BlockSpec((ROW_TILE, DN_W), lambda i: (i, 0))
    vec = pl.BlockSpec((1, DN_D), lambda i: (0, 0))
    return pl.pallas_call(
        body, name="gdr_out_bwd", grid=(s // ROW_TILE,),
        in_specs=[row, pl.BlockSpec((ROW_TILE, DN_W), lambda i: (i, OFF_Z_A // DN_W)), vec, row],
        out_specs=[row, row, vec],
        out_shape=[jax.ShapeDtypeStruct((s, DN_W), F32), jax.ShapeDtypeStruct((s, DN_W), BF16),
                   jax.ShapeDtypeStruct((1, DN_D), F32)],
        compiler_params=_cparams("arbitrary"))(o, proj, dnw, dy)


def _slope(group, head):
    idx = (group * DIL_HEADS + head + 1).astype(F32)
    return jnp.exp(jnp.full((1, 128), -8.0 * math.log(2.0) / (N_DIL * DIL_HEADS), F32) * idx)


def _att_scores(qb, k_cur, k_prev, slope_d, has_prev):
    iq = lax.broadcasted_iota(jnp.int32, (ATT_BLOCK, ATT_BLOCK), 0)
    jk = lax.broadcasted_iota(jnp.int32, (ATT_BLOCK, ATT_BLOCK), 1)
    dist_c = (iq - jk).astype(F32)
    s_cur = jnp.where(iq >= jk, _dot_nt(qb, k_cur) - slope_d * dist_c, NEG)
    s_prev = jnp.where(jnp.logical_and(jk >= iq, has_prev),
                       _dot_nt(qb, k_prev) - slope_d * (dist_c + float(ATT_BLOCK)), NEG)
    return s_cur, s_prev


ATT_UNROLL = 4


def _att_blocks(i, dil, nb):
    per = dil * nb // ATT_UNROLL
    assert per * ATT_UNROLL == dil * nb
    for i0 in range(per):
        blocks = [divmod(i0 + u * per, nb) for u in range(ATT_UNROLL)]
        assert all(a[0] != b[0] or abs(a[1] - b[1]) >= 2 for n, a in enumerate(blocks) for b in blocks[n + 1:])
    curs, prvs, has_prev = [], [], []
    for u in range(ATT_UNROLL):
        t = i + u * per
        r = lax.div(t, nb)
        j = lax.rem(t, nb)
        base = r + dil * ATT_BLOCK * j
        pbase = base - dil * ATT_BLOCK * jnp.minimum(j, 1)
        if dil == 1:
            base, pbase = pl.multiple_of(base, ATT_BLOCK), pl.multiple_of(pbase, ATT_BLOCK)
        curs.append(pl.ds(base, ATT_BLOCK, stride=dil))
        prvs.append(pl.ds(pbase, ATT_BLOCK, stride=dil))
        has_prev.append(j > 0)
    return curs, prvs, has_prev


def _att_fwd(proj, group):
    s = proj.shape[0]
    dil = DIL_GROUPS[group][1]
    assert DIL_GROUPS[group][0] // dil == ATT_BLOCK
    nb = s // dil // ATT_BLOCK
    assert nb * dil * ATT_BLOCK == s

    def body(q_ref, k_ref, v_ref, num_ref, den_ref, mx_ref):
        slope_d = _slope(group, pl.program_id(0)) * float(dil)

        def step(i, carry):
            curs, prvs, has_prev = _att_blocks(i, dil, nb)
            us = range(ATT_UNROLL)
            qb = [q_ref[c, :] * (DIL_DH ** -0.5) for c in curs]
            sc = [_att_scores(qb[u], k_ref[curs[u], :], k_ref[prvs[u], :], slope_d, has_prev[u]) for u in us]
            mx = [jnp.maximum(jnp.max(a, axis=-1, keepdims=True), jnp.max(b, axis=-1, keepdims=True)) for a, b in sc]
            p_cur = [jnp.exp(sc[u][0] - mx[u]) for u in us]
            p_prev = [jnp.exp(sc[u][1] - mx[u]) for u in us]
            den = [jnp.sum(p_cur[u], axis=-1, keepdims=True) + jnp.sum(p_prev[u], axis=-1, keepdims=True) for u in us]
            num = [_dot(p_cur[u], v_ref[curs[u], :]) + _dot(p_prev[u], v_ref[prvs[u], :]) for u in us]
            for u in us:
                num_ref[curs[u], :] = num[u]
                den_ref[curs[u], :] = jnp.broadcast_to(den[u], (ATT_BLOCK, DIL_DH))
                mx_ref[curs[u], :] = jnp.broadcast_to(mx[u], (ATT_BLOCK, DIL_DH))
            return carry

        lax.fori_loop(0, dil * nb // ATT_UNROLL, step, 0)

    def col(off):
        return pl.BlockSpec((s, DIL_DH), lambda h: (0, off // DIL_DH + group * DIL_HEADS + h))

    out = pl.BlockSpec((s, DIL_DH), lambda h: (0, h))
    return pl.pallas_call(
        body, name=f"att_fwd{group}", grid=(DIL_HEADS,), in_specs=[col(OFF_Q_B), col(OFF_K_B), col(OFF_V_B)],
        out_specs=[out, out, out], out_shape=[jax.ShapeDtypeStruct((s, DIL_W), F32)] * 3,
        compiler_params=_cparams("parallel"))(proj, proj, proj)


def _att_bwd(proj, group, do, lse, delta):
    s = proj.shape[0]
    dil = DIL_GROUPS[group][1]
    nb = s // dil // ATT_BLOCK

    def body(q_ref, k_ref, v_ref, do_ref, lse_ref, dl_ref, dq_ref, dk_ref, dv_ref, dq_acc, dk_acc, dv_acc):
        slope_d = _slope(group, pl.program_id(0)) * float(dil)
        dk_acc[...] = jnp.zeros_like(dk_acc)
        dv_acc[...] = jnp.zeros_like(dv_acc)

        def step(i, carry):
            curs, prvs, has_prev = _att_blocks(i, dil, nb)
            us = range(ATT_UNROLL)
            qb = [q_ref[c, :] * (DIL_DH ** -0.5) for c in curs]
            k_cur, k_prev = [k_ref[c, :] for c in curs], [k_ref[p, :] for p in prvs]
            v_cur, v_prev = [v_ref[c, :] for c in curs], [v_ref[p, :] for p in prvs]
            sc = [_att_scores(qb[u], k_cur[u], k_prev[u], slope_d, has_prev[u]) for u in us]
            lse_b, delta_b, dob = [lse_ref[c, :] for c in curs], [dl_ref[c, :] for c in curs], [do_ref[c, :] for c in curs]
            p_cur = [jnp.exp(sc[u][0] - lse_b[u]) for u in us]
            p_prev = [jnp.exp(sc[u][1] - lse_b[u]) for u in us]
            ds_cur = [p_cur[u] * (_dot_nt(dob[u], v_cur[u]) - delta_b[u]) for u in us]
            ds_prev = [p_prev[u] * (_dot_nt(dob[u], v_prev[u]) - delta_b[u]) for u in us]
            dq = [(_dot(ds_cur[u], k_cur[u]) + _dot(ds_prev[u], k_prev[u])) * (DIL_DH ** -0.5) for u in us]
            dk_c = [_dot_tn(ds_cur[u], qb[u]) for u in us]
            dv_c = [_dot_tn(p_cur[u], dob[u]) for u in us]
            dk_p = [_dot_tn(ds_prev[u], qb[u]) for u in us]
            dv_p = [_dot_tn(p_prev[u], dob[u]) for u in us]
            for u in us:
                dq_acc[curs[u], :] = dq[u]
                dk_acc[curs[u], :] += dk_c[u]
                dv_acc[curs[u], :] += dv_c[u]
            for u in us:
                dk_acc[prvs[u], :] += dk_p[u]
                dv_acc[prvs[u], :] += dv_p[u]
            return carry

        lax.fori_loop(0, dil * nb // ATT_UNROLL, step, 0)
        dq_ref[...] = dq_acc[...].astype(BF16)
        dk_ref[...] = dk_acc[...].astype(BF16)
        dv_ref[...] = dv_acc[...].astype(BF16)

    def col(off):
        return pl.BlockSpec((s, DIL_DH), lambda h: (0, off // DIL_DH + group * DIL_HEADS + h))

    hd = pl.BlockSpec((s, DIL_DH), lambda h: (0, h))
    return pl.pallas_call(
        body, name=f"att_bwd{group}", grid=(DIL_HEADS,),
        in_specs=[col(OFF_Q_B), col(OFF_K_B), col(OFF_V_B), hd, hd, hd], out_specs=[hd, hd, hd],
        out_shape=[jax.ShapeDtypeStruct((s, DIL_W), BF16)] * 3,
        scratch_shapes=[pltpu.VMEM((s, DIL_DH), F32)] * 3,
        compiler_params=_cparams("parallel"))(proj, proj, proj, do, lse, delta)


def _att_merge(parts, proj):
    s = proj.shape[0]

    def body(n0, d0, m0, n1, d1, m1, n2, d2, m2, z_ref, ob_ref, o_ref, lse_ref, obt_ref):
        m = jnp.maximum(jnp.maximum(m0[...], m1[...]), m2[...])
        num = jnp.zeros_like(m)
        den = jnp.zeros_like(m)
        for nr, dr, mr in ((n0, d0, m0), (n1, d1, m1), (n2, d2, m2)):
            sc = jnp.exp(mr[...] - m)
            num = num + nr[...] * sc
            den = den + dr[...] * sc
        o = num / den
        o_ref[...] = o
        lse_ref[...] = m + jnp.log(den)
        ob = o * _silu(z_ref[...])
        ob_ref[...] = ob.astype(BF16)
        obt_ref[...] = ob.T.astype(BF16)

    row = pl.BlockSpec((ROW_TILE, DIL_W), lambda i: (i, 0))
    flat = [a for p in parts for a in p]
    return pl.pallas_call(
        body, name="att_merge", grid=(s // ROW_TILE,),
        in_specs=[row] * 9 + [pl.BlockSpec((ROW_TILE, DIL_W), lambda i: (i, OFF_Z_B // DIL_W))],
        out_specs=[row, row, row, pl.BlockSpec((DIL_W, ROW_TILE), lambda i: (0, i))],
        out_shape=[jax.ShapeDtypeStruct((s, DIL_W), BF16), jax.ShapeDtypeStruct((s, DIL_W), F32),
                   jax.ShapeDtypeStruct((s, DIL_W), F32), jax.ShapeDtypeStruct((DIL_W, s), BF16)],
        compiler_params=_cparams("parallel"))(*flat, proj)


def _att_merge_bwd(o, proj, dob):
    s = o.shape[0]

    def body(o_ref, z_ref, d_ref, do_ref, dl_ref, dz_ref):
        ov, zv, dv = o_ref[...], z_ref[...], d_ref[...]
        do = dv * _silu(zv)
        do_ref[...] = do
        dz_ref[...] = (dv * ov * _silu_grad(zv)).astype(BF16)
        for h in range(DIL_HEADS):
            sl = slice(h * DIL_DH, (h + 1) * DIL_DH)
            dl_ref[:, sl] = jnp.broadcast_to(jnp.sum(do[:, sl] * ov[:, sl], axis=-1, keepdims=True), (ROW_TILE, DIL_DH))

    row = pl.BlockSpec((ROW_TILE, DIL_W), lambda i: (i, 0))
    return pl.pallas_call(
        body, name="att_merge_bwd", grid=(s // ROW_TILE,),
        in_specs=[row, pl.BlockSpec((ROW_TILE, DIL_W), lambda i: (i, OFF_Z_B // DIL_W)), row],
        out_specs=[row, row, row],
        out_shape=[jax.ShapeDtypeStruct((s, DIL_W), F32), jax.ShapeDtypeStruct((s, DIL_W), F32),
                   jax.ShapeDtypeStruct((s, DIL_W), BF16)],
        compiler_params=_cparams("parallel"))(o, proj, dob)


def _merge(proj, ya, yb):
    s = proj.shape[0]

    def body(ga_ref, gb_ref, ya_ref, yb_ref, o_ref, ot_ref):
        m = _sigmoid(ga_ref[...]) * ya_ref[...] + _sigmoid(gb_ref[...]) * yb_ref[...]
        o_ref[...] = m.astype(BF16)
        ot_ref[...] = m.T.astype(BF16)

    row = pl.BlockSpec((ROW_TILE, D_MODEL), lambda i: (i, 0))
    return pl.pallas_call(
        body, name="merge", grid=(s // ROW_TILE,),
        in_specs=[pl.BlockSpec((ROW_TILE, D_MODEL), lambda i: (i, OFF_G_A // D_MODEL)),
                  pl.BlockSpec((ROW_TILE, D_MODEL), lambda i: (i, OFF_G_B // D_MODEL)), row, row],
        out_specs=[row, pl.BlockSpec((D_MODEL, ROW_TILE), lambda i: (0, i))],
        out_shape=[jax.ShapeDtypeStruct((s, D_MODEL), BF16), jax.ShapeDtypeStruct((D_MODEL, s), BF16)],
        compiler_params=_cparams("parallel"))(proj, proj, ya, yb)


def _merge_bwd(proj, ya, yb, dm):
    s = proj.shape[0]

    def body(ga_ref, gb_ref, ya_ref, yb_ref, dm_ref, dya_ref, dyb_ref, dga_ref, dgb_ref):
        dmv = dm_ref[...]
        sa, sb = _sigmoid(ga_ref[...]), _sigmoid(gb_ref[...])
        dya_ref[...] = (dmv * sa).astype(BF16)
        dyb_ref[...] = (dmv * sb).astype(BF16)
        dga_ref[...] = (dmv * ya_ref[...] * sa * (1.0 - sa)).astype(BF16)
        dgb_ref[...] = (dmv * yb_ref[...] * sb * (1.0 - sb)).astype(BF16)

    row = pl.BlockSpec((ROW_TILE, D_MODEL), lambda i: (i, 0))
    return pl.pallas_call(
        body, name="merge_bwd", grid=(s // ROW_TILE,),
        in_specs=[pl.BlockSpec((ROW_TILE, D_MODEL), lambda i: (i, OFF_G_A // D_MODEL)),
                  pl.BlockSpec((ROW_TILE, D_MODEL), lambda i: (i, OFF_G_B // D_MODEL)), row, row, row],
        out_specs=[row] * 4, out_shape=[jax.ShapeDtypeStruct((s, D_MODEL), BF16)] * 4,
        compiler_params=_cparams("parallel"))(proj, proj, ya, yb, dm)


def _final(x, t, fw, tgt):
    s, d = x.shape

    def body(x_ref, t_ref, w_ref, y_ref, dx_ref, dw_ref, l_ref):
        i = pl.program_id(0)
        x2 = x_ref[...] + t_ref[...]
        wv = w_ref[...]
        r = lax.rsqrt(jnp.mean(x2 * x2, axis=-1, keepdims=True) + NORM_EPS)
        e = x2 * r * wv - y_ref[...]
        lrow = jnp.mean(e * e, axis=-1, keepdims=True)
        lpart = jnp.broadcast_to(0.5 * jnp.sum(lrow, axis=0, keepdims=True), (1, 128))
        dy = e * (1.0 / d)
        dwp = jnp.sum(dy * x2 * r, axis=0, keepdims=True)
        dyw = dy * wv
        dx_ref[...] = r * dyw - x2 * (r * r * r) * jnp.mean(dyw * x2, axis=-1, keepdims=True)

        @pl.when(i == 0)
        def _():
            dw_ref[...] = dwp
            l_ref[...] = lpart

        @pl.when(i > 0)
        def _():
            dw_ref[...] += dwp
            l_ref[...] += lpart

    row = pl.BlockSpec((ROW_TILE, d), lambda i: (i, 0))
    vec = pl.BlockSpec((1, d), lambda i: (0, 0))
    return pl.pallas_call(
        body, name="final", grid=(s // ROW_TILE,), in_specs=[row, row, vec, row],
        out_specs=[row, vec, pl.BlockSpec((1, 128), lambda i: (0, 0))],
        out_shape=[jax.ShapeDtypeStruct((s, d), F32), jax.ShapeDtypeStruct((1, d), F32), jax.ShapeDtypeStruct((1, 128), F32)],
        compiler_params=_cparams("arbitrary"))(x, t, fw, tgt)


def _adamw(w, g, m, v, name):
    r, c = w.shape
    cap = max(8, (1 << 18) // c)
    divisors = [t for t in range(8, min(r, cap) + 1, 8) if r % t == 0]
    tr = r if r <= 8 else (max(divisors) if divisors else cap)

    def body(w_ref, g_ref, m_ref, v_ref, d_ref, nm_ref, nv_ref):
        gv = g_ref[...]
        mn = ADAM_B1 * m_ref[...] + (1.0 - ADAM_B1) * gv
        vn = ADAM_B2 * v_ref[...] + (1.0 - ADAM_B2) * (gv * gv)
        m_hat = mn / (1.0 - ADAM_B1 ** ADAM_STEP)
        v_hat = vn / (1.0 - ADAM_B2 ** ADAM_STEP)
        d_ref[...] = -ADAM_LR * (m_hat / (jnp.sqrt(v_hat) + ADAM_EPS) + ADAM_WD * w_ref[...])
        nm_ref[...] = mn
        nv_ref[...] = vn

    blk = pl.BlockSpec((tr, c), lambda i: (i, 0))
    return pl.pallas_call(
        body, name=name, grid=(pl.cdiv(r, tr),), in_specs=[blk] * 4, out_specs=[blk] * 3,
        out_shape=[jax.ShapeDtypeStruct((r, c), F32)] * 3, compiler_params=_cparams("parallel"))(w, g, m, v)


HBM_SPEC = pl.BlockSpec(memory_space=pl.ANY)


def _place():
    x, y, c = lax.axis_index("x"), lax.axis_index("y"), lax.axis_index("c")
    chips = [(1 - x, y), (x, 1 - y), (1 - x, 1 - y)]
    return x, y, c, chips


def _ag_weights(packs):
    na = len(packs)
    nsem = 7

    def body(*refs):
        p_refs, out_refs = refs[:na], refs[na:2 * na]
        send_sems, recv_sems = refs[2 * na:]
        x, y, c, _ = _place()
        me, sib, j = (x, y, c), (x, y, 1 - c), 2 * x + y
        xn, yn = (1 - x, y, c), (x, 1 - y, c)
        jx, jy, jd = 2 * (1 - x) + y, 2 * x + (1 - y), 2 * (1 - x) + (1 - y)

        def rc(a, k, src, dst, to):
            return pltpu.make_async_remote_copy(src_ref=src, dst_ref=dst, send_sem=send_sems.at[nsem * a + k],
                                                recv_sem=recv_sems.at[nsem * a + k], device_id=to, device_id_type=MESH)

        sent = []
        for a in range(na):
            mine, land = p_refs[a].at[c], out_refs[a].at[j, c]
            sent += [rc(a, 0, mine, land, xn), rc(a, 1, mine, land, yn)]
        for cp in sent:
            cp.start()
        for a in range(na):
            half = p_refs[a].shape[1] // 2
            top, bottom = pl.ds(0, half), pl.ds(half, half)
            from_x, from_y, from_d = out_refs[a].at[jx, c], out_refs[a].at[jy, c], out_refs[a].at[jd, c]
            rc(a, 0, p_refs[a].at[c], from_x, me).wait_recv()
            later = [rc(a, 2, from_x.at[top], from_x.at[top], yn), rc(a, 4, from_x, from_x, sib)]
            for cp in later:
                cp.start()
            sent += later
            rc(a, 1, p_refs[a].at[c], from_y, me).wait_recv()
            later = [rc(a, 3, from_y.at[bottom], from_y.at[bottom], xn), rc(a, 5, from_y, from_y, sib)]
            for cp in later:
                cp.start()
            sent += later
            rc(a, 2, from_d.at[top], from_d.at[top], me).wait_recv()
            rc(a, 3, from_d.at[bottom], from_d.at[bottom], me).wait_recv()
            cp = rc(a, 6, from_d, from_d, sib)
            cp.start()
            sent.append(cp)
        for a in range(na):
            for k, jj in ((4, jx), (5, jy), (6, jd)):
                rc(a, k, p_refs[a].at[c], out_refs[a].at[jj, 1 - c], me).wait_recv()
        for cp in sent:
            cp.wait_send()

    return pl.pallas_call(
        body, name="ag_weights",
        out_shape=[jax.ShapeDtypeStruct((N_CHIPS,) + p.shape, p.dtype) for p in packs],
        in_specs=[HBM_SPEC] * na, out_specs=[HBM_SPEC] * na,
        scratch_shapes=[pltpu.SemaphoreType.DMA((nsem * na,)), pltpu.SemaphoreType.DMA((nsem * na,))])(*packs)


def _rs_pair(dwpt, gpack):
    n = N_CHIPS
    hw = SHARD_PAD // 2

    def body(d_ref, g_ref, out_d, out_g, send_sems, recv_sems):
        x, y, c, _ = _place()
        sib = (x, y, 1 - c)
        cps = []
        for p in range(n):
            start = pl.multiple_of(WIN_BASE[p] + (1 - c) * hw, TILE_ROWS)
            cps.append(pltpu.make_async_remote_copy(
                src_ref=d_ref.at[pl.ds(start, hw)], dst_ref=out_d.at[p], send_sem=send_sems.at[p],
                recv_sem=recv_sems.at[p], device_id=sib, device_id_type=MESH))
            cps.append(pltpu.make_async_remote_copy(
                src_ref=g_ref.at[p, 1 - c], dst_ref=out_g.at[p], send_sem=send_sems.at[n + p],
                recv_sem=recv_sems.at[n + p], device_id=sib, device_id_type=MESH))
        for cp in cps:
            cp.start()
        for cp in cps:
            cp.wait_recv()
        for cp in cps:
            cp.wait_send()

    return pl.pallas_call(
        body, name="rs_pair",
        out_shape=[jax.ShapeDtypeStruct((n, hw, dwpt.shape[1]), dwpt.dtype),
                   jax.ShapeDtypeStruct((n,) + gpack.shape[2:], gpack.dtype)],
        in_specs=[HBM_SPEC] * 2, out_specs=[HBM_SPEC] * 2,
        scratch_shapes=[pltpu.SemaphoreType.DMA((2 * n,)), pltpu.SemaphoreType.DMA((2 * n,))])(dwpt, gpack)


def _add_halves_win(dwpt, other, c):
    n, rh, wd = other.shape
    tr = _row_tile(rh)

    def body(s_ref, d_ref, o_ref, out_ref):
        out_ref[0] = (d_ref[...] + o_ref[0]).astype(BF16)

    scal = jnp.concatenate([jnp.reshape(c, (1,)).astype(jnp.int32), jnp.asarray(WIN_BASE, jnp.int32)])
    grid_spec = pltpu.PrefetchScalarGridSpec(
        num_scalar_prefetch=1, grid=(n, rh // tr),
        in_specs=[pl.BlockSpec((pl.Element(tr), pl.Element(wd)),
                               lambda p, i, sr: (pl.multiple_of(sr[1 + p] + sr[0] * rh + i * tr, TILE_ROWS), 0)),
                  pl.BlockSpec((1, tr, wd), lambda p, i, sr: (p, i, 0))],
        out_specs=pl.BlockSpec((1, tr, wd), lambda p, i, sr: (p, i, 0)))
    return pl.pallas_call(
        body, name="add_halves_in", grid_spec=grid_spec, out_shape=jax.ShapeDtypeStruct((n, rh, wd), BF16),
        compiler_params=_cparams("parallel", "parallel"))(scal, dwpt, other)


SEM_SPEC = pl.BlockSpec(memory_space=pltpu.SEMAPHORE)
DATAFLOW_EFFECT = pltpu.SideEffectType.DATAFLOW_SIDE_EFFECTING


def _rs_chips_start(csums):
    na = len(csums)

    def body(*refs):
        s_refs, land_refs = refs[:na], refs[na:2 * na]
        send_sems, recv_sems = refs[2 * na], refs[2 * na + 1]
        token = refs[-1]
        x, y, c, chips = _place()
        j = 2 * x + y
        for a in range(na):
            for k, (cx, cy) in enumerate(chips):
                pltpu.make_async_remote_copy(src_ref=s_refs[a].at[2 * cx + cy], dst_ref=land_refs[a].at[j],
                                             send_sem=send_sems.at[3 * a + k], recv_sem=recv_sems.at[3 * a + k],
                                             device_id=(cx, cy, c), device_id_type=MESH).start()
        token[...] = jnp.zeros_like(token)

    hbm = [pltpu.HBM(s.shape, s.dtype) for s in csums]
    args = [pltpu.with_memory_space_constraint(s, pltpu.HBM) for s in csums]
    args += [pltpu.with_memory_space_constraint(lax.empty(s.shape, s.dtype), pltpu.HBM) for s in csums]
    res = pl.pallas_call(
        body, name="rs_chips_start",
        out_shape=(pltpu.SemaphoreType.DMA((3 * na,)), pltpu.SemaphoreType.DMA((3 * na,)), *hbm, *hbm,
                   jax.ShapeDtypeStruct((8, 128), F32)),
        in_specs=[pl.BlockSpec(memory_space=pltpu.HBM)] * (2 * na),
        out_specs=(SEM_SPEC, SEM_SPEC, *[pl.BlockSpec(memory_space=pltpu.HBM)] * (2 * na),
                   pl.BlockSpec(memory_space=pltpu.VMEM)),
        input_output_aliases={i: 2 + i for i in range(2 * na)},
        compiler_params=pltpu.CompilerParams(has_side_effects=DATAFLOW_EFFECT))(*args)
    return res[0], res[1], list(res[2:2 + na]), list(res[2 + na:2 + 2 * na]), res[-1]


def _rs_chips_wait(send_sems, recv_sems, csums, lands, after):
    na = len(csums)

    def body(*refs):
        s_refs, land_refs = refs[:na], refs[na:2 * na]
        send_sems, recv_sems = refs[2 * na], refs[2 * na + 1]
        x, y, c, chips = _place()
        j = 2 * x + y
        for a in range(na):
            for k, (cx, cy) in enumerate(chips):
                cp = pltpu.make_async_remote_copy(src_ref=s_refs[a].at[2 * cx + cy], dst_ref=land_refs[a].at[2 * cx + cy],
                                                  send_sem=send_sems.at[3 * a + k], recv_sem=recv_sems.at[3 * a + k],
                                                  device_id=(cx, cy, c), device_id_type=MESH)
                cp.wait_send()
                cp.wait_recv()

    hbm = [pltpu.HBM(s.shape, s.dtype) for s in csums]
    res = pl.pallas_call(
        body, name="rs_chips_wait", out_shape=(*hbm, *hbm),
        in_specs=[pl.BlockSpec(memory_space=pltpu.HBM)] * (2 * na) + [SEM_SPEC, SEM_SPEC, pl.BlockSpec(memory_space=pl.ANY)],
        out_specs=tuple([pl.BlockSpec(memory_space=pltpu.HBM)] * (2 * na)),
        input_output_aliases={i: i for i in range(2 * na)},
        compiler_params=pltpu.CompilerParams(has_side_effects=DATAFLOW_EFFECT))(*csums, *lands, send_sems, recv_sems, after)
    return list(res[:na]), list(res[na:])


SWAP_CHUNKS = 4


def _pair_swap(halves):
    na = len(halves)

    def body(*refs):
        h_refs, out_refs = refs[:na], refs[na:2 * na]
        send_sems, recv_sems = refs[2 * na:]
        x, y, c, _ = _place()
        cps = []
        for a in range(na):
            rows = h_refs[a].shape[0] // SWAP_CHUNKS
            assert rows * SWAP_CHUNKS == h_refs[a].shape[0]
            for q in range(SWAP_CHUNKS):
                k = SWAP_CHUNKS * a + q
                cps.append(pltpu.make_async_remote_copy(
                    src_ref=h_refs[a].at[pl.ds(q * rows, rows)], dst_ref=out_refs[a].at[pl.ds(q * rows, rows)],
                    send_sem=send_sems.at[k], recv_sem=recv_sems.at[k], device_id=(x, y, 1 - c), device_id_type=MESH))
        for cp in cps:
            cp.start()
        for cp in cps:
            cp.wait_recv()
        for cp in cps:
            cp.wait_send()

    return pl.pallas_call(
        body, name="pair_swap", out_shape=[jax.ShapeDtypeStruct(h.shape, h.dtype) for h in halves],
        in_specs=[HBM_SPEC] * na, out_specs=[HBM_SPEC] * na,
        scratch_shapes=[pltpu.SemaphoreType.DMA((SWAP_CHUNKS * na,)), pltpu.SemaphoreType.DMA((SWAP_CHUNKS * na,))])(*halves)


def _ag_small(v):
    m_per, n = v.shape

    def body(x_ref, out_ref, send_sems, recv_sems, local_sem):
        x, y, c, chips = _place()
        me, sibling = (x, y, c), (x, y, 1 - c)

        def rows(px, py, pc):
            return out_ref.at[pl.ds((4 * px + 2 * py + pc) * m_per, m_per), :]

        def copy(k, block, to, src=None):
            return pltpu.make_async_remote_copy(
                src_ref=rows(*block) if src is None else src, dst_ref=rows(*block), send_sem=send_sems.at[k],
                recv_sem=recv_sems.at[k], device_id=to, device_id_type=MESH)

        mine = pltpu.make_async_copy(x_ref, rows(*me), local_sem)
        mine.start()
        first = [copy(0, me, sibling, src=x_ref)]
        first += [copy(1 + k, me, (*chip, c), src=x_ref) for k, chip in enumerate(chips)]
        for cp in first:
            cp.start()
        passed = [copy(4 + k, (*chip, c), sibling) for k, chip in enumerate(chips)]
        for k, chip in enumerate(chips):
            copy(1 + k, (*chip, c), me).wait_recv()
            passed[k].start()
        copy(0, sibling, me).wait_recv()
        for k, chip in enumerate(chips):
            copy(4 + k, (*chip, 1 - c), me).wait_recv()
        for cp in first + passed:
            cp.wait_send()
        mine.wait()

    return pl.pallas_call(
        body, name="ag_small", out_shape=jax.ShapeDtypeStruct((8 * m_per, n), v.dtype),
        in_specs=[pl.BlockSpec(memory_space=pltpu.VMEM)], out_specs=pl.BlockSpec(memory_space=pltpu.VMEM),
        scratch_shapes=[pltpu.SemaphoreType.DMA((7,)), pltpu.SemaphoreType.DMA((7,)), pltpu.SemaphoreType.DMA])(v)


def _sum_blocks(a, nblk, name):
    rows, wd = a.shape
    r = rows // nblk
    tr = min(r, ROW_TILE)
    assert r % tr == 0

    def body(*refs):
        acc = refs[0][...].astype(F32)
        for ref in refs[1:nblk]:
            acc = acc + ref[...].astype(F32)
        refs[nblk][...] = acc

    nt = r // tr
    return pl.pallas_call(
        body, name=name, grid=(nt,),
        in_specs=[pl.BlockSpec((tr, wd), functools.partial(lambda i, b: (b * nt + i, 0), b=b)) for b in range(nblk)],
        out_specs=pl.BlockSpec((tr, wd), lambda i: (i, 0)),
        out_shape=jax.ShapeDtypeStruct((r, wd), F32), compiler_params=_cparams("parallel"))(*([a] * nblk))


def _row_tile(rows):
    best = max(t for t in range(16, 513, 16) if rows % t == 0)
    return best


def _sum_chips(by_src, csum, j, name):
    n, rh, wd = by_src.shape
    tr = _row_tile(rh)

    def body(j_ref, *refs):
        own = refs[n][0].astype(F32)
        acc = None
        for k in range(n):
            term = jnp.where(j_ref[0] == k, own, refs[k][0].astype(F32))
            acc = term if acc is None else acc + term
        refs[n + 1][...] = acc

    def other(k):
        return pl.BlockSpec((1, tr, wd), lambda i, jr: (jnp.where(jr[0] == k, (k + 1) % n, k), i, 0))

    grid_spec = pltpu.PrefetchScalarGridSpec(
        num_scalar_prefetch=1, grid=(rh // tr,),
        in_specs=[other(k) for k in range(n)] + [pl.BlockSpec((1, tr, wd), lambda i, jr: (jr[0], i, 0))],
        out_specs=pl.BlockSpec((tr, wd), lambda i, jr: (i, 0)))
    return pl.pallas_call(
        body, name=name, grid_spec=grid_spec, out_shape=jax.ShapeDtypeStruct((rh, wd), F32),
        compiler_params=_cparams("parallel"))(jnp.reshape(j, (1,)).astype(jnp.int32), *([by_src] * n), csum)


def _add_halves(gpack, other, c, name):
    n, _, rh, wd = gpack.shape
    tr = _row_tile(rh)

    def body(c_ref, g_ref, o_ref, out_ref):
        out_ref[0] = (g_ref[0, 0] + o_ref[0]).astype(BF16)

    grid_spec = pltpu.PrefetchScalarGridSpec(
        num_scalar_prefetch=1, grid=(n, rh // tr),
        in_specs=[pl.BlockSpec((1, 1, tr, wd), lambda p, i, cr: (p, cr[0], i, 0)),
                  pl.BlockSpec((1, tr, wd), lambda p, i, cr: (p, i, 0))],
        out_specs=pl.BlockSpec((1, tr, wd), lambda p, i, cr: (p, i, 0)))
    return pl.pallas_call(
        body, name=name, grid_spec=grid_spec, out_shape=jax.ShapeDtypeStruct((n, rh, wd), BF16),
        compiler_params=_cparams("parallel", "parallel"))(jnp.reshape(c, (1,)).astype(jnp.int32), gpack, other)


PACK_W = 1024
ROWS_O_DN = DN_W // N_CHIPS
ROWS_O_DIL = DIL_W * (D_MODEL // N_CHIPS) // PACK_W
ROWS_OUT = D_MODEL // N_CHIPS
ROWS_CONV = 4 * (3 * DN_W // N_CHIPS) // PACK_W
R1 = ROWS_O_DN
R2 = R1 + ROWS_O_DIL
R3 = R2 + ROWS_OUT
R4 = R3 + 16
R5 = R4 + 16
PACK_ROWS = 704
HALF_ROWS = PACK_ROWS // 2
SHARD_PAD = 2880


R6 = R5 + 2 * DN_HEADS

TILE_ROWS = 16
BA_IN_SHARD1 = REF_OFF_BA - SHARD_W
LOCAL_START = (0, SHARD_W, 2 * SHARD_W - 2 * DN_HEADS, 3 * SHARD_W - 2 * DN_HEADS)
LOCAL_END = LOCAL_START[1:] + (OFF_BA,)
WIN_BASE = tuple(s // TILE_ROWS * TILE_ROWS for s in LOCAL_START)


def _to_window(k, shard):
    nba = 2 * DN_HEADS
    body = shard
    if k == 1:
        row = lax.broadcasted_iota(jnp.int32, (SHARD_W - nba, 1), 0)
        body = jnp.where(row < BA_IN_SHARD1, shard[:SHARD_W - nba], shard[nba:])
    lead = LOCAL_START[k] - WIN_BASE[k]
    return jnp.pad(body, ((lead, SHARD_PAD - lead - body.shape[0]), (0, 0)))


def _from_window(k, win, ba):
    nba = 2 * DN_HEADS
    lead = LOCAL_START[k] - WIN_BASE[k]
    if k != 1:
        return win[lead:lead + SHARD_W]
    row = lax.broadcasted_iota(jnp.int32, (SHARD_W, 1), 0)
    before = win[lead:lead + SHARD_W]
    after = jnp.pad(win, ((nba, 0), (0, 0)))[lead:lead + SHARD_W]
    mid = jnp.pad(ba, ((BA_IN_SHARD1, SHARD_W - BA_IN_SHARD1 - nba), (0, 0)))
    return jnp.where(row < BA_IN_SHARD1, before, jnp.where(row < BA_IN_SHARD1 + nba, mid, after))


def _stack_windows(wins, ba):
    pieces = []
    for k in range(N_CHIPS):
        lo = WIN_BASE[k] + (TILE_ROWS if k else 0)
        hi = LOCAL_END[k] // TILE_ROWS * TILE_ROWS
        pieces.append(wins[k][lo - WIN_BASE[k]:hi - WIN_BASE[k]])
        if k + 1 < N_CHIPS:
            assert hi == WIN_BASE[k + 1]
            pieces.append(wins[k][hi - WIN_BASE[k]:hi - WIN_BASE[k] + TILE_ROWS] + wins[k + 1][:TILE_ROWS])
    pieces += [ba, jnp.zeros((PW - OFF_BA - ba.shape[0], ba.shape[1]), ba.dtype)]
    out = jnp.concatenate(pieces, axis=0)
    assert out.shape[0] == PW
    return out


def _to_ref_layout(wpt):
    return jnp.concatenate([wpt[:REF_OFF_BA], wpt[OFF_BA:OFF_BA + 2 * DN_HEADS], wpt[REF_OFF_BA:OFF_BA]], axis=0)


def _from_ref_layout(wt):
    pad = jnp.zeros((PW - PROJ_W, wt.shape[1]), wt.dtype)
    return jnp.concatenate([wt[:REF_OFF_BA], wt[REF_OFF_BA + 2 * DN_HEADS:], wt[REF_OFF_BA:REF_OFF_BA + 2 * DN_HEADS], pad],
                           axis=0)


def _local_step(x, tgt, norm_w, wpt, conv_full, a_log, dt_bias, dn_norm_w, w_o_dn, w_o_dil, w_out, final_norm_w):
    s = x.shape[0]
    h, h_t = _rms_in(x, norm_w)
    proj = _matmul(h, wpt, F32, 2048, 1280, 1024, "proj", nt=True)
    c_pre, qkv = _conv_fwd(proj, conv_full)
    gate_par = jnp.zeros((8, 128), F32).at[0, 8:16].set(a_log[0]).at[1, 8:16].set(dt_bias[0])
    bg = _gates_fwd(proj, gate_par)
    o_a, u, w, vn, tmat, states = _gdr_fwd(qkv, bg)
    oa2, oa2_t = _gdr_out(o_a, proj, dn_norm_w)
    ya = _matmul(oa2, w_o_dn, F32, 512, 1024, 1024, "ya")
    parts = [_att_fwd(proj, g) for g in range(N_DIL)]
    ob, o_att, lse, ob_t = _att_merge(parts, proj)
    yb = _matmul(ob, w_o_dil, F32, 512, 1024, 512, "yb")
    mg, mg_t = _merge(proj, ya, yb)
    t = _matmul(mg, w_out, F32, 512, 1024, 1024, "t_out")
    dx2, dfw, lpart = _final(x, t, final_norm_w, tgt)

    dmg = _matmul(dx2, w_out, F32, 512, 1024, 1024, "d_merged", nt=True)
    dw_out = _matmul(mg_t, dx2, F32, 1024, 1024, 1024, "dw_out")
    dya, dyb, dga, dgb = _merge_bwd(proj, ya, yb, dmg)
    doa2 = _matmul(dya, w_o_dn, F32, 512, 1024, 1024, "d_oa2", nt=True)
    dw_o_dn = _matmul(oa2_t, dya, F32, 1024, 1024, 1024, "dw_o_dn")
    dob = _matmul(dyb, w_o_dil, F32, 512, 512, 1024, "d_ob", nt=True)
    dw_o_dil = _matmul(ob_t, dyb, F32, 512, 1024, 1024, "dw_o_dil")
    do_a, dz_a, ddnw = _gdr_out_bwd(o_a, proj, dn_norm_w, doa2)
    dq_a, dk_a, dv_a, dbg = _gdr_bwd(qkv, bg, u, w, vn, tmat, states, do_a)
    dba, dpar = _gates_bwd(proj, gate_par, dbg)
    dc = _conv_bwd_act(c_pre, dq_a, dk_a, dv_a)
    du_a, dconv = _conv_bwd(proj, dc, conv_full)
    do_att, delta, dz_b = _att_merge_bwd(o_att, proj, dob)
    dqkv_b = [_att_bwd(proj, g, do_att, lse, delta) for g in range(N_DIL)]
    dproj = jnp.concatenate(
        [du_a, dz_a] + [dqkv_b[g][i] for i in range(3) for g in range(N_DIL)]
        + [dz_b, dga, dgb, dba, jnp.zeros((s, PW - OFF_BA - 128), BF16)], axis=1)
    dwpt = _matmul(h_t, dproj, F32, 1024, 1280, 2048, "dw_in", transpose_out=True)

    def finish(after=None):
        dh = _matmul(dproj, wpt, F32, 2048, 1024, 1152, "d_h", after=after)
        grad_x, dnw = _rms_in_bwd(x, norm_w, dh, dx2)
        small = jnp.zeros((8, PACK_W), F32)
        small = small.at[0].set(dnw[0]).at[1].set(dfw[0]).at[2, :DN_D].set(ddnw[0])
        small = small.at[3, :DN_HEADS].set(dpar[0, 8:16]).at[3, DN_HEADS:2 * DN_HEADS].set(dpar[1, 8:16])
        small = small.at[4, 0].set(lpart[0, 0])
        return grad_x, small

    return finish, dwpt, dconv, dw_o_dn, dw_o_dil, dw_out


def kernel(x, norm_w, w_in, conv_w, a_log, dt_bias, dn_norm_w, w_o_dn, w_o_dil, w_out, final_norm_w, loss_target, m_norm_w, m_w_in, m_conv_w, m_a_log, m_dt_bias, m_dn_norm_w, m_w_o_dn, m_w_o_dil, m_w_out, m_final_norm_w, v_norm_w, v_w_in, v_conv_w, v_a_log, v_dt_bias, v_dn_norm_w, v_w_o_dn, v_w_o_dil, v_w_out, v_final_norm_w):
    c = lax.axis_index("c")
    j = 2 * lax.axis_index("x") + lax.axis_index("y")
    qw = D_MODEL // N_CHIPS

    cw = conv_w[0].reshape(ROWS_CONV, PACK_W)
    cw = jnp.pad(cw, ((0, 16 - ROWS_CONV), (0, 0)))
    cw_hi = cw.astype(BF16)
    cw_lo = (cw - cw_hi.astype(F32)).astype(BF16)
    shard = w_in[0].T.astype(BF16)
    own_ba = jnp.where(j == 1, shard[BA_IN_SHARD1:BA_IN_SHARD1 + 2 * DN_HEADS], jnp.zeros((2 * DN_HEADS, D_MODEL), BF16))
    pack = jnp.concatenate(
        [w_o_dn[0].astype(BF16), w_o_dil[0].astype(BF16).reshape(ROWS_O_DIL, PACK_W), w_out[0].astype(BF16), cw_hi, cw_lo,
         own_ba, jnp.zeros((PACK_ROWS - R6, PACK_W), BF16)], axis=0).reshape(2, HALF_ROWS, PACK_W)
    chips = range(N_CHIPS)
    own_win = lax.switch(j, [functools.partial(_to_window, k) for k in chips], shard).reshape(2, SHARD_PAD // 2, D_MODEL)
    all_in, allw = _ag_weights([own_win, pack])
    wins = [jnp.where(j == k, own_win, all_in[k]).reshape(SHARD_PAD, D_MODEL) for k in chips]
    allw = [jnp.where(j == k, pack, allw[k]).reshape(PACK_ROWS, PACK_W) for k in chips]
    wpt = _stack_windows(wins, allw[1][R5:R6])
    w_o_dn_full = jnp.concatenate([allw[k][:R1] for k in chips], axis=0)
    w_o_dil_full = jnp.concatenate([allw[k][R1:R2].reshape(DIL_W, qw) for k in chips], axis=1)
    w_out_full = jnp.concatenate([allw[k][R2:R3] for k in chips], axis=0)
    conv_full = jnp.concatenate(
        [(allw[k][R3:R3 + ROWS_CONV].astype(F32) + allw[k][R4:R4 + ROWS_CONV].astype(F32)).reshape(4, 3 * DN_W // N_CHIPS)
         for k in chips], axis=1)

    finish, dwpt, dconv, dw_o_dn, dw_o_dil, dw_out = _local_step(
        x[0], loss_target[0], norm_w, wpt, conv_full, a_log, dt_bias, dn_norm_w, w_o_dn_full, w_o_dil_full, w_out_full,
        final_norm_w.reshape(1, D_MODEL))

    cq = 3 * DN_W // N_CHIPS
    gpack = jnp.stack([
        jnp.concatenate(
            [dw_o_dn[k * qw:(k + 1) * qw], dw_o_dil[:, k * qw:(k + 1) * qw].reshape(ROWS_O_DIL, PACK_W),
             dw_out[k * qw:(k + 1) * qw],
             jnp.pad(dconv[:, k * cq:(k + 1) * cq].reshape(ROWS_CONV, PACK_W), ((0, 16 - ROWS_CONV), (0, 0))),
             dwpt[OFF_BA:OFF_BA + 2 * DN_HEADS] if k == 1 else jnp.zeros((2 * DN_HEADS, PACK_W), F32),
             jnp.zeros((PACK_ROWS - R4 - 2 * DN_HEADS, PACK_W), F32)], axis=0)
        for k in chips]).reshape(N_CHIPS, 2, HALF_ROWS, PACK_W)
    sib_in, sib_pack = _rs_pair(dwpt, gpack)
    csum_in = _add_halves_win(dwpt, sib_in, c)
    csum_pack = _add_halves(gpack, sib_pack, c, "add_halves_pack")
    send_sems, recv_sems, csums, lands, token = _rs_chips_start([csum_in, csum_pack])
    grad_x, small = finish(after=token)
    (csum_in, csum_pack), (src_in, src_pack) = _rs_chips_wait(send_sems, recv_sems, csums, lands, grad_x)
    half_in = _sum_chips(src_in, csum_in, j, "sum_chips_in")
    half_pack = _sum_chips(src_pack, csum_pack, j, "sum_chips_pack")
    sib_half_in, sib_half_pack = _pair_swap([half_in, half_pack])

    def both_halves(mine, theirs):
        return jnp.where(c == 0, jnp.concatenate([mine, theirs], axis=0), jnp.concatenate([theirs, mine], axis=0))

    g = both_halves(half_pack, sib_half_pack)
    g_w_in = lax.switch(j, [functools.partial(_from_window, k) for k in chips], both_halves(half_in, sib_half_in),
                        g[R4:R4 + 2 * DN_HEADS])
    g_w_o_dn = g[:R1]
    g_w_o_dil = g[R1:R2].reshape(DIL_W, qw)
    g_w_out = g[R2:R3]
    g_conv = g[R3:R3 + ROWS_CONV].reshape(4, cq)

    gs = _sum_blocks(_ag_small(small), 8, "sum_small")
    loss = gs[4, 0]
    w_small = jnp.zeros((8, PACK_W), F32)

    def pack_small(nw, fw, dnw_, al, db):
        t = w_small.at[0].set(nw[0]).at[1].set(fw).at[2, :DN_D].set(dnw_[0])
        return t.at[3, :DN_HEADS].set(al[0]).at[3, DN_HEADS:2 * DN_HEADS].set(db[0])

    sm = _adamw(pack_small(norm_w, final_norm_w, dn_norm_w, a_log, dt_bias), gs,
                pack_small(m_norm_w, m_final_norm_w, m_dn_norm_w, m_a_log, m_dt_bias),
                pack_small(v_norm_w, v_final_norm_w, v_dn_norm_w, v_a_log, v_dt_bias), "adamw_small")

    def unpack_small(t):
        return dict(norm_w=t[0:1], final_norm_w=t[1], dn_norm_w=t[2:3, :DN_D], a_log=t[3:4, :DN_HEADS],
                    dt_bias=t[3:4, DN_HEADS:2 * DN_HEADS])

    res = {"grad": unpack_small(gs)}
    for kind, arr in zip(("delta", "new_m", "new_v"), sm):
        res[kind] = unpack_small(arr)
    big = dict(conv_w=(conv_w, g_conv, m_conv_w, v_conv_w), w_o_dn=(w_o_dn, g_w_o_dn, m_w_o_dn, v_w_o_dn),
               w_o_dil=(w_o_dil, g_w_o_dil, m_w_o_dil, v_w_o_dil), w_out=(w_out, g_w_out, m_w_out, v_w_out))
    for name, (wt, gt, mt, vt) in big.items():
        d, nm, nv = _adamw(wt[0], gt, mt[0], vt[0], "adamw_" + name)
        res["grad"][name] = gt[None]
        res["delta"][name], res["new_m"][name], res["new_v"][name] = d[None], nm[None], nv[None]

    d, nm, nv = _adamw(w_in[0].T, g_w_in, m_w_in[0].T, v_w_in[0].T, "adamw_w_in")
    res["grad"]["w_in"] = g_w_in.T[None]
    res["delta"]["w_in"], res["new_m"]["w_in"], res["new_v"]["w_in"] = d.T[None], nm.T[None], nv.T[None]
    order = ["norm_w", "w_in", "conv_w", "a_log", "dt_bias", "dn_norm_w", "w_o_dn", "w_o_dil", "w_out", "final_norm_w"]
    outs = [loss, grad_x[None]]
    for kind in ("grad", "delta", "new_m", "new_v"):
        outs += [res[kind][nm] for nm in order]
    return tuple(outs)
```

```python
import functools
import math

import jax
import jax.numpy as jnp
from jax import lax
from jax.experimental import pallas as pl
from jax.experimental.pallas import tpu as pltpu

F32 = jnp.float32
BF16 = jnp.bfloat16
MESH = pl.DeviceIdType.MESH

D_MODEL = 1024
DN_HEADS = 8
DN_D = 128
DN_CHUNK = 64
DN_W = DN_HEADS * DN_D
DIL_GROUPS = ((128, 1), (512, 4), (2048, 16))
N_DIL = len(DIL_GROUPS)
DIL_HEADS = 4
DIL_DH = 128
DIL_W = DIL_HEADS * DIL_DH
ATT_BLOCK = 128
NORM_EPS = 1e-6
PROJ_W = 11280
N_CHIPS = 4
SHARD_W = PROJ_W // N_CHIPS

OFF_QKV_A = 0
OFF_Z_A = 3072
OFF_Q_B = 4096
OFF_K_B = 5632
OFF_V_B = 7168
OFF_Z_B = 8704
OFF_G_A = 9216
OFF_G_B = 10240
OFF_BA = 11264
PW = 11520
REF_OFF_BA = 4096

ADAM_LR = 0.001
ADAM_B1 = 0.9
ADAM_B2 = 0.999
ADAM_EPS = 1e-08
ADAM_WD = 0.01
ADAM_STEP = 10

ROW_TILE = 256
NEG = -1e30


def _dot(a, b):
    return jnp.dot(a.astype(BF16), b.astype(BF16), preferred_element_type=F32)


def _dot_nt(a, b):
    return lax.dot_general(a.astype(BF16), b.astype(BF16), (((1,), (1,)), ((), ())), preferred_element_type=F32)


def _dot_tn(a, b):
    return lax.dot_general(a.astype(BF16), b.astype(BF16), (((0,), (0,)), ((), ())), preferred_element_type=F32)


def _split(a):
    hi = a.astype(BF16)
    lo = (a - hi.astype(F32)).astype(BF16)
    return hi, lo


def _dot_exact_lhs(c, a):
    hi, lo = _split(a)
    cb = c.astype(BF16)
    return jnp.dot(cb, hi, preferred_element_type=F32) + jnp.dot(cb, lo, preferred_element_type=F32)


def _dot_exact_rhs(a, c):
    hi, lo = _split(a)
    cb = c.astype(BF16)
    return jnp.dot(hi, cb, preferred_element_type=F32) + jnp.dot(lo, cb, preferred_element_type=F32)


def _dot_tn_exact_rhs(a, c):
    hi, lo = _split(a)
    cb = c.astype(BF16)
    dn = (((0,), (0,)), ((), ()))
    return (lax.dot_general(hi, cb, dn, preferred_element_type=F32)
            + lax.dot_general(lo, cb, dn, preferred_element_type=F32))


def _sigmoid(x):
    return 1.0 / (1.0 + jnp.exp(-x))


def _silu(x):
    return x * _sigmoid(x)


def _silu_grad(x):
    s = _sigmoid(x)
    return s * (1.0 + x * (1.0 - s))


def _softplus(x):
    return jnp.maximum(x, 0.0) + jnp.log(1.0 + jnp.exp(-jnp.abs(x)))


def _cparams(*sem):
    return pltpu.CompilerParams(dimension_semantics=sem)


def _matmul(a, b, out_dtype, tm, tn, tk, name, nt=False, transpose_out=False, after=None):
    m, kdim = a.shape
    n = b.shape[0] if nt else b.shape[1]
    tm, tn, tk = min(tm, m), min(tn, n), min(tk, kdim)
    assert m % tm == 0 and n % tn == 0 and kdim % tk == 0, (name, a.shape, b.shape, tm, tn, tk)
    nk = kdim // tk
    dot = _dot_nt if nt else _dot
    b_spec = (pl.BlockSpec((tn, tk), lambda i, j, k: (j, k)) if nt else pl.BlockSpec((tk, tn), lambda i, j, k: (k, j)))

    def emit(o_ref, acc):
        o_ref[...] = (acc.T if transpose_out else acc).astype(o_ref.dtype)

    if nk == 1:
        def body(a_ref, b_ref, *rest):
            emit(rest[-1], dot(a_ref[...], b_ref[...]))
        scratch = []
    else:
        def body(a_ref, b_ref, *rest):
            o_ref, acc_ref = rest[-2:]
            k = pl.program_id(2)
            p = dot(a_ref[...], b_ref[...])

            @pl.when(k == 0)
            def _():
                acc_ref[...] = p

            @pl.when(k > 0)
            def _():
                acc_ref[...] += p

            @pl.when(k == nk - 1)
            def _():
                emit(o_ref, acc_ref[...])
        scratch = [pltpu.VMEM((tm, tn), F32)]

    if transpose_out:
        out_spec, out_shape = pl.BlockSpec((tn, tm), lambda i, j, k: (j, i)), (n, m)
    else:
        out_spec, out_shape = pl.BlockSpec((tm, tn), lambda i, j, k: (i, j)), (m, n)
    extra = [] if after is None else [after]
    return pl.pallas_call(
        body, name=name, grid=(m // tm, n // tn, nk),
        in_specs=[pl.BlockSpec((tm, tk), lambda i, j, k: (i, k)), b_spec] + [pl.BlockSpec(memory_space=pl.ANY)] * len(extra),
        out_specs=out_spec, out_shape=jax.ShapeDtypeStruct(out_shape, out_dtype), scratch_shapes=scratch,
        compiler_params=_cparams("parallel", "parallel", "arbitrary"))(a, b, *extra)


def _rms_in(x, nw):
    s, d = x.shape

    def body(x_ref, w_ref, h_ref, ht_ref):
        xv = x_ref[...]
        r = lax.rsqrt(jnp.mean(xv * xv, axis=-1, keepdims=True) + NORM_EPS)
        h = xv * r * w_ref[...]
        h_ref[...] = h.astype(BF16)
        ht_ref[...] = h.T.astype(BF16)

    return pl.pallas_call(
        body, name="rms_in", grid=(s // ROW_TILE,),
        in_specs=[pl.BlockSpec((ROW_TILE, d), lambda i: (i, 0)), pl.BlockSpec((1, d), lambda i: (0, 0))],
        out_specs=[pl.BlockSpec((ROW_TILE, d), lambda i: (i, 0)), pl.BlockSpec((d, ROW_TILE), lambda i: (0, i))],
        out_shape=[jax.ShapeDtypeStruct((s, d), BF16), jax.ShapeDtypeStruct((d, s), BF16)],
        compiler_params=_cparams("parallel"))(x, nw)


def _rms_in_bwd(x, nw, dh, dx2):
    s, d = x.shape

    def body(x_ref, w_ref, dh_ref, dx2_ref, dx_ref, dw_ref):
        i = pl.program_id(0)
        xv = x_ref[...]
        r = lax.rsqrt(jnp.mean(xv * xv, axis=-1, keepdims=True) + NORM_EPS)
        dhv = dh_ref[...]
        dyw = dhv * w_ref[...]
        dx_ref[...] = dx2_ref[...] + r * dyw - xv * (r * r * r) * jnp.mean(dyw * xv, axis=-1, keepdims=True)
        part = jnp.sum(dhv * xv * r, axis=0, keepdims=True)

        @pl.when(i == 0)
        def _():
            dw_ref[...] = part

        @pl.when(i > 0)
        def _():
            dw_ref[...] += part

    row = pl.BlockSpec((ROW_TILE, d), lambda i: (i, 0))
    vec = pl.BlockSpec((1, d), lambda i: (0, 0))
    return pl.pallas_call(
        body, name="rms_in_bwd", grid=(s // ROW_TILE,), in_specs=[row, vec, row, row], out_specs=[row, vec],
        out_shape=[jax.ShapeDtypeStruct((s, d), F32), jax.ShapeDtypeStruct((1, d), F32)],
        compiler_params=_cparams("arbitrary"))(x, nw, dh, dx2)


def _shift_down(cur, prev8, k):
    rc = pltpu.roll(cur, k, 0)
    rp = pltpu.roll(prev8, k, 0)
    row = lax.broadcasted_iota(jnp.int32, prev8.shape, 0)
    top = jnp.where(row < k, rp, rc[:8])
    return jnp.concatenate([top, rc[8:]], axis=0)


def _shift_up(cur, next8, k):
    t = cur.shape[0]
    rc = pltpu.roll(cur, t - k, 0)
    rn = pltpu.roll(next8, 8 - k, 0)
    row = lax.broadcasted_iota(jnp.int32, next8.shape, 0)
    bot = jnp.where(row >= 8 - k, rn, rc[t - 8:])
    return jnp.concatenate([rc[:t - 8], bot], axis=0)


def _conv_fwd(proj, conv_w):
    s = proj.shape[0]
    t8 = ROW_TILE // 8

    def body(u_ref, up_ref, w_ref, c_ref, y_ref):
        i = pl.program_id(0)
        part = pl.program_id(1)
        cur = u_ref[...]
        prev8 = jnp.where(i > 0, up_ref[...], 0.0)
        w = w_ref[...]
        c = cur * w[3:4, :]
        for k in (1, 2, 3):
            c = c + _shift_down(cur, prev8, k) * w[3 - k:4 - k, :]
        c_ref[...] = c
        a = _silu(c)
        for h in range(DN_HEADS):
            ah = a[:, h * DN_D:(h + 1) * DN_D]
            r = lax.rsqrt(jnp.sum(ah * ah, axis=-1, keepdims=True) + NORM_EPS)
            y_ref[:, h * DN_D:(h + 1) * DN_D] = jnp.where(part < 2, ah * r, ah)

    return pl.pallas_call(
        body, name="conv_fwd", grid=(s // ROW_TILE, 3),
        in_specs=[pl.BlockSpec((ROW_TILE, DN_W), lambda i, p: (i, p)),
                  pl.BlockSpec((8, DN_W), lambda i, p: (jnp.maximum(i * t8 - 1, 0), p)),
                  pl.BlockSpec((4, DN_W), lambda i, p: (0, p))],
        out_specs=[pl.BlockSpec((ROW_TILE, DN_W), lambda i, p: (i, p))] * 2,
        out_shape=[jax.ShapeDtypeStruct((s, 3 * DN_W), F32)] * 2,
        compiler_params=_cparams("parallel", "parallel"))(proj, proj, conv_w)


def _conv_bwd_act(c, dq, dk, dv):
    s = c.shape[0]

    def body(c_ref, dq_ref, dk_ref, dv_ref, dc_ref):
        for part, d_ref in enumerate((dq_ref, dk_ref, dv_ref)):
            for h in range(DN_HEADS):
                sl = slice(part * DN_W + h * DN_D, part * DN_W + (h + 1) * DN_D)
                ch = c_ref[:, sl]
                dyh = d_ref[:, h * DN_D:(h + 1) * DN_D]
                if part < 2:
                    ah = _silu(ch)
                    r = lax.rsqrt(jnp.sum(ah * ah, axis=-1, keepdims=True) + NORM_EPS)
                    dyh = r * dyh - ah * (r * r * r) * jnp.sum(dyh * ah, axis=-1, keepdims=True)
                dc_ref[:, sl] = dyh * _silu_grad(ch)

    wide = pl.BlockSpec((ROW_TILE, 3 * DN_W), lambda i: (i, 0))
    row = pl.BlockSpec((ROW_TILE, DN_W), lambda i: (i, 0))
    return pl.pallas_call(
        body, name="conv_bwd_act", grid=(s // ROW_TILE,), in_specs=[wide, row, row, row], out_specs=wide,
        out_shape=jax.ShapeDtypeStruct((s, 3 * DN_W), F32), compiler_params=_cparams("parallel"))(c, dq, dk, dv)


def _conv_bwd(proj, dc, conv_w):
    s = proj.shape[0]
    t8 = ROW_TILE // 8
    nrow = s // ROW_TILE
    last8 = s // 8 - 1

    def body(u_ref, up_ref, dc_ref, dcn_ref, w_ref, du_ref, dw_ref):
        i = pl.program_id(1)
        cur = u_ref[...]
        prev8 = jnp.where(i > 0, up_ref[...], 0.0)
        dcv = dc_ref[...]
        next8 = jnp.where(i < nrow - 1, dcn_ref[...], 0.0)
        w = w_ref[...]
        du = dcv * w[3:4, :]
        for k in (1, 2, 3):
            du = du + _shift_up(dcv, next8, k) * w[3 - k:4 - k, :]
        du_ref[...] = du.astype(BF16)

        @pl.when(i == 0)
        def _():
            dw_ref[...] = jnp.zeros_like(dw_ref)

        dw_ref[3:4, :] += jnp.sum(cur * dcv, axis=0, keepdims=True)
        for k in (1, 2, 3):
            dw_ref[3 - k:4 - k, :] += jnp.sum(_shift_down(cur, prev8, k) * dcv, axis=0, keepdims=True)

    blk = pl.BlockSpec((ROW_TILE, DN_W), lambda p, i: (i, p))
    return pl.pallas_call(
        body, name="conv_bwd", grid=(3, nrow),
        in_specs=[blk, pl.BlockSpec((8, DN_W), lambda p, i: (jnp.maximum(i * t8 - 1, 0), p)),
                  blk, pl.BlockSpec((8, DN_W), lambda p, i: (jnp.minimum((i + 1) * t8, last8), p)),
                  pl.BlockSpec((4, DN_W), lambda p, i: (0, p))],
        out_specs=[blk, pl.BlockSpec((4, DN_W), lambda p, i: (0, p))],
        out_shape=[jax.ShapeDtypeStruct((s, 3 * DN_W), BF16), jax.ShapeDtypeStruct((4, 3 * DN_W), F32)],
        compiler_params=_cparams("parallel", "arbitrary"))(proj, proj, dc, dc, conv_w)


def _gates_fwd(proj, gate_par):
    s = proj.shape[0]

    def body(ba_ref, par_ref, o_ref):
        v = ba_ref[...]
        lane = lax.broadcasted_iota(jnp.int32, v.shape, 1)
        beta = _sigmoid(v)
        g = -jnp.exp(par_ref[0:1, :]) * _softplus(v + par_ref[1:2, :])
        o_ref[...] = jnp.where(lane < DN_HEADS, beta, jnp.where(lane < 2 * DN_HEADS, g, 0.0))

    return pl.pallas_call(
        body, name="gates_fwd", grid=(s // ROW_TILE,),
        in_specs=[pl.BlockSpec((ROW_TILE, 128), lambda i: (i, OFF_BA // 128)), pl.BlockSpec((8, 128), lambda i: (0, 0))],
        out_specs=pl.BlockSpec((ROW_TILE, 128), lambda i: (i, 0)),
        out_shape=jax.ShapeDtypeStruct((s, 128), F32), compiler_params=_cparams("parallel"))(proj, gate_par)


def _gates_bwd(proj, gate_par, dbg):
    s = proj.shape[0]

    def body(ba_ref, par_ref, d_ref, o_ref, dpar_ref):
        i = pl.program_id(0)
        v = ba_ref[...]
        dv = d_ref[...]
        lane = lax.broadcasted_iota(jnp.int32, v.shape, 1)
        beta = _sigmoid(v)
        nega = -jnp.exp(par_ref[0:1, :])
        xs = v + par_ref[1:2, :]
        dsp = dv * nega * _sigmoid(xs)
        dal = dv * nega * _softplus(xs)
        is_b = lane < DN_HEADS
        is_g = jnp.logical_and(lane >= DN_HEADS, lane < 2 * DN_HEADS)
        o_ref[...] = jnp.where(is_b, dv * beta * (1.0 - beta), jnp.where(is_g, dsp, 0.0)).astype(BF16)
        r0 = jnp.sum(jnp.where(is_g, dal, 0.0), axis=0, keepdims=True)
        r1 = jnp.sum(jnp.where(is_g, dsp, 0.0), axis=0, keepdims=True)

        @pl.when(i == 0)
        def _():
            dpar_ref[...] = jnp.zeros_like(dpar_ref)

        dpar_ref[0:1, :] += r0
        dpar_ref[1:2, :] += r1

    return pl.pallas_call(
        body, name="gates_bwd", grid=(s // ROW_TILE,),
        in_specs=[pl.BlockSpec((ROW_TILE, 128), lambda i: (i, OFF_BA // 128)), pl.BlockSpec((8, 128), lambda i: (0, 0)),
                  pl.BlockSpec((ROW_TILE, 128), lambda i: (i, 0))],
        out_specs=[pl.BlockSpec((ROW_TILE, 128), lambda i: (i, 0)), pl.BlockSpec((8, 128), lambda i: (0, 0))],
        out_shape=[jax.ShapeDtypeStruct((s, 128), BF16), jax.ShapeDtypeStruct((8, 128), F32)],
        compiler_params=_cparams("arbitrary"))(proj, gate_par, dbg)


def _chunk_masks():
    c = DN_CHUNK
    ii = lax.broadcasted_iota(jnp.int32, (c, c), 0)
    jj = lax.broadcasted_iota(jnp.int32, (c, c), 1)
    return dict(ii=ii, jj=jj, lower=(ii >= jj), strict=(ii > jj), eye=(ii == jj),
                lower_f=(ii >= jj).astype(BF16), upper_f=(ii <= jj).astype(BF16), ones8=jnp.ones((8, c), BF16))


class _Heads:
    def __init__(self, xs):
        self.xs = list(xs)

    def _bin(self, o, f):
        if isinstance(o, _Heads):
            return _Heads([f(a, b) for a, b in zip(self.xs, o.xs)])
        return _Heads([f(a, o) for a in self.xs])

    def __add__(self, o):
        return self._bin(o, lambda a, b: a + b)

    def __sub__(self, o):
        return self._bin(o, lambda a, b: a - b)

    def __mul__(self, o):
        return self._bin(o, lambda a, b: a * b)

    __radd__ = __add__
    __rmul__ = __mul__

    def __neg__(self):
        return _Heads([-a for a in self.xs])

    def __getitem__(self, i):
        return _Heads([a[i] for a in self.xs])


def _hmap(f, *args):
    n = next(len(a.xs) for a in args if isinstance(a, _Heads))
    return _Heads([f(*[(a.xs[h] if isinstance(a, _Heads) else a) for a in args]) for h in range(n)])


def _hdot(a, b):
    return _hmap(_dot, a, b)


def _hdot_nt(a, b):
    return _hmap(_dot_nt, a, b)


def _hdot_tn(a, b):
    return _hmap(_dot_tn, a, b)


def _hcat(a, b, axis):
    return _hmap(lambda x, y: jnp.concatenate([x, y], axis=axis), a, b)


def _hsum(a, axis):
    return _hmap(lambda t: jnp.sum(t, axis=axis, keepdims=True), a)


def _hwhere(c, a, b):
    return _hmap(jnp.where, c, a, b)


def _chunk_gates(mk, bg):
    c = DN_CHUNK
    gc_all = _dot_exact_lhs(mk["lower_f"], bg)
    rows = jnp.concatenate([gc_all, gc_all], axis=0).T
    hs = range(DN_HEADS)
    return (_Heads(bg[:, h:h + 1] for h in hs), _Heads(gc_all[:, DN_HEADS + h:DN_HEADS + h + 1] for h in hs),
            _Heads(rows[DN_HEADS + h:DN_HEADS + h + 1, :] for h in hs))


def _chunk_common(mk, q, k, beta_col, gc_col, gc_r):
    c = DN_CHUNK
    lower, strict = mk["lower"], mk["strict"]
    qs = q * (DN_D ** -0.5)
    beta_b = _hmap(lambda t: jnp.broadcast_to(t, (c, DN_D)), beta_col)
    gc_b = _hmap(lambda t: jnp.broadcast_to(t, (c, DN_D)), gc_col)
    gc_sq = gc_b[:, :c]
    gam = _hwhere(lower, _hmap(lambda t: jnp.exp(jnp.minimum(t, 0.0)), gc_sq - gc_r[:, :c]), 0.0)
    egc = _hmap(jnp.exp, gc_b)
    gl = gc_b[c - 1:c, :]
    ekd = _hmap(jnp.exp, gl - gc_b)
    dl = _hmap(jnp.exp, gl)
    kb = k * beta_b
    scores = _hdot_nt(_hcat(kb, qs, 0), k)
    a_strict = _hwhere(strict, scores[:c] * gam, 0.0)
    aqk = _hwhere(lower, scores[c:] * gam, 0.0)
    return dict(k=k, qs=qs, beta_b=beta_b, gc_b=gc_b, gam=gam, egc=egc, ekd=ekd, dl=dl, kb=kb, a_strict=a_strict, aqk=aqk)


def _unit_lower_inverse_minus_eye(n_strict, ii, jj):
    same = lax.shift_right_logical(ii, 4) == lax.shift_right_logical(jj, 4)
    dmat = _hwhere(same, n_strict, 0.0)
    omat = n_strict - dmat
    d2 = _hdot(dmat, dmat)
    d4 = _hdot(d2, d2)
    d8 = _hdot(d4, d4)
    x1 = d2 - dmat - _hdot(dmat, d2)
    x2 = x1 + d4 + _hdot(x1, d4)
    x3 = x2 + d8 + _hdot(x2, d8)
    n1 = omat + _hdot(x3, omat)
    n2 = _hdot(n1, n1)
    y = n2 - n1 - _hdot(n1, n2)
    return y + x3 + _hdot(y, x3)


def _gdr_fwd(qkv, bg):
    s = qkv.shape[0]
    c = DN_CHUNK
    n = s // c

    def body(q_ref, k_ref, v_ref, bg_ref, o_ref, u_ref, w_ref, vn_ref, tm_ref, st_ref, state):
        @pl.when(pl.program_id(0) == 0)
        def _():
            state[...] = jnp.zeros_like(state)

        mk = _chunk_masks()
        bg = bg_ref[...]
        hs = range(DN_HEADS)
        sls = [slice(h * DN_D, (h + 1) * DN_D) for h in hs]
        cm = _chunk_common(mk, _Heads(q_ref[:, sl] for sl in sls), _Heads(k_ref[:, sl] for sl in sls),
                           *_chunk_gates(mk, bg))
        tm = _unit_lower_inverse_minus_eye(cm["a_strict"], mk["ii"], mk["jj"])
        rhs_u = _Heads(v_ref[:, sl] for sl in sls) * cm["beta_b"]
        rhs_w = cm["kb"] * cm["egc"]
        t_rhs = _hdot(tm, _hcat(rhs_u, rhs_w, 1))
        u = rhs_u + t_rhs[:, :DN_D]
        w = rhs_w + t_rhs[:, DN_D:]
        st = _Heads(state[h] for h in hs)
        on_state = _hdot(_hcat(w, cm["qs"] * cm["egc"], 0), st)
        v_new = u - on_state[:c]
        o = on_state[c:] + _hdot(cm["aqk"], v_new)
        st_new = st * cm["dl"] + _hdot_tn(cm["k"] * cm["ekd"], v_new)
        for h, sl in zip(hs, sls):
            o_ref[:, sl] = o.xs[h]
            u_ref[:, sl] = u.xs[h]
            w_ref[:, sl] = w.xs[h]
            vn_ref[:, sl] = v_new.xs[h]
            tm_ref[h, 0] = tm.xs[h]
            st_ref[h, 0] = st.xs[h]
            state[h] = st_new.xs[h]

    def part(p):
        return pl.BlockSpec((c, DN_W), lambda j: (j, p))

    return pl.pallas_call(
        body, name="gdr_fwd", grid=(n,),
        in_specs=[part(0), part(1), part(2), pl.BlockSpec((c, 128), lambda j: (j, 0))],
        out_specs=[part(0)] * 4 + [pl.BlockSpec((DN_HEADS, 1, c, c), lambda j: (0, j, 0, 0)),
                                   pl.BlockSpec((DN_HEADS, 1, DN_D, DN_D), lambda j: (0, j, 0, 0))],
        out_shape=[jax.ShapeDtypeStruct((s, DN_W), F32)] * 4
        + [jax.ShapeDtypeStruct((DN_HEADS, n, c, c), F32), jax.ShapeDtypeStruct((DN_HEADS, n, DN_D, DN_D), F32)],
        scratch_shapes=[pltpu.VMEM((DN_HEADS, DN_D, DN_D), F32)],
        compiler_params=_cparams("arbitrary"))(qkv, qkv, qkv, bg)


def _gdr_bwd(qkv, bg, u, w, vn, tmat, states, do):
    s = qkv.shape[0]
    c = DN_CHUNK
    n = s // c

    def body(q_ref, k_ref, v_ref, bg_ref, u_ref, w_ref, vn_ref, tm_ref, st_ref, do_ref,
             dq_ref, dk_ref, dv_ref, dbg_ref, dstate):
        @pl.when(pl.program_id(0) == 0)
        def _():
            dstate[...] = jnp.zeros_like(dstate)

        mk = _chunk_masks()
        lower, strict = mk["lower"], mk["strict"]
        bg = bg_ref[...]
        ones = jnp.ones((c, DN_D), BF16)
        rowi = lax.broadcasted_iota(jnp.int32, (c, DN_D), 0)
        lane = lax.broadcasted_iota(jnp.int32, (c, 128), 1)
        hs = range(DN_HEADS)
        sls = [slice(h * DN_D, (h + 1) * DN_D) for h in hs]

        def heads_of(ref):
            return _Heads(ref[:, sl] for sl in sls)

        cm = _chunk_common(mk, heads_of(q_ref), heads_of(k_ref), *_chunk_gates(mk, bg))
        k, qs, beta_b = cm["k"], cm["qs"], cm["beta_b"]
        gam, egc, ekd, dl, kb = cm["gam"], cm["egc"], cm["ekd"], cm["dl"], cm["kb"]
        aqk, a_strict = cm["aqk"], cm["a_strict"]
        v, uu, ww, v_new, dov = heads_of(v_ref), heads_of(u_ref), heads_of(w_ref), heads_of(vn_ref), heads_of(do_ref)
        st = _Heads(st_ref[h, 0] for h in hs)
        dsn = _Heads(dstate[h] for h in hs)
        qd = qs * egc
        kd = k * ekd

        dv_new = _hdot_tn(aqk, dov) + _hdot(kd, dsn)
        do_sv = _hdot_nt(dov, _hcat(st, v_new, 0))
        dqd = do_sv[:, :DN_D]
        daqk = _hwhere(lower, do_sv[:, DN_D:], 0.0)
        dkd = _hdot_nt(v_new, dsn)
        ddl = _hsum(_hsum(dsn * st, 1), 0)
        dw = -_hdot_nt(dv_new, st)
        ds_new = dsn * dl + _hdot_tn(_hcat(qd, -ww, 0), _hcat(dov, dv_new, 0))

        tm = _Heads(tm_ref[h, 0] for h in hs)
        tt = _hdot_tn(tm, _hcat(dv_new, dw, 1))
        dru = dv_new + tt[:, :DN_D]
        drw = dw + tt[:, DN_D:]
        dn = _hwhere(strict, -_hdot_nt(_hcat(dru, drw, 1), _hcat(uu, ww, 1)), 0.0)
        dag = dn * gam
        dqg = daqk * gam
        both = _hcat(dag, dqg, 0)
        on_k = _hdot(both, k)
        dkb = on_k[:c] + drw * egc
        dqs = on_k[c:] + dqd * egc
        dk = _hdot_tn(both, _hcat(kb, qs, 0)) + dkb * beta_b + dkd * ekd
        pmat = dn * a_strict + daqk * aqk
        tkd = _hsum(dkd * kd, -1)
        dgc = (_hsum(pmat, -1) - _hmap(_dot_tn_exact_rhs, pmat, ones) + _hsum(drw * (kb * egc), -1)
               + _hsum(dqd * qd, -1) - tkd)
        last = _hsum(tkd, 0) + ddl * dl
        dgc = dgc + _hwhere(rowi == c - 1, last, 0.0)
        dbeta = _hsum(dru * v, -1) + _hsum(dkb * k, -1)
        dq = dqs * (DN_D ** -0.5)
        dv = dru * beta_b

        dgc_all = jnp.zeros((c, 128), F32)
        dbg = jnp.zeros((c, 128), F32)
        for h, sl in zip(hs, sls):
            dq_ref[:, sl] = dq.xs[h]
            dk_ref[:, sl] = dk.xs[h]
            dv_ref[:, sl] = dv.xs[h]
            dstate[h] = ds_new.xs[h]
            dgc_all = dgc_all + jnp.where(lane == DN_HEADS + h, dgc.xs[h], 0.0)
            dbg = dbg + jnp.where(lane == h, dbeta.xs[h], 0.0)
        dbg_ref[...] = dbg + _dot_exact_lhs(mk["upper_f"], dgc_all)

    def part(p):
        return pl.BlockSpec((c, DN_W), lambda j: (n - 1 - j, p))

    vec = pl.BlockSpec((c, 128), lambda j: (n - 1 - j, 0))
    return pl.pallas_call(
        body, name="gdr_bwd", grid=(n,),
        in_specs=[part(0), part(1), part(2), vec, part(0), part(0), part(0),
                  pl.BlockSpec((DN_HEADS, 1, c, c), lambda j: (0, n - 1 - j, 0, 0)),
                  pl.BlockSpec((DN_HEADS, 1, DN_D, DN_D), lambda j: (0, n - 1 - j, 0, 0)), part(0)],
        out_specs=[part(0), part(0), part(0), vec],
        out_shape=[jax.ShapeDtypeStruct((s, DN_W), F32)] * 3 + [jax.ShapeDtypeStruct((s, 128), F32)],
        scratch_shapes=[pltpu.VMEM((DN_HEADS, DN_D, DN_D), F32)],
        compiler_params=_cparams("arbitrary"))(qkv, qkv, qkv, bg, u, w, vn, tmat, states, do)


def _gdr_out(o, proj, dnw):
    s = o.shape[0]

    def body(o_ref, z_ref, w_ref, y_ref, yt_ref):
        ov, zv, wv = o_ref[...], z_ref[...], w_ref[...]
        for h in range(DN_HEADS):
            sl = slice(h * DN_D, (h + 1) * DN_D)
            oh = ov[:, sl]
            r = lax.rsqrt(jnp.mean(oh * oh, axis=-1, keepdims=True) + NORM_EPS)
            y = (oh * r * wv) * _silu(zv[:, sl])
            y_ref[:, sl] = y.astype(BF16)
            yt_ref[sl, :] = y.T.astype(BF16)

    row = pl.BlockSpec((ROW_TILE, DN_W), lambda i: (i, 0))
    return pl.pallas_call(
        body, name="gdr_out", grid=(s // ROW_TILE,),
        in_specs=[row, pl.BlockSpec((ROW_TILE, DN_W), lambda i: (i, OFF_Z_A // DN_W)), pl.BlockSpec((1, DN_D), lambda i: (0, 0))],
        out_specs=[row, pl.BlockSpec((DN_W, ROW_TILE), lambda i: (0, i))],
        out_shape=[jax.ShapeDtypeStruct((s, DN_W), BF16), jax.ShapeDtypeStruct((DN_W, s), BF16)],
        compiler_params=_cparams("parallel"))(o, proj, dnw)


def _gdr_out_bwd(o, proj, dnw, dy):
    s = o.shape[0]

    def body(o_ref, z_ref, w_ref, dy_ref, do_ref, dz_ref, dw_ref):
        i = pl.program_id(0)
        ov, zv, wv, dyv = o_ref[...], z_ref[...], w_ref[...], dy_ref[...]
        acc = jnp.zeros((1, DN_D), F32)
        for h in range(DN_HEADS):
            sl = slice(h * DN_D, (h + 1) * DN_D)
            oh, zh, dh = ov[:, sl], zv[:, sl], dyv[:, sl]
            r = lax.rsqrt(jnp.mean(oh * oh, axis=-1, keepdims=True) + NORM_EPS)
            dn = dh * _silu(zh)
            dz_ref[:, sl] = (dh * (oh * r * wv) * _silu_grad(zh)).astype(BF16)
            acc = acc + jnp.sum(dn * oh * r, axis=0, keepdims=True)
            dnw_ = dn * wv
            do_ref[:, sl] = r * dnw_ - oh * (r * r * r) * jnp.mean(dnw_ * oh, axis=-1, keepdims=True)

        @pl.when(i == 0)
        def _():
            dw_ref[...] = acc

        @pl.when(i > 0)
        def _():
            dw_ref[...] += acc

    row = pl.BlockSpec((ROW_TILE, DN_W), lambda i: (i, 0))
    vec = pl.BlockSpec((1, DN_D), lambda i: (0, 0))
    return pl.pallas_call(
        body, name="gdr_out_bwd", grid=(s // ROW_TILE,),
        in_specs=[row, pl.BlockSpec((ROW_TILE, DN_W), lambda i: (i, OFF_Z_A // DN_W)), vec, row],
        out_specs=[row, row, vec],
        out_shape=[jax.ShapeDtypeStruct((s, DN_W), F32), jax.ShapeDtypeStruct((s, DN_W), BF16),
                   jax.ShapeDtypeStruct((1, DN_D), F32)],
        compiler_params=_cparams("arbitrary"))(o, proj, dnw, dy)


def _slope(group, head):
    idx = (group * DIL_HEADS + head + 1).astype(F32)
    return jnp.exp(jnp.full((1, 128), -8.0 * math.log(2.0) / (N_DIL * DIL_HEADS), F32) * idx)


def _att_scores(qb, k_cur, k_prev, slope_d, has_prev):
    iq = lax.broadcasted_iota(jnp.int32, (ATT_BLOCK, ATT_BLOCK), 0)
    jk = lax.broadcasted_iota(jnp.int32, (ATT_BLOCK, ATT_BLOCK), 1)
    dist_c = (iq - jk).astype(F32)
    s_cur = jnp.where(iq >= jk, _dot_nt(qb, k_cur) - slope_d * dist_c, NEG)
    s_prev = jnp.where(jnp.logical_and(jk >= iq, has_prev),
                       _dot_nt(qb, k_prev) - slope_d * (dist_c + float(ATT_BLOCK)), NEG)
    return s_cur, s_prev


def _att_scores_whole(qb, k, slope_d):
    n = 2 * ATT_BLOCK
    dist = lax.broadcasted_iota(jnp.int32, (n, n), 0) - lax.broadcasted_iota(jnp.int32, (n, n), 1)
    valid = jnp.logical_and(dist >= 0, dist <= ATT_BLOCK)
    return jnp.where(valid, _dot_nt(qb, k) - slope_d[:, 0:1] * dist.astype(F32), NEG)


ATT_UNROLL = 4


def _att_blocks(i, dil, nb):
    per = dil * nb // ATT_UNROLL
    assert per * ATT_UNROLL == dil * nb
    for i0 in range(per):
        blocks = [divmod(i0 + u * per, nb) for u in range(ATT_UNROLL)]
        assert all(a[0] != b[0] or abs(a[1] - b[1]) >= 2 for n, a in enumerate(blocks) for b in blocks[n + 1:])
    curs, prvs, has_prev = [], [], []
    for u in range(ATT_UNROLL):
        t = i + u * per
        r = lax.div(t, nb)
        j = lax.rem(t, nb)
        base = r + dil * ATT_BLOCK * j
        pbase = base - dil * ATT_BLOCK * jnp.minimum(j, 1)
        if dil == 1:
            base, pbase = pl.multiple_of(base, ATT_BLOCK), pl.multiple_of(pbase, ATT_BLOCK)
        curs.append(pl.ds(base, ATT_BLOCK, stride=dil))
        prvs.append(pl.ds(pbase, ATT_BLOCK, stride=dil))
        has_prev.append(j > 0)
    return curs, prvs, has_prev


def _att_fwd(proj, group):
    s = proj.shape[0]
    dil = DIL_GROUPS[group][1]
    assert DIL_GROUPS[group][0] // dil == ATT_BLOCK
    nb = s // dil // ATT_BLOCK
    assert nb * dil * ATT_BLOCK == s

    def body(q_ref, k_ref, v_ref, num_ref, den_ref, mx_ref):
        slope_d = _slope(group, pl.program_id(0)) * float(dil)

        def step(i, carry):
            curs, prvs, has_prev = _att_blocks(i, dil, nb)
            us = range(ATT_UNROLL)
            qb = [q_ref[c, :] * (DIL_DH ** -0.5) for c in curs]
            sc = [_att_scores(qb[u], k_ref[curs[u], :], k_ref[prvs[u], :], slope_d, has_prev[u]) for u in us]
            mx = [jnp.maximum(jnp.max(a, axis=-1, keepdims=True), jnp.max(b, axis=-1, keepdims=True)) for a, b in sc]
            p_cur = [jnp.exp(sc[u][0] - mx[u]) for u in us]
            p_prev = [jnp.exp(sc[u][1] - mx[u]) for u in us]
            den = [jnp.sum(p_cur[u], axis=-1, keepdims=True) + jnp.sum(p_prev[u], axis=-1, keepdims=True) for u in us]
            num = [_dot(p_cur[u], v_ref[curs[u], :]) + _dot(p_prev[u], v_ref[prvs[u], :]) for u in us]
            for u in us:
                num_ref[curs[u], :] = num[u]
                den_ref[curs[u], :] = jnp.broadcast_to(den[u], (ATT_BLOCK, DIL_DH))
                mx_ref[curs[u], :] = jnp.broadcast_to(mx[u], (ATT_BLOCK, DIL_DH))
            return carry

        def step_whole(i, carry):
            rows = [pl.ds(i * ATT_UNROLL + u, 2 * ATT_BLOCK, stride=dil) for u in range(ATT_UNROLL)]
            sc = [_att_scores_whole(q_ref[r, :] * (DIL_DH ** -0.5), k_ref[r, :], slope_d) for r in rows]
            mx = [jnp.max(a, axis=-1, keepdims=True) for a in sc]
            p = [jnp.exp(a - m) for a, m in zip(sc, mx)]
            num = [_dot(pu, v_ref[r, :]) for pu, r in zip(p, rows)]
            for u, r in enumerate(rows):
                num_ref[r, :] = num[u]
                den_ref[r, :] = jnp.broadcast_to(jnp.sum(p[u], axis=-1, keepdims=True), (2 * ATT_BLOCK, DIL_DH))
                mx_ref[r, :] = jnp.broadcast_to(mx[u], (2 * ATT_BLOCK, DIL_DH))
            return carry

        if nb == 2:
            lax.fori_loop(0, dil // ATT_UNROLL, step_whole, 0)
        else:
            lax.fori_loop(0, dil * nb // ATT_UNROLL, step, 0)

    def col(off):
        return pl.BlockSpec((s, DIL_DH), lambda h: (0, off // DIL_DH + group * DIL_HEADS + h))

    out = pl.BlockSpec((s, DIL_DH), lambda h: (0, h))
    return pl.pallas_call(
        body, name=f"att_fwd{group}", grid=(DIL_HEADS,), in_specs=[col(OFF_Q_B), col(OFF_K_B), col(OFF_V_B)],
        out_specs=[out, out, out], out_shape=[jax.ShapeDtypeStruct((s, DIL_W), F32)] * 3,
        compiler_params=_cparams("parallel"))(proj, proj, proj)


def _att_bwd(proj, group, do, lse, delta):
    s = proj.shape[0]
    dil = DIL_GROUPS[group][1]
    nb = s // dil // ATT_BLOCK

    def body(q_ref, k_ref, v_ref, do_ref, lse_ref, dl_ref, dq_ref, dk_ref, dv_ref, dq_acc, dk_acc, dv_acc):
        slope_d = _slope(group, pl.program_id(0)) * float(dil)
        dk_acc[...] = jnp.zeros_like(dk_acc)
        dv_acc[...] = jnp.zeros_like(dv_acc)

        def step(i, carry):
            curs, prvs, has_prev = _att_blocks(i, dil, nb)
            us = range(ATT_UNROLL)
            qb = [q_ref[c, :] * (DIL_DH ** -0.5) for c in curs]
            k_cur, k_prev = [k_ref[c, :] for c in curs], [k_ref[p, :] for p in prvs]
            v_cur, v_prev = [v_ref[c, :] for c in curs], [v_ref[p, :] for p in prvs]
            sc = [_att_scores(qb[u], k_cur[u], k_prev[u], slope_d, has_prev[u]) for u in us]
            lse_b, delta_b, dob = [lse_ref[c, :] for c in curs], [dl_ref[c, :] for c in curs], [do_ref[c, :] for c in curs]
            p_cur = [jnp.exp(sc[u][0] - lse_b[u]) for u in us]
            p_prev = [jnp.exp(sc[u][1] - lse_b[u]) for u in us]
            ds_cur = [p_cur[u] * (_dot_nt(dob[u], v_cur[u]) - delta_b[u]) for u in us]
            ds_prev = [p_prev[u] * (_dot_nt(dob[u], v_prev[u]) - delta_b[u]) for u in us]
            dq = [(_dot(ds_cur[u], k_cur[u]) + _dot(ds_prev[u], k_prev[u])) * (DIL_DH ** -0.5) for u in us]
            dk_c = [_dot_tn(ds_cur[u], qb[u]) for u in us]
            dv_c = [_dot_tn(p_cur[u], dob[u]) for u in us]
            dk_p = [_dot_tn(ds_prev[u], qb[u]) for u in us]
            dv_p = [_dot_tn(p_prev[u], dob[u]) for u in us]
            for u in us:
                dq_acc[curs[u], :] = dq[u]
                dk_acc[curs[u], :] += dk_c[u]
                dv_acc[curs[u], :] += dv_c[u]
            for u in us:
                dk_acc[prvs[u], :] += dk_p[u]
                dv_acc[prvs[u], :] += dv_p[u]
            return carry

        def step_whole(i, carry):
            rows = [pl.ds(i * ATT_UNROLL + u, 2 * ATT_BLOCK, stride=dil) for u in range(ATT_UNROLL)]
            qb = [q_ref[r, :] * (DIL_DH ** -0.5) for r in rows]
            kk, vv, dob = [k_ref[r, :] for r in rows], [v_ref[r, :] for r in rows], [do_ref[r, :] for r in rows]
            sc = [_att_scores_whole(qb[u], kk[u], slope_d) for u in range(ATT_UNROLL)]
            p = [jnp.exp(sc[u] - lse_ref[r, :][:, 0:1]) for u, r in enumerate(rows)]
            ds = [p[u] * (_dot_nt(dob[u], vv[u]) - dl_ref[r, :][:, 0:1]) for u, r in enumerate(rows)]
            dq = [_dot(ds[u], kk[u]) * (DIL_DH ** -0.5) for u in range(ATT_UNROLL)]
            dk = [_dot_tn(ds[u], qb[u]) for u in range(ATT_UNROLL)]
            dv = [_dot_tn(p[u], dob[u]) for u in range(ATT_UNROLL)]
            for u, r in enumerate(rows):
                dq_acc[r, :] = dq[u]
                dk_acc[r, :] = dk[u]
                dv_acc[r, :] = dv[u]
            return carry

        if nb == 2:
            lax.fori_loop(0, dil // ATT_UNROLL, step_whole, 0)
        else:
            lax.fori_loop(0, dil * nb // ATT_UNROLL, step, 0)
        dq_ref[...] = dq_acc[...].astype(BF16)
        dk_ref[...] = dk_acc[...].astype(BF16)
        dv_ref[...] = dv_acc[...].astype(BF16)

    def col(off):
        return pl.BlockSpec((s, DIL_DH), lambda h: (0, off // DIL_DH + group * DIL_HEADS + h))

    hd = pl.BlockSpec((s, DIL_DH), lambda h: (0, h))
    return pl.pallas_call(
        body, name=f"att_bwd{group}", grid=(DIL_HEADS,),
        in_specs=[col(OFF_Q_B), col(OFF_K_B), col(OFF_V_B), hd, hd, hd], out_specs=[hd, hd, hd],
        out_shape=[jax.ShapeDtypeStruct((s, DIL_W), BF16)] * 3,
        scratch_shapes=[pltpu.VMEM((s, DIL_DH), F32)] * 3,
        compiler_params=_cparams("parallel"))(proj, proj, proj, do, lse, delta)


def _att_merge(parts, proj):
    s = proj.shape[0]

    def body(n0, d0, m0, n1, d1, m1, n2, d2, m2, z_ref, ob_ref, o_ref, lse_ref, obt_ref):
        m = jnp.maximum(jnp.maximum(m0[...], m1[...]), m2[...])
        num = jnp.zeros_like(m)
        den = jnp.zeros_like(m)
        for nr, dr, mr in ((n0, d0, m0), (n1, d1, m1), (n2, d2, m2)):
            sc = jnp.exp(mr[...] - m)
            num = num + nr[...] * sc
            den = den + dr[...] * sc
        o = num / den
        o_ref[...] = o
        lse_ref[...] = m + jnp.log(den)
        ob = o * _silu(z_ref[...])
        ob_ref[...] = ob.astype(BF16)
        obt_ref[...] = ob.T.astype(BF16)

    row = pl.BlockSpec((ROW_TILE, DIL_W), lambda i: (i, 0))
    flat = [a for p in parts for a in p]
    return pl.pallas_call(
        body, name="att_merge", grid=(s // ROW_TILE,),
        in_specs=[row] * 9 + [pl.BlockSpec((ROW_TILE, DIL_W), lambda i: (i, OFF_Z_B // DIL_W))],
        out_specs=[row, row, row, pl.BlockSpec((DIL_W, ROW_TILE), lambda i: (0, i))],
        out_shape=[jax.ShapeDtypeStruct((s, DIL_W), BF16), jax.ShapeDtypeStruct((s, DIL_W), F32),
                   jax.ShapeDtypeStruct((s, DIL_W), F32), jax.ShapeDtypeStruct((DIL_W, s), BF16)],
        compiler_params=_cparams("parallel"))(*flat, proj)


def _att_merge_bwd(o, proj, dob):
    s = o.shape[0]

    def body(o_ref, z_ref, d_ref, do_ref, dl_ref, dz_ref):
        ov, zv, dv = o_ref[...], z_ref[...], d_ref[...]
        do = dv * _silu(zv)
        do_ref[...] = do
        dz_ref[...] = (dv * ov * _silu_grad(zv)).astype(BF16)
        for h in range(DIL_HEADS):
            sl = slice(h * DIL_DH, (h + 1) * DIL_DH)
            dl_ref[:, sl] = jnp.broadcast_to(jnp.sum(do[:, sl] * ov[:, sl], axis=-1, keepdims=True), (ROW_TILE, DIL_DH))

    row = pl.BlockSpec((ROW_TILE, DIL_W), lambda i: (i, 0))
    return pl.pallas_call(
        body, name="att_merge_bwd", grid=(s // ROW_TILE,),
        in_specs=[row, pl.BlockSpec((ROW_TILE, DIL_W), lambda i: (i, OFF_Z_B // DIL_W)), row],
        out_specs=[row, row, row],
        out_shape=[jax.ShapeDtypeStruct((s, DIL_W), F32), jax.ShapeDtypeStruct((s, DIL_W), F32),
                   jax.ShapeDtypeStruct((s, DIL_W), BF16)],
        compiler_params=_cparams("parallel"))(o, proj, dob)


def _merge(proj, ya, yb):
    s = proj.shape[0]

    def body(ga_ref, gb_ref, ya_ref, yb_ref, o_ref, ot_ref):
        m = _sigmoid(ga_ref[...]) * ya_ref[...] + _sigmoid(gb_ref[...]) * yb_ref[...]
        o_ref[...] = m.astype(BF16)
        ot_ref[...] = m.T.astype(BF16)

    row = pl.BlockSpec((ROW_TILE, D_MODEL), lambda i: (i, 0))
    return pl.pallas_call(
        body, name="merge", grid=(s // ROW_TILE,),
        in_specs=[pl.BlockSpec((ROW_TILE, D_MODEL), lambda i: (i, OFF_G_A // D_MODEL)),
                  pl.BlockSpec((ROW_TILE, D_MODEL), lambda i: (i, OFF_G_B // D_MODEL)), row, row],
        out_specs=[row, pl.BlockSpec((D_MODEL, ROW_TILE), lambda i: (0, i))],
        out_shape=[jax.ShapeDtypeStruct((s, D_MODEL), BF16), jax.ShapeDtypeStruct((D_MODEL, s), BF16)],
        compiler_params=_cparams("parallel"))(proj, proj, ya, yb)


def _merge_bwd(proj, ya, yb, dm):
    s = proj.shape[0]

    def body(ga_ref, gb_ref, ya_ref, yb_ref, dm_ref, dya_ref, dyb_ref, dga_ref, dgb_ref):
        dmv = dm_ref[...]
        sa, sb = _sigmoid(ga_ref[...]), _sigmoid(gb_ref[...])
        dya_ref[...] = (dmv * sa).astype(BF16)
        dyb_ref[...] = (dmv * sb).astype(BF16)
        dga_ref[...] = (dmv * ya_ref[...] * sa * (1.0 - sa)).astype(BF16)
        dgb_ref[...] = (dmv * yb_ref[...] * sb * (1.0 - sb)).astype(BF16)

    row = pl.BlockSpec((ROW_TILE, D_MODEL), lambda i: (i, 0))
    return pl.pallas_call(
        body, name="merge_bwd", grid=(s // ROW_TILE,),
        in_specs=[pl.BlockSpec((ROW_TILE, D_MODEL), lambda i: (i, OFF_G_A // D_MODEL)),
                  pl.BlockSpec((ROW_TILE, D_MODEL), lambda i: (i, OFF_G_B // D_MODEL)), row, row, row],
        out_specs=[row] * 4, out_shape=[jax.ShapeDtypeStruct((s, D_MODEL), BF16)] * 4,
        compiler_params=_cparams("parallel"))(proj, proj, ya, yb, dm)


def _final(x, t, fw, tgt):
    s, d = x.shape

    def body(x_ref, t_ref, w_ref, y_ref, dx_ref, dw_ref, l_ref):
        i = pl.program_id(0)
        x2 = x_ref[...] + t_ref[...]
        wv = w_ref[...]
        r = lax.rsqrt(jnp.mean(x2 * x2, axis=-1, keepdims=True) + NORM_EPS)
        e = x2 * r * wv - y_ref[...]
        lrow = jnp.mean(e * e, axis=-1, keepdims=True)
        lpart = jnp.broadcast_to(0.5 * jnp.sum(lrow, axis=0, keepdims=True), (1, 128))
        dy = e * (1.0 / d)
        dwp = jnp.sum(dy * x2 * r, axis=0, keepdims=True)
        dyw = dy * wv
        dx_ref[...] = r * dyw - x2 * (r * r * r) * jnp.mean(dyw * x2, axis=-1, keepdims=True)

        @pl.when(i == 0)
        def _():
            dw_ref[...] = dwp
            l_ref[...] = lpart

        @pl.when(i > 0)
        def _():
            dw_ref[...] += dwp
            l_ref[...] += lpart

    row = pl.BlockSpec((ROW_TILE, d), lambda i: (i, 0))
    vec = pl.BlockSpec((1, d), lambda i: (0, 0))
    return pl.pallas_call(
        body, name="final", grid=(s // ROW_TILE,), in_specs=[row, row, vec, row],
        out_specs=[row, vec, pl.BlockSpec((1, 128), lambda i: (0, 0))],
        out_shape=[jax.ShapeDtypeStruct((s, d), F32), jax.ShapeDtypeStruct((1, d), F32), jax.ShapeDtypeStruct((1, 128), F32)],
        compiler_params=_cparams("arbitrary"))(x, t, fw, tgt)


def _adamw(w, g, m, v, name):
    r, c = w.shape
    cap = max(8, (1 << 18) // c)
    divisors = [t for t in range(8, min(r, cap) + 1, 8) if r % t == 0]
    tr = r if r <= 8 else (max(divisors) if divisors else cap)

    def body(w_ref, g_ref, m_ref, v_ref, d_ref, nm_ref, nv_ref):
        gv = g_ref[...]
        mn = ADAM_B1 * m_ref[...] + (1.0 - ADAM_B1) * gv
        vn = ADAM_B2 * v_ref[...] + (1.0 - ADAM_B2) * (gv * gv)
        m_hat = mn / (1.0 - ADAM_B1 ** ADAM_STEP)
        v_hat = vn / (1.0 - ADAM_B2 ** ADAM_STEP)
        d_ref[...] = -ADAM_LR * (m_hat / (jnp.sqrt(v_hat) + ADAM_EPS) + ADAM_WD * w_ref[...])
        nm_ref[...] = mn
        nv_ref[...] = vn

    blk = pl.BlockSpec((tr, c), lambda i: (i, 0))
    return pl.pallas_call(
        body, name=name, grid=(pl.cdiv(r, tr),), in_specs=[blk] * 4, out_specs=[blk] * 3,
        out_shape=[jax.ShapeDtypeStruct((r, c), F32)] * 3, compiler_params=_cparams("parallel"))(w, g, m, v)


HBM_SPEC = pl.BlockSpec(memory_space=pl.ANY)


def _place():
    x, y, c = lax.axis_index("x"), lax.axis_index("y"), lax.axis_index("c")
    chips = [(1 - x, y), (x, 1 - y), (1 - x, 1 - y)]
    return x, y, c, chips


def _ag_weights(packs):
    na = len(packs)
    nsem = 7

    def body(*refs):
        p_refs, out_refs = refs[:na], refs[na:2 * na]
        send_sems, recv_sems = refs[2 * na:]
        x, y, c, _ = _place()
        me, sib, j = (x, y, c), (x, y, 1 - c), 2 * x + y
        xn, yn = (1 - x, y, c), (x, 1 - y, c)
        jx, jy, jd = 2 * (1 - x) + y, 2 * x + (1 - y), 2 * (1 - x) + (1 - y)

        def rc(a, k, src, dst, to):
            return pltpu.make_async_remote_copy(src_ref=src, dst_ref=dst, send_sem=send_sems.at[nsem * a + k],
                                                recv_sem=recv_sems.at[nsem * a + k], device_id=to, device_id_type=MESH)

        sent = []
        for a in range(na):
            mine, land = p_refs[a].at[c], out_refs[a].at[j, c]
            sent += [rc(a, 0, mine, land, xn), rc(a, 1, mine, land, yn)]
        for cp in sent:
            cp.start()
        for a in range(na):
            half = p_refs[a].shape[1] // 2
            top, bottom = pl.ds(0, half), pl.ds(half, half)
            from_x, from_y, from_d = out_refs[a].at[jx, c], out_refs[a].at[jy, c], out_refs[a].at[jd, c]
            rc(a, 0, p_refs[a].at[c], from_x, me).wait_recv()
            later = [rc(a, 2, from_x.at[top], from_x.at[top], yn), rc(a, 4, from_x, from_x, sib)]
            for cp in later:
                cp.start()
            sent += later
            rc(a, 1, p_refs[a].at[c], from_y, me).wait_recv()
            later = [rc(a, 3, from_y.at[bottom], from_y.at[bottom], xn), rc(a, 5, from_y, from_y, sib)]
            for cp in later:
                cp.start()
            sent += later
            rc(a, 2, from_d.at[top], from_d.at[top], me).wait_recv()
            rc(a, 3, from_d.at[bottom], from_d.at[bottom], me).wait_recv()
            cp = rc(a, 6, from_d, from_d, sib)
            cp.start()
            sent.append(cp)
        for a in range(na):
            for k, jj in ((4, jx), (5, jy), (6, jd)):
                rc(a, k, p_refs[a].at[c], out_refs[a].at[jj, 1 - c], me).wait_recv()
        for cp in sent:
            cp.wait_send()

    return pl.pallas_call(
        body, name="ag_weights",
        out_shape=[jax.ShapeDtypeStruct((N_CHIPS,) + p.shape, p.dtype) for p in packs],
        in_specs=[HBM_SPEC] * na, out_specs=[HBM_SPEC] * na,
        scratch_shapes=[pltpu.SemaphoreType.DMA((nsem * na,)), pltpu.SemaphoreType.DMA((nsem * na,))])(*packs)


def _rs_pair(dwpt, gpack):
    n = N_CHIPS
    hw = SHARD_PAD // 2

    def body(d_ref, g_ref, out_d, out_g, send_sems, recv_sems):
        x, y, c, _ = _place()
        sib = (x, y, 1 - c)
        cps = []
        for p in range(n):
            start = pl.multiple_of(WIN_BASE[p] + (1 - c) * hw, TILE_ROWS)
            cps.append(pltpu.make_async_remote_copy(
                src_ref=d_ref.at[pl.ds(start, hw)], dst_ref=out_d.at[p], send_sem=send_sems.at[p],
                recv_sem=recv_sems.at[p], device_id=sib, device_id_type=MESH))
            cps.append(pltpu.make_async_remote_copy(
                src_ref=g_ref.at[p, 1 - c], dst_ref=out_g.at[p], send_sem=send_sems.at[n + p],
                recv_sem=recv_sems.at[n + p], device_id=sib, device_id_type=MESH))
        for cp in cps:
            cp.start()
        for cp in cps:
            cp.wait_recv()
        for cp in cps:
            cp.wait_send()

    return pl.pallas_call(
        body, name="rs_pair",
        out_shape=[jax.ShapeDtypeStruct((n, hw, dwpt.shape[1]), dwpt.dtype),
                   jax.ShapeDtypeStruct((n,) + gpack.shape[2:], gpack.dtype)],
        in_specs=[HBM_SPEC] * 2, out_specs=[HBM_SPEC] * 2,
        scratch_shapes=[pltpu.SemaphoreType.DMA((2 * n,)), pltpu.SemaphoreType.DMA((2 * n,))])(dwpt, gpack)


def _add_halves_win(dwpt, other, c):
    n, rh, wd = other.shape
    tr = _row_tile(rh)

    def body(s_ref, d_ref, o_ref, out_ref):
        out_ref[0] = (d_ref[...] + o_ref[0]).astype(BF16)

    scal = jnp.concatenate([jnp.reshape(c, (1,)).astype(jnp.int32), jnp.asarray(WIN_BASE, jnp.int32)])
    grid_spec = pltpu.PrefetchScalarGridSpec(
        num_scalar_prefetch=1, grid=(n, rh // tr),
        in_specs=[pl.BlockSpec((pl.Element(tr), pl.Element(wd)),
                               lambda p, i, sr: (pl.multiple_of(sr[1 + p] + sr[0] * rh + i * tr, TILE_ROWS), 0)),
                  pl.BlockSpec((1, tr, wd), lambda p, i, sr: (p, i, 0))],
        out_specs=pl.BlockSpec((1, tr, wd), lambda p, i, sr: (p, i, 0)))
    return pl.pallas_call(
        body, name="add_halves_in", grid_spec=grid_spec, out_shape=jax.ShapeDtypeStruct((n, rh, wd), BF16),
        compiler_params=_cparams("parallel", "parallel"))(scal, dwpt, other)


SEM_SPEC = pl.BlockSpec(memory_space=pltpu.SEMAPHORE)
DATAFLOW_EFFECT = pltpu.SideEffectType.DATAFLOW_SIDE_EFFECTING


def _rs_chips_start(csums):
    na = len(csums)

    def body(*refs):
        s_refs, land_refs = refs[:na], refs[na:2 * na]
        send_sems, recv_sems = refs[2 * na], refs[2 * na + 1]
        token = refs[-1]
        x, y, c, chips = _place()
        j = 2 * x + y
        for a in range(na):
            for k, (cx, cy) in enumerate(chips):
                pltpu.make_async_remote_copy(src_ref=s_refs[a].at[2 * cx + cy], dst_ref=land_refs[a].at[j],
                                             send_sem=send_sems.at[3 * a + k], recv_sem=recv_sems.at[3 * a + k],
                                             device_id=(cx, cy, c), device_id_type=MESH).start()
        token[...] = jnp.zeros_like(token)

    hbm = [pltpu.HBM(s.shape, s.dtype) for s in csums]
    args = [pltpu.with_memory_space_constraint(s, pltpu.HBM) for s in csums]
    args += [pltpu.with_memory_space_constraint(lax.empty(s.shape, s.dtype), pltpu.HBM) for s in csums]
    res = pl.pallas_call(
        body, name="rs_chips_start",
        out_shape=(pltpu.SemaphoreType.DMA((3 * na,)), pltpu.SemaphoreType.DMA((3 * na,)), *hbm, *hbm,
                   jax.ShapeDtypeStruct((8, 128), F32)),
        in_specs=[pl.BlockSpec(memory_space=pltpu.HBM)] * (2 * na),
        out_specs=(SEM_SPEC, SEM_SPEC, *[pl.BlockSpec(memory_space=pltpu.HBM)] * (2 * na),
                   pl.BlockSpec(memory_space=pltpu.VMEM)),
        input_output_aliases={i: 2 + i for i in range(2 * na)},
        compiler_params=pltpu.CompilerParams(has_side_effects=DATAFLOW_EFFECT))(*args)
    return res[0], res[1], list(res[2:2 + na]), list(res[2 + na:2 + 2 * na]), res[-1]


def _rs_chips_wait(send_sems, recv_sems, csums, lands, after):
    na = len(csums)

    def body(*refs):
        s_refs, land_refs = refs[:na], refs[na:2 * na]
        send_sems, recv_sems = refs[2 * na], refs[2 * na + 1]
        x, y, c, chips = _place()
        j = 2 * x + y
        for a in range(na):
            for k, (cx, cy) in enumerate(chips):
                cp = pltpu.make_async_remote_copy(src_ref=s_refs[a].at[2 * cx + cy], dst_ref=land_refs[a].at[2 * cx + cy],
                                                  send_sem=send_sems.at[3 * a + k], recv_sem=recv_sems.at[3 * a + k],
                                                  device_id=(cx, cy, c), device_id_type=MESH)
                cp.wait_send()
                cp.wait_recv()

    hbm = [pltpu.HBM(s.shape, s.dtype) for s in csums]
    res = pl.pallas_call(
        body, name="rs_chips_wait", out_shape=(*hbm, *hbm),
        in_specs=[pl.BlockSpec(memory_space=pltpu.HBM)] * (2 * na) + [SEM_SPEC, SEM_SPEC, pl.BlockSpec(memory_space=pl.ANY)],
        out_specs=tuple([pl.BlockSpec(memory_space=pltpu.HBM)] * (2 * na)),
        input_output_aliases={i: i for i in range(2 * na)},
        compiler_params=pltpu.CompilerParams(has_side_effects=DATAFLOW_EFFECT))(*csums, *lands, send_sems, recv_sems, after)
    return list(res[:na]), list(res[na:])


SWAP_CHUNKS = 4


def _pair_swap(halves):
    na = len(halves)

    def body(*refs):
        h_refs, out_refs = refs[:na], refs[na:2 * na]
        send_sems, recv_sems = refs[2 * na:]
        x, y, c, _ = _place()
        cps = []
        for a in range(na):
            rows = h_refs[a].shape[0] // SWAP_CHUNKS
            assert rows * SWAP_CHUNKS == h_refs[a].shape[0]
            for q in range(SWAP_CHUNKS):
                k = SWAP_CHUNKS * a + q
                cps.append(pltpu.make_async_remote_copy(
                    src_ref=h_refs[a].at[pl.ds(q * rows, rows)], dst_ref=out_refs[a].at[pl.ds(q * rows, rows)],
                    send_sem=send_sems.at[k], recv_sem=recv_sems.at[k], device_id=(x, y, 1 - c), device_id_type=MESH))
        for cp in cps:
            cp.start()
        for cp in cps:
            cp.wait_recv()
        for cp in cps:
            cp.wait_send()

    return pl.pallas_call(
        body, name="pair_swap", out_shape=[jax.ShapeDtypeStruct(h.shape, h.dtype) for h in halves],
        in_specs=[HBM_SPEC] * na, out_specs=[HBM_SPEC] * na,
        scratch_shapes=[pltpu.SemaphoreType.DMA((SWAP_CHUNKS * na,)), pltpu.SemaphoreType.DMA((SWAP_CHUNKS * na,))])(*halves)


def _ag_small(v):
    m_per, n = v.shape

    def body(x_ref, out_ref, send_sems, recv_sems, local_sem):
        x, y, c, chips = _place()
        me, sibling = (x, y, c), (x, y, 1 - c)

        def rows(px, py, pc):
            return out_ref.at[pl.ds((4 * px + 2 * py + pc) * m_per, m_per), :]

        def copy(k, block, to, src=None):
            return pltpu.make_async_remote_copy(
                src_ref=rows(*block) if src is None else src, dst_ref=rows(*block), send_sem=send_sems.at[k],
                recv_sem=recv_sems.at[k], device_id=to, device_id_type=MESH)

        mine = pltpu.make_async_copy(x_ref, rows(*me), local_sem)
        mine.start()
        first = [copy(0, me, sibling, src=x_ref)]
        first += [copy(1 + k, me, (*chip, c), src=x_ref) for k, chip in enumerate(chips)]
        for cp in first:
            cp.start()
        passed = [copy(4 + k, (*chip, c), sibling) for k, chip in enumerate(chips)]
        for k, chip in enumerate(chips):
            copy(1 + k, (*chip, c), me).wait_recv()
            passed[k].start()
        copy(0, sibling, me).wait_recv()
        for k, chip in enumerate(chips):
            copy(4 + k, (*chip, 1 - c), me).wait_recv()
        for cp in first + passed:
            cp.wait_send()
        mine.wait()

    return pl.pallas_call(
        body, name="ag_small", out_shape=jax.ShapeDtypeStruct((8 * m_per, n), v.dtype),
        in_specs=[pl.BlockSpec(memory_space=pltpu.VMEM)], out_specs=pl.BlockSpec(memory_space=pltpu.VMEM),
        scratch_shapes=[pltpu.SemaphoreType.DMA((7,)), pltpu.SemaphoreType.DMA((7,)), pltpu.SemaphoreType.DMA])(v)


def _sum_blocks(a, nblk, name):
    rows, wd = a.shape
    r = rows // nblk
    tr = min(r, ROW_TILE)
    assert r % tr == 0

    def body(*refs):
        acc = refs[0][...].astype(F32)
        for ref in refs[1:nblk]:
            acc = acc + ref[...].astype(F32)
        refs[nblk][...] = acc

    nt = r // tr
    return pl.pallas_call(
        body, name=name, grid=(nt,),
        in_specs=[pl.BlockSpec((tr, wd), functools.partial(lambda i, b: (b * nt + i, 0), b=b)) for b in range(nblk)],
        out_specs=pl.BlockSpec((tr, wd), lambda i: (i, 0)),
        out_shape=jax.ShapeDtypeStruct((r, wd), F32), compiler_params=_cparams("parallel"))(*([a] * nblk))


def _row_tile(rows):
    best = max(t for t in range(16, 513, 16) if rows % t == 0)
    return best


def _sum_chips(by_src, csum, j, name):
    n, rh, wd = by_src.shape
    tr = _row_tile(rh)

    def body(j_ref, *refs):
        own = refs[n][0].astype(F32)
        acc = None
        for k in range(n):
            term = jnp.where(j_ref[0] == k, own, refs[k][0].astype(F32))
            acc = term if acc is None else acc + term
        refs[n + 1][...] = acc

    def other(k):
        return pl.BlockSpec((1, tr, wd), lambda i, jr: (jnp.where(jr[0] == k, (k + 1) % n, k), i, 0))

    grid_spec = pltpu.PrefetchScalarGridSpec(
        num_scalar_prefetch=1, grid=(rh // tr,),
        in_specs=[other(k) for k in range(n)] + [pl.BlockSpec((1, tr, wd), lambda i, jr: (jr[0], i, 0))],
        out_specs=pl.BlockSpec((tr, wd), lambda i, jr: (i, 0)))
    return pl.pallas_call(
        body, name=name, grid_spec=grid_spec, out_shape=jax.ShapeDtypeStruct((rh, wd), F32),
        compiler_params=_cparams("parallel"))(jnp.reshape(j, (1,)).astype(jnp.int32), *([by_src] * n), csum)


def _add_halves(gpack, other, c, name):
    n, _, rh, wd = gpack.shape
    tr = _row_tile(rh)

    def body(c_ref, g_ref, o_ref, out_ref):
        out_ref[0] = (g_ref[0, 0] + o_ref[0]).astype(BF16)

    grid_spec = pltpu.PrefetchScalarGridSpec(
        num_scalar_prefetch=1, grid=(n, rh // tr),
        in_specs=[pl.BlockSpec((1, 1, tr, wd), lambda p, i, cr: (p, cr[0], i, 0)),
                  pl.BlockSpec((1, tr, wd), lambda p, i, cr: (p, i, 0))],
        out_specs=pl.BlockSpec((1, tr, wd), lambda p, i, cr: (p, i, 0)))
    return pl.pallas_call(
        body, name=name, grid_spec=grid_spec, out_shape=jax.ShapeDtypeStruct((n, rh, wd), BF16),
        compiler_params=_cparams("parallel", "parallel"))(jnp.reshape(c, (1,)).astype(jnp.int32), gpack, other)


PACK_W = 1024
ROWS_O_DN = DN_W // N_CHIPS
ROWS_O_DIL = DIL_W * (D_MODEL // N_CHIPS) // PACK_W
ROWS_OUT = D_MODEL // N_CHIPS
ROWS_CONV = 4 * (3 * DN_W // N_CHIPS) // PACK_W
R1 = ROWS_O_DN
R2 = R1 + ROWS_O_DIL
R3 = R2 + ROWS_OUT
R4 = R3 + 16
R5 = R4 + 16
PACK_ROWS = 704
HALF_ROWS = PACK_ROWS // 2
SHARD_PAD = 2880


R6 = R5 + 2 * DN_HEADS

TILE_ROWS = 16
BA_IN_SHARD1 = REF_OFF_BA - SHARD_W
LOCAL_START = (0, SHARD_W, 2 * SHARD_W - 2 * DN_HEADS, 3 * SHARD_W - 2 * DN_HEADS)
LOCAL_END = LOCAL_START[1:] + (OFF_BA,)
WIN_BASE = tuple(s // TILE_ROWS * TILE_ROWS for s in LOCAL_START)


def _to_window(k, shard):
    nba = 2 * DN_HEADS
    body = shard
    if k == 1:
        row = lax.broadcasted_iota(jnp.int32, (SHARD_W - nba, 1), 0)
        body = jnp.where(row < BA_IN_SHARD1, shard[:SHARD_W - nba], shard[nba:])
    lead = LOCAL_START[k] - WIN_BASE[k]
    return jnp.pad(body, ((lead, SHARD_PAD - lead - body.shape[0]), (0, 0)))


def _from_window(k, win, ba):
    nba = 2 * DN_HEADS
    lead = LOCAL_START[k] - WIN_BASE[k]
    if k != 1:
        return win[lead:lead + SHARD_W]
    row = lax.broadcasted_iota(jnp.int32, (SHARD_W, 1), 0)
    before = win[lead:lead + SHARD_W]
    after = jnp.pad(win, ((nba, 0), (0, 0)))[lead:lead + SHARD_W]
    mid = jnp.pad(ba, ((BA_IN_SHARD1, SHARD_W - BA_IN_SHARD1 - nba), (0, 0)))
    return jnp.where(row < BA_IN_SHARD1, before, jnp.where(row < BA_IN_SHARD1 + nba, mid, after))


def _stack_windows(wins, ba):
    pieces = []
    for k in range(N_CHIPS):
        lo = WIN_BASE[k] + (TILE_ROWS if k else 0)
        hi = LOCAL_END[k] // TILE_ROWS * TILE_ROWS
        pieces.append(wins[k][lo - WIN_BASE[k]:hi - WIN_BASE[k]])
        if k + 1 < N_CHIPS:
            assert hi == WIN_BASE[k + 1]
            pieces.append(wins[k][hi - WIN_BASE[k]:hi - WIN_BASE[k] + TILE_ROWS] + wins[k + 1][:TILE_ROWS])
    pieces += [ba, jnp.zeros((PW - OFF_BA - ba.shape[0], ba.shape[1]), ba.dtype)]
    out = jnp.concatenate(pieces, axis=0)
    assert out.shape[0] == PW
    return out


def _to_ref_layout(wpt):
    return jnp.concatenate([wpt[:REF_OFF_BA], wpt[OFF_BA:OFF_BA + 2 * DN_HEADS], wpt[REF_OFF_BA:OFF_BA]], axis=0)


def _from_ref_layout(wt):
    pad = jnp.zeros((PW - PROJ_W, wt.shape[1]), wt.dtype)
    return jnp.concatenate([wt[:REF_OFF_BA], wt[REF_OFF_BA + 2 * DN_HEADS:], wt[REF_OFF_BA:REF_OFF_BA + 2 * DN_HEADS], pad],
                           axis=0)


def _local_step(x, tgt, norm_w, wpt, conv_full, a_log, dt_bias, dn_norm_w, w_o_dn, w_o_dil, w_out, final_norm_w):
    s = x.shape[0]
    h, h_t = _rms_in(x, norm_w)
    proj = _matmul(h, wpt, F32, 2048, 1280, 1024, "proj", nt=True)
    c_pre, qkv = _conv_fwd(proj, conv_full)
    gate_par = jnp.zeros((8, 128), F32).at[0, 8:16].set(a_log[0]).at[1, 8:16].set(dt_bias[0])
    bg = _gates_fwd(proj, gate_par)
    o_a, u, w, vn, tmat, states = _gdr_fwd(qkv, bg)
    oa2, oa2_t = _gdr_out(o_a, proj, dn_norm_w)
    ya = _matmul(oa2, w_o_dn, F32, 512, 1024, 1024, "ya")
    parts = [_att_fwd(proj, g) for g in range(N_DIL)]
    ob, o_att, lse, ob_t = _att_merge(parts, proj)
    yb = _matmul(ob, w_o_dil, F32, 512, 1024, 512, "yb")
    mg, mg_t = _merge(proj, ya, yb)
    t = _matmul(mg, w_out, F32, 512, 1024, 1024, "t_out")
    dx2, dfw, lpart = _final(x, t, final_norm_w, tgt)

    dmg = _matmul(dx2, w_out, F32, 512, 1024, 1024, "d_merged", nt=True)
    dw_out = _matmul(mg_t, dx2, F32, 1024, 1024, 1024, "dw_out")
    dya, dyb, dga, dgb = _merge_bwd(proj, ya, yb, dmg)
    doa2 = _matmul(dya, w_o_dn, F32, 512, 1024, 1024, "d_oa2", nt=True)
    dw_o_dn = _matmul(oa2_t, dya, F32, 1024, 1024, 1024, "dw_o_dn")
    dob = _matmul(dyb, w_o_dil, F32, 512, 512, 1024, "d_ob", nt=True)
    dw_o_dil = _matmul(ob_t, dyb, F32, 512, 1024, 1024, "dw_o_dil")
    do_a, dz_a, ddnw = _gdr_out_bwd(o_a, proj, dn_norm_w, doa2)
    dq_a, dk_a, dv_a, dbg = _gdr_bwd(qkv, bg, u, w, vn, tmat, states, do_a)
    dba, dpar = _gates_bwd(proj, gate_par, dbg)
    dc = _conv_bwd_act(c_pre, dq_a, dk_a, dv_a)
    du_a, dconv = _conv_bwd(proj, dc, conv_full)
    do_att, delta, dz_b = _att_merge_bwd(o_att, proj, dob)
    dqkv_b = [_att_bwd(proj, g, do_att, lse, delta) for g in range(N_DIL)]
    dproj = jnp.concatenate(
        [du_a, dz_a] + [dqkv_b[g][i] for i in range(3) for g in range(N_DIL)]
        + [dz_b, dga, dgb, dba, jnp.zeros((s, PW - OFF_BA - 128), BF16)], axis=1)
    dwpt = _matmul(h_t, dproj, F32, 1024, 1280, 2048, "dw_in", transpose_out=True)

    def finish(after=None):
        dh = _matmul(dproj, wpt, F32, 1024, 1024, 3840, "d_h", after=after)
        grad_x, dnw = _rms_in_bwd(x, norm_w, dh, dx2)
        small = jnp.zeros((8, PACK_W), F32)
        small = small.at[0].set(dnw[0]).at[1].set(dfw[0]).at[2, :DN_D].set(ddnw[0])
        small = small.at[3, :DN_HEADS].set(dpar[0, 8:16]).at[3, DN_HEADS:2 * DN_HEADS].set(dpar[1, 8:16])
        small = small.at[4, 0].set(lpart[0, 0])
        return grad_x, small

    return finish, dwpt, dconv, dw_o_dn, dw_o_dil, dw_out


def kernel(x, norm_w, w_in, conv_w, a_log, dt_bias, dn_norm_w, w_o_dn, w_o_dil, w_out, final_norm_w, loss_target, m_norm_w, m_w_in, m_conv_w, m_a_log, m_dt_bias, m_dn_norm_w, m_w_o_dn, m_w_o_dil, m_w_out, m_final_norm_w, v_norm_w, v_w_in, v_conv_w, v_a_log, v_dt_bias, v_dn_norm_w, v_w_o_dn, v_w_o_dil, v_w_out, v_final_norm_w):
    c = lax.axis_index("c")
    j = 2 * lax.axis_index("x") + lax.axis_index("y")
    qw = D_MODEL // N_CHIPS

    cw = conv_w[0].reshape(ROWS_CONV, PACK_W)
    cw = jnp.pad(cw, ((0, 16 - ROWS_CONV), (0, 0)))
    cw_hi = cw.astype(BF16)
    cw_lo = (cw - cw_hi.astype(F32)).astype(BF16)
    shard = w_in[0].T.astype(BF16)
    own_ba = jnp.where(j == 1, shard[BA_IN_SHARD1:BA_IN_SHARD1 + 2 * DN_HEADS], jnp.zeros((2 * DN_HEADS, D_MODEL), BF16))
    pack = jnp.concatenate(
        [w_o_dn[0].astype(BF16), w_o_dil[0].astype(BF16).reshape(ROWS_O_DIL, PACK_W), w_out[0].astype(BF16), cw_hi, cw_lo,
         own_ba, jnp.zeros((PACK_ROWS - R6, PACK_W), BF16)], axis=0).reshape(2, HALF_ROWS, PACK_W)
    chips = range(N_CHIPS)
    own_win = lax.switch(j, [functools.partial(_to_window, k) for k in chips], shard).reshape(2, SHARD_PAD // 2, D_MODEL)
    all_in, allw = _ag_weights([own_win, pack])
    wins = [jnp.where(j == k, own_win, all_in[k]).reshape(SHARD_PAD, D_MODEL) for k in chips]
    allw = [jnp.where(j == k, pack, allw[k]).reshape(PACK_ROWS, PACK_W) for k in chips]
    wpt = _stack_windows(wins, allw[1][R5:R6])
    w_o_dn_full = jnp.concatenate([allw[k][:R1] for k in chips], axis=0)
    w_o_dil_full = jnp.concatenate([allw[k][R1:R2].reshape(DIL_W, qw) for k in chips], axis=1)
    w_out_full = jnp.concatenate([allw[k][R2:R3] for k in chips], axis=0)
    conv_full = jnp.concatenate(
        [(allw[k][R3:R3 + ROWS_CONV].astype(F32) + allw[k][R4:R4 + ROWS_CONV].astype(F32)).reshape(4, 3 * DN_W // N_CHIPS)
         for k in chips], axis=1)

    finish, dwpt, dconv, dw_o_dn, dw_o_dil, dw_out = _local_step(
        x[0], loss_target[0], norm_w, wpt, conv_full, a_log, dt_bias, dn_norm_w, w_o_dn_full, w_o_dil_full, w_out_full,
        final_norm_w.reshape(1, D_MODEL))

    cq = 3 * DN_W // N_CHIPS
    gpack = jnp.stack([
        jnp.concatenate(
            [dw_o_dn[k * qw:(k + 1) * qw], dw_o_dil[:, k * qw:(k + 1) * qw].reshape(ROWS_O_DIL, PACK_W),
             dw_out[k * qw:(k + 1) * qw],
             jnp.pad(dconv[:, k * cq:(k + 1) * cq].reshape(ROWS_CONV, PACK_W), ((0, 16 - ROWS_CONV), (0, 0))),
             dwpt[OFF_BA:OFF_BA + 2 * DN_HEADS] if k == 1 else jnp.zeros((2 * DN_HEADS, PACK_W), F32),
             jnp.zeros((PACK_ROWS - R4 - 2 * DN_HEADS, PACK_W), F32)], axis=0)
        for k in chips]).reshape(N_CHIPS, 2, HALF_ROWS, PACK_W)
    sib_in, sib_pack = _rs_pair(dwpt, gpack)
    csum_in = _add_halves_win(dwpt, sib_in, c)
    csum_pack = _add_halves(gpack, sib_pack, c, "add_halves_pack")
    send_sems, recv_sems, csums, lands, token = _rs_chips_start([csum_in, csum_pack])
    grad_x, small = finish(after=token)
    (csum_in, csum_pack), (src_in, src_pack) = _rs_chips_wait(send_sems, recv_sems, csums, lands, grad_x)
    half_in = _sum_chips(src_in, csum_in, j, "sum_chips_in")
    half_pack = _sum_chips(src_pack, csum_pack, j, "sum_chips_pack")
    sib_half_in, sib_half_pack = _pair_swap([half_in, half_pack])

    def both_halves(mine, theirs):
        return jnp.where(c == 0, jnp.concatenate([mine, theirs], axis=0), jnp.concatenate([theirs, mine], axis=0))

    g = both_halves(half_pack, sib_half_pack)
    g_w_in = lax.switch(j, [functools.partial(_from_window, k) for k in chips], both_halves(half_in, sib_half_in),
                        g[R4:R4 + 2 * DN_HEADS])
    g_w_o_dn = g[:R1]
    g_w_o_dil = g[R1:R2].reshape(DIL_W, qw)
    g_w_out = g[R2:R3]
    g_conv = g[R3:R3 + ROWS_CONV].reshape(4, cq)

    gs = _sum_blocks(_ag_small(small), 8, "sum_small")
    loss = gs[4, 0]
    w_small = jnp.zeros((8, PACK_W), F32)

    def pack_small(nw, fw, dnw_, al, db):
        t = w_small.at[0].set(nw[0]).at[1].set(fw).at[2, :DN_D].set(dnw_[0])
        return t.at[3, :DN_HEADS].set(al[0]).at[3, DN_HEADS:2 * DN_HEADS].set(db[0])

    sm = _adamw(pack_small(norm_w, final_norm_w, dn_norm_w, a_log, dt_bias), gs,
                pack_small(m_norm_w, m_final_norm_w, m_dn_norm_w, m_a_log, m_dt_bias),
                pack_small(v_norm_w, v_final_norm_w, v_dn_norm_w, v_a_log, v_dt_bias), "adamw_small")

    def unpack_small(t):
        return dict(norm_w=t[0:1], final_norm_w=t[1], dn_norm_w=t[2:3, :DN_D], a_log=t[3:4, :DN_HEADS],
                    dt_bias=t[3:4, DN_HEADS:2 * DN_HEADS])

    res = {"grad": unpack_small(gs)}
    for kind, arr in zip(("delta", "new_m", "new_v"), sm):
        res[kind] = unpack_small(arr)
    big = dict(conv_w=(conv_w, g_conv, m_conv_w, v_conv_w), w_o_dn=(w_o_dn, g_w_o_dn, m_w_o_dn, v_w_o_dn),
               w_o_dil=(w_o_dil, g_w_o_dil, m_w_o_dil, v_w_o_dil), w_out=(w_out, g_w_out, m_w_out, v_w_out))
    for name, (wt, gt, mt, vt) in big.items():
        d, nm, nv = _adamw(wt[0], gt, mt[0], vt[0], "adamw_" + name)
        res["grad"][name] = gt[None]
        res["delta"][name], res["new_m"][name], res["new_v"][name] = d[None], nm[None], nv[None]

    d, nm, nv = _adamw(w_in[0].T, g_w_in, m_w_in[0].T, v_w_in[0].T, "adamw_w_in")
    res["grad"]["w_in"] = g_w_in.T[None]
    res["delta"]["w_in"], res["new_m"]["w_in"], res["new_v"]["w_in"] = d.T[None], nm.T[None], nv.T[None]
    order = ["norm_w", "w_in", "conv_w", "a_log", "dt_bias", "dn_norm_w", "w_o_dn", "w_o_dil", "w_out", "final_norm_w"]
    outs = [loss, grad_x[None]]
    for kind in ("grad", "delta", "new_m", "new_v"):
        outs += [res[kind][nm] for nm in order]
    return tuple(outs)
```

```python
import functools
import math

import jax
import jax.numpy as jnp
from jax import lax
from jax.experimental import pallas as pl
from jax.experimental.pallas import tpu as pltpu

F32 = jnp.float32
BF16 = jnp.bfloat16
MESH = pl.DeviceIdType.MESH

D_MODEL = 1024
DN_HEADS = 8
DN_D = 128
DN_CHUNK = 64
DN_W = DN_HEADS * DN_D
DIL_GROUPS = ((128, 1), (512, 4), (2048, 16))
N_DIL = len(DIL_GROUPS)
DIL_HEADS = 4
DIL_DH = 128
DIL_W = DIL_HEADS * DIL_DH
ATT_BLOCK = 128
NORM_EPS = 1e-6
PROJ_W = 11280
N_CHIPS = 4
SHARD_W = PROJ_W // N_CHIPS

OFF_QKV_A = 0
OFF_Z_A = 3072
OFF_Q_B = 4096
OFF_K_B = 5632
OFF_V_B = 7168
OFF_Z_B = 8704
OFF_G_A = 9216
OFF_G_B = 10240
OFF_BA = 11264
PW = 11520
REF_OFF_BA = 4096

ADAM_LR = 0.001
ADAM_B1 = 0.9
ADAM_B2 = 0.999
ADAM_EPS = 1e-08
ADAM_WD = 0.01
ADAM_STEP = 10

ROW_TILE = 256
NEG = -1e30


def _dot(a, b):
    return jnp.dot(a.astype(BF16), b.astype(BF16), preferred_element_type=F32)


def _dot_nt(a, b):
    return lax.dot_general(a.astype(BF16), b.astype(BF16), (((1,), (1,)), ((), ())), preferred_element_type=F32)


def _dot_tn(a, b):
    return lax.dot_general(a.astype(BF16), b.astype(BF16), (((0,), (0,)), ((), ())), preferred_element_type=F32)


def _split(a):
    hi = a.astype(BF16)
    lo = (a - hi.astype(F32)).astype(BF16)
    return hi, lo


def _dot_exact_lhs(c, a):
    hi, lo = _split(a)
    cb = c.astype(BF16)
    return jnp.dot(cb, hi, preferred_element_type=F32) + jnp.dot(cb, lo, preferred_element_type=F32)


def _dot_exact_rhs(a, c):
    hi, lo = _split(a)
    cb = c.astype(BF16)
    return jnp.dot(hi, cb, preferred_element_type=F32) + jnp.dot(lo, cb, preferred_element_type=F32)


def _dot_tn_exact_rhs(a, c):
    hi, lo = _split(a)
    cb = c.astype(BF16)
    dn = (((0,), (0,)), ((), ()))
    return (lax.dot_general(hi, cb, dn, preferred_element_type=F32)
            + lax.dot_general(lo, cb, dn, preferred_element_type=F32))


def _sigmoid(x):
    return 1.0 / (1.0 + jnp.exp(-x))


def _silu(x):
    return x * _sigmoid(x)


def _silu_grad(x):
    s = _sigmoid(x)
    return s * (1.0 + x * (1.0 - s))


def _softplus(x):
    return jnp.maximum(x, 0.0) + jnp.log(1.0 + jnp.exp(-jnp.abs(x)))


def _cparams(*sem):
    return pltpu.CompilerParams(dimension_semantics=sem)


def _matmul(a, b, out_dtype, tm, tn, tk, name, nt=False, transpose_out=False, after=None, also_bf16=False):
    m, kdim = a.shape
    n = b.shape[0] if nt else b.shape[1]
    tm, tn, tk = min(tm, m), min(tn, n), min(tk, kdim)
    assert m % tm == 0 and n % tn == 0 and kdim % tk == 0, (name, a.shape, b.shape, tm, tn, tk)
    nk = kdim // tk
    dot = _dot_nt if nt else _dot
    b_spec = (pl.BlockSpec((tn, tk), lambda i, j, k: (j, k)) if nt else pl.BlockSpec((tk, tn), lambda i, j, k: (k, j)))
    extra = [] if after is None else [after]
    out_dtypes = [out_dtype] + ([BF16] if also_bf16 else [])

    def emit(o_refs, acc):
        val = acc.T if transpose_out else acc
        for o_ref in o_refs:
            o_ref[...] = val.astype(o_ref.dtype)

    def outs_of(rest):
        return rest[len(extra):len(extra) + len(out_dtypes)]

    if nk == 1:
        def body(a_ref, b_ref, *rest):
            emit(outs_of(rest), dot(a_ref[...], b_ref[...]))
        scratch = []
    else:
        def body(a_ref, b_ref, *rest):
            o_ref, acc_ref = outs_of(rest), rest[-1]
            k = pl.program_id(2)
            p = dot(a_ref[...], b_ref[...])

            @pl.when(k == 0)
            def _():
                acc_ref[...] = p

            @pl.when(k > 0)
            def _():
                acc_ref[...] += p

            @pl.when(k == nk - 1)
            def _():
                emit(o_ref, acc_ref[...])
        scratch = [pltpu.VMEM((tm, tn), F32)]

    if transpose_out:
        out_spec, out_shape = pl.BlockSpec((tn, tm), lambda i, j, k: (j, i)), (n, m)
    else:
        out_spec, out_shape = pl.BlockSpec((tm, tn), lambda i, j, k: (i, j)), (m, n)
    res = pl.pallas_call(
        body, name=name, grid=(m // tm, n // tn, nk),
        in_specs=[pl.BlockSpec((tm, tk), lambda i, j, k: (i, k)), b_spec] + [pl.BlockSpec(memory_space=pl.ANY)] * len(extra),
        out_specs=[out_spec] * len(out_dtypes), out_shape=[jax.ShapeDtypeStruct(out_shape, d) for d in out_dtypes],
        scratch_shapes=scratch, compiler_params=_cparams("parallel", "parallel", "arbitrary"))(a, b, *extra)
    return res if also_bf16 else res[0]


def _rms_in(x, nw):
    s, d = x.shape

    def body(x_ref, w_ref, h_ref, ht_ref):
        xv = x_ref[...]
        r = lax.rsqrt(jnp.mean(xv * xv, axis=-1, keepdims=True) + NORM_EPS)
        h = xv * r * w_ref[...]
        h_ref[...] = h.astype(BF16)
        ht_ref[...] = h.T.astype(BF16)

    return pl.pallas_call(
        body, name="rms_in", grid=(s // ROW_TILE,),
        in_specs=[pl.BlockSpec((ROW_TILE, d), lambda i: (i, 0)), pl.BlockSpec((1, d), lambda i: (0, 0))],
        out_specs=[pl.BlockSpec((ROW_TILE, d), lambda i: (i, 0)), pl.BlockSpec((d, ROW_TILE), lambda i: (0, i))],
        out_shape=[jax.ShapeDtypeStruct((s, d), BF16), jax.ShapeDtypeStruct((d, s), BF16)],
        compiler_params=_cparams("parallel"))(x, nw)


def _rms_in_bwd(x, nw, dh, dx2):
    s, d = x.shape

    def body(x_ref, w_ref, dh_ref, dx2_ref, dx_ref, dw_ref):
        i = pl.program_id(0)
        xv = x_ref[...]
        r = lax.rsqrt(jnp.mean(xv * xv, axis=-1, keepdims=True) + NORM_EPS)
        dhv = dh_ref[...]
        dyw = dhv * w_ref[...]
        dx_ref[...] = dx2_ref[...] + r * dyw - xv * (r * r * r) * jnp.mean(dyw * xv, axis=-1, keepdims=True)
        part = jnp.sum(dhv * xv * r, axis=0, keepdims=True)

        @pl.when(i == 0)
        def _():
            dw_ref[...] = part

        @pl.when(i > 0)
        def _():
            dw_ref[...] += part

    row = pl.BlockSpec((ROW_TILE, d), lambda i: (i, 0))
    vec = pl.BlockSpec((1, d), lambda i: (0, 0))
    return pl.pallas_call(
        body, name="rms_in_bwd", grid=(s // ROW_TILE,), in_specs=[row, vec, row, row], out_specs=[row, vec],
        out_shape=[jax.ShapeDtypeStruct((s, d), F32), jax.ShapeDtypeStruct((1, d), F32)],
        compiler_params=_cparams("arbitrary"))(x, nw, dh, dx2)


def _shift_down(cur, prev8, k):
    rc = pltpu.roll(cur, k, 0)
    rp = pltpu.roll(prev8, k, 0)
    row = lax.broadcasted_iota(jnp.int32, prev8.shape, 0)
    top = jnp.where(row < k, rp, rc[:8])
    return jnp.concatenate([top, rc[8:]], axis=0)


def _shift_up(cur, next8, k):
    t = cur.shape[0]
    rc = pltpu.roll(cur, t - k, 0)
    rn = pltpu.roll(next8, 8 - k, 0)
    row = lax.broadcasted_iota(jnp.int32, next8.shape, 0)
    bot = jnp.where(row >= 8 - k, rn, rc[t - 8:])
    return jnp.concatenate([rc[:t - 8], bot], axis=0)


def _conv_fwd(proj, conv_w):
    s = proj.shape[0]
    t8 = ROW_TILE // 8

    def body(u_ref, up_ref, w_ref, c_ref, y_ref):
        i = pl.program_id(0)
        part = pl.program_id(1)
        cur = u_ref[...]
        prev8 = jnp.where(i > 0, up_ref[...], 0.0)
        w = w_ref[...]
        c = cur * w[3:4, :]
        for k in (1, 2, 3):
            c = c + _shift_down(cur, prev8, k) * w[3 - k:4 - k, :]
        c_ref[...] = c
        a = _silu(c)
        for h in range(DN_HEADS):
            ah = a[:, h * DN_D:(h + 1) * DN_D]
            r = lax.rsqrt(jnp.sum(ah * ah, axis=-1, keepdims=True) + NORM_EPS)
            y_ref[:, h * DN_D:(h + 1) * DN_D] = jnp.where(part < 2, ah * r, ah)

    return pl.pallas_call(
        body, name="conv_fwd", grid=(s // ROW_TILE, 3),
        in_specs=[pl.BlockSpec((ROW_TILE, DN_W), lambda i, p: (i, p)),
                  pl.BlockSpec((8, DN_W), lambda i, p: (jnp.maximum(i * t8 - 1, 0), p)),
                  pl.BlockSpec((4, DN_W), lambda i, p: (0, p))],
        out_specs=[pl.BlockSpec((ROW_TILE, DN_W), lambda i, p: (i, p))] * 2,
        out_shape=[jax.ShapeDtypeStruct((s, 3 * DN_W), F32)] * 2,
        compiler_params=_cparams("parallel", "parallel"))(proj, proj, conv_w)


def _conv_bwd_act(c, dq, dk, dv):
    s = c.shape[0]

    def body(c_ref, dq_ref, dk_ref, dv_ref, dc_ref):
        for part, d_ref in enumerate((dq_ref, dk_ref, dv_ref)):
            for h in range(DN_HEADS):
                sl = slice(part * DN_W + h * DN_D, part * DN_W + (h + 1) * DN_D)
                ch = c_ref[:, sl]
                dyh = d_ref[:, h * DN_D:(h + 1) * DN_D]
                if part < 2:
                    ah = _silu(ch)
                    r = lax.rsqrt(jnp.sum(ah * ah, axis=-1, keepdims=True) + NORM_EPS)
                    dyh = r * dyh - ah * (r * r * r) * jnp.sum(dyh * ah, axis=-1, keepdims=True)
                dc_ref[:, sl] = dyh * _silu_grad(ch)

    wide = pl.BlockSpec((ROW_TILE, 3 * DN_W), lambda i: (i, 0))
    row = pl.BlockSpec((ROW_TILE, DN_W), lambda i: (i, 0))
    return pl.pallas_call(
        body, name="conv_bwd_act", grid=(s // ROW_TILE,), in_specs=[wide, row, row, row], out_specs=wide,
        out_shape=jax.ShapeDtypeStruct((s, 3 * DN_W), F32), compiler_params=_cparams("parallel"))(c, dq, dk, dv)


def _conv_bwd(proj, dc, conv_w):
    s = proj.shape[0]
    t8 = ROW_TILE // 8
    nrow = s // ROW_TILE
    last8 = s // 8 - 1

    def body(u_ref, dc_ref, dcn_ref, w_ref, du_ref, dw_ref):
        i = pl.program_id(1)
        cur = u_ref[...]
        dcv = dc_ref[...]
        next8 = jnp.where(i < nrow - 1, dcn_ref[...], 0.0)
        w = w_ref[...]

        @pl.when(i == 0)
        def _():
            dw_ref[...] = jnp.zeros_like(dw_ref)

        du = dcv * w[3:4, :]
        dw_ref[3:4, :] += jnp.sum(cur * dcv, axis=0, keepdims=True)
        for k in (1, 2, 3):
            ahead = _shift_up(dcv, next8, k)
            du = du + ahead * w[3 - k:4 - k, :]
            dw_ref[3 - k:4 - k, :] += jnp.sum(cur * ahead, axis=0, keepdims=True)
        du_ref[...] = du.astype(BF16)

    blk = pl.BlockSpec((ROW_TILE, DN_W), lambda p, i: (i, p))
    return pl.pallas_call(
        body, name="conv_bwd", grid=(3, nrow),
        in_specs=[blk, blk, pl.BlockSpec((8, DN_W), lambda p, i: (jnp.minimum((i + 1) * t8, last8), p)),
                  pl.BlockSpec((4, DN_W), lambda p, i: (0, p))],
        out_specs=[blk, pl.BlockSpec((4, DN_W), lambda p, i: (0, p))],
        out_shape=[jax.ShapeDtypeStruct((s, 3 * DN_W), BF16), jax.ShapeDtypeStruct((4, 3 * DN_W), F32)],
        compiler_params=_cparams("parallel", "arbitrary"))(proj, dc, dc, conv_w)


def _gates_fwd(proj, gate_par):
    s = proj.shape[0]

    def body(ba_ref, par_ref, o_ref):
        v = ba_ref[...]
        lane = lax.broadcasted_iota(jnp.int32, v.shape, 1)
        beta = _sigmoid(v)
        g = -jnp.exp(par_ref[0:1, :]) * _softplus(v + par_ref[1:2, :])
        o_ref[...] = jnp.where(lane < DN_HEADS, beta, jnp.where(lane < 2 * DN_HEADS, g, 0.0))

    return pl.pallas_call(
        body, name="gates_fwd", grid=(s // ROW_TILE,),
        in_specs=[pl.BlockSpec((ROW_TILE, 128), lambda i: (i, OFF_BA // 128)), pl.BlockSpec((8, 128), lambda i: (0, 0))],
        out_specs=pl.BlockSpec((ROW_TILE, 128), lambda i: (i, 0)),
        out_shape=jax.ShapeDtypeStruct((s, 128), F32), compiler_params=_cparams("parallel"))(proj, gate_par)


def _gates_bwd(proj, gate_par, dbg):
    s = proj.shape[0]

    def body(ba_ref, par_ref, d_ref, o_ref, dpar_ref):
        i = pl.program_id(0)
        v = ba_ref[...]
        dv = d_ref[...]
        lane = lax.broadcasted_iota(jnp.int32, v.shape, 1)
        beta = _sigmoid(v)
        nega = -jnp.exp(par_ref[0:1, :])
        xs = v + par_ref[1:2, :]
        dsp = dv * nega * _sigmoid(xs)
        dal = dv * nega * _softplus(xs)
        is_b = lane < DN_HEADS
        is_g = jnp.logical_and(lane >= DN_HEADS, lane < 2 * DN_HEADS)
        o_ref[...] = jnp.where(is_b, dv * beta * (1.0 - beta), jnp.where(is_g, dsp, 0.0)).astype(BF16)
        r0 = jnp.sum(jnp.where(is_g, dal, 0.0), axis=0, keepdims=True)
        r1 = jnp.sum(jnp.where(is_g, dsp, 0.0), axis=0, keepdims=True)

        @pl.when(i == 0)
        def _():
            dpar_ref[...] = jnp.zeros_like(dpar_ref)

        dpar_ref[0:1, :] += r0
        dpar_ref[1:2, :] += r1

    return pl.pallas_call(
        body, name="gates_bwd", grid=(s // ROW_TILE,),
        in_specs=[pl.BlockSpec((ROW_TILE, 128), lambda i: (i, OFF_BA // 128)), pl.BlockSpec((8, 128), lambda i: (0, 0)),
                  pl.BlockSpec((ROW_TILE, 128), lambda i: (i, 0))],
        out_specs=[pl.BlockSpec((ROW_TILE, 128), lambda i: (i, 0)), pl.BlockSpec((8, 128), lambda i: (0, 0))],
        out_shape=[jax.ShapeDtypeStruct((s, 128), BF16), jax.ShapeDtypeStruct((8, 128), F32)],
        compiler_params=_cparams("arbitrary"))(proj, gate_par, dbg)


def _chunk_masks():
    c = DN_CHUNK
    ii = lax.broadcasted_iota(jnp.int32, (c, c), 0)
    jj = lax.broadcasted_iota(jnp.int32, (c, c), 1)
    return dict(ii=ii, jj=jj, lower=(ii >= jj), strict=(ii > jj), eye=(ii == jj),
                lower_f=(ii >= jj).astype(BF16), upper_f=(ii <= jj).astype(BF16), ones8=jnp.ones((8, c), BF16))


class _Heads:
    def __init__(self, xs):
        self.xs = list(xs)

    def _bin(self, o, f):
        if isinstance(o, _Heads):
            return _Heads([f(a, b) for a, b in zip(self.xs, o.xs)])
        return _Heads([f(a, o) for a in self.xs])

    def __add__(self, o):
        return self._bin(o, lambda a, b: a + b)

    def __sub__(self, o):
        return self._bin(o, lambda a, b: a - b)

    def __mul__(self, o):
        return self._bin(o, lambda a, b: a * b)

    __radd__ = __add__
    __rmul__ = __mul__

    def __neg__(self):
        return _Heads([-a for a in self.xs])

    def __getitem__(self, i):
        return _Heads([a[i] for a in self.xs])


def _hmap(f, *args):
    n = next(len(a.xs) for a in args if isinstance(a, _Heads))
    return _Heads([f(*[(a.xs[h] if isinstance(a, _Heads) else a) for a in args]) for h in range(n)])


def _hdot(a, b):
    return _hmap(_dot, a, b)


def _hdot_nt(a, b):
    return _hmap(_dot_nt, a, b)


def _hdot_tn(a, b):
    return _hmap(_dot_tn, a, b)


def _hcat(a, b, axis):
    return _hmap(lambda x, y: jnp.concatenate([x, y], axis=axis), a, b)


def _hsum(a, axis):
    return _hmap(lambda t: jnp.sum(t, axis=axis, keepdims=True), a)


def _hwhere(c, a, b):
    return _hmap(jnp.where, c, a, b)


def _chunk_gates(mk, bg):
    c = DN_CHUNK
    gc_all = _dot_exact_lhs(mk["lower_f"], bg)
    rows = jnp.concatenate([gc_all, gc_all], axis=0).T
    hs = range(DN_HEADS)
    return (_Heads(bg[:, h:h + 1] for h in hs), _Heads(gc_all[:, DN_HEADS + h:DN_HEADS + h + 1] for h in hs),
            _Heads(rows[DN_HEADS + h:DN_HEADS + h + 1, :] for h in hs))


def _chunk_common(mk, q, k, beta_col, gc_col, gc_r):
    c = DN_CHUNK
    lower, strict = mk["lower"], mk["strict"]
    qs = q * (DN_D ** -0.5)
    beta_b = _hmap(lambda t: jnp.broadcast_to(t, (c, DN_D)), beta_col)
    gc_b = _hmap(lambda t: jnp.broadcast_to(t, (c, DN_D)), gc_col)
    gc_sq = gc_b[:, :c]
    gam = _hwhere(lower, _hmap(lambda t: jnp.exp(jnp.minimum(t, 0.0)), gc_sq - gc_r[:, :c]), 0.0)
    egc = _hmap(jnp.exp, gc_b)
    gl = gc_b[c - 1:c, :]
    ekd = _hmap(jnp.exp, gl - gc_b)
    dl = _hmap(jnp.exp, gl)
    kb = k * beta_b
    scores = _hdot_nt(_hcat(kb, qs, 0), k)
    a_strict = _hwhere(strict, scores[:c] * gam, 0.0)
    aqk = _hwhere(lower, scores[c:] * gam, 0.0)
    return dict(k=k, qs=qs, beta_b=beta_b, gc_b=gc_b, gam=gam, egc=egc, ekd=ekd, dl=dl, kb=kb, a_strict=a_strict, aqk=aqk)


def _unit_lower_inverse_minus_eye(n_strict, ii, jj):
    same = lax.shift_right_logical(ii, 4) == lax.shift_right_logical(jj, 4)
    dmat = _hwhere(same, n_strict, 0.0)
    omat = n_strict - dmat
    d2 = _hdot(dmat, dmat)
    d4 = _hdot(d2, d2)
    d8 = _hdot(d4, d4)
    x1 = d2 - dmat - _hdot(dmat, d2)
    x2 = x1 + d4 + _hdot(x1, d4)
    x3 = x2 + d8 + _hdot(x2, d8)
    n1 = omat + _hdot(x3, omat)
    n2 = _hdot(n1, n1)
    y = n2 - n1 - _hdot(n1, n2)
    return y + x3 + _hdot(y, x3)


def _gdr_fwd(qkv, bg):
    s = qkv.shape[0]
    c = DN_CHUNK
    n = s // c

    def body(q_ref, k_ref, v_ref, bg_ref, o_ref, u_ref, w_ref, vn_ref, tm_ref, st_ref, state):
        @pl.when(pl.program_id(0) == 0)
        def _():
            state[...] = jnp.zeros_like(state)

        mk = _chunk_masks()
        bg = bg_ref[...]
        hs = range(DN_HEADS)
        sls = [slice(h * DN_D, (h + 1) * DN_D) for h in hs]
        cm = _chunk_common(mk, _Heads(q_ref[:, sl] for sl in sls), _Heads(k_ref[:, sl] for sl in sls),
                           *_chunk_gates(mk, bg))
        tm = _unit_lower_inverse_minus_eye(cm["a_strict"], mk["ii"], mk["jj"])
        rhs_u = _Heads(v_ref[:, sl] for sl in sls) * cm["beta_b"]
        rhs_w = cm["kb"] * cm["egc"]
        t_rhs = _hdot(tm, _hcat(rhs_u, rhs_w, 1))
        u = rhs_u + t_rhs[:, :DN_D]
        w = rhs_w + t_rhs[:, DN_D:]
        st = _Heads(state[h] for h in hs)
        on_state = _hdot(_hcat(w, cm["qs"] * cm["egc"], 0), st)
        v_new = u - on_state[:c]
        o = on_state[c:] + _hdot(cm["aqk"], v_new)
        st_new = st * cm["dl"] + _hdot_tn(cm["k"] * cm["ekd"], v_new)
        for h, sl in zip(hs, sls):
            o_ref[:, sl] = o.xs[h]
            u_ref[:, sl] = u.xs[h]
            w_ref[:, sl] = w.xs[h]
            vn_ref[:, sl] = v_new.xs[h]
            tm_ref[h, 0] = tm.xs[h]
            st_ref[h, 0] = st.xs[h]
            state[h] = st_new.xs[h]

    def part(p):
        return pl.BlockSpec((c, DN_W), lambda j: (j, p))

    return pl.pallas_call(
        body, name="gdr_fwd", grid=(n,),
        in_specs=[part(0), part(1), part(2), pl.BlockSpec((c, 128), lambda j: (j, 0))],
        out_specs=[part(0)] * 4 + [pl.BlockSpec((DN_HEADS, 1, c, c), lambda j: (0, j, 0, 0)),
                                   pl.BlockSpec((DN_HEADS, 1, DN_D, DN_D), lambda j: (0, j, 0, 0))],
        out_shape=[jax.ShapeDtypeStruct((s, DN_W), F32)] * 4
        + [jax.ShapeDtypeStruct((DN_HEADS, n, c, c), F32), jax.ShapeDtypeStruct((DN_HEADS, n, DN_D, DN_D), F32)],
        scratch_shapes=[pltpu.VMEM((DN_HEADS, DN_D, DN_D), F32)],
        compiler_params=_cparams("arbitrary"))(qkv, qkv, qkv, bg)


def _gdr_bwd(qkv, bg, u, w, vn, tmat, states, do):
    s = qkv.shape[0]
    c = DN_CHUNK
    n = s // c

    def body(q_ref, k_ref, v_ref, bg_ref, u_ref, w_ref, vn_ref, tm_ref, st_ref, do_ref,
             dq_ref, dk_ref, dv_ref, dbg_ref, dstate):
        @pl.when(pl.program_id(0) == 0)
        def _():
            dstate[...] = jnp.zeros_like(dstate)

        mk = _chunk_masks()
        lower, strict = mk["lower"], mk["strict"]
        bg = bg_ref[...]
        ones = jnp.ones((c, DN_D), BF16)
        rowi = lax.broadcasted_iota(jnp.int32, (c, DN_D), 0)
        lane = lax.broadcasted_iota(jnp.int32, (c, 128), 1)
        hs = range(DN_HEADS)
        sls = [slice(h * DN_D, (h + 1) * DN_D) for h in hs]

        def heads_of(ref):
            return _Heads(ref[:, sl] for sl in sls)

        cm = _chunk_common(mk, heads_of(q_ref), heads_of(k_ref), *_chunk_gates(mk, bg))
        k, qs, beta_b = cm["k"], cm["qs"], cm["beta_b"]
        gam, egc, ekd, dl, kb = cm["gam"], cm["egc"], cm["ekd"], cm["dl"], cm["kb"]
        aqk, a_strict = cm["aqk"], cm["a_strict"]
        v, uu, ww, v_new, dov = heads_of(v_ref), heads_of(u_ref), heads_of(w_ref), heads_of(vn_ref), heads_of(do_ref)
        st = _Heads(st_ref[h, 0] for h in hs)
        dsn = _Heads(dstate[h] for h in hs)
        qd = qs * egc
        kd = k * ekd

        dv_new = _hdot_tn(aqk, dov) + _hdot(kd, dsn)
        do_sv = _hdot_nt(dov, _hcat(st, v_new, 0))
        dqd = do_sv[:, :DN_D]
        daqk = _hwhere(lower, do_sv[:, DN_D:], 0.0)
        dkd = _hdot_nt(v_new, dsn)
        ddl = _hsum(_hsum(dsn * st, 1), 0)
        dw = -_hdot_nt(dv_new, st)
        ds_new = dsn * dl + _hdot_tn(_hcat(qd, -ww, 0), _hcat(dov, dv_new, 0))

        tm = _Heads(tm_ref[h, 0] for h in hs)
        tt = _hdot_tn(tm, _hcat(dv_new, dw, 1))
        dru = dv_new + tt[:, :DN_D]
        drw = dw + tt[:, DN_D:]
        dn = _hwhere(strict, -_hdot_nt(_hcat(dru, drw, 1), _hcat(uu, ww, 1)), 0.0)
        dag = dn * gam
        dqg = daqk * gam
        both = _hcat(dag, dqg, 0)
        on_k = _hdot(both, k)
        dkb = on_k[:c] + drw * egc
        dqs = on_k[c:] + dqd * egc
        dk = _hdot_tn(both, _hcat(kb, qs, 0)) + dkb * beta_b + dkd * ekd
        pmat = dn * a_strict + daqk * aqk
        tkd = _hsum(dkd * kd, -1)
        dgc = (_hsum(pmat, -1) - _hmap(_dot_tn_exact_rhs, pmat, ones) + _hsum(drw * (kb * egc), -1)
               + _hsum(dqd * qd, -1) - tkd)
        last = _hsum(tkd, 0) + ddl * dl
        dgc = dgc + _hwhere(rowi == c - 1, last, 0.0)
        dbeta = _hsum(dru * v, -1) + _hsum(dkb * k, -1)
        dq = dqs * (DN_D ** -0.5)
        dv = dru * beta_b

        dgc_all = jnp.zeros((c, 128), F32)
        dbg = jnp.zeros((c, 128), F32)
        for h, sl in zip(hs, sls):
            dq_ref[:, sl] = dq.xs[h]
            dk_ref[:, sl] = dk.xs[h]
            dv_ref[:, sl] = dv.xs[h]
            dstate[h] = ds_new.xs[h]
            dgc_all = dgc_all + jnp.where(lane == DN_HEADS + h, dgc.xs[h], 0.0)
            dbg = dbg + jnp.where(lane == h, dbeta.xs[h], 0.0)
        dbg_ref[...] = dbg + _dot_exact_lhs(mk["upper_f"], dgc_all)

    def part(p):
        return pl.BlockSpec((c, DN_W), lambda j: (n - 1 - j, p))

    vec = pl.BlockSpec((c, 128), lambda j: (n - 1 - j, 0))
    return pl.pallas_call(
        body, name="gdr_bwd", grid=(n,),
        in_specs=[part(0), part(1), part(2), vec, part(0), part(0), part(0),
                  pl.BlockSpec((DN_HEADS, 1, c, c), lambda j: (0, n - 1 - j, 0, 0)),
                  pl.BlockSpec((DN_HEADS, 1, DN_D, DN_D), lambda j: (0, n - 1 - j, 0, 0)), part(0)],
        out_specs=[part(0), part(0), part(0), vec],
        out_shape=[jax.ShapeDtypeStruct((s, DN_W), F32)] * 3 + [jax.ShapeDtypeStruct((s, 128), F32)],
        scratch_shapes=[pltpu.VMEM((DN_HEADS, DN_D, DN_D), F32)],
        compiler_params=_cparams("arbitrary"))(qkv, qkv, qkv, bg, u, w, vn, tmat, states, do)


def _gdr_out(o, proj, dnw):
    s = o.shape[0]

    def body(o_ref, z_ref, w_ref, y_ref, yt_ref):
        ov, zv, wv = o_ref[...], z_ref[...], w_ref[...]
        for h in range(DN_HEADS):
            sl = slice(h * DN_D, (h + 1) * DN_D)
            oh = ov[:, sl]
            r = lax.rsqrt(jnp.mean(oh * oh, axis=-1, keepdims=True) + NORM_EPS)
            y = (oh * r * wv) * _silu(zv[:, sl])
            y_ref[:, sl] = y.astype(BF16)
            yt_ref[sl, :] = y.T.astype(BF16)

    row = pl.BlockSpec((ROW_TILE, DN_W), lambda i: (i, 0))
    return pl.pallas_call(
        body, name="gdr_out", grid=(s // ROW_TILE,),
        in_specs=[row, pl.BlockSpec((ROW_TILE, DN_W), lambda i: (i, OFF_Z_A // DN_W)), pl.BlockSpec((1, DN_D), lambda i: (0, 0))],
        out_specs=[row, pl.BlockSpec((DN_W, ROW_TILE), lambda i: (0, i))],
        out_shape=[jax.ShapeDtypeStruct((s, DN_W), BF16), jax.ShapeDtypeStruct((DN_W, s), BF16)],
        compiler_params=_cparams("parallel"))(o, proj, dnw)


def _gdr_out_bwd(o, proj, dnw, dy):
    s = o.shape[0]

    def body(o_ref, z_ref, w_ref, dy_ref, do_ref, dz_ref, dw_ref):
        i = pl.program_id(0)
        ov, zv, wv, dyv = o_ref[...], z_ref[...], w_ref[...], dy_ref[...]
        acc = jnp.zeros((1, DN_D), F32)
        for h in range(DN_HEADS):
            sl = slice(h * DN_D, (h + 1) * DN_D)
            oh, zh, dh = ov[:, sl], zv[:, sl], dyv[:, sl]
            r = lax.rsqrt(jnp.mean(oh * oh, axis=-1, keepdims=True) + NORM_EPS)
            dn = dh * _silu(zh)
            dz_ref[:, sl] = (dh * (oh * r * wv) * _silu_grad(zh)).astype(BF16)
            acc = acc + jnp.sum(dn * oh * r, axis=0, keepdims=True)
            dnw_ = dn * wv
            do_ref[:, sl] = r * dnw_ - oh * (r * r * r) * jnp.mean(dnw_ * oh, axis=-1, keepdims=True)

        @pl.when(i == 0)
        def _():
            dw_ref[...] = acc

        @pl.when(i > 0)
        def _():
            dw_ref[...] += acc

    row = pl.BlockSpec((ROW_TILE, DN_W), lambda i: (i, 0))
    vec = pl.BlockSpec((1, DN_D), lambda i: (0, 0))
    return pl.pallas_call(
        body, name="gdr_out_bwd", grid=(s // ROW_TILE,),
        in_specs=[row, pl.BlockSpec((ROW_TILE, DN_W), lambda i: (i, OFF_Z_A // DN_W)), vec, row],
        out_specs=[row, row, vec],
        out_shape=[jax.ShapeDtypeStruct((s, DN_W), F32), jax.ShapeDtypeStruct((s, DN_W), BF16),
                   jax.ShapeDtypeStruct((1, DN_D), F32)],
        compiler_params=_cparams("arbitrary"))(o, proj, dnw, dy)


def _slope(group, head):
    idx = (group * DIL_HEADS + head + 1).astype(F32)
    return jnp.exp(jnp.full((1, 128), -8.0 * math.log(2.0) / (N_DIL * DIL_HEADS), F32) * idx)


def _att_scores(qb, k_cur, k_prev, slope_d, has_prev):
    iq = lax.broadcasted_iota(jnp.int32, (ATT_BLOCK, ATT_BLOCK), 0)
    jk = lax.broadcasted_iota(jnp.int32, (ATT_BLOCK, ATT_BLOCK), 1)
    dist_c = (iq - jk).astype(F32)
    s_cur = jnp.where(iq >= jk, _dot_nt(qb, k_cur) - slope_d * dist_c, NEG)
    s_prev = jnp.where(jnp.logical_and(jk >= iq, has_prev),
                       _dot_nt(qb, k_prev) - slope_d * (dist_c + float(ATT_BLOCK)), NEG)
    return s_cur, s_prev


def _att_scores_whole(qb, k, slope_d):
    n = 2 * ATT_BLOCK
    dist = lax.broadcasted_iota(jnp.int32, (n, n), 0) - lax.broadcasted_iota(jnp.int32, (n, n), 1)
    valid = jnp.logical_and(dist >= 0, dist <= ATT_BLOCK)
    return jnp.where(valid, _dot_nt(qb, k) - slope_d[:, 0:1] * dist.astype(F32), NEG)


ATT_UNROLL = 4


def _att_blocks(i, dil, nb):
    per = dil * nb // ATT_UNROLL
    assert per * ATT_UNROLL == dil * nb
    for i0 in range(per):
        blocks = [divmod(i0 + u * per, nb) for u in range(ATT_UNROLL)]
        assert all(a[0] != b[0] or abs(a[1] - b[1]) >= 2 for n, a in enumerate(blocks) for b in blocks[n + 1:])
    curs, prvs, has_prev = [], [], []
    for u in range(ATT_UNROLL):
        t = i + u * per
        r = lax.div(t, nb)
        j = lax.rem(t, nb)
        base = r + dil * ATT_BLOCK * j
        pbase = base - dil * ATT_BLOCK * jnp.minimum(j, 1)
        if dil == 1:
            base, pbase = pl.multiple_of(base, ATT_BLOCK), pl.multiple_of(pbase, ATT_BLOCK)
        curs.append(pl.ds(base, ATT_BLOCK, stride=dil))
        prvs.append(pl.ds(pbase, ATT_BLOCK, stride=dil))
        has_prev.append(j > 0)
    return curs, prvs, has_prev


def _att_fwd(proj, group):
    s = proj.shape[0]
    dil = DIL_GROUPS[group][1]
    assert DIL_GROUPS[group][0] // dil == ATT_BLOCK
    nb = s // dil // ATT_BLOCK
    assert nb * dil * ATT_BLOCK == s

    def body(q_ref, k_ref, v_ref, num_ref, den_ref, mx_ref):
        slope_d = _slope(group, pl.program_id(0)) * float(dil)

        def step(i, carry):
            curs, prvs, has_prev = _att_blocks(i, dil, nb)
            us = range(ATT_UNROLL)
            qb = [q_ref[c, :] * (DIL_DH ** -0.5) for c in curs]
            sc = [_att_scores(qb[u], k_ref[curs[u], :], k_ref[prvs[u], :], slope_d, has_prev[u]) for u in us]
            mx = [jnp.maximum(jnp.max(a, axis=-1, keepdims=True), jnp.max(b, axis=-1, keepdims=True)) for a, b in sc]
            p_cur = [jnp.exp(sc[u][0] - mx[u]) for u in us]
            p_prev = [jnp.exp(sc[u][1] - mx[u]) for u in us]
            den = [jnp.sum(p_cur[u], axis=-1, keepdims=True) + jnp.sum(p_prev[u], axis=-1, keepdims=True) for u in us]
            num = [_dot(p_cur[u], v_ref[curs[u], :]) + _dot(p_prev[u], v_ref[prvs[u], :]) for u in us]
            for u in us:
                num_ref[curs[u], :] = num[u]
                den_ref[curs[u], :] = jnp.broadcast_to(den[u], (ATT_BLOCK, DIL_DH))
                mx_ref[curs[u], :] = jnp.broadcast_to(mx[u], (ATT_BLOCK, DIL_DH))
            return carry

        def step_whole(i, carry):
            rows = [pl.ds(i * ATT_UNROLL + u, 2 * ATT_BLOCK, stride=dil) for u in range(ATT_UNROLL)]
            sc = [_att_scores_whole(q_ref[r, :] * (DIL_DH ** -0.5), k_ref[r, :], slope_d) for r in rows]
            mx = [jnp.max(a, axis=-1, keepdims=True) for a in sc]
            p = [jnp.exp(a - m) for a, m in zip(sc, mx)]
            num = [_dot(pu, v_ref[r, :]) for pu, r in zip(p, rows)]
            for u, r in enumerate(rows):
                num_ref[r, :] = num[u]
                den_ref[r, :] = jnp.broadcast_to(jnp.sum(p[u], axis=-1, keepdims=True), (2 * ATT_BLOCK, DIL_DH))
                mx_ref[r, :] = jnp.broadcast_to(mx[u], (2 * ATT_BLOCK, DIL_DH))
            return carry

        if nb == 2:
            lax.fori_loop(0, dil // ATT_UNROLL, step_whole, 0)
        else:
            lax.fori_loop(0, dil * nb // ATT_UNROLL, step, 0)

    def col(off):
        return pl.BlockSpec((s, DIL_DH), lambda h: (0, off // DIL_DH + group * DIL_HEADS + h))

    out = pl.BlockSpec((s, DIL_DH), lambda h: (0, h))
    return pl.pallas_call(
        body, name=f"att_fwd{group}", grid=(DIL_HEADS,), in_specs=[col(OFF_Q_B), col(OFF_K_B), col(OFF_V_B)],
        out_specs=[out, out, out], out_shape=[jax.ShapeDtypeStruct((s, DIL_W), F32)] * 3,
        compiler_params=_cparams("parallel"))(proj, proj, proj)


def _att_bwd(proj, group, do, lse, delta):
    s = proj.shape[0]
    dil = DIL_GROUPS[group][1]
    nb = s // dil // ATT_BLOCK

    def body(q_ref, k_ref, v_ref, do_ref, lse_ref, dl_ref, dq_ref, dk_ref, dv_ref, dq_acc, dk_acc, dv_acc):
        slope_d = _slope(group, pl.program_id(0)) * float(dil)
        dk_acc[...] = jnp.zeros_like(dk_acc)
        dv_acc[...] = jnp.zeros_like(dv_acc)

        def step(i, carry):
            curs, prvs, has_prev = _att_blocks(i, dil, nb)
            us = range(ATT_UNROLL)
            qb = [q_ref[c, :] * (DIL_DH ** -0.5) for c in curs]
            k_cur, k_prev = [k_ref[c, :] for c in curs], [k_ref[p, :] for p in prvs]
            v_cur, v_prev = [v_ref[c, :] for c in curs], [v_ref[p, :] for p in prvs]
            sc = [_att_scores(qb[u], k_cur[u], k_prev[u], slope_d, has_prev[u]) for u in us]
            lse_b, delta_b, dob = [lse_ref[c, :] for c in curs], [dl_ref[c, :] for c in curs], [do_ref[c, :] for c in curs]
            p_cur = [jnp.exp(sc[u][0] - lse_b[u]) for u in us]
            p_prev = [jnp.exp(sc[u][1] - lse_b[u]) for u in us]
            ds_cur = [p_cur[u] * (_dot_nt(dob[u], v_cur[u]) - delta_b[u]) for u in us]
            ds_prev = [p_prev[u] * (_dot_nt(dob[u], v_prev[u]) - delta_b[u]) for u in us]
            dq = [(_dot(ds_cur[u], k_cur[u]) + _dot(ds_prev[u], k_prev[u])) * (DIL_DH ** -0.5) for u in us]
            dk_c = [_dot_tn(ds_cur[u], qb[u]) for u in us]
            dv_c = [_dot_tn(p_cur[u], dob[u]) for u in us]
            dk_p = [_dot_tn(ds_prev[u], qb[u]) for u in us]
            dv_p = [_dot_tn(p_prev[u], dob[u]) for u in us]
            for u in us:
                dq_acc[curs[u], :] = dq[u]
                dk_acc[curs[u], :] += dk_c[u]
                dv_acc[curs[u], :] += dv_c[u]
            for u in us:
                dk_acc[prvs[u], :] += dk_p[u]
                dv_acc[prvs[u], :] += dv_p[u]
            return carry

        def step_whole(i, carry):
            rows = [pl.ds(i * ATT_UNROLL + u, 2 * ATT_BLOCK, stride=dil) for u in range(ATT_UNROLL)]
            qb = [q_ref[r, :] * (DIL_DH ** -0.5) for r in rows]
            kk, vv, dob = [k_ref[r, :] for r in rows], [v_ref[r, :] for r in rows], [do_ref[r, :] for r in rows]
            sc = [_att_scores_whole(qb[u], kk[u], slope_d) for u in range(ATT_UNROLL)]
            p = [jnp.exp(sc[u] - lse_ref[r, :][:, 0:1]) for u, r in enumerate(rows)]
            ds = [p[u] * (_dot_nt(dob[u], vv[u]) - dl_ref[r, :][:, 0:1]) for u, r in enumerate(rows)]
            dq = [_dot(ds[u], kk[u]) * (DIL_DH ** -0.5) for u in range(ATT_UNROLL)]
            dk = [_dot_tn(ds[u], qb[u]) for u in range(ATT_UNROLL)]
            dv = [_dot_tn(p[u], dob[u]) for u in range(ATT_UNROLL)]
            for u, r in enumerate(rows):
                dq_acc[r, :] = dq[u]
                dk_acc[r, :] = dk[u]
                dv_acc[r, :] = dv[u]
            return carry

        if nb == 2:
            lax.fori_loop(0, dil // ATT_UNROLL, step_whole, 0)
        else:
            lax.fori_loop(0, dil * nb // ATT_UNROLL, step, 0)
        dq_ref[...] = dq_acc[...].astype(BF16)
        dk_ref[...] = dk_acc[...].astype(BF16)
        dv_ref[...] = dv_acc[...].astype(BF16)

    def col(off):
        return pl.BlockSpec((s, DIL_DH), lambda h: (0, off // DIL_DH + group * DIL_HEADS + h))

    hd = pl.BlockSpec((s, DIL_DH), lambda h: (0, h))
    return pl.pallas_call(
        body, name=f"att_bwd{group}", grid=(DIL_HEADS,),
        in_specs=[col(OFF_Q_B), col(OFF_K_B), col(OFF_V_B), hd, hd, hd], out_specs=[hd, hd, hd],
        out_shape=[jax.ShapeDtypeStruct((s, DIL_W), BF16)] * 3,
        scratch_shapes=[pltpu.VMEM((s, DIL_DH), F32)] * 3,
        compiler_params=_cparams("parallel"))(proj, proj, proj, do, lse, delta)


def _att_merge(parts, proj):
    s = proj.shape[0]

    def body(n0, d0, m0, n1, d1, m1, n2, d2, m2, z_ref, ob_ref, o_ref, lse_ref, obt_ref):
        m = jnp.maximum(jnp.maximum(m0[...], m1[...]), m2[...])
        num = jnp.zeros_like(m)
        den = jnp.zeros_like(m)
        for nr, dr, mr in ((n0, d0, m0), (n1, d1, m1), (n2, d2, m2)):
            sc = jnp.exp(mr[...] - m)
            num = num + nr[...] * sc
            den = den + dr[...] * sc
        o = num / den
        o_ref[...] = o
        lse_ref[...] = m + jnp.log(den)
        ob = o * _silu(z_ref[...])
        ob_ref[...] = ob.astype(BF16)
        obt_ref[...] = ob.T.astype(BF16)

    row = pl.BlockSpec((ROW_TILE, DIL_W), lambda i: (i, 0))
    flat = [a for p in parts for a in p]
    return pl.pallas_call(
        body, name="att_merge", grid=(s // ROW_TILE,),
        in_specs=[row] * 9 + [pl.BlockSpec((ROW_TILE, DIL_W), lambda i: (i, OFF_Z_B // DIL_W))],
        out_specs=[row, row, row, pl.BlockSpec((DIL_W, ROW_TILE), lambda i: (0, i))],
        out_shape=[jax.ShapeDtypeStruct((s, DIL_W), BF16), jax.ShapeDtypeStruct((s, DIL_W), F32),
                   jax.ShapeDtypeStruct((s, DIL_W), F32), jax.ShapeDtypeStruct((DIL_W, s), BF16)],
        compiler_params=_cparams("parallel"))(*flat, proj)


def _att_merge_bwd(o, proj, dob):
    s = o.shape[0]

    def body(o_ref, z_ref, d_ref, do_ref, dl_ref, dz_ref):
        ov, zv, dv = o_ref[...], z_ref[...], d_ref[...]
        do = dv * _silu(zv)
        do_ref[...] = do
        dz_ref[...] = (dv * ov * _silu_grad(zv)).astype(BF16)
        for h in range(DIL_HEADS):
            sl = slice(h * DIL_DH, (h + 1) * DIL_DH)
            dl_ref[:, sl] = jnp.broadcast_to(jnp.sum(do[:, sl] * ov[:, sl], axis=-1, keepdims=True), (ROW_TILE, DIL_DH))

    row = pl.BlockSpec((ROW_TILE, DIL_W), lambda i: (i, 0))
    return pl.pallas_call(
        body, name="att_merge_bwd", grid=(s // ROW_TILE,),
        in_specs=[row, pl.BlockSpec((ROW_TILE, DIL_W), lambda i: (i, OFF_Z_B // DIL_W)), row],
        out_specs=[row, row, row],
        out_shape=[jax.ShapeDtypeStruct((s, DIL_W), F32), jax.ShapeDtypeStruct((s, DIL_W), F32),
                   jax.ShapeDtypeStruct((s, DIL_W), BF16)],
        compiler_params=_cparams("parallel"))(o, proj, dob)


def _merge(proj, ya, yb):
    s = proj.shape[0]

    def body(ga_ref, gb_ref, ya_ref, yb_ref, o_ref, ot_ref):
        m = _sigmoid(ga_ref[...]) * ya_ref[...] + _sigmoid(gb_ref[...]) * yb_ref[...]
        o_ref[...] = m.astype(BF16)
        ot_ref[...] = m.T.astype(BF16)

    row = pl.BlockSpec((ROW_TILE, D_MODEL), lambda i: (i, 0))
    return pl.pallas_call(
        body, name="merge", grid=(s // ROW_TILE,),
        in_specs=[pl.BlockSpec((ROW_TILE, D_MODEL), lambda i: (i, OFF_G_A // D_MODEL)),
                  pl.BlockSpec((ROW_TILE, D_MODEL), lambda i: (i, OFF_G_B // D_MODEL)), row, row],
        out_specs=[row, pl.BlockSpec((D_MODEL, ROW_TILE), lambda i: (0, i))],
        out_shape=[jax.ShapeDtypeStruct((s, D_MODEL), BF16), jax.ShapeDtypeStruct((D_MODEL, s), BF16)],
        compiler_params=_cparams("parallel"))(proj, proj, ya, yb)


def _merge_bwd(proj, ya, yb, dm):
    s = proj.shape[0]

    def body(ga_ref, gb_ref, ya_ref, yb_ref, dm_ref, dya_ref, dyb_ref, dga_ref, dgb_ref):
        dmv = dm_ref[...]
        sa, sb = _sigmoid(ga_ref[...]), _sigmoid(gb_ref[...])
        dya_ref[...] = (dmv * sa).astype(BF16)
        dyb_ref[...] = (dmv * sb).astype(BF16)
        dga_ref[...] = (dmv * ya_ref[...] * sa * (1.0 - sa)).astype(BF16)
        dgb_ref[...] = (dmv * yb_ref[...] * sb * (1.0 - sb)).astype(BF16)

    row = pl.BlockSpec((ROW_TILE, D_MODEL), lambda i: (i, 0))
    return pl.pallas_call(
        body, name="merge_bwd", grid=(s // ROW_TILE,),
        in_specs=[pl.BlockSpec((ROW_TILE, D_MODEL), lambda i: (i, OFF_G_A // D_MODEL)),
                  pl.BlockSpec((ROW_TILE, D_MODEL), lambda i: (i, OFF_G_B // D_MODEL)), row, row, row],
        out_specs=[row] * 4, out_shape=[jax.ShapeDtypeStruct((s, D_MODEL), BF16)] * 4,
        compiler_params=_cparams("parallel"))(proj, proj, ya, yb, dm)


def _final(x, t, fw, tgt):
    s, d = x.shape

    def body(x_ref, t_ref, w_ref, y_ref, dx_ref, dw_ref, l_ref):
        i = pl.program_id(0)
        x2 = x_ref[...] + t_ref[...]
        wv = w_ref[...]
        r = lax.rsqrt(jnp.mean(x2 * x2, axis=-1, keepdims=True) + NORM_EPS)
        e = x2 * r * wv - y_ref[...]
        lrow = jnp.mean(e * e, axis=-1, keepdims=True)
        lpart = jnp.broadcast_to(0.5 * jnp.sum(lrow, axis=0, keepdims=True), (1, 128))
        dy = e * (1.0 / d)
        dwp = jnp.sum(dy * x2 * r, axis=0, keepdims=True)
        dyw = dy * wv
        dx_ref[...] = r * dyw - x2 * (r * r * r) * jnp.mean(dyw * x2, axis=-1, keepdims=True)

        @pl.when(i == 0)
        def _():
            dw_ref[...] = dwp
            l_ref[...] = lpart

        @pl.when(i > 0)
        def _():
            dw_ref[...] += dwp
            l_ref[...] += lpart

    row = pl.BlockSpec((ROW_TILE, d), lambda i: (i, 0))
    vec = pl.BlockSpec((1, d), lambda i: (0, 0))
    return pl.pallas_call(
        body, name="final", grid=(s // ROW_TILE,), in_specs=[row, row, vec, row],
        out_specs=[row, vec, pl.BlockSpec((1, 128), lambda i: (0, 0))],
        out_shape=[jax.ShapeDtypeStruct((s, d), F32), jax.ShapeDtypeStruct((1, d), F32), jax.ShapeDtypeStruct((1, 128), F32)],
        compiler_params=_cparams("arbitrary"))(x, t, fw, tgt)


def _adamw(w, g, m, v, name):
    r, c = w.shape
    cap = max(8, (1 << 18) // c)
    divisors = [t for t in range(8, min(r, cap) + 1, 8) if r % t == 0]
    tr = r if r <= 8 else (max(divisors) if divisors else cap)

    def body(w_ref, g_ref, m_ref, v_ref, d_ref, nm_ref, nv_ref):
        gv = g_ref[...]
        mn = ADAM_B1 * m_ref[...] + (1.0 - ADAM_B1) * gv
        vn = ADAM_B2 * v_ref[...] + (1.0 - ADAM_B2) * (gv * gv)
        m_hat = mn / (1.0 - ADAM_B1 ** ADAM_STEP)
        v_hat = vn / (1.0 - ADAM_B2 ** ADAM_STEP)
        d_ref[...] = -ADAM_LR * (m_hat / (jnp.sqrt(v_hat) + ADAM_EPS) + ADAM_WD * w_ref[...])
        nm_ref[...] = mn
        nv_ref[...] = vn

    blk = pl.BlockSpec((tr, c), lambda i: (i, 0))
    return pl.pallas_call(
        body, name=name, grid=(pl.cdiv(r, tr),), in_specs=[blk] * 4, out_specs=[blk] * 3,
        out_shape=[jax.ShapeDtypeStruct((r, c), F32)] * 3, compiler_params=_cparams("parallel"))(w, g, m, v)


HBM_SPEC = pl.BlockSpec(memory_space=pl.ANY)


def _place():
    x, y, c = lax.axis_index("x"), lax.axis_index("y"), lax.axis_index("c")
    chips = [(1 - x, y), (x, 1 - y), (1 - x, 1 - y)]
    return x, y, c, chips


def _ag_weights(packs):
    na = len(packs)
    nsem = 7

    def body(*refs):
        p_refs, out_refs = refs[:na], refs[na:2 * na]
        send_sems, recv_sems = refs[2 * na:]
        x, y, c, _ = _place()
        me, sib, j = (x, y, c), (x, y, 1 - c), 2 * x + y
        xn, yn = (1 - x, y, c), (x, 1 - y, c)
        jx, jy, jd = 2 * (1 - x) + y, 2 * x + (1 - y), 2 * (1 - x) + (1 - y)

        def rc(a, k, src, dst, to):
            return pltpu.make_async_remote_copy(src_ref=src, dst_ref=dst, send_sem=send_sems.at[nsem * a + k],
                                                recv_sem=recv_sems.at[nsem * a + k], device_id=to, device_id_type=MESH)

        sent = []
        for a in range(na):
            mine, land = p_refs[a].at[c], out_refs[a].at[j, c]
            sent += [rc(a, 0, mine, land, xn), rc(a, 1, mine, land, yn)]
        for cp in sent:
            cp.start()
        for a in range(na):
            half = p_refs[a].shape[1] // 2
            top, bottom = pl.ds(0, half), pl.ds(half, half)
            from_x, from_y, from_d = out_refs[a].at[jx, c], out_refs[a].at[jy, c], out_refs[a].at[jd, c]
            rc(a, 0, p_refs[a].at[c], from_x, me).wait_recv()
            later = [rc(a, 2, from_x.at[top], from_x.at[top], yn), rc(a, 4, from_x, from_x, sib)]
            for cp in later:
                cp.start()
            sent += later
            rc(a, 1, p_refs[a].at[c], from_y, me).wait_recv()
            later = [rc(a, 3, from_y.at[bottom], from_y.at[bottom], xn), rc(a, 5, from_y, from_y, sib)]
            for cp in later:
                cp.start()
            sent += later
            rc(a, 2, from_d.at[top], from_d.at[top], me).wait_recv()
            rc(a, 3, from_d.at[bottom], from_d.at[bottom], me).wait_recv()
            cp = rc(a, 6, from_d, from_d, sib)
            cp.start()
            sent.append(cp)
        for a in range(na):
            for k, jj in ((4, jx), (5, jy), (6, jd)):
                rc(a, k, p_refs[a].at[c], out_refs[a].at[jj, 1 - c], me).wait_recv()
        for cp in sent:
            cp.wait_send()

    return pl.pallas_call(
        body, name="ag_weights",
        out_shape=[jax.ShapeDtypeStruct((N_CHIPS,) + p.shape, p.dtype) for p in packs],
        in_specs=[HBM_SPEC] * na, out_specs=[HBM_SPEC] * na,
        scratch_shapes=[pltpu.SemaphoreType.DMA((nsem * na,)), pltpu.SemaphoreType.DMA((nsem * na,))])(*packs)


def _rs_pair(dwpt, gpack):
    n = N_CHIPS
    hw = SHARD_PAD // 2

    def body(d_ref, g_ref, out_d, out_g, send_sems, recv_sems):
        x, y, c, _ = _place()
        sib = (x, y, 1 - c)
        cps = []
        for p in range(n):
            start = pl.multiple_of(WIN_BASE[p] + (1 - c) * hw, TILE_ROWS)
            cps.append(pltpu.make_async_remote_copy(
                src_ref=d_ref.at[pl.ds(start, hw)], dst_ref=out_d.at[p], send_sem=send_sems.at[p],
                recv_sem=recv_sems.at[p], device_id=sib, device_id_type=MESH))
            cps.append(pltpu.make_async_remote_copy(
                src_ref=g_ref.at[p, 1 - c], dst_ref=out_g.at[p], send_sem=send_sems.at[n + p],
                recv_sem=recv_sems.at[n + p], device_id=sib, device_id_type=MESH))
        for cp in cps:
            cp.start()
        for cp in cps:
            cp.wait_recv()
        for cp in cps:
            cp.wait_send()

    return pl.pallas_call(
        body, name="rs_pair",
        out_shape=[jax.ShapeDtypeStruct((n, hw, dwpt.shape[1]), dwpt.dtype),
                   jax.ShapeDtypeStruct((n,) + gpack.shape[2:], gpack.dtype)],
        in_specs=[HBM_SPEC] * 2, out_specs=[HBM_SPEC] * 2,
        scratch_shapes=[pltpu.SemaphoreType.DMA((2 * n,)), pltpu.SemaphoreType.DMA((2 * n,))])(dwpt, gpack)


def _add_halves_win(dwpt, other, c):
    n, rh, wd = other.shape
    tr = _row_tile(rh)

    def body(s_ref, d_ref, o_ref, out_ref):
        out_ref[0] = (d_ref[...] + o_ref[0]).astype(BF16)

    scal = jnp.concatenate([jnp.reshape(c, (1,)).astype(jnp.int32), jnp.asarray(WIN_BASE, jnp.int32)])
    grid_spec = pltpu.PrefetchScalarGridSpec(
        num_scalar_prefetch=1, grid=(n, rh // tr),
        in_specs=[pl.BlockSpec((pl.Element(tr), pl.Element(wd)),
                               lambda p, i, sr: (pl.multiple_of(sr[1 + p] + sr[0] * rh + i * tr, TILE_ROWS), 0)),
                  pl.BlockSpec((1, tr, wd), lambda p, i, sr: (p, i, 0))],
        out_specs=pl.BlockSpec((1, tr, wd), lambda p, i, sr: (p, i, 0)))
    return pl.pallas_call(
        body, name="add_halves_in", grid_spec=grid_spec, out_shape=jax.ShapeDtypeStruct((n, rh, wd), BF16),
        compiler_params=_cparams("parallel", "parallel"))(scal, dwpt, other)


SEM_SPEC = pl.BlockSpec(memory_space=pltpu.SEMAPHORE)
DATAFLOW_EFFECT = pltpu.SideEffectType.DATAFLOW_SIDE_EFFECTING


def _rs_chips_start(csums):
    na = len(csums)

    def body(*refs):
        s_refs, land_refs = refs[:na], refs[na:2 * na]
        send_sems, recv_sems = refs[2 * na], refs[2 * na + 1]
        token = refs[-1]
        x, y, c, chips = _place()
        j = 2 * x + y
        for a in range(na):
            for k, (cx, cy) in enumerate(chips):
                pltpu.make_async_remote_copy(src_ref=s_refs[a].at[2 * cx + cy], dst_ref=land_refs[a].at[j],
                                             send_sem=send_sems.at[3 * a + k], recv_sem=recv_sems.at[3 * a + k],
                                             device_id=(cx, cy, c), device_id_type=MESH).start()
        token[...] = jnp.zeros_like(token)

    hbm = [pltpu.HBM(s.shape, s.dtype) for s in csums]
    args = [pltpu.with_memory_space_constraint(s, pltpu.HBM) for s in csums]
    args += [pltpu.with_memory_space_constraint(lax.empty(s.shape, s.dtype), pltpu.HBM) for s in csums]
    res = pl.pallas_call(
        body, name="rs_chips_start",
        out_shape=(pltpu.SemaphoreType.DMA((3 * na,)), pltpu.SemaphoreType.DMA((3 * na,)), *hbm, *hbm,
                   jax.ShapeDtypeStruct((8, 128), F32)),
        in_specs=[pl.BlockSpec(memory_space=pltpu.HBM)] * (2 * na),
        out_specs=(SEM_SPEC, SEM_SPEC, *[pl.BlockSpec(memory_space=pltpu.HBM)] * (2 * na),
                   pl.BlockSpec(memory_space=pltpu.VMEM)),
        input_output_aliases={i: 2 + i for i in range(2 * na)},
        compiler_params=pltpu.CompilerParams(has_side_effects=DATAFLOW_EFFECT))(*args)
    return res[0], res[1], list(res[2:2 + na]), list(res[2 + na:2 + 2 * na]), res[-1]


def _rs_chips_wait(send_sems, recv_sems, csums, lands, after):
    na = len(csums)

    def body(*refs):
        s_refs, land_refs = refs[:na], refs[na:2 * na]
        send_sems, recv_sems = refs[2 * na], refs[2 * na + 1]
        x, y, c, chips = _place()
        j = 2 * x + y
        for a in range(na):
            for k, (cx, cy) in enumerate(chips):
                cp = pltpu.make_async_remote_copy(src_ref=s_refs[a].at[2 * cx + cy], dst_ref=land_refs[a].at[2 * cx + cy],
                                                  send_sem=send_sems.at[3 * a + k], recv_sem=recv_sems.at[3 * a + k],
                                                  device_id=(cx, cy, c), device_id_type=MESH)
                cp.wait_send()
                cp.wait_recv()

    hbm = [pltpu.HBM(s.shape, s.dtype) for s in csums]
    res = pl.pallas_call(
        body, name="rs_chips_wait", out_shape=(*hbm, *hbm),
        in_specs=[pl.BlockSpec(memory_space=pltpu.HBM)] * (2 * na) + [SEM_SPEC, SEM_SPEC, pl.BlockSpec(memory_space=pl.ANY)],
        out_specs=tuple([pl.BlockSpec(memory_space=pltpu.HBM)] * (2 * na)),
        input_output_aliases={i: i for i in range(2 * na)},
        compiler_params=pltpu.CompilerParams(has_side_effects=DATAFLOW_EFFECT))(*csums, *lands, send_sems, recv_sems, after)
    return list(res[:na]), list(res[na:])


SWAP_CHUNKS = 4


def _pair_swap(halves):
    na = len(halves)

    def body(*refs):
        h_refs, out_refs = refs[:na], refs[na:2 * na]
        send_sems, recv_sems = refs[2 * na:]
        x, y, c, _ = _place()
        cps = []
        for a in range(na):
            rows = h_refs[a].shape[0] // SWAP_CHUNKS
            assert rows * SWAP_CHUNKS == h_refs[a].shape[0]
            for q in range(SWAP_CHUNKS):
                k = SWAP_CHUNKS * a + q
                cps.append(pltpu.make_async_remote_copy(
                    src_ref=h_refs[a].at[pl.ds(q * rows, rows)], dst_ref=out_refs[a].at[pl.ds(q * rows, rows)],
                    send_sem=send_sems.at[k], recv_sem=recv_sems.at[k], device_id=(x, y, 1 - c), device_id_type=MESH))
        for cp in cps:
            cp.start()
        for cp in cps:
            cp.wait_recv()
        for cp in cps:
            cp.wait_send()

    return pl.pallas_call(
        body, name="pair_swap", out_shape=[jax.ShapeDtypeStruct(h.shape, h.dtype) for h in halves],
        in_specs=[HBM_SPEC] * na, out_specs=[HBM_SPEC] * na,
        scratch_shapes=[pltpu.SemaphoreType.DMA((SWAP_CHUNKS * na,)), pltpu.SemaphoreType.DMA((SWAP_CHUNKS * na,))])(*halves)


def _ag_small(v):
    m_per, n = v.shape

    def body(x_ref, out_ref, send_sems, recv_sems, local_sem):
        x, y, c, chips = _place()
        me, sibling = (x, y, c), (x, y, 1 - c)

        def rows(px, py, pc):
            return out_ref.at[pl.ds((4 * px + 2 * py + pc) * m_per, m_per), :]

        def copy(k, block, to, src=None):
            return pltpu.make_async_remote_copy(
                src_ref=rows(*block) if src is None else src, dst_ref=rows(*block), send_sem=send_sems.at[k],
                recv_sem=recv_sems.at[k], device_id=to, device_id_type=MESH)

        mine = pltpu.make_async_copy(x_ref, rows(*me), local_sem)
        mine.start()
        first = [copy(0, me, sibling, src=x_ref)]
        first += [copy(1 + k, me, (*chip, c), src=x_ref) for k, chip in enumerate(chips)]
        for cp in first:
            cp.start()
        passed = [copy(4 + k, (*chip, c), sibling) for k, chip in enumerate(chips)]
        for k, chip in enumerate(chips):
            copy(1 + k, (*chip, c), me).wait_recv()
            passed[k].start()
        copy(0, sibling, me).wait_recv()
        for k, chip in enumerate(chips):
            copy(4 + k, (*chip, 1 - c), me).wait_recv()
        for cp in first + passed:
            cp.wait_send()
        mine.wait()

    return pl.pallas_call(
        body, name="ag_small", out_shape=jax.ShapeDtypeStruct((8 * m_per, n), v.dtype),
        in_specs=[pl.BlockSpec(memory_space=pltpu.VMEM)], out_specs=pl.BlockSpec(memory_space=pltpu.VMEM),
        scratch_shapes=[pltpu.SemaphoreType.DMA((7,)), pltpu.SemaphoreType.DMA((7,)), pltpu.SemaphoreType.DMA])(v)


def _sum_blocks(a, nblk, name):
    rows, wd = a.shape
    r = rows // nblk
    tr = min(r, ROW_TILE)
    assert r % tr == 0

    def body(*refs):
        acc = refs[0][...].astype(F32)
        for ref in refs[1:nblk]:
            acc = acc + ref[...].astype(F32)
        refs[nblk][...] = acc

    nt = r // tr
    return pl.pallas_call(
        body, name=name, grid=(nt,),
        in_specs=[pl.BlockSpec((tr, wd), functools.partial(lambda i, b: (b * nt + i, 0), b=b)) for b in range(nblk)],
        out_specs=pl.BlockSpec((tr, wd), lambda i: (i, 0)),
        out_shape=jax.ShapeDtypeStruct((r, wd), F32), compiler_params=_cparams("parallel"))(*([a] * nblk))


def _row_tile(rows):
    best = max(t for t in range(16, 513, 16) if rows % t == 0)
    return best


def _sum_chips(by_src, csum, j, name):
    n, rh, wd = by_src.shape
    tr = _row_tile(rh)

    def body(j_ref, *refs):
        own = refs[n][0].astype(F32)
        acc = None
        for k in range(n):
            term = jnp.where(j_ref[0] == k, own, refs[k][0].astype(F32))
            acc = term if acc is None else acc + term
        refs[n + 1][...] = acc

    def other(k):
        return pl.BlockSpec((1, tr, wd), lambda i, jr: (jnp.where(jr[0] == k, (k + 1) % n, k), i, 0))

    grid_spec = pltpu.PrefetchScalarGridSpec(
        num_scalar_prefetch=1, grid=(rh // tr,),
        in_specs=[other(k) for k in range(n)] + [pl.BlockSpec((1, tr, wd), lambda i, jr: (jr[0], i, 0))],
        out_specs=pl.BlockSpec((tr, wd), lambda i, jr: (i, 0)))
    return pl.pallas_call(
        body, name=name, grid_spec=grid_spec, out_shape=jax.ShapeDtypeStruct((rh, wd), F32),
        compiler_params=_cparams("parallel"))(jnp.reshape(j, (1,)).astype(jnp.int32), *([by_src] * n), csum)


def _add_halves(gpack, other, c, name):
    n, _, rh, wd = gpack.shape
    tr = _row_tile(rh)

    def body(c_ref, g_ref, o_ref, out_ref):
        out_ref[0] = (g_ref[0, 0] + o_ref[0]).astype(BF16)

    grid_spec = pltpu.PrefetchScalarGridSpec(
        num_scalar_prefetch=1, grid=(n, rh // tr),
        in_specs=[pl.BlockSpec((1, 1, tr, wd), lambda p, i, cr: (p, cr[0], i, 0)),
                  pl.BlockSpec((1, tr, wd), lambda p, i, cr: (p, i, 0))],
        out_specs=pl.BlockSpec((1, tr, wd), lambda p, i, cr: (p, i, 0)))
    return pl.pallas_call(
        body, name=name, grid_spec=grid_spec, out_shape=jax.ShapeDtypeStruct((n, rh, wd), BF16),
        compiler_params=_cparams("parallel", "parallel"))(jnp.reshape(c, (1,)).astype(jnp.int32), gpack, other)


PACK_W = 1024
ROWS_O_DN = DN_W // N_CHIPS
ROWS_O_DIL = DIL_W * (D_MODEL // N_CHIPS) // PACK_W
ROWS_OUT = D_MODEL // N_CHIPS
ROWS_CONV = 4 * (3 * DN_W // N_CHIPS) // PACK_W
R1 = ROWS_O_DN
R2 = R1 + ROWS_O_DIL
R3 = R2 + ROWS_OUT
R4 = R3 + 16
R5 = R4 + 16
PACK_ROWS = 704
HALF_ROWS = PACK_ROWS // 2
SHARD_PAD = 2880


R6 = R5 + 2 * DN_HEADS

TILE_ROWS = 16
BA_IN_SHARD1 = REF_OFF_BA - SHARD_W
LOCAL_START = (0, SHARD_W, 2 * SHARD_W - 2 * DN_HEADS, 3 * SHARD_W - 2 * DN_HEADS)
LOCAL_END = LOCAL_START[1:] + (OFF_BA,)
WIN_BASE = tuple(s // TILE_ROWS * TILE_ROWS for s in LOCAL_START)


def _to_window(k, shard):
    nba = 2 * DN_HEADS
    body = shard
    if k == 1:
        row = lax.broadcasted_iota(jnp.int32, (SHARD_W - nba, 1), 0)
        body = jnp.where(row < BA_IN_SHARD1, shard[:SHARD_W - nba], shard[nba:])
    lead = LOCAL_START[k] - WIN_BASE[k]
    return jnp.pad(body, ((lead, SHARD_PAD - lead - body.shape[0]), (0, 0)))


def _from_window(k, win, ba):
    nba = 2 * DN_HEADS
    lead = LOCAL_START[k] - WIN_BASE[k]
    if k != 1:
        return win[lead:lead + SHARD_W]
    row = lax.broadcasted_iota(jnp.int32, (SHARD_W, 1), 0)
    before = win[lead:lead + SHARD_W]
    after = jnp.pad(win, ((nba, 0), (0, 0)))[lead:lead + SHARD_W]
    mid = jnp.pad(ba, ((BA_IN_SHARD1, SHARD_W - BA_IN_SHARD1 - nba), (0, 0)))
    return jnp.where(row < BA_IN_SHARD1, before, jnp.where(row < BA_IN_SHARD1 + nba, mid, after))


def _stack_windows(wins, ba):
    pieces = []
    for k in range(N_CHIPS):
        lo = WIN_BASE[k] + (TILE_ROWS if k else 0)
        hi = LOCAL_END[k] // TILE_ROWS * TILE_ROWS
        pieces.append(wins[k][lo - WIN_BASE[k]:hi - WIN_BASE[k]])
        if k + 1 < N_CHIPS:
            assert hi == WIN_BASE[k + 1]
            pieces.append(wins[k][hi - WIN_BASE[k]:hi - WIN_BASE[k] + TILE_ROWS] + wins[k + 1][:TILE_ROWS])
    pieces += [ba, jnp.zeros((PW - OFF_BA - ba.shape[0], ba.shape[1]), ba.dtype)]
    out = jnp.concatenate(pieces, axis=0)
    assert out.shape[0] == PW
    return out


def _to_ref_layout(wpt):
    return jnp.concatenate([wpt[:REF_OFF_BA], wpt[OFF_BA:OFF_BA + 2 * DN_HEADS], wpt[REF_OFF_BA:OFF_BA]], axis=0)


def _from_ref_layout(wt):
    pad = jnp.zeros((PW - PROJ_W, wt.shape[1]), wt.dtype)
    return jnp.concatenate([wt[:REF_OFF_BA], wt[REF_OFF_BA + 2 * DN_HEADS:], wt[REF_OFF_BA:REF_OFF_BA + 2 * DN_HEADS], pad],
                           axis=0)


def _local_step(x, tgt, norm_w, wpt, conv_full, a_log, dt_bias, dn_norm_w, w_o_dn, w_o_dil, w_out, final_norm_w):
    s = x.shape[0]
    h, h_t = _rms_in(x, norm_w)
    proj = _matmul(h, wpt, F32, 2048, 1280, 1024, "proj", nt=True)
    c_pre, qkv = _conv_fwd(proj, conv_full)
    gate_par = jnp.zeros((8, 128), F32).at[0, 8:16].set(a_log[0]).at[1, 8:16].set(dt_bias[0])
    bg = _gates_fwd(proj, gate_par)
    o_a, u, w, vn, tmat, states = _gdr_fwd(qkv, bg)
    oa2, oa2_t = _gdr_out(o_a, proj, dn_norm_w)
    ya = _matmul(oa2, w_o_dn, F32, 512, 1024, 1024, "ya")
    parts = [_att_fwd(proj, g) for g in range(N_DIL)]
    ob, o_att, lse, ob_t = _att_merge(parts, proj)
    yb = _matmul(ob, w_o_dil, F32, 512, 1024, 512, "yb")
    mg, mg_t = _merge(proj, ya, yb)
    t = _matmul(mg, w_out, F32, 512, 1024, 1024, "t_out")
    dx2, dfw, lpart = _final(x, t, final_norm_w, tgt)

    dmg = _matmul(dx2, w_out, F32, 512, 1024, 1024, "d_merged", nt=True)
    dw_out = _matmul(mg_t, dx2, F32, 1024, 1024, 1024, "dw_out")
    dya, dyb, dga, dgb = _merge_bwd(proj, ya, yb, dmg)
    doa2 = _matmul(dya, w_o_dn, F32, 512, 1024, 1024, "d_oa2", nt=True)
    dw_o_dn = _matmul(oa2_t, dya, F32, 1024, 1024, 1024, "dw_o_dn")
    dob = _matmul(dyb, w_o_dil, F32, 512, 512, 1024, "d_ob", nt=True)
    dw_o_dil = _matmul(ob_t, dyb, F32, 512, 1024, 1024, "dw_o_dil")
    do_a, dz_a, ddnw = _gdr_out_bwd(o_a, proj, dn_norm_w, doa2)
    dq_a, dk_a, dv_a, dbg = _gdr_bwd(qkv, bg, u, w, vn, tmat, states, do_a)
    dba, dpar = _gates_bwd(proj, gate_par, dbg)
    dc = _conv_bwd_act(c_pre, dq_a, dk_a, dv_a)
    du_a, dconv = _conv_bwd(proj, dc, conv_full)
    do_att, delta, dz_b = _att_merge_bwd(o_att, proj, dob)
    dqkv_b = [_att_bwd(proj, g, do_att, lse, delta) for g in range(N_DIL)]
    dproj = jnp.concatenate(
        [du_a, dz_a] + [dqkv_b[g][i] for i in range(3) for g in range(N_DIL)]
        + [dz_b, dga, dgb, dba, jnp.zeros((s, PW - OFF_BA - 128), BF16)], axis=1)
    dwpt, dwpt_b = _matmul(h_t, dproj, F32, 1024, 1280, 2048, "dw_in", transpose_out=True, also_bf16=True)

    def finish(after=None):
        dh = _matmul(dproj, wpt, F32, 1024, 1024, 3840, "d_h", after=after)
        grad_x, dnw = _rms_in_bwd(x, norm_w, dh, dx2)
        small = jnp.zeros((8, PACK_W), F32)
        small = small.at[0].set(dnw[0]).at[1].set(dfw[0]).at[2, :DN_D].set(ddnw[0])
        small = small.at[3, :DN_HEADS].set(dpar[0, 8:16]).at[3, DN_HEADS:2 * DN_HEADS].set(dpar[1, 8:16])
        small = small.at[4, 0].set(lpart[0, 0])
        return grad_x, small

    return finish, (dwpt, dwpt_b), dconv, dw_o_dn, dw_o_dil, dw_out


def kernel(x, norm_w, w_in, conv_w, a_log, dt_bias, dn_norm_w, w_o_dn, w_o_dil, w_out, final_norm_w, loss_target, m_norm_w, m_w_in, m_conv_w, m_a_log, m_dt_bias, m_dn_norm_w, m_w_o_dn, m_w_o_dil, m_w_out, m_final_norm_w, v_norm_w, v_w_in, v_conv_w, v_a_log, v_dt_bias, v_dn_norm_w, v_w_o_dn, v_w_o_dil, v_w_out, v_final_norm_w):
    c = lax.axis_index("c")
    j = 2 * lax.axis_index("x") + lax.axis_index("y")
    qw = D_MODEL // N_CHIPS

    cw = conv_w[0].reshape(ROWS_CONV, PACK_W)
    cw = jnp.pad(cw, ((0, 16 - ROWS_CONV), (0, 0)))
    cw_hi = cw.astype(BF16)
    cw_lo = (cw - cw_hi.astype(F32)).astype(BF16)
    shard = w_in[0].T.astype(BF16)
    own_ba = jnp.where(j == 1, shard[BA_IN_SHARD1:BA_IN_SHARD1 + 2 * DN_HEADS], jnp.zeros((2 * DN_HEADS, D_MODEL), BF16))
    pack = jnp.concatenate(
        [w_o_dn[0].astype(BF16), w_o_dil[0].astype(BF16).reshape(ROWS_O_DIL, PACK_W), w_out[0].astype(BF16), cw_hi, cw_lo,
         own_ba, jnp.zeros((PACK_ROWS - R6, PACK_W), BF16)], axis=0).reshape(2, HALF_ROWS, PACK_W)
    chips = range(N_CHIPS)
    own_win = lax.switch(j, [functools.partial(_to_window, k) for k in chips], shard).reshape(2, SHARD_PAD // 2, D_MODEL)
    all_in, allw = _ag_weights([own_win, pack])
    wins = [jnp.where(j == k, own_win, all_in[k]).reshape(SHARD_PAD, D_MODEL) for k in chips]
    allw = [jnp.where(j == k, pack, allw[k]).reshape(PACK_ROWS, PACK_W) for k in chips]
    wpt = _stack_windows(wins, allw[1][R5:R6])
    w_o_dn_full = jnp.concatenate([allw[k][:R1] for k in chips], axis=0)
    w_o_dil_full = jnp.concatenate([allw[k][R1:R2].reshape(DIL_W, qw) for k in chips], axis=1)
    w_out_full = jnp.concatenate([allw[k][R2:R3] for k in chips], axis=0)
    conv_full = jnp.concatenate(
        [(allw[k][R3:R3 + ROWS_CONV].astype(F32) + allw[k][R4:R4 + ROWS_CONV].astype(F32)).reshape(4, 3 * DN_W // N_CHIPS)
         for k in chips], axis=1)

    finish, (dwpt, dwpt_b), dconv, dw_o_dn, dw_o_dil, dw_out = _local_step(
        x[0], loss_target[0], norm_w, wpt, conv_full, a_log, dt_bias, dn_norm_w, w_o_dn_full, w_o_dil_full, w_out_full,
        final_norm_w.reshape(1, D_MODEL))

    cq = 3 * DN_W // N_CHIPS
    gpack = jnp.stack([
        jnp.concatenate(
            [dw_o_dn[k * qw:(k + 1) * qw], dw_o_dil[:, k * qw:(k + 1) * qw].reshape(ROWS_O_DIL, PACK_W),
             dw_out[k * qw:(k + 1) * qw],
             jnp.pad(dconv[:, k * cq:(k + 1) * cq].reshape(ROWS_CONV, PACK_W), ((0, 16 - ROWS_CONV), (0, 0))),
             dwpt[OFF_BA:OFF_BA + 2 * DN_HEADS] if k == 1 else jnp.zeros((2 * DN_HEADS, PACK_W), F32),
             jnp.zeros((PACK_ROWS - R4 - 2 * DN_HEADS, PACK_W), F32)], axis=0)
        for k in chips]).reshape(N_CHIPS, 2, HALF_ROWS, PACK_W)
    sib_in, sib_pack = _rs_pair(dwpt_b, gpack)
    csum_in = _add_halves_win(dwpt, sib_in, c)
    csum_pack = _add_halves(gpack, sib_pack, c, "add_halves_pack")
    send_sems, recv_sems, csums, lands, token = _rs_chips_start([csum_in, csum_pack])
    grad_x, small = finish(after=token)
    (csum_in, csum_pack), (src_in, src_pack) = _rs_chips_wait(send_sems, recv_sems, csums, lands, grad_x)
    half_in = _sum_chips(src_in, csum_in, j, "sum_chips_in")
    half_pack = _sum_chips(src_pack, csum_pack, j, "sum_chips_pack")
    sib_half_in, sib_half_pack = _pair_swap([half_in, half_pack])

    def both_halves(mine, theirs):
        return jnp.where(c == 0, jnp.concatenate([mine, theirs], axis=0), jnp.concatenate([theirs, mine], axis=0))

    g = both_halves(half_pack, sib_half_pack)
    g_w_in = lax.switch(j, [functools.partial(_from_window, k) for k in chips], both_halves(half_in, sib_half_in),
                        g[R4:R4 + 2 * DN_HEADS])
    g_w_o_dn = g[:R1]
    g_w_o_dil = g[R1:R2].reshape(DIL_W, qw)
    g_w_out = g[R2:R3]
    g_conv = g[R3:R3 + ROWS_CONV].reshape(4, cq)

    gs = _sum_blocks(_ag_small(small), 8, "sum_small")
    loss = gs[4, 0]
    w_small = jnp.zeros((8, PACK_W), F32)

    def pack_small(nw, fw, dnw_, al, db):
        t = w_small.at[0].set(nw[0]).at[1].set(fw).at[2, :DN_D].set(dnw_[0])
        return t.at[3, :DN_HEADS].set(al[0]).at[3, DN_HEADS:2 * DN_HEADS].set(db[0])

    sm = _adamw(pack_small(norm_w, final_norm_w, dn_norm_w, a_log, dt_bias), gs,
                pack_small(m_norm_w, m_final_norm_w, m_dn_norm_w, m_a_log, m_dt_bias),
                pack_small(v_norm_w, v_final_norm_w, v_dn_norm_w, v_a_log, v_dt_bias), "adamw_small")

    def unpack_small(t):
        return dict(norm_w=t[0:1], final_norm_w=t[1], dn_norm_w=t[2:3, :DN_D], a_log=t[3:4, :DN_HEADS],
                    dt_bias=t[3:4, DN_HEADS:2 * DN_HEADS])

    res = {"grad": unpack_small(gs)}
    for kind, arr in zip(("delta", "new_m", "new_v"), sm):
        res[kind] = unpack_small(arr)
    big = dict(conv_w=(conv_w, g_conv, m_conv_w, v_conv_w), w_o_dn=(w_o_dn, g_w_o_dn, m_w_o_dn, v_w_o_dn),
               w_o_dil=(w_o_dil, g_w_o_dil, m_w_o_dil, v_w_o_dil), w_out=(w_out, g_w_out, m_w_out, v_w_out))
    for name, (wt, gt, mt, vt) in big.items():
        d, nm, nv = _adamw(wt[0], gt, mt[0], vt[0], "adamw_" + name)
        res["grad"][name] = gt[None]
        res["delta"][name], res["new_m"][name], res["new_v"][name] = d[None], nm[None], nv[None]

    d, nm, nv = _adamw(w_in[0].T, g_w_in, m_w_in[0].T, v_w_in[0].T, "adamw_w_in")
    res["grad"]["w_in"] = g_w_in.T[None]
    res["delta"]["w_in"], res["new_m"]["w_in"], res["new_v"]["w_in"] = d.T[None], nm.T[None], nv.T[None]
    order = ["norm_w", "w_in", "conv_w", "a_log", "dt_bias", "dn_norm_w", "w_o_dn", "w_o_dil", "w_out", "final_norm_w"]
    outs = [loss, grad_x[None]]
    for kind in ("grad", "delta", "new_m", "new_v"):
        outs += [res[kind][nm] for nm in order]
    return tuple(outs)
```

```python
import functools
import math

import jax
import jax.numpy as jnp
from jax import lax
from jax.experimental import pallas as pl
from jax.experimental.pallas import tpu as pltpu

F32 = jnp.float32
BF16 = jnp.bfloat16
MESH = pl.DeviceIdType.MESH

D_MODEL = 1024
DN_HEADS = 8
DN_D = 128
DN_CHUNK = 64
DN_W = DN_HEADS * DN_D
DIL_GROUPS = ((128, 1), (512, 4), (2048, 16))
N_DIL = len(DIL_GROUPS)
DIL_HEADS = 4
DIL_DH = 128
DIL_W = DIL_HEADS * DIL_DH
ATT_BLOCK = 128
NORM_EPS = 1e-6
PROJ_W = 11280
N_CHIPS = 4
SHARD_W = PROJ_W // N_CHIPS

OFF_QKV_A = 0
OFF_Z_A = 3072
OFF_Q_B = 4096
OFF_K_B = 5632
OFF_V_B = 7168
OFF_Z_B = 8704
OFF_G_A = 9216
OFF_G_B = 10240
OFF_BA = 11264
PW = 11520
REF_OFF_BA = 4096

ADAM_LR = 0.001
ADAM_B1 = 0.9
ADAM_B2 = 0.999
ADAM_EPS = 1e-08
ADAM_WD = 0.01
ADAM_STEP = 10

ROW_TILE = 256
NEG = -1e30


def _dot(a, b):
    return jnp.dot(a.astype(BF16), b.astype(BF16), preferred_element_type=F32)


def _dot_nt(a, b):
    return lax.dot_general(a.astype(BF16), b.astype(BF16), (((1,), (1,)), ((), ())), preferred_element_type=F32)


def _dot_tn(a, b):
    return lax.dot_general(a.astype(BF16), b.astype(BF16), (((0,), (0,)), ((), ())), preferred_element_type=F32)


def _split(a):
    hi = a.astype(BF16)
    lo = (a - hi.astype(F32)).astype(BF16)
    return hi, lo


def _dot_exact_lhs(c, a):
    hi, lo = _split(a)
    cb = c.astype(BF16)
    return jnp.dot(cb, hi, preferred_element_type=F32) + jnp.dot(cb, lo, preferred_element_type=F32)


def _dot_exact_rhs(a, c):
    hi, lo = _split(a)
    cb = c.astype(BF16)
    return jnp.dot(hi, cb, preferred_element_type=F32) + jnp.dot(lo, cb, preferred_element_type=F32)


def _dot_tn_exact_rhs(a, c):
    hi, lo = _split(a)
    cb = c.astype(BF16)
    dn = (((0,), (0,)), ((), ()))
    return (lax.dot_general(hi, cb, dn, preferred_element_type=F32)
            + lax.dot_general(lo, cb, dn, preferred_element_type=F32))


def _sigmoid(x):
    return 1.0 / (1.0 + jnp.exp(-x))


def _silu(x):
    return x * _sigmoid(x)


def _silu_grad(x):
    s = _sigmoid(x)
    return s * (1.0 + x * (1.0 - s))


def _softplus(x):
    return jnp.maximum(x, 0.0) + jnp.log(1.0 + jnp.exp(-jnp.abs(x)))


def _cparams(*sem):
    return pltpu.CompilerParams(dimension_semantics=sem)


def _matmul(a, b, out_dtype, tm, tn, tk, name, nt=False, transpose_out=False, after=None, also_bf16=False):
    m, kdim = a.shape
    n = b.shape[0] if nt else b.shape[1]
    tm, tn, tk = min(tm, m), min(tn, n), min(tk, kdim)
    assert m % tm == 0 and n % tn == 0 and kdim % tk == 0, (name, a.shape, b.shape, tm, tn, tk)
    nk = kdim // tk
    dot = _dot_nt if nt else _dot
    b_spec = (pl.BlockSpec((tn, tk), lambda i, j, k: (j, k)) if nt else pl.BlockSpec((tk, tn), lambda i, j, k: (k, j)))
    extra = [] if after is None else [after]
    out_dtypes = [out_dtype] + ([BF16] if also_bf16 else [])

    def emit(o_refs, acc):
        val = acc.T if transpose_out else acc
        for o_ref in o_refs:
            o_ref[...] = val.astype(o_ref.dtype)

    def outs_of(rest):
        return rest[len(extra):len(extra) + len(out_dtypes)]

    if nk == 1:
        def body(a_ref, b_ref, *rest):
            emit(outs_of(rest), dot(a_ref[...], b_ref[...]))
        scratch = []
    else:
        def body(a_ref, b_ref, *rest):
            o_ref, acc_ref = outs_of(rest), rest[-1]
            k = pl.program_id(2)
            p = dot(a_ref[...], b_ref[...])

            @pl.when(k == 0)
            def _():
                acc_ref[...] = p

            @pl.when(k > 0)
            def _():
                acc_ref[...] += p

            @pl.when(k == nk - 1)
            def _():
                emit(o_ref, acc_ref[...])
        scratch = [pltpu.VMEM((tm, tn), F32)]

    if transpose_out:
        out_spec, out_shape = pl.BlockSpec((tn, tm), lambda i, j, k: (j, i)), (n, m)
    else:
        out_spec, out_shape = pl.BlockSpec((tm, tn), lambda i, j, k: (i, j)), (m, n)
    res = pl.pallas_call(
        body, name=name, grid=(m // tm, n // tn, nk),
        in_specs=[pl.BlockSpec((tm, tk), lambda i, j, k: (i, k)), b_spec] + [pl.BlockSpec(memory_space=pl.ANY)] * len(extra),
        out_specs=[out_spec] * len(out_dtypes), out_shape=[jax.ShapeDtypeStruct(out_shape, d) for d in out_dtypes],
        scratch_shapes=scratch, compiler_params=_cparams("parallel", "parallel", "arbitrary"))(a, b, *extra)
    return res if also_bf16 else res[0]


def _rms_in(x, nw):
    s, d = x.shape

    def body(x_ref, w_ref, h_ref, ht_ref):
        xv = x_ref[...]
        r = lax.rsqrt(jnp.mean(xv * xv, axis=-1, keepdims=True) + NORM_EPS)
        h = xv * r * w_ref[...]
        h_ref[...] = h.astype(BF16)
        ht_ref[...] = h.T.astype(BF16)

    return pl.pallas_call(
        body, name="rms_in", grid=(s // ROW_TILE,),
        in_specs=[pl.BlockSpec((ROW_TILE, d), lambda i: (i, 0)), pl.BlockSpec((1, d), lambda i: (0, 0))],
        out_specs=[pl.BlockSpec((ROW_TILE, d), lambda i: (i, 0)), pl.BlockSpec((d, ROW_TILE), lambda i: (0, i))],
        out_shape=[jax.ShapeDtypeStruct((s, d), BF16), jax.ShapeDtypeStruct((d, s), BF16)],
        compiler_params=_cparams("parallel"))(x, nw)


def _rms_in_bwd(x, nw, dh, dx2):
    s, d = x.shape

    def body(x_ref, w_ref, dh_ref, dx2_ref, dx_ref, dw_ref):
        i = pl.program_id(0)
        xv = x_ref[...]
        r = lax.rsqrt(jnp.mean(xv * xv, axis=-1, keepdims=True) + NORM_EPS)
        dhv = dh_ref[...]
        dyw = dhv * w_ref[...]
        dx_ref[...] = dx2_ref[...] + r * dyw - xv * (r * r * r) * jnp.mean(dyw * xv, axis=-1, keepdims=True)
        part = jnp.sum(dhv * xv * r, axis=0, keepdims=True)

        @pl.when(i == 0)
        def _():
            dw_ref[...] = part

        @pl.when(i > 0)
        def _():
            dw_ref[...] += part

    row = pl.BlockSpec((ROW_TILE, d), lambda i: (i, 0))
    vec = pl.BlockSpec((1, d), lambda i: (0, 0))
    return pl.pallas_call(
        body, name="rms_in_bwd", grid=(s // ROW_TILE,), in_specs=[row, vec, row, row], out_specs=[row, vec],
        out_shape=[jax.ShapeDtypeStruct((s, d), F32), jax.ShapeDtypeStruct((1, d), F32)],
        compiler_params=_cparams("arbitrary"))(x, nw, dh, dx2)


def _shift_down(cur, prev8, k):
    rc = pltpu.roll(cur, k, 0)
    rp = pltpu.roll(prev8, k, 0)
    row = lax.broadcasted_iota(jnp.int32, prev8.shape, 0)
    top = jnp.where(row < k, rp, rc[:8])
    return jnp.concatenate([top, rc[8:]], axis=0)


def _shift_up(cur, next8, k):
    t = cur.shape[0]
    rc = pltpu.roll(cur, t - k, 0)
    rn = pltpu.roll(next8, 8 - k, 0)
    row = lax.broadcasted_iota(jnp.int32, next8.shape, 0)
    bot = jnp.where(row >= 8 - k, rn, rc[t - 8:])
    return jnp.concatenate([rc[:t - 8], bot], axis=0)


def _conv_fwd(proj, conv_w):
    s = proj.shape[0]
    t8 = ROW_TILE // 8

    def body(u_ref, up_ref, w_ref, c_ref, y_ref):
        i = pl.program_id(0)
        part = pl.program_id(1)
        cur = u_ref[...]
        prev8 = jnp.where(i > 0, up_ref[...], 0.0)
        w = w_ref[...]
        c = cur * w[3:4, :]
        for k in (1, 2, 3):
            c = c + _shift_down(cur, prev8, k) * w[3 - k:4 - k, :]
        c_ref[...] = c
        a = _silu(c)
        for h in range(DN_HEADS):
            ah = a[:, h * DN_D:(h + 1) * DN_D]
            r = lax.rsqrt(jnp.sum(ah * ah, axis=-1, keepdims=True) + NORM_EPS)
            y_ref[:, h * DN_D:(h + 1) * DN_D] = jnp.where(part < 2, ah * r, ah)

    return pl.pallas_call(
        body, name="conv_fwd", grid=(s // ROW_TILE, 3),
        in_specs=[pl.BlockSpec((ROW_TILE, DN_W), lambda i, p: (i, p)),
                  pl.BlockSpec((8, DN_W), lambda i, p: (jnp.maximum(i * t8 - 1, 0), p)),
                  pl.BlockSpec((4, DN_W), lambda i, p: (0, p))],
        out_specs=[pl.BlockSpec((ROW_TILE, DN_W), lambda i, p: (i, p))] * 2,
        out_shape=[jax.ShapeDtypeStruct((s, 3 * DN_W), F32)] * 2,
        compiler_params=_cparams("parallel", "parallel"))(proj, proj, conv_w)


def _conv_bwd_act(c, dq, dk, dv):
    s = c.shape[0]

    def body(c_ref, dq_ref, dk_ref, dv_ref, dc_ref):
        for part, d_ref in enumerate((dq_ref, dk_ref, dv_ref)):
            for h in range(DN_HEADS):
                sl = slice(part * DN_W + h * DN_D, part * DN_W + (h + 1) * DN_D)
                ch = c_ref[:, sl]
                dyh = d_ref[:, h * DN_D:(h + 1) * DN_D]
                if part < 2:
                    ah = _silu(ch)
                    r = lax.rsqrt(jnp.sum(ah * ah, axis=-1, keepdims=True) + NORM_EPS)
                    dyh = r * dyh - ah * (r * r * r) * jnp.sum(dyh * ah, axis=-1, keepdims=True)
                dc_ref[:, sl] = dyh * _silu_grad(ch)

    wide = pl.BlockSpec((ROW_TILE, 3 * DN_W), lambda i: (i, 0))
    row = pl.BlockSpec((ROW_TILE, DN_W), lambda i: (i, 0))
    return pl.pallas_call(
        body, name="conv_bwd_act", grid=(s // ROW_TILE,), in_specs=[wide, row, row, row], out_specs=wide,
        out_shape=jax.ShapeDtypeStruct((s, 3 * DN_W), F32), compiler_params=_cparams("parallel"))(c, dq, dk, dv)


def _conv_bwd(proj, dc, conv_w):
    s = proj.shape[0]
    t8 = ROW_TILE // 8
    nrow = s // ROW_TILE
    last8 = s // 8 - 1

    def body(u_ref, dc_ref, dcn_ref, w_ref, du_ref, dw_ref):
        i = pl.program_id(1)
        cur = u_ref[...]
        dcv = dc_ref[...]
        next8 = jnp.where(i < nrow - 1, dcn_ref[...], 0.0)
        w = w_ref[...]

        @pl.when(i == 0)
        def _():
            dw_ref[...] = jnp.zeros_like(dw_ref)

        du = dcv * w[3:4, :]
        dw_ref[3:4, :] += jnp.sum(cur * dcv, axis=0, keepdims=True)
        for k in (1, 2, 3):
            ahead = _shift_up(dcv, next8, k)
            du = du + ahead * w[3 - k:4 - k, :]
            dw_ref[3 - k:4 - k, :] += jnp.sum(cur * ahead, axis=0, keepdims=True)
        du_ref[...] = du.astype(BF16)

    blk = pl.BlockSpec((ROW_TILE, DN_W), lambda p, i: (i, p))
    return pl.pallas_call(
        body, name="conv_bwd", grid=(3, nrow),
        in_specs=[blk, blk, pl.BlockSpec((8, DN_W), lambda p, i: (jnp.minimum((i + 1) * t8, last8), p)),
                  pl.BlockSpec((4, DN_W), lambda p, i: (0, p))],
        out_specs=[blk, pl.BlockSpec((4, DN_W), lambda p, i: (0, p))],
        out_shape=[jax.ShapeDtypeStruct((s, 3 * DN_W), BF16), jax.ShapeDtypeStruct((4, 3 * DN_W), F32)],
        compiler_params=_cparams("parallel", "arbitrary"))(proj, dc, dc, conv_w)


def _gates_fwd(proj, gate_par):
    s = proj.shape[0]

    def body(ba_ref, par_ref, o_ref):
        v = ba_ref[...]
        lane = lax.broadcasted_iota(jnp.int32, v.shape, 1)
        beta = _sigmoid(v)
        g = -jnp.exp(par_ref[0:1, :]) * _softplus(v + par_ref[1:2, :])
        o_ref[...] = jnp.where(lane < DN_HEADS, beta, jnp.where(lane < 2 * DN_HEADS, g, 0.0))

    return pl.pallas_call(
        body, name="gates_fwd", grid=(s // ROW_TILE,),
        in_specs=[pl.BlockSpec((ROW_TILE, 128), lambda i: (i, OFF_BA // 128)), pl.BlockSpec((8, 128), lambda i: (0, 0))],
        out_specs=pl.BlockSpec((ROW_TILE, 128), lambda i: (i, 0)),
        out_shape=jax.ShapeDtypeStruct((s, 128), F32), compiler_params=_cparams("parallel"))(proj, gate_par)


def _gates_bwd(proj, gate_par, dbg):
    s = proj.shape[0]

    def body(ba_ref, par_ref, d_ref, o_ref, dpar_ref):
        i = pl.program_id(0)
        v = ba_ref[...]
        dv = d_ref[...]
        lane = lax.broadcasted_iota(jnp.int32, v.shape, 1)
        beta = _sigmoid(v)
        nega = -jnp.exp(par_ref[0:1, :])
        xs = v + par_ref[1:2, :]
        dsp = dv * nega * _sigmoid(xs)
        dal = dv * nega * _softplus(xs)
        is_b = lane < DN_HEADS
        is_g = jnp.logical_and(lane >= DN_HEADS, lane < 2 * DN_HEADS)
        o_ref[:, :128] = jnp.where(is_b, dv * beta * (1.0 - beta), jnp.where(is_g, dsp, 0.0)).astype(BF16)
        o_ref[:, 128:] = jnp.zeros((ROW_TILE, PW - OFF_BA - 128), BF16)
        r0 = jnp.sum(jnp.where(is_g, dal, 0.0), axis=0, keepdims=True)
        r1 = jnp.sum(jnp.where(is_g, dsp, 0.0), axis=0, keepdims=True)

        @pl.when(i == 0)
        def _():
            dpar_ref[...] = jnp.zeros_like(dpar_ref)

        dpar_ref[0:1, :] += r0
        dpar_ref[1:2, :] += r1

    return pl.pallas_call(
        body, name="gates_bwd", grid=(s // ROW_TILE,),
        in_specs=[pl.BlockSpec((ROW_TILE, 128), lambda i: (i, OFF_BA // 128)), pl.BlockSpec((8, 128), lambda i: (0, 0)),
                  pl.BlockSpec((ROW_TILE, 128), lambda i: (i, 0))],
        out_specs=[pl.BlockSpec((ROW_TILE, PW - OFF_BA), lambda i: (i, 0)), pl.BlockSpec((8, 128), lambda i: (0, 0))],
        out_shape=[jax.ShapeDtypeStruct((s, PW - OFF_BA), BF16), jax.ShapeDtypeStruct((8, 128), F32)],
        compiler_params=_cparams("arbitrary"))(proj, gate_par, dbg)


def _chunk_masks():
    c = DN_CHUNK
    ii = lax.broadcasted_iota(jnp.int32, (c, c), 0)
    jj = lax.broadcasted_iota(jnp.int32, (c, c), 1)
    return dict(ii=ii, jj=jj, lower=(ii >= jj), strict=(ii > jj), eye=(ii == jj),
                lower_f=(ii >= jj).astype(BF16), upper_f=(ii <= jj).astype(BF16), ones8=jnp.ones((8, c), BF16))


class _Heads:
    def __init__(self, xs):
        self.xs = list(xs)

    def _bin(self, o, f):
        if isinstance(o, _Heads):
            return _Heads([f(a, b) for a, b in zip(self.xs, o.xs)])
        return _Heads([f(a, o) for a in self.xs])

    def __add__(self, o):
        return self._bin(o, lambda a, b: a + b)

    def __sub__(self, o):
        return self._bin(o, lambda a, b: a - b)

    def __mul__(self, o):
        return self._bin(o, lambda a, b: a * b)

    __radd__ = __add__
    __rmul__ = __mul__

    def __neg__(self):
        return _Heads([-a for a in self.xs])

    def __getitem__(self, i):
        return _Heads([a[i] for a in self.xs])


def _hmap(f, *args):
    n = next(len(a.xs) for a in args if isinstance(a, _Heads))
    return _Heads([f(*[(a.xs[h] if isinstance(a, _Heads) else a) for a in args]) for h in range(n)])


def _hdot(a, b):
    return _hmap(_dot, a, b)


def _hdot_nt(a, b):
    return _hmap(_dot_nt, a, b)


def _hdot_tn(a, b):
    return _hmap(_dot_tn, a, b)


def _hcat(a, b, axis):
    return _hmap(lambda x, y: jnp.concatenate([x, y], axis=axis), a, b)


def _hsum(a, axis):
    return _hmap(lambda t: jnp.sum(t, axis=axis, keepdims=True), a)


def _hwhere(c, a, b):
    return _hmap(jnp.where, c, a, b)


def _chunk_gates(mk, bg):
    c = DN_CHUNK
    gc_all = _dot_exact_lhs(mk["lower_f"], bg)
    rows = jnp.concatenate([gc_all, gc_all], axis=0).T
    hs = range(DN_HEADS)
    return (_Heads(bg[:, h:h + 1] for h in hs), _Heads(gc_all[:, DN_HEADS + h:DN_HEADS + h + 1] for h in hs),
            _Heads(rows[DN_HEADS + h:DN_HEADS + h + 1, :] for h in hs))


def _chunk_common(mk, q, k, beta_col, gc_col, gc_r):
    c = DN_CHUNK
    lower, strict = mk["lower"], mk["strict"]
    qs = q * (DN_D ** -0.5)
    beta_b = _hmap(lambda t: jnp.broadcast_to(t, (c, DN_D)), beta_col)
    gc_b = _hmap(lambda t: jnp.broadcast_to(t, (c, DN_D)), gc_col)
    gc_sq = gc_b[:, :c]
    gam = _hwhere(lower, _hmap(lambda t: jnp.exp(jnp.minimum(t, 0.0)), gc_sq - gc_r[:, :c]), 0.0)
    egc = _hmap(jnp.exp, gc_b)
    gl = gc_b[c - 1:c, :]
    ekd = _hmap(jnp.exp, gl - gc_b)
    dl = _hmap(jnp.exp, gl)
    kb = k * beta_b
    scores = _hdot_nt(_hcat(kb, qs, 0), k)
    a_strict = _hwhere(strict, scores[:c] * gam, 0.0)
    aqk = _hwhere(lower, scores[c:] * gam, 0.0)
    return dict(k=k, qs=qs, beta_b=beta_b, gc_b=gc_b, gam=gam, egc=egc, ekd=ekd, dl=dl, kb=kb, a_strict=a_strict, aqk=aqk)


def _unit_lower_inverse_minus_eye(n_strict, ii, jj):
    same = lax.shift_right_logical(ii, 4) == lax.shift_right_logical(jj, 4)
    dmat = _hwhere(same, n_strict, 0.0)
    omat = n_strict - dmat
    d2 = _hdot(dmat, dmat)
    d4 = _hdot(d2, d2)
    d8 = _hdot(d4, d4)
    x1 = d2 - dmat - _hdot(dmat, d2)
    x2 = x1 + d4 + _hdot(x1, d4)
    x3 = x2 + d8 + _hdot(x2, d8)
    n1 = omat + _hdot(x3, omat)
    n2 = _hdot(n1, n1)
    y = n2 - n1 - _hdot(n1, n2)
    return y + x3 + _hdot(y, x3)


GDR_HEAD_SETS = (range(0, DN_HEADS),)


def _gdr_fwd(qkv, bg):
    s = qkv.shape[0]
    c = DN_CHUNK
    n = s // c

    def body(q_ref, k_ref, v_ref, bg_ref, o_ref, u_ref, w_ref, vn_ref, tm_ref, st_ref, state):
        @pl.when(pl.program_id(0) == 0)
        def _():
            state[...] = jnp.zeros_like(state)

        mk = _chunk_masks()
        gates = _chunk_gates(mk, bg_ref[...])
        for hs in GDR_HEAD_SETS:
            sls = [slice(h * DN_D, (h + 1) * DN_D) for h in hs]
            cm = _chunk_common(mk, _Heads(q_ref[:, sl] for sl in sls), _Heads(k_ref[:, sl] for sl in sls),
                               *[_Heads(g.xs[h] for h in hs) for g in gates])
            tm = _unit_lower_inverse_minus_eye(cm["a_strict"], mk["ii"], mk["jj"])
            rhs_u = _Heads(v_ref[:, sl] for sl in sls) * cm["beta_b"]
            rhs_w = cm["kb"] * cm["egc"]
            t_rhs = _hdot(tm, _hcat(rhs_u, rhs_w, 1))
            u = rhs_u + t_rhs[:, :DN_D]
            w = rhs_w + t_rhs[:, DN_D:]
            st = _Heads(state[h] for h in hs)
            on_state = _hdot(_hcat(w, cm["qs"] * cm["egc"], 0), st)
            v_new = u - on_state[:c]
            o = on_state[c:] + _hdot(cm["aqk"], v_new)
            st_new = st * cm["dl"] + _hdot_tn(cm["k"] * cm["ekd"], v_new)
            for i, (h, sl) in enumerate(zip(hs, sls)):
                o_ref[:, sl] = o.xs[i]
                u_ref[:, sl] = u.xs[i]
                w_ref[:, sl] = w.xs[i]
                vn_ref[:, sl] = v_new.xs[i]
                tm_ref[h, 0] = tm.xs[i]
                st_ref[h, 0] = st.xs[i]
                state[h] = st_new.xs[i]

    def part(p):
        return pl.BlockSpec((c, DN_W), lambda j: (j, p))

    return pl.pallas_call(
        body, name="gdr_fwd", grid=(n,),
        in_specs=[part(0), part(1), part(2), pl.BlockSpec((c, 128), lambda j: (j, 0))],
        out_specs=[part(0)] * 4 + [pl.BlockSpec((DN_HEADS, 1, c, c), lambda j: (0, j, 0, 0)),
                                   pl.BlockSpec((DN_HEADS, 1, DN_D, DN_D), lambda j: (0, j, 0, 0))],
        out_shape=[jax.ShapeDtypeStruct((s, DN_W), F32)] * 4
        + [jax.ShapeDtypeStruct((DN_HEADS, n, c, c), F32), jax.ShapeDtypeStruct((DN_HEADS, n, DN_D, DN_D), F32)],
        scratch_shapes=[pltpu.VMEM((DN_HEADS, DN_D, DN_D), F32)],
        compiler_params=_cparams("arbitrary"))(qkv, qkv, qkv, bg)


def _gdr_bwd(qkv, bg, u, w, vn, tmat, states, do):
    s = qkv.shape[0]
    c = DN_CHUNK
    n = s // c

    def body(q_ref, k_ref, v_ref, bg_ref, u_ref, w_ref, vn_ref, tm_ref, st_ref, do_ref,
             dq_ref, dk_ref, dv_ref, dbg_ref, dstate):
        @pl.when(pl.program_id(0) == 0)
        def _():
            dstate[...] = jnp.zeros_like(dstate)

        mk = _chunk_masks()
        lower, strict = mk["lower"], mk["strict"]
        bg = bg_ref[...]
        ones = jnp.ones((c, DN_D), BF16)
        rowi = lax.broadcasted_iota(jnp.int32, (c, DN_D), 0)
        lane = lax.broadcasted_iota(jnp.int32, (c, 128), 1)
        hs = range(DN_HEADS)
        sls = [slice(h * DN_D, (h + 1) * DN_D) for h in hs]

        def heads_of(ref):
            return _Heads(ref[:, sl] for sl in sls)

        cm = _chunk_common(mk, heads_of(q_ref), heads_of(k_ref), *_chunk_gates(mk, bg))
        k, qs, beta_b = cm["k"], cm["qs"], cm["beta_b"]
        gam, egc, ekd, dl, kb = cm["gam"], cm["egc"], cm["ekd"], cm["dl"], cm["kb"]
        aqk, a_strict = cm["aqk"], cm["a_strict"]
        v, uu, ww, v_new, dov = heads_of(v_ref), heads_of(u_ref), heads_of(w_ref), heads_of(vn_ref), heads_of(do_ref)
        st = _Heads(st_ref[h, 0] for h in hs)
        dsn = _Heads(dstate[h] for h in hs)
        qd = qs * egc
        kd = k * ekd

        dv_new = _hdot_tn(aqk, dov) + _hdot(kd, dsn)
        do_sv = _hdot_nt(dov, _hcat(st, v_new, 0))
        dqd = do_sv[:, :DN_D]
        daqk = _hwhere(lower, do_sv[:, DN_D:], 0.0)
        dkd = _hdot_nt(v_new, dsn)
        ddl = _hsum(_hsum(dsn * st, 1), 0)
        dw = -_hdot_nt(dv_new, st)
        ds_new = dsn * dl + _hdot_tn(_hcat(qd, -ww, 0), _hcat(dov, dv_new, 0))

        tm = _Heads(tm_ref[h, 0] for h in hs)
        tt = _hdot_tn(tm, _hcat(dv_new, dw, 1))
        dru = dv_new + tt[:, :DN_D]
        drw = dw + tt[:, DN_D:]
        dn = _hwhere(strict, -_hdot_nt(_hcat(dru, drw, 1), _hcat(uu, ww, 1)), 0.0)
        dag = dn * gam
        dqg = daqk * gam
        both = _hcat(dag, dqg, 0)
        on_k = _hdot(both, k)
        dkb = on_k[:c] + drw * egc
        dqs = on_k[c:] + dqd * egc
        dk = _hdot_tn(both, _hcat(kb, qs, 0)) + dkb * beta_b + dkd * ekd
        pmat = dn * a_strict + daqk * aqk
        tkd = _hsum(dkd * kd, -1)
        dgc = (_hsum(pmat, -1) - _hmap(_dot_tn_exact_rhs, pmat, ones) + _hsum(drw * (kb * egc), -1)
               + _hsum(dqd * qd, -1) - tkd)
        last = _hsum(tkd, 0) + ddl * dl
        dgc = dgc + _hwhere(rowi == c - 1, last, 0.0)
        dbeta = _hsum(dru * v, -1) + _hsum(dkb * k, -1)
        dq = dqs * (DN_D ** -0.5)
        dv = dru * beta_b

        dgc_all = jnp.zeros((c, 128), F32)
        dbg = jnp.zeros((c, 128), F32)
        for h, sl in zip(hs, sls):
            dq_ref[:, sl] = dq.xs[h]
            dk_ref[:, sl] = dk.xs[h]
            dv_ref[:, sl] = dv.xs[h]
            dstate[h] = ds_new.xs[h]
            dgc_all = dgc_all + jnp.where(lane == DN_HEADS + h, dgc.xs[h], 0.0)
            dbg = dbg + jnp.where(lane == h, dbeta.xs[h], 0.0)
        dbg_ref[...] = dbg + _dot_exact_lhs(mk["upper_f"], dgc_all)

    def part(p):
        return pl.BlockSpec((c, DN_W), lambda j: (n - 1 - j, p))

    vec = pl.BlockSpec((c, 128), lambda j: (n - 1 - j, 0))
    return pl.pallas_call(
        body, name="gdr_bwd", grid=(n,),
        in_specs=[part(0), part(1), part(2), vec, part(0), part(0), part(0),
                  pl.BlockSpec((DN_HEADS, 1, c, c), lambda j: (0, n - 1 - j, 0, 0)),
                  pl.BlockSpec((DN_HEADS, 1, DN_D, DN_D), lambda j: (0, n - 1 - j, 0, 0)), part(0)],
        out_specs=[part(0), part(0), part(0), vec],
        out_shape=[jax.ShapeDtypeStruct((s, DN_W), F32)] * 3 + [jax.ShapeDtypeStruct((s, 128), F32)],
        scratch_shapes=[pltpu.VMEM((DN_HEADS, DN_D, DN_D), F32)],
        compiler_params=_cparams("arbitrary"))(qkv, qkv, qkv, bg, u, w, vn, tmat, states, do)


def _gdr_out(o, proj, dnw):
    s = o.shape[0]

    def body(o_ref, z_ref, w_ref, y_ref, yt_ref):
        ov, zv, wv = o_ref[...], z_ref[...], w_ref[...]
        for h in range(DN_HEADS):
            sl = slice(h * DN_D, (h + 1) * DN_D)
            oh = ov[:, sl]
            r = lax.rsqrt(jnp.mean(oh * oh, axis=-1, keepdims=True) + NORM_EPS)
            y = (oh * r * wv) * _silu(zv[:, sl])
            y_ref[:, sl] = y.astype(BF16)
            yt_ref[sl, :] = y.T.astype(BF16)

    row = pl.BlockSpec((ROW_TILE, DN_W), lambda i: (i, 0))
    return pl.pallas_call(
        body, name="gdr_out", grid=(s // ROW_TILE,),
        in_specs=[row, pl.BlockSpec((ROW_TILE, DN_W), lambda i: (i, OFF_Z_A // DN_W)), pl.BlockSpec((1, DN_D), lambda i: (0, 0))],
        out_specs=[row, pl.BlockSpec((DN_W, ROW_TILE), lambda i: (0, i))],
        out_shape=[jax.ShapeDtypeStruct((s, DN_W), BF16), jax.ShapeDtypeStruct((DN_W, s), BF16)],
        compiler_params=_cparams("parallel"))(o, proj, dnw)


def _gdr_out_bwd(o, proj, dnw, dy):
    s = o.shape[0]

    def body(o_ref, z_ref, w_ref, dy_ref, do_ref, dz_ref, dw_ref):
        i = pl.program_id(0)
        ov, zv, wv, dyv = o_ref[...], z_ref[...], w_ref[...], dy_ref[...]
        acc = jnp.zeros((1, DN_D), F32)
        for h in range(DN_HEADS):
            sl = slice(h * DN_D, (h + 1) * DN_D)
            oh, zh, dh = ov[:, sl], zv[:, sl], dyv[:, sl]
            r = lax.rsqrt(jnp.mean(oh * oh, axis=-1, keepdims=True) + NORM_EPS)
            dn = dh * _silu(zh)
            dz_ref[:, sl] = (dh * (oh * r * wv) * _silu_grad(zh)).astype(BF16)
            acc = acc + jnp.sum(dn * oh * r, axis=0, keepdims=True)
            dnw_ = dn * wv
            do_ref[:, sl] = r * dnw_ - oh * (r * r * r) * jnp.mean(dnw_ * oh, axis=-1, keepdims=True)

        @pl.when(i == 0)
        def _():
            dw_ref[...] = acc

        @pl.when(i > 0)
        def _():
            dw_ref[...] += acc

    row = pl.BlockSpec((ROW_TILE, DN_W), lambda i: (i, 0))
    vec = pl.BlockSpec((1, DN_D), lambda i: (0, 0))
    return pl.pallas_call(
        body, name="gdr_out_bwd", grid=(s // ROW_TILE,),
        in_specs=[row, pl.BlockSpec((ROW_TILE, DN_W), lambda i: (i, OFF_Z_A // DN_W)), vec, row],
        out_specs=[row, row, vec],
        out_shape=[jax.ShapeDtypeStruct((s, DN_W), F32), jax.ShapeDtypeStruct((s, DN_W), BF16),
                   jax.ShapeDtypeStruct((1, DN_D), F32)],
        compiler_params=_cparams("arbitrary"))(o, proj, dnw, dy)


def _slope(group, head):
    idx = (group * DIL_HEADS + head + 1).astype(F32)
    return jnp.exp(jnp.full((1, 128), -8.0 * math.log(2.0) / (N_DIL * DIL_HEADS), F32) * idx)


def _att_scores(qb, k_cur, k_prev, slope_d, has_prev):
    iq = lax.broadcasted_iota(jnp.int32, (ATT_BLOCK, ATT_BLOCK), 0)
    jk = lax.broadcasted_iota(jnp.int32, (ATT_BLOCK, ATT_BLOCK), 1)
    dist_c = (iq - jk).astype(F32)
    s_cur = jnp.where(iq >= jk, _dot_nt(qb, k_cur) - slope_d * dist_c, NEG)
    s_prev = jnp.where(jnp.logical_and(jk >= iq, has_prev),
                       _dot_nt(qb, k_prev) - slope_d * (dist_c + float(ATT_BLOCK)), NEG)
    return s_cur, s_prev


def _att_scores_whole(qb, k, slope_d):
    n = 2 * ATT_BLOCK
    dist = lax.broadcasted_iota(jnp.int32, (n, n), 0) - lax.broadcasted_iota(jnp.int32, (n, n), 1)
    valid = jnp.logical_and(dist >= 0, dist <= ATT_BLOCK)
    return jnp.where(valid, _dot_nt(qb, k) - slope_d[:, 0:1] * dist.astype(F32), NEG)


def _att_tiles(i, dil, nb):
    tiles = nb // 2
    per = dil * tiles // ATT_UNROLL
    assert nb % 2 == 0 and tiles >= 2 and per * ATT_UNROLL == dil * tiles
    for i0 in range(per):
        ts = [divmod(i0 + u * per, tiles) for u in range(ATT_UNROLL)]
        assert all(a[0] != b[0] or abs(a[1] - b[1]) >= 2 for n, a in enumerate(ts) for b in ts[n + 1:])
    qrows, krows, has_prev = [], [], []
    for u in range(ATT_UNROLL):
        t = i + u * per
        r = lax.div(t, tiles)
        j = lax.rem(t, tiles)
        qbase = r + dil * 2 * ATT_BLOCK * j
        kbase = qbase - dil * ATT_BLOCK * jnp.minimum(j, 1)
        if dil == 1:
            qbase, kbase = pl.multiple_of(qbase, ATT_BLOCK), pl.multiple_of(kbase, ATT_BLOCK)
        qrows.append(pl.ds(qbase, 2 * ATT_BLOCK, stride=dil))
        krows.append(pl.ds(kbase, 3 * ATT_BLOCK, stride=dil))
        has_prev.append(j > 0)
    return qrows, krows, has_prev


def _att_scores_tile(qb, k, slope_d, has_prev):
    iq = lax.broadcasted_iota(jnp.int32, (2 * ATT_BLOCK, 3 * ATT_BLOCK), 0)
    ck = lax.broadcasted_iota(jnp.int32, (2 * ATT_BLOCK, 3 * ATT_BLOCK), 1)
    dist = iq - ck + jnp.where(has_prev, ATT_BLOCK, 0)
    valid = jnp.logical_and(dist >= 0, dist <= ATT_BLOCK)
    return jnp.where(valid, _dot_nt(qb, k) - slope_d[:, 0:1] * dist.astype(F32), NEG)


ATT_UNROLL = 4


def _att_blocks(i, dil, nb):
    per = dil * nb // ATT_UNROLL
    assert per * ATT_UNROLL == dil * nb
    for i0 in range(per):
        blocks = [divmod(i0 + u * per, nb) for u in range(ATT_UNROLL)]
        assert all(a[0] != b[0] or abs(a[1] - b[1]) >= 2 for n, a in enumerate(blocks) for b in blocks[n + 1:])
    curs, prvs, has_prev = [], [], []
    for u in range(ATT_UNROLL):
        t = i + u * per
        r = lax.div(t, nb)
        j = lax.rem(t, nb)
        base = r + dil * ATT_BLOCK * j
        pbase = base - dil * ATT_BLOCK * jnp.minimum(j, 1)
        if dil == 1:
            base, pbase = pl.multiple_of(base, ATT_BLOCK), pl.multiple_of(pbase, ATT_BLOCK)
        curs.append(pl.ds(base, ATT_BLOCK, stride=dil))
        prvs.append(pl.ds(pbase, ATT_BLOCK, stride=dil))
        has_prev.append(j > 0)
    return curs, prvs, has_prev


def _att_fwd(proj, group):
    s = proj.shape[0]
    dil = DIL_GROUPS[group][1]
    assert DIL_GROUPS[group][0] // dil == ATT_BLOCK
    nb = s // dil // ATT_BLOCK
    assert nb * dil * ATT_BLOCK == s

    def body(q_ref, k_ref, v_ref, num_ref, den_ref, mx_ref):
        slope_d = _slope(group, pl.program_id(0)) * float(dil)

        def step(i, carry):
            curs, prvs, has_prev = _att_blocks(i, dil, nb)
            us = range(ATT_UNROLL)
            qb = [q_ref[c, :] * (DIL_DH ** -0.5) for c in curs]
            sc = [_att_scores(qb[u], k_ref[curs[u], :], k_ref[prvs[u], :], slope_d, has_prev[u]) for u in us]
            mx = [jnp.maximum(jnp.max(a, axis=-1, keepdims=True), jnp.max(b, axis=-1, keepdims=True)) for a, b in sc]
            p_cur = [jnp.exp(sc[u][0] - mx[u]) for u in us]
            p_prev = [jnp.exp(sc[u][1] - mx[u]) for u in us]
            den = [jnp.sum(p_cur[u], axis=-1, keepdims=True) + jnp.sum(p_prev[u], axis=-1, keepdims=True) for u in us]
            num = [_dot(p_cur[u], v_ref[curs[u], :]) + _dot(p_prev[u], v_ref[prvs[u], :]) for u in us]
            for u in us:
                num_ref[curs[u], :] = num[u]
                den_ref[curs[u], :] = jnp.broadcast_to(den[u], (ATT_BLOCK, DIL_DH))
                mx_ref[curs[u], :] = jnp.broadcast_to(mx[u], (ATT_BLOCK, DIL_DH))
            return carry

        def step_whole(i, carry):
            rows = [pl.ds(i * ATT_UNROLL + u, 2 * ATT_BLOCK, stride=dil) for u in range(ATT_UNROLL)]
            sc = [_att_scores_whole(q_ref[r, :] * (DIL_DH ** -0.5), k_ref[r, :], slope_d) for r in rows]
            mx = [jnp.max(a, axis=-1, keepdims=True) for a in sc]
            p = [jnp.exp(a - m) for a, m in zip(sc, mx)]
            num = [_dot(pu, v_ref[r, :]) for pu, r in zip(p, rows)]
            for u, r in enumerate(rows):
                num_ref[r, :] = num[u]
                den_ref[r, :] = jnp.broadcast_to(jnp.sum(p[u], axis=-1, keepdims=True), (2 * ATT_BLOCK, DIL_DH))
                mx_ref[r, :] = jnp.broadcast_to(mx[u], (2 * ATT_BLOCK, DIL_DH))
            return carry

        def step_tile(i, carry):
            qrows, krows, has_prev = _att_tiles(i, dil, nb)
            us = range(ATT_UNROLL)
            sc = [_att_scores_tile(q_ref[qrows[u], :] * (DIL_DH ** -0.5), k_ref[krows[u], :], slope_d, has_prev[u]) for u in us]
            mx = [jnp.max(a, axis=-1, keepdims=True) for a in sc]
            p = [jnp.exp(a - m) for a, m in zip(sc, mx)]
            num = [_dot(p[u], v_ref[krows[u], :]) for u in us]
            for u in us:
                num_ref[qrows[u], :] = num[u]
                den_ref[qrows[u], :] = jnp.broadcast_to(jnp.sum(p[u], axis=-1, keepdims=True), (2 * ATT_BLOCK, DIL_DH))
                mx_ref[qrows[u], :] = jnp.broadcast_to(mx[u], (2 * ATT_BLOCK, DIL_DH))
            return carry

        if nb == 2:
            lax.fori_loop(0, dil // ATT_UNROLL, step_whole, 0)
        elif nb % 2 == 0:
            lax.fori_loop(0, dil * nb // 2 // ATT_UNROLL, step_tile, 0)
        else:
            lax.fori_loop(0, dil * nb // ATT_UNROLL, step, 0)

    def col(off):
        return pl.BlockSpec((s, DIL_DH), lambda h: (0, off // DIL_DH + group * DIL_HEADS + h))

    out = pl.BlockSpec((s, DIL_DH), lambda h: (0, h))
    return pl.pallas_call(
        body, name=f"att_fwd{group}", grid=(DIL_HEADS,), in_specs=[col(OFF_Q_B), col(OFF_K_B), col(OFF_V_B)],
        out_specs=[out, out, out], out_shape=[jax.ShapeDtypeStruct((s, DIL_W), F32)] * 3,
        compiler_params=_cparams("parallel"))(proj, proj, proj)


def _att_bwd(proj, group, do, lse, delta):
    s = proj.shape[0]
    dil = DIL_GROUPS[group][1]
    nb = s // dil // ATT_BLOCK

    def body(q_ref, k_ref, v_ref, do_ref, lse_ref, dl_ref, dq_ref, dk_ref, dv_ref, dq_acc, dk_acc, dv_acc):
        slope_d = _slope(group, pl.program_id(0)) * float(dil)
        dk_acc[...] = jnp.zeros_like(dk_acc)
        dv_acc[...] = jnp.zeros_like(dv_acc)

        def step(i, carry):
            curs, prvs, has_prev = _att_blocks(i, dil, nb)
            us = range(ATT_UNROLL)
            qb = [q_ref[c, :] * (DIL_DH ** -0.5) for c in curs]
            k_cur, k_prev = [k_ref[c, :] for c in curs], [k_ref[p, :] for p in prvs]
            v_cur, v_prev = [v_ref[c, :] for c in curs], [v_ref[p, :] for p in prvs]
            sc = [_att_scores(qb[u], k_cur[u], k_prev[u], slope_d, has_prev[u]) for u in us]
            lse_b, delta_b, dob = [lse_ref[c, :] for c in curs], [dl_ref[c, :] for c in curs], [do_ref[c, :] for c in curs]
            p_cur = [jnp.exp(sc[u][0] - lse_b[u]) for u in us]
            p_prev = [jnp.exp(sc[u][1] - lse_b[u]) for u in us]
            ds_cur = [p_cur[u] * (_dot_nt(dob[u], v_cur[u]) - delta_b[u]) for u in us]
            ds_prev = [p_prev[u] * (_dot_nt(dob[u], v_prev[u]) - delta_b[u]) for u in us]
            dq = [(_dot(ds_cur[u], k_cur[u]) + _dot(ds_prev[u], k_prev[u])) * (DIL_DH ** -0.5) for u in us]
            dk_c = [_dot_tn(ds_cur[u], qb[u]) for u in us]
            dv_c = [_dot_tn(p_cur[u], dob[u]) for u in us]
            dk_p = [_dot_tn(ds_prev[u], qb[u]) for u in us]
            dv_p = [_dot_tn(p_prev[u], dob[u]) for u in us]
            for u in us:
                dq_acc[curs[u], :] = dq[u]
                dk_acc[curs[u], :] += dk_c[u]
                dv_acc[curs[u], :] += dv_c[u]
            for u in us:
                dk_acc[prvs[u], :] += dk_p[u]
                dv_acc[prvs[u], :] += dv_p[u]
            return carry

        def step_whole(i, carry):
            rows = [pl.ds(i * ATT_UNROLL + u, 2 * ATT_BLOCK, stride=dil) for u in range(ATT_UNROLL)]
            qb = [q_ref[r, :] * (DIL_DH ** -0.5) for r in rows]
            kk, vv, dob = [k_ref[r, :] for r in rows], [v_ref[r, :] for r in rows], [do_ref[r, :] for r in rows]
            sc = [_att_scores_whole(qb[u], kk[u], slope_d) for u in range(ATT_UNROLL)]
            p = [jnp.exp(sc[u] - lse_ref[r, :][:, 0:1]) for u, r in enumerate(rows)]
            ds = [p[u] * (_dot_nt(dob[u], vv[u]) - dl_ref[r, :][:, 0:1]) for u, r in enumerate(rows)]
            dq = [_dot(ds[u], kk[u]) * (DIL_DH ** -0.5) for u in range(ATT_UNROLL)]
            dk = [_dot_tn(ds[u], qb[u]) for u in range(ATT_UNROLL)]
            dv = [_dot_tn(p[u], dob[u]) for u in range(ATT_UNROLL)]
            for u, r in enumerate(rows):
                dq_acc[r, :] = dq[u]
                dk_acc[r, :] = dk[u]
                dv_acc[r, :] = dv[u]
            return carry

        def step_tile(i, carry):
            qrows, krows, has_prev = _att_tiles(i, dil, nb)
            us = range(ATT_UNROLL)
            qb = [q_ref[r, :] * (DIL_DH ** -0.5) for r in qrows]
            kk, vv, dob = [k_ref[r, :] for r in krows], [v_ref[r, :] for r in krows], [do_ref[r, :] for r in qrows]
            sc = [_att_scores_tile(qb[u], kk[u], slope_d, has_prev[u]) for u in us]
            p = [jnp.exp(sc[u] - lse_ref[qrows[u], :][:, 0:1]) for u in us]
            ds = [p[u] * (_dot_nt(dob[u], vv[u]) - dl_ref[qrows[u], :][:, 0:1]) for u in us]
            dq = [_dot(ds[u], kk[u]) * (DIL_DH ** -0.5) for u in us]
            dk = [_dot_tn(ds[u], qb[u]) for u in us]
            dv = [_dot_tn(p[u], dob[u]) for u in us]
            for u in us:
                dq_acc[qrows[u], :] = dq[u]
                dk_acc[krows[u], :] += dk[u]
                dv_acc[krows[u], :] += dv[u]
            return carry

        if nb == 2:
            lax.fori_loop(0, dil // ATT_UNROLL, step_whole, 0)
        elif nb % 2 == 0:
            lax.fori_loop(0, dil * nb // 2 // ATT_UNROLL, step_tile, 0)
        else:
            lax.fori_loop(0, dil * nb // ATT_UNROLL, step, 0)
        dq_ref[...] = dq_acc[...].astype(BF16)
        dk_ref[...] = dk_acc[...].astype(BF16)
        dv_ref[...] = dv_acc[...].astype(BF16)

    def col(off):
        return pl.BlockSpec((s, DIL_DH), lambda h: (0, off // DIL_DH + group * DIL_HEADS + h))

    hd = pl.BlockSpec((s, DIL_DH), lambda h: (0, h))
    return pl.pallas_call(
        body, name=f"att_bwd{group}", grid=(DIL_HEADS,),
        in_specs=[col(OFF_Q_B), col(OFF_K_B), col(OFF_V_B), hd, hd, hd], out_specs=[hd, hd, hd],
        out_shape=[jax.ShapeDtypeStruct((s, DIL_W), BF16)] * 3,
        scratch_shapes=[pltpu.VMEM((s, DIL_DH), F32)] * 3,
        compiler_params=_cparams("parallel"))(proj, proj, proj, do, lse, delta)


def _att_merge(parts, proj):
    s = proj.shape[0]

    def body(n0, d0, m0, n1, d1, m1, n2, d2, m2, z_ref, ob_ref, o_ref, lse_ref, obt_ref):
        m = jnp.maximum(jnp.maximum(m0[...], m1[...]), m2[...])
        num = jnp.zeros_like(m)
        den = jnp.zeros_like(m)
        for nr, dr, mr in ((n0, d0, m0), (n1, d1, m1), (n2, d2, m2)):
            sc = jnp.exp(mr[...] - m)
            num = num + nr[...] * sc
            den = den + dr[...] * sc
        o = num / den
        o_ref[...] = o
        lse_ref[...] = m + jnp.log(den)
        ob = o * _silu(z_ref[...])
        ob_ref[...] = ob.astype(BF16)
        obt_ref[...] = ob.T.astype(BF16)

    row = pl.BlockSpec((ROW_TILE, DIL_W), lambda i: (i, 0))
    flat = [a for p in parts for a in p]
    return pl.pallas_call(
        body, name="att_merge", grid=(s // ROW_TILE,),
        in_specs=[row] * 9 + [pl.BlockSpec((ROW_TILE, DIL_W), lambda i: (i, OFF_Z_B // DIL_W))],
        out_specs=[row, row, row, pl.BlockSpec((DIL_W, ROW_TILE), lambda i: (0, i))],
        out_shape=[jax.ShapeDtypeStruct((s, DIL_W), BF16), jax.ShapeDtypeStruct((s, DIL_W), F32),
                   jax.ShapeDtypeStruct((s, DIL_W), F32), jax.ShapeDtypeStruct((DIL_W, s), BF16)],
        compiler_params=_cparams("parallel"))(*flat, proj)


def _att_merge_bwd(o, proj, dob):
    s = o.shape[0]

    def body(o_ref, z_ref, d_ref, do_ref, dl_ref, dz_ref):
        ov, zv, dv = o_ref[...], z_ref[...], d_ref[...]
        do = dv * _silu(zv)
        do_ref[...] = do
        dz_ref[...] = (dv * ov * _silu_grad(zv)).astype(BF16)
        for h in range(DIL_HEADS):
            sl = slice(h * DIL_DH, (h + 1) * DIL_DH)
            dl_ref[:, sl] = jnp.broadcast_to(jnp.sum(do[:, sl] * ov[:, sl], axis=-1, keepdims=True), (ROW_TILE, DIL_DH))

    row = pl.BlockSpec((ROW_TILE, DIL_W), lambda i: (i, 0))
    return pl.pallas_call(
        body, name="att_merge_bwd", grid=(s // ROW_TILE,),
        in_specs=[row, pl.BlockSpec((ROW_TILE, DIL_W), lambda i: (i, OFF_Z_B // DIL_W)), row],
        out_specs=[row, row, row],
        out_shape=[jax.ShapeDtypeStruct((s, DIL_W), F32), jax.ShapeDtypeStruct((s, DIL_W), F32),
                   jax.ShapeDtypeStruct((s, DIL_W), BF16)],
        compiler_params=_cparams("parallel"))(o, proj, dob)


def _merge(proj, ya, yb):
    s = proj.shape[0]

    def body(ga_ref, gb_ref, ya_ref, yb_ref, o_ref, ot_ref):
        m = _sigmoid(ga_ref[...]) * ya_ref[...] + _sigmoid(gb_ref[...]) * yb_ref[...]
        o_ref[...] = m.astype(BF16)
        ot_ref[...] = m.T.astype(BF16)

    row = pl.BlockSpec((ROW_TILE, D_MODEL), lambda i: (i, 0))
    return pl.pallas_call(
        body, name="merge", grid=(s // ROW_TILE,),
        in_specs=[pl.BlockSpec((ROW_TILE, D_MODEL), lambda i: (i, OFF_G_A // D_MODEL)),
                  pl.BlockSpec((ROW_TILE, D_MODEL), lambda i: (i, OFF_G_B // D_MODEL)), row, row],
        out_specs=[row, pl.BlockSpec((D_MODEL, ROW_TILE), lambda i: (0, i))],
        out_shape=[jax.ShapeDtypeStruct((s, D_MODEL), BF16), jax.ShapeDtypeStruct((D_MODEL, s), BF16)],
        compiler_params=_cparams("parallel"))(proj, proj, ya, yb)


def _merge_bwd(proj, ya, yb, dm):
    s = proj.shape[0]

    def body(ga_ref, gb_ref, ya_ref, yb_ref, dm_ref, dya_ref, dyb_ref, dga_ref, dgb_ref):
        dmv = dm_ref[...]
        sa, sb = _sigmoid(ga_ref[...]), _sigmoid(gb_ref[...])
        dya_ref[...] = (dmv * sa).astype(BF16)
        dyb_ref[...] = (dmv * sb).astype(BF16)
        dga_ref[...] = (dmv * ya_ref[...] * sa * (1.0 - sa)).astype(BF16)
        dgb_ref[...] = (dmv * yb_ref[...] * sb * (1.0 - sb)).astype(BF16)

    row = pl.BlockSpec((ROW_TILE, D_MODEL), lambda i: (i, 0))
    return pl.pallas_call(
        body, name="merge_bwd", grid=(s // ROW_TILE,),
        in_specs=[pl.BlockSpec((ROW_TILE, D_MODEL), lambda i: (i, OFF_G_A // D_MODEL)),
                  pl.BlockSpec((ROW_TILE, D_MODEL), lambda i: (i, OFF_G_B // D_MODEL)), row, row, row],
        out_specs=[row] * 4, out_shape=[jax.ShapeDtypeStruct((s, D_MODEL), BF16)] * 4,
        compiler_params=_cparams("parallel"))(proj, proj, ya, yb, dm)


def _final(x, t, fw, tgt):
    s, d = x.shape

    def body(x_ref, t_ref, w_ref, y_ref, dx_ref, dw_ref, l_ref):
        i = pl.program_id(0)
        x2 = x_ref[...] + t_ref[...]
        wv = w_ref[...]
        r = lax.rsqrt(jnp.mean(x2 * x2, axis=-1, keepdims=True) + NORM_EPS)
        e = x2 * r * wv - y_ref[...]
        lrow = jnp.mean(e * e, axis=-1, keepdims=True)
        lpart = jnp.broadcast_to(0.5 * jnp.sum(lrow, axis=0, keepdims=True), (1, 128))
        dy = e * (1.0 / d)
        dwp = jnp.sum(dy * x2 * r, axis=0, keepdims=True)
        dyw = dy * wv
        dx_ref[...] = r * dyw - x2 * (r * r * r) * jnp.mean(dyw * x2, axis=-1, keepdims=True)

        @pl.when(i == 0)
        def _():
            dw_ref[...] = dwp
            l_ref[...] = lpart

        @pl.when(i > 0)
        def _():
            dw_ref[...] += dwp
            l_ref[...] += lpart

    row = pl.BlockSpec((ROW_TILE, d), lambda i: (i, 0))
    vec = pl.BlockSpec((1, d), lambda i: (0, 0))
    return pl.pallas_call(
        body, name="final", grid=(s // ROW_TILE,), in_specs=[row, row, vec, row],
        out_specs=[row, vec, pl.BlockSpec((1, 128), lambda i: (0, 0))],
        out_shape=[jax.ShapeDtypeStruct((s, d), F32), jax.ShapeDtypeStruct((1, d), F32), jax.ShapeDtypeStruct((1, 128), F32)],
        compiler_params=_cparams("arbitrary"))(x, t, fw, tgt)


def _adamw(w, g, m, v, name):
    r, c = w.shape
    cap = max(8, (1 << 18) // c)
    divisors = [t for t in range(8, min(r, cap) + 1, 8) if r % t == 0]
    tr = r if r <= 8 else (max(divisors) if divisors else cap)

    def body(w_ref, g_ref, m_ref, v_ref, d_ref, nm_ref, nv_ref):
        gv = g_ref[...]
        mn = ADAM_B1 * m_ref[...] + (1.0 - ADAM_B1) * gv
        vn = ADAM_B2 * v_ref[...] + (1.0 - ADAM_B2) * (gv * gv)
        m_hat = mn / (1.0 - ADAM_B1 ** ADAM_STEP)
        v_hat = vn / (1.0 - ADAM_B2 ** ADAM_STEP)
        d_ref[...] = -ADAM_LR * (m_hat / (jnp.sqrt(v_hat) + ADAM_EPS) + ADAM_WD * w_ref[...])
        nm_ref[...] = mn
        nv_ref[...] = vn

    blk = pl.BlockSpec((tr, c), lambda i: (i, 0))
    return pl.pallas_call(
        body, name=name, grid=(pl.cdiv(r, tr),), in_specs=[blk] * 4, out_specs=[blk] * 3,
        out_shape=[jax.ShapeDtypeStruct((r, c), F32)] * 3, compiler_params=_cparams("parallel"))(w, g, m, v)


HBM_SPEC = pl.BlockSpec(memory_space=pl.ANY)


def _place():
    x, y, c = lax.axis_index("x"), lax.axis_index("y"), lax.axis_index("c")
    chips = [(1 - x, y), (x, 1 - y), (1 - x, 1 - y)]
    return x, y, c, chips


def _ag_weights(packs):
    na = len(packs)
    nsem = 7

    def body(*refs):
        p_refs, out_refs = refs[:na], refs[na:2 * na]
        send_sems, recv_sems = refs[2 * na:]
        x, y, c, _ = _place()
        me, sib, j = (x, y, c), (x, y, 1 - c), 2 * x + y
        xn, yn = (1 - x, y, c), (x, 1 - y, c)
        jx, jy, jd = 2 * (1 - x) + y, 2 * x + (1 - y), 2 * (1 - x) + (1 - y)

        def rc(a, k, src, dst, to):
            return pltpu.make_async_remote_copy(src_ref=src, dst_ref=dst, send_sem=send_sems.at[nsem * a + k],
                                                recv_sem=recv_sems.at[nsem * a + k], device_id=to, device_id_type=MESH)

        sent = []
        for a in range(na):
            mine, land = p_refs[a].at[c], out_refs[a].at[j, c]
            sent += [rc(a, 0, mine, land, xn), rc(a, 1, mine, land, yn)]
        for cp in sent:
            cp.start()
        for a in range(na):
            half = p_refs[a].shape[1] // 2
            top, bottom = pl.ds(0, half), pl.ds(half, half)
            from_x, from_y, from_d = out_refs[a].at[jx, c], out_refs[a].at[jy, c], out_refs[a].at[jd, c]
            rc(a, 0, p_refs[a].at[c], from_x, me).wait_recv()
            later = [rc(a, 2, from_x.at[top], from_x.at[top], yn), rc(a, 4, from_x, from_x, sib)]
            for cp in later:
                cp.start()
            sent += later
            rc(a, 1, p_refs[a].at[c], from_y, me).wait_recv()
            later = [rc(a, 3, from_y.at[bottom], from_y.at[bottom], xn), rc(a, 5, from_y, from_y, sib)]
            for cp in later:
                cp.start()
            sent += later
            rc(a, 2, from_d.at[top], from_d.at[top], me).wait_recv()
            rc(a, 3, from_d.at[bottom], from_d.at[bottom], me).wait_recv()
            cp = rc(a, 6, from_d, from_d, sib)
            cp.start()
            sent.append(cp)
        for a in range(na):
            for k, jj in ((4, jx), (5, jy), (6, jd)):
                rc(a, k, p_refs[a].at[c], out_refs[a].at[jj, 1 - c], me).wait_recv()
        for cp in sent:
            cp.wait_send()

    return pl.pallas_call(
        body, name="ag_weights",
        out_shape=[jax.ShapeDtypeStruct((N_CHIPS,) + p.shape, p.dtype) for p in packs],
        in_specs=[HBM_SPEC] * na, out_specs=[HBM_SPEC] * na,
        scratch_shapes=[pltpu.SemaphoreType.DMA((nsem * na,)), pltpu.SemaphoreType.DMA((nsem * na,))])(*packs)


def _rs_pair(dwpt, gpack):
    n = N_CHIPS
    hw = SHARD_PAD // 2

    def body(d_ref, g_ref, out_d, out_g, send_sems, recv_sems):
        x, y, c, _ = _place()
        sib = (x, y, 1 - c)
        cps = []
        for p in range(n):
            start = pl.multiple_of(WIN_BASE[p] + (1 - c) * hw, TILE_ROWS)
            cps.append(pltpu.make_async_remote_copy(
                src_ref=d_ref.at[pl.ds(start, hw)], dst_ref=out_d.at[p], send_sem=send_sems.at[p],
                recv_sem=recv_sems.at[p], device_id=sib, device_id_type=MESH))
            cps.append(pltpu.make_async_remote_copy(
                src_ref=g_ref.at[p, 1 - c], dst_ref=out_g.at[p], send_sem=send_sems.at[n + p],
                recv_sem=recv_sems.at[n + p], device_id=sib, device_id_type=MESH))
        for cp in cps:
            cp.start()
        for cp in cps:
            cp.wait_recv()
        for cp in cps:
            cp.wait_send()

    return pl.pallas_call(
        body, name="rs_pair",
        out_shape=[jax.ShapeDtypeStruct((n, hw, dwpt.shape[1]), dwpt.dtype),
                   jax.ShapeDtypeStruct((n,) + gpack.shape[2:], gpack.dtype)],
        in_specs=[HBM_SPEC] * 2, out_specs=[HBM_SPEC] * 2,
        scratch_shapes=[pltpu.SemaphoreType.DMA((2 * n,)), pltpu.SemaphoreType.DMA((2 * n,))])(dwpt, gpack)


def _add_halves_win(dwpt, other, c):
    n, rh, wd = other.shape
    tr = _row_tile(rh)

    def body(s_ref, d_ref, o_ref, out_ref):
        out_ref[0] = (d_ref[...] + o_ref[0]).astype(BF16)

    scal = jnp.concatenate([jnp.reshape(c, (1,)).astype(jnp.int32), jnp.asarray(WIN_BASE, jnp.int32)])
    grid_spec = pltpu.PrefetchScalarGridSpec(
        num_scalar_prefetch=1, grid=(n, rh // tr),
        in_specs=[pl.BlockSpec((pl.Element(tr), pl.Element(wd)),
                               lambda p, i, sr: (pl.multiple_of(sr[1 + p] + sr[0] * rh + i * tr, TILE_ROWS), 0)),
                  pl.BlockSpec((1, tr, wd), lambda p, i, sr: (p, i, 0))],
        out_specs=pl.BlockSpec((1, tr, wd), lambda p, i, sr: (p, i, 0)))
    return pl.pallas_call(
        body, name="add_halves_in", grid_spec=grid_spec, out_shape=jax.ShapeDtypeStruct((n, rh, wd), BF16),
        compiler_params=_cparams("parallel", "parallel"))(scal, dwpt, other)


SEM_SPEC = pl.BlockSpec(memory_space=pltpu.SEMAPHORE)
DATAFLOW_EFFECT = pltpu.SideEffectType.DATAFLOW_SIDE_EFFECTING


def _rs_chips_start(csums):
    na = len(csums)

    def body(*refs):
        s_refs, land_refs = refs[:na], refs[na:2 * na]
        send_sems, recv_sems = refs[2 * na], refs[2 * na + 1]
        token = refs[-1]
        x, y, c, chips = _place()
        j = 2 * x + y
        for a in range(na):
            for k, (cx, cy) in enumerate(chips):
                pltpu.make_async_remote_copy(src_ref=s_refs[a].at[2 * cx + cy], dst_ref=land_refs[a].at[j],
                                             send_sem=send_sems.at[3 * a + k], recv_sem=recv_sems.at[3 * a + k],
                                             device_id=(cx, cy, c), device_id_type=MESH).start()
        token[...] = jnp.zeros_like(token)

    hbm = [pltpu.HBM(s.shape, s.dtype) for s in csums]
    args = [pltpu.with_memory_space_constraint(s, pltpu.HBM) for s in csums]
    args += [pltpu.with_memory_space_constraint(lax.empty(s.shape, s.dtype), pltpu.HBM) for s in csums]
    res = pl.pallas_call(
        body, name="rs_chips_start",
        out_shape=(pltpu.SemaphoreType.DMA((3 * na,)), pltpu.SemaphoreType.DMA((3 * na,)), *hbm, *hbm,
                   jax.ShapeDtypeStruct((8, 128), F32)),
        in_specs=[pl.BlockSpec(memory_space=pltpu.HBM)] * (2 * na),
        out_specs=(SEM_SPEC, SEM_SPEC, *[pl.BlockSpec(memory_space=pltpu.HBM)] * (2 * na),
                   pl.BlockSpec(memory_space=pltpu.VMEM)),
        input_output_aliases={i: 2 + i for i in range(2 * na)},
        compiler_params=pltpu.CompilerParams(has_side_effects=DATAFLOW_EFFECT))(*args)
    return res[0], res[1], list(res[2:2 + na]), list(res[2 + na:2 + 2 * na]), res[-1]


def _rs_chips_wait(send_sems, recv_sems, csums, lands, after):
    na = len(csums)

    def body(*refs):
        s_refs, land_refs = refs[:na], refs[na:2 * na]
        send_sems, recv_sems = refs[2 * na], refs[2 * na + 1]
        x, y, c, chips = _place()
        j = 2 * x + y
        for a in range(na):
            for k, (cx, cy) in enumerate(chips):
                cp = pltpu.make_async_remote_copy(src_ref=s_refs[a].at[2 * cx + cy], dst_ref=land_refs[a].at[2 * cx + cy],
                                                  send_sem=send_sems.at[3 * a + k], recv_sem=recv_sems.at[3 * a + k],
                                                  device_id=(cx, cy, c), device_id_type=MESH)
                cp.wait_send()
                cp.wait_recv()

    hbm = [pltpu.HBM(s.shape, s.dtype) for s in csums]
    res = pl.pallas_call(
        body, name="rs_chips_wait", out_shape=(*hbm, *hbm),
        in_specs=[pl.BlockSpec(memory_space=pltpu.HBM)] * (2 * na) + [SEM_SPEC, SEM_SPEC, pl.BlockSpec(memory_space=pl.ANY)],
        out_specs=tuple([pl.BlockSpec(memory_space=pltpu.HBM)] * (2 * na)),
        input_output_aliases={i: i for i in range(2 * na)},
        compiler_params=pltpu.CompilerParams(has_side_effects=DATAFLOW_EFFECT))(*csums, *lands, send_sems, recv_sems, after)
    return list(res[:na]), list(res[na:])


SWAP_CHUNKS = 4


def _pair_swap(halves):
    na = len(halves)

    def body(*refs):
        h_refs, out_refs = refs[:na], refs[na:2 * na]
        send_sems, recv_sems = refs[2 * na:]
        x, y, c, _ = _place()
        cps = []
        for a in range(na):
            rows = h_refs[a].shape[0] // SWAP_CHUNKS
            assert rows * SWAP_CHUNKS == h_refs[a].shape[0]
            for q in range(SWAP_CHUNKS):
                k = SWAP_CHUNKS * a + q
                cps.append(pltpu.make_async_remote_copy(
                    src_ref=h_refs[a].at[pl.ds(q * rows, rows)], dst_ref=out_refs[a].at[pl.ds(q * rows, rows)],
                    send_sem=send_sems.at[k], recv_sem=recv_sems.at[k], device_id=(x, y, 1 - c), device_id_type=MESH))
        for cp in cps:
            cp.start()
        for cp in cps:
            cp.wait_recv()
        for cp in cps:
            cp.wait_send()

    return pl.pallas_call(
        body, name="pair_swap", out_shape=[jax.ShapeDtypeStruct(h.shape, h.dtype) for h in halves],
        in_specs=[HBM_SPEC] * na, out_specs=[HBM_SPEC] * na,
        scratch_shapes=[pltpu.SemaphoreType.DMA((SWAP_CHUNKS * na,)), pltpu.SemaphoreType.DMA((SWAP_CHUNKS * na,))])(*halves)


def _ag_small(v):
    m_per, n = v.shape

    def body(x_ref, out_ref, send_sems, recv_sems, local_sem):
        x, y, c, chips = _place()
        me, sibling = (x, y, c), (x, y, 1 - c)

        def rows(px, py, pc):
            return out_ref.at[pl.ds((4 * px + 2 * py + pc) * m_per, m_per), :]

        def copy(k, block, to, src=None):
            return pltpu.make_async_remote_copy(
                src_ref=rows(*block) if src is None else src, dst_ref=rows(*block), send_sem=send_sems.at[k],
                recv_sem=recv_sems.at[k], device_id=to, device_id_type=MESH)

        mine = pltpu.make_async_copy(x_ref, rows(*me), local_sem)
        mine.start()
        first = [copy(0, me, sibling, src=x_ref)]
        first += [copy(1 + k, me, (*chip, c), src=x_ref) for k, chip in enumerate(chips)]
        for cp in first:
            cp.start()
        passed = [copy(4 + k, (*chip, c), sibling) for k, chip in enumerate(chips)]
        for k, chip in enumerate(chips):
            copy(1 + k, (*chip, c), me).wait_recv()
            passed[k].start()
        copy(0, sibling, me).wait_recv()
        for k, chip in enumerate(chips):
            copy(4 + k, (*chip, 1 - c), me).wait_recv()
        for cp in first + passed:
            cp.wait_send()
        mine.wait()

    return pl.pallas_call(
        body, name="ag_small", out_shape=jax.ShapeDtypeStruct((8 * m_per, n), v.dtype),
        in_specs=[pl.BlockSpec(memory_space=pltpu.VMEM)], out_specs=pl.BlockSpec(memory_space=pltpu.VMEM),
        scratch_shapes=[pltpu.SemaphoreType.DMA((7,)), pltpu.SemaphoreType.DMA((7,)), pltpu.SemaphoreType.DMA])(v)


def _sum_blocks(a, nblk, name):
    rows, wd = a.shape
    r = rows // nblk
    tr = min(r, ROW_TILE)
    assert r % tr == 0

    def body(*refs):
        acc = refs[0][...].astype(F32)
        for ref in refs[1:nblk]:
            acc = acc + ref[...].astype(F32)
        refs[nblk][...] = acc

    nt = r // tr
    return pl.pallas_call(
        body, name=name, grid=(nt,),
        in_specs=[pl.BlockSpec((tr, wd), functools.partial(lambda i, b: (b * nt + i, 0), b=b)) for b in range(nblk)],
        out_specs=pl.BlockSpec((tr, wd), lambda i: (i, 0)),
        out_shape=jax.ShapeDtypeStruct((r, wd), F32), compiler_params=_cparams("parallel"))(*([a] * nblk))


def _row_tile(rows):
    best = max(t for t in range(16, 513, 16) if rows % t == 0)
    return best


def _sum_chips(by_src, csum, j, name):
    n, rh, wd = by_src.shape
    tr = _row_tile(rh)

    def body(j_ref, *refs):
        own = refs[n][0].astype(F32)
        acc = None
        for k in range(n):
            term = jnp.where(j_ref[0] == k, own, refs[k][0].astype(F32))
            acc = term if acc is None else acc + term
        refs[n + 1][...] = acc

    def other(k):
        return pl.BlockSpec((1, tr, wd), lambda i, jr: (jnp.where(jr[0] == k, (k + 1) % n, k), i, 0))

    grid_spec = pltpu.PrefetchScalarGridSpec(
        num_scalar_prefetch=1, grid=(rh // tr,),
        in_specs=[other(k) for k in range(n)] + [pl.BlockSpec((1, tr, wd), lambda i, jr: (jr[0], i, 0))],
        out_specs=pl.BlockSpec((tr, wd), lambda i, jr: (i, 0)))
    return pl.pallas_call(
        body, name=name, grid_spec=grid_spec, out_shape=jax.ShapeDtypeStruct((rh, wd), F32),
        compiler_params=_cparams("parallel"))(jnp.reshape(j, (1,)).astype(jnp.int32), *([by_src] * n), csum)


def _add_halves(gpack, other, c, name):
    n, _, rh, wd = gpack.shape
    tr = _row_tile(rh)

    def body(c_ref, g_ref, o_ref, out_ref):
        out_ref[0] = (g_ref[0, 0] + o_ref[0]).astype(BF16)

    grid_spec = pltpu.PrefetchScalarGridSpec(
        num_scalar_prefetch=1, grid=(n, rh // tr),
        in_specs=[pl.BlockSpec((1, 1, tr, wd), lambda p, i, cr: (p, cr[0], i, 0)),
                  pl.BlockSpec((1, tr, wd), lambda p, i, cr: (p, i, 0))],
        out_specs=pl.BlockSpec((1, tr, wd), lambda p, i, cr: (p, i, 0)))
    return pl.pallas_call(
        body, name=name, grid_spec=grid_spec, out_shape=jax.ShapeDtypeStruct((n, rh, wd), BF16),
        compiler_params=_cparams("parallel", "parallel"))(jnp.reshape(c, (1,)).astype(jnp.int32), gpack, other)


PACK_W = 1024
ROWS_O_DN = DN_W // N_CHIPS
ROWS_O_DIL = DIL_W * (D_MODEL // N_CHIPS) // PACK_W
ROWS_OUT = D_MODEL // N_CHIPS
ROWS_CONV = 4 * (3 * DN_W // N_CHIPS) // PACK_W
R1 = ROWS_O_DN
R2 = R1 + ROWS_O_DIL
R3 = R2 + ROWS_OUT
R4 = R3 + 16
R5 = R4 + 16
PACK_ROWS = 704
HALF_ROWS = PACK_ROWS // 2
SHARD_PAD = 2880


R6 = R5 + 2 * DN_HEADS

TILE_ROWS = 16
BA_IN_SHARD1 = REF_OFF_BA - SHARD_W
LOCAL_START = (0, SHARD_W, 2 * SHARD_W - 2 * DN_HEADS, 3 * SHARD_W - 2 * DN_HEADS)
LOCAL_END = LOCAL_START[1:] + (OFF_BA,)
WIN_BASE = tuple(s // TILE_ROWS * TILE_ROWS for s in LOCAL_START)


def _to_window(k, shard):
    nba = 2 * DN_HEADS
    body = shard
    if k == 1:
        row = lax.broadcasted_iota(jnp.int32, (SHARD_W - nba, 1), 0)
        body = jnp.where(row < BA_IN_SHARD1, shard[:SHARD_W - nba], shard[nba:])
    lead = LOCAL_START[k] - WIN_BASE[k]
    return jnp.pad(body, ((lead, SHARD_PAD - lead - body.shape[0]), (0, 0)))


def _from_window(k, win, ba):
    nba = 2 * DN_HEADS
    lead = LOCAL_START[k] - WIN_BASE[k]
    if k != 1:
        return win[lead:lead + SHARD_W]
    row = lax.broadcasted_iota(jnp.int32, (SHARD_W, 1), 0)
    before = win[lead:lead + SHARD_W]
    after = jnp.pad(win, ((nba, 0), (0, 0)))[lead:lead + SHARD_W]
    mid = jnp.pad(ba, ((BA_IN_SHARD1, SHARD_W - BA_IN_SHARD1 - nba), (0, 0)))
    return jnp.where(row < BA_IN_SHARD1, before, jnp.where(row < BA_IN_SHARD1 + nba, mid, after))


def _stack_windows(wins, ba):
    pieces = []
    for k in range(N_CHIPS):
        lo = WIN_BASE[k] + (TILE_ROWS if k else 0)
        hi = LOCAL_END[k] // TILE_ROWS * TILE_ROWS
        pieces.append(wins[k][lo - WIN_BASE[k]:hi - WIN_BASE[k]])
        if k + 1 < N_CHIPS:
            assert hi == WIN_BASE[k + 1]
            pieces.append(wins[k][hi - WIN_BASE[k]:hi - WIN_BASE[k] + TILE_ROWS] + wins[k + 1][:TILE_ROWS])
    pieces += [ba, jnp.zeros((PW - OFF_BA - ba.shape[0], ba.shape[1]), ba.dtype)]
    out = jnp.concatenate(pieces, axis=0)
    assert out.shape[0] == PW
    return out


def _to_ref_layout(wpt):
    return jnp.concatenate([wpt[:REF_OFF_BA], wpt[OFF_BA:OFF_BA + 2 * DN_HEADS], wpt[REF_OFF_BA:OFF_BA]], axis=0)


def _from_ref_layout(wt):
    pad = jnp.zeros((PW - PROJ_W, wt.shape[1]), wt.dtype)
    return jnp.concatenate([wt[:REF_OFF_BA], wt[REF_OFF_BA + 2 * DN_HEADS:], wt[REF_OFF_BA:REF_OFF_BA + 2 * DN_HEADS], pad],
                           axis=0)


def _local_step(x, tgt, norm_w, wpt, conv_full, a_log, dt_bias, dn_norm_w, w_o_dn, w_o_dil, w_out, final_norm_w):
    s = x.shape[0]
    h, h_t = _rms_in(x, norm_w)
    proj = _matmul(h, wpt, F32, 2048, 1280, 1024, "proj", nt=True)
    c_pre, qkv = _conv_fwd(proj, conv_full)
    gate_par = jnp.zeros((8, 128), F32).at[0, 8:16].set(a_log[0]).at[1, 8:16].set(dt_bias[0])
    bg = _gates_fwd(proj, gate_par)
    o_a, u, w, vn, tmat, states = _gdr_fwd(qkv, bg)
    oa2, oa2_t = _gdr_out(o_a, proj, dn_norm_w)
    ya = _matmul(oa2, w_o_dn, F32, 512, 1024, 1024, "ya")
    parts = [_att_fwd(proj, g) for g in range(N_DIL)]
    ob, o_att, lse, ob_t = _att_merge(parts, proj)
    yb = _matmul(ob, w_o_dil, F32, 512, 1024, 512, "yb")
    mg, mg_t = _merge(proj, ya, yb)
    t = _matmul(mg, w_out, F32, 512, 1024, 1024, "t_out")
    dx2, dfw, lpart = _final(x, t, final_norm_w, tgt)

    dmg = _matmul(dx2, w_out, F32, 512, 1024, 1024, "d_merged", nt=True)
    dw_out = _matmul(mg_t, dx2, F32, 1024, 1024, 1024, "dw_out")
    dya, dyb, dga, dgb = _merge_bwd(proj, ya, yb, dmg)
    doa2 = _matmul(dya, w_o_dn, F32, 512, 1024, 1024, "d_oa2", nt=True)
    dw_o_dn = _matmul(oa2_t, dya, F32, 1024, 1024, 1024, "dw_o_dn")
    dob = _matmul(dyb, w_o_dil, F32, 512, 512, 1024, "d_ob", nt=True)
    dw_o_dil = _matmul(ob_t, dyb, F32, 512, 1024, 1024, "dw_o_dil")
    do_a, dz_a, ddnw = _gdr_out_bwd(o_a, proj, dn_norm_w, doa2)
    dq_a, dk_a, dv_a, dbg = _gdr_bwd(qkv, bg, u, w, vn, tmat, states, do_a)
    dba, dpar = _gates_bwd(proj, gate_par, dbg)
    dc = _conv_bwd_act(c_pre, dq_a, dk_a, dv_a)
    du_a, dconv = _conv_bwd(proj, dc, conv_full)
    do_att, delta, dz_b = _att_merge_bwd(o_att, proj, dob)
    dqkv_b = [_att_bwd(proj, g, do_att, lse, delta) for g in range(N_DIL)]
    dproj = jnp.concatenate(
        [du_a, dz_a] + [dqkv_b[g][i] for i in range(3) for g in range(N_DIL)]
        + [dz_b, dga, dgb, dba], axis=1)
    dwpt, dwpt_b = _matmul(h_t, dproj, F32, 1024, 1280, 2048, "dw_in", transpose_out=True, also_bf16=True)

    def finish(after=None):
        dh = _matmul(dproj, wpt, F32, 1024, 1024, 3840, "d_h", after=after)
        grad_x, dnw = _rms_in_bwd(x, norm_w, dh, dx2)
        small = jnp.zeros((8, PACK_W), F32)
        small = small.at[0].set(dnw[0]).at[1].set(dfw[0]).at[2, :DN_D].set(ddnw[0])
        small = small.at[3, :DN_HEADS].set(dpar[0, 8:16]).at[3, DN_HEADS:2 * DN_HEADS].set(dpar[1, 8:16])
        small = small.at[4, 0].set(lpart[0, 0])
        return grad_x, small

    return finish, (dwpt, dwpt_b), dconv, dw_o_dn, dw_o_dil, dw_out


def kernel(x, norm_w, w_in, conv_w, a_log, dt_bias, dn_norm_w, w_o_dn, w_o_dil, w_out, final_norm_w, loss_target, m_norm_w, m_w_in, m_conv_w, m_a_log, m_dt_bias, m_dn_norm_w, m_w_o_dn, m_w_o_dil, m_w_out, m_final_norm_w, v_norm_w, v_w_in, v_conv_w, v_a_log, v_dt_bias, v_dn_norm_w, v_w_o_dn, v_w_o_dil, v_w_out, v_final_norm_w):
    c = lax.axis_index("c")
    j = 2 * lax.axis_index("x") + lax.axis_index("y")
    qw = D_MODEL // N_CHIPS

    cw = conv_w[0].reshape(ROWS_CONV, PACK_W)
    cw = jnp.pad(cw, ((0, 16 - ROWS_CONV), (0, 0)))
    cw_hi = cw.astype(BF16)
    cw_lo = (cw - cw_hi.astype(F32)).astype(BF16)
    shard = w_in[0].T.astype(BF16)
    own_ba = jnp.where(j == 1, shard[BA_IN_SHARD1:BA_IN_SHARD1 + 2 * DN_HEADS], jnp.zeros((2 * DN_HEADS, D_MODEL), BF16))
    pack = jnp.concatenate(
        [w_o_dn[0].astype(BF16), w_o_dil[0].astype(BF16).reshape(ROWS_O_DIL, PACK_W), w_out[0].astype(BF16), cw_hi, cw_lo,
         own_ba, jnp.zeros((PACK_ROWS - R6, PACK_W), BF16)], axis=0).reshape(2, HALF_ROWS, PACK_W)
    chips = range(N_CHIPS)
    own_win = lax.switch(j, [functools.partial(_to_window, k) for k in chips], shard).reshape(2, SHARD_PAD // 2, D_MODEL)
    all_in, allw = _ag_weights([own_win, pack])
    wins = [jnp.where(j == k, own_win, all_in[k]).reshape(SHARD_PAD, D_MODEL) for k in chips]
    allw = [jnp.where(j == k, pack, allw[k]).reshape(PACK_ROWS, PACK_W) for k in chips]
    wpt = _stack_windows(wins, allw[1][R5:R6])
    w_o_dn_full = jnp.concatenate([allw[k][:R1] for k in chips], axis=0)
    w_o_dil_full = jnp.concatenate([allw[k][R1:R2].reshape(DIL_W, qw) for k in chips], axis=1)
    w_out_full = jnp.concatenate([allw[k][R2:R3] for k in chips], axis=0)
    conv_full = jnp.concatenate(
        [(allw[k][R3:R3 + ROWS_CONV].astype(F32) + allw[k][R4:R4 + ROWS_CONV].astype(F32)).reshape(4, 3 * DN_W // N_CHIPS)
         for k in chips], axis=1)

    finish, (dwpt, dwpt_b), dconv, dw_o_dn, dw_o_dil, dw_out = _local_step(
        x[0], loss_target[0], norm_w, wpt, conv_full, a_log, dt_bias, dn_norm_w, w_o_dn_full, w_o_dil_full, w_out_full,
        final_norm_w.reshape(1, D_MODEL))

    cq = 3 * DN_W // N_CHIPS
    gpack = jnp.stack([
        jnp.concatenate(
            [dw_o_dn[k * qw:(k + 1) * qw], dw_o_dil[:, k * qw:(k + 1) * qw].reshape(ROWS_O_DIL, PACK_W),
             dw_out[k * qw:(k + 1) * qw],
             jnp.pad(dconv[:, k * cq:(k + 1) * cq].reshape(ROWS_CONV, PACK_W), ((0, 16 - ROWS_CONV), (0, 0))),
             dwpt[OFF_BA:OFF_BA + 2 * DN_HEADS] if k == 1 else jnp.zeros((2 * DN_HEADS, PACK_W), F32),
             jnp.zeros((PACK_ROWS - R4 - 2 * DN_HEADS, PACK_W), F32)], axis=0)
        for k in chips]).reshape(N_CHIPS, 2, HALF_ROWS, PACK_W)
    sib_in, sib_pack = _rs_pair(dwpt_b, gpack)
    csum_in = _add_halves_win(dwpt, sib_in, c)
    csum_pack = _add_halves(gpack, sib_pack, c, "add_halves_pack")
    send_sems, recv_sems, csums, lands, token = _rs_chips_start([csum_in, csum_pack])
    grad_x, small = finish(after=token)
    (csum_in, csum_pack), (src_in, src_pack) = _rs_chips_wait(send_sems, recv_sems, csums, lands, grad_x)
    half_in = _sum_chips(src_in, csum_in, j, "sum_chips_in")
    half_pack = _sum_chips(src_pack, csum_pack, j, "sum_chips_pack")
    sib_half_in, sib_half_pack = _pair_swap([half_in, half_pack])

    def both_halves(mine, theirs):
        return jnp.where(c == 0, jnp.concatenate([mine, theirs], axis=0), jnp.concatenate([theirs, mine], axis=0))

    g = both_halves(half_pack, sib_half_pack)
    g_w_in = lax.switch(j, [functools.partial(_from_window, k) for k in chips], both_halves(half_in, sib_half_in),
                        g[R4:R4 + 2 * DN_HEADS])
    g_w_o_dn = g[:R1]
    g_w_o_dil = g[R1:R2].reshape(DIL_W, qw)
    g_w_out = g[R2:R3]
    g_conv = g[R3:R3 + ROWS_CONV].reshape(4, cq)

    gs = _sum_blocks(_ag_small(small), 8, "sum_small")
    loss = gs[4, 0]
    w_small = jnp.zeros((8, PACK_W), F32)

    def pack_small(nw, fw, dnw_, al, db):
        t = w_small.at[0].set(nw[0]).at[1].set(fw).at[2, :DN_D].set(dnw_[0])
        return t.at[3, :DN_HEADS].set(al[0]).at[3, DN_HEADS:2 * DN_HEADS].set(db[0])

    sm = _adamw(pack_small(norm_w, final_norm_w, dn_norm_w, a_log, dt_bias), gs,
                pack_small(m_norm_w, m_final_norm_w, m_dn_norm_w, m_a_log, m_dt_bias),
                pack_small(v_norm_w, v_final_norm_w, v_dn_norm_w, v_a_log, v_dt_bias), "adamw_small")

    def unpack_small(t):
        return dict(norm_w=t[0:1], final_norm_w=t[1], dn_norm_w=t[2:3, :DN_D], a_log=t[3:4, :DN_HEADS],
                    dt_bias=t[3:4, DN_HEADS:2 * DN_HEADS])

    res = {"grad": unpack_small(gs)}
    for kind, arr in zip(("delta", "new_m", "new_v"), sm):
        res[kind] = unpack_small(arr)
    big = dict(conv_w=(conv_w, g_conv, m_conv_w, v_conv_w), w_o_dn=(w_o_dn, g_w_o_dn, m_w_o_dn, v_w_o_dn),
               w_o_dil=(w_o_dil, g_w_o_dil, m_w_o_dil, v_w_o_dil), w_out=(w_out, g_w_out, m_w_out, v_w_out))
    for name, (wt, gt, mt, vt) in big.items():
        d, nm, nv = _adamw(wt[0], gt, mt[0], vt[0], "adamw_" + name)
        res["grad"][name] = gt[None]
        res["delta"][name], res["new_m"][name], res["new_v"][name] = d[None], nm[None], nv[None]

    d, nm, nv = _adamw(w_in[0].T, g_w_in, m_w_in[0].T, v_w_in[0].T, "adamw_w_in")
    res["grad"]["w_in"] = g_w_in.T[None]
    res["delta"]["w_in"], res["new_m"]["w_in"], res["new_v"]["w_in"] = d.T[None], nm.T[None], nv.T[None]
    order = ["norm_w", "w_in", "conv_w", "a_log", "dt_bias", "dn_norm_w", "w_o_dn", "w_o_dil", "w_out", "final_norm_w"]
    outs = [loss, grad_x[None]]
    for kind in ("grad", "delta", "new_m", "new_v"):
        outs += [res[kind][nm] for nm in order]
    return tuple(outs)
```

```python
import functools
import math

import jax
import jax.numpy as jnp
from jax import lax
from jax.experimental import pallas as pl
from jax.experimental.pallas import tpu as pltpu

F32 = jnp.float32
BF16 = jnp.bfloat16
MESH = pl.DeviceIdType.MESH

D_MODEL = 1024
DN_HEADS = 8
DN_D = 128
DN_CHUNK = 64
DN_W = DN_HEADS * DN_D
DIL_GROUPS = ((128, 1), (512, 4), (2048, 16))
N_DIL = len(DIL_GROUPS)
DIL_HEADS = 4
DIL_DH = 128
DIL_W = DIL_HEADS * DIL_DH
ATT_BLOCK = 128
NORM_EPS = 1e-6
PROJ_W = 11280
N_CHIPS = 4
SHARD_W = PROJ_W // N_CHIPS

OFF_QKV_A = 0
OFF_Z_A = 3072
OFF_Q_B = 4096
OFF_K_B = 5632
OFF_V_B = 7168
OFF_Z_B = 8704
OFF_G_A = 9216
OFF_G_B = 10240
OFF_BA = 11264
PW = 11520
REF_OFF_BA = 4096

ADAM_LR = 0.001
ADAM_B1 = 0.9
ADAM_B2 = 0.999
ADAM_EPS = 1e-08
ADAM_WD = 0.01
ADAM_STEP = 10

ROW_TILE = 256
NEG = -1e30


def _dot(a, b):
    return jnp.dot(a.astype(BF16), b.astype(BF16), preferred_element_type=F32)


def _dot_nt(a, b):
    return lax.dot_general(a.astype(BF16), b.astype(BF16), (((1,), (1,)), ((), ())), preferred_element_type=F32)


def _dot_tn(a, b):
    return lax.dot_general(a.astype(BF16), b.astype(BF16), (((0,), (0,)), ((), ())), preferred_element_type=F32)


def _split(a):
    hi = a.astype(BF16)
    lo = (a - hi.astype(F32)).astype(BF16)
    return hi, lo


def _dot_exact_lhs(c, a):
    hi, lo = _split(a)
    cb = c.astype(BF16)
    return jnp.dot(cb, hi, preferred_element_type=F32) + jnp.dot(cb, lo, preferred_element_type=F32)


def _dot_exact_rhs(a, c):
    hi, lo = _split(a)
    cb = c.astype(BF16)
    return jnp.dot(hi, cb, preferred_element_type=F32) + jnp.dot(lo, cb, preferred_element_type=F32)


def _dot_tn_exact_rhs(a, c):
    hi, lo = _split(a)
    cb = c.astype(BF16)
    dn = (((0,), (0,)), ((), ()))
    return (lax.dot_general(hi, cb, dn, preferred_element_type=F32)
            + lax.dot_general(lo, cb, dn, preferred_element_type=F32))


def _sigmoid(x):
    return 1.0 / (1.0 + jnp.exp(-x))


def _silu(x):
    return x * _sigmoid(x)


def _silu_grad(x):
    s = _sigmoid(x)
    return s * (1.0 + x * (1.0 - s))


def _softplus(x):
    return jnp.maximum(x, 0.0) + jnp.log(1.0 + jnp.exp(-jnp.abs(x)))


def _cparams(*sem):
    return pltpu.CompilerParams(dimension_semantics=sem)


def _matmul(a, b, out_dtype, tm, tn, tk, name, nt=False, transpose_out=False, after=None, also_bf16=False):
    m, kdim = a.shape
    n = b.shape[0] if nt else b.shape[1]
    tm, tn, tk = min(tm, m), min(tn, n), min(tk, kdim)
    assert m % tm == 0 and n % tn == 0 and kdim % tk == 0, (name, a.shape, b.shape, tm, tn, tk)
    nk = kdim // tk
    dot = _dot_nt if nt else _dot
    b_spec = (pl.BlockSpec((tn, tk), lambda i, j, k: (j, k)) if nt else pl.BlockSpec((tk, tn), lambda i, j, k: (k, j)))
    extra = [] if after is None else [after]
    out_dtypes = [out_dtype] + ([BF16] if also_bf16 else [])

    def emit(o_refs, acc):
        val = acc.T if transpose_out else acc
        for o_ref in o_refs:
            o_ref[...] = val.astype(o_ref.dtype)

    def outs_of(rest):
        return rest[len(extra):len(extra) + len(out_dtypes)]

    if nk == 1:
        def body(a_ref, b_ref, *rest):
            emit(outs_of(rest), dot(a_ref[...], b_ref[...]))
        scratch = []
    else:
        def body(a_ref, b_ref, *rest):
            o_ref, acc_ref = outs_of(rest), rest[-1]
            k = pl.program_id(2)
            p = dot(a_ref[...], b_ref[...])

            @pl.when(k == 0)
            def _():
                acc_ref[...] = p

            @pl.when(k > 0)
            def _():
                acc_ref[...] += p

            @pl.when(k == nk - 1)
            def _():
                emit(o_ref, acc_ref[...])
        scratch = [pltpu.VMEM((tm, tn), F32)]

    if transpose_out:
        out_spec, out_shape = pl.BlockSpec((tn, tm), lambda i, j, k: (j, i)), (n, m)
    else:
        out_spec, out_shape = pl.BlockSpec((tm, tn), lambda i, j, k: (i, j)), (m, n)
    res = pl.pallas_call(
        body, name=name, grid=(m // tm, n // tn, nk),
        in_specs=[pl.BlockSpec((tm, tk), lambda i, j, k: (i, k)), b_spec] + [pl.BlockSpec(memory_space=pl.ANY)] * len(extra),
        out_specs=[out_spec] * len(out_dtypes), out_shape=[jax.ShapeDtypeStruct(out_shape, d) for d in out_dtypes],
        scratch_shapes=scratch, compiler_params=_cparams("parallel", "parallel", "arbitrary"))(a, b, *extra)
    return res if also_bf16 else res[0]


def _rms_in(x, nw):
    s, d = x.shape

    def body(x_ref, w_ref, h_ref, ht_ref):
        xv = x_ref[...]
        r = lax.rsqrt(jnp.mean(xv * xv, axis=-1, keepdims=True) + NORM_EPS)
        h = xv * r * w_ref[...]
        h_ref[...] = h.astype(BF16)
        ht_ref[...] = h.T.astype(BF16)

    return pl.pallas_call(
        body, name="rms_in", grid=(s // ROW_TILE,),
        in_specs=[pl.BlockSpec((ROW_TILE, d), lambda i: (i, 0)), pl.BlockSpec((1, d), lambda i: (0, 0))],
        out_specs=[pl.BlockSpec((ROW_TILE, d), lambda i: (i, 0)), pl.BlockSpec((d, ROW_TILE), lambda i: (0, i))],
        out_shape=[jax.ShapeDtypeStruct((s, d), BF16), jax.ShapeDtypeStruct((d, s), BF16)],
        compiler_params=_cparams("parallel"))(x, nw)


def _rms_in_bwd(x, nw, dh, dx2):
    s, d = x.shape

    def body(x_ref, w_ref, dh_ref, dx2_ref, dx_ref, dw_ref):
        i = pl.program_id(0)
        xv = x_ref[...]
        r = lax.rsqrt(jnp.mean(xv * xv, axis=-1, keepdims=True) + NORM_EPS)
        dhv = dh_ref[...]
        dyw = dhv * w_ref[...]
        dx_ref[...] = dx2_ref[...] + r * dyw - xv * (r * r * r) * jnp.mean(dyw * xv, axis=-1, keepdims=True)
        part = jnp.sum(dhv * xv * r, axis=0, keepdims=True)

        @pl.when(i == 0)
        def _():
            dw_ref[...] = part

        @pl.when(i > 0)
        def _():
            dw_ref[...] += part

    row = pl.BlockSpec((ROW_TILE, d), lambda i: (i, 0))
    vec = pl.BlockSpec((1, d), lambda i: (0, 0))
    return pl.pallas_call(
        body, name="rms_in_bwd", grid=(s // ROW_TILE,), in_specs=[row, vec, row, row], out_specs=[row, vec],
        out_shape=[jax.ShapeDtypeStruct((s, d), F32), jax.ShapeDtypeStruct((1, d), F32)],
        compiler_params=_cparams("arbitrary"))(x, nw, dh, dx2)


def _shift_down(cur, prev8, k):
    rc = pltpu.roll(cur, k, 0)
    rp = pltpu.roll(prev8, k, 0)
    row = lax.broadcasted_iota(jnp.int32, prev8.shape, 0)
    top = jnp.where(row < k, rp, rc[:8])
    return jnp.concatenate([top, rc[8:]], axis=0)


def _shift_up(cur, next8, k):
    t = cur.shape[0]
    rc = pltpu.roll(cur, t - k, 0)
    rn = pltpu.roll(next8, 8 - k, 0)
    row = lax.broadcasted_iota(jnp.int32, next8.shape, 0)
    bot = jnp.where(row >= 8 - k, rn, rc[t - 8:])
    return jnp.concatenate([rc[:t - 8], bot], axis=0)


def _conv_fwd(proj, conv_w):
    s = proj.shape[0]
    t8 = ROW_TILE // 8

    def body(u_ref, up_ref, w_ref, c_ref, y_ref):
        i = pl.program_id(0)
        part = pl.program_id(1)
        cur = u_ref[...]
        prev8 = jnp.where(i > 0, up_ref[...], 0.0)
        w = w_ref[...]
        c = cur * w[3:4, :]
        for k in (1, 2, 3):
            c = c + _shift_down(cur, prev8, k) * w[3 - k:4 - k, :]
        c_ref[...] = c
        a = _silu(c)
        for h in range(DN_HEADS):
            ah = a[:, h * DN_D:(h + 1) * DN_D]
            r = lax.rsqrt(jnp.sum(ah * ah, axis=-1, keepdims=True) + NORM_EPS)
            y_ref[:, h * DN_D:(h + 1) * DN_D] = jnp.where(part < 2, ah * r, ah)

    return pl.pallas_call(
        body, name="conv_fwd", grid=(s // ROW_TILE, 3),
        in_specs=[pl.BlockSpec((ROW_TILE, DN_W), lambda i, p: (i, p)),
                  pl.BlockSpec((8, DN_W), lambda i, p: (jnp.maximum(i * t8 - 1, 0), p)),
                  pl.BlockSpec((4, DN_W), lambda i, p: (0, p))],
        out_specs=[pl.BlockSpec((ROW_TILE, DN_W), lambda i, p: (i, p))] * 2,
        out_shape=[jax.ShapeDtypeStruct((s, 3 * DN_W), F32)] * 2,
        compiler_params=_cparams("parallel", "parallel"))(proj, proj, conv_w)


def _conv_bwd_act(c, dq, dk, dv):
    s = c.shape[0]

    def body(c_ref, dq_ref, dk_ref, dv_ref, dc_ref):
        for part, d_ref in enumerate((dq_ref, dk_ref, dv_ref)):
            for h in range(DN_HEADS):
                sl = slice(part * DN_W + h * DN_D, part * DN_W + (h + 1) * DN_D)
                ch = c_ref[:, sl]
                dyh = d_ref[:, h * DN_D:(h + 1) * DN_D]
                if part < 2:
                    ah = _silu(ch)
                    r = lax.rsqrt(jnp.sum(ah * ah, axis=-1, keepdims=True) + NORM_EPS)
                    dyh = r * dyh - ah * (r * r * r) * jnp.sum(dyh * ah, axis=-1, keepdims=True)
                dc_ref[:, sl] = dyh * _silu_grad(ch)

    wide = pl.BlockSpec((ROW_TILE, 3 * DN_W), lambda i: (i, 0))
    row = pl.BlockSpec((ROW_TILE, DN_W), lambda i: (i, 0))
    return pl.pallas_call(
        body, name="conv_bwd_act", grid=(s // ROW_TILE,), in_specs=[wide, row, row, row], out_specs=wide,
        out_shape=jax.ShapeDtypeStruct((s, 3 * DN_W), F32), compiler_params=_cparams("parallel"))(c, dq, dk, dv)


DPROJ_IN = pl.BlockSpec(memory_space=pl.ANY)


def _conv_bwd(proj, dc, conv_w, dproj):
    s = proj.shape[0]
    t8 = ROW_TILE // 8
    nrow = s // ROW_TILE
    last8 = s // 8 - 1

    def body(u_ref, dc_ref, dcn_ref, w_ref, dproj_in, du_ref, dw_ref):
        i = pl.program_id(1)
        cur = u_ref[...]
        dcv = dc_ref[...]
        next8 = jnp.where(i < nrow - 1, dcn_ref[...], 0.0)
        w = w_ref[...]

        @pl.when(i == 0)
        def _():
            dw_ref[...] = jnp.zeros_like(dw_ref)

        du = dcv * w[3:4, :]
        dw_ref[3:4, :] += jnp.sum(cur * dcv, axis=0, keepdims=True)
        for k in (1, 2, 3):
            ahead = _shift_up(dcv, next8, k)
            du = du + ahead * w[3 - k:4 - k, :]
            dw_ref[3 - k:4 - k, :] += jnp.sum(cur * ahead, axis=0, keepdims=True)
        du_ref[...] = du.astype(BF16)

    blk = pl.BlockSpec((ROW_TILE, DN_W), lambda p, i: (i, p))
    return pl.pallas_call(
        body, name="conv_bwd", grid=(3, nrow),
        in_specs=[blk, blk, pl.BlockSpec((8, DN_W), lambda p, i: (jnp.minimum((i + 1) * t8, last8), p)),
                  pl.BlockSpec((4, DN_W), lambda p, i: (0, p)), DPROJ_IN],
        out_specs=[blk, pl.BlockSpec((4, DN_W), lambda p, i: (0, p))],
        out_shape=[jax.ShapeDtypeStruct((s, PW), BF16), jax.ShapeDtypeStruct((4, 3 * DN_W), F32)],
        input_output_aliases={4: 0},
        compiler_params=_cparams("parallel", "arbitrary"))(proj, dc, dc, conv_w, dproj)


def _gates_fwd(proj, gate_par):
    s = proj.shape[0]

    def body(ba_ref, par_ref, o_ref):
        v = ba_ref[...]
        lane = lax.broadcasted_iota(jnp.int32, v.shape, 1)
        beta = _sigmoid(v)
        g = -jnp.exp(par_ref[0:1, :]) * _softplus(v + par_ref[1:2, :])
        o_ref[...] = jnp.where(lane < DN_HEADS, beta, jnp.where(lane < 2 * DN_HEADS, g, 0.0))

    return pl.pallas_call(
        body, name="gates_fwd", grid=(s // ROW_TILE,),
        in_specs=[pl.BlockSpec((ROW_TILE, 128), lambda i: (i, OFF_BA // 128)), pl.BlockSpec((8, 128), lambda i: (0, 0))],
        out_specs=pl.BlockSpec((ROW_TILE, 128), lambda i: (i, 0)),
        out_shape=jax.ShapeDtypeStruct((s, 128), F32), compiler_params=_cparams("parallel"))(proj, gate_par)


def _gates_bwd(proj, gate_par, dbg, dproj):
    s = proj.shape[0]

    def body(ba_ref, par_ref, d_ref, dproj_in, o_ref, dpar_ref):
        i = pl.program_id(0)
        v = ba_ref[...]
        dv = d_ref[...]
        lane = lax.broadcasted_iota(jnp.int32, v.shape, 1)
        beta = _sigmoid(v)
        nega = -jnp.exp(par_ref[0:1, :])
        xs = v + par_ref[1:2, :]
        dsp = dv * nega * _sigmoid(xs)
        dal = dv * nega * _softplus(xs)
        is_b = lane < DN_HEADS
        is_g = jnp.logical_and(lane >= DN_HEADS, lane < 2 * DN_HEADS)
        o_ref[:, :128] = jnp.where(is_b, dv * beta * (1.0 - beta), jnp.where(is_g, dsp, 0.0)).astype(BF16)
        o_ref[:, 128:] = jnp.zeros((ROW_TILE, PW - OFF_BA - 128), BF16)
        r0 = jnp.sum(jnp.where(is_g, dal, 0.0), axis=0, keepdims=True)
        r1 = jnp.sum(jnp.where(is_g, dsp, 0.0), axis=0, keepdims=True)

        @pl.when(i == 0)
        def _():
            dpar_ref[...] = jnp.zeros_like(dpar_ref)

        dpar_ref[0:1, :] += r0
        dpar_ref[1:2, :] += r1

    return pl.pallas_call(
        body, name="gates_bwd", grid=(s // ROW_TILE,),
        in_specs=[pl.BlockSpec((ROW_TILE, 128), lambda i: (i, OFF_BA // 128)), pl.BlockSpec((8, 128), lambda i: (0, 0)),
                  pl.BlockSpec((ROW_TILE, 128), lambda i: (i, 0)), DPROJ_IN],
        out_specs=[pl.BlockSpec((ROW_TILE, PW - OFF_BA), lambda i: (i, OFF_BA // (PW - OFF_BA))),
                   pl.BlockSpec((8, 128), lambda i: (0, 0))],
        out_shape=[jax.ShapeDtypeStruct((s, PW), BF16), jax.ShapeDtypeStruct((8, 128), F32)],
        input_output_aliases={3: 0},
        compiler_params=_cparams("arbitrary"))(proj, gate_par, dbg, dproj)


def _chunk_masks():
    c = DN_CHUNK
    ii = lax.broadcasted_iota(jnp.int32, (c, c), 0)
    jj = lax.broadcasted_iota(jnp.int32, (c, c), 1)
    return dict(ii=ii, jj=jj, lower=(ii >= jj), strict=(ii > jj), eye=(ii == jj),
                lower_f=(ii >= jj).astype(BF16), upper_f=(ii <= jj).astype(BF16), ones8=jnp.ones((8, c), BF16))


class _Heads:
    def __init__(self, xs):
        self.xs = list(xs)

    def _bin(self, o, f):
        if isinstance(o, _Heads):
            return _Heads([f(a, b) for a, b in zip(self.xs, o.xs)])
        return _Heads([f(a, o) for a in self.xs])

    def __add__(self, o):
        return self._bin(o, lambda a, b: a + b)

    def __sub__(self, o):
        return self._bin(o, lambda a, b: a - b)

    def __mul__(self, o):
        return self._bin(o, lambda a, b: a * b)

    __radd__ = __add__
    __rmul__ = __mul__

    def __neg__(self):
        return _Heads([-a for a in self.xs])

    def __getitem__(self, i):
        return _Heads([a[i] for a in self.xs])


def _hmap(f, *args):
    n = next(len(a.xs) for a in args if isinstance(a, _Heads))
    return _Heads([f(*[(a.xs[h] if isinstance(a, _Heads) else a) for a in args]) for h in range(n)])


def _hdot(a, b):
    return _hmap(_dot, a, b)


def _hdot_nt(a, b):
    return _hmap(_dot_nt, a, b)


def _hdot_tn(a, b):
    return _hmap(_dot_tn, a, b)


def _hcat(a, b, axis):
    return _hmap(lambda x, y: jnp.concatenate([x, y], axis=axis), a, b)


def _hsum(a, axis):
    return _hmap(lambda t: jnp.sum(t, axis=axis, keepdims=True), a)


def _hwhere(c, a, b):
    return _hmap(jnp.where, c, a, b)


def _chunk_gates(mk, bg):
    c = DN_CHUNK
    gc_all = _dot_exact_lhs(mk["lower_f"], bg)
    rows = jnp.concatenate([gc_all, gc_all], axis=0).T
    hs = range(DN_HEADS)
    return (_Heads(bg[:, h:h + 1] for h in hs), _Heads(gc_all[:, DN_HEADS + h:DN_HEADS + h + 1] for h in hs),
            _Heads(rows[DN_HEADS + h:DN_HEADS + h + 1, :] for h in hs))


def _chunk_common(mk, q, k, beta_col, gc_col, gc_r):
    c = DN_CHUNK
    lower, strict = mk["lower"], mk["strict"]
    qs = q * (DN_D ** -0.5)
    beta_b = _hmap(lambda t: jnp.broadcast_to(t, (c, DN_D)), beta_col)
    gc_b = _hmap(lambda t: jnp.broadcast_to(t, (c, DN_D)), gc_col)
    gc_sq = gc_b[:, :c]
    gam = _hwhere(lower, _hmap(lambda t: jnp.exp(jnp.minimum(t, 0.0)), gc_sq - gc_r[:, :c]), 0.0)
    egc = _hmap(jnp.exp, gc_b)
    gl = gc_b[c - 1:c, :]
    ekd = _hmap(jnp.exp, gl - gc_b)
    dl = _hmap(jnp.exp, gl)
    kb = k * beta_b
    scores = _hdot_nt(_hcat(kb, qs, 0), k)
    a_strict = _hwhere(strict, scores[:c] * gam, 0.0)
    aqk = _hwhere(lower, scores[c:] * gam, 0.0)
    return dict(k=k, qs=qs, beta_b=beta_b, gc_b=gc_b, gam=gam, egc=egc, ekd=ekd, dl=dl, kb=kb, a_strict=a_strict, aqk=aqk)


def _unit_lower_inverse_minus_eye(n_strict, ii, jj):
    same = lax.shift_right_logical(ii, 4) == lax.shift_right_logical(jj, 4)
    dmat = _hwhere(same, n_strict, 0.0)
    omat = n_strict - dmat
    d2 = _hdot(dmat, dmat)
    d4 = _hdot(d2, d2)
    d8 = _hdot(d4, d4)
    x1 = d2 - dmat - _hdot(dmat, d2)
    x2 = x1 + d4 + _hdot(x1, d4)
    x3 = x2 + d8 + _hdot(x2, d8)
    n1 = omat + _hdot(x3, omat)
    n2 = _hdot(n1, n1)
    y = n2 - n1 - _hdot(n1, n2)
    return y + x3 + _hdot(y, x3)


GDR_HEAD_SETS = (range(0, DN_HEADS),)


def _gdr_fwd(qkv, bg):
    s = qkv.shape[0]
    c = DN_CHUNK
    n = s // c

    def body(q_ref, k_ref, v_ref, bg_ref, o_ref, u_ref, w_ref, vn_ref, tm_ref, st_ref, state):
        @pl.when(pl.program_id(0) == 0)
        def _():
            state[...] = jnp.zeros_like(state)

        mk = _chunk_masks()
        gates = _chunk_gates(mk, bg_ref[...])
        for hs in GDR_HEAD_SETS:
            sls = [slice(h * DN_D, (h + 1) * DN_D) for h in hs]
            cm = _chunk_common(mk, _Heads(q_ref[:, sl] for sl in sls), _Heads(k_ref[:, sl] for sl in sls),
                               *[_Heads(g.xs[h] for h in hs) for g in gates])
            tm = _unit_lower_inverse_minus_eye(cm["a_strict"], mk["ii"], mk["jj"])
            rhs_u = _Heads(v_ref[:, sl] for sl in sls) * cm["beta_b"]
            rhs_w = cm["kb"] * cm["egc"]
            t_rhs = _hdot(tm, _hcat(rhs_u, rhs_w, 1))
            u = rhs_u + t_rhs[:, :DN_D]
            w = rhs_w + t_rhs[:, DN_D:]
            st = _Heads(state[h] for h in hs)
            on_state = _hdot(_hcat(w, cm["qs"] * cm["egc"], 0), st)
            v_new = u - on_state[:c]
            o = on_state[c:] + _hdot(cm["aqk"], v_new)
            st_new = st * cm["dl"] + _hdot_tn(cm["k"] * cm["ekd"], v_new)
            for i, (h, sl) in enumerate(zip(hs, sls)):
                o_ref[:, sl] = o.xs[i]
                u_ref[:, sl] = u.xs[i]
                w_ref[:, sl] = w.xs[i]
                vn_ref[:, sl] = v_new.xs[i]
                tm_ref[h, 0] = tm.xs[i]
                st_ref[h, 0] = st.xs[i]
                state[h] = st_new.xs[i]

    def part(p):
        return pl.BlockSpec((c, DN_W), lambda j: (j, p))

    return pl.pallas_call(
        body, name="gdr_fwd", grid=(n,),
        in_specs=[part(0), part(1), part(2), pl.BlockSpec((c, 128), lambda j: (j, 0))],
        out_specs=[part(0)] * 4 + [pl.BlockSpec((DN_HEADS, 1, c, c), lambda j: (0, j, 0, 0)),
                                   pl.BlockSpec((DN_HEADS, 1, DN_D, DN_D), lambda j: (0, j, 0, 0))],
        out_shape=[jax.ShapeDtypeStruct((s, DN_W), F32)] * 4
        + [jax.ShapeDtypeStruct((DN_HEADS, n, c, c), F32), jax.ShapeDtypeStruct((DN_HEADS, n, DN_D, DN_D), F32)],
        scratch_shapes=[pltpu.VMEM((DN_HEADS, DN_D, DN_D), F32)],
        compiler_params=_cparams("arbitrary"))(qkv, qkv, qkv, bg)


def _gdr_bwd(qkv, bg, u, w, vn, tmat, states, do):
    s = qkv.shape[0]
    c = DN_CHUNK
    n = s // c

    def body(q_ref, k_ref, v_ref, bg_ref, u_ref, w_ref, vn_ref, tm_ref, st_ref, do_ref,
             dq_ref, dk_ref, dv_ref, dbg_ref, dstate):
        @pl.when(pl.program_id(0) == 0)
        def _():
            dstate[...] = jnp.zeros_like(dstate)

        mk = _chunk_masks()
        lower, strict = mk["lower"], mk["strict"]
        bg = bg_ref[...]
        ones = jnp.ones((c, DN_D), BF16)
        rowi = lax.broadcasted_iota(jnp.int32, (c, DN_D), 0)
        lane = lax.broadcasted_iota(jnp.int32, (c, 128), 1)
        hs = range(DN_HEADS)
        sls = [slice(h * DN_D, (h + 1) * DN_D) for h in hs]

        def heads_of(ref):
            return _Heads(ref[:, sl] for sl in sls)

        cm = _chunk_common(mk, heads_of(q_ref), heads_of(k_ref), *_chunk_gates(mk, bg))
        k, qs, beta_b = cm["k"], cm["qs"], cm["beta_b"]
        gam, egc, ekd, dl, kb = cm["gam"], cm["egc"], cm["ekd"], cm["dl"], cm["kb"]
        aqk, a_strict = cm["aqk"], cm["a_strict"]
        v, uu, ww, v_new, dov = heads_of(v_ref), heads_of(u_ref), heads_of(w_ref), heads_of(vn_ref), heads_of(do_ref)
        st = _Heads(st_ref[h, 0] for h in hs)
        dsn = _Heads(dstate[h] for h in hs)
        qd = qs * egc
        kd = k * ekd

        dv_new = _hdot_tn(aqk, dov) + _hdot(kd, dsn)
        do_sv = _hdot_nt(dov, _hcat(st, v_new, 0))
        dqd = do_sv[:, :DN_D]
        daqk = _hwhere(lower, do_sv[:, DN_D:], 0.0)
        dkd = _hdot_nt(v_new, dsn)
        ddl = _hsum(_hsum(dsn * st, 1), 0)
        dw = -_hdot_nt(dv_new, st)
        ds_new = dsn * dl + _hdot_tn(_hcat(qd, -ww, 0), _hcat(dov, dv_new, 0))

        tm = _Heads(tm_ref[h, 0] for h in hs)
        tt = _hdot_tn(tm, _hcat(dv_new, dw, 1))
        dru = dv_new + tt[:, :DN_D]
        drw = dw + tt[:, DN_D:]
        dn = _hwhere(strict, -_hdot_nt(_hcat(dru, drw, 1), _hcat(uu, ww, 1)), 0.0)
        dag = dn * gam
        dqg = daqk * gam
        both = _hcat(dag, dqg, 0)
        on_k = _hdot(both, k)
        dkb = on_k[:c] + drw * egc
        dqs = on_k[c:] + dqd * egc
        dk = _hdot_tn(both, _hcat(kb, qs, 0)) + dkb * beta_b + dkd * ekd
        pmat = dn * a_strict + daqk * aqk
        tkd = _hsum(dkd * kd, -1)
        dgc = (_hsum(pmat, -1) - _hmap(_dot_tn_exact_rhs, pmat, ones) + _hsum(drw * (kb * egc), -1)
               + _hsum(dqd * qd, -1) - tkd)
        last = _hsum(tkd, 0) + ddl * dl
        dgc = dgc + _hwhere(rowi == c - 1, last, 0.0)
        dbeta = _hsum(dru * v, -1) + _hsum(dkb * k, -1)
        dq = dqs * (DN_D ** -0.5)
        dv = dru * beta_b

        dgc_all = jnp.zeros((c, 128), F32)
        dbg = jnp.zeros((c, 128), F32)
        for h, sl in zip(hs, sls):
            dq_ref[:, sl] = dq.xs[h]
            dk_ref[:, sl] = dk.xs[h]
            dv_ref[:, sl] = dv.xs[h]
            dstate[h] = ds_new.xs[h]
            dgc_all = dgc_all + jnp.where(lane == DN_HEADS + h, dgc.xs[h], 0.0)
            dbg = dbg + jnp.where(lane == h, dbeta.xs[h], 0.0)
        dbg_ref[...] = dbg + _dot_exact_lhs(mk["upper_f"], dgc_all)

    def part(p):
        return pl.BlockSpec((c, DN_W), lambda j: (n - 1 - j, p))

    vec = pl.BlockSpec((c, 128), lambda j: (n - 1 - j, 0))
    return pl.pallas_call(
        body, name="gdr_bwd", grid=(n,),
        in_specs=[part(0), part(1), part(2), vec, part(0), part(0), part(0),
                  pl.BlockSpec((DN_HEADS, 1, c, c), lambda j: (0, n - 1 - j, 0, 0)),
                  pl.BlockSpec((DN_HEADS, 1, DN_D, DN_D), lambda j: (0, n - 1 - j, 0, 0)), part(0)],
        out_specs=[part(0), part(0), part(0), vec],
        out_shape=[jax.ShapeDtypeStruct((s, DN_W), F32)] * 3 + [jax.ShapeDtypeStruct((s, 128), F32)],
        scratch_shapes=[pltpu.VMEM((DN_HEADS, DN_D, DN_D), F32)],
        compiler_params=_cparams("arbitrary"))(qkv, qkv, qkv, bg, u, w, vn, tmat, states, do)


def _gdr_out(o, proj, dnw):
    s = o.shape[0]

    def body(o_ref, z_ref, w_ref, y_ref, yt_ref):
        ov, zv, wv = o_ref[...], z_ref[...], w_ref[...]
        for h in range(DN_HEADS):
            sl = slice(h * DN_D, (h + 1) * DN_D)
            oh = ov[:, sl]
            r = lax.rsqrt(jnp.mean(oh * oh, axis=-1, keepdims=True) + NORM_EPS)
            y = (oh * r * wv) * _silu(zv[:, sl])
            y_ref[:, sl] = y.astype(BF16)
            yt_ref[sl, :] = y.T.astype(BF16)

    row = pl.BlockSpec((ROW_TILE, DN_W), lambda i: (i, 0))
    return pl.pallas_call(
        body, name="gdr_out", grid=(s // ROW_TILE,),
        in_specs=[row, pl.BlockSpec((ROW_TILE, DN_W), lambda i: (i, OFF_Z_A // DN_W)), pl.BlockSpec((1, DN_D), lambda i: (0, 0))],
        out_specs=[row, pl.BlockSpec((DN_W, ROW_TILE), lambda i: (0, i))],
        out_shape=[jax.ShapeDtypeStruct((s, DN_W), BF16), jax.ShapeDtypeStruct((DN_W, s), BF16)],
        compiler_params=_cparams("parallel"))(o, proj, dnw)


def _gdr_out_bwd(o, proj, dnw, dy):
    s = o.shape[0]

    def body(o_ref, z_ref, w_ref, dy_ref, do_ref, dz_ref, dw_ref):
        i = pl.program_id(0)
        ov, zv, wv, dyv = o_ref[...], z_ref[...], w_ref[...], dy_ref[...]
        acc = jnp.zeros((1, DN_D), F32)
        for h in range(DN_HEADS):
            sl = slice(h * DN_D, (h + 1) * DN_D)
            oh, zh, dh = ov[:, sl], zv[:, sl], dyv[:, sl]
            r = lax.rsqrt(jnp.mean(oh * oh, axis=-1, keepdims=True) + NORM_EPS)
            dn = dh * _silu(zh)
            dz_ref[:, sl] = (dh * (oh * r * wv) * _silu_grad(zh)).astype(BF16)
            acc = acc + jnp.sum(dn * oh * r, axis=0, keepdims=True)
            dnw_ = dn * wv
            do_ref[:, sl] = r * dnw_ - oh * (r * r * r) * jnp.mean(dnw_ * oh, axis=-1, keepdims=True)

        @pl.when(i == 0)
        def _():
            dw_ref[...] = acc

        @pl.when(i > 0)
        def _():
            dw_ref[...] += acc

    row = pl.BlockSpec((ROW_TILE, DN_W), lambda i: (i, 0))
    vec = pl.BlockSpec((1, DN_D), lambda i: (0, 0))
    return pl.pallas_call(
        body, name="gdr_out_bwd", grid=(s // ROW_TILE,),
        in_specs=[row, pl.BlockSpec((ROW_TILE, DN_W), lambda i: (i, OFF_Z_A // DN_W)), vec, row],
        out_specs=[row, pl.BlockSpec((ROW_TILE, DN_W), lambda i: (i, OFF_Z_A // DN_W)), vec],
        out_shape=[jax.ShapeDtypeStruct((s, DN_W), F32), jax.ShapeDtypeStruct((s, PW), BF16),
                   jax.ShapeDtypeStruct((1, DN_D), F32)],
        compiler_params=_cparams("arbitrary"))(o, proj, dnw, dy)


def _slope(group, head):
    idx = (group * DIL_HEADS + head + 1).astype(F32)
    return jnp.exp(jnp.full((1, 128), -8.0 * math.log(2.0) / (N_DIL * DIL_HEADS), F32) * idx)


def _att_scores(qb, k_cur, k_prev, slope_d, has_prev):
    iq = lax.broadcasted_iota(jnp.int32, (ATT_BLOCK, ATT_BLOCK), 0)
    jk = lax.broadcasted_iota(jnp.int32, (ATT_BLOCK, ATT_BLOCK), 1)
    dist_c = (iq - jk).astype(F32)
    s_cur = jnp.where(iq >= jk, _dot_nt(qb, k_cur) - slope_d * dist_c, NEG)
    s_prev = jnp.where(jnp.logical_and(jk >= iq, has_prev),
                       _dot_nt(qb, k_prev) - slope_d * (dist_c + float(ATT_BLOCK)), NEG)
    return s_cur, s_prev


def _att_scores_whole(qb, k, slope_d):
    n = 2 * ATT_BLOCK
    dist = lax.broadcasted_iota(jnp.int32, (n, n), 0) - lax.broadcasted_iota(jnp.int32, (n, n), 1)
    valid = jnp.logical_and(dist >= 0, dist <= ATT_BLOCK)
    return jnp.where(valid, _dot_nt(qb, k) - slope_d[:, 0:1] * dist.astype(F32), NEG)


def _att_tiles(i, dil, nb):
    tiles = nb // 2
    per = dil * tiles // ATT_UNROLL
    assert nb % 2 == 0 and tiles >= 2 and per * ATT_UNROLL == dil * tiles
    for i0 in range(per):
        ts = [divmod(i0 + u * per, tiles) for u in range(ATT_UNROLL)]
        assert all(a[0] != b[0] or abs(a[1] - b[1]) >= 2 for n, a in enumerate(ts) for b in ts[n + 1:])
    qrows, krows, has_prev = [], [], []
    for u in range(ATT_UNROLL):
        t = i + u * per
        r = lax.div(t, tiles)
        j = lax.rem(t, tiles)
        qbase = r + dil * 2 * ATT_BLOCK * j
        kbase = qbase - dil * ATT_BLOCK * jnp.minimum(j, 1)
        if dil == 1:
            qbase, kbase = pl.multiple_of(qbase, ATT_BLOCK), pl.multiple_of(kbase, ATT_BLOCK)
        qrows.append(pl.ds(qbase, 2 * ATT_BLOCK, stride=dil))
        krows.append(pl.ds(kbase, 3 * ATT_BLOCK, stride=dil))
        has_prev.append(j > 0)
    return qrows, krows, has_prev


def _att_scores_tile(qb, k, slope_d, has_prev):
    iq = lax.broadcasted_iota(jnp.int32, (2 * ATT_BLOCK, 3 * ATT_BLOCK), 0)
    ck = lax.broadcasted_iota(jnp.int32, (2 * ATT_BLOCK, 3 * ATT_BLOCK), 1)
    dist = iq - ck + jnp.where(has_prev, ATT_BLOCK, 0)
    valid = jnp.logical_and(dist >= 0, dist <= ATT_BLOCK)
    return jnp.where(valid, _dot_nt(qb, k) - slope_d[:, 0:1] * dist.astype(F32), NEG)


ATT_UNROLL = 4


def _att_blocks(i, dil, nb):
    per = dil * nb // ATT_UNROLL
    assert per * ATT_UNROLL == dil * nb
    for i0 in range(per):
        blocks = [divmod(i0 + u * per, nb) for u in range(ATT_UNROLL)]
        assert all(a[0] != b[0] or abs(a[1] - b[1]) >= 2 for n, a in enumerate(blocks) for b in blocks[n + 1:])
    curs, prvs, has_prev = [], [], []
    for u in range(ATT_UNROLL):
        t = i + u * per
        r = lax.div(t, nb)
        j = lax.rem(t, nb)
        base = r + dil * ATT_BLOCK * j
        pbase = base - dil * ATT_BLOCK * jnp.minimum(j, 1)
        if dil == 1:
            base, pbase = pl.multiple_of(base, ATT_BLOCK), pl.multiple_of(pbase, ATT_BLOCK)
        curs.append(pl.ds(base, ATT_BLOCK, stride=dil))
        prvs.append(pl.ds(pbase, ATT_BLOCK, stride=dil))
        has_prev.append(j > 0)
    return curs, prvs, has_prev


def _att_fwd(proj, group):
    s = proj.shape[0]
    dil = DIL_GROUPS[group][1]
    assert DIL_GROUPS[group][0] // dil == ATT_BLOCK
    nb = s // dil // ATT_BLOCK
    assert nb * dil * ATT_BLOCK == s

    def body(q_ref, k_ref, v_ref, num_ref, den_ref, mx_ref):
        slope_d = _slope(group, pl.program_id(0)) * float(dil)

        def step(i, carry):
            curs, prvs, has_prev = _att_blocks(i, dil, nb)
            us = range(ATT_UNROLL)
            qb = [q_ref[c, :] * (DIL_DH ** -0.5) for c in curs]
            sc = [_att_scores(qb[u], k_ref[curs[u], :], k_ref[prvs[u], :], slope_d, has_prev[u]) for u in us]
            mx = [jnp.maximum(jnp.max(a, axis=-1, keepdims=True), jnp.max(b, axis=-1, keepdims=True)) for a, b in sc]
            p_cur = [jnp.exp(sc[u][0] - mx[u]) for u in us]
            p_prev = [jnp.exp(sc[u][1] - mx[u]) for u in us]
            den = [jnp.sum(p_cur[u], axis=-1, keepdims=True) + jnp.sum(p_prev[u], axis=-1, keepdims=True) for u in us]
            num = [_dot(p_cur[u], v_ref[curs[u], :]) + _dot(p_prev[u], v_ref[prvs[u], :]) for u in us]
            for u in us:
                num_ref[curs[u], :] = num[u]
                den_ref[curs[u], :] = jnp.broadcast_to(den[u], (ATT_BLOCK, DIL_DH))
                mx_ref[curs[u], :] = jnp.broadcast_to(mx[u], (ATT_BLOCK, DIL_DH))
            return carry

        def step_whole(i, carry):
            rows = [pl.ds(i * ATT_UNROLL + u, 2 * ATT_BLOCK, stride=dil) for u in range(ATT_UNROLL)]
            sc = [_att_scores_whole(q_ref[r, :] * (DIL_DH ** -0.5), k_ref[r, :], slope_d) for r in rows]
            mx = [jnp.max(a, axis=-1, keepdims=True) for a in sc]
            p = [jnp.exp(a - m) for a, m in zip(sc, mx)]
            num = [_dot(pu, v_ref[r, :]) for pu, r in zip(p, rows)]
            for u, r in enumerate(rows):
                num_ref[r, :] = num[u]
                den_ref[r, :] = jnp.broadcast_to(jnp.sum(p[u], axis=-1, keepdims=True), (2 * ATT_BLOCK, DIL_DH))
                mx_ref[r, :] = jnp.broadcast_to(mx[u], (2 * ATT_BLOCK, DIL_DH))
            return carry

        def step_tile(i, carry):
            qrows, krows, has_prev = _att_tiles(i, dil, nb)
            us = range(ATT_UNROLL)
            sc = [_att_scores_tile(q_ref[qrows[u], :] * (DIL_DH ** -0.5), k_ref[krows[u], :], slope_d, has_prev[u]) for u in us]
            mx = [jnp.max(a, axis=-1, keepdims=True) for a in sc]
            p = [jnp.exp(a - m) for a, m in zip(sc, mx)]
            num = [_dot(p[u], v_ref[krows[u], :]) for u in us]
            for u in us:
                num_ref[qrows[u], :] = num[u]
                den_ref[qrows[u], :] = jnp.broadcast_to(jnp.sum(p[u], axis=-1, keepdims=True), (2 * ATT_BLOCK, DIL_DH))
                mx_ref[qrows[u], :] = jnp.broadcast_to(mx[u], (2 * ATT_BLOCK, DIL_DH))
            return carry

        if nb == 2:
            lax.fori_loop(0, dil // ATT_UNROLL, step_whole, 0)
        elif nb % 2 == 0:
            lax.fori_loop(0, dil * nb // 2 // ATT_UNROLL, step_tile, 0)
        else:
            lax.fori_loop(0, dil * nb // ATT_UNROLL, step, 0)

    def col(off):
        return pl.BlockSpec((s, DIL_DH), lambda h: (0, off // DIL_DH + group * DIL_HEADS + h))

    out = pl.BlockSpec((s, DIL_DH), lambda h: (0, h))
    return pl.pallas_call(
        body, name=f"att_fwd{group}", grid=(DIL_HEADS,), in_specs=[col(OFF_Q_B), col(OFF_K_B), col(OFF_V_B)],
        out_specs=[out, out, out], out_shape=[jax.ShapeDtypeStruct((s, DIL_W), F32)] * 3,
        compiler_params=_cparams("parallel"))(proj, proj, proj)


def _att_bwd(proj, group, do, lse, delta):
    s = proj.shape[0]
    dil = DIL_GROUPS[group][1]
    nb = s // dil // ATT_BLOCK

    def body(q_ref, k_ref, v_ref, do_ref, lse_ref, dl_ref, dq_ref, dk_ref, dv_ref, dq_acc, dk_acc, dv_acc):
        slope_d = _slope(group, pl.program_id(0)) * float(dil)
        dk_acc[...] = jnp.zeros_like(dk_acc)
        dv_acc[...] = jnp.zeros_like(dv_acc)

        def step(i, carry):
            curs, prvs, has_prev = _att_blocks(i, dil, nb)
            us = range(ATT_UNROLL)
            qb = [q_ref[c, :] * (DIL_DH ** -0.5) for c in curs]
            k_cur, k_prev = [k_ref[c, :] for c in curs], [k_ref[p, :] for p in prvs]
            v_cur, v_prev = [v_ref[c, :] for c in curs], [v_ref[p, :] for p in prvs]
            sc = [_att_scores(qb[u], k_cur[u], k_prev[u], slope_d, has_prev[u]) for u in us]
            lse_b, delta_b, dob = [lse_ref[c, :] for c in curs], [dl_ref[c, :] for c in curs], [do_ref[c, :] for c in curs]
            p_cur = [jnp.exp(sc[u][0] - lse_b[u]) for u in us]
            p_prev = [jnp.exp(sc[u][1] - lse_b[u]) for u in us]
            ds_cur = [p_cur[u] * (_dot_nt(dob[u], v_cur[u]) - delta_b[u]) for u in us]
            ds_prev = [p_prev[u] * (_dot_nt(dob[u], v_prev[u]) - delta_b[u]) for u in us]
            dq = [(_dot(ds_cur[u], k_cur[u]) + _dot(ds_prev[u], k_prev[u])) * (DIL_DH ** -0.5) for u in us]
            dk_c = [_dot_tn(ds_cur[u], qb[u]) for u in us]
            dv_c = [_dot_tn(p_cur[u], dob[u]) for u in us]
            dk_p = [_dot_tn(ds_prev[u], qb[u]) for u in us]
            dv_p = [_dot_tn(p_prev[u], dob[u]) for u in us]
            for u in us:
                dq_acc[curs[u], :] = dq[u]
                dk_acc[curs[u], :] += dk_c[u]
                dv_acc[curs[u], :] += dv_c[u]
            for u in us:
                dk_acc[prvs[u], :] += dk_p[u]
                dv_acc[prvs[u], :] += dv_p[u]
            return carry

        def step_whole(i, carry):
            rows = [pl.ds(i * ATT_UNROLL + u, 2 * ATT_BLOCK, stride=dil) for u in range(ATT_UNROLL)]
            qb = [q_ref[r, :] * (DIL_DH ** -0.5) for r in rows]
            kk, vv, dob = [k_ref[r, :] for r in rows], [v_ref[r, :] for r in rows], [do_ref[r, :] for r in rows]
            sc = [_att_scores_whole(qb[u], kk[u], slope_d) for u in range(ATT_UNROLL)]
            p = [jnp.exp(sc[u] - lse_ref[r, :][:, 0:1]) for u, r in enumerate(rows)]
            ds = [p[u] * (_dot_nt(dob[u], vv[u]) - dl_ref[r, :][:, 0:1]) for u, r in enumerate(rows)]
            dq = [_dot(ds[u], kk[u]) * (DIL_DH ** -0.5) for u in range(ATT_UNROLL)]
            dk = [_dot_tn(ds[u], qb[u]) for u in range(ATT_UNROLL)]
            dv = [_dot_tn(p[u], dob[u]) for u in range(ATT_UNROLL)]
            for u, r in enumerate(rows):
                dq_acc[r, :] = dq[u]
                dk_acc[r, :] = dk[u]
                dv_acc[r, :] = dv[u]
            return carry

        def step_tile(i, carry):
            qrows, krows, has_prev = _att_tiles(i, dil, nb)
            us = range(ATT_UNROLL)
            qb = [q_ref[r, :] * (DIL_DH ** -0.5) for r in qrows]
            kk, vv, dob = [k_ref[r, :] for r in krows], [v_ref[r, :] for r in krows], [do_ref[r, :] for r in qrows]
            sc = [_att_scores_tile(qb[u], kk[u], slope_d, has_prev[u]) for u in us]
            p = [jnp.exp(sc[u] - lse_ref[qrows[u], :][:, 0:1]) for u in us]
            ds = [p[u] * (_dot_nt(dob[u], vv[u]) - dl_ref[qrows[u], :][:, 0:1]) for u in us]
            dq = [_dot(ds[u], kk[u]) * (DIL_DH ** -0.5) for u in us]
            dk = [_dot_tn(ds[u], qb[u]) for u in us]
            dv = [_dot_tn(p[u], dob[u]) for u in us]
            for u in us:
                dq_acc[qrows[u], :] = dq[u]
                dk_acc[krows[u], :] += dk[u]
                dv_acc[krows[u], :] += dv[u]
            return carry

        if nb == 2:
            lax.fori_loop(0, dil // ATT_UNROLL, step_whole, 0)
        elif nb % 2 == 0:
            lax.fori_loop(0, dil * nb // 2 // ATT_UNROLL, step_tile, 0)
        else:
            lax.fori_loop(0, dil * nb // ATT_UNROLL, step, 0)
        dq_ref[...] = dq_acc[...].astype(BF16)
        dk_ref[...] = dk_acc[...].astype(BF16)
        dv_ref[...] = dv_acc[...].astype(BF16)

    def col(off):
        return pl.BlockSpec((s, DIL_DH), lambda h: (0, off // DIL_DH + group * DIL_HEADS + h))

    hd = pl.BlockSpec((s, DIL_DH), lambda h: (0, h))
    return pl.pallas_call(
        body, name=f"att_bwd{group}", grid=(DIL_HEADS,),
        in_specs=[col(OFF_Q_B), col(OFF_K_B), col(OFF_V_B), hd, hd, hd], out_specs=[hd, hd, hd],
        out_shape=[jax.ShapeDtypeStruct((s, DIL_W), BF16)] * 3,
        scratch_shapes=[pltpu.VMEM((s, DIL_DH), F32)] * 3,
        compiler_params=_cparams("parallel"))(proj, proj, proj, do, lse, delta)


def _att_merge(parts, proj):
    s = proj.shape[0]

    def body(n0, d0, m0, n1, d1, m1, n2, d2, m2, z_ref, ob_ref, o_ref, lse_ref, obt_ref):
        m = jnp.maximum(jnp.maximum(m0[...], m1[...]), m2[...])
        num = jnp.zeros_like(m)
        den = jnp.zeros_like(m)
        for nr, dr, mr in ((n0, d0, m0), (n1, d1, m1), (n2, d2, m2)):
            sc = jnp.exp(mr[...] - m)
            num = num + nr[...] * sc
            den = den + dr[...] * sc
        o = num / den
        o_ref[...] = o
        lse_ref[...] = m + jnp.log(den)
        ob = o * _silu(z_ref[...])
        ob_ref[...] = ob.astype(BF16)
        obt_ref[...] = ob.T.astype(BF16)

    row = pl.BlockSpec((ROW_TILE, DIL_W), lambda i: (i, 0))
    flat = [a for p in parts for a in p]
    return pl.pallas_call(
        body, name="att_merge", grid=(s // ROW_TILE,),
        in_specs=[row] * 9 + [pl.BlockSpec((ROW_TILE, DIL_W), lambda i: (i, OFF_Z_B // DIL_W))],
        out_specs=[row, row, row, pl.BlockSpec((DIL_W, ROW_TILE), lambda i: (0, i))],
        out_shape=[jax.ShapeDtypeStruct((s, DIL_W), BF16), jax.ShapeDtypeStruct((s, DIL_W), F32),
                   jax.ShapeDtypeStruct((s, DIL_W), F32), jax.ShapeDtypeStruct((DIL_W, s), BF16)],
        compiler_params=_cparams("parallel"))(*flat, proj)


def _att_merge_bwd(o, proj, dob, dproj):
    s = o.shape[0]

    def body(o_ref, z_ref, d_ref, dproj_in, do_ref, dl_ref, dz_ref):
        ov, zv, dv = o_ref[...], z_ref[...], d_ref[...]
        do = dv * _silu(zv)
        do_ref[...] = do
        dz_ref[...] = (dv * ov * _silu_grad(zv)).astype(BF16)
        for h in range(DIL_HEADS):
            sl = slice(h * DIL_DH, (h + 1) * DIL_DH)
            dl_ref[:, sl] = jnp.broadcast_to(jnp.sum(do[:, sl] * ov[:, sl], axis=-1, keepdims=True), (ROW_TILE, DIL_DH))

    row = pl.BlockSpec((ROW_TILE, DIL_W), lambda i: (i, 0))
    return pl.pallas_call(
        body, name="att_merge_bwd", grid=(s // ROW_TILE,),
        in_specs=[row, pl.BlockSpec((ROW_TILE, DIL_W), lambda i: (i, OFF_Z_B // DIL_W)), row, DPROJ_IN],
        out_specs=[row, row, pl.BlockSpec((ROW_TILE, DIL_W), lambda i: (i, OFF_Z_B // DIL_W))],
        out_shape=[jax.ShapeDtypeStruct((s, DIL_W), F32), jax.ShapeDtypeStruct((s, DIL_W), F32),
                   jax.ShapeDtypeStruct((s, PW), BF16)],
        input_output_aliases={3: 2},
        compiler_params=_cparams("parallel"))(o, proj, dob, dproj)


def _merge(proj, ya, yb):
    s = proj.shape[0]

    def body(ga_ref, gb_ref, ya_ref, yb_ref, o_ref, ot_ref):
        m = _sigmoid(ga_ref[...]) * ya_ref[...] + _sigmoid(gb_ref[...]) * yb_ref[...]
        o_ref[...] = m.astype(BF16)
        ot_ref[...] = m.T.astype(BF16)

    row = pl.BlockSpec((ROW_TILE, D_MODEL), lambda i: (i, 0))
    return pl.pallas_call(
        body, name="merge", grid=(s // ROW_TILE,),
        in_specs=[pl.BlockSpec((ROW_TILE, D_MODEL), lambda i: (i, OFF_G_A // D_MODEL)),
                  pl.BlockSpec((ROW_TILE, D_MODEL), lambda i: (i, OFF_G_B // D_MODEL)), row, row],
        out_specs=[row, pl.BlockSpec((D_MODEL, ROW_TILE), lambda i: (0, i))],
        out_shape=[jax.ShapeDtypeStruct((s, D_MODEL), BF16), jax.ShapeDtypeStruct((D_MODEL, s), BF16)],
        compiler_params=_cparams("parallel"))(proj, proj, ya, yb)


def _merge_bwd(proj, ya, yb, dm):
    s = proj.shape[0]

    def body(ga_ref, gb_ref, ya_ref, yb_ref, dm_ref, dya_ref, dyb_ref, dga_ref, dgb_ref):
        dmv = dm_ref[...]
        sa, sb = _sigmoid(ga_ref[...]), _sigmoid(gb_ref[...])
        dya_ref[...] = (dmv * sa).astype(BF16)
        dyb_ref[...] = (dmv * sb).astype(BF16)
        dga_ref[...] = (dmv * ya_ref[...] * sa * (1.0 - sa)).astype(BF16)
        dgb_ref[...] = (dmv * yb_ref[...] * sb * (1.0 - sb)).astype(BF16)

    row = pl.BlockSpec((ROW_TILE, D_MODEL), lambda i: (i, 0))
    return pl.pallas_call(
        body, name="merge_bwd", grid=(s // ROW_TILE,),
        in_specs=[pl.BlockSpec((ROW_TILE, D_MODEL), lambda i: (i, OFF_G_A // D_MODEL)),
                  pl.BlockSpec((ROW_TILE, D_MODEL), lambda i: (i, OFF_G_B // D_MODEL)), row, row, row],
        out_specs=[row] * 4, out_shape=[jax.ShapeDtypeStruct((s, D_MODEL), BF16)] * 4,
        compiler_params=_cparams("parallel"))(proj, proj, ya, yb, dm)


def _final(x, t, fw, tgt):
    s, d = x.shape

    def body(x_ref, t_ref, w_ref, y_ref, dx_ref, dw_ref, l_ref):
        i = pl.program_id(0)
        x2 = x_ref[...] + t_ref[...]
        wv = w_ref[...]
        r = lax.rsqrt(jnp.mean(x2 * x2, axis=-1, keepdims=True) + NORM_EPS)
        e = x2 * r * wv - y_ref[...]
        lrow = jnp.mean(e * e, axis=-1, keepdims=True)
        lpart = jnp.broadcast_to(0.5 * jnp.sum(lrow, axis=0, keepdims=True), (1, 128))
        dy = e * (1.0 / d)
        dwp = jnp.sum(dy * x2 * r, axis=0, keepdims=True)
        dyw = dy * wv
        dx_ref[...] = r * dyw - x2 * (r * r * r) * jnp.mean(dyw * x2, axis=-1, keepdims=True)

        @pl.when(i == 0)
        def _():
            dw_ref[...] = dwp
            l_ref[...] = lpart

        @pl.when(i > 0)
        def _():
            dw_ref[...] += dwp
            l_ref[...] += lpart

    row = pl.BlockSpec((ROW_TILE, d), lambda i: (i, 0))
    vec = pl.BlockSpec((1, d), lambda i: (0, 0))
    return pl.pallas_call(
        body, name="final", grid=(s // ROW_TILE,), in_specs=[row, row, vec, row],
        out_specs=[row, vec, pl.BlockSpec((1, 128), lambda i: (0, 0))],
        out_shape=[jax.ShapeDtypeStruct((s, d), F32), jax.ShapeDtypeStruct((1, d), F32), jax.ShapeDtypeStruct((1, 128), F32)],
        compiler_params=_cparams("arbitrary"))(x, t, fw, tgt)


def _adamw(w, g, m, v, name):
    r, c = w.shape
    cap = max(8, (1 << 18) // c)
    divisors = [t for t in range(8, min(r, cap) + 1, 8) if r % t == 0]
    tr = r if r <= 8 else (max(divisors) if divisors else cap)

    def body(w_ref, g_ref, m_ref, v_ref, d_ref, nm_ref, nv_ref):
        gv = g_ref[...]
        mn = ADAM_B1 * m_ref[...] + (1.0 - ADAM_B1) * gv
        vn = ADAM_B2 * v_ref[...] + (1.0 - ADAM_B2) * (gv * gv)
        m_hat = mn / (1.0 - ADAM_B1 ** ADAM_STEP)
        v_hat = vn / (1.0 - ADAM_B2 ** ADAM_STEP)
        d_ref[...] = -ADAM_LR * (m_hat / (jnp.sqrt(v_hat) + ADAM_EPS) + ADAM_WD * w_ref[...])
        nm_ref[...] = mn
        nv_ref[...] = vn

    blk = pl.BlockSpec((tr, c), lambda i: (i, 0))
    return pl.pallas_call(
        body, name=name, grid=(pl.cdiv(r, tr),), in_specs=[blk] * 4, out_specs=[blk] * 3,
        out_shape=[jax.ShapeDtypeStruct((r, c), F32)] * 3, compiler_params=_cparams("parallel"))(w, g, m, v)


HBM_SPEC = pl.BlockSpec(memory_space=pl.ANY)


def _place():
    x, y, c = lax.axis_index("x"), lax.axis_index("y"), lax.axis_index("c")
    chips = [(1 - x, y), (x, 1 - y), (1 - x, 1 - y)]
    return x, y, c, chips


def _ag_weights(packs):
    na = len(packs)
    nsem = 7

    def body(*refs):
        p_refs, out_refs = refs[:na], refs[na:2 * na]
        send_sems, recv_sems = refs[2 * na:]
        x, y, c, _ = _place()
        me, sib, j = (x, y, c), (x, y, 1 - c), 2 * x + y
        xn, yn = (1 - x, y, c), (x, 1 - y, c)
        jx, jy, jd = 2 * (1 - x) + y, 2 * x + (1 - y), 2 * (1 - x) + (1 - y)

        def rc(a, k, src, dst, to):
            return pltpu.make_async_remote_copy(src_ref=src, dst_ref=dst, send_sem=send_sems.at[nsem * a + k],
                                                recv_sem=recv_sems.at[nsem * a + k], device_id=to, device_id_type=MESH)

        sent = []
        for a in range(na):
            mine, land = p_refs[a].at[c], out_refs[a].at[j, c]
            sent += [rc(a, 0, mine, land, xn), rc(a, 1, mine, land, yn)]
        for cp in sent:
            cp.start()
        for a in range(na):
            half = p_refs[a].shape[1] // 2
            top, bottom = pl.ds(0, half), pl.ds(half, half)
            from_x, from_y, from_d = out_refs[a].at[jx, c], out_refs[a].at[jy, c], out_refs[a].at[jd, c]
            rc(a, 0, p_refs[a].at[c], from_x, me).wait_recv()
            later = [rc(a, 2, from_x.at[top], from_x.at[top], yn), rc(a, 4, from_x, from_x, sib)]
            for cp in later:
                cp.start()
            sent += later
            rc(a, 1, p_refs[a].at[c], from_y, me).wait_recv()
            later = [rc(a, 3, from_y.at[bottom], from_y.at[bottom], xn), rc(a, 5, from_y, from_y, sib)]
            for cp in later:
                cp.start()
            sent += later
            rc(a, 2, from_d.at[top], from_d.at[top], me).wait_recv()
            rc(a, 3, from_d.at[bottom], from_d.at[bottom], me).wait_recv()
            cp = rc(a, 6, from_d, from_d, sib)
            cp.start()
            sent.append(cp)
        for a in range(na):
            for k, jj in ((4, jx), (5, jy), (6, jd)):
                rc(a, k, p_refs[a].at[c], out_refs[a].at[jj, 1 - c], me).wait_recv()
        for cp in sent:
            cp.wait_send()

    return pl.pallas_call(
        body, name="ag_weights",
        out_shape=[jax.ShapeDtypeStruct((N_CHIPS,) + p.shape, p.dtype) for p in packs],
        in_specs=[HBM_SPEC] * na, out_specs=[HBM_SPEC] * na,
        scratch_shapes=[pltpu.SemaphoreType.DMA((nsem * na,)), pltpu.SemaphoreType.DMA((nsem * na,))])(*packs)


def _rs_pair(dwpt, gpack):
    n = N_CHIPS
    hw = SHARD_PAD // 2

    def body(d_ref, g_ref, out_d, out_g, send_sems, recv_sems):
        x, y, c, _ = _place()
        sib = (x, y, 1 - c)
        cps = []
        for p in range(n):
            start = pl.multiple_of(WIN_BASE[p] + (1 - c) * hw, TILE_ROWS)
            cps.append(pltpu.make_async_remote_copy(
                src_ref=d_ref.at[pl.ds(start, hw)], dst_ref=out_d.at[p], send_sem=send_sems.at[p],
                recv_sem=recv_sems.at[p], device_id=sib, device_id_type=MESH))
            cps.append(pltpu.make_async_remote_copy(
                src_ref=g_ref.at[p, 1 - c], dst_ref=out_g.at[p], send_sem=send_sems.at[n + p],
                recv_sem=recv_sems.at[n + p], device_id=sib, device_id_type=MESH))
        for cp in cps:
            cp.start()
        for cp in cps:
            cp.wait_recv()
        for cp in cps:
            cp.wait_send()

    return pl.pallas_call(
        body, name="rs_pair",
        out_shape=[jax.ShapeDtypeStruct((n, hw, dwpt.shape[1]), dwpt.dtype),
                   jax.ShapeDtypeStruct((n,) + gpack.shape[2:], gpack.dtype)],
        in_specs=[HBM_SPEC] * 2, out_specs=[HBM_SPEC] * 2,
        scratch_shapes=[pltpu.SemaphoreType.DMA((2 * n,)), pltpu.SemaphoreType.DMA((2 * n,))])(dwpt, gpack)


def _add_halves_win(dwpt, other, c):
    n, rh, wd = other.shape
    tr = _row_tile(rh)

    def body(s_ref, d_ref, o_ref, out_ref):
        out_ref[0] = (d_ref[...] + o_ref[0]).astype(BF16)

    scal = jnp.concatenate([jnp.reshape(c, (1,)).astype(jnp.int32), jnp.asarray(WIN_BASE, jnp.int32)])
    grid_spec = pltpu.PrefetchScalarGridSpec(
        num_scalar_prefetch=1, grid=(n, rh // tr),
        in_specs=[pl.BlockSpec((pl.Element(tr), pl.Element(wd)),
                               lambda p, i, sr: (pl.multiple_of(sr[1 + p] + sr[0] * rh + i * tr, TILE_ROWS), 0)),
                  pl.BlockSpec((1, tr, wd), lambda p, i, sr: (p, i, 0))],
        out_specs=pl.BlockSpec((1, tr, wd), lambda p, i, sr: (p, i, 0)))
    return pl.pallas_call(
        body, name="add_halves_in", grid_spec=grid_spec, out_shape=jax.ShapeDtypeStruct((n, rh, wd), BF16),
        compiler_params=_cparams("parallel", "parallel"))(scal, dwpt, other)


SEM_SPEC = pl.BlockSpec(memory_space=pltpu.SEMAPHORE)
DATAFLOW_EFFECT = pltpu.SideEffectType.DATAFLOW_SIDE_EFFECTING


def _rs_chips_start(csums):
    na = len(csums)

    def body(*refs):
        s_refs, land_refs = refs[:na], refs[na:2 * na]
        send_sems, recv_sems = refs[2 * na], refs[2 * na + 1]
        token = refs[-1]
        x, y, c, chips = _place()
        j = 2 * x + y
        for a in range(na):
            for k, (cx, cy) in enumerate(chips):
                pltpu.make_async_remote_copy(src_ref=s_refs[a].at[2 * cx + cy], dst_ref=land_refs[a].at[j],
                                             send_sem=send_sems.at[3 * a + k], recv_sem=recv_sems.at[3 * a + k],
                                             device_id=(cx, cy, c), device_id_type=MESH).start()
        token[...] = jnp.zeros_like(token)

    hbm = [pltpu.HBM(s.shape, s.dtype) for s in csums]
    args = [pltpu.with_memory_space_constraint(s, pltpu.HBM) for s in csums]
    args += [pltpu.with_memory_space_constraint(lax.empty(s.shape, s.dtype), pltpu.HBM) for s in csums]
    res = pl.pallas_call(
        body, name="rs_chips_start",
        out_shape=(pltpu.SemaphoreType.DMA((3 * na,)), pltpu.SemaphoreType.DMA((3 * na,)), *hbm, *hbm,
                   jax.ShapeDtypeStruct((8, 128), F32)),
        in_specs=[pl.BlockSpec(memory_space=pltpu.HBM)] * (2 * na),
        out_specs=(SEM_SPEC, SEM_SPEC, *[pl.BlockSpec(memory_space=pltpu.HBM)] * (2 * na),
                   pl.BlockSpec(memory_space=pltpu.VMEM)),
        input_output_aliases={i: 2 + i for i in range(2 * na)},
        compiler_params=pltpu.CompilerParams(has_side_effects=DATAFLOW_EFFECT))(*args)
    return res[0], res[1], list(res[2:2 + na]), list(res[2 + na:2 + 2 * na]), res[-1]


def _rs_chips_wait(send_sems, recv_sems, csums, lands, after):
    na = len(csums)

    def body(*refs):
        s_refs, land_refs = refs[:na], refs[na:2 * na]
        send_sems, recv_sems = refs[2 * na], refs[2 * na + 1]
        x, y, c, chips = _place()
        j = 2 * x + y
        for a in range(na):
            for k, (cx, cy) in enumerate(chips):
                cp = pltpu.make_async_remote_copy(src_ref=s_refs[a].at[2 * cx + cy], dst_ref=land_refs[a].at[2 * cx + cy],
                                                  send_sem=send_sems.at[3 * a + k], recv_sem=recv_sems.at[3 * a + k],
                                                  device_id=(cx, cy, c), device_id_type=MESH)
                cp.wait_send()
                cp.wait_recv()

    hbm = [pltpu.HBM(s.shape, s.dtype) for s in csums]
    res = pl.pallas_call(
        body, name="rs_chips_wait", out_shape=(*hbm, *hbm),
        in_specs=[pl.BlockSpec(memory_space=pltpu.HBM)] * (2 * na) + [SEM_SPEC, SEM_SPEC, pl.BlockSpec(memory_space=pl.ANY)],
        out_specs=tuple([pl.BlockSpec(memory_space=pltpu.HBM)] * (2 * na)),
        input_output_aliases={i: i for i in range(2 * na)},
        compiler_params=pltpu.CompilerParams(has_side_effects=DATAFLOW_EFFECT))(*csums, *lands, send_sems, recv_sems, after)
    return list(res[:na]), list(res[na:])


SWAP_CHUNKS = 4


def _pair_swap(halves):
    na = len(halves)

    def body(*refs):
        h_refs, out_refs = refs[:na], refs[na:2 * na]
        send_sems, recv_sems = refs[2 * na:]
        x, y, c, _ = _place()
        cps = []
        for a in range(na):
            rows = h_refs[a].shape[0] // SWAP_CHUNKS
            assert rows * SWAP_CHUNKS == h_refs[a].shape[0]
            for q in range(SWAP_CHUNKS):
                k = SWAP_CHUNKS * a + q
                cps.append(pltpu.make_async_remote_copy(
                    src_ref=h_refs[a].at[pl.ds(q * rows, rows)], dst_ref=out_refs[a].at[pl.ds(q * rows, rows)],
                    send_sem=send_sems.at[k], recv_sem=recv_sems.at[k], device_id=(x, y, 1 - c), device_id_type=MESH))
        for cp in cps:
            cp.start()
        for cp in cps:
            cp.wait_recv()
        for cp in cps:
            cp.wait_send()

    return pl.pallas_call(
        body, name="pair_swap", out_shape=[jax.ShapeDtypeStruct(h.shape, h.dtype) for h in halves],
        in_specs=[HBM_SPEC] * na, out_specs=[HBM_SPEC] * na,
        scratch_shapes=[pltpu.SemaphoreType.DMA((SWAP_CHUNKS * na,)), pltpu.SemaphoreType.DMA((SWAP_CHUNKS * na,))])(*halves)


def _ag_small(v):
    m_per, n = v.shape

    def body(x_ref, out_ref, send_sems, recv_sems, local_sem):
        x, y, c, chips = _place()
        me, sibling = (x, y, c), (x, y, 1 - c)

        def rows(px, py, pc):
            return out_ref.at[pl.ds((4 * px + 2 * py + pc) * m_per, m_per), :]

        def copy(k, block, to, src=None):
            return pltpu.make_async_remote_copy(
                src_ref=rows(*block) if src is None else src, dst_ref=rows(*block), send_sem=send_sems.at[k],
                recv_sem=recv_sems.at[k], device_id=to, device_id_type=MESH)

        mine = pltpu.make_async_copy(x_ref, rows(*me), local_sem)
        mine.start()
        first = [copy(0, me, sibling, src=x_ref)]
        first += [copy(1 + k, me, (*chip, c), src=x_ref) for k, chip in enumerate(chips)]
        for cp in first:
            cp.start()
        passed = [copy(4 + k, (*chip, c), sibling) for k, chip in enumerate(chips)]
        for k, chip in enumerate(chips):
            copy(1 + k, (*chip, c), me).wait_recv()
            passed[k].start()
        copy(0, sibling, me).wait_recv()
        for k, chip in enumerate(chips):
            copy(4 + k, (*chip, 1 - c), me).wait_recv()
        for cp in first + passed:
            cp.wait_send()
        mine.wait()

    return pl.pallas_call(
        body, name="ag_small", out_shape=jax.ShapeDtypeStruct((8 * m_per, n), v.dtype),
        in_specs=[pl.BlockSpec(memory_space=pltpu.VMEM)], out_specs=pl.BlockSpec(memory_space=pltpu.VMEM),
        scratch_shapes=[pltpu.SemaphoreType.DMA((7,)), pltpu.SemaphoreType.DMA((7,)), pltpu.SemaphoreType.DMA])(v)


def _sum_blocks(a, nblk, name):
    rows, wd = a.shape
    r = rows // nblk
    tr = min(r, ROW_TILE)
    assert r % tr == 0

    def body(*refs):
        acc = refs[0][...].astype(F32)
        for ref in refs[1:nblk]:
            acc = acc + ref[...].astype(F32)
        refs[nblk][...] = acc

    nt = r // tr
    return pl.pallas_call(
        body, name=name, grid=(nt,),
        in_specs=[pl.BlockSpec((tr, wd), functools.partial(lambda i, b: (b * nt + i, 0), b=b)) for b in range(nblk)],
        out_specs=pl.BlockSpec((tr, wd), lambda i: (i, 0)),
        out_shape=jax.ShapeDtypeStruct((r, wd), F32), compiler_params=_cparams("parallel"))(*([a] * nblk))


def _row_tile(rows):
    best = max(t for t in range(16, 513, 16) if rows % t == 0)
    return best


def _sum_chips(by_src, csum, j, name):
    n, rh, wd = by_src.shape
    tr = _row_tile(rh)

    def body(j_ref, *refs):
        own = refs[n][0].astype(F32)
        acc = None
        for k in range(n):
            term = jnp.where(j_ref[0] == k, own, refs[k][0].astype(F32))
            acc = term if acc is None else acc + term
        refs[n + 1][...] = acc

    def other(k):
        return pl.BlockSpec((1, tr, wd), lambda i, jr: (jnp.where(jr[0] == k, (k + 1) % n, k), i, 0))

    grid_spec = pltpu.PrefetchScalarGridSpec(
        num_scalar_prefetch=1, grid=(rh // tr,),
        in_specs=[other(k) for k in range(n)] + [pl.BlockSpec((1, tr, wd), lambda i, jr: (jr[0], i, 0))],
        out_specs=pl.BlockSpec((tr, wd), lambda i, jr: (i, 0)))
    return pl.pallas_call(
        body, name=name, grid_spec=grid_spec, out_shape=jax.ShapeDtypeStruct((rh, wd), F32),
        compiler_params=_cparams("parallel"))(jnp.reshape(j, (1,)).astype(jnp.int32), *([by_src] * n), csum)


def _add_halves(gpack, other, c, name):
    n, _, rh, wd = gpack.shape
    tr = _row_tile(rh)

    def body(c_ref, g_ref, o_ref, out_ref):
        out_ref[0] = (g_ref[0, 0] + o_ref[0]).astype(BF16)

    grid_spec = pltpu.PrefetchScalarGridSpec(
        num_scalar_prefetch=1, grid=(n, rh // tr),
        in_specs=[pl.BlockSpec((1, 1, tr, wd), lambda p, i, cr: (p, cr[0], i, 0)),
                  pl.BlockSpec((1, tr, wd), lambda p, i, cr: (p, i, 0))],
        out_specs=pl.BlockSpec((1, tr, wd), lambda p, i, cr: (p, i, 0)))
    return pl.pallas_call(
        body, name=name, grid_spec=grid_spec, out_shape=jax.ShapeDtypeStruct((n, rh, wd), BF16),
        compiler_params=_cparams("parallel", "parallel"))(jnp.reshape(c, (1,)).astype(jnp.int32), gpack, other)


PACK_W = 1024
ROWS_O_DN = DN_W // N_CHIPS
ROWS_O_DIL = DIL_W * (D_MODEL // N_CHIPS) // PACK_W
ROWS_OUT = D_MODEL // N_CHIPS
ROWS_CONV = 4 * (3 * DN_W // N_CHIPS) // PACK_W
R1 = ROWS_O_DN
R2 = R1 + ROWS_O_DIL
R3 = R2 + ROWS_OUT
R4 = R3 + 16
R5 = R4 + 16
PACK_ROWS = 704
HALF_ROWS = PACK_ROWS // 2
SHARD_PAD = 2880


R6 = R5 + 2 * DN_HEADS

TILE_ROWS = 16
BA_IN_SHARD1 = REF_OFF_BA - SHARD_W
LOCAL_START = (0, SHARD_W, 2 * SHARD_W - 2 * DN_HEADS, 3 * SHARD_W - 2 * DN_HEADS)
LOCAL_END = LOCAL_START[1:] + (OFF_BA,)
WIN_BASE = tuple(s // TILE_ROWS * TILE_ROWS for s in LOCAL_START)


def _to_window(k, shard):
    nba = 2 * DN_HEADS
    body = shard
    if k == 1:
        row = lax.broadcasted_iota(jnp.int32, (SHARD_W - nba, 1), 0)
        body = jnp.where(row < BA_IN_SHARD1, shard[:SHARD_W - nba], shard[nba:])
    lead = LOCAL_START[k] - WIN_BASE[k]
    return jnp.pad(body, ((lead, SHARD_PAD - lead - body.shape[0]), (0, 0)))


def _from_window(k, win, ba):
    nba = 2 * DN_HEADS
    lead = LOCAL_START[k] - WIN_BASE[k]
    if k != 1:
        return win[lead:lead + SHARD_W]
    row = lax.broadcasted_iota(jnp.int32, (SHARD_W, 1), 0)
    before = win[lead:lead + SHARD_W]
    after = jnp.pad(win, ((nba, 0), (0, 0)))[lead:lead + SHARD_W]
    mid = jnp.pad(ba, ((BA_IN_SHARD1, SHARD_W - BA_IN_SHARD1 - nba), (0, 0)))
    return jnp.where(row < BA_IN_SHARD1, before, jnp.where(row < BA_IN_SHARD1 + nba, mid, after))


def _stack_windows(wins, ba):
    pieces = []
    for k in range(N_CHIPS):
        lo = WIN_BASE[k] + (TILE_ROWS if k else 0)
        hi = LOCAL_END[k] // TILE_ROWS * TILE_ROWS
        pieces.append(wins[k][lo - WIN_BASE[k]:hi - WIN_BASE[k]])
        if k + 1 < N_CHIPS:
            assert hi == WIN_BASE[k + 1]
            pieces.append(wins[k][hi - WIN_BASE[k]:hi - WIN_BASE[k] + TILE_ROWS] + wins[k + 1][:TILE_ROWS])
    pieces += [ba, jnp.zeros((PW - OFF_BA - ba.shape[0], ba.shape[1]), ba.dtype)]
    out = jnp.concatenate(pieces, axis=0)
    assert out.shape[0] == PW
    return out


def _to_ref_layout(wpt):
    return jnp.concatenate([wpt[:REF_OFF_BA], wpt[OFF_BA:OFF_BA + 2 * DN_HEADS], wpt[REF_OFF_BA:OFF_BA]], axis=0)


def _from_ref_layout(wt):
    pad = jnp.zeros((PW - PROJ_W, wt.shape[1]), wt.dtype)
    return jnp.concatenate([wt[:REF_OFF_BA], wt[REF_OFF_BA + 2 * DN_HEADS:], wt[REF_OFF_BA:REF_OFF_BA + 2 * DN_HEADS], pad],
                           axis=0)


def _local_step(x, tgt, norm_w, wpt, conv_full, a_log, dt_bias, dn_norm_w, w_o_dn, w_o_dil, w_out, final_norm_w):
    s = x.shape[0]
    h, h_t = _rms_in(x, norm_w)
    proj = _matmul(h, wpt, F32, 2048, 1280, 1024, "proj", nt=True)
    c_pre, qkv = _conv_fwd(proj, conv_full)
    gate_par = jnp.zeros((8, 128), F32).at[0, 8:16].set(a_log[0]).at[1, 8:16].set(dt_bias[0])
    bg = _gates_fwd(proj, gate_par)
    o_a, u, w, vn, tmat, states = _gdr_fwd(qkv, bg)
    oa2, oa2_t = _gdr_out(o_a, proj, dn_norm_w)
    ya = _matmul(oa2, w_o_dn, F32, 512, 1024, 1024, "ya")
    parts = [_att_fwd(proj, g) for g in range(N_DIL)]
    ob, o_att, lse, ob_t = _att_merge(parts, proj)
    yb = _matmul(ob, w_o_dil, F32, 512, 1024, 512, "yb")
    mg, mg_t = _merge(proj, ya, yb)
    t = _matmul(mg, w_out, F32, 512, 1024, 1024, "t_out")
    dx2, dfw, lpart = _final(x, t, final_norm_w, tgt)

    dmg = _matmul(dx2, w_out, F32, 512, 1024, 1024, "d_merged", nt=True)
    dw_out = _matmul(mg_t, dx2, F32, 1024, 1024, 1024, "dw_out")
    dya, dyb, dga, dgb = _merge_bwd(proj, ya, yb, dmg)
    doa2 = _matmul(dya, w_o_dn, F32, 512, 1024, 1024, "d_oa2", nt=True)
    dw_o_dn = _matmul(oa2_t, dya, F32, 1024, 1024, 1024, "dw_o_dn")
    dob = _matmul(dyb, w_o_dil, F32, 512, 512, 1024, "d_ob", nt=True)
    dw_o_dil = _matmul(ob_t, dyb, F32, 512, 1024, 1024, "dw_o_dil")
    do_a, dproj, ddnw = _gdr_out_bwd(o_a, proj, dn_norm_w, doa2)
    dq_a, dk_a, dv_a, dbg = _gdr_bwd(qkv, bg, u, w, vn, tmat, states, do_a)
    dproj, dpar = _gates_bwd(proj, gate_par, dbg, dproj)
    dc = _conv_bwd_act(c_pre, dq_a, dk_a, dv_a)
    dproj, dconv = _conv_bwd(proj, dc, conv_full, dproj)
    do_att, delta, dproj = _att_merge_bwd(o_att, proj, dob, dproj)
    dqkv_b = [_att_bwd(proj, g, do_att, lse, delta) for g in range(N_DIL)]
    pieces = [(OFF_Q_B + (N_DIL * i + g) * DIL_W, dqkv_b[g][i]) for i in range(3) for g in range(N_DIL)]
    for off, piece in pieces + [(OFF_G_A, dga), (OFF_G_B, dgb)]:
        dproj = lax.dynamic_update_slice(dproj, piece, (0, off))
    dwpt, dwpt_b = _matmul(h_t, dproj, F32, 1024, 1280, 2048, "dw_in", transpose_out=True, also_bf16=True)

    def finish(after=None):
        dh = _matmul(dproj, wpt, F32, 1024, 1024, 3840, "d_h", after=after)
        grad_x, dnw = _rms_in_bwd(x, norm_w, dh, dx2)
        small = jnp.zeros((8, PACK_W), F32)
        small = small.at[0].set(dnw[0]).at[1].set(dfw[0]).at[2, :DN_D].set(ddnw[0])
        small = small.at[3, :DN_HEADS].set(dpar[0, 8:16]).at[3, DN_HEADS:2 * DN_HEADS].set(dpar[1, 8:16])
        small = small.at[4, 0].set(lpart[0, 0])
        return grad_x, small

    return finish, (dwpt, dwpt_b), dconv, dw_o_dn, dw_o_dil, dw_out


def kernel(x, norm_w, w_in, conv_w, a_log, dt_bias, dn_norm_w, w_o_dn, w_o_dil, w_out, final_norm_w, loss_target, m_norm_w, m_w_in, m_conv_w, m_a_log, m_dt_bias, m_dn_norm_w, m_w_o_dn, m_w_o_dil, m_w_out, m_final_norm_w, v_norm_w, v_w_in, v_conv_w, v_a_log, v_dt_bias, v_dn_norm_w, v_w_o_dn, v_w_o_dil, v_w_out, v_final_norm_w):
    c = lax.axis_index("c")
    j = 2 * lax.axis_index("x") + lax.axis_index("y")
    qw = D_MODEL // N_CHIPS

    cw = conv_w[0].reshape(ROWS_CONV, PACK_W)
    cw = jnp.pad(cw, ((0, 16 - ROWS_CONV), (0, 0)))
    cw_hi = cw.astype(BF16)
    cw_lo = (cw - cw_hi.astype(F32)).astype(BF16)
    shard = w_in[0].T.astype(BF16)
    own_ba = jnp.where(j == 1, shard[BA_IN_SHARD1:BA_IN_SHARD1 + 2 * DN_HEADS], jnp.zeros((2 * DN_HEADS, D_MODEL), BF16))
    pack = jnp.concatenate(
        [w_o_dn[0].astype(BF16), w_o_dil[0].astype(BF16).reshape(ROWS_O_DIL, PACK_W), w_out[0].astype(BF16), cw_hi, cw_lo,
         own_ba, jnp.zeros((PACK_ROWS - R6, PACK_W), BF16)], axis=0).reshape(2, HALF_ROWS, PACK_W)
    chips = range(N_CHIPS)
    own_win = lax.switch(j, [functools.partial(_to_window, k) for k in chips], shard).reshape(2, SHARD_PAD // 2, D_MODEL)
    all_in, allw = _ag_weights([own_win, pack])
    wins = [jnp.where(j == k, own_win, all_in[k]).reshape(SHARD_PAD, D_MODEL) for k in chips]
    allw = [jnp.where(j == k, pack, allw[k]).reshape(PACK_ROWS, PACK_W) for k in chips]
    wpt = _stack_windows(wins, allw[1][R5:R6])
    w_o_dn_full = jnp.concatenate([allw[k][:R1] for k in chips], axis=0)
    w_o_dil_full = jnp.concatenate([allw[k][R1:R2].reshape(DIL_W, qw) for k in chips], axis=1)
    w_out_full = jnp.concatenate([allw[k][R2:R3] for k in chips], axis=0)
    conv_full = jnp.concatenate(
        [(allw[k][R3:R3 + ROWS_CONV].astype(F32) + allw[k][R4:R4 + ROWS_CONV].astype(F32)).reshape(4, 3 * DN_W // N_CHIPS)
         for k in chips], axis=1)

    finish, (dwpt, dwpt_b), dconv, dw_o_dn, dw_o_dil, dw_out = _local_step(
        x[0], loss_target[0], norm_w, wpt, conv_full, a_log, dt_bias, dn_norm_w, w_o_dn_full, w_o_dil_full, w_out_full,
        final_norm_w.reshape(1, D_MODEL))

    cq = 3 * DN_W // N_CHIPS
    gpack = jnp.stack([
        jnp.concatenate(
            [dw_o_dn[k * qw:(k + 1) * qw], dw_o_dil[:, k * qw:(k + 1) * qw].reshape(ROWS_O_DIL, PACK_W),
             dw_out[k * qw:(k + 1) * qw],
             jnp.pad(dconv[:, k * cq:(k + 1) * cq].reshape(ROWS_CONV, PACK_W), ((0, 16 - ROWS_CONV), (0, 0))),
             dwpt[OFF_BA:OFF_BA + 2 * DN_HEADS] if k == 1 else jnp.zeros((2 * DN_HEADS, PACK_W), F32),
             jnp.zeros((PACK_ROWS - R4 - 2 * DN_HEADS, PACK_W), F32)], axis=0)
        for k in chips]).reshape(N_CHIPS, 2, HALF_ROWS, PACK_W)
    sib_in, sib_pack = _rs_pair(dwpt_b, gpack)
    csum_in = _add_halves_win(dwpt, sib_in, c)
    csum_pack = _add_halves(gpack, sib_pack, c, "add_halves_pack")
    send_sems, recv_sems, csums, lands, token = _rs_chips_start([csum_in, csum_pack])
    grad_x, small = finish(after=token)
    (csum_in, csum_pack), (src_in, src_pack) = _rs_chips_wait(send_sems, recv_sems, csums, lands, grad_x)
    half_in = _sum_chips(src_in, csum_in, j, "sum_chips_in")
    half_pack = _sum_chips(src_pack, csum_pack, j, "sum_chips_pack")
    sib_half_in, sib_half_pack = _pair_swap([half_in, half_pack])

    def both_halves(mine, theirs):
        return jnp.where(c == 0, jnp.concatenate([mine, theirs], axis=0), jnp.concatenate([theirs, mine], axis=0))

    g = both_halves(half_pack, sib_half_pack)
    g_w_in = lax.switch(j, [functools.partial(_from_window, k) for k in chips], both_halves(half_in, sib_half_in),
                        g[R4:R4 + 2 * DN_HEADS])
    g_w_o_dn = g[:R1]
    g_w_o_dil = g[R1:R2].reshape(DIL_W, qw)
    g_w_out = g[R2:R3]
    g_conv = g[R3:R3 + ROWS_CONV].reshape(4, cq)

    gs = _sum_blocks(_ag_small(small), 8, "sum_small")
    loss = gs[4, 0]
    w_small = jnp.zeros((8, PACK_W), F32)

    def pack_small(nw, fw, dnw_, al, db):
        t = w_small.at[0].set(nw[0]).at[1].set(fw).at[2, :DN_D].set(dnw_[0])
        return t.at[3, :DN_HEADS].set(al[0]).at[3, DN_HEADS:2 * DN_HEADS].set(db[0])

    sm = _adamw(pack_small(norm_w, final_norm_w, dn_norm_w, a_log, dt_bias), gs,
                pack_small(m_norm_w, m_final_norm_w, m_dn_norm_w, m_a_log, m_dt_bias),
                pack_small(v_norm_w, v_final_norm_w, v_dn_norm_w, v_a_log, v_dt_bias), "adamw_small")

    def unpack_small(t):
        return dict(norm_w=t[0:1], final_norm_w=t[1], dn_norm_w=t[2:3, :DN_D], a_log=t[3:4, :DN_HEADS],
                    dt_bias=t[3:4, DN_HEADS:2 * DN_HEADS])

    res = {"grad": unpack_small(gs)}
    for kind, arr in zip(("delta", "new_m", "new_v"), sm):
        res[kind] = unpack_small(arr)
    big = dict(conv_w=(conv_w, g_conv, m_conv_w, v_conv_w), w_o_dn=(w_o_dn, g_w_o_dn, m_w_o_dn, v_w_o_dn),
               w_o_dil=(w_o_dil, g_w_o_dil, m_w_o_dil, v_w_o_dil), w_out=(w_out, g_w_out, m_w_out, v_w_out))
    for name, (wt, gt, mt, vt) in big.items():
        d, nm, nv = _adamw(wt[0], gt, mt[0], vt[0], "adamw_" + name)
        res["grad"][name] = gt[None]
        res["delta"][name], res["new_m"][name], res["new_v"][name] = d[None], nm[None], nv[None]

    d, nm, nv = _adamw(w_in[0].T, g_w_in, m_w_in[0].T, v_w_in[0].T, "adamw_w_in")
    res["grad"]["w_in"] = g_w_in.T[None]
    res["delta"]["w_in"], res["new_m"]["w_in"], res["new_v"]["w_in"] = d.T[None], nm.T[None], nv.T[None]
    order = ["norm_w", "w_in", "conv_w", "a_log", "dt_bias", "dn_norm_w", "w_o_dn", "w_o_dil", "w_out", "final_norm_w"]
    outs = [loss, grad_x[None]]
    for kind in ("grad", "delta", "new_m", "new_v"):
        outs += [res[kind][nm] for nm in order]
    return tuple(outs)
```

```python
import functools
import math

import jax
import jax.numpy as jnp
from jax import lax
from jax.experimental import pallas as pl
from jax.experimental.pallas import tpu as pltpu

F32 = jnp.float32
BF16 = jnp.bfloat16
MESH = pl.DeviceIdType.MESH

D_MODEL = 1024
DN_HEADS = 8
DN_D = 128
DN_CHUNK = 64
DN_W = DN_HEADS * DN_D
DIL_GROUPS = ((128, 1), (512, 4), (2048, 16))
N_DIL = len(DIL_GROUPS)
DIL_HEADS = 4
DIL_DH = 128
DIL_W = DIL_HEADS * DIL_DH
ATT_BLOCK = 128
NORM_EPS = 1e-6
PROJ_W = 11280
N_CHIPS = 4
SHARD_W = PROJ_W // N_CHIPS

OFF_QKV_A = 0
OFF_Z_A = 3072
OFF_Q_B = 4096
OFF_K_B = 5632
OFF_V_B = 7168
OFF_Z_B = 8704
OFF_G_A = 9216
OFF_G_B = 10240
OFF_BA = 11264
PW = 11520
REF_OFF_BA = 4096

ADAM_LR = 0.001
ADAM_B1 = 0.9
ADAM_B2 = 0.999
ADAM_EPS = 1e-08
ADAM_WD = 0.01
ADAM_STEP = 10

ROW_TILE = 256
NEG = -1e30


def _dot(a, b):
    return jnp.dot(a.astype(BF16), b.astype(BF16), preferred_element_type=F32)


def _dot_nt(a, b):
    return lax.dot_general(a.astype(BF16), b.astype(BF16), (((1,), (1,)), ((), ())), preferred_element_type=F32)


def _dot_tn(a, b):
    return lax.dot_general(a.astype(BF16), b.astype(BF16), (((0,), (0,)), ((), ())), preferred_element_type=F32)


def _split(a):
    hi = a.astype(BF16)
    lo = (a - hi.astype(F32)).astype(BF16)
    return hi, lo


def _dot_exact_lhs(c, a):
    hi, lo = _split(a)
    cb = c.astype(BF16)
    return jnp.dot(cb, hi, preferred_element_type=F32) + jnp.dot(cb, lo, preferred_element_type=F32)


def _dot_exact_rhs(a, c):
    hi, lo = _split(a)
    cb = c.astype(BF16)
    return jnp.dot(hi, cb, preferred_element_type=F32) + jnp.dot(lo, cb, preferred_element_type=F32)


def _dot_tn_exact_rhs(a, c):
    hi, lo = _split(a)
    cb = c.astype(BF16)
    dn = (((0,), (0,)), ((), ()))
    return (lax.dot_general(hi, cb, dn, preferred_element_type=F32)
            + lax.dot_general(lo, cb, dn, preferred_element_type=F32))


def _sigmoid(x):
    return 1.0 / (1.0 + jnp.exp(-x))


def _silu(x):
    return x * _sigmoid(x)


def _silu_grad(x):
    s = _sigmoid(x)
    return s * (1.0 + x * (1.0 - s))


def _softplus(x):
    return jnp.maximum(x, 0.0) + jnp.log(1.0 + jnp.exp(-jnp.abs(x)))


def _cparams(*sem):
    return pltpu.CompilerParams(dimension_semantics=sem)


def _matmul(a, b, out_dtype, tm, tn, tk, name, nt=False, transpose_out=False, after=None, also_bf16=False):
    m, kdim = a.shape
    n = b.shape[0] if nt else b.shape[1]
    tm, tn, tk = min(tm, m), min(tn, n), min(tk, kdim)
    assert m % tm == 0 and n % tn == 0 and kdim % tk == 0, (name, a.shape, b.shape, tm, tn, tk)
    nk = kdim // tk
    dot = _dot_nt if nt else _dot
    b_spec = (pl.BlockSpec((tn, tk), lambda i, j, k: (j, k)) if nt else pl.BlockSpec((tk, tn), lambda i, j, k: (k, j)))
    extra = [] if after is None else [after]
    out_dtypes = [out_dtype] + ([BF16] if also_bf16 else [])

    def emit(o_refs, acc):
        val = acc.T if transpose_out else acc
        for o_ref in o_refs:
            o_ref[...] = val.astype(o_ref.dtype)

    def outs_of(rest):
        return rest[len(extra):len(extra) + len(out_dtypes)]

    if nk == 1:
        def body(a_ref, b_ref, *rest):
            emit(outs_of(rest), dot(a_ref[...], b_ref[...]))
        scratch = []
    else:
        def body(a_ref, b_ref, *rest):
            o_ref, acc_ref = outs_of(rest), rest[-1]
            k = pl.program_id(2)
            p = dot(a_ref[...], b_ref[...])

            @pl.when(k == 0)
            def _():
                acc_ref[...] = p

            @pl.when(k > 0)
            def _():
                acc_ref[...] += p

            @pl.when(k == nk - 1)
            def _():
                emit(o_ref, acc_ref[...])
        scratch = [pltpu.VMEM((tm, tn), F32)]

    if transpose_out:
        out_spec, out_shape = pl.BlockSpec((tn, tm), lambda i, j, k: (j, i)), (n, m)
    else:
        out_spec, out_shape = pl.BlockSpec((tm, tn), lambda i, j, k: (i, j)), (m, n)
    res = pl.pallas_call(
        body, name=name, grid=(m // tm, n // tn, nk),
        in_specs=[pl.BlockSpec((tm, tk), lambda i, j, k: (i, k)), b_spec] + [pl.BlockSpec(memory_space=pl.ANY)] * len(extra),
        out_specs=[out_spec] * len(out_dtypes), out_shape=[jax.ShapeDtypeStruct(out_shape, d) for d in out_dtypes],
        scratch_shapes=scratch, compiler_params=_cparams("parallel", "parallel", "arbitrary"))(a, b, *extra)
    return res if also_bf16 else res[0]


def _rms_in(x, nw):
    s, d = x.shape

    def body(x_ref, w_ref, h_ref, ht_ref):
        xv = x_ref[...]
        r = lax.rsqrt(jnp.mean(xv * xv, axis=-1, keepdims=True) + NORM_EPS)
        h = xv * r * w_ref[...]
        h_ref[...] = h.astype(BF16)
        ht_ref[...] = h.T.astype(BF16)

    return pl.pallas_call(
        body, name="rms_in", grid=(s // ROW_TILE,),
        in_specs=[pl.BlockSpec((ROW_TILE, d), lambda i: (i, 0)), pl.BlockSpec((1, d), lambda i: (0, 0))],
        out_specs=[pl.BlockSpec((ROW_TILE, d), lambda i: (i, 0)), pl.BlockSpec((d, ROW_TILE), lambda i: (0, i))],
        out_shape=[jax.ShapeDtypeStruct((s, d), BF16), jax.ShapeDtypeStruct((d, s), BF16)],
        compiler_params=_cparams("parallel"))(x, nw)


def _rms_in_bwd(x, nw, dh, dx2):
    s, d = x.shape

    def body(x_ref, w_ref, dh_ref, dx2_ref, dx_ref, dw_ref):
        i = pl.program_id(0)
        xv = x_ref[...]
        r = lax.rsqrt(jnp.mean(xv * xv, axis=-1, keepdims=True) + NORM_EPS)
        dhv = dh_ref[...]
        dyw = dhv * w_ref[...]
        dx_ref[...] = dx2_ref[...] + r * dyw - xv * (r * r * r) * jnp.mean(dyw * xv, axis=-1, keepdims=True)
        part = jnp.sum(dhv * xv * r, axis=0, keepdims=True)

        @pl.when(i == 0)
        def _():
            dw_ref[...] = part

        @pl.when(i > 0)
        def _():
            dw_ref[...] += part

    row = pl.BlockSpec((ROW_TILE, d), lambda i: (i, 0))
    vec = pl.BlockSpec((1, d), lambda i: (0, 0))
    return pl.pallas_call(
        body, name="rms_in_bwd", grid=(s // ROW_TILE,), in_specs=[row, vec, row, row], out_specs=[row, vec],
        out_shape=[jax.ShapeDtypeStruct((s, d), F32), jax.ShapeDtypeStruct((1, d), F32)],
        compiler_params=_cparams("arbitrary"))(x, nw, dh, dx2)


def _shift_down(cur, prev8, k):
    rc = pltpu.roll(cur, k, 0)
    rp = pltpu.roll(prev8, k, 0)
    row = lax.broadcasted_iota(jnp.int32, prev8.shape, 0)
    top = jnp.where(row < k, rp, rc[:8])
    return jnp.concatenate([top, rc[8:]], axis=0)


def _shift_up(cur, next8, k):
    t = cur.shape[0]
    rc = pltpu.roll(cur, t - k, 0)
    rn = pltpu.roll(next8, 8 - k, 0)
    row = lax.broadcasted_iota(jnp.int32, next8.shape, 0)
    bot = jnp.where(row >= 8 - k, rn, rc[t - 8:])
    return jnp.concatenate([rc[:t - 8], bot], axis=0)


def _conv_fwd(proj, conv_w):
    s = proj.shape[0]
    t8 = ROW_TILE // 8

    def body(u_ref, up_ref, w_ref, c_ref, y_ref):
        i = pl.program_id(0)
        part = pl.program_id(1)
        cur = u_ref[...]
        prev8 = jnp.where(i > 0, up_ref[...], 0.0)
        w = w_ref[...]
        c = cur * w[3:4, :]
        for k in (1, 2, 3):
            c = c + _shift_down(cur, prev8, k) * w[3 - k:4 - k, :]
        c_ref[...] = c
        a = _silu(c)
        for h in range(DN_HEADS):
            ah = a[:, h * DN_D:(h + 1) * DN_D]
            r = lax.rsqrt(jnp.sum(ah * ah, axis=-1, keepdims=True) + NORM_EPS)
            y_ref[:, h * DN_D:(h + 1) * DN_D] = jnp.where(part < 2, ah * r, ah)

    return pl.pallas_call(
        body, name="conv_fwd", grid=(s // ROW_TILE, 3),
        in_specs=[pl.BlockSpec((ROW_TILE, DN_W), lambda i, p: (i, p)),
                  pl.BlockSpec((8, DN_W), lambda i, p: (jnp.maximum(i * t8 - 1, 0), p)),
                  pl.BlockSpec((4, DN_W), lambda i, p: (0, p))],
        out_specs=[pl.BlockSpec((ROW_TILE, DN_W), lambda i, p: (i, p))] * 2,
        out_shape=[jax.ShapeDtypeStruct((s, 3 * DN_W), F32)] * 2,
        compiler_params=_cparams("parallel", "parallel"))(proj, proj, conv_w)


def _conv_bwd_act(c, dq, dk, dv):
    s = c.shape[0]

    def body(c_ref, dq_ref, dk_ref, dv_ref, dc_ref):
        for part, d_ref in enumerate((dq_ref, dk_ref, dv_ref)):
            for h in range(DN_HEADS):
                sl = slice(part * DN_W + h * DN_D, part * DN_W + (h + 1) * DN_D)
                ch = c_ref[:, sl]
                dyh = d_ref[:, h * DN_D:(h + 1) * DN_D]
                if part < 2:
                    ah = _silu(ch)
                    r = lax.rsqrt(jnp.sum(ah * ah, axis=-1, keepdims=True) + NORM_EPS)
                    dyh = r * dyh - ah * (r * r * r) * jnp.sum(dyh * ah, axis=-1, keepdims=True)
                dc_ref[:, sl] = dyh * _silu_grad(ch)

    wide = pl.BlockSpec((ROW_TILE, 3 * DN_W), lambda i: (i, 0))
    row = pl.BlockSpec((ROW_TILE, DN_W), lambda i: (i, 0))
    return pl.pallas_call(
        body, name="conv_bwd_act", grid=(s // ROW_TILE,), in_specs=[wide, row, row, row], out_specs=wide,
        out_shape=jax.ShapeDtypeStruct((s, 3 * DN_W), F32), compiler_params=_cparams("parallel"))(c, dq, dk, dv)


DPROJ_IN = pl.BlockSpec(memory_space=pl.ANY)


def _conv_bwd(proj, dc, conv_w, dproj):
    s = proj.shape[0]
    t8 = ROW_TILE // 8
    nrow = s // ROW_TILE
    last8 = s // 8 - 1

    def body(u_ref, dc_ref, dcn_ref, w_ref, dproj_in, du_ref, dw_ref):
        i = pl.program_id(1)
        cur = u_ref[...]
        dcv = dc_ref[...]
        next8 = jnp.where(i < nrow - 1, dcn_ref[...], 0.0)
        w = w_ref[...]

        @pl.when(i == 0)
        def _():
            dw_ref[...] = jnp.zeros_like(dw_ref)

        du = dcv * w[3:4, :]
        dw_ref[3:4, :] += jnp.sum(cur * dcv, axis=0, keepdims=True)
        for k in (1, 2, 3):
            ahead = _shift_up(dcv, next8, k)
            du = du + ahead * w[3 - k:4 - k, :]
            dw_ref[3 - k:4 - k, :] += jnp.sum(cur * ahead, axis=0, keepdims=True)
        du_ref[...] = du.astype(BF16)

    blk = pl.BlockSpec((ROW_TILE, DN_W), lambda p, i: (i, p))
    return pl.pallas_call(
        body, name="conv_bwd", grid=(3, nrow),
        in_specs=[blk, blk, pl.BlockSpec((8, DN_W), lambda p, i: (jnp.minimum((i + 1) * t8, last8), p)),
                  pl.BlockSpec((4, DN_W), lambda p, i: (0, p)), DPROJ_IN],
        out_specs=[blk, pl.BlockSpec((4, DN_W), lambda p, i: (0, p))],
        out_shape=[jax.ShapeDtypeStruct((s, PW), BF16), jax.ShapeDtypeStruct((4, 3 * DN_W), F32)],
        input_output_aliases={4: 0},
        compiler_params=_cparams("parallel", "arbitrary"))(proj, dc, dc, conv_w, dproj)


def _gates_fwd(proj, gate_par):
    s = proj.shape[0]

    def body(ba_ref, par_ref, o_ref):
        v = ba_ref[...]
        lane = lax.broadcasted_iota(jnp.int32, v.shape, 1)
        beta = _sigmoid(v)
        g = -jnp.exp(par_ref[0:1, :]) * _softplus(v + par_ref[1:2, :])
        o_ref[...] = jnp.where(lane < DN_HEADS, beta, jnp.where(lane < 2 * DN_HEADS, g, 0.0))

    return pl.pallas_call(
        body, name="gates_fwd", grid=(s // ROW_TILE,),
        in_specs=[pl.BlockSpec((ROW_TILE, 128), lambda i: (i, OFF_BA // 128)), pl.BlockSpec((8, 128), lambda i: (0, 0))],
        out_specs=pl.BlockSpec((ROW_TILE, 128), lambda i: (i, 0)),
        out_shape=jax.ShapeDtypeStruct((s, 128), F32), compiler_params=_cparams("parallel"))(proj, gate_par)


def _gates_bwd(proj, gate_par, dbg, dproj):
    s = proj.shape[0]

    def body(ba_ref, par_ref, d_ref, dproj_in, o_ref, dpar_ref):
        i = pl.program_id(0)
        v = ba_ref[...]
        dv = d_ref[...]
        lane = lax.broadcasted_iota(jnp.int32, v.shape, 1)
        beta = _sigmoid(v)
        nega = -jnp.exp(par_ref[0:1, :])
        xs = v + par_ref[1:2, :]
        dsp = dv * nega * _sigmoid(xs)
        dal = dv * nega * _softplus(xs)
        is_b = lane < DN_HEADS
        is_g = jnp.logical_and(lane >= DN_HEADS, lane < 2 * DN_HEADS)
        o_ref[:, :128] = jnp.where(is_b, dv * beta * (1.0 - beta), jnp.where(is_g, dsp, 0.0)).astype(BF16)
        o_ref[:, 128:] = jnp.zeros((ROW_TILE, PW - OFF_BA - 128), BF16)
        r0 = jnp.sum(jnp.where(is_g, dal, 0.0), axis=0, keepdims=True)
        r1 = jnp.sum(jnp.where(is_g, dsp, 0.0), axis=0, keepdims=True)

        @pl.when(i == 0)
        def _():
            dpar_ref[...] = jnp.zeros_like(dpar_ref)

        dpar_ref[0:1, :] += r0
        dpar_ref[1:2, :] += r1

    return pl.pallas_call(
        body, name="gates_bwd", grid=(s // ROW_TILE,),
        in_specs=[pl.BlockSpec((ROW_TILE, 128), lambda i: (i, OFF_BA // 128)), pl.BlockSpec((8, 128), lambda i: (0, 0)),
                  pl.BlockSpec((ROW_TILE, 128), lambda i: (i, 0)), DPROJ_IN],
        out_specs=[pl.BlockSpec((ROW_TILE, PW - OFF_BA), lambda i: (i, OFF_BA // (PW - OFF_BA))),
                   pl.BlockSpec((8, 128), lambda i: (0, 0))],
        out_shape=[jax.ShapeDtypeStruct((s, PW), BF16), jax.ShapeDtypeStruct((8, 128), F32)],
        input_output_aliases={3: 0},
        compiler_params=_cparams("arbitrary"))(proj, gate_par, dbg, dproj)


def _chunk_masks():
    c = DN_CHUNK
    ii = lax.broadcasted_iota(jnp.int32, (c, c), 0)
    jj = lax.broadcasted_iota(jnp.int32, (c, c), 1)
    return dict(ii=ii, jj=jj, lower=(ii >= jj), strict=(ii > jj), eye=(ii == jj),
                lower_f=(ii >= jj).astype(BF16), upper_f=(ii <= jj).astype(BF16), ones8=jnp.ones((8, c), BF16))


class _Heads:
    def __init__(self, xs):
        self.xs = list(xs)

    def _bin(self, o, f):
        if isinstance(o, _Heads):
            return _Heads([f(a, b) for a, b in zip(self.xs, o.xs)])
        return _Heads([f(a, o) for a in self.xs])

    def __add__(self, o):
        return self._bin(o, lambda a, b: a + b)

    def __sub__(self, o):
        return self._bin(o, lambda a, b: a - b)

    def __mul__(self, o):
        return self._bin(o, lambda a, b: a * b)

    __radd__ = __add__
    __rmul__ = __mul__

    def __neg__(self):
        return _Heads([-a for a in self.xs])

    def __getitem__(self, i):
        return _Heads([a[i] for a in self.xs])


def _hmap(f, *args):
    n = next(len(a.xs) for a in args if isinstance(a, _Heads))
    return _Heads([f(*[(a.xs[h] if isinstance(a, _Heads) else a) for a in args]) for h in range(n)])


def _hdot(a, b):
    return _hmap(_dot, a, b)


def _hdot_nt(a, b):
    return _hmap(_dot_nt, a, b)


def _hdot_tn(a, b):
    return _hmap(_dot_tn, a, b)


def _hcat(a, b, axis):
    return _hmap(lambda x, y: jnp.concatenate([x, y], axis=axis), a, b)


def _hsum(a, axis):
    return _hmap(lambda t: jnp.sum(t, axis=axis, keepdims=True), a)


def _hwhere(c, a, b):
    return _hmap(jnp.where, c, a, b)


def _chunk_gates(mk, bg):
    c = DN_CHUNK
    gc_all = _dot_exact_lhs(mk["lower_f"], bg)
    rows = jnp.concatenate([gc_all, gc_all], axis=0).T
    hs = range(DN_HEADS)
    return (_Heads(bg[:, h:h + 1] for h in hs), _Heads(gc_all[:, DN_HEADS + h:DN_HEADS + h + 1] for h in hs),
            _Heads(rows[DN_HEADS + h:DN_HEADS + h + 1, :] for h in hs))


def _chunk_common(mk, q, k, beta_col, gc_col, gc_r):
    c = DN_CHUNK
    lower, strict = mk["lower"], mk["strict"]
    qs = q * (DN_D ** -0.5)
    beta_b = _hmap(lambda t: jnp.broadcast_to(t, (c, DN_D)), beta_col)
    gc_b = _hmap(lambda t: jnp.broadcast_to(t, (c, DN_D)), gc_col)
    gc_sq = gc_b[:, :c]
    gam = _hwhere(lower, _hmap(lambda t: jnp.exp(jnp.minimum(t, 0.0)), gc_sq - gc_r[:, :c]), 0.0)
    egc = _hmap(jnp.exp, gc_b)
    gl = gc_b[c - 1:c, :]
    ekd = _hmap(jnp.exp, gl - gc_b)
    dl = _hmap(jnp.exp, gl)
    kb = k * beta_b
    scores = _hdot_nt(_hcat(kb, qs, 0), k)
    a_strict = _hwhere(strict, scores[:c] * gam, 0.0)
    aqk = _hwhere(lower, scores[c:] * gam, 0.0)
    return dict(k=k, qs=qs, beta_b=beta_b, gc_b=gc_b, gam=gam, egc=egc, ekd=ekd, dl=dl, kb=kb, a_strict=a_strict, aqk=aqk)


def _unit_lower_inverse_minus_eye(n_strict, ii, jj):
    same = lax.shift_right_logical(ii, 4) == lax.shift_right_logical(jj, 4)
    dmat = _hwhere(same, n_strict, 0.0)
    omat = n_strict - dmat
    d2 = _hdot(dmat, dmat)
    d4 = _hdot(d2, d2)
    d8 = _hdot(d4, d4)
    x1 = d2 - dmat - _hdot(dmat, d2)
    x2 = x1 + d4 + _hdot(x1, d4)
    x3 = x2 + d8 + _hdot(x2, d8)
    n1 = omat + _hdot(x3, omat)
    n2 = _hdot(n1, n1)
    y = n2 - n1 - _hdot(n1, n2)
    return y + x3 + _hdot(y, x3)


GDR_HEAD_SETS = (range(0, DN_HEADS),)


def _gdr_fwd(qkv, bg):
    s = qkv.shape[0]
    c = DN_CHUNK
    n = s // c

    def body(q_ref, k_ref, v_ref, bg_ref, o_ref, u_ref, w_ref, vn_ref, tm_ref, st_ref, state):
        @pl.when(pl.program_id(0) == 0)
        def _():
            state[...] = jnp.zeros_like(state)

        mk = _chunk_masks()
        gates = _chunk_gates(mk, bg_ref[...])
        for hs in GDR_HEAD_SETS:
            sls = [slice(h * DN_D, (h + 1) * DN_D) for h in hs]
            cm = _chunk_common(mk, _Heads(q_ref[:, sl] for sl in sls), _Heads(k_ref[:, sl] for sl in sls),
                               *[_Heads(g.xs[h] for h in hs) for g in gates])
            tm = _unit_lower_inverse_minus_eye(cm["a_strict"], mk["ii"], mk["jj"])
            rhs_u = _Heads(v_ref[:, sl] for sl in sls) * cm["beta_b"]
            rhs_w = cm["kb"] * cm["egc"]
            t_rhs = _hdot(tm, _hcat(rhs_u, rhs_w, 1))
            u = rhs_u + t_rhs[:, :DN_D]
            w = rhs_w + t_rhs[:, DN_D:]
            st = _Heads(state[h] for h in hs)
            on_state = _hdot(_hcat(w, cm["qs"] * cm["egc"], 0), st)
            v_new = u - on_state[:c]
            o = on_state[c:] + _hdot(cm["aqk"], v_new)
            st_new = st * cm["dl"] + _hdot_tn(cm["k"] * cm["ekd"], v_new)
            for i, (h, sl) in enumerate(zip(hs, sls)):
                o_ref[:, sl] = o.xs[i]
                u_ref[:, sl] = u.xs[i]
                w_ref[:, sl] = w.xs[i]
                vn_ref[:, sl] = v_new.xs[i]
                tm_ref[h, 0] = tm.xs[i]
                st_ref[h, 0] = st.xs[i]
                state[h] = st_new.xs[i]

    def part(p):
        return pl.BlockSpec((c, DN_W), lambda j: (j, p))

    return pl.pallas_call(
        body, name="gdr_fwd", grid=(n,),
        in_specs=[part(0), part(1), part(2), pl.BlockSpec((c, 128), lambda j: (j, 0))],
        out_specs=[part(0)] * 4 + [pl.BlockSpec((DN_HEADS, 1, c, c), lambda j: (0, j, 0, 0)),
                                   pl.BlockSpec((DN_HEADS, 1, DN_D, DN_D), lambda j: (0, j, 0, 0))],
        out_shape=[jax.ShapeDtypeStruct((s, DN_W), F32)] * 4
        + [jax.ShapeDtypeStruct((DN_HEADS, n, c, c), F32), jax.ShapeDtypeStruct((DN_HEADS, n, DN_D, DN_D), F32)],
        scratch_shapes=[pltpu.VMEM((DN_HEADS, DN_D, DN_D), F32)],
        compiler_params=_cparams("arbitrary"))(qkv, qkv, qkv, bg)


def _gdr_bwd(qkv, bg, u, w, vn, tmat, states, do):
    s = qkv.shape[0]
    c = DN_CHUNK
    n = s // c

    def body(q_ref, k_ref, v_ref, bg_ref, u_ref, w_ref, vn_ref, tm_ref, st_ref, do_ref,
             dq_ref, dk_ref, dv_ref, dbg_ref, dstate):
        @pl.when(pl.program_id(0) == 0)
        def _():
            dstate[...] = jnp.zeros_like(dstate)

        mk = _chunk_masks()
        lower, strict = mk["lower"], mk["strict"]
        bg = bg_ref[...]
        ones = jnp.ones((c, DN_D), BF16)
        rowi = lax.broadcasted_iota(jnp.int32, (c, DN_D), 0)
        lane = lax.broadcasted_iota(jnp.int32, (c, 128), 1)
        hs = range(DN_HEADS)
        sls = [slice(h * DN_D, (h + 1) * DN_D) for h in hs]

        def heads_of(ref):
            return _Heads(ref[:, sl] for sl in sls)

        cm = _chunk_common(mk, heads_of(q_ref), heads_of(k_ref), *_chunk_gates(mk, bg))
        k, qs, beta_b = cm["k"], cm["qs"], cm["beta_b"]
        gam, egc, ekd, dl, kb = cm["gam"], cm["egc"], cm["ekd"], cm["dl"], cm["kb"]
        aqk, a_strict = cm["aqk"], cm["a_strict"]
        v, uu, ww, v_new, dov = heads_of(v_ref), heads_of(u_ref), heads_of(w_ref), heads_of(vn_ref), heads_of(do_ref)
        st = _Heads(st_ref[h, 0] for h in hs)
        dsn = _Heads(dstate[h] for h in hs)
        qd = qs * egc
        kd = k * ekd

        dv_new = _hdot_tn(aqk, dov) + _hdot(kd, dsn)
        do_sv = _hdot_nt(dov, _hcat(st, v_new, 0))
        dqd = do_sv[:, :DN_D]
        daqk = _hwhere(lower, do_sv[:, DN_D:], 0.0)
        dkd = _hdot_nt(v_new, dsn)
        ddl = _hsum(_hsum(dsn * st, 1), 0)
        dw = -_hdot_nt(dv_new, st)
        ds_new = dsn * dl + _hdot_tn(_hcat(qd, -ww, 0), _hcat(dov, dv_new, 0))

        tm = _Heads(tm_ref[h, 0] for h in hs)
        tt = _hdot_tn(tm, _hcat(dv_new, dw, 1))
        dru = dv_new + tt[:, :DN_D]
        drw = dw + tt[:, DN_D:]
        dn = _hwhere(strict, -_hdot_nt(_hcat(dru, drw, 1), _hcat(uu, ww, 1)), 0.0)
        dag = dn * gam
        dqg = daqk * gam
        both = _hcat(dag, dqg, 0)
        on_k = _hdot(both, k)
        dkb = on_k[:c] + drw * egc
        dqs = on_k[c:] + dqd * egc
        dk = _hdot_tn(both, _hcat(kb, qs, 0)) + dkb * beta_b + dkd * ekd
        pmat = dn * a_strict + daqk * aqk
        tkd = _hsum(dkd * kd, -1)
        dgc = (_hsum(pmat, -1) - _hmap(_dot_tn_exact_rhs, pmat, ones) + _hsum(drw * (kb * egc), -1)
               + _hsum(dqd * qd, -1) - tkd)
        last = _hsum(tkd, 0) + ddl * dl
        dgc = dgc + _hwhere(rowi == c - 1, last, 0.0)
        dbeta = _hsum(dru * v, -1) + _hsum(dkb * k, -1)
        dq = dqs * (DN_D ** -0.5)
        dv = dru * beta_b

        dgc_all = jnp.zeros((c, 128), F32)
        dbg = jnp.zeros((c, 128), F32)
        for h, sl in zip(hs, sls):
            dq_ref[:, sl] = dq.xs[h]
            dk_ref[:, sl] = dk.xs[h]
            dv_ref[:, sl] = dv.xs[h]
            dstate[h] = ds_new.xs[h]
            dgc_all = dgc_all + jnp.where(lane == DN_HEADS + h, dgc.xs[h], 0.0)
            dbg = dbg + jnp.where(lane == h, dbeta.xs[h], 0.0)
        dbg_ref[...] = dbg + _dot_exact_lhs(mk["upper_f"], dgc_all)

    def part(p):
        return pl.BlockSpec((c, DN_W), lambda j: (n - 1 - j, p))

    vec = pl.BlockSpec((c, 128), lambda j: (n - 1 - j, 0))
    return pl.pallas_call(
        body, name="gdr_bwd", grid=(n,),
        in_specs=[part(0), part(1), part(2), vec, part(0), part(0), part(0),
                  pl.BlockSpec((DN_HEADS, 1, c, c), lambda j: (0, n - 1 - j, 0, 0)),
                  pl.BlockSpec((DN_HEADS, 1, DN_D, DN_D), lambda j: (0, n - 1 - j, 0, 0)), part(0)],
        out_specs=[part(0), part(0), part(0), vec],
        out_shape=[jax.ShapeDtypeStruct((s, DN_W), F32)] * 3 + [jax.ShapeDtypeStruct((s, 128), F32)],
        scratch_shapes=[pltpu.VMEM((DN_HEADS, DN_D, DN_D), F32)],
        compiler_params=_cparams("arbitrary"))(qkv, qkv, qkv, bg, u, w, vn, tmat, states, do)


def _gdr_out(o, proj, dnw):
    s = o.shape[0]

    def body(o_ref, z_ref, w_ref, y_ref, yt_ref):
        ov, zv, wv = o_ref[...], z_ref[...], w_ref[...]
        for h in range(DN_HEADS):
            sl = slice(h * DN_D, (h + 1) * DN_D)
            oh = ov[:, sl]
            r = lax.rsqrt(jnp.mean(oh * oh, axis=-1, keepdims=True) + NORM_EPS)
            y = (oh * r * wv) * _silu(zv[:, sl])
            y_ref[:, sl] = y.astype(BF16)
            yt_ref[sl, :] = y.T.astype(BF16)

    row = pl.BlockSpec((ROW_TILE, DN_W), lambda i: (i, 0))
    return pl.pallas_call(
        body, name="gdr_out", grid=(s // ROW_TILE,),
        in_specs=[row, pl.BlockSpec((ROW_TILE, DN_W), lambda i: (i, OFF_Z_A // DN_W)), pl.BlockSpec((1, DN_D), lambda i: (0, 0))],
        out_specs=[row, pl.BlockSpec((DN_W, ROW_TILE), lambda i: (0, i))],
        out_shape=[jax.ShapeDtypeStruct((s, DN_W), BF16), jax.ShapeDtypeStruct((DN_W, s), BF16)],
        compiler_params=_cparams("parallel"))(o, proj, dnw)


def _gdr_out_bwd(o, proj, dnw, dy):
    s = o.shape[0]

    def body(o_ref, z_ref, w_ref, dy_ref, do_ref, dz_ref, dw_ref):
        i = pl.program_id(0)
        ov, zv, wv, dyv = o_ref[...], z_ref[...], w_ref[...], dy_ref[...]
        acc = jnp.zeros((1, DN_D), F32)
        for h in range(DN_HEADS):
            sl = slice(h * DN_D, (h + 1) * DN_D)
            oh, zh, dh = ov[:, sl], zv[:, sl], dyv[:, sl]
            r = lax.rsqrt(jnp.mean(oh * oh, axis=-1, keepdims=True) + NORM_EPS)
            dn = dh * _silu(zh)
            dz_ref[:, sl] = (dh * (oh * r * wv) * _silu_grad(zh)).astype(BF16)
            acc = acc + jnp.sum(dn * oh * r, axis=0, keepdims=True)
            dnw_ = dn * wv
            do_ref[:, sl] = r * dnw_ - oh * (r * r * r) * jnp.mean(dnw_ * oh, axis=-1, keepdims=True)

        @pl.when(i == 0)
        def _():
            dw_ref[...] = acc

        @pl.when(i > 0)
        def _():
            dw_ref[...] += acc

    row = pl.BlockSpec((ROW_TILE, DN_W), lambda i: (i, 0))
    vec = pl.BlockSpec((1, DN_D), lambda i: (0, 0))
    return pl.pallas_call(
        body, name="gdr_out_bwd", grid=(s // ROW_TILE,),
        in_specs=[row, pl.BlockSpec((ROW_TILE, DN_W), lambda i: (i, OFF_Z_A // DN_W)), vec, row],
        out_specs=[row, pl.BlockSpec((ROW_TILE, DN_W), lambda i: (i, OFF_Z_A // DN_W)), vec],
        out_shape=[jax.ShapeDtypeStruct((s, DN_W), F32), jax.ShapeDtypeStruct((s, PW), BF16),
                   jax.ShapeDtypeStruct((1, DN_D), F32)],
        compiler_params=_cparams("arbitrary"))(o, proj, dnw, dy)


def _slope(group, head):
    idx = (group * DIL_HEADS + head + 1).astype(F32)
    return jnp.exp(jnp.full((1, 128), -8.0 * math.log(2.0) / (N_DIL * DIL_HEADS), F32) * idx)


def _att_scores(qb, k_cur, k_prev, slope_d, has_prev):
    iq = lax.broadcasted_iota(jnp.int32, (ATT_BLOCK, ATT_BLOCK), 0)
    jk = lax.broadcasted_iota(jnp.int32, (ATT_BLOCK, ATT_BLOCK), 1)
    dist_c = (iq - jk).astype(F32)
    s_cur = jnp.where(iq >= jk, _dot_nt(qb, k_cur) - slope_d * dist_c, NEG)
    s_prev = jnp.where(jnp.logical_and(jk >= iq, has_prev),
                       _dot_nt(qb, k_prev) - slope_d * (dist_c + float(ATT_BLOCK)), NEG)
    return s_cur, s_prev


def _att_scores_whole(qb, k, slope_d):
    n = 2 * ATT_BLOCK
    dist = lax.broadcasted_iota(jnp.int32, (n, n), 0) - lax.broadcasted_iota(jnp.int32, (n, n), 1)
    valid = jnp.logical_and(dist >= 0, dist <= ATT_BLOCK)
    return jnp.where(valid, _dot_nt(qb, k) - slope_d[:, 0:1] * dist.astype(F32), NEG)


def _att_tiles(i, dil, nb):
    tiles = nb // 2
    per = dil * tiles // ATT_UNROLL
    assert nb % 2 == 0 and tiles >= 2 and per * ATT_UNROLL == dil * tiles
    for i0 in range(per):
        ts = [divmod(i0 + u * per, tiles) for u in range(ATT_UNROLL)]
        assert all(a[0] != b[0] or abs(a[1] - b[1]) >= 2 for n, a in enumerate(ts) for b in ts[n + 1:])
    qrows, krows, has_prev = [], [], []
    for u in range(ATT_UNROLL):
        t = i + u * per
        r = lax.div(t, tiles)
        j = lax.rem(t, tiles)
        qbase = r + dil * 2 * ATT_BLOCK * j
        kbase = qbase - dil * ATT_BLOCK * jnp.minimum(j, 1)
        if dil == 1:
            qbase, kbase = pl.multiple_of(qbase, ATT_BLOCK), pl.multiple_of(kbase, ATT_BLOCK)
        qrows.append(pl.ds(qbase, 2 * ATT_BLOCK, stride=dil))
        krows.append(pl.ds(kbase, 3 * ATT_BLOCK, stride=dil))
        has_prev.append(j > 0)
    return qrows, krows, has_prev


def _att_scores_tile(qb, k, slope_d, has_prev):
    iq = lax.broadcasted_iota(jnp.int32, (2 * ATT_BLOCK, 3 * ATT_BLOCK), 0)
    ck = lax.broadcasted_iota(jnp.int32, (2 * ATT_BLOCK, 3 * ATT_BLOCK), 1)
    dist = iq - ck + jnp.where(has_prev, ATT_BLOCK, 0)
    valid = jnp.logical_and(dist >= 0, dist <= ATT_BLOCK)
    return jnp.where(valid, _dot_nt(qb, k) - slope_d[:, 0:1] * dist.astype(F32), NEG)


ATT_UNROLL = 4


def _att_blocks(i, dil, nb):
    per = dil * nb // ATT_UNROLL
    assert per * ATT_UNROLL == dil * nb
    for i0 in range(per):
        blocks = [divmod(i0 + u * per, nb) for u in range(ATT_UNROLL)]
        assert all(a[0] != b[0] or abs(a[1] - b[1]) >= 2 for n, a in enumerate(blocks) for b in blocks[n + 1:])
    curs, prvs, has_prev = [], [], []
    for u in range(ATT_UNROLL):
        t = i + u * per
        r = lax.div(t, nb)
        j = lax.rem(t, nb)
        base = r + dil * ATT_BLOCK * j
        pbase = base - dil * ATT_BLOCK * jnp.minimum(j, 1)
        if dil == 1:
            base, pbase = pl.multiple_of(base, ATT_BLOCK), pl.multiple_of(pbase, ATT_BLOCK)
        curs.append(pl.ds(base, ATT_BLOCK, stride=dil))
        prvs.append(pl.ds(pbase, ATT_BLOCK, stride=dil))
        has_prev.append(j > 0)
    return curs, prvs, has_prev


def _att_fwd(proj, group):
    s = proj.shape[0]
    dil = DIL_GROUPS[group][1]
    assert DIL_GROUPS[group][0] // dil == ATT_BLOCK
    nb = s // dil // ATT_BLOCK
    assert nb * dil * ATT_BLOCK == s

    def body(q_ref, k_ref, v_ref, num_ref, den_ref, mx_ref):
        slope_d = _slope(group, pl.program_id(0)) * float(dil)

        def step(i, carry):
            curs, prvs, has_prev = _att_blocks(i, dil, nb)
            us = range(ATT_UNROLL)
            qb = [q_ref[c, :] * (DIL_DH ** -0.5) for c in curs]
            sc = [_att_scores(qb[u], k_ref[curs[u], :], k_ref[prvs[u], :], slope_d, has_prev[u]) for u in us]
            mx = [jnp.maximum(jnp.max(a, axis=-1, keepdims=True), jnp.max(b, axis=-1, keepdims=True)) for a, b in sc]
            p_cur = [jnp.exp(sc[u][0] - mx[u]) for u in us]
            p_prev = [jnp.exp(sc[u][1] - mx[u]) for u in us]
            den = [jnp.sum(p_cur[u], axis=-1, keepdims=True) + jnp.sum(p_prev[u], axis=-1, keepdims=True) for u in us]
            num = [_dot(p_cur[u], v_ref[curs[u], :]) + _dot(p_prev[u], v_ref[prvs[u], :]) for u in us]
            for u in us:
                num_ref[curs[u], :] = num[u]
                den_ref[curs[u], :] = jnp.broadcast_to(den[u], (ATT_BLOCK, DIL_DH))
                mx_ref[curs[u], :] = jnp.broadcast_to(mx[u], (ATT_BLOCK, DIL_DH))
            return carry

        def step_whole(i, carry):
            rows = [pl.ds(i * ATT_UNROLL + u, 2 * ATT_BLOCK, stride=dil) for u in range(ATT_UNROLL)]
            sc = [_att_scores_whole(q_ref[r, :] * (DIL_DH ** -0.5), k_ref[r, :], slope_d) for r in rows]
            mx = [jnp.max(a, axis=-1, keepdims=True) for a in sc]
            p = [jnp.exp(a - m) for a, m in zip(sc, mx)]
            num = [_dot(pu, v_ref[r, :]) for pu, r in zip(p, rows)]
            for u, r in enumerate(rows):
                num_ref[r, :] = num[u]
                den_ref[r, :] = jnp.broadcast_to(jnp.sum(p[u], axis=-1, keepdims=True), (2 * ATT_BLOCK, DIL_DH))
                mx_ref[r, :] = jnp.broadcast_to(mx[u], (2 * ATT_BLOCK, DIL_DH))
            return carry

        def step_tile(i, carry):
            qrows, krows, has_prev = _att_tiles(i, dil, nb)
            us = range(ATT_UNROLL)
            sc = [_att_scores_tile(q_ref[qrows[u], :] * (DIL_DH ** -0.5), k_ref[krows[u], :], slope_d, has_prev[u]) for u in us]
            mx = [jnp.max(a, axis=-1, keepdims=True) for a in sc]
            p = [jnp.exp(a - m) for a, m in zip(sc, mx)]
            num = [_dot(p[u], v_ref[krows[u], :]) for u in us]
            for u in us:
                num_ref[qrows[u], :] = num[u]
                den_ref[qrows[u], :] = jnp.broadcast_to(jnp.sum(p[u], axis=-1, keepdims=True), (2 * ATT_BLOCK, DIL_DH))
                mx_ref[qrows[u], :] = jnp.broadcast_to(mx[u], (2 * ATT_BLOCK, DIL_DH))
            return carry

        if nb == 2:
            lax.fori_loop(0, dil // ATT_UNROLL, step_whole, 0)
        elif nb % 2 == 0:
            lax.fori_loop(0, dil * nb // 2 // ATT_UNROLL, step_tile, 0)
        else:
            lax.fori_loop(0, dil * nb // ATT_UNROLL, step, 0)

    def col(off):
        return pl.BlockSpec((s, DIL_DH), lambda h: (0, off // DIL_DH + group * DIL_HEADS + h))

    out = pl.BlockSpec((s, DIL_DH), lambda h: (0, h))
    return pl.pallas_call(
        body, name=f"att_fwd{group}", grid=(DIL_HEADS,), in_specs=[col(OFF_Q_B), col(OFF_K_B), col(OFF_V_B)],
        out_specs=[out, out, out], out_shape=[jax.ShapeDtypeStruct((s, DIL_W), F32)] * 3,
        compiler_params=_cparams("parallel"))(proj, proj, proj)


def _att_bwd(proj, group, do, lse, delta):
    s = proj.shape[0]
    dil = DIL_GROUPS[group][1]
    nb = s // dil // ATT_BLOCK

    def body(q_ref, k_ref, v_ref, do_ref, lse_ref, dl_ref, dq_ref, dk_ref, dv_ref, dq_acc, dk_acc, dv_acc):
        slope_d = _slope(group, pl.program_id(0)) * float(dil)
        dk_acc[...] = jnp.zeros_like(dk_acc)
        dv_acc[...] = jnp.zeros_like(dv_acc)

        def step(i, carry):
            curs, prvs, has_prev = _att_blocks(i, dil, nb)
            us = range(ATT_UNROLL)
            qb = [q_ref[c, :] * (DIL_DH ** -0.5) for c in curs]
            k_cur, k_prev = [k_ref[c, :] for c in curs], [k_ref[p, :] for p in prvs]
            v_cur, v_prev = [v_ref[c, :] for c in curs], [v_ref[p, :] for p in prvs]
            sc = [_att_scores(qb[u], k_cur[u], k_prev[u], slope_d, has_prev[u]) for u in us]
            lse_b, delta_b, dob = [lse_ref[c, :] for c in curs], [dl_ref[c, :] for c in curs], [do_ref[c, :] for c in curs]
            p_cur = [jnp.exp(sc[u][0] - lse_b[u]) for u in us]
            p_prev = [jnp.exp(sc[u][1] - lse_b[u]) for u in us]
            ds_cur = [p_cur[u] * (_dot_nt(dob[u], v_cur[u]) - delta_b[u]) for u in us]
            ds_prev = [p_prev[u] * (_dot_nt(dob[u], v_prev[u]) - delta_b[u]) for u in us]
            dq = [(_dot(ds_cur[u], k_cur[u]) + _dot(ds_prev[u], k_prev[u])) * (DIL_DH ** -0.5) for u in us]
            dk_c = [_dot_tn(ds_cur[u], qb[u]) for u in us]
            dv_c = [_dot_tn(p_cur[u], dob[u]) for u in us]
            dk_p = [_dot_tn(ds_prev[u], qb[u]) for u in us]
            dv_p = [_dot_tn(p_prev[u], dob[u]) for u in us]
            for u in us:
                dq_acc[curs[u], :] = dq[u]
                dk_acc[curs[u], :] += dk_c[u]
                dv_acc[curs[u], :] += dv_c[u]
            for u in us:
                dk_acc[prvs[u], :] += dk_p[u]
                dv_acc[prvs[u], :] += dv_p[u]
            return carry

        def step_whole(i, carry):
            rows = [pl.ds(i * ATT_UNROLL + u, 2 * ATT_BLOCK, stride=dil) for u in range(ATT_UNROLL)]
            qb = [q_ref[r, :] * (DIL_DH ** -0.5) for r in rows]
            kk, vv, dob = [k_ref[r, :] for r in rows], [v_ref[r, :] for r in rows], [do_ref[r, :] for r in rows]
            sc = [_att_scores_whole(qb[u], kk[u], slope_d) for u in range(ATT_UNROLL)]
            p = [jnp.exp(sc[u] - lse_ref[r, :][:, 0:1]) for u, r in enumerate(rows)]
            ds = [p[u] * (_dot_nt(dob[u], vv[u]) - dl_ref[r, :][:, 0:1]) for u, r in enumerate(rows)]
            dq = [_dot(ds[u], kk[u]) * (DIL_DH ** -0.5) for u in range(ATT_UNROLL)]
            dk = [_dot_tn(ds[u], qb[u]) for u in range(ATT_UNROLL)]
            dv = [_dot_tn(p[u], dob[u]) for u in range(ATT_UNROLL)]
            for u, r in enumerate(rows):
                dq_acc[r, :] = dq[u]
                dk_acc[r, :] = dk[u]
                dv_acc[r, :] = dv[u]
            return carry

        def step_tile(i, carry):
            qrows, krows, has_prev = _att_tiles(i, dil, nb)
            us = range(ATT_UNROLL)
            qb = [q_ref[r, :] * (DIL_DH ** -0.5) for r in qrows]
            kk, vv, dob = [k_ref[r, :] for r in krows], [v_ref[r, :] for r in krows], [do_ref[r, :] for r in qrows]
            sc = [_att_scores_tile(qb[u], kk[u], slope_d, has_prev[u]) for u in us]
            p = [jnp.exp(sc[u] - lse_ref[qrows[u], :][:, 0:1]) for u in us]
            ds = [p[u] * (_dot_nt(dob[u], vv[u]) - dl_ref[qrows[u], :][:, 0:1]) for u in us]
            dq = [_dot(ds[u], kk[u]) * (DIL_DH ** -0.5) for u in us]
            dk = [_dot_tn(ds[u], qb[u]) for u in us]
            dv = [_dot_tn(p[u], dob[u]) for u in us]
            for u in us:
                dq_acc[qrows[u], :] = dq[u]
                dk_acc[krows[u], :] += dk[u]
                dv_acc[krows[u], :] += dv[u]
            return carry

        if nb == 2:
            lax.fori_loop(0, dil // ATT_UNROLL, step_whole, 0)
        elif nb % 2 == 0:
            lax.fori_loop(0, dil * nb // 2 // ATT_UNROLL, step_tile, 0)
        else:
            lax.fori_loop(0, dil * nb // ATT_UNROLL, step, 0)
        dq_ref[...] = dq_acc[...].astype(BF16)
        dk_ref[...] = dk_acc[...].astype(BF16)
        dv_ref[...] = dv_acc[...].astype(BF16)

    def col(off):
        return pl.BlockSpec((s, DIL_DH), lambda h: (0, off // DIL_DH + group * DIL_HEADS + h))

    hd = pl.BlockSpec((s, DIL_DH), lambda h: (0, h))
    return pl.pallas_call(
        body, name=f"att_bwd{group}", grid=(DIL_HEADS,),
        in_specs=[col(OFF_Q_B), col(OFF_K_B), col(OFF_V_B), hd, hd, hd], out_specs=[hd, hd, hd],
        out_shape=[jax.ShapeDtypeStruct((s, DIL_W), BF16)] * 3,
        scratch_shapes=[pltpu.VMEM((s, DIL_DH), F32)] * 3,
        compiler_params=_cparams("parallel"))(proj, proj, proj, do, lse, delta)


def _att_merge(parts, proj):
    s = proj.shape[0]

    def body(n0, d0, m0, n1, d1, m1, n2, d2, m2, z_ref, ob_ref, o_ref, lse_ref, obt_ref):
        m = jnp.maximum(jnp.maximum(m0[...], m1[...]), m2[...])
        num = jnp.zeros_like(m)
        den = jnp.zeros_like(m)
        for nr, dr, mr in ((n0, d0, m0), (n1, d1, m1), (n2, d2, m2)):
            sc = jnp.exp(mr[...] - m)
            num = num + nr[...] * sc
            den = den + dr[...] * sc
        o = num / den
        o_ref[...] = o
        lse_ref[...] = m + jnp.log(den)
        ob = o * _silu(z_ref[...])
        ob_ref[...] = ob.astype(BF16)
        obt_ref[...] = ob.T.astype(BF16)

    row = pl.BlockSpec((ROW_TILE, DIL_W), lambda i: (i, 0))
    flat = [a for p in parts for a in p]
    return pl.pallas_call(
        body, name="att_merge", grid=(s // ROW_TILE,),
        in_specs=[row] * 9 + [pl.BlockSpec((ROW_TILE, DIL_W), lambda i: (i, OFF_Z_B // DIL_W))],
        out_specs=[row, row, row, pl.BlockSpec((DIL_W, ROW_TILE), lambda i: (0, i))],
        out_shape=[jax.ShapeDtypeStruct((s, DIL_W), BF16), jax.ShapeDtypeStruct((s, DIL_W), F32),
                   jax.ShapeDtypeStruct((s, DIL_W), F32), jax.ShapeDtypeStruct((DIL_W, s), BF16)],
        compiler_params=_cparams("parallel"))(*flat, proj)


def _att_merge_bwd(o, proj, dob, dproj):
    s = o.shape[0]

    def body(o_ref, z_ref, d_ref, dproj_in, do_ref, dl_ref, dz_ref):
        ov, zv, dv = o_ref[...], z_ref[...], d_ref[...]
        do = dv * _silu(zv)
        do_ref[...] = do
        dz_ref[...] = (dv * ov * _silu_grad(zv)).astype(BF16)
        for h in range(DIL_HEADS):
            sl = slice(h * DIL_DH, (h + 1) * DIL_DH)
            dl_ref[:, sl] = jnp.broadcast_to(jnp.sum(do[:, sl] * ov[:, sl], axis=-1, keepdims=True), (ROW_TILE, DIL_DH))

    row = pl.BlockSpec((ROW_TILE, DIL_W), lambda i: (i, 0))
    return pl.pallas_call(
        body, name="att_merge_bwd", grid=(s // ROW_TILE,),
        in_specs=[row, pl.BlockSpec((ROW_TILE, DIL_W), lambda i: (i, OFF_Z_B // DIL_W)), row, DPROJ_IN],
        out_specs=[row, row, pl.BlockSpec((ROW_TILE, DIL_W), lambda i: (i, OFF_Z_B // DIL_W))],
        out_shape=[jax.ShapeDtypeStruct((s, DIL_W), F32), jax.ShapeDtypeStruct((s, DIL_W), F32),
                   jax.ShapeDtypeStruct((s, PW), BF16)],
        input_output_aliases={3: 2},
        compiler_params=_cparams("parallel"))(o, proj, dob, dproj)


def _merge(proj, ya, yb):
    s = proj.shape[0]

    def body(ga_ref, gb_ref, ya_ref, yb_ref, o_ref, ot_ref):
        m = _sigmoid(ga_ref[...]) * ya_ref[...] + _sigmoid(gb_ref[...]) * yb_ref[...]
        o_ref[...] = m.astype(BF16)
        ot_ref[...] = m.T.astype(BF16)

    row = pl.BlockSpec((ROW_TILE, D_MODEL), lambda i: (i, 0))
    return pl.pallas_call(
        body, name="merge", grid=(s // ROW_TILE,),
        in_specs=[pl.BlockSpec((ROW_TILE, D_MODEL), lambda i: (i, OFF_G_A // D_MODEL)),
                  pl.BlockSpec((ROW_TILE, D_MODEL), lambda i: (i, OFF_G_B // D_MODEL)), row, row],
        out_specs=[row, pl.BlockSpec((D_MODEL, ROW_TILE), lambda i: (0, i))],
        out_shape=[jax.ShapeDtypeStruct((s, D_MODEL), BF16), jax.ShapeDtypeStruct((D_MODEL, s), BF16)],
        compiler_params=_cparams("parallel"))(proj, proj, ya, yb)


def _merge_bwd(proj, ya, yb, dm):
    s = proj.shape[0]

    def body(ga_ref, gb_ref, ya_ref, yb_ref, dm_ref, dya_ref, dyb_ref, dga_ref, dgb_ref):
        dmv = dm_ref[...]
        sa, sb = _sigmoid(ga_ref[...]), _sigmoid(gb_ref[...])
        dya_ref[...] = (dmv * sa).astype(BF16)
        dyb_ref[...] = (dmv * sb).astype(BF16)
        dga_ref[...] = (dmv * ya_ref[...] * sa * (1.0 - sa)).astype(BF16)
        dgb_ref[...] = (dmv * yb_ref[...] * sb * (1.0 - sb)).astype(BF16)

    row = pl.BlockSpec((ROW_TILE, D_MODEL), lambda i: (i, 0))
    return pl.pallas_call(
        body, name="merge_bwd", grid=(s // ROW_TILE,),
        in_specs=[pl.BlockSpec((ROW_TILE, D_MODEL), lambda i: (i, OFF_G_A // D_MODEL)),
                  pl.BlockSpec((ROW_TILE, D_MODEL), lambda i: (i, OFF_G_B // D_MODEL)), row, row, row],
        out_specs=[row] * 4, out_shape=[jax.ShapeDtypeStruct((s, D_MODEL), BF16)] * 4,
        compiler_params=_cparams("parallel"))(proj, proj, ya, yb, dm)


def _final(x, t, fw, tgt):
    s, d = x.shape

    def body(x_ref, t_ref, w_ref, y_ref, dx_ref, dw_ref, l_ref):
        i = pl.program_id(0)
        x2 = x_ref[...] + t_ref[...]
        wv = w_ref[...]
        r = lax.rsqrt(jnp.mean(x2 * x2, axis=-1, keepdims=True) + NORM_EPS)
        e = x2 * r * wv - y_ref[...]
        lrow = jnp.mean(e * e, axis=-1, keepdims=True)
        lpart = jnp.broadcast_to(0.5 * jnp.sum(lrow, axis=0, keepdims=True), (1, 128))
        dy = e * (1.0 / d)
        dwp = jnp.sum(dy * x2 * r, axis=0, keepdims=True)
        dyw = dy * wv
        dx_ref[...] = r * dyw - x2 * (r * r * r) * jnp.mean(dyw * x2, axis=-1, keepdims=True)

        @pl.when(i == 0)
        def _():
            dw_ref[...] = dwp
            l_ref[...] = lpart

        @pl.when(i > 0)
        def _():
            dw_ref[...] += dwp
            l_ref[...] += lpart

    row = pl.BlockSpec((ROW_TILE, d), lambda i: (i, 0))
    vec = pl.BlockSpec((1, d), lambda i: (0, 0))
    return pl.pallas_call(
        body, name="final", grid=(s // ROW_TILE,), in_specs=[row, row, vec, row],
        out_specs=[row, vec, pl.BlockSpec((1, 128), lambda i: (0, 0))],
        out_shape=[jax.ShapeDtypeStruct((s, d), F32), jax.ShapeDtypeStruct((1, d), F32), jax.ShapeDtypeStruct((1, 128), F32)],
        compiler_params=_cparams("arbitrary"))(x, t, fw, tgt)


def _adamw(w, g, m, v, name):
    r, c = w.shape
    cap = max(8, (1 << 18) // c)
    divisors = [t for t in range(8, min(r, cap) + 1, 8) if r % t == 0]
    tr = r if r <= 8 else (max(divisors) if divisors else cap)

    def body(w_ref, g_ref, m_ref, v_ref, d_ref, nm_ref, nv_ref):
        gv = g_ref[...]
        mn = ADAM_B1 * m_ref[...] + (1.0 - ADAM_B1) * gv
        vn = ADAM_B2 * v_ref[...] + (1.0 - ADAM_B2) * (gv * gv)
        m_hat = mn / (1.0 - ADAM_B1 ** ADAM_STEP)
        v_hat = vn / (1.0 - ADAM_B2 ** ADAM_STEP)
        d_ref[...] = -ADAM_LR * (m_hat / (jnp.sqrt(v_hat) + ADAM_EPS) + ADAM_WD * w_ref[...])
        nm_ref[...] = mn
        nv_ref[...] = vn

    blk = pl.BlockSpec((tr, c), lambda i: (i, 0))
    return pl.pallas_call(
        body, name=name, grid=(pl.cdiv(r, tr),), in_specs=[blk] * 4, out_specs=[blk] * 3,
        out_shape=[jax.ShapeDtypeStruct((r, c), F32)] * 3, compiler_params=_cparams("parallel"))(w, g, m, v)


HBM_SPEC = pl.BlockSpec(memory_space=pl.ANY)


def _place():
    x, y, c = lax.axis_index("x"), lax.axis_index("y"), lax.axis_index("c")
    chips = [(1 - x, y), (x, 1 - y), (1 - x, 1 - y)]
    return x, y, c, chips


def _ag_weights(packs):
    na = len(packs)
    nsem = 8

    def body(*refs):
        p_refs, out_refs = refs[:na], refs[na:2 * na]
        send_sems, recv_sems = refs[2 * na:]
        x, y, c, _ = _place()
        me, sib, j = (x, y, c), (x, y, 1 - c), 2 * x + y
        xn, yn = (1 - x, y, c), (x, 1 - y, c)
        jx, jy, jd = 2 * (1 - x) + y, 2 * x + (1 - y), 2 * (1 - x) + (1 - y)

        def rc(a, k, src, dst, to):
            return pltpu.make_async_remote_copy(src_ref=src, dst_ref=dst, send_sem=send_sems.at[nsem * a + k],
                                                recv_sem=recv_sems.at[nsem * a + k], device_id=to, device_id_type=MESH)

        sent = []
        for a in range(na):
            mine, land = p_refs[a].at[c], out_refs[a].at[j, c]
            sent += [rc(a, 0, mine, land, xn), rc(a, 1, mine, land, yn), rc(a, 7, p_refs[a], out_refs[a].at[j], sib)]
        for cp in sent:
            cp.start()
        for a in range(na):
            half = p_refs[a].shape[1] // 2
            top, bottom = pl.ds(0, half), pl.ds(half, half)
            from_x, from_y, from_d = out_refs[a].at[jx, c], out_refs[a].at[jy, c], out_refs[a].at[jd, c]
            rc(a, 0, p_refs[a].at[c], from_x, me).wait_recv()
            later = [rc(a, 2, from_x.at[top], from_x.at[top], yn), rc(a, 4, from_x, from_x, sib)]
            for cp in later:
                cp.start()
            sent += later
            rc(a, 1, p_refs[a].at[c], from_y, me).wait_recv()
            later = [rc(a, 3, from_y.at[bottom], from_y.at[bottom], xn), rc(a, 5, from_y, from_y, sib)]
            for cp in later:
                cp.start()
            sent += later
            rc(a, 2, from_d.at[top], from_d.at[top], me).wait_recv()
            rc(a, 3, from_d.at[bottom], from_d.at[bottom], me).wait_recv()
            cp = rc(a, 6, from_d, from_d, sib)
            cp.start()
            sent.append(cp)
        for a in range(na):
            for k, jj in ((4, jx), (5, jy), (6, jd)):
                rc(a, k, p_refs[a].at[c], out_refs[a].at[jj, 1 - c], me).wait_recv()
            rc(a, 7, p_refs[a], out_refs[a].at[j], me).wait_recv()
        for cp in sent:
            cp.wait_send()

    return pl.pallas_call(
        body, name="ag_weights",
        out_shape=[jax.ShapeDtypeStruct((N_CHIPS,) + p.shape, p.dtype) for p in packs],
        in_specs=[HBM_SPEC] * na, out_specs=[HBM_SPEC] * na,
        scratch_shapes=[pltpu.SemaphoreType.DMA((nsem * na,)), pltpu.SemaphoreType.DMA((nsem * na,))])(*packs)


def _rs_pair(dwpt, gpack):
    n = N_CHIPS
    hw = SHARD_PAD // 2

    def body(d_ref, g_ref, out_d, out_g, send_sems, recv_sems):
        x, y, c, _ = _place()
        sib = (x, y, 1 - c)
        cps = []
        for p in range(n):
            start = pl.multiple_of(WIN_BASE[p] + (1 - c) * hw, TILE_ROWS)
            cps.append(pltpu.make_async_remote_copy(
                src_ref=d_ref.at[pl.ds(start, hw)], dst_ref=out_d.at[p], send_sem=send_sems.at[p],
                recv_sem=recv_sems.at[p], device_id=sib, device_id_type=MESH))
            cps.append(pltpu.make_async_remote_copy(
                src_ref=g_ref.at[p, 1 - c], dst_ref=out_g.at[p], send_sem=send_sems.at[n + p],
                recv_sem=recv_sems.at[n + p], device_id=sib, device_id_type=MESH))
        for cp in cps:
            cp.start()
        for cp in cps:
            cp.wait_recv()
        for cp in cps:
            cp.wait_send()

    return pl.pallas_call(
        body, name="rs_pair",
        out_shape=[jax.ShapeDtypeStruct((n, hw, dwpt.shape[1]), dwpt.dtype),
                   jax.ShapeDtypeStruct((n,) + gpack.shape[2:], gpack.dtype)],
        in_specs=[HBM_SPEC] * 2, out_specs=[HBM_SPEC] * 2,
        scratch_shapes=[pltpu.SemaphoreType.DMA((2 * n,)), pltpu.SemaphoreType.DMA((2 * n,))])(dwpt, gpack)


def _add_halves_win(dwpt, other, c):
    n, rh, wd = other.shape
    tr = _row_tile(rh)

    def body(s_ref, d_ref, o_ref, out_ref):
        out_ref[0] = (d_ref[...] + o_ref[0]).astype(BF16)

    scal = jnp.concatenate([jnp.reshape(c, (1,)).astype(jnp.int32), jnp.asarray(WIN_BASE, jnp.int32)])
    grid_spec = pltpu.PrefetchScalarGridSpec(
        num_scalar_prefetch=1, grid=(n, rh // tr),
        in_specs=[pl.BlockSpec((pl.Element(tr), pl.Element(wd)),
                               lambda p, i, sr: (pl.multiple_of(sr[1 + p] + sr[0] * rh + i * tr, TILE_ROWS), 0)),
                  pl.BlockSpec((1, tr, wd), lambda p, i, sr: (p, i, 0))],
        out_specs=pl.BlockSpec((1, tr, wd), lambda p, i, sr: (p, i, 0)))
    return pl.pallas_call(
        body, name="add_halves_in", grid_spec=grid_spec, out_shape=jax.ShapeDtypeStruct((n, rh, wd), BF16),
        compiler_params=_cparams("parallel", "parallel"))(scal, dwpt, other)


SEM_SPEC = pl.BlockSpec(memory_space=pltpu.SEMAPHORE)
DATAFLOW_EFFECT = pltpu.SideEffectType.DATAFLOW_SIDE_EFFECTING


def _rs_chips_start(csums):
    na = len(csums)

    def body(*refs):
        s_refs, land_refs = refs[:na], refs[na:2 * na]
        send_sems, recv_sems = refs[2 * na], refs[2 * na + 1]
        token = refs[-1]
        x, y, c, chips = _place()
        j = 2 * x + y
        for a in range(na):
            for k, (cx, cy) in enumerate(chips):
                pltpu.make_async_remote_copy(src_ref=s_refs[a].at[2 * cx + cy], dst_ref=land_refs[a].at[j],
                                             send_sem=send_sems.at[3 * a + k], recv_sem=recv_sems.at[3 * a + k],
                                             device_id=(cx, cy, c), device_id_type=MESH).start()
        token[...] = jnp.zeros_like(token)

    hbm = [pltpu.HBM(s.shape, s.dtype) for s in csums]
    args = [pltpu.with_memory_space_constraint(s, pltpu.HBM) for s in csums]
    args += [pltpu.with_memory_space_constraint(lax.empty(s.shape, s.dtype), pltpu.HBM) for s in csums]
    res = pl.pallas_call(
        body, name="rs_chips_start",
        out_shape=(pltpu.SemaphoreType.DMA((3 * na,)), pltpu.SemaphoreType.DMA((3 * na,)), *hbm, *hbm,
                   jax.ShapeDtypeStruct((8, 128), F32)),
        in_specs=[pl.BlockSpec(memory_space=pltpu.HBM)] * (2 * na),
        out_specs=(SEM_SPEC, SEM_SPEC, *[pl.BlockSpec(memory_space=pltpu.HBM)] * (2 * na),
                   pl.BlockSpec(memory_space=pltpu.VMEM)),
        input_output_aliases={i: 2 + i for i in range(2 * na)},
        compiler_params=pltpu.CompilerParams(has_side_effects=DATAFLOW_EFFECT))(*args)
    return res[0], res[1], list(res[2:2 + na]), list(res[2 + na:2 + 2 * na]), res[-1]


def _rs_chips_wait(send_sems, recv_sems, csums, lands, after):
    na = len(csums)

    def body(*refs):
        s_refs, land_refs = refs[:na], refs[na:2 * na]
        send_sems, recv_sems = refs[2 * na], refs[2 * na + 1]
        x, y, c, chips = _place()
        j = 2 * x + y
        for a in range(na):
            for k, (cx, cy) in enumerate(chips):
                cp = pltpu.make_async_remote_copy(src_ref=s_refs[a].at[2 * cx + cy], dst_ref=land_refs[a].at[2 * cx + cy],
                                                  send_sem=send_sems.at[3 * a + k], recv_sem=recv_sems.at[3 * a + k],
                                                  device_id=(cx, cy, c), device_id_type=MESH)
                cp.wait_send()
                cp.wait_recv()

    hbm = [pltpu.HBM(s.shape, s.dtype) for s in csums]
    res = pl.pallas_call(
        body, name="rs_chips_wait", out_shape=(*hbm, *hbm),
        in_specs=[pl.BlockSpec(memory_space=pltpu.HBM)] * (2 * na) + [SEM_SPEC, SEM_SPEC, pl.BlockSpec(memory_space=pl.ANY)],
        out_specs=tuple([pl.BlockSpec(memory_space=pltpu.HBM)] * (2 * na)),
        input_output_aliases={i: i for i in range(2 * na)},
        compiler_params=pltpu.CompilerParams(has_side_effects=DATAFLOW_EFFECT))(*csums, *lands, send_sems, recv_sems, after)
    return list(res[:na]), list(res[na:])


SWAP_CHUNKS = 4


def _pair_swap(halves):
    na = len(halves)

    def body(*refs):
        h_refs, out_refs = refs[:na], refs[na:2 * na]
        send_sems, recv_sems = refs[2 * na:]
        x, y, c, _ = _place()
        cps = []
        for a in range(na):
            rows = h_refs[a].shape[0] // SWAP_CHUNKS
            assert rows * SWAP_CHUNKS == h_refs[a].shape[0]
            for q in range(SWAP_CHUNKS):
                k = SWAP_CHUNKS * a + q
                cps.append(pltpu.make_async_remote_copy(
                    src_ref=h_refs[a].at[pl.ds(q * rows, rows)], dst_ref=out_refs[a].at[pl.ds(q * rows, rows)],
                    send_sem=send_sems.at[k], recv_sem=recv_sems.at[k], device_id=(x, y, 1 - c), device_id_type=MESH))
        for cp in cps:
            cp.start()
        for cp in cps:
            cp.wait_recv()
        for cp in cps:
            cp.wait_send()

    return pl.pallas_call(
        body, name="pair_swap", out_shape=[jax.ShapeDtypeStruct(h.shape, h.dtype) for h in halves],
        in_specs=[HBM_SPEC] * na, out_specs=[HBM_SPEC] * na,
        scratch_shapes=[pltpu.SemaphoreType.DMA((SWAP_CHUNKS * na,)), pltpu.SemaphoreType.DMA((SWAP_CHUNKS * na,))])(*halves)


def _ag_small(v):
    m_per, n = v.shape

    def body(x_ref, out_ref, send_sems, recv_sems, local_sem):
        x, y, c, chips = _place()
        me, sibling = (x, y, c), (x, y, 1 - c)

        def rows(px, py, pc):
            return out_ref.at[pl.ds((4 * px + 2 * py + pc) * m_per, m_per), :]

        def copy(k, block, to, src=None):
            return pltpu.make_async_remote_copy(
                src_ref=rows(*block) if src is None else src, dst_ref=rows(*block), send_sem=send_sems.at[k],
                recv_sem=recv_sems.at[k], device_id=to, device_id_type=MESH)

        mine = pltpu.make_async_copy(x_ref, rows(*me), local_sem)
        mine.start()
        first = [copy(0, me, sibling, src=x_ref)]
        first += [copy(1 + k, me, (*chip, c), src=x_ref) for k, chip in enumerate(chips)]
        for cp in first:
            cp.start()
        passed = [copy(4 + k, (*chip, c), sibling) for k, chip in enumerate(chips)]
        for k, chip in enumerate(chips):
            copy(1 + k, (*chip, c), me).wait_recv()
            passed[k].start()
        copy(0, sibling, me).wait_recv()
        for k, chip in enumerate(chips):
            copy(4 + k, (*chip, 1 - c), me).wait_recv()
        for cp in first + passed:
            cp.wait_send()
        mine.wait()

    return pl.pallas_call(
        body, name="ag_small", out_shape=jax.ShapeDtypeStruct((8 * m_per, n), v.dtype),
        in_specs=[pl.BlockSpec(memory_space=pltpu.VMEM)], out_specs=pl.BlockSpec(memory_space=pltpu.VMEM),
        scratch_shapes=[pltpu.SemaphoreType.DMA((7,)), pltpu.SemaphoreType.DMA((7,)), pltpu.SemaphoreType.DMA])(v)


def _sum_blocks(a, nblk, name):
    rows, wd = a.shape
    r = rows // nblk
    tr = min(r, ROW_TILE)
    assert r % tr == 0

    def body(*refs):
        acc = refs[0][...].astype(F32)
        for ref in refs[1:nblk]:
            acc = acc + ref[...].astype(F32)
        refs[nblk][...] = acc

    nt = r // tr
    return pl.pallas_call(
        body, name=name, grid=(nt,),
        in_specs=[pl.BlockSpec((tr, wd), functools.partial(lambda i, b: (b * nt + i, 0), b=b)) for b in range(nblk)],
        out_specs=pl.BlockSpec((tr, wd), lambda i: (i, 0)),
        out_shape=jax.ShapeDtypeStruct((r, wd), F32), compiler_params=_cparams("parallel"))(*([a] * nblk))


def _row_tile(rows):
    best = max(t for t in range(16, 513, 16) if rows % t == 0)
    return best


def _sum_chips(by_src, csum, j, name):
    n, rh, wd = by_src.shape
    tr = _row_tile(rh)

    def body(j_ref, *refs):
        own = refs[n][0].astype(F32)
        acc = None
        for k in range(n):
            term = jnp.where(j_ref[0] == k, own, refs[k][0].astype(F32))
            acc = term if acc is None else acc + term
        refs[n + 1][...] = acc

    def other(k):
        return pl.BlockSpec((1, tr, wd), lambda i, jr: (jnp.where(jr[0] == k, (k + 1) % n, k), i, 0))

    grid_spec = pltpu.PrefetchScalarGridSpec(
        num_scalar_prefetch=1, grid=(rh // tr,),
        in_specs=[other(k) for k in range(n)] + [pl.BlockSpec((1, tr, wd), lambda i, jr: (jr[0], i, 0))],
        out_specs=pl.BlockSpec((tr, wd), lambda i, jr: (i, 0)))
    return pl.pallas_call(
        body, name=name, grid_spec=grid_spec, out_shape=jax.ShapeDtypeStruct((rh, wd), F32),
        compiler_params=_cparams("parallel"))(jnp.reshape(j, (1,)).astype(jnp.int32), *([by_src] * n), csum)


def _add_halves(gpack, other, c, name):
    n, _, rh, wd = gpack.shape
    tr = _row_tile(rh)

    def body(c_ref, g_ref, o_ref, out_ref):
        out_ref[0] = (g_ref[0, 0] + o_ref[0]).astype(BF16)

    grid_spec = pltpu.PrefetchScalarGridSpec(
        num_scalar_prefetch=1, grid=(n, rh // tr),
        in_specs=[pl.BlockSpec((1, 1, tr, wd), lambda p, i, cr: (p, cr[0], i, 0)),
                  pl.BlockSpec((1, tr, wd), lambda p, i, cr: (p, i, 0))],
        out_specs=pl.BlockSpec((1, tr, wd), lambda p, i, cr: (p, i, 0)))
    return pl.pallas_call(
        body, name=name, grid_spec=grid_spec, out_shape=jax.ShapeDtypeStruct((n, rh, wd), BF16),
        compiler_params=_cparams("parallel", "parallel"))(jnp.reshape(c, (1,)).astype(jnp.int32), gpack, other)


PACK_W = 1024
ROWS_O_DN = DN_W // N_CHIPS
ROWS_O_DIL = DIL_W * (D_MODEL // N_CHIPS) // PACK_W
ROWS_OUT = D_MODEL // N_CHIPS
ROWS_CONV = 4 * (3 * DN_W // N_CHIPS) // PACK_W
R1 = ROWS_O_DN
R2 = R1 + ROWS_O_DIL
R3 = R2 + ROWS_OUT
R4 = R3 + 16
R5 = R4 + 16
PACK_ROWS = 704
HALF_ROWS = PACK_ROWS // 2
SHARD_PAD = 2880


R6 = R5 + 2 * DN_HEADS

TILE_ROWS = 16
BA_IN_SHARD1 = REF_OFF_BA - SHARD_W
LOCAL_START = (0, SHARD_W, 2 * SHARD_W - 2 * DN_HEADS, 3 * SHARD_W - 2 * DN_HEADS)
LOCAL_END = LOCAL_START[1:] + (OFF_BA,)
WIN_BASE = tuple(s // TILE_ROWS * TILE_ROWS for s in LOCAL_START)


def _to_window(k, shard):
    nba = 2 * DN_HEADS
    body = shard
    if k == 1:
        row = lax.broadcasted_iota(jnp.int32, (SHARD_W - nba, 1), 0)
        body = jnp.where(row < BA_IN_SHARD1, shard[:SHARD_W - nba], shard[nba:])
    lead = LOCAL_START[k] - WIN_BASE[k]
    return jnp.pad(body, ((lead, SHARD_PAD - lead - body.shape[0]), (0, 0)))


def _from_window(k, win, ba):
    nba = 2 * DN_HEADS
    lead = LOCAL_START[k] - WIN_BASE[k]
    if k != 1:
        return win[lead:lead + SHARD_W]
    row = lax.broadcasted_iota(jnp.int32, (SHARD_W, 1), 0)
    before = win[lead:lead + SHARD_W]
    after = jnp.pad(win, ((nba, 0), (0, 0)))[lead:lead + SHARD_W]
    mid = jnp.pad(ba, ((BA_IN_SHARD1, SHARD_W - BA_IN_SHARD1 - nba), (0, 0)))
    return jnp.where(row < BA_IN_SHARD1, before, jnp.where(row < BA_IN_SHARD1 + nba, mid, after))


def _stack_windows(wins, ba):
    pieces = []
    for k in range(N_CHIPS):
        lo = WIN_BASE[k] + (TILE_ROWS if k else 0)
        hi = LOCAL_END[k] // TILE_ROWS * TILE_ROWS
        pieces.append(wins[k][lo - WIN_BASE[k]:hi - WIN_BASE[k]])
        if k + 1 < N_CHIPS:
            assert hi == WIN_BASE[k + 1]
            pieces.append(wins[k][hi - WIN_BASE[k]:hi - WIN_BASE[k] + TILE_ROWS] + wins[k + 1][:TILE_ROWS])
    pieces += [ba, jnp.zeros((PW - OFF_BA - ba.shape[0], ba.shape[1]), ba.dtype)]
    out = jnp.concatenate(pieces, axis=0)
    assert out.shape[0] == PW
    return out


def _to_ref_layout(wpt):
    return jnp.concatenate([wpt[:REF_OFF_BA], wpt[OFF_BA:OFF_BA + 2 * DN_HEADS], wpt[REF_OFF_BA:OFF_BA]], axis=0)


def _from_ref_layout(wt):
    pad = jnp.zeros((PW - PROJ_W, wt.shape[1]), wt.dtype)
    return jnp.concatenate([wt[:REF_OFF_BA], wt[REF_OFF_BA + 2 * DN_HEADS:], wt[REF_OFF_BA:REF_OFF_BA + 2 * DN_HEADS], pad],
                           axis=0)


def _local_step(x, tgt, norm_w, wpt, conv_full, a_log, dt_bias, dn_norm_w, w_o_dn, w_o_dil, w_out, final_norm_w):
    s = x.shape[0]
    h, h_t = _rms_in(x, norm_w)
    proj = _matmul(h, wpt, F32, 2048, 1280, 1024, "proj", nt=True)
    c_pre, qkv = _conv_fwd(proj, conv_full)
    gate_par = jnp.zeros((8, 128), F32).at[0, 8:16].set(a_log[0]).at[1, 8:16].set(dt_bias[0])
    bg = _gates_fwd(proj, gate_par)
    o_a, u, w, vn, tmat, states = _gdr_fwd(qkv, bg)
    oa2, oa2_t = _gdr_out(o_a, proj, dn_norm_w)
    ya = _matmul(oa2, w_o_dn, F32, 512, 1024, 1024, "ya")
    parts = [_att_fwd(proj, g) for g in range(N_DIL)]
    ob, o_att, lse, ob_t = _att_merge(parts, proj)
    yb = _matmul(ob, w_o_dil, F32, 512, 1024, 512, "yb")
    mg, mg_t = _merge(proj, ya, yb)
    t = _matmul(mg, w_out, F32, 512, 1024, 1024, "t_out")
    dx2, dfw, lpart = _final(x, t, final_norm_w, tgt)

    dmg = _matmul(dx2, w_out, F32, 512, 1024, 1024, "d_merged", nt=True)
    dw_out = _matmul(mg_t, dx2, F32, 1024, 1024, 1024, "dw_out")
    dya, dyb, dga, dgb = _merge_bwd(proj, ya, yb, dmg)
    doa2 = _matmul(dya, w_o_dn, F32, 512, 1024, 1024, "d_oa2", nt=True)
    dw_o_dn = _matmul(oa2_t, dya, F32, 1024, 1024, 1024, "dw_o_dn")
    dob = _matmul(dyb, w_o_dil, F32, 512, 512, 1024, "d_ob", nt=True)
    dw_o_dil = _matmul(ob_t, dyb, F32, 512, 1024, 1024, "dw_o_dil")
    do_a, dproj, ddnw = _gdr_out_bwd(o_a, proj, dn_norm_w, doa2)
    dq_a, dk_a, dv_a, dbg = _gdr_bwd(qkv, bg, u, w, vn, tmat, states, do_a)
    dproj, dpar = _gates_bwd(proj, gate_par, dbg, dproj)
    dc = _conv_bwd_act(c_pre, dq_a, dk_a, dv_a)
    dproj, dconv = _conv_bwd(proj, dc, conv_full, dproj)
    do_att, delta, dproj = _att_merge_bwd(o_att, proj, dob, dproj)
    dqkv_b = [_att_bwd(proj, g, do_att, lse, delta) for g in range(N_DIL)]
    pieces = [(OFF_Q_B + (N_DIL * i + g) * DIL_W, dqkv_b[g][i]) for i in range(3) for g in range(N_DIL)]
    for off, piece in pieces + [(OFF_G_A, dga), (OFF_G_B, dgb)]:
        dproj = lax.dynamic_update_slice(dproj, piece, (0, off))
    dwpt, dwpt_b = _matmul(h_t, dproj, F32, 1024, 1280, 2048, "dw_in", transpose_out=True, also_bf16=True)

    def finish(after=None):
        dh = _matmul(dproj, wpt, F32, 1024, 1024, 3840, "d_h", after=after)
        grad_x, dnw = _rms_in_bwd(x, norm_w, dh, dx2)
        small = jnp.zeros((8, PACK_W), F32)
        small = small.at[0].set(dnw[0]).at[1].set(dfw[0]).at[2, :DN_D].set(ddnw[0])
        small = small.at[3, :DN_HEADS].set(dpar[0, 8:16]).at[3, DN_HEADS:2 * DN_HEADS].set(dpar[1, 8:16])
        small = small.at[4, 0].set(lpart[0, 0])
        return grad_x, small

    return finish, (dwpt, dwpt_b), dconv, dw_o_dn, dw_o_dil, dw_out


def kernel(x, norm_w, w_in, conv_w, a_log, dt_bias, dn_norm_w, w_o_dn, w_o_dil, w_out, final_norm_w, loss_target, m_norm_w, m_w_in, m_conv_w, m_a_log, m_dt_bias, m_dn_norm_w, m_w_o_dn, m_w_o_dil, m_w_out, m_final_norm_w, v_norm_w, v_w_in, v_conv_w, v_a_log, v_dt_bias, v_dn_norm_w, v_w_o_dn, v_w_o_dil, v_w_out, v_final_norm_w):
    c = lax.axis_index("c")
    j = 2 * lax.axis_index("x") + lax.axis_index("y")
    qw = D_MODEL // N_CHIPS

    cw = conv_w[0].reshape(ROWS_CONV, PACK_W)
    cw = jnp.pad(cw, ((0, 16 - ROWS_CONV), (0, 0)))
    cw_hi = cw.astype(BF16)
    cw_lo = (cw - cw_hi.astype(F32)).astype(BF16)
    shard = w_in[0].T.astype(BF16)
    own_ba = jnp.where(j == 1, shard[BA_IN_SHARD1:BA_IN_SHARD1 + 2 * DN_HEADS], jnp.zeros((2 * DN_HEADS, D_MODEL), BF16))
    pack = jnp.concatenate(
        [w_o_dn[0].astype(BF16), w_o_dil[0].astype(BF16).reshape(ROWS_O_DIL, PACK_W), w_out[0].astype(BF16), cw_hi, cw_lo,
         own_ba, jnp.zeros((PACK_ROWS - R6, PACK_W), BF16)], axis=0).reshape(2, HALF_ROWS, PACK_W)
    chips = range(N_CHIPS)
    own_win = lax.switch(j, [functools.partial(_to_window, k) for k in chips], shard).reshape(2, SHARD_PAD // 2, D_MODEL)
    all_in, allw = _ag_weights([own_win, pack])
    wins = [all_in[k].reshape(SHARD_PAD, D_MODEL) for k in chips]
    allw = [allw[k].reshape(PACK_ROWS, PACK_W) for k in chips]
    wpt = _stack_windows(wins, allw[1][R5:R6])
    w_o_dn_full = jnp.concatenate([allw[k][:R1] for k in chips], axis=0)
    w_o_dil_full = jnp.concatenate([allw[k][R1:R2].reshape(DIL_W, qw) for k in chips], axis=1)
    w_out_full = jnp.concatenate([allw[k][R2:R3] for k in chips], axis=0)
    conv_full = jnp.concatenate(
        [(allw[k][R3:R3 + ROWS_CONV].astype(F32) + allw[k][R4:R4 + ROWS_CONV].astype(F32)).reshape(4, 3 * DN_W // N_CHIPS)
         for k in chips], axis=1)

    finish, (dwpt, dwpt_b), dconv, dw_o_dn, dw_o_dil, dw_out = _local_step(
        x[0], loss_target[0], norm_w, wpt, conv_full, a_log, dt_bias, dn_norm_w, w_o_dn_full, w_o_dil_full, w_out_full,
        final_norm_w.reshape(1, D_MODEL))

    cq = 3 * DN_W // N_CHIPS
    gpack = jnp.stack([
        jnp.concatenate(
            [dw_o_dn[k * qw:(k + 1) * qw], dw_o_dil[:, k * qw:(k + 1) * qw].reshape(ROWS_O_DIL, PACK_W),
             dw_out[k * qw:(k + 1) * qw],
             jnp.pad(dconv[:, k * cq:(k + 1) * cq].reshape(ROWS_CONV, PACK_W), ((0, 16 - ROWS_CONV), (0, 0))),
             dwpt[OFF_BA:OFF_BA + 2 * DN_HEADS] if k == 1 else jnp.zeros((2 * DN_HEADS, PACK_W), F32),
             jnp.zeros((PACK_ROWS - R4 - 2 * DN_HEADS, PACK_W), F32)], axis=0)
        for k in chips]).reshape(N_CHIPS, 2, HALF_ROWS, PACK_W)
    sib_in, sib_pack = _rs_pair(dwpt_b, gpack)
    csum_in = _add_halves_win(dwpt, sib_in, c)
    csum_pack = _add_halves(gpack, sib_pack, c, "add_halves_pack")
    send_sems, recv_sems, csums, lands, token = _rs_chips_start([csum_in, csum_pack])
    grad_x, small = finish(after=token)
    (csum_in, csum_pack), (src_in, src_pack) = _rs_chips_wait(send_sems, recv_sems, csums, lands, grad_x)
    half_in = _sum_chips(src_in, csum_in, j, "sum_chips_in")
    half_pack = _sum_chips(src_pack, csum_pack, j, "sum_chips_pack")
    sib_half_in, sib_half_pack = _pair_swap([half_in, half_pack])

    def both_halves(mine, theirs):
        return jnp.where(c == 0, jnp.concatenate([mine, theirs], axis=0), jnp.concatenate([theirs, mine], axis=0))

    g = both_halves(half_pack, sib_half_pack)
    g_w_in = lax.switch(j, [functools.partial(_from_window, k) for k in chips], both_halves(half_in, sib_half_in),
                        g[R4:R4 + 2 * DN_HEADS])
    g_w_o_dn = g[:R1]
    g_w_o_dil = g[R1:R2].reshape(DIL_W, qw)
    g_w_out = g[R2:R3]
    g_conv = g[R3:R3 + ROWS_CONV].reshape(4, cq)

    gs = _sum_blocks(_ag_small(small), 8, "sum_small")
    loss = gs[4, 0]
    w_small = jnp.zeros((8, PACK_W), F32)

    def pack_small(nw, fw, dnw_, al, db):
        t = w_small.at[0].set(nw[0]).at[1].set(fw).at[2, :DN_D].set(dnw_[0])
        return t.at[3, :DN_HEADS].set(al[0]).at[3, DN_HEADS:2 * DN_HEADS].set(db[0])

    sm = _adamw(pack_small(norm_w, final_norm_w, dn_norm_w, a_log, dt_bias), gs,
                pack_small(m_norm_w, m_final_norm_w, m_dn_norm_w, m_a_log, m_dt_bias),
                pack_small(v_norm_w, v_final_norm_w, v_dn_norm_w, v_a_log, v_dt_bias), "adamw_small")

    def unpack_small(t):
        return dict(norm_w=t[0:1], final_norm_w=t[1], dn_norm_w=t[2:3, :DN_D], a_log=t[3:4, :DN_HEADS],
                    dt_bias=t[3:4, DN_HEADS:2 * DN_HEADS])

    res = {"grad": unpack_small(gs)}
    for kind, arr in zip(("delta", "new_m", "new_v"), sm):
        res[kind] = unpack_small(arr)
    big = dict(conv_w=(conv_w, g_conv, m_conv_w, v_conv_w), w_o_dn=(w_o_dn, g_w_o_dn, m_w_o_dn, v_w_o_dn),
               w_o_dil=(w_o_dil, g_w_o_dil, m_w_o_dil, v_w_o_dil), w_out=(w_out, g_w_out, m_w_out, v_w_out))
    for name, (wt, gt, mt, vt) in big.items():
        d, nm, nv = _adamw(wt[0], gt, mt[0], vt[0], "adamw_" + name)
        res["grad"][name] = gt[None]
        res["delta"][name], res["new_m"][name], res["new_v"][name] = d[None], nm[None], nv[None]

    d, nm, nv = _adamw(w_in[0].T, g_w_in, m_w_in[0].T, v_w_in[0].T, "adamw_w_in")
    res["grad"]["w_in"] = g_w_in.T[None]
    res["delta"]["w_in"], res["new_m"]["w_in"], res["new_v"]["w_in"] = d.T[None], nm.T[None], nv.T[None]
    order = ["norm_w", "w_in", "conv_w", "a_log", "dt_bias", "dn_norm_w", "w_o_dn", "w_o_dil", "w_out", "final_norm_w"]
    outs = [loss, grad_x[None]]
    for kind in ("grad", "delta", "new_m", "new_v"):
        outs += [res[kind][nm] for nm in order]
    return tuple(outs)
```

```python
import functools
import math

import jax
import jax.numpy as jnp
from jax import lax
from jax.experimental import pallas as pl
from jax.experimental.pallas import tpu as pltpu

F32 = jnp.float32
BF16 = jnp.bfloat16
MESH = pl.DeviceIdType.MESH

D_MODEL = 1024
DN_HEADS = 8
DN_D = 128
DN_CHUNK = 64
DN_W = DN_HEADS * DN_D
DIL_GROUPS = ((128, 1), (512, 4), (2048, 16))
N_DIL = len(DIL_GROUPS)
DIL_HEADS = 4
DIL_DH = 128
DIL_W = DIL_HEADS * DIL_DH
ATT_BLOCK = 128
NORM_EPS = 1e-6
PROJ_W = 11280
N_CHIPS = 4
SHARD_W = PROJ_W // N_CHIPS

OFF_QKV_A = 0
OFF_Z_A = 3072
OFF_Q_B = 4096
OFF_K_B = 5632
OFF_V_B = 7168
OFF_Z_B = 8704
OFF_G_A = 9216
OFF_G_B = 10240
OFF_BA = 11264
PW = 11520
REF_OFF_BA = 4096

ADAM_LR = 0.001
ADAM_B1 = 0.9
ADAM_B2 = 0.999
ADAM_EPS = 1e-08
ADAM_WD = 0.01
ADAM_STEP = 10

ROW_TILE = 256
NEG = -1e30


def _dot(a, b):
    return jnp.dot(a.astype(BF16), b.astype(BF16), preferred_element_type=F32)


def _dot_nt(a, b):
    return lax.dot_general(a.astype(BF16), b.astype(BF16), (((1,), (1,)), ((), ())), preferred_element_type=F32)


def _dot_tn(a, b):
    return lax.dot_general(a.astype(BF16), b.astype(BF16), (((0,), (0,)), ((), ())), preferred_element_type=F32)


def _split(a):
    hi = a.astype(BF16)
    lo = (a - hi.astype(F32)).astype(BF16)
    return hi, lo


def _dot_exact_lhs(c, a):
    hi, lo = _split(a)
    cb = c.astype(BF16)
    return jnp.dot(cb, hi, preferred_element_type=F32) + jnp.dot(cb, lo, preferred_element_type=F32)


def _dot_exact_rhs(a, c):
    hi, lo = _split(a)
    cb = c.astype(BF16)
    return jnp.dot(hi, cb, preferred_element_type=F32) + jnp.dot(lo, cb, preferred_element_type=F32)


def _dot_tn_exact_rhs(a, c):
    hi, lo = _split(a)
    cb = c.astype(BF16)
    dn = (((0,), (0,)), ((), ()))
    return (lax.dot_general(hi, cb, dn, preferred_element_type=F32)
            + lax.dot_general(lo, cb, dn, preferred_element_type=F32))


def _sigmoid(x):
    return 1.0 / (1.0 + jnp.exp(-x))


def _silu(x):
    return x * _sigmoid(x)


def _silu_grad(x):
    s = _sigmoid(x)
    return s * (1.0 + x * (1.0 - s))


def _softplus(x):
    return jnp.maximum(x, 0.0) + jnp.log(1.0 + jnp.exp(-jnp.abs(x)))


def _cparams(*sem):
    return pltpu.CompilerParams(dimension_semantics=sem)


def _matmul(a, b, out_dtype, tm, tn, tk, name, nt=False, transpose_out=False, after=None, also_bf16=False):
    m, kdim = a.shape
    n = b.shape[0] if nt else b.shape[1]
    tm, tn, tk = min(tm, m), min(tn, n), min(tk, kdim)
    assert m % tm == 0 and n % tn == 0 and kdim % tk == 0, (name, a.shape, b.shape, tm, tn, tk)
    nk = kdim // tk
    dot = _dot_nt if nt else _dot
    b_spec = (pl.BlockSpec((tn, tk), lambda i, j, k: (j, k)) if nt else pl.BlockSpec((tk, tn), lambda i, j, k: (k, j)))
    extra = [] if after is None else [after]
    out_dtypes = [out_dtype] + ([BF16] if also_bf16 else [])

    def emit(o_refs, acc):
        val = acc.T if transpose_out else acc
        for o_ref in o_refs:
            o_ref[...] = val.astype(o_ref.dtype)

    def outs_of(rest):
        return rest[len(extra):len(extra) + len(out_dtypes)]

    if nk == 1:
        def body(a_ref, b_ref, *rest):
            emit(outs_of(rest), dot(a_ref[...], b_ref[...]))
        scratch = []
    else:
        def body(a_ref, b_ref, *rest):
            o_ref, acc_ref = outs_of(rest), rest[-1]
            k = pl.program_id(2)
            p = dot(a_ref[...], b_ref[...])

            @pl.when(k == 0)
            def _():
                acc_ref[...] = p

            @pl.when(k > 0)
            def _():
                acc_ref[...] += p

            @pl.when(k == nk - 1)
            def _():
                emit(o_ref, acc_ref[...])
        scratch = [pltpu.VMEM((tm, tn), F32)]

    if transpose_out:
        out_spec, out_shape = pl.BlockSpec((tn, tm), lambda i, j, k: (j, i)), (n, m)
    else:
        out_spec, out_shape = pl.BlockSpec((tm, tn), lambda i, j, k: (i, j)), (m, n)
    res = pl.pallas_call(
        body, name=name, grid=(m // tm, n // tn, nk),
        in_specs=[pl.BlockSpec((tm, tk), lambda i, j, k: (i, k)), b_spec] + [pl.BlockSpec(memory_space=pl.ANY)] * len(extra),
        out_specs=[out_spec] * len(out_dtypes), out_shape=[jax.ShapeDtypeStruct(out_shape, d) for d in out_dtypes],
        scratch_shapes=scratch, compiler_params=_cparams("parallel", "parallel", "arbitrary"))(a, b, *extra)
    return res if also_bf16 else res[0]


def _rms_in(x, nw):
    s, d = x.shape

    def body(x_ref, w_ref, h_ref, ht_ref):
        xv = x_ref[...]
        r = lax.rsqrt(jnp.mean(xv * xv, axis=-1, keepdims=True) + NORM_EPS)
        h = xv * r * w_ref[...]
        h_ref[...] = h.astype(BF16)
        ht_ref[...] = h.T.astype(BF16)

    return pl.pallas_call(
        body, name="rms_in", grid=(s // ROW_TILE,),
        in_specs=[pl.BlockSpec((ROW_TILE, d), lambda i: (i, 0)), pl.BlockSpec((1, d), lambda i: (0, 0))],
        out_specs=[pl.BlockSpec((ROW_TILE, d), lambda i: (i, 0)), pl.BlockSpec((d, ROW_TILE), lambda i: (0, i))],
        out_shape=[jax.ShapeDtypeStruct((s, d), BF16), jax.ShapeDtypeStruct((d, s), BF16)],
        compiler_params=_cparams("parallel"))(x, nw)


def _rms_in_bwd(x, nw, dh, dx2):
    s, d = x.shape

    def body(x_ref, w_ref, dh_ref, dx2_ref, dx_ref, dw_ref):
        i = pl.program_id(0)
        xv = x_ref[...]
        r = lax.rsqrt(jnp.mean(xv * xv, axis=-1, keepdims=True) + NORM_EPS)
        dhv = dh_ref[...]
        dyw = dhv * w_ref[...]
        dx_ref[...] = dx2_ref[...] + r * dyw - xv * (r * r * r) * jnp.mean(dyw * xv, axis=-1, keepdims=True)
        part = jnp.sum(dhv * xv * r, axis=0, keepdims=True)

        @pl.when(i == 0)
        def _():
            dw_ref[...] = part

        @pl.when(i > 0)
        def _():
            dw_ref[...] += part

    row = pl.BlockSpec((ROW_TILE, d), lambda i: (i, 0))
    vec = pl.BlockSpec((1, d), lambda i: (0, 0))
    return pl.pallas_call(
        body, name="rms_in_bwd", grid=(s // ROW_TILE,), in_specs=[row, vec, row, row], out_specs=[row, vec],
        out_shape=[jax.ShapeDtypeStruct((s, d), F32), jax.ShapeDtypeStruct((1, d), F32)],
        compiler_params=_cparams("arbitrary"))(x, nw, dh, dx2)


def _shift_down(cur, prev8, k):
    rc = pltpu.roll(cur, k, 0)
    rp = pltpu.roll(prev8, k, 0)
    row = lax.broadcasted_iota(jnp.int32, prev8.shape, 0)
    top = jnp.where(row < k, rp, rc[:8])
    return jnp.concatenate([top, rc[8:]], axis=0)


def _shift_up(cur, next8, k):
    t = cur.shape[0]
    rc = pltpu.roll(cur, t - k, 0)
    rn = pltpu.roll(next8, 8 - k, 0)
    row = lax.broadcasted_iota(jnp.int32, next8.shape, 0)
    bot = jnp.where(row >= 8 - k, rn, rc[t - 8:])
    return jnp.concatenate([rc[:t - 8], bot], axis=0)


def _conv_fwd(proj, conv_w):
    s = proj.shape[0]
    t8 = ROW_TILE // 8

    def body(u_ref, up_ref, w_ref, c_ref, y_ref):
        i = pl.program_id(0)
        part = pl.program_id(1)
        cur = u_ref[...]
        prev8 = jnp.where(i > 0, up_ref[...], 0.0)
        w = w_ref[...]
        c = cur * w[3:4, :]
        for k in (1, 2, 3):
            c = c + _shift_down(cur, prev8, k) * w[3 - k:4 - k, :]
        c_ref[...] = c
        a = _silu(c)
        for h in range(DN_HEADS):
            ah = a[:, h * DN_D:(h + 1) * DN_D]
            r = lax.rsqrt(jnp.sum(ah * ah, axis=-1, keepdims=True) + NORM_EPS)
            y_ref[:, h * DN_D:(h + 1) * DN_D] = jnp.where(part < 2, ah * r, ah)

    return pl.pallas_call(
        body, name="conv_fwd", grid=(s // ROW_TILE, 3),
        in_specs=[pl.BlockSpec((ROW_TILE, DN_W), lambda i, p: (i, p)),
                  pl.BlockSpec((8, DN_W), lambda i, p: (jnp.maximum(i * t8 - 1, 0), p)),
                  pl.BlockSpec((4, DN_W), lambda i, p: (0, p))],
        out_specs=[pl.BlockSpec((ROW_TILE, DN_W), lambda i, p: (i, p))] * 2,
        out_shape=[jax.ShapeDtypeStruct((s, 3 * DN_W), F32)] * 2,
        compiler_params=_cparams("parallel", "parallel"))(proj, proj, conv_w)


def _conv_bwd_act(c, dq, dk, dv):
    s = c.shape[0]

    def body(c_ref, dq_ref, dk_ref, dv_ref, dc_ref):
        for part, d_ref in enumerate((dq_ref, dk_ref, dv_ref)):
            for h in range(DN_HEADS):
                sl = slice(part * DN_W + h * DN_D, part * DN_W + (h + 1) * DN_D)
                ch = c_ref[:, sl]
                dyh = d_ref[:, h * DN_D:(h + 1) * DN_D]
                if part < 2:
                    ah = _silu(ch)
                    r = lax.rsqrt(jnp.sum(ah * ah, axis=-1, keepdims=True) + NORM_EPS)
                    dyh = r * dyh - ah * (r * r * r) * jnp.sum(dyh * ah, axis=-1, keepdims=True)
                dc_ref[:, sl] = dyh * _silu_grad(ch)

    wide = pl.BlockSpec((ROW_TILE, 3 * DN_W), lambda i: (i, 0))
    row = pl.BlockSpec((ROW_TILE, DN_W), lambda i: (i, 0))
    return pl.pallas_call(
        body, name="conv_bwd_act", grid=(s // ROW_TILE,), in_specs=[wide, row, row, row], out_specs=wide,
        out_shape=jax.ShapeDtypeStruct((s, 3 * DN_W), F32), compiler_params=_cparams("parallel"))(c, dq, dk, dv)


DPROJ_IN = pl.BlockSpec(memory_space=pl.ANY)


def _conv_bwd(proj, dc, conv_w, dproj):
    s = proj.shape[0]
    t8 = ROW_TILE // 8
    nrow = s // ROW_TILE
    last8 = s // 8 - 1

    def body(u_ref, dc_ref, dcn_ref, w_ref, dproj_in, du_ref, dw_ref):
        i = pl.program_id(1)
        cur = u_ref[...]
        dcv = dc_ref[...]
        next8 = jnp.where(i < nrow - 1, dcn_ref[...], 0.0)
        w = w_ref[...]

        @pl.when(i == 0)
        def _():
            dw_ref[...] = jnp.zeros_like(dw_ref)

        du = dcv * w[3:4, :]
        dw_ref[3:4, :] += jnp.sum(cur * dcv, axis=0, keepdims=True)
        for k in (1, 2, 3):
            ahead = _shift_up(dcv, next8, k)
            du = du + ahead * w[3 - k:4 - k, :]
            dw_ref[3 - k:4 - k, :] += jnp.sum(cur * ahead, axis=0, keepdims=True)
        du_ref[...] = du.astype(BF16)

    blk = pl.BlockSpec((ROW_TILE, DN_W), lambda p, i: (i, p))
    return pl.pallas_call(
        body, name="conv_bwd", grid=(3, nrow),
        in_specs=[blk, blk, pl.BlockSpec((8, DN_W), lambda p, i: (jnp.minimum((i + 1) * t8, last8), p)),
                  pl.BlockSpec((4, DN_W), lambda p, i: (0, p)), DPROJ_IN],
        out_specs=[blk, pl.BlockSpec((4, DN_W), lambda p, i: (0, p))],
        out_shape=[jax.ShapeDtypeStruct((s, PW), BF16), jax.ShapeDtypeStruct((4, 3 * DN_W), F32)],
        input_output_aliases={4: 0},
        compiler_params=_cparams("parallel", "arbitrary"))(proj, dc, dc, conv_w, dproj)


def _gates_fwd(proj, gate_par):
    s = proj.shape[0]

    def body(ba_ref, par_ref, o_ref):
        v = ba_ref[...]
        lane = lax.broadcasted_iota(jnp.int32, v.shape, 1)
        beta = _sigmoid(v)
        g = -jnp.exp(par_ref[0:1, :]) * _softplus(v + par_ref[1:2, :])
        o_ref[...] = jnp.where(lane < DN_HEADS, beta, jnp.where(lane < 2 * DN_HEADS, g, 0.0))

    return pl.pallas_call(
        body, name="gates_fwd", grid=(s // ROW_TILE,),
        in_specs=[pl.BlockSpec((ROW_TILE, 128), lambda i: (i, OFF_BA // 128)), pl.BlockSpec((8, 128), lambda i: (0, 0))],
        out_specs=pl.BlockSpec((ROW_TILE, 128), lambda i: (i, 0)),
        out_shape=jax.ShapeDtypeStruct((s, 128), F32), compiler_params=_cparams("parallel"))(proj, gate_par)


def _gates_bwd(proj, gate_par, dbg, dproj):
    s = proj.shape[0]

    def body(ba_ref, par_ref, d_ref, dproj_in, o_ref, dpar_ref):
        i = pl.program_id(0)
        v = ba_ref[...]
        dv = d_ref[...]
        lane = lax.broadcasted_iota(jnp.int32, v.shape, 1)
        beta = _sigmoid(v)
        nega = -jnp.exp(par_ref[0:1, :])
        xs = v + par_ref[1:2, :]
        dsp = dv * nega * _sigmoid(xs)
        dal = dv * nega * _softplus(xs)
        is_b = lane < DN_HEADS
        is_g = jnp.logical_and(lane >= DN_HEADS, lane < 2 * DN_HEADS)
        o_ref[:, :128] = jnp.where(is_b, dv * beta * (1.0 - beta), jnp.where(is_g, dsp, 0.0)).astype(BF16)
        o_ref[:, 128:] = jnp.zeros((ROW_TILE, PW - OFF_BA - 128), BF16)
        r0 = jnp.sum(jnp.where(is_g, dal, 0.0), axis=0, keepdims=True)
        r1 = jnp.sum(jnp.where(is_g, dsp, 0.0), axis=0, keepdims=True)

        @pl.when(i == 0)
        def _():
            dpar_ref[...] = jnp.zeros_like(dpar_ref)

        dpar_ref[0:1, :] += r0
        dpar_ref[1:2, :] += r1

    return pl.pallas_call(
        body, name="gates_bwd", grid=(s // ROW_TILE,),
        in_specs=[pl.BlockSpec((ROW_TILE, 128), lambda i: (i, OFF_BA // 128)), pl.BlockSpec((8, 128), lambda i: (0, 0)),
                  pl.BlockSpec((ROW_TILE, 128), lambda i: (i, 0)), DPROJ_IN],
        out_specs=[pl.BlockSpec((ROW_TILE, PW - OFF_BA), lambda i: (i, OFF_BA // (PW - OFF_BA))),
                   pl.BlockSpec((8, 128), lambda i: (0, 0))],
        out_shape=[jax.ShapeDtypeStruct((s, PW), BF16), jax.ShapeDtypeStruct((8, 128), F32)],
        input_output_aliases={3: 0},
        compiler_params=_cparams("arbitrary"))(proj, gate_par, dbg, dproj)


def _chunk_masks():
    c = DN_CHUNK
    ii = lax.broadcasted_iota(jnp.int32, (c, c), 0)
    jj = lax.broadcasted_iota(jnp.int32, (c, c), 1)
    return dict(ii=ii, jj=jj, lower=(ii >= jj), strict=(ii > jj), eye=(ii == jj),
                lower_f=(ii >= jj).astype(BF16), upper_f=(ii <= jj).astype(BF16), ones8=jnp.ones((8, c), BF16))


class _Heads:
    def __init__(self, xs):
        self.xs = list(xs)

    def _bin(self, o, f):
        if isinstance(o, _Heads):
            return _Heads([f(a, b) for a, b in zip(self.xs, o.xs)])
        return _Heads([f(a, o) for a in self.xs])

    def __add__(self, o):
        return self._bin(o, lambda a, b: a + b)

    def __sub__(self, o):
        return self._bin(o, lambda a, b: a - b)

    def __mul__(self, o):
        return self._bin(o, lambda a, b: a * b)

    __radd__ = __add__
    __rmul__ = __mul__

    def __neg__(self):
        return _Heads([-a for a in self.xs])

    def __getitem__(self, i):
        return _Heads([a[i] for a in self.xs])


def _hmap(f, *args):
    n = next(len(a.xs) for a in args if isinstance(a, _Heads))
    return _Heads([f(*[(a.xs[h] if isinstance(a, _Heads) else a) for a in args]) for h in range(n)])


def _hdot(a, b):
    return _hmap(_dot, a, b)


def _hdot_nt(a, b):
    return _hmap(_dot_nt, a, b)


def _hdot_tn(a, b):
    return _hmap(_dot_tn, a, b)


def _hcat(a, b, axis):
    return _hmap(lambda x, y: jnp.concatenate([x, y], axis=axis), a, b)


def _hsum(a, axis):
    return _hmap(lambda t: jnp.sum(t, axis=axis, keepdims=True), a)


def _hwhere(c, a, b):
    return _hmap(jnp.where, c, a, b)


def _chunk_gates(mk, bg):
    c = DN_CHUNK
    gc_all = _dot_exact_lhs(mk["lower_f"], bg)
    rows = jnp.concatenate([gc_all, gc_all], axis=0).T
    hs = range(DN_HEADS)
    return (_Heads(bg[:, h:h + 1] for h in hs), _Heads(gc_all[:, DN_HEADS + h:DN_HEADS + h + 1] for h in hs),
            _Heads(rows[DN_HEADS + h:DN_HEADS + h + 1, :] for h in hs))


def _chunk_common(mk, q, k, beta_col, gc_col, gc_r):
    c = DN_CHUNK
    lower, strict = mk["lower"], mk["strict"]
    qs = q * (DN_D ** -0.5)
    beta_b = _hmap(lambda t: jnp.broadcast_to(t, (c, DN_D)), beta_col)
    gc_b = _hmap(lambda t: jnp.broadcast_to(t, (c, DN_D)), gc_col)
    gc_sq = gc_b[:, :c]
    gam = _hwhere(lower, _hmap(lambda t: jnp.exp(jnp.minimum(t, 0.0)), gc_sq - gc_r[:, :c]), 0.0)
    egc = _hmap(jnp.exp, gc_b)
    gl = gc_b[c - 1:c, :]
    ekd = _hmap(jnp.exp, gl - gc_b)
    dl = _hmap(jnp.exp, gl)
    kb = k * beta_b
    scores = _hdot_nt(_hcat(kb, qs, 0), k)
    a_strict = _hwhere(strict, scores[:c] * gam, 0.0)
    aqk = _hwhere(lower, scores[c:] * gam, 0.0)
    return dict(k=k, qs=qs, beta_b=beta_b, gc_b=gc_b, gam=gam, egc=egc, ekd=ekd, dl=dl, kb=kb, a_strict=a_strict, aqk=aqk)


def _unit_lower_inverse_minus_eye(n_strict, ii, jj):
    same = lax.shift_right_logical(ii, 4) == lax.shift_right_logical(jj, 4)
    dmat = _hwhere(same, n_strict, 0.0)
    omat = n_strict - dmat
    d2 = _hdot(dmat, dmat)
    d4 = _hdot(d2, d2)
    d8 = _hdot(d4, d4)
    x1 = d2 - dmat - _hdot(dmat, d2)
    x2 = x1 + d4 + _hdot(x1, d4)
    x3 = x2 + d8 + _hdot(x2, d8)
    n1 = omat + _hdot(x3, omat)
    n2 = _hdot(n1, n1)
    y = n2 - n1 - _hdot(n1, n2)
    return y + x3 + _hdot(y, x3)


GDR_HEAD_SETS = (range(0, DN_HEADS),)


def _gdr_fwd(qkv, bg):
    s = qkv.shape[0]
    c = DN_CHUNK
    n = s // c

    def body(q_ref, k_ref, v_ref, bg_ref, o_ref, u_ref, w_ref, vn_ref, tm_ref, st_ref, state):
        @pl.when(pl.program_id(0) == 0)
        def _():
            state[...] = jnp.zeros_like(state)

        mk = _chunk_masks()
        gates = _chunk_gates(mk, bg_ref[...])
        for hs in GDR_HEAD_SETS:
            sls = [slice(h * DN_D, (h + 1) * DN_D) for h in hs]
            cm = _chunk_common(mk, _Heads(q_ref[:, sl] for sl in sls), _Heads(k_ref[:, sl] for sl in sls),
                               *[_Heads(g.xs[h] for h in hs) for g in gates])
            tm = _unit_lower_inverse_minus_eye(cm["a_strict"], mk["ii"], mk["jj"])
            rhs_u = _Heads(v_ref[:, sl] for sl in sls) * cm["beta_b"]
            rhs_w = cm["kb"] * cm["egc"]
            t_rhs = _hdot(tm, _hcat(rhs_u, rhs_w, 1))
            u = rhs_u + t_rhs[:, :DN_D]
            w = rhs_w + t_rhs[:, DN_D:]
            st = _Heads(state[h] for h in hs)
            on_state = _hdot(_hcat(w, cm["qs"] * cm["egc"], 0), st)
            v_new = u - on_state[:c]
            o = on_state[c:] + _hdot(cm["aqk"], v_new)
            st_new = st * cm["dl"] + _hdot_tn(cm["k"] * cm["ekd"], v_new)
            for i, (h, sl) in enumerate(zip(hs, sls)):
                o_ref[:, sl] = o.xs[i]
                u_ref[:, sl] = u.xs[i]
                w_ref[:, sl] = w.xs[i]
                vn_ref[:, sl] = v_new.xs[i]
                tm_ref[h, 0] = tm.xs[i]
                st_ref[h, 0] = st.xs[i]
                state[h] = st_new.xs[i]

    def part(p):
        return pl.BlockSpec((c, DN_W), lambda j: (j, p))

    return pl.pallas_call(
        body, name="gdr_fwd", grid=(n,),
        in_specs=[part(0), part(1), part(2), pl.BlockSpec((c, 128), lambda j: (j, 0))],
        out_specs=[part(0)] * 4 + [pl.BlockSpec((DN_HEADS, 1, c, c), lambda j: (0, j, 0, 0)),
                                   pl.BlockSpec((DN_HEADS, 1, DN_D, DN_D), lambda j: (0, j, 0, 0))],
        out_shape=[jax.ShapeDtypeStruct((s, DN_W), F32)] * 4
        + [jax.ShapeDtypeStruct((DN_HEADS, n, c, c), F32), jax.ShapeDtypeStruct((DN_HEADS, n, DN_D, DN_D), F32)],
        scratch_shapes=[pltpu.VMEM((DN_HEADS, DN_D, DN_D), F32)],
        compiler_params=_cparams("arbitrary"))(qkv, qkv, qkv, bg)


def _gdr_bwd(qkv, bg, u, w, vn, tmat, states, do):
    s = qkv.shape[0]
    c = DN_CHUNK
    n = s // c

    def body(q_ref, k_ref, v_ref, bg_ref, u_ref, w_ref, vn_ref, tm_ref, st_ref, do_ref,
             dq_ref, dk_ref, dv_ref, dbg_ref, dstate):
        @pl.when(pl.program_id(0) == 0)
        def _():
            dstate[...] = jnp.zeros_like(dstate)

        mk = _chunk_masks()
        lower, strict = mk["lower"], mk["strict"]
        bg = bg_ref[...]
        ones = jnp.ones((c, DN_D), BF16)
        rowi = lax.broadcasted_iota(jnp.int32, (c, DN_D), 0)
        lane = lax.broadcasted_iota(jnp.int32, (c, 128), 1)
        hs = range(DN_HEADS)
        sls = [slice(h * DN_D, (h + 1) * DN_D) for h in hs]

        def heads_of(ref):
            return _Heads(ref[:, sl] for sl in sls)

        cm = _chunk_common(mk, heads_of(q_ref), heads_of(k_ref), *_chunk_gates(mk, bg))
        k, qs, beta_b = cm["k"], cm["qs"], cm["beta_b"]
        gam, egc, ekd, dl, kb = cm["gam"], cm["egc"], cm["ekd"], cm["dl"], cm["kb"]
        aqk, a_strict = cm["aqk"], cm["a_strict"]
        v, uu, ww, v_new, dov = heads_of(v_ref), heads_of(u_ref), heads_of(w_ref), heads_of(vn_ref), heads_of(do_ref)
        st = _Heads(st_ref[h, 0] for h in hs)
        dsn = _Heads(dstate[h] for h in hs)
        qd = qs * egc
        kd = k * ekd

        dv_new = _hdot_tn(aqk, dov) + _hdot(kd, dsn)
        do_sv = _hdot_nt(dov, _hcat(st, v_new, 0))
        dqd = do_sv[:, :DN_D]
        daqk = _hwhere(lower, do_sv[:, DN_D:], 0.0)
        dkd = _hdot_nt(v_new, dsn)
        ddl = _hsum(_hsum(dsn * st, 1), 0)
        dw = -_hdot_nt(dv_new, st)
        ds_new = dsn * dl + _hdot_tn(_hcat(qd, -ww, 0), _hcat(dov, dv_new, 0))

        tm = _Heads(tm_ref[h, 0] for h in hs)
        tt = _hdot_tn(tm, _hcat(dv_new, dw, 1))
        dru = dv_new + tt[:, :DN_D]
        drw = dw + tt[:, DN_D:]
        dn = _hwhere(strict, -_hdot_nt(_hcat(dru, drw, 1), _hcat(uu, ww, 1)), 0.0)
        dag = dn * gam
        dqg = daqk * gam
        both = _hcat(dag, dqg, 0)
        on_k = _hdot(both, k)
        dkb = on_k[:c] + drw * egc
        dqs = on_k[c:] + dqd * egc
        dk = _hdot_tn(both, _hcat(kb, qs, 0)) + dkb * beta_b + dkd * ekd
        pmat = dn * a_strict + daqk * aqk
        tkd = _hsum(dkd * kd, -1)
        dgc = (_hsum(pmat, -1) - _hmap(_dot_tn_exact_rhs, pmat, ones) + _hsum(drw * (kb * egc), -1)
               + _hsum(dqd * qd, -1) - tkd)
        last = _hsum(tkd, 0) + ddl * dl
        dgc = dgc + _hwhere(rowi == c - 1, last, 0.0)
        dbeta = _hsum(dru * v, -1) + _hsum(dkb * k, -1)
        dq = dqs * (DN_D ** -0.5)
        dv = dru * beta_b

        dgc_all = jnp.zeros((c, 128), F32)
        dbg = jnp.zeros((c, 128), F32)
        for h, sl in zip(hs, sls):
            dq_ref[:, sl] = dq.xs[h]
            dk_ref[:, sl] = dk.xs[h]
            dv_ref[:, sl] = dv.xs[h]
            dstate[h] = ds_new.xs[h]
            dgc_all = dgc_all + jnp.where(lane == DN_HEADS + h, dgc.xs[h], 0.0)
            dbg = dbg + jnp.where(lane == h, dbeta.xs[h], 0.0)
        dbg_ref[...] = dbg + _dot_exact_lhs(mk["upper_f"], dgc_all)

    def part(p):
        return pl.BlockSpec((c, DN_W), lambda j: (n - 1 - j, p))

    vec = pl.BlockSpec((c, 128), lambda j: (n - 1 - j, 0))
    return pl.pallas_call(
        body, name="gdr_bwd", grid=(n,),
        in_specs=[part(0), part(1), part(2), vec, part(0), part(0), part(0),
                  pl.BlockSpec((DN_HEADS, 1, c, c), lambda j: (0, n - 1 - j, 0, 0)),
                  pl.BlockSpec((DN_HEADS, 1, DN_D, DN_D), lambda j: (0, n - 1 - j, 0, 0)), part(0)],
        out_specs=[part(0), part(0), part(0), vec],
        out_shape=[jax.ShapeDtypeStruct((s, DN_W), F32)] * 3 + [jax.ShapeDtypeStruct((s, 128), F32)],
        scratch_shapes=[pltpu.VMEM((DN_HEADS, DN_D, DN_D), F32)],
        compiler_params=_cparams("arbitrary"))(qkv, qkv, qkv, bg, u, w, vn, tmat, states, do)


def _gdr_out(o, proj, dnw):
    s = o.shape[0]

    def body(o_ref, z_ref, w_ref, y_ref, yt_ref):
        ov, zv, wv = o_ref[...], z_ref[...], w_ref[...]
        for h in range(DN_HEADS):
            sl = slice(h * DN_D, (h + 1) * DN_D)
            oh = ov[:, sl]
            r = lax.rsqrt(jnp.mean(oh * oh, axis=-1, keepdims=True) + NORM_EPS)
            y = (oh * r * wv) * _silu(zv[:, sl])
            y_ref[:, sl] = y.astype(BF16)
            yt_ref[sl, :] = y.T.astype(BF16)

    row = pl.BlockSpec((ROW_TILE, DN_W), lambda i: (i, 0))
    return pl.pallas_call(
        body, name="gdr_out", grid=(s // ROW_TILE,),
        in_specs=[row, pl.BlockSpec((ROW_TILE, DN_W), lambda i: (i, OFF_Z_A // DN_W)), pl.BlockSpec((1, DN_D), lambda i: (0, 0))],
        out_specs=[row, pl.BlockSpec((DN_W, ROW_TILE), lambda i: (0, i))],
        out_shape=[jax.ShapeDtypeStruct((s, DN_W), BF16), jax.ShapeDtypeStruct((DN_W, s), BF16)],
        compiler_params=_cparams("parallel"))(o, proj, dnw)


def _gdr_out_bwd(o, proj, dnw, dy):
    s = o.shape[0]

    def body(o_ref, z_ref, w_ref, dy_ref, do_ref, dz_ref, dw_ref):
        i = pl.program_id(0)
        ov, zv, wv, dyv = o_ref[...], z_ref[...], w_ref[...], dy_ref[...]
        acc = jnp.zeros((1, DN_D), F32)
        for h in range(DN_HEADS):
            sl = slice(h * DN_D, (h + 1) * DN_D)
            oh, zh, dh = ov[:, sl], zv[:, sl], dyv[:, sl]
            r = lax.rsqrt(jnp.mean(oh * oh, axis=-1, keepdims=True) + NORM_EPS)
            dn = dh * _silu(zh)
            dz_ref[:, sl] = (dh * (oh * r * wv) * _silu_grad(zh)).astype(BF16)
            acc = acc + jnp.sum(dn * oh * r, axis=0, keepdims=True)
            dnw_ = dn * wv
            do_ref[:, sl] = r * dnw_ - oh * (r * r * r) * jnp.mean(dnw_ * oh, axis=-1, keepdims=True)

        @pl.when(i == 0)
        def _():
            dw_ref[...] = acc

        @pl.when(i > 0)
        def _():
            dw_ref[...] += acc

    row = pl.BlockSpec((ROW_TILE, DN_W), lambda i: (i, 0))
    vec = pl.BlockSpec((1, DN_D), lambda i: (0, 0))
    return pl.pallas_call(
        body, name="gdr_out_bwd", grid=(s // ROW_TILE,),
        in_specs=[row, pl.BlockSpec((ROW_TILE, DN_W), lambda i: (i, OFF_Z_A // DN_W)), vec, row],
        out_specs=[row, pl.BlockSpec((ROW_TILE, DN_W), lambda i: (i, OFF_Z_A // DN_W)), vec],
        out_shape=[jax.ShapeDtypeStruct((s, DN_W), F32), jax.ShapeDtypeStruct((s, PW), BF16),
                   jax.ShapeDtypeStruct((1, DN_D), F32)],
        compiler_params=_cparams("arbitrary"))(o, proj, dnw, dy)


def _slope(group, head):
    idx = (group * DIL_HEADS + head + 1).astype(F32)
    return jnp.exp(jnp.full((1, 128), -8.0 * math.log(2.0) / (N_DIL * DIL_HEADS), F32) * idx)


def _att_scores(qb, k_cur, k_prev, slope_d, has_prev):
    iq = lax.broadcasted_iota(jnp.int32, (ATT_BLOCK, ATT_BLOCK), 0)
    jk = lax.broadcasted_iota(jnp.int32, (ATT_BLOCK, ATT_BLOCK), 1)
    dist_c = (iq - jk).astype(F32)
    s_cur = jnp.where(iq >= jk, _dot_nt(qb, k_cur) - slope_d * dist_c, NEG)
    s_prev = jnp.where(jnp.logical_and(jk >= iq, has_prev),
                       _dot_nt(qb, k_prev) - slope_d * (dist_c + float(ATT_BLOCK)), NEG)
    return s_cur, s_prev


def _att_scores_whole(qb, k, slope_d):
    n = 2 * ATT_BLOCK
    dist = lax.broadcasted_iota(jnp.int32, (n, n), 0) - lax.broadcasted_iota(jnp.int32, (n, n), 1)
    valid = jnp.logical_and(dist >= 0, dist <= ATT_BLOCK)
    return jnp.where(valid, _dot_nt(qb, k) - slope_d[:, 0:1] * dist.astype(F32), NEG)


def _att_tiles(i, dil, nb):
    tiles = nb // 2
    per = dil * tiles // ATT_UNROLL
    assert nb % 2 == 0 and tiles >= 2 and per * ATT_UNROLL == dil * tiles
    for i0 in range(per):
        ts = [divmod(i0 + u * per, tiles) for u in range(ATT_UNROLL)]
        assert all(a[0] != b[0] or abs(a[1] - b[1]) >= 2 for n, a in enumerate(ts) for b in ts[n + 1:])
    qrows, krows, has_prev = [], [], []
    for u in range(ATT_UNROLL):
        t = i + u * per
        r = lax.div(t, tiles)
        j = lax.rem(t, tiles)
        qbase = r + dil * 2 * ATT_BLOCK * j
        kbase = qbase - dil * ATT_BLOCK * jnp.minimum(j, 1)
        if dil == 1:
            qbase, kbase = pl.multiple_of(qbase, ATT_BLOCK), pl.multiple_of(kbase, ATT_BLOCK)
        qrows.append(pl.ds(qbase, 2 * ATT_BLOCK, stride=dil))
        krows.append(pl.ds(kbase, 3 * ATT_BLOCK, stride=dil))
        has_prev.append(j > 0)
    return qrows, krows, has_prev


def _att_scores_tile(qb, k, slope_d, has_prev):
    iq = lax.broadcasted_iota(jnp.int32, (2 * ATT_BLOCK, 3 * ATT_BLOCK), 0)
    ck = lax.broadcasted_iota(jnp.int32, (2 * ATT_BLOCK, 3 * ATT_BLOCK), 1)
    dist = iq - ck + jnp.where(has_prev, ATT_BLOCK, 0)
    valid = jnp.logical_and(dist >= 0, dist <= ATT_BLOCK)
    return jnp.where(valid, _dot_nt(qb, k) - slope_d[:, 0:1] * dist.astype(F32), NEG)


ATT_UNROLL = 4


def _att_blocks(i, dil, nb):
    per = dil * nb // ATT_UNROLL
    assert per * ATT_UNROLL == dil * nb
    for i0 in range(per):
        blocks = [divmod(i0 + u * per, nb) for u in range(ATT_UNROLL)]
        assert all(a[0] != b[0] or abs(a[1] - b[1]) >= 2 for n, a in enumerate(blocks) for b in blocks[n + 1:])
    curs, prvs, has_prev = [], [], []
    for u in range(ATT_UNROLL):
        t = i + u * per
        r = lax.div(t, nb)
        j = lax.rem(t, nb)
        base = r + dil * ATT_BLOCK * j
        pbase = base - dil * ATT_BLOCK * jnp.minimum(j, 1)
        if dil == 1:
            base, pbase = pl.multiple_of(base, ATT_BLOCK), pl.multiple_of(pbase, ATT_BLOCK)
        curs.append(pl.ds(base, ATT_BLOCK, stride=dil))
        prvs.append(pl.ds(pbase, ATT_BLOCK, stride=dil))
        has_prev.append(j > 0)
    return curs, prvs, has_prev


def _att_fwd(proj, group):
    s = proj.shape[0]
    dil = DIL_GROUPS[group][1]
    assert DIL_GROUPS[group][0] // dil == ATT_BLOCK
    nb = s // dil // ATT_BLOCK
    assert nb * dil * ATT_BLOCK == s

    def body(q_ref, k_ref, v_ref, num_ref, den_ref, mx_ref):
        slope_d = _slope(group, pl.program_id(0)) * float(dil)

        def step(i, carry):
            curs, prvs, has_prev = _att_blocks(i, dil, nb)
            us = range(ATT_UNROLL)
            qb = [q_ref[c, :] * (DIL_DH ** -0.5) for c in curs]
            sc = [_att_scores(qb[u], k_ref[curs[u], :], k_ref[prvs[u], :], slope_d, has_prev[u]) for u in us]
            mx = [jnp.maximum(jnp.max(a, axis=-1, keepdims=True), jnp.max(b, axis=-1, keepdims=True)) for a, b in sc]
            p_cur = [jnp.exp(sc[u][0] - mx[u]) for u in us]
            p_prev = [jnp.exp(sc[u][1] - mx[u]) for u in us]
            den = [jnp.sum(p_cur[u], axis=-1, keepdims=True) + jnp.sum(p_prev[u], axis=-1, keepdims=True) for u in us]
            num = [_dot(p_cur[u], v_ref[curs[u], :]) + _dot(p_prev[u], v_ref[prvs[u], :]) for u in us]
            for u in us:
                num_ref[curs[u], :] = num[u]
                den_ref[curs[u], :] = jnp.broadcast_to(den[u], (ATT_BLOCK, DIL_DH))
                mx_ref[curs[u], :] = jnp.broadcast_to(mx[u], (ATT_BLOCK, DIL_DH))
            return carry

        def step_whole(i, carry):
            rows = [pl.ds(i * ATT_UNROLL + u, 2 * ATT_BLOCK, stride=dil) for u in range(ATT_UNROLL)]
            sc = [_att_scores_whole(q_ref[r, :] * (DIL_DH ** -0.5), k_ref[r, :], slope_d) for r in rows]
            mx = [jnp.max(a, axis=-1, keepdims=True) for a in sc]
            p = [jnp.exp(a - m) for a, m in zip(sc, mx)]
            num = [_dot(pu, v_ref[r, :]) for pu, r in zip(p, rows)]
            for u, r in enumerate(rows):
                num_ref[r, :] = num[u]
                den_ref[r, :] = jnp.broadcast_to(jnp.sum(p[u], axis=-1, keepdims=True), (2 * ATT_BLOCK, DIL_DH))
                mx_ref[r, :] = jnp.broadcast_to(mx[u], (2 * ATT_BLOCK, DIL_DH))
            return carry

        def step_tile(i, carry):
            qrows, krows, has_prev = _att_tiles(i, dil, nb)
            us = range(ATT_UNROLL)
            sc = [_att_scores_tile(q_ref[qrows[u], :] * (DIL_DH ** -0.5), k_ref[krows[u], :], slope_d, has_prev[u]) for u in us]
            mx = [jnp.max(a, axis=-1, keepdims=True) for a in sc]
            p = [jnp.exp(a - m) for a, m in zip(sc, mx)]
            num = [_dot(p[u], v_ref[krows[u], :]) for u in us]
            for u in us:
                num_ref[qrows[u], :] = num[u]
                den_ref[qrows[u], :] = jnp.broadcast_to(jnp.sum(p[u], axis=-1, keepdims=True), (2 * ATT_BLOCK, DIL_DH))
                mx_ref[qrows[u], :] = jnp.broadcast_to(mx[u], (2 * ATT_BLOCK, DIL_DH))
            return carry

        if nb == 2:
            lax.fori_loop(0, dil // ATT_UNROLL, step_whole, 0)
        elif nb % 2 == 0:
            lax.fori_loop(0, dil * nb // 2 // ATT_UNROLL, step_tile, 0)
        else:
            lax.fori_loop(0, dil * nb // ATT_UNROLL, step, 0)

    def col(off):
        return pl.BlockSpec((s, DIL_DH), lambda h: (0, off // DIL_DH + group * DIL_HEADS + h))

    out = pl.BlockSpec((s, DIL_DH), lambda h: (0, h))
    return pl.pallas_call(
        body, name=f"att_fwd{group}", grid=(DIL_HEADS,), in_specs=[col(OFF_Q_B), col(OFF_K_B), col(OFF_V_B)],
        out_specs=[out, out, out], out_shape=[jax.ShapeDtypeStruct((s, DIL_W), F32)] * 3,
        compiler_params=_cparams("parallel"))(proj, proj, proj)


def _att_bwd(proj, group, do, lse, delta):
    s = proj.shape[0]
    dil = DIL_GROUPS[group][1]
    nb = s // dil // ATT_BLOCK

    def body(q_ref, k_ref, v_ref, do_ref, lse_ref, dl_ref, dq_ref, dk_ref, dv_ref, dq_acc, dk_acc, dv_acc):
        slope_d = _slope(group, pl.program_id(0)) * float(dil)
        dk_acc[...] = jnp.zeros_like(dk_acc)
        dv_acc[...] = jnp.zeros_like(dv_acc)

        def step(i, carry):
            curs, prvs, has_prev = _att_blocks(i, dil, nb)
            us = range(ATT_UNROLL)
            qb = [q_ref[c, :] * (DIL_DH ** -0.5) for c in curs]
            k_cur, k_prev = [k_ref[c, :] for c in curs], [k_ref[p, :] for p in prvs]
            v_cur, v_prev = [v_ref[c, :] for c in curs], [v_ref[p, :] for p in prvs]
            sc = [_att_scores(qb[u], k_cur[u], k_prev[u], slope_d, has_prev[u]) for u in us]
            lse_b, delta_b, dob = [lse_ref[c, :] for c in curs], [dl_ref[c, :] for c in curs], [do_ref[c, :] for c in curs]
            p_cur = [jnp.exp(sc[u][0] - lse_b[u]) for u in us]
            p_prev = [jnp.exp(sc[u][1] - lse_b[u]) for u in us]
            ds_cur = [p_cur[u] * (_dot_nt(dob[u], v_cur[u]) - delta_b[u]) for u in us]
            ds_prev = [p_prev[u] * (_dot_nt(dob[u], v_prev[u]) - delta_b[u]) for u in us]
            dq = [(_dot(ds_cur[u], k_cur[u]) + _dot(ds_prev[u], k_prev[u])) * (DIL_DH ** -0.5) for u in us]
            dk_c = [_dot_tn(ds_cur[u], qb[u]) for u in us]
            dv_c = [_dot_tn(p_cur[u], dob[u]) for u in us]
            dk_p = [_dot_tn(ds_prev[u], qb[u]) for u in us]
            dv_p = [_dot_tn(p_prev[u], dob[u]) for u in us]
            for u in us:
                dq_acc[curs[u], :] = dq[u]
                dk_acc[curs[u], :] += dk_c[u]
                dv_acc[curs[u], :] += dv_c[u]
            for u in us:
                dk_acc[prvs[u], :] += dk_p[u]
                dv_acc[prvs[u], :] += dv_p[u]
            return carry

        def step_whole(i, carry):
            rows = [pl.ds(i * ATT_UNROLL + u, 2 * ATT_BLOCK, stride=dil) for u in range(ATT_UNROLL)]
            qb = [q_ref[r, :] * (DIL_DH ** -0.5) for r in rows]
            kk, vv, dob = [k_ref[r, :] for r in rows], [v_ref[r, :] for r in rows], [do_ref[r, :] for r in rows]
            sc = [_att_scores_whole(qb[u], kk[u], slope_d) for u in range(ATT_UNROLL)]
            p = [jnp.exp(sc[u] - lse_ref[r, :][:, 0:1]) for u, r in enumerate(rows)]
            ds = [p[u] * (_dot_nt(dob[u], vv[u]) - dl_ref[r, :][:, 0:1]) for u, r in enumerate(rows)]
            dq = [_dot(ds[u], kk[u]) * (DIL_DH ** -0.5) for u in range(ATT_UNROLL)]
            dk = [_dot_tn(ds[u], qb[u]) for u in range(ATT_UNROLL)]
            dv = [_dot_tn(p[u], dob[u]) for u in range(ATT_UNROLL)]
            for u, r in enumerate(rows):
                dq_acc[r, :] = dq[u]
                dk_acc[r, :] = dk[u]
                dv_acc[r, :] = dv[u]
            return carry

        def step_tile(i, carry):
            qrows, krows, has_prev = _att_tiles(i, dil, nb)
            us = range(ATT_UNROLL)
            qb = [q_ref[r, :] * (DIL_DH ** -0.5) for r in qrows]
            kk, vv, dob = [k_ref[r, :] for r in krows], [v_ref[r, :] for r in krows], [do_ref[r, :] for r in qrows]
            sc = [_att_scores_tile(qb[u], kk[u], slope_d, has_prev[u]) for u in us]
            p = [jnp.exp(sc[u] - lse_ref[qrows[u], :][:, 0:1]) for u in us]
            ds = [p[u] * (_dot_nt(dob[u], vv[u]) - dl_ref[qrows[u], :][:, 0:1]) for u in us]
            dq = [_dot(ds[u], kk[u]) * (DIL_DH ** -0.5) for u in us]
            dk = [_dot_tn(ds[u], qb[u]) for u in us]
            dv = [_dot_tn(p[u], dob[u]) for u in us]
            for u in us:
                dq_acc[qrows[u], :] = dq[u]
                dk_acc[krows[u], :] += dk[u]
                dv_acc[krows[u], :] += dv[u]
            return carry

        if nb == 2:
            lax.fori_loop(0, dil // ATT_UNROLL, step_whole, 0)
        elif nb % 2 == 0:
            lax.fori_loop(0, dil * nb // 2 // ATT_UNROLL, step_tile, 0)
        else:
            lax.fori_loop(0, dil * nb // ATT_UNROLL, step, 0)
        dq_ref[...] = dq_acc[...].astype(BF16)
        dk_ref[...] = dk_acc[...].astype(BF16)
        dv_ref[...] = dv_acc[...].astype(BF16)

    def col(off):
        return pl.BlockSpec((s, DIL_DH), lambda h: (0, off // DIL_DH + group * DIL_HEADS + h))

    hd = pl.BlockSpec((s, DIL_DH), lambda h: (0, h))
    return pl.pallas_call(
        body, name=f"att_bwd{group}", grid=(DIL_HEADS,),
        in_specs=[col(OFF_Q_B), col(OFF_K_B), col(OFF_V_B), hd, hd, hd], out_specs=[hd, hd, hd],
        out_shape=[jax.ShapeDtypeStruct((s, DIL_W), BF16)] * 3,
        scratch_shapes=[pltpu.VMEM((s, DIL_DH), F32)] * 3,
        compiler_params=_cparams("parallel"))(proj, proj, proj, do, lse, delta)


def _att_merge(parts, proj):
    s = proj.shape[0]

    def body(n0, d0, m0, n1, d1, m1, n2, d2, m2, z_ref, ob_ref, o_ref, lse_ref, obt_ref):
        m = jnp.maximum(jnp.maximum(m0[...], m1[...]), m2[...])
        num = jnp.zeros_like(m)
        den = jnp.zeros_like(m)
        for nr, dr, mr in ((n0, d0, m0), (n1, d1, m1), (n2, d2, m2)):
            sc = jnp.exp(mr[...] - m)
            num = num + nr[...] * sc
            den = den + dr[...] * sc
        o = num / den
        o_ref[...] = o
        lse_ref[...] = m + jnp.log(den)
        ob = o * _silu(z_ref[...])
        ob_ref[...] = ob.astype(BF16)
        obt_ref[...] = ob.T.astype(BF16)

    row = pl.BlockSpec((ROW_TILE, DIL_W), lambda i: (i, 0))
    flat = [a for p in parts for a in p]
    return pl.pallas_call(
        body, name="att_merge", grid=(s // ROW_TILE,),
        in_specs=[row] * 9 + [pl.BlockSpec((ROW_TILE, DIL_W), lambda i: (i, OFF_Z_B // DIL_W))],
        out_specs=[row, row, row, pl.BlockSpec((DIL_W, ROW_TILE), lambda i: (0, i))],
        out_shape=[jax.ShapeDtypeStruct((s, DIL_W), BF16), jax.ShapeDtypeStruct((s, DIL_W), F32),
                   jax.ShapeDtypeStruct((s, DIL_W), F32), jax.ShapeDtypeStruct((DIL_W, s), BF16)],
        compiler_params=_cparams("parallel"))(*flat, proj)


def _att_merge_bwd(o, proj, dob, dproj):
    s = o.shape[0]

    def body(o_ref, z_ref, d_ref, dproj_in, do_ref, dl_ref, dz_ref):
        ov, zv, dv = o_ref[...], z_ref[...], d_ref[...]
        do = dv * _silu(zv)
        do_ref[...] = do
        dz_ref[...] = (dv * ov * _silu_grad(zv)).astype(BF16)
        for h in range(DIL_HEADS):
            sl = slice(h * DIL_DH, (h + 1) * DIL_DH)
            dl_ref[:, sl] = jnp.broadcast_to(jnp.sum(do[:, sl] * ov[:, sl], axis=-1, keepdims=True), (ROW_TILE, DIL_DH))

    row = pl.BlockSpec((ROW_TILE, DIL_W), lambda i: (i, 0))
    return pl.pallas_call(
        body, name="att_merge_bwd", grid=(s // ROW_TILE,),
        in_specs=[row, pl.BlockSpec((ROW_TILE, DIL_W), lambda i: (i, OFF_Z_B // DIL_W)), row, DPROJ_IN],
        out_specs=[row, row, pl.BlockSpec((ROW_TILE, DIL_W), lambda i: (i, OFF_Z_B // DIL_W))],
        out_shape=[jax.ShapeDtypeStruct((s, DIL_W), F32), jax.ShapeDtypeStruct((s, DIL_W), F32),
                   jax.ShapeDtypeStruct((s, PW), BF16)],
        input_output_aliases={3: 2},
        compiler_params=_cparams("parallel"))(o, proj, dob, dproj)


def _merge(proj, ya, yb):
    s = proj.shape[0]

    def body(ga_ref, gb_ref, ya_ref, yb_ref, o_ref, ot_ref):
        m = _sigmoid(ga_ref[...]) * ya_ref[...] + _sigmoid(gb_ref[...]) * yb_ref[...]
        o_ref[...] = m.astype(BF16)
        ot_ref[...] = m.T.astype(BF16)

    row = pl.BlockSpec((ROW_TILE, D_MODEL), lambda i: (i, 0))
    return pl.pallas_call(
        body, name="merge", grid=(s // ROW_TILE,),
        in_specs=[pl.BlockSpec((ROW_TILE, D_MODEL), lambda i: (i, OFF_G_A // D_MODEL)),
                  pl.BlockSpec((ROW_TILE, D_MODEL), lambda i: (i, OFF_G_B // D_MODEL)), row, row],
        out_specs=[row, pl.BlockSpec((D_MODEL, ROW_TILE), lambda i: (0, i))],
        out_shape=[jax.ShapeDtypeStruct((s, D_MODEL), BF16), jax.ShapeDtypeStruct((D_MODEL, s), BF16)],
        compiler_params=_cparams("parallel"))(proj, proj, ya, yb)


def _merge_bwd(proj, ya, yb, dm):
    s = proj.shape[0]

    def body(ga_ref, gb_ref, ya_ref, yb_ref, dm_ref, dya_ref, dyb_ref, dga_ref, dgb_ref):
        dmv = dm_ref[...]
        sa, sb = _sigmoid(ga_ref[...]), _sigmoid(gb_ref[...])
        dya_ref[...] = (dmv * sa).astype(BF16)
        dyb_ref[...] = (dmv * sb).astype(BF16)
        dga_ref[...] = (dmv * ya_ref[...] * sa * (1.0 - sa)).astype(BF16)
        dgb_ref[...] = (dmv * yb_ref[...] * sb * (1.0 - sb)).astype(BF16)

    row = pl.BlockSpec((ROW_TILE, D_MODEL), lambda i: (i, 0))
    return pl.pallas_call(
        body, name="merge_bwd", grid=(s // ROW_TILE,),
        in_specs=[pl.BlockSpec((ROW_TILE, D_MODEL), lambda i: (i, OFF_G_A // D_MODEL)),
                  pl.BlockSpec((ROW_TILE, D_MODEL), lambda i: (i, OFF_G_B // D_MODEL)), row, row, row],
        out_specs=[row] * 4, out_shape=[jax.ShapeDtypeStruct((s, D_MODEL), BF16)] * 4,
        compiler_params=_cparams("parallel"))(proj, proj, ya, yb, dm)


def _final(x, t, fw, tgt):
    s, d = x.shape

    def body(x_ref, t_ref, w_ref, y_ref, dx_ref, dw_ref, l_ref):
        i = pl.program_id(0)
        x2 = x_ref[...] + t_ref[...]
        wv = w_ref[...]
        r = lax.rsqrt(jnp.mean(x2 * x2, axis=-1, keepdims=True) + NORM_EPS)
        e = x2 * r * wv - y_ref[...]
        lrow = jnp.mean(e * e, axis=-1, keepdims=True)
        lpart = jnp.broadcast_to(0.5 * jnp.sum(lrow, axis=0, keepdims=True), (1, 128))
        dy = e * (1.0 / d)
        dwp = jnp.sum(dy * x2 * r, axis=0, keepdims=True)
        dyw = dy * wv
        dx_ref[...] = r * dyw - x2 * (r * r * r) * jnp.mean(dyw * x2, axis=-1, keepdims=True)

        @pl.when(i == 0)
        def _():
            dw_ref[...] = dwp
            l_ref[...] = lpart

        @pl.when(i > 0)
        def _():
            dw_ref[...] += dwp
            l_ref[...] += lpart

    row = pl.BlockSpec((ROW_TILE, d), lambda i: (i, 0))
    vec = pl.BlockSpec((1, d), lambda i: (0, 0))
    return pl.pallas_call(
        body, name="final", grid=(s // ROW_TILE,), in_specs=[row, row, vec, row],
        out_specs=[row, vec, pl.BlockSpec((1, 128), lambda i: (0, 0))],
        out_shape=[jax.ShapeDtypeStruct((s, d), F32), jax.ShapeDtypeStruct((1, d), F32), jax.ShapeDtypeStruct((1, 128), F32)],
        compiler_params=_cparams("arbitrary"))(x, t, fw, tgt)


def _adamw(w, g, m, v, name):
    r, c = w.shape
    cap = max(8, (1 << 18) // c)
    divisors = [t for t in range(8, min(r, cap) + 1, 8) if r % t == 0]
    tr = r if r <= 8 else (max(divisors) if divisors else cap)

    def body(w_ref, g_ref, m_ref, v_ref, d_ref, nm_ref, nv_ref):
        gv = g_ref[...]
        mn = ADAM_B1 * m_ref[...] + (1.0 - ADAM_B1) * gv
        vn = ADAM_B2 * v_ref[...] + (1.0 - ADAM_B2) * (gv * gv)
        m_hat = mn / (1.0 - ADAM_B1 ** ADAM_STEP)
        v_hat = vn / (1.0 - ADAM_B2 ** ADAM_STEP)
        d_ref[...] = -ADAM_LR * (m_hat / (jnp.sqrt(v_hat) + ADAM_EPS) + ADAM_WD * w_ref[...])
        nm_ref[...] = mn
        nv_ref[...] = vn

    blk = pl.BlockSpec((tr, c), lambda i: (i, 0))
    return pl.pallas_call(
        body, name=name, grid=(pl.cdiv(r, tr),), in_specs=[blk] * 4, out_specs=[blk] * 3,
        out_shape=[jax.ShapeDtypeStruct((r, c), F32)] * 3, compiler_params=_cparams("parallel"))(w, g, m, v)


HBM_SPEC = pl.BlockSpec(memory_space=pl.ANY)


def _place():
    x, y, c = lax.axis_index("x"), lax.axis_index("y"), lax.axis_index("c")
    chips = [(1 - x, y), (x, 1 - y), (1 - x, 1 - y)]
    return x, y, c, chips


def _ag_weights(packs):
    na = len(packs)
    nsem = 8

    def body(*refs):
        p_refs, out_refs = refs[:na], refs[na:2 * na]
        send_sems, recv_sems = refs[2 * na:]
        x, y, c, _ = _place()
        me, sib, j = (x, y, c), (x, y, 1 - c), 2 * x + y
        xn, yn = (1 - x, y, c), (x, 1 - y, c)
        jx, jy, jd = 2 * (1 - x) + y, 2 * x + (1 - y), 2 * (1 - x) + (1 - y)

        def rc(a, k, src, dst, to):
            return pltpu.make_async_remote_copy(src_ref=src, dst_ref=dst, send_sem=send_sems.at[nsem * a + k],
                                                recv_sem=recv_sems.at[nsem * a + k], device_id=to, device_id_type=MESH)

        sent = []
        for a in range(na):
            mine, land = p_refs[a].at[c], out_refs[a].at[j, c]
            sent += [rc(a, 0, mine, land, xn), rc(a, 1, mine, land, yn), rc(a, 7, p_refs[a], out_refs[a].at[j], sib)]
        for cp in sent:
            cp.start()
        for a in range(na):
            half = p_refs[a].shape[1] // 2
            top, bottom = pl.ds(0, half), pl.ds(half, half)
            from_x, from_y, from_d = out_refs[a].at[jx, c], out_refs[a].at[jy, c], out_refs[a].at[jd, c]
            rc(a, 0, p_refs[a].at[c], from_x, me).wait_recv()
            later = [rc(a, 2, from_x.at[top], from_x.at[top], yn), rc(a, 4, from_x, from_x, sib)]
            for cp in later:
                cp.start()
            sent += later
            rc(a, 1, p_refs[a].at[c], from_y, me).wait_recv()
            later = [rc(a, 3, from_y.at[bottom], from_y.at[bottom], xn), rc(a, 5, from_y, from_y, sib)]
            for cp in later:
                cp.start()
            sent += later
            rc(a, 2, from_d.at[top], from_d.at[top], me).wait_recv()
            rc(a, 3, from_d.at[bottom], from_d.at[bottom], me).wait_recv()
            cp = rc(a, 6, from_d, from_d, sib)
            cp.start()
            sent.append(cp)
        for a in range(na):
            for k, jj in ((4, jx), (5, jy), (6, jd)):
                rc(a, k, p_refs[a].at[c], out_refs[a].at[jj, 1 - c], me).wait_recv()
            rc(a, 7, p_refs[a], out_refs[a].at[j], me).wait_recv()
        for cp in sent:
            cp.wait_send()

    return pl.pallas_call(
        body, name="ag_weights",
        out_shape=[jax.ShapeDtypeStruct((N_CHIPS,) + p.shape, p.dtype) for p in packs],
        in_specs=[HBM_SPEC] * na, out_specs=[HBM_SPEC] * na,
        scratch_shapes=[pltpu.SemaphoreType.DMA((nsem * na,)), pltpu.SemaphoreType.DMA((nsem * na,))])(*packs)


def _rs_pair(dwpt, gpack):
    n = N_CHIPS
    hw = SHARD_PAD // 2

    def body(d_ref, g_ref, out_d, out_g, send_sems, recv_sems):
        x, y, c, _ = _place()
        sib = (x, y, 1 - c)
        cps = []
        for p in range(n):
            start = pl.multiple_of(WIN_BASE[p] + (1 - c) * hw, TILE_ROWS)
            cps.append(pltpu.make_async_remote_copy(
                src_ref=d_ref.at[pl.ds(start, hw)], dst_ref=out_d.at[p], send_sem=send_sems.at[p],
                recv_sem=recv_sems.at[p], device_id=sib, device_id_type=MESH))
            cps.append(pltpu.make_async_remote_copy(
                src_ref=g_ref.at[p, 1 - c], dst_ref=out_g.at[p], send_sem=send_sems.at[n + p],
                recv_sem=recv_sems.at[n + p], device_id=sib, device_id_type=MESH))
        for cp in cps:
            cp.start()
        for cp in cps:
            cp.wait_recv()
        for cp in cps:
            cp.wait_send()

    return pl.pallas_call(
        body, name="rs_pair",
        out_shape=[jax.ShapeDtypeStruct((n, hw, dwpt.shape[1]), dwpt.dtype),
                   jax.ShapeDtypeStruct((n,) + gpack.shape[2:], gpack.dtype)],
        in_specs=[HBM_SPEC] * 2, out_specs=[HBM_SPEC] * 2,
        scratch_shapes=[pltpu.SemaphoreType.DMA((2 * n,)), pltpu.SemaphoreType.DMA((2 * n,))])(dwpt, gpack)


def _add_halves_win(dwpt, other, c):
    n, rh, wd = other.shape
    tr = _row_tile(rh)

    def body(s_ref, d_ref, o_ref, out_ref):
        out_ref[0] = (d_ref[...] + o_ref[0]).astype(BF16)

    scal = jnp.concatenate([jnp.reshape(c, (1,)).astype(jnp.int32), jnp.asarray(WIN_BASE, jnp.int32)])
    grid_spec = pltpu.PrefetchScalarGridSpec(
        num_scalar_prefetch=1, grid=(n, rh // tr),
        in_specs=[pl.BlockSpec((pl.Element(tr), pl.Element(wd)),
                               lambda p, i, sr: (pl.multiple_of(sr[1 + p] + sr[0] * rh + i * tr, TILE_ROWS), 0)),
                  pl.BlockSpec((1, tr, wd), lambda p, i, sr: (p, i, 0))],
        out_specs=pl.BlockSpec((1, tr, wd), lambda p, i, sr: (p, i, 0)))
    return pl.pallas_call(
        body, name="add_halves_in", grid_spec=grid_spec, out_shape=jax.ShapeDtypeStruct((n, rh, wd), BF16),
        compiler_params=_cparams("parallel", "parallel"))(scal, dwpt, other)


SEM_SPEC = pl.BlockSpec(memory_space=pltpu.SEMAPHORE)
DATAFLOW_EFFECT = pltpu.SideEffectType.DATAFLOW_SIDE_EFFECTING


def _rs_chips_start(csums):
    na = len(csums)

    def body(*refs):
        s_refs, land_refs = refs[:na], refs[na:2 * na]
        send_sems, recv_sems = refs[2 * na], refs[2 * na + 1]
        token = refs[-1]
        x, y, c, chips = _place()
        j = 2 * x + y
        for a in range(na):
            for k, (cx, cy) in enumerate(chips):
                pltpu.make_async_remote_copy(src_ref=s_refs[a].at[2 * cx + cy], dst_ref=land_refs[a].at[j],
                                             send_sem=send_sems.at[3 * a + k], recv_sem=recv_sems.at[3 * a + k],
                                             device_id=(cx, cy, c), device_id_type=MESH).start()
        token[...] = jnp.zeros_like(token)

    hbm = [pltpu.HBM(s.shape, s.dtype) for s in csums]
    args = [pltpu.with_memory_space_constraint(s, pltpu.HBM) for s in csums]
    args += [pltpu.with_memory_space_constraint(lax.empty(s.shape, s.dtype), pltpu.HBM) for s in csums]
    res = pl.pallas_call(
        body, name="rs_chips_start",
        out_shape=(pltpu.SemaphoreType.DMA((3 * na,)), pltpu.SemaphoreType.DMA((3 * na,)), *hbm, *hbm,
                   jax.ShapeDtypeStruct((8, 128), F32)),
        in_specs=[pl.BlockSpec(memory_space=pltpu.HBM)] * (2 * na),
        out_specs=(SEM_SPEC, SEM_SPEC, *[pl.BlockSpec(memory_space=pltpu.HBM)] * (2 * na),
                   pl.BlockSpec(memory_space=pltpu.VMEM)),
        input_output_aliases={i: 2 + i for i in range(2 * na)},
        compiler_params=pltpu.CompilerParams(has_side_effects=DATAFLOW_EFFECT))(*args)
    return res[0], res[1], list(res[2:2 + na]), list(res[2 + na:2 + 2 * na]), res[-1]


def _rs_chips_wait(send_sems, recv_sems, csums, lands, after):
    na = len(csums)

    def body(*refs):
        s_refs, land_refs = refs[:na], refs[na:2 * na]
        send_sems, recv_sems = refs[2 * na], refs[2 * na + 1]
        x, y, c, chips = _place()
        j = 2 * x + y
        for a in range(na):
            for k, (cx, cy) in enumerate(chips):
                cp = pltpu.make_async_remote_copy(src_ref=s_refs[a].at[2 * cx + cy], dst_ref=land_refs[a].at[2 * cx + cy],
                                                  send_sem=send_sems.at[3 * a + k], recv_sem=recv_sems.at[3 * a + k],
                                                  device_id=(cx, cy, c), device_id_type=MESH)
                cp.wait_send()
                cp.wait_recv()

    hbm = [pltpu.HBM(s.shape, s.dtype) for s in csums]
    res = pl.pallas_call(
        body, name="rs_chips_wait", out_shape=(*hbm, *hbm),
        in_specs=[pl.BlockSpec(memory_space=pltpu.HBM)] * (2 * na) + [SEM_SPEC, SEM_SPEC, pl.BlockSpec(memory_space=pl.ANY)],
        out_specs=tuple([pl.BlockSpec(memory_space=pltpu.HBM)] * (2 * na)),
        input_output_aliases={i: i for i in range(2 * na)},
        compiler_params=pltpu.CompilerParams(has_side_effects=DATAFLOW_EFFECT))(*csums, *lands, send_sems, recv_sems, after)
    return list(res[:na]), list(res[na:])


SWAP_CHUNKS = 4


def _pair_swap(halves):
    na = len(halves)

    def body(*refs):
        h_refs, out_refs = refs[:na], refs[na:2 * na]
        send_sems, recv_sems = refs[2 * na:]
        x, y, c, _ = _place()
        cps = []
        for a in range(na):
            rows = h_refs[a].shape[0] // SWAP_CHUNKS
            assert rows * SWAP_CHUNKS == h_refs[a].shape[0]
            for q in range(SWAP_CHUNKS):
                k = SWAP_CHUNKS * a + q
                cps.append(pltpu.make_async_remote_copy(
                    src_ref=h_refs[a].at[pl.ds(q * rows, rows)], dst_ref=out_refs[a].at[pl.ds(q * rows, rows)],
                    send_sem=send_sems.at[k], recv_sem=recv_sems.at[k], device_id=(x, y, 1 - c), device_id_type=MESH))
        for cp in cps:
            cp.start()
        for cp in cps:
            cp.wait_recv()
        for cp in cps:
            cp.wait_send()

    return pl.pallas_call(
        body, name="pair_swap", out_shape=[jax.ShapeDtypeStruct(h.shape, h.dtype) for h in halves],
        in_specs=[HBM_SPEC] * na, out_specs=[HBM_SPEC] * na,
        scratch_shapes=[pltpu.SemaphoreType.DMA((SWAP_CHUNKS * na,)), pltpu.SemaphoreType.DMA((SWAP_CHUNKS * na,))])(*halves)


def _ag_small(v):
    m_per, n = v.shape

    def body(x_ref, out_ref, send_sems, recv_sems, local_sem):
        x, y, c, chips = _place()
        me, sibling = (x, y, c), (x, y, 1 - c)

        def rows(px, py, pc):
            return out_ref.at[pl.ds((4 * px + 2 * py + pc) * m_per, m_per), :]

        def copy(k, block, to, src=None):
            return pltpu.make_async_remote_copy(
                src_ref=rows(*block) if src is None else src, dst_ref=rows(*block), send_sem=send_sems.at[k],
                recv_sem=recv_sems.at[k], device_id=to, device_id_type=MESH)

        mine = pltpu.make_async_copy(x_ref, rows(*me), local_sem)
        mine.start()
        first = [copy(0, me, sibling, src=x_ref)]
        first += [copy(1 + k, me, (*chip, c), src=x_ref) for k, chip in enumerate(chips)]
        for cp in first:
            cp.start()
        passed = [copy(4 + k, (*chip, c), sibling) for k, chip in enumerate(chips)]
        for k, chip in enumerate(chips):
            copy(1 + k, (*chip, c), me).wait_recv()
            passed[k].start()
        copy(0, sibling, me).wait_recv()
        for k, chip in enumerate(chips):
            copy(4 + k, (*chip, 1 - c), me).wait_recv()
        for cp in first + passed:
            cp.wait_send()
        mine.wait()

    return pl.pallas_call(
        body, name="ag_small", out_shape=jax.ShapeDtypeStruct((8 * m_per, n), v.dtype),
        in_specs=[pl.BlockSpec(memory_space=pltpu.VMEM)], out_specs=pl.BlockSpec(memory_space=pltpu.VMEM),
        scratch_shapes=[pltpu.SemaphoreType.DMA((7,)), pltpu.SemaphoreType.DMA((7,)), pltpu.SemaphoreType.DMA])(v)


def _sum_blocks(a, nblk, name):
    rows, wd = a.shape
    r = rows // nblk
    tr = min(r, ROW_TILE)
    assert r % tr == 0

    def body(*refs):
        acc = refs[0][...].astype(F32)
        for ref in refs[1:nblk]:
            acc = acc + ref[...].astype(F32)
        refs[nblk][...] = acc

    nt = r // tr
    return pl.pallas_call(
        body, name=name, grid=(nt,),
        in_specs=[pl.BlockSpec((tr, wd), functools.partial(lambda i, b: (b * nt + i, 0), b=b)) for b in range(nblk)],
        out_specs=pl.BlockSpec((tr, wd), lambda i: (i, 0)),
        out_shape=jax.ShapeDtypeStruct((r, wd), F32), compiler_params=_cparams("parallel"))(*([a] * nblk))


def _row_tile(rows):
    best = max(t for t in range(16, 513, 16) if rows % t == 0)
    return best


def _sum_chips(by_src, csum, j, name):
    n, rh, wd = by_src.shape
    tr = _row_tile(rh)

    def body(j_ref, *refs):
        own = refs[n][0].astype(F32)
        acc = None
        for k in range(n):
            term = jnp.where(j_ref[0] == k, own, refs[k][0].astype(F32))
            acc = term if acc is None else acc + term
        refs[n + 1][...] = acc

    def other(k):
        return pl.BlockSpec((1, tr, wd), lambda i, jr: (jnp.where(jr[0] == k, (k + 1) % n, k), i, 0))

    grid_spec = pltpu.PrefetchScalarGridSpec(
        num_scalar_prefetch=1, grid=(rh // tr,),
        in_specs=[other(k) for k in range(n)] + [pl.BlockSpec((1, tr, wd), lambda i, jr: (jr[0], i, 0))],
        out_specs=pl.BlockSpec((tr, wd), lambda i, jr: (i, 0)))
    return pl.pallas_call(
        body, name=name, grid_spec=grid_spec, out_shape=jax.ShapeDtypeStruct((rh, wd), F32),
        compiler_params=_cparams("parallel"))(jnp.reshape(j, (1,)).astype(jnp.int32), *([by_src] * n), csum)


def _add_halves(gpack, other, c, name):
    n, _, rh, wd = gpack.shape
    tr = _row_tile(rh)

    def body(c_ref, g_ref, o_ref, out_ref):
        out_ref[0] = (g_ref[0, 0] + o_ref[0]).astype(BF16)

    grid_spec = pltpu.PrefetchScalarGridSpec(
        num_scalar_prefetch=1, grid=(n, rh // tr),
        in_specs=[pl.BlockSpec((1, 1, tr, wd), lambda p, i, cr: (p, cr[0], i, 0)),
                  pl.BlockSpec((1, tr, wd), lambda p, i, cr: (p, i, 0))],
        out_specs=pl.BlockSpec((1, tr, wd), lambda p, i, cr: (p, i, 0)))
    return pl.pallas_call(
        body, name=name, grid_spec=grid_spec, out_shape=jax.ShapeDtypeStruct((n, rh, wd), BF16),
        compiler_params=_cparams("parallel", "parallel"))(jnp.reshape(c, (1,)).astype(jnp.int32), gpack, other)


PACK_W = 1024
ROWS_O_DN = DN_W // N_CHIPS
ROWS_O_DIL = DIL_W * (D_MODEL // N_CHIPS) // PACK_W
ROWS_OUT = D_MODEL // N_CHIPS
ROWS_CONV = 4 * (3 * DN_W // N_CHIPS) // PACK_W
R1 = ROWS_O_DN
R2 = R1 + ROWS_O_DIL
R3 = R2 + ROWS_OUT
R4 = R3 + 16
R5 = R4 + 16
PACK_ROWS = 704
HALF_ROWS = PACK_ROWS // 2
SHARD_PAD = 2880


R6 = R5 + 2 * DN_HEADS

TILE_ROWS = 16
BA_IN_SHARD1 = REF_OFF_BA - SHARD_W
LOCAL_START = (0, SHARD_W, 2 * SHARD_W - 2 * DN_HEADS, 3 * SHARD_W - 2 * DN_HEADS)
LOCAL_END = LOCAL_START[1:] + (OFF_BA,)
WIN_BASE = tuple(s // TILE_ROWS * TILE_ROWS for s in LOCAL_START)


def _to_window(k, shard):
    nba = 2 * DN_HEADS
    body = shard
    if k == 1:
        row = lax.broadcasted_iota(jnp.int32, (SHARD_W - nba, 1), 0)
        body = jnp.where(row < BA_IN_SHARD1, shard[:SHARD_W - nba], shard[nba:])
    lead = LOCAL_START[k] - WIN_BASE[k]
    return jnp.pad(body, ((lead, SHARD_PAD - lead - body.shape[0]), (0, 0)))


def _from_window(k, win, ba):
    nba = 2 * DN_HEADS
    lead = LOCAL_START[k] - WIN_BASE[k]
    if k != 1:
        return win[lead:lead + SHARD_W]
    row = lax.broadcasted_iota(jnp.int32, (SHARD_W, 1), 0)
    before = win[lead:lead + SHARD_W]
    after = jnp.pad(win, ((nba, 0), (0, 0)))[lead:lead + SHARD_W]
    mid = jnp.pad(ba, ((BA_IN_SHARD1, SHARD_W - BA_IN_SHARD1 - nba), (0, 0)))
    return jnp.where(row < BA_IN_SHARD1, before, jnp.where(row < BA_IN_SHARD1 + nba, mid, after))


def _stack_windows(wins, ba):
    pieces = []
    for k in range(N_CHIPS):
        lo = WIN_BASE[k] + (TILE_ROWS if k else 0)
        hi = LOCAL_END[k] // TILE_ROWS * TILE_ROWS
        pieces.append(wins[k][lo - WIN_BASE[k]:hi - WIN_BASE[k]])
        if k + 1 < N_CHIPS:
            assert hi == WIN_BASE[k + 1]
            pieces.append(wins[k][hi - WIN_BASE[k]:hi - WIN_BASE[k] + TILE_ROWS] + wins[k + 1][:TILE_ROWS])
    pieces += [ba, jnp.zeros((PW - OFF_BA - ba.shape[0], ba.shape[1]), ba.dtype)]
    out = jnp.concatenate(pieces, axis=0)
    assert out.shape[0] == PW
    return out


def _to_ref_layout(wpt):
    return jnp.concatenate([wpt[:REF_OFF_BA], wpt[OFF_BA:OFF_BA + 2 * DN_HEADS], wpt[REF_OFF_BA:OFF_BA]], axis=0)


def _from_ref_layout(wt):
    pad = jnp.zeros((PW - PROJ_W, wt.shape[1]), wt.dtype)
    return jnp.concatenate([wt[:REF_OFF_BA], wt[REF_OFF_BA + 2 * DN_HEADS:], wt[REF_OFF_BA:REF_OFF_BA + 2 * DN_HEADS], pad],
                           axis=0)


def _local_step(x, tgt, norm_w, wpt, conv_full, a_log, dt_bias, dn_norm_w, w_o_dn, w_o_dil, w_out, final_norm_w):
    s = x.shape[0]
    h, h_t = _rms_in(x, norm_w)
    proj = _matmul(h, wpt, F32, 2048, 1280, 1024, "proj", nt=True)
    c_pre, qkv = _conv_fwd(proj, conv_full)
    gate_par = jnp.zeros((8, 128), F32).at[0, 8:16].set(a_log[0]).at[1, 8:16].set(dt_bias[0])
    bg = _gates_fwd(proj, gate_par)
    o_a, u, w, vn, tmat, states = _gdr_fwd(qkv, bg)
    oa2, oa2_t = _gdr_out(o_a, proj, dn_norm_w)
    ya = _matmul(oa2, w_o_dn, F32, 512, 1024, 1024, "ya")
    parts = [_att_fwd(proj, g) for g in range(N_DIL)]
    ob, o_att, lse, ob_t = _att_merge(parts, proj)
    yb = _matmul(ob, w_o_dil, F32, 512, 1024, 512, "yb")
    mg, mg_t = _merge(proj, ya, yb)
    t = _matmul(mg, w_out, F32, 512, 1024, 1024, "t_out")
    dx2, dfw, lpart = _final(x, t, final_norm_w, tgt)

    dmg = _matmul(dx2, w_out, F32, 512, 1024, 1024, "d_merged", nt=True)
    dw_out = _matmul(mg_t, dx2, F32, 1024, 1024, 1024, "dw_out")
    dya, dyb, dga, dgb = _merge_bwd(proj, ya, yb, dmg)
    doa2 = _matmul(dya, w_o_dn, F32, 512, 1024, 1024, "d_oa2", nt=True)
    dw_o_dn = _matmul(oa2_t, dya, F32, 1024, 1024, 1024, "dw_o_dn")
    dob = _matmul(dyb, w_o_dil, F32, 512, 512, 1024, "d_ob", nt=True)
    dw_o_dil = _matmul(ob_t, dyb, F32, 512, 1024, 1024, "dw_o_dil")
    do_a, dproj, ddnw = _gdr_out_bwd(o_a, proj, dn_norm_w, doa2)
    dq_a, dk_a, dv_a, dbg = _gdr_bwd(qkv, bg, u, w, vn, tmat, states, do_a)
    dproj, dpar = _gates_bwd(proj, gate_par, dbg, dproj)
    dc = _conv_bwd_act(c_pre, dq_a, dk_a, dv_a)
    dproj, dconv = _conv_bwd(proj, dc, conv_full, dproj)
    do_att, delta, dproj = _att_merge_bwd(o_att, proj, dob, dproj)
    dqkv_b = [_att_bwd(proj, g, do_att, lse, delta) for g in range(N_DIL)]
    pieces = [(OFF_Q_B + (N_DIL * i + g) * DIL_W, dqkv_b[g][i]) for i in range(3) for g in range(N_DIL)]
    for off, piece in pieces + [(OFF_G_A, dga), (OFF_G_B, dgb)]:
        dproj = lax.dynamic_update_slice(dproj, piece, (0, off))
    dwpt, dwpt_b = _matmul(h_t, dproj, F32, 1024, 1280, 2048, "dw_in", transpose_out=True, also_bf16=True)

    def finish(after=None):
        dh = _matmul(dproj, wpt, F32, 1024, 1024, 3840, "d_h", after=after)
        grad_x, dnw = _rms_in_bwd(x, norm_w, dh, dx2)
        small = jnp.zeros((8, PACK_W), F32)
        small = small.at[0].set(dnw[0]).at[1].set(dfw[0]).at[2, :DN_D].set(ddnw[0])
        small = small.at[3, :DN_HEADS].set(dpar[0, 8:16]).at[3, DN_HEADS:2 * DN_HEADS].set(dpar[1, 8:16])
        small = small.at[4, 0].set(lpart[0, 0])
        return grad_x, small

    return finish, (dwpt, dwpt_b), dconv, dw_o_dn, dw_o_dil, dw_out


def kernel(x, norm_w, w_in, conv_w, a_log, dt_bias, dn_norm_w, w_o_dn, w_o_dil, w_out, final_norm_w, loss_target, m_norm_w, m_w_in, m_conv_w, m_a_log, m_dt_bias, m_dn_norm_w, m_w_o_dn, m_w_o_dil, m_w_out, m_final_norm_w, v_norm_w, v_w_in, v_conv_w, v_a_log, v_dt_bias, v_dn_norm_w, v_w_o_dn, v_w_o_dil, v_w_out, v_final_norm_w):
    c = lax.axis_index("c")
    j = 2 * lax.axis_index("x") + lax.axis_index("y")
    qw = D_MODEL // N_CHIPS

    cw = conv_w[0].reshape(ROWS_CONV, PACK_W)
    cw = jnp.pad(cw, ((0, 16 - ROWS_CONV), (0, 0)))
    cw_hi = cw.astype(BF16)
    cw_lo = (cw - cw_hi.astype(F32)).astype(BF16)
    shard = w_in[0].T.astype(BF16)
    own_ba = jnp.where(j == 1, shard[BA_IN_SHARD1:BA_IN_SHARD1 + 2 * DN_HEADS], jnp.zeros((2 * DN_HEADS, D_MODEL), BF16))
    pack = jnp.concatenate(
        [w_o_dn[0].astype(BF16), w_o_dil[0].astype(BF16).reshape(ROWS_O_DIL, PACK_W), w_out[0].astype(BF16), cw_hi, cw_lo,
         own_ba, jnp.zeros((PACK_ROWS - R6, PACK_W), BF16)], axis=0).reshape(2, HALF_ROWS, PACK_W)
    chips = range(N_CHIPS)
    own_win = lax.switch(j, [functools.partial(_to_window, k) for k in chips], shard).reshape(2, SHARD_PAD // 2, D_MODEL)
    all_in, allw = _ag_weights([own_win, pack])
    wins = [all_in[k].reshape(SHARD_PAD, D_MODEL) for k in chips]
    allw = [allw[k].reshape(PACK_ROWS, PACK_W) for k in chips]
    wpt = _stack_windows(wins, allw[1][R5:R6])
    w_o_dn_full = jnp.concatenate([allw[k][:R1] for k in chips], axis=0)
    w_o_dil_full = jnp.concatenate([allw[k][R1:R2].reshape(DIL_W, qw) for k in chips], axis=1)
    w_out_full = jnp.concatenate([allw[k][R2:R3] for k in chips], axis=0)
    conv_full = jnp.concatenate(
        [(allw[k][R3:R3 + ROWS_CONV].astype(F32) + allw[k][R4:R4 + ROWS_CONV].astype(F32)).reshape(4, 3 * DN_W // N_CHIPS)
         for k in chips], axis=1)

    finish, (dwpt, dwpt_b), dconv, dw_o_dn, dw_o_dil, dw_out = _local_step(
        x[0], loss_target[0], norm_w, wpt, conv_full, a_log, dt_bias, dn_norm_w, w_o_dn_full, w_o_dil_full, w_out_full,
        final_norm_w.reshape(1, D_MODEL))

    cq = 3 * DN_W // N_CHIPS
    gpack = jnp.stack([
        jnp.concatenate(
            [dw_o_dn[k * qw:(k + 1) * qw], dw_o_dil[:, k * qw:(k + 1) * qw].reshape(ROWS_O_DIL, PACK_W),
             dw_out[k * qw:(k + 1) * qw],
             jnp.pad(dconv[:, k * cq:(k + 1) * cq].reshape(ROWS_CONV, PACK_W), ((0, 16 - ROWS_CONV), (0, 0))),
             dwpt[OFF_BA:OFF_BA + 2 * DN_HEADS] if k == 1 else jnp.zeros((2 * DN_HEADS, PACK_W), F32),
             jnp.zeros((PACK_ROWS - R4 - 2 * DN_HEADS, PACK_W), F32)], axis=0)
        for k in chips]).reshape(N_CHIPS, 2, HALF_ROWS, PACK_W)
    sib_in, sib_pack = _rs_pair(dwpt_b, gpack)
    csum_in = _add_halves_win(dwpt, sib_in, c)
    csum_pack = _add_halves(gpack, sib_pack, c, "add_halves_pack")
    send_sems, recv_sems, csums, lands, token = _rs_chips_start([csum_in, csum_pack])
    grad_x, small = finish(after=token)

    gs = _sum_blocks(_ag_small(small), 8, "sum_small")
    loss = gs[4, 0]
    w_small = jnp.zeros((8, PACK_W), F32)

    def pack_small(nw, fw, dnw_, al, db):
        t = w_small.at[0].set(nw[0]).at[1].set(fw).at[2, :DN_D].set(dnw_[0])
        return t.at[3, :DN_HEADS].set(al[0]).at[3, DN_HEADS:2 * DN_HEADS].set(db[0])

    sm = _adamw(pack_small(norm_w, final_norm_w, dn_norm_w, a_log, dt_bias), gs,
                pack_small(m_norm_w, m_final_norm_w, m_dn_norm_w, m_a_log, m_dt_bias),
                pack_small(v_norm_w, v_final_norm_w, v_dn_norm_w, v_a_log, v_dt_bias), "adamw_small")

    (csum_in, csum_pack), (src_in, src_pack) = _rs_chips_wait(send_sems, recv_sems, csums, lands, sm[0])
    half_in = _sum_chips(src_in, csum_in, j, "sum_chips_in")
    half_pack = _sum_chips(src_pack, csum_pack, j, "sum_chips_pack")
    sib_half_in, sib_half_pack = _pair_swap([half_in, half_pack])

    def both_halves(mine, theirs):
        return jnp.where(c == 0, jnp.concatenate([mine, theirs], axis=0), jnp.concatenate([theirs, mine], axis=0))

    g = both_halves(half_pack, sib_half_pack)
    g_w_in = lax.switch(j, [functools.partial(_from_window, k) for k in chips], both_halves(half_in, sib_half_in),
                        g[R4:R4 + 2 * DN_HEADS])
    g_w_o_dn = g[:R1]
    g_w_o_dil = g[R1:R2].reshape(DIL_W, qw)
    g_w_out = g[R2:R3]
    g_conv = g[R3:R3 + ROWS_CONV].reshape(4, cq)

    def unpack_small(t):
        return dict(norm_w=t[0:1], final_norm_w=t[1], dn_norm_w=t[2:3, :DN_D], a_log=t[3:4, :DN_HEADS],
                    dt_bias=t[3:4, DN_HEADS:2 * DN_HEADS])

    res = {"grad": unpack_small(gs)}
    for kind, arr in zip(("delta", "new_m", "new_v"), sm):
        res[kind] = unpack_small(arr)
    big = dict(conv_w=(conv_w, g_conv, m_conv_w, v_conv_w), w_o_dn=(w_o_dn, g_w_o_dn, m_w_o_dn, v_w_o_dn),
               w_o_dil=(w_o_dil, g_w_o_dil, m_w_o_dil, v_w_o_dil), w_out=(w_out, g_w_out, m_w_out, v_w_out))
    for name, (wt, gt, mt, vt) in big.items():
        d, nm, nv = _adamw(wt[0], gt, mt[0], vt[0], "adamw_" + name)
        res["grad"][name] = gt[None]
        res["delta"][name], res["new_m"][name], res["new_v"][name] = d[None], nm[None], nv[None]

    d, nm, nv = _adamw(w_in[0].T, g_w_in, m_w_in[0].T, v_w_in[0].T, "adamw_w_in")
    res["grad"]["w_in"] = g_w_in.T[None]
    res["delta"]["w_in"], res["new_m"]["w_in"], res["new_v"]["w_in"] = d.T[None], nm.T[None], nv.T[None]
    order = ["norm_w", "w_in", "conv_w", "a_log", "dt_bias", "dn_norm_w", "w_o_dn", "w_o_dil", "w_out", "final_norm_w"]
    outs = [loss, grad_x[None]]
    for kind in ("grad", "delta", "new_m", "new_v"):
        outs += [res[kind][nm] for nm in order]
    return tuple(outs)
```

```python
import functools
import math

import jax
import jax.numpy as jnp
from jax import lax
from jax.experimental import pallas as pl
from jax.experimental.pallas import tpu as pltpu

F32 = jnp.float32
BF16 = jnp.bfloat16
MESH = pl.DeviceIdType.MESH

D_MODEL = 1024
DN_HEADS = 8
DN_D = 128
DN_CHUNK = 64
DN_W = DN_HEADS * DN_D
DIL_GROUPS = ((128, 1), (512, 4), (2048, 16))
N_DIL = len(DIL_GROUPS)
DIL_HEADS = 4
DIL_DH = 128
DIL_W = DIL_HEADS * DIL_DH
ATT_BLOCK = 128
NORM_EPS = 1e-6
PROJ_W = 11280
N_CHIPS = 4
SHARD_W = PROJ_W // N_CHIPS

OFF_QKV_A = 0
OFF_Z_A = 3072
OFF_Q_B = 4096
OFF_K_B = 5632
OFF_V_B = 7168
OFF_Z_B = 8704
OFF_G_A = 9216
OFF_G_B = 10240
OFF_BA = 11264
PW = 11520
REF_OFF_BA = 4096

ADAM_LR = 0.001
ADAM_B1 = 0.9
ADAM_B2 = 0.999
ADAM_EPS = 1e-08
ADAM_WD = 0.01
ADAM_STEP = 10

ROW_TILE = 256
NEG = -1e30


def _dot(a, b):
    return jnp.dot(a.astype(BF16), b.astype(BF16), preferred_element_type=F32)


def _dot_nt(a, b):
    return lax.dot_general(a.astype(BF16), b.astype(BF16), (((1,), (1,)), ((), ())), preferred_element_type=F32)


def _dot_tn(a, b):
    return lax.dot_general(a.astype(BF16), b.astype(BF16), (((0,), (0,)), ((), ())), preferred_element_type=F32)


def _split(a):
    hi = a.astype(BF16)
    lo = (a - hi.astype(F32)).astype(BF16)
    return hi, lo


def _dot_exact_lhs(c, a):
    hi, lo = _split(a)
    cb = c.astype(BF16)
    return jnp.dot(cb, hi, preferred_element_type=F32) + jnp.dot(cb, lo, preferred_element_type=F32)


def _dot_exact_rhs(a, c):
    hi, lo = _split(a)
    cb = c.astype(BF16)
    return jnp.dot(hi, cb, preferred_element_type=F32) + jnp.dot(lo, cb, preferred_element_type=F32)


def _dot_tn_exact_rhs(a, c):
    hi, lo = _split(a)
    cb = c.astype(BF16)
    dn = (((0,), (0,)), ((), ()))
    return (lax.dot_general(hi, cb, dn, preferred_element_type=F32)
            + lax.dot_general(lo, cb, dn, preferred_element_type=F32))


def _sigmoid(x):
    return 1.0 / (1.0 + jnp.exp(-x))


def _silu(x):
    return x * _sigmoid(x)


def _silu_grad(x):
    s = _sigmoid(x)
    return s * (1.0 + x * (1.0 - s))


def _softplus(x):
    return jnp.maximum(x, 0.0) + jnp.log(1.0 + jnp.exp(-jnp.abs(x)))


def _cparams(*sem):
    return pltpu.CompilerParams(dimension_semantics=sem)


def _matmul(a, b, out_dtype, tm, tn, tk, name, nt=False, transpose_out=False, after=None, also_bf16=False):
    m, kdim = a.shape
    n = b.shape[0] if nt else b.shape[1]
    tm, tn, tk = min(tm, m), min(tn, n), min(tk, kdim)
    assert m % tm == 0 and n % tn == 0 and kdim % tk == 0, (name, a.shape, b.shape, tm, tn, tk)
    nk = kdim // tk
    dot = _dot_nt if nt else _dot
    b_spec = (pl.BlockSpec((tn, tk), lambda i, j, k: (j, k)) if nt else pl.BlockSpec((tk, tn), lambda i, j, k: (k, j)))
    extra = [] if after is None else [after]
    out_dtypes = [out_dtype] + ([BF16] if also_bf16 else [])

    def emit(o_refs, acc):
        val = acc.T if transpose_out else acc
        for o_ref in o_refs:
            o_ref[...] = val.astype(o_ref.dtype)

    def outs_of(rest):
        return rest[len(extra):len(extra) + len(out_dtypes)]

    if nk == 1:
        def body(a_ref, b_ref, *rest):
            emit(outs_of(rest), dot(a_ref[...], b_ref[...]))
        scratch = []
    else:
        def body(a_ref, b_ref, *rest):
            o_ref, acc_ref = outs_of(rest), rest[-1]
            k = pl.program_id(2)
            p = dot(a_ref[...], b_ref[...])

            @pl.when(k == 0)
            def _():
                acc_ref[...] = p

            @pl.when(k > 0)
            def _():
                acc_ref[...] += p

            @pl.when(k == nk - 1)
            def _():
                emit(o_ref, acc_ref[...])
        scratch = [pltpu.VMEM((tm, tn), F32)]

    if transpose_out:
        out_spec, out_shape = pl.BlockSpec((tn, tm), lambda i, j, k: (j, i)), (n, m)
    else:
        out_spec, out_shape = pl.BlockSpec((tm, tn), lambda i, j, k: (i, j)), (m, n)
    res = pl.pallas_call(
        body, name=name, grid=(m // tm, n // tn, nk),
        in_specs=[pl.BlockSpec((tm, tk), lambda i, j, k: (i, k)), b_spec] + [pl.BlockSpec(memory_space=pl.ANY)] * len(extra),
        out_specs=[out_spec] * len(out_dtypes), out_shape=[jax.ShapeDtypeStruct(out_shape, d) for d in out_dtypes],
        scratch_shapes=scratch, compiler_params=_cparams("parallel", "parallel", "arbitrary"))(a, b, *extra)
    return res if also_bf16 else res[0]


def _rms_in(x, nw):
    s, d = x.shape

    def body(x_ref, w_ref, h_ref, ht_ref):
        xv = x_ref[...]
        r = lax.rsqrt(jnp.mean(xv * xv, axis=-1, keepdims=True) + NORM_EPS)
        h = xv * r * w_ref[...]
        h_ref[...] = h.astype(BF16)
        ht_ref[...] = h.T.astype(BF16)

    return pl.pallas_call(
        body, name="rms_in", grid=(s // ROW_TILE,),
        in_specs=[pl.BlockSpec((ROW_TILE, d), lambda i: (i, 0)), pl.BlockSpec((1, d), lambda i: (0, 0))],
        out_specs=[pl.BlockSpec((ROW_TILE, d), lambda i: (i, 0)), pl.BlockSpec((d, ROW_TILE), lambda i: (0, i))],
        out_shape=[jax.ShapeDtypeStruct((s, d), BF16), jax.ShapeDtypeStruct((d, s), BF16)],
        compiler_params=_cparams("parallel"))(x, nw)


def _rms_in_bwd(x, nw, dh, dx2):
    s, d = x.shape

    def body(x_ref, w_ref, dh_ref, dx2_ref, dx_ref, dw_ref):
        i = pl.program_id(0)
        xv = x_ref[...]
        r = lax.rsqrt(jnp.mean(xv * xv, axis=-1, keepdims=True) + NORM_EPS)
        dhv = dh_ref[...]
        dyw = dhv * w_ref[...]
        dx_ref[...] = dx2_ref[...] + r * dyw - xv * (r * r * r) * jnp.mean(dyw * xv, axis=-1, keepdims=True)
        part = jnp.sum(dhv * xv * r, axis=0, keepdims=True)

        @pl.when(i == 0)
        def _():
            dw_ref[...] = part

        @pl.when(i > 0)
        def _():
            dw_ref[...] += part

    row = pl.BlockSpec((ROW_TILE, d), lambda i: (i, 0))
    vec = pl.BlockSpec((1, d), lambda i: (0, 0))
    return pl.pallas_call(
        body, name="rms_in_bwd", grid=(s // ROW_TILE,), in_specs=[row, vec, row, row], out_specs=[row, vec],
        out_shape=[jax.ShapeDtypeStruct((s, d), F32), jax.ShapeDtypeStruct((1, d), F32)],
        compiler_params=_cparams("arbitrary"))(x, nw, dh, dx2)


def _shift_down(cur, prev8, k):
    rc = pltpu.roll(cur, k, 0)
    rp = pltpu.roll(prev8, k, 0)
    row = lax.broadcasted_iota(jnp.int32, prev8.shape, 0)
    top = jnp.where(row < k, rp, rc[:8])
    return jnp.concatenate([top, rc[8:]], axis=0)


def _shift_up(cur, next8, k):
    t = cur.shape[0]
    rc = pltpu.roll(cur, t - k, 0)
    rn = pltpu.roll(next8, 8 - k, 0)
    row = lax.broadcasted_iota(jnp.int32, next8.shape, 0)
    bot = jnp.where(row >= 8 - k, rn, rc[t - 8:])
    return jnp.concatenate([rc[:t - 8], bot], axis=0)


def _conv_fwd(proj, conv_w):
    s = proj.shape[0]
    t8 = ROW_TILE // 8

    def body(u_ref, up_ref, w_ref, c_ref, y_ref):
        i = pl.program_id(0)
        part = pl.program_id(1)
        cur = u_ref[...]
        prev8 = jnp.where(i > 0, up_ref[...], 0.0)
        w = w_ref[...]
        c = cur * w[3:4, :]
        for k in (1, 2, 3):
            c = c + _shift_down(cur, prev8, k) * w[3 - k:4 - k, :]
        c_ref[...] = c
        a = _silu(c)
        for h in range(DN_HEADS):
            ah = a[:, h * DN_D:(h + 1) * DN_D]
            r = lax.rsqrt(jnp.sum(ah * ah, axis=-1, keepdims=True) + NORM_EPS)
            y_ref[:, h * DN_D:(h + 1) * DN_D] = jnp.where(part < 2, ah * r, ah)

    return pl.pallas_call(
        body, name="conv_fwd", grid=(s // ROW_TILE, 3),
        in_specs=[pl.BlockSpec((ROW_TILE, DN_W), lambda i, p: (i, p)),
                  pl.BlockSpec((8, DN_W), lambda i, p: (jnp.maximum(i * t8 - 1, 0), p)),
                  pl.BlockSpec((4, DN_W), lambda i, p: (0, p))],
        out_specs=[pl.BlockSpec((ROW_TILE, DN_W), lambda i, p: (i, p))] * 2,
        out_shape=[jax.ShapeDtypeStruct((s, 3 * DN_W), F32)] * 2,
        compiler_params=_cparams("parallel", "parallel"))(proj, proj, conv_w)


def _conv_bwd_act(c, dq, dk, dv):
    s = c.shape[0]

    def body(c_ref, dq_ref, dk_ref, dv_ref, dc_ref):
        for part, d_ref in enumerate((dq_ref, dk_ref, dv_ref)):
            for h in range(DN_HEADS):
                sl = slice(part * DN_W + h * DN_D, part * DN_W + (h + 1) * DN_D)
                ch = c_ref[:, sl]
                dyh = d_ref[:, h * DN_D:(h + 1) * DN_D]
                if part < 2:
                    ah = _silu(ch)
                    r = lax.rsqrt(jnp.sum(ah * ah, axis=-1, keepdims=True) + NORM_EPS)
                    dyh = r * dyh - ah * (r * r * r) * jnp.sum(dyh * ah, axis=-1, keepdims=True)
                dc_ref[:, sl] = dyh * _silu_grad(ch)

    wide = pl.BlockSpec((ROW_TILE, 3 * DN_W), lambda i: (i, 0))
    row = pl.BlockSpec((ROW_TILE, DN_W), lambda i: (i, 0))
    return pl.pallas_call(
        body, name="conv_bwd_act", grid=(s // ROW_TILE,), in_specs=[wide, row, row, row], out_specs=wide,
        out_shape=jax.ShapeDtypeStruct((s, 3 * DN_W), F32), compiler_params=_cparams("parallel"))(c, dq, dk, dv)


DPROJ_IN = pl.BlockSpec(memory_space=pl.ANY)


def _conv_bwd(proj, dc, conv_w, dproj):
    s = proj.shape[0]
    t8 = ROW_TILE // 8
    nrow = s // ROW_TILE
    last8 = s // 8 - 1

    def body(u_ref, dc_ref, dcn_ref, w_ref, dproj_in, du_ref, dw_ref):
        i = pl.program_id(1)
        cur = u_ref[...]
        dcv = dc_ref[...]
        next8 = jnp.where(i < nrow - 1, dcn_ref[...], 0.0)
        w = w_ref[...]

        @pl.when(i == 0)
        def _():
            dw_ref[...] = jnp.zeros_like(dw_ref)

        du = dcv * w[3:4, :]
        dw_ref[3:4, :] += jnp.sum(cur * dcv, axis=0, keepdims=True)
        for k in (1, 2, 3):
            ahead = _shift_up(dcv, next8, k)
            du = du + ahead * w[3 - k:4 - k, :]
            dw_ref[3 - k:4 - k, :] += jnp.sum(cur * ahead, axis=0, keepdims=True)
        du_ref[...] = du.astype(BF16)

    blk = pl.BlockSpec((ROW_TILE, DN_W), lambda p, i: (i, p))
    return pl.pallas_call(
        body, name="conv_bwd", grid=(3, nrow),
        in_specs=[blk, blk, pl.BlockSpec((8, DN_W), lambda p, i: (jnp.minimum((i + 1) * t8, last8), p)),
                  pl.BlockSpec((4, DN_W), lambda p, i: (0, p)), DPROJ_IN],
        out_specs=[blk, pl.BlockSpec((4, DN_W), lambda p, i: (0, p))],
        out_shape=[jax.ShapeDtypeStruct((s, PW), BF16), jax.ShapeDtypeStruct((4, 3 * DN_W), F32)],
        input_output_aliases={4: 0},
        compiler_params=_cparams("parallel", "arbitrary"))(proj, dc, dc, conv_w, dproj)


def _gates_fwd(proj, gate_par):
    s = proj.shape[0]

    def body(ba_ref, par_ref, o_ref):
        v = ba_ref[...]
        lane = lax.broadcasted_iota(jnp.int32, v.shape, 1)
        beta = _sigmoid(v)
        g = -jnp.exp(par_ref[0:1, :]) * _softplus(v + par_ref[1:2, :])
        o_ref[...] = jnp.where(lane < DN_HEADS, beta, jnp.where(lane < 2 * DN_HEADS, g, 0.0))

    return pl.pallas_call(
        body, name="gates_fwd", grid=(s // ROW_TILE,),
        in_specs=[pl.BlockSpec((ROW_TILE, 128), lambda i: (i, OFF_BA // 128)), pl.BlockSpec((8, 128), lambda i: (0, 0))],
        out_specs=pl.BlockSpec((ROW_TILE, 128), lambda i: (i, 0)),
        out_shape=jax.ShapeDtypeStruct((s, 128), F32), compiler_params=_cparams("parallel"))(proj, gate_par)


def _gates_bwd(proj, gate_par, dbg, dproj):
    s = proj.shape[0]

    def body(ba_ref, par_ref, d_ref, dproj_in, o_ref, dpar_ref):
        i = pl.program_id(0)
        v = ba_ref[...]
        dv = d_ref[...]
        lane = lax.broadcasted_iota(jnp.int32, v.shape, 1)
        beta = _sigmoid(v)
        nega = -jnp.exp(par_ref[0:1, :])
        xs = v + par_ref[1:2, :]
        dsp = dv * nega * _sigmoid(xs)
        dal = dv * nega * _softplus(xs)
        is_b = lane < DN_HEADS
        is_g = jnp.logical_and(lane >= DN_HEADS, lane < 2 * DN_HEADS)
        o_ref[:, :128] = jnp.where(is_b, dv * beta * (1.0 - beta), jnp.where(is_g, dsp, 0.0)).astype(BF16)
        o_ref[:, 128:] = jnp.zeros((ROW_TILE, PW - OFF_BA - 128), BF16)
        r0 = jnp.sum(jnp.where(is_g, dal, 0.0), axis=0, keepdims=True)
        r1 = jnp.sum(jnp.where(is_g, dsp, 0.0), axis=0, keepdims=True)

        @pl.when(i == 0)
        def _():
            dpar_ref[...] = jnp.zeros_like(dpar_ref)

        dpar_ref[0:1, :] += r0
        dpar_ref[1:2, :] += r1

    return pl.pallas_call(
        body, name="gates_bwd", grid=(s // ROW_TILE,),
        in_specs=[pl.BlockSpec((ROW_TILE, 128), lambda i: (i, OFF_BA // 128)), pl.BlockSpec((8, 128), lambda i: (0, 0)),
                  pl.BlockSpec((ROW_TILE, 128), lambda i: (i, 0)), DPROJ_IN],
        out_specs=[pl.BlockSpec((ROW_TILE, PW - OFF_BA), lambda i: (i, OFF_BA // (PW - OFF_BA))),
                   pl.BlockSpec((8, 128), lambda i: (0, 0))],
        out_shape=[jax.ShapeDtypeStruct((s, PW), BF16), jax.ShapeDtypeStruct((8, 128), F32)],
        input_output_aliases={3: 0},
        compiler_params=_cparams("arbitrary"))(proj, gate_par, dbg, dproj)


def _chunk_masks():
    c = DN_CHUNK
    ii = lax.broadcasted_iota(jnp.int32, (c, c), 0)
    jj = lax.broadcasted_iota(jnp.int32, (c, c), 1)
    return dict(ii=ii, jj=jj, lower=(ii >= jj), strict=(ii > jj), eye=(ii == jj),
                lower_f=(ii >= jj).astype(BF16), upper_f=(ii <= jj).astype(BF16), ones8=jnp.ones((8, c), BF16))


class _Heads:
    def __init__(self, xs):
        self.xs = list(xs)

    def _bin(self, o, f):
        if isinstance(o, _Heads):
            return _Heads([f(a, b) for a, b in zip(self.xs, o.xs)])
        return _Heads([f(a, o) for a in self.xs])

    def __add__(self, o):
        return self._bin(o, lambda a, b: a + b)

    def __sub__(self, o):
        return self._bin(o, lambda a, b: a - b)

    def __mul__(self, o):
        return self._bin(o, lambda a, b: a * b)

    __radd__ = __add__
    __rmul__ = __mul__

    def __neg__(self):
        return _Heads([-a for a in self.xs])

    def __getitem__(self, i):
        return _Heads([a[i] for a in self.xs])


def _hmap(f, *args):
    n = next(len(a.xs) for a in args if isinstance(a, _Heads))
    return _Heads([f(*[(a.xs[h] if isinstance(a, _Heads) else a) for a in args]) for h in range(n)])


def _hdot(a, b):
    return _hmap(_dot, a, b)


def _hdot_nt(a, b):
    return _hmap(_dot_nt, a, b)


def _hdot_tn(a, b):
    return _hmap(_dot_tn, a, b)


def _hcat(a, b, axis):
    return _hmap(lambda x, y: jnp.concatenate([x, y], axis=axis), a, b)


def _hsum(a, axis):
    return _hmap(lambda t: jnp.sum(t, axis=axis, keepdims=True), a)


def _hwhere(c, a, b):
    return _hmap(jnp.where, c, a, b)


def _chunk_gates(mk, bg):
    c = DN_CHUNK
    gc_all = _dot_exact_lhs(mk["lower_f"], bg)
    rows = jnp.concatenate([gc_all, gc_all], axis=0).T
    hs = range(DN_HEADS)
    return (_Heads(bg[:, h:h + 1] for h in hs), _Heads(gc_all[:, DN_HEADS + h:DN_HEADS + h + 1] for h in hs),
            _Heads(rows[DN_HEADS + h:DN_HEADS + h + 1, :] for h in hs))


def _chunk_common(mk, q, k, beta_col, gc_col, gc_r):
    c = DN_CHUNK
    lower, strict = mk["lower"], mk["strict"]
    qs = q * (DN_D ** -0.5)
    beta_b = _hmap(lambda t: jnp.broadcast_to(t, (c, DN_D)), beta_col)
    gc_b = _hmap(lambda t: jnp.broadcast_to(t, (c, DN_D)), gc_col)
    gc_sq = gc_b[:, :c]
    gam = _hwhere(lower, _hmap(lambda t: jnp.exp(jnp.minimum(t, 0.0)), gc_sq - gc_r[:, :c]), 0.0)
    egc = _hmap(jnp.exp, gc_b)
    gl = gc_b[c - 1:c, :]
    ekd = _hmap(jnp.exp, gl - gc_b)
    dl = _hmap(jnp.exp, gl)
    kb = k * beta_b
    scores = _hdot_nt(_hcat(kb, qs, 0), k)
    a_strict = _hwhere(strict, scores[:c] * gam, 0.0)
    aqk = _hwhere(lower, scores[c:] * gam, 0.0)
    return dict(k=k, qs=qs, beta_b=beta_b, gc_b=gc_b, gam=gam, egc=egc, ekd=ekd, dl=dl, kb=kb, a_strict=a_strict, aqk=aqk)


def _unit_lower_inverse_minus_eye(n_strict, ii, jj):
    same = lax.shift_right_logical(ii, 4) == lax.shift_right_logical(jj, 4)
    dmat = _hwhere(same, n_strict, 0.0)
    omat = n_strict - dmat
    d2 = _hdot(dmat, dmat)
    d4 = _hdot(d2, d2)
    d8 = _hdot(d4, d4)
    x1 = d2 - dmat - _hdot(dmat, d2)
    x2 = x1 + d4 + _hdot(x1, d4)
    x3 = x2 + d8 + _hdot(x2, d8)
    n1 = omat + _hdot(x3, omat)
    n2 = _hdot(n1, n1)
    y = n2 - n1 - _hdot(n1, n2)
    return y + x3 + _hdot(y, x3)


GDR_HEAD_SETS = (range(0, DN_HEADS),)


def _gdr_fwd(qkv, bg):
    s = qkv.shape[0]
    c = DN_CHUNK
    n = s // c

    def body(q_ref, k_ref, v_ref, bg_ref, o_ref, u_ref, w_ref, vn_ref, tm_ref, st_ref, state):
        @pl.when(pl.program_id(0) == 0)
        def _():
            state[...] = jnp.zeros_like(state)

        mk = _chunk_masks()
        gates = _chunk_gates(mk, bg_ref[...])
        for hs in GDR_HEAD_SETS:
            sls = [slice(h * DN_D, (h + 1) * DN_D) for h in hs]
            cm = _chunk_common(mk, _Heads(q_ref[:, sl] for sl in sls), _Heads(k_ref[:, sl] for sl in sls),
                               *[_Heads(g.xs[h] for h in hs) for g in gates])
            tm = _unit_lower_inverse_minus_eye(cm["a_strict"], mk["ii"], mk["jj"])
            rhs_u = _Heads(v_ref[:, sl] for sl in sls) * cm["beta_b"]
            rhs_w = cm["kb"] * cm["egc"]
            t_rhs = _hdot(tm, _hcat(rhs_u, rhs_w, 1))
            u = rhs_u + t_rhs[:, :DN_D]
            w = rhs_w + t_rhs[:, DN_D:]
            st = _Heads(state[h] for h in hs)
            on_state = _hdot(_hcat(w, cm["qs"] * cm["egc"], 0), st)
            v_new = u - on_state[:c]
            o = on_state[c:] + _hdot(cm["aqk"], v_new)
            st_new = st * cm["dl"] + _hdot_tn(cm["k"] * cm["ekd"], v_new)
            for i, (h, sl) in enumerate(zip(hs, sls)):
                o_ref[:, sl] = o.xs[i]
                u_ref[:, sl] = u.xs[i]
                w_ref[:, sl] = w.xs[i]
                vn_ref[:, sl] = v_new.xs[i]
                tm_ref[h, 0] = tm.xs[i]
                st_ref[h, 0] = st.xs[i]
                state[h] = st_new.xs[i]

    def part(p):
        return pl.BlockSpec((c, DN_W), lambda j: (j, p))

    return pl.pallas_call(
        body, name="gdr_fwd", grid=(n,),
        in_specs=[part(0), part(1), part(2), pl.BlockSpec((c, 128), lambda j: (j, 0))],
        out_specs=[part(0)] * 4 + [pl.BlockSpec((DN_HEADS, 1, c, c), lambda j: (0, j, 0, 0)),
                                   pl.BlockSpec((DN_HEADS, 1, DN_D, DN_D), lambda j: (0, j, 0, 0))],
        out_shape=[jax.ShapeDtypeStruct((s, DN_W), F32)] * 4
        + [jax.ShapeDtypeStruct((DN_HEADS, n, c, c), F32), jax.ShapeDtypeStruct((DN_HEADS, n, DN_D, DN_D), F32)],
        scratch_shapes=[pltpu.VMEM((DN_HEADS, DN_D, DN_D), F32)],
        compiler_params=_cparams("arbitrary"))(qkv, qkv, qkv, bg)


def _gdr_bwd(qkv, bg, u, w, vn, tmat, states, do):
    s = qkv.shape[0]
    c = DN_CHUNK
    n = s // c

    def body(q_ref, k_ref, v_ref, bg_ref, u_ref, w_ref, vn_ref, tm_ref, st_ref, do_ref,
             dq_ref, dk_ref, dv_ref, dbg_ref, dstate):
        @pl.when(pl.program_id(0) == 0)
        def _():
            dstate[...] = jnp.zeros_like(dstate)

        mk = _chunk_masks()
        lower, strict = mk["lower"], mk["strict"]
        bg = bg_ref[...]
        ones = jnp.ones((c, DN_D), BF16)
        rowi = lax.broadcasted_iota(jnp.int32, (c, DN_D), 0)
        lane = lax.broadcasted_iota(jnp.int32, (c, 128), 1)
        hs = range(DN_HEADS)
        sls = [slice(h * DN_D, (h + 1) * DN_D) for h in hs]

        def heads_of(ref):
            return _Heads(ref[:, sl] for sl in sls)

        cm = _chunk_common(mk, heads_of(q_ref), heads_of(k_ref), *_chunk_gates(mk, bg))
        k, qs, beta_b = cm["k"], cm["qs"], cm["beta_b"]
        gam, egc, ekd, dl, kb = cm["gam"], cm["egc"], cm["ekd"], cm["dl"], cm["kb"]
        aqk, a_strict = cm["aqk"], cm["a_strict"]
        v, uu, ww, v_new, dov = heads_of(v_ref), heads_of(u_ref), heads_of(w_ref), heads_of(vn_ref), heads_of(do_ref)
        st = _Heads(st_ref[h, 0] for h in hs)
        dsn = _Heads(dstate[h] for h in hs)
        qd = qs * egc
        kd = k * ekd

        dv_new = _hdot_tn(aqk, dov) + _hdot(kd, dsn)
        do_sv = _hdot_nt(dov, _hcat(st, v_new, 0))
        dqd = do_sv[:, :DN_D]
        daqk = _hwhere(lower, do_sv[:, DN_D:], 0.0)
        dkd = _hdot_nt(v_new, dsn)
        ddl = _hsum(_hsum(dsn * st, 1), 0)
        dw = -_hdot_nt(dv_new, st)
        ds_new = dsn * dl + _hdot_tn(_hcat(qd, -ww, 0), _hcat(dov, dv_new, 0))

        tm = _Heads(tm_ref[h, 0] for h in hs)
        tt = _hdot_tn(tm, _hcat(dv_new, dw, 1))
        dru = dv_new + tt[:, :DN_D]
        drw = dw + tt[:, DN_D:]
        dn = _hwhere(strict, -_hdot_nt(_hcat(dru, drw, 1), _hcat(uu, ww, 1)), 0.0)
        dag = dn * gam
        dqg = daqk * gam
        both = _hcat(dag, dqg, 0)
        on_k = _hdot(both, k)
        dkb = on_k[:c] + drw * egc
        dqs = on_k[c:] + dqd * egc
        dk = _hdot_tn(both, _hcat(kb, qs, 0)) + dkb * beta_b + dkd * ekd
        pmat = dn * a_strict + daqk * aqk
        tkd = _hsum(dkd * kd, -1)
        dgc = (_hsum(pmat, -1) - _hmap(_dot_tn_exact_rhs, pmat, ones) + _hsum(drw * (kb * egc), -1)
               + _hsum(dqd * qd, -1) - tkd)
        last = _hsum(tkd, 0) + ddl * dl
        dgc = dgc + _hwhere(rowi == c - 1, last, 0.0)
        dbeta = _hsum(dru * v, -1) + _hsum(dkb * k, -1)
        dq = dqs * (DN_D ** -0.5)
        dv = dru * beta_b

        dgc_all = jnp.zeros((c, 128), F32)
        dbg = jnp.zeros((c, 128), F32)
        for h, sl in zip(hs, sls):
            dq_ref[:, sl] = dq.xs[h]
            dk_ref[:, sl] = dk.xs[h]
            dv_ref[:, sl] = dv.xs[h]
            dstate[h] = ds_new.xs[h]
            dgc_all = dgc_all + jnp.where(lane == DN_HEADS + h, dgc.xs[h], 0.0)
            dbg = dbg + jnp.where(lane == h, dbeta.xs[h], 0.0)
        dbg_ref[...] = dbg + _dot_exact_lhs(mk["upper_f"], dgc_all)

    def part(p):
        return pl.BlockSpec((c, DN_W), lambda j: (n - 1 - j, p))

    vec = pl.BlockSpec((c, 128), lambda j: (n - 1 - j, 0))
    return pl.pallas_call(
        body, name="gdr_bwd", grid=(n,),
        in_specs=[part(0), part(1), part(2), vec, part(0), part(0), part(0),
                  pl.BlockSpec((DN_HEADS, 1, c, c), lambda j: (0, n - 1 - j, 0, 0)),
                  pl.BlockSpec((DN_HEADS, 1, DN_D, DN_D), lambda j: (0, n - 1 - j, 0, 0)), part(0)],
        out_specs=[part(0), part(0), part(0), vec],
        out_shape=[jax.ShapeDtypeStruct((s, DN_W), F32)] * 3 + [jax.ShapeDtypeStruct((s, 128), F32)],
        scratch_shapes=[pltpu.VMEM((DN_HEADS, DN_D, DN_D), F32)],
        compiler_params=_cparams("arbitrary"))(qkv, qkv, qkv, bg, u, w, vn, tmat, states, do)


def _gdr_out(o, proj, dnw):
    s = o.shape[0]

    def body(o_ref, z_ref, w_ref, y_ref, yt_ref):
        ov, zv, wv = o_ref[...], z_ref[...], w_ref[...]
        for h in range(DN_HEADS):
            sl = slice(h * DN_D, (h + 1) * DN_D)
            oh = ov[:, sl]
            r = lax.rsqrt(jnp.mean(oh * oh, axis=-1, keepdims=True) + NORM_EPS)
            y = (oh * r * wv) * _silu(zv[:, sl])
            y_ref[:, sl] = y.astype(BF16)
            yt_ref[sl, :] = y.T.astype(BF16)

    row = pl.BlockSpec((ROW_TILE, DN_W), lambda i: (i, 0))
    return pl.pallas_call(
        body, name="gdr_out", grid=(s // ROW_TILE,),
        in_specs=[row, pl.BlockSpec((ROW_TILE, DN_W), lambda i: (i, OFF_Z_A // DN_W)), pl.BlockSpec((1, DN_D), lambda i: (0, 0))],
        out_specs=[row, pl.BlockSpec((DN_W, ROW_TILE), lambda i: (0, i))],
        out_shape=[jax.ShapeDtypeStruct((s, DN_W), BF16), jax.ShapeDtypeStruct((DN_W, s), BF16)],
        compiler_params=_cparams("parallel"))(o, proj, dnw)


def _gdr_out_bwd(o, proj, dnw, dy):
    s = o.shape[0]

    def body(o_ref, z_ref, w_ref, dy_ref, do_ref, dz_ref, dw_ref):
        i = pl.program_id(0)
        ov, zv, wv, dyv = o_ref[...], z_ref[...], w_ref[...], dy_ref[...]
        acc = jnp.zeros((1, DN_D), F32)
        for h in range(DN_HEADS):
            sl = slice(h * DN_D, (h + 1) * DN_D)
            oh, zh, dh = ov[:, sl], zv[:, sl], dyv[:, sl]
            r = lax.rsqrt(jnp.mean(oh * oh, axis=-1, keepdims=True) + NORM_EPS)
            dn = dh * _silu(zh)
            dz_ref[:, sl] = (dh * (oh * r * wv) * _silu_grad(zh)).astype(BF16)
            acc = acc + jnp.sum(dn * oh * r, axis=0, keepdims=True)
            dnw_ = dn * wv
            do_ref[:, sl] = r * dnw_ - oh * (r * r * r) * jnp.mean(dnw_ * oh, axis=-1, keepdims=True)

        @pl.when(i == 0)
        def _():
            dw_ref[...] = acc

        @pl.when(i > 0)
        def _():
            dw_ref[...] += acc

    row = pl.BlockSpec((ROW_TILE, DN_W), lambda i: (i, 0))
    vec = pl.BlockSpec((1, DN_D), lambda i: (0, 0))
    return pl.pallas_call(
        body, name="gdr_out_bwd", grid=(s // ROW_TILE,),
        in_specs=[row, pl.BlockSpec((ROW_TILE, DN_W), lambda i: (i, OFF_Z_A // DN_W)), vec, row],
        out_specs=[row, pl.BlockSpec((ROW_TILE, DN_W), lambda i: (i, OFF_Z_A // DN_W)), vec],
        out_shape=[jax.ShapeDtypeStruct((s, DN_W), F32), jax.ShapeDtypeStruct((s, PW), BF16),
                   jax.ShapeDtypeStruct((1, DN_D), F32)],
        compiler_params=_cparams("arbitrary"))(o, proj, dnw, dy)


def _slope(group, head):
    idx = (group * DIL_HEADS + head + 1).astype(F32)
    return jnp.exp(jnp.full((1, 128), -8.0 * math.log(2.0) / (N_DIL * DIL_HEADS), F32) * idx)


def _att_scores(qb, k_cur, k_prev, slope_d, has_prev):
    iq = lax.broadcasted_iota(jnp.int32, (ATT_BLOCK, ATT_BLOCK), 0)
    jk = lax.broadcasted_iota(jnp.int32, (ATT_BLOCK, ATT_BLOCK), 1)
    dist_c = (iq - jk).astype(F32)
    s_cur = jnp.where(iq >= jk, _dot_nt(qb, k_cur) - slope_d * dist_c, NEG)
    s_prev = jnp.where(jnp.logical_and(jk >= iq, has_prev),
                       _dot_nt(qb, k_prev) - slope_d * (dist_c + float(ATT_BLOCK)), NEG)
    return s_cur, s_prev


def _att_scores_whole(qb, k, slope_d):
    n = 2 * ATT_BLOCK
    dist = lax.broadcasted_iota(jnp.int32, (n, n), 0) - lax.broadcasted_iota(jnp.int32, (n, n), 1)
    valid = jnp.logical_and(dist >= 0, dist <= ATT_BLOCK)
    return jnp.where(valid, _dot_nt(qb, k) - slope_d[:, 0:1] * dist.astype(F32), NEG)


def _att_tiles(i, dil, nb):
    tiles = nb // 2
    per = dil * tiles // ATT_UNROLL
    assert nb % 2 == 0 and tiles >= 2 and per * ATT_UNROLL == dil * tiles
    for i0 in range(per):
        ts = [divmod(i0 + u * per, tiles) for u in range(ATT_UNROLL)]
        assert all(a[0] != b[0] or abs(a[1] - b[1]) >= 2 for n, a in enumerate(ts) for b in ts[n + 1:])
    qrows, krows, has_prev = [], [], []
    for u in range(ATT_UNROLL):
        t = i + u * per
        r = lax.div(t, tiles)
        j = lax.rem(t, tiles)
        qbase = r + dil * 2 * ATT_BLOCK * j
        kbase = qbase - dil * ATT_BLOCK * jnp.minimum(j, 1)
        if dil == 1:
            qbase, kbase = pl.multiple_of(qbase, ATT_BLOCK), pl.multiple_of(kbase, ATT_BLOCK)
        qrows.append(pl.ds(qbase, 2 * ATT_BLOCK, stride=dil))
        krows.append(pl.ds(kbase, 3 * ATT_BLOCK, stride=dil))
        has_prev.append(j > 0)
    return qrows, krows, has_prev


def _att_scores_tile(qb, k, slope_d, has_prev):
    iq = lax.broadcasted_iota(jnp.int32, (2 * ATT_BLOCK, 3 * ATT_BLOCK), 0)
    ck = lax.broadcasted_iota(jnp.int32, (2 * ATT_BLOCK, 3 * ATT_BLOCK), 1)
    dist = iq - ck + jnp.where(has_prev, ATT_BLOCK, 0)
    valid = jnp.logical_and(dist >= 0, dist <= ATT_BLOCK)
    return jnp.where(valid, _dot_nt(qb, k) - slope_d[:, 0:1] * dist.astype(F32), NEG)


ATT_UNROLL = 4


def _att_blocks(i, dil, nb):
    per = dil * nb // ATT_UNROLL
    assert per * ATT_UNROLL == dil * nb
    for i0 in range(per):
        blocks = [divmod(i0 + u * per, nb) for u in range(ATT_UNROLL)]
        assert all(a[0] != b[0] or abs(a[1] - b[1]) >= 2 for n, a in enumerate(blocks) for b in blocks[n + 1:])
    curs, prvs, has_prev = [], [], []
    for u in range(ATT_UNROLL):
        t = i + u * per
        r = lax.div(t, nb)
        j = lax.rem(t, nb)
        base = r + dil * ATT_BLOCK * j
        pbase = base - dil * ATT_BLOCK * jnp.minimum(j, 1)
        if dil == 1:
            base, pbase = pl.multiple_of(base, ATT_BLOCK), pl.multiple_of(pbase, ATT_BLOCK)
        curs.append(pl.ds(base, ATT_BLOCK, stride=dil))
        prvs.append(pl.ds(pbase, ATT_BLOCK, stride=dil))
        has_prev.append(j > 0)
    return curs, prvs, has_prev


def _att_fwd(proj, group):
    s = proj.shape[0]
    dil = DIL_GROUPS[group][1]
    assert DIL_GROUPS[group][0] // dil == ATT_BLOCK
    nb = s // dil // ATT_BLOCK
    assert nb * dil * ATT_BLOCK == s

    def body(q_ref, k_ref, v_ref, o_ref, lse_ref):
        def emit(rows, num, den, mx):
            o_ref[rows, :] = num / den
            lse_ref[rows, :] = jnp.broadcast_to(mx + jnp.log(den), (num.shape[0], DIL_DH))

        slope_d = _slope(group, pl.program_id(0)) * float(dil)

        def step(i, carry):
            curs, prvs, has_prev = _att_blocks(i, dil, nb)
            us = range(ATT_UNROLL)
            qb = [q_ref[c, :] * (DIL_DH ** -0.5) for c in curs]
            sc = [_att_scores(qb[u], k_ref[curs[u], :], k_ref[prvs[u], :], slope_d, has_prev[u]) for u in us]
            mx = [jnp.maximum(jnp.max(a, axis=-1, keepdims=True), jnp.max(b, axis=-1, keepdims=True)) for a, b in sc]
            p_cur = [jnp.exp(sc[u][0] - mx[u]) for u in us]
            p_prev = [jnp.exp(sc[u][1] - mx[u]) for u in us]
            den = [jnp.sum(p_cur[u], axis=-1, keepdims=True) + jnp.sum(p_prev[u], axis=-1, keepdims=True) for u in us]
            num = [_dot(p_cur[u], v_ref[curs[u], :]) + _dot(p_prev[u], v_ref[prvs[u], :]) for u in us]
            for u in us:
                emit(curs[u], num[u], den[u], mx[u])
            return carry

        def step_whole(i, carry):
            rows = [pl.ds(i * ATT_UNROLL + u, 2 * ATT_BLOCK, stride=dil) for u in range(ATT_UNROLL)]
            sc = [_att_scores_whole(q_ref[r, :] * (DIL_DH ** -0.5), k_ref[r, :], slope_d) for r in rows]
            mx = [jnp.max(a, axis=-1, keepdims=True) for a in sc]
            p = [jnp.exp(a - m) for a, m in zip(sc, mx)]
            num = [_dot(pu, v_ref[r, :]) for pu, r in zip(p, rows)]
            for u, r in enumerate(rows):
                emit(r, num[u], jnp.sum(p[u], axis=-1, keepdims=True), mx[u])
            return carry

        def step_tile(i, carry):
            qrows, krows, has_prev = _att_tiles(i, dil, nb)
            us = range(ATT_UNROLL)
            sc = [_att_scores_tile(q_ref[qrows[u], :] * (DIL_DH ** -0.5), k_ref[krows[u], :], slope_d, has_prev[u]) for u in us]
            mx = [jnp.max(a, axis=-1, keepdims=True) for a in sc]
            p = [jnp.exp(a - m) for a, m in zip(sc, mx)]
            num = [_dot(p[u], v_ref[krows[u], :]) for u in us]
            for u in us:
                emit(qrows[u], num[u], jnp.sum(p[u], axis=-1, keepdims=True), mx[u])
            return carry

        if nb == 2:
            lax.fori_loop(0, dil // ATT_UNROLL, step_whole, 0)
        elif nb % 2 == 0:
            lax.fori_loop(0, dil * nb // 2 // ATT_UNROLL, step_tile, 0)
        else:
            lax.fori_loop(0, dil * nb // ATT_UNROLL, step, 0)

    def col(off):
        return pl.BlockSpec((s, DIL_DH), lambda h: (0, off // DIL_DH + group * DIL_HEADS + h))

    out = pl.BlockSpec((s, DIL_DH), lambda h: (0, h))
    return pl.pallas_call(
        body, name=f"att_fwd{group}", grid=(DIL_HEADS,), in_specs=[col(OFF_Q_B), col(OFF_K_B), col(OFF_V_B)],
        out_specs=[out, out], out_shape=[jax.ShapeDtypeStruct((s, DIL_W), F32)] * 2,
        compiler_params=_cparams("parallel"))(proj, proj, proj)


def _att_bwd(proj, group, do, lse, delta):
    s = proj.shape[0]
    dil = DIL_GROUPS[group][1]
    nb = s // dil // ATT_BLOCK

    def body(q_ref, k_ref, v_ref, do_ref, lse_ref, dl_ref, dq_ref, dk_ref, dv_ref, dq_acc, dk_acc, dv_acc):
        slope_d = _slope(group, pl.program_id(0)) * float(dil)
        dk_acc[...] = jnp.zeros_like(dk_acc)
        dv_acc[...] = jnp.zeros_like(dv_acc)

        def step(i, carry):
            curs, prvs, has_prev = _att_blocks(i, dil, nb)
            us = range(ATT_UNROLL)
            qb = [q_ref[c, :] * (DIL_DH ** -0.5) for c in curs]
            k_cur, k_prev = [k_ref[c, :] for c in curs], [k_ref[p, :] for p in prvs]
            v_cur, v_prev = [v_ref[c, :] for c in curs], [v_ref[p, :] for p in prvs]
            sc = [_att_scores(qb[u], k_cur[u], k_prev[u], slope_d, has_prev[u]) for u in us]
            lse_b, delta_b, dob = [lse_ref[c, :] for c in curs], [dl_ref[c, :] for c in curs], [do_ref[c, :] for c in curs]
            p_cur = [jnp.exp(sc[u][0] - lse_b[u]) for u in us]
            p_prev = [jnp.exp(sc[u][1] - lse_b[u]) for u in us]
            ds_cur = [p_cur[u] * (_dot_nt(dob[u], v_cur[u]) - delta_b[u]) for u in us]
            ds_prev = [p_prev[u] * (_dot_nt(dob[u], v_prev[u]) - delta_b[u]) for u in us]
            dq = [(_dot(ds_cur[u], k_cur[u]) + _dot(ds_prev[u], k_prev[u])) * (DIL_DH ** -0.5) for u in us]
            dk_c = [_dot_tn(ds_cur[u], qb[u]) for u in us]
            dv_c = [_dot_tn(p_cur[u], dob[u]) for u in us]
            dk_p = [_dot_tn(ds_prev[u], qb[u]) for u in us]
            dv_p = [_dot_tn(p_prev[u], dob[u]) for u in us]
            for u in us:
                dq_acc[curs[u], :] = dq[u]
                dk_acc[curs[u], :] += dk_c[u]
                dv_acc[curs[u], :] += dv_c[u]
            for u in us:
                dk_acc[prvs[u], :] += dk_p[u]
                dv_acc[prvs[u], :] += dv_p[u]
            return carry

        def step_whole(i, carry):
            rows = [pl.ds(i * ATT_UNROLL + u, 2 * ATT_BLOCK, stride=dil) for u in range(ATT_UNROLL)]
            qb = [q_ref[r, :] * (DIL_DH ** -0.5) for r in rows]
            kk, vv, dob = [k_ref[r, :] for r in rows], [v_ref[r, :] for r in rows], [do_ref[r, :] for r in rows]
            sc = [_att_scores_whole(qb[u], kk[u], slope_d) for u in range(ATT_UNROLL)]
            p = [jnp.exp(sc[u] - lse_ref[r, :][:, 0:1]) for u, r in enumerate(rows)]
            ds = [p[u] * (_dot_nt(dob[u], vv[u]) - dl_ref[r, :][:, 0:1]) for u, r in enumerate(rows)]
            dq = [_dot(ds[u], kk[u]) * (DIL_DH ** -0.5) for u in range(ATT_UNROLL)]
            dk = [_dot_tn(ds[u], qb[u]) for u in range(ATT_UNROLL)]
            dv = [_dot_tn(p[u], dob[u]) for u in range(ATT_UNROLL)]
            for u, r in enumerate(rows):
                dq_acc[r, :] = dq[u]
                dk_acc[r, :] = dk[u]
                dv_acc[r, :] = dv[u]
            return carry

        def step_tile(i, carry):
            qrows, krows, has_prev = _att_tiles(i, dil, nb)
            us = range(ATT_UNROLL)
            qb = [q_ref[r, :] * (DIL_DH ** -0.5) for r in qrows]
            kk, vv, dob = [k_ref[r, :] for r in krows], [v_ref[r, :] for r in krows], [do_ref[r, :] for r in qrows]
            sc = [_att_scores_tile(qb[u], kk[u], slope_d, has_prev[u]) for u in us]
            p = [jnp.exp(sc[u] - lse_ref[qrows[u], :][:, 0:1]) for u in us]
            ds = [p[u] * (_dot_nt(dob[u], vv[u]) - dl_ref[qrows[u], :][:, 0:1]) for u in us]
            dq = [_dot(ds[u], kk[u]) * (DIL_DH ** -0.5) for u in us]
            dk = [_dot_tn(ds[u], qb[u]) for u in us]
            dv = [_dot_tn(p[u], dob[u]) for u in us]
            for u in us:
                dq_acc[qrows[u], :] = dq[u]
                dk_acc[krows[u], :] += dk[u]
                dv_acc[krows[u], :] += dv[u]
            return carry

        if nb == 2:
            lax.fori_loop(0, dil // ATT_UNROLL, step_whole, 0)
        elif nb % 2 == 0:
            lax.fori_loop(0, dil * nb // 2 // ATT_UNROLL, step_tile, 0)
        else:
            lax.fori_loop(0, dil * nb // ATT_UNROLL, step, 0)
        dq_ref[...] = dq_acc[...].astype(BF16)
        dk_ref[...] = dk_acc[...].astype(BF16)
        dv_ref[...] = dv_acc[...].astype(BF16)

    def col(off):
        return pl.BlockSpec((s, DIL_DH), lambda h: (0, off // DIL_DH + group * DIL_HEADS + h))

    hd = pl.BlockSpec((s, DIL_DH), lambda h: (0, h))
    return pl.pallas_call(
        body, name=f"att_bwd{group}", grid=(DIL_HEADS,),
        in_specs=[col(OFF_Q_B), col(OFF_K_B), col(OFF_V_B), hd, hd, hd], out_specs=[hd, hd, hd],
        out_shape=[jax.ShapeDtypeStruct((s, DIL_W), BF16)] * 3,
        scratch_shapes=[pltpu.VMEM((s, DIL_DH), F32)] * 3,
        compiler_params=_cparams("parallel"))(proj, proj, proj, do, lse, delta)


def _att_merge(parts, proj):
    s = proj.shape[0]

    def body(o0, l0, o1, l1, o2, l2, z_ref, ob_ref, o_ref, lse_ref, obt_ref):
        m = jnp.maximum(jnp.maximum(l0[...], l1[...]), l2[...])
        num = jnp.zeros_like(m)
        den = jnp.zeros_like(m)
        for og, lg in ((o0, l0), (o1, l1), (o2, l2)):
            sc = jnp.exp(lg[...] - m)
            num = num + og[...] * sc
            den = den + sc
        o = num / den
        o_ref[...] = o
        lse_ref[...] = m + jnp.log(den)
        ob = o * _silu(z_ref[...])
        ob_ref[...] = ob.astype(BF16)
        obt_ref[...] = ob.T.astype(BF16)

    row = pl.BlockSpec((ROW_TILE, DIL_W), lambda i: (i, 0))
    flat = [a for p in parts for a in p]
    return pl.pallas_call(
        body, name="att_merge", grid=(s // ROW_TILE,),
        in_specs=[row] * 6 + [pl.BlockSpec((ROW_TILE, DIL_W), lambda i: (i, OFF_Z_B // DIL_W))],
        out_specs=[row, row, row, pl.BlockSpec((DIL_W, ROW_TILE), lambda i: (0, i))],
        out_shape=[jax.ShapeDtypeStruct((s, DIL_W), BF16), jax.ShapeDtypeStruct((s, DIL_W), F32),
                   jax.ShapeDtypeStruct((s, DIL_W), F32), jax.ShapeDtypeStruct((DIL_W, s), BF16)],
        compiler_params=_cparams("parallel"))(*flat, proj)


def _att_merge_bwd(o, proj, dob, dproj):
    s = o.shape[0]

    def body(o_ref, z_ref, d_ref, dproj_in, do_ref, dl_ref, dz_ref):
        ov, zv, dv = o_ref[...], z_ref[...], d_ref[...]
        do = dv * _silu(zv)
        do_ref[...] = do
        dz_ref[...] = (dv * ov * _silu_grad(zv)).astype(BF16)
        for h in range(DIL_HEADS):
            sl = slice(h * DIL_DH, (h + 1) * DIL_DH)
            dl_ref[:, sl] = jnp.broadcast_to(jnp.sum(do[:, sl] * ov[:, sl], axis=-1, keepdims=True), (ROW_TILE, DIL_DH))

    row = pl.BlockSpec((ROW_TILE, DIL_W), lambda i: (i, 0))
    return pl.pallas_call(
        body, name="att_merge_bwd", grid=(s // ROW_TILE,),
        in_specs=[row, pl.BlockSpec((ROW_TILE, DIL_W), lambda i: (i, OFF_Z_B // DIL_W)), row, DPROJ_IN],
        out_specs=[row, row, pl.BlockSpec((ROW_TILE, DIL_W), lambda i: (i, OFF_Z_B // DIL_W))],
        out_shape=[jax.ShapeDtypeStruct((s, DIL_W), F32), jax.ShapeDtypeStruct((s, DIL_W), F32),
                   jax.ShapeDtypeStruct((s, PW), BF16)],
        input_output_aliases={3: 2},
        compiler_params=_cparams("parallel"))(o, proj, dob, dproj)


def _merge(proj, ya, yb):
    s = proj.shape[0]

    def body(ga_ref, gb_ref, ya_ref, yb_ref, o_ref, ot_ref):
        m = _sigmoid(ga_ref[...]) * ya_ref[...] + _sigmoid(gb_ref[...]) * yb_ref[...]
        o_ref[...] = m.astype(BF16)
        ot_ref[...] = m.T.astype(BF16)

    row = pl.BlockSpec((ROW_TILE, D_MODEL), lambda i: (i, 0))
    return pl.pallas_call(
        body, name="merge", grid=(s // ROW_TILE,),
        in_specs=[pl.BlockSpec((ROW_TILE, D_MODEL), lambda i: (i, OFF_G_A // D_MODEL)),
                  pl.BlockSpec((ROW_TILE, D_MODEL), lambda i: (i, OFF_G_B // D_MODEL)), row, row],
        out_specs=[row, pl.BlockSpec((D_MODEL, ROW_TILE), lambda i: (0, i))],
        out_shape=[jax.ShapeDtypeStruct((s, D_MODEL), BF16), jax.ShapeDtypeStruct((D_MODEL, s), BF16)],
        compiler_params=_cparams("parallel"))(proj, proj, ya, yb)


def _merge_bwd(proj, ya, yb, dm):
    s = proj.shape[0]

    def body(ga_ref, gb_ref, ya_ref, yb_ref, dm_ref, dya_ref, dyb_ref, dga_ref, dgb_ref):
        dmv = dm_ref[...]
        sa, sb = _sigmoid(ga_ref[...]), _sigmoid(gb_ref[...])
        dya_ref[...] = (dmv * sa).astype(BF16)
        dyb_ref[...] = (dmv * sb).astype(BF16)
        dga_ref[...] = (dmv * ya_ref[...] * sa * (1.0 - sa)).astype(BF16)
        dgb_ref[...] = (dmv * yb_ref[...] * sb * (1.0 - sb)).astype(BF16)

    row = pl.BlockSpec((ROW_TILE, D_MODEL), lambda i: (i, 0))
    return pl.pallas_call(
        body, name="merge_bwd", grid=(s // ROW_TILE,),
        in_specs=[pl.BlockSpec((ROW_TILE, D_MODEL), lambda i: (i, OFF_G_A // D_MODEL)),
                  pl.BlockSpec((ROW_TILE, D_MODEL), lambda i: (i, OFF_G_B // D_MODEL)), row, row, row],
        out_specs=[row] * 4, out_shape=[jax.ShapeDtypeStruct((s, D_MODEL), BF16)] * 4,
        compiler_params=_cparams("parallel"))(proj, proj, ya, yb, dm)


def _final(x, t, fw, tgt):
    s, d = x.shape

    def body(x_ref, t_ref, w_ref, y_ref, dx_ref, dw_ref, l_ref):
        i = pl.program_id(0)
        x2 = x_ref[...] + t_ref[...]
        wv = w_ref[...]
        r = lax.rsqrt(jnp.mean(x2 * x2, axis=-1, keepdims=True) + NORM_EPS)
        e = x2 * r * wv - y_ref[...]
        lrow = jnp.mean(e * e, axis=-1, keepdims=True)
        lpart = jnp.broadcast_to(0.5 * jnp.sum(lrow, axis=0, keepdims=True), (1, 128))
        dy = e * (1.0 / d)
        dwp = jnp.sum(dy * x2 * r, axis=0, keepdims=True)
        dyw = dy * wv
        dx_ref[...] = r * dyw - x2 * (r * r * r) * jnp.mean(dyw * x2, axis=-1, keepdims=True)

        @pl.when(i == 0)
        def _():
            dw_ref[...] = dwp
            l_ref[...] = lpart

        @pl.when(i > 0)
        def _():
            dw_ref[...] += dwp
            l_ref[...] += lpart

    row = pl.BlockSpec((ROW_TILE, d), lambda i: (i, 0))
    vec = pl.BlockSpec((1, d), lambda i: (0, 0))
    return pl.pallas_call(
        body, name="final", grid=(s // ROW_TILE,), in_specs=[row, row, vec, row],
        out_specs=[row, vec, pl.BlockSpec((1, 128), lambda i: (0, 0))],
        out_shape=[jax.ShapeDtypeStruct((s, d), F32), jax.ShapeDtypeStruct((1, d), F32), jax.ShapeDtypeStruct((1, 128), F32)],
        compiler_params=_cparams("arbitrary"))(x, t, fw, tgt)


def _adamw(w, g, m, v, name):
    r, c = w.shape
    cap = max(8, (1 << 18) // c)
    divisors = [t for t in range(8, min(r, cap) + 1, 8) if r % t == 0]
    tr = r if r <= 8 else (max(divisors) if divisors else cap)

    def body(w_ref, g_ref, m_ref, v_ref, d_ref, nm_ref, nv_ref):
        gv = g_ref[...]
        mn = ADAM_B1 * m_ref[...] + (1.0 - ADAM_B1) * gv
        vn = ADAM_B2 * v_ref[...] + (1.0 - ADAM_B2) * (gv * gv)
        m_hat = mn / (1.0 - ADAM_B1 ** ADAM_STEP)
        v_hat = vn / (1.0 - ADAM_B2 ** ADAM_STEP)
        d_ref[...] = -ADAM_LR * (m_hat / (jnp.sqrt(v_hat) + ADAM_EPS) + ADAM_WD * w_ref[...])
        nm_ref[...] = mn
        nv_ref[...] = vn

    blk = pl.BlockSpec((tr, c), lambda i: (i, 0))
    return pl.pallas_call(
        body, name=name, grid=(pl.cdiv(r, tr),), in_specs=[blk] * 4, out_specs=[blk] * 3,
        out_shape=[jax.ShapeDtypeStruct((r, c), F32)] * 3, compiler_params=_cparams("parallel"))(w, g, m, v)


HBM_SPEC = pl.BlockSpec(memory_space=pl.ANY)


def _place():
    x, y, c = lax.axis_index("x"), lax.axis_index("y"), lax.axis_index("c")
    chips = [(1 - x, y), (x, 1 - y), (1 - x, 1 - y)]
    return x, y, c, chips


def _ag_weights(packs):
    na = len(packs)
    nsem = 8

    def body(*refs):
        p_refs, out_refs = refs[:na], refs[na:2 * na]
        send_sems, recv_sems = refs[2 * na:]
        x, y, c, _ = _place()
        me, sib, j = (x, y, c), (x, y, 1 - c), 2 * x + y
        xn, yn = (1 - x, y, c), (x, 1 - y, c)
        jx, jy, jd = 2 * (1 - x) + y, 2 * x + (1 - y), 2 * (1 - x) + (1 - y)

        def rc(a, k, src, dst, to):
            return pltpu.make_async_remote_copy(src_ref=src, dst_ref=dst, send_sem=send_sems.at[nsem * a + k],
                                                recv_sem=recv_sems.at[nsem * a + k], device_id=to, device_id_type=MESH)

        sent = []
        for a in range(na):
            mine, land = p_refs[a].at[c], out_refs[a].at[j, c]
            sent += [rc(a, 0, mine, land, xn), rc(a, 1, mine, land, yn), rc(a, 7, p_refs[a], out_refs[a].at[j], sib)]
        for cp in sent:
            cp.start()
        for a in range(na):
            half = p_refs[a].shape[1] // 2
            top, bottom = pl.ds(0, half), pl.ds(half, half)
            from_x, from_y, from_d = out_refs[a].at[jx, c], out_refs[a].at[jy, c], out_refs[a].at[jd, c]
            rc(a, 0, p_refs[a].at[c], from_x, me).wait_recv()
            later = [rc(a, 2, from_x.at[top], from_x.at[top], yn), rc(a, 4, from_x, from_x, sib)]
            for cp in later:
                cp.start()
            sent += later
            rc(a, 1, p_refs[a].at[c], from_y, me).wait_recv()
            later = [rc(a, 3, from_y.at[bottom], from_y.at[bottom], xn), rc(a, 5, from_y, from_y, sib)]
            for cp in later:
                cp.start()
            sent += later
            rc(a, 2, from_d.at[top], from_d.at[top], me).wait_recv()
            rc(a, 3, from_d.at[bottom], from_d.at[bottom], me).wait_recv()
            cp = rc(a, 6, from_d, from_d, sib)
            cp.start()
            sent.append(cp)
        for a in range(na):
            for k, jj in ((4, jx), (5, jy), (6, jd)):
                rc(a, k, p_refs[a].at[c], out_refs[a].at[jj, 1 - c], me).wait_recv()
            rc(a, 7, p_refs[a], out_refs[a].at[j], me).wait_recv()
        for cp in sent:
            cp.wait_send()

    return pl.pallas_call(
        body, name="ag_weights",
        out_shape=[jax.ShapeDtypeStruct((N_CHIPS,) + p.shape, p.dtype) for p in packs],
        in_specs=[HBM_SPEC] * na, out_specs=[HBM_SPEC] * na,
        scratch_shapes=[pltpu.SemaphoreType.DMA((nsem * na,)), pltpu.SemaphoreType.DMA((nsem * na,))])(*packs)


def _rs_pair(dwpt, gpack):
    n = N_CHIPS
    hw = SHARD_PAD // 2

    def body(d_ref, g_ref, out_d, out_g, send_sems, recv_sems):
        x, y, c, _ = _place()
        sib = (x, y, 1 - c)
        cps = []
        for p in range(n):
            start = pl.multiple_of(WIN_BASE[p] + (1 - c) * hw, TILE_ROWS)
            cps.append(pltpu.make_async_remote_copy(
                src_ref=d_ref.at[pl.ds(start, hw)], dst_ref=out_d.at[p], send_sem=send_sems.at[p],
                recv_sem=recv_sems.at[p], device_id=sib, device_id_type=MESH))
            cps.append(pltpu.make_async_remote_copy(
                src_ref=g_ref.at[p, 1 - c], dst_ref=out_g.at[p], send_sem=send_sems.at[n + p],
                recv_sem=recv_sems.at[n + p], device_id=sib, device_id_type=MESH))
        for cp in cps:
            cp.start()
        for cp in cps:
            cp.wait_recv()
        for cp in cps:
            cp.wait_send()

    return pl.pallas_call(
        body, name="rs_pair",
        out_shape=[jax.ShapeDtypeStruct((n, hw, dwpt.shape[1]), dwpt.dtype),
                   jax.ShapeDtypeStruct((n,) + gpack.shape[2:], gpack.dtype)],
        in_specs=[HBM_SPEC] * 2, out_specs=[HBM_SPEC] * 2,
        scratch_shapes=[pltpu.SemaphoreType.DMA((2 * n,)), pltpu.SemaphoreType.DMA((2 * n,))])(dwpt, gpack)


def _add_halves_win(dwpt, other, c):
    n, rh, wd = other.shape
    tr = _row_tile(rh)

    def body(s_ref, d_ref, o_ref, out_ref):
        out_ref[0] = (d_ref[...] + o_ref[0]).astype(BF16)

    scal = jnp.concatenate([jnp.reshape(c, (1,)).astype(jnp.int32), jnp.asarray(WIN_BASE, jnp.int32)])
    grid_spec = pltpu.PrefetchScalarGridSpec(
        num_scalar_prefetch=1, grid=(n, rh // tr),
        in_specs=[pl.BlockSpec((pl.Element(tr), pl.Element(wd)),
                               lambda p, i, sr: (pl.multiple_of(sr[1 + p] + sr[0] * rh + i * tr, TILE_ROWS), 0)),
                  pl.BlockSpec((1, tr, wd), lambda p, i, sr: (p, i, 0))],
        out_specs=pl.BlockSpec((1, tr, wd), lambda p, i, sr: (p, i, 0)))
    return pl.pallas_call(
        body, name="add_halves_in", grid_spec=grid_spec, out_shape=jax.ShapeDtypeStruct((n, rh, wd), BF16),
        compiler_params=_cparams("parallel", "parallel"))(scal, dwpt, other)


SEM_SPEC = pl.BlockSpec(memory_space=pltpu.SEMAPHORE)
DATAFLOW_EFFECT = pltpu.SideEffectType.DATAFLOW_SIDE_EFFECTING


def _rs_chips_start(csums):
    na = len(csums)

    def body(*refs):
        s_refs, land_refs = refs[:na], refs[na:2 * na]
        send_sems, recv_sems = refs[2 * na], refs[2 * na + 1]
        token = refs[-1]
        x, y, c, chips = _place()
        j = 2 * x + y
        for a in range(na):
            for k, (cx, cy) in enumerate(chips):
                pltpu.make_async_remote_copy(src_ref=s_refs[a].at[2 * cx + cy], dst_ref=land_refs[a].at[j],
                                             send_sem=send_sems.at[3 * a + k], recv_sem=recv_sems.at[3 * a + k],
                                             device_id=(cx, cy, c), device_id_type=MESH).start()
        token[...] = jnp.zeros_like(token)

    hbm = [pltpu.HBM(s.shape, s.dtype) for s in csums]
    args = [pltpu.with_memory_space_constraint(s, pltpu.HBM) for s in csums]
    args += [pltpu.with_memory_space_constraint(lax.empty(s.shape, s.dtype), pltpu.HBM) for s in csums]
    res = pl.pallas_call(
        body, name="rs_chips_start",
        out_shape=(pltpu.SemaphoreType.DMA((3 * na,)), pltpu.SemaphoreType.DMA((3 * na,)), *hbm, *hbm,
                   jax.ShapeDtypeStruct((8, 128), F32)),
        in_specs=[pl.BlockSpec(memory_space=pltpu.HBM)] * (2 * na),
        out_specs=(SEM_SPEC, SEM_SPEC, *[pl.BlockSpec(memory_space=pltpu.HBM)] * (2 * na),
                   pl.BlockSpec(memory_space=pltpu.VMEM)),
        input_output_aliases={i: 2 + i for i in range(2 * na)},
        compiler_params=pltpu.CompilerParams(has_side_effects=DATAFLOW_EFFECT))(*args)
    return res[0], res[1], list(res[2:2 + na]), list(res[2 + na:2 + 2 * na]), res[-1]


def _rs_chips_wait(send_sems, recv_sems, csums, lands, after):
    na = len(csums)

    def body(*refs):
        s_refs, land_refs = refs[:na], refs[na:2 * na]
        send_sems, recv_sems = refs[2 * na], refs[2 * na + 1]
        x, y, c, chips = _place()
        j = 2 * x + y
        for a in range(na):
            for k, (cx, cy) in enumerate(chips):
                cp = pltpu.make_async_remote_copy(src_ref=s_refs[a].at[2 * cx + cy], dst_ref=land_refs[a].at[2 * cx + cy],
                                                  send_sem=send_sems.at[3 * a + k], recv_sem=recv_sems.at[3 * a + k],
                                                  device_id=(cx, cy, c), device_id_type=MESH)
                cp.wait_send()
                cp.wait_recv()

    hbm = [pltpu.HBM(s.shape, s.dtype) for s in csums]
    res = pl.pallas_call(
        body, name="rs_chips_wait", out_shape=(*hbm, *hbm),
        in_specs=[pl.BlockSpec(memory_space=pltpu.HBM)] * (2 * na) + [SEM_SPEC, SEM_SPEC, pl.BlockSpec(memory_space=pl.ANY)],
        out_specs=tuple([pl.BlockSpec(memory_space=pltpu.HBM)] * (2 * na)),
        input_output_aliases={i: i for i in range(2 * na)},
        compiler_params=pltpu.CompilerParams(has_side_effects=DATAFLOW_EFFECT))(*csums, *lands, send_sems, recv_sems, after)
    return list(res[:na]), list(res[na:])


SWAP_CHUNKS = 4


def _pair_swap(halves):
    na = len(halves)

    def body(*refs):
        h_refs, out_refs = refs[:na], refs[na:2 * na]
        send_sems, recv_sems = refs[2 * na:]
        x, y, c, _ = _place()
        cps = []
        for a in range(na):
            rows = h_refs[a].shape[0] // SWAP_CHUNKS
            assert rows * SWAP_CHUNKS == h_refs[a].shape[0]
            for q in range(SWAP_CHUNKS):
                k = SWAP_CHUNKS * a + q
                cps.append(pltpu.make_async_remote_copy(
                    src_ref=h_refs[a].at[pl.ds(q * rows, rows)], dst_ref=out_refs[a].at[pl.ds(q * rows, rows)],
                    send_sem=send_sems.at[k], recv_sem=recv_sems.at[k], device_id=(x, y, 1 - c), device_id_type=MESH))
        for cp in cps:
            cp.start()
        for cp in cps:
            cp.wait_recv()
        for cp in cps:
            cp.wait_send()

    return pl.pallas_call(
        body, name="pair_swap", out_shape=[jax.ShapeDtypeStruct(h.shape, h.dtype) for h in halves],
        in_specs=[HBM_SPEC] * na, out_specs=[HBM_SPEC] * na,
        scratch_shapes=[pltpu.SemaphoreType.DMA((SWAP_CHUNKS * na,)), pltpu.SemaphoreType.DMA((SWAP_CHUNKS * na,))])(*halves)


def _ag_small(v):
    m_per, n = v.shape

    def body(x_ref, out_ref, send_sems, recv_sems, local_sem):
        x, y, c, chips = _place()
        me, sibling = (x, y, c), (x, y, 1 - c)

        def rows(px, py, pc):
            return out_ref.at[pl.ds((4 * px + 2 * py + pc) * m_per, m_per), :]

        def copy(k, block, to, src=None):
            return pltpu.make_async_remote_copy(
                src_ref=rows(*block) if src is None else src, dst_ref=rows(*block), send_sem=send_sems.at[k],
                recv_sem=recv_sems.at[k], device_id=to, device_id_type=MESH)

        mine = pltpu.make_async_copy(x_ref, rows(*me), local_sem)
        mine.start()
        first = [copy(0, me, sibling, src=x_ref)]
        first += [copy(1 + k, me, (*chip, c), src=x_ref) for k, chip in enumerate(chips)]
        for cp in first:
            cp.start()
        passed = [copy(4 + k, (*chip, c), sibling) for k, chip in enumerate(chips)]
        for k, chip in enumerate(chips):
            copy(1 + k, (*chip, c), me).wait_recv()
            passed[k].start()
        copy(0, sibling, me).wait_recv()
        for k, chip in enumerate(chips):
            copy(4 + k, (*chip, 1 - c), me).wait_recv()
        for cp in first + passed:
            cp.wait_send()
        mine.wait()

    return pl.pallas_call(
        body, name="ag_small", out_shape=jax.ShapeDtypeStruct((8 * m_per, n), v.dtype),
        in_specs=[pl.BlockSpec(memory_space=pltpu.VMEM)], out_specs=pl.BlockSpec(memory_space=pltpu.VMEM),
        scratch_shapes=[pltpu.SemaphoreType.DMA((7,)), pltpu.SemaphoreType.DMA((7,)), pltpu.SemaphoreType.DMA])(v)


def _sum_blocks(a, nblk, name):
    rows, wd = a.shape
    r = rows // nblk
    tr = min(r, ROW_TILE)
    assert r % tr == 0

    def body(*refs):
        acc = refs[0][...].astype(F32)
        for ref in refs[1:nblk]:
            acc = acc + ref[...].astype(F32)
        refs[nblk][...] = acc

    nt = r // tr
    return pl.pallas_call(
        body, name=name, grid=(nt,),
        in_specs=[pl.BlockSpec((tr, wd), functools.partial(lambda i, b: (b * nt + i, 0), b=b)) for b in range(nblk)],
        out_specs=pl.BlockSpec((tr, wd), lambda i: (i, 0)),
        out_shape=jax.ShapeDtypeStruct((r, wd), F32), compiler_params=_cparams("parallel"))(*([a] * nblk))


def _row_tile(rows):
    best = max(t for t in range(16, 513, 16) if rows % t == 0)
    return best


def _sum_chips(by_src, csum, j, name):
    n, rh, wd = by_src.shape
    tr = _row_tile(rh)

    def body(j_ref, *refs):
        own = refs[n][0].astype(F32)
        acc = None
        for k in range(n):
            term = jnp.where(j_ref[0] == k, own, refs[k][0].astype(F32))
            acc = term if acc is None else acc + term
        refs[n + 1][...] = acc

    def other(k):
        return pl.BlockSpec((1, tr, wd), lambda i, jr: (jnp.where(jr[0] == k, (k + 1) % n, k), i, 0))

    grid_spec = pltpu.PrefetchScalarGridSpec(
        num_scalar_prefetch=1, grid=(rh // tr,),
        in_specs=[other(k) for k in range(n)] + [pl.BlockSpec((1, tr, wd), lambda i, jr: (jr[0], i, 0))],
        out_specs=pl.BlockSpec((tr, wd), lambda i, jr: (i, 0)))
    return pl.pallas_call(
        body, name=name, grid_spec=grid_spec, out_shape=jax.ShapeDtypeStruct((rh, wd), F32),
        compiler_params=_cparams("parallel"))(jnp.reshape(j, (1,)).astype(jnp.int32), *([by_src] * n), csum)


def _add_halves(gpack, other, c, name):
    n, _, rh, wd = gpack.shape
    tr = _row_tile(rh)

    def body(c_ref, g_ref, o_ref, out_ref):
        out_ref[0] = (g_ref[0, 0] + o_ref[0]).astype(BF16)

    grid_spec = pltpu.PrefetchScalarGridSpec(
        num_scalar_prefetch=1, grid=(n, rh // tr),
        in_specs=[pl.BlockSpec((1, 1, tr, wd), lambda p, i, cr: (p, cr[0], i, 0)),
                  pl.BlockSpec((1, tr, wd), lambda p, i, cr: (p, i, 0))],
        out_specs=pl.BlockSpec((1, tr, wd), lambda p, i, cr: (p, i, 0)))
    return pl.pallas_call(
        body, name=name, grid_spec=grid_spec, out_shape=jax.ShapeDtypeStruct((n, rh, wd), BF16),
        compiler_params=_cparams("parallel", "parallel"))(jnp.reshape(c, (1,)).astype(jnp.int32), gpack, other)


PACK_W = 1024
ROWS_O_DN = DN_W // N_CHIPS
ROWS_O_DIL = DIL_W * (D_MODEL // N_CHIPS) // PACK_W
ROWS_OUT = D_MODEL // N_CHIPS
ROWS_CONV = 4 * (3 * DN_W // N_CHIPS) // PACK_W
R1 = ROWS_O_DN
R2 = R1 + ROWS_O_DIL
R3 = R2 + ROWS_OUT
R4 = R3 + 16
R5 = R4 + 16
PACK_ROWS = 704
HALF_ROWS = PACK_ROWS // 2
SHARD_PAD = 2880


R6 = R5 + 2 * DN_HEADS

TILE_ROWS = 16
BA_IN_SHARD1 = REF_OFF_BA - SHARD_W
LOCAL_START = (0, SHARD_W, 2 * SHARD_W - 2 * DN_HEADS, 3 * SHARD_W - 2 * DN_HEADS)
LOCAL_END = LOCAL_START[1:] + (OFF_BA,)
WIN_BASE = tuple(s // TILE_ROWS * TILE_ROWS for s in LOCAL_START)


def _to_window(k, shard):
    nba = 2 * DN_HEADS
    body = shard
    if k == 1:
        row = lax.broadcasted_iota(jnp.int32, (SHARD_W - nba, 1), 0)
        body = jnp.where(row < BA_IN_SHARD1, shard[:SHARD_W - nba], shard[nba:])
    lead = LOCAL_START[k] - WIN_BASE[k]
    return jnp.pad(body, ((lead, SHARD_PAD - lead - body.shape[0]), (0, 0)))


def _from_window(k, win, ba):
    nba = 2 * DN_HEADS
    lead = LOCAL_START[k] - WIN_BASE[k]
    if k != 1:
        return win[lead:lead + SHARD_W]
    row = lax.broadcasted_iota(jnp.int32, (SHARD_W, 1), 0)
    before = win[lead:lead + SHARD_W]
    after = jnp.pad(win, ((nba, 0), (0, 0)))[lead:lead + SHARD_W]
    mid = jnp.pad(ba, ((BA_IN_SHARD1, SHARD_W - BA_IN_SHARD1 - nba), (0, 0)))
    return jnp.where(row < BA_IN_SHARD1, before, jnp.where(row < BA_IN_SHARD1 + nba, mid, after))


def _stack_windows(wins, ba):
    pieces = []
    for k in range(N_CHIPS):
        lo = WIN_BASE[k] + (TILE_ROWS if k else 0)
        hi = LOCAL_END[k] // TILE_ROWS * TILE_ROWS
        pieces.append(wins[k][lo - WIN_BASE[k]:hi - WIN_BASE[k]])
        if k + 1 < N_CHIPS:
            assert hi == WIN_BASE[k + 1]
            pieces.append(wins[k][hi - WIN_BASE[k]:hi - WIN_BASE[k] + TILE_ROWS] + wins[k + 1][:TILE_ROWS])
    pieces += [ba, jnp.zeros((PW - OFF_BA - ba.shape[0], ba.shape[1]), ba.dtype)]
    out = jnp.concatenate(pieces, axis=0)
    assert out.shape[0] == PW
    return out


def _to_ref_layout(wpt):
    return jnp.concatenate([wpt[:REF_OFF_BA], wpt[OFF_BA:OFF_BA + 2 * DN_HEADS], wpt[REF_OFF_BA:OFF_BA]], axis=0)


def _from_ref_layout(wt):
    pad = jnp.zeros((PW - PROJ_W, wt.shape[1]), wt.dtype)
    return jnp.concatenate([wt[:REF_OFF_BA], wt[REF_OFF_BA + 2 * DN_HEADS:], wt[REF_OFF_BA:REF_OFF_BA + 2 * DN_HEADS], pad],
                           axis=0)


def _local_step(x, tgt, norm_w, wpt, conv_full, a_log, dt_bias, dn_norm_w, w_o_dn, w_o_dil, w_out, final_norm_w):
    s = x.shape[0]
    h, h_t = _rms_in(x, norm_w)
    proj = _matmul(h, wpt, F32, 2048, 1280, 1024, "proj", nt=True)
    c_pre, qkv = _conv_fwd(proj, conv_full)
    gate_par = jnp.zeros((8, 128), F32).at[0, 8:16].set(a_log[0]).at[1, 8:16].set(dt_bias[0])
    bg = _gates_fwd(proj, gate_par)
    o_a, u, w, vn, tmat, states = _gdr_fwd(qkv, bg)
    oa2, oa2_t = _gdr_out(o_a, proj, dn_norm_w)
    ya = _matmul(oa2, w_o_dn, F32, 512, 1024, 1024, "ya")
    parts = [_att_fwd(proj, g) for g in range(N_DIL)]
    ob, o_att, lse, ob_t = _att_merge(parts, proj)
    yb = _matmul(ob, w_o_dil, F32, 512, 1024, 512, "yb")
    mg, mg_t = _merge(proj, ya, yb)
    t = _matmul(mg, w_out, F32, 512, 1024, 1024, "t_out")
    dx2, dfw, lpart = _final(x, t, final_norm_w, tgt)

    dmg = _matmul(dx2, w_out, F32, 512, 1024, 1024, "d_merged", nt=True)
    dw_out = _matmul(mg_t, dx2, F32, 1024, 1024, 1024, "dw_out")
    dya, dyb, dga, dgb = _merge_bwd(proj, ya, yb, dmg)
    doa2 = _matmul(dya, w_o_dn, F32, 512, 1024, 1024, "d_oa2", nt=True)
    dw_o_dn = _matmul(oa2_t, dya, F32, 1024, 1024, 1024, "dw_o_dn")
    dob = _matmul(dyb, w_o_dil, F32, 512, 512, 1024, "d_ob", nt=True)
    dw_o_dil = _matmul(ob_t, dyb, F32, 512, 1024, 1024, "dw_o_dil")
    do_a, dproj, ddnw = _gdr_out_bwd(o_a, proj, dn_norm_w, doa2)
    dq_a, dk_a, dv_a, dbg = _gdr_bwd(qkv, bg, u, w, vn, tmat, states, do_a)
    dproj, dpar = _gates_bwd(proj, gate_par, dbg, dproj)
    dc = _conv_bwd_act(c_pre, dq_a, dk_a, dv_a)
    dproj, dconv = _conv_bwd(proj, dc, conv_full, dproj)
    do_att, delta, dproj = _att_merge_bwd(o_att, proj, dob, dproj)
    dqkv_b = [_att_bwd(proj, g, do_att, lse, delta) for g in range(N_DIL)]
    pieces = [(OFF_Q_B + (N_DIL * i + g) * DIL_W, dqkv_b[g][i]) for i in range(3) for g in range(N_DIL)]
    for off, piece in pieces + [(OFF_G_A, dga), (OFF_G_B, dgb)]:
        dproj = lax.dynamic_update_slice(dproj, piece, (0, off))
    dwpt, dwpt_b = _matmul(h_t, dproj, F32, 1024, 1280, 2048, "dw_in", transpose_out=True, also_bf16=True)

    def finish(after=None):
        dh = _matmul(dproj, wpt, F32, 1024, 1024, 3840, "d_h", after=after)
        grad_x, dnw = _rms_in_bwd(x, norm_w, dh, dx2)
        small = jnp.zeros((8, PACK_W), F32)
        small = small.at[0].set(dnw[0]).at[1].set(dfw[0]).at[2, :DN_D].set(ddnw[0])
        small = small.at[3, :DN_HEADS].set(dpar[0, 8:16]).at[3, DN_HEADS:2 * DN_HEADS].set(dpar[1, 8:16])
        small = small.at[4, 0].set(lpart[0, 0])
        return grad_x, small

    return finish, (dwpt, dwpt_b), dconv, dw_o_dn, dw_o_dil, dw_out


def kernel(x, norm_w, w_in, conv_w, a_log, dt_bias, dn_norm_w, w_o_dn, w_o_dil, w_out, final_norm_w, loss_target, m_norm_w, m_w_in, m_conv_w, m_a_log, m_dt_bias, m_dn_norm_w, m_w_o_dn, m_w_o_dil, m_w_out, m_final_norm_w, v_norm_w, v_w_in, v_conv_w, v_a_log, v_dt_bias, v_dn_norm_w, v_w_o_dn, v_w_o_dil, v_w_out, v_final_norm_w):
    c = lax.axis_index("c")
    j = 2 * lax.axis_index("x") + lax.axis_index("y")
    qw = D_MODEL // N_CHIPS

    cw = conv_w[0].reshape(ROWS_CONV, PACK_W)
    cw = jnp.pad(cw, ((0, 16 - ROWS_CONV), (0, 0)))
    cw_hi = cw.astype(BF16)
    cw_lo = (cw - cw_hi.astype(F32)).astype(BF16)
    shard = w_in[0].T.astype(BF16)
    own_ba = jnp.where(j == 1, shard[BA_IN_SHARD1:BA_IN_SHARD1 + 2 * DN_HEADS], jnp.zeros((2 * DN_HEADS, D_MODEL), BF16))
    pack = jnp.concatenate(
        [w_o_dn[0].astype(BF16), w_o_dil[0].astype(BF16).reshape(ROWS_O_DIL, PACK_W), w_out[0].astype(BF16), cw_hi, cw_lo,
         own_ba, jnp.zeros((PACK_ROWS - R6, PACK_W), BF16)], axis=0).reshape(2, HALF_ROWS, PACK_W)
    chips = range(N_CHIPS)
    own_win = lax.switch(j, [functools.partial(_to_window, k) for k in chips], shard).reshape(2, SHARD_PAD // 2, D_MODEL)
    all_in, allw = _ag_weights([own_win, pack])
    wins = [all_in[k].reshape(SHARD_PAD, D_MODEL) for k in chips]
    allw = [allw[k].reshape(PACK_ROWS, PACK_W) for k in chips]
    wpt = _stack_windows(wins, allw[1][R5:R6])
    w_o_dn_full = jnp.concatenate([allw[k][:R1] for k in chips], axis=0)
    w_o_dil_full = jnp.concatenate([allw[k][R1:R2].reshape(DIL_W, qw) for k in chips], axis=1)
    w_out_full = jnp.concatenate([allw[k][R2:R3] for k in chips], axis=0)
    conv_full = jnp.concatenate(
        [(allw[k][R3:R3 + ROWS_CONV].astype(F32) + allw[k][R4:R4 + ROWS_CONV].astype(F32)).reshape(4, 3 * DN_W // N_CHIPS)
         for k in chips], axis=1)

    finish, (dwpt, dwpt_b), dconv, dw_o_dn, dw_o_dil, dw_out = _local_step(
        x[0], loss_target[0], norm_w, wpt, conv_full, a_log, dt_bias, dn_norm_w, w_o_dn_full, w_o_dil_full, w_out_full,
        final_norm_w.reshape(1, D_MODEL))

    cq = 3 * DN_W // N_CHIPS
    gpack = jnp.stack([
        jnp.concatenate(
            [dw_o_dn[k * qw:(k + 1) * qw], dw_o_dil[:, k * qw:(k + 1) * qw].reshape(ROWS_O_DIL, PACK_W),
             dw_out[k * qw:(k + 1) * qw],
             jnp.pad(dconv[:, k * cq:(k + 1) * cq].reshape(ROWS_CONV, PACK_W), ((0, 16 - ROWS_CONV), (0, 0))),
             dwpt[OFF_BA:OFF_BA + 2 * DN_HEADS] if k == 1 else jnp.zeros((2 * DN_HEADS, PACK_W), F32),
             jnp.zeros((PACK_ROWS - R4 - 2 * DN_HEADS, PACK_W), F32)], axis=0)
        for k in chips]).reshape(N_CHIPS, 2, HALF_ROWS, PACK_W)
    sib_in, sib_pack = _rs_pair(dwpt_b, gpack)
    csum_in = _add_halves_win(dwpt, sib_in, c)
    csum_pack = _add_halves(gpack, sib_pack, c, "add_halves_pack")
    send_sems, recv_sems, csums, lands, token = _rs_chips_start([csum_in, csum_pack])
    grad_x, small = finish(after=token)

    gs = _sum_blocks(_ag_small(small), 8, "sum_small")
    loss = gs[4, 0]
    w_small = jnp.zeros((8, PACK_W), F32)

    def pack_small(nw, fw, dnw_, al, db):
        t = w_small.at[0].set(nw[0]).at[1].set(fw).at[2, :DN_D].set(dnw_[0])
        return t.at[3, :DN_HEADS].set(al[0]).at[3, DN_HEADS:2 * DN_HEADS].set(db[0])

    sm = _adamw(pack_small(norm_w, final_norm_w, dn_norm_w, a_log, dt_bias), gs,
                pack_small(m_norm_w, m_final_norm_w, m_dn_norm_w, m_a_log, m_dt_bias),
                pack_small(v_norm_w, v_final_norm_w, v_dn_norm_w, v_a_log, v_dt_bias), "adamw_small")

    (csum_in, csum_pack), (src_in, src_pack) = _rs_chips_wait(send_sems, recv_sems, csums, lands, sm[0])
    half_in = _sum_chips(src_in, csum_in, j, "sum_chips_in")
    half_pack = _sum_chips(src_pack, csum_pack, j, "sum_chips_pack")
    sib_half_in, sib_half_pack = _pair_swap([half_in, half_pack])

    def both_halves(mine, theirs):
        return jnp.where(c == 0, jnp.concatenate([mine, theirs], axis=0), jnp.concatenate([theirs, mine], axis=0))

    g = both_halves(half_pack, sib_half_pack)
    g_w_in = lax.switch(j, [functools.partial(_from_window, k) for k in chips], both_halves(half_in, sib_half_in),
                        g[R4:R4 + 2 * DN_HEADS])
    g_w_o_dn = g[:R1]
    g_w_o_dil = g[R1:R2].reshape(DIL_W, qw)
    g_w_out = g[R2:R3]
    g_conv = g[R3:R3 + ROWS_CONV].reshape(4, cq)

    def unpack_small(t):
        return dict(norm_w=t[0:1], final_norm_w=t[1], dn_norm_w=t[2:3, :DN_D], a_log=t[3:4, :DN_HEADS],
                    dt_bias=t[3:4, DN_HEADS:2 * DN_HEADS])

    res = {"grad": unpack_small(gs)}
    for kind, arr in zip(("delta", "new_m", "new_v"), sm):
        res[kind] = unpack_small(arr)
    big = dict(conv_w=(conv_w, g_conv, m_conv_w, v_conv_w), w_o_dn=(w_o_dn, g_w_o_dn, m_w_o_dn, v_w_o_dn),
               w_o_dil=(w_o_dil, g_w_o_dil, m_w_o_dil, v_w_o_dil), w_out=(w_out, g_w_out, m_w_out, v_w_out))
    for name, (wt, gt, mt, vt) in big.items():
        d, nm, nv = _adamw(wt[0], gt, mt[0], vt[0], "adamw_" + name)
        res["grad"][name] = gt[None]
        res["delta"][name], res["new_m"][name], res["new_v"][name] = d[None], nm[None], nv[None]

    d, nm, nv = _adamw(w_in[0].T, g_w_in, m_w_in[0].T, v_w_in[0].T, "adamw_w_in")
    res["grad"]["w_in"] = g_w_in.T[None]
    res["delta"]["w_in"], res["new_m"]["w_in"], res["new_v"]["w_in"] = d.T[None], nm.T[None], nv.T[None]
    order = ["norm_w", "w_in", "conv_w", "a_log", "dt_bias", "dn_norm_w", "w_o_dn", "w_o_dil", "w_out", "final_norm_w"]
    outs = [loss, grad_x[None]]
    for kind in ("grad", "delta", "new_m", "new_v"):
        outs += [res[kind][nm] for nm in order]
    return tuple(outs)
```

```python
import functools
import math

import jax
import jax.numpy as jnp
from jax import lax
from jax.experimental import pallas as pl
from jax.experimental.pallas import tpu as pltpu

F32 = jnp.float32
BF16 = jnp.bfloat16
MESH = pl.DeviceIdType.MESH

D_MODEL = 1024
DN_HEADS = 8
DN_D = 128
DN_CHUNK = 64
DN_W = DN_HEADS * DN_D
DIL_GROUPS = ((128, 1), (512, 4), (2048, 16))
N_DIL = len(DIL_GROUPS)
DIL_HEADS = 4
DIL_DH = 128
DIL_W = DIL_HEADS * DIL_DH
ATT_BLOCK = 128
NORM_EPS = 1e-6
PROJ_W = 11280
N_CHIPS = 4
SHARD_W = PROJ_W // N_CHIPS

OFF_QKV_A = 0
OFF_Z_A = 3072
OFF_Q_B = 4096
OFF_K_B = 5632
OFF_V_B = 7168
OFF_Z_B = 8704
OFF_G_A = 9216
OFF_G_B = 10240
OFF_BA = 11264
PW = 11520
REF_OFF_BA = 4096

ADAM_LR = 0.001
ADAM_B1 = 0.9
ADAM_B2 = 0.999
ADAM_EPS = 1e-08
ADAM_WD = 0.01
ADAM_STEP = 10

ROW_TILE = 512
NEG = -1e30


def _dot(a, b):
    return jnp.dot(a.astype(BF16), b.astype(BF16), preferred_element_type=F32)


def _dot_nt(a, b):
    return lax.dot_general(a.astype(BF16), b.astype(BF16), (((1,), (1,)), ((), ())), preferred_element_type=F32)


def _dot_tn(a, b):
    return lax.dot_general(a.astype(BF16), b.astype(BF16), (((0,), (0,)), ((), ())), preferred_element_type=F32)


def _split(a):
    hi = a.astype(BF16)
    lo = (a - hi.astype(F32)).astype(BF16)
    return hi, lo


def _dot_exact_lhs(c, a):
    hi, lo = _split(a)
    cb = c.astype(BF16)
    return jnp.dot(cb, hi, preferred_element_type=F32) + jnp.dot(cb, lo, preferred_element_type=F32)


def _dot_tn_exact_rhs(a, c):
    hi, lo = _split(a)
    cb = c.astype(BF16)
    dn = (((0,), (0,)), ((), ()))
    return (lax.dot_general(hi, cb, dn, preferred_element_type=F32)
            + lax.dot_general(lo, cb, dn, preferred_element_type=F32))


def _sigmoid(x):
    return 1.0 / (1.0 + jnp.exp(-x))


def _silu(x):
    return x * _sigmoid(x)


def _silu_grad(x):
    s = _sigmoid(x)
    return s * (1.0 + x * (1.0 - s))


def _softplus(x):
    return jnp.maximum(x, 0.0) + jnp.log(1.0 + jnp.exp(-jnp.abs(x)))


def _cparams(*sem):
    return pltpu.CompilerParams(dimension_semantics=sem)


def _matmul(a, b, out_dtype, tm, tn, tk, name, nt=False, transpose_out=False, after=None, also_bf16=False):
    m, kdim = a.shape
    n = b.shape[0] if nt else b.shape[1]
    tm, tn, tk = min(tm, m), min(tn, n), min(tk, kdim)
    assert m % tm == 0 and n % tn == 0 and kdim % tk == 0, (name, a.shape, b.shape, tm, tn, tk)
    nk = kdim // tk
    dot = _dot_nt if nt else _dot
    b_spec = (pl.BlockSpec((tn, tk), lambda i, j, k: (j, k)) if nt else pl.BlockSpec((tk, tn), lambda i, j, k: (k, j)))
    extra = [] if after is None else [after]
    out_dtypes = [out_dtype] + ([BF16] if also_bf16 else [])

    def emit(o_refs, acc):
        val = acc.T if transpose_out else acc
        for o_ref in o_refs:
            o_ref[...] = val.astype(o_ref.dtype)

    def outs_of(rest):
        return rest[len(extra):len(extra) + len(out_dtypes)]

    if nk == 1:
        def body(a_ref, b_ref, *rest):
            emit(outs_of(rest), dot(a_ref[...], b_ref[...]))
        scratch = []
    else:
        def body(a_ref, b_ref, *rest):
            o_ref, acc_ref = outs_of(rest), rest[-1]
            k = pl.program_id(2)
            p = dot(a_ref[...], b_ref[...])

            @pl.when(k == 0)
            def _():
                acc_ref[...] = p

            @pl.when(k > 0)
            def _():
                acc_ref[...] += p

            @pl.when(k == nk - 1)
            def _():
                emit(o_ref, acc_ref[...])
        scratch = [pltpu.VMEM((tm, tn), F32)]

    if transpose_out:
        out_spec, out_shape = pl.BlockSpec((tn, tm), lambda i, j, k: (j, i)), (n, m)
    else:
        out_spec, out_shape = pl.BlockSpec((tm, tn), lambda i, j, k: (i, j)), (m, n)
    res = pl.pallas_call(
        body, name=name, grid=(m // tm, n // tn, nk),
        in_specs=[pl.BlockSpec((tm, tk), lambda i, j, k: (i, k)), b_spec] + [pl.BlockSpec(memory_space=pl.ANY)] * len(extra),
        out_specs=[out_spec] * len(out_dtypes), out_shape=[jax.ShapeDtypeStruct(out_shape, d) for d in out_dtypes],
        scratch_shapes=scratch, compiler_params=_cparams("parallel", "parallel", "arbitrary"))(a, b, *extra)
    return res if also_bf16 else res[0]


def _rms_in(x, nw):
    s, d = x.shape

    def body(x_ref, w_ref, h_ref, ht_ref):
        xv = x_ref[...]
        r = lax.rsqrt(jnp.mean(xv * xv, axis=-1, keepdims=True) + NORM_EPS)
        h = xv * r * w_ref[...]
        h_ref[...] = h.astype(BF16)
        ht_ref[...] = h.T.astype(BF16)

    return pl.pallas_call(
        body, name="rms_in", grid=(s // ROW_TILE,),
        in_specs=[pl.BlockSpec((ROW_TILE, d), lambda i: (i, 0)), pl.BlockSpec((1, d), lambda i: (0, 0))],
        out_specs=[pl.BlockSpec((ROW_TILE, d), lambda i: (i, 0)), pl.BlockSpec((d, ROW_TILE), lambda i: (0, i))],
        out_shape=[jax.ShapeDtypeStruct((s, d), BF16), jax.ShapeDtypeStruct((d, s), BF16)],
        compiler_params=_cparams("parallel"))(x, nw)


def _rms_in_bwd(x, nw, dh, dx2):
    s, d = x.shape

    def body(x_ref, w_ref, dh_ref, dx2_ref, dx_ref, dw_ref):
        i = pl.program_id(0)
        xv = x_ref[...]
        r = lax.rsqrt(jnp.mean(xv * xv, axis=-1, keepdims=True) + NORM_EPS)
        dhv = dh_ref[...]
        dyw = dhv * w_ref[...]
        dx_ref[...] = dx2_ref[...] + r * dyw - xv * (r * r * r) * jnp.mean(dyw * xv, axis=-1, keepdims=True)
        part = jnp.sum(dhv * xv * r, axis=0, keepdims=True)

        @pl.when(i == 0)
        def _():
            dw_ref[...] = part

        @pl.when(i > 0)
        def _():
            dw_ref[...] += part

    row = pl.BlockSpec((ROW_TILE, d), lambda i: (i, 0))
    vec = pl.BlockSpec((1, d), lambda i: (0, 0))
    return pl.pallas_call(
        body, name="rms_in_bwd", grid=(s // ROW_TILE,), in_specs=[row, vec, row, row], out_specs=[row, vec],
        out_shape=[jax.ShapeDtypeStruct((s, d), F32), jax.ShapeDtypeStruct((1, d), F32)],
        compiler_params=_cparams("arbitrary"))(x, nw, dh, dx2)


def _shift_down(cur, prev8, k):
    rc = pltpu.roll(cur, k, 0)
    rp = pltpu.roll(prev8, k, 0)
    row = lax.broadcasted_iota(jnp.int32, prev8.shape, 0)
    top = jnp.where(row < k, rp, rc[:8])
    return jnp.concatenate([top, rc[8:]], axis=0)


def _shift_up(cur, next8, k):
    t = cur.shape[0]
    rc = pltpu.roll(cur, t - k, 0)
    rn = pltpu.roll(next8, 8 - k, 0)
    row = lax.broadcasted_iota(jnp.int32, next8.shape, 0)
    bot = jnp.where(row >= 8 - k, rn, rc[t - 8:])
    return jnp.concatenate([rc[:t - 8], bot], axis=0)


def _conv_fwd(proj, conv_w):
    s = proj.shape[0]
    t8 = ROW_TILE // 8

    def body(u_ref, up_ref, w_ref, c_ref, y_ref):
        i = pl.program_id(0)
        part = pl.program_id(1)
        cur = u_ref[...]
        prev8 = jnp.where(i > 0, up_ref[...], 0.0)
        w = w_ref[...]
        c = cur * w[3:4, :]
        for k in (1, 2, 3):
            c = c + _shift_down(cur, prev8, k) * w[3 - k:4 - k, :]
        c_ref[...] = c
        a = _silu(c)
        for h in range(DN_HEADS):
            ah = a[:, h * DN_D:(h + 1) * DN_D]
            r = lax.rsqrt(jnp.sum(ah * ah, axis=-1, keepdims=True) + NORM_EPS)
            y_ref[:, h * DN_D:(h + 1) * DN_D] = jnp.where(part < 2, ah * r, ah)

    return pl.pallas_call(
        body, name="conv_fwd", grid=(s // ROW_TILE, 3),
        in_specs=[pl.BlockSpec((ROW_TILE, DN_W), lambda i, p: (i, p)),
                  pl.BlockSpec((8, DN_W), lambda i, p: (jnp.maximum(i * t8 - 1, 0), p)),
                  pl.BlockSpec((4, DN_W), lambda i, p: (0, p))],
        out_specs=[pl.BlockSpec((ROW_TILE, DN_W), lambda i, p: (i, p))] * 2,
        out_shape=[jax.ShapeDtypeStruct((s, 3 * DN_W), F32)] * 2,
        compiler_params=_cparams("parallel", "parallel"))(proj, proj, conv_w)


def _conv_bwd_act(c, dq, dk, dv):
    s = c.shape[0]

    def body(c_ref, dq_ref, dk_ref, dv_ref, dc_ref):
        for part, d_ref in enumerate((dq_ref, dk_ref, dv_ref)):
            for h in range(DN_HEADS):
                sl = slice(part * DN_W + h * DN_D, part * DN_W + (h + 1) * DN_D)
                ch = c_ref[:, sl]
                dyh = d_ref[:, h * DN_D:(h + 1) * DN_D]
                if part < 2:
                    ah = _silu(ch)
                    r = lax.rsqrt(jnp.sum(ah * ah, axis=-1, keepdims=True) + NORM_EPS)
                    dyh = r * dyh - ah * (r * r * r) * jnp.sum(dyh * ah, axis=-1, keepdims=True)
                dc_ref[:, sl] = dyh * _silu_grad(ch)

    wide = pl.BlockSpec((ROW_TILE, 3 * DN_W), lambda i: (i, 0))
    row = pl.BlockSpec((ROW_TILE, DN_W), lambda i: (i, 0))
    return pl.pallas_call(
        body, name="conv_bwd_act", grid=(s // ROW_TILE,), in_specs=[wide, row, row, row], out_specs=wide,
        out_shape=jax.ShapeDtypeStruct((s, 3 * DN_W), F32), compiler_params=_cparams("parallel"))(c, dq, dk, dv)


DPROJ_IN = pl.BlockSpec(memory_space=pl.ANY)


def _conv_bwd(proj, dc, conv_w, dproj):
    s = proj.shape[0]
    t8 = ROW_TILE // 8
    nrow = s // ROW_TILE
    last8 = s // 8 - 1

    def body(u_ref, dc_ref, dcn_ref, w_ref, dproj_in, du_ref, dw_ref):
        i = pl.program_id(1)
        cur = u_ref[...]
        dcv = dc_ref[...]
        next8 = jnp.where(i < nrow - 1, dcn_ref[...], 0.0)
        w = w_ref[...]

        @pl.when(i == 0)
        def _():
            dw_ref[...] = jnp.zeros_like(dw_ref)

        du = dcv * w[3:4, :]
        dw_ref[3:4, :] += jnp.sum(cur * dcv, axis=0, keepdims=True)
        for k in (1, 2, 3):
            ahead = _shift_up(dcv, next8, k)
            du = du + ahead * w[3 - k:4 - k, :]
            dw_ref[3 - k:4 - k, :] += jnp.sum(cur * ahead, axis=0, keepdims=True)
        du_ref[...] = du.astype(BF16)

    blk = pl.BlockSpec((ROW_TILE, DN_W), lambda p, i: (i, p))
    return pl.pallas_call(
        body, name="conv_bwd", grid=(3, nrow),
        in_specs=[blk, blk, pl.BlockSpec((8, DN_W), lambda p, i: (jnp.minimum((i + 1) * t8, last8), p)),
                  pl.BlockSpec((4, DN_W), lambda p, i: (0, p)), DPROJ_IN],
        out_specs=[blk, pl.BlockSpec((4, DN_W), lambda p, i: (0, p))],
        out_shape=[jax.ShapeDtypeStruct((s, PW), BF16), jax.ShapeDtypeStruct((4, 3 * DN_W), F32)],
        input_output_aliases={4: 0},
        compiler_params=_cparams("parallel", "arbitrary"))(proj, dc, dc, conv_w, dproj)


def _gates_fwd(proj, gate_par):
    s = proj.shape[0]

    def body(ba_ref, par_ref, o_ref):
        v = ba_ref[...]
        lane = lax.broadcasted_iota(jnp.int32, v.shape, 1)
        beta = _sigmoid(v)
        g = -jnp.exp(par_ref[0:1, :]) * _softplus(v + par_ref[1:2, :])
        o_ref[...] = jnp.where(lane < DN_HEADS, beta, jnp.where(lane < 2 * DN_HEADS, g, 0.0))

    return pl.pallas_call(
        body, name="gates_fwd", grid=(s // ROW_TILE,),
        in_specs=[pl.BlockSpec((ROW_TILE, 128), lambda i: (i, OFF_BA // 128)), pl.BlockSpec((8, 128), lambda i: (0, 0))],
        out_specs=pl.BlockSpec((ROW_TILE, 128), lambda i: (i, 0)),
        out_shape=jax.ShapeDtypeStruct((s, 128), F32), compiler_params=_cparams("parallel"))(proj, gate_par)


def _gates_bwd(proj, gate_par, dbg, dproj):
    s = proj.shape[0]

    def body(ba_ref, par_ref, d_ref, dproj_in, o_ref, dpar_ref):
        i = pl.program_id(0)
        v = ba_ref[...]
        dv = d_ref[...]
        lane = lax.broadcasted_iota(jnp.int32, v.shape, 1)
        beta = _sigmoid(v)
        nega = -jnp.exp(par_ref[0:1, :])
        xs = v + par_ref[1:2, :]
        dsp = dv * nega * _sigmoid(xs)
        dal = dv * nega * _softplus(xs)
        is_b = lane < DN_HEADS
        is_g = jnp.logical_and(lane >= DN_HEADS, lane < 2 * DN_HEADS)
        o_ref[:, :128] = jnp.where(is_b, dv * beta * (1.0 - beta), jnp.where(is_g, dsp, 0.0)).astype(BF16)
        o_ref[:, 128:] = jnp.zeros((ROW_TILE, PW - OFF_BA - 128), BF16)
        r0 = jnp.sum(jnp.where(is_g, dal, 0.0), axis=0, keepdims=True)
        r1 = jnp.sum(jnp.where(is_g, dsp, 0.0), axis=0, keepdims=True)

        @pl.when(i == 0)
        def _():
            dpar_ref[...] = jnp.zeros_like(dpar_ref)

        dpar_ref[0:1, :] += r0
        dpar_ref[1:2, :] += r1

    return pl.pallas_call(
        body, name="gates_bwd", grid=(s // ROW_TILE,),
        in_specs=[pl.BlockSpec((ROW_TILE, 128), lambda i: (i, OFF_BA // 128)), pl.BlockSpec((8, 128), lambda i: (0, 0)),
                  pl.BlockSpec((ROW_TILE, 128), lambda i: (i, 0)), DPROJ_IN],
        out_specs=[pl.BlockSpec((ROW_TILE, PW - OFF_BA), lambda i: (i, OFF_BA // (PW - OFF_BA))),
                   pl.BlockSpec((8, 128), lambda i: (0, 0))],
        out_shape=[jax.ShapeDtypeStruct((s, PW), BF16), jax.ShapeDtypeStruct((8, 128), F32)],
        input_output_aliases={3: 0},
        compiler_params=_cparams("arbitrary"))(proj, gate_par, dbg, dproj)


def _chunk_masks():
    c = DN_CHUNK
    ii = lax.broadcasted_iota(jnp.int32, (c, c), 0)
    jj = lax.broadcasted_iota(jnp.int32, (c, c), 1)
    return dict(ii=ii, jj=jj, lower=(ii >= jj), strict=(ii > jj),
                lower_f=(ii >= jj).astype(BF16), upper_f=(ii <= jj).astype(BF16))


class _Heads:
    def __init__(self, xs):
        self.xs = list(xs)

    def _bin(self, o, f):
        if isinstance(o, _Heads):
            return _Heads([f(a, b) for a, b in zip(self.xs, o.xs)])
        return _Heads([f(a, o) for a in self.xs])

    def __add__(self, o):
        return self._bin(o, lambda a, b: a + b)

    def __sub__(self, o):
        return self._bin(o, lambda a, b: a - b)

    def __mul__(self, o):
        return self._bin(o, lambda a, b: a * b)

    __radd__ = __add__
    __rmul__ = __mul__

    def __neg__(self):
        return _Heads([-a for a in self.xs])

    def __getitem__(self, i):
        return _Heads([a[i] for a in self.xs])


def _hmap(f, *args):
    n = next(len(a.xs) for a in args if isinstance(a, _Heads))
    return _Heads([f(*[(a.xs[h] if isinstance(a, _Heads) else a) for a in args]) for h in range(n)])


def _hdot(a, b):
    return _hmap(_dot, a, b)


def _hdot_nt(a, b):
    return _hmap(_dot_nt, a, b)


def _hdot_tn(a, b):
    return _hmap(_dot_tn, a, b)


def _hcat(a, b, axis):
    return _hmap(lambda x, y: jnp.concatenate([x, y], axis=axis), a, b)


def _hsum(a, axis):
    return _hmap(lambda t: jnp.sum(t, axis=axis, keepdims=True), a)


def _hwhere(c, a, b):
    return _hmap(jnp.where, c, a, b)


def _chunk_gates(mk, bg):
    c = DN_CHUNK
    gc_all = _dot_exact_lhs(mk["lower_f"], bg)
    rows = jnp.concatenate([gc_all, gc_all], axis=0).T
    hs = range(DN_HEADS)
    return (_Heads(bg[:, h:h + 1] for h in hs), _Heads(gc_all[:, DN_HEADS + h:DN_HEADS + h + 1] for h in hs),
            _Heads(rows[DN_HEADS + h:DN_HEADS + h + 1, :] for h in hs))


def _chunk_common(mk, q, k, beta_col, gc_col, gc_r):
    c = DN_CHUNK
    lower, strict = mk["lower"], mk["strict"]
    qs = q * (DN_D ** -0.5)
    beta_b = _hmap(lambda t: jnp.broadcast_to(t, (c, DN_D)), beta_col)
    gc_b = _hmap(lambda t: jnp.broadcast_to(t, (c, DN_D)), gc_col)
    gc_sq = gc_b[:, :c]
    gam = _hwhere(lower, _hmap(lambda t: jnp.exp(jnp.minimum(t, 0.0)), gc_sq - gc_r[:, :c]), 0.0)
    egc = _hmap(jnp.exp, gc_b)
    gl = gc_b[c - 1:c, :]
    ekd = _hmap(jnp.exp, gl - gc_b)
    dl = _hmap(jnp.exp, gl)
    kb = k * beta_b
    scores = _hdot_nt(_hcat(kb, qs, 0), k)
    a_strict = _hwhere(strict, scores[:c] * gam, 0.0)
    aqk = _hwhere(lower, scores[c:] * gam, 0.0)
    return dict(k=k, qs=qs, beta_b=beta_b, gc_b=gc_b, gam=gam, egc=egc, ekd=ekd, dl=dl, kb=kb, a_strict=a_strict, aqk=aqk)


def _unit_lower_inverse_minus_eye(n_strict, ii, jj):
    same = lax.shift_right_logical(ii, 4) == lax.shift_right_logical(jj, 4)
    dmat = _hwhere(same, n_strict, 0.0)
    omat = n_strict - dmat
    d2 = _hdot(dmat, dmat)
    d4 = _hdot(d2, d2)
    d8 = _hdot(d4, d4)
    x1 = d2 - dmat - _hdot(dmat, d2)
    x2 = x1 + d4 + _hdot(x1, d4)
    x3 = x2 + d8 + _hdot(x2, d8)
    n1 = omat + _hdot(x3, omat)
    n2 = _hdot(n1, n1)
    y = n2 - n1 - _hdot(n1, n2)
    return y + x3 + _hdot(y, x3)


GDR_HEAD_SETS = (range(0, DN_HEADS),)


def _gdr_fwd(qkv, bg):
    s = qkv.shape[0]
    c = DN_CHUNK
    n = s // c

    def body(q_ref, k_ref, v_ref, bg_ref, o_ref, u_ref, w_ref, vn_ref, tm_ref, st_ref, state):
        @pl.when(pl.program_id(0) == 0)
        def _():
            state[...] = jnp.zeros_like(state)

        mk = _chunk_masks()
        gates = _chunk_gates(mk, bg_ref[...])
        for hs in GDR_HEAD_SETS:
            sls = [slice(h * DN_D, (h + 1) * DN_D) for h in hs]
            cm = _chunk_common(mk, _Heads(q_ref[:, sl] for sl in sls), _Heads(k_ref[:, sl] for sl in sls),
                               *[_Heads(g.xs[h] for h in hs) for g in gates])
            tm = _unit_lower_inverse_minus_eye(cm["a_strict"], mk["ii"], mk["jj"])
            rhs_u = _Heads(v_ref[:, sl] for sl in sls) * cm["beta_b"]
            rhs_w = cm["kb"] * cm["egc"]
            t_rhs = _hdot(tm, _hcat(rhs_u, rhs_w, 1))
            u = rhs_u + t_rhs[:, :DN_D]
            w = rhs_w + t_rhs[:, DN_D:]
            st = _Heads(state[h] for h in hs)
            on_state = _hdot(_hcat(w, cm["qs"] * cm["egc"], 0), st)
            v_new = u - on_state[:c]
            o = on_state[c:] + _hdot(cm["aqk"], v_new)
            st_new = st * cm["dl"] + _hdot_tn(cm["k"] * cm["ekd"], v_new)
            for i, (h, sl) in enumerate(zip(hs, sls)):
                o_ref[:, sl] = o.xs[i]
                u_ref[:, sl] = u.xs[i]
                w_ref[:, sl] = w.xs[i]
                vn_ref[:, sl] = v_new.xs[i]
                tm_ref[h, 0] = tm.xs[i]
                st_ref[h, 0] = st.xs[i]
                state[h] = st_new.xs[i]

    def part(p):
        return pl.BlockSpec((c, DN_W), lambda j: (j, p))

    return pl.pallas_call(
        body, name="gdr_fwd", grid=(n,),
        in_specs=[part(0), part(1), part(2), pl.BlockSpec((c, 128), lambda j: (j, 0))],
        out_specs=[part(0)] * 4 + [pl.BlockSpec((DN_HEADS, 1, c, c), lambda j: (0, j, 0, 0)),
                                   pl.BlockSpec((DN_HEADS, 1, DN_D, DN_D), lambda j: (0, j, 0, 0))],
        out_shape=[jax.ShapeDtypeStruct((s, DN_W), F32)] * 4
        + [jax.ShapeDtypeStruct((DN_HEADS, n, c, c), F32), jax.ShapeDtypeStruct((DN_HEADS, n, DN_D, DN_D), F32)],
        scratch_shapes=[pltpu.VMEM((DN_HEADS, DN_D, DN_D), F32)],
        compiler_params=_cparams("arbitrary"))(qkv, qkv, qkv, bg)


def _gdr_bwd(qkv, bg, u, w, vn, tmat, states, do):
    s = qkv.shape[0]
    c = DN_CHUNK
    n = s // c

    def body(q_ref, k_ref, v_ref, bg_ref, u_ref, w_ref, vn_ref, tm_ref, st_ref, do_ref,
             dq_ref, dk_ref, dv_ref, dbg_ref, dstate):
        @pl.when(pl.program_id(0) == 0)
        def _():
            dstate[...] = jnp.zeros_like(dstate)

        mk = _chunk_masks()
        lower, strict = mk["lower"], mk["strict"]
        bg = bg_ref[...]
        ones = jnp.ones((c, DN_D), BF16)
        rowi = lax.broadcasted_iota(jnp.int32, (c, DN_D), 0)
        lane = lax.broadcasted_iota(jnp.int32, (c, 128), 1)
        hs = range(DN_HEADS)
        sls = [slice(h * DN_D, (h + 1) * DN_D) for h in hs]

        def heads_of(ref):
            return _Heads(ref[:, sl] for sl in sls)

        cm = _chunk_common(mk, heads_of(q_ref), heads_of(k_ref), *_chunk_gates(mk, bg))
        k, qs, beta_b = cm["k"], cm["qs"], cm["beta_b"]
        gam, egc, ekd, dl, kb = cm["gam"], cm["egc"], cm["ekd"], cm["dl"], cm["kb"]
        aqk, a_strict = cm["aqk"], cm["a_strict"]
        v, uu, ww, v_new, dov = heads_of(v_ref), heads_of(u_ref), heads_of(w_ref), heads_of(vn_ref), heads_of(do_ref)
        st = _Heads(st_ref[h, 0] for h in hs)
        dsn = _Heads(dstate[h] for h in hs)
        qd = qs * egc
        kd = k * ekd

        dv_new = _hdot_tn(aqk, dov) + _hdot(kd, dsn)
        do_sv = _hdot_nt(dov, _hcat(st, v_new, 0))
        dqd = do_sv[:, :DN_D]
        daqk = _hwhere(lower, do_sv[:, DN_D:], 0.0)
        dkd = _hdot_nt(v_new, dsn)
        ddl = _hsum(_hsum(dsn * st, 1), 0)
        dw = -_hdot_nt(dv_new, st)
        ds_new = dsn * dl + _hdot_tn(_hcat(qd, -ww, 0), _hcat(dov, dv_new, 0))

        tm = _Heads(tm_ref[h, 0] for h in hs)
        tt = _hdot_tn(tm, _hcat(dv_new, dw, 1))
        dru = dv_new + tt[:, :DN_D]
        drw = dw + tt[:, DN_D:]
        dn = _hwhere(strict, -_hdot_nt(_hcat(dru, drw, 1), _hcat(uu, ww, 1)), 0.0)
        dag = dn * gam
        dqg = daqk * gam
        both = _hcat(dag, dqg, 0)
        on_k = _hdot(both, k)
        dkb = on_k[:c] + drw * egc
        dqs = on_k[c:] + dqd * egc
        dk = _hdot_tn(both, _hcat(kb, qs, 0)) + dkb * beta_b + dkd * ekd
        pmat = dn * a_strict + daqk * aqk
        tkd = _hsum(dkd * kd, -1)
        dgc = (_hsum(pmat, -1) - _hmap(_dot_tn_exact_rhs, pmat, ones) + _hsum(drw * (kb * egc), -1)
               + _hsum(dqd * qd, -1) - tkd)
        last = _hsum(tkd, 0) + ddl * dl
        dgc = dgc + _hwhere(rowi == c - 1, last, 0.0)
        dbeta = _hsum(dru * v, -1) + _hsum(dkb * k, -1)
        dq = dqs * (DN_D ** -0.5)
        dv = dru * beta_b

        dgc_all = jnp.zeros((c, 128), F32)
        dbg = jnp.zeros((c, 128), F32)
        for h, sl in zip(hs, sls):
            dq_ref[:, sl] = dq.xs[h]
            dk_ref[:, sl] = dk.xs[h]
            dv_ref[:, sl] = dv.xs[h]
            dstate[h] = ds_new.xs[h]
            dgc_all = dgc_all + jnp.where(lane == DN_HEADS + h, dgc.xs[h], 0.0)
            dbg = dbg + jnp.where(lane == h, dbeta.xs[h], 0.0)
        dbg_ref[...] = dbg + _dot_exact_lhs(mk["upper_f"], dgc_all)

    def part(p):
        return pl.BlockSpec((c, DN_W), lambda j: (n - 1 - j, p))

    vec = pl.BlockSpec((c, 128), lambda j: (n - 1 - j, 0))
    return pl.pallas_call(
        body, name="gdr_bwd", grid=(n,),
        in_specs=[part(0), part(1), part(2), vec, part(0), part(0), part(0),
                  pl.BlockSpec((DN_HEADS, 1, c, c), lambda j: (0, n - 1 - j, 0, 0)),
                  pl.BlockSpec((DN_HEADS, 1, DN_D, DN_D), lambda j: (0, n - 1 - j, 0, 0)), part(0)],
        out_specs=[part(0), part(0), part(0), vec],
        out_shape=[jax.ShapeDtypeStruct((s, DN_W), F32)] * 3 + [jax.ShapeDtypeStruct((s, 128), F32)],
        scratch_shapes=[pltpu.VMEM((DN_HEADS, DN_D, DN_D), F32)],
        compiler_params=_cparams("arbitrary"))(qkv, qkv, qkv, bg, u, w, vn, tmat, states, do)


def _gdr_out(o, proj, dnw):
    s = o.shape[0]

    def body(o_ref, z_ref, w_ref, y_ref, yt_ref):
        ov, zv, wv = o_ref[...], z_ref[...], w_ref[...]
        for h in range(DN_HEADS):
            sl = slice(h * DN_D, (h + 1) * DN_D)
            oh = ov[:, sl]
            r = lax.rsqrt(jnp.mean(oh * oh, axis=-1, keepdims=True) + NORM_EPS)
            y = (oh * r * wv) * _silu(zv[:, sl])
            y_ref[:, sl] = y.astype(BF16)
            yt_ref[sl, :] = y.T.astype(BF16)

    row = pl.BlockSpec((ROW_TILE, DN_W), lambda i: (i, 0))
    return pl.pallas_call(
        body, name="gdr_out", grid=(s // ROW_TILE,),
        in_specs=[row, pl.BlockSpec((ROW_TILE, DN_W), lambda i: (i, OFF_Z_A // DN_W)), pl.BlockSpec((1, DN_D), lambda i: (0, 0))],
        out_specs=[row, pl.BlockSpec((DN_W, ROW_TILE), lambda i: (0, i))],
        out_shape=[jax.ShapeDtypeStruct((s, DN_W), BF16), jax.ShapeDtypeStruct((DN_W, s), BF16)],
        compiler_params=_cparams("parallel"))(o, proj, dnw)


def _gdr_out_bwd(o, proj, dnw, dy):
    s = o.shape[0]

    def body(o_ref, z_ref, w_ref, dy_ref, do_ref, dz_ref, dw_ref):
        i = pl.program_id(0)
        ov, zv, wv, dyv = o_ref[...], z_ref[...], w_ref[...], dy_ref[...]
        acc = jnp.zeros((1, DN_D), F32)
        for h in range(DN_HEADS):
            sl = slice(h * DN_D, (h + 1) * DN_D)
            oh, zh, dh = ov[:, sl], zv[:, sl], dyv[:, sl]
            r = lax.rsqrt(jnp.mean(oh * oh, axis=-1, keepdims=True) + NORM_EPS)
            dn = dh * _silu(zh)
            dz_ref[:, sl] = (dh * (oh * r * wv) * _silu_grad(zh)).astype(BF16)
            acc = acc + jnp.sum(dn * oh * r, axis=0, keepdims=True)
            dnw_ = dn * wv
            do_ref[:, sl] = r * dnw_ - oh * (r * r * r) * jnp.mean(dnw_ * oh, axis=-1, keepdims=True)

        @pl.when(i == 0)
        def _():
            dw_ref[...] = acc

        @pl.when(i > 0)
        def _():
            dw_ref[...] += acc

    row = pl.BlockSpec((ROW_TILE, DN_W), lambda i: (i, 0))
    vec = pl.BlockSpec((1, DN_D), lambda i: (0, 0))
    return pl.pallas_call(
        body, name="gdr_out_bwd", grid=(s // ROW_TILE,),
        in_specs=[row, pl.BlockSpec((ROW_TILE, DN_W), lambda i: (i, OFF_Z_A // DN_W)), vec, row],
        out_specs=[row, pl.BlockSpec((ROW_TILE, DN_W), lambda i: (i, OFF_Z_A // DN_W)), vec],
        out_shape=[jax.ShapeDtypeStruct((s, DN_W), F32), jax.ShapeDtypeStruct((s, PW), BF16),
                   jax.ShapeDtypeStruct((1, DN_D), F32)],
        compiler_params=_cparams("arbitrary"))(o, proj, dnw, dy)


def _slope(group, head):
    idx = (group * DIL_HEADS + head + 1).astype(F32)
    return jnp.exp(jnp.full((1, 128), -8.0 * math.log(2.0) / (N_DIL * DIL_HEADS), F32) * idx)


def _att_scores(qb, k_cur, k_prev, slope_d, has_prev):
    iq = lax.broadcasted_iota(jnp.int32, (ATT_BLOCK, ATT_BLOCK), 0)
    jk = lax.broadcasted_iota(jnp.int32, (ATT_BLOCK, ATT_BLOCK), 1)
    dist_c = (iq - jk).astype(F32)
    s_cur = jnp.where(iq >= jk, _dot_nt(qb, k_cur) - slope_d * dist_c, NEG)
    s_prev = jnp.where(jnp.logical_and(jk >= iq, has_prev),
                       _dot_nt(qb, k_prev) - slope_d * (dist_c + float(ATT_BLOCK)), NEG)
    return s_cur, s_prev


def _att_scores_whole(qb, k, slope_d):
    n = 2 * ATT_BLOCK
    dist = lax.broadcasted_iota(jnp.int32, (n, n), 0) - lax.broadcasted_iota(jnp.int32, (n, n), 1)
    valid = jnp.logical_and(dist >= 0, dist <= ATT_BLOCK)
    return jnp.where(valid, _dot_nt(qb, k) - slope_d[:, 0:1] * dist.astype(F32), NEG)


def _att_tiles(i, dil, nb):
    tiles = nb // 2
    per = dil * tiles // ATT_UNROLL
    assert nb % 2 == 0 and tiles >= 2 and per * ATT_UNROLL == dil * tiles
    for i0 in range(per):
        ts = [divmod(i0 + u * per, tiles) for u in range(ATT_UNROLL)]
        assert all(a[0] != b[0] or abs(a[1] - b[1]) >= 2 for n, a in enumerate(ts) for b in ts[n + 1:])
    qrows, krows, has_prev = [], [], []
    for u in range(ATT_UNROLL):
        t = i + u * per
        r = lax.div(t, tiles)
        j = lax.rem(t, tiles)
        qbase = r + dil * 2 * ATT_BLOCK * j
        kbase = qbase - dil * ATT_BLOCK * jnp.minimum(j, 1)
        if dil == 1:
            qbase, kbase = pl.multiple_of(qbase, ATT_BLOCK), pl.multiple_of(kbase, ATT_BLOCK)
        qrows.append(pl.ds(qbase, 2 * ATT_BLOCK, stride=dil))
        krows.append(pl.ds(kbase, 3 * ATT_BLOCK, stride=dil))
        has_prev.append(j > 0)
    return qrows, krows, has_prev


def _att_scores_tile(qb, k, slope_d, has_prev):
    iq = lax.broadcasted_iota(jnp.int32, (2 * ATT_BLOCK, 3 * ATT_BLOCK), 0)
    ck = lax.broadcasted_iota(jnp.int32, (2 * ATT_BLOCK, 3 * ATT_BLOCK), 1)
    dist = iq - ck + jnp.where(has_prev, ATT_BLOCK, 0)
    valid = jnp.logical_and(dist >= 0, dist <= ATT_BLOCK)
    return jnp.where(valid, _dot_nt(qb, k) - slope_d[:, 0:1] * dist.astype(F32), NEG)


ATT_UNROLL = 4


def _att_blocks(i, dil, nb):
    per = dil * nb // ATT_UNROLL
    assert per * ATT_UNROLL == dil * nb
    for i0 in range(per):
        blocks = [divmod(i0 + u * per, nb) for u in range(ATT_UNROLL)]
        assert all(a[0] != b[0] or abs(a[1] - b[1]) >= 2 for n, a in enumerate(blocks) for b in blocks[n + 1:])
    curs, prvs, has_prev = [], [], []
    for u in range(ATT_UNROLL):
        t = i + u * per
        r = lax.div(t, nb)
        j = lax.rem(t, nb)
        base = r + dil * ATT_BLOCK * j
        pbase = base - dil * ATT_BLOCK * jnp.minimum(j, 1)
        if dil == 1:
            base, pbase = pl.multiple_of(base, ATT_BLOCK), pl.multiple_of(pbase, ATT_BLOCK)
        curs.append(pl.ds(base, ATT_BLOCK, stride=dil))
        prvs.append(pl.ds(pbase, ATT_BLOCK, stride=dil))
        has_prev.append(j > 0)
    return curs, prvs, has_prev


def _att_fwd(proj, group):
    s = proj.shape[0]
    dil = DIL_GROUPS[group][1]
    assert DIL_GROUPS[group][0] // dil == ATT_BLOCK
    nb = s // dil // ATT_BLOCK
    assert nb * dil * ATT_BLOCK == s

    def body(q_ref, k_ref, v_ref, o_ref, lse_ref):
        def emit(rows, num, den, mx):
            o_ref[rows, :] = num / den
            lse_ref[rows, :] = jnp.broadcast_to(mx + jnp.log(den), (num.shape[0], DIL_DH))

        slope_d = _slope(group, pl.program_id(0)) * float(dil)

        def step(i, carry):
            curs, prvs, has_prev = _att_blocks(i, dil, nb)
            us = range(ATT_UNROLL)
            qb = [q_ref[c, :] * (DIL_DH ** -0.5) for c in curs]
            sc = [_att_scores(qb[u], k_ref[curs[u], :], k_ref[prvs[u], :], slope_d, has_prev[u]) for u in us]
            mx = [jnp.maximum(jnp.max(a, axis=-1, keepdims=True), jnp.max(b, axis=-1, keepdims=True)) for a, b in sc]
            p_cur = [jnp.exp(sc[u][0] - mx[u]) for u in us]
            p_prev = [jnp.exp(sc[u][1] - mx[u]) for u in us]
            den = [jnp.sum(p_cur[u], axis=-1, keepdims=True) + jnp.sum(p_prev[u], axis=-1, keepdims=True) for u in us]
            num = [_dot(p_cur[u], v_ref[curs[u], :]) + _dot(p_prev[u], v_ref[prvs[u], :]) for u in us]
            for u in us:
                emit(curs[u], num[u], den[u], mx[u])
            return carry

        def step_whole(i, carry):
            rows = [pl.ds(i * ATT_UNROLL + u, 2 * ATT_BLOCK, stride=dil) for u in range(ATT_UNROLL)]
            sc = [_att_scores_whole(q_ref[r, :] * (DIL_DH ** -0.5), k_ref[r, :], slope_d) for r in rows]
            mx = [jnp.max(a, axis=-1, keepdims=True) for a in sc]
            p = [jnp.exp(a - m) for a, m in zip(sc, mx)]
            num = [_dot(pu, v_ref[r, :]) for pu, r in zip(p, rows)]
            for u, r in enumerate(rows):
                emit(r, num[u], jnp.sum(p[u], axis=-1, keepdims=True), mx[u])
            return carry

        def step_tile(i, carry):
            qrows, krows, has_prev = _att_tiles(i, dil, nb)
            us = range(ATT_UNROLL)
            sc = [_att_scores_tile(q_ref[qrows[u], :] * (DIL_DH ** -0.5), k_ref[krows[u], :], slope_d, has_prev[u]) for u in us]
            mx = [jnp.max(a, axis=-1, keepdims=True) for a in sc]
            p = [jnp.exp(a - m) for a, m in zip(sc, mx)]
            num = [_dot(p[u], v_ref[krows[u], :]) for u in us]
            for u in us:
                emit(qrows[u], num[u], jnp.sum(p[u], axis=-1, keepdims=True), mx[u])
            return carry

        if nb == 2:
            lax.fori_loop(0, dil // ATT_UNROLL, step_whole, 0)
        elif nb % 2 == 0:
            lax.fori_loop(0, dil * nb // 2 // ATT_UNROLL, step_tile, 0)
        else:
            lax.fori_loop(0, dil * nb // ATT_UNROLL, step, 0)

    def col(off):
        return pl.BlockSpec((s, DIL_DH), lambda h: (0, off // DIL_DH + group * DIL_HEADS + h))

    out = pl.BlockSpec((s, DIL_DH), lambda h: (0, h))
    return pl.pallas_call(
        body, name=f"att_fwd{group}", grid=(DIL_HEADS,), in_specs=[col(OFF_Q_B), col(OFF_K_B), col(OFF_V_B)],
        out_specs=[out, out], out_shape=[jax.ShapeDtypeStruct((s, DIL_W), F32)] * 2,
        compiler_params=_cparams("parallel"))(proj, proj, proj)


def _att_bwd(proj, group, do, lse, delta):
    s = proj.shape[0]
    dil = DIL_GROUPS[group][1]
    nb = s // dil // ATT_BLOCK

    def body(q_ref, k_ref, v_ref, do_ref, lse_ref, dl_ref, dq_ref, dk_ref, dv_ref, dq_acc, dk_acc, dv_acc):
        slope_d = _slope(group, pl.program_id(0)) * float(dil)
        dk_acc[...] = jnp.zeros_like(dk_acc)
        dv_acc[...] = jnp.zeros_like(dv_acc)

        def step(i, carry):
            curs, prvs, has_prev = _att_blocks(i, dil, nb)
            us = range(ATT_UNROLL)
            qb = [q_ref[c, :] * (DIL_DH ** -0.5) for c in curs]
            k_cur, k_prev = [k_ref[c, :] for c in curs], [k_ref[p, :] for p in prvs]
            v_cur, v_prev = [v_ref[c, :] for c in curs], [v_ref[p, :] for p in prvs]
            sc = [_att_scores(qb[u], k_cur[u], k_prev[u], slope_d, has_prev[u]) for u in us]
            lse_b, delta_b, dob = [lse_ref[c, :] for c in curs], [dl_ref[c, :] for c in curs], [do_ref[c, :] for c in curs]
            p_cur = [jnp.exp(sc[u][0] - lse_b[u]) for u in us]
            p_prev = [jnp.exp(sc[u][1] - lse_b[u]) for u in us]
            ds_cur = [p_cur[u] * (_dot_nt(dob[u], v_cur[u]) - delta_b[u]) for u in us]
            ds_prev = [p_prev[u] * (_dot_nt(dob[u], v_prev[u]) - delta_b[u]) for u in us]
            dq = [(_dot(ds_cur[u], k_cur[u]) + _dot(ds_prev[u], k_prev[u])) * (DIL_DH ** -0.5) for u in us]
            dk_c = [_dot_tn(ds_cur[u], qb[u]) for u in us]
            dv_c = [_dot_tn(p_cur[u], dob[u]) for u in us]
            dk_p = [_dot_tn(ds_prev[u], qb[u]) for u in us]
            dv_p = [_dot_tn(p_prev[u], dob[u]) for u in us]
            for u in us:
                dq_acc[curs[u], :] = dq[u]
                dk_acc[curs[u], :] += dk_c[u]
                dv_acc[curs[u], :] += dv_c[u]
            for u in us:
                dk_acc[prvs[u], :] += dk_p[u]
                dv_acc[prvs[u], :] += dv_p[u]
            return carry

        def step_whole(i, carry):
            rows = [pl.ds(i * ATT_UNROLL + u, 2 * ATT_BLOCK, stride=dil) for u in range(ATT_UNROLL)]
            qb = [q_ref[r, :] * (DIL_DH ** -0.5) for r in rows]
            kk, vv, dob = [k_ref[r, :] for r in rows], [v_ref[r, :] for r in rows], [do_ref[r, :] for r in rows]
            sc = [_att_scores_whole(qb[u], kk[u], slope_d) for u in range(ATT_UNROLL)]
            p = [jnp.exp(sc[u] - lse_ref[r, :][:, 0:1]) for u, r in enumerate(rows)]
            ds = [p[u] * (_dot_nt(dob[u], vv[u]) - dl_ref[r, :][:, 0:1]) for u, r in enumerate(rows)]
            dq = [_dot(ds[u], kk[u]) * (DIL_DH ** -0.5) for u in range(ATT_UNROLL)]
            dk = [_dot_tn(ds[u], qb[u]) for u in range(ATT_UNROLL)]
            dv = [_dot_tn(p[u], dob[u]) for u in range(ATT_UNROLL)]
            for u, r in enumerate(rows):
                dq_acc[r, :] = dq[u]
                dk_acc[r, :] = dk[u]
                dv_acc[r, :] = dv[u]
            return carry

        def step_tile(i, carry):
            qrows, krows, has_prev = _att_tiles(i, dil, nb)
            us = range(ATT_UNROLL)
            qb = [q_ref[r, :] * (DIL_DH ** -0.5) for r in qrows]
            kk, vv, dob = [k_ref[r, :] for r in krows], [v_ref[r, :] for r in krows], [do_ref[r, :] for r in qrows]
            sc = [_att_scores_tile(qb[u], kk[u], slope_d, has_prev[u]) for u in us]
            p = [jnp.exp(sc[u] - lse_ref[qrows[u], :][:, 0:1]) for u in us]
            ds = [p[u] * (_dot_nt(dob[u], vv[u]) - dl_ref[qrows[u], :][:, 0:1]) for u in us]
            dq = [_dot(ds[u], kk[u]) * (DIL_DH ** -0.5) for u in us]
            dk = [_dot_tn(ds[u], qb[u]) for u in us]
            dv = [_dot_tn(p[u], dob[u]) for u in us]
            for u in us:
                dq_acc[qrows[u], :] = dq[u]
                dk_acc[krows[u], :] += dk[u]
                dv_acc[krows[u], :] += dv[u]
            return carry

        if nb == 2:
            lax.fori_loop(0, dil // ATT_UNROLL, step_whole, 0)
        elif nb % 2 == 0:
            lax.fori_loop(0, dil * nb // 2 // ATT_UNROLL, step_tile, 0)
        else:
            lax.fori_loop(0, dil * nb // ATT_UNROLL, step, 0)
        dq_ref[...] = dq_acc[...].astype(BF16)
        dk_ref[...] = dk_acc[...].astype(BF16)
        dv_ref[...] = dv_acc[...].astype(BF16)

    def col(off):
        return pl.BlockSpec((s, DIL_DH), lambda h: (0, off // DIL_DH + group * DIL_HEADS + h))

    hd = pl.BlockSpec((s, DIL_DH), lambda h: (0, h))
    return pl.pallas_call(
        body, name=f"att_bwd{group}", grid=(DIL_HEADS,),
        in_specs=[col(OFF_Q_B), col(OFF_K_B), col(OFF_V_B), hd, hd, hd], out_specs=[hd, hd, hd],
        out_shape=[jax.ShapeDtypeStruct((s, DIL_W), BF16)] * 3,
        scratch_shapes=[pltpu.VMEM((s, DIL_DH), F32)] * 3,
        compiler_params=_cparams("parallel"))(proj, proj, proj, do, lse, delta)


def _att_merge(parts, proj):
    s = proj.shape[0]

    def body(o0, l0, o1, l1, o2, l2, z_ref, ob_ref, o_ref, lse_ref, obt_ref):
        m = jnp.maximum(jnp.maximum(l0[...], l1[...]), l2[...])
        num = jnp.zeros_like(m)
        den = jnp.zeros_like(m)
        for og, lg in ((o0, l0), (o1, l1), (o2, l2)):
            sc = jnp.exp(lg[...] - m)
            num = num + og[...] * sc
            den = den + sc
        o = num / den
        o_ref[...] = o
        lse_ref[...] = m + jnp.log(den)
        ob = o * _silu(z_ref[...])
        ob_ref[...] = ob.astype(BF16)
        obt_ref[...] = ob.T.astype(BF16)

    row = pl.BlockSpec((ROW_TILE, DIL_W), lambda i: (i, 0))
    flat = [a for p in parts for a in p]
    return pl.pallas_call(
        body, name="att_merge", grid=(s // ROW_TILE,),
        in_specs=[row] * 6 + [pl.BlockSpec((ROW_TILE, DIL_W), lambda i: (i, OFF_Z_B // DIL_W))],
        out_specs=[row, row, row, pl.BlockSpec((DIL_W, ROW_TILE), lambda i: (0, i))],
        out_shape=[jax.ShapeDtypeStruct((s, DIL_W), BF16), jax.ShapeDtypeStruct((s, DIL_W), F32),
                   jax.ShapeDtypeStruct((s, DIL_W), F32), jax.ShapeDtypeStruct((DIL_W, s), BF16)],
        compiler_params=_cparams("parallel"))(*flat, proj)


def _att_merge_bwd(o, proj, dob, dproj):
    s = o.shape[0]

    def body(o_ref, z_ref, d_ref, dproj_in, do_ref, dl_ref, dz_ref):
        ov, zv, dv = o_ref[...], z_ref[...], d_ref[...]
        do = dv * _silu(zv)
        do_ref[...] = do
        dz_ref[...] = (dv * ov * _silu_grad(zv)).astype(BF16)
        for h in range(DIL_HEADS):
            sl = slice(h * DIL_DH, (h + 1) * DIL_DH)
            dl_ref[:, sl] = jnp.broadcast_to(jnp.sum(do[:, sl] * ov[:, sl], axis=-1, keepdims=True), (ROW_TILE, DIL_DH))

    row = pl.BlockSpec((ROW_TILE, DIL_W), lambda i: (i, 0))
    return pl.pallas_call(
        body, name="att_merge_bwd", grid=(s // ROW_TILE,),
        in_specs=[row, pl.BlockSpec((ROW_TILE, DIL_W), lambda i: (i, OFF_Z_B // DIL_W)), row, DPROJ_IN],
        out_specs=[row, row, pl.BlockSpec((ROW_TILE, DIL_W), lambda i: (i, OFF_Z_B // DIL_W))],
        out_shape=[jax.ShapeDtypeStruct((s, DIL_W), F32), jax.ShapeDtypeStruct((s, DIL_W), F32),
                   jax.ShapeDtypeStruct((s, PW), BF16)],
        input_output_aliases={3: 2},
        compiler_params=_cparams("parallel"))(o, proj, dob, dproj)


def _merge(proj, ya, yb):
    s = proj.shape[0]

    def body(ga_ref, gb_ref, ya_ref, yb_ref, o_ref, ot_ref):
        m = _sigmoid(ga_ref[...]) * ya_ref[...] + _sigmoid(gb_ref[...]) * yb_ref[...]
        o_ref[...] = m.astype(BF16)
        ot_ref[...] = m.T.astype(BF16)

    row = pl.BlockSpec((ROW_TILE, D_MODEL), lambda i: (i, 0))
    return pl.pallas_call(
        body, name="merge", grid=(s // ROW_TILE,),
        in_specs=[pl.BlockSpec((ROW_TILE, D_MODEL), lambda i: (i, OFF_G_A // D_MODEL)),
                  pl.BlockSpec((ROW_TILE, D_MODEL), lambda i: (i, OFF_G_B // D_MODEL)), row, row],
        out_specs=[row, pl.BlockSpec((D_MODEL, ROW_TILE), lambda i: (0, i))],
        out_shape=[jax.ShapeDtypeStruct((s, D_MODEL), BF16), jax.ShapeDtypeStruct((D_MODEL, s), BF16)],
        compiler_params=_cparams("parallel"))(proj, proj, ya, yb)


def _merge_bwd(proj, ya, yb, dm):
    s = proj.shape[0]

    def body(ga_ref, gb_ref, ya_ref, yb_ref, dm_ref, dya_ref, dyb_ref, dga_ref, dgb_ref):
        dmv = dm_ref[...]
        sa, sb = _sigmoid(ga_ref[...]), _sigmoid(gb_ref[...])
        dya_ref[...] = (dmv * sa).astype(BF16)
        dyb_ref[...] = (dmv * sb).astype(BF16)
        dga_ref[...] = (dmv * ya_ref[...] * sa * (1.0 - sa)).astype(BF16)
        dgb_ref[...] = (dmv * yb_ref[...] * sb * (1.0 - sb)).astype(BF16)

    row = pl.BlockSpec((ROW_TILE, D_MODEL), lambda i: (i, 0))
    return pl.pallas_call(
        body, name="merge_bwd", grid=(s // ROW_TILE,),
        in_specs=[pl.BlockSpec((ROW_TILE, D_MODEL), lambda i: (i, OFF_G_A // D_MODEL)),
                  pl.BlockSpec((ROW_TILE, D_MODEL), lambda i: (i, OFF_G_B // D_MODEL)), row, row, row],
        out_specs=[row] * 4, out_shape=[jax.ShapeDtypeStruct((s, D_MODEL), BF16)] * 4,
        compiler_params=_cparams("parallel"))(proj, proj, ya, yb, dm)


def _final(x, t, fw, tgt):
    s, d = x.shape

    def body(x_ref, t_ref, w_ref, y_ref, dx_ref, dw_ref, l_ref):
        i = pl.program_id(0)
        x2 = x_ref[...] + t_ref[...]
        wv = w_ref[...]
        r = lax.rsqrt(jnp.mean(x2 * x2, axis=-1, keepdims=True) + NORM_EPS)
        e = x2 * r * wv - y_ref[...]
        lrow = jnp.mean(e * e, axis=-1, keepdims=True)
        lpart = jnp.broadcast_to(0.5 * jnp.sum(lrow, axis=0, keepdims=True), (1, 128))
        dy = e * (1.0 / d)
        dwp = jnp.sum(dy * x2 * r, axis=0, keepdims=True)
        dyw = dy * wv
        dx_ref[...] = r * dyw - x2 * (r * r * r) * jnp.mean(dyw * x2, axis=-1, keepdims=True)

        @pl.when(i == 0)
        def _():
            dw_ref[...] = dwp
            l_ref[...] = lpart

        @pl.when(i > 0)
        def _():
            dw_ref[...] += dwp
            l_ref[...] += lpart

    row = pl.BlockSpec((ROW_TILE, d), lambda i: (i, 0))
    vec = pl.BlockSpec((1, d), lambda i: (0, 0))
    return pl.pallas_call(
        body, name="final", grid=(s // ROW_TILE,), in_specs=[row, row, vec, row],
        out_specs=[row, vec, pl.BlockSpec((1, 128), lambda i: (0, 0))],
        out_shape=[jax.ShapeDtypeStruct((s, d), F32), jax.ShapeDtypeStruct((1, d), F32), jax.ShapeDtypeStruct((1, 128), F32)],
        compiler_params=_cparams("arbitrary"))(x, t, fw, tgt)


def _adamw(w, g, m, v, name):
    r, c = w.shape
    cap = max(8, (1 << 18) // c)
    divisors = [t for t in range(8, min(r, cap) + 1, 8) if r % t == 0]
    tr = r if r <= 8 else (max(divisors) if divisors else cap)

    def body(w_ref, g_ref, m_ref, v_ref, d_ref, nm_ref, nv_ref):
        gv = g_ref[...]
        mn = ADAM_B1 * m_ref[...] + (1.0 - ADAM_B1) * gv
        vn = ADAM_B2 * v_ref[...] + (1.0 - ADAM_B2) * (gv * gv)
        m_hat = mn / (1.0 - ADAM_B1 ** ADAM_STEP)
        v_hat = vn / (1.0 - ADAM_B2 ** ADAM_STEP)
        d_ref[...] = -ADAM_LR * (m_hat / (jnp.sqrt(v_hat) + ADAM_EPS) + ADAM_WD * w_ref[...])
        nm_ref[...] = mn
        nv_ref[...] = vn

    blk = pl.BlockSpec((tr, c), lambda i: (i, 0))
    return pl.pallas_call(
        body, name=name, grid=(pl.cdiv(r, tr),), in_specs=[blk] * 4, out_specs=[blk] * 3,
        out_shape=[jax.ShapeDtypeStruct((r, c), F32)] * 3, compiler_params=_cparams("parallel"))(w, g, m, v)


HBM_SPEC = pl.BlockSpec(memory_space=pl.ANY)


def _place():
    x, y, c = lax.axis_index("x"), lax.axis_index("y"), lax.axis_index("c")
    chips = [(1 - x, y), (x, 1 - y), (1 - x, 1 - y)]
    return x, y, c, chips


def _ag_weights(packs):
    na = len(packs)
    nsem = 8

    def body(*refs):
        p_refs, out_refs = refs[:na], refs[na:2 * na]
        send_sems, recv_sems = refs[2 * na:]
        x, y, c, _ = _place()
        me, sib, j = (x, y, c), (x, y, 1 - c), 2 * x + y
        xn, yn = (1 - x, y, c), (x, 1 - y, c)
        jx, jy, jd = 2 * (1 - x) + y, 2 * x + (1 - y), 2 * (1 - x) + (1 - y)

        def rc(a, k, src, dst, to):
            return pltpu.make_async_remote_copy(src_ref=src, dst_ref=dst, send_sem=send_sems.at[nsem * a + k],
                                                recv_sem=recv_sems.at[nsem * a + k], device_id=to, device_id_type=MESH)

        sent = []
        for a in range(na):
            mine, land = p_refs[a].at[c], out_refs[a].at[j, c]
            sent += [rc(a, 0, mine, land, xn), rc(a, 1, mine, land, yn), rc(a, 7, p_refs[a], out_refs[a].at[j], sib)]
        for cp in sent:
            cp.start()
        for a in range(na):
            half = p_refs[a].shape[1] // 2
            top, bottom = pl.ds(0, half), pl.ds(half, half)
            from_x, from_y, from_d = out_refs[a].at[jx, c], out_refs[a].at[jy, c], out_refs[a].at[jd, c]
            rc(a, 0, p_refs[a].at[c], from_x, me).wait_recv()
            later = [rc(a, 2, from_x.at[top], from_x.at[top], yn), rc(a, 4, from_x, from_x, sib)]
            for cp in later:
                cp.start()
            sent += later
            rc(a, 1, p_refs[a].at[c], from_y, me).wait_recv()
            later = [rc(a, 3, from_y.at[bottom], from_y.at[bottom], xn), rc(a, 5, from_y, from_y, sib)]
            for cp in later:
                cp.start()
            sent += later
            rc(a, 2, from_d.at[top], from_d.at[top], me).wait_recv()
            rc(a, 3, from_d.at[bottom], from_d.at[bottom], me).wait_recv()
            cp = rc(a, 6, from_d, from_d, sib)
            cp.start()
            sent.append(cp)
        for a in range(na):
            for k, jj in ((4, jx), (5, jy), (6, jd)):
                rc(a, k, p_refs[a].at[c], out_refs[a].at[jj, 1 - c], me).wait_recv()
            rc(a, 7, p_refs[a], out_refs[a].at[j], me).wait_recv()
        for cp in sent:
            cp.wait_send()

    return pl.pallas_call(
        body, name="ag_weights",
        out_shape=[jax.ShapeDtypeStruct((N_CHIPS,) + p.shape, p.dtype) for p in packs],
        in_specs=[HBM_SPEC] * na, out_specs=[HBM_SPEC] * na,
        scratch_shapes=[pltpu.SemaphoreType.DMA((nsem * na,)), pltpu.SemaphoreType.DMA((nsem * na,))])(*packs)


def _rs_pair(dwpt, gpack):
    n = N_CHIPS
    hw = SHARD_PAD // 2

    def body(d_ref, g_ref, out_d, out_g, send_sems, recv_sems):
        x, y, c, _ = _place()
        sib = (x, y, 1 - c)
        cps = []
        for p in range(n):
            start = pl.multiple_of(WIN_BASE[p] + (1 - c) * hw, TILE_ROWS)
            cps.append(pltpu.make_async_remote_copy(
                src_ref=d_ref.at[pl.ds(start, hw)], dst_ref=out_d.at[p], send_sem=send_sems.at[p],
                recv_sem=recv_sems.at[p], device_id=sib, device_id_type=MESH))
            cps.append(pltpu.make_async_remote_copy(
                src_ref=g_ref.at[p, 1 - c], dst_ref=out_g.at[p], send_sem=send_sems.at[n + p],
                recv_sem=recv_sems.at[n + p], device_id=sib, device_id_type=MESH))
        for cp in cps:
            cp.start()
        for cp in cps:
            cp.wait_recv()
        for cp in cps:
            cp.wait_send()

    return pl.pallas_call(
        body, name="rs_pair",
        out_shape=[jax.ShapeDtypeStruct((n, hw, dwpt.shape[1]), dwpt.dtype),
                   jax.ShapeDtypeStruct((n,) + gpack.shape[2:], gpack.dtype)],
        in_specs=[HBM_SPEC] * 2, out_specs=[HBM_SPEC] * 2,
        scratch_shapes=[pltpu.SemaphoreType.DMA((2 * n,)), pltpu.SemaphoreType.DMA((2 * n,))])(dwpt, gpack)


def _add_halves_win(dwpt, other, c):
    n, rh, wd = other.shape
    tr = _row_tile(rh)

    def body(s_ref, d_ref, o_ref, out_ref):
        out_ref[0] = (d_ref[...] + o_ref[0]).astype(BF16)

    scal = jnp.concatenate([jnp.reshape(c, (1,)).astype(jnp.int32), jnp.asarray(WIN_BASE, jnp.int32)])
    grid_spec = pltpu.PrefetchScalarGridSpec(
        num_scalar_prefetch=1, grid=(n, rh // tr),
        in_specs=[pl.BlockSpec((pl.Element(tr), pl.Element(wd)),
                               lambda p, i, sr: (pl.multiple_of(sr[1 + p] + sr[0] * rh + i * tr, TILE_ROWS), 0)),
                  pl.BlockSpec((1, tr, wd), lambda p, i, sr: (p, i, 0))],
        out_specs=pl.BlockSpec((1, tr, wd), lambda p, i, sr: (p, i, 0)))
    return pl.pallas_call(
        body, name="add_halves_in", grid_spec=grid_spec, out_shape=jax.ShapeDtypeStruct((n, rh, wd), BF16),
        compiler_params=_cparams("parallel", "parallel"))(scal, dwpt, other)


SEM_SPEC = pl.BlockSpec(memory_space=pltpu.SEMAPHORE)
DATAFLOW_EFFECT = pltpu.SideEffectType.DATAFLOW_SIDE_EFFECTING


def _rs_chips_start(csums):
    na = len(csums)

    def body(*refs):
        s_refs, land_refs = refs[:na], refs[na:2 * na]
        send_sems, recv_sems = refs[2 * na], refs[2 * na + 1]
        token = refs[-1]
        x, y, c, chips = _place()
        j = 2 * x + y
        for a in range(na):
            for k, (cx, cy) in enumerate(chips):
                pltpu.make_async_remote_copy(src_ref=s_refs[a].at[2 * cx + cy], dst_ref=land_refs[a].at[j],
                                             send_sem=send_sems.at[3 * a + k], recv_sem=recv_sems.at[3 * a + k],
                                             device_id=(cx, cy, c), device_id_type=MESH).start()
        token[...] = jnp.zeros_like(token)

    hbm = [pltpu.HBM(s.shape, s.dtype) for s in csums]
    args = [pltpu.with_memory_space_constraint(s, pltpu.HBM) for s in csums]
    args += [pltpu.with_memory_space_constraint(lax.empty(s.shape, s.dtype), pltpu.HBM) for s in csums]
    res = pl.pallas_call(
        body, name="rs_chips_start",
        out_shape=(pltpu.SemaphoreType.DMA((3 * na,)), pltpu.SemaphoreType.DMA((3 * na,)), *hbm, *hbm,
                   jax.ShapeDtypeStruct((8, 128), F32)),
        in_specs=[pl.BlockSpec(memory_space=pltpu.HBM)] * (2 * na),
        out_specs=(SEM_SPEC, SEM_SPEC, *[pl.BlockSpec(memory_space=pltpu.HBM)] * (2 * na),
                   pl.BlockSpec(memory_space=pltpu.VMEM)),
        input_output_aliases={i: 2 + i for i in range(2 * na)},
        compiler_params=pltpu.CompilerParams(has_side_effects=DATAFLOW_EFFECT))(*args)
    return res[0], res[1], list(res[2:2 + na]), list(res[2 + na:2 + 2 * na]), res[-1]


def _rs_chips_wait(send_sems, recv_sems, csums, lands, after):
    na = len(csums)

    def body(*refs):
        s_refs, land_refs = refs[:na], refs[na:2 * na]
        send_sems, recv_sems = refs[2 * na], refs[2 * na + 1]
        x, y, c, chips = _place()
        j = 2 * x + y
        for a in range(na):
            for k, (cx, cy) in enumerate(chips):
                cp = pltpu.make_async_remote_copy(src_ref=s_refs[a].at[2 * cx + cy], dst_ref=land_refs[a].at[2 * cx + cy],
                                                  send_sem=send_sems.at[3 * a + k], recv_sem=recv_sems.at[3 * a + k],
                                                  device_id=(cx, cy, c), device_id_type=MESH)
                cp.wait_send()
                cp.wait_recv()

    hbm = [pltpu.HBM(s.shape, s.dtype) for s in csums]
    res = pl.pallas_call(
        body, name="rs_chips_wait", out_shape=(*hbm, *hbm),
        in_specs=[pl.BlockSpec(memory_space=pltpu.HBM)] * (2 * na) + [SEM_SPEC, SEM_SPEC, pl.BlockSpec(memory_space=pl.ANY)],
        out_specs=tuple([pl.BlockSpec(memory_space=pltpu.HBM)] * (2 * na)),
        input_output_aliases={i: i for i in range(2 * na)},
        compiler_params=pltpu.CompilerParams(has_side_effects=DATAFLOW_EFFECT))(*csums, *lands, send_sems, recv_sems, after)
    return list(res[:na]), list(res[na:])


SWAP_CHUNKS = 4


def _pair_swap(halves):
    na = len(halves)

    def body(*refs):
        h_refs, out_refs = refs[:na], refs[na:2 * na]
        send_sems, recv_sems = refs[2 * na:]
        x, y, c, _ = _place()
        cps = []
        for a in range(na):
            rows = h_refs[a].shape[0] // SWAP_CHUNKS
            assert rows * SWAP_CHUNKS == h_refs[a].shape[0]
            for q in range(SWAP_CHUNKS):
                k = SWAP_CHUNKS * a + q
                cps.append(pltpu.make_async_remote_copy(
                    src_ref=h_refs[a].at[pl.ds(q * rows, rows)], dst_ref=out_refs[a].at[pl.ds(q * rows, rows)],
                    send_sem=send_sems.at[k], recv_sem=recv_sems.at[k], device_id=(x, y, 1 - c), device_id_type=MESH))
        for cp in cps:
            cp.start()
        for cp in cps:
            cp.wait_recv()
        for cp in cps:
            cp.wait_send()

    return pl.pallas_call(
        body, name="pair_swap", out_shape=[jax.ShapeDtypeStruct(h.shape, h.dtype) for h in halves],
        in_specs=[HBM_SPEC] * na, out_specs=[HBM_SPEC] * na,
        scratch_shapes=[pltpu.SemaphoreType.DMA((SWAP_CHUNKS * na,)), pltpu.SemaphoreType.DMA((SWAP_CHUNKS * na,))])(*halves)


def _ag_small(v):
    m_per, n = v.shape

    def body(x_ref, out_ref, send_sems, recv_sems, local_sem):
        x, y, c, chips = _place()
        me, sibling = (x, y, c), (x, y, 1 - c)

        def rows(px, py, pc):
            return out_ref.at[pl.ds((4 * px + 2 * py + pc) * m_per, m_per), :]

        def copy(k, block, to, src=None):
            return pltpu.make_async_remote_copy(
                src_ref=rows(*block) if src is None else src, dst_ref=rows(*block), send_sem=send_sems.at[k],
                recv_sem=recv_sems.at[k], device_id=to, device_id_type=MESH)

        mine = pltpu.make_async_copy(x_ref, rows(*me), local_sem)
        mine.start()
        first = [copy(0, me, sibling, src=x_ref)]
        first += [copy(1 + k, me, (*chip, c), src=x_ref) for k, chip in enumerate(chips)]
        for cp in first:
            cp.start()
        passed = [copy(4 + k, (*chip, c), sibling) for k, chip in enumerate(chips)]
        for k, chip in enumerate(chips):
            copy(1 + k, (*chip, c), me).wait_recv()
            passed[k].start()
        copy(0, sibling, me).wait_recv()
        for k, chip in enumerate(chips):
            copy(4 + k, (*chip, 1 - c), me).wait_recv()
        for cp in first + passed:
            cp.wait_send()
        mine.wait()

    return pl.pallas_call(
        body, name="ag_small", out_shape=jax.ShapeDtypeStruct((8 * m_per, n), v.dtype),
        in_specs=[pl.BlockSpec(memory_space=pltpu.VMEM)], out_specs=pl.BlockSpec(memory_space=pltpu.VMEM),
        scratch_shapes=[pltpu.SemaphoreType.DMA((7,)), pltpu.SemaphoreType.DMA((7,)), pltpu.SemaphoreType.DMA])(v)


def _sum_blocks(a, nblk, name):
    rows, wd = a.shape
    r = rows // nblk
    tr = min(r, ROW_TILE)
    assert r % tr == 0

    def body(*refs):
        acc = refs[0][...].astype(F32)
        for ref in refs[1:nblk]:
            acc = acc + ref[...].astype(F32)
        refs[nblk][...] = acc

    nt = r // tr
    return pl.pallas_call(
        body, name=name, grid=(nt,),
        in_specs=[pl.BlockSpec((tr, wd), functools.partial(lambda i, b: (b * nt + i, 0), b=b)) for b in range(nblk)],
        out_specs=pl.BlockSpec((tr, wd), lambda i: (i, 0)),
        out_shape=jax.ShapeDtypeStruct((r, wd), F32), compiler_params=_cparams("parallel"))(*([a] * nblk))


def _row_tile(rows):
    best = max(t for t in range(16, 513, 16) if rows % t == 0)
    return best


def _sum_chips(by_src, csum, j, name):
    n, rh, wd = by_src.shape
    tr = _row_tile(rh)

    def body(j_ref, *refs):
        own = refs[n][0].astype(F32)
        acc = None
        for k in range(n):
            term = jnp.where(j_ref[0] == k, own, refs[k][0].astype(F32))
            acc = term if acc is None else acc + term
        refs[n + 1][...] = acc

    def other(k):
        return pl.BlockSpec((1, tr, wd), lambda i, jr: (jnp.where(jr[0] == k, (k + 1) % n, k), i, 0))

    grid_spec = pltpu.PrefetchScalarGridSpec(
        num_scalar_prefetch=1, grid=(rh // tr,),
        in_specs=[other(k) for k in range(n)] + [pl.BlockSpec((1, tr, wd), lambda i, jr: (jr[0], i, 0))],
        out_specs=pl.BlockSpec((tr, wd), lambda i, jr: (i, 0)))
    return pl.pallas_call(
        body, name=name, grid_spec=grid_spec, out_shape=jax.ShapeDtypeStruct((rh, wd), F32),
        compiler_params=_cparams("parallel"))(jnp.reshape(j, (1,)).astype(jnp.int32), *([by_src] * n), csum)


def _add_halves(gpack, other, c, name):
    n, _, rh, wd = gpack.shape
    tr = _row_tile(rh)

    def body(c_ref, g_ref, o_ref, out_ref):
        out_ref[0] = (g_ref[0, 0] + o_ref[0]).astype(BF16)

    grid_spec = pltpu.PrefetchScalarGridSpec(
        num_scalar_prefetch=1, grid=(n, rh // tr),
        in_specs=[pl.BlockSpec((1, 1, tr, wd), lambda p, i, cr: (p, cr[0], i, 0)),
                  pl.BlockSpec((1, tr, wd), lambda p, i, cr: (p, i, 0))],
        out_specs=pl.BlockSpec((1, tr, wd), lambda p, i, cr: (p, i, 0)))
    return pl.pallas_call(
        body, name=name, grid_spec=grid_spec, out_shape=jax.ShapeDtypeStruct((n, rh, wd), BF16),
        compiler_params=_cparams("parallel", "parallel"))(jnp.reshape(c, (1,)).astype(jnp.int32), gpack, other)


PACK_W = 1024
ROWS_O_DN = DN_W // N_CHIPS
ROWS_O_DIL = DIL_W * (D_MODEL // N_CHIPS) // PACK_W
ROWS_OUT = D_MODEL // N_CHIPS
ROWS_CONV = 4 * (3 * DN_W // N_CHIPS) // PACK_W
R1 = ROWS_O_DN
R2 = R1 + ROWS_O_DIL
R3 = R2 + ROWS_OUT
R4 = R3 + 16
R5 = R4 + 16
PACK_ROWS = 704
HALF_ROWS = PACK_ROWS // 2
SHARD_PAD = 2880


R6 = R5 + 2 * DN_HEADS

TILE_ROWS = 16
BA_IN_SHARD1 = REF_OFF_BA - SHARD_W
LOCAL_START = (0, SHARD_W, 2 * SHARD_W - 2 * DN_HEADS, 3 * SHARD_W - 2 * DN_HEADS)
LOCAL_END = LOCAL_START[1:] + (OFF_BA,)
WIN_BASE = tuple(s // TILE_ROWS * TILE_ROWS for s in LOCAL_START)


def _to_window(k, shard):
    nba = 2 * DN_HEADS
    body = shard
    if k == 1:
        row = lax.broadcasted_iota(jnp.int32, (SHARD_W - nba, 1), 0)
        body = jnp.where(row < BA_IN_SHARD1, shard[:SHARD_W - nba], shard[nba:])
    lead = LOCAL_START[k] - WIN_BASE[k]
    return jnp.pad(body, ((lead, SHARD_PAD - lead - body.shape[0]), (0, 0)))


def _from_window(k, win, ba):
    nba = 2 * DN_HEADS
    lead = LOCAL_START[k] - WIN_BASE[k]
    if k != 1:
        return win[lead:lead + SHARD_W]
    row = lax.broadcasted_iota(jnp.int32, (SHARD_W, 1), 0)
    before = win[lead:lead + SHARD_W]
    after = jnp.pad(win, ((nba, 0), (0, 0)))[lead:lead + SHARD_W]
    mid = jnp.pad(ba, ((BA_IN_SHARD1, SHARD_W - BA_IN_SHARD1 - nba), (0, 0)))
    return jnp.where(row < BA_IN_SHARD1, before, jnp.where(row < BA_IN_SHARD1 + nba, mid, after))


def _stack_windows(wins, ba):
    pieces = []
    for k in range(N_CHIPS):
        lo = WIN_BASE[k] + (TILE_ROWS if k else 0)
        hi = LOCAL_END[k] // TILE_ROWS * TILE_ROWS
        pieces.append(wins[k][lo - WIN_BASE[k]:hi - WIN_BASE[k]])
        if k + 1 < N_CHIPS:
            assert hi == WIN_BASE[k + 1]
            pieces.append(wins[k][hi - WIN_BASE[k]:hi - WIN_BASE[k] + TILE_ROWS] + wins[k + 1][:TILE_ROWS])
    pieces += [ba, jnp.zeros((PW - OFF_BA - ba.shape[0], ba.shape[1]), ba.dtype)]
    out = jnp.concatenate(pieces, axis=0)
    assert out.shape[0] == PW
    return out


def _local_step(x, tgt, norm_w, wpt, conv_full, a_log, dt_bias, dn_norm_w, w_o_dn, w_o_dil, w_out, final_norm_w):
    s = x.shape[0]
    h, h_t = _rms_in(x, norm_w)
    proj = _matmul(h, wpt, F32, 2048, 1280, 1024, "proj", nt=True)
    c_pre, qkv = _conv_fwd(proj, conv_full)
    gate_par = jnp.zeros((8, 128), F32).at[0, 8:16].set(a_log[0]).at[1, 8:16].set(dt_bias[0])
    bg = _gates_fwd(proj, gate_par)
    o_a, u, w, vn, tmat, states = _gdr_fwd(qkv, bg)
    oa2, oa2_t = _gdr_out(o_a, proj, dn_norm_w)
    ya = _matmul(oa2, w_o_dn, F32, 512, 1024, 1024, "ya")
    parts = [_att_fwd(proj, g) for g in range(N_DIL)]
    ob, o_att, lse, ob_t = _att_merge(parts, proj)
    yb = _matmul(ob, w_o_dil, F32, 512, 1024, 512, "yb")
    mg, mg_t = _merge(proj, ya, yb)
    t = _matmul(mg, w_out, F32, 512, 1024, 1024, "t_out")
    dx2, dfw, lpart = _final(x, t, final_norm_w, tgt)

    dmg = _matmul(dx2, w_out, F32, 512, 1024, 1024, "d_merged", nt=True)
    dw_out = _matmul(mg_t, dx2, F32, 1024, 1024, 1024, "dw_out")
    dya, dyb, dga, dgb = _merge_bwd(proj, ya, yb, dmg)
    doa2 = _matmul(dya, w_o_dn, F32, 512, 1024, 1024, "d_oa2", nt=True)
    dw_o_dn = _matmul(oa2_t, dya, F32, 1024, 1024, 1024, "dw_o_dn")
    dob = _matmul(dyb, w_o_dil, F32, 512, 512, 1024, "d_ob", nt=True)
    dw_o_dil = _matmul(ob_t, dyb, F32, 512, 1024, 1024, "dw_o_dil")
    do_a, dproj, ddnw = _gdr_out_bwd(o_a, proj, dn_norm_w, doa2)
    dq_a, dk_a, dv_a, dbg = _gdr_bwd(qkv, bg, u, w, vn, tmat, states, do_a)
    dproj, dpar = _gates_bwd(proj, gate_par, dbg, dproj)
    dc = _conv_bwd_act(c_pre, dq_a, dk_a, dv_a)
    dproj, dconv = _conv_bwd(proj, dc, conv_full, dproj)
    do_att, delta, dproj = _att_merge_bwd(o_att, proj, dob, dproj)
    dqkv_b = [_att_bwd(proj, g, do_att, lse, delta) for g in range(N_DIL)]
    pieces = [(OFF_Q_B + (N_DIL * i + g) * DIL_W, dqkv_b[g][i]) for i in range(3) for g in range(N_DIL)]
    for off, piece in pieces + [(OFF_G_A, dga), (OFF_G_B, dgb)]:
        dproj = lax.dynamic_update_slice(dproj, piece, (0, off))
    dwpt, dwpt_b = _matmul(h_t, dproj, F32, 1024, 1280, 2048, "dw_in", transpose_out=True, also_bf16=True)

    def finish(after=None):
        dh = _matmul(dproj, wpt, F32, 1024, 1024, 3840, "d_h", after=after)
        grad_x, dnw = _rms_in_bwd(x, norm_w, dh, dx2)
        small = jnp.zeros((8, PACK_W), F32)
        small = small.at[0].set(dnw[0]).at[1].set(dfw[0]).at[2, :DN_D].set(ddnw[0])
        small = small.at[3, :DN_HEADS].set(dpar[0, 8:16]).at[3, DN_HEADS:2 * DN_HEADS].set(dpar[1, 8:16])
        small = small.at[4, 0].set(lpart[0, 0])
        return grad_x, small

    return finish, (dwpt, dwpt_b), dconv, dw_o_dn, dw_o_dil, dw_out


def kernel(x, norm_w, w_in, conv_w, a_log, dt_bias, dn_norm_w, w_o_dn, w_o_dil, w_out, final_norm_w, loss_target, m_norm_w, m_w_in, m_conv_w, m_a_log, m_dt_bias, m_dn_norm_w, m_w_o_dn, m_w_o_dil, m_w_out, m_final_norm_w, v_norm_w, v_w_in, v_conv_w, v_a_log, v_dt_bias, v_dn_norm_w, v_w_o_dn, v_w_o_dil, v_w_out, v_final_norm_w):
    c = lax.axis_index("c")
    j = 2 * lax.axis_index("x") + lax.axis_index("y")
    qw = D_MODEL // N_CHIPS

    cw = conv_w[0].reshape(ROWS_CONV, PACK_W)
    cw = jnp.pad(cw, ((0, 16 - ROWS_CONV), (0, 0)))
    cw_hi = cw.astype(BF16)
    cw_lo = (cw - cw_hi.astype(F32)).astype(BF16)
    shard = w_in[0].T.astype(BF16)
    own_ba = jnp.where(j == 1, shard[BA_IN_SHARD1:BA_IN_SHARD1 + 2 * DN_HEADS], jnp.zeros((2 * DN_HEADS, D_MODEL), BF16))
    pack = jnp.concatenate(
        [w_o_dn[0].astype(BF16), w_o_dil[0].astype(BF16).reshape(ROWS_O_DIL, PACK_W), w_out[0].astype(BF16), cw_hi, cw_lo,
         own_ba, jnp.zeros((PACK_ROWS - R6, PACK_W), BF16)], axis=0).reshape(2, HALF_ROWS, PACK_W)
    chips = range(N_CHIPS)
    own_win = lax.switch(j, [functools.partial(_to_window, k) for k in chips], shard).reshape(2, SHARD_PAD // 2, D_MODEL)
    all_in, allw = _ag_weights([own_win, pack])
    wins = [all_in[k].reshape(SHARD_PAD, D_MODEL) for k in chips]
    allw = [allw[k].reshape(PACK_ROWS, PACK_W) for k in chips]
    wpt = _stack_windows(wins, allw[1][R5:R6])
    w_o_dn_full = jnp.concatenate([allw[k][:R1] for k in chips], axis=0)
    w_o_dil_full = jnp.concatenate([allw[k][R1:R2].reshape(DIL_W, qw) for k in chips], axis=1)
    w_out_full = jnp.concatenate([allw[k][R2:R3] for k in chips], axis=0)
    conv_full = jnp.concatenate(
        [(allw[k][R3:R3 + ROWS_CONV].astype(F32) + allw[k][R4:R4 + ROWS_CONV].astype(F32)).reshape(4, 3 * DN_W // N_CHIPS)
         for k in chips], axis=1)

    finish, (dwpt, dwpt_b), dconv, dw_o_dn, dw_o_dil, dw_out = _local_step(
        x[0], loss_target[0], norm_w, wpt, conv_full, a_log, dt_bias, dn_norm_w, w_o_dn_full, w_o_dil_full, w_out_full,
        final_norm_w.reshape(1, D_MODEL))

    cq = 3 * DN_W // N_CHIPS
    gpack = jnp.stack([
        jnp.concatenate(
            [dw_o_dn[k * qw:(k + 1) * qw], dw_o_dil[:, k * qw:(k + 1) * qw].reshape(ROWS_O_DIL, PACK_W),
             dw_out[k * qw:(k + 1) * qw],
             jnp.pad(dconv[:, k * cq:(k + 1) * cq].reshape(ROWS_CONV, PACK_W), ((0, 16 - ROWS_CONV), (0, 0))),
             dwpt[OFF_BA:OFF_BA + 2 * DN_HEADS] if k == 1 else jnp.zeros((2 * DN_HEADS, PACK_W), F32),
             jnp.zeros((PACK_ROWS - R4 - 2 * DN_HEADS, PACK_W), F32)], axis=0)
        for k in chips]).reshape(N_CHIPS, 2, HALF_ROWS, PACK_W)
    sib_in, sib_pack = _rs_pair(dwpt_b, gpack)
    csum_in = _add_halves_win(dwpt, sib_in, c)
    csum_pack = _add_halves(gpack, sib_pack, c, "add_halves_pack")
    send_sems, recv_sems, csums, lands, token = _rs_chips_start([csum_in, csum_pack])
    grad_x, small = finish(after=token)

    gs = _sum_blocks(_ag_small(small), 8, "sum_small")
    loss = gs[4, 0]
    w_small = jnp.zeros((8, PACK_W), F32)

    def pack_small(nw, fw, dnw_, al, db):
        t = w_small.at[0].set(nw[0]).at[1].set(fw).at[2, :DN_D].set(dnw_[0])
        return t.at[3, :DN_HEADS].set(al[0]).at[3, DN_HEADS:2 * DN_HEADS].set(db[0])

    sm = _adamw(pack_small(norm_w, final_norm_w, dn_norm_w, a_log, dt_bias), gs,
                pack_small(m_norm_w, m_final_norm_w, m_dn_norm_w, m_a_log, m_dt_bias),
                pack_small(v_norm_w, v_final_norm_w, v_dn_norm_w, v_a_log, v_dt_bias), "adamw_small")

    (csum_in, csum_pack), (src_in, src_pack) = _rs_chips_wait(send_sems, recv_sems, csums, lands, sm[0])
    half_in = _sum_chips(src_in, csum_in, j, "sum_chips_in")
    half_pack = _sum_chips(src_pack, csum_pack, j, "sum_chips_pack")
    sib_half_in, sib_half_pack = _pair_swap([half_in, half_pack])

    def both_halves(mine, theirs):
        return jnp.where(c == 0, jnp.concatenate([mine, theirs], axis=0), jnp.concatenate([theirs, mine], axis=0))

    g = both_halves(half_pack, sib_half_pack)
    g_w_in = lax.switch(j, [functools.partial(_from_window, k) for k in chips], both_halves(half_in, sib_half_in),
                        g[R4:R4 + 2 * DN_HEADS])
    g_w_o_dn = g[:R1]
    g_w_o_dil = g[R1:R2].reshape(DIL_W, qw)
    g_w_out = g[R2:R3]
    g_conv = g[R3:R3 + ROWS_CONV].reshape(4, cq)

    def unpack_small(t):
        return dict(norm_w=t[0:1], final_norm_w=t[1], dn_norm_w=t[2:3, :DN_D], a_log=t[3:4, :DN_HEADS],
                    dt_bias=t[3:4, DN_HEADS:2 * DN_HEADS])

    res = {"grad": unpack_small(gs)}
    for kind, arr in zip(("delta", "new_m", "new_v"), sm):
        res[kind] = unpack_small(arr)
    big = dict(conv_w=(conv_w, g_conv, m_conv_w, v_conv_w), w_o_dn=(w_o_dn, g_w_o_dn, m_w_o_dn, v_w_o_dn),
               w_o_dil=(w_o_dil, g_w_o_dil, m_w_o_dil, v_w_o_dil), w_out=(w_out, g_w_out, m_w_out, v_w_out))
    for name, (wt, gt, mt, vt) in big.items():
        d, nm, nv = _adamw(wt[0], gt, mt[0], vt[0], "adamw_" + name)
        res["grad"][name] = gt[None]
        res["delta"][name], res["new_m"][name], res["new_v"][name] = d[None], nm[None], nv[None]

    d, nm, nv = _adamw(w_in[0].T, g_w_in, m_w_in[0].T, v_w_in[0].T, "adamw_w_in")
    res["grad"]["w_in"] = g_w_in.T[None]
    res["delta"]["w_in"], res["new_m"]["w_in"], res["new_v"]["w_in"] = d.T[None], nm.T[None], nv.T[None]
    order = ["norm_w", "w_in", "conv_w", "a_log", "dt_bias", "dn_norm_w", "w_o_dn", "w_o_dil", "w_out", "final_norm_w"]
    outs = [loss, grad_x[None]]
    for kind in ("grad", "delta", "new_m", "new_v"):
        outs += [res[kind][nm] for nm in order]
    return tuple(outs)
```

```python
import functools
import math

import jax
import jax.numpy as jnp
from jax import lax
from jax.experimental import pallas as pl
from jax.experimental.pallas import tpu as pltpu

F32 = jnp.float32
BF16 = jnp.bfloat16
MESH = pl.DeviceIdType.MESH

D_MODEL = 1024
DN_HEADS = 8
DN_D = 128
DN_CHUNK = 64
DN_W = DN_HEADS * DN_D
DIL_GROUPS = ((128, 1), (512, 4), (2048, 16))
N_DIL = len(DIL_GROUPS)
DIL_HEADS = 4
DIL_DH = 128
DIL_W = DIL_HEADS * DIL_DH
ATT_BLOCK = 128
NORM_EPS = 1e-6
PROJ_W = 11280
N_CHIPS = 4
SHARD_W = PROJ_W // N_CHIPS

OFF_QKV_A = 0
OFF_Z_A = 3072
OFF_Q_B = 4096
OFF_K_B = 5632
OFF_V_B = 7168
OFF_Z_B = 8704
OFF_G_A = 9216
OFF_G_B = 10240
OFF_BA = 11264
PW = 11520
REF_OFF_BA = 4096

ADAM_LR = 0.001
ADAM_B1 = 0.9
ADAM_B2 = 0.999
ADAM_EPS = 1e-08
ADAM_WD = 0.01
ADAM_STEP = 10

ROW_TILE = 512
CONV_TILE = 1024
NEG = -1e30


def _dot(a, b):
    return jnp.dot(a.astype(BF16), b.astype(BF16), preferred_element_type=F32)


def _dot_nt(a, b):
    return lax.dot_general(a.astype(BF16), b.astype(BF16), (((1,), (1,)), ((), ())), preferred_element_type=F32)


def _dot_tn(a, b):
    return lax.dot_general(a.astype(BF16), b.astype(BF16), (((0,), (0,)), ((), ())), preferred_element_type=F32)


def _split(a):
    hi = a.astype(BF16)
    lo = (a - hi.astype(F32)).astype(BF16)
    return hi, lo


def _dot_exact_lhs(c, a):
    hi, lo = _split(a)
    cb = c.astype(BF16)
    return jnp.dot(cb, hi, preferred_element_type=F32) + jnp.dot(cb, lo, preferred_element_type=F32)


def _dot_tn_exact_rhs(a, c):
    hi, lo = _split(a)
    cb = c.astype(BF16)
    dn = (((0,), (0,)), ((), ()))
    return (lax.dot_general(hi, cb, dn, preferred_element_type=F32)
            + lax.dot_general(lo, cb, dn, preferred_element_type=F32))


def _sigmoid(x):
    return 1.0 / (1.0 + jnp.exp(-x))


def _silu(x):
    return x * _sigmoid(x)


def _silu_grad(x):
    s = _sigmoid(x)
    return s * (1.0 + x * (1.0 - s))


def _softplus(x):
    return jnp.maximum(x, 0.0) + jnp.log(1.0 + jnp.exp(-jnp.abs(x)))


def _cparams(*sem):
    return pltpu.CompilerParams(dimension_semantics=sem)


def _matmul(a, b, out_dtype, tm, tn, tk, name, nt=False, transpose_out=False, after=None, also_bf16=False):
    m, kdim = a.shape
    n = b.shape[0] if nt else b.shape[1]
    tm, tn, tk = min(tm, m), min(tn, n), min(tk, kdim)
    assert m % tm == 0 and n % tn == 0 and kdim % tk == 0, (name, a.shape, b.shape, tm, tn, tk)
    nk = kdim // tk
    dot = _dot_nt if nt else _dot
    b_spec = (pl.BlockSpec((tn, tk), lambda i, j, k: (j, k)) if nt else pl.BlockSpec((tk, tn), lambda i, j, k: (k, j)))
    extra = [] if after is None else [after]
    out_dtypes = [out_dtype] + ([BF16] if also_bf16 else [])

    def emit(o_refs, acc):
        val = acc.T if transpose_out else acc
        for o_ref in o_refs:
            o_ref[...] = val.astype(o_ref.dtype)

    def outs_of(rest):
        return rest[len(extra):len(extra) + len(out_dtypes)]

    if nk == 1:
        def body(a_ref, b_ref, *rest):
            emit(outs_of(rest), dot(a_ref[...], b_ref[...]))
        scratch = []
    else:
        def body(a_ref, b_ref, *rest):
            o_ref, acc_ref = outs_of(rest), rest[-1]
            k = pl.program_id(2)
            p = dot(a_ref[...], b_ref[...])

            @pl.when(k == 0)
            def _():
                acc_ref[...] = p

            @pl.when(k > 0)
            def _():
                acc_ref[...] += p

            @pl.when(k == nk - 1)
            def _():
                emit(o_ref, acc_ref[...])
        scratch = [pltpu.VMEM((tm, tn), F32)]

    if transpose_out:
        out_spec, out_shape = pl.BlockSpec((tn, tm), lambda i, j, k: (j, i)), (n, m)
    else:
        out_spec, out_shape = pl.BlockSpec((tm, tn), lambda i, j, k: (i, j)), (m, n)
    res = pl.pallas_call(
        body, name=name, grid=(m // tm, n // tn, nk),
        in_specs=[pl.BlockSpec((tm, tk), lambda i, j, k: (i, k)), b_spec] + [pl.BlockSpec(memory_space=pl.ANY)] * len(extra),
        out_specs=[out_spec] * len(out_dtypes), out_shape=[jax.ShapeDtypeStruct(out_shape, d) for d in out_dtypes],
        scratch_shapes=scratch, compiler_params=_cparams("parallel", "parallel", "arbitrary"))(a, b, *extra)
    return res if also_bf16 else res[0]


def _rms_in(x, nw):
    s, d = x.shape

    def body(x_ref, w_ref, h_ref, ht_ref):
        xv = x_ref[...]
        r = lax.rsqrt(jnp.mean(xv * xv, axis=-1, keepdims=True) + NORM_EPS)
        h = xv * r * w_ref[...]
        h_ref[...] = h.astype(BF16)
        ht_ref[...] = h.T.astype(BF16)

    return pl.pallas_call(
        body, name="rms_in", grid=(s // ROW_TILE,),
        in_specs=[pl.BlockSpec((ROW_TILE, d), lambda i: (i, 0)), pl.BlockSpec((1, d), lambda i: (0, 0))],
        out_specs=[pl.BlockSpec((ROW_TILE, d), lambda i: (i, 0)), pl.BlockSpec((d, ROW_TILE), lambda i: (0, i))],
        out_shape=[jax.ShapeDtypeStruct((s, d), BF16), jax.ShapeDtypeStruct((d, s), BF16)],
        compiler_params=_cparams("parallel"))(x, nw)


def _rms_in_bwd(x, nw, dh, dx2):
    s, d = x.shape

    def body(x_ref, w_ref, dh_ref, dx2_ref, dx_ref, dw_ref):
        i = pl.program_id(0)
        xv = x_ref[...]
        r = lax.rsqrt(jnp.mean(xv * xv, axis=-1, keepdims=True) + NORM_EPS)
        dhv = dh_ref[...]
        dyw = dhv * w_ref[...]
        dx_ref[...] = dx2_ref[...] + r * dyw - xv * (r * r * r) * jnp.mean(dyw * xv, axis=-1, keepdims=True)
        part = jnp.sum(dhv * xv * r, axis=0, keepdims=True)

        @pl.when(i == 0)
        def _():
            dw_ref[...] = part

        @pl.when(i > 0)
        def _():
            dw_ref[...] += part

    row = pl.BlockSpec((ROW_TILE, d), lambda i: (i, 0))
    vec = pl.BlockSpec((1, d), lambda i: (0, 0))
    return pl.pallas_call(
        body, name="rms_in_bwd", grid=(s // ROW_TILE,), in_specs=[row, vec, row, row], out_specs=[row, vec],
        out_shape=[jax.ShapeDtypeStruct((s, d), F32), jax.ShapeDtypeStruct((1, d), F32)],
        compiler_params=_cparams("arbitrary"))(x, nw, dh, dx2)


def _shift_down(cur, prev8, k):
    rc = pltpu.roll(cur, k, 0)
    rp = pltpu.roll(prev8, k, 0)
    row = lax.broadcasted_iota(jnp.int32, prev8.shape, 0)
    top = jnp.where(row < k, rp, rc[:8])
    return jnp.concatenate([top, rc[8:]], axis=0)


def _shift_up(cur, next8, k):
    t = cur.shape[0]
    rc = pltpu.roll(cur, t - k, 0)
    rn = pltpu.roll(next8, 8 - k, 0)
    row = lax.broadcasted_iota(jnp.int32, next8.shape, 0)
    bot = jnp.where(row >= 8 - k, rn, rc[t - 8:])
    return jnp.concatenate([rc[:t - 8], bot], axis=0)


def _conv_fwd(proj, conv_w):
    s = proj.shape[0]
    tile = min(s, CONV_TILE)
    t8 = tile // 8

    def body(u_ref, up_ref, w_ref, c_ref, y_ref):
        i = pl.program_id(0)
        part = pl.program_id(1)
        cur = u_ref[...]
        prev8 = jnp.where(i > 0, up_ref[...], 0.0)
        w = w_ref[...]
        c = cur * w[3:4, :]
        for k in (1, 2, 3):
            c = c + _shift_down(cur, prev8, k) * w[3 - k:4 - k, :]
        c_ref[...] = c
        a = _silu(c)
        for h in range(DN_HEADS):
            ah = a[:, h * DN_D:(h + 1) * DN_D]
            r = lax.rsqrt(jnp.sum(ah * ah, axis=-1, keepdims=True) + NORM_EPS)
            y_ref[:, h * DN_D:(h + 1) * DN_D] = jnp.where(part < 2, ah * r, ah)

    return pl.pallas_call(
        body, name="conv_fwd", grid=(s // tile, 3),
        in_specs=[pl.BlockSpec((tile, DN_W), lambda i, p: (i, p)),
                  pl.BlockSpec((8, DN_W), lambda i, p: (jnp.maximum(i * t8 - 1, 0), p)),
                  pl.BlockSpec((4, DN_W), lambda i, p: (0, p))],
        out_specs=[pl.BlockSpec((tile, DN_W), lambda i, p: (i, p))] * 2,
        out_shape=[jax.ShapeDtypeStruct((s, 3 * DN_W), F32)] * 2,
        compiler_params=_cparams("parallel", "parallel"))(proj, proj, conv_w)


def _conv_bwd_act(c, dq, dk, dv):
    s = c.shape[0]

    def body(c_ref, dq_ref, dk_ref, dv_ref, dc_ref):
        for part, d_ref in enumerate((dq_ref, dk_ref, dv_ref)):
            for h in range(DN_HEADS):
                sl = slice(part * DN_W + h * DN_D, part * DN_W + (h + 1) * DN_D)
                ch = c_ref[:, sl]
                dyh = d_ref[:, h * DN_D:(h + 1) * DN_D]
                if part < 2:
                    ah = _silu(ch)
                    r = lax.rsqrt(jnp.sum(ah * ah, axis=-1, keepdims=True) + NORM_EPS)
                    dyh = r * dyh - ah * (r * r * r) * jnp.sum(dyh * ah, axis=-1, keepdims=True)
                dc_ref[:, sl] = dyh * _silu_grad(ch)

    wide = pl.BlockSpec((ROW_TILE, 3 * DN_W), lambda i: (i, 0))
    row = pl.BlockSpec((ROW_TILE, DN_W), lambda i: (i, 0))
    return pl.pallas_call(
        body, name="conv_bwd_act", grid=(s // ROW_TILE,), in_specs=[wide, row, row, row], out_specs=wide,
        out_shape=jax.ShapeDtypeStruct((s, 3 * DN_W), F32), compiler_params=_cparams("parallel"))(c, dq, dk, dv)


DPROJ_IN = pl.BlockSpec(memory_space=pl.ANY)


def _conv_bwd(proj, dc, conv_w, dproj):
    s = proj.shape[0]
    tile = min(s, CONV_TILE)
    t8 = tile // 8
    nrow = s // tile
    last8 = s // 8 - 1

    def body(u_ref, dc_ref, dcn_ref, w_ref, dproj_in, du_ref, dw_ref):
        i = pl.program_id(1)
        cur = u_ref[...]
        dcv = dc_ref[...]
        next8 = jnp.where(i < nrow - 1, dcn_ref[...], 0.0)
        w = w_ref[...]

        @pl.when(i == 0)
        def _():
            dw_ref[...] = jnp.zeros_like(dw_ref)

        du = dcv * w[3:4, :]
        dw_ref[3:4, :] += jnp.sum(cur * dcv, axis=0, keepdims=True)
        for k in (1, 2, 3):
            ahead = _shift_up(dcv, next8, k)
            du = du + ahead * w[3 - k:4 - k, :]
            dw_ref[3 - k:4 - k, :] += jnp.sum(cur * ahead, axis=0, keepdims=True)
        du_ref[...] = du.astype(BF16)

    blk = pl.BlockSpec((tile, DN_W), lambda p, i: (i, p))
    return pl.pallas_call(
        body, name="conv_bwd", grid=(3, nrow),
        in_specs=[blk, blk, pl.BlockSpec((8, DN_W), lambda p, i: (jnp.minimum((i + 1) * t8, last8), p)),
                  pl.BlockSpec((4, DN_W), lambda p, i: (0, p)), DPROJ_IN],
        out_specs=[blk, pl.BlockSpec((4, DN_W), lambda p, i: (0, p))],
        out_shape=[jax.ShapeDtypeStruct((s, PW), BF16), jax.ShapeDtypeStruct((4, 3 * DN_W), F32)],
        input_output_aliases={4: 0},
        compiler_params=_cparams("parallel", "arbitrary"))(proj, dc, dc, conv_w, dproj)


def _gates_fwd(proj, gate_par):
    s = proj.shape[0]

    def body(ba_ref, par_ref, o_ref):
        v = ba_ref[...]
        lane = lax.broadcasted_iota(jnp.int32, v.shape, 1)
        beta = _sigmoid(v)
        g = -jnp.exp(par_ref[0:1, :]) * _softplus(v + par_ref[1:2, :])
        o_ref[...] = jnp.where(lane < DN_HEADS, beta, jnp.where(lane < 2 * DN_HEADS, g, 0.0))

    return pl.pallas_call(
        body, name="gates_fwd", grid=(s // ROW_TILE,),
        in_specs=[pl.BlockSpec((ROW_TILE, 128), lambda i: (i, OFF_BA // 128)), pl.BlockSpec((8, 128), lambda i: (0, 0))],
        out_specs=pl.BlockSpec((ROW_TILE, 128), lambda i: (i, 0)),
        out_shape=jax.ShapeDtypeStruct((s, 128), F32), compiler_params=_cparams("parallel"))(proj, gate_par)


def _gates_bwd(proj, gate_par, dbg, dproj):
    s = proj.shape[0]

    def body(ba_ref, par_ref, d_ref, dproj_in, o_ref, dpar_ref):
        i = pl.program_id(0)
        v = ba_ref[...]
        dv = d_ref[...]
        lane = lax.broadcasted_iota(jnp.int32, v.shape, 1)
        beta = _sigmoid(v)
        nega = -jnp.exp(par_ref[0:1, :])
        xs = v + par_ref[1:2, :]
        dsp = dv * nega * _sigmoid(xs)
        dal = dv * nega * _softplus(xs)
        is_b = lane < DN_HEADS
        is_g = jnp.logical_and(lane >= DN_HEADS, lane < 2 * DN_HEADS)
        o_ref[:, :128] = jnp.where(is_b, dv * beta * (1.0 - beta), jnp.where(is_g, dsp, 0.0)).astype(BF16)
        o_ref[:, 128:] = jnp.zeros((ROW_TILE, PW - OFF_BA - 128), BF16)
        r0 = jnp.sum(jnp.where(is_g, dal, 0.0), axis=0, keepdims=True)
        r1 = jnp.sum(jnp.where(is_g, dsp, 0.0), axis=0, keepdims=True)

        @pl.when(i == 0)
        def _():
            dpar_ref[...] = jnp.zeros_like(dpar_ref)

        dpar_ref[0:1, :] += r0
        dpar_ref[1:2, :] += r1

    return pl.pallas_call(
        body, name="gates_bwd", grid=(s // ROW_TILE,),
        in_specs=[pl.BlockSpec((ROW_TILE, 128), lambda i: (i, OFF_BA // 128)), pl.BlockSpec((8, 128), lambda i: (0, 0)),
                  pl.BlockSpec((ROW_TILE, 128), lambda i: (i, 0)), DPROJ_IN],
        out_specs=[pl.BlockSpec((ROW_TILE, PW - OFF_BA), lambda i: (i, OFF_BA // (PW - OFF_BA))),
                   pl.BlockSpec((8, 128), lambda i: (0, 0))],
        out_shape=[jax.ShapeDtypeStruct((s, PW), BF16), jax.ShapeDtypeStruct((8, 128), F32)],
        input_output_aliases={3: 0},
        compiler_params=_cparams("arbitrary"))(proj, gate_par, dbg, dproj)


def _chunk_masks():
    c = DN_CHUNK
    ii = lax.broadcasted_iota(jnp.int32, (c, c), 0)
    jj = lax.broadcasted_iota(jnp.int32, (c, c), 1)
    return dict(ii=ii, jj=jj, lower=(ii >= jj), strict=(ii > jj),
                lower_f=(ii >= jj).astype(BF16), upper_f=(ii <= jj).astype(BF16))


class _Heads:
    def __init__(self, xs):
        self.xs = list(xs)

    def _bin(self, o, f):
        if isinstance(o, _Heads):
            return _Heads([f(a, b) for a, b in zip(self.xs, o.xs)])
        return _Heads([f(a, o) for a in self.xs])

    def __add__(self, o):
        return self._bin(o, lambda a, b: a + b)

    def __sub__(self, o):
        return self._bin(o, lambda a, b: a - b)

    def __mul__(self, o):
        return self._bin(o, lambda a, b: a * b)

    __radd__ = __add__
    __rmul__ = __mul__

    def __neg__(self):
        return _Heads([-a for a in self.xs])

    def __getitem__(self, i):
        return _Heads([a[i] for a in self.xs])


def _hmap(f, *args):
    n = next(len(a.xs) for a in args if isinstance(a, _Heads))
    return _Heads([f(*[(a.xs[h] if isinstance(a, _Heads) else a) for a in args]) for h in range(n)])


def _hdot(a, b):
    return _hmap(_dot, a, b)


def _hdot_nt(a, b):
    return _hmap(_dot_nt, a, b)


def _hdot_tn(a, b):
    return _hmap(_dot_tn, a, b)


def _hcat(a, b, axis):
    return _hmap(lambda x, y: jnp.concatenate([x, y], axis=axis), a, b)


def _hsum(a, axis):
    return _hmap(lambda t: jnp.sum(t, axis=axis, keepdims=True), a)


def _hwhere(c, a, b):
    return _hmap(jnp.where, c, a, b)


def _chunk_gates(mk, bg):
    c = DN_CHUNK
    gc_all = _dot_exact_lhs(mk["lower_f"], bg)
    rows = jnp.concatenate([gc_all, gc_all], axis=0).T
    hs = range(DN_HEADS)
    return (_Heads(bg[:, h:h + 1] for h in hs), _Heads(gc_all[:, DN_HEADS + h:DN_HEADS + h + 1] for h in hs),
            _Heads(rows[DN_HEADS + h:DN_HEADS + h + 1, :] for h in hs))


def _chunk_common(mk, q, k, beta_col, gc_col, gc_r):
    c = DN_CHUNK
    lower, strict = mk["lower"], mk["strict"]
    qs = q * (DN_D ** -0.5)
    beta_b = _hmap(lambda t: jnp.broadcast_to(t, (c, DN_D)), beta_col)
    gc_b = _hmap(lambda t: jnp.broadcast_to(t, (c, DN_D)), gc_col)
    gc_sq = gc_b[:, :c]
    gam = _hwhere(lower, _hmap(lambda t: jnp.exp(jnp.minimum(t, 0.0)), gc_sq - gc_r[:, :c]), 0.0)
    egc = _hmap(jnp.exp, gc_b)
    gl = gc_b[c - 1:c, :]
    ekd = _hmap(jnp.exp, gl - gc_b)
    dl = _hmap(jnp.exp, gl)
    kb = k * beta_b
    scores = _hdot_nt(_hcat(kb, qs, 0), k)
    a_strict = _hwhere(strict, scores[:c] * gam, 0.0)
    aqk = _hwhere(lower, scores[c:] * gam, 0.0)
    return dict(k=k, qs=qs, beta_b=beta_b, gc_b=gc_b, gam=gam, egc=egc, ekd=ekd, dl=dl, kb=kb, a_strict=a_strict, aqk=aqk)


def _unit_lower_inverse_minus_eye(n_strict, ii, jj):
    same = lax.shift_right_logical(ii, 4) == lax.shift_right_logical(jj, 4)
    dmat = _hwhere(same, n_strict, 0.0)
    omat = n_strict - dmat
    d2 = _hdot(dmat, dmat)
    d4 = _hdot(d2, d2)
    d8 = _hdot(d4, d4)
    x1 = d2 - dmat - _hdot(dmat, d2)
    x2 = x1 + d4 + _hdot(x1, d4)
    x3 = x2 + d8 + _hdot(x2, d8)
    n1 = omat + _hdot(x3, omat)
    n2 = _hdot(n1, n1)
    y = n2 - n1 - _hdot(n1, n2)
    return y + x3 + _hdot(y, x3)


GDR_HEAD_SETS = (range(0, DN_HEADS),)


def _gdr_fwd(qkv, bg):
    s = qkv.shape[0]
    c = DN_CHUNK
    n = s // c

    def body(q_ref, k_ref, v_ref, bg_ref, o_ref, u_ref, w_ref, vn_ref, tm_ref, st_ref, state):
        @pl.when(pl.program_id(0) == 0)
        def _():
            state[...] = jnp.zeros_like(state)

        mk = _chunk_masks()
        gates = _chunk_gates(mk, bg_ref[...])
        for hs in GDR_HEAD_SETS:
            sls = [slice(h * DN_D, (h + 1) * DN_D) for h in hs]
            cm = _chunk_common(mk, _Heads(q_ref[:, sl] for sl in sls), _Heads(k_ref[:, sl] for sl in sls),
                               *[_Heads(g.xs[h] for h in hs) for g in gates])
            tm = _unit_lower_inverse_minus_eye(cm["a_strict"], mk["ii"], mk["jj"])
            rhs_u = _Heads(v_ref[:, sl] for sl in sls) * cm["beta_b"]
            rhs_w = cm["kb"] * cm["egc"]
            t_rhs = _hdot(tm, _hcat(rhs_u, rhs_w, 1))
            u = rhs_u + t_rhs[:, :DN_D]
            w = rhs_w + t_rhs[:, DN_D:]
            st = _Heads(state[h] for h in hs)
            on_state = _hdot(_hcat(w, cm["qs"] * cm["egc"], 0), st)
            v_new = u - on_state[:c]
            o = on_state[c:] + _hdot(cm["aqk"], v_new)
            st_new = st * cm["dl"] + _hdot_tn(cm["k"] * cm["ekd"], v_new)
            for i, (h, sl) in enumerate(zip(hs, sls)):
                o_ref[:, sl] = o.xs[i]
                u_ref[:, sl] = u.xs[i]
                w_ref[:, sl] = w.xs[i]
                vn_ref[:, sl] = v_new.xs[i]
                tm_ref[h, 0] = tm.xs[i]
                st_ref[h, 0] = st.xs[i]
                state[h] = st_new.xs[i]

    def part(p):
        return pl.BlockSpec((c, DN_W), lambda j: (j, p))

    return pl.pallas_call(
        body, name="gdr_fwd", grid=(n,),
        in_specs=[part(0), part(1), part(2), pl.BlockSpec((c, 128), lambda j: (j, 0))],
        out_specs=[part(0)] * 4 + [pl.BlockSpec((DN_HEADS, 1, c, c), lambda j: (0, j, 0, 0)),
                                   pl.BlockSpec((DN_HEADS, 1, DN_D, DN_D), lambda j: (0, j, 0, 0))],
        out_shape=[jax.ShapeDtypeStruct((s, DN_W), F32)] * 4
        + [jax.ShapeDtypeStruct((DN_HEADS, n, c, c), F32), jax.ShapeDtypeStruct((DN_HEADS, n, DN_D, DN_D), F32)],
        scratch_shapes=[pltpu.VMEM((DN_HEADS, DN_D, DN_D), F32)],
        compiler_params=_cparams("arbitrary"))(qkv, qkv, qkv, bg)


def _gdr_bwd(qkv, bg, u, w, vn, tmat, states, do):
    s = qkv.shape[0]
    c = DN_CHUNK
    n = s // c

    def body(q_ref, k_ref, v_ref, bg_ref, u_ref, w_ref, vn_ref, tm_ref, st_ref, do_ref,
             dq_ref, dk_ref, dv_ref, dbg_ref, dstate):
        @pl.when(pl.program_id(0) == 0)
        def _():
            dstate[...] = jnp.zeros_like(dstate)

        mk = _chunk_masks()
        lower, strict = mk["lower"], mk["strict"]
        bg = bg_ref[...]
        ones = jnp.ones((c, DN_D), BF16)
        rowi = lax.broadcasted_iota(jnp.int32, (c, DN_D), 0)
        lane = lax.broadcasted_iota(jnp.int32, (c, 128), 1)
        hs = range(DN_HEADS)
        sls = [slice(h * DN_D, (h + 1) * DN_D) for h in hs]

        def heads_of(ref):
            return _Heads(ref[:, sl] for sl in sls)

        cm = _chunk_common(mk, heads_of(q_ref), heads_of(k_ref), *_chunk_gates(mk, bg))
        k, qs, beta_b = cm["k"], cm["qs"], cm["beta_b"]
        gam, egc, ekd, dl, kb = cm["gam"], cm["egc"], cm["ekd"], cm["dl"], cm["kb"]
        aqk, a_strict = cm["aqk"], cm["a_strict"]
        v, uu, ww, v_new, dov = heads_of(v_ref), heads_of(u_ref), heads_of(w_ref), heads_of(vn_ref), heads_of(do_ref)
        st = _Heads(st_ref[h, 0] for h in hs)
        dsn = _Heads(dstate[h] for h in hs)
        qd = qs * egc
        kd = k * ekd

        dv_new = _hdot_tn(aqk, dov) + _hdot(kd, dsn)
        do_sv = _hdot_nt(dov, _hcat(st, v_new, 0))
        dqd = do_sv[:, :DN_D]
        daqk = _hwhere(lower, do_sv[:, DN_D:], 0.0)
        dkd = _hdot_nt(v_new, dsn)
        ddl = _hsum(_hsum(dsn * st, 1), 0)
        dw = -_hdot_nt(dv_new, st)
        ds_new = dsn * dl + _hdot_tn(_hcat(qd, -ww, 0), _hcat(dov, dv_new, 0))

        tm = _Heads(tm_ref[h, 0] for h in hs)
        tt = _hdot_tn(tm, _hcat(dv_new, dw, 1))
        dru = dv_new + tt[:, :DN_D]
        drw = dw + tt[:, DN_D:]
        dn = _hwhere(strict, -_hdot_nt(_hcat(dru, drw, 1), _hcat(uu, ww, 1)), 0.0)
        dag = dn * gam
        dqg = daqk * gam
        both = _hcat(dag, dqg, 0)
        on_k = _hdot(both, k)
        dkb = on_k[:c] + drw * egc
        dqs = on_k[c:] + dqd * egc
        dk = _hdot_tn(both, _hcat(kb, qs, 0)) + dkb * beta_b + dkd * ekd
        pmat = dn * a_strict + daqk * aqk
        tkd = _hsum(dkd * kd, -1)
        dgc = (_hsum(pmat, -1) - _hmap(_dot_tn_exact_rhs, pmat, ones) + _hsum(drw * (kb * egc), -1)
               + _hsum(dqd * qd, -1) - tkd)
        last = _hsum(tkd, 0) + ddl * dl
        dgc = dgc + _hwhere(rowi == c - 1, last, 0.0)
        dbeta = _hsum(dru * v, -1) + _hsum(dkb * k, -1)
        dq = dqs * (DN_D ** -0.5)
        dv = dru * beta_b

        dgc_all = jnp.zeros((c, 128), F32)
        dbg = jnp.zeros((c, 128), F32)
        for h, sl in zip(hs, sls):
            dq_ref[:, sl] = dq.xs[h]
            dk_ref[:, sl] = dk.xs[h]
            dv_ref[:, sl] = dv.xs[h]
            dstate[h] = ds_new.xs[h]
            dgc_all = dgc_all + jnp.where(lane == DN_HEADS + h, dgc.xs[h], 0.0)
            dbg = dbg + jnp.where(lane == h, dbeta.xs[h], 0.0)
        dbg_ref[...] = dbg + _dot_exact_lhs(mk["upper_f"], dgc_all)

    def part(p):
        return pl.BlockSpec((c, DN_W), lambda j: (n - 1 - j, p))

    vec = pl.BlockSpec((c, 128), lambda j: (n - 1 - j, 0))
    return pl.pallas_call(
        body, name="gdr_bwd", grid=(n,),
        in_specs=[part(0), part(1), part(2), vec, part(0), part(0), part(0),
                  pl.BlockSpec((DN_HEADS, 1, c, c), lambda j: (0, n - 1 - j, 0, 0)),
                  pl.BlockSpec((DN_HEADS, 1, DN_D, DN_D), lambda j: (0, n - 1 - j, 0, 0)), part(0)],
        out_specs=[part(0), part(0), part(0), vec],
        out_shape=[jax.ShapeDtypeStruct((s, DN_W), F32)] * 3 + [jax.ShapeDtypeStruct((s, 128), F32)],
        scratch_shapes=[pltpu.VMEM((DN_HEADS, DN_D, DN_D), F32)],
        compiler_params=_cparams("arbitrary"))(qkv, qkv, qkv, bg, u, w, vn, tmat, states, do)


def _gdr_out(o, proj, dnw):
    s = o.shape[0]

    def body(o_ref, z_ref, w_ref, y_ref, yt_ref):
        ov, zv, wv = o_ref[...], z_ref[...], w_ref[...]
        for h in range(DN_HEADS):
            sl = slice(h * DN_D, (h + 1) * DN_D)
            oh = ov[:, sl]
            r = lax.rsqrt(jnp.mean(oh * oh, axis=-1, keepdims=True) + NORM_EPS)
            y = (oh * r * wv) * _silu(zv[:, sl])
            y_ref[:, sl] = y.astype(BF16)
            yt_ref[sl, :] = y.T.astype(BF16)

    row = pl.BlockSpec((ROW_TILE, DN_W), lambda i: (i, 0))
    return pl.pallas_call(
        body, name="gdr_out", grid=(s // ROW_TILE,),
        in_specs=[row, pl.BlockSpec((ROW_TILE, DN_W), lambda i: (i, OFF_Z_A // DN_W)), pl.BlockSpec((1, DN_D), lambda i: (0, 0))],
        out_specs=[row, pl.BlockSpec((DN_W, ROW_TILE), lambda i: (0, i))],
        out_shape=[jax.ShapeDtypeStruct((s, DN_W), BF16), jax.ShapeDtypeStruct((DN_W, s), BF16)],
        compiler_params=_cparams("parallel"))(o, proj, dnw)


def _gdr_out_bwd(o, proj, dnw, dy):
    s = o.shape[0]

    def body(o_ref, z_ref, w_ref, dy_ref, do_ref, dz_ref, dw_ref):
        i = pl.program_id(0)
        ov, zv, wv, dyv = o_ref[...], z_ref[...], w_ref[...], dy_ref[...]
        acc = jnp.zeros((1, DN_D), F32)
        for h in range(DN_HEADS):
            sl = slice(h * DN_D, (h + 1) * DN_D)
            oh, zh, dh = ov[:, sl], zv[:, sl], dyv[:, sl]
            r = lax.rsqrt(jnp.mean(oh * oh, axis=-1, keepdims=True) + NORM_EPS)
            dn = dh * _silu(zh)
            dz_ref[:, sl] = (dh * (oh * r * wv) * _silu_grad(zh)).astype(BF16)
            acc = acc + jnp.sum(dn * oh * r, axis=0, keepdims=True)
            dnw_ = dn * wv
            do_ref[:, sl] = r * dnw_ - oh * (r * r * r) * jnp.mean(dnw_ * oh, axis=-1, keepdims=True)

        @pl.when(i == 0)
        def _():
            dw_ref[...] = acc

        @pl.when(i > 0)
        def _():
            dw_ref[...] += acc

    row = pl.BlockSpec((ROW_TILE, DN_W), lambda i: (i, 0))
    vec = pl.BlockSpec((1, DN_D), lambda i: (0, 0))
    return pl.pallas_call(
        body, name="gdr_out_bwd", grid=(s // ROW_TILE,),
        in_specs=[row, pl.BlockSpec((ROW_TILE, DN_W), lambda i: (i, OFF_Z_A // DN_W)), vec, row],
        out_specs=[row, pl.BlockSpec((ROW_TILE, DN_W), lambda i: (i, OFF_Z_A // DN_W)), vec],
        out_shape=[jax.ShapeDtypeStruct((s, DN_W), F32), jax.ShapeDtypeStruct((s, PW), BF16),
                   jax.ShapeDtypeStruct((1, DN_D), F32)],
        compiler_params=_cparams("arbitrary"))(o, proj, dnw, dy)


def _slope(group, head):
    idx = (group * DIL_HEADS + head + 1).astype(F32)
    return jnp.exp(jnp.full((1, 128), -8.0 * math.log(2.0) / (N_DIL * DIL_HEADS), F32) * idx)


def _att_scores(qb, k_cur, k_prev, slope_d, has_prev):
    iq = lax.broadcasted_iota(jnp.int32, (ATT_BLOCK, ATT_BLOCK), 0)
    jk = lax.broadcasted_iota(jnp.int32, (ATT_BLOCK, ATT_BLOCK), 1)
    dist_c = (iq - jk).astype(F32)
    s_cur = jnp.where(iq >= jk, _dot_nt(qb, k_cur) - slope_d * dist_c, NEG)
    s_prev = jnp.where(jnp.logical_and(jk >= iq, has_prev),
                       _dot_nt(qb, k_prev) - slope_d * (dist_c + float(ATT_BLOCK)), NEG)
    return s_cur, s_prev


def _att_scores_whole(qb, k, slope_d):
    n = 2 * ATT_BLOCK
    dist = lax.broadcasted_iota(jnp.int32, (n, n), 0) - lax.broadcasted_iota(jnp.int32, (n, n), 1)
    valid = jnp.logical_and(dist >= 0, dist <= ATT_BLOCK)
    return jnp.where(valid, _dot_nt(qb, k) - slope_d[:, 0:1] * dist.astype(F32), NEG)


def _att_tiles(i, dil, nb):
    tiles = nb // 2
    per = dil * tiles // ATT_UNROLL
    assert nb % 2 == 0 and tiles >= 2 and per * ATT_UNROLL == dil * tiles
    for i0 in range(per):
        ts = [divmod(i0 + u * per, tiles) for u in range(ATT_UNROLL)]
        assert all(a[0] != b[0] or abs(a[1] - b[1]) >= 2 for n, a in enumerate(ts) for b in ts[n + 1:])
    qrows, krows, has_prev = [], [], []
    for u in range(ATT_UNROLL):
        t = i + u * per
        r = lax.div(t, tiles)
        j = lax.rem(t, tiles)
        qbase = r + dil * 2 * ATT_BLOCK * j
        kbase = qbase - dil * ATT_BLOCK * jnp.minimum(j, 1)
        if dil == 1:
            qbase, kbase = pl.multiple_of(qbase, ATT_BLOCK), pl.multiple_of(kbase, ATT_BLOCK)
        qrows.append(pl.ds(qbase, 2 * ATT_BLOCK, stride=dil))
        krows.append(pl.ds(kbase, 3 * ATT_BLOCK, stride=dil))
        has_prev.append(j > 0)
    return qrows, krows, has_prev


def _att_scores_tile(qb, k, slope_d, has_prev):
    iq = lax.broadcasted_iota(jnp.int32, (2 * ATT_BLOCK, 3 * ATT_BLOCK), 0)
    ck = lax.broadcasted_iota(jnp.int32, (2 * ATT_BLOCK, 3 * ATT_BLOCK), 1)
    dist = iq - ck + jnp.where(has_prev, ATT_BLOCK, 0)
    valid = jnp.logical_and(dist >= 0, dist <= ATT_BLOCK)
    return jnp.where(valid, _dot_nt(qb, k) - slope_d[:, 0:1] * dist.astype(F32), NEG)


ATT_UNROLL = 4


def _att_blocks(i, dil, nb):
    per = dil * nb // ATT_UNROLL
    assert per * ATT_UNROLL == dil * nb
    for i0 in range(per):
        blocks = [divmod(i0 + u * per, nb) for u in range(ATT_UNROLL)]
        assert all(a[0] != b[0] or abs(a[1] - b[1]) >= 2 for n, a in enumerate(blocks) for b in blocks[n + 1:])
    curs, prvs, has_prev = [], [], []
    for u in range(ATT_UNROLL):
        t = i + u * per
        r = lax.div(t, nb)
        j = lax.rem(t, nb)
        base = r + dil * ATT_BLOCK * j
        pbase = base - dil * ATT_BLOCK * jnp.minimum(j, 1)
        if dil == 1:
            base, pbase = pl.multiple_of(base, ATT_BLOCK), pl.multiple_of(pbase, ATT_BLOCK)
        curs.append(pl.ds(base, ATT_BLOCK, stride=dil))
        prvs.append(pl.ds(pbase, ATT_BLOCK, stride=dil))
        has_prev.append(j > 0)
    return curs, prvs, has_prev


def _att_fwd(proj, group):
    s = proj.shape[0]
    dil = DIL_GROUPS[group][1]
    assert DIL_GROUPS[group][0] // dil == ATT_BLOCK
    nb = s // dil // ATT_BLOCK
    assert nb * dil * ATT_BLOCK == s

    def body(q_ref, k_ref, v_ref, o_ref, lse_ref):
        def emit(rows, num, den, mx):
            o_ref[rows, :] = num / den
            lse_ref[rows, :] = jnp.broadcast_to(mx + jnp.log(den), (num.shape[0], DIL_DH))

        slope_d = _slope(group, pl.program_id(0)) * float(dil)

        def step(i, carry):
            curs, prvs, has_prev = _att_blocks(i, dil, nb)
            us = range(ATT_UNROLL)
            qb = [q_ref[c, :] * (DIL_DH ** -0.5) for c in curs]
            sc = [_att_scores(qb[u], k_ref[curs[u], :], k_ref[prvs[u], :], slope_d, has_prev[u]) for u in us]
            mx = [jnp.maximum(jnp.max(a, axis=-1, keepdims=True), jnp.max(b, axis=-1, keepdims=True)) for a, b in sc]
            p_cur = [jnp.exp(sc[u][0] - mx[u]) for u in us]
            p_prev = [jnp.exp(sc[u][1] - mx[u]) for u in us]
            den = [jnp.sum(p_cur[u], axis=-1, keepdims=True) + jnp.sum(p_prev[u], axis=-1, keepdims=True) for u in us]
            num = [_dot(p_cur[u], v_ref[curs[u], :]) + _dot(p_prev[u], v_ref[prvs[u], :]) for u in us]
            for u in us:
                emit(curs[u], num[u], den[u], mx[u])
            return carry

        def step_whole(i, carry):
            rows = [pl.ds(i * ATT_UNROLL + u, 2 * ATT_BLOCK, stride=dil) for u in range(ATT_UNROLL)]
            sc = [_att_scores_whole(q_ref[r, :] * (DIL_DH ** -0.5), k_ref[r, :], slope_d) for r in rows]
            mx = [jnp.max(a, axis=-1, keepdims=True) for a in sc]
            p = [jnp.exp(a - m) for a, m in zip(sc, mx)]
            num = [_dot(pu, v_ref[r, :]) for pu, r in zip(p, rows)]
            for u, r in enumerate(rows):
                emit(r, num[u], jnp.sum(p[u], axis=-1, keepdims=True), mx[u])
            return carry

        def step_tile(i, carry):
            qrows, krows, has_prev = _att_tiles(i, dil, nb)
            us = range(ATT_UNROLL)
            sc = [_att_scores_tile(q_ref[qrows[u], :] * (DIL_DH ** -0.5), k_ref[krows[u], :], slope_d, has_prev[u]) for u in us]
            mx = [jnp.max(a, axis=-1, keepdims=True) for a in sc]
            p = [jnp.exp(a - m) for a, m in zip(sc, mx)]
            num = [_dot(p[u], v_ref[krows[u], :]) for u in us]
            for u in us:
                emit(qrows[u], num[u], jnp.sum(p[u], axis=-1, keepdims=True), mx[u])
            return carry

        if nb == 2:
            lax.fori_loop(0, dil // ATT_UNROLL, step_whole, 0)
        elif nb % 2 == 0:
            lax.fori_loop(0, dil * nb // 2 // ATT_UNROLL, step_tile, 0)
        else:
            lax.fori_loop(0, dil * nb // ATT_UNROLL, step, 0)

    def col(off):
        return pl.BlockSpec((s, DIL_DH), lambda h: (0, off // DIL_DH + group * DIL_HEADS + h))

    out = pl.BlockSpec((s, DIL_DH), lambda h: (0, h))
    return pl.pallas_call(
        body, name=f"att_fwd{group}", grid=(DIL_HEADS,), in_specs=[col(OFF_Q_B), col(OFF_K_B), col(OFF_V_B)],
        out_specs=[out, out], out_shape=[jax.ShapeDtypeStruct((s, DIL_W), F32)] * 2,
        compiler_params=_cparams("parallel"))(proj, proj, proj)


def _att_bwd(proj, group, do, lse, delta):
    s = proj.shape[0]
    dil = DIL_GROUPS[group][1]
    nb = s // dil // ATT_BLOCK

    def body(q_ref, k_ref, v_ref, do_ref, lse_ref, dl_ref, dq_ref, dk_ref, dv_ref, dq_acc, dk_acc, dv_acc):
        slope_d = _slope(group, pl.program_id(0)) * float(dil)
        dk_acc[...] = jnp.zeros_like(dk_acc)
        dv_acc[...] = jnp.zeros_like(dv_acc)

        def step(i, carry):
            curs, prvs, has_prev = _att_blocks(i, dil, nb)
            us = range(ATT_UNROLL)
            qb = [q_ref[c, :] * (DIL_DH ** -0.5) for c in curs]
            k_cur, k_prev = [k_ref[c, :] for c in curs], [k_ref[p, :] for p in prvs]
            v_cur, v_prev = [v_ref[c, :] for c in curs], [v_ref[p, :] for p in prvs]
            sc = [_att_scores(qb[u], k_cur[u], k_prev[u], slope_d, has_prev[u]) for u in us]
            lse_b, delta_b, dob = [lse_ref[c, :] for c in curs], [dl_ref[c, :] for c in curs], [do_ref[c, :] for c in curs]
            p_cur = [jnp.exp(sc[u][0] - lse_b[u]) for u in us]
            p_prev = [jnp.exp(sc[u][1] - lse_b[u]) for u in us]
            ds_cur = [p_cur[u] * (_dot_nt(dob[u], v_cur[u]) - delta_b[u]) for u in us]
            ds_prev = [p_prev[u] * (_dot_nt(dob[u], v_prev[u]) - delta_b[u]) for u in us]
            dq = [(_dot(ds_cur[u], k_cur[u]) + _dot(ds_prev[u], k_prev[u])) * (DIL_DH ** -0.5) for u in us]
            dk_c = [_dot_tn(ds_cur[u], qb[u]) for u in us]
            dv_c = [_dot_tn(p_cur[u], dob[u]) for u in us]
            dk_p = [_dot_tn(ds_prev[u], qb[u]) for u in us]
            dv_p = [_dot_tn(p_prev[u], dob[u]) for u in us]
            for u in us:
                dq_acc[curs[u], :] = dq[u]
                dk_acc[curs[u], :] += dk_c[u]
                dv_acc[curs[u], :] += dv_c[u]
            for u in us:
                dk_acc[prvs[u], :] += dk_p[u]
                dv_acc[prvs[u], :] += dv_p[u]
            return carry

        def step_whole(i, carry):
            rows = [pl.ds(i * ATT_UNROLL + u, 2 * ATT_BLOCK, stride=dil) for u in range(ATT_UNROLL)]
            qb = [q_ref[r, :] * (DIL_DH ** -0.5) for r in rows]
            kk, vv, dob = [k_ref[r, :] for r in rows], [v_ref[r, :] for r in rows], [do_ref[r, :] for r in rows]
            sc = [_att_scores_whole(qb[u], kk[u], slope_d) for u in range(ATT_UNROLL)]
            p = [jnp.exp(sc[u] - lse_ref[r, :][:, 0:1]) for u, r in enumerate(rows)]
            ds = [p[u] * (_dot_nt(dob[u], vv[u]) - dl_ref[r, :][:, 0:1]) for u, r in enumerate(rows)]
            dq = [_dot(ds[u], kk[u]) * (DIL_DH ** -0.5) for u in range(ATT_UNROLL)]
            dk = [_dot_tn(ds[u], qb[u]) for u in range(ATT_UNROLL)]
            dv = [_dot_tn(p[u], dob[u]) for u in range(ATT_UNROLL)]
            for u, r in enumerate(rows):
                dq_acc[r, :] = dq[u]
                dk_acc[r, :] = dk[u]
                dv_acc[r, :] = dv[u]
            return carry

        def step_tile(i, carry):
            qrows, krows, has_prev = _att_tiles(i, dil, nb)
            us = range(ATT_UNROLL)
            qb = [q_ref[r, :] * (DIL_DH ** -0.5) for r in qrows]
            kk, vv, dob = [k_ref[r, :] for r in krows], [v_ref[r, :] for r in krows], [do_ref[r, :] for r in qrows]
            sc = [_att_scores_tile(qb[u], kk[u], slope_d, has_prev[u]) for u in us]
            p = [jnp.exp(sc[u] - lse_ref[qrows[u], :][:, 0:1]) for u in us]
            ds = [p[u] * (_dot_nt(dob[u], vv[u]) - dl_ref[qrows[u], :][:, 0:1]) for u in us]
            dq = [_dot(ds[u], kk[u]) * (DIL_DH ** -0.5) for u in us]
            dk = [_dot_tn(ds[u], qb[u]) for u in us]
            dv = [_dot_tn(p[u], dob[u]) for u in us]
            for u in us:
                dq_acc[qrows[u], :] = dq[u]
                dk_acc[krows[u], :] += dk[u]
                dv_acc[krows[u], :] += dv[u]
            return carry

        if nb == 2:
            lax.fori_loop(0, dil // ATT_UNROLL, step_whole, 0)
        elif nb % 2 == 0:
            lax.fori_loop(0, dil * nb // 2 // ATT_UNROLL, step_tile, 0)
        else:
            lax.fori_loop(0, dil * nb // ATT_UNROLL, step, 0)
        dq_ref[...] = dq_acc[...].astype(BF16)
        dk_ref[...] = dk_acc[...].astype(BF16)
        dv_ref[...] = dv_acc[...].astype(BF16)

    def col(off):
        return pl.BlockSpec((s, DIL_DH), lambda h: (0, off // DIL_DH + group * DIL_HEADS + h))

    hd = pl.BlockSpec((s, DIL_DH), lambda h: (0, h))
    return pl.pallas_call(
        body, name=f"att_bwd{group}", grid=(DIL_HEADS,),
        in_specs=[col(OFF_Q_B), col(OFF_K_B), col(OFF_V_B), hd, hd, hd], out_specs=[hd, hd, hd],
        out_shape=[jax.ShapeDtypeStruct((s, DIL_W), BF16)] * 3,
        scratch_shapes=[pltpu.VMEM((s, DIL_DH), F32)] * 3,
        compiler_params=_cparams("parallel"))(proj, proj, proj, do, lse, delta)


def _att_merge(parts, proj):
    s = proj.shape[0]

    def body(o0, l0, o1, l1, o2, l2, z_ref, ob_ref, o_ref, lse_ref, obt_ref):
        m = jnp.maximum(jnp.maximum(l0[...], l1[...]), l2[...])
        num = jnp.zeros_like(m)
        den = jnp.zeros_like(m)
        for og, lg in ((o0, l0), (o1, l1), (o2, l2)):
            sc = jnp.exp(lg[...] - m)
            num = num + og[...] * sc
            den = den + sc
        o = num / den
        o_ref[...] = o
        lse_ref[...] = m + jnp.log(den)
        ob = o * _silu(z_ref[...])
        ob_ref[...] = ob.astype(BF16)
        obt_ref[...] = ob.T.astype(BF16)

    row = pl.BlockSpec((ROW_TILE, DIL_W), lambda i: (i, 0))
    flat = [a for p in parts for a in p]
    return pl.pallas_call(
        body, name="att_merge", grid=(s // ROW_TILE,),
        in_specs=[row] * 6 + [pl.BlockSpec((ROW_TILE, DIL_W), lambda i: (i, OFF_Z_B // DIL_W))],
        out_specs=[row, row, row, pl.BlockSpec((DIL_W, ROW_TILE), lambda i: (0, i))],
        out_shape=[jax.ShapeDtypeStruct((s, DIL_W), BF16), jax.ShapeDtypeStruct((s, DIL_W), F32),
                   jax.ShapeDtypeStruct((s, DIL_W), F32), jax.ShapeDtypeStruct((DIL_W, s), BF16)],
        compiler_params=_cparams("parallel"))(*flat, proj)


def _att_merge_bwd(o, proj, dob, dproj):
    s = o.shape[0]

    def body(o_ref, z_ref, d_ref, dproj_in, do_ref, dl_ref, dz_ref):
        ov, zv, dv = o_ref[...], z_ref[...], d_ref[...]
        do = dv * _silu(zv)
        do_ref[...] = do
        dz_ref[...] = (dv * ov * _silu_grad(zv)).astype(BF16)
        for h in range(DIL_HEADS):
            sl = slice(h * DIL_DH, (h + 1) * DIL_DH)
            dl_ref[:, sl] = jnp.broadcast_to(jnp.sum(do[:, sl] * ov[:, sl], axis=-1, keepdims=True), (ROW_TILE, DIL_DH))

    row = pl.BlockSpec((ROW_TILE, DIL_W), lambda i: (i, 0))
    return pl.pallas_call(
        body, name="att_merge_bwd", grid=(s // ROW_TILE,),
        in_specs=[row, pl.BlockSpec((ROW_TILE, DIL_W), lambda i: (i, OFF_Z_B // DIL_W)), row, DPROJ_IN],
        out_specs=[row, row, pl.BlockSpec((ROW_TILE, DIL_W), lambda i: (i, OFF_Z_B // DIL_W))],
        out_shape=[jax.ShapeDtypeStruct((s, DIL_W), F32), jax.ShapeDtypeStruct((s, DIL_W), F32),
                   jax.ShapeDtypeStruct((s, PW), BF16)],
        input_output_aliases={3: 2},
        compiler_params=_cparams("parallel"))(o, proj, dob, dproj)


def _merge(proj, ya, yb):
    s = proj.shape[0]

    def body(ga_ref, gb_ref, ya_ref, yb_ref, o_ref, ot_ref):
        m = _sigmoid(ga_ref[...]) * ya_ref[...] + _sigmoid(gb_ref[...]) * yb_ref[...]
        o_ref[...] = m.astype(BF16)
        ot_ref[...] = m.T.astype(BF16)

    row = pl.BlockSpec((ROW_TILE, D_MODEL), lambda i: (i, 0))
    return pl.pallas_call(
        body, name="merge", grid=(s // ROW_TILE,),
        in_specs=[pl.BlockSpec((ROW_TILE, D_MODEL), lambda i: (i, OFF_G_A // D_MODEL)),
                  pl.BlockSpec((ROW_TILE, D_MODEL), lambda i: (i, OFF_G_B // D_MODEL)), row, row],
        out_specs=[row, pl.BlockSpec((D_MODEL, ROW_TILE), lambda i: (0, i))],
        out_shape=[jax.ShapeDtypeStruct((s, D_MODEL), BF16), jax.ShapeDtypeStruct((D_MODEL, s), BF16)],
        compiler_params=_cparams("parallel"))(proj, proj, ya, yb)


def _merge_bwd(proj, ya, yb, dm):
    s = proj.shape[0]

    def body(ga_ref, gb_ref, ya_ref, yb_ref, dm_ref, dya_ref, dyb_ref, dga_ref, dgb_ref):
        dmv = dm_ref[...]
        sa, sb = _sigmoid(ga_ref[...]), _sigmoid(gb_ref[...])
        dya_ref[...] = (dmv * sa).astype(BF16)
        dyb_ref[...] = (dmv * sb).astype(BF16)
        dga_ref[...] = (dmv * ya_ref[...] * sa * (1.0 - sa)).astype(BF16)
        dgb_ref[...] = (dmv * yb_ref[...] * sb * (1.0 - sb)).astype(BF16)

    row = pl.BlockSpec((ROW_TILE, D_MODEL), lambda i: (i, 0))
    return pl.pallas_call(
        body, name="merge_bwd", grid=(s // ROW_TILE,),
        in_specs=[pl.BlockSpec((ROW_TILE, D_MODEL), lambda i: (i, OFF_G_A // D_MODEL)),
                  pl.BlockSpec((ROW_TILE, D_MODEL), lambda i: (i, OFF_G_B // D_MODEL)), row, row, row],
        out_specs=[row] * 4, out_shape=[jax.ShapeDtypeStruct((s, D_MODEL), BF16)] * 4,
        compiler_params=_cparams("parallel"))(proj, proj, ya, yb, dm)


def _final(x, t, fw, tgt):
    s, d = x.shape

    def body(x_ref, t_ref, w_ref, y_ref, dx_ref, dw_ref, l_ref):
        i = pl.program_id(0)
        x2 = x_ref[...] + t_ref[...]
        wv = w_ref[...]
        r = lax.rsqrt(jnp.mean(x2 * x2, axis=-1, keepdims=True) + NORM_EPS)
        e = x2 * r * wv - y_ref[...]
        lrow = jnp.mean(e * e, axis=-1, keepdims=True)
        lpart = jnp.broadcast_to(0.5 * jnp.sum(lrow, axis=0, keepdims=True), (1, 128))
        dy = e * (1.0 / d)
        dwp = jnp.sum(dy * x2 * r, axis=0, keepdims=True)
        dyw = dy * wv
        dx_ref[...] = r * dyw - x2 * (r * r * r) * jnp.mean(dyw * x2, axis=-1, keepdims=True)

        @pl.when(i == 0)
        def _():
            dw_ref[...] = dwp
            l_ref[...] = lpart

        @pl.when(i > 0)
        def _():
            dw_ref[...] += dwp
            l_ref[...] += lpart

    row = pl.BlockSpec((ROW_TILE, d), lambda i: (i, 0))
    vec = pl.BlockSpec((1, d), lambda i: (0, 0))
    return pl.pallas_call(
        body, name="final", grid=(s // ROW_TILE,), in_specs=[row, row, vec, row],
        out_specs=[row, vec, pl.BlockSpec((1, 128), lambda i: (0, 0))],
        out_shape=[jax.ShapeDtypeStruct((s, d), F32), jax.ShapeDtypeStruct((1, d), F32), jax.ShapeDtypeStruct((1, 128), F32)],
        compiler_params=_cparams("arbitrary"))(x, t, fw, tgt)


def _adamw(w, g, m, v, name):
    r, c = w.shape
    cap = max(8, (1 << 18) // c)
    divisors = [t for t in range(8, min(r, cap) + 1, 8) if r % t == 0]
    tr = r if r <= 8 else (max(divisors) if divisors else cap)

    def body(w_ref, g_ref, m_ref, v_ref, d_ref, nm_ref, nv_ref):
        gv = g_ref[...]
        mn = ADAM_B1 * m_ref[...] + (1.0 - ADAM_B1) * gv
        vn = ADAM_B2 * v_ref[...] + (1.0 - ADAM_B2) * (gv * gv)
        m_hat = mn / (1.0 - ADAM_B1 ** ADAM_STEP)
        v_hat = vn / (1.0 - ADAM_B2 ** ADAM_STEP)
        d_ref[...] = -ADAM_LR * (m_hat / (jnp.sqrt(v_hat) + ADAM_EPS) + ADAM_WD * w_ref[...])
        nm_ref[...] = mn
        nv_ref[...] = vn

    blk = pl.BlockSpec((tr, c), lambda i: (i, 0))
    return pl.pallas_call(
        body, name=name, grid=(pl.cdiv(r, tr),), in_specs=[blk] * 4, out_specs=[blk] * 3,
        out_shape=[jax.ShapeDtypeStruct((r, c), F32)] * 3, compiler_params=_cparams("parallel"))(w, g, m, v)


HBM_SPEC = pl.BlockSpec(memory_space=pl.ANY)


def _place():
    x, y, c = lax.axis_index("x"), lax.axis_index("y"), lax.axis_index("c")
    chips = [(1 - x, y), (x, 1 - y), (1 - x, 1 - y)]
    return x, y, c, chips


def _ag_weights(packs):
    na = len(packs)
    nsem = 8

    def body(*refs):
        p_refs, out_refs = refs[:na], refs[na:2 * na]
        send_sems, recv_sems = refs[2 * na:]
        x, y, c, _ = _place()
        me, sib, j = (x, y, c), (x, y, 1 - c), 2 * x + y
        xn, yn = (1 - x, y, c), (x, 1 - y, c)
        jx, jy, jd = 2 * (1 - x) + y, 2 * x + (1 - y), 2 * (1 - x) + (1 - y)

        def rc(a, k, src, dst, to):
            return pltpu.make_async_remote_copy(src_ref=src, dst_ref=dst, send_sem=send_sems.at[nsem * a + k],
                                                recv_sem=recv_sems.at[nsem * a + k], device_id=to, device_id_type=MESH)

        sent = []
        for a in range(na):
            mine, land = p_refs[a].at[c], out_refs[a].at[j, c]
            sent += [rc(a, 0, mine, land, xn), rc(a, 1, mine, land, yn), rc(a, 7, p_refs[a], out_refs[a].at[j], sib)]
        for cp in sent:
            cp.start()
        for a in range(na):
            half = p_refs[a].shape[1] // 2
            top, bottom = pl.ds(0, half), pl.ds(half, half)
            from_x, from_y, from_d = out_refs[a].at[jx, c], out_refs[a].at[jy, c], out_refs[a].at[jd, c]
            rc(a, 0, p_refs[a].at[c], from_x, me).wait_recv()
            later = [rc(a, 2, from_x.at[top], from_x.at[top], yn), rc(a, 4, from_x, from_x, sib)]
            for cp in later:
                cp.start()
            sent += later
            rc(a, 1, p_refs[a].at[c], from_y, me).wait_recv()
            later = [rc(a, 3, from_y.at[bottom], from_y.at[bottom], xn), rc(a, 5, from_y, from_y, sib)]
            for cp in later:
                cp.start()
            sent += later
            rc(a, 2, from_d.at[top], from_d.at[top], me).wait_recv()
            rc(a, 3, from_d.at[bottom], from_d.at[bottom], me).wait_recv()
            cp = rc(a, 6, from_d, from_d, sib)
            cp.start()
            sent.append(cp)
        for a in range(na):
            for k, jj in ((4, jx), (5, jy), (6, jd)):
                rc(a, k, p_refs[a].at[c], out_refs[a].at[jj, 1 - c], me).wait_recv()
            rc(a, 7, p_refs[a], out_refs[a].at[j], me).wait_recv()
        for cp in sent:
            cp.wait_send()

    return pl.pallas_call(
        body, name="ag_weights",
        out_shape=[jax.ShapeDtypeStruct((N_CHIPS,) + p.shape, p.dtype) for p in packs],
        in_specs=[HBM_SPEC] * na, out_specs=[HBM_SPEC] * na,
        scratch_shapes=[pltpu.SemaphoreType.DMA((nsem * na,)), pltpu.SemaphoreType.DMA((nsem * na,))])(*packs)


def _rs_pair(dwpt, gpack):
    n = N_CHIPS
    hw = SHARD_PAD // 2

    def body(d_ref, g_ref, out_d, out_g, send_sems, recv_sems):
        x, y, c, _ = _place()
        sib = (x, y, 1 - c)
        cps = []
        for p in range(n):
            start = pl.multiple_of(WIN_BASE[p] + (1 - c) * hw, TILE_ROWS)
            cps.append(pltpu.make_async_remote_copy(
                src_ref=d_ref.at[pl.ds(start, hw)], dst_ref=out_d.at[p], send_sem=send_sems.at[p],
                recv_sem=recv_sems.at[p], device_id=sib, device_id_type=MESH))
            cps.append(pltpu.make_async_remote_copy(
                src_ref=g_ref.at[p, 1 - c], dst_ref=out_g.at[p], send_sem=send_sems.at[n + p],
                recv_sem=recv_sems.at[n + p], device_id=sib, device_id_type=MESH))
        for cp in cps:
            cp.start()
        for cp in cps:
            cp.wait_recv()
        for cp in cps:
            cp.wait_send()

    return pl.pallas_call(
        body, name="rs_pair",
        out_shape=[jax.ShapeDtypeStruct((n, hw, dwpt.shape[1]), dwpt.dtype),
                   jax.ShapeDtypeStruct((n,) + gpack.shape[2:], gpack.dtype)],
        in_specs=[HBM_SPEC] * 2, out_specs=[HBM_SPEC] * 2,
        scratch_shapes=[pltpu.SemaphoreType.DMA((2 * n,)), pltpu.SemaphoreType.DMA((2 * n,))])(dwpt, gpack)


def _add_halves_win(dwpt, other, c):
    n, rh, wd = other.shape
    tr = _row_tile(rh)

    def body(s_ref, d_ref, o_ref, out_ref):
        out_ref[0] = (d_ref[...] + o_ref[0]).astype(BF16)

    scal = jnp.concatenate([jnp.reshape(c, (1,)).astype(jnp.int32), jnp.asarray(WIN_BASE, jnp.int32)])
    grid_spec = pltpu.PrefetchScalarGridSpec(
        num_scalar_prefetch=1, grid=(n, rh // tr),
        in_specs=[pl.BlockSpec((pl.Element(tr), pl.Element(wd)),
                               lambda p, i, sr: (pl.multiple_of(sr[1 + p] + sr[0] * rh + i * tr, TILE_ROWS), 0)),
                  pl.BlockSpec((1, tr, wd), lambda p, i, sr: (p, i, 0))],
        out_specs=pl.BlockSpec((1, tr, wd), lambda p, i, sr: (p, i, 0)))
    return pl.pallas_call(
        body, name="add_halves_in", grid_spec=grid_spec, out_shape=jax.ShapeDtypeStruct((n, rh, wd), BF16),
        compiler_params=_cparams("parallel", "parallel"))(scal, dwpt, other)


SEM_SPEC = pl.BlockSpec(memory_space=pltpu.SEMAPHORE)
DATAFLOW_EFFECT = pltpu.SideEffectType.DATAFLOW_SIDE_EFFECTING


def _rs_chips_start(csums):
    na = len(csums)

    def body(*refs):
        s_refs, land_refs = refs[:na], refs[na:2 * na]
        send_sems, recv_sems = refs[2 * na], refs[2 * na + 1]
        token = refs[-1]
        x, y, c, chips = _place()
        j = 2 * x + y
        for a in range(na):
            for k, (cx, cy) in enumerate(chips):
                pltpu.make_async_remote_copy(src_ref=s_refs[a].at[2 * cx + cy], dst_ref=land_refs[a].at[j],
                                             send_sem=send_sems.at[3 * a + k], recv_sem=recv_sems.at[3 * a + k],
                                             device_id=(cx, cy, c), device_id_type=MESH).start()
        token[...] = jnp.zeros_like(token)

    hbm = [pltpu.HBM(s.shape, s.dtype) for s in csums]
    args = [pltpu.with_memory_space_constraint(s, pltpu.HBM) for s in csums]
    args += [pltpu.with_memory_space_constraint(lax.empty(s.shape, s.dtype), pltpu.HBM) for s in csums]
    res = pl.pallas_call(
        body, name="rs_chips_start",
        out_shape=(pltpu.SemaphoreType.DMA((3 * na,)), pltpu.SemaphoreType.DMA((3 * na,)), *hbm, *hbm,
                   jax.ShapeDtypeStruct((8, 128), F32)),
        in_specs=[pl.BlockSpec(memory_space=pltpu.HBM)] * (2 * na),
        out_specs=(SEM_SPEC, SEM_SPEC, *[pl.BlockSpec(memory_space=pltpu.HBM)] * (2 * na),
                   pl.BlockSpec(memory_space=pltpu.VMEM)),
        input_output_aliases={i: 2 + i for i in range(2 * na)},
        compiler_params=pltpu.CompilerParams(has_side_effects=DATAFLOW_EFFECT))(*args)
    return res[0], res[1], list(res[2:2 + na]), list(res[2 + na:2 + 2 * na]), res[-1]


def _rs_chips_wait(send_sems, recv_sems, csums, lands, after):
    na = len(csums)

    def body(*refs):
        s_refs, land_refs = refs[:na], refs[na:2 * na]
        send_sems, recv_sems = refs[2 * na], refs[2 * na + 1]
        x, y, c, chips = _place()
        j = 2 * x + y
        for a in range(na):
            for k, (cx, cy) in enumerate(chips):
                cp = pltpu.make_async_remote_copy(src_ref=s_refs[a].at[2 * cx + cy], dst_ref=land_refs[a].at[2 * cx + cy],
                                                  send_sem=send_sems.at[3 * a + k], recv_sem=recv_sems.at[3 * a + k],
                                                  device_id=(cx, cy, c), device_id_type=MESH)
                cp.wait_send()
                cp.wait_recv()

    hbm = [pltpu.HBM(s.shape, s.dtype) for s in csums]
    res = pl.pallas_call(
        body, name="rs_chips_wait", out_shape=(*hbm, *hbm),
        in_specs=[pl.BlockSpec(memory_space=pltpu.HBM)] * (2 * na) + [SEM_SPEC, SEM_SPEC, pl.BlockSpec(memory_space=pl.ANY)],
        out_specs=tuple([pl.BlockSpec(memory_space=pltpu.HBM)] * (2 * na)),
        input_output_aliases={i: i for i in range(2 * na)},
        compiler_params=pltpu.CompilerParams(has_side_effects=DATAFLOW_EFFECT))(*csums, *lands, send_sems, recv_sems, after)
    return list(res[:na]), list(res[na:])


SWAP_CHUNKS = 4


def _pair_swap(halves):
    na = len(halves)

    def body(*refs):
        h_refs, out_refs = refs[:na], refs[na:2 * na]
        send_sems, recv_sems = refs[2 * na:]
        x, y, c, _ = _place()
        cps = []
        for a in range(na):
            rows = h_refs[a].shape[0] // SWAP_CHUNKS
            assert rows * SWAP_CHUNKS == h_refs[a].shape[0]
            for q in range(SWAP_CHUNKS):
                k = SWAP_CHUNKS * a + q
                cps.append(pltpu.make_async_remote_copy(
                    src_ref=h_refs[a].at[pl.ds(q * rows, rows)], dst_ref=out_refs[a].at[pl.ds(q * rows, rows)],
                    send_sem=send_sems.at[k], recv_sem=recv_sems.at[k], device_id=(x, y, 1 - c), device_id_type=MESH))
        for cp in cps:
            cp.start()
        for cp in cps:
            cp.wait_recv()
        for cp in cps:
            cp.wait_send()

    return pl.pallas_call(
        body, name="pair_swap", out_shape=[jax.ShapeDtypeStruct(h.shape, h.dtype) for h in halves],
        in_specs=[HBM_SPEC] * na, out_specs=[HBM_SPEC] * na,
        scratch_shapes=[pltpu.SemaphoreType.DMA((SWAP_CHUNKS * na,)), pltpu.SemaphoreType.DMA((SWAP_CHUNKS * na,))])(*halves)


def _ag_small(v):
    m_per, n = v.shape

    def body(x_ref, out_ref, send_sems, recv_sems, local_sem):
        x, y, c, chips = _place()
        me, sibling = (x, y, c), (x, y, 1 - c)

        def rows(px, py, pc):
            return out_ref.at[pl.ds((4 * px + 2 * py + pc) * m_per, m_per), :]

        def copy(k, block, to, src=None):
            return pltpu.make_async_remote_copy(
                src_ref=rows(*block) if src is None else src, dst_ref=rows(*block), send_sem=send_sems.at[k],
                recv_sem=recv_sems.at[k], device_id=to, device_id_type=MESH)

        mine = pltpu.make_async_copy(x_ref, rows(*me), local_sem)
        mine.start()
        first = [copy(0, me, sibling, src=x_ref)]
        first += [copy(1 + k, me, (*chip, c), src=x_ref) for k, chip in enumerate(chips)]
        for cp in first:
            cp.start()
        passed = [copy(4 + k, (*chip, c), sibling) for k, chip in enumerate(chips)]
        for k, chip in enumerate(chips):
            copy(1 + k, (*chip, c), me).wait_recv()
            passed[k].start()
        copy(0, sibling, me).wait_recv()
        for k, chip in enumerate(chips):
            copy(4 + k, (*chip, 1 - c), me).wait_recv()
        for cp in first + passed:
            cp.wait_send()
        mine.wait()

    return pl.pallas_call(
        body, name="ag_small", out_shape=jax.ShapeDtypeStruct((8 * m_per, n), v.dtype),
        in_specs=[pl.BlockSpec(memory_space=pltpu.VMEM)], out_specs=pl.BlockSpec(memory_space=pltpu.VMEM),
        scratch_shapes=[pltpu.SemaphoreType.DMA((7,)), pltpu.SemaphoreType.DMA((7,)), pltpu.SemaphoreType.DMA])(v)


def _sum_blocks(a, nblk, name):
    rows, wd = a.shape
    r = rows // nblk
    tr = min(r, ROW_TILE)
    assert r % tr == 0

    def body(*refs):
        acc = refs[0][...].astype(F32)
        for ref in refs[1:nblk]:
            acc = acc + ref[...].astype(F32)
        refs[nblk][...] = acc

    nt = r // tr
    return pl.pallas_call(
        body, name=name, grid=(nt,),
        in_specs=[pl.BlockSpec((tr, wd), functools.partial(lambda i, b: (b * nt + i, 0), b=b)) for b in range(nblk)],
        out_specs=pl.BlockSpec((tr, wd), lambda i: (i, 0)),
        out_shape=jax.ShapeDtypeStruct((r, wd), F32), compiler_params=_cparams("parallel"))(*([a] * nblk))


def _row_tile(rows):
    best = max(t for t in range(16, 513, 16) if rows % t == 0)
    return best


def _sum_chips(by_src, csum, j, name):
    n, rh, wd = by_src.shape
    tr = _row_tile(rh)

    def body(j_ref, *refs):
        own = refs[n][0].astype(F32)
        acc = None
        for k in range(n):
            term = jnp.where(j_ref[0] == k, own, refs[k][0].astype(F32))
            acc = term if acc is None else acc + term
        refs[n + 1][...] = acc

    def other(k):
        return pl.BlockSpec((1, tr, wd), lambda i, jr: (jnp.where(jr[0] == k, (k + 1) % n, k), i, 0))

    grid_spec = pltpu.PrefetchScalarGridSpec(
        num_scalar_prefetch=1, grid=(rh // tr,),
        in_specs=[other(k) for k in range(n)] + [pl.BlockSpec((1, tr, wd), lambda i, jr: (jr[0], i, 0))],
        out_specs=pl.BlockSpec((tr, wd), lambda i, jr: (i, 0)))
    return pl.pallas_call(
        body, name=name, grid_spec=grid_spec, out_shape=jax.ShapeDtypeStruct((rh, wd), F32),
        compiler_params=_cparams("parallel"))(jnp.reshape(j, (1,)).astype(jnp.int32), *([by_src] * n), csum)


def _add_halves(gpack, other, c, name):
    n, _, rh, wd = gpack.shape
    tr = _row_tile(rh)

    def body(c_ref, g_ref, o_ref, out_ref):
        out_ref[0] = (g_ref[0, 0] + o_ref[0]).astype(BF16)

    grid_spec = pltpu.PrefetchScalarGridSpec(
        num_scalar_prefetch=1, grid=(n, rh // tr),
        in_specs=[pl.BlockSpec((1, 1, tr, wd), lambda p, i, cr: (p, cr[0], i, 0)),
                  pl.BlockSpec((1, tr, wd), lambda p, i, cr: (p, i, 0))],
        out_specs=pl.BlockSpec((1, tr, wd), lambda p, i, cr: (p, i, 0)))
    return pl.pallas_call(
        body, name=name, grid_spec=grid_spec, out_shape=jax.ShapeDtypeStruct((n, rh, wd), BF16),
        compiler_params=_cparams("parallel", "parallel"))(jnp.reshape(c, (1,)).astype(jnp.int32), gpack, other)


PACK_W = 1024
ROWS_O_DN = DN_W // N_CHIPS
ROWS_O_DIL = DIL_W * (D_MODEL // N_CHIPS) // PACK_W
ROWS_OUT = D_MODEL // N_CHIPS
ROWS_CONV = 4 * (3 * DN_W // N_CHIPS) // PACK_W
R1 = ROWS_O_DN
R2 = R1 + ROWS_O_DIL
R3 = R2 + ROWS_OUT
R4 = R3 + 16
R5 = R4 + 16
PACK_ROWS = 704
HALF_ROWS = PACK_ROWS // 2
SHARD_PAD = 2880


R6 = R5 + 2 * DN_HEADS

TILE_ROWS = 16
BA_IN_SHARD1 = REF_OFF_BA - SHARD_W
LOCAL_START = (0, SHARD_W, 2 * SHARD_W - 2 * DN_HEADS, 3 * SHARD_W - 2 * DN_HEADS)
LOCAL_END = LOCAL_START[1:] + (OFF_BA,)
WIN_BASE = tuple(s // TILE_ROWS * TILE_ROWS for s in LOCAL_START)


def _to_window(k, shard):
    nba = 2 * DN_HEADS
    body = shard
    if k == 1:
        row = lax.broadcasted_iota(jnp.int32, (SHARD_W - nba, 1), 0)
        body = jnp.where(row < BA_IN_SHARD1, shard[:SHARD_W - nba], shard[nba:])
    lead = LOCAL_START[k] - WIN_BASE[k]
    return jnp.pad(body, ((lead, SHARD_PAD - lead - body.shape[0]), (0, 0)))


def _from_window(k, win, ba):
    nba = 2 * DN_HEADS
    lead = LOCAL_START[k] - WIN_BASE[k]
    if k != 1:
        return win[lead:lead + SHARD_W]
    row = lax.broadcasted_iota(jnp.int32, (SHARD_W, 1), 0)
    before = win[lead:lead + SHARD_W]
    after = jnp.pad(win, ((nba, 0), (0, 0)))[lead:lead + SHARD_W]
    mid = jnp.pad(ba, ((BA_IN_SHARD1, SHARD_W - BA_IN_SHARD1 - nba), (0, 0)))
    return jnp.where(row < BA_IN_SHARD1, before, jnp.where(row < BA_IN_SHARD1 + nba, mid, after))


def _stack_windows(wins, ba):
    pieces = []
    for k in range(N_CHIPS):
        lo = WIN_BASE[k] + (TILE_ROWS if k else 0)
        hi = LOCAL_END[k] // TILE_ROWS * TILE_ROWS
        pieces.append(wins[k][lo - WIN_BASE[k]:hi - WIN_BASE[k]])
        if k + 1 < N_CHIPS:
            assert hi == WIN_BASE[k + 1]
            pieces.append(wins[k][hi - WIN_BASE[k]:hi - WIN_BASE[k] + TILE_ROWS] + wins[k + 1][:TILE_ROWS])
    pieces += [ba, jnp.zeros((PW - OFF_BA - ba.shape[0], ba.shape[1]), ba.dtype)]
    out = jnp.concatenate(pieces, axis=0)
    assert out.shape[0] == PW
    return out


def _local_step(x, tgt, norm_w, wpt, conv_full, a_log, dt_bias, dn_norm_w, w_o_dn, w_o_dil, w_out, final_norm_w):
    s = x.shape[0]
    h, h_t = _rms_in(x, norm_w)
    proj = _matmul(h, wpt, F32, 2048, 1280, 1024, "proj", nt=True)
    c_pre, qkv = _conv_fwd(proj, conv_full)
    gate_par = jnp.zeros((8, 128), F32).at[0, 8:16].set(a_log[0]).at[1, 8:16].set(dt_bias[0])
    bg = _gates_fwd(proj, gate_par)
    o_a, u, w, vn, tmat, states = _gdr_fwd(qkv, bg)
    oa2, oa2_t = _gdr_out(o_a, proj, dn_norm_w)
    ya = _matmul(oa2, w_o_dn, F32, 1024, 1024, 1024, "ya")
    parts = [_att_fwd(proj, g) for g in range(N_DIL)]
    ob, o_att, lse, ob_t = _att_merge(parts, proj)
    yb = _matmul(ob, w_o_dil, F32, 1024, 1024, 512, "yb")
    mg, mg_t = _merge(proj, ya, yb)
    t = _matmul(mg, w_out, F32, 1024, 1024, 1024, "t_out")
    dx2, dfw, lpart = _final(x, t, final_norm_w, tgt)

    dmg = _matmul(dx2, w_out, F32, 1024, 1024, 1024, "d_merged", nt=True)
    dw_out = _matmul(mg_t, dx2, F32, 1024, 1024, 1024, "dw_out")
    dya, dyb, dga, dgb = _merge_bwd(proj, ya, yb, dmg)
    doa2 = _matmul(dya, w_o_dn, F32, 1024, 1024, 1024, "d_oa2", nt=True)
    dw_o_dn = _matmul(oa2_t, dya, F32, 1024, 1024, 1024, "dw_o_dn")
    dob = _matmul(dyb, w_o_dil, F32, 1024, 512, 1024, "d_ob", nt=True)
    dw_o_dil = _matmul(ob_t, dyb, F32, 512, 1024, 1024, "dw_o_dil")
    do_a, dproj, ddnw = _gdr_out_bwd(o_a, proj, dn_norm_w, doa2)
    dq_a, dk_a, dv_a, dbg = _gdr_bwd(qkv, bg, u, w, vn, tmat, states, do_a)
    dproj, dpar = _gates_bwd(proj, gate_par, dbg, dproj)
    dc = _conv_bwd_act(c_pre, dq_a, dk_a, dv_a)
    dproj, dconv = _conv_bwd(proj, dc, conv_full, dproj)
    do_att, delta, dproj = _att_merge_bwd(o_att, proj, dob, dproj)
    dqkv_b = [_att_bwd(proj, g, do_att, lse, delta) for g in range(N_DIL)]
    pieces = [(OFF_Q_B + (N_DIL * i + g) * DIL_W, dqkv_b[g][i]) for i in range(3) for g in range(N_DIL)]
    for off, piece in pieces + [(OFF_G_A, dga), (OFF_G_B, dgb)]:
        dproj = lax.dynamic_update_slice(dproj, piece, (0, off))
    dwpt, dwpt_b = _matmul(h_t, dproj, F32, 1024, 1280, 2048, "dw_in", transpose_out=True, also_bf16=True)

    def finish(after=None):
        dh = _matmul(dproj, wpt, F32, 1024, 1024, 3840, "d_h", after=after)
        grad_x, dnw = _rms_in_bwd(x, norm_w, dh, dx2)
        small = jnp.zeros((8, PACK_W), F32)
        small = small.at[0].set(dnw[0]).at[1].set(dfw[0]).at[2, :DN_D].set(ddnw[0])
        small = small.at[3, :DN_HEADS].set(dpar[0, 8:16]).at[3, DN_HEADS:2 * DN_HEADS].set(dpar[1, 8:16])
        small = small.at[4, 0].set(lpart[0, 0])
        return grad_x, small

    return finish, (dwpt, dwpt_b), dconv, dw_o_dn, dw_o_dil, dw_out


def kernel(x, norm_w, w_in, conv_w, a_log, dt_bias, dn_norm_w, w_o_dn, w_o_dil, w_out, final_norm_w, loss_target, m_norm_w, m_w_in, m_conv_w, m_a_log, m_dt_bias, m_dn_norm_w, m_w_o_dn, m_w_o_dil, m_w_out, m_final_norm_w, v_norm_w, v_w_in, v_conv_w, v_a_log, v_dt_bias, v_dn_norm_w, v_w_o_dn, v_w_o_dil, v_w_out, v_final_norm_w):
    c = lax.axis_index("c")
    j = 2 * lax.axis_index("x") + lax.axis_index("y")
    qw = D_MODEL // N_CHIPS

    cw = conv_w[0].reshape(ROWS_CONV, PACK_W)
    cw = jnp.pad(cw, ((0, 16 - ROWS_CONV), (0, 0)))
    cw_hi = cw.astype(BF16)
    cw_lo = (cw - cw_hi.astype(F32)).astype(BF16)
    shard = w_in[0].T.astype(BF16)
    own_ba = jnp.where(j == 1, shard[BA_IN_SHARD1:BA_IN_SHARD1 + 2 * DN_HEADS], jnp.zeros((2 * DN_HEADS, D_MODEL), BF16))
    pack = jnp.concatenate(
        [w_o_dn[0].astype(BF16), w_o_dil[0].astype(BF16).reshape(ROWS_O_DIL, PACK_W), w_out[0].astype(BF16), cw_hi, cw_lo,
         own_ba, jnp.zeros((PACK_ROWS - R6, PACK_W), BF16)], axis=0).reshape(2, HALF_ROWS, PACK_W)
    chips = range(N_CHIPS)
    own_win = lax.switch(j, [functools.partial(_to_window, k) for k in chips], shard).reshape(2, SHARD_PAD // 2, D_MODEL)
    all_in, allw = _ag_weights([own_win, pack])
    wins = [all_in[k].reshape(SHARD_PAD, D_MODEL) for k in chips]
    allw = [allw[k].reshape(PACK_ROWS, PACK_W) for k in chips]
    wpt = _stack_windows(wins, allw[1][R5:R6])
    w_o_dn_full = jnp.concatenate([allw[k][:R1] for k in chips], axis=0)
    w_o_dil_full = jnp.concatenate([allw[k][R1:R2].reshape(DIL_W, qw) for k in chips], axis=1)
    w_out_full = jnp.concatenate([allw[k][R2:R3] for k in chips], axis=0)
    conv_full = jnp.concatenate(
        [(allw[k][R3:R3 + ROWS_CONV].astype(F32) + allw[k][R4:R4 + ROWS_CONV].astype(F32)).reshape(4, 3 * DN_W // N_CHIPS)
         for k in chips], axis=1)

    finish, (dwpt, dwpt_b), dconv, dw_o_dn, dw_o_dil, dw_out = _local_step(
        x[0], loss_target[0], norm_w, wpt, conv_full, a_log, dt_bias, dn_norm_w, w_o_dn_full, w_o_dil_full, w_out_full,
        final_norm_w.reshape(1, D_MODEL))

    cq = 3 * DN_W // N_CHIPS
    gpack = jnp.stack([
        jnp.concatenate(
            [dw_o_dn[k * qw:(k + 1) * qw], dw_o_dil[:, k * qw:(k + 1) * qw].reshape(ROWS_O_DIL, PACK_W),
             dw_out[k * qw:(k + 1) * qw],
             jnp.pad(dconv[:, k * cq:(k + 1) * cq].reshape(ROWS_CONV, PACK_W), ((0, 16 - ROWS_CONV), (0, 0))),
             dwpt[OFF_BA:OFF_BA + 2 * DN_HEADS] if k == 1 else jnp.zeros((2 * DN_HEADS, PACK_W), F32),
             jnp.zeros((PACK_ROWS - R4 - 2 * DN_HEADS, PACK_W), F32)], axis=0)
        for k in chips]).reshape(N_CHIPS, 2, HALF_ROWS, PACK_W)
    sib_in, sib_pack = _rs_pair(dwpt_b, gpack)
    csum_in = _add_halves_win(dwpt, sib_in, c)
    csum_pack = _add_halves(gpack, sib_pack, c, "add_halves_pack")
    send_sems, recv_sems, csums, lands, token = _rs_chips_start([csum_in, csum_pack])
    grad_x, small = finish(after=token)

    gs = _sum_blocks(_ag_small(small), 8, "sum_small")
    loss = gs[4, 0]
    w_small = jnp.zeros((8, PACK_W), F32)

    def pack_small(nw, fw, dnw_, al, db):
        t = w_small.at[0].set(nw[0]).at[1].set(fw).at[2, :DN_D].set(dnw_[0])
        return t.at[3, :DN_HEADS].set(al[0]).at[3, DN_HEADS:2 * DN_HEADS].set(db[0])

    sm = _adamw(pack_small(norm_w, final_norm_w, dn_norm_w, a_log, dt_bias), gs,
                pack_small(m_norm_w, m_final_norm_w, m_dn_norm_w, m_a_log, m_dt_bias),
                pack_small(v_norm_w, v_final_norm_w, v_dn_norm_w, v_a_log, v_dt_bias), "adamw_small")

    (csum_in, csum_pack), (src_in, src_pack) = _rs_chips_wait(send_sems, recv_sems, csums, lands, sm[0])
    half_in = _sum_chips(src_in, csum_in, j, "sum_chips_in")
    half_pack = _sum_chips(src_pack, csum_pack, j, "sum_chips_pack")
    sib_half_in, sib_half_pack = _pair_swap([half_in, half_pack])

    def both_halves(mine, theirs):
        return jnp.where(c == 0, jnp.concatenate([mine, theirs], axis=0), jnp.concatenate([theirs, mine], axis=0))

    g = both_halves(half_pack, sib_half_pack)
    g_w_in = lax.switch(j, [functools.partial(_from_window, k) for k in chips], both_halves(half_in, sib_half_in),
                        g[R4:R4 + 2 * DN_HEADS])
    g_w_o_dn = g[:R1]
    g_w_o_dil = g[R1:R2].reshape(DIL_W, qw)
    g_w_out = g[R2:R3]
    g_conv = g[R3:R3 + ROWS_CONV].reshape(4, cq)

    def unpack_small(t):
        return dict(norm_w=t[0:1], final_norm_w=t[1], dn_norm_w=t[2:3, :DN_D], a_log=t[3:4, :DN_HEADS],
                    dt_bias=t[3:4, DN_HEADS:2 * DN_HEADS])

    res = {"grad": unpack_small(gs)}
    for kind, arr in zip(("delta", "new_m", "new_v"), sm):
        res[kind] = unpack_small(arr)
    big = dict(conv_w=(conv_w, g_conv, m_conv_w, v_conv_w), w_o_dn=(w_o_dn, g_w_o_dn, m_w_o_dn, v_w_o_dn),
               w_o_dil=(w_o_dil, g_w_o_dil, m_w_o_dil, v_w_o_dil), w_out=(w_out, g_w_out, m_w_out, v_w_out))
    for name, (wt, gt, mt, vt) in big.items():
        d, nm, nv = _adamw(wt[0], gt, mt[0], vt[0], "adamw_" + name)
        res["grad"][name] = gt[None]
        res["delta"][name], res["new_m"][name], res["new_v"][name] = d[None], nm[None], nv[None]

    d, nm, nv = _adamw(w_in[0].T, g_w_in, m_w_in[0].T, v_w_in[0].T, "adamw_w_in")
    res["grad"]["w_in"] = g_w_in.T[None]
    res["delta"]["w_in"], res["new_m"]["w_in"], res["new_v"]["w_in"] = d.T[None], nm.T[None], nv.T[None]
    order = ["norm_w", "w_in", "conv_w", "a_log", "dt_bias", "dn_norm_w", "w_o_dn", "w_o_dil", "w_out", "final_norm_w"]
    outs = [loss, grad_x[None]]
    for kind in ("grad", "delta", "new_m", "new_v"):
        outs += [res[kind][nm] for nm in order]
    return tuple(outs)
```

```python
import functools
import math

import jax
import jax.numpy as jnp
from jax import lax
from jax.experimental import pallas as pl
from jax.experimental.pallas import tpu as pltpu

F32 = jnp.float32
BF16 = jnp.bfloat16
MESH = pl.DeviceIdType.MESH

D_MODEL = 1024
DN_HEADS = 8
DN_D = 128
DN_CHUNK = 64
DN_W = DN_HEADS * DN_D
DIL_GROUPS = ((128, 1), (512, 4), (2048, 16))
N_DIL = len(DIL_GROUPS)
DIL_HEADS = 4
DIL_DH = 128
DIL_W = DIL_HEADS * DIL_DH
ATT_BLOCK = 128
NORM_EPS = 1e-6
PROJ_W = 11280
N_CHIPS = 4
SHARD_W = PROJ_W // N_CHIPS

OFF_QKV_A = 0
OFF_Z_A = 3072
OFF_Q_B = 4096
OFF_K_B = 5632
OFF_V_B = 7168
OFF_Z_B = 8704
OFF_G_A = 9216
OFF_G_B = 10240
OFF_BA = 11264
PW = 11520
REF_OFF_BA = 4096

ADAM_LR = 0.001
ADAM_B1 = 0.9
ADAM_B2 = 0.999
ADAM_EPS = 1e-08
ADAM_WD = 0.01
ADAM_STEP = 10

ROW_TILE = 512
CONV_TILE = 1024
BIG_TILE = 1024
NEG = -1e30


def _dot(a, b):
    return jnp.dot(a.astype(BF16), b.astype(BF16), preferred_element_type=F32)


def _dot_nt(a, b):
    return lax.dot_general(a.astype(BF16), b.astype(BF16), (((1,), (1,)), ((), ())), preferred_element_type=F32)


def _dot_tn(a, b):
    return lax.dot_general(a.astype(BF16), b.astype(BF16), (((0,), (0,)), ((), ())), preferred_element_type=F32)


def _split(a):
    hi = a.astype(BF16)
    lo = (a - hi.astype(F32)).astype(BF16)
    return hi, lo


def _dot_exact_lhs(c, a):
    hi, lo = _split(a)
    cb = c.astype(BF16)
    return jnp.dot(cb, hi, preferred_element_type=F32) + jnp.dot(cb, lo, preferred_element_type=F32)


def _dot_tn_exact_rhs(a, c):
    hi, lo = _split(a)
    cb = c.astype(BF16)
    dn = (((0,), (0,)), ((), ()))
    return (lax.dot_general(hi, cb, dn, preferred_element_type=F32)
            + lax.dot_general(lo, cb, dn, preferred_element_type=F32))


def _sigmoid(x):
    return 1.0 / (1.0 + jnp.exp(-x))


def _silu(x):
    return x * _sigmoid(x)


def _silu_grad(x):
    s = _sigmoid(x)
    return s * (1.0 + x * (1.0 - s))


def _softplus(x):
    return jnp.maximum(x, 0.0) + jnp.log(1.0 + jnp.exp(-jnp.abs(x)))


def _cparams(*sem):
    return pltpu.CompilerParams(dimension_semantics=sem)


def _matmul(a, b, out_dtype, tm, tn, tk, name, nt=False, transpose_out=False, after=None, also_bf16=False):
    m, kdim = a.shape
    n = b.shape[0] if nt else b.shape[1]
    tm, tn, tk = min(tm, m), min(tn, n), min(tk, kdim)
    assert m % tm == 0 and n % tn == 0 and kdim % tk == 0, (name, a.shape, b.shape, tm, tn, tk)
    nk = kdim // tk
    dot = _dot_nt if nt else _dot
    b_spec = (pl.BlockSpec((tn, tk), lambda i, j, k: (j, k)) if nt else pl.BlockSpec((tk, tn), lambda i, j, k: (k, j)))
    extra = [] if after is None else [after]
    out_dtypes = [out_dtype] + ([BF16] if also_bf16 else [])

    def emit(o_refs, acc):
        val = acc.T if transpose_out else acc
        for o_ref in o_refs:
            o_ref[...] = val.astype(o_ref.dtype)

    def outs_of(rest):
        return rest[len(extra):len(extra) + len(out_dtypes)]

    if nk == 1:
        def body(a_ref, b_ref, *rest):
            emit(outs_of(rest), dot(a_ref[...], b_ref[...]))
        scratch = []
    else:
        def body(a_ref, b_ref, *rest):
            o_ref, acc_ref = outs_of(rest), rest[-1]
            k = pl.program_id(2)
            p = dot(a_ref[...], b_ref[...])

            @pl.when(k == 0)
            def _():
                acc_ref[...] = p

            @pl.when(k > 0)
            def _():
                acc_ref[...] += p

            @pl.when(k == nk - 1)
            def _():
                emit(o_ref, acc_ref[...])
        scratch = [pltpu.VMEM((tm, tn), F32)]

    if transpose_out:
        out_spec, out_shape = pl.BlockSpec((tn, tm), lambda i, j, k: (j, i)), (n, m)
    else:
        out_spec, out_shape = pl.BlockSpec((tm, tn), lambda i, j, k: (i, j)), (m, n)
    res = pl.pallas_call(
        body, name=name, grid=(m // tm, n // tn, nk),
        in_specs=[pl.BlockSpec((tm, tk), lambda i, j, k: (i, k)), b_spec] + [pl.BlockSpec(memory_space=pl.ANY)] * len(extra),
        out_specs=[out_spec] * len(out_dtypes), out_shape=[jax.ShapeDtypeStruct(out_shape, d) for d in out_dtypes],
        scratch_shapes=scratch, compiler_params=_cparams("parallel", "parallel", "arbitrary"))(a, b, *extra)
    return res if also_bf16 else res[0]


def _rms_in(x, nw):
    s, d = x.shape

    def body(x_ref, w_ref, h_ref, ht_ref):
        xv = x_ref[...]
        r = lax.rsqrt(jnp.mean(xv * xv, axis=-1, keepdims=True) + NORM_EPS)
        h = xv * r * w_ref[...]
        h_ref[...] = h.astype(BF16)
        ht_ref[...] = h.T.astype(BF16)

    return pl.pallas_call(
        body, name="rms_in", grid=(s // BIG_TILE,),
        in_specs=[pl.BlockSpec((BIG_TILE, d), lambda i: (i, 0)), pl.BlockSpec((1, d), lambda i: (0, 0))],
        out_specs=[pl.BlockSpec((BIG_TILE, d), lambda i: (i, 0)), pl.BlockSpec((d, BIG_TILE), lambda i: (0, i))],
        out_shape=[jax.ShapeDtypeStruct((s, d), BF16), jax.ShapeDtypeStruct((d, s), BF16)],
        compiler_params=_cparams("parallel"))(x, nw)


def _rms_in_bwd(x, nw, dh, dx2):
    s, d = x.shape

    def body(x_ref, w_ref, dh_ref, dx2_ref, dx_ref, dw_ref):
        i = pl.program_id(0)
        xv = x_ref[...]
        r = lax.rsqrt(jnp.mean(xv * xv, axis=-1, keepdims=True) + NORM_EPS)
        dhv = dh_ref[...]
        dyw = dhv * w_ref[...]
        dx_ref[...] = dx2_ref[...] + r * dyw - xv * (r * r * r) * jnp.mean(dyw * xv, axis=-1, keepdims=True)
        part = jnp.sum(dhv * xv * r, axis=0, keepdims=True)

        @pl.when(i == 0)
        def _():
            dw_ref[...] = part

        @pl.when(i > 0)
        def _():
            dw_ref[...] += part

    row = pl.BlockSpec((BIG_TILE, d), lambda i: (i, 0))
    vec = pl.BlockSpec((1, d), lambda i: (0, 0))
    return pl.pallas_call(
        body, name="rms_in_bwd", grid=(s // BIG_TILE,), in_specs=[row, vec, row, row], out_specs=[row, vec],
        out_shape=[jax.ShapeDtypeStruct((s, d), F32), jax.ShapeDtypeStruct((1, d), F32)],
        compiler_params=_cparams("arbitrary"))(x, nw, dh, dx2)


def _shift_down(cur, prev8, k):
    rc = pltpu.roll(cur, k, 0)
    rp = pltpu.roll(prev8, k, 0)
    row = lax.broadcasted_iota(jnp.int32, prev8.shape, 0)
    top = jnp.where(row < k, rp, rc[:8])
    return jnp.concatenate([top, rc[8:]], axis=0)


def _shift_up(cur, next8, k):
    t = cur.shape[0]
    rc = pltpu.roll(cur, t - k, 0)
    rn = pltpu.roll(next8, 8 - k, 0)
    row = lax.broadcasted_iota(jnp.int32, next8.shape, 0)
    bot = jnp.where(row >= 8 - k, rn, rc[t - 8:])
    return jnp.concatenate([rc[:t - 8], bot], axis=0)


def _conv_fwd(proj, conv_w):
    s = proj.shape[0]
    tile = min(s, CONV_TILE)
    t8 = tile // 8

    def body(u_ref, up_ref, w_ref, c_ref, y_ref):
        i = pl.program_id(0)
        part = pl.program_id(1)
        cur = u_ref[...]
        prev8 = jnp.where(i > 0, up_ref[...], 0.0)
        w = w_ref[...]
        c = cur * w[3:4, :]
        for k in (1, 2, 3):
            c = c + _shift_down(cur, prev8, k) * w[3 - k:4 - k, :]
        c_ref[...] = c
        a = _silu(c)
        for h in range(DN_HEADS):
            ah = a[:, h * DN_D:(h + 1) * DN_D]
            r = lax.rsqrt(jnp.sum(ah * ah, axis=-1, keepdims=True) + NORM_EPS)
            y_ref[:, h * DN_D:(h + 1) * DN_D] = jnp.where(part < 2, ah * r, ah)

    return pl.pallas_call(
        body, name="conv_fwd", grid=(s // tile, 3),
        in_specs=[pl.BlockSpec((tile, DN_W), lambda i, p: (i, p)),
                  pl.BlockSpec((8, DN_W), lambda i, p: (jnp.maximum(i * t8 - 1, 0), p)),
                  pl.BlockSpec((4, DN_W), lambda i, p: (0, p))],
        out_specs=[pl.BlockSpec((tile, DN_W), lambda i, p: (i, p))] * 2,
        out_shape=[jax.ShapeDtypeStruct((s, 3 * DN_W), F32)] * 2,
        compiler_params=_cparams("parallel", "parallel"))(proj, proj, conv_w)


def _conv_bwd_act(c, dq, dk, dv):
    s = c.shape[0]

    def body(c_ref, dq_ref, dk_ref, dv_ref, dc_ref):
        for part, d_ref in enumerate((dq_ref, dk_ref, dv_ref)):
            for h in range(DN_HEADS):
                sl = slice(part * DN_W + h * DN_D, part * DN_W + (h + 1) * DN_D)
                ch = c_ref[:, sl]
                dyh = d_ref[:, h * DN_D:(h + 1) * DN_D]
                if part < 2:
                    ah = _silu(ch)
                    r = lax.rsqrt(jnp.sum(ah * ah, axis=-1, keepdims=True) + NORM_EPS)
                    dyh = r * dyh - ah * (r * r * r) * jnp.sum(dyh * ah, axis=-1, keepdims=True)
                dc_ref[:, sl] = dyh * _silu_grad(ch)

    wide = pl.BlockSpec((ROW_TILE, 3 * DN_W), lambda i: (i, 0))
    row = pl.BlockSpec((ROW_TILE, DN_W), lambda i: (i, 0))
    return pl.pallas_call(
        body, name="conv_bwd_act", grid=(s // ROW_TILE,), in_specs=[wide, row, row, row], out_specs=wide,
        out_shape=jax.ShapeDtypeStruct((s, 3 * DN_W), F32), compiler_params=_cparams("parallel"))(c, dq, dk, dv)


DPROJ_IN = pl.BlockSpec(memory_space=pl.ANY)


def _conv_bwd(proj, dc, conv_w, dproj):
    s = proj.shape[0]
    tile = min(s, CONV_TILE)
    t8 = tile // 8
    nrow = s // tile
    last8 = s // 8 - 1

    def body(u_ref, dc_ref, dcn_ref, w_ref, dproj_in, du_ref, dw_ref):
        i = pl.program_id(1)
        cur = u_ref[...]
        dcv = dc_ref[...]
        next8 = jnp.where(i < nrow - 1, dcn_ref[...], 0.0)
        w = w_ref[...]

        @pl.when(i == 0)
        def _():
            dw_ref[...] = jnp.zeros_like(dw_ref)

        du = dcv * w[3:4, :]
        dw_ref[3:4, :] += jnp.sum(cur * dcv, axis=0, keepdims=True)
        for k in (1, 2, 3):
            ahead = _shift_up(dcv, next8, k)
            du = du + ahead * w[3 - k:4 - k, :]
            dw_ref[3 - k:4 - k, :] += jnp.sum(cur * ahead, axis=0, keepdims=True)
        du_ref[...] = du.astype(BF16)

    blk = pl.BlockSpec((tile, DN_W), lambda p, i: (i, p))
    return pl.pallas_call(
        body, name="conv_bwd", grid=(3, nrow),
        in_specs=[blk, blk, pl.BlockSpec((8, DN_W), lambda p, i: (jnp.minimum((i + 1) * t8, last8), p)),
                  pl.BlockSpec((4, DN_W), lambda p, i: (0, p)), DPROJ_IN],
        out_specs=[blk, pl.BlockSpec((4, DN_W), lambda p, i: (0, p))],
        out_shape=[jax.ShapeDtypeStruct((s, PW), BF16), jax.ShapeDtypeStruct((4, 3 * DN_W), F32)],
        input_output_aliases={4: 0},
        compiler_params=_cparams("parallel", "arbitrary"))(proj, dc, dc, conv_w, dproj)


def _gates_fwd(proj, gate_par):
    s = proj.shape[0]

    def body(ba_ref, par_ref, o_ref):
        v = ba_ref[...]
        lane = lax.broadcasted_iota(jnp.int32, v.shape, 1)
        beta = _sigmoid(v)
        g = -jnp.exp(par_ref[0:1, :]) * _softplus(v + par_ref[1:2, :])
        o_ref[...] = jnp.where(lane < DN_HEADS, beta, jnp.where(lane < 2 * DN_HEADS, g, 0.0))

    return pl.pallas_call(
        body, name="gates_fwd", grid=(s // ROW_TILE,),
        in_specs=[pl.BlockSpec((ROW_TILE, 128), lambda i: (i, OFF_BA // 128)), pl.BlockSpec((8, 128), lambda i: (0, 0))],
        out_specs=pl.BlockSpec((ROW_TILE, 128), lambda i: (i, 0)),
        out_shape=jax.ShapeDtypeStruct((s, 128), F32), compiler_params=_cparams("parallel"))(proj, gate_par)


def _gates_bwd(proj, gate_par, dbg, dproj):
    s = proj.shape[0]

    def body(ba_ref, par_ref, d_ref, dproj_in, o_ref, dpar_ref):
        i = pl.program_id(0)
        v = ba_ref[...]
        dv = d_ref[...]
        lane = lax.broadcasted_iota(jnp.int32, v.shape, 1)
        beta = _sigmoid(v)
        nega = -jnp.exp(par_ref[0:1, :])
        xs = v + par_ref[1:2, :]
        dsp = dv * nega * _sigmoid(xs)
        dal = dv * nega * _softplus(xs)
        is_b = lane < DN_HEADS
        is_g = jnp.logical_and(lane >= DN_HEADS, lane < 2 * DN_HEADS)
        o_ref[:, :128] = jnp.where(is_b, dv * beta * (1.0 - beta), jnp.where(is_g, dsp, 0.0)).astype(BF16)
        o_ref[:, 128:] = jnp.zeros((ROW_TILE, PW - OFF_BA - 128), BF16)
        r0 = jnp.sum(jnp.where(is_g, dal, 0.0), axis=0, keepdims=True)
        r1 = jnp.sum(jnp.where(is_g, dsp, 0.0), axis=0, keepdims=True)

        @pl.when(i == 0)
        def _():
            dpar_ref[...] = jnp.zeros_like(dpar_ref)

        dpar_ref[0:1, :] += r0
        dpar_ref[1:2, :] += r1

    return pl.pallas_call(
        body, name="gates_bwd", grid=(s // ROW_TILE,),
        in_specs=[pl.BlockSpec((ROW_TILE, 128), lambda i: (i, OFF_BA // 128)), pl.BlockSpec((8, 128), lambda i: (0, 0)),
                  pl.BlockSpec((ROW_TILE, 128), lambda i: (i, 0)), DPROJ_IN],
        out_specs=[pl.BlockSpec((ROW_TILE, PW - OFF_BA), lambda i: (i, OFF_BA // (PW - OFF_BA))),
                   pl.BlockSpec((8, 128), lambda i: (0, 0))],
        out_shape=[jax.ShapeDtypeStruct((s, PW), BF16), jax.ShapeDtypeStruct((8, 128), F32)],
        input_output_aliases={3: 0},
        compiler_params=_cparams("arbitrary"))(proj, gate_par, dbg, dproj)


def _chunk_masks():
    c = DN_CHUNK
    ii = lax.broadcasted_iota(jnp.int32, (c, c), 0)
    jj = lax.broadcasted_iota(jnp.int32, (c, c), 1)
    return dict(ii=ii, jj=jj, lower=(ii >= jj), strict=(ii > jj),
                lower_f=(ii >= jj).astype(BF16), upper_f=(ii <= jj).astype(BF16))


class _Heads:
    def __init__(self, xs):
        self.xs = list(xs)

    def _bin(self, o, f):
        if isinstance(o, _Heads):
            return _Heads([f(a, b) for a, b in zip(self.xs, o.xs)])
        return _Heads([f(a, o) for a in self.xs])

    def __add__(self, o):
        return self._bin(o, lambda a, b: a + b)

    def __sub__(self, o):
        return self._bin(o, lambda a, b: a - b)

    def __mul__(self, o):
        return self._bin(o, lambda a, b: a * b)

    __radd__ = __add__
    __rmul__ = __mul__

    def __neg__(self):
        return _Heads([-a for a in self.xs])

    def __getitem__(self, i):
        return _Heads([a[i] for a in self.xs])


def _hmap(f, *args):
    n = next(len(a.xs) for a in args if isinstance(a, _Heads))
    return _Heads([f(*[(a.xs[h] if isinstance(a, _Heads) else a) for a in args]) for h in range(n)])


def _hdot(a, b):
    return _hmap(_dot, a, b)


def _hdot_nt(a, b):
    return _hmap(_dot_nt, a, b)


def _hdot_tn(a, b):
    return _hmap(_dot_tn, a, b)


def _hcat(a, b, axis):
    return _hmap(lambda x, y: jnp.concatenate([x, y], axis=axis), a, b)


def _hsum(a, axis):
    return _hmap(lambda t: jnp.sum(t, axis=axis, keepdims=True), a)


def _hwhere(c, a, b):
    return _hmap(jnp.where, c, a, b)


def _chunk_gates(mk, bg):
    c = DN_CHUNK
    gc_all = _dot_exact_lhs(mk["lower_f"], bg)
    rows = jnp.concatenate([gc_all, gc_all], axis=0).T
    hs = range(DN_HEADS)
    return (_Heads(bg[:, h:h + 1] for h in hs), _Heads(gc_all[:, DN_HEADS + h:DN_HEADS + h + 1] for h in hs),
            _Heads(rows[DN_HEADS + h:DN_HEADS + h + 1, :] for h in hs))


def _chunk_common(mk, q, k, beta_col, gc_col, gc_r):
    c = DN_CHUNK
    lower, strict = mk["lower"], mk["strict"]
    qs = q * (DN_D ** -0.5)
    beta_b = _hmap(lambda t: jnp.broadcast_to(t, (c, DN_D)), beta_col)
    gc_b = _hmap(lambda t: jnp.broadcast_to(t, (c, DN_D)), gc_col)
    gc_sq = gc_b[:, :c]
    gam = _hwhere(lower, _hmap(lambda t: jnp.exp(jnp.minimum(t, 0.0)), gc_sq - gc_r[:, :c]), 0.0)
    egc = _hmap(jnp.exp, gc_b)
    gl = gc_b[c - 1:c, :]
    ekd = _hmap(jnp.exp, gl - gc_b)
    dl = _hmap(jnp.exp, gl)
    kb = k * beta_b
    scores = _hdot_nt(_hcat(kb, qs, 0), k)
    a_strict = _hwhere(strict, scores[:c] * gam, 0.0)
    aqk = _hwhere(lower, scores[c:] * gam, 0.0)
    return dict(k=k, qs=qs, beta_b=beta_b, gc_b=gc_b, gam=gam, egc=egc, ekd=ekd, dl=dl, kb=kb, a_strict=a_strict, aqk=aqk)


def _unit_lower_inverse_minus_eye(n_strict, ii, jj):
    same = lax.shift_right_logical(ii, 4) == lax.shift_right_logical(jj, 4)
    dmat = _hwhere(same, n_strict, 0.0)
    omat = n_strict - dmat
    d2 = _hdot(dmat, dmat)
    d4 = _hdot(d2, d2)
    d8 = _hdot(d4, d4)
    x1 = d2 - dmat - _hdot(dmat, d2)
    x2 = x1 + d4 + _hdot(x1, d4)
    x3 = x2 + d8 + _hdot(x2, d8)
    n1 = omat + _hdot(x3, omat)
    n2 = _hdot(n1, n1)
    y = n2 - n1 - _hdot(n1, n2)
    return y + x3 + _hdot(y, x3)


GDR_HEAD_SETS = (range(0, DN_HEADS),)


def _gdr_fwd(qkv, bg):
    s = qkv.shape[0]
    c = DN_CHUNK
    n = s // c

    def body(q_ref, k_ref, v_ref, bg_ref, o_ref, u_ref, w_ref, vn_ref, tm_ref, st_ref, state):
        @pl.when(pl.program_id(0) == 0)
        def _():
            state[...] = jnp.zeros_like(state)

        mk = _chunk_masks()
        gates = _chunk_gates(mk, bg_ref[...])
        for hs in GDR_HEAD_SETS:
            sls = [slice(h * DN_D, (h + 1) * DN_D) for h in hs]
            cm = _chunk_common(mk, _Heads(q_ref[:, sl] for sl in sls), _Heads(k_ref[:, sl] for sl in sls),
                               *[_Heads(g.xs[h] for h in hs) for g in gates])
            tm = _unit_lower_inverse_minus_eye(cm["a_strict"], mk["ii"], mk["jj"])
            rhs_u = _Heads(v_ref[:, sl] for sl in sls) * cm["beta_b"]
            rhs_w = cm["kb"] * cm["egc"]
            t_rhs = _hdot(tm, _hcat(rhs_u, rhs_w, 1))
            u = rhs_u + t_rhs[:, :DN_D]
            w = rhs_w + t_rhs[:, DN_D:]
            st = _Heads(state[h] for h in hs)
            on_state = _hdot(_hcat(w, cm["qs"] * cm["egc"], 0), st)
            v_new = u - on_state[:c]
            o = on_state[c:] + _hdot(cm["aqk"], v_new)
            st_new = st * cm["dl"] + _hdot_tn(cm["k"] * cm["ekd"], v_new)
            for i, (h, sl) in enumerate(zip(hs, sls)):
                o_ref[:, sl] = o.xs[i]
                u_ref[:, sl] = u.xs[i]
                w_ref[:, sl] = w.xs[i]
                vn_ref[:, sl] = v_new.xs[i]
                tm_ref[h, 0] = tm.xs[i]
                st_ref[h, 0] = st.xs[i]
                state[h] = st_new.xs[i]

    def part(p):
        return pl.BlockSpec((c, DN_W), lambda j: (j, p))

    return pl.pallas_call(
        body, name="gdr_fwd", grid=(n,),
        in_specs=[part(0), part(1), part(2), pl.BlockSpec((c, 128), lambda j: (j, 0))],
        out_specs=[part(0)] * 4 + [pl.BlockSpec((DN_HEADS, 1, c, c), lambda j: (0, j, 0, 0)),
                                   pl.BlockSpec((DN_HEADS, 1, DN_D, DN_D), lambda j: (0, j, 0, 0))],
        out_shape=[jax.ShapeDtypeStruct((s, DN_W), F32)] * 4
        + [jax.ShapeDtypeStruct((DN_HEADS, n, c, c), F32), jax.ShapeDtypeStruct((DN_HEADS, n, DN_D, DN_D), F32)],
        scratch_shapes=[pltpu.VMEM((DN_HEADS, DN_D, DN_D), F32)],
        compiler_params=_cparams("arbitrary"))(qkv, qkv, qkv, bg)


def _gdr_bwd(qkv, bg, u, w, vn, tmat, states, do):
    s = qkv.shape[0]
    c = DN_CHUNK
    n = s // c

    def body(q_ref, k_ref, v_ref, bg_ref, u_ref, w_ref, vn_ref, tm_ref, st_ref, do_ref,
             dq_ref, dk_ref, dv_ref, dbg_ref, dstate):
        @pl.when(pl.program_id(0) == 0)
        def _():
            dstate[...] = jnp.zeros_like(dstate)

        mk = _chunk_masks()
        lower, strict = mk["lower"], mk["strict"]
        bg = bg_ref[...]
        ones = jnp.ones((c, DN_D), BF16)
        rowi = lax.broadcasted_iota(jnp.int32, (c, DN_D), 0)
        lane = lax.broadcasted_iota(jnp.int32, (c, 128), 1)
        hs = range(DN_HEADS)
        sls = [slice(h * DN_D, (h + 1) * DN_D) for h in hs]

        def heads_of(ref):
            return _Heads(ref[:, sl] for sl in sls)

        cm = _chunk_common(mk, heads_of(q_ref), heads_of(k_ref), *_chunk_gates(mk, bg))
        k, qs, beta_b = cm["k"], cm["qs"], cm["beta_b"]
        gam, egc, ekd, dl, kb = cm["gam"], cm["egc"], cm["ekd"], cm["dl"], cm["kb"]
        aqk, a_strict = cm["aqk"], cm["a_strict"]
        v, uu, ww, v_new, dov = heads_of(v_ref), heads_of(u_ref), heads_of(w_ref), heads_of(vn_ref), heads_of(do_ref)
        st = _Heads(st_ref[h, 0] for h in hs)
        dsn = _Heads(dstate[h] for h in hs)
        qd = qs * egc
        kd = k * ekd

        dv_new = _hdot_tn(aqk, dov) + _hdot(kd, dsn)
        do_sv = _hdot_nt(dov, _hcat(st, v_new, 0))
        dqd = do_sv[:, :DN_D]
        daqk = _hwhere(lower, do_sv[:, DN_D:], 0.0)
        dkd = _hdot_nt(v_new, dsn)
        ddl = _hsum(_hsum(dsn * st, 1), 0)
        dw = -_hdot_nt(dv_new, st)
        ds_new = dsn * dl + _hdot_tn(_hcat(qd, -ww, 0), _hcat(dov, dv_new, 0))

        tm = _Heads(tm_ref[h, 0] for h in hs)
        tt = _hdot_tn(tm, _hcat(dv_new, dw, 1))
        dru = dv_new + tt[:, :DN_D]
        drw = dw + tt[:, DN_D:]
        dn = _hwhere(strict, -_hdot_nt(_hcat(dru, drw, 1), _hcat(uu, ww, 1)), 0.0)
        dag = dn * gam
        dqg = daqk * gam
        both = _hcat(dag, dqg, 0)
        on_k = _hdot(both, k)
        dkb = on_k[:c] + drw * egc
        dqs = on_k[c:] + dqd * egc
        dk = _hdot_tn(both, _hcat(kb, qs, 0)) + dkb * beta_b + dkd * ekd
        pmat = dn * a_strict + daqk * aqk
        tkd = _hsum(dkd * kd, -1)
        dgc = (_hsum(pmat, -1) - _hmap(_dot_tn_exact_rhs, pmat, ones) + _hsum(drw * (kb * egc), -1)
               + _hsum(dqd * qd, -1) - tkd)
        last = _hsum(tkd, 0) + ddl * dl
        dgc = dgc + _hwhere(rowi == c - 1, last, 0.0)
        dbeta = _hsum(dru * v, -1) + _hsum(dkb * k, -1)
        dq = dqs * (DN_D ** -0.5)
        dv = dru * beta_b

        dgc_all = jnp.zeros((c, 128), F32)
        dbg = jnp.zeros((c, 128), F32)
        for h, sl in zip(hs, sls):
            dq_ref[:, sl] = dq.xs[h]
            dk_ref[:, sl] = dk.xs[h]
            dv_ref[:, sl] = dv.xs[h]
            dstate[h] = ds_new.xs[h]
            dgc_all = dgc_all + jnp.where(lane == DN_HEADS + h, dgc.xs[h], 0.0)
            dbg = dbg + jnp.where(lane == h, dbeta.xs[h], 0.0)
        dbg_ref[...] = dbg + _dot_exact_lhs(mk["upper_f"], dgc_all)

    def part(p):
        return pl.BlockSpec((c, DN_W), lambda j: (n - 1 - j, p))

    vec = pl.BlockSpec((c, 128), lambda j: (n - 1 - j, 0))
    return pl.pallas_call(
        body, name="gdr_bwd", grid=(n,),
        in_specs=[part(0), part(1), part(2), vec, part(0), part(0), part(0),
                  pl.BlockSpec((DN_HEADS, 1, c, c), lambda j: (0, n - 1 - j, 0, 0)),
                  pl.BlockSpec((DN_HEADS, 1, DN_D, DN_D), lambda j: (0, n - 1 - j, 0, 0)), part(0)],
        out_specs=[part(0), part(0), part(0), vec],
        out_shape=[jax.ShapeDtypeStruct((s, DN_W), F32)] * 3 + [jax.ShapeDtypeStruct((s, 128), F32)],
        scratch_shapes=[pltpu.VMEM((DN_HEADS, DN_D, DN_D), F32)],
        compiler_params=_cparams("arbitrary"))(qkv, qkv, qkv, bg, u, w, vn, tmat, states, do)


def _gdr_out(o, proj, dnw):
    s = o.shape[0]

    def body(o_ref, z_ref, w_ref, y_ref, yt_ref):
        ov, zv, wv = o_ref[...], z_ref[...], w_ref[...]
        for h in range(DN_HEADS):
            sl = slice(h * DN_D, (h + 1) * DN_D)
            oh = ov[:, sl]
            r = lax.rsqrt(jnp.mean(oh * oh, axis=-1, keepdims=True) + NORM_EPS)
            y = (oh * r * wv) * _silu(zv[:, sl])
            y_ref[:, sl] = y.astype(BF16)
            yt_ref[sl, :] = y.T.astype(BF16)

    row = pl.BlockSpec((BIG_TILE, DN_W), lambda i: (i, 0))
    return pl.pallas_call(
        body, name="gdr_out", grid=(s // BIG_TILE,),
        in_specs=[row, pl.BlockSpec((BIG_TILE, DN_W), lambda i: (i, OFF_Z_A // DN_W)), pl.BlockSpec((1, DN_D), lambda i: (0, 0))],
        out_specs=[row, pl.BlockSpec((DN_W, BIG_TILE), lambda i: (0, i))],
        out_shape=[jax.ShapeDtypeStruct((s, DN_W), BF16), jax.ShapeDtypeStruct((DN_W, s), BF16)],
        compiler_params=_cparams("parallel"))(o, proj, dnw)


def _gdr_out_bwd(o, proj, dnw, dy):
    s = o.shape[0]

    def body(o_ref, z_ref, w_ref, dy_ref, do_ref, dz_ref, dw_ref):
        i = pl.program_id(0)
        ov, zv, wv, dyv = o_ref[...], z_ref[...], w_ref[...], dy_ref[...]
        acc = jnp.zeros((1, DN_D), F32)
        for h in range(DN_HEADS):
            sl = slice(h * DN_D, (h + 1) * DN_D)
            oh, zh, dh = ov[:, sl], zv[:, sl], dyv[:, sl]
            r = lax.rsqrt(jnp.mean(oh * oh, axis=-1, keepdims=True) + NORM_EPS)
            dn = dh * _silu(zh)
            dz_ref[:, sl] = (dh * (oh * r * wv) * _silu_grad(zh)).astype(BF16)
            acc = acc + jnp.sum(dn * oh * r, axis=0, keepdims=True)
            dnw_ = dn * wv
            do_ref[:, sl] = r * dnw_ - oh * (r * r * r) * jnp.mean(dnw_ * oh, axis=-1, keepdims=True)

        @pl.when(i == 0)
        def _():
            dw_ref[...] = acc

        @pl.when(i > 0)
        def _():
            dw_ref[...] += acc

    row = pl.BlockSpec((ROW_TILE, DN_W), lambda i: (i, 0))
    vec = pl.BlockSpec((1, DN_D), lambda i: (0, 0))
    return pl.pallas_call(
        body, name="gdr_out_bwd", grid=(s // ROW_TILE,),
        in_specs=[row, pl.BlockSpec((ROW_TILE, DN_W), lambda i: (i, OFF_Z_A // DN_W)), vec, row],
        out_specs=[row, pl.BlockSpec((ROW_TILE, DN_W), lambda i: (i, OFF_Z_A // DN_W)), vec],
        out_shape=[jax.ShapeDtypeStruct((s, DN_W), F32), jax.ShapeDtypeStruct((s, PW), BF16),
                   jax.ShapeDtypeStruct((1, DN_D), F32)],
        compiler_params=_cparams("arbitrary"))(o, proj, dnw, dy)


def _slope(group, head):
    idx = (group * DIL_HEADS + head + 1).astype(F32)
    return jnp.exp(jnp.full((1, 128), -8.0 * math.log(2.0) / (N_DIL * DIL_HEADS), F32) * idx)


def _att_scores(qb, k_cur, k_prev, slope_d, has_prev):
    iq = lax.broadcasted_iota(jnp.int32, (ATT_BLOCK, ATT_BLOCK), 0)
    jk = lax.broadcasted_iota(jnp.int32, (ATT_BLOCK, ATT_BLOCK), 1)
    dist_c = (iq - jk).astype(F32)
    s_cur = jnp.where(iq >= jk, _dot_nt(qb, k_cur) - slope_d * dist_c, NEG)
    s_prev = jnp.where(jnp.logical_and(jk >= iq, has_prev),
                       _dot_nt(qb, k_prev) - slope_d * (dist_c + float(ATT_BLOCK)), NEG)
    return s_cur, s_prev


def _att_scores_whole(qb, k, slope_d):
    n = 2 * ATT_BLOCK
    dist = lax.broadcasted_iota(jnp.int32, (n, n), 0) - lax.broadcasted_iota(jnp.int32, (n, n), 1)
    valid = jnp.logical_and(dist >= 0, dist <= ATT_BLOCK)
    return jnp.where(valid, _dot_nt(qb, k) - slope_d[:, 0:1] * dist.astype(F32), NEG)


def _att_tiles(i, dil, nb):
    tiles = nb // 2
    per = dil * tiles // ATT_UNROLL
    assert nb % 2 == 0 and tiles >= 2 and per * ATT_UNROLL == dil * tiles
    for i0 in range(per):
        ts = [divmod(i0 + u * per, tiles) for u in range(ATT_UNROLL)]
        assert all(a[0] != b[0] or abs(a[1] - b[1]) >= 2 for n, a in enumerate(ts) for b in ts[n + 1:])
    qrows, krows, has_prev = [], [], []
    for u in range(ATT_UNROLL):
        t = i + u * per
        r = lax.div(t, tiles)
        j = lax.rem(t, tiles)
        qbase = r + dil * 2 * ATT_BLOCK * j
        kbase = qbase - dil * ATT_BLOCK * jnp.minimum(j, 1)
        if dil == 1:
            qbase, kbase = pl.multiple_of(qbase, ATT_BLOCK), pl.multiple_of(kbase, ATT_BLOCK)
        qrows.append(pl.ds(qbase, 2 * ATT_BLOCK, stride=dil))
        krows.append(pl.ds(kbase, 3 * ATT_BLOCK, stride=dil))
        has_prev.append(j > 0)
    return qrows, krows, has_prev


def _att_scores_tile(qb, k, slope_d, has_prev):
    iq = lax.broadcasted_iota(jnp.int32, (2 * ATT_BLOCK, 3 * ATT_BLOCK), 0)
    ck = lax.broadcasted_iota(jnp.int32, (2 * ATT_BLOCK, 3 * ATT_BLOCK), 1)
    dist = iq - ck + jnp.where(has_prev, ATT_BLOCK, 0)
    valid = jnp.logical_and(dist >= 0, dist <= ATT_BLOCK)
    return jnp.where(valid, _dot_nt(qb, k) - slope_d[:, 0:1] * dist.astype(F32), NEG)


ATT_UNROLL = 4


def _att_blocks(i, dil, nb):
    per = dil * nb // ATT_UNROLL
    assert per * ATT_UNROLL == dil * nb
    for i0 in range(per):
        blocks = [divmod(i0 + u * per, nb) for u in range(ATT_UNROLL)]
        assert all(a[0] != b[0] or abs(a[1] - b[1]) >= 2 for n, a in enumerate(blocks) for b in blocks[n + 1:])
    curs, prvs, has_prev = [], [], []
    for u in range(ATT_UNROLL):
        t = i + u * per
        r = lax.div(t, nb)
        j = lax.rem(t, nb)
        base = r + dil * ATT_BLOCK * j
        pbase = base - dil * ATT_BLOCK * jnp.minimum(j, 1)
        if dil == 1:
            base, pbase = pl.multiple_of(base, ATT_BLOCK), pl.multiple_of(pbase, ATT_BLOCK)
        curs.append(pl.ds(base, ATT_BLOCK, stride=dil))
        prvs.append(pl.ds(pbase, ATT_BLOCK, stride=dil))
        has_prev.append(j > 0)
    return curs, prvs, has_prev


def _att_fwd(proj, group):
    s = proj.shape[0]
    dil = DIL_GROUPS[group][1]
    assert DIL_GROUPS[group][0] // dil == ATT_BLOCK
    nb = s // dil // ATT_BLOCK
    assert nb * dil * ATT_BLOCK == s

    def body(q_ref, k_ref, v_ref, o_ref, lse_ref):
        def emit(rows, num, den, mx):
            o_ref[rows, :] = num / den
            lse_ref[rows, :] = jnp.broadcast_to(mx + jnp.log(den), (num.shape[0], DIL_DH))

        slope_d = _slope(group, pl.program_id(0)) * float(dil)

        def step(i, carry):
            curs, prvs, has_prev = _att_blocks(i, dil, nb)
            us = range(ATT_UNROLL)
            qb = [q_ref[c, :] * (DIL_DH ** -0.5) for c in curs]
            sc = [_att_scores(qb[u], k_ref[curs[u], :], k_ref[prvs[u], :], slope_d, has_prev[u]) for u in us]
            mx = [jnp.maximum(jnp.max(a, axis=-1, keepdims=True), jnp.max(b, axis=-1, keepdims=True)) for a, b in sc]
            p_cur = [jnp.exp(sc[u][0] - mx[u]) for u in us]
            p_prev = [jnp.exp(sc[u][1] - mx[u]) for u in us]
            den = [jnp.sum(p_cur[u], axis=-1, keepdims=True) + jnp.sum(p_prev[u], axis=-1, keepdims=True) for u in us]
            num = [_dot(p_cur[u], v_ref[curs[u], :]) + _dot(p_prev[u], v_ref[prvs[u], :]) for u in us]
            for u in us:
                emit(curs[u], num[u], den[u], mx[u])
            return carry

        def step_whole(i, carry):
            rows = [pl.ds(i * ATT_UNROLL + u, 2 * ATT_BLOCK, stride=dil) for u in range(ATT_UNROLL)]
            sc = [_att_scores_whole(q_ref[r, :] * (DIL_DH ** -0.5), k_ref[r, :], slope_d) for r in rows]
            mx = [jnp.max(a, axis=-1, keepdims=True) for a in sc]
            p = [jnp.exp(a - m) for a, m in zip(sc, mx)]
            num = [_dot(pu, v_ref[r, :]) for pu, r in zip(p, rows)]
            for u, r in enumerate(rows):
                emit(r, num[u], jnp.sum(p[u], axis=-1, keepdims=True), mx[u])
            return carry

        def step_tile(i, carry):
            qrows, krows, has_prev = _att_tiles(i, dil, nb)
            us = range(ATT_UNROLL)
            sc = [_att_scores_tile(q_ref[qrows[u], :] * (DIL_DH ** -0.5), k_ref[krows[u], :], slope_d, has_prev[u]) for u in us]
            mx = [jnp.max(a, axis=-1, keepdims=True) for a in sc]
            p = [jnp.exp(a - m) for a, m in zip(sc, mx)]
            num = [_dot(p[u], v_ref[krows[u], :]) for u in us]
            for u in us:
                emit(qrows[u], num[u], jnp.sum(p[u], axis=-1, keepdims=True), mx[u])
            return carry

        if nb == 2:
            lax.fori_loop(0, dil // ATT_UNROLL, step_whole, 0)
        elif nb % 2 == 0:
            lax.fori_loop(0, dil * nb // 2 // ATT_UNROLL, step_tile, 0)
        else:
            lax.fori_loop(0, dil * nb // ATT_UNROLL, step, 0)

    def col(off):
        return pl.BlockSpec((s, DIL_DH), lambda h: (0, off // DIL_DH + group * DIL_HEADS + h))

    out = pl.BlockSpec((s, DIL_DH), lambda h: (0, h))
    return pl.pallas_call(
        body, name=f"att_fwd{group}", grid=(DIL_HEADS,), in_specs=[col(OFF_Q_B), col(OFF_K_B), col(OFF_V_B)],
        out_specs=[out, out], out_shape=[jax.ShapeDtypeStruct((s, DIL_W), F32)] * 2,
        compiler_params=_cparams("parallel"))(proj, proj, proj)


def _att_bwd(proj, group, do, lse, delta):
    s = proj.shape[0]
    dil = DIL_GROUPS[group][1]
    nb = s // dil // ATT_BLOCK

    def body(q_ref, k_ref, v_ref, do_ref, lse_ref, dl_ref, dq_ref, dk_ref, dv_ref, dq_acc, dk_acc, dv_acc):
        slope_d = _slope(group, pl.program_id(0)) * float(dil)
        dk_acc[...] = jnp.zeros_like(dk_acc)
        dv_acc[...] = jnp.zeros_like(dv_acc)

        def step(i, carry):
            curs, prvs, has_prev = _att_blocks(i, dil, nb)
            us = range(ATT_UNROLL)
            qb = [q_ref[c, :] * (DIL_DH ** -0.5) for c in curs]
            k_cur, k_prev = [k_ref[c, :] for c in curs], [k_ref[p, :] for p in prvs]
            v_cur, v_prev = [v_ref[c, :] for c in curs], [v_ref[p, :] for p in prvs]
            sc = [_att_scores(qb[u], k_cur[u], k_prev[u], slope_d, has_prev[u]) for u in us]
            lse_b, delta_b, dob = [lse_ref[c, :] for c in curs], [dl_ref[c, :] for c in curs], [do_ref[c, :] for c in curs]
            p_cur = [jnp.exp(sc[u][0] - lse_b[u]) for u in us]
            p_prev = [jnp.exp(sc[u][1] - lse_b[u]) for u in us]
            ds_cur = [p_cur[u] * (_dot_nt(dob[u], v_cur[u]) - delta_b[u]) for u in us]
            ds_prev = [p_prev[u] * (_dot_nt(dob[u], v_prev[u]) - delta_b[u]) for u in us]
            dq = [(_dot(ds_cur[u], k_cur[u]) + _dot(ds_prev[u], k_prev[u])) * (DIL_DH ** -0.5) for u in us]
            dk_c = [_dot_tn(ds_cur[u], qb[u]) for u in us]
            dv_c = [_dot_tn(p_cur[u], dob[u]) for u in us]
            dk_p = [_dot_tn(ds_prev[u], qb[u]) for u in us]
            dv_p = [_dot_tn(p_prev[u], dob[u]) for u in us]
            for u in us:
                dq_acc[curs[u], :] = dq[u]
                dk_acc[curs[u], :] += dk_c[u]
                dv_acc[curs[u], :] += dv_c[u]
            for u in us:
                dk_acc[prvs[u], :] += dk_p[u]
                dv_acc[prvs[u], :] += dv_p[u]
            return carry

        def step_whole(i, carry):
            rows = [pl.ds(i * ATT_UNROLL + u, 2 * ATT_BLOCK, stride=dil) for u in range(ATT_UNROLL)]
            qb = [q_ref[r, :] * (DIL_DH ** -0.5) for r in rows]
            kk, vv, dob = [k_ref[r, :] for r in rows], [v_ref[r, :] for r in rows], [do_ref[r, :] for r in rows]
            sc = [_att_scores_whole(qb[u], kk[u], slope_d) for u in range(ATT_UNROLL)]
            p = [jnp.exp(sc[u] - lse_ref[r, :][:, 0:1]) for u, r in enumerate(rows)]
            ds = [p[u] * (_dot_nt(dob[u], vv[u]) - dl_ref[r, :][:, 0:1]) for u, r in enumerate(rows)]
            dq = [_dot(ds[u], kk[u]) * (DIL_DH ** -0.5) for u in range(ATT_UNROLL)]
            dk = [_dot_tn(ds[u], qb[u]) for u in range(ATT_UNROLL)]
            dv = [_dot_tn(p[u], dob[u]) for u in range(ATT_UNROLL)]
            for u, r in enumerate(rows):
                dq_acc[r, :] = dq[u]
                dk_acc[r, :] = dk[u]
                dv_acc[r, :] = dv[u]
            return carry

        def step_tile(i, carry):
            qrows, krows, has_prev = _att_tiles(i, dil, nb)
            us = range(ATT_UNROLL)
            qb = [q_ref[r, :] * (DIL_DH ** -0.5) for r in qrows]
            kk, vv, dob = [k_ref[r, :] for r in krows], [v_ref[r, :] for r in krows], [do_ref[r, :] for r in qrows]
            sc = [_att_scores_tile(qb[u], kk[u], slope_d, has_prev[u]) for u in us]
            p = [jnp.exp(sc[u] - lse_ref[qrows[u], :][:, 0:1]) for u in us]
            ds = [p[u] * (_dot_nt(dob[u], vv[u]) - dl_ref[qrows[u], :][:, 0:1]) for u in us]
            dq = [_dot(ds[u], kk[u]) * (DIL_DH ** -0.5) for u in us]
            dk = [_dot_tn(ds[u], qb[u]) for u in us]
            dv = [_dot_tn(p[u], dob[u]) for u in us]
            for u in us:
                dq_acc[qrows[u], :] = dq[u]
                dk_acc[krows[u], :] += dk[u]
                dv_acc[krows[u], :] += dv[u]
            return carry

        if nb == 2:
            lax.fori_loop(0, dil // ATT_UNROLL, step_whole, 0)
        elif nb % 2 == 0:
            lax.fori_loop(0, dil * nb // 2 // ATT_UNROLL, step_tile, 0)
        else:
            lax.fori_loop(0, dil * nb // ATT_UNROLL, step, 0)
        dq_ref[...] = dq_acc[...].astype(BF16)
        dk_ref[...] = dk_acc[...].astype(BF16)
        dv_ref[...] = dv_acc[...].astype(BF16)

    def col(off):
        return pl.BlockSpec((s, DIL_DH), lambda h: (0, off // DIL_DH + group * DIL_HEADS + h))

    hd = pl.BlockSpec((s, DIL_DH), lambda h: (0, h))
    return pl.pallas_call(
        body, name=f"att_bwd{group}", grid=(DIL_HEADS,),
        in_specs=[col(OFF_Q_B), col(OFF_K_B), col(OFF_V_B), hd, hd, hd], out_specs=[hd, hd, hd],
        out_shape=[jax.ShapeDtypeStruct((s, DIL_W), BF16)] * 3,
        scratch_shapes=[pltpu.VMEM((s, DIL_DH), F32)] * 3,
        compiler_params=_cparams("parallel"))(proj, proj, proj, do, lse, delta)


def _att_merge(parts, proj):
    s = proj.shape[0]

    def body(o0, l0, o1, l1, o2, l2, z_ref, ob_ref, o_ref, lse_ref, obt_ref):
        m = jnp.maximum(jnp.maximum(l0[...], l1[...]), l2[...])
        num = jnp.zeros_like(m)
        den = jnp.zeros_like(m)
        for og, lg in ((o0, l0), (o1, l1), (o2, l2)):
            sc = jnp.exp(lg[...] - m)
            num = num + og[...] * sc
            den = den + sc
        o = num / den
        o_ref[...] = o
        lse_ref[...] = m + jnp.log(den)
        ob = o * _silu(z_ref[...])
        ob_ref[...] = ob.astype(BF16)
        obt_ref[...] = ob.T.astype(BF16)

    row = pl.BlockSpec((ROW_TILE, DIL_W), lambda i: (i, 0))
    flat = [a for p in parts for a in p]
    return pl.pallas_call(
        body, name="att_merge", grid=(s // ROW_TILE,),
        in_specs=[row] * 6 + [pl.BlockSpec((ROW_TILE, DIL_W), lambda i: (i, OFF_Z_B // DIL_W))],
        out_specs=[row, row, row, pl.BlockSpec((DIL_W, ROW_TILE), lambda i: (0, i))],
        out_shape=[jax.ShapeDtypeStruct((s, DIL_W), BF16), jax.ShapeDtypeStruct((s, DIL_W), F32),
                   jax.ShapeDtypeStruct((s, DIL_W), F32), jax.ShapeDtypeStruct((DIL_W, s), BF16)],
        compiler_params=_cparams("parallel"))(*flat, proj)


def _att_merge_bwd(o, proj, dob, dproj):
    s = o.shape[0]

    def body(o_ref, z_ref, d_ref, dproj_in, do_ref, dl_ref, dz_ref):
        ov, zv, dv = o_ref[...], z_ref[...], d_ref[...]
        do = dv * _silu(zv)
        do_ref[...] = do
        dz_ref[...] = (dv * ov * _silu_grad(zv)).astype(BF16)
        for h in range(DIL_HEADS):
            sl = slice(h * DIL_DH, (h + 1) * DIL_DH)
            dl_ref[:, sl] = jnp.broadcast_to(jnp.sum(do[:, sl] * ov[:, sl], axis=-1, keepdims=True), (ROW_TILE, DIL_DH))

    row = pl.BlockSpec((ROW_TILE, DIL_W), lambda i: (i, 0))
    return pl.pallas_call(
        body, name="att_merge_bwd", grid=(s // ROW_TILE,),
        in_specs=[row, pl.BlockSpec((ROW_TILE, DIL_W), lambda i: (i, OFF_Z_B // DIL_W)), row, DPROJ_IN],
        out_specs=[row, row, pl.BlockSpec((ROW_TILE, DIL_W), lambda i: (i, OFF_Z_B // DIL_W))],
        out_shape=[jax.ShapeDtypeStruct((s, DIL_W), F32), jax.ShapeDtypeStruct((s, DIL_W), F32),
                   jax.ShapeDtypeStruct((s, PW), BF16)],
        input_output_aliases={3: 2},
        compiler_params=_cparams("parallel"))(o, proj, dob, dproj)


def _merge(proj, ya, yb):
    s = proj.shape[0]

    def body(ga_ref, gb_ref, ya_ref, yb_ref, o_ref, ot_ref):
        m = _sigmoid(ga_ref[...]) * ya_ref[...] + _sigmoid(gb_ref[...]) * yb_ref[...]
        o_ref[...] = m.astype(BF16)
        ot_ref[...] = m.T.astype(BF16)

    row = pl.BlockSpec((ROW_TILE, D_MODEL), lambda i: (i, 0))
    return pl.pallas_call(
        body, name="merge", grid=(s // ROW_TILE,),
        in_specs=[pl.BlockSpec((ROW_TILE, D_MODEL), lambda i: (i, OFF_G_A // D_MODEL)),
                  pl.BlockSpec((ROW_TILE, D_MODEL), lambda i: (i, OFF_G_B // D_MODEL)), row, row],
        out_specs=[row, pl.BlockSpec((D_MODEL, ROW_TILE), lambda i: (0, i))],
        out_shape=[jax.ShapeDtypeStruct((s, D_MODEL), BF16), jax.ShapeDtypeStruct((D_MODEL, s), BF16)],
        compiler_params=_cparams("parallel"))(proj, proj, ya, yb)


def _merge_bwd(proj, ya, yb, dm):
    s = proj.shape[0]

    def body(ga_ref, gb_ref, ya_ref, yb_ref, dm_ref, dya_ref, dyb_ref, dga_ref, dgb_ref):
        dmv = dm_ref[...]
        sa, sb = _sigmoid(ga_ref[...]), _sigmoid(gb_ref[...])
        dya_ref[...] = (dmv * sa).astype(BF16)
        dyb_ref[...] = (dmv * sb).astype(BF16)
        dga_ref[...] = (dmv * ya_ref[...] * sa * (1.0 - sa)).astype(BF16)
        dgb_ref[...] = (dmv * yb_ref[...] * sb * (1.0 - sb)).astype(BF16)

    row = pl.BlockSpec((ROW_TILE, D_MODEL), lambda i: (i, 0))
    return pl.pallas_call(
        body, name="merge_bwd", grid=(s // ROW_TILE,),
        in_specs=[pl.BlockSpec((ROW_TILE, D_MODEL), lambda i: (i, OFF_G_A // D_MODEL)),
                  pl.BlockSpec((ROW_TILE, D_MODEL), lambda i: (i, OFF_G_B // D_MODEL)), row, row, row],
        out_specs=[row] * 4, out_shape=[jax.ShapeDtypeStruct((s, D_MODEL), BF16)] * 4,
        compiler_params=_cparams("parallel"))(proj, proj, ya, yb, dm)


def _final(x, t, fw, tgt):
    s, d = x.shape

    def body(x_ref, t_ref, w_ref, y_ref, dx_ref, dw_ref, l_ref):
        i = pl.program_id(0)
        x2 = x_ref[...] + t_ref[...]
        wv = w_ref[...]
        r = lax.rsqrt(jnp.mean(x2 * x2, axis=-1, keepdims=True) + NORM_EPS)
        e = x2 * r * wv - y_ref[...]
        lrow = jnp.mean(e * e, axis=-1, keepdims=True)
        lpart = jnp.broadcast_to(0.5 * jnp.sum(lrow, axis=0, keepdims=True), (1, 128))
        dy = e * (1.0 / d)
        dwp = jnp.sum(dy * x2 * r, axis=0, keepdims=True)
        dyw = dy * wv
        dx_ref[...] = r * dyw - x2 * (r * r * r) * jnp.mean(dyw * x2, axis=-1, keepdims=True)

        @pl.when(i == 0)
        def _():
            dw_ref[...] = dwp
            l_ref[...] = lpart

        @pl.when(i > 0)
        def _():
            dw_ref[...] += dwp
            l_ref[...] += lpart

    row = pl.BlockSpec((BIG_TILE, d), lambda i: (i, 0))
    vec = pl.BlockSpec((1, d), lambda i: (0, 0))
    return pl.pallas_call(
        body, name="final", grid=(s // BIG_TILE,), in_specs=[row, row, vec, row],
        out_specs=[row, vec, pl.BlockSpec((1, 128), lambda i: (0, 0))],
        out_shape=[jax.ShapeDtypeStruct((s, d), F32), jax.ShapeDtypeStruct((1, d), F32), jax.ShapeDtypeStruct((1, 128), F32)],
        compiler_params=_cparams("arbitrary"))(x, t, fw, tgt)


def _adamw(w, g, m, v, name):
    r, c = w.shape
    cap = max(8, (1 << 18) // c)
    divisors = [t for t in range(8, min(r, cap) + 1, 8) if r % t == 0]
    tr = r if r <= 8 else (max(divisors) if divisors else cap)

    def body(w_ref, g_ref, m_ref, v_ref, d_ref, nm_ref, nv_ref):
        gv = g_ref[...]
        mn = ADAM_B1 * m_ref[...] + (1.0 - ADAM_B1) * gv
        vn = ADAM_B2 * v_ref[...] + (1.0 - ADAM_B2) * (gv * gv)
        m_hat = mn / (1.0 - ADAM_B1 ** ADAM_STEP)
        v_hat = vn / (1.0 - ADAM_B2 ** ADAM_STEP)
        d_ref[...] = -ADAM_LR * (m_hat / (jnp.sqrt(v_hat) + ADAM_EPS) + ADAM_WD * w_ref[...])
        nm_ref[...] = mn
        nv_ref[...] = vn

    blk = pl.BlockSpec((tr, c), lambda i: (i, 0))
    return pl.pallas_call(
        body, name=name, grid=(pl.cdiv(r, tr),), in_specs=[blk] * 4, out_specs=[blk] * 3,
        out_shape=[jax.ShapeDtypeStruct((r, c), F32)] * 3, compiler_params=_cparams("parallel"))(w, g, m, v)


HBM_SPEC = pl.BlockSpec(memory_space=pl.ANY)


def _place():
    x, y, c = lax.axis_index("x"), lax.axis_index("y"), lax.axis_index("c")
    chips = [(1 - x, y), (x, 1 - y), (1 - x, 1 - y)]
    return x, y, c, chips


def _ag_weights(packs):
    na = len(packs)
    nsem = 8

    def body(*refs):
        p_refs, out_refs = refs[:na], refs[na:2 * na]
        send_sems, recv_sems = refs[2 * na:]
        x, y, c, _ = _place()
        me, sib, j = (x, y, c), (x, y, 1 - c), 2 * x + y
        xn, yn = (1 - x, y, c), (x, 1 - y, c)
        jx, jy, jd = 2 * (1 - x) + y, 2 * x + (1 - y), 2 * (1 - x) + (1 - y)

        def rc(a, k, src, dst, to):
            return pltpu.make_async_remote_copy(src_ref=src, dst_ref=dst, send_sem=send_sems.at[nsem * a + k],
                                                recv_sem=recv_sems.at[nsem * a + k], device_id=to, device_id_type=MESH)

        sent = []
        for a in range(na):
            mine, land = p_refs[a].at[c], out_refs[a].at[j, c]
            sent += [rc(a, 0, mine, land, xn), rc(a, 1, mine, land, yn), rc(a, 7, p_refs[a], out_refs[a].at[j], sib)]
        for cp in sent:
            cp.start()
        for a in range(na):
            half = p_refs[a].shape[1] // 2
            top, bottom = pl.ds(0, half), pl.ds(half, half)
            from_x, from_y, from_d = out_refs[a].at[jx, c], out_refs[a].at[jy, c], out_refs[a].at[jd, c]
            rc(a, 0, p_refs[a].at[c], from_x, me).wait_recv()
            later = [rc(a, 2, from_x.at[top], from_x.at[top], yn), rc(a, 4, from_x, from_x, sib)]
            for cp in later:
                cp.start()
            sent += later
            rc(a, 1, p_refs[a].at[c], from_y, me).wait_recv()
            later = [rc(a, 3, from_y.at[bottom], from_y.at[bottom], xn), rc(a, 5, from_y, from_y, sib)]
            for cp in later:
                cp.start()
            sent += later
            rc(a, 2, from_d.at[top], from_d.at[top], me).wait_recv()
            rc(a, 3, from_d.at[bottom], from_d.at[bottom], me).wait_recv()
            cp = rc(a, 6, from_d, from_d, sib)
            cp.start()
            sent.append(cp)
        for a in range(na):
            for k, jj in ((4, jx), (5, jy), (6, jd)):
                rc(a, k, p_refs[a].at[c], out_refs[a].at[jj, 1 - c], me).wait_recv()
            rc(a, 7, p_refs[a], out_refs[a].at[j], me).wait_recv()
        for cp in sent:
            cp.wait_send()

    return pl.pallas_call(
        body, name="ag_weights",
        out_shape=[jax.ShapeDtypeStruct((N_CHIPS,) + p.shape, p.dtype) for p in packs],
        in_specs=[HBM_SPEC] * na, out_specs=[HBM_SPEC] * na,
        scratch_shapes=[pltpu.SemaphoreType.DMA((nsem * na,)), pltpu.SemaphoreType.DMA((nsem * na,))])(*packs)


def _rs_pair(dwpt, gpack):
    n = N_CHIPS
    hw = SHARD_PAD // 2

    def body(d_ref, g_ref, out_d, out_g, send_sems, recv_sems):
        x, y, c, _ = _place()
        sib = (x, y, 1 - c)
        cps = []
        for p in range(n):
            start = pl.multiple_of(WIN_BASE[p] + (1 - c) * hw, TILE_ROWS)
            cps.append(pltpu.make_async_remote_copy(
                src_ref=d_ref.at[pl.ds(start, hw)], dst_ref=out_d.at[p], send_sem=send_sems.at[p],
                recv_sem=recv_sems.at[p], device_id=sib, device_id_type=MESH))
            cps.append(pltpu.make_async_remote_copy(
                src_ref=g_ref.at[p, 1 - c], dst_ref=out_g.at[p], send_sem=send_sems.at[n + p],
                recv_sem=recv_sems.at[n + p], device_id=sib, device_id_type=MESH))
        for cp in cps:
            cp.start()
        for cp in cps:
            cp.wait_recv()
        for cp in cps:
            cp.wait_send()

    return pl.pallas_call(
        body, name="rs_pair",
        out_shape=[jax.ShapeDtypeStruct((n, hw, dwpt.shape[1]), dwpt.dtype),
                   jax.ShapeDtypeStruct((n,) + gpack.shape[2:], gpack.dtype)],
        in_specs=[HBM_SPEC] * 2, out_specs=[HBM_SPEC] * 2,
        scratch_shapes=[pltpu.SemaphoreType.DMA((2 * n,)), pltpu.SemaphoreType.DMA((2 * n,))])(dwpt, gpack)


def _add_halves_win(dwpt, other, c):
    n, rh, wd = other.shape
    tr = _row_tile(rh)

    def body(s_ref, d_ref, o_ref, out_ref):
        out_ref[0] = (d_ref[...] + o_ref[0]).astype(BF16)

    scal = jnp.concatenate([jnp.reshape(c, (1,)).astype(jnp.int32), jnp.asarray(WIN_BASE, jnp.int32)])
    grid_spec = pltpu.PrefetchScalarGridSpec(
        num_scalar_prefetch=1, grid=(n, rh // tr),
        in_specs=[pl.BlockSpec((pl.Element(tr), pl.Element(wd)),
                               lambda p, i, sr: (pl.multiple_of(sr[1 + p] + sr[0] * rh + i * tr, TILE_ROWS), 0)),
                  pl.BlockSpec((1, tr, wd), lambda p, i, sr: (p, i, 0))],
        out_specs=pl.BlockSpec((1, tr, wd), lambda p, i, sr: (p, i, 0)))
    return pl.pallas_call(
        body, name="add_halves_in", grid_spec=grid_spec, out_shape=jax.ShapeDtypeStruct((n, rh, wd), BF16),
        compiler_params=_cparams("parallel", "parallel"))(scal, dwpt, other)


SEM_SPEC = pl.BlockSpec(memory_space=pltpu.SEMAPHORE)
DATAFLOW_EFFECT = pltpu.SideEffectType.DATAFLOW_SIDE_EFFECTING


def _rs_chips_start(csums):
    na = len(csums)

    def body(*refs):
        s_refs, land_refs = refs[:na], refs[na:2 * na]
        send_sems, recv_sems = refs[2 * na], refs[2 * na + 1]
        token = refs[-1]
        x, y, c, chips = _place()
        j = 2 * x + y
        for a in range(na):
            for k, (cx, cy) in enumerate(chips):
                pltpu.make_async_remote_copy(src_ref=s_refs[a].at[2 * cx + cy], dst_ref=land_refs[a].at[j],
                                             send_sem=send_sems.at[3 * a + k], recv_sem=recv_sems.at[3 * a + k],
                                             device_id=(cx, cy, c), device_id_type=MESH).start()
        token[...] = jnp.zeros_like(token)

    hbm = [pltpu.HBM(s.shape, s.dtype) for s in csums]
    args = [pltpu.with_memory_space_constraint(s, pltpu.HBM) for s in csums]
    args += [pltpu.with_memory_space_constraint(lax.empty(s.shape, s.dtype), pltpu.HBM) for s in csums]
    res = pl.pallas_call(
        body, name="rs_chips_start",
        out_shape=(pltpu.SemaphoreType.DMA((3 * na,)), pltpu.SemaphoreType.DMA((3 * na,)), *hbm, *hbm,
                   jax.ShapeDtypeStruct((8, 128), F32)),
        in_specs=[pl.BlockSpec(memory_space=pltpu.HBM)] * (2 * na),
        out_specs=(SEM_SPEC, SEM_SPEC, *[pl.BlockSpec(memory_space=pltpu.HBM)] * (2 * na),
                   pl.BlockSpec(memory_space=pltpu.VMEM)),
        input_output_aliases={i: 2 + i for i in range(2 * na)},
        compiler_params=pltpu.CompilerParams(has_side_effects=DATAFLOW_EFFECT))(*args)
    return res[0], res[1], list(res[2:2 + na]), list(res[2 + na:2 + 2 * na]), res[-1]


def _rs_chips_wait(send_sems, recv_sems, csums, lands, after):
    na = len(csums)

    def body(*refs):
        s_refs, land_refs = refs[:na], refs[na:2 * na]
        send_sems, recv_sems = refs[2 * na], refs[2 * na + 1]
        x, y, c, chips = _place()
        j = 2 * x + y
        for a in range(na):
            for k, (cx, cy) in enumerate(chips):
                cp = pltpu.make_async_remote_copy(src_ref=s_refs[a].at[2 * cx + cy], dst_ref=land_refs[a].at[2 * cx + cy],
                                                  send_sem=send_sems.at[3 * a + k], recv_sem=recv_sems.at[3 * a + k],
                                                  device_id=(cx, cy, c), device_id_type=MESH)
                cp.wait_send()
                cp.wait_recv()

    hbm = [pltpu.HBM(s.shape, s.dtype) for s in csums]
    res = pl.pallas_call(
        body, name="rs_chips_wait", out_shape=(*hbm, *hbm),
        in_specs=[pl.BlockSpec(memory_space=pltpu.HBM)] * (2 * na) + [SEM_SPEC, SEM_SPEC, pl.BlockSpec(memory_space=pl.ANY)],
        out_specs=tuple([pl.BlockSpec(memory_space=pltpu.HBM)] * (2 * na)),
        input_output_aliases={i: i for i in range(2 * na)},
        compiler_params=pltpu.CompilerParams(has_side_effects=DATAFLOW_EFFECT))(*csums, *lands, send_sems, recv_sems, after)
    return list(res[:na]), list(res[na:])


SWAP_CHUNKS = 4


def _pair_swap(halves):
    na = len(halves)

    def body(*refs):
        h_refs, out_refs = refs[:na], refs[na:2 * na]
        send_sems, recv_sems = refs[2 * na:]
        x, y, c, _ = _place()
        cps = []
        for a in range(na):
            rows = h_refs[a].shape[0] // SWAP_CHUNKS
            assert rows * SWAP_CHUNKS == h_refs[a].shape[0]
            for q in range(SWAP_CHUNKS):
                k = SWAP_CHUNKS * a + q
                cps.append(pltpu.make_async_remote_copy(
                    src_ref=h_refs[a].at[pl.ds(q * rows, rows)], dst_ref=out_refs[a].at[pl.ds(q * rows, rows)],
                    send_sem=send_sems.at[k], recv_sem=recv_sems.at[k], device_id=(x, y, 1 - c), device_id_type=MESH))
        for cp in cps:
            cp.start()
        for cp in cps:
            cp.wait_recv()
        for cp in cps:
            cp.wait_send()

    return pl.pallas_call(
        body, name="pair_swap", out_shape=[jax.ShapeDtypeStruct(h.shape, h.dtype) for h in halves],
        in_specs=[HBM_SPEC] * na, out_specs=[HBM_SPEC] * na,
        scratch_shapes=[pltpu.SemaphoreType.DMA((SWAP_CHUNKS * na,)), pltpu.SemaphoreType.DMA((SWAP_CHUNKS * na,))])(*halves)


def _ag_small(v):
    m_per, n = v.shape

    def body(x_ref, out_ref, send_sems, recv_sems, local_sem):
        x, y, c, chips = _place()
        me, sibling = (x, y, c), (x, y, 1 - c)

        def rows(px, py, pc):
            return out_ref.at[pl.ds((4 * px + 2 * py + pc) * m_per, m_per), :]

        def copy(k, block, to, src=None):
            return pltpu.make_async_remote_copy(
                src_ref=rows(*block) if src is None else src, dst_ref=rows(*block), send_sem=send_sems.at[k],
                recv_sem=recv_sems.at[k], device_id=to, device_id_type=MESH)

        mine = pltpu.make_async_copy(x_ref, rows(*me), local_sem)
        mine.start()
        first = [copy(0, me, sibling, src=x_ref)]
        first += [copy(1 + k, me, (*chip, c), src=x_ref) for k, chip in enumerate(chips)]
        for cp in first:
            cp.start()
        passed = [copy(4 + k, (*chip, c), sibling) for k, chip in enumerate(chips)]
        for k, chip in enumerate(chips):
            copy(1 + k, (*chip, c), me).wait_recv()
            passed[k].start()
        copy(0, sibling, me).wait_recv()
        for k, chip in enumerate(chips):
            copy(4 + k, (*chip, 1 - c), me).wait_recv()
        for cp in first + passed:
            cp.wait_send()
        mine.wait()

    return pl.pallas_call(
        body, name="ag_small", out_shape=jax.ShapeDtypeStruct((8 * m_per, n), v.dtype),
        in_specs=[pl.BlockSpec(memory_space=pltpu.VMEM)], out_specs=pl.BlockSpec(memory_space=pltpu.VMEM),
        scratch_shapes=[pltpu.SemaphoreType.DMA((7,)), pltpu.SemaphoreType.DMA((7,)), pltpu.SemaphoreType.DMA])(v)


def _sum_blocks(a, nblk, name):
    rows, wd = a.shape
    r = rows // nblk
    tr = min(r, ROW_TILE)
    assert r % tr == 0

    def body(*refs):
        acc = refs[0][...].astype(F32)
        for ref in refs[1:nblk]:
            acc = acc + ref[...].astype(F32)
        refs[nblk][...] = acc

    nt = r // tr
    return pl.pallas_call(
        body, name=name, grid=(nt,),
        in_specs=[pl.BlockSpec((tr, wd), functools.partial(lambda i, b: (b * nt + i, 0), b=b)) for b in range(nblk)],
        out_specs=pl.BlockSpec((tr, wd), lambda i: (i, 0)),
        out_shape=jax.ShapeDtypeStruct((r, wd), F32), compiler_params=_cparams("parallel"))(*([a] * nblk))


def _row_tile(rows):
    best = max(t for t in range(16, 513, 16) if rows % t == 0)
    return best


def _sum_chips(by_src, csum, j, name):
    n, rh, wd = by_src.shape
    tr = _row_tile(rh)

    def body(j_ref, *refs):
        own = refs[n][0].astype(F32)
        acc = None
        for k in range(n):
            term = jnp.where(j_ref[0] == k, own, refs[k][0].astype(F32))
            acc = term if acc is None else acc + term
        refs[n + 1][...] = acc

    def other(k):
        return pl.BlockSpec((1, tr, wd), lambda i, jr: (jnp.where(jr[0] == k, (k + 1) % n, k), i, 0))

    grid_spec = pltpu.PrefetchScalarGridSpec(
        num_scalar_prefetch=1, grid=(rh // tr,),
        in_specs=[other(k) for k in range(n)] + [pl.BlockSpec((1, tr, wd), lambda i, jr: (jr[0], i, 0))],
        out_specs=pl.BlockSpec((tr, wd), lambda i, jr: (i, 0)))
    return pl.pallas_call(
        body, name=name, grid_spec=grid_spec, out_shape=jax.ShapeDtypeStruct((rh, wd), F32),
        compiler_params=_cparams("parallel"))(jnp.reshape(j, (1,)).astype(jnp.int32), *([by_src] * n), csum)


def _add_halves(gpack, other, c, name):
    n, _, rh, wd = gpack.shape
    tr = _row_tile(rh)

    def body(c_ref, g_ref, o_ref, out_ref):
        out_ref[0] = (g_ref[0, 0] + o_ref[0]).astype(BF16)

    grid_spec = pltpu.PrefetchScalarGridSpec(
        num_scalar_prefetch=1, grid=(n, rh // tr),
        in_specs=[pl.BlockSpec((1, 1, tr, wd), lambda p, i, cr: (p, cr[0], i, 0)),
                  pl.BlockSpec((1, tr, wd), lambda p, i, cr: (p, i, 0))],
        out_specs=pl.BlockSpec((1, tr, wd), lambda p, i, cr: (p, i, 0)))
    return pl.pallas_call(
        body, name=name, grid_spec=grid_spec, out_shape=jax.ShapeDtypeStruct((n, rh, wd), BF16),
        compiler_params=_cparams("parallel", "parallel"))(jnp.reshape(c, (1,)).astype(jnp.int32), gpack, other)


PACK_W = 1024
ROWS_O_DN = DN_W // N_CHIPS
ROWS_O_DIL = DIL_W * (D_MODEL // N_CHIPS) // PACK_W
ROWS_OUT = D_MODEL // N_CHIPS
ROWS_CONV = 4 * (3 * DN_W // N_CHIPS) // PACK_W
R1 = ROWS_O_DN
R2 = R1 + ROWS_O_DIL
R3 = R2 + ROWS_OUT
R4 = R3 + 16
R5 = R4 + 16
PACK_ROWS = 704
HALF_ROWS = PACK_ROWS // 2
SHARD_PAD = 2880


R6 = R5 + 2 * DN_HEADS

TILE_ROWS = 16
BA_IN_SHARD1 = REF_OFF_BA - SHARD_W
LOCAL_START = (0, SHARD_W, 2 * SHARD_W - 2 * DN_HEADS, 3 * SHARD_W - 2 * DN_HEADS)
LOCAL_END = LOCAL_START[1:] + (OFF_BA,)
WIN_BASE = tuple(s // TILE_ROWS * TILE_ROWS for s in LOCAL_START)


def _to_window(k, shard):
    nba = 2 * DN_HEADS
    body = shard
    if k == 1:
        row = lax.broadcasted_iota(jnp.int32, (SHARD_W - nba, 1), 0)
        body = jnp.where(row < BA_IN_SHARD1, shard[:SHARD_W - nba], shard[nba:])
    lead = LOCAL_START[k] - WIN_BASE[k]
    return jnp.pad(body, ((lead, SHARD_PAD - lead - body.shape[0]), (0, 0)))


def _from_window(k, win, ba):
    nba = 2 * DN_HEADS
    lead = LOCAL_START[k] - WIN_BASE[k]
    if k != 1:
        return win[lead:lead + SHARD_W]
    row = lax.broadcasted_iota(jnp.int32, (SHARD_W, 1), 0)
    before = win[lead:lead + SHARD_W]
    after = jnp.pad(win, ((nba, 0), (0, 0)))[lead:lead + SHARD_W]
    mid = jnp.pad(ba, ((BA_IN_SHARD1, SHARD_W - BA_IN_SHARD1 - nba), (0, 0)))
    return jnp.where(row < BA_IN_SHARD1, before, jnp.where(row < BA_IN_SHARD1 + nba, mid, after))


def _stack_windows(wins, ba):
    pieces = []
    for k in range(N_CHIPS):
        lo = WIN_BASE[k] + (TILE_ROWS if k else 0)
        hi = LOCAL_END[k] // TILE_ROWS * TILE_ROWS
        pieces.append(wins[k][lo - WIN_BASE[k]:hi - WIN_BASE[k]])
        if k + 1 < N_CHIPS:
            assert hi == WIN_BASE[k + 1]
            pieces.append(wins[k][hi - WIN_BASE[k]:hi - WIN_BASE[k] + TILE_ROWS] + wins[k + 1][:TILE_ROWS])
    pieces += [ba, jnp.zeros((PW - OFF_BA - ba.shape[0], ba.shape[1]), ba.dtype)]
    out = jnp.concatenate(pieces, axis=0)
    assert out.shape[0] == PW
    return out


def _local_step(x, tgt, norm_w, wpt, conv_full, a_log, dt_bias, dn_norm_w, w_o_dn, w_o_dil, w_out, final_norm_w):
    s = x.shape[0]
    h, h_t = _rms_in(x, norm_w)
    proj = _matmul(h, wpt, F32, 2048, 1280, 1024, "proj", nt=True)
    c_pre, qkv = _conv_fwd(proj, conv_full)
    gate_par = jnp.zeros((8, 128), F32).at[0, 8:16].set(a_log[0]).at[1, 8:16].set(dt_bias[0])
    bg = _gates_fwd(proj, gate_par)
    o_a, u, w, vn, tmat, states = _gdr_fwd(qkv, bg)
    oa2, oa2_t = _gdr_out(o_a, proj, dn_norm_w)
    ya = _matmul(oa2, w_o_dn, F32, 1024, 1024, 1024, "ya")
    parts = [_att_fwd(proj, g) for g in range(N_DIL)]
    ob, o_att, lse, ob_t = _att_merge(parts, proj)
    yb = _matmul(ob, w_o_dil, F32, 1024, 1024, 512, "yb")
    mg, mg_t = _merge(proj, ya, yb)
    t = _matmul(mg, w_out, F32, 1024, 1024, 1024, "t_out")
    dx2, dfw, lpart = _final(x, t, final_norm_w, tgt)

    dmg = _matmul(dx2, w_out, F32, 1024, 1024, 1024, "d_merged", nt=True)
    dw_out = _matmul(mg_t, dx2, F32, 1024, 1024, 1024, "dw_out")
    dya, dyb, dga, dgb = _merge_bwd(proj, ya, yb, dmg)
    doa2 = _matmul(dya, w_o_dn, F32, 1024, 1024, 1024, "d_oa2", nt=True)
    dw_o_dn = _matmul(oa2_t, dya, F32, 1024, 1024, 1024, "dw_o_dn")
    dob = _matmul(dyb, w_o_dil, F32, 1024, 512, 1024, "d_ob", nt=True)
    dw_o_dil = _matmul(ob_t, dyb, F32, 512, 1024, 1024, "dw_o_dil")
    do_a, dproj, ddnw = _gdr_out_bwd(o_a, proj, dn_norm_w, doa2)
    dq_a, dk_a, dv_a, dbg = _gdr_bwd(qkv, bg, u, w, vn, tmat, states, do_a)
    dproj, dpar = _gates_bwd(proj, gate_par, dbg, dproj)
    dc = _conv_bwd_act(c_pre, dq_a, dk_a, dv_a)
    dproj, dconv = _conv_bwd(proj, dc, conv_full, dproj)
    do_att, delta, dproj = _att_merge_bwd(o_att, proj, dob, dproj)
    dqkv_b = [_att_bwd(proj, g, do_att, lse, delta) for g in range(N_DIL)]
    pieces = [(OFF_Q_B + (N_DIL * i + g) * DIL_W, dqkv_b[g][i]) for i in range(3) for g in range(N_DIL)]
    for off, piece in pieces + [(OFF_G_A, dga), (OFF_G_B, dgb)]:
        dproj = lax.dynamic_update_slice(dproj, piece, (0, off))
    dwpt, dwpt_b = _matmul(h_t, dproj, F32, 1024, 1280, 2048, "dw_in", transpose_out=True, also_bf16=True)

    def finish(after=None):
        dh = _matmul(dproj, wpt, F32, 1024, 1024, 3840, "d_h", after=after)
        grad_x, dnw = _rms_in_bwd(x, norm_w, dh, dx2)
        small = jnp.zeros((8, PACK_W), F32)
        small = small.at[0].set(dnw[0]).at[1].set(dfw[0]).at[2, :DN_D].set(ddnw[0])
        small = small.at[3, :DN_HEADS].set(dpar[0, 8:16]).at[3, DN_HEADS:2 * DN_HEADS].set(dpar[1, 8:16])
        small = small.at[4, 0].set(lpart[0, 0])
        return grad_x, small

    return finish, (dwpt, dwpt_b), dconv, dw_o_dn, dw_o_dil, dw_out


def kernel(x, norm_w, w_in, conv_w, a_log, dt_bias, dn_norm_w, w_o_dn, w_o_dil, w_out, final_norm_w, loss_target, m_norm_w, m_w_in, m_conv_w, m_a_log, m_dt_bias, m_dn_norm_w, m_w_o_dn, m_w_o_dil, m_w_out, m_final_norm_w, v_norm_w, v_w_in, v_conv_w, v_a_log, v_dt_bias, v_dn_norm_w, v_w_o_dn, v_w_o_dil, v_w_out, v_final_norm_w):
    c = lax.axis_index("c")
    j = 2 * lax.axis_index("x") + lax.axis_index("y")
    qw = D_MODEL // N_CHIPS

    cw = conv_w[0].reshape(ROWS_CONV, PACK_W)
    cw = jnp.pad(cw, ((0, 16 - ROWS_CONV), (0, 0)))
    cw_hi = cw.astype(BF16)
    cw_lo = (cw - cw_hi.astype(F32)).astype(BF16)
    shard = w_in[0].T.astype(BF16)
    own_ba = jnp.where(j == 1, shard[BA_IN_SHARD1:BA_IN_SHARD1 + 2 * DN_HEADS], jnp.zeros((2 * DN_HEADS, D_MODEL), BF16))
    pack = jnp.concatenate(
        [w_o_dn[0].astype(BF16), w_o_dil[0].astype(BF16).reshape(ROWS_O_DIL, PACK_W), w_out[0].astype(BF16), cw_hi, cw_lo,
         own_ba, jnp.zeros((PACK_ROWS - R6, PACK_W), BF16)], axis=0).reshape(2, HALF_ROWS, PACK_W)
    chips = range(N_CHIPS)
    own_win = lax.switch(j, [functools.partial(_to_window, k) for k in chips], shard).reshape(2, SHARD_PAD // 2, D_MODEL)
    all_in, allw = _ag_weights([own_win, pack])
    wins = [all_in[k].reshape(SHARD_PAD, D_MODEL) for k in chips]
    allw = [allw[k].reshape(PACK_ROWS, PACK_W) for k in chips]
    wpt = _stack_windows(wins, allw[1][R5:R6])
    w_o_dn_full = jnp.concatenate([allw[k][:R1] for k in chips], axis=0)
    w_o_dil_full = jnp.concatenate([allw[k][R1:R2].reshape(DIL_W, qw) for k in chips], axis=1)
    w_out_full = jnp.concatenate([allw[k][R2:R3] for k in chips], axis=0)
    conv_full = jnp.concatenate(
        [(allw[k][R3:R3 + ROWS_CONV].astype(F32) + allw[k][R4:R4 + ROWS_CONV].astype(F32)).reshape(4, 3 * DN_W // N_CHIPS)
         for k in chips], axis=1)

    finish, (dwpt, dwpt_b), dconv, dw_o_dn, dw_o_dil, dw_out = _local_step(
        x[0], loss_target[0], norm_w, wpt, conv_full, a_log, dt_bias, dn_norm_w, w_o_dn_full, w_o_dil_full, w_out_full,
        final_norm_w.reshape(1, D_MODEL))

    cq = 3 * DN_W // N_CHIPS
    gpack = jnp.stack([
        jnp.concatenate(
            [dw_o_dn[k * qw:(k + 1) * qw], dw_o_dil[:, k * qw:(k + 1) * qw].reshape(ROWS_O_DIL, PACK_W),
             dw_out[k * qw:(k + 1) * qw],
             jnp.pad(dconv[:, k * cq:(k + 1) * cq].reshape(ROWS_CONV, PACK_W), ((0, 16 - ROWS_CONV), (0, 0))),
             dwpt[OFF_BA:OFF_BA + 2 * DN_HEADS] if k == 1 else jnp.zeros((2 * DN_HEADS, PACK_W), F32),
             jnp.zeros((PACK_ROWS - R4 - 2 * DN_HEADS, PACK_W), F32)], axis=0)
        for k in chips]).reshape(N_CHIPS, 2, HALF_ROWS, PACK_W)
    sib_in, sib_pack = _rs_pair(dwpt_b, gpack)
    csum_in = _add_halves_win(dwpt, sib_in, c)
    csum_pack = _add_halves(gpack, sib_pack, c, "add_halves_pack")
    send_sems, recv_sems, csums, lands, token = _rs_chips_start([csum_in, csum_pack])
    grad_x, small = finish(after=token)

    gs = _sum_blocks(_ag_small(small), 8, "sum_small")
    loss = gs[4, 0]
    w_small = jnp.zeros((8, PACK_W), F32)

    def pack_small(nw, fw, dnw_, al, db):
        t = w_small.at[0].set(nw[0]).at[1].set(fw).at[2, :DN_D].set(dnw_[0])
        return t.at[3, :DN_HEADS].set(al[0]).at[3, DN_HEADS:2 * DN_HEADS].set(db[0])

    sm = _adamw(pack_small(norm_w, final_norm_w, dn_norm_w, a_log, dt_bias), gs,
                pack_small(m_norm_w, m_final_norm_w, m_dn_norm_w, m_a_log, m_dt_bias),
                pack_small(v_norm_w, v_final_norm_w, v_dn_norm_w, v_a_log, v_dt_bias), "adamw_small")

    (csum_in, csum_pack), (src_in, src_pack) = _rs_chips_wait(send_sems, recv_sems, csums, lands, sm[0])
    half_in = _sum_chips(src_in, csum_in, j, "sum_chips_in")
    half_pack = _sum_chips(src_pack, csum_pack, j, "sum_chips_pack")
    sib_half_in, sib_half_pack = _pair_swap([half_in, half_pack])

    def both_halves(mine, theirs):
        return jnp.where(c == 0, jnp.concatenate([mine, theirs], axis=0), jnp.concatenate([theirs, mine], axis=0))

    g = both_halves(half_pack, sib_half_pack)
    g_w_in = lax.switch(j, [functools.partial(_from_window, k) for k in chips], both_halves(half_in, sib_half_in),
                        g[R4:R4 + 2 * DN_HEADS])
    g_w_o_dn = g[:R1]
    g_w_o_dil = g[R1:R2].reshape(DIL_W, qw)
    g_w_out = g[R2:R3]
    g_conv = g[R3:R3 + ROWS_CONV].reshape(4, cq)

    def unpack_small(t):
        return dict(norm_w=t[0:1], final_norm_w=t[1], dn_norm_w=t[2:3, :DN_D], a_log=t[3:4, :DN_HEADS],
                    dt_bias=t[3:4, DN_HEADS:2 * DN_HEADS])

    res = {"grad": unpack_small(gs)}
    for kind, arr in zip(("delta", "new_m", "new_v"), sm):
        res[kind] = unpack_small(arr)
    big = dict(conv_w=(conv_w, g_conv, m_conv_w, v_conv_w), w_o_dn=(w_o_dn, g_w_o_dn, m_w_o_dn, v_w_o_dn),
               w_o_dil=(w_o_dil, g_w_o_dil, m_w_o_dil, v_w_o_dil), w_out=(w_out, g_w_out, m_w_out, v_w_out))
    for name, (wt, gt, mt, vt) in big.items():
        d, nm, nv = _adamw(wt[0], gt, mt[0], vt[0], "adamw_" + name)
        res["grad"][name] = gt[None]
        res["delta"][name], res["new_m"][name], res["new_v"][name] = d[None], nm[None], nv[None]

    d, nm, nv = _adamw(w_in[0].T, g_w_in, m_w_in[0].T, v_w_in[0].T, "adamw_w_in")
    res["grad"]["w_in"] = g_w_in.T[None]
    res["delta"]["w_in"], res["new_m"]["w_in"], res["new_v"]["w_in"] = d.T[None], nm.T[None], nv.T[None]
    order = ["norm_w", "w_in", "conv_w", "a_log", "dt_bias", "dn_norm_w", "w_o_dn", "w_o_dil", "w_out", "final_norm_w"]
    outs = [loss, grad_x[None]]
    for kind in ("grad", "delta", "new_m", "new_v"):
        outs += [res[kind][nm] for nm in order]
    return tuple(outs)
```

```python
import functools
import math

import jax
import jax.numpy as jnp
from jax import lax
from jax.experimental import pallas as pl
from jax.experimental.pallas import tpu as pltpu

F32 = jnp.float32
BF16 = jnp.bfloat16
MESH = pl.DeviceIdType.MESH

D_MODEL = 1024
DN_HEADS = 8
DN_D = 128
DN_CHUNK = 64
DN_W = DN_HEADS * DN_D
DIL_GROUPS = ((128, 1), (512, 4), (2048, 16))
N_DIL = len(DIL_GROUPS)
DIL_HEADS = 4
DIL_DH = 128
DIL_W = DIL_HEADS * DIL_DH
ATT_BLOCK = 128
NORM_EPS = 1e-6
PROJ_W = 11280
N_CHIPS = 4
SHARD_W = PROJ_W // N_CHIPS

OFF_QKV_A = 0
OFF_Z_A = 3072
OFF_Q_B = 4096
OFF_K_B = 5632
OFF_V_B = 7168
OFF_Z_B = 8704
OFF_G_A = 9216
OFF_G_B = 10240
OFF_BA = 11264
PW = 11520
REF_OFF_BA = 4096

ADAM_LR = 0.001
ADAM_B1 = 0.9
ADAM_B2 = 0.999
ADAM_EPS = 1e-08
ADAM_WD = 0.01
ADAM_STEP = 10

ROW_TILE = 512
CONV_TILE = 1024
BIG_TILE = 1024
NEG = -1e30


def _dot(a, b):
    return jnp.dot(a.astype(BF16), b.astype(BF16), preferred_element_type=F32)


def _dot_nt(a, b):
    return lax.dot_general(a.astype(BF16), b.astype(BF16), (((1,), (1,)), ((), ())), preferred_element_type=F32)


def _dot_tn(a, b):
    return lax.dot_general(a.astype(BF16), b.astype(BF16), (((0,), (0,)), ((), ())), preferred_element_type=F32)


def _split(a):
    hi = a.astype(BF16)
    lo = (a - hi.astype(F32)).astype(BF16)
    return hi, lo


def _dot_exact_lhs(c, a):
    hi, lo = _split(a)
    cb = c.astype(BF16)
    return jnp.dot(cb, hi, preferred_element_type=F32) + jnp.dot(cb, lo, preferred_element_type=F32)


def _dot_tn_exact_rhs(a, c):
    hi, lo = _split(a)
    cb = c.astype(BF16)
    dn = (((0,), (0,)), ((), ()))
    return (lax.dot_general(hi, cb, dn, preferred_element_type=F32)
            + lax.dot_general(lo, cb, dn, preferred_element_type=F32))


def _sigmoid(x):
    return 1.0 / (1.0 + jnp.exp(-x))


def _silu(x):
    return x * _sigmoid(x)


def _silu_grad(x):
    s = _sigmoid(x)
    return s * (1.0 + x * (1.0 - s))


def _softplus(x):
    return jnp.maximum(x, 0.0) + jnp.log(1.0 + jnp.exp(-jnp.abs(x)))


def _cparams(*sem):
    return pltpu.CompilerParams(dimension_semantics=sem)


def _matmul(a, b, out_dtype, tm, tn, tk, name, nt=False, transpose_out=False, after=None, also_bf16=False):
    m, kdim = a.shape
    n = b.shape[0] if nt else b.shape[1]
    tm, tn, tk = min(tm, m), min(tn, n), min(tk, kdim)
    assert m % tm == 0 and n % tn == 0 and kdim % tk == 0, (name, a.shape, b.shape, tm, tn, tk)
    nk = kdim // tk
    dot = _dot_nt if nt else _dot
    b_spec = (pl.BlockSpec((tn, tk), lambda i, j, k: (j, k)) if nt else pl.BlockSpec((tk, tn), lambda i, j, k: (k, j)))
    extra = [] if after is None else [after]
    out_dtypes = [out_dtype] + ([BF16] if also_bf16 else [])

    def emit(o_refs, acc):
        val = acc.T if transpose_out else acc
        for o_ref in o_refs:
            o_ref[...] = val.astype(o_ref.dtype)

    def outs_of(rest):
        return rest[len(extra):len(extra) + len(out_dtypes)]

    if nk == 1:
        def body(a_ref, b_ref, *rest):
            emit(outs_of(rest), dot(a_ref[...], b_ref[...]))
        scratch = []
    else:
        def body(a_ref, b_ref, *rest):
            o_ref, acc_ref = outs_of(rest), rest[-1]
            k = pl.program_id(2)
            p = dot(a_ref[...], b_ref[...])

            @pl.when(k == 0)
            def _():
                acc_ref[...] = p

            @pl.when(k > 0)
            def _():
                acc_ref[...] += p

            @pl.when(k == nk - 1)
            def _():
                emit(o_ref, acc_ref[...])
        scratch = [pltpu.VMEM((tm, tn), F32)]

    if transpose_out:
        out_spec, out_shape = pl.BlockSpec((tn, tm), lambda i, j, k: (j, i)), (n, m)
    else:
        out_spec, out_shape = pl.BlockSpec((tm, tn), lambda i, j, k: (i, j)), (m, n)
    res = pl.pallas_call(
        body, name=name, grid=(m // tm, n // tn, nk),
        in_specs=[pl.BlockSpec((tm, tk), lambda i, j, k: (i, k)), b_spec] + [pl.BlockSpec(memory_space=pl.ANY)] * len(extra),
        out_specs=[out_spec] * len(out_dtypes), out_shape=[jax.ShapeDtypeStruct(out_shape, d) for d in out_dtypes],
        scratch_shapes=scratch, compiler_params=_cparams("parallel", "parallel", "arbitrary"))(a, b, *extra)
    return res if also_bf16 else res[0]


def _rms_in(x, nw):
    s, d = x.shape

    def body(x_ref, w_ref, h_ref, ht_ref):
        xv = x_ref[...]
        r = lax.rsqrt(jnp.mean(xv * xv, axis=-1, keepdims=True) + NORM_EPS)
        h = xv * r * w_ref[...]
        h_ref[...] = h.astype(BF16)
        ht_ref[...] = h.T.astype(BF16)

    return pl.pallas_call(
        body, name="rms_in", grid=(s // BIG_TILE,),
        in_specs=[pl.BlockSpec((BIG_TILE, d), lambda i: (i, 0)), pl.BlockSpec((1, d), lambda i: (0, 0))],
        out_specs=[pl.BlockSpec((BIG_TILE, d), lambda i: (i, 0)), pl.BlockSpec((d, BIG_TILE), lambda i: (0, i))],
        out_shape=[jax.ShapeDtypeStruct((s, d), BF16), jax.ShapeDtypeStruct((d, s), BF16)],
        compiler_params=_cparams("parallel"))(x, nw)


def _rms_in_bwd(x, nw, dh, dx2):
    s, d = x.shape

    def body(x_ref, w_ref, dh_ref, dx2_ref, dx_ref, dw_ref):
        i = pl.program_id(0)
        xv = x_ref[...]
        r = lax.rsqrt(jnp.mean(xv * xv, axis=-1, keepdims=True) + NORM_EPS)
        dhv = dh_ref[...]
        dyw = dhv * w_ref[...]
        dx_ref[...] = dx2_ref[...] + r * dyw - xv * (r * r * r) * jnp.mean(dyw * xv, axis=-1, keepdims=True)
        part = jnp.sum(dhv * xv * r, axis=0, keepdims=True)

        @pl.when(i == 0)
        def _():
            dw_ref[...] = part

        @pl.when(i > 0)
        def _():
            dw_ref[...] += part

    row = pl.BlockSpec((BIG_TILE, d), lambda i: (i, 0))
    vec = pl.BlockSpec((1, d), lambda i: (0, 0))
    return pl.pallas_call(
        body, name="rms_in_bwd", grid=(s // BIG_TILE,), in_specs=[row, vec, row, row], out_specs=[row, vec],
        out_shape=[jax.ShapeDtypeStruct((s, d), F32), jax.ShapeDtypeStruct((1, d), F32)],
        compiler_params=_cparams("arbitrary"))(x, nw, dh, dx2)


def _shift_down(cur, prev8, k):
    rc = pltpu.roll(cur, k, 0)
    rp = pltpu.roll(prev8, k, 0)
    row = lax.broadcasted_iota(jnp.int32, prev8.shape, 0)
    top = jnp.where(row < k, rp, rc[:8])
    return jnp.concatenate([top, rc[8:]], axis=0)


def _shift_up(cur, next8, k):
    t = cur.shape[0]
    rc = pltpu.roll(cur, t - k, 0)
    rn = pltpu.roll(next8, 8 - k, 0)
    row = lax.broadcasted_iota(jnp.int32, next8.shape, 0)
    bot = jnp.where(row >= 8 - k, rn, rc[t - 8:])
    return jnp.concatenate([rc[:t - 8], bot], axis=0)


def _conv_fwd(proj, conv_w):
    s = proj.shape[0]
    tile = min(s, CONV_TILE)
    t8 = tile // 8

    def body(u_ref, up_ref, w_ref, c_ref, y_ref):
        i = pl.program_id(0)
        part = pl.program_id(1)
        cur = u_ref[...]
        prev8 = jnp.where(i > 0, up_ref[...], 0.0)
        w = w_ref[...]
        c = cur * w[3:4, :]
        for k in (1, 2, 3):
            c = c + _shift_down(cur, prev8, k) * w[3 - k:4 - k, :]
        c_ref[...] = c
        a = _silu(c)
        for h in range(DN_HEADS):
            ah = a[:, h * DN_D:(h + 1) * DN_D]
            r = lax.rsqrt(jnp.sum(ah * ah, axis=-1, keepdims=True) + NORM_EPS)
            y_ref[:, h * DN_D:(h + 1) * DN_D] = jnp.where(part < 2, ah * r, ah)

    return pl.pallas_call(
        body, name="conv_fwd", grid=(s // tile, 3),
        in_specs=[pl.BlockSpec((tile, DN_W), lambda i, p: (i, p)),
                  pl.BlockSpec((8, DN_W), lambda i, p: (jnp.maximum(i * t8 - 1, 0), p)),
                  pl.BlockSpec((4, DN_W), lambda i, p: (0, p))],
        out_specs=[pl.BlockSpec((tile, DN_W), lambda i, p: (i, p))] * 2,
        out_shape=[jax.ShapeDtypeStruct((s, 3 * DN_W), F32)] * 2,
        compiler_params=_cparams("parallel", "parallel"))(proj, proj, conv_w)


def _conv_bwd_act(c, dq, dk, dv):
    s = c.shape[0]

    def body(c_ref, dq_ref, dk_ref, dv_ref, dc_ref):
        for part, d_ref in enumerate((dq_ref, dk_ref, dv_ref)):
            for h in range(DN_HEADS):
                sl = slice(part * DN_W + h * DN_D, part * DN_W + (h + 1) * DN_D)
                ch = c_ref[:, sl]
                dyh = d_ref[:, h * DN_D:(h + 1) * DN_D]
                if part < 2:
                    ah = _silu(ch)
                    r = lax.rsqrt(jnp.sum(ah * ah, axis=-1, keepdims=True) + NORM_EPS)
                    dyh = r * dyh - ah * (r * r * r) * jnp.sum(dyh * ah, axis=-1, keepdims=True)
                dc_ref[:, sl] = dyh * _silu_grad(ch)

    wide = pl.BlockSpec((ROW_TILE, 3 * DN_W), lambda i: (i, 0))
    row = pl.BlockSpec((ROW_TILE, DN_W), lambda i: (i, 0))
    return pl.pallas_call(
        body, name="conv_bwd_act", grid=(s // ROW_TILE,), in_specs=[wide, row, row, row], out_specs=wide,
        out_shape=jax.ShapeDtypeStruct((s, 3 * DN_W), F32), compiler_params=_cparams("parallel"))(c, dq, dk, dv)


DPROJ_IN = pl.BlockSpec(memory_space=pl.ANY)


def _conv_bwd(proj, dc, conv_w, dproj):
    s = proj.shape[0]
    tile = min(s, CONV_TILE)
    t8 = tile // 8
    nrow = s // tile
    last8 = s // 8 - 1

    def body(u_ref, dc_ref, dcn_ref, w_ref, dproj_in, du_ref, dw_ref):
        i = pl.program_id(1)
        cur = u_ref[...]
        dcv = dc_ref[...]
        next8 = jnp.where(i < nrow - 1, dcn_ref[...], 0.0)
        w = w_ref[...]

        @pl.when(i == 0)
        def _():
            dw_ref[...] = jnp.zeros_like(dw_ref)

        du = dcv * w[3:4, :]
        dw_ref[3:4, :] += jnp.sum(cur * dcv, axis=0, keepdims=True)
        for k in (1, 2, 3):
            ahead = _shift_up(dcv, next8, k)
            du = du + ahead * w[3 - k:4 - k, :]
            dw_ref[3 - k:4 - k, :] += jnp.sum(cur * ahead, axis=0, keepdims=True)
        du_ref[...] = du.astype(BF16)

    blk = pl.BlockSpec((tile, DN_W), lambda p, i: (i, p))
    return pl.pallas_call(
        body, name="conv_bwd", grid=(3, nrow),
        in_specs=[blk, blk, pl.BlockSpec((8, DN_W), lambda p, i: (jnp.minimum((i + 1) * t8, last8), p)),
                  pl.BlockSpec((4, DN_W), lambda p, i: (0, p)), DPROJ_IN],
        out_specs=[blk, pl.BlockSpec((4, DN_W), lambda p, i: (0, p))],
        out_shape=[jax.ShapeDtypeStruct((s, PW), BF16), jax.ShapeDtypeStruct((4, 3 * DN_W), F32)],
        input_output_aliases={4: 0},
        compiler_params=_cparams("parallel", "arbitrary"))(proj, dc, dc, conv_w, dproj)


def _gates_fwd(proj, gate_par):
    s = proj.shape[0]

    def body(ba_ref, par_ref, o_ref):
        v = ba_ref[...]
        lane = lax.broadcasted_iota(jnp.int32, v.shape, 1)
        beta = _sigmoid(v)
        g = -jnp.exp(par_ref[0:1, :]) * _softplus(v + par_ref[1:2, :])
        o_ref[...] = jnp.where(lane < DN_HEADS, beta, jnp.where(lane < 2 * DN_HEADS, g, 0.0))

    return pl.pallas_call(
        body, name="gates_fwd", grid=(s // ROW_TILE,),
        in_specs=[pl.BlockSpec((ROW_TILE, 128), lambda i: (i, OFF_BA // 128)), pl.BlockSpec((8, 128), lambda i: (0, 0))],
        out_specs=pl.BlockSpec((ROW_TILE, 128), lambda i: (i, 0)),
        out_shape=jax.ShapeDtypeStruct((s, 128), F32), compiler_params=_cparams("parallel"))(proj, gate_par)


def _gates_bwd(proj, gate_par, dbg, dproj):
    s = proj.shape[0]

    def body(ba_ref, par_ref, d_ref, dproj_in, o_ref, dpar_ref):
        i = pl.program_id(0)
        v = ba_ref[...]
        dv = d_ref[...]
        lane = lax.broadcasted_iota(jnp.int32, v.shape, 1)
        beta = _sigmoid(v)
        nega = -jnp.exp(par_ref[0:1, :])
        xs = v + par_ref[1:2, :]
        dsp = dv * nega * _sigmoid(xs)
        dal = dv * nega * _softplus(xs)
        is_b = lane < DN_HEADS
        is_g = jnp.logical_and(lane >= DN_HEADS, lane < 2 * DN_HEADS)
        o_ref[:, :128] = jnp.where(is_b, dv * beta * (1.0 - beta), jnp.where(is_g, dsp, 0.0)).astype(BF16)
        o_ref[:, 128:] = jnp.zeros((ROW_TILE, PW - OFF_BA - 128), BF16)
        r0 = jnp.sum(jnp.where(is_g, dal, 0.0), axis=0, keepdims=True)
        r1 = jnp.sum(jnp.where(is_g, dsp, 0.0), axis=0, keepdims=True)

        @pl.when(i == 0)
        def _():
            dpar_ref[...] = jnp.zeros_like(dpar_ref)

        dpar_ref[0:1, :] += r0
        dpar_ref[1:2, :] += r1

    return pl.pallas_call(
        body, name="gates_bwd", grid=(s // ROW_TILE,),
        in_specs=[pl.BlockSpec((ROW_TILE, 128), lambda i: (i, OFF_BA // 128)), pl.BlockSpec((8, 128), lambda i: (0, 0)),
                  pl.BlockSpec((ROW_TILE, 128), lambda i: (i, 0)), DPROJ_IN],
        out_specs=[pl.BlockSpec((ROW_TILE, PW - OFF_BA), lambda i: (i, OFF_BA // (PW - OFF_BA))),
                   pl.BlockSpec((8, 128), lambda i: (0, 0))],
        out_shape=[jax.ShapeDtypeStruct((s, PW), BF16), jax.ShapeDtypeStruct((8, 128), F32)],
        input_output_aliases={3: 0},
        compiler_params=_cparams("arbitrary"))(proj, gate_par, dbg, dproj)


def _chunk_masks():
    c = DN_CHUNK
    ii = lax.broadcasted_iota(jnp.int32, (c, c), 0)
    jj = lax.broadcasted_iota(jnp.int32, (c, c), 1)
    return dict(ii=ii, jj=jj, lower=(ii >= jj), strict=(ii > jj),
                lower_f=(ii >= jj).astype(BF16), upper_f=(ii <= jj).astype(BF16))


class _Heads:
    def __init__(self, xs):
        self.xs = list(xs)

    def _bin(self, o, f):
        if isinstance(o, _Heads):
            return _Heads([f(a, b) for a, b in zip(self.xs, o.xs)])
        return _Heads([f(a, o) for a in self.xs])

    def __add__(self, o):
        return self._bin(o, lambda a, b: a + b)

    def __sub__(self, o):
        return self._bin(o, lambda a, b: a - b)

    def __mul__(self, o):
        return self._bin(o, lambda a, b: a * b)

    __radd__ = __add__
    __rmul__ = __mul__

    def __neg__(self):
        return _Heads([-a for a in self.xs])

    def __getitem__(self, i):
        return _Heads([a[i] for a in self.xs])


def _hmap(f, *args):
    n = next(len(a.xs) for a in args if isinstance(a, _Heads))
    return _Heads([f(*[(a.xs[h] if isinstance(a, _Heads) else a) for a in args]) for h in range(n)])


def _hdot(a, b):
    return _hmap(_dot, a, b)


def _hdot_nt(a, b):
    return _hmap(_dot_nt, a, b)


def _hdot_tn(a, b):
    return _hmap(_dot_tn, a, b)


def _hcat(a, b, axis):
    return _hmap(lambda x, y: jnp.concatenate([x, y], axis=axis), a, b)


def _hsum(a, axis):
    return _hmap(lambda t: jnp.sum(t, axis=axis, keepdims=True), a)


def _hwhere(c, a, b):
    return _hmap(jnp.where, c, a, b)


def _chunk_gates(mk, bg):
    c = DN_CHUNK
    gc_all = _dot_exact_lhs(mk["lower_f"], bg)
    rows = jnp.concatenate([gc_all, gc_all], axis=0).T
    hs = range(DN_HEADS)
    return (_Heads(bg[:, h:h + 1] for h in hs), _Heads(gc_all[:, DN_HEADS + h:DN_HEADS + h + 1] for h in hs),
            _Heads(rows[DN_HEADS + h:DN_HEADS + h + 1, :] for h in hs))


def _chunk_common(mk, q, k, beta_col, gc_col, gc_r):
    c = DN_CHUNK
    lower, strict = mk["lower"], mk["strict"]
    qs = q * (DN_D ** -0.5)
    beta_b = _hmap(lambda t: jnp.broadcast_to(t, (c, DN_D)), beta_col)
    gc_b = _hmap(lambda t: jnp.broadcast_to(t, (c, DN_D)), gc_col)
    gc_sq = gc_b[:, :c]
    gam = _hwhere(lower, _hmap(lambda t: jnp.exp(jnp.minimum(t, 0.0)), gc_sq - gc_r[:, :c]), 0.0)
    egc = _hmap(jnp.exp, gc_b)
    gl = gc_b[c - 1:c, :]
    ekd = _hmap(jnp.exp, gl - gc_b)
    dl = _hmap(jnp.exp, gl)
    kb = k * beta_b
    scores = _hdot_nt(_hcat(kb, qs, 0), k)
    a_strict = _hwhere(strict, scores[:c] * gam, 0.0)
    aqk = _hwhere(lower, scores[c:] * gam, 0.0)
    return dict(k=k, qs=qs, beta_b=beta_b, gc_b=gc_b, gam=gam, egc=egc, ekd=ekd, dl=dl, kb=kb, a_strict=a_strict, aqk=aqk)


def _unit_lower_inverse_minus_eye(n_strict, ii, jj):
    same = lax.shift_right_logical(ii, 4) == lax.shift_right_logical(jj, 4)
    dmat = _hwhere(same, n_strict, 0.0)
    omat = n_strict - dmat
    d2 = _hdot(dmat, dmat)
    d4 = _hdot(d2, d2)
    d8 = _hdot(d4, d4)
    x1 = d2 - dmat - _hdot(dmat, d2)
    x2 = x1 + d4 + _hdot(x1, d4)
    x3 = x2 + d8 + _hdot(x2, d8)
    n1 = omat + _hdot(x3, omat)
    n2 = _hdot(n1, n1)
    y = n2 - n1 - _hdot(n1, n2)
    return y + x3 + _hdot(y, x3)


GDR_HEAD_SETS = (range(0, DN_HEADS),)


def _gdr_fwd(qkv, bg):
    s = qkv.shape[0]
    c = DN_CHUNK
    n = s // c

    def body(q_ref, k_ref, v_ref, bg_ref, o_ref, u_ref, w_ref, vn_ref, tm_ref, st_ref, state):
        @pl.when(pl.program_id(0) == 0)
        def _():
            state[...] = jnp.zeros_like(state)

        mk = _chunk_masks()
        gates = _chunk_gates(mk, bg_ref[...])
        for hs in GDR_HEAD_SETS:
            sls = [slice(h * DN_D, (h + 1) * DN_D) for h in hs]
            cm = _chunk_common(mk, _Heads(q_ref[:, sl] for sl in sls), _Heads(k_ref[:, sl] for sl in sls),
                               *[_Heads(g.xs[h] for h in hs) for g in gates])
            tm = _unit_lower_inverse_minus_eye(cm["a_strict"], mk["ii"], mk["jj"])
            rhs_u = _Heads(v_ref[:, sl] for sl in sls) * cm["beta_b"]
            rhs_w = cm["kb"] * cm["egc"]
            t_rhs = _hdot(tm, _hcat(rhs_u, rhs_w, 1))
            u = rhs_u + t_rhs[:, :DN_D]
            w = rhs_w + t_rhs[:, DN_D:]
            st = _Heads(state[h] for h in hs)
            on_state = _hdot(_hcat(w, cm["qs"] * cm["egc"], 0), st)
            v_new = u - on_state[:c]
            o = on_state[c:] + _hdot(cm["aqk"], v_new)
            st_new = st * cm["dl"] + _hdot_tn(cm["k"] * cm["ekd"], v_new)
            for i, (h, sl) in enumerate(zip(hs, sls)):
                o_ref[:, sl] = o.xs[i]
                u_ref[:, sl] = u.xs[i]
                w_ref[:, sl] = w.xs[i]
                vn_ref[:, sl] = v_new.xs[i]
                tm_ref[h, 0] = tm.xs[i]
                st_ref[h, 0] = st.xs[i]
                state[h] = st_new.xs[i]

    def part(p):
        return pl.BlockSpec((c, DN_W), lambda j: (j, p))

    return pl.pallas_call(
        body, name="gdr_fwd", grid=(n,),
        in_specs=[part(0), part(1), part(2), pl.BlockSpec((c, 128), lambda j: (j, 0))],
        out_specs=[part(0)] * 4 + [pl.BlockSpec((DN_HEADS, 1, c, c), lambda j: (0, j, 0, 0)),
                                   pl.BlockSpec((DN_HEADS, 1, DN_D, DN_D), lambda j: (0, j, 0, 0))],
        out_shape=[jax.ShapeDtypeStruct((s, DN_W), F32)] * 4
        + [jax.ShapeDtypeStruct((DN_HEADS, n, c, c), F32), jax.ShapeDtypeStruct((DN_HEADS, n, DN_D, DN_D), F32)],
        scratch_shapes=[pltpu.VMEM((DN_HEADS, DN_D, DN_D), F32)],
        compiler_params=_cparams("arbitrary"))(qkv, qkv, qkv, bg)


def _gdr_bwd(qkv, bg, u, w, vn, tmat, states, do):
    s = qkv.shape[0]
    c = DN_CHUNK
    n = s // c

    def body(q_ref, k_ref, v_ref, bg_ref, u_ref, w_ref, vn_ref, tm_ref, st_ref, do_ref,
             dq_ref, dk_ref, dv_ref, dbg_ref, dstate):
        @pl.when(pl.program_id(0) == 0)
        def _():
            dstate[...] = jnp.zeros_like(dstate)

        mk = _chunk_masks()
        lower, strict = mk["lower"], mk["strict"]
        bg = bg_ref[...]
        ones = jnp.ones((c, DN_D), BF16)
        rowi = lax.broadcasted_iota(jnp.int32, (c, DN_D), 0)
        lane = lax.broadcasted_iota(jnp.int32, (c, 128), 1)
        hs = range(DN_HEADS)
        sls = [slice(h * DN_D, (h + 1) * DN_D) for h in hs]

        def heads_of(ref):
            return _Heads(ref[:, sl] for sl in sls)

        cm = _chunk_common(mk, heads_of(q_ref), heads_of(k_ref), *_chunk_gates(mk, bg))
        k, qs, beta_b = cm["k"], cm["qs"], cm["beta_b"]
        gam, egc, ekd, dl, kb = cm["gam"], cm["egc"], cm["ekd"], cm["dl"], cm["kb"]
        aqk, a_strict = cm["aqk"], cm["a_strict"]
        v, uu, ww, v_new, dov = heads_of(v_ref), heads_of(u_ref), heads_of(w_ref), heads_of(vn_ref), heads_of(do_ref)
        st = _Heads(st_ref[h, 0] for h in hs)
        dsn = _Heads(dstate[h] for h in hs)
        qd = qs * egc
        kd = k * ekd

        dv_new = _hdot_tn(aqk, dov) + _hdot(kd, dsn)
        do_sv = _hdot_nt(dov, _hcat(st, v_new, 0))
        dqd = do_sv[:, :DN_D]
        daqk = _hwhere(lower, do_sv[:, DN_D:], 0.0)
        dkd = _hdot_nt(v_new, dsn)
        ddl = _hsum(_hsum(dsn * st, 1), 0)
        dw = -_hdot_nt(dv_new, st)
        ds_new = dsn * dl + _hdot_tn(_hcat(qd, -ww, 0), _hcat(dov, dv_new, 0))

        tm = _Heads(tm_ref[h, 0] for h in hs)
        tt = _hdot_tn(tm, _hcat(dv_new, dw, 1))
        dru = dv_new + tt[:, :DN_D]
        drw = dw + tt[:, DN_D:]
        dn = _hwhere(strict, -_hdot_nt(_hcat(dru, drw, 1), _hcat(uu, ww, 1)), 0.0)
        dag = dn * gam
        dqg = daqk * gam
        both = _hcat(dag, dqg, 0)
        on_k = _hdot(both, k)
        dkb = on_k[:c] + drw * egc
        dqs = on_k[c:] + dqd * egc
        dk = _hdot_tn(both, _hcat(kb, qs, 0)) + dkb * beta_b + dkd * ekd
        pmat = dn * a_strict + daqk * aqk
        tkd = _hsum(dkd * kd, -1)
        dgc = (_hsum(pmat, -1) - _hmap(_dot_tn_exact_rhs, pmat, ones) + _hsum(drw * (kb * egc), -1)
               + _hsum(dqd * qd, -1) - tkd)
        last = _hsum(tkd, 0) + ddl * dl
        dgc = dgc + _hwhere(rowi == c - 1, last, 0.0)
        dbeta = _hsum(dru * v, -1) + _hsum(dkb * k, -1)
        dq = dqs * (DN_D ** -0.5)
        dv = dru * beta_b

        dgc_all = jnp.zeros((c, 128), F32)
        dbg = jnp.zeros((c, 128), F32)
        for h, sl in zip(hs, sls):
            dq_ref[:, sl] = dq.xs[h]
            dk_ref[:, sl] = dk.xs[h]
            dv_ref[:, sl] = dv.xs[h]
            dstate[h] = ds_new.xs[h]
            dgc_all = dgc_all + jnp.where(lane == DN_HEADS + h, dgc.xs[h], 0.0)
            dbg = dbg + jnp.where(lane == h, dbeta.xs[h], 0.0)
        dbg_ref[...] = dbg + _dot_exact_lhs(mk["upper_f"], dgc_all)

    def part(p):
        return pl.BlockSpec((c, DN_W), lambda j: (n - 1 - j, p))

    vec = pl.BlockSpec((c, 128), lambda j: (n - 1 - j, 0))
    return pl.pallas_call(
        body, name="gdr_bwd", grid=(n,),
        in_specs=[part(0), part(1), part(2), vec, part(0), part(0), part(0),
                  pl.BlockSpec((DN_HEADS, 1, c, c), lambda j: (0, n - 1 - j, 0, 0)),
                  pl.BlockSpec((DN_HEADS, 1, DN_D, DN_D), lambda j: (0, n - 1 - j, 0, 0)), part(0)],
        out_specs=[part(0), part(0), part(0), vec],
        out_shape=[jax.ShapeDtypeStruct((s, DN_W), F32)] * 3 + [jax.ShapeDtypeStruct((s, 128), F32)],
        scratch_shapes=[pltpu.VMEM((DN_HEADS, DN_D, DN_D), F32)],
        compiler_params=_cparams("arbitrary"))(qkv, qkv, qkv, bg, u, w, vn, tmat, states, do)


def _gdr_out(o, proj, dnw):
    s = o.shape[0]

    def body(o_ref, z_ref, w_ref, y_ref, yt_ref):
        ov, zv, wv = o_ref[...], z_ref[...], w_ref[...]
        for h in range(DN_HEADS):
            sl = slice(h * DN_D, (h + 1) * DN_D)
            oh = ov[:, sl]
            r = lax.rsqrt(jnp.mean(oh * oh, axis=-1, keepdims=True) + NORM_EPS)
            y = (oh * r * wv) * _silu(zv[:, sl])
            y_ref[:, sl] = y.astype(BF16)
            yt_ref[sl, :] = y.T.astype(BF16)

    row = pl.BlockSpec((BIG_TILE, DN_W), lambda i: (i, 0))
    return pl.pallas_call(
        body, name="gdr_out", grid=(s // BIG_TILE,),
        in_specs=[row, pl.BlockSpec((BIG_TILE, DN_W), lambda i: (i, OFF_Z_A // DN_W)), pl.BlockSpec((1, DN_D), lambda i: (0, 0))],
        out_specs=[row, pl.BlockSpec((DN_W, BIG_TILE), lambda i: (0, i))],
        out_shape=[jax.ShapeDtypeStruct((s, DN_W), BF16), jax.ShapeDtypeStruct((DN_W, s), BF16)],
        compiler_params=_cparams("parallel"))(o, proj, dnw)


def _gdr_out_bwd(o, proj, dnw, dy):
    s = o.shape[0]

    def body(o_ref, z_ref, w_ref, dy_ref, do_ref, dz_ref, dw_ref):
        i = pl.program_id(0)
        ov, zv, wv, dyv = o_ref[...], z_ref[...], w_ref[...], dy_ref[...]
        acc = jnp.zeros((1, DN_D), F32)
        for h in range(DN_HEADS):
            sl = slice(h * DN_D, (h + 1) * DN_D)
            oh, zh, dh = ov[:, sl], zv[:, sl], dyv[:, sl]
            r = lax.rsqrt(jnp.mean(oh * oh, axis=-1, keepdims=True) + NORM_EPS)
            dn = dh * _silu(zh)
            dz_ref[:, sl] = (dh * (oh * r * wv) * _silu_grad(zh)).astype(BF16)
            acc = acc + jnp.sum(dn * oh * r, axis=0, keepdims=True)
            dnw_ = dn * wv
            do_ref[:, sl] = r * dnw_ - oh * (r * r * r) * jnp.mean(dnw_ * oh, axis=-1, keepdims=True)

        @pl.when(i == 0)
        def _():
            dw_ref[...] = acc

        @pl.when(i > 0)
        def _():
            dw_ref[...] += acc

    row = pl.BlockSpec((ROW_TILE, DN_W), lambda i: (i, 0))
    vec = pl.BlockSpec((1, DN_D), lambda i: (0, 0))
    return pl.pallas_call(
        body, name="gdr_out_bwd", grid=(s // ROW_TILE,),
        in_specs=[row, pl.BlockSpec((ROW_TILE, DN_W), lambda i: (i, OFF_Z_A // DN_W)), vec, row],
        out_specs=[row, pl.BlockSpec((ROW_TILE, DN_W), lambda i: (i, OFF_Z_A // DN_W)), vec],
        out_shape=[jax.ShapeDtypeStruct((s, DN_W), F32), jax.ShapeDtypeStruct((s, PW), BF16),
                   jax.ShapeDtypeStruct((1, DN_D), F32)],
        compiler_params=_cparams("arbitrary"))(o, proj, dnw, dy)


def _slope(group, head):
    idx = (group * DIL_HEADS + head + 1).astype(F32)
    return jnp.exp(jnp.full((1, 128), -8.0 * math.log(2.0) / (N_DIL * DIL_HEADS), F32) * idx)


def _att_scores(qb, k_cur, k_prev, slope_d, has_prev):
    iq = lax.broadcasted_iota(jnp.int32, (ATT_BLOCK, ATT_BLOCK), 0)
    jk = lax.broadcasted_iota(jnp.int32, (ATT_BLOCK, ATT_BLOCK), 1)
    dist_c = (iq - jk).astype(F32)
    s_cur = jnp.where(iq >= jk, _dot_nt(qb, k_cur) - slope_d * dist_c, NEG)
    s_prev = jnp.where(jnp.logical_and(jk >= iq, has_prev),
                       _dot_nt(qb, k_prev) - slope_d * (dist_c + float(ATT_BLOCK)), NEG)
    return s_cur, s_prev


def _att_scores_whole(qb, k, slope_d):
    n = 2 * ATT_BLOCK
    dist = lax.broadcasted_iota(jnp.int32, (n, n), 0) - lax.broadcasted_iota(jnp.int32, (n, n), 1)
    valid = jnp.logical_and(dist >= 0, dist <= ATT_BLOCK)
    return jnp.where(valid, _dot_nt(qb, k) - slope_d[:, 0:1] * dist.astype(F32), NEG)


def _att_tiles(i, dil, nb):
    tiles = nb // 2
    per = dil * tiles // ATT_UNROLL
    assert nb % 2 == 0 and tiles >= 2 and per * ATT_UNROLL == dil * tiles
    for i0 in range(per):
        ts = [divmod(i0 + u * per, tiles) for u in range(ATT_UNROLL)]
        assert all(a[0] != b[0] or abs(a[1] - b[1]) >= 2 for n, a in enumerate(ts) for b in ts[n + 1:])
    qrows, krows, has_prev = [], [], []
    for u in range(ATT_UNROLL):
        t = i + u * per
        r = lax.div(t, tiles)
        j = lax.rem(t, tiles)
        qbase = r + dil * 2 * ATT_BLOCK * j
        kbase = qbase - dil * ATT_BLOCK * jnp.minimum(j, 1)
        if dil == 1:
            qbase, kbase = pl.multiple_of(qbase, ATT_BLOCK), pl.multiple_of(kbase, ATT_BLOCK)
        qrows.append(pl.ds(qbase, 2 * ATT_BLOCK, stride=dil))
        krows.append(pl.ds(kbase, 3 * ATT_BLOCK, stride=dil))
        has_prev.append(j > 0)
    return qrows, krows, has_prev


def _att_scores_tile(qb, k, slope_d, has_prev):
    iq = lax.broadcasted_iota(jnp.int32, (2 * ATT_BLOCK, 3 * ATT_BLOCK), 0)
    ck = lax.broadcasted_iota(jnp.int32, (2 * ATT_BLOCK, 3 * ATT_BLOCK), 1)
    dist = iq - ck + jnp.where(has_prev, ATT_BLOCK, 0)
    valid = jnp.logical_and(dist >= 0, dist <= ATT_BLOCK)
    return jnp.where(valid, _dot_nt(qb, k) - slope_d[:, 0:1] * dist.astype(F32), NEG)


ATT_UNROLL = 4


def _att_blocks(i, dil, nb):
    per = dil * nb // ATT_UNROLL
    assert per * ATT_UNROLL == dil * nb
    for i0 in range(per):
        blocks = [divmod(i0 + u * per, nb) for u in range(ATT_UNROLL)]
        assert all(a[0] != b[0] or abs(a[1] - b[1]) >= 2 for n, a in enumerate(blocks) for b in blocks[n + 1:])
    curs, prvs, has_prev = [], [], []
    for u in range(ATT_UNROLL):
        t = i + u * per
        r = lax.div(t, nb)
        j = lax.rem(t, nb)
        base = r + dil * ATT_BLOCK * j
        pbase = base - dil * ATT_BLOCK * jnp.minimum(j, 1)
        if dil == 1:
            base, pbase = pl.multiple_of(base, ATT_BLOCK), pl.multiple_of(pbase, ATT_BLOCK)
        curs.append(pl.ds(base, ATT_BLOCK, stride=dil))
        prvs.append(pl.ds(pbase, ATT_BLOCK, stride=dil))
        has_prev.append(j > 0)
    return curs, prvs, has_prev


def _att_fwd(proj, group):
    s = proj.shape[0]
    dil = DIL_GROUPS[group][1]
    assert DIL_GROUPS[group][0] // dil == ATT_BLOCK
    nb = s // dil // ATT_BLOCK
    assert nb * dil * ATT_BLOCK == s

    def body(q_ref, k_ref, v_ref, o_ref, lse_ref):
        def emit(rows, num, den, mx):
            o_ref[rows, :] = num / den
            lse_ref[rows, :] = jnp.broadcast_to(mx + jnp.log(den), (num.shape[0], DIL_DH))

        slope_d = _slope(group, pl.program_id(0)) * float(dil)

        def step(i, carry):
            curs, prvs, has_prev = _att_blocks(i, dil, nb)
            us = range(ATT_UNROLL)
            qb = [q_ref[c, :] * (DIL_DH ** -0.5) for c in curs]
            sc = [_att_scores(qb[u], k_ref[curs[u], :], k_ref[prvs[u], :], slope_d, has_prev[u]) for u in us]
            mx = [jnp.maximum(jnp.max(a, axis=-1, keepdims=True), jnp.max(b, axis=-1, keepdims=True)) for a, b in sc]
            p_cur = [jnp.exp(sc[u][0] - mx[u]) for u in us]
            p_prev = [jnp.exp(sc[u][1] - mx[u]) for u in us]
            den = [jnp.sum(p_cur[u], axis=-1, keepdims=True) + jnp.sum(p_prev[u], axis=-1, keepdims=True) for u in us]
            num = [_dot(p_cur[u], v_ref[curs[u], :]) + _dot(p_prev[u], v_ref[prvs[u], :]) for u in us]
            for u in us:
                emit(curs[u], num[u], den[u], mx[u])
            return carry

        def step_whole(i, carry):
            rows = [pl.ds(i * ATT_UNROLL + u, 2 * ATT_BLOCK, stride=dil) for u in range(ATT_UNROLL)]
            sc = [_att_scores_whole(q_ref[r, :] * (DIL_DH ** -0.5), k_ref[r, :], slope_d) for r in rows]
            mx = [jnp.max(a, axis=-1, keepdims=True) for a in sc]
            p = [jnp.exp(a - m) for a, m in zip(sc, mx)]
            num = [_dot(pu, v_ref[r, :]) for pu, r in zip(p, rows)]
            for u, r in enumerate(rows):
                emit(r, num[u], jnp.sum(p[u], axis=-1, keepdims=True), mx[u])
            return carry

        def step_tile(i, carry):
            qrows, krows, has_prev = _att_tiles(i, dil, nb)
            us = range(ATT_UNROLL)
            sc = [_att_scores_tile(q_ref[qrows[u], :] * (DIL_DH ** -0.5), k_ref[krows[u], :], slope_d, has_prev[u]) for u in us]
            mx = [jnp.max(a, axis=-1, keepdims=True) for a in sc]
            p = [jnp.exp(a - m) for a, m in zip(sc, mx)]
            num = [_dot(p[u], v_ref[krows[u], :]) for u in us]
            for u in us:
                emit(qrows[u], num[u], jnp.sum(p[u], axis=-1, keepdims=True), mx[u])
            return carry

        if nb == 2:
            lax.fori_loop(0, dil // ATT_UNROLL, step_whole, 0)
        elif nb % 2 == 0:
            lax.fori_loop(0, dil * nb // 2 // ATT_UNROLL, step_tile, 0)
        else:
            lax.fori_loop(0, dil * nb // ATT_UNROLL, step, 0)

    def col(off):
        return pl.BlockSpec((s, DIL_DH), lambda h: (0, off // DIL_DH + group * DIL_HEADS + h))

    out = pl.BlockSpec((s, DIL_DH), lambda h: (0, h))
    return pl.pallas_call(
        body, name=f"att_fwd{group}", grid=(DIL_HEADS,), in_specs=[col(OFF_Q_B), col(OFF_K_B), col(OFF_V_B)],
        out_specs=[out, out], out_shape=[jax.ShapeDtypeStruct((s, DIL_W), F32)] * 2,
        compiler_params=_cparams("parallel"))(proj, proj, proj)


def _att_bwd(proj, group, do, lse, delta):
    s = proj.shape[0]
    dil = DIL_GROUPS[group][1]
    nb = s // dil // ATT_BLOCK

    def body(q_ref, k_ref, v_ref, do_ref, lse_ref, dl_ref, dq_ref, dk_ref, dv_ref, dq_acc, dk_acc, dv_acc):
        slope_d = _slope(group, pl.program_id(0)) * float(dil)
        dk_acc[...] = jnp.zeros_like(dk_acc)
        dv_acc[...] = jnp.zeros_like(dv_acc)

        def step(i, carry):
            curs, prvs, has_prev = _att_blocks(i, dil, nb)
            us = range(ATT_UNROLL)
            qb = [q_ref[c, :] * (DIL_DH ** -0.5) for c in curs]
            k_cur, k_prev = [k_ref[c, :] for c in curs], [k_ref[p, :] for p in prvs]
            v_cur, v_prev = [v_ref[c, :] for c in curs], [v_ref[p, :] for p in prvs]
            sc = [_att_scores(qb[u], k_cur[u], k_prev[u], slope_d, has_prev[u]) for u in us]
            lse_b, delta_b, dob = [lse_ref[c, :] for c in curs], [dl_ref[c, :] for c in curs], [do_ref[c, :] for c in curs]
            p_cur = [jnp.exp(sc[u][0] - lse_b[u]) for u in us]
            p_prev = [jnp.exp(sc[u][1] - lse_b[u]) for u in us]
            ds_cur = [p_cur[u] * (_dot_nt(dob[u], v_cur[u]) - delta_b[u]) for u in us]
            ds_prev = [p_prev[u] * (_dot_nt(dob[u], v_prev[u]) - delta_b[u]) for u in us]
            dq = [(_dot(ds_cur[u], k_cur[u]) + _dot(ds_prev[u], k_prev[u])) * (DIL_DH ** -0.5) for u in us]
            dk_c = [_dot_tn(ds_cur[u], qb[u]) for u in us]
            dv_c = [_dot_tn(p_cur[u], dob[u]) for u in us]
            dk_p = [_dot_tn(ds_prev[u], qb[u]) for u in us]
            dv_p = [_dot_tn(p_prev[u], dob[u]) for u in us]
            for u in us:
                dq_acc[curs[u], :] = dq[u]
                dk_acc[curs[u], :] += dk_c[u]
                dv_acc[curs[u], :] += dv_c[u]
            for u in us:
                dk_acc[prvs[u], :] += dk_p[u]
                dv_acc[prvs[u], :] += dv_p[u]
            return carry

        def step_whole(i, carry):
            rows = [pl.ds(i * ATT_UNROLL + u, 2 * ATT_BLOCK, stride=dil) for u in range(ATT_UNROLL)]
            qb = [q_ref[r, :] * (DIL_DH ** -0.5) for r in rows]
            kk, vv, dob = [k_ref[r, :] for r in rows], [v_ref[r, :] for r in rows], [do_ref[r, :] for r in rows]
            sc = [_att_scores_whole(qb[u], kk[u], slope_d) for u in range(ATT_UNROLL)]
            p = [jnp.exp(sc[u] - lse_ref[r, :][:, 0:1]) for u, r in enumerate(rows)]
            ds = [p[u] * (_dot_nt(dob[u], vv[u]) - dl_ref[r, :][:, 0:1]) for u, r in enumerate(rows)]
            dq = [_dot(ds[u], kk[u]) * (DIL_DH ** -0.5) for u in range(ATT_UNROLL)]
            dk = [_dot_tn(ds[u], qb[u]) for u in range(ATT_UNROLL)]
            dv = [_dot_tn(p[u], dob[u]) for u in range(ATT_UNROLL)]
            for u, r in enumerate(rows):
                dq_acc[r, :] = dq[u]
                dk_acc[r, :] = dk[u]
                dv_acc[r, :] = dv[u]
            return carry

        def step_tile(i, carry):
            qrows, krows, has_prev = _att_tiles(i, dil, nb)
            us = range(ATT_UNROLL)
            qb = [q_ref[r, :] * (DIL_DH ** -0.5) for r in qrows]
            kk, vv, dob = [k_ref[r, :] for r in krows], [v_ref[r, :] for r in krows], [do_ref[r, :] for r in qrows]
            sc = [_att_scores_tile(qb[u], kk[u], slope_d, has_prev[u]) for u in us]
            p = [jnp.exp(sc[u] - lse_ref[qrows[u], :][:, 0:1]) for u in us]
            ds = [p[u] * (_dot_nt(dob[u], vv[u]) - dl_ref[qrows[u], :][:, 0:1]) for u in us]
            dq = [_dot(ds[u], kk[u]) * (DIL_DH ** -0.5) for u in us]
            dk = [_dot_tn(ds[u], qb[u]) for u in us]
            dv = [_dot_tn(p[u], dob[u]) for u in us]
            for u in us:
                dq_acc[qrows[u], :] = dq[u]
                dk_acc[krows[u], :] += dk[u]
                dv_acc[krows[u], :] += dv[u]
            return carry

        if nb == 2:
            lax.fori_loop(0, dil // ATT_UNROLL, step_whole, 0)
        elif nb % 2 == 0:
            lax.fori_loop(0, dil * nb // 2 // ATT_UNROLL, step_tile, 0)
        else:
            lax.fori_loop(0, dil * nb // ATT_UNROLL, step, 0)
        dq_ref[...] = dq_acc[...].astype(BF16)
        dk_ref[...] = dk_acc[...].astype(BF16)
        dv_ref[...] = dv_acc[...].astype(BF16)

    def col(off):
        return pl.BlockSpec((s, DIL_DH), lambda h: (0, off // DIL_DH + group * DIL_HEADS + h))

    hd = pl.BlockSpec((s, DIL_DH), lambda h: (0, h))
    return pl.pallas_call(
        body, name=f"att_bwd{group}", grid=(DIL_HEADS,),
        in_specs=[col(OFF_Q_B), col(OFF_K_B), col(OFF_V_B), hd, hd, hd], out_specs=[hd, hd, hd],
        out_shape=[jax.ShapeDtypeStruct((s, DIL_W), BF16)] * 3,
        scratch_shapes=[pltpu.VMEM((s, DIL_DH), F32)] * 3,
        compiler_params=_cparams("parallel"))(proj, proj, proj, do, lse, delta)


def _att_merge(parts, proj):
    s = proj.shape[0]

    def body(o0, l0, o1, l1, o2, l2, z_ref, ob_ref, o_ref, lse_ref, obt_ref):
        m = jnp.maximum(jnp.maximum(l0[...], l1[...]), l2[...])
        num = jnp.zeros_like(m)
        den = jnp.zeros_like(m)
        for og, lg in ((o0, l0), (o1, l1), (o2, l2)):
            sc = jnp.exp(lg[...] - m)
            num = num + og[...] * sc
            den = den + sc
        o = num / den
        o_ref[...] = o
        lse_ref[...] = m + jnp.log(den)
        ob = o * _silu(z_ref[...])
        ob_ref[...] = ob.astype(BF16)
        obt_ref[...] = ob.T.astype(BF16)

    row = pl.BlockSpec((ROW_TILE, DIL_W), lambda i: (i, 0))
    flat = [a for p in parts for a in p]
    return pl.pallas_call(
        body, name="att_merge", grid=(s // ROW_TILE,),
        in_specs=[row] * 6 + [pl.BlockSpec((ROW_TILE, DIL_W), lambda i: (i, OFF_Z_B // DIL_W))],
        out_specs=[row, row, row, pl.BlockSpec((DIL_W, ROW_TILE), lambda i: (0, i))],
        out_shape=[jax.ShapeDtypeStruct((s, DIL_W), BF16), jax.ShapeDtypeStruct((s, DIL_W), F32),
                   jax.ShapeDtypeStruct((s, DIL_W), F32), jax.ShapeDtypeStruct((DIL_W, s), BF16)],
        compiler_params=_cparams("parallel"))(*flat, proj)


def _att_merge_bwd(o, proj, dob, dproj):
    s = o.shape[0]

    def body(o_ref, z_ref, d_ref, dproj_in, do_ref, dl_ref, dz_ref):
        ov, zv, dv = o_ref[...], z_ref[...], d_ref[...]
        do = dv * _silu(zv)
        do_ref[...] = do
        dz_ref[...] = (dv * ov * _silu_grad(zv)).astype(BF16)
        for h in range(DIL_HEADS):
            sl = slice(h * DIL_DH, (h + 1) * DIL_DH)
            dl_ref[:, sl] = jnp.broadcast_to(jnp.sum(do[:, sl] * ov[:, sl], axis=-1, keepdims=True), (ROW_TILE, DIL_DH))

    row = pl.BlockSpec((ROW_TILE, DIL_W), lambda i: (i, 0))
    return pl.pallas_call(
        body, name="att_merge_bwd", grid=(s // ROW_TILE,),
        in_specs=[row, pl.BlockSpec((ROW_TILE, DIL_W), lambda i: (i, OFF_Z_B // DIL_W)), row, DPROJ_IN],
        out_specs=[row, row, pl.BlockSpec((ROW_TILE, DIL_W), lambda i: (i, OFF_Z_B // DIL_W))],
        out_shape=[jax.ShapeDtypeStruct((s, DIL_W), F32), jax.ShapeDtypeStruct((s, DIL_W), F32),
                   jax.ShapeDtypeStruct((s, PW), BF16)],
        input_output_aliases={3: 2},
        compiler_params=_cparams("parallel"))(o, proj, dob, dproj)


def _merge(proj, ya, yb):
    s = proj.shape[0]

    def body(ga_ref, gb_ref, ya_ref, yb_ref, o_ref, ot_ref):
        m = _sigmoid(ga_ref[...]) * ya_ref[...] + _sigmoid(gb_ref[...]) * yb_ref[...]
        o_ref[...] = m.astype(BF16)
        ot_ref[...] = m.T.astype(BF16)

    row = pl.BlockSpec((ROW_TILE, D_MODEL), lambda i: (i, 0))
    return pl.pallas_call(
        body, name="merge", grid=(s // ROW_TILE,),
        in_specs=[pl.BlockSpec((ROW_TILE, D_MODEL), lambda i: (i, OFF_G_A // D_MODEL)),
                  pl.BlockSpec((ROW_TILE, D_MODEL), lambda i: (i, OFF_G_B // D_MODEL)), row, row],
        out_specs=[row, pl.BlockSpec((D_MODEL, ROW_TILE), lambda i: (0, i))],
        out_shape=[jax.ShapeDtypeStruct((s, D_MODEL), BF16), jax.ShapeDtypeStruct((D_MODEL, s), BF16)],
        compiler_params=_cparams("parallel"))(proj, proj, ya, yb)


def _merge_bwd(proj, ya, yb, dm):
    s = proj.shape[0]

    def body(ga_ref, gb_ref, ya_ref, yb_ref, dm_ref, dya_ref, dyb_ref, dga_ref, dgb_ref):
        dmv = dm_ref[...]
        sa, sb = _sigmoid(ga_ref[...]), _sigmoid(gb_ref[...])
        dya_ref[...] = (dmv * sa).astype(BF16)
        dyb_ref[...] = (dmv * sb).astype(BF16)
        dga_ref[...] = (dmv * ya_ref[...] * sa * (1.0 - sa)).astype(BF16)
        dgb_ref[...] = (dmv * yb_ref[...] * sb * (1.0 - sb)).astype(BF16)

    row = pl.BlockSpec((ROW_TILE, D_MODEL), lambda i: (i, 0))
    return pl.pallas_call(
        body, name="merge_bwd", grid=(s // ROW_TILE,),
        in_specs=[pl.BlockSpec((ROW_TILE, D_MODEL), lambda i: (i, OFF_G_A // D_MODEL)),
                  pl.BlockSpec((ROW_TILE, D_MODEL), lambda i: (i, OFF_G_B // D_MODEL)), row, row, row],
        out_specs=[row] * 4, out_shape=[jax.ShapeDtypeStruct((s, D_MODEL), BF16)] * 4,
        compiler_params=_cparams("parallel"))(proj, proj, ya, yb, dm)


def _final(x, t, fw, tgt):
    s, d = x.shape

    def body(x_ref, t_ref, w_ref, y_ref, dx_ref, dw_ref, l_ref):
        i = pl.program_id(0)
        x2 = x_ref[...] + t_ref[...]
        wv = w_ref[...]
        r = lax.rsqrt(jnp.mean(x2 * x2, axis=-1, keepdims=True) + NORM_EPS)
        e = x2 * r * wv - y_ref[...]
        lrow = jnp.mean(e * e, axis=-1, keepdims=True)
        lpart = jnp.broadcast_to(0.5 * jnp.sum(lrow, axis=0, keepdims=True), (1, 128))
        dy = e * (1.0 / d)
        dwp = jnp.sum(dy * x2 * r, axis=0, keepdims=True)
        dyw = dy * wv
        dx_ref[...] = r * dyw - x2 * (r * r * r) * jnp.mean(dyw * x2, axis=-1, keepdims=True)

        @pl.when(i == 0)
        def _():
            dw_ref[...] = dwp
            l_ref[...] = lpart

        @pl.when(i > 0)
        def _():
            dw_ref[...] += dwp
            l_ref[...] += lpart

    row = pl.BlockSpec((BIG_TILE, d), lambda i: (i, 0))
    vec = pl.BlockSpec((1, d), lambda i: (0, 0))
    return pl.pallas_call(
        body, name="final", grid=(s // BIG_TILE,), in_specs=[row, row, vec, row],
        out_specs=[row, vec, pl.BlockSpec((1, 128), lambda i: (0, 0))],
        out_shape=[jax.ShapeDtypeStruct((s, d), F32), jax.ShapeDtypeStruct((1, d), F32), jax.ShapeDtypeStruct((1, 128), F32)],
        compiler_params=_cparams("arbitrary"))(x, t, fw, tgt)


def _adamw(w, g, m, v, name):
    r, c = w.shape
    cap = max(8, (1 << 19) // c)
    divisors = [t for t in range(8, min(r, cap) + 1, 8) if r % t == 0]
    tr = r if r <= 8 else (max(divisors) if divisors else cap)

    def body(w_ref, g_ref, m_ref, v_ref, d_ref, nm_ref, nv_ref):
        gv = g_ref[...]
        mn = ADAM_B1 * m_ref[...] + (1.0 - ADAM_B1) * gv
        vn = ADAM_B2 * v_ref[...] + (1.0 - ADAM_B2) * (gv * gv)
        m_hat = mn / (1.0 - ADAM_B1 ** ADAM_STEP)
        v_hat = vn / (1.0 - ADAM_B2 ** ADAM_STEP)
        d_ref[...] = -ADAM_LR * (m_hat / (jnp.sqrt(v_hat) + ADAM_EPS) + ADAM_WD * w_ref[...])
        nm_ref[...] = mn
        nv_ref[...] = vn

    blk = pl.BlockSpec((tr, c), lambda i: (i, 0))
    return pl.pallas_call(
        body, name=name, grid=(pl.cdiv(r, tr),), in_specs=[blk] * 4, out_specs=[blk] * 3,
        out_shape=[jax.ShapeDtypeStruct((r, c), F32)] * 3, compiler_params=_cparams("parallel"))(w, g, m, v)


HBM_SPEC = pl.BlockSpec(memory_space=pl.ANY)


def _place():
    x, y, c = lax.axis_index("x"), lax.axis_index("y"), lax.axis_index("c")
    chips = [(1 - x, y), (x, 1 - y), (1 - x, 1 - y)]
    return x, y, c, chips


def _ag_weights(packs):
    na = len(packs)
    nsem = 8

    def body(*refs):
        p_refs, out_refs = refs[:na], refs[na:2 * na]
        send_sems, recv_sems = refs[2 * na:]
        x, y, c, _ = _place()
        me, sib, j = (x, y, c), (x, y, 1 - c), 2 * x + y
        xn, yn = (1 - x, y, c), (x, 1 - y, c)
        jx, jy, jd = 2 * (1 - x) + y, 2 * x + (1 - y), 2 * (1 - x) + (1 - y)

        def rc(a, k, src, dst, to):
            return pltpu.make_async_remote_copy(src_ref=src, dst_ref=dst, send_sem=send_sems.at[nsem * a + k],
                                                recv_sem=recv_sems.at[nsem * a + k], device_id=to, device_id_type=MESH)

        sent = []
        for a in range(na):
            mine, land = p_refs[a].at[c], out_refs[a].at[j, c]
            sent += [rc(a, 0, mine, land, xn), rc(a, 1, mine, land, yn), rc(a, 7, p_refs[a], out_refs[a].at[j], sib)]
        for cp in sent:
            cp.start()
        for a in range(na):
            half = p_refs[a].shape[1] // 2
            top, bottom = pl.ds(0, half), pl.ds(half, half)
            from_x, from_y, from_d = out_refs[a].at[jx, c], out_refs[a].at[jy, c], out_refs[a].at[jd, c]
            rc(a, 0, p_refs[a].at[c], from_x, me).wait_recv()
            later = [rc(a, 2, from_x.at[top], from_x.at[top], yn), rc(a, 4, from_x, from_x, sib)]
            for cp in later:
                cp.start()
            sent += later
            rc(a, 1, p_refs[a].at[c], from_y, me).wait_recv()
            later = [rc(a, 3, from_y.at[bottom], from_y.at[bottom], xn), rc(a, 5, from_y, from_y, sib)]
            for cp in later:
                cp.start()
            sent += later
            rc(a, 2, from_d.at[top], from_d.at[top], me).wait_recv()
            rc(a, 3, from_d.at[bottom], from_d.at[bottom], me).wait_recv()
            cp = rc(a, 6, from_d, from_d, sib)
            cp.start()
            sent.append(cp)
        for a in range(na):
            for k, jj in ((4, jx), (5, jy), (6, jd)):
                rc(a, k, p_refs[a].at[c], out_refs[a].at[jj, 1 - c], me).wait_recv()
            rc(a, 7, p_refs[a], out_refs[a].at[j], me).wait_recv()
        for cp in sent:
            cp.wait_send()

    return pl.pallas_call(
        body, name="ag_weights",
        out_shape=[jax.ShapeDtypeStruct((N_CHIPS,) + p.shape, p.dtype) for p in packs],
        in_specs=[HBM_SPEC] * na, out_specs=[HBM_SPEC] * na,
        scratch_shapes=[pltpu.SemaphoreType.DMA((nsem * na,)), pltpu.SemaphoreType.DMA((nsem * na,))])(*packs)


def _rs_pair(dwpt, gpack):
    n = N_CHIPS
    hw = SHARD_PAD // 2

    def body(d_ref, g_ref, out_d, out_g, send_sems, recv_sems):
        x, y, c, _ = _place()
        sib = (x, y, 1 - c)
        cps = []
        for p in range(n):
            start = pl.multiple_of(WIN_BASE[p] + (1 - c) * hw, TILE_ROWS)
            cps.append(pltpu.make_async_remote_copy(
                src_ref=d_ref.at[pl.ds(start, hw)], dst_ref=out_d.at[p], send_sem=send_sems.at[p],
                recv_sem=recv_sems.at[p], device_id=sib, device_id_type=MESH))
            cps.append(pltpu.make_async_remote_copy(
                src_ref=g_ref.at[p, 1 - c], dst_ref=out_g.at[p], send_sem=send_sems.at[n + p],
                recv_sem=recv_sems.at[n + p], device_id=sib, device_id_type=MESH))
        for cp in cps:
            cp.start()
        for cp in cps:
            cp.wait_recv()
        for cp in cps:
            cp.wait_send()

    return pl.pallas_call(
        body, name="rs_pair",
        out_shape=[jax.ShapeDtypeStruct((n, hw, dwpt.shape[1]), dwpt.dtype),
                   jax.ShapeDtypeStruct((n,) + gpack.shape[2:], gpack.dtype)],
        in_specs=[HBM_SPEC] * 2, out_specs=[HBM_SPEC] * 2,
        scratch_shapes=[pltpu.SemaphoreType.DMA((2 * n,)), pltpu.SemaphoreType.DMA((2 * n,))])(dwpt, gpack)


def _add_halves_win(dwpt, other, c):
    n, rh, wd = other.shape
    tr = _row_tile(rh)

    def body(s_ref, d_ref, o_ref, out_ref):
        out_ref[0] = (d_ref[...] + o_ref[0]).astype(BF16)

    scal = jnp.concatenate([jnp.reshape(c, (1,)).astype(jnp.int32), jnp.asarray(WIN_BASE, jnp.int32)])
    grid_spec = pltpu.PrefetchScalarGridSpec(
        num_scalar_prefetch=1, grid=(n, rh // tr),
        in_specs=[pl.BlockSpec((pl.Element(tr), pl.Element(wd)),
                               lambda p, i, sr: (pl.multiple_of(sr[1 + p] + sr[0] * rh + i * tr, TILE_ROWS), 0)),
                  pl.BlockSpec((1, tr, wd), lambda p, i, sr: (p, i, 0))],
        out_specs=pl.BlockSpec((1, tr, wd), lambda p, i, sr: (p, i, 0)))
    return pl.pallas_call(
        body, name="add_halves_in", grid_spec=grid_spec, out_shape=jax.ShapeDtypeStruct((n, rh, wd), BF16),
        compiler_params=_cparams("parallel", "parallel"))(scal, dwpt, other)


SEM_SPEC = pl.BlockSpec(memory_space=pltpu.SEMAPHORE)
DATAFLOW_EFFECT = pltpu.SideEffectType.DATAFLOW_SIDE_EFFECTING


def _rs_chips_start(csums):
    na = len(csums)

    def body(*refs):
        s_refs, land_refs = refs[:na], refs[na:2 * na]
        send_sems, recv_sems = refs[2 * na], refs[2 * na + 1]
        token = refs[-1]
        x, y, c, chips = _place()
        j = 2 * x + y
        for a in range(na):
            for k, (cx, cy) in enumerate(chips):
                pltpu.make_async_remote_copy(src_ref=s_refs[a].at[2 * cx + cy], dst_ref=land_refs[a].at[j],
                                             send_sem=send_sems.at[3 * a + k], recv_sem=recv_sems.at[3 * a + k],
                                             device_id=(cx, cy, c), device_id_type=MESH).start()
        token[...] = jnp.zeros_like(token)

    hbm = [pltpu.HBM(s.shape, s.dtype) for s in csums]
    args = [pltpu.with_memory_space_constraint(s, pltpu.HBM) for s in csums]
    args += [pltpu.with_memory_space_constraint(lax.empty(s.shape, s.dtype), pltpu.HBM) for s in csums]
    res = pl.pallas_call(
        body, name="rs_chips_start",
        out_shape=(pltpu.SemaphoreType.DMA((3 * na,)), pltpu.SemaphoreType.DMA((3 * na,)), *hbm, *hbm,
                   jax.ShapeDtypeStruct((8, 128), F32)),
        in_specs=[pl.BlockSpec(memory_space=pltpu.HBM)] * (2 * na),
        out_specs=(SEM_SPEC, SEM_SPEC, *[pl.BlockSpec(memory_space=pltpu.HBM)] * (2 * na),
                   pl.BlockSpec(memory_space=pltpu.VMEM)),
        input_output_aliases={i: 2 + i for i in range(2 * na)},
        compiler_params=pltpu.CompilerParams(has_side_effects=DATAFLOW_EFFECT))(*args)
    return res[0], res[1], list(res[2:2 + na]), list(res[2 + na:2 + 2 * na]), res[-1]


def _rs_chips_wait(send_sems, recv_sems, csums, lands, after):
    na = len(csums)

    def body(*refs):
        s_refs, land_refs = refs[:na], refs[na:2 * na]
        send_sems, recv_sems = refs[2 * na], refs[2 * na + 1]
        x, y, c, chips = _place()
        j = 2 * x + y
        for a in range(na):
            for k, (cx, cy) in enumerate(chips):
                cp = pltpu.make_async_remote_copy(src_ref=s_refs[a].at[2 * cx + cy], dst_ref=land_refs[a].at[2 * cx + cy],
                                                  send_sem=send_sems.at[3 * a + k], recv_sem=recv_sems.at[3 * a + k],
                                                  device_id=(cx, cy, c), device_id_type=MESH)
                cp.wait_send()
                cp.wait_recv()

    hbm = [pltpu.HBM(s.shape, s.dtype) for s in csums]
    res = pl.pallas_call(
        body, name="rs_chips_wait", out_shape=(*hbm, *hbm),
        in_specs=[pl.BlockSpec(memory_space=pltpu.HBM)] * (2 * na) + [SEM_SPEC, SEM_SPEC, pl.BlockSpec(memory_space=pl.ANY)],
        out_specs=tuple([pl.BlockSpec(memory_space=pltpu.HBM)] * (2 * na)),
        input_output_aliases={i: i for i in range(2 * na)},
        compiler_params=pltpu.CompilerParams(has_side_effects=DATAFLOW_EFFECT))(*csums, *lands, send_sems, recv_sems, after)
    return list(res[:na]), list(res[na:])


SWAP_CHUNKS = 4


def _pair_swap(halves):
    na = len(halves)

    def body(*refs):
        h_refs, out_refs = refs[:na], refs[na:2 * na]
        send_sems, recv_sems = refs[2 * na:]
        x, y, c, _ = _place()
        cps = []
        for a in range(na):
            rows = h_refs[a].shape[0] // SWAP_CHUNKS
            assert rows * SWAP_CHUNKS == h_refs[a].shape[0]
            for q in range(SWAP_CHUNKS):
                k = SWAP_CHUNKS * a + q
                cps.append(pltpu.make_async_remote_copy(
                    src_ref=h_refs[a].at[pl.ds(q * rows, rows)], dst_ref=out_refs[a].at[pl.ds(q * rows, rows)],
                    send_sem=send_sems.at[k], recv_sem=recv_sems.at[k], device_id=(x, y, 1 - c), device_id_type=MESH))
        for cp in cps:
            cp.start()
        for cp in cps:
            cp.wait_recv()
        for cp in cps:
            cp.wait_send()

    return pl.pallas_call(
        body, name="pair_swap", out_shape=[jax.ShapeDtypeStruct(h.shape, h.dtype) for h in halves],
        in_specs=[HBM_SPEC] * na, out_specs=[HBM_SPEC] * na,
        scratch_shapes=[pltpu.SemaphoreType.DMA((SWAP_CHUNKS * na,)), pltpu.SemaphoreType.DMA((SWAP_CHUNKS * na,))])(*halves)


def _ag_small(v):
    m_per, n = v.shape

    def body(x_ref, out_ref, send_sems, recv_sems, local_sem):
        x, y, c, chips = _place()
        me, sibling = (x, y, c), (x, y, 1 - c)

        def rows(px, py, pc):
            return out_ref.at[pl.ds((4 * px + 2 * py + pc) * m_per, m_per), :]

        def copy(k, block, to, src=None):
            return pltpu.make_async_remote_copy(
                src_ref=rows(*block) if src is None else src, dst_ref=rows(*block), send_sem=send_sems.at[k],
                recv_sem=recv_sems.at[k], device_id=to, device_id_type=MESH)

        mine = pltpu.make_async_copy(x_ref, rows(*me), local_sem)
        mine.start()
        first = [copy(0, me, sibling, src=x_ref)]
        first += [copy(1 + k, me, (*chip, c), src=x_ref) for k, chip in enumerate(chips)]
        for cp in first:
            cp.start()
        passed = [copy(4 + k, (*chip, c), sibling) for k, chip in enumerate(chips)]
        for k, chip in enumerate(chips):
            copy(1 + k, (*chip, c), me).wait_recv()
            passed[k].start()
        copy(0, sibling, me).wait_recv()
        for k, chip in enumerate(chips):
            copy(4 + k, (*chip, 1 - c), me).wait_recv()
        for cp in first + passed:
            cp.wait_send()
        mine.wait()

    return pl.pallas_call(
        body, name="ag_small", out_shape=jax.ShapeDtypeStruct((8 * m_per, n), v.dtype),
        in_specs=[pl.BlockSpec(memory_space=pltpu.VMEM)], out_specs=pl.BlockSpec(memory_space=pltpu.VMEM),
        scratch_shapes=[pltpu.SemaphoreType.DMA((7,)), pltpu.SemaphoreType.DMA((7,)), pltpu.SemaphoreType.DMA])(v)


def _sum_blocks(a, nblk, name):
    rows, wd = a.shape
    r = rows // nblk
    tr = min(r, ROW_TILE)
    assert r % tr == 0

    def body(*refs):
        acc = refs[0][...].astype(F32)
        for ref in refs[1:nblk]:
            acc = acc + ref[...].astype(F32)
        refs[nblk][...] = acc

    nt = r // tr
    return pl.pallas_call(
        body, name=name, grid=(nt,),
        in_specs=[pl.BlockSpec((tr, wd), functools.partial(lambda i, b: (b * nt + i, 0), b=b)) for b in range(nblk)],
        out_specs=pl.BlockSpec((tr, wd), lambda i: (i, 0)),
        out_shape=jax.ShapeDtypeStruct((r, wd), F32), compiler_params=_cparams("parallel"))(*([a] * nblk))


def _row_tile(rows):
    best = max(t for t in range(16, 513, 16) if rows % t == 0)
    return best


def _sum_chips(by_src, csum, j, name):
    n, rh, wd = by_src.shape
    tr = _row_tile(rh)

    def body(j_ref, *refs):
        own = refs[n][0].astype(F32)
        acc = None
        for k in range(n):
            term = jnp.where(j_ref[0] == k, own, refs[k][0].astype(F32))
            acc = term if acc is None else acc + term
        refs[n + 1][...] = acc

    def other(k):
        return pl.BlockSpec((1, tr, wd), lambda i, jr: (jnp.where(jr[0] == k, (k + 1) % n, k), i, 0))

    grid_spec = pltpu.PrefetchScalarGridSpec(
        num_scalar_prefetch=1, grid=(rh // tr,),
        in_specs=[other(k) for k in range(n)] + [pl.BlockSpec((1, tr, wd), lambda i, jr: (jr[0], i, 0))],
        out_specs=pl.BlockSpec((tr, wd), lambda i, jr: (i, 0)))
    return pl.pallas_call(
        body, name=name, grid_spec=grid_spec, out_shape=jax.ShapeDtypeStruct((rh, wd), F32),
        compiler_params=_cparams("parallel"))(jnp.reshape(j, (1,)).astype(jnp.int32), *([by_src] * n), csum)


def _add_halves(gpack, other, c, name):
    n, _, rh, wd = gpack.shape
    tr = _row_tile(rh)

    def body(c_ref, g_ref, o_ref, out_ref):
        out_ref[0] = (g_ref[0, 0] + o_ref[0]).astype(BF16)

    grid_spec = pltpu.PrefetchScalarGridSpec(
        num_scalar_prefetch=1, grid=(n, rh // tr),
        in_specs=[pl.BlockSpec((1, 1, tr, wd), lambda p, i, cr: (p, cr[0], i, 0)),
                  pl.BlockSpec((1, tr, wd), lambda p, i, cr: (p, i, 0))],
        out_specs=pl.BlockSpec((1, tr, wd), lambda p, i, cr: (p, i, 0)))
    return pl.pallas_call(
        body, name=name, grid_spec=grid_spec, out_shape=jax.ShapeDtypeStruct((n, rh, wd), BF16),
        compiler_params=_cparams("parallel", "parallel"))(jnp.reshape(c, (1,)).astype(jnp.int32), gpack, other)


PACK_W = 1024
ROWS_O_DN = DN_W // N_CHIPS
ROWS_O_DIL = DIL_W * (D_MODEL // N_CHIPS) // PACK_W
ROWS_OUT = D_MODEL // N_CHIPS
ROWS_CONV = 4 * (3 * DN_W // N_CHIPS) // PACK_W
R1 = ROWS_O_DN
R2 = R1 + ROWS_O_DIL
R3 = R2 + ROWS_OUT
R4 = R3 + 16
R5 = R4 + 16
PACK_ROWS = 704
HALF_ROWS = PACK_ROWS // 2
SHARD_PAD = 2880


R6 = R5 + 2 * DN_HEADS

TILE_ROWS = 16
BA_IN_SHARD1 = REF_OFF_BA - SHARD_W
LOCAL_START = (0, SHARD_W, 2 * SHARD_W - 2 * DN_HEADS, 3 * SHARD_W - 2 * DN_HEADS)
LOCAL_END = LOCAL_START[1:] + (OFF_BA,)
WIN_BASE = tuple(s // TILE_ROWS * TILE_ROWS for s in LOCAL_START)


def _to_window(k, shard):
    nba = 2 * DN_HEADS
    body = shard
    if k == 1:
        row = lax.broadcasted_iota(jnp.int32, (SHARD_W - nba, 1), 0)
        body = jnp.where(row < BA_IN_SHARD1, shard[:SHARD_W - nba], shard[nba:])
    lead = LOCAL_START[k] - WIN_BASE[k]
    return jnp.pad(body, ((lead, SHARD_PAD - lead - body.shape[0]), (0, 0)))


def _from_window(k, win, ba):
    nba = 2 * DN_HEADS
    lead = LOCAL_START[k] - WIN_BASE[k]
    if k != 1:
        return win[lead:lead + SHARD_W]
    row = lax.broadcasted_iota(jnp.int32, (SHARD_W, 1), 0)
    before = win[lead:lead + SHARD_W]
    after = jnp.pad(win, ((nba, 0), (0, 0)))[lead:lead + SHARD_W]
    mid = jnp.pad(ba, ((BA_IN_SHARD1, SHARD_W - BA_IN_SHARD1 - nba), (0, 0)))
    return jnp.where(row < BA_IN_SHARD1, before, jnp.where(row < BA_IN_SHARD1 + nba, mid, after))


def _stack_windows(wins, ba):
    pieces = []
    for k in range(N_CHIPS):
        lo = WIN_BASE[k] + (TILE_ROWS if k else 0)
        hi = LOCAL_END[k] // TILE_ROWS * TILE_ROWS
        pieces.append(wins[k][lo - WIN_BASE[k]:hi - WIN_BASE[k]])
        if k + 1 < N_CHIPS:
            assert hi == WIN_BASE[k + 1]
            pieces.append(wins[k][hi - WIN_BASE[k]:hi - WIN_BASE[k] + TILE_ROWS] + wins[k + 1][:TILE_ROWS])
    pieces += [ba, jnp.zeros((PW - OFF_BA - ba.shape[0], ba.shape[1]), ba.dtype)]
    out = jnp.concatenate(pieces, axis=0)
    assert out.shape[0] == PW
    return out


def _local_step(x, tgt, norm_w, wpt, conv_full, a_log, dt_bias, dn_norm_w, w_o_dn, w_o_dil, w_out, final_norm_w):
    s = x.shape[0]
    h, h_t = _rms_in(x, norm_w)
    proj = _matmul(h, wpt, F32, 2048, 1280, 1024, "proj", nt=True)
    c_pre, qkv = _conv_fwd(proj, conv_full)
    gate_par = jnp.zeros((8, 128), F32).at[0, 8:16].set(a_log[0]).at[1, 8:16].set(dt_bias[0])
    bg = _gates_fwd(proj, gate_par)
    o_a, u, w, vn, tmat, states = _gdr_fwd(qkv, bg)
    oa2, oa2_t = _gdr_out(o_a, proj, dn_norm_w)
    ya = _matmul(oa2, w_o_dn, F32, 1024, 1024, 1024, "ya")
    parts = [_att_fwd(proj, g) for g in range(N_DIL)]
    ob, o_att, lse, ob_t = _att_merge(parts, proj)
    yb = _matmul(ob, w_o_dil, F32, 1024, 1024, 512, "yb")
    mg, mg_t = _merge(proj, ya, yb)
    t = _matmul(mg, w_out, F32, 1024, 1024, 1024, "t_out")
    dx2, dfw, lpart = _final(x, t, final_norm_w, tgt)

    dmg = _matmul(dx2, w_out, F32, 1024, 1024, 1024, "d_merged", nt=True)
    dw_out = _matmul(mg_t, dx2, F32, 1024, 1024, 1024, "dw_out")
    dya, dyb, dga, dgb = _merge_bwd(proj, ya, yb, dmg)
    doa2 = _matmul(dya, w_o_dn, F32, 1024, 1024, 1024, "d_oa2", nt=True)
    dw_o_dn = _matmul(oa2_t, dya, F32, 1024, 1024, 1024, "dw_o_dn")
    dob = _matmul(dyb, w_o_dil, F32, 1024, 512, 1024, "d_ob", nt=True)
    dw_o_dil = _matmul(ob_t, dyb, F32, 512, 1024, 1024, "dw_o_dil")
    do_a, dproj, ddnw = _gdr_out_bwd(o_a, proj, dn_norm_w, doa2)
    dq_a, dk_a, dv_a, dbg = _gdr_bwd(qkv, bg, u, w, vn, tmat, states, do_a)
    dproj, dpar = _gates_bwd(proj, gate_par, dbg, dproj)
    dc = _conv_bwd_act(c_pre, dq_a, dk_a, dv_a)
    dproj, dconv = _conv_bwd(proj, dc, conv_full, dproj)
    do_att, delta, dproj = _att_merge_bwd(o_att, proj, dob, dproj)
    dqkv_b = [_att_bwd(proj, g, do_att, lse, delta) for g in range(N_DIL)]
    pieces = [(OFF_Q_B + (N_DIL * i + g) * DIL_W, dqkv_b[g][i]) for i in range(3) for g in range(N_DIL)]
    for off, piece in pieces + [(OFF_G_A, dga), (OFF_G_B, dgb)]:
        dproj = lax.dynamic_update_slice(dproj, piece, (0, off))
    dwpt, dwpt_b = _matmul(h_t, dproj, F32, 1024, 1280, 2048, "dw_in", transpose_out=True, also_bf16=True)

    def finish(after=None):
        dh = _matmul(dproj, wpt, F32, 1024, 1024, 3840, "d_h", after=after)
        grad_x, dnw = _rms_in_bwd(x, norm_w, dh, dx2)
        small = jnp.zeros((8, PACK_W), F32)
        small = small.at[0].set(dnw[0]).at[1].set(dfw[0]).at[2, :DN_D].set(ddnw[0])
        small = small.at[3, :DN_HEADS].set(dpar[0, 8:16]).at[3, DN_HEADS:2 * DN_HEADS].set(dpar[1, 8:16])
        small = small.at[4, 0].set(lpart[0, 0])
        return grad_x, small

    return finish, (dwpt, dwpt_b), dconv, dw_o_dn, dw_o_dil, dw_out


def kernel(x, norm_w, w_in, conv_w, a_log, dt_bias, dn_norm_w, w_o_dn, w_o_dil, w_out, final_norm_w, loss_target, m_norm_w, m_w_in, m_conv_w, m_a_log, m_dt_bias, m_dn_norm_w, m_w_o_dn, m_w_o_dil, m_w_out, m_final_norm_w, v_norm_w, v_w_in, v_conv_w, v_a_log, v_dt_bias, v_dn_norm_w, v_w_o_dn, v_w_o_dil, v_w_out, v_final_norm_w):
    c = lax.axis_index("c")
    j = 2 * lax.axis_index("x") + lax.axis_index("y")
    qw = D_MODEL // N_CHIPS

    cw = conv_w[0].reshape(ROWS_CONV, PACK_W)
    cw = jnp.pad(cw, ((0, 16 - ROWS_CONV), (0, 0)))
    cw_hi = cw.astype(BF16)
    cw_lo = (cw - cw_hi.astype(F32)).astype(BF16)
    shard = w_in[0].T.astype(BF16)
    own_ba = jnp.where(j == 1, shard[BA_IN_SHARD1:BA_IN_SHARD1 + 2 * DN_HEADS], jnp.zeros((2 * DN_HEADS, D_MODEL), BF16))
    pack = jnp.concatenate(
        [w_o_dn[0].astype(BF16), w_o_dil[0].astype(BF16).reshape(ROWS_O_DIL, PACK_W), w_out[0].astype(BF16), cw_hi, cw_lo,
         own_ba, jnp.zeros((PACK_ROWS - R6, PACK_W), BF16)], axis=0).reshape(2, HALF_ROWS, PACK_W)
    chips = range(N_CHIPS)
    own_win = lax.switch(j, [functools.partial(_to_window, k) for k in chips], shard).reshape(2, SHARD_PAD // 2, D_MODEL)
    all_in, allw = _ag_weights([own_win, pack])
    wins = [all_in[k].reshape(SHARD_PAD, D_MODEL) for k in chips]
    allw = [allw[k].reshape(PACK_ROWS, PACK_W) for k in chips]
    wpt = _stack_windows(wins, allw[1][R5:R6])
    w_o_dn_full = jnp.concatenate([allw[k][:R1] for k in chips], axis=0)
    w_o_dil_full = jnp.concatenate([allw[k][R1:R2].reshape(DIL_W, qw) for k in chips], axis=1)
    w_out_full = jnp.concatenate([allw[k][R2:R3] for k in chips], axis=0)
    conv_full = jnp.concatenate(
        [(allw[k][R3:R3 + ROWS_CONV].astype(F32) + allw[k][R4:R4 + ROWS_CONV].astype(F32)).reshape(4, 3 * DN_W // N_CHIPS)
         for k in chips], axis=1)

    finish, (dwpt, dwpt_b), dconv, dw_o_dn, dw_o_dil, dw_out = _local_step(
        x[0], loss_target[0], norm_w, wpt, conv_full, a_log, dt_bias, dn_norm_w, w_o_dn_full, w_o_dil_full, w_out_full,
        final_norm_w.reshape(1, D_MODEL))

    cq = 3 * DN_W // N_CHIPS
    gpack = jnp.stack([
        jnp.concatenate(
            [dw_o_dn[k * qw:(k + 1) * qw], dw_o_dil[:, k * qw:(k + 1) * qw].reshape(ROWS_O_DIL, PACK_W),
             dw_out[k * qw:(k + 1) * qw],
             jnp.pad(dconv[:, k * cq:(k + 1) * cq].reshape(ROWS_CONV, PACK_W), ((0, 16 - ROWS_CONV), (0, 0))),
             dwpt[OFF_BA:OFF_BA + 2 * DN_HEADS] if k == 1 else jnp.zeros((2 * DN_HEADS, PACK_W), F32),
             jnp.zeros((PACK_ROWS - R4 - 2 * DN_HEADS, PACK_W), F32)], axis=0)
        for k in chips]).reshape(N_CHIPS, 2, HALF_ROWS, PACK_W)
    sib_in, sib_pack = _rs_pair(dwpt_b, gpack)
    csum_in = _add_halves_win(dwpt, sib_in, c)
    csum_pack = _add_halves(gpack, sib_pack, c, "add_halves_pack")
    send_sems, recv_sems, csums, lands, token = _rs_chips_start([csum_in, csum_pack])
    grad_x, small = finish(after=token)

    gs = _sum_blocks(_ag_small(small), 8, "sum_small")
    loss = gs[4, 0]
    w_small = jnp.zeros((8, PACK_W), F32)

    def pack_small(nw, fw, dnw_, al, db):
        t = w_small.at[0].set(nw[0]).at[1].set(fw).at[2, :DN_D].set(dnw_[0])
        return t.at[3, :DN_HEADS].set(al[0]).at[3, DN_HEADS:2 * DN_HEADS].set(db[0])

    sm = _adamw(pack_small(norm_w, final_norm_w, dn_norm_w, a_log, dt_bias), gs,
                pack_small(m_norm_w, m_final_norm_w, m_dn_norm_w, m_a_log, m_dt_bias),
                pack_small(v_norm_w, v_final_norm_w, v_dn_norm_w, v_a_log, v_dt_bias), "adamw_small")

    (csum_in, csum_pack), (src_in, src_pack) = _rs_chips_wait(send_sems, recv_sems, csums, lands, sm[0])
    half_in = _sum_chips(src_in, csum_in, j, "sum_chips_in")
    half_pack = _sum_chips(src_pack, csum_pack, j, "sum_chips_pack")
    sib_half_in, sib_half_pack = _pair_swap([half_in, half_pack])

    def both_halves(mine, theirs):
        return jnp.where(c == 0, jnp.concatenate([mine, theirs], axis=0), jnp.concatenate([theirs, mine], axis=0))

    g = both_halves(half_pack, sib_half_pack)
    g_w_in = lax.switch(j, [functools.partial(_from_window, k) for k in chips], both_halves(half_in, sib_half_in),
                        g[R4:R4 + 2 * DN_HEADS])
    g_w_o_dn = g[:R1]
    g_w_o_dil = g[R1:R2].reshape(DIL_W, qw)
    g_w_out = g[R2:R3]
    g_conv = g[R3:R3 + ROWS_CONV].reshape(4, cq)

    def unpack_small(t):
        return dict(norm_w=t[0:1], final_norm_w=t[1], dn_norm_w=t[2:3, :DN_D], a_log=t[3:4, :DN_HEADS],
                    dt_bias=t[3:4, DN_HEADS:2 * DN_HEADS])

    res = {"grad": unpack_small(gs)}
    for kind, arr in zip(("delta", "new_m", "new_v"), sm):
        res[kind] = unpack_small(arr)
    big = dict(conv_w=(conv_w, g_conv, m_conv_w, v_conv_w), w_o_dn=(w_o_dn, g_w_o_dn, m_w_o_dn, v_w_o_dn),
               w_o_dil=(w_o_dil, g_w_o_dil, m_w_o_dil, v_w_o_dil), w_out=(w_out, g_w_out, m_w_out, v_w_out))
    for name, (wt, gt, mt, vt) in big.items():
        d, nm, nv = _adamw(wt[0], gt, mt[0], vt[0], "adamw_" + name)
        res["grad"][name] = gt[None]
        res["delta"][name], res["new_m"][name], res["new_v"][name] = d[None], nm[None], nv[None]

    d, nm, nv = _adamw(w_in[0].T, g_w_in, m_w_in[0].T, v_w_in[0].T, "adamw_w_in")
    res["grad"]["w_in"] = g_w_in.T[None]
    res["delta"]["w_in"], res["new_m"]["w_in"], res["new_v"]["w_in"] = d.T[None], nm.T[None], nv.T[None]
    order = ["norm_w", "w_in", "conv_w", "a_log", "dt_bias", "dn_norm_w", "w_o_dn", "w_o_dil", "w_out", "final_norm_w"]
    outs = [loss, grad_x[None]]
    for kind in ("grad", "delta", "new_m", "new_v"):
        outs += [res[kind][nm] for nm in order]
    return tuple(outs)
```

```python
import functools
import math

import jax
import jax.numpy as jnp
from jax import lax
from jax.experimental import pallas as pl
from jax.experimental.pallas import tpu as pltpu

F32 = jnp.float32
BF16 = jnp.bfloat16
MESH = pl.DeviceIdType.MESH

D_MODEL = 1024
DN_HEADS = 8
DN_D = 128
DN_CHUNK = 64
DN_W = DN_HEADS * DN_D
DIL_GROUPS = ((128, 1), (512, 4), (2048, 16))
N_DIL = len(DIL_GROUPS)
DIL_HEADS = 4
DIL_DH = 128
DIL_W = DIL_HEADS * DIL_DH
ATT_BLOCK = 128
NORM_EPS = 1e-6
PROJ_W = 11280
N_CHIPS = 4
SHARD_W = PROJ_W // N_CHIPS

OFF_QKV_A = 0
OFF_Z_A = 3072
OFF_Q_B = 4096
OFF_K_B = 5632
OFF_V_B = 7168
OFF_Z_B = 8704
OFF_G_A = 9216
OFF_G_B = 10240
OFF_BA = 11264
PW = 11520
REF_OFF_BA = 4096

ADAM_LR = 0.001
ADAM_B1 = 0.9
ADAM_B2 = 0.999
ADAM_EPS = 1e-08
ADAM_WD = 0.01
ADAM_STEP = 10

ROW_TILE = 512
CONV_TILE = 1024
BIG_TILE = 1024
NEG = -1e30


def _dot(a, b):
    return jnp.dot(a.astype(BF16), b.astype(BF16), preferred_element_type=F32)


def _dot_nt(a, b):
    return lax.dot_general(a.astype(BF16), b.astype(BF16), (((1,), (1,)), ((), ())), preferred_element_type=F32)


def _dot_tn(a, b):
    return lax.dot_general(a.astype(BF16), b.astype(BF16), (((0,), (0,)), ((), ())), preferred_element_type=F32)


def _split(a):
    hi = a.astype(BF16)
    lo = (a - hi.astype(F32)).astype(BF16)
    return hi, lo


def _dot_exact_lhs(c, a):
    hi, lo = _split(a)
    cb = c.astype(BF16)
    return jnp.dot(cb, hi, preferred_element_type=F32) + jnp.dot(cb, lo, preferred_element_type=F32)


def _dot_tn_exact_rhs(a, c):
    hi, lo = _split(a)
    cb = c.astype(BF16)
    dn = (((0,), (0,)), ((), ()))
    return (lax.dot_general(hi, cb, dn, preferred_element_type=F32)
            + lax.dot_general(lo, cb, dn, preferred_element_type=F32))


def _sigmoid(x):
    return 1.0 / (1.0 + jnp.exp(-x))


def _silu(x):
    return x * _sigmoid(x)


def _silu_grad(x):
    s = _sigmoid(x)
    return s * (1.0 + x * (1.0 - s))


def _softplus(x):
    return jnp.maximum(x, 0.0) + jnp.log(1.0 + jnp.exp(-jnp.abs(x)))


def _cparams(*sem):
    return pltpu.CompilerParams(dimension_semantics=sem)


def _matmul(a, b, out_dtype, tm, tn, tk, name, nt=False, transpose_out=False, after=None, also_bf16=False):
    m, kdim = a.shape
    n = b.shape[0] if nt else b.shape[1]
    tm, tn, tk = min(tm, m), min(tn, n), min(tk, kdim)
    assert m % tm == 0 and n % tn == 0 and kdim % tk == 0, (name, a.shape, b.shape, tm, tn, tk)
    nk = kdim // tk
    dot = _dot_nt if nt else _dot
    b_spec = (pl.BlockSpec((tn, tk), lambda i, j, k: (j, k)) if nt else pl.BlockSpec((tk, tn), lambda i, j, k: (k, j)))
    extra = [] if after is None else [after]
    out_dtypes = [out_dtype] + ([BF16] if also_bf16 else [])

    def emit(o_refs, acc):
        val = acc.T if transpose_out else acc
        for o_ref in o_refs:
            o_ref[...] = val.astype(o_ref.dtype)

    def outs_of(rest):
        return rest[len(extra):len(extra) + len(out_dtypes)]

    if nk == 1:
        def body(a_ref, b_ref, *rest):
            emit(outs_of(rest), dot(a_ref[...], b_ref[...]))
        scratch = []
    else:
        def body(a_ref, b_ref, *rest):
            o_ref, acc_ref = outs_of(rest), rest[-1]
            k = pl.program_id(2)
            p = dot(a_ref[...], b_ref[...])

            @pl.when(k == 0)
            def _():
                acc_ref[...] = p

            @pl.when(k > 0)
            def _():
                acc_ref[...] += p

            @pl.when(k == nk - 1)
            def _():
                emit(o_ref, acc_ref[...])
        scratch = [pltpu.VMEM((tm, tn), F32)]

    if transpose_out:
        out_spec, out_shape = pl.BlockSpec((tn, tm), lambda i, j, k: (j, i)), (n, m)
    else:
        out_spec, out_shape = pl.BlockSpec((tm, tn), lambda i, j, k: (i, j)), (m, n)
    res = pl.pallas_call(
        body, name=name, grid=(m // tm, n // tn, nk),
        in_specs=[pl.BlockSpec((tm, tk), lambda i, j, k: (i, k)), b_spec] + [pl.BlockSpec(memory_space=pl.ANY)] * len(extra),
        out_specs=[out_spec] * len(out_dtypes), out_shape=[jax.ShapeDtypeStruct(out_shape, d) for d in out_dtypes],
        scratch_shapes=scratch, compiler_params=_cparams("parallel", "parallel", "arbitrary"))(a, b, *extra)
    return res if also_bf16 else res[0]


def _rms_in(x, nw):
    s, d = x.shape

    def body(x_ref, w_ref, h_ref, ht_ref):
        xv = x_ref[...]
        r = lax.rsqrt(jnp.mean(xv * xv, axis=-1, keepdims=True) + NORM_EPS)
        h = xv * r * w_ref[...]
        h_ref[...] = h.astype(BF16)
        ht_ref[...] = h.T.astype(BF16)

    return pl.pallas_call(
        body, name="rms_in", grid=(s // BIG_TILE,),
        in_specs=[pl.BlockSpec((BIG_TILE, d), lambda i: (i, 0)), pl.BlockSpec((1, d), lambda i: (0, 0))],
        out_specs=[pl.BlockSpec((BIG_TILE, d), lambda i: (i, 0)), pl.BlockSpec((d, BIG_TILE), lambda i: (0, i))],
        out_shape=[jax.ShapeDtypeStruct((s, d), BF16), jax.ShapeDtypeStruct((d, s), BF16)],
        compiler_params=_cparams("parallel"))(x, nw)


def _rms_in_bwd(x, nw, dh, dx2):
    s, d = x.shape

    def body(x_ref, w_ref, dh_ref, dx2_ref, dx_ref, dw_ref):
        i = pl.program_id(0)
        xv = x_ref[...]
        r = lax.rsqrt(jnp.mean(xv * xv, axis=-1, keepdims=True) + NORM_EPS)
        dhv = dh_ref[...]
        dyw = dhv * w_ref[...]
        dx_ref[...] = dx2_ref[...] + r * dyw - xv * (r * r * r) * jnp.mean(dyw * xv, axis=-1, keepdims=True)
        part = jnp.sum(dhv * xv * r, axis=0, keepdims=True)

        @pl.when(i == 0)
        def _():
            dw_ref[...] = part

        @pl.when(i > 0)
        def _():
            dw_ref[...] += part

    row = pl.BlockSpec((BIG_TILE, d), lambda i: (i, 0))
    vec = pl.BlockSpec((1, d), lambda i: (0, 0))
    return pl.pallas_call(
        body, name="rms_in_bwd", grid=(s // BIG_TILE,), in_specs=[row, vec, row, row], out_specs=[row, vec],
        out_shape=[jax.ShapeDtypeStruct((s, d), F32), jax.ShapeDtypeStruct((1, d), F32)],
        compiler_params=_cparams("arbitrary"))(x, nw, dh, dx2)


def _shift_down(cur, prev8, k):
    rc = pltpu.roll(cur, k, 0)
    rp = pltpu.roll(prev8, k, 0)
    row = lax.broadcasted_iota(jnp.int32, prev8.shape, 0)
    top = jnp.where(row < k, rp, rc[:8])
    return jnp.concatenate([top, rc[8:]], axis=0)


def _shift_up(cur, next8, k):
    t = cur.shape[0]
    rc = pltpu.roll(cur, t - k, 0)
    rn = pltpu.roll(next8, 8 - k, 0)
    row = lax.broadcasted_iota(jnp.int32, next8.shape, 0)
    bot = jnp.where(row >= 8 - k, rn, rc[t - 8:])
    return jnp.concatenate([rc[:t - 8], bot], axis=0)


def _conv_fwd(proj, conv_w):
    s = proj.shape[0]
    tile = min(s, CONV_TILE)
    t8 = tile // 8

    def body(u_ref, up_ref, w_ref, c_ref, y_ref):
        i = pl.program_id(0)
        part = pl.program_id(1)
        cur = u_ref[...]
        prev8 = jnp.where(i > 0, up_ref[...], 0.0)
        w = w_ref[...]
        c = cur * w[3:4, :]
        for k in (1, 2, 3):
            c = c + _shift_down(cur, prev8, k) * w[3 - k:4 - k, :]
        c_ref[...] = c
        a = _silu(c)
        for h in range(DN_HEADS):
            ah = a[:, h * DN_D:(h + 1) * DN_D]
            r = lax.rsqrt(jnp.sum(ah * ah, axis=-1, keepdims=True) + NORM_EPS)
            y_ref[:, h * DN_D:(h + 1) * DN_D] = jnp.where(part < 2, ah * r, ah)

    return pl.pallas_call(
        body, name="conv_fwd", grid=(s // tile, 3),
        in_specs=[pl.BlockSpec((tile, DN_W), lambda i, p: (i, p)),
                  pl.BlockSpec((8, DN_W), lambda i, p: (jnp.maximum(i * t8 - 1, 0), p)),
                  pl.BlockSpec((4, DN_W), lambda i, p: (0, p))],
        out_specs=[pl.BlockSpec((tile, DN_W), lambda i, p: (i, p))] * 2,
        out_shape=[jax.ShapeDtypeStruct((s, 3 * DN_W), F32)] * 2,
        compiler_params=_cparams("parallel", "parallel"))(proj, proj, conv_w)


def _act_bwd(cv, dyv, normalised):
    out = []
    for h in range(DN_HEADS):
        sl = slice(h * DN_D, (h + 1) * DN_D)
        ch, dyh = cv[:, sl], dyv[:, sl]
        ah = _silu(ch)
        r = lax.rsqrt(jnp.sum(ah * ah, axis=-1, keepdims=True) + NORM_EPS)
        dn = r * dyh - ah * (r * r * r) * jnp.sum(dyh * ah, axis=-1, keepdims=True)
        out.append(jnp.where(normalised, dn, dyh) * _silu_grad(ch))
    return jnp.concatenate(out, axis=1)


DPROJ_IN = pl.BlockSpec(memory_space=pl.ANY)


def _conv_bwd(proj, c_pre, dqkv, conv_w, dproj):
    s = proj.shape[0]
    tile = min(s, CONV_TILE)
    t8 = tile // 8
    nrow = s // tile
    last8 = s // 8 - 1

    def body(u_ref, c_ref, cn_ref, dy_ref, dyn_ref, w_ref, dproj_in, du_ref, dw_ref):
        i = pl.program_id(1)
        normalised = pl.program_id(0) < 2
        cur = u_ref[...]
        dcv = _act_bwd(c_ref[...], dy_ref[...], normalised)
        next8 = jnp.where(i < nrow - 1, _act_bwd(cn_ref[...], dyn_ref[...], normalised), 0.0)
        w = w_ref[...]

        @pl.when(i == 0)
        def _():
            dw_ref[...] = jnp.zeros_like(dw_ref)

        du = dcv * w[3:4, :]
        dw_ref[3:4, :] += jnp.sum(cur * dcv, axis=0, keepdims=True)
        for k in (1, 2, 3):
            ahead = _shift_up(dcv, next8, k)
            du = du + ahead * w[3 - k:4 - k, :]
            dw_ref[3 - k:4 - k, :] += jnp.sum(cur * ahead, axis=0, keepdims=True)
        du_ref[...] = du.astype(BF16)

    blk = pl.BlockSpec((tile, DN_W), lambda p, i: (i, p))
    nxt = pl.BlockSpec((8, DN_W), lambda p, i: (jnp.minimum((i + 1) * t8, last8), p))
    return pl.pallas_call(
        body, name="conv_bwd", grid=(3, nrow),
        in_specs=[blk, blk, nxt, blk, nxt, pl.BlockSpec((4, DN_W), lambda p, i: (0, p)), DPROJ_IN],
        out_specs=[blk, pl.BlockSpec((4, DN_W), lambda p, i: (0, p))],
        out_shape=[jax.ShapeDtypeStruct((s, PW), BF16), jax.ShapeDtypeStruct((4, 3 * DN_W), F32)],
        input_output_aliases={6: 0},
        compiler_params=_cparams("parallel", "arbitrary"))(proj, c_pre, c_pre, dqkv, dqkv, conv_w, dproj)


def _gates_fwd(proj, gate_par):
    s = proj.shape[0]

    def body(ba_ref, par_ref, o_ref):
        v = ba_ref[...]
        lane = lax.broadcasted_iota(jnp.int32, v.shape, 1)
        beta = _sigmoid(v)
        g = -jnp.exp(par_ref[0:1, :]) * _softplus(v + par_ref[1:2, :])
        o_ref[...] = jnp.where(lane < DN_HEADS, beta, jnp.where(lane < 2 * DN_HEADS, g, 0.0))

    return pl.pallas_call(
        body, name="gates_fwd", grid=(s // ROW_TILE,),
        in_specs=[pl.BlockSpec((ROW_TILE, 128), lambda i: (i, OFF_BA // 128)), pl.BlockSpec((8, 128), lambda i: (0, 0))],
        out_specs=pl.BlockSpec((ROW_TILE, 128), lambda i: (i, 0)),
        out_shape=jax.ShapeDtypeStruct((s, 128), F32), compiler_params=_cparams("parallel"))(proj, gate_par)


def _gates_bwd(proj, gate_par, dbg, dproj):
    s = proj.shape[0]

    def body(ba_ref, par_ref, d_ref, dproj_in, o_ref, dpar_ref):
        i = pl.program_id(0)
        v = ba_ref[...]
        dv = d_ref[...]
        lane = lax.broadcasted_iota(jnp.int32, v.shape, 1)
        beta = _sigmoid(v)
        nega = -jnp.exp(par_ref[0:1, :])
        xs = v + par_ref[1:2, :]
        dsp = dv * nega * _sigmoid(xs)
        dal = dv * nega * _softplus(xs)
        is_b = lane < DN_HEADS
        is_g = jnp.logical_and(lane >= DN_HEADS, lane < 2 * DN_HEADS)
        o_ref[:, :128] = jnp.where(is_b, dv * beta * (1.0 - beta), jnp.where(is_g, dsp, 0.0)).astype(BF16)
        o_ref[:, 128:] = jnp.zeros((ROW_TILE, PW - OFF_BA - 128), BF16)
        r0 = jnp.sum(jnp.where(is_g, dal, 0.0), axis=0, keepdims=True)
        r1 = jnp.sum(jnp.where(is_g, dsp, 0.0), axis=0, keepdims=True)

        @pl.when(i == 0)
        def _():
            dpar_ref[...] = jnp.zeros_like(dpar_ref)

        dpar_ref[0:1, :] += r0
        dpar_ref[1:2, :] += r1

    return pl.pallas_call(
        body, name="gates_bwd", grid=(s // ROW_TILE,),
        in_specs=[pl.BlockSpec((ROW_TILE, 128), lambda i: (i, OFF_BA // 128)), pl.BlockSpec((8, 128), lambda i: (0, 0)),
                  pl.BlockSpec((ROW_TILE, 128), lambda i: (i, 0)), DPROJ_IN],
        out_specs=[pl.BlockSpec((ROW_TILE, PW - OFF_BA), lambda i: (i, OFF_BA // (PW - OFF_BA))),
                   pl.BlockSpec((8, 128), lambda i: (0, 0))],
        out_shape=[jax.ShapeDtypeStruct((s, PW), BF16), jax.ShapeDtypeStruct((8, 128), F32)],
        input_output_aliases={3: 0},
        compiler_params=_cparams("arbitrary"))(proj, gate_par, dbg, dproj)


def _chunk_masks():
    c = DN_CHUNK
    ii = lax.broadcasted_iota(jnp.int32, (c, c), 0)
    jj = lax.broadcasted_iota(jnp.int32, (c, c), 1)
    return dict(ii=ii, jj=jj, lower=(ii >= jj), strict=(ii > jj),
                lower_f=(ii >= jj).astype(BF16), upper_f=(ii <= jj).astype(BF16))


class _Heads:
    def __init__(self, xs):
        self.xs = list(xs)

    def _bin(self, o, f):
        if isinstance(o, _Heads):
            return _Heads([f(a, b) for a, b in zip(self.xs, o.xs)])
        return _Heads([f(a, o) for a in self.xs])

    def __add__(self, o):
        return self._bin(o, lambda a, b: a + b)

    def __sub__(self, o):
        return self._bin(o, lambda a, b: a - b)

    def __mul__(self, o):
        return self._bin(o, lambda a, b: a * b)

    __radd__ = __add__
    __rmul__ = __mul__

    def __neg__(self):
        return _Heads([-a for a in self.xs])

    def __getitem__(self, i):
        return _Heads([a[i] for a in self.xs])


def _hmap(f, *args):
    n = next(len(a.xs) for a in args if isinstance(a, _Heads))
    return _Heads([f(*[(a.xs[h] if isinstance(a, _Heads) else a) for a in args]) for h in range(n)])


def _hdot(a, b):
    return _hmap(_dot, a, b)


def _hdot_nt(a, b):
    return _hmap(_dot_nt, a, b)


def _hdot_tn(a, b):
    return _hmap(_dot_tn, a, b)


def _hcat(a, b, axis):
    return _hmap(lambda x, y: jnp.concatenate([x, y], axis=axis), a, b)


def _hsum(a, axis):
    return _hmap(lambda t: jnp.sum(t, axis=axis, keepdims=True), a)


def _hwhere(c, a, b):
    return _hmap(jnp.where, c, a, b)


def _chunk_gates(mk, bg):
    c = DN_CHUNK
    gc_all = _dot_exact_lhs(mk["lower_f"], bg)
    rows = jnp.concatenate([gc_all, gc_all], axis=0).T
    hs = range(DN_HEADS)
    return (_Heads(bg[:, h:h + 1] for h in hs), _Heads(gc_all[:, DN_HEADS + h:DN_HEADS + h + 1] for h in hs),
            _Heads(rows[DN_HEADS + h:DN_HEADS + h + 1, :] for h in hs))


def _chunk_common(mk, q, k, beta_col, gc_col, gc_r):
    c = DN_CHUNK
    lower, strict = mk["lower"], mk["strict"]
    qs = q * (DN_D ** -0.5)
    beta_b = _hmap(lambda t: jnp.broadcast_to(t, (c, DN_D)), beta_col)
    gc_b = _hmap(lambda t: jnp.broadcast_to(t, (c, DN_D)), gc_col)
    gc_sq = gc_b[:, :c]
    gam = _hwhere(lower, _hmap(lambda t: jnp.exp(jnp.minimum(t, 0.0)), gc_sq - gc_r[:, :c]), 0.0)
    egc = _hmap(jnp.exp, gc_b)
    gl = gc_b[c - 1:c, :]
    ekd = _hmap(jnp.exp, gl - gc_b)
    dl = _hmap(jnp.exp, gl)
    kb = k * beta_b
    scores = _hdot_nt(_hcat(kb, qs, 0), k)
    a_strict = _hwhere(strict, scores[:c] * gam, 0.0)
    aqk = _hwhere(lower, scores[c:] * gam, 0.0)
    return dict(k=k, qs=qs, beta_b=beta_b, gc_b=gc_b, gam=gam, egc=egc, ekd=ekd, dl=dl, kb=kb, a_strict=a_strict, aqk=aqk)


def _unit_lower_inverse_minus_eye(n_strict, ii, jj):
    same = lax.shift_right_logical(ii, 4) == lax.shift_right_logical(jj, 4)
    dmat = _hwhere(same, n_strict, 0.0)
    omat = n_strict - dmat
    d2 = _hdot(dmat, dmat)
    d4 = _hdot(d2, d2)
    d8 = _hdot(d4, d4)
    x1 = d2 - dmat - _hdot(dmat, d2)
    x2 = x1 + d4 + _hdot(x1, d4)
    x3 = x2 + d8 + _hdot(x2, d8)
    n1 = omat + _hdot(x3, omat)
    n2 = _hdot(n1, n1)
    y = n2 - n1 - _hdot(n1, n2)
    return y + x3 + _hdot(y, x3)


GDR_HEAD_SETS = (range(0, DN_HEADS),)


def _gdr_fwd(qkv, bg):
    s = qkv.shape[0]
    c = DN_CHUNK
    n = s // c

    def body(q_ref, k_ref, v_ref, bg_ref, o_ref, u_ref, w_ref, vn_ref, tm_ref, st_ref, state):
        @pl.when(pl.program_id(0) == 0)
        def _():
            state[...] = jnp.zeros_like(state)

        mk = _chunk_masks()
        gates = _chunk_gates(mk, bg_ref[...])
        for hs in GDR_HEAD_SETS:
            sls = [slice(h * DN_D, (h + 1) * DN_D) for h in hs]
            cm = _chunk_common(mk, _Heads(q_ref[:, sl] for sl in sls), _Heads(k_ref[:, sl] for sl in sls),
                               *[_Heads(g.xs[h] for h in hs) for g in gates])
            tm = _unit_lower_inverse_minus_eye(cm["a_strict"], mk["ii"], mk["jj"])
            rhs_u = _Heads(v_ref[:, sl] for sl in sls) * cm["beta_b"]
            rhs_w = cm["kb"] * cm["egc"]
            t_rhs = _hdot(tm, _hcat(rhs_u, rhs_w, 1))
            u = rhs_u + t_rhs[:, :DN_D]
            w = rhs_w + t_rhs[:, DN_D:]
            st = _Heads(state[h] for h in hs)
            on_state = _hdot(_hcat(w, cm["qs"] * cm["egc"], 0), st)
            v_new = u - on_state[:c]
            o = on_state[c:] + _hdot(cm["aqk"], v_new)
            st_new = st * cm["dl"] + _hdot_tn(cm["k"] * cm["ekd"], v_new)
            for i, (h, sl) in enumerate(zip(hs, sls)):
                o_ref[:, sl] = o.xs[i]
                u_ref[:, sl] = u.xs[i]
                w_ref[:, sl] = w.xs[i]
                vn_ref[:, sl] = v_new.xs[i]
                tm_ref[h, 0] = tm.xs[i]
                st_ref[h, 0] = st.xs[i]
                state[h] = st_new.xs[i]

    def part(p):
        return pl.BlockSpec((c, DN_W), lambda j: (j, p))

    return pl.pallas_call(
        body, name="gdr_fwd", grid=(n,),
        in_specs=[part(0), part(1), part(2), pl.BlockSpec((c, 128), lambda j: (j, 0))],
        out_specs=[part(0)] * 4 + [pl.BlockSpec((DN_HEADS, 1, c, c), lambda j: (0, j, 0, 0)),
                                   pl.BlockSpec((DN_HEADS, 1, DN_D, DN_D), lambda j: (0, j, 0, 0))],
        out_shape=[jax.ShapeDtypeStruct((s, DN_W), F32)] * 4
        + [jax.ShapeDtypeStruct((DN_HEADS, n, c, c), F32), jax.ShapeDtypeStruct((DN_HEADS, n, DN_D, DN_D), F32)],
        scratch_shapes=[pltpu.VMEM((DN_HEADS, DN_D, DN_D), F32)],
        compiler_params=_cparams("arbitrary"))(qkv, qkv, qkv, bg)


def _gdr_bwd(qkv, bg, u, w, vn, tmat, states, do):
    s = qkv.shape[0]
    c = DN_CHUNK
    n = s // c

    def body(q_ref, k_ref, v_ref, bg_ref, u_ref, w_ref, vn_ref, tm_ref, st_ref, do_ref,
             dqkv_ref, dbg_ref, dstate):
        @pl.when(pl.program_id(0) == 0)
        def _():
            dstate[...] = jnp.zeros_like(dstate)

        mk = _chunk_masks()
        lower, strict = mk["lower"], mk["strict"]
        bg = bg_ref[...]
        ones = jnp.ones((c, DN_D), BF16)
        rowi = lax.broadcasted_iota(jnp.int32, (c, DN_D), 0)
        lane = lax.broadcasted_iota(jnp.int32, (c, 128), 1)
        hs = range(DN_HEADS)
        sls = [slice(h * DN_D, (h + 1) * DN_D) for h in hs]

        def heads_of(ref):
            return _Heads(ref[:, sl] for sl in sls)

        cm = _chunk_common(mk, heads_of(q_ref), heads_of(k_ref), *_chunk_gates(mk, bg))
        k, qs, beta_b = cm["k"], cm["qs"], cm["beta_b"]
        gam, egc, ekd, dl, kb = cm["gam"], cm["egc"], cm["ekd"], cm["dl"], cm["kb"]
        aqk, a_strict = cm["aqk"], cm["a_strict"]
        v, uu, ww, v_new, dov = heads_of(v_ref), heads_of(u_ref), heads_of(w_ref), heads_of(vn_ref), heads_of(do_ref)
        st = _Heads(st_ref[h, 0] for h in hs)
        dsn = _Heads(dstate[h] for h in hs)
        qd = qs * egc
        kd = k * ekd

        dv_new = _hdot_tn(aqk, dov) + _hdot(kd, dsn)
        do_sv = _hdot_nt(dov, _hcat(st, v_new, 0))
        dqd = do_sv[:, :DN_D]
        daqk = _hwhere(lower, do_sv[:, DN_D:], 0.0)
        dkd = _hdot_nt(v_new, dsn)
        ddl = _hsum(_hsum(dsn * st, 1), 0)
        dw = -_hdot_nt(dv_new, st)
        ds_new = dsn * dl + _hdot_tn(_hcat(qd, -ww, 0), _hcat(dov, dv_new, 0))

        tm = _Heads(tm_ref[h, 0] for h in hs)
        tt = _hdot_tn(tm, _hcat(dv_new, dw, 1))
        dru = dv_new + tt[:, :DN_D]
        drw = dw + tt[:, DN_D:]
        dn = _hwhere(strict, -_hdot_nt(_hcat(dru, drw, 1), _hcat(uu, ww, 1)), 0.0)
        dag = dn * gam
        dqg = daqk * gam
        both = _hcat(dag, dqg, 0)
        on_k = _hdot(both, k)
        dkb = on_k[:c] + drw * egc
        dqs = on_k[c:] + dqd * egc
        dk = _hdot_tn(both, _hcat(kb, qs, 0)) + dkb * beta_b + dkd * ekd
        pmat = dn * a_strict + daqk * aqk
        tkd = _hsum(dkd * kd, -1)
        dgc = (_hsum(pmat, -1) - _hmap(_dot_tn_exact_rhs, pmat, ones) + _hsum(drw * (kb * egc), -1)
               + _hsum(dqd * qd, -1) - tkd)
        last = _hsum(tkd, 0) + ddl * dl
        dgc = dgc + _hwhere(rowi == c - 1, last, 0.0)
        dbeta = _hsum(dru * v, -1) + _hsum(dkb * k, -1)
        dq = dqs * (DN_D ** -0.5)
        dv = dru * beta_b

        dgc_all = jnp.zeros((c, 128), F32)
        dbg = jnp.zeros((c, 128), F32)
        for h, sl in zip(hs, sls):
            dqkv_ref[:, sl] = dq.xs[h]
            dqkv_ref[:, DN_W + h * DN_D:DN_W + (h + 1) * DN_D] = dk.xs[h]
            dqkv_ref[:, 2 * DN_W + h * DN_D:2 * DN_W + (h + 1) * DN_D] = dv.xs[h]
            dstate[h] = ds_new.xs[h]
            dgc_all = dgc_all + jnp.where(lane == DN_HEADS + h, dgc.xs[h], 0.0)
            dbg = dbg + jnp.where(lane == h, dbeta.xs[h], 0.0)
        dbg_ref[...] = dbg + _dot_exact_lhs(mk["upper_f"], dgc_all)

    def part(p):
        return pl.BlockSpec((c, DN_W), lambda j: (n - 1 - j, p))

    vec = pl.BlockSpec((c, 128), lambda j: (n - 1 - j, 0))
    return pl.pallas_call(
        body, name="gdr_bwd", grid=(n,),
        in_specs=[part(0), part(1), part(2), vec, part(0), part(0), part(0),
                  pl.BlockSpec((DN_HEADS, 1, c, c), lambda j: (0, n - 1 - j, 0, 0)),
                  pl.BlockSpec((DN_HEADS, 1, DN_D, DN_D), lambda j: (0, n - 1 - j, 0, 0)), part(0)],
        out_specs=[pl.BlockSpec((c, 3 * DN_W), lambda j: (n - 1 - j, 0)), vec],
        out_shape=[jax.ShapeDtypeStruct((s, 3 * DN_W), F32), jax.ShapeDtypeStruct((s, 128), F32)],
        scratch_shapes=[pltpu.VMEM((DN_HEADS, DN_D, DN_D), F32)],
        compiler_params=_cparams("arbitrary"))(qkv, qkv, qkv, bg, u, w, vn, tmat, states, do)


def _gdr_out(o, proj, dnw):
    s = o.shape[0]

    def body(o_ref, z_ref, w_ref, y_ref, yt_ref):
        ov, zv, wv = o_ref[...], z_ref[...], w_ref[...]
        for h in range(DN_HEADS):
            sl = slice(h * DN_D, (h + 1) * DN_D)
            oh = ov[:, sl]
            r = lax.rsqrt(jnp.mean(oh * oh, axis=-1, keepdims=True) + NORM_EPS)
            y = (oh * r * wv) * _silu(zv[:, sl])
            y_ref[:, sl] = y.astype(BF16)
            yt_ref[sl, :] = y.T.astype(BF16)

    row = pl.BlockSpec((BIG_TILE, DN_W), lambda i: (i, 0))
    return pl.pallas_call(
        body, name="gdr_out", grid=(s // BIG_TILE,),
        in_specs=[row, pl.BlockSpec((BIG_TILE, DN_W), lambda i: (i, OFF_Z_A // DN_W)), pl.BlockSpec((1, DN_D), lambda i: (0, 0))],
        out_specs=[row, pl.BlockSpec((DN_W, BIG_TILE), lambda i: (0, i))],
        out_shape=[jax.ShapeDtypeStruct((s, DN_W), BF16), jax.ShapeDtypeStruct((DN_W, s), BF16)],
        compiler_params=_cparams("parallel"))(o, proj, dnw)


def _gdr_out_bwd(o, proj, dnw, dy):
    s = o.shape[0]

    def body(o_ref, z_ref, w_ref, dy_ref, do_ref, dz_ref, dw_ref):
        i = pl.program_id(0)
        ov, zv, wv, dyv = o_ref[...], z_ref[...], w_ref[...], dy_ref[...]
        acc = jnp.zeros((1, DN_D), F32)
        for h in range(DN_HEADS):
            sl = slice(h * DN_D, (h + 1) * DN_D)
            oh, zh, dh = ov[:, sl], zv[:, sl], dyv[:, sl]
            r = lax.rsqrt(jnp.mean(oh * oh, axis=-1, keepdims=True) + NORM_EPS)
            dn = dh * _silu(zh)
            dz_ref[:, sl] = (dh * (oh * r * wv) * _silu_grad(zh)).astype(BF16)
            acc = acc + jnp.sum(dn * oh * r, axis=0, keepdims=True)
            dnw_ = dn * wv
            do_ref[:, sl] = r * dnw_ - oh * (r * r * r) * jnp.mean(dnw_ * oh, axis=-1, keepdims=True)

        @pl.when(i == 0)
        def _():
            dw_ref[...] = acc

        @pl.when(i > 0)
        def _():
            dw_ref[...] += acc

    row = pl.BlockSpec((ROW_TILE, DN_W), lambda i: (i, 0))
    vec = pl.BlockSpec((1, DN_D), lambda i: (0, 0))
    return pl.pallas_call(
        body, name="gdr_out_bwd", grid=(s // ROW_TILE,),
        in_specs=[row, pl.BlockSpec((ROW_TILE, DN_W), lambda i: (i, OFF_Z_A // DN_W)), vec, row],
        out_specs=[row, pl.BlockSpec((ROW_TILE, DN_W), lambda i: (i, OFF_Z_A // DN_W)), vec],
        out_shape=[jax.ShapeDtypeStruct((s, DN_W), F32), jax.ShapeDtypeStruct((s, PW), BF16),
                   jax.ShapeDtypeStruct((1, DN_D), F32)],
        compiler_params=_cparams("arbitrary"))(o, proj, dnw, dy)


def _slope(group, head):
    idx = (group * DIL_HEADS + head + 1).astype(F32)
    return jnp.exp(jnp.full((1, 128), -8.0 * math.log(2.0) / (N_DIL * DIL_HEADS), F32) * idx)


def _att_scores(qb, k_cur, k_prev, slope_d, has_prev):
    iq = lax.broadcasted_iota(jnp.int32, (ATT_BLOCK, ATT_BLOCK), 0)
    jk = lax.broadcasted_iota(jnp.int32, (ATT_BLOCK, ATT_BLOCK), 1)
    dist_c = (iq - jk).astype(F32)
    s_cur = jnp.where(iq >= jk, _dot_nt(qb, k_cur) - slope_d * dist_c, NEG)
    s_prev = jnp.where(jnp.logical_and(jk >= iq, has_prev),
                       _dot_nt(qb, k_prev) - slope_d * (dist_c + float(ATT_BLOCK)), NEG)
    return s_cur, s_prev


def _att_scores_whole(qb, k, slope_d):
    n = 2 * ATT_BLOCK
    dist = lax.broadcasted_iota(jnp.int32, (n, n), 0) - lax.broadcasted_iota(jnp.int32, (n, n), 1)
    valid = jnp.logical_and(dist >= 0, dist <= ATT_BLOCK)
    return jnp.where(valid, _dot_nt(qb, k) - slope_d[:, 0:1] * dist.astype(F32), NEG)


def _att_tiles(i, dil, nb):
    tiles = nb // 2
    per = dil * tiles // ATT_UNROLL
    assert nb % 2 == 0 and tiles >= 2 and per * ATT_UNROLL == dil * tiles
    for i0 in range(per):
        ts = [divmod(i0 + u * per, tiles) for u in range(ATT_UNROLL)]
        assert all(a[0] != b[0] or abs(a[1] - b[1]) >= 2 for n, a in enumerate(ts) for b in ts[n + 1:])
    qrows, krows, has_prev = [], [], []
    for u in range(ATT_UNROLL):
        t = i + u * per
        r = lax.div(t, tiles)
        j = lax.rem(t, tiles)
        qbase = r + dil * 2 * ATT_BLOCK * j
        kbase = qbase - dil * ATT_BLOCK * jnp.minimum(j, 1)
        if dil == 1:
            qbase, kbase = pl.multiple_of(qbase, ATT_BLOCK), pl.multiple_of(kbase, ATT_BLOCK)
        qrows.append(pl.ds(qbase, 2 * ATT_BLOCK, stride=dil))
        krows.append(pl.ds(kbase, 3 * ATT_BLOCK, stride=dil))
        has_prev.append(j > 0)
    return qrows, krows, has_prev


def _att_scores_tile(qb, k, slope_d, has_prev):
    iq = lax.broadcasted_iota(jnp.int32, (2 * ATT_BLOCK, 3 * ATT_BLOCK), 0)
    ck = lax.broadcasted_iota(jnp.int32, (2 * ATT_BLOCK, 3 * ATT_BLOCK), 1)
    dist = iq - ck + jnp.where(has_prev, ATT_BLOCK, 0)
    valid = jnp.logical_and(dist >= 0, dist <= ATT_BLOCK)
    return jnp.where(valid, _dot_nt(qb, k) - slope_d[:, 0:1] * dist.astype(F32), NEG)


ATT_UNROLL = 4


def _att_blocks(i, dil, nb):
    per = dil * nb // ATT_UNROLL
    assert per * ATT_UNROLL == dil * nb
    for i0 in range(per):
        blocks = [divmod(i0 + u * per, nb) for u in range(ATT_UNROLL)]
        assert all(a[0] != b[0] or abs(a[1] - b[1]) >= 2 for n, a in enumerate(blocks) for b in blocks[n + 1:])
    curs, prvs, has_prev = [], [], []
    for u in range(ATT_UNROLL):
        t = i + u * per
        r = lax.div(t, nb)
        j = lax.rem(t, nb)
        base = r + dil * ATT_BLOCK * j
        pbase = base - dil * ATT_BLOCK * jnp.minimum(j, 1)
        if dil == 1:
            base, pbase = pl.multiple_of(base, ATT_BLOCK), pl.multiple_of(pbase, ATT_BLOCK)
        curs.append(pl.ds(base, ATT_BLOCK, stride=dil))
        prvs.append(pl.ds(pbase, ATT_BLOCK, stride=dil))
        has_prev.append(j > 0)
    return curs, prvs, has_prev


def _att_fwd(proj, group):
    s = proj.shape[0]
    dil = DIL_GROUPS[group][1]
    assert DIL_GROUPS[group][0] // dil == ATT_BLOCK
    nb = s // dil // ATT_BLOCK
    assert nb * dil * ATT_BLOCK == s

    def body(q_ref, k_ref, v_ref, o_ref, lse_ref):
        def emit(rows, num, den, mx):
            o_ref[rows, :] = num / den
            lse_ref[rows, :] = jnp.broadcast_to(mx + jnp.log(den), (num.shape[0], DIL_DH))

        slope_d = _slope(group, pl.program_id(0)) * float(dil)

        def step(i, carry):
            curs, prvs, has_prev = _att_blocks(i, dil, nb)
            us = range(ATT_UNROLL)
            qb = [q_ref[c, :] * (DIL_DH ** -0.5) for c in curs]
            sc = [_att_scores(qb[u], k_ref[curs[u], :], k_ref[prvs[u], :], slope_d, has_prev[u]) for u in us]
            mx = [jnp.maximum(jnp.max(a, axis=-1, keepdims=True), jnp.max(b, axis=-1, keepdims=True)) for a, b in sc]
            p_cur = [jnp.exp(sc[u][0] - mx[u]) for u in us]
            p_prev = [jnp.exp(sc[u][1] - mx[u]) for u in us]
            den = [jnp.sum(p_cur[u], axis=-1, keepdims=True) + jnp.sum(p_prev[u], axis=-1, keepdims=True) for u in us]
            num = [_dot(p_cur[u], v_ref[curs[u], :]) + _dot(p_prev[u], v_ref[prvs[u], :]) for u in us]
            for u in us:
                emit(curs[u], num[u], den[u], mx[u])
            return carry

        def step_whole(i, carry):
            rows = [pl.ds(i * ATT_UNROLL + u, 2 * ATT_BLOCK, stride=dil) for u in range(ATT_UNROLL)]
            sc = [_att_scores_whole(q_ref[r, :] * (DIL_DH ** -0.5), k_ref[r, :], slope_d) for r in rows]
            mx = [jnp.max(a, axis=-1, keepdims=True) for a in sc]
            p = [jnp.exp(a - m) for a, m in zip(sc, mx)]
            num = [_dot(pu, v_ref[r, :]) for pu, r in zip(p, rows)]
            for u, r in enumerate(rows):
                emit(r, num[u], jnp.sum(p[u], axis=-1, keepdims=True), mx[u])
            return carry

        def step_tile(i, carry):
            qrows, krows, has_prev = _att_tiles(i, dil, nb)
            us = range(ATT_UNROLL)
            sc = [_att_scores_tile(q_ref[qrows[u], :] * (DIL_DH ** -0.5), k_ref[krows[u], :], slope_d, has_prev[u]) for u in us]
            mx = [jnp.max(a, axis=-1, keepdims=True) for a in sc]
            p = [jnp.exp(a - m) for a, m in zip(sc, mx)]
            num = [_dot(p[u], v_ref[krows[u], :]) for u in us]
            for u in us:
                emit(qrows[u], num[u], jnp.sum(p[u], axis=-1, keepdims=True), mx[u])
            return carry

        if nb == 2:
            lax.fori_loop(0, dil // ATT_UNROLL, step_whole, 0)
        elif nb % 2 == 0:
            lax.fori_loop(0, dil * nb // 2 // ATT_UNROLL, step_tile, 0)
        else:
            lax.fori_loop(0, dil * nb // ATT_UNROLL, step, 0)

    def col(off):
        return pl.BlockSpec((s, DIL_DH), lambda h: (0, off // DIL_DH + group * DIL_HEADS + h))

    out = pl.BlockSpec((s, DIL_DH), lambda h: (0, h))
    return pl.pallas_call(
        body, name=f"att_fwd{group}", grid=(DIL_HEADS,), in_specs=[col(OFF_Q_B), col(OFF_K_B), col(OFF_V_B)],
        out_specs=[out, out], out_shape=[jax.ShapeDtypeStruct((s, DIL_W), F32)] * 2,
        compiler_params=_cparams("parallel"))(proj, proj, proj)


def _att_bwd(proj, group, do, lse, delta):
    s = proj.shape[0]
    dil = DIL_GROUPS[group][1]
    nb = s // dil // ATT_BLOCK

    def body(q_ref, k_ref, v_ref, do_ref, lse_ref, dl_ref, dq_ref, dk_ref, dv_ref, dq_acc, dk_acc, dv_acc):
        slope_d = _slope(group, pl.program_id(0)) * float(dil)
        dk_acc[...] = jnp.zeros_like(dk_acc)
        dv_acc[...] = jnp.zeros_like(dv_acc)

        def step(i, carry):
            curs, prvs, has_prev = _att_blocks(i, dil, nb)
            us = range(ATT_UNROLL)
            qb = [q_ref[c, :] * (DIL_DH ** -0.5) for c in curs]
            k_cur, k_prev = [k_ref[c, :] for c in curs], [k_ref[p, :] for p in prvs]
            v_cur, v_prev = [v_ref[c, :] for c in curs], [v_ref[p, :] for p in prvs]
            sc = [_att_scores(qb[u], k_cur[u], k_prev[u], slope_d, has_prev[u]) for u in us]
            lse_b, delta_b, dob = [lse_ref[c, :] for c in curs], [dl_ref[c, :] for c in curs], [do_ref[c, :] for c in curs]
            p_cur = [jnp.exp(sc[u][0] - lse_b[u]) for u in us]
            p_prev = [jnp.exp(sc[u][1] - lse_b[u]) for u in us]
            ds_cur = [p_cur[u] * (_dot_nt(dob[u], v_cur[u]) - delta_b[u]) for u in us]
            ds_prev = [p_prev[u] * (_dot_nt(dob[u], v_prev[u]) - delta_b[u]) for u in us]
            dq = [(_dot(ds_cur[u], k_cur[u]) + _dot(ds_prev[u], k_prev[u])) * (DIL_DH ** -0.5) for u in us]
            dk_c = [_dot_tn(ds_cur[u], qb[u]) for u in us]
            dv_c = [_dot_tn(p_cur[u], dob[u]) for u in us]
            dk_p = [_dot_tn(ds_prev[u], qb[u]) for u in us]
            dv_p = [_dot_tn(p_prev[u], dob[u]) for u in us]
            for u in us:
                dq_acc[curs[u], :] = dq[u]
                dk_acc[curs[u], :] += dk_c[u]
                dv_acc[curs[u], :] += dv_c[u]
            for u in us:
                dk_acc[prvs[u], :] += dk_p[u]
                dv_acc[prvs[u], :] += dv_p[u]
            return carry

        def step_whole(i, carry):
            rows = [pl.ds(i * ATT_UNROLL + u, 2 * ATT_BLOCK, stride=dil) for u in range(ATT_UNROLL)]
            qb = [q_ref[r, :] * (DIL_DH ** -0.5) for r in rows]
            kk, vv, dob = [k_ref[r, :] for r in rows], [v_ref[r, :] for r in rows], [do_ref[r, :] for r in rows]
            sc = [_att_scores_whole(qb[u], kk[u], slope_d) for u in range(ATT_UNROLL)]
            p = [jnp.exp(sc[u] - lse_ref[r, :][:, 0:1]) for u, r in enumerate(rows)]
            ds = [p[u] * (_dot_nt(dob[u], vv[u]) - dl_ref[r, :][:, 0:1]) for u, r in enumerate(rows)]
            dq = [_dot(ds[u], kk[u]) * (DIL_DH ** -0.5) for u in range(ATT_UNROLL)]
            dk = [_dot_tn(ds[u], qb[u]) for u in range(ATT_UNROLL)]
            dv = [_dot_tn(p[u], dob[u]) for u in range(ATT_UNROLL)]
            for u, r in enumerate(rows):
                dq_acc[r, :] = dq[u]
                dk_acc[r, :] = dk[u]
                dv_acc[r, :] = dv[u]
            return carry

        def step_tile(i, carry):
            qrows, krows, has_prev = _att_tiles(i, dil, nb)
            us = range(ATT_UNROLL)
            qb = [q_ref[r, :] * (DIL_DH ** -0.5) for r in qrows]
            kk, vv, dob = [k_ref[r, :] for r in krows], [v_ref[r, :] for r in krows], [do_ref[r, :] for r in qrows]
            sc = [_att_scores_tile(qb[u], kk[u], slope_d, has_prev[u]) for u in us]
            p = [jnp.exp(sc[u] - lse_ref[qrows[u], :][:, 0:1]) for u in us]
            ds = [p[u] * (_dot_nt(dob[u], vv[u]) - dl_ref[qrows[u], :][:, 0:1]) for u in us]
            dq = [_dot(ds[u], kk[u]) * (DIL_DH ** -0.5) for u in us]
            dk = [_dot_tn(ds[u], qb[u]) for u in us]
            dv = [_dot_tn(p[u], dob[u]) for u in us]
            for u in us:
                dq_acc[qrows[u], :] = dq[u]
                dk_acc[krows[u], :] += dk[u]
                dv_acc[krows[u], :] += dv[u]
            return carry

        if nb == 2:
            lax.fori_loop(0, dil // ATT_UNROLL, step_whole, 0)
        elif nb % 2 == 0:
            lax.fori_loop(0, dil * nb // 2 // ATT_UNROLL, step_tile, 0)
        else:
            lax.fori_loop(0, dil * nb // ATT_UNROLL, step, 0)
        dq_ref[...] = dq_acc[...].astype(BF16)
        dk_ref[...] = dk_acc[...].astype(BF16)
        dv_ref[...] = dv_acc[...].astype(BF16)

    def col(off):
        return pl.BlockSpec((s, DIL_DH), lambda h: (0, off // DIL_DH + group * DIL_HEADS + h))

    hd = pl.BlockSpec((s, DIL_DH), lambda h: (0, h))
    return pl.pallas_call(
        body, name=f"att_bwd{group}", grid=(DIL_HEADS,),
        in_specs=[col(OFF_Q_B), col(OFF_K_B), col(OFF_V_B), hd, hd, hd], out_specs=[hd, hd, hd],
        out_shape=[jax.ShapeDtypeStruct((s, DIL_W), BF16)] * 3,
        scratch_shapes=[pltpu.VMEM((s, DIL_DH), F32)] * 3,
        compiler_params=_cparams("parallel"))(proj, proj, proj, do, lse, delta)


def _att_merge(parts, proj):
    s = proj.shape[0]

    def body(o0, l0, o1, l1, o2, l2, z_ref, ob_ref, o_ref, lse_ref, obt_ref):
        m = jnp.maximum(jnp.maximum(l0[...], l1[...]), l2[...])
        num = jnp.zeros_like(m)
        den = jnp.zeros_like(m)
        for og, lg in ((o0, l0), (o1, l1), (o2, l2)):
            sc = jnp.exp(lg[...] - m)
            num = num + og[...] * sc
            den = den + sc
        o = num / den
        o_ref[...] = o
        lse_ref[...] = m + jnp.log(den)
        ob = o * _silu(z_ref[...])
        ob_ref[...] = ob.astype(BF16)
        obt_ref[...] = ob.T.astype(BF16)

    row = pl.BlockSpec((ROW_TILE, DIL_W), lambda i: (i, 0))
    flat = [a for p in parts for a in p]
    return pl.pallas_call(
        body, name="att_merge", grid=(s // ROW_TILE,),
        in_specs=[row] * 6 + [pl.BlockSpec((ROW_TILE, DIL_W), lambda i: (i, OFF_Z_B // DIL_W))],
        out_specs=[row, row, row, pl.BlockSpec((DIL_W, ROW_TILE), lambda i: (0, i))],
        out_shape=[jax.ShapeDtypeStruct((s, DIL_W), BF16), jax.ShapeDtypeStruct((s, DIL_W), F32),
                   jax.ShapeDtypeStruct((s, DIL_W), F32), jax.ShapeDtypeStruct((DIL_W, s), BF16)],
        compiler_params=_cparams("parallel"))(*flat, proj)


def _att_merge_bwd(o, proj, dob, dproj):
    s = o.shape[0]

    def body(o_ref, z_ref, d_ref, dproj_in, do_ref, dl_ref, dz_ref):
        ov, zv, dv = o_ref[...], z_ref[...], d_ref[...]
        do = dv * _silu(zv)
        do_ref[...] = do
        dz_ref[...] = (dv * ov * _silu_grad(zv)).astype(BF16)
        for h in range(DIL_HEADS):
            sl = slice(h * DIL_DH, (h + 1) * DIL_DH)
            dl_ref[:, sl] = jnp.broadcast_to(jnp.sum(do[:, sl] * ov[:, sl], axis=-1, keepdims=True), (ROW_TILE, DIL_DH))

    row = pl.BlockSpec((ROW_TILE, DIL_W), lambda i: (i, 0))
    return pl.pallas_call(
        body, name="att_merge_bwd", grid=(s // ROW_TILE,),
        in_specs=[row, pl.BlockSpec((ROW_TILE, DIL_W), lambda i: (i, OFF_Z_B // DIL_W)), row, DPROJ_IN],
        out_specs=[row, row, pl.BlockSpec((ROW_TILE, DIL_W), lambda i: (i, OFF_Z_B // DIL_W))],
        out_shape=[jax.ShapeDtypeStruct((s, DIL_W), F32), jax.ShapeDtypeStruct((s, DIL_W), F32),
                   jax.ShapeDtypeStruct((s, PW), BF16)],
        input_output_aliases={3: 2},
        compiler_params=_cparams("parallel"))(o, proj, dob, dproj)


def _merge(proj, ya, yb):
    s = proj.shape[0]

    def body(ga_ref, gb_ref, ya_ref, yb_ref, o_ref, ot_ref):
        m = _sigmoid(ga_ref[...]) * ya_ref[...] + _sigmoid(gb_ref[...]) * yb_ref[...]
        o_ref[...] = m.astype(BF16)
        ot_ref[...] = m.T.astype(BF16)

    row = pl.BlockSpec((ROW_TILE, D_MODEL), lambda i: (i, 0))
    return pl.pallas_call(
        body, name="merge", grid=(s // ROW_TILE,),
        in_specs=[pl.BlockSpec((ROW_TILE, D_MODEL), lambda i: (i, OFF_G_A // D_MODEL)),
                  pl.BlockSpec((ROW_TILE, D_MODEL), lambda i: (i, OFF_G_B // D_MODEL)), row, row],
        out_specs=[row, pl.BlockSpec((D_MODEL, ROW_TILE), lambda i: (0, i))],
        out_shape=[jax.ShapeDtypeStruct((s, D_MODEL), BF16), jax.ShapeDtypeStruct((D_MODEL, s), BF16)],
        compiler_params=_cparams("parallel"))(proj, proj, ya, yb)


def _merge_bwd(proj, ya, yb, dm):
    s = proj.shape[0]

    def body(ga_ref, gb_ref, ya_ref, yb_ref, dm_ref, dya_ref, dyb_ref, dga_ref, dgb_ref):
        dmv = dm_ref[...]
        sa, sb = _sigmoid(ga_ref[...]), _sigmoid(gb_ref[...])
        dya_ref[...] = (dmv * sa).astype(BF16)
        dyb_ref[...] = (dmv * sb).astype(BF16)
        dga_ref[...] = (dmv * ya_ref[...] * sa * (1.0 - sa)).astype(BF16)
        dgb_ref[...] = (dmv * yb_ref[...] * sb * (1.0 - sb)).astype(BF16)

    row = pl.BlockSpec((ROW_TILE, D_MODEL), lambda i: (i, 0))
    return pl.pallas_call(
        body, name="merge_bwd", grid=(s // ROW_TILE,),
        in_specs=[pl.BlockSpec((ROW_TILE, D_MODEL), lambda i: (i, OFF_G_A // D_MODEL)),
                  pl.BlockSpec((ROW_TILE, D_MODEL), lambda i: (i, OFF_G_B // D_MODEL)), row, row, row],
        out_specs=[row] * 4, out_shape=[jax.ShapeDtypeStruct((s, D_MODEL), BF16)] * 4,
        compiler_params=_cparams("parallel"))(proj, proj, ya, yb, dm)


def _final(x, t, fw, tgt):
    s, d = x.shape

    def body(x_ref, t_ref, w_ref, y_ref, dx_ref, dw_ref, l_ref):
        i = pl.program_id(0)
        x2 = x_ref[...] + t_ref[...]
        wv = w_ref[...]
        r = lax.rsqrt(jnp.mean(x2 * x2, axis=-1, keepdims=True) + NORM_EPS)
        e = x2 * r * wv - y_ref[...]
        lrow = jnp.mean(e * e, axis=-1, keepdims=True)
        lpart = jnp.broadcast_to(0.5 * jnp.sum(lrow, axis=0, keepdims=True), (1, 128))
        dy = e * (1.0 / d)
        dwp = jnp.sum(dy * x2 * r, axis=0, keepdims=True)
        dyw = dy * wv
        dx_ref[...] = r * dyw - x2 * (r * r * r) * jnp.mean(dyw * x2, axis=-1, keepdims=True)

        @pl.when(i == 0)
        def _():
            dw_ref[...] = dwp
            l_ref[...] = lpart

        @pl.when(i > 0)
        def _():
            dw_ref[...] += dwp
            l_ref[...] += lpart

    row = pl.BlockSpec((BIG_TILE, d), lambda i: (i, 0))
    vec = pl.BlockSpec((1, d), lambda i: (0, 0))
    return pl.pallas_call(
        body, name="final", grid=(s // BIG_TILE,), in_specs=[row, row, vec, row],
        out_specs=[row, vec, pl.BlockSpec((1, 128), lambda i: (0, 0))],
        out_shape=[jax.ShapeDtypeStruct((s, d), F32), jax.ShapeDtypeStruct((1, d), F32), jax.ShapeDtypeStruct((1, 128), F32)],
        compiler_params=_cparams("arbitrary"))(x, t, fw, tgt)


def _adamw(w, g, m, v, name):
    r, c = w.shape
    cap = max(8, (1 << 18) // c)
    divisors = [t for t in range(8, min(r, cap) + 1, 8) if r % t == 0]
    tr = r if r <= 8 else (max(divisors) if divisors else cap)

    def body(w_ref, g_ref, m_ref, v_ref, d_ref, nm_ref, nv_ref):
        gv = g_ref[...]
        mn = ADAM_B1 * m_ref[...] + (1.0 - ADAM_B1) * gv
        vn = ADAM_B2 * v_ref[...] + (1.0 - ADAM_B2) * (gv * gv)
        m_hat = mn / (1.0 - ADAM_B1 ** ADAM_STEP)
        v_hat = vn / (1.0 - ADAM_B2 ** ADAM_STEP)
        d_ref[...] = -ADAM_LR * (m_hat / (jnp.sqrt(v_hat) + ADAM_EPS) + ADAM_WD * w_ref[...])
        nm_ref[...] = mn
        nv_ref[...] = vn

    blk = pl.BlockSpec((tr, c), lambda i: (i, 0))
    return pl.pallas_call(
        body, name=name, grid=(pl.cdiv(r, tr),), in_specs=[blk] * 4, out_specs=[blk] * 3,
        out_shape=[jax.ShapeDtypeStruct((r, c), F32)] * 3, compiler_params=_cparams("parallel"))(w, g, m, v)


HBM_SPEC = pl.BlockSpec(memory_space=pl.ANY)


def _place():
    x, y, c = lax.axis_index("x"), lax.axis_index("y"), lax.axis_index("c")
    chips = [(1 - x, y), (x, 1 - y), (1 - x, 1 - y)]
    return x, y, c, chips


def _ag_weights(packs):
    na = len(packs)
    nsem = 8

    def body(*refs):
        p_refs, out_refs = refs[:na], refs[na:2 * na]
        send_sems, recv_sems = refs[2 * na:]
        x, y, c, _ = _place()
        me, sib, j = (x, y, c), (x, y, 1 - c), 2 * x + y
        xn, yn = (1 - x, y, c), (x, 1 - y, c)
        jx, jy, jd = 2 * (1 - x) + y, 2 * x + (1 - y), 2 * (1 - x) + (1 - y)

        def rc(a, k, src, dst, to):
            return pltpu.make_async_remote_copy(src_ref=src, dst_ref=dst, send_sem=send_sems.at[nsem * a + k],
                                                recv_sem=recv_sems.at[nsem * a + k], device_id=to, device_id_type=MESH)

        sent = []
        for a in range(na):
            mine, land = p_refs[a].at[c], out_refs[a].at[j, c]
            sent += [rc(a, 0, mine, land, xn), rc(a, 1, mine, land, yn), rc(a, 7, p_refs[a], out_refs[a].at[j], sib)]
        for cp in sent:
            cp.start()
        for a in range(na):
            half = p_refs[a].shape[1] // 2
            top, bottom = pl.ds(0, half), pl.ds(half, half)
            from_x, from_y, from_d = out_refs[a].at[jx, c], out_refs[a].at[jy, c], out_refs[a].at[jd, c]
            rc(a, 0, p_refs[a].at[c], from_x, me).wait_recv()
            later = [rc(a, 2, from_x.at[top], from_x.at[top], yn), rc(a, 4, from_x, from_x, sib)]
            for cp in later:
                cp.start()
            sent += later
            rc(a, 1, p_refs[a].at[c], from_y, me).wait_recv()
            later = [rc(a, 3, from_y.at[bottom], from_y.at[bottom], xn), rc(a, 5, from_y, from_y, sib)]
            for cp in later:
                cp.start()
            sent += later
            rc(a, 2, from_d.at[top], from_d.at[top], me).wait_recv()
            rc(a, 3, from_d.at[bottom], from_d.at[bottom], me).wait_recv()
            cp = rc(a, 6, from_d, from_d, sib)
            cp.start()
            sent.append(cp)
        for a in range(na):
            for k, jj in ((4, jx), (5, jy), (6, jd)):
                rc(a, k, p_refs[a].at[c], out_refs[a].at[jj, 1 - c], me).wait_recv()
            rc(a, 7, p_refs[a], out_refs[a].at[j], me).wait_recv()
        for cp in sent:
            cp.wait_send()

    return pl.pallas_call(
        body, name="ag_weights",
        out_shape=[jax.ShapeDtypeStruct((N_CHIPS,) + p.shape, p.dtype) for p in packs],
        in_specs=[HBM_SPEC] * na, out_specs=[HBM_SPEC] * na,
        scratch_shapes=[pltpu.SemaphoreType.DMA((nsem * na,)), pltpu.SemaphoreType.DMA((nsem * na,))])(*packs)


def _rs_pair(dwpt, gpack):
    n = N_CHIPS
    hw = SHARD_PAD // 2

    def body(d_ref, g_ref, out_d, out_g, send_sems, recv_sems):
        x, y, c, _ = _place()
        sib = (x, y, 1 - c)
        cps = []
        for p in range(n):
            start = pl.multiple_of(WIN_BASE[p] + (1 - c) * hw, TILE_ROWS)
            cps.append(pltpu.make_async_remote_copy(
                src_ref=d_ref.at[pl.ds(start, hw)], dst_ref=out_d.at[p], send_sem=send_sems.at[p],
                recv_sem=recv_sems.at[p], device_id=sib, device_id_type=MESH))
            cps.append(pltpu.make_async_remote_copy(
                src_ref=g_ref.at[p, 1 - c], dst_ref=out_g.at[p], send_sem=send_sems.at[n + p],
                recv_sem=recv_sems.at[n + p], device_id=sib, device_id_type=MESH))
        for cp in cps:
            cp.start()
        for cp in cps:
            cp.wait_recv()
        for cp in cps:
            cp.wait_send()

    return pl.pallas_call(
        body, name="rs_pair",
        out_shape=[jax.ShapeDtypeStruct((n, hw, dwpt.shape[1]), dwpt.dtype),
                   jax.ShapeDtypeStruct((n,) + gpack.shape[2:], gpack.dtype)],
        in_specs=[HBM_SPEC] * 2, out_specs=[HBM_SPEC] * 2,
        scratch_shapes=[pltpu.SemaphoreType.DMA((2 * n,)), pltpu.SemaphoreType.DMA((2 * n,))])(dwpt, gpack)


def _add_halves_win(dwpt, other, c):
    n, rh, wd = other.shape
    tr = _row_tile(rh)

    def body(s_ref, d_ref, o_ref, out_ref):
        out_ref[0] = (d_ref[...] + o_ref[0]).astype(BF16)

    scal = jnp.concatenate([jnp.reshape(c, (1,)).astype(jnp.int32), jnp.asarray(WIN_BASE, jnp.int32)])
    grid_spec = pltpu.PrefetchScalarGridSpec(
        num_scalar_prefetch=1, grid=(n, rh // tr),
        in_specs=[pl.BlockSpec((pl.Element(tr), pl.Element(wd)),
                               lambda p, i, sr: (pl.multiple_of(sr[1 + p] + sr[0] * rh + i * tr, TILE_ROWS), 0)),
                  pl.BlockSpec((1, tr, wd), lambda p, i, sr: (p, i, 0))],
        out_specs=pl.BlockSpec((1, tr, wd), lambda p, i, sr: (p, i, 0)))
    return pl.pallas_call(
        body, name="add_halves_in", grid_spec=grid_spec, out_shape=jax.ShapeDtypeStruct((n, rh, wd), BF16),
        compiler_params=_cparams("parallel", "parallel"))(scal, dwpt, other)


SEM_SPEC = pl.BlockSpec(memory_space=pltpu.SEMAPHORE)
DATAFLOW_EFFECT = pltpu.SideEffectType.DATAFLOW_SIDE_EFFECTING


def _rs_chips_start(csums):
    na = len(csums)

    def body(*refs):
        s_refs, land_refs = refs[:na], refs[na:2 * na]
        send_sems, recv_sems = refs[2 * na], refs[2 * na + 1]
        token = refs[-1]
        x, y, c, chips = _place()
        j = 2 * x + y
        for a in range(na):
            for k, (cx, cy) in enumerate(chips):
                pltpu.make_async_remote_copy(src_ref=s_refs[a].at[2 * cx + cy], dst_ref=land_refs[a].at[j],
                                             send_sem=send_sems.at[3 * a + k], recv_sem=recv_sems.at[3 * a + k],
                                             device_id=(cx, cy, c), device_id_type=MESH).start()
        token[...] = jnp.zeros_like(token)

    hbm = [pltpu.HBM(s.shape, s.dtype) for s in csums]
    args = [pltpu.with_memory_space_constraint(s, pltpu.HBM) for s in csums]
    args += [pltpu.with_memory_space_constraint(lax.empty(s.shape, s.dtype), pltpu.HBM) for s in csums]
    res = pl.pallas_call(
        body, name="rs_chips_start",
        out_shape=(pltpu.SemaphoreType.DMA((3 * na,)), pltpu.SemaphoreType.DMA((3 * na,)), *hbm, *hbm,
                   jax.ShapeDtypeStruct((8, 128), F32)),
        in_specs=[pl.BlockSpec(memory_space=pltpu.HBM)] * (2 * na),
        out_specs=(SEM_SPEC, SEM_SPEC, *[pl.BlockSpec(memory_space=pltpu.HBM)] * (2 * na),
                   pl.BlockSpec(memory_space=pltpu.VMEM)),
        input_output_aliases={i: 2 + i for i in range(2 * na)},
        compiler_params=pltpu.CompilerParams(has_side_effects=DATAFLOW_EFFECT))(*args)
    return res[0], res[1], list(res[2:2 + na]), list(res[2 + na:2 + 2 * na]), res[-1]


def _rs_chips_wait(send_sems, recv_sems, csums, lands, after):
    na = len(csums)

    def body(*refs):
        s_refs, land_refs = refs[:na], refs[na:2 * na]
        send_sems, recv_sems = refs[2 * na], refs[2 * na + 1]
        x, y, c, chips = _place()
        j = 2 * x + y
        for a in range(na):
            for k, (cx, cy) in enumerate(chips):
                cp = pltpu.make_async_remote_copy(src_ref=s_refs[a].at[2 * cx + cy], dst_ref=land_refs[a].at[2 * cx + cy],
                                                  send_sem=send_sems.at[3 * a + k], recv_sem=recv_sems.at[3 * a + k],
                                                  device_id=(cx, cy, c), device_id_type=MESH)
                cp.wait_send()
                cp.wait_recv()

    hbm = [pltpu.HBM(s.shape, s.dtype) for s in csums]
    res = pl.pallas_call(
        body, name="rs_chips_wait", out_shape=(*hbm, *hbm),
        in_specs=[pl.BlockSpec(memory_space=pltpu.HBM)] * (2 * na) + [SEM_SPEC, SEM_SPEC, pl.BlockSpec(memory_space=pl.ANY)],
        out_specs=tuple([pl.BlockSpec(memory_space=pltpu.HBM)] * (2 * na)),
        input_output_aliases={i: i for i in range(2 * na)},
        compiler_params=pltpu.CompilerParams(has_side_effects=DATAFLOW_EFFECT))(*csums, *lands, send_sems, recv_sems, after)
    return list(res[:na]), list(res[na:])


SWAP_CHUNKS = 4


def _pair_swap(halves):
    na = len(halves)

    def body(*refs):
        h_refs, out_refs = refs[:na], refs[na:2 * na]
        send_sems, recv_sems = refs[2 * na:]
        x, y, c, _ = _place()
        cps = []
        for a in range(na):
            rows = h_refs[a].shape[0] // SWAP_CHUNKS
            assert rows * SWAP_CHUNKS == h_refs[a].shape[0]
            for q in range(SWAP_CHUNKS):
                k = SWAP_CHUNKS * a + q
                cps.append(pltpu.make_async_remote_copy(
                    src_ref=h_refs[a].at[pl.ds(q * rows, rows)], dst_ref=out_refs[a].at[pl.ds(q * rows, rows)],
                    send_sem=send_sems.at[k], recv_sem=recv_sems.at[k], device_id=(x, y, 1 - c), device_id_type=MESH))
        for cp in cps:
            cp.start()
        for cp in cps:
            cp.wait_recv()
        for cp in cps:
            cp.wait_send()

    return pl.pallas_call(
        body, name="pair_swap", out_shape=[jax.ShapeDtypeStruct(h.shape, h.dtype) for h in halves],
        in_specs=[HBM_SPEC] * na, out_specs=[HBM_SPEC] * na,
        scratch_shapes=[pltpu.SemaphoreType.DMA((SWAP_CHUNKS * na,)), pltpu.SemaphoreType.DMA((SWAP_CHUNKS * na,))])(*halves)


def _ag_small(v):
    m_per, n = v.shape

    def body(x_ref, out_ref, send_sems, recv_sems, local_sem):
        x, y, c, chips = _place()
        me, sibling = (x, y, c), (x, y, 1 - c)

        def rows(px, py, pc):
            return out_ref.at[pl.ds((4 * px + 2 * py + pc) * m_per, m_per), :]

        def copy(k, block, to, src=None):
            return pltpu.make_async_remote_copy(
                src_ref=rows(*block) if src is None else src, dst_ref=rows(*block), send_sem=send_sems.at[k],
                recv_sem=recv_sems.at[k], device_id=to, device_id_type=MESH)

        mine = pltpu.make_async_copy(x_ref, rows(*me), local_sem)
        mine.start()
        first = [copy(0, me, sibling, src=x_ref)]
        first += [copy(1 + k, me, (*chip, c), src=x_ref) for k, chip in enumerate(chips)]
        for cp in first:
            cp.start()
        passed = [copy(4 + k, (*chip, c), sibling) for k, chip in enumerate(chips)]
        for k, chip in enumerate(chips):
            copy(1 + k, (*chip, c), me).wait_recv()
            passed[k].start()
        copy(0, sibling, me).wait_recv()
        for k, chip in enumerate(chips):
            copy(4 + k, (*chip, 1 - c), me).wait_recv()
        for cp in first + passed:
            cp.wait_send()
        mine.wait()

    return pl.pallas_call(
        body, name="ag_small", out_shape=jax.ShapeDtypeStruct((8 * m_per, n), v.dtype),
        in_specs=[pl.BlockSpec(memory_space=pltpu.VMEM)], out_specs=pl.BlockSpec(memory_space=pltpu.VMEM),
        scratch_shapes=[pltpu.SemaphoreType.DMA((7,)), pltpu.SemaphoreType.DMA((7,)), pltpu.SemaphoreType.DMA])(v)


def _sum_blocks(a, nblk, name):
    rows, wd = a.shape
    r = rows // nblk
    tr = min(r, ROW_TILE)
    assert r % tr == 0

    def body(*refs):
        acc = refs[0][...].astype(F32)
        for ref in refs[1:nblk]:
            acc = acc + ref[...].astype(F32)
        refs[nblk][...] = acc

    nt = r // tr
    return pl.pallas_call(
        body, name=name, grid=(nt,),
        in_specs=[pl.BlockSpec((tr, wd), functools.partial(lambda i, b: (b * nt + i, 0), b=b)) for b in range(nblk)],
        out_specs=pl.BlockSpec((tr, wd), lambda i: (i, 0)),
        out_shape=jax.ShapeDtypeStruct((r, wd), F32), compiler_params=_cparams("parallel"))(*([a] * nblk))


def _row_tile(rows):
    best = max(t for t in range(16, 513, 16) if rows % t == 0)
    return best


def _sum_chips(by_src, csum, j, name):
    n, rh, wd = by_src.shape
    tr = _row_tile(rh)

    def body(j_ref, *refs):
        own = refs[n][0].astype(F32)
        acc = None
        for k in range(n):
            term = jnp.where(j_ref[0] == k, own, refs[k][0].astype(F32))
            acc = term if acc is None else acc + term
        refs[n + 1][...] = acc

    def other(k):
        return pl.BlockSpec((1, tr, wd), lambda i, jr: (jnp.where(jr[0] == k, (k + 1) % n, k), i, 0))

    grid_spec = pltpu.PrefetchScalarGridSpec(
        num_scalar_prefetch=1, grid=(rh // tr,),
        in_specs=[other(k) for k in range(n)] + [pl.BlockSpec((1, tr, wd), lambda i, jr: (jr[0], i, 0))],
        out_specs=pl.BlockSpec((tr, wd), lambda i, jr: (i, 0)))
    return pl.pallas_call(
        body, name=name, grid_spec=grid_spec, out_shape=jax.ShapeDtypeStruct((rh, wd), F32),
        compiler_params=_cparams("parallel"))(jnp.reshape(j, (1,)).astype(jnp.int32), *([by_src] * n), csum)


def _add_halves(gpack, other, c, name):
    n, _, rh, wd = gpack.shape
    tr = _row_tile(rh)

    def body(c_ref, g_ref, o_ref, out_ref):
        out_ref[0] = (g_ref[0, 0] + o_ref[0]).astype(BF16)

    grid_spec = pltpu.PrefetchScalarGridSpec(
        num_scalar_prefetch=1, grid=(n, rh // tr),
        in_specs=[pl.BlockSpec((1, 1, tr, wd), lambda p, i, cr: (p, cr[0], i, 0)),
                  pl.BlockSpec((1, tr, wd), lambda p, i, cr: (p, i, 0))],
        out_specs=pl.BlockSpec((1, tr, wd), lambda p, i, cr: (p, i, 0)))
    return pl.pallas_call(
        body, name=name, grid_spec=grid_spec, out_shape=jax.ShapeDtypeStruct((n, rh, wd), BF16),
        compiler_params=_cparams("parallel", "parallel"))(jnp.reshape(c, (1,)).astype(jnp.int32), gpack, other)


PACK_W = 1024
ROWS_O_DN = DN_W // N_CHIPS
ROWS_O_DIL = DIL_W * (D_MODEL // N_CHIPS) // PACK_W
ROWS_OUT = D_MODEL // N_CHIPS
ROWS_CONV = 4 * (3 * DN_W // N_CHIPS) // PACK_W
R1 = ROWS_O_DN
R2 = R1 + ROWS_O_DIL
R3 = R2 + ROWS_OUT
R4 = R3 + 16
R5 = R4 + 16
PACK_ROWS = 704
HALF_ROWS = PACK_ROWS // 2
SHARD_PAD = 2880


R6 = R5 + 2 * DN_HEADS

TILE_ROWS = 16
BA_IN_SHARD1 = REF_OFF_BA - SHARD_W
LOCAL_START = (0, SHARD_W, 2 * SHARD_W - 2 * DN_HEADS, 3 * SHARD_W - 2 * DN_HEADS)
LOCAL_END = LOCAL_START[1:] + (OFF_BA,)
WIN_BASE = tuple(s // TILE_ROWS * TILE_ROWS for s in LOCAL_START)


def _to_window(k, shard):
    nba = 2 * DN_HEADS
    body = shard
    if k == 1:
        row = lax.broadcasted_iota(jnp.int32, (SHARD_W - nba, 1), 0)
        body = jnp.where(row < BA_IN_SHARD1, shard[:SHARD_W - nba], shard[nba:])
    lead = LOCAL_START[k] - WIN_BASE[k]
    return jnp.pad(body, ((lead, SHARD_PAD - lead - body.shape[0]), (0, 0)))


def _from_window(k, win, ba):
    nba = 2 * DN_HEADS
    lead = LOCAL_START[k] - WIN_BASE[k]
    if k != 1:
        return win[lead:lead + SHARD_W]
    row = lax.broadcasted_iota(jnp.int32, (SHARD_W, 1), 0)
    before = win[lead:lead + SHARD_W]
    after = jnp.pad(win, ((nba, 0), (0, 0)))[lead:lead + SHARD_W]
    mid = jnp.pad(ba, ((BA_IN_SHARD1, SHARD_W - BA_IN_SHARD1 - nba), (0, 0)))
    return jnp.where(row < BA_IN_SHARD1, before, jnp.where(row < BA_IN_SHARD1 + nba, mid, after))


def _stack_windows(wins, ba):
    pieces = []
    for k in range(N_CHIPS):
        lo = WIN_BASE[k] + (TILE_ROWS if k else 0)
        hi = LOCAL_END[k] // TILE_ROWS * TILE_ROWS
        pieces.append(wins[k][lo - WIN_BASE[k]:hi - WIN_BASE[k]])
        if k + 1 < N_CHIPS:
            assert hi == WIN_BASE[k + 1]
            pieces.append(wins[k][hi - WIN_BASE[k]:hi - WIN_BASE[k] + TILE_ROWS] + wins[k + 1][:TILE_ROWS])
    pieces += [ba, jnp.zeros((PW - OFF_BA - ba.shape[0], ba.shape[1]), ba.dtype)]
    out = jnp.concatenate(pieces, axis=0)
    assert out.shape[0] == PW
    return out


def _local_step(x, tgt, norm_w, wpt, conv_full, a_log, dt_bias, dn_norm_w, w_o_dn, w_o_dil, w_out, final_norm_w):
    s = x.shape[0]
    h, h_t = _rms_in(x, norm_w)
    proj = _matmul(h, wpt, F32, 2048, 1280, 1024, "proj", nt=True)
    c_pre, qkv = _conv_fwd(proj, conv_full)
    gate_par = jnp.zeros((8, 128), F32).at[0, 8:16].set(a_log[0]).at[1, 8:16].set(dt_bias[0])
    bg = _gates_fwd(proj, gate_par)
    o_a, u, w, vn, tmat, states = _gdr_fwd(qkv, bg)
    oa2, oa2_t = _gdr_out(o_a, proj, dn_norm_w)
    ya = _matmul(oa2, w_o_dn, F32, 1024, 1024, 1024, "ya")
    parts = [_att_fwd(proj, g) for g in range(N_DIL)]
    ob, o_att, lse, ob_t = _att_merge(parts, proj)
    yb = _matmul(ob, w_o_dil, F32, 1024, 1024, 512, "yb")
    mg, mg_t = _merge(proj, ya, yb)
    t = _matmul(mg, w_out, F32, 1024, 1024, 1024, "t_out")
    dx2, dfw, lpart = _final(x, t, final_norm_w, tgt)

    dmg = _matmul(dx2, w_out, F32, 1024, 1024, 1024, "d_merged", nt=True)
    dw_out = _matmul(mg_t, dx2, F32, 1024, 1024, 1024, "dw_out")
    dya, dyb, dga, dgb = _merge_bwd(proj, ya, yb, dmg)
    doa2 = _matmul(dya, w_o_dn, F32, 1024, 1024, 1024, "d_oa2", nt=True)
    dw_o_dn = _matmul(oa2_t, dya, F32, 1024, 1024, 1024, "dw_o_dn")
    dob = _matmul(dyb, w_o_dil, F32, 1024, 512, 1024, "d_ob", nt=True)
    dw_o_dil = _matmul(ob_t, dyb, F32, 512, 1024, 1024, "dw_o_dil")
    do_a, dproj, ddnw = _gdr_out_bwd(o_a, proj, dn_norm_w, doa2)
    dqkv_a, dbg = _gdr_bwd(qkv, bg, u, w, vn, tmat, states, do_a)
    dproj, dpar = _gates_bwd(proj, gate_par, dbg, dproj)
    dproj, dconv = _conv_bwd(proj, c_pre, dqkv_a, conv_full, dproj)
    do_att, delta, dproj = _att_merge_bwd(o_att, proj, dob, dproj)
    dqkv_b = [_att_bwd(proj, g, do_att, lse, delta) for g in range(N_DIL)]
    pieces = [(OFF_Q_B + (N_DIL * i + g) * DIL_W, dqkv_b[g][i]) for i in range(3) for g in range(N_DIL)]
    for off, piece in pieces + [(OFF_G_A, dga), (OFF_G_B, dgb)]:
        dproj = lax.dynamic_update_slice(dproj, piece, (0, off))
    dwpt, dwpt_b = _matmul(h_t, dproj, F32, 1024, 1280, 2048, "dw_in", transpose_out=True, also_bf16=True)

    def finish(after=None):
        dh = _matmul(dproj, wpt, F32, 1024, 1024, 3840, "d_h", after=after)
        grad_x, dnw = _rms_in_bwd(x, norm_w, dh, dx2)
        small = jnp.zeros((8, PACK_W), F32)
        small = small.at[0].set(dnw[0]).at[1].set(dfw[0]).at[2, :DN_D].set(ddnw[0])
        small = small.at[3, :DN_HEADS].set(dpar[0, 8:16]).at[3, DN_HEADS:2 * DN_HEADS].set(dpar[1, 8:16])
        small = small.at[4, 0].set(lpart[0, 0])
        return grad_x, small

    return finish, (dwpt, dwpt_b), dconv, dw_o_dn, dw_o_dil, dw_out


def kernel(x, norm_w, w_in, conv_w, a_log, dt_bias, dn_norm_w, w_o_dn, w_o_dil, w_out, final_norm_w, loss_target, m_norm_w, m_w_in, m_conv_w, m_a_log, m_dt_bias, m_dn_norm_w, m_w_o_dn, m_w_o_dil, m_w_out, m_final_norm_w, v_norm_w, v_w_in, v_conv_w, v_a_log, v_dt_bias, v_dn_norm_w, v_w_o_dn, v_w_o_dil, v_w_out, v_final_norm_w):
    c = lax.axis_index("c")
    j = 2 * lax.axis_index("x") + lax.axis_index("y")
    qw = D_MODEL // N_CHIPS

    cw = conv_w[0].reshape(ROWS_CONV, PACK_W)
    cw = jnp.pad(cw, ((0, 16 - ROWS_CONV), (0, 0)))
    cw_hi = cw.astype(BF16)
    cw_lo = (cw - cw_hi.astype(F32)).astype(BF16)
    shard = w_in[0].T.astype(BF16)
    own_ba = jnp.where(j == 1, shard[BA_IN_SHARD1:BA_IN_SHARD1 + 2 * DN_HEADS], jnp.zeros((2 * DN_HEADS, D_MODEL), BF16))
    pack = jnp.concatenate(
        [w_o_dn[0].astype(BF16), w_o_dil[0].astype(BF16).reshape(ROWS_O_DIL, PACK_W), w_out[0].astype(BF16), cw_hi, cw_lo,
         own_ba, jnp.zeros((PACK_ROWS - R6, PACK_W), BF16)], axis=0).reshape(2, HALF_ROWS, PACK_W)
    chips = range(N_CHIPS)
    own_win = lax.switch(j, [functools.partial(_to_window, k) for k in chips], shard).reshape(2, SHARD_PAD // 2, D_MODEL)
    all_in, allw = _ag_weights([own_win, pack])
    wins = [all_in[k].reshape(SHARD_PAD, D_MODEL) for k in chips]
    allw = [allw[k].reshape(PACK_ROWS, PACK_W) for k in chips]
    wpt = _stack_windows(wins, allw[1][R5:R6])
    w_o_dn_full = jnp.concatenate([allw[k][:R1] for k in chips], axis=0)
    w_o_dil_full = jnp.concatenate([allw[k][R1:R2].reshape(DIL_W, qw) for k in chips], axis=1)
    w_out_full = jnp.concatenate([allw[k][R2:R3] for k in chips], axis=0)
    conv_full = jnp.concatenate(
        [(allw[k][R3:R3 + ROWS_CONV].astype(F32) + allw[k][R4:R4 + ROWS_CONV].astype(F32)).reshape(4, 3 * DN_W // N_CHIPS)
         for k in chips], axis=1)

    finish, (dwpt, dwpt_b), dconv, dw_o_dn, dw_o_dil, dw_out = _local_step(
        x[0], loss_target[0], norm_w, wpt, conv_full, a_log, dt_bias, dn_norm_w, w_o_dn_full, w_o_dil_full, w_out_full,
        final_norm_w.reshape(1, D_MODEL))

    cq = 3 * DN_W // N_CHIPS
    gpack = jnp.stack([
        jnp.concatenate(
            [dw_o_dn[k * qw:(k + 1) * qw], dw_o_dil[:, k * qw:(k + 1) * qw].reshape(ROWS_O_DIL, PACK_W),
             dw_out[k * qw:(k + 1) * qw],
             jnp.pad(dconv[:, k * cq:(k + 1) * cq].reshape(ROWS_CONV, PACK_W), ((0, 16 - ROWS_CONV), (0, 0))),
             dwpt[OFF_BA:OFF_BA + 2 * DN_HEADS] if k == 1 else jnp.zeros((2 * DN_HEADS, PACK_W), F32),
             jnp.zeros((PACK_ROWS - R4 - 2 * DN_HEADS, PACK_W), F32)], axis=0)
        for k in chips]).reshape(N_CHIPS, 2, HALF_ROWS, PACK_W)
    sib_in, sib_pack = _rs_pair(dwpt_b, gpack)
    csum_in = _add_halves_win(dwpt, sib_in, c)
    csum_pack = _add_halves(gpack, sib_pack, c, "add_halves_pack")
    send_sems, recv_sems, csums, lands, token = _rs_chips_start([csum_in, csum_pack])
    grad_x, small = finish(after=token)

    gs = _sum_blocks(_ag_small(small), 8, "sum_small")
    loss = gs[4, 0]
    w_small = jnp.zeros((8, PACK_W), F32)

    def pack_small(nw, fw, dnw_, al, db):
        t = w_small.at[0].set(nw[0]).at[1].set(fw).at[2, :DN_D].set(dnw_[0])
        return t.at[3, :DN_HEADS].set(al[0]).at[3, DN_HEADS:2 * DN_HEADS].set(db[0])

    sm = _adamw(pack_small(norm_w, final_norm_w, dn_norm_w, a_log, dt_bias), gs,
                pack_small(m_norm_w, m_final_norm_w, m_dn_norm_w, m_a_log, m_dt_bias),
                pack_small(v_norm_w, v_final_norm_w, v_dn_norm_w, v_a_log, v_dt_bias), "adamw_small")

    (csum_in, csum_pack), (src_in, src_pack) = _rs_chips_wait(send_sems, recv_sems, csums, lands, sm[0])
    half_in = _sum_chips(src_in, csum_in, j, "sum_chips_in")
    half_pack = _sum_chips(src_pack, csum_pack, j, "sum_chips_pack")
    sib_half_in, sib_half_pack = _pair_swap([half_in, half_pack])

    def both_halves(mine, theirs):
        return jnp.where(c == 0, jnp.concatenate([mine, theirs], axis=0), jnp.concatenate([theirs, mine], axis=0))

    g = both_halves(half_pack, sib_half_pack)
    g_w_in = lax.switch(j, [functools.partial(_from_window, k) for k in chips], both_halves(half_in, sib_half_in),
                        g[R4:R4 + 2 * DN_HEADS])
    g_w_o_dn = g[:R1]
    g_w_o_dil = g[R1:R2].reshape(DIL_W, qw)
    g_w_out = g[R2:R3]
    g_conv = g[R3:R3 + ROWS_CONV].reshape(4, cq)

    def unpack_small(t):
        return dict(norm_w=t[0:1], final_norm_w=t[1], dn_norm_w=t[2:3, :DN_D], a_log=t[3:4, :DN_HEADS],
                    dt_bias=t[3:4, DN_HEADS:2 * DN_HEADS])

    res = {"grad": unpack_small(gs)}
    for kind, arr in zip(("delta", "new_m", "new_v"), sm):
        res[kind] = unpack_small(arr)
    big = dict(conv_w=(conv_w, g_conv, m_conv_w, v_conv_w), w_o_dn=(w_o_dn, g_w_o_dn, m_w_o_dn, v_w_o_dn),
               w_o_dil=(w_o_dil, g_w_o_dil, m_w_o_dil, v_w_o_dil), w_out=(w_out, g_w_out, m_w_out, v_w_out))
    for name, (wt, gt, mt, vt) in big.items():
        d, nm, nv = _adamw(wt[0], gt, mt[0], vt[0], "adamw_" + name)
        res["grad"][name] = gt[None]
        res["delta"][name], res["new_m"][name], res["new_v"][name] = d[None], nm[None], nv[None]

    d, nm, nv = _adamw(w_in[0].T, g_w_in, m_w_in[0].T, v_w_in[0].T, "adamw_w_in")
    res["grad"]["w_in"] = g_w_in.T[None]
    res["delta"]["w_in"], res["new_m"]["w_in"], res["new_v"]["w_in"] = d.T[None], nm.T[None], nv.T[None]
    order = ["norm_w", "w_in", "conv_w", "a_log", "dt_bias", "dn_norm_w", "w_o_dn", "w_o_dil", "w_out", "final_norm_w"]
    outs = [loss, grad_x[None]]
    for kind in ("grad", "delta", "new_m", "new_v"):
        outs += [res[kind][nm] for nm in order]
    return tuple(outs)
```

```python
import functools
import math

import jax
import jax.numpy as jnp
from jax import lax
from jax.experimental import pallas as pl
from jax.experimental.pallas import tpu as pltpu

F32 = jnp.float32
BF16 = jnp.bfloat16
MESH = pl.DeviceIdType.MESH

D_MODEL = 1024
DN_HEADS = 8
DN_D = 128
DN_CHUNK = 64
DN_W = DN_HEADS * DN_D
DIL_GROUPS = ((128, 1), (512, 4), (2048, 16))
N_DIL = len(DIL_GROUPS)
DIL_HEADS = 4
DIL_DH = 128
DIL_W = DIL_HEADS * DIL_DH
ATT_BLOCK = 128
NORM_EPS = 1e-6
PROJ_W = 11280
N_CHIPS = 4
SHARD_W = PROJ_W // N_CHIPS

OFF_QKV_A = 0
OFF_Z_A = 3072
OFF_Q_B = 4096
OFF_K_B = 5632
OFF_V_B = 7168
OFF_Z_B = 8704
OFF_G_A = 9216
OFF_G_B = 10240
OFF_BA = 11264
PW = 11520
REF_OFF_BA = 4096

ADAM_LR = 0.001
ADAM_B1 = 0.9
ADAM_B2 = 0.999
ADAM_EPS = 1e-08
ADAM_WD = 0.01
ADAM_STEP = 10

ROW_TILE = 512
CONV_TILE = 1024
BIG_TILE = 1024
NEG = -1e30


def _dot(a, b):
    return jnp.dot(a.astype(BF16), b.astype(BF16), preferred_element_type=F32)


def _dot_nt(a, b):
    return lax.dot_general(a.astype(BF16), b.astype(BF16), (((1,), (1,)), ((), ())), preferred_element_type=F32)


def _dot_tn(a, b):
    return lax.dot_general(a.astype(BF16), b.astype(BF16), (((0,), (0,)), ((), ())), preferred_element_type=F32)


def _split(a):
    hi = a.astype(BF16)
    lo = (a - hi.astype(F32)).astype(BF16)
    return hi, lo


def _dot_exact_lhs(c, a):
    hi, lo = _split(a)
    cb = c.astype(BF16)
    return jnp.dot(cb, hi, preferred_element_type=F32) + jnp.dot(cb, lo, preferred_element_type=F32)


def _dot_tn_exact_rhs(a, c):
    hi, lo = _split(a)
    cb = c.astype(BF16)
    dn = (((0,), (0,)), ((), ()))
    return (lax.dot_general(hi, cb, dn, preferred_element_type=F32)
            + lax.dot_general(lo, cb, dn, preferred_element_type=F32))


def _sigmoid(x):
    return 1.0 / (1.0 + jnp.exp(-x))


def _silu(x):
    return x * _sigmoid(x)


def _silu_grad(x):
    s = _sigmoid(x)
    return s * (1.0 + x * (1.0 - s))


def _softplus(x):
    return jnp.maximum(x, 0.0) + jnp.log(1.0 + jnp.exp(-jnp.abs(x)))


def _cparams(*sem):
    return pltpu.CompilerParams(dimension_semantics=sem)


def _matmul(a, b, out_dtype, tm, tn, tk, name, nt=False, transpose_out=False, after=None, also_bf16=False):
    m, kdim = a.shape
    n = b.shape[0] if nt else b.shape[1]
    tm, tn, tk = min(tm, m), min(tn, n), min(tk, kdim)
    assert m % tm == 0 and n % tn == 0 and kdim % tk == 0, (name, a.shape, b.shape, tm, tn, tk)
    nk = kdim // tk
    dot = _dot_nt if nt else _dot
    b_spec = (pl.BlockSpec((tn, tk), lambda i, j, k: (j, k)) if nt else pl.BlockSpec((tk, tn), lambda i, j, k: (k, j)))
    extra = [] if after is None else [after]
    out_dtypes = [out_dtype] + ([BF16] if also_bf16 else [])

    def emit(o_refs, acc):
        val = acc.T if transpose_out else acc
        for o_ref in o_refs:
            o_ref[...] = val.astype(o_ref.dtype)

    def outs_of(rest):
        return rest[len(extra):len(extra) + len(out_dtypes)]

    if nk == 1:
        def body(a_ref, b_ref, *rest):
            emit(outs_of(rest), dot(a_ref[...], b_ref[...]))
        scratch = []
    else:
        def body(a_ref, b_ref, *rest):
            o_ref, acc_ref = outs_of(rest), rest[-1]
            k = pl.program_id(2)
            p = dot(a_ref[...], b_ref[...])

            @pl.when(k == 0)
            def _():
                acc_ref[...] = p

            @pl.when(k > 0)
            def _():
                acc_ref[...] += p

            @pl.when(k == nk - 1)
            def _():
                emit(o_ref, acc_ref[...])
        scratch = [pltpu.VMEM((tm, tn), F32)]

    if transpose_out:
        out_spec, out_shape = pl.BlockSpec((tn, tm), lambda i, j, k: (j, i)), (n, m)
    else:
        out_spec, out_shape = pl.BlockSpec((tm, tn), lambda i, j, k: (i, j)), (m, n)
    res = pl.pallas_call(
        body, name=name, grid=(m // tm, n // tn, nk),
        in_specs=[pl.BlockSpec((tm, tk), lambda i, j, k: (i, k)), b_spec] + [pl.BlockSpec(memory_space=pl.ANY)] * len(extra),
        out_specs=[out_spec] * len(out_dtypes), out_shape=[jax.ShapeDtypeStruct(out_shape, d) for d in out_dtypes],
        scratch_shapes=scratch, compiler_params=_cparams("parallel", "parallel", "arbitrary"))(a, b, *extra)
    return res if also_bf16 else res[0]


def _rms_in(x, nw):
    s, d = x.shape

    def body(x_ref, w_ref, h_ref, ht_ref):
        xv = x_ref[...]
        r = lax.rsqrt(jnp.mean(xv * xv, axis=-1, keepdims=True) + NORM_EPS)
        h = xv * r * w_ref[...]
        h_ref[...] = h.astype(BF16)
        ht_ref[...] = h.T.astype(BF16)

    return pl.pallas_call(
        body, name="rms_in", grid=(s // BIG_TILE,),
        in_specs=[pl.BlockSpec((BIG_TILE, d), lambda i: (i, 0)), pl.BlockSpec((1, d), lambda i: (0, 0))],
        out_specs=[pl.BlockSpec((BIG_TILE, d), lambda i: (i, 0)), pl.BlockSpec((d, BIG_TILE), lambda i: (0, i))],
        out_shape=[jax.ShapeDtypeStruct((s, d), BF16), jax.ShapeDtypeStruct((d, s), BF16)],
        compiler_params=_cparams("parallel"))(x, nw)


def _rms_in_bwd(x, nw, dh, dx2):
    s, d = x.shape

    def body(x_ref, w_ref, dh_ref, dx2_ref, dx_ref, dw_ref):
        i = pl.program_id(0)
        xv = x_ref[...]
        r = lax.rsqrt(jnp.mean(xv * xv, axis=-1, keepdims=True) + NORM_EPS)
        dhv = dh_ref[...]
        dyw = dhv * w_ref[...]
        dx_ref[...] = dx2_ref[...] + r * dyw - xv * (r * r * r) * jnp.mean(dyw * xv, axis=-1, keepdims=True)
        part = jnp.sum(dhv * xv * r, axis=0, keepdims=True)

        @pl.when(i == 0)
        def _():
            dw_ref[...] = part

        @pl.when(i > 0)
        def _():
            dw_ref[...] += part

    row = pl.BlockSpec((BIG_TILE, d), lambda i: (i, 0))
    vec = pl.BlockSpec((1, d), lambda i: (0, 0))
    return pl.pallas_call(
        body, name="rms_in_bwd", grid=(s // BIG_TILE,), in_specs=[row, vec, row, row], out_specs=[row, vec],
        out_shape=[jax.ShapeDtypeStruct((s, d), F32), jax.ShapeDtypeStruct((1, d), F32)],
        compiler_params=_cparams("arbitrary"))(x, nw, dh, dx2)


def _shift_down(cur, prev8, k):
    rc = pltpu.roll(cur, k, 0)
    rp = pltpu.roll(prev8, k, 0)
    row = lax.broadcasted_iota(jnp.int32, prev8.shape, 0)
    top = jnp.where(row < k, rp, rc[:8])
    return jnp.concatenate([top, rc[8:]], axis=0)


def _shift_up(cur, next8, k):
    t = cur.shape[0]
    rc = pltpu.roll(cur, t - k, 0)
    rn = pltpu.roll(next8, 8 - k, 0)
    row = lax.broadcasted_iota(jnp.int32, next8.shape, 0)
    bot = jnp.where(row >= 8 - k, rn, rc[t - 8:])
    return jnp.concatenate([rc[:t - 8], bot], axis=0)


def _conv_fwd(proj, conv_w):
    s = proj.shape[0]
    tile = min(s, CONV_TILE)
    t8 = tile // 8

    def body(u_ref, up_ref, w_ref, c_ref, y_ref):
        i = pl.program_id(0)
        part = pl.program_id(1)
        cur = u_ref[...]
        prev8 = jnp.where(i > 0, up_ref[...], 0.0)
        w = w_ref[...]
        c = cur * w[3:4, :]
        for k in (1, 2, 3):
            c = c + _shift_down(cur, prev8, k) * w[3 - k:4 - k, :]
        c_ref[...] = c
        a = _silu(c)
        for h in range(DN_HEADS):
            ah = a[:, h * DN_D:(h + 1) * DN_D]
            r = lax.rsqrt(jnp.sum(ah * ah, axis=-1, keepdims=True) + NORM_EPS)
            y_ref[:, h * DN_D:(h + 1) * DN_D] = jnp.where(part < 2, ah * r, ah)

    return pl.pallas_call(
        body, name="conv_fwd", grid=(s // tile, 3),
        in_specs=[pl.BlockSpec((tile, DN_W), lambda i, p: (i, p)),
                  pl.BlockSpec((8, DN_W), lambda i, p: (jnp.maximum(i * t8 - 1, 0), p)),
                  pl.BlockSpec((4, DN_W), lambda i, p: (0, p))],
        out_specs=[pl.BlockSpec((tile, DN_W), lambda i, p: (i, p))] * 2,
        out_shape=[jax.ShapeDtypeStruct((s, 3 * DN_W), F32)] * 2,
        compiler_params=_cparams("parallel", "parallel"))(proj, proj, conv_w)


def _act_bwd(cv, dyv, normalised):
    out = []
    for h in range(DN_HEADS):
        sl = slice(h * DN_D, (h + 1) * DN_D)
        ch, dyh = cv[:, sl], dyv[:, sl]
        ah = _silu(ch)
        r = lax.rsqrt(jnp.sum(ah * ah, axis=-1, keepdims=True) + NORM_EPS)
        dn = r * dyh - ah * (r * r * r) * jnp.sum(dyh * ah, axis=-1, keepdims=True)
        out.append(jnp.where(normalised, dn, dyh) * _silu_grad(ch))
    return jnp.concatenate(out, axis=1)


DPROJ_IN = pl.BlockSpec(memory_space=pl.ANY)


def _conv_bwd(proj, c_pre, dqkv, conv_w, dproj):
    s = proj.shape[0]
    tile = min(s, CONV_TILE)
    t8 = tile // 8
    nrow = s // tile
    last8 = s // 8 - 1

    def body(u_ref, c_ref, cn_ref, dy_ref, dyn_ref, w_ref, dproj_in, du_ref, dw_ref):
        i = pl.program_id(1)
        normalised = pl.program_id(0) < 2
        cur = u_ref[...]
        dcv = _act_bwd(c_ref[...], dy_ref[...], normalised)
        next8 = jnp.where(i < nrow - 1, _act_bwd(cn_ref[...], dyn_ref[...], normalised), 0.0)
        w = w_ref[...]

        @pl.when(i == 0)
        def _():
            dw_ref[...] = jnp.zeros_like(dw_ref)

        du = dcv * w[3:4, :]
        dw_ref[3:4, :] += jnp.sum(cur * dcv, axis=0, keepdims=True)
        for k in (1, 2, 3):
            ahead = _shift_up(dcv, next8, k)
            du = du + ahead * w[3 - k:4 - k, :]
            dw_ref[3 - k:4 - k, :] += jnp.sum(cur * ahead, axis=0, keepdims=True)
        du_ref[...] = du.astype(BF16)

    blk = pl.BlockSpec((tile, DN_W), lambda p, i: (i, p))
    nxt = pl.BlockSpec((8, DN_W), lambda p, i: (jnp.minimum((i + 1) * t8, last8), p))
    return pl.pallas_call(
        body, name="conv_bwd", grid=(3, nrow),
        in_specs=[blk, blk, nxt, blk, nxt, pl.BlockSpec((4, DN_W), lambda p, i: (0, p)), DPROJ_IN],
        out_specs=[blk, pl.BlockSpec((4, DN_W), lambda p, i: (0, p))],
        out_shape=[jax.ShapeDtypeStruct((s, PW), BF16), jax.ShapeDtypeStruct((4, 3 * DN_W), F32)],
        input_output_aliases={6: 0},
        compiler_params=_cparams("parallel", "arbitrary"))(proj, c_pre, c_pre, dqkv, dqkv, conv_w, dproj)


def _gates_fwd(proj, gate_par):
    s = proj.shape[0]

    def body(ba_ref, par_ref, o_ref):
        v = ba_ref[...]
        lane = lax.broadcasted_iota(jnp.int32, v.shape, 1)
        beta = _sigmoid(v)
        g = -jnp.exp(par_ref[0:1, :]) * _softplus(v + par_ref[1:2, :])
        o_ref[...] = jnp.where(lane < DN_HEADS, beta, jnp.where(lane < 2 * DN_HEADS, g, 0.0))

    return pl.pallas_call(
        body, name="gates_fwd", grid=(s // ROW_TILE,),
        in_specs=[pl.BlockSpec((ROW_TILE, 128), lambda i: (i, OFF_BA // 128)), pl.BlockSpec((8, 128), lambda i: (0, 0))],
        out_specs=pl.BlockSpec((ROW_TILE, 128), lambda i: (i, 0)),
        out_shape=jax.ShapeDtypeStruct((s, 128), F32), compiler_params=_cparams("parallel"))(proj, gate_par)


def _gates_bwd(proj, gate_par, dbg, dproj):
    s = proj.shape[0]

    def body(ba_ref, par_ref, d_ref, dproj_in, o_ref, dpar_ref):
        i = pl.program_id(0)
        v = ba_ref[...]
        dv = d_ref[...]
        lane = lax.broadcasted_iota(jnp.int32, v.shape, 1)
        beta = _sigmoid(v)
        nega = -jnp.exp(par_ref[0:1, :])
        xs = v + par_ref[1:2, :]
        dsp = dv * nega * _sigmoid(xs)
        dal = dv * nega * _softplus(xs)
        is_b = lane < DN_HEADS
        is_g = jnp.logical_and(lane >= DN_HEADS, lane < 2 * DN_HEADS)
        o_ref[:, :128] = jnp.where(is_b, dv * beta * (1.0 - beta), jnp.where(is_g, dsp, 0.0)).astype(BF16)
        o_ref[:, 128:] = jnp.zeros((ROW_TILE, PW - OFF_BA - 128), BF16)
        r0 = jnp.sum(jnp.where(is_g, dal, 0.0), axis=0, keepdims=True)
        r1 = jnp.sum(jnp.where(is_g, dsp, 0.0), axis=0, keepdims=True)

        @pl.when(i == 0)
        def _():
            dpar_ref[...] = jnp.zeros_like(dpar_ref)

        dpar_ref[0:1, :] += r0
        dpar_ref[1:2, :] += r1

    return pl.pallas_call(
        body, name="gates_bwd", grid=(s // ROW_TILE,),
        in_specs=[pl.BlockSpec((ROW_TILE, 128), lambda i: (i, OFF_BA // 128)), pl.BlockSpec((8, 128), lambda i: (0, 0)),
                  pl.BlockSpec((ROW_TILE, 128), lambda i: (i, 0)), DPROJ_IN],
        out_specs=[pl.BlockSpec((ROW_TILE, PW - OFF_BA), lambda i: (i, OFF_BA // (PW - OFF_BA))),
                   pl.BlockSpec((8, 128), lambda i: (0, 0))],
        out_shape=[jax.ShapeDtypeStruct((s, PW), BF16), jax.ShapeDtypeStruct((8, 128), F32)],
        input_output_aliases={3: 0},
        compiler_params=_cparams("arbitrary"))(proj, gate_par, dbg, dproj)


def _chunk_masks():
    c = DN_CHUNK
    ii = lax.broadcasted_iota(jnp.int32, (c, c), 0)
    jj = lax.broadcasted_iota(jnp.int32, (c, c), 1)
    return dict(ii=ii, jj=jj, lower=(ii >= jj), strict=(ii > jj),
                lower_f=(ii >= jj).astype(BF16), upper_f=(ii <= jj).astype(BF16))


class _Heads:
    def __init__(self, xs):
        self.xs = list(xs)

    def _bin(self, o, f):
        if isinstance(o, _Heads):
            return _Heads([f(a, b) for a, b in zip(self.xs, o.xs)])
        return _Heads([f(a, o) for a in self.xs])

    def __add__(self, o):
        return self._bin(o, lambda a, b: a + b)

    def __sub__(self, o):
        return self._bin(o, lambda a, b: a - b)

    def __mul__(self, o):
        return self._bin(o, lambda a, b: a * b)

    __radd__ = __add__
    __rmul__ = __mul__

    def __neg__(self):
        return _Heads([-a for a in self.xs])

    def __getitem__(self, i):
        return _Heads([a[i] for a in self.xs])


def _hmap(f, *args):
    n = next(len(a.xs) for a in args if isinstance(a, _Heads))
    return _Heads([f(*[(a.xs[h] if isinstance(a, _Heads) else a) for a in args]) for h in range(n)])


def _hdot(a, b):
    return _hmap(_dot, a, b)


def _hdot_nt(a, b):
    return _hmap(_dot_nt, a, b)


def _hdot_tn(a, b):
    return _hmap(_dot_tn, a, b)


def _hcat(a, b, axis):
    return _hmap(lambda x, y: jnp.concatenate([x, y], axis=axis), a, b)


def _hsum(a, axis):
    return _hmap(lambda t: jnp.sum(t, axis=axis, keepdims=True), a)


def _hwhere(c, a, b):
    return _hmap(jnp.where, c, a, b)


def _chunk_gates(mk, bg):
    c = DN_CHUNK
    gc_all = _dot_exact_lhs(mk["lower_f"], bg)
    rows = jnp.concatenate([gc_all, gc_all], axis=0).T
    hs = range(DN_HEADS)
    return (_Heads(bg[:, h:h + 1] for h in hs), _Heads(gc_all[:, DN_HEADS + h:DN_HEADS + h + 1] for h in hs),
            _Heads(rows[DN_HEADS + h:DN_HEADS + h + 1, :] for h in hs))


def _chunk_common(mk, q, k, beta_col, gc_col, gc_r):
    c = DN_CHUNK
    lower, strict = mk["lower"], mk["strict"]
    qs = q * (DN_D ** -0.5)
    beta_b = _hmap(lambda t: jnp.broadcast_to(t, (c, DN_D)), beta_col)
    gc_b = _hmap(lambda t: jnp.broadcast_to(t, (c, DN_D)), gc_col)
    gc_sq = gc_b[:, :c]
    gam = _hwhere(lower, _hmap(lambda t: jnp.exp(jnp.minimum(t, 0.0)), gc_sq - gc_r[:, :c]), 0.0)
    egc = _hmap(jnp.exp, gc_b)
    gl = gc_b[c - 1:c, :]
    ekd = _hmap(jnp.exp, gl - gc_b)
    dl = _hmap(jnp.exp, gl)
    kb = k * beta_b
    scores = _hdot_nt(_hcat(kb, qs, 0), k)
    a_strict = _hwhere(strict, scores[:c] * gam, 0.0)
    aqk = _hwhere(lower, scores[c:] * gam, 0.0)
    return dict(k=k, qs=qs, beta_b=beta_b, gc_b=gc_b, gam=gam, egc=egc, ekd=ekd, dl=dl, kb=kb, a_strict=a_strict, aqk=aqk)


def _unit_lower_inverse_minus_eye(n_strict, ii, jj):
    same = lax.shift_right_logical(ii, 4) == lax.shift_right_logical(jj, 4)
    dmat = _hwhere(same, n_strict, 0.0)
    omat = n_strict - dmat
    d2 = _hdot(dmat, dmat)
    d4 = _hdot(d2, d2)
    d8 = _hdot(d4, d4)
    x1 = d2 - dmat - _hdot(dmat, d2)
    x2 = x1 + d4 + _hdot(x1, d4)
    x3 = x2 + d8 + _hdot(x2, d8)
    n1 = omat + _hdot(x3, omat)
    n2 = _hdot(n1, n1)
    y = n2 - n1 - _hdot(n1, n2)
    return y + x3 + _hdot(y, x3)


GDR_HEAD_SETS = (range(0, DN_HEADS),)


def _gdr_fwd(qkv, bg):
    s = qkv.shape[0]
    c = DN_CHUNK
    n = s // c

    def body(q_ref, k_ref, v_ref, bg_ref, o_ref, u_ref, w_ref, vn_ref, tm_ref, st_ref, state):
        @pl.when(pl.program_id(0) == 0)
        def _():
            state[...] = jnp.zeros_like(state)

        mk = _chunk_masks()
        gates = _chunk_gates(mk, bg_ref[...])
        for hs in GDR_HEAD_SETS:
            sls = [slice(h * DN_D, (h + 1) * DN_D) for h in hs]
            cm = _chunk_common(mk, _Heads(q_ref[:, sl] for sl in sls), _Heads(k_ref[:, sl] for sl in sls),
                               *[_Heads(g.xs[h] for h in hs) for g in gates])
            tm = _unit_lower_inverse_minus_eye(cm["a_strict"], mk["ii"], mk["jj"])
            rhs_u = _Heads(v_ref[:, sl] for sl in sls) * cm["beta_b"]
            rhs_w = cm["kb"] * cm["egc"]
            t_rhs = _hdot(tm, _hcat(rhs_u, rhs_w, 1))
            u = rhs_u + t_rhs[:, :DN_D]
            w = rhs_w + t_rhs[:, DN_D:]
            st = _Heads(state[h] for h in hs)
            on_state = _hdot(_hcat(w, cm["qs"] * cm["egc"], 0), st)
            v_new = u - on_state[:c]
            o = on_state[c:] + _hdot(cm["aqk"], v_new)
            st_new = st * cm["dl"] + _hdot_tn(cm["k"] * cm["ekd"], v_new)
            for i, (h, sl) in enumerate(zip(hs, sls)):
                o_ref[:, sl] = o.xs[i]
                u_ref[:, sl] = u.xs[i]
                w_ref[:, sl] = w.xs[i]
                vn_ref[:, sl] = v_new.xs[i]
                tm_ref[h, 0] = tm.xs[i]
                st_ref[h, 0] = st.xs[i]
                state[h] = st_new.xs[i]

    def part(p):
        return pl.BlockSpec((c, DN_W), lambda j: (j, p))

    return pl.pallas_call(
        body, name="gdr_fwd", grid=(n,),
        in_specs=[part(0), part(1), part(2), pl.BlockSpec((c, 128), lambda j: (j, 0))],
        out_specs=[part(0)] * 4 + [pl.BlockSpec((DN_HEADS, 1, c, c), lambda j: (0, j, 0, 0)),
                                   pl.BlockSpec((DN_HEADS, 1, DN_D, DN_D), lambda j: (0, j, 0, 0))],
        out_shape=[jax.ShapeDtypeStruct((s, DN_W), F32)] * 4
        + [jax.ShapeDtypeStruct((DN_HEADS, n, c, c), F32), jax.ShapeDtypeStruct((DN_HEADS, n, DN_D, DN_D), F32)],
        scratch_shapes=[pltpu.VMEM((DN_HEADS, DN_D, DN_D), F32)],
        compiler_params=_cparams("arbitrary"))(qkv, qkv, qkv, bg)


def _gdr_bwd(qkv, bg, u, w, vn, tmat, states, do):
    s = qkv.shape[0]
    c = DN_CHUNK
    n = s // c

    def body(q_ref, k_ref, v_ref, bg_ref, u_ref, w_ref, vn_ref, tm_ref, st_ref, do_ref,
             dqkv_ref, dbg_ref, dstate):
        @pl.when(pl.program_id(0) == 0)
        def _():
            dstate[...] = jnp.zeros_like(dstate)

        mk = _chunk_masks()
        lower, strict = mk["lower"], mk["strict"]
        bg = bg_ref[...]
        ones = jnp.ones((c, DN_D), BF16)
        rowi = lax.broadcasted_iota(jnp.int32, (c, DN_D), 0)
        lane = lax.broadcasted_iota(jnp.int32, (c, 128), 1)
        hs = range(DN_HEADS)
        sls = [slice(h * DN_D, (h + 1) * DN_D) for h in hs]

        def heads_of(ref):
            return _Heads(ref[:, sl] for sl in sls)

        cm = _chunk_common(mk, heads_of(q_ref), heads_of(k_ref), *_chunk_gates(mk, bg))
        k, qs, beta_b = cm["k"], cm["qs"], cm["beta_b"]
        gam, egc, ekd, dl, kb = cm["gam"], cm["egc"], cm["ekd"], cm["dl"], cm["kb"]
        aqk, a_strict = cm["aqk"], cm["a_strict"]
        v, uu, ww, v_new, dov = heads_of(v_ref), heads_of(u_ref), heads_of(w_ref), heads_of(vn_ref), heads_of(do_ref)
        st = _Heads(st_ref[h, 0] for h in hs)
        dsn = _Heads(dstate[h] for h in hs)
        qd = qs * egc
        kd = k * ekd

        dv_new = _hdot_tn(aqk, dov) + _hdot(kd, dsn)
        do_sv = _hdot_nt(dov, _hcat(st, v_new, 0))
        dqd = do_sv[:, :DN_D]
        daqk = _hwhere(lower, do_sv[:, DN_D:], 0.0)
        dkd = _hdot_nt(v_new, dsn)
        ddl = _hsum(_hsum(dsn * st, 1), 0)
        dw = -_hdot_nt(dv_new, st)
        ds_new = dsn * dl + _hdot_tn(_hcat(qd, -ww, 0), _hcat(dov, dv_new, 0))

        tm = _Heads(tm_ref[h, 0] for h in hs)
        tt = _hdot_tn(tm, _hcat(dv_new, dw, 1))
        dru = dv_new + tt[:, :DN_D]
        drw = dw + tt[:, DN_D:]
        dn = _hwhere(strict, -_hdot_nt(_hcat(dru, drw, 1), _hcat(uu, ww, 1)), 0.0)
        dag = dn * gam
        dqg = daqk * gam
        both = _hcat(dag, dqg, 0)
        on_k = _hdot(both, k)
        dkb = on_k[:c] + drw * egc
        dqs = on_k[c:] + dqd * egc
        dk = _hdot_tn(both, _hcat(kb, qs, 0)) + dkb * beta_b + dkd * ekd
        pmat = dn * a_strict + daqk * aqk
        tkd = _hsum(dkd * kd, -1)
        dgc = (_hsum(pmat, -1) - _hmap(_dot_tn_exact_rhs, pmat, ones) + _hsum(drw * (kb * egc), -1)
               + _hsum(dqd * qd, -1) - tkd)
        last = _hsum(tkd, 0) + ddl * dl
        dgc = dgc + _hwhere(rowi == c - 1, last, 0.0)
        dbeta = _hsum(dru * v, -1) + _hsum(dkb * k, -1)
        dq = dqs * (DN_D ** -0.5)
        dv = dru * beta_b

        dgc_all = jnp.zeros((c, 128), F32)
        dbg = jnp.zeros((c, 128), F32)
        for h, sl in zip(hs, sls):
            dqkv_ref[:, sl] = dq.xs[h]
            dqkv_ref[:, DN_W + h * DN_D:DN_W + (h + 1) * DN_D] = dk.xs[h]
            dqkv_ref[:, 2 * DN_W + h * DN_D:2 * DN_W + (h + 1) * DN_D] = dv.xs[h]
            dstate[h] = ds_new.xs[h]
            dgc_all = dgc_all + jnp.where(lane == DN_HEADS + h, dgc.xs[h], 0.0)
            dbg = dbg + jnp.where(lane == h, dbeta.xs[h], 0.0)
        dbg_ref[...] = dbg + _dot_exact_lhs(mk["upper_f"], dgc_all)

    def part(p):
        return pl.BlockSpec((c, DN_W), lambda j: (n - 1 - j, p))

    vec = pl.BlockSpec((c, 128), lambda j: (n - 1 - j, 0))
    return pl.pallas_call(
        body, name="gdr_bwd", grid=(n,),
        in_specs=[part(0), part(1), part(2), vec, part(0), part(0), part(0),
                  pl.BlockSpec((DN_HEADS, 1, c, c), lambda j: (0, n - 1 - j, 0, 0)),
                  pl.BlockSpec((DN_HEADS, 1, DN_D, DN_D), lambda j: (0, n - 1 - j, 0, 0)), part(0)],
        out_specs=[pl.BlockSpec((c, 3 * DN_W), lambda j: (n - 1 - j, 0)), vec],
        out_shape=[jax.ShapeDtypeStruct((s, 3 * DN_W), F32), jax.ShapeDtypeStruct((s, 128), F32)],
        scratch_shapes=[pltpu.VMEM((DN_HEADS, DN_D, DN_D), F32)],
        compiler_params=_cparams("arbitrary"))(qkv, qkv, qkv, bg, u, w, vn, tmat, states, do)


def _gdr_out(o, proj, dnw):
    s = o.shape[0]

    def body(o_ref, z_ref, w_ref, y_ref, yt_ref):
        ov, zv, wv = o_ref[...], z_ref[...], w_ref[...]
        for h in range(DN_HEADS):
            sl = slice(h * DN_D, (h + 1) * DN_D)
            oh = ov[:, sl]
            r = lax.rsqrt(jnp.mean(oh * oh, axis=-1, keepdims=True) + NORM_EPS)
            y = (oh * r * wv) * _silu(zv[:, sl])
            y_ref[:, sl] = y.astype(BF16)
            yt_ref[sl, :] = y.T.astype(BF16)

    row = pl.BlockSpec((BIG_TILE, DN_W), lambda i: (i, 0))
    return pl.pallas_call(
        body, name="gdr_out", grid=(s // BIG_TILE,),
        in_specs=[row, pl.BlockSpec((BIG_TILE, DN_W), lambda i: (i, OFF_Z_A // DN_W)), pl.BlockSpec((1, DN_D), lambda i: (0, 0))],
        out_specs=[row, pl.BlockSpec((DN_W, BIG_TILE), lambda i: (0, i))],
        out_shape=[jax.ShapeDtypeStruct((s, DN_W), BF16), jax.ShapeDtypeStruct((DN_W, s), BF16)],
        compiler_params=_cparams("parallel"))(o, proj, dnw)


def _gdr_out_bwd(o, proj, dnw, dy):
    s = o.shape[0]

    def body(o_ref, z_ref, w_ref, dy_ref, do_ref, dz_ref, dw_ref):
        i = pl.program_id(0)
        ov, zv, wv, dyv = o_ref[...], z_ref[...], w_ref[...], dy_ref[...]
        acc = jnp.zeros((1, DN_D), F32)
        for h in range(DN_HEADS):
            sl = slice(h * DN_D, (h + 1) * DN_D)
            oh, zh, dh = ov[:, sl], zv[:, sl], dyv[:, sl]
            r = lax.rsqrt(jnp.mean(oh * oh, axis=-1, keepdims=True) + NORM_EPS)
            dn = dh * _silu(zh)
            dz_ref[:, sl] = (dh * (oh * r * wv) * _silu_grad(zh)).astype(BF16)
            acc = acc + jnp.sum(dn * oh * r, axis=0, keepdims=True)
            dnw_ = dn * wv
            do_ref[:, sl] = r * dnw_ - oh * (r * r * r) * jnp.mean(dnw_ * oh, axis=-1, keepdims=True)

        @pl.when(i == 0)
        def _():
            dw_ref[...] = acc

        @pl.when(i > 0)
        def _():
            dw_ref[...] += acc

    row = pl.BlockSpec((ROW_TILE, DN_W), lambda i: (i, 0))
    vec = pl.BlockSpec((1, DN_D), lambda i: (0, 0))
    return pl.pallas_call(
        body, name="gdr_out_bwd", grid=(s // ROW_TILE,),
        in_specs=[row, pl.BlockSpec((ROW_TILE, DN_W), lambda i: (i, OFF_Z_A // DN_W)), vec, row],
        out_specs=[row, pl.BlockSpec((ROW_TILE, DN_W), lambda i: (i, OFF_Z_A // DN_W)), vec],
        out_shape=[jax.ShapeDtypeStruct((s, DN_W), F32), jax.ShapeDtypeStruct((s, PW), BF16),
                   jax.ShapeDtypeStruct((1, DN_D), F32)],
        compiler_params=_cparams("arbitrary"))(o, proj, dnw, dy)


def _slope(group, head):
    idx = (group * DIL_HEADS + head + 1).astype(F32)
    return jnp.exp(jnp.full((1, 128), -8.0 * math.log(2.0) / (N_DIL * DIL_HEADS), F32) * idx)


def _att_scores(qb, k_cur, k_prev, slope_d, has_prev):
    iq = lax.broadcasted_iota(jnp.int32, (ATT_BLOCK, ATT_BLOCK), 0)
    jk = lax.broadcasted_iota(jnp.int32, (ATT_BLOCK, ATT_BLOCK), 1)
    dist_c = (iq - jk).astype(F32)
    s_cur = jnp.where(iq >= jk, _dot_nt(qb, k_cur) - slope_d * dist_c, NEG)
    s_prev = jnp.where(jnp.logical_and(jk >= iq, has_prev),
                       _dot_nt(qb, k_prev) - slope_d * (dist_c + float(ATT_BLOCK)), NEG)
    return s_cur, s_prev


def _att_scores_whole(qb, k, slope_d):
    n = 2 * ATT_BLOCK
    dist = lax.broadcasted_iota(jnp.int32, (n, n), 0) - lax.broadcasted_iota(jnp.int32, (n, n), 1)
    valid = jnp.logical_and(dist >= 0, dist <= ATT_BLOCK)
    return jnp.where(valid, _dot_nt(qb, k) - slope_d[:, 0:1] * dist.astype(F32), NEG)


def _att_tiles(i, dil, nb):
    tiles = nb // 2
    per = dil * tiles // ATT_UNROLL
    assert nb % 2 == 0 and tiles >= 2 and per * ATT_UNROLL == dil * tiles
    for i0 in range(per):
        ts = [divmod(i0 + u * per, tiles) for u in range(ATT_UNROLL)]
        assert all(a[0] != b[0] or abs(a[1] - b[1]) >= 2 for n, a in enumerate(ts) for b in ts[n + 1:])
    qrows, krows, has_prev = [], [], []
    for u in range(ATT_UNROLL):
        t = i + u * per
        r = lax.div(t, tiles)
        j = lax.rem(t, tiles)
        qbase = r + dil * 2 * ATT_BLOCK * j
        kbase = qbase - dil * ATT_BLOCK * jnp.minimum(j, 1)
        if dil == 1:
            qbase, kbase = pl.multiple_of(qbase, ATT_BLOCK), pl.multiple_of(kbase, ATT_BLOCK)
        qrows.append(pl.ds(qbase, 2 * ATT_BLOCK, stride=dil))
        krows.append(pl.ds(kbase, 3 * ATT_BLOCK, stride=dil))
        has_prev.append(j > 0)
    return qrows, krows, has_prev


def _att_scores_tile(qb, k, slope_d, has_prev):
    iq = lax.broadcasted_iota(jnp.int32, (2 * ATT_BLOCK, 3 * ATT_BLOCK), 0)
    ck = lax.broadcasted_iota(jnp.int32, (2 * ATT_BLOCK, 3 * ATT_BLOCK), 1)
    dist = iq - ck + jnp.where(has_prev, ATT_BLOCK, 0)
    valid = jnp.logical_and(dist >= 0, dist <= ATT_BLOCK)
    return jnp.where(valid, _dot_nt(qb, k) - slope_d[:, 0:1] * dist.astype(F32), NEG)


ATT_UNROLL = 4


def _att_blocks(i, dil, nb):
    per = dil * nb // ATT_UNROLL
    assert per * ATT_UNROLL == dil * nb
    for i0 in range(per):
        blocks = [divmod(i0 + u * per, nb) for u in range(ATT_UNROLL)]
        assert all(a[0] != b[0] or abs(a[1] - b[1]) >= 2 for n, a in enumerate(blocks) for b in blocks[n + 1:])
    curs, prvs, has_prev = [], [], []
    for u in range(ATT_UNROLL):
        t = i + u * per
        r = lax.div(t, nb)
        j = lax.rem(t, nb)
        base = r + dil * ATT_BLOCK * j
        pbase = base - dil * ATT_BLOCK * jnp.minimum(j, 1)
        if dil == 1:
            base, pbase = pl.multiple_of(base, ATT_BLOCK), pl.multiple_of(pbase, ATT_BLOCK)
        curs.append(pl.ds(base, ATT_BLOCK, stride=dil))
        prvs.append(pl.ds(pbase, ATT_BLOCK, stride=dil))
        has_prev.append(j > 0)
    return curs, prvs, has_prev


def _att_fwd(proj, group):
    s = proj.shape[0]
    dil = DIL_GROUPS[group][1]
    assert DIL_GROUPS[group][0] // dil == ATT_BLOCK
    nb = s // dil // ATT_BLOCK
    assert nb * dil * ATT_BLOCK == s

    def body(q_ref, k_ref, v_ref, o_ref, lse_ref):
        def emit(rows, num, den, mx):
            o_ref[rows, :] = num / den
            lse_ref[rows, :] = jnp.broadcast_to(mx + jnp.log(den), (num.shape[0], DIL_DH))

        slope_d = _slope(group, pl.program_id(0)) * float(dil)

        def step(i, carry):
            curs, prvs, has_prev = _att_blocks(i, dil, nb)
            us = range(ATT_UNROLL)
            qb = [q_ref[c, :] * (DIL_DH ** -0.5) for c in curs]
            sc = [_att_scores(qb[u], k_ref[curs[u], :], k_ref[prvs[u], :], slope_d, has_prev[u]) for u in us]
            mx = [jnp.maximum(jnp.max(a, axis=-1, keepdims=True), jnp.max(b, axis=-1, keepdims=True)) for a, b in sc]
            p_cur = [jnp.exp(sc[u][0] - mx[u]) for u in us]
            p_prev = [jnp.exp(sc[u][1] - mx[u]) for u in us]
            den = [jnp.sum(p_cur[u], axis=-1, keepdims=True) + jnp.sum(p_prev[u], axis=-1, keepdims=True) for u in us]
            num = [_dot(p_cur[u], v_ref[curs[u], :]) + _dot(p_prev[u], v_ref[prvs[u], :]) for u in us]
            for u in us:
                emit(curs[u], num[u], den[u], mx[u])
            return carry

        def step_whole(i, carry):
            rows = [pl.ds(i * ATT_UNROLL + u, 2 * ATT_BLOCK, stride=dil) for u in range(ATT_UNROLL)]
            sc = [_att_scores_whole(q_ref[r, :] * (DIL_DH ** -0.5), k_ref[r, :], slope_d) for r in rows]
            mx = [jnp.max(a, axis=-1, keepdims=True) for a in sc]
            p = [jnp.exp(a - m) for a, m in zip(sc, mx)]
            num = [_dot(pu, v_ref[r, :]) for pu, r in zip(p, rows)]
            for u, r in enumerate(rows):
                emit(r, num[u], jnp.sum(p[u], axis=-1, keepdims=True), mx[u])
            return carry

        def step_tile(i, carry):
            qrows, krows, has_prev = _att_tiles(i, dil, nb)
            us = range(ATT_UNROLL)
            sc = [_att_scores_tile(q_ref[qrows[u], :] * (DIL_DH ** -0.5), k_ref[krows[u], :], slope_d, has_prev[u]) for u in us]
            mx = [jnp.max(a, axis=-1, keepdims=True) for a in sc]
            p = [jnp.exp(a - m) for a, m in zip(sc, mx)]
            num = [_dot(p[u], v_ref[krows[u], :]) for u in us]
            for u in us:
                emit(qrows[u], num[u], jnp.sum(p[u], axis=-1, keepdims=True), mx[u])
            return carry

        if nb == 2:
            lax.fori_loop(0, dil // ATT_UNROLL, step_whole, 0)
        elif nb % 2 == 0:
            lax.fori_loop(0, dil * nb // 2 // ATT_UNROLL, step_tile, 0)
        else:
            lax.fori_loop(0, dil * nb // ATT_UNROLL, step, 0)

    def col(off):
        return pl.BlockSpec((s, DIL_DH), lambda h: (0, off // DIL_DH + group * DIL_HEADS + h))

    out = pl.BlockSpec((s, DIL_DH), lambda h: (0, h))
    return pl.pallas_call(
        body, name=f"att_fwd{group}", grid=(DIL_HEADS,), in_specs=[col(OFF_Q_B), col(OFF_K_B), col(OFF_V_B)],
        out_specs=[out, out], out_shape=[jax.ShapeDtypeStruct((s, DIL_W), F32)] * 2,
        compiler_params=_cparams("parallel"))(proj, proj, proj)


def _att_bwd(proj, group, do, lse, delta):
    s = proj.shape[0]
    dil = DIL_GROUPS[group][1]
    nb = s // dil // ATT_BLOCK

    def body(q_ref, k_ref, v_ref, do_ref, lse_ref, dl_ref, dq_ref, dk_ref, dv_ref, dq_acc, dk_acc, dv_acc):
        slope_d = _slope(group, pl.program_id(0)) * float(dil)
        dk_acc[...] = jnp.zeros_like(dk_acc)
        dv_acc[...] = jnp.zeros_like(dv_acc)

        def step(i, carry):
            curs, prvs, has_prev = _att_blocks(i, dil, nb)
            us = range(ATT_UNROLL)
            qb = [q_ref[c, :] * (DIL_DH ** -0.5) for c in curs]
            k_cur, k_prev = [k_ref[c, :] for c in curs], [k_ref[p, :] for p in prvs]
            v_cur, v_prev = [v_ref[c, :] for c in curs], [v_ref[p, :] for p in prvs]
            sc = [_att_scores(qb[u], k_cur[u], k_prev[u], slope_d, has_prev[u]) for u in us]
            lse_b, delta_b, dob = [lse_ref[c, :] for c in curs], [dl_ref[c, :] for c in curs], [do_ref[c, :] for c in curs]
            p_cur = [jnp.exp(sc[u][0] - lse_b[u]) for u in us]
            p_prev = [jnp.exp(sc[u][1] - lse_b[u]) for u in us]
            ds_cur = [p_cur[u] * (_dot_nt(dob[u], v_cur[u]) - delta_b[u]) for u in us]
            ds_prev = [p_prev[u] * (_dot_nt(dob[u], v_prev[u]) - delta_b[u]) for u in us]
            dq = [(_dot(ds_cur[u], k_cur[u]) + _dot(ds_prev[u], k_prev[u])) * (DIL_DH ** -0.5) for u in us]
            dk_c = [_dot_tn(ds_cur[u], qb[u]) for u in us]
            dv_c = [_dot_tn(p_cur[u], dob[u]) for u in us]
            dk_p = [_dot_tn(ds_prev[u], qb[u]) for u in us]
            dv_p = [_dot_tn(p_prev[u], dob[u]) for u in us]
            for u in us:
                dq_acc[curs[u], :] = dq[u]
                dk_acc[curs[u], :] += dk_c[u]
                dv_acc[curs[u], :] += dv_c[u]
            for u in us:
                dk_acc[prvs[u], :] += dk_p[u]
                dv_acc[prvs[u], :] += dv_p[u]
            return carry

        def step_whole(i, carry):
            rows = [pl.ds(i * ATT_UNROLL + u, 2 * ATT_BLOCK, stride=dil) for u in range(ATT_UNROLL)]
            qb = [q_ref[r, :] * (DIL_DH ** -0.5) for r in rows]
            kk, vv, dob = [k_ref[r, :] for r in rows], [v_ref[r, :] for r in rows], [do_ref[r, :] for r in rows]
            sc = [_att_scores_whole(qb[u], kk[u], slope_d) for u in range(ATT_UNROLL)]
            p = [jnp.exp(sc[u] - lse_ref[r, :][:, 0:1]) for u, r in enumerate(rows)]
            ds = [p[u] * (_dot_nt(dob[u], vv[u]) - dl_ref[r, :][:, 0:1]) for u, r in enumerate(rows)]
            dq = [_dot(ds[u], kk[u]) * (DIL_DH ** -0.5) for u in range(ATT_UNROLL)]
            dk = [_dot_tn(ds[u], qb[u]) for u in range(ATT_UNROLL)]
            dv = [_dot_tn(p[u], dob[u]) for u in range(ATT_UNROLL)]
            for u, r in enumerate(rows):
                dq_acc[r, :] = dq[u]
                dk_acc[r, :] = dk[u]
                dv_acc[r, :] = dv[u]
            return carry

        def step_tile(i, carry):
            qrows, krows, has_prev = _att_tiles(i, dil, nb)
            us = range(ATT_UNROLL)
            qb = [q_ref[r, :] * (DIL_DH ** -0.5) for r in qrows]
            kk, vv, dob = [k_ref[r, :] for r in krows], [v_ref[r, :] for r in krows], [do_ref[r, :] for r in qrows]
            sc = [_att_scores_tile(qb[u], kk[u], slope_d, has_prev[u]) for u in us]
            p = [jnp.exp(sc[u] - lse_ref[qrows[u], :][:, 0:1]) for u in us]
            ds = [p[u] * (_dot_nt(dob[u], vv[u]) - dl_ref[qrows[u], :][:, 0:1]) for u in us]
            dq = [_dot(ds[u], kk[u]) * (DIL_DH ** -0.5) for u in us]
            dk = [_dot_tn(ds[u], qb[u]) for u in us]
            dv = [_dot_tn(p[u], dob[u]) for u in us]
            for u in us:
                dq_acc[qrows[u], :] = dq[u]
                dk_acc[krows[u], :] += dk[u]
                dv_acc[krows[u], :] += dv[u]
            return carry

        if nb == 2:
            lax.fori_loop(0, dil // ATT_UNROLL, step_whole, 0)
        elif nb % 2 == 0:
            lax.fori_loop(0, dil * nb // 2 // ATT_UNROLL, step_tile, 0)
        else:
            lax.fori_loop(0, dil * nb // ATT_UNROLL, step, 0)
        dq_ref[...] = dq_acc[...].astype(BF16)
        dk_ref[...] = dk_acc[...].astype(BF16)
        dv_ref[...] = dv_acc[...].astype(BF16)

    def col(off):
        return pl.BlockSpec((s, DIL_DH), lambda h: (0, off // DIL_DH + group * DIL_HEADS + h))

    hd = pl.BlockSpec((s, DIL_DH), lambda h: (0, h))
    return pl.pallas_call(
        body, name=f"att_bwd{group}", grid=(DIL_HEADS,),
        in_specs=[col(OFF_Q_B), col(OFF_K_B), col(OFF_V_B), hd, hd, hd], out_specs=[hd, hd, hd],
        out_shape=[jax.ShapeDtypeStruct((s, DIL_W), BF16)] * 3,
        scratch_shapes=[pltpu.VMEM((s, DIL_DH), F32)] * 3,
        compiler_params=_cparams("parallel"))(proj, proj, proj, do, lse, delta)


def _att_merge(parts, proj):
    s = proj.shape[0]

    def body(o0, l0, o1, l1, o2, l2, z_ref, ob_ref, o_ref, lse_ref, obt_ref):
        m = jnp.maximum(jnp.maximum(l0[...], l1[...]), l2[...])
        num = jnp.zeros_like(m)
        den = jnp.zeros_like(m)
        for og, lg in ((o0, l0), (o1, l1), (o2, l2)):
            sc = jnp.exp(lg[...] - m)
            num = num + og[...] * sc
            den = den + sc
        o = num / den
        o_ref[...] = o
        lse_ref[...] = m + jnp.log(den)
        ob = o * _silu(z_ref[...])
        ob_ref[...] = ob.astype(BF16)
        obt_ref[...] = ob.T.astype(BF16)

    row = pl.BlockSpec((ROW_TILE, DIL_W), lambda i: (i, 0))
    flat = [a for p in parts for a in p]
    return pl.pallas_call(
        body, name="att_merge", grid=(s // ROW_TILE,),
        in_specs=[row] * 6 + [pl.BlockSpec((ROW_TILE, DIL_W), lambda i: (i, OFF_Z_B // DIL_W))],
        out_specs=[row, row, row, pl.BlockSpec((DIL_W, ROW_TILE), lambda i: (0, i))],
        out_shape=[jax.ShapeDtypeStruct((s, DIL_W), BF16), jax.ShapeDtypeStruct((s, DIL_W), F32),
                   jax.ShapeDtypeStruct((s, DIL_W), F32), jax.ShapeDtypeStruct((DIL_W, s), BF16)],
        compiler_params=_cparams("parallel"))(*flat, proj)


def _att_merge_bwd(o, proj, dob, dproj):
    s = o.shape[0]

    def body(o_ref, z_ref, d_ref, dproj_in, do_ref, dl_ref, dz_ref):
        ov, zv, dv = o_ref[...], z_ref[...], d_ref[...]
        do = dv * _silu(zv)
        do_ref[...] = do
        dz_ref[...] = (dv * ov * _silu_grad(zv)).astype(BF16)
        for h in range(DIL_HEADS):
            sl = slice(h * DIL_DH, (h + 1) * DIL_DH)
            dl_ref[:, sl] = jnp.broadcast_to(jnp.sum(do[:, sl] * ov[:, sl], axis=-1, keepdims=True), (ROW_TILE, DIL_DH))

    row = pl.BlockSpec((ROW_TILE, DIL_W), lambda i: (i, 0))
    return pl.pallas_call(
        body, name="att_merge_bwd", grid=(s // ROW_TILE,),
        in_specs=[row, pl.BlockSpec((ROW_TILE, DIL_W), lambda i: (i, OFF_Z_B // DIL_W)), row, DPROJ_IN],
        out_specs=[row, row, pl.BlockSpec((ROW_TILE, DIL_W), lambda i: (i, OFF_Z_B // DIL_W))],
        out_shape=[jax.ShapeDtypeStruct((s, DIL_W), F32), jax.ShapeDtypeStruct((s, DIL_W), F32),
                   jax.ShapeDtypeStruct((s, PW), BF16)],
        input_output_aliases={3: 2},
        compiler_params=_cparams("parallel"))(o, proj, dob, dproj)


def _merge(proj, ya, yb):
    s = proj.shape[0]

    def body(ga_ref, gb_ref, ya_ref, yb_ref, o_ref, ot_ref):
        m = _sigmoid(ga_ref[...]) * ya_ref[...] + _sigmoid(gb_ref[...]) * yb_ref[...]
        o_ref[...] = m.astype(BF16)
        ot_ref[...] = m.T.astype(BF16)

    row = pl.BlockSpec((ROW_TILE, D_MODEL), lambda i: (i, 0))
    return pl.pallas_call(
        body, name="merge", grid=(s // ROW_TILE,),
        in_specs=[pl.BlockSpec((ROW_TILE, D_MODEL), lambda i: (i, OFF_G_A // D_MODEL)),
                  pl.BlockSpec((ROW_TILE, D_MODEL), lambda i: (i, OFF_G_B // D_MODEL)), row, row],
        out_specs=[row, pl.BlockSpec((D_MODEL, ROW_TILE), lambda i: (0, i))],
        out_shape=[jax.ShapeDtypeStruct((s, D_MODEL), BF16), jax.ShapeDtypeStruct((D_MODEL, s), BF16)],
        compiler_params=_cparams("parallel"))(proj, proj, ya, yb)


def _merge_bwd(proj, ya, yb, dm):
    s = proj.shape[0]

    def body(ga_ref, gb_ref, ya_ref, yb_ref, dm_ref, dya_ref, dyb_ref, dga_ref, dgb_ref):
        dmv = dm_ref[...]
        sa, sb = _sigmoid(ga_ref[...]), _sigmoid(gb_ref[...])
        dya_ref[...] = (dmv * sa).astype(BF16)
        dyb_ref[...] = (dmv * sb).astype(BF16)
        dga_ref[...] = (dmv * ya_ref[...] * sa * (1.0 - sa)).astype(BF16)
        dgb_ref[...] = (dmv * yb_ref[...] * sb * (1.0 - sb)).astype(BF16)

    row = pl.BlockSpec((ROW_TILE, D_MODEL), lambda i: (i, 0))
    return pl.pallas_call(
        body, name="merge_bwd", grid=(s // ROW_TILE,),
        in_specs=[pl.BlockSpec((ROW_TILE, D_MODEL), lambda i: (i, OFF_G_A // D_MODEL)),
                  pl.BlockSpec((ROW_TILE, D_MODEL), lambda i: (i, OFF_G_B // D_MODEL)), row, row, row],
        out_specs=[row] * 4, out_shape=[jax.ShapeDtypeStruct((s, D_MODEL), BF16)] * 4,
        compiler_params=_cparams("parallel"))(proj, proj, ya, yb, dm)


def _final(x, mg, w_out, fw, tgt):
    s, d = x.shape

    def body(x_ref, mg_ref, wo_ref, w_ref, y_ref, dx_ref, dw_ref, l_ref):
        i = pl.program_id(0)
        x2 = x_ref[...] + _dot(mg_ref[...], wo_ref[...])
        wv = w_ref[...]
        r = lax.rsqrt(jnp.mean(x2 * x2, axis=-1, keepdims=True) + NORM_EPS)
        e = x2 * r * wv - y_ref[...]
        lrow = jnp.mean(e * e, axis=-1, keepdims=True)
        lpart = jnp.broadcast_to(0.5 * jnp.sum(lrow, axis=0, keepdims=True), (1, 128))
        dy = e * (1.0 / d)
        dwp = jnp.sum(dy * x2 * r, axis=0, keepdims=True)
        dyw = dy * wv
        dx_ref[...] = r * dyw - x2 * (r * r * r) * jnp.mean(dyw * x2, axis=-1, keepdims=True)

        @pl.when(i == 0)
        def _():
            dw_ref[...] = dwp
            l_ref[...] = lpart

        @pl.when(i > 0)
        def _():
            dw_ref[...] += dwp
            l_ref[...] += lpart

    row = pl.BlockSpec((BIG_TILE, d), lambda i: (i, 0))
    vec = pl.BlockSpec((1, d), lambda i: (0, 0))
    return pl.pallas_call(
        body, name="final", grid=(s // BIG_TILE,),
        in_specs=[row, row, pl.BlockSpec(w_out.shape, lambda i: (0, 0)), vec, row],
        out_specs=[row, vec, pl.BlockSpec((1, 128), lambda i: (0, 0))],
        out_shape=[jax.ShapeDtypeStruct((s, d), F32), jax.ShapeDtypeStruct((1, d), F32), jax.ShapeDtypeStruct((1, 128), F32)],
        compiler_params=_cparams("arbitrary"))(x, mg, w_out, fw, tgt)


def _adamw(w, g, m, v, name):
    r, c = w.shape
    cap = max(8, (1 << 18) // c)
    divisors = [t for t in range(8, min(r, cap) + 1, 8) if r % t == 0]
    tr = r if r <= 8 else (max(divisors) if divisors else cap)

    def body(w_ref, g_ref, m_ref, v_ref, d_ref, nm_ref, nv_ref):
        gv = g_ref[...]
        mn = ADAM_B1 * m_ref[...] + (1.0 - ADAM_B1) * gv
        vn = ADAM_B2 * v_ref[...] + (1.0 - ADAM_B2) * (gv * gv)
        m_hat = mn / (1.0 - ADAM_B1 ** ADAM_STEP)
        v_hat = vn / (1.0 - ADAM_B2 ** ADAM_STEP)
        d_ref[...] = -ADAM_LR * (m_hat / (jnp.sqrt(v_hat) + ADAM_EPS) + ADAM_WD * w_ref[...])
        nm_ref[...] = mn
        nv_ref[...] = vn

    blk = pl.BlockSpec((tr, c), lambda i: (i, 0))
    return pl.pallas_call(
        body, name=name, grid=(pl.cdiv(r, tr),), in_specs=[blk] * 4, out_specs=[blk] * 3,
        out_shape=[jax.ShapeDtypeStruct((r, c), F32)] * 3, compiler_params=_cparams("parallel"))(w, g, m, v)


HBM_SPEC = pl.BlockSpec(memory_space=pl.ANY)


def _place():
    x, y, c = lax.axis_index("x"), lax.axis_index("y"), lax.axis_index("c")
    chips = [(1 - x, y), (x, 1 - y), (1 - x, 1 - y)]
    return x, y, c, chips


def _ag_weights(packs):
    na = len(packs)
    nsem = 8

    def body(*refs):
        p_refs, out_refs = refs[:na], refs[na:2 * na]
        send_sems, recv_sems = refs[2 * na:]
        x, y, c, _ = _place()
        me, sib, j = (x, y, c), (x, y, 1 - c), 2 * x + y
        xn, yn = (1 - x, y, c), (x, 1 - y, c)
        jx, jy, jd = 2 * (1 - x) + y, 2 * x + (1 - y), 2 * (1 - x) + (1 - y)

        def rc(a, k, src, dst, to):
            return pltpu.make_async_remote_copy(src_ref=src, dst_ref=dst, send_sem=send_sems.at[nsem * a + k],
                                                recv_sem=recv_sems.at[nsem * a + k], device_id=to, device_id_type=MESH)

        sent = []
        for a in range(na):
            mine, land = p_refs[a].at[c], out_refs[a].at[j, c]
            sent += [rc(a, 0, mine, land, xn), rc(a, 1, mine, land, yn), rc(a, 7, p_refs[a], out_refs[a].at[j], sib)]
        for cp in sent:
            cp.start()
        for a in range(na):
            half = p_refs[a].shape[1] // 2
            top, bottom = pl.ds(0, half), pl.ds(half, half)
            from_x, from_y, from_d = out_refs[a].at[jx, c], out_refs[a].at[jy, c], out_refs[a].at[jd, c]
            rc(a, 0, p_refs[a].at[c], from_x, me).wait_recv()
            later = [rc(a, 2, from_x.at[top], from_x.at[top], yn), rc(a, 4, from_x, from_x, sib)]
            for cp in later:
                cp.start()
            sent += later
            rc(a, 1, p_refs[a].at[c], from_y, me).wait_recv()
            later = [rc(a, 3, from_y.at[bottom], from_y.at[bottom], xn), rc(a, 5, from_y, from_y, sib)]
            for cp in later:
                cp.start()
            sent += later
            rc(a, 2, from_d.at[top], from_d.at[top], me).wait_recv()
            rc(a, 3, from_d.at[bottom], from_d.at[bottom], me).wait_recv()
            cp = rc(a, 6, from_d, from_d, sib)
            cp.start()
            sent.append(cp)
        for a in range(na):
            for k, jj in ((4, jx), (5, jy), (6, jd)):
                rc(a, k, p_refs[a].at[c], out_refs[a].at[jj, 1 - c], me).wait_recv()
            rc(a, 7, p_refs[a], out_refs[a].at[j], me).wait_recv()
        for cp in sent:
            cp.wait_send()

    return pl.pallas_call(
        body, name="ag_weights",
        out_shape=[jax.ShapeDtypeStruct((N_CHIPS,) + p.shape, p.dtype) for p in packs],
        in_specs=[HBM_SPEC] * na, out_specs=[HBM_SPEC] * na,
        scratch_shapes=[pltpu.SemaphoreType.DMA((nsem * na,)), pltpu.SemaphoreType.DMA((nsem * na,))])(*packs)


def _rs_pair(dwpt, gpack):
    n = N_CHIPS
    hw = SHARD_PAD // 2

    def body(d_ref, g_ref, out_d, out_g, send_sems, recv_sems):
        x, y, c, _ = _place()
        sib = (x, y, 1 - c)
        cps = []
        for p in range(n):
            start = pl.multiple_of(WIN_BASE[p] + (1 - c) * hw, TILE_ROWS)
            cps.append(pltpu.make_async_remote_copy(
                src_ref=d_ref.at[pl.ds(start, hw)], dst_ref=out_d.at[p], send_sem=send_sems.at[p],
                recv_sem=recv_sems.at[p], device_id=sib, device_id_type=MESH))
            cps.append(pltpu.make_async_remote_copy(
                src_ref=g_ref.at[p, 1 - c], dst_ref=out_g.at[p], send_sem=send_sems.at[n + p],
                recv_sem=recv_sems.at[n + p], device_id=sib, device_id_type=MESH))
        for cp in cps:
            cp.start()
        for cp in cps:
            cp.wait_recv()
        for cp in cps:
            cp.wait_send()

    return pl.pallas_call(
        body, name="rs_pair",
        out_shape=[jax.ShapeDtypeStruct((n, hw, dwpt.shape[1]), dwpt.dtype),
                   jax.ShapeDtypeStruct((n,) + gpack.shape[2:], gpack.dtype)],
        in_specs=[HBM_SPEC] * 2, out_specs=[HBM_SPEC] * 2,
        scratch_shapes=[pltpu.SemaphoreType.DMA((2 * n,)), pltpu.SemaphoreType.DMA((2 * n,))])(dwpt, gpack)


def _add_halves_win(dwpt, other, c):
    n, rh, wd = other.shape
    tr = _row_tile(rh)

    def body(s_ref, d_ref, o_ref, out_ref):
        out_ref[0] = (d_ref[...] + o_ref[0]).astype(BF16)

    scal = jnp.concatenate([jnp.reshape(c, (1,)).astype(jnp.int32), jnp.asarray(WIN_BASE, jnp.int32)])
    grid_spec = pltpu.PrefetchScalarGridSpec(
        num_scalar_prefetch=1, grid=(n, rh // tr),
        in_specs=[pl.BlockSpec((pl.Element(tr), pl.Element(wd)),
                               lambda p, i, sr: (pl.multiple_of(sr[1 + p] + sr[0] * rh + i * tr, TILE_ROWS), 0)),
                  pl.BlockSpec((1, tr, wd), lambda p, i, sr: (p, i, 0))],
        out_specs=pl.BlockSpec((1, tr, wd), lambda p, i, sr: (p, i, 0)))
    return pl.pallas_call(
        body, name="add_halves_in", grid_spec=grid_spec, out_shape=jax.ShapeDtypeStruct((n, rh, wd), BF16),
        compiler_params=_cparams("parallel", "parallel"))(scal, dwpt, other)


SEM_SPEC = pl.BlockSpec(memory_space=pltpu.SEMAPHORE)
DATAFLOW_EFFECT = pltpu.SideEffectType.DATAFLOW_SIDE_EFFECTING


def _rs_chips_start(csums):
    na = len(csums)

    def body(*refs):
        s_refs, land_refs = refs[:na], refs[na:2 * na]
        send_sems, recv_sems = refs[2 * na], refs[2 * na + 1]
        token = refs[-1]
        x, y, c, chips = _place()
        j = 2 * x + y
        for a in range(na):
            for k, (cx, cy) in enumerate(chips):
                pltpu.make_async_remote_copy(src_ref=s_refs[a].at[2 * cx + cy], dst_ref=land_refs[a].at[j],
                                             send_sem=send_sems.at[3 * a + k], recv_sem=recv_sems.at[3 * a + k],
                                             device_id=(cx, cy, c), device_id_type=MESH).start()
        token[...] = jnp.zeros_like(token)

    hbm = [pltpu.HBM(s.shape, s.dtype) for s in csums]
    args = [pltpu.with_memory_space_constraint(s, pltpu.HBM) for s in csums]
    args += [pltpu.with_memory_space_constraint(lax.empty(s.shape, s.dtype), pltpu.HBM) for s in csums]
    res = pl.pallas_call(
        body, name="rs_chips_start",
        out_shape=(pltpu.SemaphoreType.DMA((3 * na,)), pltpu.SemaphoreType.DMA((3 * na,)), *hbm, *hbm,
                   jax.ShapeDtypeStruct((8, 128), F32)),
        in_specs=[pl.BlockSpec(memory_space=pltpu.HBM)] * (2 * na),
        out_specs=(SEM_SPEC, SEM_SPEC, *[pl.BlockSpec(memory_space=pltpu.HBM)] * (2 * na),
                   pl.BlockSpec(memory_space=pltpu.VMEM)),
        input_output_aliases={i: 2 + i for i in range(2 * na)},
        compiler_params=pltpu.CompilerParams(has_side_effects=DATAFLOW_EFFECT))(*args)
    return res[0], res[1], list(res[2:2 + na]), list(res[2 + na:2 + 2 * na]), res[-1]


def _rs_chips_wait(send_sems, recv_sems, csums, lands, after):
    na = len(csums)

    def body(*refs):
        s_refs, land_refs = refs[:na], refs[na:2 * na]
        send_sems, recv_sems = refs[2 * na], refs[2 * na + 1]
        x, y, c, chips = _place()
        j = 2 * x + y
        for a in range(na):
            for k, (cx, cy) in enumerate(chips):
                cp = pltpu.make_async_remote_copy(src_ref=s_refs[a].at[2 * cx + cy], dst_ref=land_refs[a].at[2 * cx + cy],
                                                  send_sem=send_sems.at[3 * a + k], recv_sem=recv_sems.at[3 * a + k],
                                                  device_id=(cx, cy, c), device_id_type=MESH)
                cp.wait_send()
                cp.wait_recv()

    hbm = [pltpu.HBM(s.shape, s.dtype) for s in csums]
    res = pl.pallas_call(
        body, name="rs_chips_wait", out_shape=(*hbm, *hbm),
        in_specs=[pl.BlockSpec(memory_space=pltpu.HBM)] * (2 * na) + [SEM_SPEC, SEM_SPEC, pl.BlockSpec(memory_space=pl.ANY)],
        out_specs=tuple([pl.BlockSpec(memory_space=pltpu.HBM)] * (2 * na)),
        input_output_aliases={i: i for i in range(2 * na)},
        compiler_params=pltpu.CompilerParams(has_side_effects=DATAFLOW_EFFECT))(*csums, *lands, send_sems, recv_sems, after)
    return list(res[:na]), list(res[na:])


SWAP_CHUNKS = 4


def _pair_swap(halves):
    na = len(halves)

    def body(*refs):
        h_refs, out_refs = refs[:na], refs[na:2 * na]
        send_sems, recv_sems = refs[2 * na:]
        x, y, c, _ = _place()
        cps = []
        for a in range(na):
            rows = h_refs[a].shape[0] // SWAP_CHUNKS
            assert rows * SWAP_CHUNKS == h_refs[a].shape[0]
            for q in range(SWAP_CHUNKS):
                k = SWAP_CHUNKS * a + q
                cps.append(pltpu.make_async_remote_copy(
                    src_ref=h_refs[a].at[pl.ds(q * rows, rows)], dst_ref=out_refs[a].at[pl.ds(q * rows, rows)],
                    send_sem=send_sems.at[k], recv_sem=recv_sems.at[k], device_id=(x, y, 1 - c), device_id_type=MESH))
        for cp in cps:
            cp.start()
        for cp in cps:
            cp.wait_recv()
        for cp in cps:
            cp.wait_send()

    return pl.pallas_call(
        body, name="pair_swap", out_shape=[jax.ShapeDtypeStruct(h.shape, h.dtype) for h in halves],
        in_specs=[HBM_SPEC] * na, out_specs=[HBM_SPEC] * na,
        scratch_shapes=[pltpu.SemaphoreType.DMA((SWAP_CHUNKS * na,)), pltpu.SemaphoreType.DMA((SWAP_CHUNKS * na,))])(*halves)


def _ag_small(v):
    m_per, n = v.shape

    def body(x_ref, out_ref, send_sems, recv_sems, local_sem):
        x, y, c, chips = _place()
        me, sibling = (x, y, c), (x, y, 1 - c)

        def rows(px, py, pc):
            return out_ref.at[pl.ds((4 * px + 2 * py + pc) * m_per, m_per), :]

        def copy(k, block, to, src=None):
            return pltpu.make_async_remote_copy(
                src_ref=rows(*block) if src is None else src, dst_ref=rows(*block), send_sem=send_sems.at[k],
                recv_sem=recv_sems.at[k], device_id=to, device_id_type=MESH)

        mine = pltpu.make_async_copy(x_ref, rows(*me), local_sem)
        mine.start()
        first = [copy(0, me, sibling, src=x_ref)]
        first += [copy(1 + k, me, (*chip, c), src=x_ref) for k, chip in enumerate(chips)]
        for cp in first:
            cp.start()
        passed = [copy(4 + k, (*chip, c), sibling) for k, chip in enumerate(chips)]
        for k, chip in enumerate(chips):
            copy(1 + k, (*chip, c), me).wait_recv()
            passed[k].start()
        copy(0, sibling, me).wait_recv()
        for k, chip in enumerate(chips):
            copy(4 + k, (*chip, 1 - c), me).wait_recv()
        for cp in first + passed:
            cp.wait_send()
        mine.wait()

    return pl.pallas_call(
        body, name="ag_small", out_shape=jax.ShapeDtypeStruct((8 * m_per, n), v.dtype),
        in_specs=[pl.BlockSpec(memory_space=pltpu.VMEM)], out_specs=pl.BlockSpec(memory_space=pltpu.VMEM),
        scratch_shapes=[pltpu.SemaphoreType.DMA((7,)), pltpu.SemaphoreType.DMA((7,)), pltpu.SemaphoreType.DMA])(v)


def _sum_blocks(a, nblk, name):
    rows, wd = a.shape
    r = rows // nblk
    tr = min(r, ROW_TILE)
    assert r % tr == 0

    def body(*refs):
        acc = refs[0][...].astype(F32)
        for ref in refs[1:nblk]:
            acc = acc + ref[...].astype(F32)
        refs[nblk][...] = acc

    nt = r // tr
    return pl.pallas_call(
        body, name=name, grid=(nt,),
        in_specs=[pl.BlockSpec((tr, wd), functools.partial(lambda i, b: (b * nt + i, 0), b=b)) for b in range(nblk)],
        out_specs=pl.BlockSpec((tr, wd), lambda i: (i, 0)),
        out_shape=jax.ShapeDtypeStruct((r, wd), F32), compiler_params=_cparams("parallel"))(*([a] * nblk))


def _row_tile(rows):
    best = max(t for t in range(16, 513, 16) if rows % t == 0)
    return best


def _sum_chips(by_src, csum, j, name):
    n, rh, wd = by_src.shape
    tr = _row_tile(rh)

    def body(j_ref, *refs):
        own = refs[n][0].astype(F32)
        acc = None
        for k in range(n):
            term = jnp.where(j_ref[0] == k, own, refs[k][0].astype(F32))
            acc = term if acc is None else acc + term
        refs[n + 1][...] = acc

    def other(k):
        return pl.BlockSpec((1, tr, wd), lambda i, jr: (jnp.where(jr[0] == k, (k + 1) % n, k), i, 0))

    grid_spec = pltpu.PrefetchScalarGridSpec(
        num_scalar_prefetch=1, grid=(rh // tr,),
        in_specs=[other(k) for k in range(n)] + [pl.BlockSpec((1, tr, wd), lambda i, jr: (jr[0], i, 0))],
        out_specs=pl.BlockSpec((tr, wd), lambda i, jr: (i, 0)))
    return pl.pallas_call(
        body, name=name, grid_spec=grid_spec, out_shape=jax.ShapeDtypeStruct((rh, wd), F32),
        compiler_params=_cparams("parallel"))(jnp.reshape(j, (1,)).astype(jnp.int32), *([by_src] * n), csum)


def _add_halves(gpack, other, c, name):
    n, _, rh, wd = gpack.shape
    tr = _row_tile(rh)

    def body(c_ref, g_ref, o_ref, out_ref):
        out_ref[0] = (g_ref[0, 0] + o_ref[0]).astype(BF16)

    grid_spec = pltpu.PrefetchScalarGridSpec(
        num_scalar_prefetch=1, grid=(n, rh // tr),
        in_specs=[pl.BlockSpec((1, 1, tr, wd), lambda p, i, cr: (p, cr[0], i, 0)),
                  pl.BlockSpec((1, tr, wd), lambda p, i, cr: (p, i, 0))],
        out_specs=pl.BlockSpec((1, tr, wd), lambda p, i, cr: (p, i, 0)))
    return pl.pallas_call(
        body, name=name, grid_spec=grid_spec, out_shape=jax.ShapeDtypeStruct((n, rh, wd), BF16),
        compiler_params=_cparams("parallel", "parallel"))(jnp.reshape(c, (1,)).astype(jnp.int32), gpack, other)


PACK_W = 1024
ROWS_O_DN = DN_W // N_CHIPS
ROWS_O_DIL = DIL_W * (D_MODEL // N_CHIPS) // PACK_W
ROWS_OUT = D_MODEL // N_CHIPS
ROWS_CONV = 4 * (3 * DN_W // N_CHIPS) // PACK_W
R1 = ROWS_O_DN
R2 = R1 + ROWS_O_DIL
R3 = R2 + ROWS_OUT
R4 = R3 + 16
R5 = R4 + 16
PACK_ROWS = 704
HALF_ROWS = PACK_ROWS // 2
SHARD_PAD = 2880


R6 = R5 + 2 * DN_HEADS

TILE_ROWS = 16
BA_IN_SHARD1 = REF_OFF_BA - SHARD_W
LOCAL_START = (0, SHARD_W, 2 * SHARD_W - 2 * DN_HEADS, 3 * SHARD_W - 2 * DN_HEADS)
LOCAL_END = LOCAL_START[1:] + (OFF_BA,)
WIN_BASE = tuple(s // TILE_ROWS * TILE_ROWS for s in LOCAL_START)


def _to_window(k, shard):
    nba = 2 * DN_HEADS
    body = shard
    if k == 1:
        row = lax.broadcasted_iota(jnp.int32, (SHARD_W - nba, 1), 0)
        body = jnp.where(row < BA_IN_SHARD1, shard[:SHARD_W - nba], shard[nba:])
    lead = LOCAL_START[k] - WIN_BASE[k]
    return jnp.pad(body, ((lead, SHARD_PAD - lead - body.shape[0]), (0, 0)))


def _from_window(k, win, ba):
    nba = 2 * DN_HEADS
    lead = LOCAL_START[k] - WIN_BASE[k]
    if k != 1:
        return win[lead:lead + SHARD_W]
    row = lax.broadcasted_iota(jnp.int32, (SHARD_W, 1), 0)
    before = win[lead:lead + SHARD_W]
    after = jnp.pad(win, ((nba, 0), (0, 0)))[lead:lead + SHARD_W]
    mid = jnp.pad(ba, ((BA_IN_SHARD1, SHARD_W - BA_IN_SHARD1 - nba), (0, 0)))
    return jnp.where(row < BA_IN_SHARD1, before, jnp.where(row < BA_IN_SHARD1 + nba, mid, after))


def _stack_windows(wins, ba):
    pieces = []
    for k in range(N_CHIPS):
        lo = WIN_BASE[k] + (TILE_ROWS if k else 0)
        hi = LOCAL_END[k] // TILE_ROWS * TILE_ROWS
        pieces.append(wins[k][lo - WIN_BASE[k]:hi - WIN_BASE[k]])
        if k + 1 < N_CHIPS:
            assert hi == WIN_BASE[k + 1]
            pieces.append(wins[k][hi - WIN_BASE[k]:hi - WIN_BASE[k] + TILE_ROWS] + wins[k + 1][:TILE_ROWS])
    pieces += [ba, jnp.zeros((PW - OFF_BA - ba.shape[0], ba.shape[1]), ba.dtype)]
    out = jnp.concatenate(pieces, axis=0)
    assert out.shape[0] == PW
    return out


def _local_step(x, tgt, norm_w, wpt, conv_full, a_log, dt_bias, dn_norm_w, w_o_dn, w_o_dil, w_out, final_norm_w):
    s = x.shape[0]
    h, h_t = _rms_in(x, norm_w)
    proj = _matmul(h, wpt, F32, 2048, 1280, 1024, "proj", nt=True)
    c_pre, qkv = _conv_fwd(proj, conv_full)
    gate_par = jnp.zeros((8, 128), F32).at[0, 8:16].set(a_log[0]).at[1, 8:16].set(dt_bias[0])
    bg = _gates_fwd(proj, gate_par)
    o_a, u, w, vn, tmat, states = _gdr_fwd(qkv, bg)
    oa2, oa2_t = _gdr_out(o_a, proj, dn_norm_w)
    ya = _matmul(oa2, w_o_dn, F32, 1024, 1024, 1024, "ya")
    parts = [_att_fwd(proj, g) for g in range(N_DIL)]
    ob, o_att, lse, ob_t = _att_merge(parts, proj)
    yb = _matmul(ob, w_o_dil, F32, 1024, 1024, 512, "yb")
    mg, mg_t = _merge(proj, ya, yb)
    dx2, dfw, lpart = _final(x, mg, w_out, final_norm_w, tgt)

    dmg = _matmul(dx2, w_out, F32, 1024, 1024, 1024, "d_merged", nt=True)
    dw_out = _matmul(mg_t, dx2, F32, 1024, 1024, 1024, "dw_out")
    dya, dyb, dga, dgb = _merge_bwd(proj, ya, yb, dmg)
    doa2 = _matmul(dya, w_o_dn, F32, 1024, 1024, 1024, "d_oa2", nt=True)
    dw_o_dn = _matmul(oa2_t, dya, F32, 1024, 1024, 1024, "dw_o_dn")
    dob = _matmul(dyb, w_o_dil, F32, 1024, 512, 1024, "d_ob", nt=True)
    dw_o_dil = _matmul(ob_t, dyb, F32, 512, 1024, 1024, "dw_o_dil")
    do_a, dproj, ddnw = _gdr_out_bwd(o_a, proj, dn_norm_w, doa2)
    dqkv_a, dbg = _gdr_bwd(qkv, bg, u, w, vn, tmat, states, do_a)
    dproj, dpar = _gates_bwd(proj, gate_par, dbg, dproj)
    dproj, dconv = _conv_bwd(proj, c_pre, dqkv_a, conv_full, dproj)
    do_att, delta, dproj = _att_merge_bwd(o_att, proj, dob, dproj)
    dqkv_b = [_att_bwd(proj, g, do_att, lse, delta) for g in range(N_DIL)]
    pieces = [(OFF_Q_B + (N_DIL * i + g) * DIL_W, dqkv_b[g][i]) for i in range(3) for g in range(N_DIL)]
    for off, piece in pieces + [(OFF_G_A, dga), (OFF_G_B, dgb)]:
        dproj = lax.dynamic_update_slice(dproj, piece, (0, off))
    dwpt, dwpt_b = _matmul(h_t, dproj, F32, 1024, 1280, 2048, "dw_in", transpose_out=True, also_bf16=True)

    def finish(after=None):
        dh = _matmul(dproj, wpt, F32, 1024, 1024, 3840, "d_h", after=after)
        grad_x, dnw = _rms_in_bwd(x, norm_w, dh, dx2)
        small = jnp.zeros((8, PACK_W), F32)
        small = small.at[0].set(dnw[0]).at[1].set(dfw[0]).at[2, :DN_D].set(ddnw[0])
        small = small.at[3, :DN_HEADS].set(dpar[0, 8:16]).at[3, DN_HEADS:2 * DN_HEADS].set(dpar[1, 8:16])
        small = small.at[4, 0].set(lpart[0, 0])
        return grad_x, small

    return finish, (dwpt, dwpt_b), dconv, dw_o_dn, dw_o_dil, dw_out


def kernel(x, norm_w, w_in, conv_w, a_log, dt_bias, dn_norm_w, w_o_dn, w_o_dil, w_out, final_norm_w, loss_target, m_norm_w, m_w_in, m_conv_w, m_a_log, m_dt_bias, m_dn_norm_w, m_w_o_dn, m_w_o_dil, m_w_out, m_final_norm_w, v_norm_w, v_w_in, v_conv_w, v_a_log, v_dt_bias, v_dn_norm_w, v_w_o_dn, v_w_o_dil, v_w_out, v_final_norm_w):
    c = lax.axis_index("c")
    j = 2 * lax.axis_index("x") + lax.axis_index("y")
    qw = D_MODEL // N_CHIPS

    cw = conv_w[0].reshape(ROWS_CONV, PACK_W)
    cw = jnp.pad(cw, ((0, 16 - ROWS_CONV), (0, 0)))
    cw_hi = cw.astype(BF16)
    cw_lo = (cw - cw_hi.astype(F32)).astype(BF16)
    shard = w_in[0].T.astype(BF16)
    own_ba = jnp.where(j == 1, shard[BA_IN_SHARD1:BA_IN_SHARD1 + 2 * DN_HEADS], jnp.zeros((2 * DN_HEADS, D_MODEL), BF16))
    pack = jnp.concatenate(
        [w_o_dn[0].astype(BF16), w_o_dil[0].astype(BF16).reshape(ROWS_O_DIL, PACK_W), w_out[0].astype(BF16), cw_hi, cw_lo,
         own_ba, jnp.zeros((PACK_ROWS - R6, PACK_W), BF16)], axis=0).reshape(2, HALF_ROWS, PACK_W)
    chips = range(N_CHIPS)
    own_win = lax.switch(j, [functools.partial(_to_window, k) for k in chips], shard).reshape(2, SHARD_PAD // 2, D_MODEL)
    all_in, allw = _ag_weights([own_win, pack])
    wins = [all_in[k].reshape(SHARD_PAD, D_MODEL) for k in chips]
    allw = [allw[k].reshape(PACK_ROWS, PACK_W) for k in chips]
    wpt = _stack_windows(wins, allw[1][R5:R6])
    w_o_dn_full = jnp.concatenate([allw[k][:R1] for k in chips], axis=0)
    w_o_dil_full = jnp.concatenate([allw[k][R1:R2].reshape(DIL_W, qw) for k in chips], axis=1)
    w_out_full = jnp.concatenate([allw[k][R2:R3] for k in chips], axis=0)
    conv_full = jnp.concatenate(
        [(allw[k][R3:R3 + ROWS_CONV].astype(F32) + allw[k][R4:R4 + ROWS_CONV].astype(F32)).reshape(4, 3 * DN_W // N_CHIPS)
         for k in chips], axis=1)

    finish, (dwpt, dwpt_b), dconv, dw_o_dn, dw_o_dil, dw_out = _local_step(
        x[0], loss_target[0], norm_w, wpt, conv_full, a_log, dt_bias, dn_norm_w, w_o_dn_full, w_o_dil_full, w_out_full,
        final_norm_w.reshape(1, D_MODEL))

    cq = 3 * DN_W // N_CHIPS
    gpack = jnp.stack([
        jnp.concatenate(
            [dw_o_dn[k * qw:(k + 1) * qw], dw_o_dil[:, k * qw:(k + 1) * qw].reshape(ROWS_O_DIL, PACK_W),
             dw_out[k * qw:(k + 1) * qw],
             jnp.pad(dconv[:, k * cq:(k + 1) * cq].reshape(ROWS_CONV, PACK_W), ((0, 16 - ROWS_CONV), (0, 0))),
             dwpt[OFF_BA:OFF_BA + 2 * DN_HEADS] if k == 1 else jnp.zeros((2 * DN_HEADS, PACK_W), F32),
             jnp.zeros((PACK_ROWS - R4 - 2 * DN_HEADS, PACK_W), F32)], axis=0)
        for k in chips]).reshape(N_CHIPS, 2, HALF_ROWS, PACK_W)
    sib_in, sib_pack = _rs_pair(dwpt_b, gpack)
    csum_in = _add_halves_win(dwpt, sib_in, c)
    csum_pack = _add_halves(gpack, sib_pack, c, "add_halves_pack")
    send_sems, recv_sems, csums, lands, token = _rs_chips_start([csum_in, csum_pack])
    grad_x, small = finish(after=token)

    gs = _sum_blocks(_ag_small(small), 8, "sum_small")
    loss = gs[4, 0]
    w_small = jnp.zeros((8, PACK_W), F32)

    def pack_small(nw, fw, dnw_, al, db):
        t = w_small.at[0].set(nw[0]).at[1].set(fw).at[2, :DN_D].set(dnw_[0])
        return t.at[3, :DN_HEADS].set(al[0]).at[3, DN_HEADS:2 * DN_HEADS].set(db[0])

    sm = _adamw(pack_small(norm_w, final_norm_w, dn_norm_w, a_log, dt_bias), gs,
                pack_small(m_norm_w, m_final_norm_w, m_dn_norm_w, m_a_log, m_dt_bias),
                pack_small(v_norm_w, v_final_norm_w, v_dn_norm_w, v_a_log, v_dt_bias), "adamw_small")

    (csum_in, csum_pack), (src_in, src_pack) = _rs_chips_wait(send_sems, recv_sems, csums, lands, sm[0])
    half_in = _sum_chips(src_in, csum_in, j, "sum_chips_in")
    half_pack = _sum_chips(src_pack, csum_pack, j, "sum_chips_pack")
    sib_half_in, sib_half_pack = _pair_swap([half_in, half_pack])

    def both_halves(mine, theirs):
        return jnp.where(c == 0, jnp.concatenate([mine, theirs], axis=0), jnp.concatenate([theirs, mine], axis=0))

    g = both_halves(half_pack, sib_half_pack)
    g_w_in = lax.switch(j, [functools.partial(_from_window, k) for k in chips], both_halves(half_in, sib_half_in),
                        g[R4:R4 + 2 * DN_HEADS])
    g_w_o_dn = g[:R1]
    g_w_o_dil = g[R1:R2].reshape(DIL_W, qw)
    g_w_out = g[R2:R3]
    g_conv = g[R3:R3 + ROWS_CONV].reshape(4, cq)

    def unpack_small(t):
        return dict(norm_w=t[0:1], final_norm_w=t[1], dn_norm_w=t[2:3, :DN_D], a_log=t[3:4, :DN_HEADS],
                    dt_bias=t[3:4, DN_HEADS:2 * DN_HEADS])

    res = {"grad": unpack_small(gs)}
    for kind, arr in zip(("delta", "new_m", "new_v"), sm):
        res[kind] = unpack_small(arr)
    big = dict(conv_w=(conv_w, g_conv, m_conv_w, v_conv_w), w_o_dn=(w_o_dn, g_w_o_dn, m_w_o_dn, v_w_o_dn),
               w_o_dil=(w_o_dil, g_w_o_dil, m_w_o_dil, v_w_o_dil), w_out=(w_out, g_w_out, m_w_out, v_w_out))
    for name, (wt, gt, mt, vt) in big.items():
        d, nm, nv = _adamw(wt[0], gt, mt[0], vt[0], "adamw_" + name)
        res["grad"][name] = gt[None]
        res["delta"][name], res["new_m"][name], res["new_v"][name] = d[None], nm[None], nv[None]

    d, nm, nv = _adamw(w_in[0].T, g_w_in, m_w_in[0].T, v_w_in[0].T, "adamw_w_in")
    res["grad"]["w_in"] = g_w_in.T[None]
    res["delta"]["w_in"], res["new_m"]["w_in"], res["new_v"]["w_in"] = d.T[None], nm.T[None], nv.T[None]
    order = ["norm_w", "w_in", "conv_w", "a_log", "dt_bias", "dn_norm_w", "w_o_dn", "w_o_dil", "w_out", "final_norm_w"]
    outs = [loss, grad_x[None]]
    for kind in ("grad", "delta", "new_m", "new_v"):
        outs += [res[kind][nm] for nm in order]
    return tuple(outs)
```

```python
import functools
import math

import jax
import jax.numpy as jnp
from jax import lax
from jax.experimental import pallas as pl
from jax.experimental.pallas import tpu as pltpu

F32 = jnp.float32
BF16 = jnp.bfloat16
MESH = pl.DeviceIdType.MESH

D_MODEL = 1024
DN_HEADS = 8
DN_D = 128
DN_CHUNK = 64
DN_W = DN_HEADS * DN_D
DIL_GROUPS = ((128, 1), (512, 4), (2048, 16))
N_DIL = len(DIL_GROUPS)
DIL_HEADS = 4
DIL_DH = 128
DIL_W = DIL_HEADS * DIL_DH
ATT_BLOCK = 128
NORM_EPS = 1e-6
PROJ_W = 11280
N_CHIPS = 4
SHARD_W = PROJ_W // N_CHIPS

OFF_QKV_A = 0
OFF_Z_A = 3072
OFF_Q_B = 4096
OFF_K_B = 5632
OFF_V_B = 7168
OFF_Z_B = 8704
OFF_G_A = 9216
OFF_G_B = 10240
OFF_BA = 11264
PW = 11520
REF_OFF_BA = 4096

ADAM_LR = 0.001
ADAM_B1 = 0.9
ADAM_B2 = 0.999
ADAM_EPS = 1e-08
ADAM_WD = 0.01
ADAM_STEP = 10

ROW_TILE = 512
CONV_TILE = 1024
BIG_TILE = 1024
NEG = -1e30


def _dot(a, b):
    return jnp.dot(a.astype(BF16), b.astype(BF16), preferred_element_type=F32)


def _dot_nt(a, b):
    return lax.dot_general(a.astype(BF16), b.astype(BF16), (((1,), (1,)), ((), ())), preferred_element_type=F32)


def _dot_tn(a, b):
    return lax.dot_general(a.astype(BF16), b.astype(BF16), (((0,), (0,)), ((), ())), preferred_element_type=F32)


def _split(a):
    hi = a.astype(BF16)
    lo = (a - hi.astype(F32)).astype(BF16)
    return hi, lo


def _dot_exact_lhs(c, a):
    hi, lo = _split(a)
    cb = c.astype(BF16)
    return jnp.dot(cb, hi, preferred_element_type=F32) + jnp.dot(cb, lo, preferred_element_type=F32)


def _dot_tn_exact_rhs(a, c):
    hi, lo = _split(a)
    cb = c.astype(BF16)
    dn = (((0,), (0,)), ((), ()))
    return (lax.dot_general(hi, cb, dn, preferred_element_type=F32)
            + lax.dot_general(lo, cb, dn, preferred_element_type=F32))


def _sigmoid(x):
    return 1.0 / (1.0 + jnp.exp(-x))


def _silu(x):
    return x * _sigmoid(x)


def _silu_grad(x):
    s = _sigmoid(x)
    return s * (1.0 + x * (1.0 - s))


def _softplus(x):
    return jnp.maximum(x, 0.0) + jnp.log(1.0 + jnp.exp(-jnp.abs(x)))


def _cparams(*sem):
    return pltpu.CompilerParams(dimension_semantics=sem)


def _matmul(a, b, out_dtype, tm, tn, tk, name, nt=False, transpose_out=False, after=None, also_bf16=False):
    m, kdim = a.shape
    n = b.shape[0] if nt else b.shape[1]
    tm, tn, tk = min(tm, m), min(tn, n), min(tk, kdim)
    assert m % tm == 0 and n % tn == 0 and kdim % tk == 0, (name, a.shape, b.shape, tm, tn, tk)
    nk = kdim // tk
    dot = _dot_nt if nt else _dot
    b_spec = (pl.BlockSpec((tn, tk), lambda i, j, k: (j, k)) if nt else pl.BlockSpec((tk, tn), lambda i, j, k: (k, j)))
    extra = [] if after is None else [after]
    out_dtypes = [out_dtype] + ([BF16] if also_bf16 else [])

    def emit(o_refs, acc):
        val = acc.T if transpose_out else acc
        for o_ref in o_refs:
            o_ref[...] = val.astype(o_ref.dtype)

    def outs_of(rest):
        return rest[len(extra):len(extra) + len(out_dtypes)]

    if nk == 1:
        def body(a_ref, b_ref, *rest):
            emit(outs_of(rest), dot(a_ref[...], b_ref[...]))
        scratch = []
    else:
        def body(a_ref, b_ref, *rest):
            o_ref, acc_ref = outs_of(rest), rest[-1]
            k = pl.program_id(2)
            p = dot(a_ref[...], b_ref[...])

            @pl.when(k == 0)
            def _():
                acc_ref[...] = p

            @pl.when(k > 0)
            def _():
                acc_ref[...] += p

            @pl.when(k == nk - 1)
            def _():
                emit(o_ref, acc_ref[...])
        scratch = [pltpu.VMEM((tm, tn), F32)]

    if transpose_out:
        out_spec, out_shape = pl.BlockSpec((tn, tm), lambda i, j, k: (j, i)), (n, m)
    else:
        out_spec, out_shape = pl.BlockSpec((tm, tn), lambda i, j, k: (i, j)), (m, n)
    res = pl.pallas_call(
        body, name=name, grid=(m // tm, n // tn, nk),
        in_specs=[pl.BlockSpec((tm, tk), lambda i, j, k: (i, k)), b_spec] + [pl.BlockSpec(memory_space=pl.ANY)] * len(extra),
        out_specs=[out_spec] * len(out_dtypes), out_shape=[jax.ShapeDtypeStruct(out_shape, d) for d in out_dtypes],
        scratch_shapes=scratch, compiler_params=_cparams("parallel", "parallel", "arbitrary"))(a, b, *extra)
    return res if also_bf16 else res[0]


def _rms_in(x, nw):
    s, d = x.shape

    def body(x_ref, w_ref, h_ref, ht_ref):
        xv = x_ref[...]
        r = lax.rsqrt(jnp.mean(xv * xv, axis=-1, keepdims=True) + NORM_EPS)
        h = xv * r * w_ref[...]
        h_ref[...] = h.astype(BF16)
        ht_ref[...] = h.T.astype(BF16)

    return pl.pallas_call(
        body, name="rms_in", grid=(s // BIG_TILE,),
        in_specs=[pl.BlockSpec((BIG_TILE, d), lambda i: (i, 0)), pl.BlockSpec((1, d), lambda i: (0, 0))],
        out_specs=[pl.BlockSpec((BIG_TILE, d), lambda i: (i, 0)), pl.BlockSpec((d, BIG_TILE), lambda i: (0, i))],
        out_shape=[jax.ShapeDtypeStruct((s, d), BF16), jax.ShapeDtypeStruct((d, s), BF16)],
        compiler_params=_cparams("parallel"))(x, nw)


def _rms_in_bwd(x, nw, dh, dx2):
    s, d = x.shape

    def body(x_ref, w_ref, dh_ref, dx2_ref, dx_ref, dw_ref):
        i = pl.program_id(0)
        xv = x_ref[...]
        r = lax.rsqrt(jnp.mean(xv * xv, axis=-1, keepdims=True) + NORM_EPS)
        dhv = dh_ref[...]
        dyw = dhv * w_ref[...]
        dx_ref[...] = dx2_ref[...] + r * dyw - xv * (r * r * r) * jnp.mean(dyw * xv, axis=-1, keepdims=True)
        part = jnp.sum(dhv * xv * r, axis=0, keepdims=True)

        @pl.when(i == 0)
        def _():
            dw_ref[...] = part

        @pl.when(i > 0)
        def _():
            dw_ref[...] += part

    row = pl.BlockSpec((BIG_TILE, d), lambda i: (i, 0))
    vec = pl.BlockSpec((1, d), lambda i: (0, 0))
    return pl.pallas_call(
        body, name="rms_in_bwd", grid=(s // BIG_TILE,), in_specs=[row, vec, row, row], out_specs=[row, vec],
        out_shape=[jax.ShapeDtypeStruct((s, d), F32), jax.ShapeDtypeStruct((1, d), F32)],
        compiler_params=_cparams("arbitrary"))(x, nw, dh, dx2)


def _shift_down(cur, prev8, k):
    rc = pltpu.roll(cur, k, 0)
    rp = pltpu.roll(prev8, k, 0)
    row = lax.broadcasted_iota(jnp.int32, prev8.shape, 0)
    top = jnp.where(row < k, rp, rc[:8])
    return jnp.concatenate([top, rc[8:]], axis=0)


def _shift_up(cur, next8, k):
    t = cur.shape[0]
    rc = pltpu.roll(cur, t - k, 0)
    rn = pltpu.roll(next8, 8 - k, 0)
    row = lax.broadcasted_iota(jnp.int32, next8.shape, 0)
    bot = jnp.where(row >= 8 - k, rn, rc[t - 8:])
    return jnp.concatenate([rc[:t - 8], bot], axis=0)


def _conv_fwd(proj, conv_w):
    s = proj.shape[0]
    tile = min(s, CONV_TILE)
    t8 = tile // 8

    def body(u_ref, up_ref, w_ref, c_ref, y_ref):
        i = pl.program_id(0)
        part = pl.program_id(1)
        cur = u_ref[...]
        prev8 = jnp.where(i > 0, up_ref[...], 0.0)
        w = w_ref[...]
        c = cur * w[3:4, :]
        for k in (1, 2, 3):
            c = c + _shift_down(cur, prev8, k) * w[3 - k:4 - k, :]
        c_ref[...] = c
        a = _silu(c)
        for h in range(DN_HEADS):
            ah = a[:, h * DN_D:(h + 1) * DN_D]
            r = lax.rsqrt(jnp.sum(ah * ah, axis=-1, keepdims=True) + NORM_EPS)
            y_ref[:, h * DN_D:(h + 1) * DN_D] = jnp.where(part < 2, ah * r, ah)

    return pl.pallas_call(
        body, name="conv_fwd", grid=(s // tile, 3),
        in_specs=[pl.BlockSpec((tile, DN_W), lambda i, p: (i, p)),
                  pl.BlockSpec((8, DN_W), lambda i, p: (jnp.maximum(i * t8 - 1, 0), p)),
                  pl.BlockSpec((4, DN_W), lambda i, p: (0, p))],
        out_specs=[pl.BlockSpec((tile, DN_W), lambda i, p: (i, p))] * 2,
        out_shape=[jax.ShapeDtypeStruct((s, 3 * DN_W), F32)] * 2,
        compiler_params=_cparams("parallel", "parallel"))(proj, proj, conv_w)


def _act_bwd(cv, dyv, normalised):
    out = []
    for h in range(DN_HEADS):
        sl = slice(h * DN_D, (h + 1) * DN_D)
        ch, dyh = cv[:, sl], dyv[:, sl]
        ah = _silu(ch)
        r = lax.rsqrt(jnp.sum(ah * ah, axis=-1, keepdims=True) + NORM_EPS)
        dn = r * dyh - ah * (r * r * r) * jnp.sum(dyh * ah, axis=-1, keepdims=True)
        out.append(jnp.where(normalised, dn, dyh) * _silu_grad(ch))
    return jnp.concatenate(out, axis=1)


DPROJ_IN = pl.BlockSpec(memory_space=pl.ANY)


def _conv_bwd(proj, c_pre, dqkv, conv_w, dproj):
    s = proj.shape[0]
    tile = min(s, CONV_TILE)
    t8 = tile // 8
    nrow = s // tile
    last8 = s // 8 - 1

    def body(u_ref, c_ref, cn_ref, dy_ref, dyn_ref, w_ref, dproj_in, du_ref, dw_ref):
        i = pl.program_id(1)
        normalised = pl.program_id(0) < 2
        cur = u_ref[...]
        dcv = _act_bwd(c_ref[...], dy_ref[...], normalised)
        next8 = jnp.where(i < nrow - 1, _act_bwd(cn_ref[...], dyn_ref[...], normalised), 0.0)
        w = w_ref[...]

        @pl.when(i == 0)
        def _():
            dw_ref[...] = jnp.zeros_like(dw_ref)

        du = dcv * w[3:4, :]
        dw_ref[3:4, :] += jnp.sum(cur * dcv, axis=0, keepdims=True)
        for k in (1, 2, 3):
            ahead = _shift_up(dcv, next8, k)
            du = du + ahead * w[3 - k:4 - k, :]
            dw_ref[3 - k:4 - k, :] += jnp.sum(cur * ahead, axis=0, keepdims=True)
        du_ref[...] = du.astype(BF16)

    blk = pl.BlockSpec((tile, DN_W), lambda p, i: (i, p))
    nxt = pl.BlockSpec((8, DN_W), lambda p, i: (jnp.minimum((i + 1) * t8, last8), p))
    return pl.pallas_call(
        body, name="conv_bwd", grid=(3, nrow),
        in_specs=[blk, blk, nxt, blk, nxt, pl.BlockSpec((4, DN_W), lambda p, i: (0, p)), DPROJ_IN],
        out_specs=[blk, pl.BlockSpec((4, DN_W), lambda p, i: (0, p))],
        out_shape=[jax.ShapeDtypeStruct((s, PW), BF16), jax.ShapeDtypeStruct((4, 3 * DN_W), F32)],
        input_output_aliases={6: 0},
        compiler_params=_cparams("parallel", "arbitrary"))(proj, c_pre, c_pre, dqkv, dqkv, conv_w, dproj)


def _gates_fwd(proj, gate_par):
    s = proj.shape[0]

    def body(ba_ref, par_ref, o_ref):
        v = ba_ref[...]
        lane = lax.broadcasted_iota(jnp.int32, v.shape, 1)
        beta = _sigmoid(v)
        g = -jnp.exp(par_ref[0:1, :]) * _softplus(v + par_ref[1:2, :])
        o_ref[...] = jnp.where(lane < DN_HEADS, beta, jnp.where(lane < 2 * DN_HEADS, g, 0.0))

    return pl.pallas_call(
        body, name="gates_fwd", grid=(s // ROW_TILE,),
        in_specs=[pl.BlockSpec((ROW_TILE, 128), lambda i: (i, OFF_BA // 128)), pl.BlockSpec((8, 128), lambda i: (0, 0))],
        out_specs=pl.BlockSpec((ROW_TILE, 128), lambda i: (i, 0)),
        out_shape=jax.ShapeDtypeStruct((s, 128), F32), compiler_params=_cparams("parallel"))(proj, gate_par)


def _gates_bwd(proj, gate_par, dbg, dproj):
    s = proj.shape[0]

    def body(ba_ref, par_ref, d_ref, dproj_in, o_ref, dpar_ref):
        i = pl.program_id(0)
        v = ba_ref[...]
        dv = d_ref[...]
        lane = lax.broadcasted_iota(jnp.int32, v.shape, 1)
        beta = _sigmoid(v)
        nega = -jnp.exp(par_ref[0:1, :])
        xs = v + par_ref[1:2, :]
        dsp = dv * nega * _sigmoid(xs)
        dal = dv * nega * _softplus(xs)
        is_b = lane < DN_HEADS
        is_g = jnp.logical_and(lane >= DN_HEADS, lane < 2 * DN_HEADS)
        o_ref[:, :128] = jnp.where(is_b, dv * beta * (1.0 - beta), jnp.where(is_g, dsp, 0.0)).astype(BF16)
        o_ref[:, 128:] = jnp.zeros((ROW_TILE, PW - OFF_BA - 128), BF16)
        r0 = jnp.sum(jnp.where(is_g, dal, 0.0), axis=0, keepdims=True)
        r1 = jnp.sum(jnp.where(is_g, dsp, 0.0), axis=0, keepdims=True)

        @pl.when(i == 0)
        def _():
            dpar_ref[...] = jnp.zeros_like(dpar_ref)

        dpar_ref[0:1, :] += r0
        dpar_ref[1:2, :] += r1

    return pl.pallas_call(
        body, name="gates_bwd", grid=(s // ROW_TILE,),
        in_specs=[pl.BlockSpec((ROW_TILE, 128), lambda i: (i, OFF_BA // 128)), pl.BlockSpec((8, 128), lambda i: (0, 0)),
                  pl.BlockSpec((ROW_TILE, 128), lambda i: (i, 0)), DPROJ_IN],
        out_specs=[pl.BlockSpec((ROW_TILE, PW - OFF_BA), lambda i: (i, OFF_BA // (PW - OFF_BA))),
                   pl.BlockSpec((8, 128), lambda i: (0, 0))],
        out_shape=[jax.ShapeDtypeStruct((s, PW), BF16), jax.ShapeDtypeStruct((8, 128), F32)],
        input_output_aliases={3: 0},
        compiler_params=_cparams("arbitrary"))(proj, gate_par, dbg, dproj)


def _chunk_masks():
    c = DN_CHUNK
    ii = lax.broadcasted_iota(jnp.int32, (c, c), 0)
    jj = lax.broadcasted_iota(jnp.int32, (c, c), 1)
    return dict(ii=ii, jj=jj, lower=(ii >= jj), strict=(ii > jj),
                lower_f=(ii >= jj).astype(BF16), upper_f=(ii <= jj).astype(BF16))


class _Heads:
    def __init__(self, xs):
        self.xs = list(xs)

    def _bin(self, o, f):
        if isinstance(o, _Heads):
            return _Heads([f(a, b) for a, b in zip(self.xs, o.xs)])
        return _Heads([f(a, o) for a in self.xs])

    def __add__(self, o):
        return self._bin(o, lambda a, b: a + b)

    def __sub__(self, o):
        return self._bin(o, lambda a, b: a - b)

    def __mul__(self, o):
        return self._bin(o, lambda a, b: a * b)

    __radd__ = __add__
    __rmul__ = __mul__

    def __neg__(self):
        return _Heads([-a for a in self.xs])

    def __getitem__(self, i):
        return _Heads([a[i] for a in self.xs])


def _hmap(f, *args):
    n = next(len(a.xs) for a in args if isinstance(a, _Heads))
    return _Heads([f(*[(a.xs[h] if isinstance(a, _Heads) else a) for a in args]) for h in range(n)])


def _hdot(a, b):
    return _hmap(_dot, a, b)


def _hdot_nt(a, b):
    return _hmap(_dot_nt, a, b)


def _hdot_tn(a, b):
    return _hmap(_dot_tn, a, b)


def _hcat(a, b, axis):
    return _hmap(lambda x, y: jnp.concatenate([x, y], axis=axis), a, b)


def _hsum(a, axis):
    return _hmap(lambda t: jnp.sum(t, axis=axis, keepdims=True), a)


def _hwhere(c, a, b):
    return _hmap(jnp.where, c, a, b)


def _chunk_gates(mk, bg):
    c = DN_CHUNK
    gc_all = _dot_exact_lhs(mk["lower_f"], bg)
    rows = jnp.concatenate([gc_all, gc_all], axis=0).T
    hs = range(DN_HEADS)
    return (_Heads(bg[:, h:h + 1] for h in hs), _Heads(gc_all[:, DN_HEADS + h:DN_HEADS + h + 1] for h in hs),
            _Heads(rows[DN_HEADS + h:DN_HEADS + h + 1, :] for h in hs))


def _chunk_common(mk, q, k, beta_col, gc_col, gc_r):
    c = DN_CHUNK
    lower, strict = mk["lower"], mk["strict"]
    qs = q * (DN_D ** -0.5)
    beta_b = _hmap(lambda t: jnp.broadcast_to(t, (c, DN_D)), beta_col)
    gc_b = _hmap(lambda t: jnp.broadcast_to(t, (c, DN_D)), gc_col)
    gc_sq = gc_b[:, :c]
    gam = _hwhere(lower, _hmap(lambda t: jnp.exp(jnp.minimum(t, 0.0)), gc_sq - gc_r[:, :c]), 0.0)
    egc = _hmap(jnp.exp, gc_b)
    gl = gc_b[c - 1:c, :]
    ekd = _hmap(jnp.exp, gl - gc_b)
    dl = _hmap(jnp.exp, gl)
    kb = k * beta_b
    scores = _hdot_nt(_hcat(kb, qs, 0), k)
    a_strict = _hwhere(strict, scores[:c] * gam, 0.0)
    aqk = _hwhere(lower, scores[c:] * gam, 0.0)
    return dict(k=k, qs=qs, beta_b=beta_b, gc_b=gc_b, gam=gam, egc=egc, ekd=ekd, dl=dl, kb=kb, a_strict=a_strict, aqk=aqk)


def _unit_lower_inverse_minus_eye(n_strict, ii, jj):
    same = lax.shift_right_logical(ii, 4) == lax.shift_right_logical(jj, 4)
    dmat = _hwhere(same, n_strict, 0.0)
    omat = n_strict - dmat
    d2 = _hdot(dmat, dmat)
    d4 = _hdot(d2, d2)
    d8 = _hdot(d4, d4)
    x1 = d2 - dmat - _hdot(dmat, d2)
    x2 = x1 + d4 + _hdot(x1, d4)
    x3 = x2 + d8 + _hdot(x2, d8)
    n1 = omat + _hdot(x3, omat)
    n2 = _hdot(n1, n1)
    y = n2 - n1 - _hdot(n1, n2)
    return y + x3 + _hdot(y, x3)


GDR_HEAD_SETS = (range(0, DN_HEADS),)


def _gdr_fwd(qkv, bg):
    s = qkv.shape[0]
    c = DN_CHUNK
    n = s // c

    def body(q_ref, k_ref, v_ref, bg_ref, o_ref, u_ref, w_ref, vn_ref, tm_ref, st_ref, state):
        @pl.when(pl.program_id(0) == 0)
        def _():
            state[...] = jnp.zeros_like(state)

        mk = _chunk_masks()
        gates = _chunk_gates(mk, bg_ref[...])
        for hs in GDR_HEAD_SETS:
            sls = [slice(h * DN_D, (h + 1) * DN_D) for h in hs]
            cm = _chunk_common(mk, _Heads(q_ref[:, sl] for sl in sls), _Heads(k_ref[:, sl] for sl in sls),
                               *[_Heads(g.xs[h] for h in hs) for g in gates])
            tm = _unit_lower_inverse_minus_eye(cm["a_strict"], mk["ii"], mk["jj"])
            rhs_u = _Heads(v_ref[:, sl] for sl in sls) * cm["beta_b"]
            rhs_w = cm["kb"] * cm["egc"]
            t_rhs = _hdot(tm, _hcat(rhs_u, rhs_w, 1))
            u = rhs_u + t_rhs[:, :DN_D]
            w = rhs_w + t_rhs[:, DN_D:]
            st = _Heads(state[h] for h in hs)
            on_state = _hdot(_hcat(w, cm["qs"] * cm["egc"], 0), st)
            v_new = u - on_state[:c]
            o = on_state[c:] + _hdot(cm["aqk"], v_new)
            st_new = st * cm["dl"] + _hdot_tn(cm["k"] * cm["ekd"], v_new)
            for i, (h, sl) in enumerate(zip(hs, sls)):
                o_ref[:, sl] = o.xs[i]
                u_ref[:, sl] = u.xs[i]
                w_ref[:, sl] = w.xs[i]
                vn_ref[:, sl] = v_new.xs[i]
                tm_ref[h, 0] = tm.xs[i]
                st_ref[h, 0] = st.xs[i]
                state[h] = st_new.xs[i]

    def part(p):
        return pl.BlockSpec((c, DN_W), lambda j: (j, p))

    return pl.pallas_call(
        body, name="gdr_fwd", grid=(n,),
        in_specs=[part(0), part(1), part(2), pl.BlockSpec((c, 128), lambda j: (j, 0))],
        out_specs=[part(0)] * 4 + [pl.BlockSpec((DN_HEADS, 1, c, c), lambda j: (0, j, 0, 0)),
                                   pl.BlockSpec((DN_HEADS, 1, DN_D, DN_D), lambda j: (0, j, 0, 0))],
        out_shape=[jax.ShapeDtypeStruct((s, DN_W), F32)] * 4
        + [jax.ShapeDtypeStruct((DN_HEADS, n, c, c), F32), jax.ShapeDtypeStruct((DN_HEADS, n, DN_D, DN_D), F32)],
        scratch_shapes=[pltpu.VMEM((DN_HEADS, DN_D, DN_D), F32)],
        compiler_params=_cparams("arbitrary"))(qkv, qkv, qkv, bg)


def _gdr_bwd(qkv, bg, u, w, vn, tmat, states, do):
    s = qkv.shape[0]
    c = DN_CHUNK
    n = s // c

    def body(q_ref, k_ref, v_ref, bg_ref, u_ref, w_ref, vn_ref, tm_ref, st_ref, do_ref,
             dqkv_ref, dbg_ref, dstate):
        @pl.when(pl.program_id(0) == 0)
        def _():
            dstate[...] = jnp.zeros_like(dstate)

        mk = _chunk_masks()
        lower, strict = mk["lower"], mk["strict"]
        bg = bg_ref[...]
        ones = jnp.ones((c, DN_D), BF16)
        rowi = lax.broadcasted_iota(jnp.int32, (c, DN_D), 0)
        lane = lax.broadcasted_iota(jnp.int32, (c, 128), 1)
        hs = range(DN_HEADS)
        sls = [slice(h * DN_D, (h + 1) * DN_D) for h in hs]

        def heads_of(ref):
            return _Heads(ref[:, sl] for sl in sls)

        cm = _chunk_common(mk, heads_of(q_ref), heads_of(k_ref), *_chunk_gates(mk, bg))
        k, qs, beta_b = cm["k"], cm["qs"], cm["beta_b"]
        gam, egc, ekd, dl, kb = cm["gam"], cm["egc"], cm["ekd"], cm["dl"], cm["kb"]
        aqk, a_strict = cm["aqk"], cm["a_strict"]
        v, uu, ww, v_new, dov = heads_of(v_ref), heads_of(u_ref), heads_of(w_ref), heads_of(vn_ref), heads_of(do_ref)
        st = _Heads(st_ref[h, 0] for h in hs)
        dsn = _Heads(dstate[h] for h in hs)
        qd = qs * egc
        kd = k * ekd

        dv_new = _hdot_tn(aqk, dov) + _hdot(kd, dsn)
        do_sv = _hdot_nt(dov, _hcat(st, v_new, 0))
        dqd = do_sv[:, :DN_D]
        daqk = _hwhere(lower, do_sv[:, DN_D:], 0.0)
        dkd = _hdot_nt(v_new, dsn)
        ddl = _hsum(_hsum(dsn * st, 1), 0)
        dw = -_hdot_nt(dv_new, st)
        ds_new = dsn * dl + _hdot_tn(_hcat(qd, -ww, 0), _hcat(dov, dv_new, 0))

        tm = _Heads(tm_ref[h, 0] for h in hs)
        tt = _hdot_tn(tm, _hcat(dv_new, dw, 1))
        dru = dv_new + tt[:, :DN_D]
        drw = dw + tt[:, DN_D:]
        dn = _hwhere(strict, -_hdot_nt(_hcat(dru, drw, 1), _hcat(uu, ww, 1)), 0.0)
        dag = dn * gam
        dqg = daqk * gam
        both = _hcat(dag, dqg, 0)
        on_k = _hdot(both, k)
        dkb = on_k[:c] + drw * egc
        dqs = on_k[c:] + dqd * egc
        dk = _hdot_tn(both, _hcat(kb, qs, 0)) + dkb * beta_b + dkd * ekd
        pmat = dn * a_strict + daqk * aqk
        tkd = _hsum(dkd * kd, -1)
        dgc = (_hsum(pmat, -1) - _hmap(_dot_tn_exact_rhs, pmat, ones) + _hsum(drw * (kb * egc), -1)
               + _hsum(dqd * qd, -1) - tkd)
        last = _hsum(tkd, 0) + ddl * dl
        dgc = dgc + _hwhere(rowi == c - 1, last, 0.0)
        dbeta = _hsum(dru * v, -1) + _hsum(dkb * k, -1)
        dq = dqs * (DN_D ** -0.5)
        dv = dru * beta_b

        dgc_all = jnp.zeros((c, 128), F32)
        dbg = jnp.zeros((c, 128), F32)
        for h, sl in zip(hs, sls):
            dqkv_ref[:, sl] = dq.xs[h]
            dqkv_ref[:, DN_W + h * DN_D:DN_W + (h + 1) * DN_D] = dk.xs[h]
            dqkv_ref[:, 2 * DN_W + h * DN_D:2 * DN_W + (h + 1) * DN_D] = dv.xs[h]
            dstate[h] = ds_new.xs[h]
            dgc_all = dgc_all + jnp.where(lane == DN_HEADS + h, dgc.xs[h], 0.0)
            dbg = dbg + jnp.where(lane == h, dbeta.xs[h], 0.0)
        dbg_ref[...] = dbg + _dot_exact_lhs(mk["upper_f"], dgc_all)

    def part(p):
        return pl.BlockSpec((c, DN_W), lambda j: (n - 1 - j, p))

    vec = pl.BlockSpec((c, 128), lambda j: (n - 1 - j, 0))
    return pl.pallas_call(
        body, name="gdr_bwd", grid=(n,),
        in_specs=[part(0), part(1), part(2), vec, part(0), part(0), part(0),
                  pl.BlockSpec((DN_HEADS, 1, c, c), lambda j: (0, n - 1 - j, 0, 0)),
                  pl.BlockSpec((DN_HEADS, 1, DN_D, DN_D), lambda j: (0, n - 1 - j, 0, 0)), part(0)],
        out_specs=[pl.BlockSpec((c, 3 * DN_W), lambda j: (n - 1 - j, 0)), vec],
        out_shape=[jax.ShapeDtypeStruct((s, 3 * DN_W), F32), jax.ShapeDtypeStruct((s, 128), F32)],
        scratch_shapes=[pltpu.VMEM((DN_HEADS, DN_D, DN_D), F32)],
        compiler_params=_cparams("arbitrary"))(qkv, qkv, qkv, bg, u, w, vn, tmat, states, do)


def _gdr_out(o, proj, dnw):
    s = o.shape[0]

    def body(o_ref, z_ref, w_ref, y_ref, yt_ref):
        ov, zv, wv = o_ref[...], z_ref[...], w_ref[...]
        for h in range(DN_HEADS):
            sl = slice(h * DN_D, (h + 1) * DN_D)
            oh = ov[:, sl]
            r = lax.rsqrt(jnp.mean(oh * oh, axis=-1, keepdims=True) + NORM_EPS)
            y = (oh * r * wv) * _silu(zv[:, sl])
            y_ref[:, sl] = y.astype(BF16)
            yt_ref[sl, :] = y.T.astype(BF16)

    row = pl.BlockSpec((BIG_TILE, DN_W), lambda i: (i, 0))
    return pl.pallas_call(
        body, name="gdr_out", grid=(s // BIG_TILE,),
        in_specs=[row, pl.BlockSpec((BIG_TILE, DN_W), lambda i: (i, OFF_Z_A // DN_W)), pl.BlockSpec((1, DN_D), lambda i: (0, 0))],
        out_specs=[row, pl.BlockSpec((DN_W, BIG_TILE), lambda i: (0, i))],
        out_shape=[jax.ShapeDtypeStruct((s, DN_W), BF16), jax.ShapeDtypeStruct((DN_W, s), BF16)],
        compiler_params=_cparams("parallel"))(o, proj, dnw)


def _gdr_out_bwd(o, proj, dnw, dy):
    s = o.shape[0]

    def body(o_ref, z_ref, w_ref, dy_ref, do_ref, dz_ref, dw_ref):
        i = pl.program_id(0)
        ov, zv, wv, dyv = o_ref[...], z_ref[...], w_ref[...], dy_ref[...]
        acc = jnp.zeros((1, DN_D), F32)
        for h in range(DN_HEADS):
            sl = slice(h * DN_D, (h + 1) * DN_D)
            oh, zh, dh = ov[:, sl], zv[:, sl], dyv[:, sl]
            r = lax.rsqrt(jnp.mean(oh * oh, axis=-1, keepdims=True) + NORM_EPS)
            dn = dh * _silu(zh)
            dz_ref[:, sl] = (dh * (oh * r * wv) * _silu_grad(zh)).astype(BF16)
            acc = acc + jnp.sum(dn * oh * r, axis=0, keepdims=True)
            dnw_ = dn * wv
            do_ref[:, sl] = r * dnw_ - oh * (r * r * r) * jnp.mean(dnw_ * oh, axis=-1, keepdims=True)

        @pl.when(i == 0)
        def _():
            dw_ref[...] = acc

        @pl.when(i > 0)
        def _():
            dw_ref[...] += acc

    row = pl.BlockSpec((ROW_TILE, DN_W), lambda i: (i, 0))
    vec = pl.BlockSpec((1, DN_D), lambda i: (0, 0))
    return pl.pallas_call(
        body, name="gdr_out_bwd", grid=(s // ROW_TILE,),
        in_specs=[row, pl.BlockSpec((ROW_TILE, DN_W), lambda i: (i, OFF_Z_A // DN_W)), vec, row],
        out_specs=[row, pl.BlockSpec((ROW_TILE, DN_W), lambda i: (i, OFF_Z_A // DN_W)), vec],
        out_shape=[jax.ShapeDtypeStruct((s, DN_W), F32), jax.ShapeDtypeStruct((s, PW), BF16),
                   jax.ShapeDtypeStruct((1, DN_D), F32)],
        compiler_params=_cparams("arbitrary"))(o, proj, dnw, dy)


def _slope(group, head):
    idx = (group * DIL_HEADS + head + 1).astype(F32)
    return jnp.exp(jnp.full((1, 128), -8.0 * math.log(2.0) / (N_DIL * DIL_HEADS), F32) * idx)


def _att_scores(qb, k_cur, k_prev, slope_d, has_prev):
    iq = lax.broadcasted_iota(jnp.int32, (ATT_BLOCK, ATT_BLOCK), 0)
    jk = lax.broadcasted_iota(jnp.int32, (ATT_BLOCK, ATT_BLOCK), 1)
    dist_c = (iq - jk).astype(F32)
    s_cur = jnp.where(iq >= jk, _dot_nt(qb, k_cur) - slope_d * dist_c, NEG)
    s_prev = jnp.where(jnp.logical_and(jk >= iq, has_prev),
                       _dot_nt(qb, k_prev) - slope_d * (dist_c + float(ATT_BLOCK)), NEG)
    return s_cur, s_prev


def _att_scores_whole(qb, k, slope_d):
    n = 2 * ATT_BLOCK
    dist = lax.broadcasted_iota(jnp.int32, (n, n), 0) - lax.broadcasted_iota(jnp.int32, (n, n), 1)
    valid = jnp.logical_and(dist >= 0, dist <= ATT_BLOCK)
    return jnp.where(valid, _dot_nt(qb, k) - slope_d[:, 0:1] * dist.astype(F32), NEG)


def _att_tiles(i, dil, nb):
    tiles = nb // 2
    per = dil * tiles // ATT_UNROLL
    assert nb % 2 == 0 and tiles >= 2 and per * ATT_UNROLL == dil * tiles
    for i0 in range(per):
        ts = [divmod(i0 + u * per, tiles) for u in range(ATT_UNROLL)]
        assert all(a[0] != b[0] or abs(a[1] - b[1]) >= 2 for n, a in enumerate(ts) for b in ts[n + 1:])
    qrows, krows, has_prev = [], [], []
    for u in range(ATT_UNROLL):
        t = i + u * per
        r = lax.div(t, tiles)
        j = lax.rem(t, tiles)
        qbase = r + dil * 2 * ATT_BLOCK * j
        kbase = qbase - dil * ATT_BLOCK * jnp.minimum(j, 1)
        if dil == 1:
            qbase, kbase = pl.multiple_of(qbase, ATT_BLOCK), pl.multiple_of(kbase, ATT_BLOCK)
        qrows.append(pl.ds(qbase, 2 * ATT_BLOCK, stride=dil))
        krows.append(pl.ds(kbase, 3 * ATT_BLOCK, stride=dil))
        has_prev.append(j > 0)
    return qrows, krows, has_prev


def _att_scores_tile(qb, k, slope_d, has_prev):
    iq = lax.broadcasted_iota(jnp.int32, (2 * ATT_BLOCK, 3 * ATT_BLOCK), 0)
    ck = lax.broadcasted_iota(jnp.int32, (2 * ATT_BLOCK, 3 * ATT_BLOCK), 1)
    dist = iq - ck + jnp.where(has_prev, ATT_BLOCK, 0)
    valid = jnp.logical_and(dist >= 0, dist <= ATT_BLOCK)
    return jnp.where(valid, _dot_nt(qb, k) - slope_d[:, 0:1] * dist.astype(F32), NEG)


ATT_UNROLL = 4


def _att_blocks(i, dil, nb):
    per = dil * nb // ATT_UNROLL
    assert per * ATT_UNROLL == dil * nb
    for i0 in range(per):
        blocks = [divmod(i0 + u * per, nb) for u in range(ATT_UNROLL)]
        assert all(a[0] != b[0] or abs(a[1] - b[1]) >= 2 for n, a in enumerate(blocks) for b in blocks[n + 1:])
    curs, prvs, has_prev = [], [], []
    for u in range(ATT_UNROLL):
        t = i + u * per
        r = lax.div(t, nb)
        j = lax.rem(t, nb)
        base = r + dil * ATT_BLOCK * j
        pbase = base - dil * ATT_BLOCK * jnp.minimum(j, 1)
        if dil == 1:
            base, pbase = pl.multiple_of(base, ATT_BLOCK), pl.multiple_of(pbase, ATT_BLOCK)
        curs.append(pl.ds(base, ATT_BLOCK, stride=dil))
        prvs.append(pl.ds(pbase, ATT_BLOCK, stride=dil))
        has_prev.append(j > 0)
    return curs, prvs, has_prev


def _att_fwd(proj, group):
    s = proj.shape[0]
    dil = DIL_GROUPS[group][1]
    assert DIL_GROUPS[group][0] // dil == ATT_BLOCK
    nb = s // dil // ATT_BLOCK
    assert nb * dil * ATT_BLOCK == s

    def body(q_ref, k_ref, v_ref, o_ref, lse_ref):
        def emit(rows, num, den, mx):
            o_ref[rows, :] = num / den
            lse_ref[rows, :] = jnp.broadcast_to(mx + jnp.log(den), (num.shape[0], DIL_DH))

        slope_d = _slope(group, pl.program_id(0)) * float(dil)

        def step(i, carry):
            curs, prvs, has_prev = _att_blocks(i, dil, nb)
            us = range(ATT_UNROLL)
            qb = [q_ref[c, :] * (DIL_DH ** -0.5) for c in curs]
            sc = [_att_scores(qb[u], k_ref[curs[u], :], k_ref[prvs[u], :], slope_d, has_prev[u]) for u in us]
            mx = [jnp.maximum(jnp.max(a, axis=-1, keepdims=True), jnp.max(b, axis=-1, keepdims=True)) for a, b in sc]
            p_cur = [jnp.exp(sc[u][0] - mx[u]) for u in us]
            p_prev = [jnp.exp(sc[u][1] - mx[u]) for u in us]
            den = [jnp.sum(p_cur[u], axis=-1, keepdims=True) + jnp.sum(p_prev[u], axis=-1, keepdims=True) for u in us]
            num = [_dot(p_cur[u], v_ref[curs[u], :]) + _dot(p_prev[u], v_ref[prvs[u], :]) for u in us]
            for u in us:
                emit(curs[u], num[u], den[u], mx[u])
            return carry

        def step_whole(i, carry):
            rows = [pl.ds(i * ATT_UNROLL + u, 2 * ATT_BLOCK, stride=dil) for u in range(ATT_UNROLL)]
            sc = [_att_scores_whole(q_ref[r, :] * (DIL_DH ** -0.5), k_ref[r, :], slope_d) for r in rows]
            mx = [jnp.max(a, axis=-1, keepdims=True) for a in sc]
            p = [jnp.exp(a - m) for a, m in zip(sc, mx)]
            num = [_dot(pu, v_ref[r, :]) for pu, r in zip(p, rows)]
            for u, r in enumerate(rows):
                emit(r, num[u], jnp.sum(p[u], axis=-1, keepdims=True), mx[u])
            return carry

        def step_tile(i, carry):
            qrows, krows, has_prev = _att_tiles(i, dil, nb)
            us = range(ATT_UNROLL)
            sc = [_att_scores_tile(q_ref[qrows[u], :] * (DIL_DH ** -0.5), k_ref[krows[u], :], slope_d, has_prev[u]) for u in us]
            mx = [jnp.max(a, axis=-1, keepdims=True) for a in sc]
            p = [jnp.exp(a - m) for a, m in zip(sc, mx)]
            num = [_dot(p[u], v_ref[krows[u], :]) for u in us]
            for u in us:
                emit(qrows[u], num[u], jnp.sum(p[u], axis=-1, keepdims=True), mx[u])
            return carry

        if nb == 2:
            lax.fori_loop(0, dil // ATT_UNROLL, step_whole, 0)
        elif nb % 2 == 0:
            lax.fori_loop(0, dil * nb // 2 // ATT_UNROLL, step_tile, 0)
        else:
            lax.fori_loop(0, dil * nb // ATT_UNROLL, step, 0)

    def col(off):
        return pl.BlockSpec((s, DIL_DH), lambda h: (0, off // DIL_DH + group * DIL_HEADS + h))

    out = pl.BlockSpec((s, DIL_DH), lambda h: (0, h))
    return pl.pallas_call(
        body, name=f"att_fwd{group}", grid=(DIL_HEADS,), in_specs=[col(OFF_Q_B), col(OFF_K_B), col(OFF_V_B)],
        out_specs=[out, out], out_shape=[jax.ShapeDtypeStruct((s, DIL_W), F32)] * 2,
        compiler_params=_cparams("parallel"))(proj, proj, proj)


def _att_bwd(proj, group, do, lse, delta):
    s = proj.shape[0]
    dil = DIL_GROUPS[group][1]
    nb = s // dil // ATT_BLOCK

    def body(q_ref, k_ref, v_ref, do_ref, lse_ref, dl_ref, dq_ref, dk_ref, dv_ref, dq_acc, dk_acc, dv_acc):
        slope_d = _slope(group, pl.program_id(0)) * float(dil)
        dk_acc[...] = jnp.zeros_like(dk_acc)
        dv_acc[...] = jnp.zeros_like(dv_acc)

        def step(i, carry):
            curs, prvs, has_prev = _att_blocks(i, dil, nb)
            us = range(ATT_UNROLL)
            qb = [q_ref[c, :] * (DIL_DH ** -0.5) for c in curs]
            k_cur, k_prev = [k_ref[c, :] for c in curs], [k_ref[p, :] for p in prvs]
            v_cur, v_prev = [v_ref[c, :] for c in curs], [v_ref[p, :] for p in prvs]
            sc = [_att_scores(qb[u], k_cur[u], k_prev[u], slope_d, has_prev[u]) for u in us]
            lse_b, delta_b, dob = [lse_ref[c, :] for c in curs], [dl_ref[c, :] for c in curs], [do_ref[c, :] for c in curs]
            p_cur = [jnp.exp(sc[u][0] - lse_b[u]) for u in us]
            p_prev = [jnp.exp(sc[u][1] - lse_b[u]) for u in us]
            ds_cur = [p_cur[u] * (_dot_nt(dob[u], v_cur[u]) - delta_b[u]) for u in us]
            ds_prev = [p_prev[u] * (_dot_nt(dob[u], v_prev[u]) - delta_b[u]) for u in us]
            dq = [(_dot(ds_cur[u], k_cur[u]) + _dot(ds_prev[u], k_prev[u])) * (DIL_DH ** -0.5) for u in us]
            dk_c = [_dot_tn(ds_cur[u], qb[u]) for u in us]
            dv_c = [_dot_tn(p_cur[u], dob[u]) for u in us]
            dk_p = [_dot_tn(ds_prev[u], qb[u]) for u in us]
            dv_p = [_dot_tn(p_prev[u], dob[u]) for u in us]
            for u in us:
                dq_acc[curs[u], :] = dq[u]
                dk_acc[curs[u], :] += dk_c[u]
                dv_acc[curs[u], :] += dv_c[u]
            for u in us:
                dk_acc[prvs[u], :] += dk_p[u]
                dv_acc[prvs[u], :] += dv_p[u]
            return carry

        def step_whole(i, carry):
            rows = [pl.ds(i * ATT_UNROLL + u, 2 * ATT_BLOCK, stride=dil) for u in range(ATT_UNROLL)]
            qb = [q_ref[r, :] * (DIL_DH ** -0.5) for r in rows]
            kk, vv, dob = [k_ref[r, :] for r in rows], [v_ref[r, :] for r in rows], [do_ref[r, :] for r in rows]
            sc = [_att_scores_whole(qb[u], kk[u], slope_d) for u in range(ATT_UNROLL)]
            p = [jnp.exp(sc[u] - lse_ref[r, :][:, 0:1]) for u, r in enumerate(rows)]
            ds = [p[u] * (_dot_nt(dob[u], vv[u]) - dl_ref[r, :][:, 0:1]) for u, r in enumerate(rows)]
            dq = [_dot(ds[u], kk[u]) * (DIL_DH ** -0.5) for u in range(ATT_UNROLL)]
            dk = [_dot_tn(ds[u], qb[u]) for u in range(ATT_UNROLL)]
            dv = [_dot_tn(p[u], dob[u]) for u in range(ATT_UNROLL)]
            for u, r in enumerate(rows):
                dq_acc[r, :] = dq[u]
                dk_acc[r, :] = dk[u]
                dv_acc[r, :] = dv[u]
            return carry

        def step_tile(i, carry):
            qrows, krows, has_prev = _att_tiles(i, dil, nb)
            us = range(ATT_UNROLL)
            qb = [q_ref[r, :] * (DIL_DH ** -0.5) for r in qrows]
            kk, vv, dob = [k_ref[r, :] for r in krows], [v_ref[r, :] for r in krows], [do_ref[r, :] for r in qrows]
            sc = [_att_scores_tile(qb[u], kk[u], slope_d, has_prev[u]) for u in us]
            p = [jnp.exp(sc[u] - lse_ref[qrows[u], :][:, 0:1]) for u in us]
            ds = [p[u] * (_dot_nt(dob[u], vv[u]) - dl_ref[qrows[u], :][:, 0:1]) for u in us]
            dq = [_dot(ds[u], kk[u]) * (DIL_DH ** -0.5) for u in us]
            dk = [_dot_tn(ds[u], qb[u]) for u in us]
            dv = [_dot_tn(p[u], dob[u]) for u in us]
            for u in us:
                dq_acc[qrows[u], :] = dq[u]
                dk_acc[krows[u], :] += dk[u]
                dv_acc[krows[u], :] += dv[u]
            return carry

        if nb == 2:
            lax.fori_loop(0, dil // ATT_UNROLL, step_whole, 0)
        elif nb % 2 == 0:
            lax.fori_loop(0, dil * nb // 2 // ATT_UNROLL, step_tile, 0)
        else:
            lax.fori_loop(0, dil * nb // ATT_UNROLL, step, 0)
        dq_ref[...] = dq_acc[...].astype(BF16)
        dk_ref[...] = dk_acc[...].astype(BF16)
        dv_ref[...] = dv_acc[...].astype(BF16)

    def col(off):
        return pl.BlockSpec((s, DIL_DH), lambda h: (0, off // DIL_DH + group * DIL_HEADS + h))

    hd = pl.BlockSpec((s, DIL_DH), lambda h: (0, h))
    return pl.pallas_call(
        body, name=f"att_bwd{group}", grid=(DIL_HEADS,),
        in_specs=[col(OFF_Q_B), col(OFF_K_B), col(OFF_V_B), hd, hd, hd], out_specs=[hd, hd, hd],
        out_shape=[jax.ShapeDtypeStruct((s, DIL_W), BF16)] * 3,
        scratch_shapes=[pltpu.VMEM((s, DIL_DH), F32)] * 3,
        compiler_params=_cparams("parallel"))(proj, proj, proj, do, lse, delta)


def _att_merge(parts, proj):
    s = proj.shape[0]

    def body(o0, l0, o1, l1, o2, l2, z_ref, ob_ref, o_ref, lse_ref, obt_ref):
        m = jnp.maximum(jnp.maximum(l0[...], l1[...]), l2[...])
        num = jnp.zeros_like(m)
        den = jnp.zeros_like(m)
        for og, lg in ((o0, l0), (o1, l1), (o2, l2)):
            sc = jnp.exp(lg[...] - m)
            num = num + og[...] * sc
            den = den + sc
        o = num / den
        o_ref[...] = o
        lse_ref[...] = m + jnp.log(den)
        ob = o * _silu(z_ref[...])
        ob_ref[...] = ob.astype(BF16)
        obt_ref[...] = ob.T.astype(BF16)

    row = pl.BlockSpec((ROW_TILE, DIL_W), lambda i: (i, 0))
    flat = [a for p in parts for a in p]
    return pl.pallas_call(
        body, name="att_merge", grid=(s // ROW_TILE,),
        in_specs=[row] * 6 + [pl.BlockSpec((ROW_TILE, DIL_W), lambda i: (i, OFF_Z_B // DIL_W))],
        out_specs=[row, row, row, pl.BlockSpec((DIL_W, ROW_TILE), lambda i: (0, i))],
        out_shape=[jax.ShapeDtypeStruct((s, DIL_W), BF16), jax.ShapeDtypeStruct((s, DIL_W), F32),
                   jax.ShapeDtypeStruct((s, DIL_W), F32), jax.ShapeDtypeStruct((DIL_W, s), BF16)],
        compiler_params=_cparams("parallel"))(*flat, proj)


def _att_merge_bwd(o, proj, dob, dproj):
    s = o.shape[0]

    def body(o_ref, z_ref, d_ref, dproj_in, do_ref, dl_ref, dz_ref):
        ov, zv, dv = o_ref[...], z_ref[...], d_ref[...]
        do = dv * _silu(zv)
        do_ref[...] = do
        dz_ref[...] = (dv * ov * _silu_grad(zv)).astype(BF16)
        for h in range(DIL_HEADS):
            sl = slice(h * DIL_DH, (h + 1) * DIL_DH)
            dl_ref[:, sl] = jnp.broadcast_to(jnp.sum(do[:, sl] * ov[:, sl], axis=-1, keepdims=True), (ROW_TILE, DIL_DH))

    row = pl.BlockSpec((ROW_TILE, DIL_W), lambda i: (i, 0))
    return pl.pallas_call(
        body, name="att_merge_bwd", grid=(s // ROW_TILE,),
        in_specs=[row, pl.BlockSpec((ROW_TILE, DIL_W), lambda i: (i, OFF_Z_B // DIL_W)), row, DPROJ_IN],
        out_specs=[row, row, pl.BlockSpec((ROW_TILE, DIL_W), lambda i: (i, OFF_Z_B // DIL_W))],
        out_shape=[jax.ShapeDtypeStruct((s, DIL_W), F32), jax.ShapeDtypeStruct((s, DIL_W), F32),
                   jax.ShapeDtypeStruct((s, PW), BF16)],
        input_output_aliases={3: 2},
        compiler_params=_cparams("parallel"))(o, proj, dob, dproj)


def _merge(proj, ya, yb):
    s = proj.shape[0]

    def body(ga_ref, gb_ref, ya_ref, yb_ref, o_ref, ot_ref):
        m = _sigmoid(ga_ref[...]) * ya_ref[...] + _sigmoid(gb_ref[...]) * yb_ref[...]
        o_ref[...] = m.astype(BF16)
        ot_ref[...] = m.T.astype(BF16)

    row = pl.BlockSpec((ROW_TILE, D_MODEL), lambda i: (i, 0))
    return pl.pallas_call(
        body, name="merge", grid=(s // ROW_TILE,),
        in_specs=[pl.BlockSpec((ROW_TILE, D_MODEL), lambda i: (i, OFF_G_A // D_MODEL)),
                  pl.BlockSpec((ROW_TILE, D_MODEL), lambda i: (i, OFF_G_B // D_MODEL)), row, row],
        out_specs=[row, pl.BlockSpec((D_MODEL, ROW_TILE), lambda i: (0, i))],
        out_shape=[jax.ShapeDtypeStruct((s, D_MODEL), BF16), jax.ShapeDtypeStruct((D_MODEL, s), BF16)],
        compiler_params=_cparams("parallel"))(proj, proj, ya, yb)


def _merge_bwd(proj, ya, yb, dx2, w_out):
    s = proj.shape[0]

    def body(ga_ref, gb_ref, ya_ref, yb_ref, dx_ref, wo_ref, dya_ref, dyb_ref, dga_ref, dgb_ref):
        dmv = _dot_nt(dx_ref[...], wo_ref[...])
        sa, sb = _sigmoid(ga_ref[...]), _sigmoid(gb_ref[...])
        dya_ref[...] = (dmv * sa).astype(BF16)
        dyb_ref[...] = (dmv * sb).astype(BF16)
        dga_ref[...] = (dmv * ya_ref[...] * sa * (1.0 - sa)).astype(BF16)
        dgb_ref[...] = (dmv * yb_ref[...] * sb * (1.0 - sb)).astype(BF16)

    row = pl.BlockSpec((ROW_TILE, D_MODEL), lambda i: (i, 0))
    return pl.pallas_call(
        body, name="merge_bwd", grid=(s // ROW_TILE,),
        in_specs=[pl.BlockSpec((ROW_TILE, D_MODEL), lambda i: (i, OFF_G_A // D_MODEL)),
                  pl.BlockSpec((ROW_TILE, D_MODEL), lambda i: (i, OFF_G_B // D_MODEL)), row, row, row,
                  pl.BlockSpec(w_out.shape, lambda i: (0, 0))],
        out_specs=[row] * 4, out_shape=[jax.ShapeDtypeStruct((s, D_MODEL), BF16)] * 4,
        compiler_params=_cparams("parallel"))(proj, proj, ya, yb, dx2, w_out)


def _final(x, mg, w_out, fw, tgt):
    s, d = x.shape

    def body(x_ref, mg_ref, wo_ref, w_ref, y_ref, dx_ref, dw_ref, l_ref):
        i = pl.program_id(0)
        x2 = x_ref[...] + _dot(mg_ref[...], wo_ref[...])
        wv = w_ref[...]
        r = lax.rsqrt(jnp.mean(x2 * x2, axis=-1, keepdims=True) + NORM_EPS)
        e = x2 * r * wv - y_ref[...]
        lrow = jnp.mean(e * e, axis=-1, keepdims=True)
        lpart = jnp.broadcast_to(0.5 * jnp.sum(lrow, axis=0, keepdims=True), (1, 128))
        dy = e * (1.0 / d)
        dwp = jnp.sum(dy * x2 * r, axis=0, keepdims=True)
        dyw = dy * wv
        dx_ref[...] = r * dyw - x2 * (r * r * r) * jnp.mean(dyw * x2, axis=-1, keepdims=True)

        @pl.when(i == 0)
        def _():
            dw_ref[...] = dwp
            l_ref[...] = lpart

        @pl.when(i > 0)
        def _():
            dw_ref[...] += dwp
            l_ref[...] += lpart

    row = pl.BlockSpec((BIG_TILE, d), lambda i: (i, 0))
    vec = pl.BlockSpec((1, d), lambda i: (0, 0))
    return pl.pallas_call(
        body, name="final", grid=(s // BIG_TILE,),
        in_specs=[row, row, pl.BlockSpec(w_out.shape, lambda i: (0, 0)), vec, row],
        out_specs=[row, vec, pl.BlockSpec((1, 128), lambda i: (0, 0))],
        out_shape=[jax.ShapeDtypeStruct((s, d), F32), jax.ShapeDtypeStruct((1, d), F32), jax.ShapeDtypeStruct((1, 128), F32)],
        compiler_params=_cparams("arbitrary"))(x, mg, w_out, fw, tgt)


def _adamw(w, g, m, v, name):
    r, c = w.shape
    cap = max(8, (1 << 18) // c)
    divisors = [t for t in range(8, min(r, cap) + 1, 8) if r % t == 0]
    tr = r if r <= 8 else (max(divisors) if divisors else cap)

    def body(w_ref, g_ref, m_ref, v_ref, d_ref, nm_ref, nv_ref):
        gv = g_ref[...]
        mn = ADAM_B1 * m_ref[...] + (1.0 - ADAM_B1) * gv
        vn = ADAM_B2 * v_ref[...] + (1.0 - ADAM_B2) * (gv * gv)
        m_hat = mn / (1.0 - ADAM_B1 ** ADAM_STEP)
        v_hat = vn / (1.0 - ADAM_B2 ** ADAM_STEP)
        d_ref[...] = -ADAM_LR * (m_hat / (jnp.sqrt(v_hat) + ADAM_EPS) + ADAM_WD * w_ref[...])
        nm_ref[...] = mn
        nv_ref[...] = vn

    blk = pl.BlockSpec((tr, c), lambda i: (i, 0))
    return pl.pallas_call(
        body, name=name, grid=(pl.cdiv(r, tr),), in_specs=[blk] * 4, out_specs=[blk] * 3,
        out_shape=[jax.ShapeDtypeStruct((r, c), F32)] * 3, compiler_params=_cparams("parallel"))(w, g, m, v)


HBM_SPEC = pl.BlockSpec(memory_space=pl.ANY)


def _place():
    x, y, c = lax.axis_index("x"), lax.axis_index("y"), lax.axis_index("c")
    chips = [(1 - x, y), (x, 1 - y), (1 - x, 1 - y)]
    return x, y, c, chips


def _ag_weights(packs):
    na = len(packs)
    nsem = 8

    def body(*refs):
        p_refs, out_refs = refs[:na], refs[na:2 * na]
        send_sems, recv_sems = refs[2 * na:]
        x, y, c, _ = _place()
        me, sib, j = (x, y, c), (x, y, 1 - c), 2 * x + y
        xn, yn = (1 - x, y, c), (x, 1 - y, c)
        jx, jy, jd = 2 * (1 - x) + y, 2 * x + (1 - y), 2 * (1 - x) + (1 - y)

        def rc(a, k, src, dst, to):
            return pltpu.make_async_remote_copy(src_ref=src, dst_ref=dst, send_sem=send_sems.at[nsem * a + k],
                                                recv_sem=recv_sems.at[nsem * a + k], device_id=to, device_id_type=MESH)

        sent = []
        for a in range(na):
            mine, land = p_refs[a].at[c], out_refs[a].at[j, c]
            sent += [rc(a, 0, mine, land, xn), rc(a, 1, mine, land, yn), rc(a, 7, p_refs[a], out_refs[a].at[j], sib)]
        for cp in sent:
            cp.start()
        for a in range(na):
            half = p_refs[a].shape[1] // 2
            top, bottom = pl.ds(0, half), pl.ds(half, half)
            from_x, from_y, from_d = out_refs[a].at[jx, c], out_refs[a].at[jy, c], out_refs[a].at[jd, c]
            rc(a, 0, p_refs[a].at[c], from_x, me).wait_recv()
            later = [rc(a, 2, from_x.at[top], from_x.at[top], yn), rc(a, 4, from_x, from_x, sib)]
            for cp in later:
                cp.start()
            sent += later
            rc(a, 1, p_refs[a].at[c], from_y, me).wait_recv()
            later = [rc(a, 3, from_y.at[bottom], from_y.at[bottom], xn), rc(a, 5, from_y, from_y, sib)]
            for cp in later:
                cp.start()
            sent += later
            rc(a, 2, from_d.at[top], from_d.at[top], me).wait_recv()
            rc(a, 3, from_d.at[bottom], from_d.at[bottom], me).wait_recv()
            cp = rc(a, 6, from_d, from_d, sib)
            cp.start()
            sent.append(cp)
        for a in range(na):
            for k, jj in ((4, jx), (5, jy), (6, jd)):
                rc(a, k, p_refs[a].at[c], out_refs[a].at[jj, 1 - c], me).wait_recv()
            rc(a, 7, p_refs[a], out_refs[a].at[j], me).wait_recv()
        for cp in sent:
            cp.wait_send()

    return pl.pallas_call(
        body, name="ag_weights",
        out_shape=[jax.ShapeDtypeStruct((N_CHIPS,) + p.shape, p.dtype) for p in packs],
        in_specs=[HBM_SPEC] * na, out_specs=[HBM_SPEC] * na,
        scratch_shapes=[pltpu.SemaphoreType.DMA((nsem * na,)), pltpu.SemaphoreType.DMA((nsem * na,))])(*packs)


def _rs_pair(dwpt, gpack):
    n = N_CHIPS
    hw = SHARD_PAD // 2

    def body(d_ref, g_ref, out_d, out_g, send_sems, recv_sems):
        x, y, c, _ = _place()
        sib = (x, y, 1 - c)
        cps = []
        for p in range(n):
            start = pl.multiple_of(WIN_BASE[p] + (1 - c) * hw, TILE_ROWS)
            cps.append(pltpu.make_async_remote_copy(
                src_ref=d_ref.at[pl.ds(start, hw)], dst_ref=out_d.at[p], send_sem=send_sems.at[p],
                recv_sem=recv_sems.at[p], device_id=sib, device_id_type=MESH))
            cps.append(pltpu.make_async_remote_copy(
                src_ref=g_ref.at[p, 1 - c], dst_ref=out_g.at[p], send_sem=send_sems.at[n + p],
                recv_sem=recv_sems.at[n + p], device_id=sib, device_id_type=MESH))
        for cp in cps:
            cp.start()
        for cp in cps:
            cp.wait_recv()
        for cp in cps:
            cp.wait_send()

    return pl.pallas_call(
        body, name="rs_pair",
        out_shape=[jax.ShapeDtypeStruct((n, hw, dwpt.shape[1]), dwpt.dtype),
                   jax.ShapeDtypeStruct((n,) + gpack.shape[2:], gpack.dtype)],
        in_specs=[HBM_SPEC] * 2, out_specs=[HBM_SPEC] * 2,
        scratch_shapes=[pltpu.SemaphoreType.DMA((2 * n,)), pltpu.SemaphoreType.DMA((2 * n,))])(dwpt, gpack)


def _add_halves_win(dwpt, other, c):
    n, rh, wd = other.shape
    tr = _row_tile(rh)

    def body(s_ref, d_ref, o_ref, out_ref):
        out_ref[0] = (d_ref[...] + o_ref[0]).astype(BF16)

    scal = jnp.concatenate([jnp.reshape(c, (1,)).astype(jnp.int32), jnp.asarray(WIN_BASE, jnp.int32)])
    grid_spec = pltpu.PrefetchScalarGridSpec(
        num_scalar_prefetch=1, grid=(n, rh // tr),
        in_specs=[pl.BlockSpec((pl.Element(tr), pl.Element(wd)),
                               lambda p, i, sr: (pl.multiple_of(sr[1 + p] + sr[0] * rh + i * tr, TILE_ROWS), 0)),
                  pl.BlockSpec((1, tr, wd), lambda p, i, sr: (p, i, 0))],
        out_specs=pl.BlockSpec((1, tr, wd), lambda p, i, sr: (p, i, 0)))
    return pl.pallas_call(
        body, name="add_halves_in", grid_spec=grid_spec, out_shape=jax.ShapeDtypeStruct((n, rh, wd), BF16),
        compiler_params=_cparams("parallel", "parallel"))(scal, dwpt, other)


SEM_SPEC = pl.BlockSpec(memory_space=pltpu.SEMAPHORE)
DATAFLOW_EFFECT = pltpu.SideEffectType.DATAFLOW_SIDE_EFFECTING


def _rs_chips_start(csums):
    na = len(csums)

    def body(*refs):
        s_refs, land_refs = refs[:na], refs[na:2 * na]
        send_sems, recv_sems = refs[2 * na], refs[2 * na + 1]
        token = refs[-1]
        x, y, c, chips = _place()
        j = 2 * x + y
        for a in range(na):
            for k, (cx, cy) in enumerate(chips):
                pltpu.make_async_remote_copy(src_ref=s_refs[a].at[2 * cx + cy], dst_ref=land_refs[a].at[j],
                                             send_sem=send_sems.at[3 * a + k], recv_sem=recv_sems.at[3 * a + k],
                                             device_id=(cx, cy, c), device_id_type=MESH).start()
        token[...] = jnp.zeros_like(token)

    hbm = [pltpu.HBM(s.shape, s.dtype) for s in csums]
    args = [pltpu.with_memory_space_constraint(s, pltpu.HBM) for s in csums]
    args += [pltpu.with_memory_space_constraint(lax.empty(s.shape, s.dtype), pltpu.HBM) for s in csums]
    res = pl.pallas_call(
        body, name="rs_chips_start",
        out_shape=(pltpu.SemaphoreType.DMA((3 * na,)), pltpu.SemaphoreType.DMA((3 * na,)), *hbm, *hbm,
                   jax.ShapeDtypeStruct((8, 128), F32)),
        in_specs=[pl.BlockSpec(memory_space=pltpu.HBM)] * (2 * na),
        out_specs=(SEM_SPEC, SEM_SPEC, *[pl.BlockSpec(memory_space=pltpu.HBM)] * (2 * na),
                   pl.BlockSpec(memory_space=pltpu.VMEM)),
        input_output_aliases={i: 2 + i for i in range(2 * na)},
        compiler_params=pltpu.CompilerParams(has_side_effects=DATAFLOW_EFFECT))(*args)
    return res[0], res[1], list(res[2:2 + na]), list(res[2 + na:2 + 2 * na]), res[-1]


def _rs_chips_wait(send_sems, recv_sems, csums, lands, after):
    na = len(csums)

    def body(*refs):
        s_refs, land_refs = refs[:na], refs[na:2 * na]
        send_sems, recv_sems = refs[2 * na], refs[2 * na + 1]
        x, y, c, chips = _place()
        j = 2 * x + y
        for a in range(na):
            for k, (cx, cy) in enumerate(chips):
                cp = pltpu.make_async_remote_copy(src_ref=s_refs[a].at[2 * cx + cy], dst_ref=land_refs[a].at[2 * cx + cy],
                                                  send_sem=send_sems.at[3 * a + k], recv_sem=recv_sems.at[3 * a + k],
                                                  device_id=(cx, cy, c), device_id_type=MESH)
                cp.wait_send()
                cp.wait_recv()

    hbm = [pltpu.HBM(s.shape, s.dtype) for s in csums]
    res = pl.pallas_call(
        body, name="rs_chips_wait", out_shape=(*hbm, *hbm),
        in_specs=[pl.BlockSpec(memory_space=pltpu.HBM)] * (2 * na) + [SEM_SPEC, SEM_SPEC, pl.BlockSpec(memory_space=pl.ANY)],
        out_specs=tuple([pl.BlockSpec(memory_space=pltpu.HBM)] * (2 * na)),
        input_output_aliases={i: i for i in range(2 * na)},
        compiler_params=pltpu.CompilerParams(has_side_effects=DATAFLOW_EFFECT))(*csums, *lands, send_sems, recv_sems, after)
    return list(res[:na]), list(res[na:])


SWAP_CHUNKS = 4


def _pair_swap(halves):
    na = len(halves)

    def body(*refs):
        h_refs, out_refs = refs[:na], refs[na:2 * na]
        send_sems, recv_sems = refs[2 * na:]
        x, y, c, _ = _place()
        cps = []
        for a in range(na):
            rows = h_refs[a].shape[0] // SWAP_CHUNKS
            assert rows * SWAP_CHUNKS == h_refs[a].shape[0]
            for q in range(SWAP_CHUNKS):
                k = SWAP_CHUNKS * a + q
                cps.append(pltpu.make_async_remote_copy(
                    src_ref=h_refs[a].at[pl.ds(q * rows, rows)], dst_ref=out_refs[a].at[pl.ds(q * rows, rows)],
                    send_sem=send_sems.at[k], recv_sem=recv_sems.at[k], device_id=(x, y, 1 - c), device_id_type=MESH))
        for cp in cps:
            cp.start()
        for cp in cps:
            cp.wait_recv()
        for cp in cps:
            cp.wait_send()

    return pl.pallas_call(
        body, name="pair_swap", out_shape=[jax.ShapeDtypeStruct(h.shape, h.dtype) for h in halves],
        in_specs=[HBM_SPEC] * na, out_specs=[HBM_SPEC] * na,
        scratch_shapes=[pltpu.SemaphoreType.DMA((SWAP_CHUNKS * na,)), pltpu.SemaphoreType.DMA((SWAP_CHUNKS * na,))])(*halves)


def _ag_small(v):
    m_per, n = v.shape

    def body(x_ref, out_ref, send_sems, recv_sems, local_sem):
        x, y, c, chips = _place()
        me, sibling = (x, y, c), (x, y, 1 - c)

        def rows(px, py, pc):
            return out_ref.at[pl.ds((4 * px + 2 * py + pc) * m_per, m_per), :]

        def copy(k, block, to, src=None):
            return pltpu.make_async_remote_copy(
                src_ref=rows(*block) if src is None else src, dst_ref=rows(*block), send_sem=send_sems.at[k],
                recv_sem=recv_sems.at[k], device_id=to, device_id_type=MESH)

        mine = pltpu.make_async_copy(x_ref, rows(*me), local_sem)
        mine.start()
        first = [copy(0, me, sibling, src=x_ref)]
        first += [copy(1 + k, me, (*chip, c), src=x_ref) for k, chip in enumerate(chips)]
        for cp in first:
            cp.start()
        passed = [copy(4 + k, (*chip, c), sibling) for k, chip in enumerate(chips)]
        for k, chip in enumerate(chips):
            copy(1 + k, (*chip, c), me).wait_recv()
            passed[k].start()
        copy(0, sibling, me).wait_recv()
        for k, chip in enumerate(chips):
            copy(4 + k, (*chip, 1 - c), me).wait_recv()
        for cp in first + passed:
            cp.wait_send()
        mine.wait()

    return pl.pallas_call(
        body, name="ag_small", out_shape=jax.ShapeDtypeStruct((8 * m_per, n), v.dtype),
        in_specs=[pl.BlockSpec(memory_space=pltpu.VMEM)], out_specs=pl.BlockSpec(memory_space=pltpu.VMEM),
        scratch_shapes=[pltpu.SemaphoreType.DMA((7,)), pltpu.SemaphoreType.DMA((7,)), pltpu.SemaphoreType.DMA])(v)


def _sum_blocks(a, nblk, name):
    rows, wd = a.shape
    r = rows // nblk
    tr = min(r, ROW_TILE)
    assert r % tr == 0

    def body(*refs):
        acc = refs[0][...].astype(F32)
        for ref in refs[1:nblk]:
            acc = acc + ref[...].astype(F32)
        refs[nblk][...] = acc

    nt = r // tr
    return pl.pallas_call(
        body, name=name, grid=(nt,),
        in_specs=[pl.BlockSpec((tr, wd), functools.partial(lambda i, b: (b * nt + i, 0), b=b)) for b in range(nblk)],
        out_specs=pl.BlockSpec((tr, wd), lambda i: (i, 0)),
        out_shape=jax.ShapeDtypeStruct((r, wd), F32), compiler_params=_cparams("parallel"))(*([a] * nblk))


def _row_tile(rows):
    best = max(t for t in range(16, 513, 16) if rows % t == 0)
    return best


def _sum_chips(by_src, csum, j, name):
    n, rh, wd = by_src.shape
    tr = _row_tile(rh)

    def body(j_ref, *refs):
        own = refs[n][0].astype(F32)
        acc = None
        for k in range(n):
            term = jnp.where(j_ref[0] == k, own, refs[k][0].astype(F32))
            acc = term if acc is None else acc + term
        refs[n + 1][...] = acc

    def other(k):
        return pl.BlockSpec((1, tr, wd), lambda i, jr: (jnp.where(jr[0] == k, (k + 1) % n, k), i, 0))

    grid_spec = pltpu.PrefetchScalarGridSpec(
        num_scalar_prefetch=1, grid=(rh // tr,),
        in_specs=[other(k) for k in range(n)] + [pl.BlockSpec((1, tr, wd), lambda i, jr: (jr[0], i, 0))],
        out_specs=pl.BlockSpec((tr, wd), lambda i, jr: (i, 0)))
    return pl.pallas_call(
        body, name=name, grid_spec=grid_spec, out_shape=jax.ShapeDtypeStruct((rh, wd), F32),
        compiler_params=_cparams("parallel"))(jnp.reshape(j, (1,)).astype(jnp.int32), *([by_src] * n), csum)


def _add_halves(gpack, other, c, name):
    n, _, rh, wd = gpack.shape
    tr = _row_tile(rh)

    def body(c_ref, g_ref, o_ref, out_ref):
        out_ref[0] = (g_ref[0, 0] + o_ref[0]).astype(BF16)

    grid_spec = pltpu.PrefetchScalarGridSpec(
        num_scalar_prefetch=1, grid=(n, rh // tr),
        in_specs=[pl.BlockSpec((1, 1, tr, wd), lambda p, i, cr: (p, cr[0], i, 0)),
                  pl.BlockSpec((1, tr, wd), lambda p, i, cr: (p, i, 0))],
        out_specs=pl.BlockSpec((1, tr, wd), lambda p, i, cr: (p, i, 0)))
    return pl.pallas_call(
        body, name=name, grid_spec=grid_spec, out_shape=jax.ShapeDtypeStruct((n, rh, wd), BF16),
        compiler_params=_cparams("parallel", "parallel"))(jnp.reshape(c, (1,)).astype(jnp.int32), gpack, other)


PACK_W = 1024
ROWS_O_DN = DN_W // N_CHIPS
ROWS_O_DIL = DIL_W * (D_MODEL // N_CHIPS) // PACK_W
ROWS_OUT = D_MODEL // N_CHIPS
ROWS_CONV = 4 * (3 * DN_W // N_CHIPS) // PACK_W
R1 = ROWS_O_DN
R2 = R1 + ROWS_O_DIL
R3 = R2 + ROWS_OUT
R4 = R3 + 16
R5 = R4 + 16
PACK_ROWS = 704
HALF_ROWS = PACK_ROWS // 2
SHARD_PAD = 2880


R6 = R5 + 2 * DN_HEADS

TILE_ROWS = 16
BA_IN_SHARD1 = REF_OFF_BA - SHARD_W
LOCAL_START = (0, SHARD_W, 2 * SHARD_W - 2 * DN_HEADS, 3 * SHARD_W - 2 * DN_HEADS)
LOCAL_END = LOCAL_START[1:] + (OFF_BA,)
WIN_BASE = tuple(s // TILE_ROWS * TILE_ROWS for s in LOCAL_START)


def _to_window(k, shard):
    nba = 2 * DN_HEADS
    body = shard
    if k == 1:
        row = lax.broadcasted_iota(jnp.int32, (SHARD_W - nba, 1), 0)
        body = jnp.where(row < BA_IN_SHARD1, shard[:SHARD_W - nba], shard[nba:])
    lead = LOCAL_START[k] - WIN_BASE[k]
    return jnp.pad(body, ((lead, SHARD_PAD - lead - body.shape[0]), (0, 0)))


def _from_window(k, win, ba):
    nba = 2 * DN_HEADS
    lead = LOCAL_START[k] - WIN_BASE[k]
    if k != 1:
        return win[lead:lead + SHARD_W]
    row = lax.broadcasted_iota(jnp.int32, (SHARD_W, 1), 0)
    before = win[lead:lead + SHARD_W]
    after = jnp.pad(win, ((nba, 0), (0, 0)))[lead:lead + SHARD_W]
    mid = jnp.pad(ba, ((BA_IN_SHARD1, SHARD_W - BA_IN_SHARD1 - nba), (0, 0)))
    return jnp.where(row < BA_IN_SHARD1, before, jnp.where(row < BA_IN_SHARD1 + nba, mid, after))


def _stack_windows(wins, ba):
    pieces = []
    for k in range(N_CHIPS):
        lo = WIN_BASE[k] + (TILE_ROWS if k else 0)
        hi = LOCAL_END[k] // TILE_ROWS * TILE_ROWS
        pieces.append(wins[k][lo - WIN_BASE[k]:hi - WIN_BASE[k]])
        if k + 1 < N_CHIPS:
            assert hi == WIN_BASE[k + 1]
            pieces.append(wins[k][hi - WIN_BASE[k]:hi - WIN_BASE[k] + TILE_ROWS] + wins[k + 1][:TILE_ROWS])
    pieces += [ba, jnp.zeros((PW - OFF_BA - ba.shape[0], ba.shape[1]), ba.dtype)]
    out = jnp.concatenate(pieces, axis=0)
    assert out.shape[0] == PW
    return out


def _local_step(x, tgt, norm_w, wpt, conv_full, a_log, dt_bias, dn_norm_w, w_o_dn, w_o_dil, w_out, final_norm_w):
    s = x.shape[0]
    h, h_t = _rms_in(x, norm_w)
    proj = _matmul(h, wpt, F32, 2048, 1280, 1024, "proj", nt=True)
    c_pre, qkv = _conv_fwd(proj, conv_full)
    gate_par = jnp.zeros((8, 128), F32).at[0, 8:16].set(a_log[0]).at[1, 8:16].set(dt_bias[0])
    bg = _gates_fwd(proj, gate_par)
    o_a, u, w, vn, tmat, states = _gdr_fwd(qkv, bg)
    oa2, oa2_t = _gdr_out(o_a, proj, dn_norm_w)
    ya = _matmul(oa2, w_o_dn, F32, 1024, 1024, 1024, "ya")
    parts = [_att_fwd(proj, g) for g in range(N_DIL)]
    ob, o_att, lse, ob_t = _att_merge(parts, proj)
    yb = _matmul(ob, w_o_dil, F32, 1024, 1024, 512, "yb")
    mg, mg_t = _merge(proj, ya, yb)
    dx2, dfw, lpart = _final(x, mg, w_out, final_norm_w, tgt)

    dw_out = _matmul(mg_t, dx2, F32, 1024, 1024, 1024, "dw_out")
    dya, dyb, dga, dgb = _merge_bwd(proj, ya, yb, dx2, w_out)
    doa2 = _matmul(dya, w_o_dn, F32, 1024, 1024, 1024, "d_oa2", nt=True)
    dw_o_dn = _matmul(oa2_t, dya, F32, 1024, 1024, 1024, "dw_o_dn")
    dob = _matmul(dyb, w_o_dil, F32, 1024, 512, 1024, "d_ob", nt=True)
    dw_o_dil = _matmul(ob_t, dyb, F32, 512, 1024, 1024, "dw_o_dil")
    do_a, dproj, ddnw = _gdr_out_bwd(o_a, proj, dn_norm_w, doa2)
    dqkv_a, dbg = _gdr_bwd(qkv, bg, u, w, vn, tmat, states, do_a)
    dproj, dpar = _gates_bwd(proj, gate_par, dbg, dproj)
    dproj, dconv = _conv_bwd(proj, c_pre, dqkv_a, conv_full, dproj)
    do_att, delta, dproj = _att_merge_bwd(o_att, proj, dob, dproj)
    dqkv_b = [_att_bwd(proj, g, do_att, lse, delta) for g in range(N_DIL)]
    pieces = [(OFF_Q_B + (N_DIL * i + g) * DIL_W, dqkv_b[g][i]) for i in range(3) for g in range(N_DIL)]
    for off, piece in pieces + [(OFF_G_A, dga), (OFF_G_B, dgb)]:
        dproj = lax.dynamic_update_slice(dproj, piece, (0, off))
    dwpt, dwpt_b = _matmul(h_t, dproj, F32, 1024, 1280, 2048, "dw_in", transpose_out=True, also_bf16=True)

    def finish(after=None):
        dh = _matmul(dproj, wpt, F32, 1024, 1024, 3840, "d_h", after=after)
        grad_x, dnw = _rms_in_bwd(x, norm_w, dh, dx2)
        small = jnp.zeros((8, PACK_W), F32)
        small = small.at[0].set(dnw[0]).at[1].set(dfw[0]).at[2, :DN_D].set(ddnw[0])
        small = small.at[3, :DN_HEADS].set(dpar[0, 8:16]).at[3, DN_HEADS:2 * DN_HEADS].set(dpar[1, 8:16])
        small = small.at[4, 0].set(lpart[0, 0])
        return grad_x, small

    return finish, (dwpt, dwpt_b), dconv, dw_o_dn, dw_o_dil, dw_out


def kernel(x, norm_w, w_in, conv_w, a_log, dt_bias, dn_norm_w, w_o_dn, w_o_dil, w_out, final_norm_w, loss_target, m_norm_w, m_w_in, m_conv_w, m_a_log, m_dt_bias, m_dn_norm_w, m_w_o_dn, m_w_o_dil, m_w_out, m_final_norm_w, v_norm_w, v_w_in, v_conv_w, v_a_log, v_dt_bias, v_dn_norm_w, v_w_o_dn, v_w_o_dil, v_w_out, v_final_norm_w):
    c = lax.axis_index("c")
    j = 2 * lax.axis_index("x") + lax.axis_index("y")
    qw = D_MODEL // N_CHIPS

    cw = conv_w[0].reshape(ROWS_CONV, PACK_W)
    cw = jnp.pad(cw, ((0, 16 - ROWS_CONV), (0, 0)))
    cw_hi = cw.astype(BF16)
    cw_lo = (cw - cw_hi.astype(F32)).astype(BF16)
    shard = w_in[0].T.astype(BF16)
    own_ba = jnp.where(j == 1, shard[BA_IN_SHARD1:BA_IN_SHARD1 + 2 * DN_HEADS], jnp.zeros((2 * DN_HEADS, D_MODEL), BF16))
    pack = jnp.concatenate(
        [w_o_dn[0].astype(BF16), w_o_dil[0].astype(BF16).reshape(ROWS_O_DIL, PACK_W), w_out[0].astype(BF16), cw_hi, cw_lo,
         own_ba, jnp.zeros((PACK_ROWS - R6, PACK_W), BF16)], axis=0).reshape(2, HALF_ROWS, PACK_W)
    chips = range(N_CHIPS)
    own_win = lax.switch(j, [functools.partial(_to_window, k) for k in chips], shard).reshape(2, SHARD_PAD // 2, D_MODEL)
    all_in, allw = _ag_weights([own_win, pack])
    wins = [all_in[k].reshape(SHARD_PAD, D_MODEL) for k in chips]
    allw = [allw[k].reshape(PACK_ROWS, PACK_W) for k in chips]
    wpt = _stack_windows(wins, allw[1][R5:R6])
    w_o_dn_full = jnp.concatenate([allw[k][:R1] for k in chips], axis=0)
    w_o_dil_full = jnp.concatenate([allw[k][R1:R2].reshape(DIL_W, qw) for k in chips], axis=1)
    w_out_full = jnp.concatenate([allw[k][R2:R3] for k in chips], axis=0)
    conv_full = jnp.concatenate(
        [(allw[k][R3:R3 + ROWS_CONV].astype(F32) + allw[k][R4:R4 + ROWS_CONV].astype(F32)).reshape(4, 3 * DN_W // N_CHIPS)
         for k in chips], axis=1)

    finish, (dwpt, dwpt_b), dconv, dw_o_dn, dw_o_dil, dw_out = _local_step(
        x[0], loss_target[0], norm_w, wpt, conv_full, a_log, dt_bias, dn_norm_w, w_o_dn_full, w_o_dil_full, w_out_full,
        final_norm_w.reshape(1, D_MODEL))

    cq = 3 * DN_W // N_CHIPS
    gpack = jnp.stack([
        jnp.concatenate(
            [dw_o_dn[k * qw:(k + 1) * qw], dw_o_dil[:, k * qw:(k + 1) * qw].reshape(ROWS_O_DIL, PACK_W),
             dw_out[k * qw:(k + 1) * qw],
             jnp.pad(dconv[:, k * cq:(k + 1) * cq].reshape(ROWS_CONV, PACK_W), ((0, 16 - ROWS_CONV), (0, 0))),
             dwpt[OFF_BA:OFF_BA + 2 * DN_HEADS] if k == 1 else jnp.zeros((2 * DN_HEADS, PACK_W), F32),
             jnp.zeros((PACK_ROWS - R4 - 2 * DN_HEADS, PACK_W), F32)], axis=0)
        for k in chips]).reshape(N_CHIPS, 2, HALF_ROWS, PACK_W)
    sib_in, sib_pack = _rs_pair(dwpt_b, gpack)
    csum_in = _add_halves_win(dwpt, sib_in, c)
    csum_pack = _add_halves(gpack, sib_pack, c, "add_halves_pack")
    send_sems, recv_sems, csums, lands, token = _rs_chips_start([csum_in, csum_pack])
    grad_x, small = finish(after=token)

    gs = _sum_blocks(_ag_small(small), 8, "sum_small")
    loss = gs[4, 0]
    w_small = jnp.zeros((8, PACK_W), F32)

    def pack_small(nw, fw, dnw_, al, db):
        t = w_small.at[0].set(nw[0]).at[1].set(fw).at[2, :DN_D].set(dnw_[0])
        return t.at[3, :DN_HEADS].set(al[0]).at[3, DN_HEADS:2 * DN_HEADS].set(db[0])

    sm = _adamw(pack_small(norm_w, final_norm_w, dn_norm_w, a_log, dt_bias), gs,
                pack_small(m_norm_w, m_final_norm_w, m_dn_norm_w, m_a_log, m_dt_bias),
                pack_small(v_norm_w, v_final_norm_w, v_dn_norm_w, v_a_log, v_dt_bias), "adamw_small")

    (csum_in, csum_pack), (src_in, src_pack) = _rs_chips_wait(send_sems, recv_sems, csums, lands, sm[0])
    half_in = _sum_chips(src_in, csum_in, j, "sum_chips_in")
    half_pack = _sum_chips(src_pack, csum_pack, j, "sum_chips_pack")
    sib_half_in, sib_half_pack = _pair_swap([half_in, half_pack])

    def both_halves(mine, theirs):
        return jnp.where(c == 0, jnp.concatenate([mine, theirs], axis=0), jnp.concatenate([theirs, mine], axis=0))

    g = both_halves(half_pack, sib_half_pack)
    g_w_in = lax.switch(j, [functools.partial(_from_window, k) for k in chips], both_halves(half_in, sib_half_in),
                        g[R4:R4 + 2 * DN_HEADS])
    g_w_o_dn = g[:R1]
    g_w_o_dil = g[R1:R2].reshape(DIL_W, qw)
    g_w_out = g[R2:R3]
    g_conv = g[R3:R3 + ROWS_CONV].reshape(4, cq)

    def unpack_small(t):
        return dict(norm_w=t[0:1], final_norm_w=t[1], dn_norm_w=t[2:3, :DN_D], a_log=t[3:4, :DN_HEADS],
                    dt_bias=t[3:4, DN_HEADS:2 * DN_HEADS])

    res = {"grad": unpack_small(gs)}
    for kind, arr in zip(("delta", "new_m", "new_v"), sm):
        res[kind] = unpack_small(arr)
    big = dict(conv_w=(conv_w, g_conv, m_conv_w, v_conv_w), w_o_dn=(w_o_dn, g_w_o_dn, m_w_o_dn, v_w_o_dn),
               w_o_dil=(w_o_dil, g_w_o_dil, m_w_o_dil, v_w_o_dil), w_out=(w_out, g_w_out, m_w_out, v_w_out))
    for name, (wt, gt, mt, vt) in big.items():
        d, nm, nv = _adamw(wt[0], gt, mt[0], vt[0], "adamw_" + name)
        res["grad"][name] = gt[None]
        res["delta"][name], res["new_m"][name], res["new_v"][name] = d[None], nm[None], nv[None]

    d, nm, nv = _adamw(w_in[0].T, g_w_in, m_w_in[0].T, v_w_in[0].T, "adamw_w_in")
    res["grad"]["w_in"] = g_w_in.T[None]
    res["delta"]["w_in"], res["new_m"]["w_in"], res["new_v"]["w_in"] = d.T[None], nm.T[None], nv.T[None]
    order = ["norm_w", "w_in", "conv_w", "a_log", "dt_bias", "dn_norm_w", "w_o_dn", "w_o_dil", "w_out", "final_norm_w"]
    outs = [loss, grad_x[None]]
    for kind in ("grad", "delta", "new_m", "new_v"):
        outs += [res[kind][nm] for nm in order]
    return tuple(outs)
```

```python
import functools
import math

import jax
import jax.numpy as jnp
from jax import lax
from jax.experimental import pallas as pl
from jax.experimental.pallas import tpu as pltpu

F32 = jnp.float32
BF16 = jnp.bfloat16
MESH = pl.DeviceIdType.MESH

D_MODEL = 1024
DN_HEADS = 8
DN_D = 128
DN_CHUNK = 64
DN_W = DN_HEADS * DN_D
DIL_GROUPS = ((128, 1), (512, 4), (2048, 16))
N_DIL = len(DIL_GROUPS)
DIL_HEADS = 4
DIL_DH = 128
DIL_W = DIL_HEADS * DIL_DH
ATT_BLOCK = 128
NORM_EPS = 1e-6
PROJ_W = 11280
N_CHIPS = 4
SHARD_W = PROJ_W // N_CHIPS

OFF_QKV_A = 0
OFF_Z_A = 3072
OFF_Q_B = 4096
OFF_K_B = 5632
OFF_V_B = 7168
OFF_Z_B = 8704
OFF_G_A = 9216
OFF_G_B = 10240
OFF_BA = 11264
PW = 11520
REF_OFF_BA = 4096

ADAM_LR = 0.001
ADAM_B1 = 0.9
ADAM_B2 = 0.999
ADAM_EPS = 1e-08
ADAM_WD = 0.01
ADAM_STEP = 10

ROW_TILE = 512
CONV_TILE = 1024
BIG_TILE = 1024
NEG = -1e30


def _dot(a, b):
    return jnp.dot(a.astype(BF16), b.astype(BF16), preferred_element_type=F32)


def _dot_nt(a, b):
    return lax.dot_general(a.astype(BF16), b.astype(BF16), (((1,), (1,)), ((), ())), preferred_element_type=F32)


def _dot_tn(a, b):
    return lax.dot_general(a.astype(BF16), b.astype(BF16), (((0,), (0,)), ((), ())), preferred_element_type=F32)


def _split(a):
    hi = a.astype(BF16)
    lo = (a - hi.astype(F32)).astype(BF16)
    return hi, lo


def _dot_exact_lhs(c, a):
    hi, lo = _split(a)
    cb = c.astype(BF16)
    return jnp.dot(cb, hi, preferred_element_type=F32) + jnp.dot(cb, lo, preferred_element_type=F32)


def _dot_tn_exact_rhs(a, c):
    hi, lo = _split(a)
    cb = c.astype(BF16)
    dn = (((0,), (0,)), ((), ()))
    return (lax.dot_general(hi, cb, dn, preferred_element_type=F32)
            + lax.dot_general(lo, cb, dn, preferred_element_type=F32))


def _sigmoid(x):
    return 1.0 / (1.0 + jnp.exp(-x))


def _silu(x):
    return x * _sigmoid(x)


def _silu_grad(x):
    s = _sigmoid(x)
    return s * (1.0 + x * (1.0 - s))


def _softplus(x):
    return jnp.maximum(x, 0.0) + jnp.log(1.0 + jnp.exp(-jnp.abs(x)))


def _cparams(*sem):
    return pltpu.CompilerParams(dimension_semantics=sem)


def _matmul(a, b, out_dtype, tm, tn, tk, name, nt=False, transpose_out=False, after=None, also_bf16=False):
    m, kdim = a.shape
    n = b.shape[0] if nt else b.shape[1]
    tm, tn, tk = min(tm, m), min(tn, n), min(tk, kdim)
    assert m % tm == 0 and n % tn == 0 and kdim % tk == 0, (name, a.shape, b.shape, tm, tn, tk)
    nk = kdim // tk
    dot = _dot_nt if nt else _dot
    b_spec = (pl.BlockSpec((tn, tk), lambda i, j, k: (j, k)) if nt else pl.BlockSpec((tk, tn), lambda i, j, k: (k, j)))
    extra = [] if after is None else [after]
    out_dtypes = [out_dtype] + ([BF16] if also_bf16 else [])

    def emit(o_refs, acc):
        val = acc.T if transpose_out else acc
        for o_ref in o_refs:
            o_ref[...] = val.astype(o_ref.dtype)

    def outs_of(rest):
        return rest[len(extra):len(extra) + len(out_dtypes)]

    if nk == 1:
        def body(a_ref, b_ref, *rest):
            emit(outs_of(rest), dot(a_ref[...], b_ref[...]))
        scratch = []
    else:
        def body(a_ref, b_ref, *rest):
            o_ref, acc_ref = outs_of(rest), rest[-1]
            k = pl.program_id(2)
            p = dot(a_ref[...], b_ref[...])

            @pl.when(k == 0)
            def _():
                acc_ref[...] = p

            @pl.when(k > 0)
            def _():
                acc_ref[...] += p

            @pl.when(k == nk - 1)
            def _():
                emit(o_ref, acc_ref[...])
        scratch = [pltpu.VMEM((tm, tn), F32)]

    if transpose_out:
        out_spec, out_shape = pl.BlockSpec((tn, tm), lambda i, j, k: (j, i)), (n, m)
    else:
        out_spec, out_shape = pl.BlockSpec((tm, tn), lambda i, j, k: (i, j)), (m, n)
    res = pl.pallas_call(
        body, name=name, grid=(m // tm, n // tn, nk),
        in_specs=[pl.BlockSpec((tm, tk), lambda i, j, k: (i, k)), b_spec] + [pl.BlockSpec(memory_space=pl.ANY)] * len(extra),
        out_specs=[out_spec] * len(out_dtypes), out_shape=[jax.ShapeDtypeStruct(out_shape, d) for d in out_dtypes],
        scratch_shapes=scratch, compiler_params=_cparams("parallel", "parallel", "arbitrary"))(a, b, *extra)
    return res if also_bf16 else res[0]


def _rms_in(x, nw):
    s, d = x.shape

    def body(x_ref, w_ref, h_ref, ht_ref):
        xv = x_ref[...]
        r = lax.rsqrt(jnp.mean(xv * xv, axis=-1, keepdims=True) + NORM_EPS)
        h = xv * r * w_ref[...]
        h_ref[...] = h.astype(BF16)
        ht_ref[...] = h.T.astype(BF16)

    return pl.pallas_call(
        body, name="rms_in", grid=(s // BIG_TILE,),
        in_specs=[pl.BlockSpec((BIG_TILE, d), lambda i: (i, 0)), pl.BlockSpec((1, d), lambda i: (0, 0))],
        out_specs=[pl.BlockSpec((BIG_TILE, d), lambda i: (i, 0)), pl.BlockSpec((d, BIG_TILE), lambda i: (0, i))],
        out_shape=[jax.ShapeDtypeStruct((s, d), BF16), jax.ShapeDtypeStruct((d, s), BF16)],
        compiler_params=_cparams("parallel"))(x, nw)


def _rms_in_bwd(x, nw, dh, dx2):
    s, d = x.shape

    def body(x_ref, w_ref, dh_ref, dx2_ref, dx_ref, dw_ref):
        i = pl.program_id(0)
        xv = x_ref[...]
        r = lax.rsqrt(jnp.mean(xv * xv, axis=-1, keepdims=True) + NORM_EPS)
        dhv = dh_ref[...]
        dyw = dhv * w_ref[...]
        dx_ref[...] = dx2_ref[...] + r * dyw - xv * (r * r * r) * jnp.mean(dyw * xv, axis=-1, keepdims=True)
        part = jnp.sum(dhv * xv * r, axis=0, keepdims=True)

        @pl.when(i == 0)
        def _():
            dw_ref[...] = part

        @pl.when(i > 0)
        def _():
            dw_ref[...] += part

    row = pl.BlockSpec((BIG_TILE, d), lambda i: (i, 0))
    vec = pl.BlockSpec((1, d), lambda i: (0, 0))
    return pl.pallas_call(
        body, name="rms_in_bwd", grid=(s // BIG_TILE,), in_specs=[row, vec, row, row], out_specs=[row, vec],
        out_shape=[jax.ShapeDtypeStruct((s, d), F32), jax.ShapeDtypeStruct((1, d), F32)],
        compiler_params=_cparams("arbitrary"))(x, nw, dh, dx2)


def _shift_down(cur, prev8, k):
    rc = pltpu.roll(cur, k, 0)
    rp = pltpu.roll(prev8, k, 0)
    row = lax.broadcasted_iota(jnp.int32, prev8.shape, 0)
    top = jnp.where(row < k, rp, rc[:8])
    return jnp.concatenate([top, rc[8:]], axis=0)


def _shift_up(cur, next8, k):
    t = cur.shape[0]
    rc = pltpu.roll(cur, t - k, 0)
    rn = pltpu.roll(next8, 8 - k, 0)
    row = lax.broadcasted_iota(jnp.int32, next8.shape, 0)
    bot = jnp.where(row >= 8 - k, rn, rc[t - 8:])
    return jnp.concatenate([rc[:t - 8], bot], axis=0)


def _conv_fwd(proj, conv_w):
    s = proj.shape[0]
    tile = min(s, CONV_TILE)
    t8 = tile // 8

    def body(u_ref, up_ref, w_ref, c_ref, y_ref):
        i = pl.program_id(0)
        part = pl.program_id(1)
        cur = u_ref[...]
        prev8 = jnp.where(i > 0, up_ref[...], 0.0)
        w = w_ref[...]
        c = cur * w[3:4, :]
        for k in (1, 2, 3):
            c = c + _shift_down(cur, prev8, k) * w[3 - k:4 - k, :]
        c_ref[...] = c
        a = _silu(c)
        for h in range(DN_HEADS):
            ah = a[:, h * DN_D:(h + 1) * DN_D]
            r = lax.rsqrt(jnp.sum(ah * ah, axis=-1, keepdims=True) + NORM_EPS)
            y_ref[:, h * DN_D:(h + 1) * DN_D] = jnp.where(part < 2, ah * r, ah)

    return pl.pallas_call(
        body, name="conv_fwd", grid=(s // tile, 3),
        in_specs=[pl.BlockSpec((tile, DN_W), lambda i, p: (i, p)),
                  pl.BlockSpec((8, DN_W), lambda i, p: (jnp.maximum(i * t8 - 1, 0), p)),
                  pl.BlockSpec((4, DN_W), lambda i, p: (0, p))],
        out_specs=[pl.BlockSpec((tile, DN_W), lambda i, p: (i, p))] * 2,
        out_shape=[jax.ShapeDtypeStruct((s, 3 * DN_W), F32)] * 2,
        compiler_params=_cparams("parallel", "parallel"))(proj, proj, conv_w)


def _act_bwd(cv, dyv, normalised):
    out = []
    for h in range(DN_HEADS):
        sl = slice(h * DN_D, (h + 1) * DN_D)
        ch, dyh = cv[:, sl], dyv[:, sl]
        ah = _silu(ch)
        r = lax.rsqrt(jnp.sum(ah * ah, axis=-1, keepdims=True) + NORM_EPS)
        dn = r * dyh - ah * (r * r * r) * jnp.sum(dyh * ah, axis=-1, keepdims=True)
        out.append(jnp.where(normalised, dn, dyh) * _silu_grad(ch))
    return jnp.concatenate(out, axis=1)


DPROJ_IN = pl.BlockSpec(memory_space=pl.ANY)


def _conv_bwd(proj, c_pre, dqkv, conv_w, dproj):
    s = proj.shape[0]
    tile = min(s, CONV_TILE)
    t8 = tile // 8
    nrow = s // tile
    last8 = s // 8 - 1

    def body(u_ref, c_ref, cn_ref, dy_ref, dyn_ref, w_ref, dproj_in, du_ref, dw_ref):
        i = pl.program_id(1)
        normalised = pl.program_id(0) < 2
        cur = u_ref[...]
        dcv = _act_bwd(c_ref[...], dy_ref[...], normalised)
        next8 = jnp.where(i < nrow - 1, _act_bwd(cn_ref[...], dyn_ref[...], normalised), 0.0)
        w = w_ref[...]

        @pl.when(i == 0)
        def _():
            dw_ref[...] = jnp.zeros_like(dw_ref)

        du = dcv * w[3:4, :]
        dw_ref[3:4, :] += jnp.sum(cur * dcv, axis=0, keepdims=True)
        for k in (1, 2, 3):
            ahead = _shift_up(dcv, next8, k)
            du = du + ahead * w[3 - k:4 - k, :]
            dw_ref[3 - k:4 - k, :] += jnp.sum(cur * ahead, axis=0, keepdims=True)
        du_ref[...] = du.astype(BF16)

    blk = pl.BlockSpec((tile, DN_W), lambda p, i: (i, p))
    nxt = pl.BlockSpec((8, DN_W), lambda p, i: (jnp.minimum((i + 1) * t8, last8), p))
    return pl.pallas_call(
        body, name="conv_bwd", grid=(3, nrow),
        in_specs=[blk, blk, nxt, blk, nxt, pl.BlockSpec((4, DN_W), lambda p, i: (0, p)), DPROJ_IN],
        out_specs=[blk, pl.BlockSpec((4, DN_W), lambda p, i: (0, p))],
        out_shape=[jax.ShapeDtypeStruct((s, PW), BF16), jax.ShapeDtypeStruct((4, 3 * DN_W), F32)],
        input_output_aliases={6: 0},
        compiler_params=_cparams("parallel", "arbitrary"))(proj, c_pre, c_pre, dqkv, dqkv, conv_w, dproj)


def _gates_fwd(proj, gate_par):
    s = proj.shape[0]

    def body(ba_ref, par_ref, o_ref):
        v = ba_ref[...]
        lane = lax.broadcasted_iota(jnp.int32, v.shape, 1)
        beta = _sigmoid(v)
        g = -jnp.exp(par_ref[0:1, :]) * _softplus(v + par_ref[1:2, :])
        o_ref[...] = jnp.where(lane < DN_HEADS, beta, jnp.where(lane < 2 * DN_HEADS, g, 0.0))

    return pl.pallas_call(
        body, name="gates_fwd", grid=(s // ROW_TILE,),
        in_specs=[pl.BlockSpec((ROW_TILE, 128), lambda i: (i, OFF_BA // 128)), pl.BlockSpec((8, 128), lambda i: (0, 0))],
        out_specs=pl.BlockSpec((ROW_TILE, 128), lambda i: (i, 0)),
        out_shape=jax.ShapeDtypeStruct((s, 128), F32), compiler_params=_cparams("parallel"))(proj, gate_par)


def _gates_bwd(proj, gate_par, dbg, dproj):
    s = proj.shape[0]

    def body(ba_ref, par_ref, d_ref, dproj_in, o_ref, dpar_ref):
        i = pl.program_id(0)
        v = ba_ref[...]
        dv = d_ref[...]
        lane = lax.broadcasted_iota(jnp.int32, v.shape, 1)
        beta = _sigmoid(v)
        nega = -jnp.exp(par_ref[0:1, :])
        xs = v + par_ref[1:2, :]
        dsp = dv * nega * _sigmoid(xs)
        dal = dv * nega * _softplus(xs)
        is_b = lane < DN_HEADS
        is_g = jnp.logical_and(lane >= DN_HEADS, lane < 2 * DN_HEADS)
        o_ref[:, :128] = jnp.where(is_b, dv * beta * (1.0 - beta), jnp.where(is_g, dsp, 0.0)).astype(BF16)
        o_ref[:, 128:] = jnp.zeros((ROW_TILE, PW - OFF_BA - 128), BF16)
        r0 = jnp.sum(jnp.where(is_g, dal, 0.0), axis=0, keepdims=True)
        r1 = jnp.sum(jnp.where(is_g, dsp, 0.0), axis=0, keepdims=True)

        @pl.when(i == 0)
        def _():
            dpar_ref[...] = jnp.zeros_like(dpar_ref)

        dpar_ref[0:1, :] += r0
        dpar_ref[1:2, :] += r1

    return pl.pallas_call(
        body, name="gates_bwd", grid=(s // ROW_TILE,),
        in_specs=[pl.BlockSpec((ROW_TILE, 128), lambda i: (i, OFF_BA // 128)), pl.BlockSpec((8, 128), lambda i: (0, 0)),
                  pl.BlockSpec((ROW_TILE, 128), lambda i: (i, 0)), DPROJ_IN],
        out_specs=[pl.BlockSpec((ROW_TILE, PW - OFF_BA), lambda i: (i, OFF_BA // (PW - OFF_BA))),
                   pl.BlockSpec((8, 128), lambda i: (0, 0))],
        out_shape=[jax.ShapeDtypeStruct((s, PW), BF16), jax.ShapeDtypeStruct((8, 128), F32)],
        input_output_aliases={3: 0},
        compiler_params=_cparams("arbitrary"))(proj, gate_par, dbg, dproj)


def _chunk_masks():
    c = DN_CHUNK
    ii = lax.broadcasted_iota(jnp.int32, (c, c), 0)
    jj = lax.broadcasted_iota(jnp.int32, (c, c), 1)
    return dict(ii=ii, jj=jj, lower=(ii >= jj), strict=(ii > jj),
                lower_f=(ii >= jj).astype(BF16), upper_f=(ii <= jj).astype(BF16))


class _Heads:
    def __init__(self, xs):
        self.xs = list(xs)

    def _bin(self, o, f):
        if isinstance(o, _Heads):
            return _Heads([f(a, b) for a, b in zip(self.xs, o.xs)])
        return _Heads([f(a, o) for a in self.xs])

    def __add__(self, o):
        return self._bin(o, lambda a, b: a + b)

    def __sub__(self, o):
        return self._bin(o, lambda a, b: a - b)

    def __mul__(self, o):
        return self._bin(o, lambda a, b: a * b)

    __radd__ = __add__
    __rmul__ = __mul__

    def __neg__(self):
        return _Heads([-a for a in self.xs])

    def __getitem__(self, i):
        return _Heads([a[i] for a in self.xs])


def _hmap(f, *args):
    n = next(len(a.xs) for a in args if isinstance(a, _Heads))
    return _Heads([f(*[(a.xs[h] if isinstance(a, _Heads) else a) for a in args]) for h in range(n)])


def _hdot(a, b):
    return _hmap(_dot, a, b)


def _hdot_nt(a, b):
    return _hmap(_dot_nt, a, b)


def _hdot_tn(a, b):
    return _hmap(_dot_tn, a, b)


def _hcat(a, b, axis):
    return _hmap(lambda x, y: jnp.concatenate([x, y], axis=axis), a, b)


def _hsum(a, axis):
    return _hmap(lambda t: jnp.sum(t, axis=axis, keepdims=True), a)


def _hwhere(c, a, b):
    return _hmap(jnp.where, c, a, b)


def _chunk_gates(mk, bg):
    c = DN_CHUNK
    gc_all = _dot_exact_lhs(mk["lower_f"], bg)
    rows = jnp.concatenate([gc_all, gc_all], axis=0).T
    hs = range(DN_HEADS)
    return (_Heads(bg[:, h:h + 1] for h in hs), _Heads(gc_all[:, DN_HEADS + h:DN_HEADS + h + 1] for h in hs),
            _Heads(rows[DN_HEADS + h:DN_HEADS + h + 1, :] for h in hs))


def _chunk_common(mk, q, k, beta_col, gc_col, gc_r):
    c = DN_CHUNK
    lower, strict = mk["lower"], mk["strict"]
    qs = q * (DN_D ** -0.5)
    beta_b = _hmap(lambda t: jnp.broadcast_to(t, (c, DN_D)), beta_col)
    gc_b = _hmap(lambda t: jnp.broadcast_to(t, (c, DN_D)), gc_col)
    gc_sq = gc_b[:, :c]
    gam = _hwhere(lower, _hmap(lambda t: jnp.exp(jnp.minimum(t, 0.0)), gc_sq - gc_r[:, :c]), 0.0)
    egc = _hmap(jnp.exp, gc_b)
    gl = gc_b[c - 1:c, :]
    ekd = _hmap(jnp.exp, gl - gc_b)
    dl = _hmap(jnp.exp, gl)
    kb = k * beta_b
    scores = _hdot_nt(_hcat(kb, qs, 0), k)
    a_strict = _hwhere(strict, scores[:c] * gam, 0.0)
    aqk = _hwhere(lower, scores[c:] * gam, 0.0)
    return dict(k=k, qs=qs, beta_b=beta_b, gc_b=gc_b, gam=gam, egc=egc, ekd=ekd, dl=dl, kb=kb, a_strict=a_strict, aqk=aqk)


def _unit_lower_inverse_minus_eye(n_strict, ii, jj):
    same = lax.shift_right_logical(ii, 4) == lax.shift_right_logical(jj, 4)
    dmat = _hwhere(same, n_strict, 0.0)
    omat = n_strict - dmat
    d2 = _hdot(dmat, dmat)
    d4 = _hdot(d2, d2)
    d8 = _hdot(d4, d4)
    x1 = d2 - dmat - _hdot(dmat, d2)
    x2 = x1 + d4 + _hdot(x1, d4)
    x3 = x2 + d8 + _hdot(x2, d8)
    n1 = omat + _hdot(x3, omat)
    n2 = _hdot(n1, n1)
    y = n2 - n1 - _hdot(n1, n2)
    return y + x3 + _hdot(y, x3)


GDR_HEAD_SETS = (range(0, DN_HEADS),)


def _gdr_fwd(qkv, bg):
    s = qkv.shape[0]
    c = DN_CHUNK
    n = s // c

    def body(q_ref, k_ref, v_ref, bg_ref, o_ref, u_ref, w_ref, vn_ref, tm_ref, st_ref, state):
        @pl.when(pl.program_id(0) == 0)
        def _():
            state[...] = jnp.zeros_like(state)

        mk = _chunk_masks()
        gates = _chunk_gates(mk, bg_ref[...])
        for hs in GDR_HEAD_SETS:
            sls = [slice(h * DN_D, (h + 1) * DN_D) for h in hs]
            cm = _chunk_common(mk, _Heads(q_ref[:, sl] for sl in sls), _Heads(k_ref[:, sl] for sl in sls),
                               *[_Heads(g.xs[h] for h in hs) for g in gates])
            tm = _unit_lower_inverse_minus_eye(cm["a_strict"], mk["ii"], mk["jj"])
            rhs_u = _Heads(v_ref[:, sl] for sl in sls) * cm["beta_b"]
            rhs_w = cm["kb"] * cm["egc"]
            t_rhs = _hdot(tm, _hcat(rhs_u, rhs_w, 1))
            u = rhs_u + t_rhs[:, :DN_D]
            w = rhs_w + t_rhs[:, DN_D:]
            st = _Heads(state[h] for h in hs)
            on_state = _hdot(_hcat(w, cm["qs"] * cm["egc"], 0), st)
            v_new = u - on_state[:c]
            o = on_state[c:] + _hdot(cm["aqk"], v_new)
            st_new = st * cm["dl"] + _hdot_tn(cm["k"] * cm["ekd"], v_new)
            for i, (h, sl) in enumerate(zip(hs, sls)):
                o_ref[:, sl] = o.xs[i]
                u_ref[:, sl] = u.xs[i]
                w_ref[:, sl] = w.xs[i]
                vn_ref[:, sl] = v_new.xs[i]
                tm_ref[h, 0] = tm.xs[i]
                st_ref[h, 0] = st.xs[i]
                state[h] = st_new.xs[i]

    def part(p):
        return pl.BlockSpec((c, DN_W), lambda j: (j, p))

    return pl.pallas_call(
        body, name="gdr_fwd", grid=(n,),
        in_specs=[part(0), part(1), part(2), pl.BlockSpec((c, 128), lambda j: (j, 0))],
        out_specs=[part(0)] * 4 + [pl.BlockSpec((DN_HEADS, 1, c, c), lambda j: (0, j, 0, 0)),
                                   pl.BlockSpec((DN_HEADS, 1, DN_D, DN_D), lambda j: (0, j, 0, 0))],
        out_shape=[jax.ShapeDtypeStruct((s, DN_W), F32)] * 4
        + [jax.ShapeDtypeStruct((DN_HEADS, n, c, c), F32), jax.ShapeDtypeStruct((DN_HEADS, n, DN_D, DN_D), F32)],
        scratch_shapes=[pltpu.VMEM((DN_HEADS, DN_D, DN_D), F32)],
        compiler_params=_cparams("arbitrary"))(qkv, qkv, qkv, bg)


def _gdr_bwd(qkv, bg, u, w, vn, tmat, states, do):
    s = qkv.shape[0]
    c = DN_CHUNK
    n = s // c

    def body(q_ref, k_ref, v_ref, bg_ref, u_ref, w_ref, vn_ref, tm_ref, st_ref, do_ref,
             dqkv_ref, dbg_ref, dstate):
        @pl.when(pl.program_id(0) == 0)
        def _():
            dstate[...] = jnp.zeros_like(dstate)

        mk = _chunk_masks()
        lower, strict = mk["lower"], mk["strict"]
        bg = bg_ref[...]
        ones = jnp.ones((c, DN_D), BF16)
        rowi = lax.broadcasted_iota(jnp.int32, (c, DN_D), 0)
        lane = lax.broadcasted_iota(jnp.int32, (c, 128), 1)
        hs = range(DN_HEADS)
        sls = [slice(h * DN_D, (h + 1) * DN_D) for h in hs]

        def heads_of(ref):
            return _Heads(ref[:, sl] for sl in sls)

        cm = _chunk_common(mk, heads_of(q_ref), heads_of(k_ref), *_chunk_gates(mk, bg))
        k, qs, beta_b = cm["k"], cm["qs"], cm["beta_b"]
        gam, egc, ekd, dl, kb = cm["gam"], cm["egc"], cm["ekd"], cm["dl"], cm["kb"]
        aqk, a_strict = cm["aqk"], cm["a_strict"]
        v, uu, ww, v_new, dov = heads_of(v_ref), heads_of(u_ref), heads_of(w_ref), heads_of(vn_ref), heads_of(do_ref)
        st = _Heads(st_ref[h, 0] for h in hs)
        dsn = _Heads(dstate[h] for h in hs)
        qd = qs * egc
        kd = k * ekd

        dv_new = _hdot_tn(aqk, dov) + _hdot(kd, dsn)
        do_sv = _hdot_nt(dov, _hcat(st, v_new, 0))
        dqd = do_sv[:, :DN_D]
        daqk = _hwhere(lower, do_sv[:, DN_D:], 0.0)
        dkd = _hdot_nt(v_new, dsn)
        ddl = _hsum(_hsum(dsn * st, 1), 0)
        dw = -_hdot_nt(dv_new, st)
        ds_new = dsn * dl + _hdot_tn(_hcat(qd, -ww, 0), _hcat(dov, dv_new, 0))

        tm = _Heads(tm_ref[h, 0] for h in hs)
        tt = _hdot_tn(tm, _hcat(dv_new, dw, 1))
        dru = dv_new + tt[:, :DN_D]
        drw = dw + tt[:, DN_D:]
        dn = _hwhere(strict, -_hdot_nt(_hcat(dru, drw, 1), _hcat(uu, ww, 1)), 0.0)
        dag = dn * gam
        dqg = daqk * gam
        both = _hcat(dag, dqg, 0)
        on_k = _hdot(both, k)
        dkb = on_k[:c] + drw * egc
        dqs = on_k[c:] + dqd * egc
        dk = _hdot_tn(both, _hcat(kb, qs, 0)) + dkb * beta_b + dkd * ekd
        pmat = dn * a_strict + daqk * aqk
        tkd = _hsum(dkd * kd, -1)
        dgc = (_hsum(pmat, -1) - _hmap(_dot_tn_exact_rhs, pmat, ones) + _hsum(drw * (kb * egc), -1)
               + _hsum(dqd * qd, -1) - tkd)
        last = _hsum(tkd, 0) + ddl * dl
        dgc = dgc + _hwhere(rowi == c - 1, last, 0.0)
        dbeta = _hsum(dru * v, -1) + _hsum(dkb * k, -1)
        dq = dqs * (DN_D ** -0.5)
        dv = dru * beta_b

        dgc_all = jnp.zeros((c, 128), F32)
        dbg = jnp.zeros((c, 128), F32)
        for h, sl in zip(hs, sls):
            dqkv_ref[:, sl] = dq.xs[h]
            dqkv_ref[:, DN_W + h * DN_D:DN_W + (h + 1) * DN_D] = dk.xs[h]
            dqkv_ref[:, 2 * DN_W + h * DN_D:2 * DN_W + (h + 1) * DN_D] = dv.xs[h]
            dstate[h] = ds_new.xs[h]
            dgc_all = dgc_all + jnp.where(lane == DN_HEADS + h, dgc.xs[h], 0.0)
            dbg = dbg + jnp.where(lane == h, dbeta.xs[h], 0.0)
        dbg_ref[...] = dbg + _dot_exact_lhs(mk["upper_f"], dgc_all)

    def part(p):
        return pl.BlockSpec((c, DN_W), lambda j: (n - 1 - j, p))

    vec = pl.BlockSpec((c, 128), lambda j: (n - 1 - j, 0))
    return pl.pallas_call(
        body, name="gdr_bwd", grid=(n,),
        in_specs=[part(0), part(1), part(2), vec, part(0), part(0), part(0),
                  pl.BlockSpec((DN_HEADS, 1, c, c), lambda j: (0, n - 1 - j, 0, 0)),
                  pl.BlockSpec((DN_HEADS, 1, DN_D, DN_D), lambda j: (0, n - 1 - j, 0, 0)), part(0)],
        out_specs=[pl.BlockSpec((c, 3 * DN_W), lambda j: (n - 1 - j, 0)), vec],
        out_shape=[jax.ShapeDtypeStruct((s, 3 * DN_W), F32), jax.ShapeDtypeStruct((s, 128), F32)],
        scratch_shapes=[pltpu.VMEM((DN_HEADS, DN_D, DN_D), F32)],
        compiler_params=_cparams("arbitrary"))(qkv, qkv, qkv, bg, u, w, vn, tmat, states, do)


def _gdr_out(o, proj, dnw):
    s = o.shape[0]

    def body(o_ref, z_ref, w_ref, y_ref, yt_ref):
        ov, zv, wv = o_ref[...], z_ref[...], w_ref[...]
        for h in range(DN_HEADS):
            sl = slice(h * DN_D, (h + 1) * DN_D)
            oh = ov[:, sl]
            r = lax.rsqrt(jnp.mean(oh * oh, axis=-1, keepdims=True) + NORM_EPS)
            y = (oh * r * wv) * _silu(zv[:, sl])
            y_ref[:, sl] = y.astype(BF16)
            yt_ref[sl, :] = y.T.astype(BF16)

    row = pl.BlockSpec((BIG_TILE, DN_W), lambda i: (i, 0))
    return pl.pallas_call(
        body, name="gdr_out", grid=(s // BIG_TILE,),
        in_specs=[row, pl.BlockSpec((BIG_TILE, DN_W), lambda i: (i, OFF_Z_A // DN_W)), pl.BlockSpec((1, DN_D), lambda i: (0, 0))],
        out_specs=[row, pl.BlockSpec((DN_W, BIG_TILE), lambda i: (0, i))],
        out_shape=[jax.ShapeDtypeStruct((s, DN_W), BF16), jax.ShapeDtypeStruct((DN_W, s), BF16)],
        compiler_params=_cparams("parallel"))(o, proj, dnw)


def _gdr_out_bwd(o, proj, dnw, dya, w_o_dn):
    s = o.shape[0]

    def body(o_ref, z_ref, w_ref, dy_ref, wo_ref, do_ref, dz_ref, dw_ref):
        i = pl.program_id(0)
        ov, zv, wv = o_ref[...], z_ref[...], w_ref[...]
        dyv = _dot_nt(dy_ref[...], wo_ref[...])
        acc = jnp.zeros((1, DN_D), F32)
        for h in range(DN_HEADS):
            sl = slice(h * DN_D, (h + 1) * DN_D)
            oh, zh, dh = ov[:, sl], zv[:, sl], dyv[:, sl]
            r = lax.rsqrt(jnp.mean(oh * oh, axis=-1, keepdims=True) + NORM_EPS)
            dn = dh * _silu(zh)
            dz_ref[:, sl] = (dh * (oh * r * wv) * _silu_grad(zh)).astype(BF16)
            acc = acc + jnp.sum(dn * oh * r, axis=0, keepdims=True)
            dnw_ = dn * wv
            do_ref[:, sl] = r * dnw_ - oh * (r * r * r) * jnp.mean(dnw_ * oh, axis=-1, keepdims=True)

        @pl.when(i == 0)
        def _():
            dw_ref[...] = acc

        @pl.when(i > 0)
        def _():
            dw_ref[...] += acc

    row = pl.BlockSpec((ROW_TILE, DN_W), lambda i: (i, 0))
    vec = pl.BlockSpec((1, DN_D), lambda i: (0, 0))
    return pl.pallas_call(
        body, name="gdr_out_bwd", grid=(s // ROW_TILE,),
        in_specs=[row, pl.BlockSpec((ROW_TILE, DN_W), lambda i: (i, OFF_Z_A // DN_W)), vec, row,
                  pl.BlockSpec(w_o_dn.shape, lambda i: (0, 0))],
        out_specs=[row, pl.BlockSpec((ROW_TILE, DN_W), lambda i: (i, OFF_Z_A // DN_W)), vec],
        out_shape=[jax.ShapeDtypeStruct((s, DN_W), F32), jax.ShapeDtypeStruct((s, PW), BF16),
                   jax.ShapeDtypeStruct((1, DN_D), F32)],
        compiler_params=_cparams("arbitrary"))(o, proj, dnw, dya, w_o_dn)


def _slope(group, head):
    idx = (group * DIL_HEADS + head + 1).astype(F32)
    return jnp.exp(jnp.full((1, 128), -8.0 * math.log(2.0) / (N_DIL * DIL_HEADS), F32) * idx)


def _att_scores(qb, k_cur, k_prev, slope_d, has_prev):
    iq = lax.broadcasted_iota(jnp.int32, (ATT_BLOCK, ATT_BLOCK), 0)
    jk = lax.broadcasted_iota(jnp.int32, (ATT_BLOCK, ATT_BLOCK), 1)
    dist_c = (iq - jk).astype(F32)
    s_cur = jnp.where(iq >= jk, _dot_nt(qb, k_cur) - slope_d * dist_c, NEG)
    s_prev = jnp.where(jnp.logical_and(jk >= iq, has_prev),
                       _dot_nt(qb, k_prev) - slope_d * (dist_c + float(ATT_BLOCK)), NEG)
    return s_cur, s_prev


def _att_scores_whole(qb, k, slope_d):
    n = 2 * ATT_BLOCK
    dist = lax.broadcasted_iota(jnp.int32, (n, n), 0) - lax.broadcasted_iota(jnp.int32, (n, n), 1)
    valid = jnp.logical_and(dist >= 0, dist <= ATT_BLOCK)
    return jnp.where(valid, _dot_nt(qb, k) - slope_d[:, 0:1] * dist.astype(F32), NEG)


def _att_tiles(i, dil, nb):
    tiles = nb // 2
    per = dil * tiles // ATT_UNROLL
    assert nb % 2 == 0 and tiles >= 2 and per * ATT_UNROLL == dil * tiles
    for i0 in range(per):
        ts = [divmod(i0 + u * per, tiles) for u in range(ATT_UNROLL)]
        assert all(a[0] != b[0] or abs(a[1] - b[1]) >= 2 for n, a in enumerate(ts) for b in ts[n + 1:])
    qrows, krows, has_prev = [], [], []
    for u in range(ATT_UNROLL):
        t = i + u * per
        r = lax.div(t, tiles)
        j = lax.rem(t, tiles)
        qbase = r + dil * 2 * ATT_BLOCK * j
        kbase = qbase - dil * ATT_BLOCK * jnp.minimum(j, 1)
        if dil == 1:
            qbase, kbase = pl.multiple_of(qbase, ATT_BLOCK), pl.multiple_of(kbase, ATT_BLOCK)
        qrows.append(pl.ds(qbase, 2 * ATT_BLOCK, stride=dil))
        krows.append(pl.ds(kbase, 3 * ATT_BLOCK, stride=dil))
        has_prev.append(j > 0)
    return qrows, krows, has_prev


def _att_scores_tile(qb, k, slope_d, has_prev):
    iq = lax.broadcasted_iota(jnp.int32, (2 * ATT_BLOCK, 3 * ATT_BLOCK), 0)
    ck = lax.broadcasted_iota(jnp.int32, (2 * ATT_BLOCK, 3 * ATT_BLOCK), 1)
    dist = iq - ck + jnp.where(has_prev, ATT_BLOCK, 0)
    valid = jnp.logical_and(dist >= 0, dist <= ATT_BLOCK)
    return jnp.where(valid, _dot_nt(qb, k) - slope_d[:, 0:1] * dist.astype(F32), NEG)


ATT_UNROLL = 4


def _att_blocks(i, dil, nb):
    per = dil * nb // ATT_UNROLL
    assert per * ATT_UNROLL == dil * nb
    for i0 in range(per):
        blocks = [divmod(i0 + u * per, nb) for u in range(ATT_UNROLL)]
        assert all(a[0] != b[0] or abs(a[1] - b[1]) >= 2 for n, a in enumerate(blocks) for b in blocks[n + 1:])
    curs, prvs, has_prev = [], [], []
    for u in range(ATT_UNROLL):
        t = i + u * per
        r = lax.div(t, nb)
        j = lax.rem(t, nb)
        base = r + dil * ATT_BLOCK * j
        pbase = base - dil * ATT_BLOCK * jnp.minimum(j, 1)
        if dil == 1:
            base, pbase = pl.multiple_of(base, ATT_BLOCK), pl.multiple_of(pbase, ATT_BLOCK)
        curs.append(pl.ds(base, ATT_BLOCK, stride=dil))
        prvs.append(pl.ds(pbase, ATT_BLOCK, stride=dil))
        has_prev.append(j > 0)
    return curs, prvs, has_prev


def _att_fwd(proj, group):
    s = proj.shape[0]
    dil = DIL_GROUPS[group][1]
    assert DIL_GROUPS[group][0] // dil == ATT_BLOCK
    nb = s // dil // ATT_BLOCK
    assert nb * dil * ATT_BLOCK == s

    def body(q_ref, k_ref, v_ref, o_ref, lse_ref):
        def emit(rows, num, den, mx):
            o_ref[rows, :] = num / den
            lse_ref[rows, :] = jnp.broadcast_to(mx + jnp.log(den), (num.shape[0], DIL_DH))

        slope_d = _slope(group, pl.program_id(0)) * float(dil)

        def step(i, carry):
            curs, prvs, has_prev = _att_blocks(i, dil, nb)
            us = range(ATT_UNROLL)
            qb = [q_ref[c, :] * (DIL_DH ** -0.5) for c in curs]
            sc = [_att_scores(qb[u], k_ref[curs[u], :], k_ref[prvs[u], :], slope_d, has_prev[u]) for u in us]
            mx = [jnp.maximum(jnp.max(a, axis=-1, keepdims=True), jnp.max(b, axis=-1, keepdims=True)) for a, b in sc]
            p_cur = [jnp.exp(sc[u][0] - mx[u]) for u in us]
            p_prev = [jnp.exp(sc[u][1] - mx[u]) for u in us]
            den = [jnp.sum(p_cur[u], axis=-1, keepdims=True) + jnp.sum(p_prev[u], axis=-1, keepdims=True) for u in us]
            num = [_dot(p_cur[u], v_ref[curs[u], :]) + _dot(p_prev[u], v_ref[prvs[u], :]) for u in us]
            for u in us:
                emit(curs[u], num[u], den[u], mx[u])
            return carry

        def step_whole(i, carry):
            rows = [pl.ds(i * ATT_UNROLL + u, 2 * ATT_BLOCK, stride=dil) for u in range(ATT_UNROLL)]
            sc = [_att_scores_whole(q_ref[r, :] * (DIL_DH ** -0.5), k_ref[r, :], slope_d) for r in rows]
            mx = [jnp.max(a, axis=-1, keepdims=True) for a in sc]
            p = [jnp.exp(a - m) for a, m in zip(sc, mx)]
            num = [_dot(pu, v_ref[r, :]) for pu, r in zip(p, rows)]
            for u, r in enumerate(rows):
                emit(r, num[u], jnp.sum(p[u], axis=-1, keepdims=True), mx[u])
            return carry

        def step_tile(i, carry):
            qrows, krows, has_prev = _att_tiles(i, dil, nb)
            us = range(ATT_UNROLL)
            sc = [_att_scores_tile(q_ref[qrows[u], :] * (DIL_DH ** -0.5), k_ref[krows[u], :], slope_d, has_prev[u]) for u in us]
            mx = [jnp.max(a, axis=-1, keepdims=True) for a in sc]
            p = [jnp.exp(a - m) for a, m in zip(sc, mx)]
            num = [_dot(p[u], v_ref[krows[u], :]) for u in us]
            for u in us:
                emit(qrows[u], num[u], jnp.sum(p[u], axis=-1, keepdims=True), mx[u])
            return carry

        if nb == 2:
            lax.fori_loop(0, dil // ATT_UNROLL, step_whole, 0)
        elif nb % 2 == 0:
            lax.fori_loop(0, dil * nb // 2 // ATT_UNROLL, step_tile, 0)
        else:
            lax.fori_loop(0, dil * nb // ATT_UNROLL, step, 0)

    def col(off):
        return pl.BlockSpec((s, DIL_DH), lambda h: (0, off // DIL_DH + group * DIL_HEADS + h))

    out = pl.BlockSpec((s, DIL_DH), lambda h: (0, h))
    return pl.pallas_call(
        body, name=f"att_fwd{group}", grid=(DIL_HEADS,), in_specs=[col(OFF_Q_B), col(OFF_K_B), col(OFF_V_B)],
        out_specs=[out, out], out_shape=[jax.ShapeDtypeStruct((s, DIL_W), F32)] * 2,
        compiler_params=_cparams("parallel"))(proj, proj, proj)


def _att_bwd(proj, group, do, lse, delta):
    s = proj.shape[0]
    dil = DIL_GROUPS[group][1]
    nb = s // dil // ATT_BLOCK

    def body(q_ref, k_ref, v_ref, do_ref, lse_ref, dl_ref, dq_ref, dk_ref, dv_ref, dq_acc, dk_acc, dv_acc):
        slope_d = _slope(group, pl.program_id(0)) * float(dil)
        dk_acc[...] = jnp.zeros_like(dk_acc)
        dv_acc[...] = jnp.zeros_like(dv_acc)

        def step(i, carry):
            curs, prvs, has_prev = _att_blocks(i, dil, nb)
            us = range(ATT_UNROLL)
            qb = [q_ref[c, :] * (DIL_DH ** -0.5) for c in curs]
            k_cur, k_prev = [k_ref[c, :] for c in curs], [k_ref[p, :] for p in prvs]
            v_cur, v_prev = [v_ref[c, :] for c in curs], [v_ref[p, :] for p in prvs]
            sc = [_att_scores(qb[u], k_cur[u], k_prev[u], slope_d, has_prev[u]) for u in us]
            lse_b, delta_b, dob = [lse_ref[c, :] for c in curs], [dl_ref[c, :] for c in curs], [do_ref[c, :] for c in curs]
            p_cur = [jnp.exp(sc[u][0] - lse_b[u]) for u in us]
            p_prev = [jnp.exp(sc[u][1] - lse_b[u]) for u in us]
            ds_cur = [p_cur[u] * (_dot_nt(dob[u], v_cur[u]) - delta_b[u]) for u in us]
            ds_prev = [p_prev[u] * (_dot_nt(dob[u], v_prev[u]) - delta_b[u]) for u in us]
            dq = [(_dot(ds_cur[u], k_cur[u]) + _dot(ds_prev[u], k_prev[u])) * (DIL_DH ** -0.5) for u in us]
            dk_c = [_dot_tn(ds_cur[u], qb[u]) for u in us]
            dv_c = [_dot_tn(p_cur[u], dob[u]) for u in us]
            dk_p = [_dot_tn(ds_prev[u], qb[u]) for u in us]
            dv_p = [_dot_tn(p_prev[u], dob[u]) for u in us]
            for u in us:
                dq_acc[curs[u], :] = dq[u]
                dk_acc[curs[u], :] += dk_c[u]
                dv_acc[curs[u], :] += dv_c[u]
            for u in us:
                dk_acc[prvs[u], :] += dk_p[u]
                dv_acc[prvs[u], :] += dv_p[u]
            return carry

        def step_whole(i, carry):
            rows = [pl.ds(i * ATT_UNROLL + u, 2 * ATT_BLOCK, stride=dil) for u in range(ATT_UNROLL)]
            qb = [q_ref[r, :] * (DIL_DH ** -0.5) for r in rows]
            kk, vv, dob = [k_ref[r, :] for r in rows], [v_ref[r, :] for r in rows], [do_ref[r, :] for r in rows]
            sc = [_att_scores_whole(qb[u], kk[u], slope_d) for u in range(ATT_UNROLL)]
            p = [jnp.exp(sc[u] - lse_ref[r, :][:, 0:1]) for u, r in enumerate(rows)]
            ds = [p[u] * (_dot_nt(dob[u], vv[u]) - dl_ref[r, :][:, 0:1]) for u, r in enumerate(rows)]
            dq = [_dot(ds[u], kk[u]) * (DIL_DH ** -0.5) for u in range(ATT_UNROLL)]
            dk = [_dot_tn(ds[u], qb[u]) for u in range(ATT_UNROLL)]
            dv = [_dot_tn(p[u], dob[u]) for u in range(ATT_UNROLL)]
            for u, r in enumerate(rows):
                dq_acc[r, :] = dq[u]
                dk_acc[r, :] = dk[u]
                dv_acc[r, :] = dv[u]
            return carry

        def step_tile(i, carry):
            qrows, krows, has_prev = _att_tiles(i, dil, nb)
            us = range(ATT_UNROLL)
            qb = [q_ref[r, :] * (DIL_DH ** -0.5) for r in qrows]
            kk, vv, dob = [k_ref[r, :] for r in krows], [v_ref[r, :] for r in krows], [do_ref[r, :] for r in qrows]
            sc = [_att_scores_tile(qb[u], kk[u], slope_d, has_prev[u]) for u in us]
            p = [jnp.exp(sc[u] - lse_ref[qrows[u], :][:, 0:1]) for u in us]
            ds = [p[u] * (_dot_nt(dob[u], vv[u]) - dl_ref[qrows[u], :][:, 0:1]) for u in us]
            dq = [_dot(ds[u], kk[u]) * (DIL_DH ** -0.5) for u in us]
            dk = [_dot_tn(ds[u], qb[u]) for u in us]
            dv = [_dot_tn(p[u], dob[u]) for u in us]
            for u in us:
                dq_acc[qrows[u], :] = dq[u]
                dk_acc[krows[u], :] += dk[u]
                dv_acc[krows[u], :] += dv[u]
            return carry

        if nb == 2:
            lax.fori_loop(0, dil // ATT_UNROLL, step_whole, 0)
        elif nb % 2 == 0:
            lax.fori_loop(0, dil * nb // 2 // ATT_UNROLL, step_tile, 0)
        else:
            lax.fori_loop(0, dil * nb // ATT_UNROLL, step, 0)
        dq_ref[...] = dq_acc[...].astype(BF16)
        dk_ref[...] = dk_acc[...].astype(BF16)
        dv_ref[...] = dv_acc[...].astype(BF16)

    def col(off):
        return pl.BlockSpec((s, DIL_DH), lambda h: (0, off // DIL_DH + group * DIL_HEADS + h))

    hd = pl.BlockSpec((s, DIL_DH), lambda h: (0, h))
    return pl.pallas_call(
        body, name=f"att_bwd{group}", grid=(DIL_HEADS,),
        in_specs=[col(OFF_Q_B), col(OFF_K_B), col(OFF_V_B), hd, hd, hd], out_specs=[hd, hd, hd],
        out_shape=[jax.ShapeDtypeStruct((s, DIL_W), BF16)] * 3,
        scratch_shapes=[pltpu.VMEM((s, DIL_DH), F32)] * 3,
        compiler_params=_cparams("parallel"))(proj, proj, proj, do, lse, delta)


def _att_merge(parts, proj):
    s = proj.shape[0]

    def body(o0, l0, o1, l1, o2, l2, z_ref, ob_ref, o_ref, lse_ref, obt_ref):
        m = jnp.maximum(jnp.maximum(l0[...], l1[...]), l2[...])
        num = jnp.zeros_like(m)
        den = jnp.zeros_like(m)
        for og, lg in ((o0, l0), (o1, l1), (o2, l2)):
            sc = jnp.exp(lg[...] - m)
            num = num + og[...] * sc
            den = den + sc
        o = num / den
        o_ref[...] = o
        lse_ref[...] = m + jnp.log(den)
        ob = o * _silu(z_ref[...])
        ob_ref[...] = ob.astype(BF16)
        obt_ref[...] = ob.T.astype(BF16)

    row = pl.BlockSpec((ROW_TILE, DIL_W), lambda i: (i, 0))
    flat = [a for p in parts for a in p]
    return pl.pallas_call(
        body, name="att_merge", grid=(s // ROW_TILE,),
        in_specs=[row] * 6 + [pl.BlockSpec((ROW_TILE, DIL_W), lambda i: (i, OFF_Z_B // DIL_W))],
        out_specs=[row, row, row, pl.BlockSpec((DIL_W, ROW_TILE), lambda i: (0, i))],
        out_shape=[jax.ShapeDtypeStruct((s, DIL_W), BF16), jax.ShapeDtypeStruct((s, DIL_W), F32),
                   jax.ShapeDtypeStruct((s, DIL_W), F32), jax.ShapeDtypeStruct((DIL_W, s), BF16)],
        compiler_params=_cparams("parallel"))(*flat, proj)


def _att_merge_bwd(o, proj, dob, dproj):
    s = o.shape[0]

    def body(o_ref, z_ref, d_ref, dproj_in, do_ref, dl_ref, dz_ref):
        ov, zv, dv = o_ref[...], z_ref[...], d_ref[...]
        do = dv * _silu(zv)
        do_ref[...] = do
        dz_ref[...] = (dv * ov * _silu_grad(zv)).astype(BF16)
        for h in range(DIL_HEADS):
            sl = slice(h * DIL_DH, (h + 1) * DIL_DH)
            dl_ref[:, sl] = jnp.broadcast_to(jnp.sum(do[:, sl] * ov[:, sl], axis=-1, keepdims=True), (ROW_TILE, DIL_DH))

    row = pl.BlockSpec((ROW_TILE, DIL_W), lambda i: (i, 0))
    return pl.pallas_call(
        body, name="att_merge_bwd", grid=(s // ROW_TILE,),
        in_specs=[row, pl.BlockSpec((ROW_TILE, DIL_W), lambda i: (i, OFF_Z_B // DIL_W)), row, DPROJ_IN],
        out_specs=[row, row, pl.BlockSpec((ROW_TILE, DIL_W), lambda i: (i, OFF_Z_B // DIL_W))],
        out_shape=[jax.ShapeDtypeStruct((s, DIL_W), F32), jax.ShapeDtypeStruct((s, DIL_W), F32),
                   jax.ShapeDtypeStruct((s, PW), BF16)],
        input_output_aliases={3: 2},
        compiler_params=_cparams("parallel"))(o, proj, dob, dproj)


def _merge(proj, ya, yb):
    s = proj.shape[0]

    def body(ga_ref, gb_ref, ya_ref, yb_ref, o_ref, ot_ref):
        m = _sigmoid(ga_ref[...]) * ya_ref[...] + _sigmoid(gb_ref[...]) * yb_ref[...]
        o_ref[...] = m.astype(BF16)
        ot_ref[...] = m.T.astype(BF16)

    row = pl.BlockSpec((ROW_TILE, D_MODEL), lambda i: (i, 0))
    return pl.pallas_call(
        body, name="merge", grid=(s // ROW_TILE,),
        in_specs=[pl.BlockSpec((ROW_TILE, D_MODEL), lambda i: (i, OFF_G_A // D_MODEL)),
                  pl.BlockSpec((ROW_TILE, D_MODEL), lambda i: (i, OFF_G_B // D_MODEL)), row, row],
        out_specs=[row, pl.BlockSpec((D_MODEL, ROW_TILE), lambda i: (0, i))],
        out_shape=[jax.ShapeDtypeStruct((s, D_MODEL), BF16), jax.ShapeDtypeStruct((D_MODEL, s), BF16)],
        compiler_params=_cparams("parallel"))(proj, proj, ya, yb)


def _merge_bwd(proj, ya, yb, dx2, w_out):
    s = proj.shape[0]

    def body(ga_ref, gb_ref, ya_ref, yb_ref, dx_ref, wo_ref, dya_ref, dyb_ref, dga_ref, dgb_ref):
        dmv = _dot_nt(dx_ref[...], wo_ref[...])
        sa, sb = _sigmoid(ga_ref[...]), _sigmoid(gb_ref[...])
        dya_ref[...] = (dmv * sa).astype(BF16)
        dyb_ref[...] = (dmv * sb).astype(BF16)
        dga_ref[...] = (dmv * ya_ref[...] * sa * (1.0 - sa)).astype(BF16)
        dgb_ref[...] = (dmv * yb_ref[...] * sb * (1.0 - sb)).astype(BF16)

    row = pl.BlockSpec((ROW_TILE, D_MODEL), lambda i: (i, 0))
    return pl.pallas_call(
        body, name="merge_bwd", grid=(s // ROW_TILE,),
        in_specs=[pl.BlockSpec((ROW_TILE, D_MODEL), lambda i: (i, OFF_G_A // D_MODEL)),
                  pl.BlockSpec((ROW_TILE, D_MODEL), lambda i: (i, OFF_G_B // D_MODEL)), row, row, row,
                  pl.BlockSpec(w_out.shape, lambda i: (0, 0))],
        out_specs=[row] * 4, out_shape=[jax.ShapeDtypeStruct((s, D_MODEL), BF16)] * 4,
        compiler_params=_cparams("parallel"))(proj, proj, ya, yb, dx2, w_out)


def _final(x, mg, w_out, fw, tgt):
    s, d = x.shape

    def body(x_ref, mg_ref, wo_ref, w_ref, y_ref, dx_ref, dw_ref, l_ref):
        i = pl.program_id(0)
        x2 = x_ref[...] + _dot(mg_ref[...], wo_ref[...])
        wv = w_ref[...]
        r = lax.rsqrt(jnp.mean(x2 * x2, axis=-1, keepdims=True) + NORM_EPS)
        e = x2 * r * wv - y_ref[...]
        lrow = jnp.mean(e * e, axis=-1, keepdims=True)
        lpart = jnp.broadcast_to(0.5 * jnp.sum(lrow, axis=0, keepdims=True), (1, 128))
        dy = e * (1.0 / d)
        dwp = jnp.sum(dy * x2 * r, axis=0, keepdims=True)
        dyw = dy * wv
        dx_ref[...] = r * dyw - x2 * (r * r * r) * jnp.mean(dyw * x2, axis=-1, keepdims=True)

        @pl.when(i == 0)
        def _():
            dw_ref[...] = dwp
            l_ref[...] = lpart

        @pl.when(i > 0)
        def _():
            dw_ref[...] += dwp
            l_ref[...] += lpart

    row = pl.BlockSpec((BIG_TILE, d), lambda i: (i, 0))
    vec = pl.BlockSpec((1, d), lambda i: (0, 0))
    return pl.pallas_call(
        body, name="final", grid=(s // BIG_TILE,),
        in_specs=[row, row, pl.BlockSpec(w_out.shape, lambda i: (0, 0)), vec, row],
        out_specs=[row, vec, pl.BlockSpec((1, 128), lambda i: (0, 0))],
        out_shape=[jax.ShapeDtypeStruct((s, d), F32), jax.ShapeDtypeStruct((1, d), F32), jax.ShapeDtypeStruct((1, 128), F32)],
        compiler_params=_cparams("arbitrary"))(x, mg, w_out, fw, tgt)


def _adamw(w, g, m, v, name):
    r, c = w.shape
    cap = max(8, (1 << 18) // c)
    divisors = [t for t in range(8, min(r, cap) + 1, 8) if r % t == 0]
    tr = r if r <= 8 else (max(divisors) if divisors else cap)

    def body(w_ref, g_ref, m_ref, v_ref, d_ref, nm_ref, nv_ref):
        gv = g_ref[...]
        mn = ADAM_B1 * m_ref[...] + (1.0 - ADAM_B1) * gv
        vn = ADAM_B2 * v_ref[...] + (1.0 - ADAM_B2) * (gv * gv)
        m_hat = mn / (1.0 - ADAM_B1 ** ADAM_STEP)
        v_hat = vn / (1.0 - ADAM_B2 ** ADAM_STEP)
        d_ref[...] = -ADAM_LR * (m_hat / (jnp.sqrt(v_hat) + ADAM_EPS) + ADAM_WD * w_ref[...])
        nm_ref[...] = mn
        nv_ref[...] = vn

    blk = pl.BlockSpec((tr, c), lambda i: (i, 0))
    return pl.pallas_call(
        body, name=name, grid=(pl.cdiv(r, tr),), in_specs=[blk] * 4, out_specs=[blk] * 3,
        out_shape=[jax.ShapeDtypeStruct((r, c), F32)] * 3, compiler_params=_cparams("parallel"))(w, g, m, v)


HBM_SPEC = pl.BlockSpec(memory_space=pl.ANY)


def _place():
    x, y, c = lax.axis_index("x"), lax.axis_index("y"), lax.axis_index("c")
    chips = [(1 - x, y), (x, 1 - y), (1 - x, 1 - y)]
    return x, y, c, chips


def _ag_weights(packs):
    na = len(packs)
    nsem = 8

    def body(*refs):
        p_refs, out_refs = refs[:na], refs[na:2 * na]
        send_sems, recv_sems = refs[2 * na:]
        x, y, c, _ = _place()
        me, sib, j = (x, y, c), (x, y, 1 - c), 2 * x + y
        xn, yn = (1 - x, y, c), (x, 1 - y, c)
        jx, jy, jd = 2 * (1 - x) + y, 2 * x + (1 - y), 2 * (1 - x) + (1 - y)

        def rc(a, k, src, dst, to):
            return pltpu.make_async_remote_copy(src_ref=src, dst_ref=dst, send_sem=send_sems.at[nsem * a + k],
                                                recv_sem=recv_sems.at[nsem * a + k], device_id=to, device_id_type=MESH)

        sent = []
        for a in range(na):
            mine, land = p_refs[a].at[c], out_refs[a].at[j, c]
            sent += [rc(a, 0, mine, land, xn), rc(a, 1, mine, land, yn), rc(a, 7, p_refs[a], out_refs[a].at[j], sib)]
        for cp in sent:
            cp.start()
        for a in range(na):
            half = p_refs[a].shape[1] // 2
            top, bottom = pl.ds(0, half), pl.ds(half, half)
            from_x, from_y, from_d = out_refs[a].at[jx, c], out_refs[a].at[jy, c], out_refs[a].at[jd, c]
            rc(a, 0, p_refs[a].at[c], from_x, me).wait_recv()
            later = [rc(a, 2, from_x.at[top], from_x.at[top], yn), rc(a, 4, from_x, from_x, sib)]
            for cp in later:
                cp.start()
            sent += later
            rc(a, 1, p_refs[a].at[c], from_y, me).wait_recv()
            later = [rc(a, 3, from_y.at[bottom], from_y.at[bottom], xn), rc(a, 5, from_y, from_y, sib)]
            for cp in later:
                cp.start()
            sent += later
            rc(a, 2, from_d.at[top], from_d.at[top], me).wait_recv()
            rc(a, 3, from_d.at[bottom], from_d.at[bottom], me).wait_recv()
            cp = rc(a, 6, from_d, from_d, sib)
            cp.start()
            sent.append(cp)
        for a in range(na):
            for k, jj in ((4, jx), (5, jy), (6, jd)):
                rc(a, k, p_refs[a].at[c], out_refs[a].at[jj, 1 - c], me).wait_recv()
            rc(a, 7, p_refs[a], out_refs[a].at[j], me).wait_recv()
        for cp in sent:
            cp.wait_send()

    return pl.pallas_call(
        body, name="ag_weights",
        out_shape=[jax.ShapeDtypeStruct((N_CHIPS,) + p.shape, p.dtype) for p in packs],
        in_specs=[HBM_SPEC] * na, out_specs=[HBM_SPEC] * na,
        scratch_shapes=[pltpu.SemaphoreType.DMA((nsem * na,)), pltpu.SemaphoreType.DMA((nsem * na,))])(*packs)


def _rs_pair(dwpt, gpack):
    n = N_CHIPS
    hw = SHARD_PAD // 2

    def body(d_ref, g_ref, out_d, out_g, send_sems, recv_sems):
        x, y, c, _ = _place()
        sib = (x, y, 1 - c)
        cps = []
        for p in range(n):
            start = pl.multiple_of(WIN_BASE[p] + (1 - c) * hw, TILE_ROWS)
            cps.append(pltpu.make_async_remote_copy(
                src_ref=d_ref.at[pl.ds(start, hw)], dst_ref=out_d.at[p], send_sem=send_sems.at[p],
                recv_sem=recv_sems.at[p], device_id=sib, device_id_type=MESH))
            cps.append(pltpu.make_async_remote_copy(
                src_ref=g_ref.at[p, 1 - c], dst_ref=out_g.at[p], send_sem=send_sems.at[n + p],
                recv_sem=recv_sems.at[n + p], device_id=sib, device_id_type=MESH))
        for cp in cps:
            cp.start()
        for cp in cps:
            cp.wait_recv()
        for cp in cps:
            cp.wait_send()

    return pl.pallas_call(
        body, name="rs_pair",
        out_shape=[jax.ShapeDtypeStruct((n, hw, dwpt.shape[1]), dwpt.dtype),
                   jax.ShapeDtypeStruct((n,) + gpack.shape[2:], gpack.dtype)],
        in_specs=[HBM_SPEC] * 2, out_specs=[HBM_SPEC] * 2,
        scratch_shapes=[pltpu.SemaphoreType.DMA((2 * n,)), pltpu.SemaphoreType.DMA((2 * n,))])(dwpt, gpack)


def _add_halves_win(dwpt, other, c):
    n, rh, wd = other.shape
    tr = _row_tile(rh)

    def body(s_ref, d_ref, o_ref, out_ref):
        out_ref[0] = (d_ref[...] + o_ref[0]).astype(BF16)

    scal = jnp.concatenate([jnp.reshape(c, (1,)).astype(jnp.int32), jnp.asarray(WIN_BASE, jnp.int32)])
    grid_spec = pltpu.PrefetchScalarGridSpec(
        num_scalar_prefetch=1, grid=(n, rh // tr),
        in_specs=[pl.BlockSpec((pl.Element(tr), pl.Element(wd)),
                               lambda p, i, sr: (pl.multiple_of(sr[1 + p] + sr[0] * rh + i * tr, TILE_ROWS), 0)),
                  pl.BlockSpec((1, tr, wd), lambda p, i, sr: (p, i, 0))],
        out_specs=pl.BlockSpec((1, tr, wd), lambda p, i, sr: (p, i, 0)))
    return pl.pallas_call(
        body, name="add_halves_in", grid_spec=grid_spec, out_shape=jax.ShapeDtypeStruct((n, rh, wd), BF16),
        compiler_params=_cparams("parallel", "parallel"))(scal, dwpt, other)


SEM_SPEC = pl.BlockSpec(memory_space=pltpu.SEMAPHORE)
DATAFLOW_EFFECT = pltpu.SideEffectType.DATAFLOW_SIDE_EFFECTING


def _rs_chips_start(csums):
    na = len(csums)

    def body(*refs):
        s_refs, land_refs = refs[:na], refs[na:2 * na]
        send_sems, recv_sems = refs[2 * na], refs[2 * na + 1]
        token = refs[-1]
        x, y, c, chips = _place()
        j = 2 * x + y
        for a in range(na):
            for k, (cx, cy) in enumerate(chips):
                pltpu.make_async_remote_copy(src_ref=s_refs[a].at[2 * cx + cy], dst_ref=land_refs[a].at[j],
                                             send_sem=send_sems.at[3 * a + k], recv_sem=recv_sems.at[3 * a + k],
                                             device_id=(cx, cy, c), device_id_type=MESH).start()
        token[...] = jnp.zeros_like(token)

    hbm = [pltpu.HBM(s.shape, s.dtype) for s in csums]
    args = [pltpu.with_memory_space_constraint(s, pltpu.HBM) for s in csums]
    args += [pltpu.with_memory_space_constraint(lax.empty(s.shape, s.dtype), pltpu.HBM) for s in csums]
    res = pl.pallas_call(
        body, name="rs_chips_start",
        out_shape=(pltpu.SemaphoreType.DMA((3 * na,)), pltpu.SemaphoreType.DMA((3 * na,)), *hbm, *hbm,
                   jax.ShapeDtypeStruct((8, 128), F32)),
        in_specs=[pl.BlockSpec(memory_space=pltpu.HBM)] * (2 * na),
        out_specs=(SEM_SPEC, SEM_SPEC, *[pl.BlockSpec(memory_space=pltpu.HBM)] * (2 * na),
                   pl.BlockSpec(memory_space=pltpu.VMEM)),
        input_output_aliases={i: 2 + i for i in range(2 * na)},
        compiler_params=pltpu.CompilerParams(has_side_effects=DATAFLOW_EFFECT))(*args)
    return res[0], res[1], list(res[2:2 + na]), list(res[2 + na:2 + 2 * na]), res[-1]


def _rs_chips_wait(send_sems, recv_sems, csums, lands, after):
    na = len(csums)

    def body(*refs):
        s_refs, land_refs = refs[:na], refs[na:2 * na]
        send_sems, recv_sems = refs[2 * na], refs[2 * na + 1]
        x, y, c, chips = _place()
        j = 2 * x + y
        for a in range(na):
            for k, (cx, cy) in enumerate(chips):
                cp = pltpu.make_async_remote_copy(src_ref=s_refs[a].at[2 * cx + cy], dst_ref=land_refs[a].at[2 * cx + cy],
                                                  send_sem=send_sems.at[3 * a + k], recv_sem=recv_sems.at[3 * a + k],
                                                  device_id=(cx, cy, c), device_id_type=MESH)
                cp.wait_send()
                cp.wait_recv()

    hbm = [pltpu.HBM(s.shape, s.dtype) for s in csums]
    res = pl.pallas_call(
        body, name="rs_chips_wait", out_shape=(*hbm, *hbm),
        in_specs=[pl.BlockSpec(memory_space=pltpu.HBM)] * (2 * na) + [SEM_SPEC, SEM_SPEC, pl.BlockSpec(memory_space=pl.ANY)],
        out_specs=tuple([pl.BlockSpec(memory_space=pltpu.HBM)] * (2 * na)),
        input_output_aliases={i: i for i in range(2 * na)},
        compiler_params=pltpu.CompilerParams(has_side_effects=DATAFLOW_EFFECT))(*csums, *lands, send_sems, recv_sems, after)
    return list(res[:na]), list(res[na:])


SWAP_CHUNKS = 4


def _pair_swap(halves):
    na = len(halves)

    def body(*refs):
        h_refs, out_refs = refs[:na], refs[na:2 * na]
        send_sems, recv_sems = refs[2 * na:]
        x, y, c, _ = _place()
        cps = []
        for a in range(na):
            rows = h_refs[a].shape[0] // SWAP_CHUNKS
            assert rows * SWAP_CHUNKS == h_refs[a].shape[0]
            for q in range(SWAP_CHUNKS):
                k = SWAP_CHUNKS * a + q
                cps.append(pltpu.make_async_remote_copy(
                    src_ref=h_refs[a].at[pl.ds(q * rows, rows)], dst_ref=out_refs[a].at[pl.ds(q * rows, rows)],
                    send_sem=send_sems.at[k], recv_sem=recv_sems.at[k], device_id=(x, y, 1 - c), device_id_type=MESH))
        for cp in cps:
            cp.start()
        for cp in cps:
            cp.wait_recv()
        for cp in cps:
            cp.wait_send()

    return pl.pallas_call(
        body, name="pair_swap", out_shape=[jax.ShapeDtypeStruct(h.shape, h.dtype) for h in halves],
        in_specs=[HBM_SPEC] * na, out_specs=[HBM_SPEC] * na,
        scratch_shapes=[pltpu.SemaphoreType.DMA((SWAP_CHUNKS * na,)), pltpu.SemaphoreType.DMA((SWAP_CHUNKS * na,))])(*halves)


def _ag_small(v):
    m_per, n = v.shape

    def body(x_ref, out_ref, send_sems, recv_sems, local_sem):
        x, y, c, chips = _place()
        me, sibling = (x, y, c), (x, y, 1 - c)

        def rows(px, py, pc):
            return out_ref.at[pl.ds((4 * px + 2 * py + pc) * m_per, m_per), :]

        def copy(k, block, to, src=None):
            return pltpu.make_async_remote_copy(
                src_ref=rows(*block) if src is None else src, dst_ref=rows(*block), send_sem=send_sems.at[k],
                recv_sem=recv_sems.at[k], device_id=to, device_id_type=MESH)

        mine = pltpu.make_async_copy(x_ref, rows(*me), local_sem)
        mine.start()
        first = [copy(0, me, sibling, src=x_ref)]
        first += [copy(1 + k, me, (*chip, c), src=x_ref) for k, chip in enumerate(chips)]
        for cp in first:
            cp.start()
        passed = [copy(4 + k, (*chip, c), sibling) for k, chip in enumerate(chips)]
        for k, chip in enumerate(chips):
            copy(1 + k, (*chip, c), me).wait_recv()
            passed[k].start()
        copy(0, sibling, me).wait_recv()
        for k, chip in enumerate(chips):
            copy(4 + k, (*chip, 1 - c), me).wait_recv()
        for cp in first + passed:
            cp.wait_send()
        mine.wait()

    return pl.pallas_call(
        body, name="ag_small", out_shape=jax.ShapeDtypeStruct((8 * m_per, n), v.dtype),
        in_specs=[pl.BlockSpec(memory_space=pltpu.VMEM)], out_specs=pl.BlockSpec(memory_space=pltpu.VMEM),
        scratch_shapes=[pltpu.SemaphoreType.DMA((7,)), pltpu.SemaphoreType.DMA((7,)), pltpu.SemaphoreType.DMA])(v)


def _sum_blocks(a, nblk, name):
    rows, wd = a.shape
    r = rows // nblk
    tr = min(r, ROW_TILE)
    assert r % tr == 0

    def body(*refs):
        acc = refs[0][...].astype(F32)
        for ref in refs[1:nblk]:
            acc = acc + ref[...].astype(F32)
        refs[nblk][...] = acc

    nt = r // tr
    return pl.pallas_call(
        body, name=name, grid=(nt,),
        in_specs=[pl.BlockSpec((tr, wd), functools.partial(lambda i, b: (b * nt + i, 0), b=b)) for b in range(nblk)],
        out_specs=pl.BlockSpec((tr, wd), lambda i: (i, 0)),
        out_shape=jax.ShapeDtypeStruct((r, wd), F32), compiler_params=_cparams("parallel"))(*([a] * nblk))


def _row_tile(rows):
    best = max(t for t in range(16, 513, 16) if rows % t == 0)
    return best


def _sum_chips(by_src, csum, j, name):
    n, rh, wd = by_src.shape
    tr = _row_tile(rh)

    def body(j_ref, *refs):
        own = refs[n][0].astype(F32)
        acc = None
        for k in range(n):
            term = jnp.where(j_ref[0] == k, own, refs[k][0].astype(F32))
            acc = term if acc is None else acc + term
        refs[n + 1][...] = acc

    def other(k):
        return pl.BlockSpec((1, tr, wd), lambda i, jr: (jnp.where(jr[0] == k, (k + 1) % n, k), i, 0))

    grid_spec = pltpu.PrefetchScalarGridSpec(
        num_scalar_prefetch=1, grid=(rh // tr,),
        in_specs=[other(k) for k in range(n)] + [pl.BlockSpec((1, tr, wd), lambda i, jr: (jr[0], i, 0))],
        out_specs=pl.BlockSpec((tr, wd), lambda i, jr: (i, 0)))
    return pl.pallas_call(
        body, name=name, grid_spec=grid_spec, out_shape=jax.ShapeDtypeStruct((rh, wd), F32),
        compiler_params=_cparams("parallel"))(jnp.reshape(j, (1,)).astype(jnp.int32), *([by_src] * n), csum)


def _add_halves(gpack, other, c, name):
    n, _, rh, wd = gpack.shape
    tr = _row_tile(rh)

    def body(c_ref, g_ref, o_ref, out_ref):
        out_ref[0] = (g_ref[0, 0] + o_ref[0]).astype(BF16)

    grid_spec = pltpu.PrefetchScalarGridSpec(
        num_scalar_prefetch=1, grid=(n, rh // tr),
        in_specs=[pl.BlockSpec((1, 1, tr, wd), lambda p, i, cr: (p, cr[0], i, 0)),
                  pl.BlockSpec((1, tr, wd), lambda p, i, cr: (p, i, 0))],
        out_specs=pl.BlockSpec((1, tr, wd), lambda p, i, cr: (p, i, 0)))
    return pl.pallas_call(
        body, name=name, grid_spec=grid_spec, out_shape=jax.ShapeDtypeStruct((n, rh, wd), BF16),
        compiler_params=_cparams("parallel", "parallel"))(jnp.reshape(c, (1,)).astype(jnp.int32), gpack, other)


PACK_W = 1024
ROWS_O_DN = DN_W // N_CHIPS
ROWS_O_DIL = DIL_W * (D_MODEL // N_CHIPS) // PACK_W
ROWS_OUT = D_MODEL // N_CHIPS
ROWS_CONV = 4 * (3 * DN_W // N_CHIPS) // PACK_W
R1 = ROWS_O_DN
R2 = R1 + ROWS_O_DIL
R3 = R2 + ROWS_OUT
R4 = R3 + 16
R5 = R4 + 16
PACK_ROWS = 704
HALF_ROWS = PACK_ROWS // 2
SHARD_PAD = 2880


R6 = R5 + 2 * DN_HEADS

TILE_ROWS = 16
BA_IN_SHARD1 = REF_OFF_BA - SHARD_W
LOCAL_START = (0, SHARD_W, 2 * SHARD_W - 2 * DN_HEADS, 3 * SHARD_W - 2 * DN_HEADS)
LOCAL_END = LOCAL_START[1:] + (OFF_BA,)
WIN_BASE = tuple(s // TILE_ROWS * TILE_ROWS for s in LOCAL_START)


def _to_window(k, shard):
    nba = 2 * DN_HEADS
    body = shard
    if k == 1:
        row = lax.broadcasted_iota(jnp.int32, (SHARD_W - nba, 1), 0)
        body = jnp.where(row < BA_IN_SHARD1, shard[:SHARD_W - nba], shard[nba:])
    lead = LOCAL_START[k] - WIN_BASE[k]
    return jnp.pad(body, ((lead, SHARD_PAD - lead - body.shape[0]), (0, 0)))


def _from_window(k, win, ba):
    nba = 2 * DN_HEADS
    lead = LOCAL_START[k] - WIN_BASE[k]
    if k != 1:
        return win[lead:lead + SHARD_W]
    row = lax.broadcasted_iota(jnp.int32, (SHARD_W, 1), 0)
    before = win[lead:lead + SHARD_W]
    after = jnp.pad(win, ((nba, 0), (0, 0)))[lead:lead + SHARD_W]
    mid = jnp.pad(ba, ((BA_IN_SHARD1, SHARD_W - BA_IN_SHARD1 - nba), (0, 0)))
    return jnp.where(row < BA_IN_SHARD1, before, jnp.where(row < BA_IN_SHARD1 + nba, mid, after))


def _stack_windows(wins, ba):
    pieces = []
    for k in range(N_CHIPS):
        lo = WIN_BASE[k] + (TILE_ROWS if k else 0)
        hi = LOCAL_END[k] // TILE_ROWS * TILE_ROWS
        pieces.append(wins[k][lo - WIN_BASE[k]:hi - WIN_BASE[k]])
        if k + 1 < N_CHIPS:
            assert hi == WIN_BASE[k + 1]
            pieces.append(wins[k][hi - WIN_BASE[k]:hi - WIN_BASE[k] + TILE_ROWS] + wins[k + 1][:TILE_ROWS])
    pieces += [ba, jnp.zeros((PW - OFF_BA - ba.shape[0], ba.shape[1]), ba.dtype)]
    out = jnp.concatenate(pieces, axis=0)
    assert out.shape[0] == PW
    return out


def _local_step(x, tgt, norm_w, wpt, conv_full, a_log, dt_bias, dn_norm_w, w_o_dn, w_o_dil, w_out, final_norm_w):
    s = x.shape[0]
    h, h_t = _rms_in(x, norm_w)
    proj = _matmul(h, wpt, F32, 2048, 1280, 1024, "proj", nt=True)
    c_pre, qkv = _conv_fwd(proj, conv_full)
    gate_par = jnp.zeros((8, 128), F32).at[0, 8:16].set(a_log[0]).at[1, 8:16].set(dt_bias[0])
    bg = _gates_fwd(proj, gate_par)
    o_a, u, w, vn, tmat, states = _gdr_fwd(qkv, bg)
    oa2, oa2_t = _gdr_out(o_a, proj, dn_norm_w)
    ya = _matmul(oa2, w_o_dn, F32, 1024, 1024, 1024, "ya")
    parts = [_att_fwd(proj, g) for g in range(N_DIL)]
    ob, o_att, lse, ob_t = _att_merge(parts, proj)
    yb = _matmul(ob, w_o_dil, F32, 1024, 1024, 512, "yb")
    mg, mg_t = _merge(proj, ya, yb)
    dx2, dfw, lpart = _final(x, mg, w_out, final_norm_w, tgt)

    dw_out = _matmul(mg_t, dx2, F32, 1024, 1024, 1024, "dw_out")
    dya, dyb, dga, dgb = _merge_bwd(proj, ya, yb, dx2, w_out)
    dw_o_dn = _matmul(oa2_t, dya, F32, 1024, 1024, 1024, "dw_o_dn")
    dob = _matmul(dyb, w_o_dil, F32, 1024, 512, 1024, "d_ob", nt=True)
    dw_o_dil = _matmul(ob_t, dyb, F32, 512, 1024, 1024, "dw_o_dil")
    do_a, dproj, ddnw = _gdr_out_bwd(o_a, proj, dn_norm_w, dya, w_o_dn)
    dqkv_a, dbg = _gdr_bwd(qkv, bg, u, w, vn, tmat, states, do_a)
    dproj, dpar = _gates_bwd(proj, gate_par, dbg, dproj)
    dproj, dconv = _conv_bwd(proj, c_pre, dqkv_a, conv_full, dproj)
    do_att, delta, dproj = _att_merge_bwd(o_att, proj, dob, dproj)
    dqkv_b = [_att_bwd(proj, g, do_att, lse, delta) for g in range(N_DIL)]
    pieces = [(OFF_Q_B + (N_DIL * i + g) * DIL_W, dqkv_b[g][i]) for i in range(3) for g in range(N_DIL)]
    for off, piece in pieces + [(OFF_G_A, dga), (OFF_G_B, dgb)]:
        dproj = lax.dynamic_update_slice(dproj, piece, (0, off))
    dwpt, dwpt_b = _matmul(h_t, dproj, F32, 1024, 1280, 2048, "dw_in", transpose_out=True, also_bf16=True)

    def finish(after=None):
        dh = _matmul(dproj, wpt, F32, 1024, 1024, 3840, "d_h", after=after)
        grad_x, dnw = _rms_in_bwd(x, norm_w, dh, dx2)
        small = jnp.zeros((8, PACK_W), F32)
        small = small.at[0].set(dnw[0]).at[1].set(dfw[0]).at[2, :DN_D].set(ddnw[0])
        small = small.at[3, :DN_HEADS].set(dpar[0, 8:16]).at[3, DN_HEADS:2 * DN_HEADS].set(dpar[1, 8:16])
        small = small.at[4, 0].set(lpart[0, 0])
        return grad_x, small

    return finish, (dwpt, dwpt_b), dconv, dw_o_dn, dw_o_dil, dw_out


def kernel(x, norm_w, w_in, conv_w, a_log, dt_bias, dn_norm_w, w_o_dn, w_o_dil, w_out, final_norm_w, loss_target, m_norm_w, m_w_in, m_conv_w, m_a_log, m_dt_bias, m_dn_norm_w, m_w_o_dn, m_w_o_dil, m_w_out, m_final_norm_w, v_norm_w, v_w_in, v_conv_w, v_a_log, v_dt_bias, v_dn_norm_w, v_w_o_dn, v_w_o_dil, v_w_out, v_final_norm_w):
    c = lax.axis_index("c")
    j = 2 * lax.axis_index("x") + lax.axis_index("y")
    qw = D_MODEL // N_CHIPS

    cw = conv_w[0].reshape(ROWS_CONV, PACK_W)
    cw = jnp.pad(cw, ((0, 16 - ROWS_CONV), (0, 0)))
    cw_hi = cw.astype(BF16)
    cw_lo = (cw - cw_hi.astype(F32)).astype(BF16)
    shard = w_in[0].T.astype(BF16)
    own_ba = jnp.where(j == 1, shard[BA_IN_SHARD1:BA_IN_SHARD1 + 2 * DN_HEADS], jnp.zeros((2 * DN_HEADS, D_MODEL), BF16))
    pack = jnp.concatenate(
        [w_o_dn[0].astype(BF16), w_o_dil[0].astype(BF16).reshape(ROWS_O_DIL, PACK_W), w_out[0].astype(BF16), cw_hi, cw_lo,
         own_ba, jnp.zeros((PACK_ROWS - R6, PACK_W), BF16)], axis=0).reshape(2, HALF_ROWS, PACK_W)
    chips = range(N_CHIPS)
    own_win = lax.switch(j, [functools.partial(_to_window, k) for k in chips], shard).reshape(2, SHARD_PAD // 2, D_MODEL)
    all_in, allw = _ag_weights([own_win, pack])
    wins = [all_in[k].reshape(SHARD_PAD, D_MODEL) for k in chips]
    allw = [allw[k].reshape(PACK_ROWS, PACK_W) for k in chips]
    wpt = _stack_windows(wins, allw[1][R5:R6])
    w_o_dn_full = jnp.concatenate([allw[k][:R1] for k in chips], axis=0)
    w_o_dil_full = jnp.concatenate([allw[k][R1:R2].reshape(DIL_W, qw) for k in chips], axis=1)
    w_out_full = jnp.concatenate([allw[k][R2:R3] for k in chips], axis=0)
    conv_full = jnp.concatenate(
        [(allw[k][R3:R3 + ROWS_CONV].astype(F32) + allw[k][R4:R4 + ROWS_CONV].astype(F32)).reshape(4, 3 * DN_W // N_CHIPS)
         for k in chips], axis=1)

    finish, (dwpt, dwpt_b), dconv, dw_o_dn, dw_o_dil, dw_out = _local_step(
        x[0], loss_target[0], norm_w, wpt, conv_full, a_log, dt_bias, dn_norm_w, w_o_dn_full, w_o_dil_full, w_out_full,
        final_norm_w.reshape(1, D_MODEL))

    cq = 3 * DN_W // N_CHIPS
    gpack = jnp.stack([
        jnp.concatenate(
            [dw_o_dn[k * qw:(k + 1) * qw], dw_o_dil[:, k * qw:(k + 1) * qw].reshape(ROWS_O_DIL, PACK_W),
             dw_out[k * qw:(k + 1) * qw],
             jnp.pad(dconv[:, k * cq:(k + 1) * cq].reshape(ROWS_CONV, PACK_W), ((0, 16 - ROWS_CONV), (0, 0))),
             dwpt[OFF_BA:OFF_BA + 2 * DN_HEADS] if k == 1 else jnp.zeros((2 * DN_HEADS, PACK_W), F32),
             jnp.zeros((PACK_ROWS - R4 - 2 * DN_HEADS, PACK_W), F32)], axis=0)
        for k in chips]).reshape(N_CHIPS, 2, HALF_ROWS, PACK_W)
    sib_in, sib_pack = _rs_pair(dwpt_b, gpack)
    csum_in = _add_halves_win(dwpt, sib_in, c)
    csum_pack = _add_halves(gpack, sib_pack, c, "add_halves_pack")
    send_sems, recv_sems, csums, lands, token = _rs_chips_start([csum_in, csum_pack])
    grad_x, small = finish(after=token)

    gs = _sum_blocks(_ag_small(small), 8, "sum_small")
    loss = gs[4, 0]
    w_small = jnp.zeros((8, PACK_W), F32)

    def pack_small(nw, fw, dnw_, al, db):
        t = w_small.at[0].set(nw[0]).at[1].set(fw).at[2, :DN_D].set(dnw_[0])
        return t.at[3, :DN_HEADS].set(al[0]).at[3, DN_HEADS:2 * DN_HEADS].set(db[0])

    sm = _adamw(pack_small(norm_w, final_norm_w, dn_norm_w, a_log, dt_bias), gs,
                pack_small(m_norm_w, m_final_norm_w, m_dn_norm_w, m_a_log, m_dt_bias),
                pack_small(v_norm_w, v_final_norm_w, v_dn_norm_w, v_a_log, v_dt_bias), "adamw_small")

    (csum_in, csum_pack), (src_in, src_pack) = _rs_chips_wait(send_sems, recv_sems, csums, lands, sm[0])
    half_in = _sum_chips(src_in, csum_in, j, "sum_chips_in")
    half_pack = _sum_chips(src_pack, csum_pack, j, "sum_chips_pack")
    sib_half_in, sib_half_pack = _pair_swap([half_in, half_pack])

    def both_halves(mine, theirs):
        return jnp.where(c == 0, jnp.concatenate([mine, theirs], axis=0), jnp.concatenate([theirs, mine], axis=0))

    g = both_halves(half_pack, sib_half_pack)
    g_w_in = lax.switch(j, [functools.partial(_from_window, k) for k in chips], both_halves(half_in, sib_half_in),
                        g[R4:R4 + 2 * DN_HEADS])
    g_w_o_dn = g[:R1]
    g_w_o_dil = g[R1:R2].reshape(DIL_W, qw)
    g_w_out = g[R2:R3]
    g_conv = g[R3:R3 + ROWS_CONV].reshape(4, cq)

    def unpack_small(t):
        return dict(norm_w=t[0:1], final_norm_w=t[1], dn_norm_w=t[2:3, :DN_D], a_log=t[3:4, :DN_HEADS],
                    dt_bias=t[3:4, DN_HEADS:2 * DN_HEADS])

    res = {"grad": unpack_small(gs)}
    for kind, arr in zip(("delta", "new_m", "new_v"), sm):
        res[kind] = unpack_small(arr)
    big = dict(conv_w=(conv_w, g_conv, m_conv_w, v_conv_w), w_o_dn=(w_o_dn, g_w_o_dn, m_w_o_dn, v_w_o_dn),
               w_o_dil=(w_o_dil, g_w_o_dil, m_w_o_dil, v_w_o_dil), w_out=(w_out, g_w_out, m_w_out, v_w_out))
    for name, (wt, gt, mt, vt) in big.items():
        d, nm, nv = _adamw(wt[0], gt, mt[0], vt[0], "adamw_" + name)
        res["grad"][name] = gt[None]
        res["delta"][name], res["new_m"][name], res["new_v"][name] = d[None], nm[None], nv[None]

    d, nm, nv = _adamw(w_in[0].T, g_w_in, m_w_in[0].T, v_w_in[0].T, "adamw_w_in")
    res["grad"]["w_in"] = g_w_in.T[None]
    res["delta"]["w_in"], res["new_m"]["w_in"], res["new_v"]["w_in"] = d.T[None], nm.T[None], nv.T[None]
    order = ["norm_w", "w_in", "conv_w", "a_log", "dt_bias", "dn_norm_w", "w_o_dn", "w_o_dil", "w_out", "final_norm_w"]
    outs = [loss, grad_x[None]]
    for kind in ("grad", "delta", "new_m", "new_v"):
        outs += [res[kind][nm] for nm in order]
    return tuple(outs)
```
